```python
import math
import jax, jax.numpy as jnp
from jax import lax
import numpy as np

D_MODEL = 1024
BATCH = 16
SEQ = 2048
DEPTH = 1

HEAD_DIM = 64
SB_HEADS = 8
DIL_PAIRS = ((128, 1), (512, 4), (2048, 16))
DIL_HEADS_PER_GROUP = 4
DIL_HEADS = DIL_HEADS_PER_GROUP * len(DIL_PAIRS)
SB_WIDTH = SB_HEADS * HEAD_DIM
DIL_WIDTH = DIL_HEADS * HEAD_DIM
DIL_OUT_WIDTH = DIL_HEADS_PER_GROUP * HEAD_DIM
IN_WIDTH = 3 * SB_WIDTH + 3 * DIL_WIDTH + 2 * D_MODEL
D_FF = ((8 * D_MODEL + 3 * 256 - 1) // (3 * 256)) * 256
Q_BLOCK = 128
RMS_EPS = 1e-6
ALIBI_MAX_BIAS = 8.0
SPLITS = tuple(int(c) for c in np.cumsum([SB_WIDTH, SB_WIDTH, SB_WIDTH, DIL_WIDTH, DIL_WIDTH, DIL_WIDTH, D_MODEL]))

kernel_name = "hybrid_stickbreak_dilated_gated"


def rms_norm(x, g):
    xf = x.astype(jnp.float32)
    y = xf * lax.rsqrt(jnp.mean(xf * xf, axis=-1, keepdims=True) + RMS_EPS) * g.astype(jnp.float32)
    return y.astype(x.dtype)


def alibi_slopes(n):
    return jnp.exp2(-ALIBI_MAX_BIAS * jnp.arange(1, n + 1, dtype=jnp.float32) / n)


def stick_breaking_attention(q, k, v):
    b, s, h, dh = q.shape
    nb = s // Q_BLOCK
    scale = 1.0 / math.sqrt(dh)
    qb = q.reshape(b, nb, Q_BLOCK, h, dh).transpose(1, 0, 3, 2, 4)
    kpos = jnp.arange(s)

    def block(args):
        q_blk, t0 = args
        z = jnp.einsum('bhqd,bkhd->bhqk', q_blk, k, preferred_element_type=jnp.float32) * scale
        tpos = t0 + jnp.arange(Q_BLOCK)
        causal = kpos[None, :] < tpos[:, None]
        log_keep = jnp.where(causal, jax.nn.log_sigmoid(-z), 0.0)
        log_after = lax.cumsum(log_keep, axis=3, reverse=True) - log_keep
        a = jnp.where(causal, jnp.exp(jax.nn.log_sigmoid(z) + log_after), 0.0)
        return jnp.einsum('bhqk,bkhd->bqhd', a.astype(v.dtype), v)

    out = lax.map(block, (qb, jnp.arange(nb) * Q_BLOCK))
    return out.transpose(1, 0, 2, 3, 4).reshape(b, s, h * dh)


def dilated_group_attention(q, k, v, window, dilation, slopes):
    b, s, h, dh = q.shape
    L = s // dilation
    w = window // dilation
    blk = w
    nb = -(-L // blk)
    lp = nb * blk

    def to_sub(t):
        t = t.reshape(b, L, dilation, h, dh).transpose(0, 2, 3, 1, 4)
        return jnp.pad(t, ((0, 0), (0, 0), (0, 0), (0, lp - L), (0, 0)))

    def band(t):
        t = jnp.pad(t, ((0, 0), (0, 0), (0, 0), (blk, 0), (0, 0))).reshape(b, dilation, h, nb + 1, blk, dh)
        return jnp.concatenate([t[:, :, :, :-1], t[:, :, :, 1:]], axis=4)

    qb = to_sub(q).reshape(b, dilation, h, nb, blk, dh)
    kb = band(to_sub(k))
    vb = band(to_sub(v))
    scores = jnp.einsum('brhnqd,brhnkd->brhnqk', qb, kb, preferred_element_type=jnp.float32) / math.sqrt(dh)
    qa = jnp.arange(blk)
    kc = jnp.arange(2 * blk)
    dist = blk + qa[:, None] - kc[None, :]
    key_idx = (jnp.arange(nb)[:, None] - 1) * blk + kc[None, :]
    valid = ((dist >= 0) & (dist <= w))[None, :, :] & (key_idx >= 0)[:, None, :]
    scores = scores - slopes[:, None, None, None] * (dist * dilation).astype(jnp.float32)
    scores = jnp.where(valid, scores, -jnp.inf)
    m = scores.max(-1)
    p = jnp.exp(scores - m[..., None])
    l = p.sum(-1)
    num = jnp.einsum('brhnqk,brhnkd->brhnqd', p, vb.astype(jnp.float32))

    def from_sub(t):
        t = t.reshape((b, dilation, h, lp) + t.shape[5:])[:, :, :, :L]
        t = jnp.moveaxis(t, 3, 1)
        return t.reshape((b, s, h) + t.shape[4:])

    return from_sub(num), from_sub(m), from_sub(l)


def dilated_mixture_attention(q, k, v):
    b, s, _, dh = q.shape
    slopes = alibi_slopes(DIL_HEADS)
    nums, ms, ls = [], [], []
    for g, (window, dilation) in enumerate(DIL_PAIRS):
        sl = slice(g * DIL_HEADS_PER_GROUP, (g + 1) * DIL_HEADS_PER_GROUP)
        n_g, m_g, l_g = dilated_group_attention(q[:, :, sl], k[:, :, sl], v[:, :, sl], window, dilation, slopes[sl])
        nums.append(n_g); ms.append(m_g); ls.append(l_g)
    m = jnp.stack(ms)
    wts = jnp.exp(m - m.max(0))
    den = (wts * jnp.stack(ls)).sum(0)
    num = (wts[..., None] * jnp.stack(nums)).sum(0)
    out = num / den[..., None]
    return out.reshape(b, s, DIL_OUT_WIDTH)


def _fwd_setup_inputs(seed: int = 0) -> dict:
    key = jax.random.key(seed)
    ks = jax.random.split(key, 11)
    f32 = jnp.float32

    def nrm(k, shape, fan_in):
        return jax.random.normal(k, shape, f32) * (fan_in ** -0.5)

    return {
        "x": jax.random.normal(ks[0], (BATCH, SEQ, D_MODEL), f32),
        "norm_mix_g": 1.0 + 0.01 * jax.random.normal(ks[1], (DEPTH, D_MODEL), f32),
        "w_in": nrm(ks[2], (DEPTH, D_MODEL, IN_WIDTH), D_MODEL),
        "w_sb_up": nrm(ks[3], (DEPTH, SB_WIDTH, D_MODEL), SB_WIDTH),
        "w_dil_up": nrm(ks[4], (DEPTH, DIL_OUT_WIDTH, D_MODEL), DIL_OUT_WIDTH),
        "w_out": nrm(ks[5], (DEPTH, D_MODEL, D_MODEL), D_MODEL),
        "norm_ffn_g": 1.0 + 0.01 * jax.random.normal(ks[6], (DEPTH, D_MODEL), f32),
        "w_ffn_in": nrm(ks[7], (DEPTH, D_MODEL, 2 * D_FF), D_MODEL),
        "w_ffn_out": nrm(ks[8], (DEPTH, D_FF, D_MODEL), D_FF),
        "norm_final_g": 1.0 + 0.01 * jax.random.normal(ks[9], (D_MODEL,), f32),
    }


def _fwd_reference(x, norm_mix_g, w_in, w_sb_up, w_dil_up, w_out, norm_ffn_g, w_ffn_in, w_ffn_out, norm_final_g):
    b, s, _ = x.shape
    for i in range(DEPTH):
        u = rms_norm(x, norm_mix_g[i])
        proj = u @ w_in[i]
        q_sb, k_sb, v_sb, q_dl, k_dl, v_dl, gate_sb, gate_dl = jnp.split(proj, SPLITS, axis=-1)
        heads_sb = lambda t: t.reshape(b, s, SB_HEADS, HEAD_DIM)
        heads_dl = lambda t: t.reshape(b, s, DIL_HEADS, HEAD_DIM)
        o_sb = stick_breaking_attention(heads_sb(q_sb), heads_sb(k_sb), heads_sb(v_sb))
        o_dl = dilated_mixture_attention(heads_dl(q_dl), heads_dl(k_dl), heads_dl(v_dl)).astype(x.dtype)
        y_sb = o_sb @ w_sb_up[i]
        y_dl = o_dl @ w_dil_up[i]
        merged = jax.nn.sigmoid(gate_sb) * y_sb + jax.nn.sigmoid(gate_dl) * y_dl
        x = x + merged @ w_out[i]
        u2 = rms_norm(x, norm_ffn_g[i])
        g_ff, up_ff = jnp.split(u2 @ w_ffn_in[i], 2, axis=-1)
        x = x + (jax.nn.silu(g_ff) * up_ff) @ w_ffn_out[i]
    return rms_norm(x, norm_final_g)


import jax as _jax
import jax.numpy as _jnp

TWIN_FORMAT = 'train_step'
FWD_PARAMS = ['x', 'norm_mix_g', 'w_in', 'w_sb_up', 'w_dil_up', 'w_out', 'norm_ffn_g', 'w_ffn_in', 'w_ffn_out', 'norm_final_g']
TWIN_WEIGHTS = ['norm_mix_g', 'w_in', 'w_sb_up', 'w_dil_up', 'w_out', 'norm_ffn_g', 'w_ffn_in', 'w_ffn_out', 'norm_final_g']
TWIN_DIFF_INPUT = 'x'
TWIN_INPUTS = ['x', 'norm_mix_g', 'w_in', 'w_sb_up', 'w_dil_up', 'w_out', 'norm_ffn_g', 'w_ffn_in', 'w_ffn_out', 'norm_final_g', 'loss_target', 'm_norm_mix_g', 'm_w_in', 'm_w_sb_up', 'm_w_dil_up', 'm_w_out', 'm_norm_ffn_g', 'm_w_ffn_in', 'm_w_ffn_out', 'm_norm_final_g', 'v_norm_mix_g', 'v_w_in', 'v_w_sb_up', 'v_w_dil_up', 'v_w_out', 'v_norm_ffn_g', 'v_w_ffn_in', 'v_w_ffn_out', 'v_norm_final_g']
TWIN_OUTPUTS = ['loss', 'grad_x', 'grad_norm_mix_g', 'grad_w_in', 'grad_w_sb_up', 'grad_w_dil_up', 'grad_w_out', 'grad_norm_ffn_g', 'grad_w_ffn_in', 'grad_w_ffn_out', 'grad_norm_final_g', 'delta_norm_mix_g', 'delta_w_in', 'delta_w_sb_up', 'delta_w_dil_up', 'delta_w_out', 'delta_norm_ffn_g', 'delta_w_ffn_in', 'delta_w_ffn_out', 'delta_norm_final_g', 'new_m_norm_mix_g', 'new_m_w_in', 'new_m_w_sb_up', 'new_m_w_dil_up', 'new_m_w_out', 'new_m_norm_ffn_g', 'new_m_w_ffn_in', 'new_m_w_ffn_out', 'new_m_norm_final_g', 'new_v_norm_mix_g', 'new_v_w_in', 'new_v_w_sb_up', 'new_v_w_dil_up', 'new_v_w_out', 'new_v_norm_ffn_g', 'new_v_w_ffn_in', 'new_v_w_ffn_out', 'new_v_norm_final_g']
TWIN_LEAF_KINDS = {'loss': 'loss', 'grad_x': 'grad_x', 'grad_norm_mix_g': 'grad_w', 'grad_w_in': 'grad_w', 'grad_w_sb_up': 'grad_w', 'grad_w_dil_up': 'grad_w', 'grad_w_out': 'grad_w', 'grad_norm_ffn_g': 'grad_w', 'grad_w_ffn_in': 'grad_w', 'grad_w_ffn_out': 'grad_w', 'grad_norm_final_g': 'grad_w', 'delta_norm_mix_g': 'delta_w', 'delta_w_in': 'delta_w', 'delta_w_sb_up': 'delta_w', 'delta_w_dil_up': 'delta_w', 'delta_w_out': 'delta_w', 'delta_norm_ffn_g': 'delta_w', 'delta_w_ffn_in': 'delta_w', 'delta_w_ffn_out': 'delta_w', 'delta_norm_final_g': 'delta_w', 'new_m_norm_mix_g': 'new_m', 'new_m_w_in': 'new_m', 'new_m_w_sb_up': 'new_m', 'new_m_w_dil_up': 'new_m', 'new_m_w_out': 'new_m', 'new_m_norm_ffn_g': 'new_m', 'new_m_w_ffn_in': 'new_m', 'new_m_w_ffn_out': 'new_m', 'new_m_norm_final_g': 'new_m', 'new_v_norm_mix_g': 'new_v', 'new_v_w_in': 'new_v', 'new_v_w_sb_up': 'new_v', 'new_v_w_dil_up': 'new_v', 'new_v_w_out': 'new_v', 'new_v_norm_ffn_g': 'new_v', 'new_v_w_ffn_in': 'new_v', 'new_v_w_ffn_out': 'new_v', 'new_v_norm_final_g': 'new_v'}


def _forward(args):
    return _fwd_reference(*[args[k] for k in FWD_PARAMS])


def _output_shape():
    out = _jax.eval_shape(lambda: _forward(_fwd_setup_inputs(0)))
    return out.shape, out.dtype

N_MICROBATCH = 1
ADAM_LR = 0.001
ADAM_B1 = 0.9
ADAM_B2 = 0.999
ADAM_EPS = 1e-08
ADAM_WD = 0.01
ADAM_STEP = 10
PER_EXAMPLE_BATCH_AXIS = {'x': 0, 'loss_target': 0}
SHARED_INPUTS = []
_WEIGHT_DTYPES = {'norm_mix_g': _jnp.float32, 'w_in': _jnp.float32, 'w_sb_up': _jnp.float32, 'w_dil_up': _jnp.float32, 'w_out': _jnp.float32, 'norm_ffn_g': _jnp.float32, 'w_ffn_in': _jnp.float32, 'w_ffn_out': _jnp.float32, 'norm_final_g': _jnp.float32}
MOMENT_SCALE = {'norm_mix_g': 9.505111e-02, 'w_in': 3.996802e-02, 'w_sb_up': 6.427659e-02, 'w_dil_up': 3.285840e-02, 'w_out': 7.228431e-02, 'norm_ffn_g': 1.282035e-01, 'w_ffn_in': 5.272745e-02, 'w_ffn_out': 8.650068e-02, 'norm_final_g': 3.196470e+01}


def _to_microbatches(a, axis):
    t = _jnp.moveaxis(a, axis, 0)
    t = t.reshape((N_MICROBATCH, t.shape[0] // N_MICROBATCH) + t.shape[1:])
    return _jnp.moveaxis(t, 1, axis + 1)


def setup_inputs(seed: int = 0) -> dict:
    inp = _fwd_setup_inputs(seed)
    key = _jax.random.fold_in(_jax.random.key(seed), 7919)
    shape, _ = _output_shape()
    out = dict(inp)
    out["loss_target"] = _jax.random.normal(_jax.random.fold_in(key, 0), shape, _jnp.float32)
    for i, name in enumerate(TWIN_WEIGHTS):
        w = inp[name].astype(_jnp.float32)
        if MOMENT_SCALE is None:
            s = _jnp.sqrt(_jnp.mean(_jnp.square(w)) + 1e-30)
        else:
            s = MOMENT_SCALE[name]
        km, kv = _jax.random.split(_jax.random.fold_in(key, i + 1))
        out[name] = w
        out["m_" + name] = s * _jax.random.normal(km, w.shape, _jnp.float32)
        out["v_" + name] = (s * s) * _jax.random.uniform(kv, w.shape, _jnp.float32, 0.5, 1.5)
    if N_MICROBATCH > 1:
        for name, axis in PER_EXAMPLE_BATCH_AXIS.items():
            out[name] = _to_microbatches(out[name], axis)
    return {'x': out['x'], 'norm_mix_g': out['norm_mix_g'], 'w_in': out['w_in'], 'w_sb_up': out['w_sb_up'], 'w_dil_up': out['w_dil_up'], 'w_out': out['w_out'], 'norm_ffn_g': out['norm_ffn_g'], 'w_ffn_in': out['w_ffn_in'], 'w_ffn_out': out['w_ffn_out'], 'norm_final_g': out['norm_final_g'], 'loss_target': out['loss_target'], 'm_norm_mix_g': out['m_norm_mix_g'], 'm_w_in': out['m_w_in'], 'm_w_sb_up': out['m_w_sb_up'], 'm_w_dil_up': out['m_w_dil_up'], 'm_w_out': out['m_w_out'], 'm_norm_ffn_g': out['m_norm_ffn_g'], 'm_w_ffn_in': out['m_w_ffn_in'], 'm_w_ffn_out': out['m_w_ffn_out'], 'm_norm_final_g': out['m_norm_final_g'], 'v_norm_mix_g': out['v_norm_mix_g'], 'v_w_in': out['v_w_in'], 'v_w_sb_up': out['v_w_sb_up'], 'v_w_dil_up': out['v_w_dil_up'], 'v_w_out': out['v_w_out'], 'v_norm_ffn_g': out['v_norm_ffn_g'], 'v_w_ffn_in': out['v_w_ffn_in'], 'v_w_ffn_out': out['v_w_ffn_out'], 'v_norm_final_g': out['v_norm_final_g']}


def _loss(weights, diff, rest, loss_target):
    with _jax.named_scope("forward"):
        args = {**rest, TWIN_DIFF_INPUT: diff, **{k: w.astype(_WEIGHT_DTYPES[k]) for k, w in weights.items()}}
        y = _forward(args)
    with _jax.named_scope("loss_head"):
        err = _jnp.square(y.astype(_jnp.float32) - loss_target)
        return 0.5 * _jnp.sum(_jnp.mean(err, axis=-1)) if err.ndim else 0.5 * err


def _adamw(w, g, m, v):
    m = ADAM_B1 * m + (1.0 - ADAM_B1) * g
    v = ADAM_B2 * v + (1.0 - ADAM_B2) * _jnp.square(g)
    m_hat = m / (1.0 - ADAM_B1 ** ADAM_STEP)
    v_hat = v / (1.0 - ADAM_B2 ** ADAM_STEP)
    delta = -ADAM_LR * (m_hat / (_jnp.sqrt(v_hat) + ADAM_EPS) + ADAM_WD * w)
    return delta, m, v


def reference(x, norm_mix_g, w_in, w_sb_up, w_dil_up, w_out, norm_ffn_g, w_ffn_in, w_ffn_out, norm_final_g, loss_target, m_norm_mix_g, m_w_in, m_w_sb_up, m_w_dil_up, m_w_out, m_norm_ffn_g, m_w_ffn_in, m_w_ffn_out, m_norm_final_g, v_norm_mix_g, v_w_in, v_w_sb_up, v_w_dil_up, v_w_out, v_norm_ffn_g, v_w_ffn_in, v_w_ffn_out, v_norm_final_g):
    given = dict(x=x, norm_mix_g=norm_mix_g, w_in=w_in, w_sb_up=w_sb_up, w_dil_up=w_dil_up, w_out=w_out, norm_ffn_g=norm_ffn_g, w_ffn_in=w_ffn_in, w_ffn_out=w_ffn_out, norm_final_g=norm_final_g, loss_target=loss_target, m_norm_mix_g=m_norm_mix_g, m_w_in=m_w_in, m_w_sb_up=m_w_sb_up, m_w_dil_up=m_w_dil_up, m_w_out=m_w_out, m_norm_ffn_g=m_norm_ffn_g, m_w_ffn_in=m_w_ffn_in, m_w_ffn_out=m_w_ffn_out, m_norm_final_g=m_norm_final_g, v_norm_mix_g=v_norm_mix_g, v_w_in=v_w_in, v_w_sb_up=v_w_sb_up, v_w_dil_up=v_w_dil_up, v_w_out=v_w_out, v_norm_ffn_g=v_norm_ffn_g, v_w_ffn_in=v_w_ffn_in, v_w_ffn_out=v_w_ffn_out, v_norm_final_g=v_norm_final_g)
    weights = {n: given[n] for n in TWIN_WEIGHTS}
    shared = {n: given[n] for n in SHARED_INPUTS}
    per_example = {n: given[n] for n in ['x']}
    grad_fn = _jax.value_and_grad(_loss, argnums=(0, 1))

    def one_microbatch(ex, loss_target):
        ex = dict(ex)
        diff = ex.pop(TWIN_DIFF_INPUT)
        return grad_fn(weights, diff, {**shared, **ex}, loss_target)

    if N_MICROBATCH == 1:
        loss, (grad_w, grad_x) = one_microbatch(per_example, given["loss_target"])
    else:
        def body(carry, xs):
            loss_sum, grad_sum = carry
            l_k, (gw_k, gx_k) = one_microbatch(xs[0], xs[1])
            with _jax.named_scope("update"):
                return (loss_sum + l_k, _jax.tree.map(_jnp.add, grad_sum, gw_k)), gx_k

        init = (_jnp.zeros((), _jnp.float32), _jax.tree.map(_jnp.zeros_like, weights))
        (loss, grad_w), grad_x = _jax.lax.scan(body, init, (per_example, given["loss_target"]))
    with _jax.named_scope("update"):
        delta_w, new_m, new_v = {}, {}, {}
        for n in TWIN_WEIGHTS:
            delta_w[n], new_m[n], new_v[n] = _adamw(weights[n], grad_w[n], given["m_" + n], given["v_" + n])
    return (loss, grad_x, *[grad_w[n] for n in TWIN_WEIGHTS], *[delta_w[n] for n in TWIN_WEIGHTS],
            *[new_m[n] for n in TWIN_WEIGHTS], *[new_v[n] for n in TWIN_WEIGHTS])
```

```python
import math

import jax
import jax.numpy as jnp
from jax import lax
from jax.experimental import pallas as pl
from jax.experimental.pallas import tpu as pltpu

F32 = jnp.float32
BF16 = jnp.bfloat16

N_DEV = 8
D_MODEL = 1024
SEQ = 2048
B_LOC = 2
TOK = B_LOC * SEQ
HEAD_DIM = 64
SB_WIDTH = 512
DIL_WIDTH = 768
DIL_OUT = 256
QKV_WIDTH = 3 * SB_WIDTH + 3 * DIL_WIDTH
IN_WIDTH = QKV_WIDTH + 2 * D_MODEL
D_FF = 2816
DIL_PAIRS = ((128, 1), (512, 4), (2048, 16))
DIL_HEADS = 12
RMS_EPS = 1e-6
ALIBI_MAX_BIAS = 8.0
QK_SCALE = 1.0 / math.sqrt(HEAD_DIM)
BLK = 128
LANES = 128
NEG_BIG = -1e30

ADAM_LR = 0.001
ADAM_B1 = 0.9
ADAM_B2 = 0.999
ADAM_EPS = 1e-08
ADAM_WD = 0.01
ADAM_STEP = 10

VMEM_LIMIT = 56 * 1024 * 1024

SHARD_SHAPES = (
    ("w_in", (D_MODEL, IN_WIDTH // N_DEV)),
    ("w_sb_up", (SB_WIDTH, D_MODEL // N_DEV)),
    ("w_dil_up", (DIL_OUT, D_MODEL // N_DEV)),
    ("w_out", (D_MODEL // N_DEV, D_MODEL)),
    ("w_ffn_in", (D_MODEL, 2 * D_FF // N_DEV)),
    ("w_ffn_out", (D_FF // N_DEV, D_MODEL)),
)
COL_SHARDED = ("w_in", "w_sb_up", "w_dil_up", "w_ffn_in")
PACK_ROWS = sum(r * c for _, (r, c) in SHARD_SHAPES) // LANES


def _dot(a, b):
    return jnp.dot(a, b, preferred_element_type=F32)


def _dot_nt(a, b):
    return lax.dot_general(a, b, (((1,), (1,)), ((), ())), preferred_element_type=F32)


def _dot_tn(a, b):
    return lax.dot_general(a, b, (((0,), (0,)), ((), ())), preferred_element_type=F32)


def _softplus(z):
    return jnp.maximum(z, 0.0) + jnp.log1p(jnp.exp(-jnp.abs(z)))


def _sigmoid(z):
    return 1.0 / (1.0 + jnp.exp(-z))


def _split_bf16(v):
    hi = v.astype(BF16)
    lo = (v - hi.astype(F32)).astype(BF16)
    return hi, lo


def _chunks(width, step=512):
    out, c = [], 0
    while c < width:
        w = min(step, width - c)
        out.append((c, w))
        c += w
    return out


def _resident(shape):
    nd = len(shape)
    return pl.BlockSpec(shape, lambda *_: (0,) * nd, pipeline_mode=pl.Buffered(1))


def _params(sem):
    return pltpu.CompilerParams(dimension_semantics=sem, vmem_limit_bytes=VMEM_LIMIT)


def _rms_fwd(x, g):
    r = lax.rsqrt(jnp.mean(x * x, axis=-1, keepdims=True) + RMS_EPS)
    n = x * r
    return n, r, n * g


def _rms_bwd(dy, n, r, g):
    dg = jnp.sum(dy * n, axis=0, keepdims=True)
    dn = dy * g
    dx = r * (dn - n * jnp.mean(dn * n, axis=-1, keepdims=True))
    return dx, dg


TM = 256


def _norm_proj(x, g, w_in):
    def body(x_ref, g_ref, w_ref, qkv_ref, gate_ref, u_ref):
        _, _, u = _rms_fwd(x_ref[...], g_ref[...])
        u = u.astype(BF16)
        u_ref[...] = u
        for c0, w in _chunks(QKV_WIDTH):
            qkv_ref[:, c0:c0 + w] = _dot(u, w_ref[:, c0:c0 + w]).astype(BF16)
        for c0, w in _chunks(2 * D_MODEL):
            gate_ref[:, c0:c0 + w] = _dot(u, w_ref[:, QKV_WIDTH + c0:QKV_WIDTH + c0 + w])

    return pl.pallas_call(
        body, name="norm_proj", grid=(TOK // TM,),
        in_specs=[pl.BlockSpec((TM, D_MODEL), lambda i: (i, 0)), _resident((1, D_MODEL)),
                  _resident((D_MODEL, IN_WIDTH))],
        out_specs=[pl.BlockSpec((TM, QKV_WIDTH), lambda i: (i, 0)),
                   pl.BlockSpec((TM, 2 * D_MODEL), lambda i: (i, 0)),
                   pl.BlockSpec((TM, D_MODEL), lambda i: (i, 0))],
        out_shape=[jax.ShapeDtypeStruct((TOK, QKV_WIDTH), BF16),
                   jax.ShapeDtypeStruct((TOK, 2 * D_MODEL), F32),
                   jax.ShapeDtypeStruct((TOK, D_MODEL), BF16)],
        compiler_params=_params(("parallel",)),
    )(x, g, w_in)


def _mix_out(x, o_sb, o_dl, gates, w_sb_up, w_dil_up, w_out):
    def body(x_ref, osb_ref, odl_ref, gate_ref, wsb_ref, wdl_ref, wout_ref, x1_ref, mg_ref):
        y_sb = _dot(osb_ref[...], wsb_ref[...])
        y_dl = _dot(odl_ref[...].astype(BF16), wdl_ref[...])
        merged = (_sigmoid(gate_ref[:, :D_MODEL]) * y_sb
                  + _sigmoid(gate_ref[:, D_MODEL:]) * y_dl).astype(BF16)
        mg_ref[...] = merged
        x1_ref[...] = x_ref[...] + _dot(merged, wout_ref[...])

    return pl.pallas_call(
        body, name="mix_out", grid=(TOK // TM,),
        in_specs=[pl.BlockSpec((TM, D_MODEL), lambda i: (i, 0)),
                  pl.BlockSpec((TM, SB_WIDTH), lambda i: (i, 0)),
                  pl.BlockSpec((TM, DIL_OUT), lambda i: (i, 0)),
                  pl.BlockSpec((TM, 2 * D_MODEL), lambda i: (i, 0)),
                  _resident((SB_WIDTH, D_MODEL)), _resident((DIL_OUT, D_MODEL)),
                  _resident((D_MODEL, D_MODEL))],
        out_specs=[pl.BlockSpec((TM, D_MODEL), lambda i: (i, 0)),
                   pl.BlockSpec((TM, D_MODEL), lambda i: (i, 0))],
        out_shape=[jax.ShapeDtypeStruct((TOK, D_MODEL), F32),
                   jax.ShapeDtypeStruct((TOK, D_MODEL), BF16)],
        compiler_params=_params(("parallel",)),
    )(x, o_sb, o_dl, gates, w_sb_up, w_dil_up, w_out)


FF_CHUNK = D_FF // 2


def _ffn_fwd_bwd(x1, target, g_ffn, g_fin, w_ffn_in, w_ffn_out):
    def body(x1_ref, t_ref, gffn_ref, gfin_ref, win_ref, wout_ref,
             loss_ref, dx1_ref, u2_ref, act_ref, dh_ref, dx2_ref, dgfin_ref, dgffn_ref, h_scr):
        i = pl.program_id(0)

        @pl.when(i == 0)
        def _():
            loss_ref[...] = jnp.zeros_like(loss_ref)
            dgfin_ref[...] = jnp.zeros_like(dgfin_ref)
            dgffn_ref[...] = jnp.zeros_like(dgffn_ref)

        x1 = x1_ref[...]
        g_ffn_v = gffn_ref[...]
        g_fin_v = gfin_ref[...]
        n2, r2, u2 = _rms_fwd(x1, g_ffn_v)
        u2 = u2.astype(BF16)
        u2_ref[...] = u2
        x2 = x1
        for c0 in range(0, D_FF, FF_CHUNK):
            gate = _dot(u2, win_ref[:, c0:c0 + FF_CHUNK])
            up = _dot(u2, win_ref[:, D_FF + c0:D_FF + c0 + FF_CHUNK])
            h_scr[:, c0:c0 + FF_CHUNK] = gate
            h_scr[:, D_FF + c0:D_FF + c0 + FF_CHUNK] = up
            act = (gate * _sigmoid(gate) * up).astype(BF16)
            act_ref[:, c0:c0 + FF_CHUNK] = act
            x2 = x2 + _dot(act, wout_ref[c0:c0 + FF_CHUNK, :])
        n3, r3, y = _rms_fwd(x2, g_fin_v)
        err = y - t_ref[...]
        sq = jnp.sum(jnp.sum(err * err, axis=1, keepdims=True), axis=0, keepdims=True)
        loss_ref[...] += sq * (0.5 / D_MODEL)
        dx2, dgfin = _rms_bwd(err * (1.0 / D_MODEL), n3, r3, g_fin_v)
        dgfin_ref[...] += dgfin
        dx2_b = dx2.astype(BF16)
        dx2_ref[...] = dx2_b
        du2 = jnp.zeros((TM, D_MODEL), F32)
        for c0 in range(0, D_FF, FF_CHUNK):
            gate = h_scr[:, c0:c0 + FF_CHUNK]
            up = h_scr[:, D_FF + c0:D_FF + c0 + FF_CHUNK]
            dact = _dot_nt(dx2_b, wout_ref[c0:c0 + FF_CHUNK, :])
            sg = _sigmoid(gate)
            dgate = (dact * up * (sg * (1.0 + gate * (1.0 - sg)))).astype(BF16)
            dup = (dact * (gate * sg)).astype(BF16)
            dh_ref[:, c0:c0 + FF_CHUNK] = dgate
            dh_ref[:, D_FF + c0:D_FF + c0 + FF_CHUNK] = dup
            du2 = du2 + _dot_nt(dgate, win_ref[:, c0:c0 + FF_CHUNK])
            du2 = du2 + _dot_nt(dup, win_ref[:, D_FF + c0:D_FF + c0 + FF_CHUNK])
        dx1_n, dgffn = _rms_bwd(du2, n2, r2, g_ffn_v)
        dgffn_ref[...] += dgffn
        dx1_ref[...] = dx2 + dx1_n

    tile = lambda w: pl.BlockSpec((TM, w), lambda i: (i, 0))
    acc = lambda w: pl.BlockSpec((1, w), lambda i: (0, 0))
    return pl.pallas_call(
        body, name="ffn_fwd_bwd", grid=(TOK // TM,),
        in_specs=[tile(D_MODEL), tile(D_MODEL), _resident((1, D_MODEL)), _resident((1, D_MODEL)),
                  _resident((D_MODEL, 2 * D_FF)), _resident((D_FF, D_MODEL))],
        out_specs=[acc(LANES), tile(D_MODEL), tile(D_MODEL), tile(D_FF), tile(2 * D_FF), tile(D_MODEL),
                   acc(D_MODEL), acc(D_MODEL)],
        out_shape=[jax.ShapeDtypeStruct((1, LANES), F32),
                   jax.ShapeDtypeStruct((TOK, D_MODEL), F32),
                   jax.ShapeDtypeStruct((TOK, D_MODEL), BF16),
                   jax.ShapeDtypeStruct((TOK, D_FF), BF16),
                   jax.ShapeDtypeStruct((TOK, 2 * D_FF), BF16),
                   jax.ShapeDtypeStruct((TOK, D_MODEL), BF16),
                   jax.ShapeDtypeStruct((1, D_MODEL), F32),
                   jax.ShapeDtypeStruct((1, D_MODEL), F32)],
        scratch_shapes=[pltpu.VMEM((TM, 2 * D_FF), F32)],
        compiler_params=_params(("arbitrary",)),
    )(x1, target, g_ffn, g_fin, w_ffn_in, w_ffn_out)


def _mix_bwd(dx1, o_sb, o_dl, gates, w_sb_up, w_dil_up, w_out):
    def body(dx1_ref, osb_ref, odl_ref, gate_ref, wsb_ref, wdl_ref, wout_ref,
             dgate_ref, dysb_ref, dydl_ref, dosb_ref, dodl_ref, dsum_ref):
        dmerged = _dot_nt(dx1_ref[...].astype(BF16), wout_ref[...])
        o_dl = odl_ref[...]
        y_sb = _dot(osb_ref[...], wsb_ref[...])
        y_dl = _dot(o_dl.astype(BF16), wdl_ref[...])
        s_sb = _sigmoid(gate_ref[:, :D_MODEL])
        s_dl = _sigmoid(gate_ref[:, D_MODEL:])
        dgate_ref[:, :D_MODEL] = (dmerged * y_sb * (s_sb * (1.0 - s_sb))).astype(BF16)
        dgate_ref[:, D_MODEL:] = (dmerged * y_dl * (s_dl * (1.0 - s_dl))).astype(BF16)
        dy_sb = (dmerged * s_sb).astype(BF16)
        dy_dl = (dmerged * s_dl).astype(BF16)
        dysb_ref[...] = dy_sb
        dydl_ref[...] = dy_dl
        dosb_ref[...] = _dot_nt(dy_sb, wsb_ref[...]).astype(BF16)
        do_dl = _dot_nt(dy_dl, wdl_ref[...])
        dodl_ref[...] = do_dl
        row = lax.broadcasted_iota(jnp.int32, (DIL_OUT, DIL_OUT), 0) // HEAD_DIM
        col = lax.broadcasted_iota(jnp.int32, (DIL_OUT, DIL_OUT), 1) // HEAD_DIM
        same_head = (row == col).astype(BF16)
        hi, lo = _split_bf16(do_dl * o_dl)
        dsum_ref[...] = _dot(hi, same_head) + _dot(lo, same_head)

    tile = lambda w: pl.BlockSpec((TM, w), lambda i: (i, 0))
    return pl.pallas_call(
        body, name="mix_bwd", grid=(TOK // TM,),
        in_specs=[tile(D_MODEL), tile(SB_WIDTH), tile(DIL_OUT), tile(2 * D_MODEL),
                  _resident((SB_WIDTH, D_MODEL)), _resident((DIL_OUT, D_MODEL)),
                  _resident((D_MODEL, D_MODEL))],
        out_specs=[tile(2 * D_MODEL), tile(D_MODEL), tile(D_MODEL), tile(SB_WIDTH), tile(DIL_OUT),
                   tile(DIL_OUT)],
        out_shape=[jax.ShapeDtypeStruct((TOK, 2 * D_MODEL), BF16),
                   jax.ShapeDtypeStruct((TOK, D_MODEL), BF16),
                   jax.ShapeDtypeStruct((TOK, D_MODEL), BF16),
                   jax.ShapeDtypeStruct((TOK, SB_WIDTH), BF16),
                   jax.ShapeDtypeStruct((TOK, DIL_OUT), F32),
                   jax.ShapeDtypeStruct((TOK, DIL_OUT), F32)],
        compiler_params=_params(("parallel",)),
    )(dx1, o_sb, o_dl, gates, w_sb_up, w_dil_up, w_out)


def _proj_bwd(dproj, dx1, x, g, w_in):
    def body(dp_ref, dx1_ref, x_ref, g_ref, w_ref, dx_ref, dg_ref):
        @pl.when(pl.program_id(0) == 0)
        def _():
            dg_ref[...] = jnp.zeros_like(dg_ref)

        du = jnp.zeros((TM, D_MODEL), F32)
        for c0, w in _chunks(IN_WIDTH, 1024):
            du = du + _dot_nt(dp_ref[:, c0:c0 + w], w_ref[:, c0:c0 + w])
        g_v = g_ref[...]
        n, r, _ = _rms_fwd(x_ref[...], g_v)
        dx, dg = _rms_bwd(du, n, r, g_v)
        dg_ref[...] += dg
        dx_ref[...] = dx1_ref[...] + dx

    tile = lambda w: pl.BlockSpec((TM, w), lambda i: (i, 0))
    return pl.pallas_call(
        body, name="proj_bwd", grid=(TOK // TM,),
        in_specs=[tile(IN_WIDTH), tile(D_MODEL), tile(D_MODEL), _resident((1, D_MODEL)),
                  _resident((D_MODEL, IN_WIDTH))],
        out_specs=[tile(D_MODEL), pl.BlockSpec((1, D_MODEL), lambda i: (0, 0))],
        out_shape=[jax.ShapeDtypeStruct((TOK, D_MODEL), F32),
                   jax.ShapeDtypeStruct((1, D_MODEL), F32)],
        compiler_params=_params(("arbitrary",)),
    )(dproj, dx1, x, g, w_in)


def _atb(a, b, name, tm, tn, tk=512):
    m, n = a.shape[1], b.shape[1]
    nk = TOK // tk

    def body(a_ref, b_ref, o_ref):
        @pl.when(pl.program_id(2) == 0)
        def _():
            o_ref[...] = jnp.zeros_like(o_ref)

        o_ref[...] += _dot_tn(a_ref[...], b_ref[...])

    return pl.pallas_call(
        body, name=name, grid=(m // tm, n // tn, nk),
        in_specs=[pl.BlockSpec((tk, tm), lambda i, j, k: (k, i)),
                  pl.BlockSpec((tk, tn), lambda i, j, k: (k, j))],
        out_specs=pl.BlockSpec((tm, tn), lambda i, j, k: (i, j)),
        out_shape=jax.ShapeDtypeStruct((m, n), F32),
        compiler_params=_params(("parallel", "parallel", "arbitrary")),
    )(a, b)


N_QBLK = SEQ // BLK
SB_PAIRS = SB_WIDTH // LANES


def _sb_masks():
    row = lax.broadcasted_iota(jnp.int32, (BLK, BLK), 0)
    col = lax.broadcasted_iota(jnp.int32, (BLK, BLK), 1)
    return row, col, col < HEAD_DIM


def _two_heads(v, lane0):
    zero = jnp.zeros_like(v)
    return jnp.where(lane0, v, zero), jnp.where(lane0, zero, v)


def _sb_fwd(qkv):
    def body(q_ref, k_ref, v_ref, o_ref):
        i = pl.program_id(2)
        row, col, lane0 = _sb_masks()
        later = (row > col).astype(BF16)
        q_h = _two_heads(q_ref[0], lane0)

        def step(jj, carry):
            j = i - jj
            off = pl.multiple_of(j * BLK, BLK)
            k_j = k_ref[0, pl.ds(off, BLK), :]
            v_j = v_ref[0, pl.ds(off, BLK), :]
            causal = jnp.logical_or(col < row, j < i)
            out = []
            for h in range(2):
                acc, run = carry[h]
                z = _dot_nt(q_h[h], k_j) * QK_SCALE
                log_keep = jnp.where(causal, -_softplus(z), 0.0)
                hi, lo = _split_bf16(log_keep)
                log_after = _dot(hi, later) + _dot(lo, later) + run
                a = jnp.where(causal, jnp.exp(log_keep + z + log_after), 0.0)
                acc = acc + _dot(a.astype(BF16), v_j)
                run = run + jnp.sum(log_keep, axis=1, keepdims=True)
                out.append((acc, run))
            return tuple(out)

        init = tuple((jnp.zeros((BLK, LANES), F32), jnp.zeros((BLK, 1), F32)) for _ in range(2))
        res = lax.fori_loop(0, i + 1, step, init)
        o_ref[0] = jnp.where(lane0, res[0][0], res[1][0]).astype(BF16)

    return pl.pallas_call(
        body, name="sb_fwd", grid=(B_LOC, SB_PAIRS, N_QBLK),
        in_specs=[pl.BlockSpec((1, BLK, LANES), lambda b, h, i: (b, i, h)),
                  pl.BlockSpec((1, SEQ, LANES), lambda b, h, i: (b, 0, SB_PAIRS + h)),
                  pl.BlockSpec((1, SEQ, LANES), lambda b, h, i: (b, 0, 2 * SB_PAIRS + h))],
        out_specs=pl.BlockSpec((1, BLK, LANES), lambda b, h, i: (b, i, h)),
        out_shape=jax.ShapeDtypeStruct((B_LOC, SEQ, SB_WIDTH), BF16),
        compiler_params=_params(("parallel", "parallel", "arbitrary")),
    )(qkv, qkv, qkv)


def _sb_bwd(qkv, d_o):
    def body(q_ref, k_ref, v_ref, do_ref, dq_ref, dk_ref, dv_ref, dk_acc, dv_acc, z_scr, run_scr):
        i = pl.program_id(2)
        row, col, lane0 = _sb_masks()
        later = (row > col).astype(BF16)
        earlier = (row < col).astype(BF16)
        q_h = _two_heads(q_ref[0], lane0)
        do_h = _two_heads(do_ref[0], lane0)

        @pl.when(i == 0)
        def _():
            dk_acc[...] = jnp.zeros_like(dk_acc)
            dv_acc[...] = jnp.zeros_like(dv_acc)

        def down(jj, runs):
            j = i - jj
            off = pl.multiple_of(j * BLK, BLK)
            k_j = k_ref[0, pl.ds(off, BLK), :]
            causal = jnp.logical_or(col < row, j < i)
            out = []
            for h in range(2):
                z = _dot_nt(q_h[h], k_j) * QK_SCALE
                z_scr[h, j] = z
                run_scr[h, j] = jnp.broadcast_to(runs[h], (BLK, BLK))
                log_keep = jnp.where(causal, -_softplus(z), 0.0)
                out.append(runs[h] + jnp.sum(log_keep, axis=1, keepdims=True))
            return tuple(out)

        zero_col = jnp.zeros((BLK, 1), F32)
        lax.fori_loop(0, i + 1, down, (zero_col, zero_col))

        def up(j, carry):
            dq, pre = carry
            off = pl.multiple_of(j * BLK, BLK)
            k_j = k_ref[0, pl.ds(off, BLK), :]
            v_j = v_ref[0, pl.ds(off, BLK), :]
            k_h = _two_heads(k_j, lane0)
            causal = jnp.logical_or(col < row, j < i)
            new_pre = []
            dk_j = jnp.zeros((BLK, LANES), F32)
            dv_j = jnp.zeros((BLK, LANES), F32)
            for h in range(2):
                z = z_scr[h, j]
                log_keep = jnp.where(causal, -_softplus(z), 0.0)
                log_beta = log_keep + z
                hi, lo = _split_bf16(log_keep)
                log_after = _dot(hi, later) + _dot(lo, later) + run_scr[h, j]
                a = jnp.where(causal, jnp.exp(log_beta + log_after), 0.0)
                e = _dot_nt(do_h[h], v_j) * a
                hi, lo = _split_bf16(e)
                before = _dot(hi, earlier) + _dot(lo, earlier) + pre[h]
                beta = jnp.exp(log_beta)
                dz = jnp.where(causal, e * (1.0 - beta) - before * beta, 0.0) * QK_SCALE
                dz = dz.astype(BF16)
                a = a.astype(BF16)
                dq = dq + _dot(dz, k_h[h])
                dk_j = dk_j + _dot_tn(dz, q_h[h])
                dv_j = dv_j + _dot_tn(a, do_h[h])
                new_pre.append(pre[h] + jnp.sum(e, axis=1, keepdims=True))
            dk_acc[pl.ds(off, BLK), :] += dk_j
            dv_acc[pl.ds(off, BLK), :] += dv_j
            return dq, tuple(new_pre)

        dq, _ = lax.fori_loop(0, i + 1, up, (jnp.zeros((BLK, LANES), F32), (zero_col, zero_col)))
        dq_ref[0] = dq.astype(BF16)

        @pl.when(i == N_QBLK - 1)
        def _():
            dk_ref[0] = dk_acc[...].astype(BF16)
            dv_ref[0] = dv_acc[...].astype(BF16)

    blk = pl.BlockSpec((1, BLK, LANES), lambda b, h, i: (b, i, h))
    whole = lambda c: pl.BlockSpec((1, SEQ, LANES), lambda b, h, i: (b, 0, c * SB_PAIRS + h))
    out = jax.ShapeDtypeStruct((B_LOC, SEQ, SB_WIDTH), BF16)
    return pl.pallas_call(
        body, name="sb_bwd", grid=(B_LOC, SB_PAIRS, N_QBLK),
        in_specs=[blk, whole(1), whole(2), blk],
        out_specs=[blk, whole(0), whole(0)],
        out_shape=[out, out, out],
        scratch_shapes=[pltpu.VMEM((SEQ, LANES), F32), pltpu.VMEM((SEQ, LANES), F32),
                        pltpu.VMEM((2, N_QBLK, BLK, BLK), F32), pltpu.VMEM((2, N_QBLK, BLK, BLK), F32)],
        compiler_params=_params(("parallel", "parallel", "arbitrary")),
    )(qkv, qkv, qkv, d_o)


def _dil_scores(q_h, k_cur, k_prev, slope, dilation, has_prev, row, col):
    dist = (row - col).astype(F32) * float(dilation)
    s_cur = _dot_nt(q_h, k_cur) * QK_SCALE - slope * dist
    s_prev = _dot_nt(q_h, k_prev) * QK_SCALE - slope * (dist + float(BLK * dilation))
    s_cur = jnp.where(col <= row, s_cur, NEG_BIG)
    s_prev = jnp.where(jnp.logical_and(col >= row, has_prev), s_prev, NEG_BIG)
    return s_cur, s_prev


def _dil_slopes(group):
    pair = pl.program_id(1) % 2
    first = float(4 * group + 1) + 2.0 * pair.astype(F32)
    coef = -ALIBI_MAX_BIAS / DIL_HEADS * math.log(2.0)
    return jnp.exp(coef * first), jnp.exp(coef * (first + 1.0))


def _head_col(v, lane_mask):
    return jnp.max(jnp.where(lane_mask, v, NEG_BIG), axis=1, keepdims=True)


def _dil_fwd(q, k, v, group):
    dilation = DIL_PAIRS[group][1]
    length = SEQ // dilation
    nblk = length // BLK

    def body(q_ref, k_ref, v_ref, o_ref, lse_ref):
        row, col, lane0 = _sb_masks()
        slopes = _dil_slopes(group)

        def step(n, _):
            off = pl.multiple_of(n * BLK, BLK)
            off_prev = pl.multiple_of(jnp.maximum(n - 1, 0) * BLK, BLK)
            q_h = _two_heads(q_ref[0, pl.ds(off, BLK), :], lane0)
            k_cur = k_ref[0, pl.ds(off, BLK), :]
            v_cur = v_ref[0, pl.ds(off, BLK), :]
            k_prev = k_ref[0, pl.ds(off_prev, BLK), :]
            v_prev = v_ref[0, pl.ds(off_prev, BLK), :]
            outs, lses = [], []
            for h in range(2):
                s_cur, s_prev = _dil_scores(q_h[h], k_cur, k_prev, slopes[h], dilation, n > 0, row, col)
                m = jnp.maximum(jnp.max(s_cur, axis=1, keepdims=True), jnp.max(s_prev, axis=1, keepdims=True))
                p_cur = jnp.exp(s_cur - m)
                p_prev = jnp.exp(s_prev - m)
                den = jnp.sum(p_cur, axis=1, keepdims=True) + jnp.sum(p_prev, axis=1, keepdims=True)
                num = _dot(p_cur.astype(BF16), v_cur) + _dot(p_prev.astype(BF16), v_prev)
                outs.append(num / den)
                lses.append(m + jnp.log(den))
            o_ref[0, pl.ds(off, BLK), :] = jnp.where(lane0, outs[0], outs[1])
            lse_ref[0, pl.ds(off, BLK), :] = jnp.where(lane0, lses[0], lses[1])
            return 0

        lax.fori_loop(0, nblk, step, 0)

    spec = pl.BlockSpec((1, length, LANES), lambda b, c: (b, 0, c))
    out = jax.ShapeDtypeStruct((B_LOC, length, dilation * DIL_OUT), F32)
    return pl.pallas_call(
        body, name=f"dil_fwd_{group}", grid=(B_LOC, 2 * dilation),
        in_specs=[spec, spec, spec], out_specs=[spec, spec], out_shape=[out, out],
        compiler_params=_params(("parallel", "parallel")),
    )(q, k, v)


def _dil_bwd(q, k, v, d_o, lse, dsum, group):
    dilation = DIL_PAIRS[group][1]
    length = SEQ // dilation
    nblk = length // BLK

    def body(q_ref, k_ref, v_ref, do_ref, lse_ref, dsum_ref, dq_ref, dk_ref, dv_ref, dk_acc, dv_acc):
        row, col, lane0 = _sb_masks()
        lane1 = jnp.logical_not(lane0)
        slopes = _dil_slopes(group)
        dk_acc[...] = jnp.zeros_like(dk_acc)
        dv_acc[...] = jnp.zeros_like(dv_acc)

        def step(n, _):
            off = pl.multiple_of(n * BLK, BLK)
            off_prev = pl.multiple_of(jnp.maximum(n - 1, 0) * BLK, BLK)
            q_h = _two_heads(q_ref[0, pl.ds(off, BLK), :], lane0)
            do_h = _two_heads(do_ref[0, pl.ds(off, BLK), :].astype(BF16), lane0)
            k_cur = k_ref[0, pl.ds(off, BLK), :]
            v_cur = v_ref[0, pl.ds(off, BLK), :]
            k_prev = k_ref[0, pl.ds(off_prev, BLK), :]
            v_prev = v_ref[0, pl.ds(off_prev, BLK), :]
            kc_h = _two_heads(k_cur, lane0)
            kp_h = _two_heads(k_prev, lane0)
            lse_blk = lse_ref[0, pl.ds(off, BLK), :]
            dsum_blk = dsum_ref[0, pl.ds(off, BLK), :]
            dq = jnp.zeros((BLK, LANES), F32)
            dk_c = jnp.zeros((BLK, LANES), F32)
            dk_p = jnp.zeros((BLK, LANES), F32)
            dv_c = jnp.zeros((BLK, LANES), F32)
            dv_p = jnp.zeros((BLK, LANES), F32)
            for h, lanes in enumerate((lane0, lane1)):
                s_cur, s_prev = _dil_scores(q_h[h], k_cur, k_prev, slopes[h], dilation, n > 0, row, col)
                lse_h = _head_col(lse_blk, lanes)
                dsum_h = _head_col(dsum_blk, lanes)
                p_cur = jnp.exp(s_cur - lse_h)
                p_prev = jnp.exp(s_prev - lse_h)
                ds_cur = (p_cur * (_dot_nt(do_h[h], v_cur) - dsum_h) * QK_SCALE).astype(BF16)
                ds_prev = (p_prev * (_dot_nt(do_h[h], v_prev) - dsum_h) * QK_SCALE).astype(BF16)
                dq = dq + _dot(ds_cur, kc_h[h]) + _dot(ds_prev, kp_h[h])
                dk_c = dk_c + _dot_tn(ds_cur, q_h[h])
                dk_p = dk_p + _dot_tn(ds_prev, q_h[h])
                dv_c = dv_c + _dot_tn(p_cur.astype(BF16), do_h[h])
                dv_p = dv_p + _dot_tn(p_prev.astype(BF16), do_h[h])
            dq_ref[0, pl.ds(off, BLK), :] = dq.astype(BF16)
            dk_acc[pl.ds(off, BLK), :] += dk_c
            dv_acc[pl.ds(off, BLK), :] += dv_c
            dk_acc[pl.ds(off_prev, BLK), :] += dk_p
            dv_acc[pl.ds(off_prev, BLK), :] += dv_p
            return 0

        lax.fori_loop(0, nblk, step, 0)
        dk_ref[0] = dk_acc[...].astype(BF16)
        dv_ref[0] = dv_acc[...].astype(BF16)

    spec = pl.BlockSpec((1, length, LANES), lambda b, c: (b, 0, c))
    out = jax.ShapeDtypeStruct((B_LOC, length, dilation * DIL_OUT), BF16)
    return pl.pallas_call(
        body, name=f"dil_bwd_{group}", grid=(B_LOC, 2 * dilation),
        in_specs=[spec] * 6, out_specs=[spec] * 3, out_shape=[out] * 3,
        scratch_shapes=[pltpu.VMEM((length, LANES), F32), pltpu.VMEM((length, LANES), F32)],
        compiler_params=_params(("parallel", "parallel")),
    )(q, k, v, d_o, lse, dsum)


def _dil_combine(outs, lses):
    def body(o0, o1, o2, l0, l1, l2, o_ref, lse_ref):
        ls = (l0[...], l1[...], l2[...])
        m = jnp.maximum(jnp.maximum(ls[0], ls[1]), ls[2])
        w = [jnp.exp(l - m) for l in ls]
        den = w[0] + w[1] + w[2]
        o_ref[...] = (w[0] * o0[...] + w[1] * o1[...] + w[2] * o2[...]) / den
        lse_ref[...] = m + jnp.log(den)

    tile = pl.BlockSpec((512, DIL_OUT), lambda i: (i, 0))
    out = jax.ShapeDtypeStruct((TOK, DIL_OUT), F32)
    return pl.pallas_call(
        body, name="dil_combine", grid=(TOK // 512,),
        in_specs=[tile] * 6, out_specs=[tile, tile], out_shape=[out, out],
        compiler_params=_params(("parallel",)),
    )(*outs, *lses)


def _to_residues(t, dilation):
    return t.reshape(B_LOC, SEQ // dilation, dilation * DIL_OUT)


def _peers():
    x, y, c = lax.axis_index("x"), lax.axis_index("y"), lax.axis_index("c")
    me = 4 * x + 2 * y + c
    peers = []
    for mask in range(1, N_DEV):
        px = 1 - x if mask & 4 else x
        py = 1 - y if mask & 2 else y
        pc = 1 - c if mask & 1 else c
        peers.append(((px, py, pc), 4 * px + 2 * py + pc))
    return me, peers


def _all_gather(packed):
    rows = packed.shape[0]

    def body(src_ref, out_ref, send_sems, recv_sems, local_sem):
        me, peers = _peers()
        mine = pltpu.make_async_copy(src_ref, out_ref.at[me], local_sem)
        mine.start()
        sends = []
        for k, (peer, _) in enumerate(peers):
            cp = pltpu.make_async_remote_copy(
                src_ref=src_ref, dst_ref=out_ref.at[me], send_sem=send_sems.at[k], recv_sem=recv_sems.at[k],
                device_id=peer, device_id_type=pl.DeviceIdType.MESH)
            cp.start()
            sends.append(cp)
        for k, (peer, peer_idx) in enumerate(peers):
            pltpu.make_async_remote_copy(
                src_ref=src_ref, dst_ref=out_ref.at[peer_idx], send_sem=send_sems.at[k], recv_sem=recv_sems.at[k],
                device_id=peer, device_id_type=pl.DeviceIdType.MESH).wait_recv()
        for cp in sends:
            cp.wait_send()
        mine.wait()

    return pl.pallas_call(
        body, name="all_gather_weights",
        in_specs=[pl.BlockSpec(memory_space=pl.ANY)],
        out_specs=pl.BlockSpec(memory_space=pl.ANY),
        out_shape=jax.ShapeDtypeStruct((N_DEV, rows, LANES), packed.dtype),
        scratch_shapes=[pltpu.SemaphoreType.DMA((N_DEV - 1,)), pltpu.SemaphoreType.DMA((N_DEV - 1,)),
                        pltpu.SemaphoreType.DMA],
    )(packed)


def _exchange(blocks, gains):
    rows = blocks.shape[1]

    def body(blk_ref, gain_ref, blk_out, gain_out, send_sems, recv_sems, local_sems):
        me, peers = _peers()
        own = [pltpu.make_async_copy(blk_ref.at[me], blk_out.at[me], local_sems.at[0]),
               pltpu.make_async_copy(gain_ref, gain_out.at[me], local_sems.at[1])]
        for cp in own:
            cp.start()
        sends = []
        for k, (peer, peer_idx) in enumerate(peers):
            for a, (src, dst) in enumerate(((blk_ref.at[peer_idx], blk_out.at[me]), (gain_ref, gain_out.at[me]))):
                cp = pltpu.make_async_remote_copy(
                    src_ref=src, dst_ref=dst, send_sem=send_sems.at[a, k], recv_sem=recv_sems.at[a, k],
                    device_id=peer, device_id_type=pl.DeviceIdType.MESH)
                cp.start()
                sends.append(cp)
        for k, (peer, peer_idx) in enumerate(peers):
            for a, (src, dst) in enumerate(((blk_ref.at[peer_idx], blk_out.at[peer_idx]),
                                            (gain_ref, gain_out.at[peer_idx]))):
                pltpu.make_async_remote_copy(
                    src_ref=src, dst_ref=dst, send_sem=send_sems.at[a, k], recv_sem=recv_sems.at[a, k],
                    device_id=peer, device_id_type=pl.DeviceIdType.MESH).wait_recv()
        for cp in sends:
            cp.wait_send()
        for cp in own:
            cp.wait()

    return pl.pallas_call(
        body, name="exchange_grads",
        in_specs=[pl.BlockSpec(memory_space=pl.ANY), pl.BlockSpec(memory_space=pl.ANY)],
        out_specs=[pl.BlockSpec(memory_space=pl.ANY), pl.BlockSpec(memory_space=pl.ANY)],
        out_shape=[jax.ShapeDtypeStruct((N_DEV, rows, LANES), blocks.dtype),
                   jax.ShapeDtypeStruct((N_DEV,) + gains.shape, gains.dtype)],
        scratch_shapes=[pltpu.SemaphoreType.DMA((2, N_DEV - 1)), pltpu.SemaphoreType.DMA((2, N_DEV - 1)),
                        pltpu.SemaphoreType.DMA((2,))],
    )(blocks, gains)


def _sum_slots(slots, name, tile_rows):
    _, rows, cols = slots.shape

    def body(s_ref, o_ref):
        acc = s_ref[0].astype(F32)
        for j in range(1, N_DEV):
            acc = acc + s_ref[j].astype(F32)
        o_ref[...] = acc

    return pl.pallas_call(
        body, name=name, grid=(rows // tile_rows,),
        in_specs=[pl.BlockSpec((N_DEV, tile_rows, cols), lambda i: (0, i, 0))],
        out_specs=pl.BlockSpec((tile_rows, cols), lambda i: (i, 0)),
        out_shape=jax.ShapeDtypeStruct((rows, cols), F32),
        compiler_params=_params(("parallel",)),
    )(slots)


def _adamw(w, g, m, v, name):
    rows, cols = w.shape
    tile_rows = max(t for t in range(8, 257, 8) if rows % t == 0) if rows % 8 == 0 else rows
    c1 = 1.0 - ADAM_B1 ** ADAM_STEP
    c2 = 1.0 - ADAM_B2 ** ADAM_STEP

    def body(w_ref, g_ref, m_ref, v_ref, d_ref, nm_ref, nv_ref):
        g_v = g_ref[...]
        m_new = ADAM_B1 * m_ref[...] + (1.0 - ADAM_B1) * g_v
        v_new = ADAM_B2 * v_ref[...] + (1.0 - ADAM_B2) * (g_v * g_v)
        nm_ref[...] = m_new
        nv_ref[...] = v_new
        d_ref[...] = -ADAM_LR * ((m_new / c1) / (jnp.sqrt(v_new / c2) + ADAM_EPS) + ADAM_WD * w_ref[...])

    tile = pl.BlockSpec((tile_rows, cols), lambda i: (i, 0))
    out = jax.ShapeDtypeStruct((rows, cols), F32)
    return pl.pallas_call(
        body, name=name, grid=(rows // tile_rows,),
        in_specs=[tile] * 4, out_specs=[tile] * 3, out_shape=[out] * 3,
        compiler_params=_params(("parallel",)),
    )(w, g, m, v)


def _pack_shards(shards):
    return jnp.concatenate([s.reshape(-1, LANES) for s in shards], axis=0)


def _unpack_shards(packed):
    out, r0 = [], 0
    for _, (r, c) in SHARD_SHAPES:
        n = r * c // LANES
        out.append(packed[r0:r0 + n].reshape(r, c))
        r0 += n
    return out


def _unpack_full(gathered):
    out, r0 = {}, 0
    for name, (r, c) in SHARD_SHAPES:
        n = r * c // LANES
        blk = gathered[:, r0:r0 + n].reshape(N_DEV, r, c)
        if name in COL_SHARDED:
            out[name] = blk.transpose(1, 0, 2).reshape(r, N_DEV * c)
        else:
            out[name] = blk.reshape(N_DEV * r, c)
        r0 += n
    return out


def _pack_full(full):
    parts = []
    for name, (r, c) in SHARD_SHAPES:
        t = full[name]
        if name in COL_SHARDED:
            t = t.reshape(r, N_DEV, c).transpose(1, 0, 2)
        parts.append(t.reshape(N_DEV, r * c // LANES, LANES))
    return jnp.concatenate(parts, axis=1)


def _local_step(x, target, g_mix, g_ffn, g_fin, w):
    qkv, gates, u = _norm_proj(x, g_mix, w["w_in"])
    qkv3 = qkv.reshape(B_LOC, SEQ, QKV_WIDTH)
    o_sb = _sb_fwd(qkv3).reshape(TOK, SB_WIDTH)

    dil_in, dil_o, dil_lse = [], [], []
    for grp, (_, dilation) in enumerate(DIL_PAIRS):
        parts = []
        for part in range(3):
            c0 = 3 * SB_WIDTH + part * DIL_WIDTH + grp * DIL_OUT
            parts.append(_to_residues(qkv[:, c0:c0 + DIL_OUT], dilation))
        dil_in.append(parts)
        o_g, lse_g = _dil_fwd(*parts, grp)
        dil_o.append(o_g.reshape(TOK, DIL_OUT))
        dil_lse.append(lse_g.reshape(TOK, DIL_OUT))
    o_dl, lse = _dil_combine(dil_o, dil_lse)

    x1, merged = _mix_out(x, o_sb, o_dl, gates, w["w_sb_up"], w["w_dil_up"], w["w_out"])
    loss, dx1, u2, act, dh, dx2, dg_fin, dg_ffn = _ffn_fwd_bwd(x1, target, g_ffn, g_fin, w["w_ffn_in"], w["w_ffn_out"])
    dgates, dy_sb, dy_dl, do_sb, do_dl, dsum = _mix_bwd(dx1, o_sb, o_dl, gates, w["w_sb_up"], w["w_dil_up"], w["w_out"])

    dq_sb, dk_sb, dv_sb = _sb_bwd(qkv3, do_sb.reshape(B_LOC, SEQ, SB_WIDTH))
    dq_dl, dk_dl, dv_dl = [], [], []
    for grp, (_, dilation) in enumerate(DIL_PAIRS):
        dq, dk, dv = _dil_bwd(*dil_in[grp], _to_residues(do_dl, dilation), _to_residues(lse, dilation),
                              _to_residues(dsum, dilation), grp)
        dq_dl.append(dq.reshape(TOK, DIL_OUT))
        dk_dl.append(dk.reshape(TOK, DIL_OUT))
        dv_dl.append(dv.reshape(TOK, DIL_OUT))
    flat = lambda t: t.reshape(TOK, SB_WIDTH)
    dproj = jnp.concatenate([flat(dq_sb), flat(dk_sb), flat(dv_sb), *dq_dl, *dk_dl, *dv_dl, dgates], axis=1)

    grad_x, dg_mix = _proj_bwd(dproj, dx1, x, g_mix, w["w_in"])
    grads = {
        "w_in": _atb(u, dproj, "grad_w_in", D_MODEL, 256),
        "w_sb_up": _atb(o_sb, dy_sb, "grad_w_sb_up", SB_WIDTH, 512),
        "w_dil_up": _atb(o_dl.astype(BF16), dy_dl, "grad_w_dil_up", DIL_OUT, 512),
        "w_out": _atb(merged, dx1.astype(BF16), "grad_w_out", D_MODEL, 512),
        "w_ffn_in": _atb(u2, dh, "grad_w_ffn_in", D_MODEL, 512),
        "w_ffn_out": _atb(act, dx2, "grad_w_ffn_out", D_FF // 2, 512),
    }
    gain_grads = jnp.concatenate([dg_mix, dg_ffn, dg_fin], axis=0)
    return loss, grad_x, grads, gain_grads


def kernel(x, norm_mix_g, w_in, w_sb_up, w_dil_up, w_out, norm_ffn_g, w_ffn_in, w_ffn_out, norm_final_g, loss_target, m_norm_mix_g, m_w_in, m_w_sb_up, m_w_dil_up, m_w_out, m_norm_ffn_g, m_w_ffn_in, m_w_ffn_out, m_norm_final_g, v_norm_mix_g, v_w_in, v_w_sb_up, v_w_dil_up, v_w_out, v_norm_ffn_g, v_w_ffn_in, v_w_ffn_out, v_norm_final_g):
    mats = {"w_in": w_in, "w_sb_up": w_sb_up, "w_dil_up": w_dil_up, "w_out": w_out,
            "w_ffn_in": w_ffn_in, "w_ffn_out": w_ffn_out}
    moments_m = {"w_in": m_w_in, "w_sb_up": m_w_sb_up, "w_dil_up": m_w_dil_up, "w_out": m_w_out,
                 "w_ffn_in": m_w_ffn_in, "w_ffn_out": m_w_ffn_out}
    moments_v = {"w_in": v_w_in, "w_sb_up": v_w_sb_up, "w_dil_up": v_w_dil_up, "w_out": v_w_out,
                 "w_ffn_in": v_w_ffn_in, "w_ffn_out": v_w_ffn_out}
    shards = [mats[name][0] for name, _ in SHARD_SHAPES]

    gathered = _all_gather(_pack_shards([s.astype(BF16) for s in shards]))
    full = _unpack_full(gathered)

    g_fin = norm_final_g.reshape(1, D_MODEL)
    loss, grad_x, grads, gain_grads = _local_step(
        x.reshape(TOK, D_MODEL), loss_target.reshape(TOK, D_MODEL), norm_mix_g, norm_ffn_g, g_fin, full)

    gain_rows = jnp.concatenate([gain_grads, jnp.zeros((8 - 3, D_MODEL), F32)], axis=0)
    blk_slots, gain_slots = _exchange(_pack_full(grads).astype(BF16), gain_rows)
    g_shards = _unpack_shards(_sum_slots(blk_slots, "sum_grad_blocks", 384))
    g_gains = _sum_slots(gain_slots, "sum_gain_grads", 8)

    out_g, out_d, out_m, out_v = {}, {}, {}, {}
    for (name, _), g in zip(SHARD_SHAPES, g_shards):
        d, nm, nv = _adamw(mats[name][0], g, moments_m[name][0], moments_v[name][0], "adamw_" + name)
        out_g[name], out_d[name], out_m[name], out_v[name] = g[None], d[None], nm[None], nv[None]
    gain_w = jnp.concatenate([norm_mix_g, norm_ffn_g, g_fin], axis=0)
    gain_m = jnp.concatenate([m_norm_mix_g, m_norm_ffn_g, m_norm_final_g.reshape(1, D_MODEL)], axis=0)
    gain_v = jnp.concatenate([v_norm_mix_g, v_norm_ffn_g, v_norm_final_g.reshape(1, D_MODEL)], axis=0)
    gd, gm, gv = _adamw(gain_w, g_gains[:3], gain_m, gain_v, "adamw_gains")
    for idx, name in enumerate(("norm_mix_g", "norm_ffn_g", "norm_final_g")):
        shape = (D_MODEL,) if name == "norm_final_g" else (1, D_MODEL)
        out_g[name] = g_gains[idx].reshape(shape)
        out_d[name], out_m[name], out_v[name] = gd[idx].reshape(shape), gm[idx].reshape(shape), gv[idx].reshape(shape)

    order = ("norm_mix_g", "w_in", "w_sb_up", "w_dil_up", "w_out", "norm_ffn_g", "w_ffn_in", "w_ffn_out",
             "norm_final_g")
    total_loss = lax.psum(loss[0, 0], ("x", "y", "c"))
    return (total_loss, grad_x.reshape(B_LOC, SEQ, D_MODEL),
            *[out_g[n] for n in order], *[out_d[n] for n in order],
            *[out_m[n] for n in order], *[out_v[n] for n in order])
```

```python
import math

import jax
import jax.numpy as jnp
from jax import lax
from jax.experimental import pallas as pl
from jax.experimental.pallas import tpu as pltpu

F32 = jnp.float32
BF16 = jnp.bfloat16

N_DEV = 8
D_MODEL = 1024
SEQ = 2048
B_LOC = 2
TOK = B_LOC * SEQ
HEAD_DIM = 64
SB_WIDTH = 512
DIL_WIDTH = 768
DIL_OUT = 256
QKV_WIDTH = 3 * SB_WIDTH + 3 * DIL_WIDTH
IN_WIDTH = QKV_WIDTH + 2 * D_MODEL
D_FF = 2816
DIL_PAIRS = ((128, 1), (512, 4), (2048, 16))
DIL_HEADS = 12
RMS_EPS = 1e-6
ALIBI_MAX_BIAS = 8.0
QK_SCALE = 1.0 / math.sqrt(HEAD_DIM)
BLK = 128
LANES = 128
NEG_BIG = -1e30

ADAM_LR = 0.001
ADAM_B1 = 0.9
ADAM_B2 = 0.999
ADAM_EPS = 1e-08
ADAM_WD = 0.01
ADAM_STEP = 10

VMEM_LIMIT = 56 * 1024 * 1024

SHARD_SHAPES = (
    ("w_in", (D_MODEL, IN_WIDTH // N_DEV)),
    ("w_sb_up", (SB_WIDTH, D_MODEL // N_DEV)),
    ("w_dil_up", (DIL_OUT, D_MODEL // N_DEV)),
    ("w_out", (D_MODEL // N_DEV, D_MODEL)),
    ("w_ffn_in", (D_MODEL, 2 * D_FF // N_DEV)),
    ("w_ffn_out", (D_FF // N_DEV, D_MODEL)),
)
COL_SHARDED = ("w_in", "w_sb_up", "w_dil_up", "w_ffn_in")
PACK_ROWS = sum(r * c for _, (r, c) in SHARD_SHAPES) // LANES


def _dot(a, b):
    return jnp.dot(a, b, preferred_element_type=F32)


def _dot_nt(a, b):
    return lax.dot_general(a, b, (((1,), (1,)), ((), ())), preferred_element_type=F32)


def _dot_tn(a, b):
    return lax.dot_general(a, b, (((0,), (0,)), ((), ())), preferred_element_type=F32)


def _softplus(z):
    return jnp.maximum(z, 0.0) + jnp.log1p(jnp.exp(-jnp.abs(z)))


def _sigmoid(z):
    return 1.0 / (1.0 + jnp.exp(-z))


def _split_bf16(v):
    hi = v.astype(BF16)
    lo = (v - hi.astype(F32)).astype(BF16)
    return hi, lo


def _chunks(width, step=512):
    out, c = [], 0
    while c < width:
        w = min(step, width - c)
        out.append((c, w))
        c += w
    return out


def _resident(shape):
    nd = len(shape)
    return pl.BlockSpec(shape, lambda *_: (0,) * nd, pipeline_mode=pl.Buffered(1))


def _params(sem):
    return pltpu.CompilerParams(dimension_semantics=sem, vmem_limit_bytes=VMEM_LIMIT)


def _rms_fwd(x, g):
    r = lax.rsqrt(jnp.mean(x * x, axis=-1, keepdims=True) + RMS_EPS)
    n = x * r
    return n, r, n * g


def _rms_bwd(dy, n, r, g):
    dg = jnp.sum(dy * n, axis=0, keepdims=True)
    dn = dy * g
    dx = r * (dn - n * jnp.mean(dn * n, axis=-1, keepdims=True))
    return dx, dg


TM = 256


def _norm_proj(x, g, w_in):
    def body(x_ref, g_ref, w_ref, qkv_ref, gate_ref, u_ref):
        _, _, u = _rms_fwd(x_ref[...], g_ref[...])
        u = u.astype(BF16)
        u_ref[...] = u
        for c0, w in _chunks(QKV_WIDTH):
            qkv_ref[:, c0:c0 + w] = _dot(u, w_ref[:, c0:c0 + w]).astype(BF16)
        for c0, w in _chunks(2 * D_MODEL):
            gate_ref[:, c0:c0 + w] = _dot(u, w_ref[:, QKV_WIDTH + c0:QKV_WIDTH + c0 + w])

    return pl.pallas_call(
        body, name="norm_proj", grid=(TOK // TM,),
        in_specs=[pl.BlockSpec((TM, D_MODEL), lambda i: (i, 0)), _resident((1, D_MODEL)),
                  _resident((D_MODEL, IN_WIDTH))],
        out_specs=[pl.BlockSpec((TM, QKV_WIDTH), lambda i: (i, 0)),
                   pl.BlockSpec((TM, 2 * D_MODEL), lambda i: (i, 0)),
                   pl.BlockSpec((TM, D_MODEL), lambda i: (i, 0))],
        out_shape=[jax.ShapeDtypeStruct((TOK, QKV_WIDTH), BF16),
                   jax.ShapeDtypeStruct((TOK, 2 * D_MODEL), F32),
                   jax.ShapeDtypeStruct((TOK, D_MODEL), BF16)],
        compiler_params=_params(("parallel",)),
    )(x, g, w_in)


def _mix_out(x, o_sb, o_dl, gates, w_sb_up, w_dil_up, w_out):
    def body(x_ref, osb_ref, odl_ref, gate_ref, wsb_ref, wdl_ref, wout_ref, x1_ref, mg_ref):
        y_sb = _dot(osb_ref[...], wsb_ref[...])
        y_dl = _dot(odl_ref[...].astype(BF16), wdl_ref[...])
        merged = (_sigmoid(gate_ref[:, :D_MODEL]) * y_sb
                  + _sigmoid(gate_ref[:, D_MODEL:]) * y_dl).astype(BF16)
        mg_ref[...] = merged
        x1_ref[...] = x_ref[...] + _dot(merged, wout_ref[...])

    return pl.pallas_call(
        body, name="mix_out", grid=(TOK // TM,),
        in_specs=[pl.BlockSpec((TM, D_MODEL), lambda i: (i, 0)),
                  pl.BlockSpec((TM, SB_WIDTH), lambda i: (i, 0)),
                  pl.BlockSpec((TM, DIL_OUT), lambda i: (i, 0)),
                  pl.BlockSpec((TM, 2 * D_MODEL), lambda i: (i, 0)),
                  _resident((SB_WIDTH, D_MODEL)), _resident((DIL_OUT, D_MODEL)),
                  _resident((D_MODEL, D_MODEL))],
        out_specs=[pl.BlockSpec((TM, D_MODEL), lambda i: (i, 0)),
                   pl.BlockSpec((TM, D_MODEL), lambda i: (i, 0))],
        out_shape=[jax.ShapeDtypeStruct((TOK, D_MODEL), F32),
                   jax.ShapeDtypeStruct((TOK, D_MODEL), BF16)],
        compiler_params=_params(("parallel",)),
    )(x, o_sb, o_dl, gates, w_sb_up, w_dil_up, w_out)


FF_CHUNK = D_FF // 2


def _ffn_fwd_bwd(x1, target, g_ffn, g_fin, w_ffn_in, w_ffn_out):
    def body(x1_ref, t_ref, gffn_ref, gfin_ref, win_ref, wout_ref,
             loss_ref, dx1_ref, u2_ref, act_ref, dh_ref, dx2_ref, dgfin_ref, dgffn_ref, h_scr):
        i = pl.program_id(0)

        @pl.when(i == 0)
        def _():
            loss_ref[...] = jnp.zeros_like(loss_ref)
            dgfin_ref[...] = jnp.zeros_like(dgfin_ref)
            dgffn_ref[...] = jnp.zeros_like(dgffn_ref)

        x1 = x1_ref[...]
        g_ffn_v = gffn_ref[...]
        g_fin_v = gfin_ref[...]
        n2, r2, u2 = _rms_fwd(x1, g_ffn_v)
        u2 = u2.astype(BF16)
        u2_ref[...] = u2
        x2 = x1
        for c0 in range(0, D_FF, FF_CHUNK):
            gate = _dot(u2, win_ref[:, c0:c0 + FF_CHUNK])
            up = _dot(u2, win_ref[:, D_FF + c0:D_FF + c0 + FF_CHUNK])
            h_scr[:, c0:c0 + FF_CHUNK] = gate
            h_scr[:, D_FF + c0:D_FF + c0 + FF_CHUNK] = up
            act = (gate * _sigmoid(gate) * up).astype(BF16)
            act_ref[:, c0:c0 + FF_CHUNK] = act
            x2 = x2 + _dot(act, wout_ref[c0:c0 + FF_CHUNK, :])
        n3, r3, y = _rms_fwd(x2, g_fin_v)
        err = y - t_ref[...]
        sq = jnp.sum(jnp.sum(err * err, axis=1, keepdims=True), axis=0, keepdims=True)
        loss_ref[...] += sq * (0.5 / D_MODEL)
        dx2, dgfin = _rms_bwd(err * (1.0 / D_MODEL), n3, r3, g_fin_v)
        dgfin_ref[...] += dgfin
        dx2_b = dx2.astype(BF16)
        dx2_ref[...] = dx2_b
        du2 = jnp.zeros((TM, D_MODEL), F32)
        for c0 in range(0, D_FF, FF_CHUNK):
            gate = h_scr[:, c0:c0 + FF_CHUNK]
            up = h_scr[:, D_FF + c0:D_FF + c0 + FF_CHUNK]
            dact = _dot_nt(dx2_b, wout_ref[c0:c0 + FF_CHUNK, :])
            sg = _sigmoid(gate)
            dgate = (dact * up * (sg * (1.0 + gate * (1.0 - sg)))).astype(BF16)
            dup = (dact * (gate * sg)).astype(BF16)
            dh_ref[:, c0:c0 + FF_CHUNK] = dgate
            dh_ref[:, D_FF + c0:D_FF + c0 + FF_CHUNK] = dup
            du2 = du2 + _dot_nt(dgate, win_ref[:, c0:c0 + FF_CHUNK])
            du2 = du2 + _dot_nt(dup, win_ref[:, D_FF + c0:D_FF + c0 + FF_CHUNK])
        dx1_n, dgffn = _rms_bwd(du2, n2, r2, g_ffn_v)
        dgffn_ref[...] += dgffn
        dx1_ref[...] = dx2 + dx1_n

    tile = lambda w: pl.BlockSpec((TM, w), lambda i: (i, 0))
    acc = lambda w: pl.BlockSpec((1, w), lambda i: (0, 0))
    return pl.pallas_call(
        body, name="ffn_fwd_bwd", grid=(TOK // TM,),
        in_specs=[tile(D_MODEL), tile(D_MODEL), _resident((1, D_MODEL)), _resident((1, D_MODEL)),
                  _resident((D_MODEL, 2 * D_FF)), _resident((D_FF, D_MODEL))],
        out_specs=[acc(LANES), tile(D_MODEL), tile(D_MODEL), tile(D_FF), tile(2 * D_FF), tile(D_MODEL),
                   acc(D_MODEL), acc(D_MODEL)],
        out_shape=[jax.ShapeDtypeStruct((1, LANES), F32),
                   jax.ShapeDtypeStruct((TOK, D_MODEL), F32),
                   jax.ShapeDtypeStruct((TOK, D_MODEL), BF16),
                   jax.ShapeDtypeStruct((TOK, D_FF), BF16),
                   jax.ShapeDtypeStruct((TOK, 2 * D_FF), BF16),
                   jax.ShapeDtypeStruct((TOK, D_MODEL), BF16),
                   jax.ShapeDtypeStruct((1, D_MODEL), F32),
                   jax.ShapeDtypeStruct((1, D_MODEL), F32)],
        scratch_shapes=[pltpu.VMEM((TM, 2 * D_FF), F32)],
        compiler_params=_params(("arbitrary",)),
    )(x1, target, g_ffn, g_fin, w_ffn_in, w_ffn_out)


def _mix_bwd(dx1, o_sb, o_dl, gates, w_sb_up, w_dil_up, w_out):
    def body(dx1_ref, osb_ref, odl_ref, gate_ref, wsb_ref, wdl_ref, wout_ref,
             dgate_ref, dysb_ref, dydl_ref, dosb_ref, dodl_ref, dsum_ref):
        dmerged = _dot_nt(dx1_ref[...].astype(BF16), wout_ref[...])
        o_dl = odl_ref[...]
        y_sb = _dot(osb_ref[...], wsb_ref[...])
        y_dl = _dot(o_dl.astype(BF16), wdl_ref[...])
        s_sb = _sigmoid(gate_ref[:, :D_MODEL])
        s_dl = _sigmoid(gate_ref[:, D_MODEL:])
        dgate_ref[:, :D_MODEL] = (dmerged * y_sb * (s_sb * (1.0 - s_sb))).astype(BF16)
        dgate_ref[:, D_MODEL:] = (dmerged * y_dl * (s_dl * (1.0 - s_dl))).astype(BF16)
        dy_sb = (dmerged * s_sb).astype(BF16)
        dy_dl = (dmerged * s_dl).astype(BF16)
        dysb_ref[...] = dy_sb
        dydl_ref[...] = dy_dl
        dosb_ref[...] = _dot_nt(dy_sb, wsb_ref[...]).astype(BF16)
        do_dl = _dot_nt(dy_dl, wdl_ref[...])
        dodl_ref[...] = do_dl
        row = lax.broadcasted_iota(jnp.int32, (DIL_OUT, DIL_OUT), 0) // HEAD_DIM
        col = lax.broadcasted_iota(jnp.int32, (DIL_OUT, DIL_OUT), 1) // HEAD_DIM
        same_head = (row == col).astype(BF16)
        hi, lo = _split_bf16(do_dl * o_dl)
        dsum_ref[...] = _dot(hi, same_head) + _dot(lo, same_head)

    tile = lambda w: pl.BlockSpec((TM, w), lambda i: (i, 0))
    return pl.pallas_call(
        body, name="mix_bwd", grid=(TOK // TM,),
        in_specs=[tile(D_MODEL), tile(SB_WIDTH), tile(DIL_OUT), tile(2 * D_MODEL),
                  _resident((SB_WIDTH, D_MODEL)), _resident((DIL_OUT, D_MODEL)),
                  _resident((D_MODEL, D_MODEL))],
        out_specs=[tile(2 * D_MODEL), tile(D_MODEL), tile(D_MODEL), tile(SB_WIDTH), tile(DIL_OUT),
                   tile(DIL_OUT)],
        out_shape=[jax.ShapeDtypeStruct((TOK, 2 * D_MODEL), BF16),
                   jax.ShapeDtypeStruct((TOK, D_MODEL), BF16),
                   jax.ShapeDtypeStruct((TOK, D_MODEL), BF16),
                   jax.ShapeDtypeStruct((TOK, SB_WIDTH), BF16),
                   jax.ShapeDtypeStruct((TOK, DIL_OUT), F32),
                   jax.ShapeDtypeStruct((TOK, DIL_OUT), F32)],
        compiler_params=_params(("parallel",)),
    )(dx1, o_sb, o_dl, gates, w_sb_up, w_dil_up, w_out)


def _proj_bwd(dproj, dx1, x, g, w_in):
    def body(dp_ref, dx1_ref, x_ref, g_ref, w_ref, dx_ref, dg_ref):
        @pl.when(pl.program_id(0) == 0)
        def _():
            dg_ref[...] = jnp.zeros_like(dg_ref)

        du = jnp.zeros((TM, D_MODEL), F32)
        for c0, w in _chunks(IN_WIDTH, 1024):
            du = du + _dot_nt(dp_ref[:, c0:c0 + w], w_ref[:, c0:c0 + w])
        g_v = g_ref[...]
        n, r, _ = _rms_fwd(x_ref[...], g_v)
        dx, dg = _rms_bwd(du, n, r, g_v)
        dg_ref[...] += dg
        dx_ref[...] = dx1_ref[...] + dx

    tile = lambda w: pl.BlockSpec((TM, w), lambda i: (i, 0))
    return pl.pallas_call(
        body, name="proj_bwd", grid=(TOK // TM,),
        in_specs=[tile(IN_WIDTH), tile(D_MODEL), tile(D_MODEL), _resident((1, D_MODEL)),
                  _resident((D_MODEL, IN_WIDTH))],
        out_specs=[tile(D_MODEL), pl.BlockSpec((1, D_MODEL), lambda i: (0, 0))],
        out_shape=[jax.ShapeDtypeStruct((TOK, D_MODEL), F32),
                   jax.ShapeDtypeStruct((1, D_MODEL), F32)],
        compiler_params=_params(("arbitrary",)),
    )(dproj, dx1, x, g, w_in)


def _atb(a, b, name, tm, tn, tk=512):
    m, n = a.shape[1], b.shape[1]
    nk = TOK // tk

    def body(a_ref, b_ref, o_ref):
        @pl.when(pl.program_id(2) == 0)
        def _():
            o_ref[...] = jnp.zeros_like(o_ref)

        o_ref[...] += _dot_tn(a_ref[...], b_ref[...])

    return pl.pallas_call(
        body, name=name, grid=(m // tm, n // tn, nk),
        in_specs=[pl.BlockSpec((tk, tm), lambda i, j, k: (k, i)),
                  pl.BlockSpec((tk, tn), lambda i, j, k: (k, j))],
        out_specs=pl.BlockSpec((tm, tn), lambda i, j, k: (i, j)),
        out_shape=jax.ShapeDtypeStruct((m, n), F32),
        compiler_params=_params(("parallel", "parallel", "arbitrary")),
    )(a, b)


SB_PAIRS = SB_WIDTH // LANES


def _sb_masks():
    row = lax.broadcasted_iota(jnp.int32, (BLK, BLK), 0)
    col = lax.broadcasted_iota(jnp.int32, (BLK, BLK), 1)
    return row, col, col < HEAD_DIM


def _two_heads(v, lane0):
    zero = jnp.zeros_like(v)
    return jnp.where(lane0, v, zero), jnp.where(lane0, zero, v)


SB_QBLK = 256
N_SB_STEPS = SEQ // SB_QBLK


def _sb_tile_masks():
    row = lax.broadcasted_iota(jnp.int32, (SB_QBLK, BLK), 0)
    col = lax.broadcasted_iota(jnp.int32, (SB_QBLK, BLK), 1)
    return row, col, col < HEAD_DIM


def _sb_fwd(qkv):
    def body(q_ref, k_ref, v_ref, o_ref, tot_ref):
        i = pl.program_id(2)
        row, col, lane0 = _sb_tile_masks()
        krow = lax.broadcasted_iota(jnp.int32, (BLK, BLK), 0)
        kcol = lax.broadcasted_iota(jnp.int32, (BLK, BLK), 1)
        later = (krow > kcol).astype(BF16)
        q_h = _two_heads(q_ref[0], lane0)
        n_kblk = (i + 1) * (SB_QBLK // BLK)

        def step(jj, carry):
            j = n_kblk - 1 - jj
            off = pl.multiple_of(j * BLK, BLK)
            k_j = k_ref[0, pl.ds(off, BLK), :]
            v_j = v_ref[0, pl.ds(off, BLK), :]
            causal = col + j * BLK < row + i * SB_QBLK
            out = []
            for h in range(2):
                acc, run = carry[h]
                z = _dot_nt(q_h[h], k_j) * QK_SCALE
                log_keep = jnp.where(causal, -_softplus(z), 0.0)
                hi, lo = _split_bf16(log_keep)
                log_after = _dot(hi, later) + _dot(lo, later) + run
                a = jnp.where(causal, jnp.exp(log_keep + z + log_after), 0.0)
                acc = acc + _dot(a.astype(BF16), v_j)
                run = run + jnp.sum(log_keep, axis=1, keepdims=True)
                out.append((acc, run))
            return tuple(out)

        init = tuple((jnp.zeros((SB_QBLK, LANES), F32), jnp.zeros((SB_QBLK, 1), F32)) for _ in range(2))
        res = lax.fori_loop(0, n_kblk, step, init)
        o_ref[0] = jnp.where(lane0, res[0][0], res[1][0]).astype(BF16)
        tot_ref[0] = jnp.where(lane0, res[0][1], res[1][1])

    blk = pl.BlockSpec((1, SB_QBLK, LANES), lambda b, h, i: (b, i, h))
    return pl.pallas_call(
        body, name="sb_fwd", grid=(B_LOC, SB_PAIRS, N_SB_STEPS),
        in_specs=[blk,
                  pl.BlockSpec((1, SEQ, LANES), lambda b, h, i: (b, 0, SB_PAIRS + h)),
                  pl.BlockSpec((1, SEQ, LANES), lambda b, h, i: (b, 0, 2 * SB_PAIRS + h))],
        out_specs=[blk, blk],
        out_shape=[jax.ShapeDtypeStruct((B_LOC, SEQ, SB_WIDTH), BF16),
                   jax.ShapeDtypeStruct((B_LOC, SEQ, SB_WIDTH), F32)],
        compiler_params=_params(("parallel", "parallel", "arbitrary")),
    )(qkv, qkv, qkv)


def _sb_bwd(qkv, d_o, tot):
    def body(q_ref, k_ref, v_ref, do_ref, tot_ref, dq_ref, dk_ref, dv_ref, dk_acc, dv_acc):
        i = pl.program_id(2)
        row, col, lane0 = _sb_tile_masks()
        lane1 = jnp.logical_not(lane0)
        krow = lax.broadcasted_iota(jnp.int32, (BLK, BLK), 0)
        kcol = lax.broadcasted_iota(jnp.int32, (BLK, BLK), 1)
        upto = (krow <= kcol).astype(BF16)
        earlier = (krow < kcol).astype(BF16)
        q_h = _two_heads(q_ref[0], lane0)
        do_h = _two_heads(do_ref[0], lane0)
        tot = tot_ref[0]
        tot_h = (_head_col(tot, lane0), _head_col(tot, lane1))

        @pl.when(i == 0)
        def _():
            dk_acc[...] = jnp.zeros_like(dk_acc)
            dv_acc[...] = jnp.zeros_like(dv_acc)

        def up(j, carry):
            dq, pre = carry
            off = pl.multiple_of(j * BLK, BLK)
            k_j = k_ref[0, pl.ds(off, BLK), :]
            v_j = v_ref[0, pl.ds(off, BLK), :]
            k_h = _two_heads(k_j, kcol < HEAD_DIM)
            causal = col + j * BLK < row + i * SB_QBLK
            new_pre = []
            dk_j = jnp.zeros((BLK, LANES), F32)
            dv_j = jnp.zeros((BLK, LANES), F32)
            for h in range(2):
                pre_keep, pre_e = pre[h]
                z = _dot_nt(q_h[h], k_j) * QK_SCALE
                log_keep = jnp.where(causal, -_softplus(z), 0.0)
                log_beta = log_keep + z
                hi, lo = _split_bf16(log_keep)
                log_after = (tot_h[h] - pre_keep) - (_dot(hi, upto) + _dot(lo, upto))
                a = jnp.where(causal, jnp.exp(log_beta + log_after), 0.0)
                e = _dot_nt(do_h[h], v_j) * a
                hi, lo = _split_bf16(e)
                before = _dot(hi, earlier) + _dot(lo, earlier) + pre_e
                beta = jnp.exp(log_beta)
                dz = jnp.where(causal, e * (1.0 - beta) - before * beta, 0.0) * QK_SCALE
                dz = dz.astype(BF16)
                dq = dq + _dot(dz, k_h[h])
                dk_j = dk_j + _dot_tn(dz, q_h[h])
                dv_j = dv_j + _dot_tn(a.astype(BF16), do_h[h])
                new_pre.append((pre_keep + jnp.sum(log_keep, axis=1, keepdims=True),
                                pre_e + jnp.sum(e, axis=1, keepdims=True)))
            dk_acc[pl.ds(off, BLK), :] += dk_j
            dv_acc[pl.ds(off, BLK), :] += dv_j
            return dq, tuple(new_pre)

        zero_col = jnp.zeros((SB_QBLK, 1), F32)
        dq, _ = lax.fori_loop(0, (i + 1) * (SB_QBLK // BLK), up,
                              (jnp.zeros((SB_QBLK, LANES), F32), ((zero_col, zero_col), (zero_col, zero_col))))
        dq_ref[0] = dq.astype(BF16)

        @pl.when(i == N_SB_STEPS - 1)
        def _():
            dk_ref[0] = dk_acc[...].astype(BF16)
            dv_ref[0] = dv_acc[...].astype(BF16)

    blk = pl.BlockSpec((1, SB_QBLK, LANES), lambda b, h, i: (b, i, h))
    whole = lambda c: pl.BlockSpec((1, SEQ, LANES), lambda b, h, i: (b, 0, c * SB_PAIRS + h))
    out = jax.ShapeDtypeStruct((B_LOC, SEQ, SB_WIDTH), BF16)
    return pl.pallas_call(
        body, name="sb_bwd", grid=(B_LOC, SB_PAIRS, N_SB_STEPS),
        in_specs=[blk, whole(1), whole(2), blk, blk],
        out_specs=[blk, whole(0), whole(0)],
        out_shape=[out, out, out],
        scratch_shapes=[pltpu.VMEM((SEQ, LANES), F32), pltpu.VMEM((SEQ, LANES), F32)],
        compiler_params=_params(("parallel", "parallel", "arbitrary")),
    )(qkv, qkv, qkv, d_o, tot)


def _dil_scores(q_h, k_cur, k_prev, slope, dilation, has_prev, row, col):
    dist = (row - col).astype(F32) * float(dilation)
    s_cur = _dot_nt(q_h, k_cur) * QK_SCALE - slope * dist
    s_prev = _dot_nt(q_h, k_prev) * QK_SCALE - slope * (dist + float(BLK * dilation))
    s_cur = jnp.where(col <= row, s_cur, NEG_BIG)
    s_prev = jnp.where(jnp.logical_and(col >= row, has_prev), s_prev, NEG_BIG)
    return s_cur, s_prev


def _dil_slopes(group):
    pair = pl.program_id(1) % 2
    first = float(4 * group + 1) + 2.0 * pair.astype(F32)
    coef = -ALIBI_MAX_BIAS / DIL_HEADS * math.log(2.0)
    return jnp.exp(coef * first), jnp.exp(coef * (first + 1.0))


def _head_col(v, lane_mask):
    return jnp.max(jnp.where(lane_mask, v, NEG_BIG), axis=1, keepdims=True)


def _dil_fwd(q, k, v, group):
    dilation = DIL_PAIRS[group][1]
    length = SEQ // dilation
    nblk = length // BLK

    def body(q_ref, k_ref, v_ref, o_ref, lse_ref):
        row, col, lane0 = _sb_masks()
        slopes = _dil_slopes(group)

        def step(n, _):
            off = pl.multiple_of(n * BLK, BLK)
            off_prev = pl.multiple_of(jnp.maximum(n - 1, 0) * BLK, BLK)
            q_h = _two_heads(q_ref[0, pl.ds(off, BLK), :], lane0)
            k_cur = k_ref[0, pl.ds(off, BLK), :]
            v_cur = v_ref[0, pl.ds(off, BLK), :]
            k_prev = k_ref[0, pl.ds(off_prev, BLK), :]
            v_prev = v_ref[0, pl.ds(off_prev, BLK), :]
            outs, lses = [], []
            for h in range(2):
                s_cur, s_prev = _dil_scores(q_h[h], k_cur, k_prev, slopes[h], dilation, n > 0, row, col)
                m = jnp.maximum(jnp.max(s_cur, axis=1, keepdims=True), jnp.max(s_prev, axis=1, keepdims=True))
                p_cur = jnp.exp(s_cur - m)
                p_prev = jnp.exp(s_prev - m)
                den = jnp.sum(p_cur, axis=1, keepdims=True) + jnp.sum(p_prev, axis=1, keepdims=True)
                num = _dot(p_cur.astype(BF16), v_cur) + _dot(p_prev.astype(BF16), v_prev)
                outs.append(num / den)
                lses.append(m + jnp.log(den))
            o_ref[0, pl.ds(off, BLK), :] = jnp.where(lane0, outs[0], outs[1])
            lse_ref[0, pl.ds(off, BLK), :] = jnp.where(lane0, lses[0], lses[1])
            return 0

        lax.fori_loop(0, nblk, step, 0)

    spec = pl.BlockSpec((1, length, LANES), lambda b, c: (b, 0, c))
    out = jax.ShapeDtypeStruct((B_LOC, length, dilation * DIL_OUT), F32)
    return pl.pallas_call(
        body, name=f"dil_fwd_{group}", grid=(B_LOC, 2 * dilation),
        in_specs=[spec, spec, spec], out_specs=[spec, spec], out_shape=[out, out],
        compiler_params=_params(("parallel", "parallel")),
    )(q, k, v)


def _dil_bwd(q, k, v, d_o, lse, dsum, group):
    dilation = DIL_PAIRS[group][1]
    length = SEQ // dilation
    nblk = length // BLK

    def body(q_ref, k_ref, v_ref, do_ref, lse_ref, dsum_ref, dq_ref, dk_ref, dv_ref, dk_acc, dv_acc):
        row, col, lane0 = _sb_masks()
        lane1 = jnp.logical_not(lane0)
        slopes = _dil_slopes(group)
        dk_acc[...] = jnp.zeros_like(dk_acc)
        dv_acc[...] = jnp.zeros_like(dv_acc)

        def step(n, _):
            off = pl.multiple_of(n * BLK, BLK)
            off_prev = pl.multiple_of(jnp.maximum(n - 1, 0) * BLK, BLK)
            q_h = _two_heads(q_ref[0, pl.ds(off, BLK), :], lane0)
            do_h = _two_heads(do_ref[0, pl.ds(off, BLK), :].astype(BF16), lane0)
            k_cur = k_ref[0, pl.ds(off, BLK), :]
            v_cur = v_ref[0, pl.ds(off, BLK), :]
            k_prev = k_ref[0, pl.ds(off_prev, BLK), :]
            v_prev = v_ref[0, pl.ds(off_prev, BLK), :]
            kc_h = _two_heads(k_cur, lane0)
            kp_h = _two_heads(k_prev, lane0)
            lse_blk = lse_ref[0, pl.ds(off, BLK), :]
            dsum_blk = dsum_ref[0, pl.ds(off, BLK), :]
            dq = jnp.zeros((BLK, LANES), F32)
            dk_c = jnp.zeros((BLK, LANES), F32)
            dk_p = jnp.zeros((BLK, LANES), F32)
            dv_c = jnp.zeros((BLK, LANES), F32)
            dv_p = jnp.zeros((BLK, LANES), F32)
            for h, lanes in enumerate((lane0, lane1)):
                s_cur, s_prev = _dil_scores(q_h[h], k_cur, k_prev, slopes[h], dilation, n > 0, row, col)
                lse_h = _head_col(lse_blk, lanes)
                dsum_h = _head_col(dsum_blk, lanes)
                p_cur = jnp.exp(s_cur - lse_h)
                p_prev = jnp.exp(s_prev - lse_h)
                ds_cur = (p_cur * (_dot_nt(do_h[h], v_cur) - dsum_h) * QK_SCALE).astype(BF16)
                ds_prev = (p_prev * (_dot_nt(do_h[h], v_prev) - dsum_h) * QK_SCALE).astype(BF16)
                dq = dq + _dot(ds_cur, kc_h[h]) + _dot(ds_prev, kp_h[h])
                dk_c = dk_c + _dot_tn(ds_cur, q_h[h])
                dk_p = dk_p + _dot_tn(ds_prev, q_h[h])
                dv_c = dv_c + _dot_tn(p_cur.astype(BF16), do_h[h])
                dv_p = dv_p + _dot_tn(p_prev.astype(BF16), do_h[h])
            dq_ref[0, pl.ds(off, BLK), :] = dq.astype(BF16)
            dk_acc[pl.ds(off, BLK), :] += dk_c
            dv_acc[pl.ds(off, BLK), :] += dv_c
            dk_acc[pl.ds(off_prev, BLK), :] += dk_p
            dv_acc[pl.ds(off_prev, BLK), :] += dv_p
            return 0

        lax.fori_loop(0, nblk, step, 0)
        dk_ref[0] = dk_acc[...].astype(BF16)
        dv_ref[0] = dv_acc[...].astype(BF16)

    spec = pl.BlockSpec((1, length, LANES), lambda b, c: (b, 0, c))
    out = jax.ShapeDtypeStruct((B_LOC, length, dilation * DIL_OUT), BF16)
    return pl.pallas_call(
        body, name=f"dil_bwd_{group}", grid=(B_LOC, 2 * dilation),
        in_specs=[spec] * 6, out_specs=[spec] * 3, out_shape=[out] * 3,
        scratch_shapes=[pltpu.VMEM((length, LANES), F32), pltpu.VMEM((length, LANES), F32)],
        compiler_params=_params(("parallel", "parallel")),
    )(q, k, v, d_o, lse, dsum)


def _dil_combine(outs, lses):
    def body(o0, o1, o2, l0, l1, l2, o_ref, lse_ref):
        ls = (l0[...], l1[...], l2[...])
        m = jnp.maximum(jnp.maximum(ls[0], ls[1]), ls[2])
        w = [jnp.exp(l - m) for l in ls]
        den = w[0] + w[1] + w[2]
        o_ref[...] = (w[0] * o0[...] + w[1] * o1[...] + w[2] * o2[...]) / den
        lse_ref[...] = m + jnp.log(den)

    tile = pl.BlockSpec((512, DIL_OUT), lambda i: (i, 0))
    out = jax.ShapeDtypeStruct((TOK, DIL_OUT), F32)
    return pl.pallas_call(
        body, name="dil_combine", grid=(TOK // 512,),
        in_specs=[tile] * 6, out_specs=[tile, tile], out_shape=[out, out],
        compiler_params=_params(("parallel",)),
    )(*outs, *lses)


def _to_residues(t, dilation):
    return t.reshape(B_LOC, SEQ // dilation, dilation * DIL_OUT)


def _peers():
    x, y, c = lax.axis_index("x"), lax.axis_index("y"), lax.axis_index("c")
    me = 4 * x + 2 * y + c
    peers = []
    for mask in range(1, N_DEV):
        px = 1 - x if mask & 4 else x
        py = 1 - y if mask & 2 else y
        pc = 1 - c if mask & 1 else c
        peers.append(((px, py, pc), 4 * px + 2 * py + pc))
    return me, peers


def _all_gather(packed):
    rows = packed.shape[0]

    def body(src_ref, out_ref, send_sems, recv_sems, local_sem):
        me, peers = _peers()
        mine = pltpu.make_async_copy(src_ref, out_ref.at[me], local_sem)
        mine.start()
        sends = []
        for k, (peer, _) in enumerate(peers):
            cp = pltpu.make_async_remote_copy(
                src_ref=src_ref, dst_ref=out_ref.at[me], send_sem=send_sems.at[k], recv_sem=recv_sems.at[k],
                device_id=peer, device_id_type=pl.DeviceIdType.MESH)
            cp.start()
            sends.append(cp)
        for k, (peer, peer_idx) in enumerate(peers):
            pltpu.make_async_remote_copy(
                src_ref=src_ref, dst_ref=out_ref.at[peer_idx], send_sem=send_sems.at[k], recv_sem=recv_sems.at[k],
                device_id=peer, device_id_type=pl.DeviceIdType.MESH).wait_recv()
        for cp in sends:
            cp.wait_send()
        mine.wait()

    return pl.pallas_call(
        body, name="all_gather_weights",
        in_specs=[pl.BlockSpec(memory_space=pl.ANY)],
        out_specs=pl.BlockSpec(memory_space=pl.ANY),
        out_shape=jax.ShapeDtypeStruct((N_DEV, rows, LANES), packed.dtype),
        scratch_shapes=[pltpu.SemaphoreType.DMA((N_DEV - 1,)), pltpu.SemaphoreType.DMA((N_DEV - 1,)),
                        pltpu.SemaphoreType.DMA],
    )(packed)


def _exchange(blocks, gains):
    rows = blocks.shape[1]

    def body(blk_ref, gain_ref, blk_out, gain_out, send_sems, recv_sems, local_sems):
        me, peers = _peers()
        own = [pltpu.make_async_copy(blk_ref.at[me], blk_out.at[me], local_sems.at[0]),
               pltpu.make_async_copy(gain_ref, gain_out.at[me], local_sems.at[1])]
        for cp in own:
            cp.start()
        sends = []
        for k, (peer, peer_idx) in enumerate(peers):
            for a, (src, dst) in enumerate(((blk_ref.at[peer_idx], blk_out.at[me]), (gain_ref, gain_out.at[me]))):
                cp = pltpu.make_async_remote_copy(
                    src_ref=src, dst_ref=dst, send_sem=send_sems.at[a, k], recv_sem=recv_sems.at[a, k],
                    device_id=peer, device_id_type=pl.DeviceIdType.MESH)
                cp.start()
                sends.append(cp)
        for k, (peer, peer_idx) in enumerate(peers):
            for a, (src, dst) in enumerate(((blk_ref.at[peer_idx], blk_out.at[peer_idx]),
                                            (gain_ref, gain_out.at[peer_idx]))):
                pltpu.make_async_remote_copy(
                    src_ref=src, dst_ref=dst, send_sem=send_sems.at[a, k], recv_sem=recv_sems.at[a, k],
                    device_id=peer, device_id_type=pl.DeviceIdType.MESH).wait_recv()
        for cp in sends:
            cp.wait_send()
        for cp in own:
            cp.wait()

    return pl.pallas_call(
        body, name="exchange_grads",
        in_specs=[pl.BlockSpec(memory_space=pl.ANY), pl.BlockSpec(memory_space=pl.ANY)],
        out_specs=[pl.BlockSpec(memory_space=pl.ANY), pl.BlockSpec(memory_space=pl.ANY)],
        out_shape=[jax.ShapeDtypeStruct((N_DEV, rows, LANES), blocks.dtype),
                   jax.ShapeDtypeStruct((N_DEV,) + gains.shape, gains.dtype)],
        scratch_shapes=[pltpu.SemaphoreType.DMA((2, N_DEV - 1)), pltpu.SemaphoreType.DMA((2, N_DEV - 1)),
                        pltpu.SemaphoreType.DMA((2,))],
    )(blocks, gains)


def _sum_slots(slots, name, tile_rows):
    _, rows, cols = slots.shape

    def body(s_ref, o_ref):
        acc = s_ref[0].astype(F32)
        for j in range(1, N_DEV):
            acc = acc + s_ref[j].astype(F32)
        o_ref[...] = acc

    return pl.pallas_call(
        body, name=name, grid=(rows // tile_rows,),
        in_specs=[pl.BlockSpec((N_DEV, tile_rows, cols), lambda i: (0, i, 0))],
        out_specs=pl.BlockSpec((tile_rows, cols), lambda i: (i, 0)),
        out_shape=jax.ShapeDtypeStruct((rows, cols), F32),
        compiler_params=_params(("parallel",)),
    )(slots)


def _adamw(w, g, m, v, name):
    rows, cols = w.shape
    tile_rows = max(t for t in range(8, 257, 8) if rows % t == 0) if rows % 8 == 0 else rows
    c1 = 1.0 - ADAM_B1 ** ADAM_STEP
    c2 = 1.0 - ADAM_B2 ** ADAM_STEP

    def body(w_ref, g_ref, m_ref, v_ref, d_ref, nm_ref, nv_ref):
        g_v = g_ref[...]
        m_new = ADAM_B1 * m_ref[...] + (1.0 - ADAM_B1) * g_v
        v_new = ADAM_B2 * v_ref[...] + (1.0 - ADAM_B2) * (g_v * g_v)
        nm_ref[...] = m_new
        nv_ref[...] = v_new
        d_ref[...] = -ADAM_LR * ((m_new / c1) / (jnp.sqrt(v_new / c2) + ADAM_EPS) + ADAM_WD * w_ref[...])

    tile = pl.BlockSpec((tile_rows, cols), lambda i: (i, 0))
    out = jax.ShapeDtypeStruct((rows, cols), F32)
    return pl.pallas_call(
        body, name=name, grid=(rows // tile_rows,),
        in_specs=[tile] * 4, out_specs=[tile] * 3, out_shape=[out] * 3,
        compiler_params=_params(("parallel",)),
    )(w, g, m, v)


def _pack_shards(shards):
    return jnp.concatenate([s.reshape(-1, LANES) for s in shards], axis=0)


def _unpack_shards(packed):
    out, r0 = [], 0
    for _, (r, c) in SHARD_SHAPES:
        n = r * c // LANES
        out.append(packed[r0:r0 + n].reshape(r, c))
        r0 += n
    return out


def _unpack_full(gathered):
    out, r0 = {}, 0
    for name, (r, c) in SHARD_SHAPES:
        n = r * c // LANES
        blk = gathered[:, r0:r0 + n].reshape(N_DEV, r, c)
        if name in COL_SHARDED:
            out[name] = blk.transpose(1, 0, 2).reshape(r, N_DEV * c)
        else:
            out[name] = blk.reshape(N_DEV * r, c)
        r0 += n
    return out


def _pack_full(full):
    parts = []
    for name, (r, c) in SHARD_SHAPES:
        t = full[name]
        if name in COL_SHARDED:
            t = t.reshape(r, N_DEV, c).transpose(1, 0, 2)
        parts.append(t.reshape(N_DEV, r * c // LANES, LANES))
    return jnp.concatenate(parts, axis=1)


def _local_step(x, target, g_mix, g_ffn, g_fin, w):
    qkv, gates, u = _norm_proj(x, g_mix, w["w_in"])
    qkv3 = qkv.reshape(B_LOC, SEQ, QKV_WIDTH)
    o_sb, sb_tot = _sb_fwd(qkv3)
    o_sb = o_sb.reshape(TOK, SB_WIDTH)

    dil_in, dil_o, dil_lse = [], [], []
    for grp, (_, dilation) in enumerate(DIL_PAIRS):
        parts = []
        for part in range(3):
            c0 = 3 * SB_WIDTH + part * DIL_WIDTH + grp * DIL_OUT
            parts.append(_to_residues(qkv[:, c0:c0 + DIL_OUT], dilation))
        dil_in.append(parts)
        o_g, lse_g = _dil_fwd(*parts, grp)
        dil_o.append(o_g.reshape(TOK, DIL_OUT))
        dil_lse.append(lse_g.reshape(TOK, DIL_OUT))
    o_dl, lse = _dil_combine(dil_o, dil_lse)

    x1, merged = _mix_out(x, o_sb, o_dl, gates, w["w_sb_up"], w["w_dil_up"], w["w_out"])
    loss, dx1, u2, act, dh, dx2, dg_fin, dg_ffn = _ffn_fwd_bwd(x1, target, g_ffn, g_fin, w["w_ffn_in"], w["w_ffn_out"])
    dgates, dy_sb, dy_dl, do_sb, do_dl, dsum = _mix_bwd(dx1, o_sb, o_dl, gates, w["w_sb_up"], w["w_dil_up"], w["w_out"])

    dq_sb, dk_sb, dv_sb = _sb_bwd(qkv3, do_sb.reshape(B_LOC, SEQ, SB_WIDTH), sb_tot)
    dq_dl, dk_dl, dv_dl = [], [], []
    for grp, (_, dilation) in enumerate(DIL_PAIRS):
        dq, dk, dv = _dil_bwd(*dil_in[grp], _to_residues(do_dl, dilation), _to_residues(lse, dilation),
                              _to_residues(dsum, dilation), grp)
        dq_dl.append(dq.reshape(TOK, DIL_OUT))
        dk_dl.append(dk.reshape(TOK, DIL_OUT))
        dv_dl.append(dv.reshape(TOK, DIL_OUT))
    flat = lambda t: t.reshape(TOK, SB_WIDTH)
    dproj = jnp.concatenate([flat(dq_sb), flat(dk_sb), flat(dv_sb), *dq_dl, *dk_dl, *dv_dl, dgates], axis=1)

    grad_x, dg_mix = _proj_bwd(dproj, dx1, x, g_mix, w["w_in"])
    grads = {
        "w_in": _atb(u, dproj, "grad_w_in", D_MODEL, IN_WIDTH // 2),
        "w_sb_up": _atb(o_sb, dy_sb, "grad_w_sb_up", SB_WIDTH, D_MODEL),
        "w_dil_up": _atb(o_dl.astype(BF16), dy_dl, "grad_w_dil_up", DIL_OUT, D_MODEL),
        "w_out": _atb(merged, dx1.astype(BF16), "grad_w_out", D_MODEL, D_MODEL),
        "w_ffn_in": _atb(u2, dh, "grad_w_ffn_in", D_MODEL, D_FF),
        "w_ffn_out": _atb(act, dx2, "grad_w_ffn_out", D_FF // 2, D_MODEL),
    }
    gain_grads = jnp.concatenate([dg_mix, dg_ffn, dg_fin], axis=0)
    return loss, grad_x, grads, gain_grads


def kernel(x, norm_mix_g, w_in, w_sb_up, w_dil_up, w_out, norm_ffn_g, w_ffn_in, w_ffn_out, norm_final_g, loss_target, m_norm_mix_g, m_w_in, m_w_sb_up, m_w_dil_up, m_w_out, m_norm_ffn_g, m_w_ffn_in, m_w_ffn_out, m_norm_final_g, v_norm_mix_g, v_w_in, v_w_sb_up, v_w_dil_up, v_w_out, v_norm_ffn_g, v_w_ffn_in, v_w_ffn_out, v_norm_final_g):
    mats = {"w_in": w_in, "w_sb_up": w_sb_up, "w_dil_up": w_dil_up, "w_out": w_out,
            "w_ffn_in": w_ffn_in, "w_ffn_out": w_ffn_out}
    moments_m = {"w_in": m_w_in, "w_sb_up": m_w_sb_up, "w_dil_up": m_w_dil_up, "w_out": m_w_out,
                 "w_ffn_in": m_w_ffn_in, "w_ffn_out": m_w_ffn_out}
    moments_v = {"w_in": v_w_in, "w_sb_up": v_w_sb_up, "w_dil_up": v_w_dil_up, "w_out": v_w_out,
                 "w_ffn_in": v_w_ffn_in, "w_ffn_out": v_w_ffn_out}
    shards = [mats[name][0] for name, _ in SHARD_SHAPES]

    gathered = _all_gather(_pack_shards([s.astype(BF16) for s in shards]))
    full = _unpack_full(gathered)

    g_fin = norm_final_g.reshape(1, D_MODEL)
    loss, grad_x, grads, gain_grads = _local_step(
        x.reshape(TOK, D_MODEL), loss_target.reshape(TOK, D_MODEL), norm_mix_g, norm_ffn_g, g_fin, full)

    gain_rows = jnp.concatenate([gain_grads, jnp.zeros((8 - 3, D_MODEL), F32)], axis=0)
    blk_slots, gain_slots = _exchange(_pack_full(grads).astype(BF16), gain_rows)
    g_shards = _unpack_shards(_sum_slots(blk_slots, "sum_grad_blocks", 384))
    g_gains = _sum_slots(gain_slots, "sum_gain_grads", 8)

    out_g, out_d, out_m, out_v = {}, {}, {}, {}
    for (name, _), g in zip(SHARD_SHAPES, g_shards):
        d, nm, nv = _adamw(mats[name][0], g, moments_m[name][0], moments_v[name][0], "adamw_" + name)
        out_g[name], out_d[name], out_m[name], out_v[name] = g[None], d[None], nm[None], nv[None]
    gain_w = jnp.concatenate([norm_mix_g, norm_ffn_g, g_fin], axis=0)
    gain_m = jnp.concatenate([m_norm_mix_g, m_norm_ffn_g, m_norm_final_g.reshape(1, D_MODEL)], axis=0)
    gain_v = jnp.concatenate([v_norm_mix_g, v_norm_ffn_g, v_norm_final_g.reshape(1, D_MODEL)], axis=0)
    gd, gm, gv = _adamw(gain_w, g_gains[:3], gain_m, gain_v, "adamw_gains")
    for idx, name in enumerate(("norm_mix_g", "norm_ffn_g", "norm_final_g")):
        shape = (D_MODEL,) if name == "norm_final_g" else (1, D_MODEL)
        out_g[name] = g_gains[idx].reshape(shape)
        out_d[name], out_m[name], out_v[name] = gd[idx].reshape(shape), gm[idx].reshape(shape), gv[idx].reshape(shape)

    order = ("norm_mix_g", "w_in", "w_sb_up", "w_dil_up", "w_out", "norm_ffn_g", "w_ffn_in", "w_ffn_out",
             "norm_final_g")
    total_loss = lax.psum(loss[0, 0], ("x", "y", "c"))
    return (total_loss, grad_x.reshape(B_LOC, SEQ, D_MODEL),
            *[out_g[n] for n in order], *[out_d[n] for n in order],
            *[out_m[n] for n in order], *[out_v[n] for n in order])
```

```python
import math

import jax
import jax.numpy as jnp
from jax import lax
from jax.experimental import pallas as pl
from jax.experimental.pallas import tpu as pltpu

F32 = jnp.float32
BF16 = jnp.bfloat16

N_DEV = 8
D_MODEL = 1024
SEQ = 2048
B_LOC = 2
TOK = B_LOC * SEQ
HEAD_DIM = 64
SB_WIDTH = 512
DIL_WIDTH = 768
DIL_OUT = 256
QKV_WIDTH = 3 * SB_WIDTH + 3 * DIL_WIDTH
IN_WIDTH = QKV_WIDTH + 2 * D_MODEL
D_FF = 2816
DIL_PAIRS = ((128, 1), (512, 4), (2048, 16))
DIL_HEADS = 12
RMS_EPS = 1e-6
ALIBI_MAX_BIAS = 8.0
QK_SCALE = 1.0 / math.sqrt(HEAD_DIM)
BLK = 128
LANES = 128
NEG_BIG = -1e30

ADAM_LR = 0.001
ADAM_B1 = 0.9
ADAM_B2 = 0.999
ADAM_EPS = 1e-08
ADAM_WD = 0.01
ADAM_STEP = 10

VMEM_LIMIT = 56 * 1024 * 1024

SHARD_SHAPES = (
    ("w_in", (D_MODEL, IN_WIDTH // N_DEV)),
    ("w_sb_up", (SB_WIDTH, D_MODEL // N_DEV)),
    ("w_dil_up", (DIL_OUT, D_MODEL // N_DEV)),
    ("w_out", (D_MODEL // N_DEV, D_MODEL)),
    ("w_ffn_in", (D_MODEL, 2 * D_FF // N_DEV)),
    ("w_ffn_out", (D_FF // N_DEV, D_MODEL)),
)
COL_SHARDED = ("w_in", "w_sb_up", "w_dil_up", "w_ffn_in")
PACK_ROWS = sum(r * c for _, (r, c) in SHARD_SHAPES) // LANES


def _dot(a, b):
    return jnp.dot(a, b, preferred_element_type=F32)


def _dot_nt(a, b):
    return lax.dot_general(a, b, (((1,), (1,)), ((), ())), preferred_element_type=F32)


def _dot_tn(a, b):
    return lax.dot_general(a, b, (((0,), (0,)), ((), ())), preferred_element_type=F32)


def _softplus(z):
    return jnp.maximum(z, 0.0) + jnp.log1p(jnp.exp(-jnp.abs(z)))


def _sigmoid(z):
    return 1.0 / (1.0 + jnp.exp(-z))


def _split_bf16(v):
    hi = v.astype(BF16)
    lo = (v - hi.astype(F32)).astype(BF16)
    return hi, lo


def _chunks(width, step=512):
    out, c = [], 0
    while c < width:
        w = min(step, width - c)
        out.append((c, w))
        c += w
    return out


def _resident(shape):
    nd = len(shape)
    return pl.BlockSpec(shape, lambda *_: (0,) * nd, pipeline_mode=pl.Buffered(1))


def _params(sem):
    return pltpu.CompilerParams(dimension_semantics=sem, vmem_limit_bytes=VMEM_LIMIT)


def _rms_fwd(x, g):
    r = lax.rsqrt(jnp.mean(x * x, axis=-1, keepdims=True) + RMS_EPS)
    n = x * r
    return n, r, n * g


def _rms_bwd(dy, n, r, g):
    dg = jnp.sum(dy * n, axis=0, keepdims=True)
    dn = dy * g
    dx = r * (dn - n * jnp.mean(dn * n, axis=-1, keepdims=True))
    return dx, dg


TM = 256


def _norm_proj(x, g, w_in):
    def body(x_ref, g_ref, w_ref, qkv_ref, gate_ref, u_ref):
        _, _, u = _rms_fwd(x_ref[...], g_ref[...])
        u = u.astype(BF16)
        u_ref[...] = u
        for c0, w in _chunks(QKV_WIDTH):
            qkv_ref[:, c0:c0 + w] = _dot(u, w_ref[:, c0:c0 + w]).astype(BF16)
        for c0, w in _chunks(2 * D_MODEL):
            gate_ref[:, c0:c0 + w] = _dot(u, w_ref[:, QKV_WIDTH + c0:QKV_WIDTH + c0 + w])

    return pl.pallas_call(
        body, name="norm_proj", grid=(TOK // TM,),
        in_specs=[pl.BlockSpec((TM, D_MODEL), lambda i: (i, 0)), _resident((1, D_MODEL)),
                  _resident((D_MODEL, IN_WIDTH))],
        out_specs=[pl.BlockSpec((TM, QKV_WIDTH), lambda i: (i, 0)),
                   pl.BlockSpec((TM, 2 * D_MODEL), lambda i: (i, 0)),
                   pl.BlockSpec((TM, D_MODEL), lambda i: (i, 0))],
        out_shape=[jax.ShapeDtypeStruct((TOK, QKV_WIDTH), BF16),
                   jax.ShapeDtypeStruct((TOK, 2 * D_MODEL), F32),
                   jax.ShapeDtypeStruct((TOK, D_MODEL), BF16)],
        compiler_params=_params(("parallel",)),
    )(x, g, w_in)


def _mix_out(x, o_sb, o_dl, gates, w_sb_up, w_dil_up, w_out):
    def body(x_ref, osb_ref, odl_ref, gate_ref, wsb_ref, wdl_ref, wout_ref, x1_ref, mg_ref):
        y_sb = _dot(osb_ref[...], wsb_ref[...])
        y_dl = _dot(odl_ref[...].astype(BF16), wdl_ref[...])
        merged = (_sigmoid(gate_ref[:, :D_MODEL]) * y_sb
                  + _sigmoid(gate_ref[:, D_MODEL:]) * y_dl).astype(BF16)
        mg_ref[...] = merged
        x1_ref[...] = x_ref[...] + _dot(merged, wout_ref[...])

    return pl.pallas_call(
        body, name="mix_out", grid=(TOK // TM,),
        in_specs=[pl.BlockSpec((TM, D_MODEL), lambda i: (i, 0)),
                  pl.BlockSpec((TM, SB_WIDTH), lambda i: (i, 0)),
                  pl.BlockSpec((TM, DIL_OUT), lambda i: (i, 0)),
                  pl.BlockSpec((TM, 2 * D_MODEL), lambda i: (i, 0)),
                  _resident((SB_WIDTH, D_MODEL)), _resident((DIL_OUT, D_MODEL)),
                  _resident((D_MODEL, D_MODEL))],
        out_specs=[pl.BlockSpec((TM, D_MODEL), lambda i: (i, 0)),
                   pl.BlockSpec((TM, D_MODEL), lambda i: (i, 0))],
        out_shape=[jax.ShapeDtypeStruct((TOK, D_MODEL), F32),
                   jax.ShapeDtypeStruct((TOK, D_MODEL), BF16)],
        compiler_params=_params(("parallel",)),
    )(x, o_sb, o_dl, gates, w_sb_up, w_dil_up, w_out)


FF_CHUNK = D_FF // 2


def _ffn_fwd_bwd(x1, target, g_ffn, g_fin, w_ffn_in, w_ffn_out):
    def body(x1_ref, t_ref, gffn_ref, gfin_ref, win_ref, wout_ref,
             loss_ref, dx1_ref, u2_ref, act_ref, dh_ref, dx2_ref, dgfin_ref, dgffn_ref, h_scr):
        i = pl.program_id(0)

        @pl.when(i == 0)
        def _():
            loss_ref[...] = jnp.zeros_like(loss_ref)
            dgfin_ref[...] = jnp.zeros_like(dgfin_ref)
            dgffn_ref[...] = jnp.zeros_like(dgffn_ref)

        x1 = x1_ref[...]
        g_ffn_v = gffn_ref[...]
        g_fin_v = gfin_ref[...]
        n2, r2, u2 = _rms_fwd(x1, g_ffn_v)
        u2 = u2.astype(BF16)
        u2_ref[...] = u2
        x2 = x1
        for c0 in range(0, D_FF, FF_CHUNK):
            gate = _dot(u2, win_ref[:, c0:c0 + FF_CHUNK])
            up = _dot(u2, win_ref[:, D_FF + c0:D_FF + c0 + FF_CHUNK])
            h_scr[:, c0:c0 + FF_CHUNK] = gate
            h_scr[:, D_FF + c0:D_FF + c0 + FF_CHUNK] = up
            act = (gate * _sigmoid(gate) * up).astype(BF16)
            act_ref[:, c0:c0 + FF_CHUNK] = act
            x2 = x2 + _dot(act, wout_ref[c0:c0 + FF_CHUNK, :])
        n3, r3, y = _rms_fwd(x2, g_fin_v)
        err = y - t_ref[...]
        sq = jnp.sum(jnp.sum(err * err, axis=1, keepdims=True), axis=0, keepdims=True)
        loss_ref[...] += sq * (0.5 / D_MODEL)
        dx2, dgfin = _rms_bwd(err * (1.0 / D_MODEL), n3, r3, g_fin_v)
        dgfin_ref[...] += dgfin
        dx2_b = dx2.astype(BF16)
        dx2_ref[...] = dx2_b
        du2 = jnp.zeros((TM, D_MODEL), F32)
        for c0 in range(0, D_FF, FF_CHUNK):
            gate = h_scr[:, c0:c0 + FF_CHUNK]
            up = h_scr[:, D_FF + c0:D_FF + c0 + FF_CHUNK]
            dact = _dot_nt(dx2_b, wout_ref[c0:c0 + FF_CHUNK, :])
            sg = _sigmoid(gate)
            dgate = (dact * up * (sg * (1.0 + gate * (1.0 - sg)))).astype(BF16)
            dup = (dact * (gate * sg)).astype(BF16)
            dh_ref[:, c0:c0 + FF_CHUNK] = dgate
            dh_ref[:, D_FF + c0:D_FF + c0 + FF_CHUNK] = dup
            du2 = du2 + _dot_nt(dgate, win_ref[:, c0:c0 + FF_CHUNK])
            du2 = du2 + _dot_nt(dup, win_ref[:, D_FF + c0:D_FF + c0 + FF_CHUNK])
        dx1_n, dgffn = _rms_bwd(du2, n2, r2, g_ffn_v)
        dgffn_ref[...] += dgffn
        dx1_ref[...] = dx2 + dx1_n

    tile = lambda w: pl.BlockSpec((TM, w), lambda i: (i, 0))
    acc = lambda w: pl.BlockSpec((1, w), lambda i: (0, 0))
    return pl.pallas_call(
        body, name="ffn_fwd_bwd", grid=(TOK // TM,),
        in_specs=[tile(D_MODEL), tile(D_MODEL), _resident((1, D_MODEL)), _resident((1, D_MODEL)),
                  _resident((D_MODEL, 2 * D_FF)), _resident((D_FF, D_MODEL))],
        out_specs=[acc(LANES), tile(D_MODEL), tile(D_MODEL), tile(D_FF), tile(2 * D_FF), tile(D_MODEL),
                   acc(D_MODEL), acc(D_MODEL)],
        out_shape=[jax.ShapeDtypeStruct((1, LANES), F32),
                   jax.ShapeDtypeStruct((TOK, D_MODEL), F32),
                   jax.ShapeDtypeStruct((TOK, D_MODEL), BF16),
                   jax.ShapeDtypeStruct((TOK, D_FF), BF16),
                   jax.ShapeDtypeStruct((TOK, 2 * D_FF), BF16),
                   jax.ShapeDtypeStruct((TOK, D_MODEL), BF16),
                   jax.ShapeDtypeStruct((1, D_MODEL), F32),
                   jax.ShapeDtypeStruct((1, D_MODEL), F32)],
        scratch_shapes=[pltpu.VMEM((TM, 2 * D_FF), F32)],
        compiler_params=_params(("arbitrary",)),
    )(x1, target, g_ffn, g_fin, w_ffn_in, w_ffn_out)


def _mix_bwd(dx1, o_sb, o_dl, gates, w_sb_up, w_dil_up, w_out):
    def body(dx1_ref, osb_ref, odl_ref, gate_ref, wsb_ref, wdl_ref, wout_ref,
             dgate_ref, dysb_ref, dydl_ref, dosb_ref, dodl_ref, dsum_ref):
        dmerged = _dot_nt(dx1_ref[...].astype(BF16), wout_ref[...])
        o_dl = odl_ref[...]
        y_sb = _dot(osb_ref[...], wsb_ref[...])
        y_dl = _dot(o_dl.astype(BF16), wdl_ref[...])
        s_sb = _sigmoid(gate_ref[:, :D_MODEL])
        s_dl = _sigmoid(gate_ref[:, D_MODEL:])
        dgate_ref[:, :D_MODEL] = (dmerged * y_sb * (s_sb * (1.0 - s_sb))).astype(BF16)
        dgate_ref[:, D_MODEL:] = (dmerged * y_dl * (s_dl * (1.0 - s_dl))).astype(BF16)
        dy_sb = (dmerged * s_sb).astype(BF16)
        dy_dl = (dmerged * s_dl).astype(BF16)
        dysb_ref[...] = dy_sb
        dydl_ref[...] = dy_dl
        dosb_ref[...] = _dot_nt(dy_sb, wsb_ref[...]).astype(BF16)
        do_dl = _dot_nt(dy_dl, wdl_ref[...])
        dodl_ref[...] = do_dl
        row = lax.broadcasted_iota(jnp.int32, (DIL_OUT, DIL_OUT), 0) // HEAD_DIM
        col = lax.broadcasted_iota(jnp.int32, (DIL_OUT, DIL_OUT), 1) // HEAD_DIM
        same_head = (row == col).astype(BF16)
        hi, lo = _split_bf16(do_dl * o_dl)
        dsum_ref[...] = _dot(hi, same_head) + _dot(lo, same_head)

    tile = lambda w: pl.BlockSpec((TM, w), lambda i: (i, 0))
    return pl.pallas_call(
        body, name="mix_bwd", grid=(TOK // TM,),
        in_specs=[tile(D_MODEL), tile(SB_WIDTH), tile(DIL_OUT), tile(2 * D_MODEL),
                  _resident((SB_WIDTH, D_MODEL)), _resident((DIL_OUT, D_MODEL)),
                  _resident((D_MODEL, D_MODEL))],
        out_specs=[tile(2 * D_MODEL), tile(D_MODEL), tile(D_MODEL), tile(SB_WIDTH), tile(DIL_OUT),
                   tile(DIL_OUT)],
        out_shape=[jax.ShapeDtypeStruct((TOK, 2 * D_MODEL), BF16),
                   jax.ShapeDtypeStruct((TOK, D_MODEL), BF16),
                   jax.ShapeDtypeStruct((TOK, D_MODEL), BF16),
                   jax.ShapeDtypeStruct((TOK, SB_WIDTH), BF16),
                   jax.ShapeDtypeStruct((TOK, DIL_OUT), F32),
                   jax.ShapeDtypeStruct((TOK, DIL_OUT), F32)],
        compiler_params=_params(("parallel",)),
    )(dx1, o_sb, o_dl, gates, w_sb_up, w_dil_up, w_out)


def _proj_bwd(dproj, dx1, x, g, w_in):
    def body(dp_ref, dx1_ref, x_ref, g_ref, w_ref, dx_ref, dg_ref):
        @pl.when(pl.program_id(0) == 0)
        def _():
            dg_ref[...] = jnp.zeros_like(dg_ref)

        du = jnp.zeros((TM, D_MODEL), F32)
        for c0, w in _chunks(IN_WIDTH, 1024):
            du = du + _dot_nt(dp_ref[:, c0:c0 + w], w_ref[:, c0:c0 + w])
        g_v = g_ref[...]
        n, r, _ = _rms_fwd(x_ref[...], g_v)
        dx, dg = _rms_bwd(du, n, r, g_v)
        dg_ref[...] += dg
        dx_ref[...] = dx1_ref[...] + dx

    tile = lambda w: pl.BlockSpec((TM, w), lambda i: (i, 0))
    return pl.pallas_call(
        body, name="proj_bwd", grid=(TOK // TM,),
        in_specs=[tile(IN_WIDTH), tile(D_MODEL), tile(D_MODEL), _resident((1, D_MODEL)),
                  _resident((D_MODEL, IN_WIDTH))],
        out_specs=[tile(D_MODEL), pl.BlockSpec((1, D_MODEL), lambda i: (0, 0))],
        out_shape=[jax.ShapeDtypeStruct((TOK, D_MODEL), F32),
                   jax.ShapeDtypeStruct((1, D_MODEL), F32)],
        compiler_params=_params(("arbitrary",)),
    )(dproj, dx1, x, g, w_in)


def _atb(a, b, name, tm, tn, tk=512):
    m, n = a.shape[1], b.shape[1]
    nk = TOK // tk

    def body(a_ref, b_ref, o_ref):
        @pl.when(pl.program_id(2) == 0)
        def _():
            o_ref[...] = jnp.zeros_like(o_ref)

        o_ref[...] += _dot_tn(a_ref[...], b_ref[...])

    return pl.pallas_call(
        body, name=name, grid=(m // tm, n // tn, nk),
        in_specs=[pl.BlockSpec((tk, tm), lambda i, j, k: (k, i)),
                  pl.BlockSpec((tk, tn), lambda i, j, k: (k, j))],
        out_specs=pl.BlockSpec((tm, tn), lambda i, j, k: (i, j)),
        out_shape=jax.ShapeDtypeStruct((m, n), F32),
        compiler_params=_params(("parallel", "parallel", "arbitrary")),
    )(a, b)


SB_PAIRS = SB_WIDTH // LANES


def _sb_masks():
    row = lax.broadcasted_iota(jnp.int32, (BLK, BLK), 0)
    col = lax.broadcasted_iota(jnp.int32, (BLK, BLK), 1)
    return row, col, col < HEAD_DIM


def _two_heads(v, lane0):
    zero = jnp.zeros_like(v)
    return jnp.where(lane0, v, zero), jnp.where(lane0, zero, v)


SB_QBLK = 256
N_SB_STEPS = SEQ // SB_QBLK


SB_KCHUNK = 2 * BLK
SB_ROWS = 2 * SB_QBLK


def _log_keep(z):
    neg_z = -z
    return jnp.minimum(neg_z, 0.0) - jnp.log(1.0 + jnp.exp(jnp.minimum(z, neg_z)))


def _stack_heads(v, lane0):
    return jnp.concatenate(_two_heads(v, lane0), axis=0)


def _block_sums(v, tri):
    halves = (v[:, :BLK], v[:, BLK:])
    hi, lo = _split_bf16(jnp.concatenate(halves, axis=0))
    prod = _dot(jnp.concatenate([hi, lo], axis=0), tri)
    tri_sum = prod[:2 * SB_ROWS] + prod[2 * SB_ROWS:]
    sums = tuple(jnp.sum(h, axis=1, keepdims=True) for h in halves)
    return (tri_sum[:SB_ROWS], tri_sum[SB_ROWS:]), sums


def _sb_diag_mask():
    row = lax.broadcasted_iota(jnp.int32, (SB_ROWS, SB_KCHUNK), 0)
    col = lax.broadcasted_iota(jnp.int32, (SB_ROWS, SB_KCHUNK), 1)
    return col < jnp.where(row >= SB_QBLK, row - SB_QBLK, row)


def _sb_fwd(qkv):
    def body(q_ref, k_ref, v_ref, o_ref, tot_ref):
        i = pl.program_id(2)
        krow = lax.broadcasted_iota(jnp.int32, (BLK, BLK), 0)
        kcol = lax.broadcasted_iota(jnp.int32, (BLK, BLK), 1)
        later = (krow > kcol).astype(BF16)
        lane0 = lax.broadcasted_iota(jnp.int32, (SB_QBLK, LANES), 1) < HEAD_DIM
        q2 = _stack_heads(q_ref[0] * QK_SCALE, lane0)

        def chunk(c, carry, causal):
            acc, run = carry
            off = pl.multiple_of(c * SB_KCHUNK, SB_KCHUNK)
            z = _dot_nt(q2, k_ref[0, pl.ds(off, SB_KCHUNK), :])
            log_keep = _log_keep(z)
            if causal is not None:
                log_keep = jnp.where(causal, log_keep, 0.0)
            suffix, sums = _block_sums(log_keep, later)
            log_after = jnp.concatenate([suffix[0] + (run + sums[1]), suffix[1] + run], axis=1)
            a = jnp.exp(log_keep + z + log_after)
            if causal is not None:
                a = jnp.where(causal, a, 0.0)
            acc = acc + _dot(a.astype(BF16), v_ref[0, pl.ds(off, SB_KCHUNK), :])
            return acc, run + (sums[0] + sums[1])

        carry = chunk(i, (jnp.zeros((SB_ROWS, LANES), F32), jnp.zeros((SB_ROWS, 1), F32)), _sb_diag_mask())
        acc, run = lax.fori_loop(0, i, lambda t, c: chunk(i - 1 - t, c, None), carry)
        o_ref[0] = jnp.where(lane0, acc[:SB_QBLK], acc[SB_QBLK:]).astype(BF16)
        tot_ref[0] = jnp.where(lane0, run[:SB_QBLK], run[SB_QBLK:])

    blk = pl.BlockSpec((1, SB_QBLK, LANES), lambda b, h, i: (b, i, h))
    return pl.pallas_call(
        body, name="sb_fwd", grid=(B_LOC, SB_PAIRS, N_SB_STEPS),
        in_specs=[blk,
                  pl.BlockSpec((1, SEQ, LANES), lambda b, h, i: (b, 0, SB_PAIRS + h)),
                  pl.BlockSpec((1, SEQ, LANES), lambda b, h, i: (b, 0, 2 * SB_PAIRS + h))],
        out_specs=[blk, blk],
        out_shape=[jax.ShapeDtypeStruct((B_LOC, SEQ, SB_WIDTH), BF16),
                   jax.ShapeDtypeStruct((B_LOC, SEQ, SB_WIDTH), F32)],
        compiler_params=_params(("parallel", "parallel", "arbitrary")),
    )(qkv, qkv, qkv)


def _sb_bwd(qkv, d_o, tot):
    def body(q_ref, k_ref, v_ref, do_ref, tot_ref, dq_ref, dk_ref, dv_ref, dk_acc, dv_acc):
        i = pl.program_id(2)
        krow = lax.broadcasted_iota(jnp.int32, (BLK, BLK), 0)
        kcol = lax.broadcasted_iota(jnp.int32, (BLK, BLK), 1)
        upto = (krow <= kcol).astype(BF16)
        earlier = (krow < kcol).astype(BF16)
        lane0 = lax.broadcasted_iota(jnp.int32, (SB_QBLK, LANES), 1) < HEAD_DIM
        q2 = _stack_heads(q_ref[0] * QK_SCALE, lane0)
        do2 = _stack_heads(do_ref[0], lane0)
        tot = tot_ref[0]
        tot2 = jnp.concatenate([_head_col(tot, lane0), _head_col(tot, jnp.logical_not(lane0))], axis=0)

        @pl.when(i == 0)
        def _():
            dk_acc[...] = jnp.zeros_like(dk_acc)
            dv_acc[...] = jnp.zeros_like(dv_acc)

        def chunk(c, carry, causal):
            dq, pre_keep, pre_e = carry
            off = pl.multiple_of(c * SB_KCHUNK, SB_KCHUNK)
            k_c = k_ref[0, pl.ds(off, SB_KCHUNK), :]
            v_c = v_ref[0, pl.ds(off, SB_KCHUNK), :]
            z = _dot_nt(q2, k_c)
            d_a = _dot_nt(do2, v_c)
            log_keep = _log_keep(z)
            if causal is not None:
                log_keep = jnp.where(causal, log_keep, 0.0)
            log_beta = log_keep + z
            prefix, sums = _block_sums(log_keep, upto)
            inclusive = jnp.concatenate([prefix[0], prefix[1] + sums[0]], axis=1)
            a = jnp.exp(log_beta + ((tot2 - pre_keep) - inclusive))
            if causal is not None:
                a = jnp.where(causal, a, 0.0)
            e = d_a * a
            e_prefix, e_sums = _block_sums(e, earlier)
            before = jnp.concatenate([e_prefix[0] + pre_e, e_prefix[1] + (pre_e + e_sums[0])], axis=1)
            dz = e - (e + before) * jnp.exp(log_beta)
            if causal is not None:
                dz = jnp.where(causal, dz, 0.0)
            dz = dz.astype(BF16)
            dq = dq + _dot(dz, k_c)
            dk_acc[pl.ds(off, SB_KCHUNK), :] += _dot_tn(dz, q2)
            dv_acc[pl.ds(off, SB_KCHUNK), :] += _dot_tn(a.astype(BF16), do2)
            return dq, pre_keep + (sums[0] + sums[1]), pre_e + (e_sums[0] + e_sums[1])

        zero_col = jnp.zeros((SB_ROWS, 1), F32)
        carry = lax.fori_loop(0, i, lambda t, c: chunk(t, c, None),
                              (jnp.zeros((SB_ROWS, LANES), F32), zero_col, zero_col))
        dq, _, _ = chunk(i, carry, _sb_diag_mask())
        dq_ref[0] = (jnp.where(lane0, dq[:SB_QBLK], dq[SB_QBLK:]) * QK_SCALE).astype(BF16)

        @pl.when(i == N_SB_STEPS - 1)
        def _():
            dk_ref[0] = dk_acc[...].astype(BF16)
            dv_ref[0] = dv_acc[...].astype(BF16)

    blk = pl.BlockSpec((1, SB_QBLK, LANES), lambda b, h, i: (b, i, h))
    whole = lambda c: pl.BlockSpec((1, SEQ, LANES), lambda b, h, i: (b, 0, c * SB_PAIRS + h))
    out = jax.ShapeDtypeStruct((B_LOC, SEQ, SB_WIDTH), BF16)
    return pl.pallas_call(
        body, name="sb_bwd", grid=(B_LOC, SB_PAIRS, N_SB_STEPS),
        in_specs=[blk, whole(1), whole(2), blk, blk],
        out_specs=[blk, whole(0), whole(0)],
        out_shape=[out, out, out],
        scratch_shapes=[pltpu.VMEM((SEQ, LANES), F32), pltpu.VMEM((SEQ, LANES), F32)],
        compiler_params=_params(("parallel", "parallel", "arbitrary")),
    )(qkv, qkv, qkv, d_o, tot)


def _dil_scores(q_h, k_cur, k_prev, slope, dilation, has_prev, row, col):
    dist = (row - col).astype(F32) * float(dilation)
    s_cur = _dot_nt(q_h, k_cur) * QK_SCALE - slope * dist
    s_prev = _dot_nt(q_h, k_prev) * QK_SCALE - slope * (dist + float(BLK * dilation))
    s_cur = jnp.where(col <= row, s_cur, NEG_BIG)
    s_prev = jnp.where(jnp.logical_and(col >= row, has_prev), s_prev, NEG_BIG)
    return s_cur, s_prev


def _dil_slopes(group):
    pair = pl.program_id(1) % 2
    first = float(4 * group + 1) + 2.0 * pair.astype(F32)
    coef = -ALIBI_MAX_BIAS / DIL_HEADS * math.log(2.0)
    return jnp.exp(coef * first), jnp.exp(coef * (first + 1.0))


def _head_col(v, lane_mask):
    return jnp.max(jnp.where(lane_mask, v, NEG_BIG), axis=1, keepdims=True)


def _dil_fwd(q, k, v, group):
    dilation = DIL_PAIRS[group][1]
    length = SEQ // dilation
    nblk = length // BLK

    def body(q_ref, k_ref, v_ref, o_ref, lse_ref):
        row, col, lane0 = _sb_masks()
        slopes = _dil_slopes(group)

        def step(n, _):
            off = pl.multiple_of(n * BLK, BLK)
            off_prev = pl.multiple_of(jnp.maximum(n - 1, 0) * BLK, BLK)
            q_h = _two_heads(q_ref[0, pl.ds(off, BLK), :], lane0)
            k_cur = k_ref[0, pl.ds(off, BLK), :]
            v_cur = v_ref[0, pl.ds(off, BLK), :]
            k_prev = k_ref[0, pl.ds(off_prev, BLK), :]
            v_prev = v_ref[0, pl.ds(off_prev, BLK), :]
            outs, lses = [], []
            for h in range(2):
                s_cur, s_prev = _dil_scores(q_h[h], k_cur, k_prev, slopes[h], dilation, n > 0, row, col)
                m = jnp.maximum(jnp.max(s_cur, axis=1, keepdims=True), jnp.max(s_prev, axis=1, keepdims=True))
                p_cur = jnp.exp(s_cur - m)
                p_prev = jnp.exp(s_prev - m)
                den = jnp.sum(p_cur, axis=1, keepdims=True) + jnp.sum(p_prev, axis=1, keepdims=True)
                num = _dot(p_cur.astype(BF16), v_cur) + _dot(p_prev.astype(BF16), v_prev)
                outs.append(num / den)
                lses.append(m + jnp.log(den))
            o_ref[0, pl.ds(off, BLK), :] = jnp.where(lane0, outs[0], outs[1])
            lse_ref[0, pl.ds(off, BLK), :] = jnp.where(lane0, lses[0], lses[1])
            return 0

        lax.fori_loop(0, nblk, step, 0)

    spec = pl.BlockSpec((1, length, LANES), lambda b, c: (b, 0, c))
    out = jax.ShapeDtypeStruct((B_LOC, length, dilation * DIL_OUT), F32)
    return pl.pallas_call(
        body, name=f"dil_fwd_{group}", grid=(B_LOC, 2 * dilation),
        in_specs=[spec, spec, spec], out_specs=[spec, spec], out_shape=[out, out],
        compiler_params=_params(("parallel", "parallel")),
    )(q, k, v)


def _dil_bwd(q, k, v, d_o, lse, dsum, group):
    dilation = DIL_PAIRS[group][1]
    length = SEQ // dilation
    nblk = length // BLK

    def body(q_ref, k_ref, v_ref, do_ref, lse_ref, dsum_ref, dq_ref, dk_ref, dv_ref, dk_acc, dv_acc):
        row, col, lane0 = _sb_masks()
        lane1 = jnp.logical_not(lane0)
        slopes = _dil_slopes(group)
        dk_acc[...] = jnp.zeros_like(dk_acc)
        dv_acc[...] = jnp.zeros_like(dv_acc)

        def step(n, _):
            off = pl.multiple_of(n * BLK, BLK)
            off_prev = pl.multiple_of(jnp.maximum(n - 1, 0) * BLK, BLK)
            q_h = _two_heads(q_ref[0, pl.ds(off, BLK), :], lane0)
            do_h = _two_heads(do_ref[0, pl.ds(off, BLK), :].astype(BF16), lane0)
            k_cur = k_ref[0, pl.ds(off, BLK), :]
            v_cur = v_ref[0, pl.ds(off, BLK), :]
            k_prev = k_ref[0, pl.ds(off_prev, BLK), :]
            v_prev = v_ref[0, pl.ds(off_prev, BLK), :]
            kc_h = _two_heads(k_cur, lane0)
            kp_h = _two_heads(k_prev, lane0)
            lse_blk = lse_ref[0, pl.ds(off, BLK), :]
            dsum_blk = dsum_ref[0, pl.ds(off, BLK), :]
            dq = jnp.zeros((BLK, LANES), F32)
            dk_c = jnp.zeros((BLK, LANES), F32)
            dk_p = jnp.zeros((BLK, LANES), F32)
            dv_c = jnp.zeros((BLK, LANES), F32)
            dv_p = jnp.zeros((BLK, LANES), F32)
            for h, lanes in enumerate((lane0, lane1)):
                s_cur, s_prev = _dil_scores(q_h[h], k_cur, k_prev, slopes[h], dilation, n > 0, row, col)
                lse_h = _head_col(lse_blk, lanes)
                dsum_h = _head_col(dsum_blk, lanes)
                p_cur = jnp.exp(s_cur - lse_h)
                p_prev = jnp.exp(s_prev - lse_h)
                ds_cur = (p_cur * (_dot_nt(do_h[h], v_cur) - dsum_h) * QK_SCALE).astype(BF16)
                ds_prev = (p_prev * (_dot_nt(do_h[h], v_prev) - dsum_h) * QK_SCALE).astype(BF16)
                dq = dq + _dot(ds_cur, kc_h[h]) + _dot(ds_prev, kp_h[h])
                dk_c = dk_c + _dot_tn(ds_cur, q_h[h])
                dk_p = dk_p + _dot_tn(ds_prev, q_h[h])
                dv_c = dv_c + _dot_tn(p_cur.astype(BF16), do_h[h])
                dv_p = dv_p + _dot_tn(p_prev.astype(BF16), do_h[h])
            dq_ref[0, pl.ds(off, BLK), :] = dq.astype(BF16)
            dk_acc[pl.ds(off, BLK), :] += dk_c
            dv_acc[pl.ds(off, BLK), :] += dv_c
            dk_acc[pl.ds(off_prev, BLK), :] += dk_p
            dv_acc[pl.ds(off_prev, BLK), :] += dv_p
            return 0

        lax.fori_loop(0, nblk, step, 0)
        dk_ref[0] = dk_acc[...].astype(BF16)
        dv_ref[0] = dv_acc[...].astype(BF16)

    spec = pl.BlockSpec((1, length, LANES), lambda b, c: (b, 0, c))
    out = jax.ShapeDtypeStruct((B_LOC, length, dilation * DIL_OUT), BF16)
    return pl.pallas_call(
        body, name=f"dil_bwd_{group}", grid=(B_LOC, 2 * dilation),
        in_specs=[spec] * 6, out_specs=[spec] * 3, out_shape=[out] * 3,
        scratch_shapes=[pltpu.VMEM((length, LANES), F32), pltpu.VMEM((length, LANES), F32)],
        compiler_params=_params(("parallel", "parallel")),
    )(q, k, v, d_o, lse, dsum)


def _dil_combine(outs, lses):
    def body(o0, o1, o2, l0, l1, l2, o_ref, lse_ref):
        ls = (l0[...], l1[...], l2[...])
        m = jnp.maximum(jnp.maximum(ls[0], ls[1]), ls[2])
        w = [jnp.exp(l - m) for l in ls]
        den = w[0] + w[1] + w[2]
        o_ref[...] = (w[0] * o0[...] + w[1] * o1[...] + w[2] * o2[...]) / den
        lse_ref[...] = m + jnp.log(den)

    tile = pl.BlockSpec((512, DIL_OUT), lambda i: (i, 0))
    out = jax.ShapeDtypeStruct((TOK, DIL_OUT), F32)
    return pl.pallas_call(
        body, name="dil_combine", grid=(TOK // 512,),
        in_specs=[tile] * 6, out_specs=[tile, tile], out_shape=[out, out],
        compiler_params=_params(("parallel",)),
    )(*outs, *lses)


def _to_residues(t, dilation):
    return t.reshape(B_LOC, SEQ // dilation, dilation * DIL_OUT)


def _peers():
    x, y, c = lax.axis_index("x"), lax.axis_index("y"), lax.axis_index("c")
    me = 4 * x + 2 * y + c
    peers = []
    for mask in range(1, N_DEV):
        px = 1 - x if mask & 4 else x
        py = 1 - y if mask & 2 else y
        pc = 1 - c if mask & 1 else c
        peers.append(((px, py, pc), 4 * px + 2 * py + pc))
    return me, peers


def _all_gather(packed):
    rows = packed.shape[0]

    def body(src_ref, out_ref, send_sems, recv_sems, local_sem):
        me, peers = _peers()
        mine = pltpu.make_async_copy(src_ref, out_ref.at[me], local_sem)
        mine.start()
        sends = []
        for k, (peer, _) in enumerate(peers):
            cp = pltpu.make_async_remote_copy(
                src_ref=src_ref, dst_ref=out_ref.at[me], send_sem=send_sems.at[k], recv_sem=recv_sems.at[k],
                device_id=peer, device_id_type=pl.DeviceIdType.MESH)
            cp.start()
            sends.append(cp)
        for k, (peer, peer_idx) in enumerate(peers):
            pltpu.make_async_remote_copy(
                src_ref=src_ref, dst_ref=out_ref.at[peer_idx], send_sem=send_sems.at[k], recv_sem=recv_sems.at[k],
                device_id=peer, device_id_type=pl.DeviceIdType.MESH).wait_recv()
        for cp in sends:
            cp.wait_send()
        mine.wait()

    return pl.pallas_call(
        body, name="all_gather_weights",
        in_specs=[pl.BlockSpec(memory_space=pl.ANY)],
        out_specs=pl.BlockSpec(memory_space=pl.ANY),
        out_shape=jax.ShapeDtypeStruct((N_DEV, rows, LANES), packed.dtype),
        scratch_shapes=[pltpu.SemaphoreType.DMA((N_DEV - 1,)), pltpu.SemaphoreType.DMA((N_DEV - 1,)),
                        pltpu.SemaphoreType.DMA],
    )(packed)


def _exchange(blocks, gains):
    rows = blocks.shape[1]

    def body(blk_ref, gain_ref, blk_out, gain_out, send_sems, recv_sems, local_sems):
        me, peers = _peers()
        own = [pltpu.make_async_copy(blk_ref.at[me], blk_out.at[me], local_sems.at[0]),
               pltpu.make_async_copy(gain_ref, gain_out.at[me], local_sems.at[1])]
        for cp in own:
            cp.start()
        sends = []
        for k, (peer, peer_idx) in enumerate(peers):
            for a, (src, dst) in enumerate(((blk_ref.at[peer_idx], blk_out.at[me]), (gain_ref, gain_out.at[me]))):
                cp = pltpu.make_async_remote_copy(
                    src_ref=src, dst_ref=dst, send_sem=send_sems.at[a, k], recv_sem=recv_sems.at[a, k],
                    device_id=peer, device_id_type=pl.DeviceIdType.MESH)
                cp.start()
                sends.append(cp)
        for k, (peer, peer_idx) in enumerate(peers):
            for a, (src, dst) in enumerate(((blk_ref.at[peer_idx], blk_out.at[peer_idx]),
                                            (gain_ref, gain_out.at[peer_idx]))):
                pltpu.make_async_remote_copy(
                    src_ref=src, dst_ref=dst, send_sem=send_sems.at[a, k], recv_sem=recv_sems.at[a, k],
                    device_id=peer, device_id_type=pl.DeviceIdType.MESH).wait_recv()
        for cp in sends:
            cp.wait_send()
        for cp in own:
            cp.wait()

    return pl.pallas_call(
        body, name="exchange_grads",
        in_specs=[pl.BlockSpec(memory_space=pl.ANY), pl.BlockSpec(memory_space=pl.ANY)],
        out_specs=[pl.BlockSpec(memory_space=pl.ANY), pl.BlockSpec(memory_space=pl.ANY)],
        out_shape=[jax.ShapeDtypeStruct((N_DEV, rows, LANES), blocks.dtype),
                   jax.ShapeDtypeStruct((N_DEV,) + gains.shape, gains.dtype)],
        scratch_shapes=[pltpu.SemaphoreType.DMA((2, N_DEV - 1)), pltpu.SemaphoreType.DMA((2, N_DEV - 1)),
                        pltpu.SemaphoreType.DMA((2,))],
    )(blocks, gains)


def _sum_slots(slots, name, tile_rows):
    _, rows, cols = slots.shape

    def body(s_ref, o_ref):
        acc = s_ref[0].astype(F32)
        for j in range(1, N_DEV):
            acc = acc + s_ref[j].astype(F32)
        o_ref[...] = acc

    return pl.pallas_call(
        body, name=name, grid=(rows // tile_rows,),
        in_specs=[pl.BlockSpec((N_DEV, tile_rows, cols), lambda i: (0, i, 0))],
        out_specs=pl.BlockSpec((tile_rows, cols), lambda i: (i, 0)),
        out_shape=jax.ShapeDtypeStruct((rows, cols), F32),
        compiler_params=_params(("parallel",)),
    )(slots)


def _adamw(w, g, m, v, name):
    rows, cols = w.shape
    tile_rows = max(t for t in range(8, 257, 8) if rows % t == 0) if rows % 8 == 0 else rows
    c1 = 1.0 - ADAM_B1 ** ADAM_STEP
    c2 = 1.0 - ADAM_B2 ** ADAM_STEP

    def body(w_ref, g_ref, m_ref, v_ref, d_ref, nm_ref, nv_ref):
        g_v = g_ref[...]
        m_new = ADAM_B1 * m_ref[...] + (1.0 - ADAM_B1) * g_v
        v_new = ADAM_B2 * v_ref[...] + (1.0 - ADAM_B2) * (g_v * g_v)
        nm_ref[...] = m_new
        nv_ref[...] = v_new
        d_ref[...] = -ADAM_LR * ((m_new / c1) / (jnp.sqrt(v_new / c2) + ADAM_EPS) + ADAM_WD * w_ref[...])

    tile = pl.BlockSpec((tile_rows, cols), lambda i: (i, 0))
    out = jax.ShapeDtypeStruct((rows, cols), F32)
    return pl.pallas_call(
        body, name=name, grid=(rows // tile_rows,),
        in_specs=[tile] * 4, out_specs=[tile] * 3, out_shape=[out] * 3,
        compiler_params=_params(("parallel",)),
    )(w, g, m, v)


def _pack_shards(shards):
    return jnp.concatenate([s.reshape(-1, LANES) for s in shards], axis=0)


def _unpack_shards(packed):
    out, r0 = [], 0
    for _, (r, c) in SHARD_SHAPES:
        n = r * c // LANES
        out.append(packed[r0:r0 + n].reshape(r, c))
        r0 += n
    return out


def _unpack_full(gathered):
    out, r0 = {}, 0
    for name, (r, c) in SHARD_SHAPES:
        n = r * c // LANES
        blk = gathered[:, r0:r0 + n].reshape(N_DEV, r, c)
        if name in COL_SHARDED:
            out[name] = blk.transpose(1, 0, 2).reshape(r, N_DEV * c)
        else:
            out[name] = blk.reshape(N_DEV * r, c)
        r0 += n
    return out


def _pack_full(full):
    parts = []
    for name, (r, c) in SHARD_SHAPES:
        t = full[name]
        if name in COL_SHARDED:
            t = t.reshape(r, N_DEV, c).transpose(1, 0, 2)
        parts.append(t.reshape(N_DEV, r * c // LANES, LANES))
    return jnp.concatenate(parts, axis=1)


def _local_step(x, target, g_mix, g_ffn, g_fin, w):
    qkv, gates, u = _norm_proj(x, g_mix, w["w_in"])
    qkv3 = qkv.reshape(B_LOC, SEQ, QKV_WIDTH)
    o_sb, sb_tot = _sb_fwd(qkv3)
    o_sb = o_sb.reshape(TOK, SB_WIDTH)

    dil_in, dil_o, dil_lse = [], [], []
    for grp, (_, dilation) in enumerate(DIL_PAIRS):
        parts = []
        for part in range(3):
            c0 = 3 * SB_WIDTH + part * DIL_WIDTH + grp * DIL_OUT
            parts.append(_to_residues(qkv[:, c0:c0 + DIL_OUT], dilation))
        dil_in.append(parts)
        o_g, lse_g = _dil_fwd(*parts, grp)
        dil_o.append(o_g.reshape(TOK, DIL_OUT))
        dil_lse.append(lse_g.reshape(TOK, DIL_OUT))
    o_dl, lse = _dil_combine(dil_o, dil_lse)

    x1, merged = _mix_out(x, o_sb, o_dl, gates, w["w_sb_up"], w["w_dil_up"], w["w_out"])
    loss, dx1, u2, act, dh, dx2, dg_fin, dg_ffn = _ffn_fwd_bwd(x1, target, g_ffn, g_fin, w["w_ffn_in"], w["w_ffn_out"])
    dgates, dy_sb, dy_dl, do_sb, do_dl, dsum = _mix_bwd(dx1, o_sb, o_dl, gates, w["w_sb_up"], w["w_dil_up"], w["w_out"])

    dq_sb, dk_sb, dv_sb = _sb_bwd(qkv3, do_sb.reshape(B_LOC, SEQ, SB_WIDTH), sb_tot)
    dq_dl, dk_dl, dv_dl = [], [], []
    for grp, (_, dilation) in enumerate(DIL_PAIRS):
        dq, dk, dv = _dil_bwd(*dil_in[grp], _to_residues(do_dl, dilation), _to_residues(lse, dilation),
                              _to_residues(dsum, dilation), grp)
        dq_dl.append(dq.reshape(TOK, DIL_OUT))
        dk_dl.append(dk.reshape(TOK, DIL_OUT))
        dv_dl.append(dv.reshape(TOK, DIL_OUT))
    flat = lambda t: t.reshape(TOK, SB_WIDTH)
    dproj = jnp.concatenate([flat(dq_sb), flat(dk_sb), flat(dv_sb), *dq_dl, *dk_dl, *dv_dl, dgates], axis=1)

    grad_x, dg_mix = _proj_bwd(dproj, dx1, x, g_mix, w["w_in"])
    grads = {
        "w_in": _atb(u, dproj, "grad_w_in", D_MODEL, IN_WIDTH // 2),
        "w_sb_up": _atb(o_sb, dy_sb, "grad_w_sb_up", SB_WIDTH, D_MODEL),
        "w_dil_up": _atb(o_dl.astype(BF16), dy_dl, "grad_w_dil_up", DIL_OUT, D_MODEL),
        "w_out": _atb(merged, dx1.astype(BF16), "grad_w_out", D_MODEL, D_MODEL),
        "w_ffn_in": _atb(u2, dh, "grad_w_ffn_in", D_MODEL, D_FF),
        "w_ffn_out": _atb(act, dx2, "grad_w_ffn_out", D_FF // 2, D_MODEL),
    }
    gain_grads = jnp.concatenate([dg_mix, dg_ffn, dg_fin], axis=0)
    return loss, grad_x, grads, gain_grads


def kernel(x, norm_mix_g, w_in, w_sb_up, w_dil_up, w_out, norm_ffn_g, w_ffn_in, w_ffn_out, norm_final_g, loss_target, m_norm_mix_g, m_w_in, m_w_sb_up, m_w_dil_up, m_w_out, m_norm_ffn_g, m_w_ffn_in, m_w_ffn_out, m_norm_final_g, v_norm_mix_g, v_w_in, v_w_sb_up, v_w_dil_up, v_w_out, v_norm_ffn_g, v_w_ffn_in, v_w_ffn_out, v_norm_final_g):
    mats = {"w_in": w_in, "w_sb_up": w_sb_up, "w_dil_up": w_dil_up, "w_out": w_out,
            "w_ffn_in": w_ffn_in, "w_ffn_out": w_ffn_out}
    moments_m = {"w_in": m_w_in, "w_sb_up": m_w_sb_up, "w_dil_up": m_w_dil_up, "w_out": m_w_out,
                 "w_ffn_in": m_w_ffn_in, "w_ffn_out": m_w_ffn_out}
    moments_v = {"w_in": v_w_in, "w_sb_up": v_w_sb_up, "w_dil_up": v_w_dil_up, "w_out": v_w_out,
                 "w_ffn_in": v_w_ffn_in, "w_ffn_out": v_w_ffn_out}
    shards = [mats[name][0] for name, _ in SHARD_SHAPES]

    gathered = _all_gather(_pack_shards([s.astype(BF16) for s in shards]))
    full = _unpack_full(gathered)

    g_fin = norm_final_g.reshape(1, D_MODEL)
    loss, grad_x, grads, gain_grads = _local_step(
        x.reshape(TOK, D_MODEL), loss_target.reshape(TOK, D_MODEL), norm_mix_g, norm_ffn_g, g_fin, full)

    gain_rows = jnp.concatenate([gain_grads, jnp.zeros((8 - 3, D_MODEL), F32)], axis=0)
    blk_slots, gain_slots = _exchange(_pack_full(grads).astype(BF16), gain_rows)
    g_shards = _unpack_shards(_sum_slots(blk_slots, "sum_grad_blocks", 384))
    g_gains = _sum_slots(gain_slots, "sum_gain_grads", 8)

    out_g, out_d, out_m, out_v = {}, {}, {}, {}
    for (name, _), g in zip(SHARD_SHAPES, g_shards):
        d, nm, nv = _adamw(mats[name][0], g, moments_m[name][0], moments_v[name][0], "adamw_" + name)
        out_g[name], out_d[name], out_m[name], out_v[name] = g[None], d[None], nm[None], nv[None]
    gain_w = jnp.concatenate([norm_mix_g, norm_ffn_g, g_fin], axis=0)
    gain_m = jnp.concatenate([m_norm_mix_g, m_norm_ffn_g, m_norm_final_g.reshape(1, D_MODEL)], axis=0)
    gain_v = jnp.concatenate([v_norm_mix_g, v_norm_ffn_g, v_norm_final_g.reshape(1, D_MODEL)], axis=0)
    gd, gm, gv = _adamw(gain_w, g_gains[:3], gain_m, gain_v, "adamw_gains")
    for idx, name in enumerate(("norm_mix_g", "norm_ffn_g", "norm_final_g")):
        shape = (D_MODEL,) if name == "norm_final_g" else (1, D_MODEL)
        out_g[name] = g_gains[idx].reshape(shape)
        out_d[name], out_m[name], out_v[name] = gd[idx].reshape(shape), gm[idx].reshape(shape), gv[idx].reshape(shape)

    order = ("norm_mix_g", "w_in", "w_sb_up", "w_dil_up", "w_out", "norm_ffn_g", "w_ffn_in", "w_ffn_out",
             "norm_final_g")
    total_loss = lax.psum(loss[0, 0], ("x", "y", "c"))
    return (total_loss, grad_x.reshape(B_LOC, SEQ, D_MODEL),
            *[out_g[n] for n in order], *[out_d[n] for n in order],
            *[out_m[n] for n in order], *[out_v[n] for n in order])
```

```python
import math

import jax
import jax.numpy as jnp
from jax import lax
from jax.experimental import pallas as pl
from jax.experimental.pallas import tpu as pltpu

F32 = jnp.float32
BF16 = jnp.bfloat16

N_DEV = 8
D_MODEL = 1024
SEQ = 2048
B_LOC = 2
TOK = B_LOC * SEQ
HEAD_DIM = 64
SB_WIDTH = 512
DIL_WIDTH = 768
DIL_OUT = 256
QKV_WIDTH = 3 * SB_WIDTH + 3 * DIL_WIDTH
IN_WIDTH = QKV_WIDTH + 2 * D_MODEL
D_FF = 2816
DIL_PAIRS = ((128, 1), (512, 4), (2048, 16))
DIL_HEADS = 12
RMS_EPS = 1e-6
ALIBI_MAX_BIAS = 8.0
QK_SCALE = 1.0 / math.sqrt(HEAD_DIM)
BLK = 128
LANES = 128
NEG_BIG = -1e30

ADAM_LR = 0.001
ADAM_B1 = 0.9
ADAM_B2 = 0.999
ADAM_EPS = 1e-08
ADAM_WD = 0.01
ADAM_STEP = 10

VMEM_LIMIT = 56 * 1024 * 1024

SHARD_SHAPES = (
    ("w_in", (D_MODEL, IN_WIDTH // N_DEV)),
    ("w_sb_up", (SB_WIDTH, D_MODEL // N_DEV)),
    ("w_dil_up", (DIL_OUT, D_MODEL // N_DEV)),
    ("w_out", (D_MODEL // N_DEV, D_MODEL)),
    ("w_ffn_in", (D_MODEL, 2 * D_FF // N_DEV)),
    ("w_ffn_out", (D_FF // N_DEV, D_MODEL)),
)
COL_SHARDED = ("w_in", "w_sb_up", "w_dil_up", "w_ffn_in")
PACK_ROWS = sum(r * c for _, (r, c) in SHARD_SHAPES) // LANES


def _dot(a, b):
    return jnp.dot(a, b, preferred_element_type=F32)


def _dot_nt(a, b):
    return lax.dot_general(a, b, (((1,), (1,)), ((), ())), preferred_element_type=F32)


def _dot_tn(a, b):
    return lax.dot_general(a, b, (((0,), (0,)), ((), ())), preferred_element_type=F32)


def _softplus(z):
    return jnp.maximum(z, 0.0) + jnp.log1p(jnp.exp(-jnp.abs(z)))


def _sigmoid(z):
    return 1.0 / (1.0 + jnp.exp(-z))


def _split_bf16(v):
    hi = v.astype(BF16)
    lo = (v - hi.astype(F32)).astype(BF16)
    return hi, lo


def _chunks(width, step=512):
    out, c = [], 0
    while c < width:
        w = min(step, width - c)
        out.append((c, w))
        c += w
    return out


def _resident(shape):
    nd = len(shape)
    return pl.BlockSpec(shape, lambda *_: (0,) * nd, pipeline_mode=pl.Buffered(1))


def _params(sem):
    return pltpu.CompilerParams(dimension_semantics=sem, vmem_limit_bytes=VMEM_LIMIT)


def _rms_fwd(x, g):
    r = lax.rsqrt(jnp.mean(x * x, axis=-1, keepdims=True) + RMS_EPS)
    n = x * r
    return n, r, n * g


def _rms_bwd(dy, n, r, g):
    dg = jnp.sum(dy * n, axis=0, keepdims=True)
    dn = dy * g
    dx = r * (dn - n * jnp.mean(dn * n, axis=-1, keepdims=True))
    return dx, dg


TM = 256


def _norm_proj(x, g, w_in):
    def body(x_ref, g_ref, w_ref, qkv_ref, gate_ref, u_ref):
        _, _, u = _rms_fwd(x_ref[...], g_ref[...])
        u = u.astype(BF16)
        u_ref[...] = u
        for c0, w in _chunks(QKV_WIDTH):
            qkv_ref[:, c0:c0 + w] = _dot(u, w_ref[:, c0:c0 + w]).astype(BF16)
        for c0, w in _chunks(2 * D_MODEL):
            gate_ref[:, c0:c0 + w] = _dot(u, w_ref[:, QKV_WIDTH + c0:QKV_WIDTH + c0 + w])

    return pl.pallas_call(
        body, name="norm_proj", grid=(TOK // TM,),
        in_specs=[pl.BlockSpec((TM, D_MODEL), lambda i: (i, 0)), _resident((1, D_MODEL)),
                  _resident((D_MODEL, IN_WIDTH))],
        out_specs=[pl.BlockSpec((TM, QKV_WIDTH), lambda i: (i, 0)),
                   pl.BlockSpec((TM, 2 * D_MODEL), lambda i: (i, 0)),
                   pl.BlockSpec((TM, D_MODEL), lambda i: (i, 0))],
        out_shape=[jax.ShapeDtypeStruct((TOK, QKV_WIDTH), BF16),
                   jax.ShapeDtypeStruct((TOK, 2 * D_MODEL), F32),
                   jax.ShapeDtypeStruct((TOK, D_MODEL), BF16)],
        compiler_params=_params(("parallel",)),
    )(x, g, w_in)


def _mix_out(x, o_sb, o_dl, gates, w_sb_up, w_dil_up, w_out):
    def body(x_ref, osb_ref, odl_ref, gate_ref, wsb_ref, wdl_ref, wout_ref, x1_ref, mg_ref):
        y_sb = _dot(osb_ref[...], wsb_ref[...])
        y_dl = _dot(odl_ref[...].astype(BF16), wdl_ref[...])
        merged = (_sigmoid(gate_ref[:, :D_MODEL]) * y_sb
                  + _sigmoid(gate_ref[:, D_MODEL:]) * y_dl).astype(BF16)
        mg_ref[...] = merged
        x1_ref[...] = x_ref[...] + _dot(merged, wout_ref[...])

    return pl.pallas_call(
        body, name="mix_out", grid=(TOK // TM,),
        in_specs=[pl.BlockSpec((TM, D_MODEL), lambda i: (i, 0)),
                  pl.BlockSpec((TM, SB_WIDTH), lambda i: (i, 0)),
                  pl.BlockSpec((TM, DIL_OUT), lambda i: (i, 0)),
                  pl.BlockSpec((TM, 2 * D_MODEL), lambda i: (i, 0)),
                  _resident((SB_WIDTH, D_MODEL)), _resident((DIL_OUT, D_MODEL)),
                  _resident((D_MODEL, D_MODEL))],
        out_specs=[pl.BlockSpec((TM, D_MODEL), lambda i: (i, 0)),
                   pl.BlockSpec((TM, D_MODEL), lambda i: (i, 0))],
        out_shape=[jax.ShapeDtypeStruct((TOK, D_MODEL), F32),
                   jax.ShapeDtypeStruct((TOK, D_MODEL), BF16)],
        compiler_params=_params(("parallel",)),
    )(x, o_sb, o_dl, gates, w_sb_up, w_dil_up, w_out)


FF_CHUNK = D_FF // 2


def _ffn_fwd_bwd(x1, target, g_ffn, g_fin, w_ffn_in, w_ffn_out):
    def body(x1_ref, t_ref, gffn_ref, gfin_ref, win_ref, wout_ref,
             loss_ref, dx1_ref, u2_ref, act_ref, dh_ref, dx2_ref, dgfin_ref, dgffn_ref, h_scr):
        i = pl.program_id(0)

        @pl.when(i == 0)
        def _():
            loss_ref[...] = jnp.zeros_like(loss_ref)
            dgfin_ref[...] = jnp.zeros_like(dgfin_ref)
            dgffn_ref[...] = jnp.zeros_like(dgffn_ref)

        x1 = x1_ref[...]
        g_ffn_v = gffn_ref[...]
        g_fin_v = gfin_ref[...]
        n2, r2, u2 = _rms_fwd(x1, g_ffn_v)
        u2 = u2.astype(BF16)
        u2_ref[...] = u2
        x2 = x1
        for c0 in range(0, D_FF, FF_CHUNK):
            gate = _dot(u2, win_ref[:, c0:c0 + FF_CHUNK])
            up = _dot(u2, win_ref[:, D_FF + c0:D_FF + c0 + FF_CHUNK])
            h_scr[:, c0:c0 + FF_CHUNK] = gate
            h_scr[:, D_FF + c0:D_FF + c0 + FF_CHUNK] = up
            act = (gate * _sigmoid(gate) * up).astype(BF16)
            act_ref[:, c0:c0 + FF_CHUNK] = act
            x2 = x2 + _dot(act, wout_ref[c0:c0 + FF_CHUNK, :])
        n3, r3, y = _rms_fwd(x2, g_fin_v)
        err = y - t_ref[...]
        sq = jnp.sum(jnp.sum(err * err, axis=1, keepdims=True), axis=0, keepdims=True)
        loss_ref[...] += sq * (0.5 / D_MODEL)
        dx2, dgfin = _rms_bwd(err * (1.0 / D_MODEL), n3, r3, g_fin_v)
        dgfin_ref[...] += dgfin
        dx2_b = dx2.astype(BF16)
        dx2_ref[...] = dx2_b
        du2 = jnp.zeros((TM, D_MODEL), F32)
        for c0 in range(0, D_FF, FF_CHUNK):
            gate = h_scr[:, c0:c0 + FF_CHUNK]
            up = h_scr[:, D_FF + c0:D_FF + c0 + FF_CHUNK]
            dact = _dot_nt(dx2_b, wout_ref[c0:c0 + FF_CHUNK, :])
            sg = _sigmoid(gate)
            dgate = (dact * up * (sg * (1.0 + gate * (1.0 - sg)))).astype(BF16)
            dup = (dact * (gate * sg)).astype(BF16)
            dh_ref[:, c0:c0 + FF_CHUNK] = dgate
            dh_ref[:, D_FF + c0:D_FF + c0 + FF_CHUNK] = dup
            du2 = du2 + _dot_nt(dgate, win_ref[:, c0:c0 + FF_CHUNK])
            du2 = du2 + _dot_nt(dup, win_ref[:, D_FF + c0:D_FF + c0 + FF_CHUNK])
        dx1_n, dgffn = _rms_bwd(du2, n2, r2, g_ffn_v)
        dgffn_ref[...] += dgffn
        dx1_ref[...] = dx2 + dx1_n

    tile = lambda w: pl.BlockSpec((TM, w), lambda i: (i, 0))
    acc = lambda w: pl.BlockSpec((1, w), lambda i: (0, 0))
    return pl.pallas_call(
        body, name="ffn_fwd_bwd", grid=(TOK // TM,),
        in_specs=[tile(D_MODEL), tile(D_MODEL), _resident((1, D_MODEL)), _resident((1, D_MODEL)),
                  _resident((D_MODEL, 2 * D_FF)), _resident((D_FF, D_MODEL))],
        out_specs=[acc(LANES), tile(D_MODEL), tile(D_MODEL), tile(D_FF), tile(2 * D_FF), tile(D_MODEL),
                   acc(D_MODEL), acc(D_MODEL)],
        out_shape=[jax.ShapeDtypeStruct((1, LANES), F32),
                   jax.ShapeDtypeStruct((TOK, D_MODEL), F32),
                   jax.ShapeDtypeStruct((TOK, D_MODEL), BF16),
                   jax.ShapeDtypeStruct((TOK, D_FF), BF16),
                   jax.ShapeDtypeStruct((TOK, 2 * D_FF), BF16),
                   jax.ShapeDtypeStruct((TOK, D_MODEL), BF16),
                   jax.ShapeDtypeStruct((1, D_MODEL), F32),
                   jax.ShapeDtypeStruct((1, D_MODEL), F32)],
        scratch_shapes=[pltpu.VMEM((TM, 2 * D_FF), F32)],
        compiler_params=_params(("arbitrary",)),
    )(x1, target, g_ffn, g_fin, w_ffn_in, w_ffn_out)


def _mix_bwd(dx1, o_sb, o_dl, gates, w_sb_up, w_dil_up, w_out):
    def body(dx1_ref, osb_ref, odl_ref, gate_ref, wsb_ref, wdl_ref, wout_ref,
             dgate_ref, dysb_ref, dydl_ref, dosb_ref, dodl_ref, dsum_ref):
        dmerged = _dot_nt(dx1_ref[...].astype(BF16), wout_ref[...])
        o_dl = odl_ref[...]
        y_sb = _dot(osb_ref[...], wsb_ref[...])
        y_dl = _dot(o_dl.astype(BF16), wdl_ref[...])
        s_sb = _sigmoid(gate_ref[:, :D_MODEL])
        s_dl = _sigmoid(gate_ref[:, D_MODEL:])
        dgate_ref[:, :D_MODEL] = (dmerged * y_sb * (s_sb * (1.0 - s_sb))).astype(BF16)
        dgate_ref[:, D_MODEL:] = (dmerged * y_dl * (s_dl * (1.0 - s_dl))).astype(BF16)
        dy_sb = (dmerged * s_sb).astype(BF16)
        dy_dl = (dmerged * s_dl).astype(BF16)
        dysb_ref[...] = dy_sb
        dydl_ref[...] = dy_dl
        dosb_ref[...] = _dot_nt(dy_sb, wsb_ref[...]).astype(BF16)
        do_dl = _dot_nt(dy_dl, wdl_ref[...])
        dodl_ref[...] = do_dl
        row = lax.broadcasted_iota(jnp.int32, (DIL_OUT, DIL_OUT), 0) // HEAD_DIM
        col = lax.broadcasted_iota(jnp.int32, (DIL_OUT, DIL_OUT), 1) // HEAD_DIM
        same_head = (row == col).astype(BF16)
        hi, lo = _split_bf16(do_dl * o_dl)
        dsum_ref[...] = _dot(hi, same_head) + _dot(lo, same_head)

    tile = lambda w: pl.BlockSpec((TM, w), lambda i: (i, 0))
    return pl.pallas_call(
        body, name="mix_bwd", grid=(TOK // TM,),
        in_specs=[tile(D_MODEL), tile(SB_WIDTH), tile(DIL_OUT), tile(2 * D_MODEL),
                  _resident((SB_WIDTH, D_MODEL)), _resident((DIL_OUT, D_MODEL)),
                  _resident((D_MODEL, D_MODEL))],
        out_specs=[tile(2 * D_MODEL), tile(D_MODEL), tile(D_MODEL), tile(SB_WIDTH), tile(DIL_OUT),
                   tile(DIL_OUT)],
        out_shape=[jax.ShapeDtypeStruct((TOK, 2 * D_MODEL), BF16),
                   jax.ShapeDtypeStruct((TOK, D_MODEL), BF16),
                   jax.ShapeDtypeStruct((TOK, D_MODEL), BF16),
                   jax.ShapeDtypeStruct((TOK, SB_WIDTH), BF16),
                   jax.ShapeDtypeStruct((TOK, DIL_OUT), F32),
                   jax.ShapeDtypeStruct((TOK, DIL_OUT), F32)],
        compiler_params=_params(("parallel",)),
    )(dx1, o_sb, o_dl, gates, w_sb_up, w_dil_up, w_out)


def _proj_bwd(dproj, dx1, x, g, w_in):
    def body(dp_ref, dx1_ref, x_ref, g_ref, w_ref, dx_ref, dg_ref):
        @pl.when(pl.program_id(0) == 0)
        def _():
            dg_ref[...] = jnp.zeros_like(dg_ref)

        du = jnp.zeros((TM, D_MODEL), F32)
        for c0, w in _chunks(IN_WIDTH, 1024):
            du = du + _dot_nt(dp_ref[:, c0:c0 + w], w_ref[:, c0:c0 + w])
        g_v = g_ref[...]
        n, r, _ = _rms_fwd(x_ref[...], g_v)
        dx, dg = _rms_bwd(du, n, r, g_v)
        dg_ref[...] += dg
        dx_ref[...] = dx1_ref[...] + dx

    tile = lambda w: pl.BlockSpec((TM, w), lambda i: (i, 0))
    return pl.pallas_call(
        body, name="proj_bwd", grid=(TOK // TM,),
        in_specs=[tile(IN_WIDTH), tile(D_MODEL), tile(D_MODEL), _resident((1, D_MODEL)),
                  _resident((D_MODEL, IN_WIDTH))],
        out_specs=[tile(D_MODEL), pl.BlockSpec((1, D_MODEL), lambda i: (0, 0))],
        out_shape=[jax.ShapeDtypeStruct((TOK, D_MODEL), F32),
                   jax.ShapeDtypeStruct((1, D_MODEL), F32)],
        compiler_params=_params(("arbitrary",)),
    )(dproj, dx1, x, g, w_in)


def _atb(a, b, name, tm, tn, tk=512):
    m, n = a.shape[1], b.shape[1]
    nk = TOK // tk

    def body(a_ref, b_ref, o_ref):
        @pl.when(pl.program_id(2) == 0)
        def _():
            o_ref[...] = jnp.zeros_like(o_ref)

        o_ref[...] += _dot_tn(a_ref[...], b_ref[...])

    return pl.pallas_call(
        body, name=name, grid=(m // tm, n // tn, nk),
        in_specs=[pl.BlockSpec((tk, tm), lambda i, j, k: (k, i)),
                  pl.BlockSpec((tk, tn), lambda i, j, k: (k, j))],
        out_specs=pl.BlockSpec((tm, tn), lambda i, j, k: (i, j)),
        out_shape=jax.ShapeDtypeStruct((m, n), F32),
        compiler_params=_params(("parallel", "parallel", "arbitrary")),
    )(a, b)


SB_PAIRS = SB_WIDTH // LANES


def _sb_masks():
    row = lax.broadcasted_iota(jnp.int32, (BLK, BLK), 0)
    col = lax.broadcasted_iota(jnp.int32, (BLK, BLK), 1)
    return row, col, col < HEAD_DIM


def _two_heads(v, lane0):
    zero = jnp.zeros_like(v)
    return jnp.where(lane0, v, zero), jnp.where(lane0, zero, v)


SB_QBLK = 256
N_SB_STEPS = SEQ // SB_QBLK


SB_KCHUNK = 2 * BLK
SB_ROWS = 2 * SB_QBLK


def _log_keep(z):
    neg_z = -z
    return jnp.minimum(neg_z, 0.0) - jnp.log(1.0 + jnp.exp(jnp.minimum(z, neg_z)))


def _stack_heads(v, lane0):
    return jnp.concatenate(_two_heads(v, lane0), axis=0)


def _block_sums(v, tri):
    halves = (v[:, :BLK], v[:, BLK:])
    hi, lo = _split_bf16(jnp.concatenate(halves, axis=0))
    prod = _dot(jnp.concatenate([hi, lo], axis=0), tri)
    tri_sum = prod[:2 * SB_ROWS] + prod[2 * SB_ROWS:]
    sums = tuple(jnp.sum(h, axis=1, keepdims=True) for h in halves)
    return (tri_sum[:SB_ROWS], tri_sum[SB_ROWS:]), sums


def _sb_diag_mask():
    row = lax.broadcasted_iota(jnp.int32, (SB_ROWS, SB_KCHUNK), 0)
    col = lax.broadcasted_iota(jnp.int32, (SB_ROWS, SB_KCHUNK), 1)
    return col < jnp.where(row >= SB_QBLK, row - SB_QBLK, row)


def _sb_fwd(qkv):
    def body(q_ref, k_ref, v_ref, o_ref, tot_ref):
        i = pl.program_id(2)
        krow = lax.broadcasted_iota(jnp.int32, (BLK, BLK), 0)
        kcol = lax.broadcasted_iota(jnp.int32, (BLK, BLK), 1)
        later = (krow > kcol).astype(BF16)
        lane0 = lax.broadcasted_iota(jnp.int32, (SB_QBLK, LANES), 1) < HEAD_DIM
        q2 = _stack_heads(q_ref[0] * QK_SCALE, lane0)

        def chunk(c, carry, causal):
            acc, run = carry
            off = pl.multiple_of(c * SB_KCHUNK, SB_KCHUNK)
            z = _dot_nt(q2, k_ref[0, pl.ds(off, SB_KCHUNK), :])
            log_keep = _log_keep(z)
            if causal is not None:
                log_keep = jnp.where(causal, log_keep, 0.0)
            suffix, sums = _block_sums(log_keep, later)
            log_after = jnp.concatenate([suffix[0] + (run + sums[1]), suffix[1] + run], axis=1)
            a = jnp.exp(log_keep + z + log_after)
            if causal is not None:
                a = jnp.where(causal, a, 0.0)
            acc = acc + _dot(a.astype(BF16), v_ref[0, pl.ds(off, SB_KCHUNK), :])
            return acc, run + (sums[0] + sums[1])

        carry = chunk(i, (jnp.zeros((SB_ROWS, LANES), F32), jnp.zeros((SB_ROWS, 1), F32)), _sb_diag_mask())
        acc, run = lax.fori_loop(0, i, lambda t, c: chunk(i - 1 - t, c, None), carry)
        o_ref[0] = jnp.where(lane0, acc[:SB_QBLK], acc[SB_QBLK:]).astype(BF16)
        tot_ref[0] = jnp.where(lane0, run[:SB_QBLK], run[SB_QBLK:])

    blk = pl.BlockSpec((1, SB_QBLK, LANES), lambda b, h, i: (b, i, h))
    return pl.pallas_call(
        body, name="sb_fwd", grid=(B_LOC, SB_PAIRS, N_SB_STEPS),
        in_specs=[blk,
                  pl.BlockSpec((1, SEQ, LANES), lambda b, h, i: (b, 0, SB_PAIRS + h)),
                  pl.BlockSpec((1, SEQ, LANES), lambda b, h, i: (b, 0, 2 * SB_PAIRS + h))],
        out_specs=[blk, blk],
        out_shape=[jax.ShapeDtypeStruct((B_LOC, SEQ, SB_WIDTH), BF16),
                   jax.ShapeDtypeStruct((B_LOC, SEQ, SB_WIDTH), F32)],
        compiler_params=_params(("parallel", "parallel", "arbitrary")),
    )(qkv, qkv, qkv)


def _sb_bwd(qkv, d_o, tot, token):
    def body(q_ref, k_ref, v_ref, do_ref, tot_ref, token_ref, dq_ref, dk_ref, dv_ref, dk_acc, dv_acc):
        i = pl.program_id(2)
        krow = lax.broadcasted_iota(jnp.int32, (BLK, BLK), 0)
        kcol = lax.broadcasted_iota(jnp.int32, (BLK, BLK), 1)
        upto = (krow <= kcol).astype(BF16)
        earlier = (krow < kcol).astype(BF16)
        lane0 = lax.broadcasted_iota(jnp.int32, (SB_QBLK, LANES), 1) < HEAD_DIM
        q2 = _stack_heads(q_ref[0] * QK_SCALE, lane0)
        do2 = _stack_heads(do_ref[0], lane0)
        tot = tot_ref[0]
        tot2 = jnp.concatenate([_head_col(tot, lane0), _head_col(tot, jnp.logical_not(lane0))], axis=0)

        @pl.when(i == 0)
        def _():
            dk_acc[...] = jnp.zeros_like(dk_acc)
            dv_acc[...] = jnp.zeros_like(dv_acc)

        def chunk(c, carry, causal):
            dq, pre_keep, pre_e = carry
            off = pl.multiple_of(c * SB_KCHUNK, SB_KCHUNK)
            k_c = k_ref[0, pl.ds(off, SB_KCHUNK), :]
            v_c = v_ref[0, pl.ds(off, SB_KCHUNK), :]
            z = _dot_nt(q2, k_c)
            d_a = _dot_nt(do2, v_c)
            log_keep = _log_keep(z)
            if causal is not None:
                log_keep = jnp.where(causal, log_keep, 0.0)
            log_beta = log_keep + z
            prefix, sums = _block_sums(log_keep, upto)
            inclusive = jnp.concatenate([prefix[0], prefix[1] + sums[0]], axis=1)
            a = jnp.exp(log_beta + ((tot2 - pre_keep) - inclusive))
            if causal is not None:
                a = jnp.where(causal, a, 0.0)
            e = d_a * a
            e_prefix, e_sums = _block_sums(e, earlier)
            before = jnp.concatenate([e_prefix[0] + pre_e, e_prefix[1] + (pre_e + e_sums[0])], axis=1)
            dz = e - (e + before) * jnp.exp(log_beta)
            if causal is not None:
                dz = jnp.where(causal, dz, 0.0)
            dz = dz.astype(BF16)
            dq = dq + _dot(dz, k_c)
            dk_acc[pl.ds(off, SB_KCHUNK), :] += _dot_tn(dz, q2)
            dv_acc[pl.ds(off, SB_KCHUNK), :] += _dot_tn(a.astype(BF16), do2)
            return dq, pre_keep + (sums[0] + sums[1]), pre_e + (e_sums[0] + e_sums[1])

        zero_col = jnp.zeros((SB_ROWS, 1), F32)
        carry = lax.fori_loop(0, i, lambda t, c: chunk(t, c, None),
                              (jnp.zeros((SB_ROWS, LANES), F32), zero_col, zero_col))
        dq, _, _ = chunk(i, carry, _sb_diag_mask())
        dq_ref[0] = (jnp.where(lane0, dq[:SB_QBLK], dq[SB_QBLK:]) * QK_SCALE).astype(BF16)

        @pl.when(i == N_SB_STEPS - 1)
        def _():
            dk_ref[0] = dk_acc[...].astype(BF16)
            dv_ref[0] = dv_acc[...].astype(BF16)

    blk = pl.BlockSpec((1, SB_QBLK, LANES), lambda b, h, i: (b, i, h))
    whole = lambda c: pl.BlockSpec((1, SEQ, LANES), lambda b, h, i: (b, 0, c * SB_PAIRS + h))
    out = jax.ShapeDtypeStruct((B_LOC, SEQ, SB_WIDTH), BF16)
    return pl.pallas_call(
        body, name="sb_bwd", grid=(B_LOC, SB_PAIRS, N_SB_STEPS),
        in_specs=[blk, whole(1), whole(2), blk, blk, pl.BlockSpec((8, LANES), lambda b, h, i: (0, 0))],
        out_specs=[blk, whole(0), whole(0)],
        out_shape=[out, out, out],
        scratch_shapes=[pltpu.VMEM((SEQ, LANES), F32), pltpu.VMEM((SEQ, LANES), F32)],
        compiler_params=_params(("parallel", "parallel", "arbitrary")),
    )(qkv, qkv, qkv, d_o, tot, token)


def _dil_scores(q_h, k_cur, k_prev, slope, dilation, has_prev, row, col):
    dist = (row - col).astype(F32) * float(dilation)
    s_cur = _dot_nt(q_h, k_cur) * QK_SCALE - slope * dist
    s_prev = _dot_nt(q_h, k_prev) * QK_SCALE - slope * (dist + float(BLK * dilation))
    s_cur = jnp.where(col <= row, s_cur, NEG_BIG)
    s_prev = jnp.where(jnp.logical_and(col >= row, has_prev), s_prev, NEG_BIG)
    return s_cur, s_prev


def _dil_slopes(group):
    pair = pl.program_id(1) % 2
    first = float(4 * group + 1) + 2.0 * pair.astype(F32)
    coef = -ALIBI_MAX_BIAS / DIL_HEADS * math.log(2.0)
    return jnp.exp(coef * first), jnp.exp(coef * (first + 1.0))


def _head_col(v, lane_mask):
    return jnp.max(jnp.where(lane_mask, v, NEG_BIG), axis=1, keepdims=True)


def _dil_fwd(q, k, v, group):
    dilation = DIL_PAIRS[group][1]
    length = SEQ // dilation
    nblk = length // BLK

    def body(q_ref, k_ref, v_ref, o_ref, lse_ref):
        row, col, lane0 = _sb_masks()
        slopes = _dil_slopes(group)

        def step(n, _):
            off = pl.multiple_of(n * BLK, BLK)
            off_prev = pl.multiple_of(jnp.maximum(n - 1, 0) * BLK, BLK)
            q_h = _two_heads(q_ref[0, pl.ds(off, BLK), :], lane0)
            k_cur = k_ref[0, pl.ds(off, BLK), :]
            v_cur = v_ref[0, pl.ds(off, BLK), :]
            k_prev = k_ref[0, pl.ds(off_prev, BLK), :]
            v_prev = v_ref[0, pl.ds(off_prev, BLK), :]
            outs, lses = [], []
            for h in range(2):
                s_cur, s_prev = _dil_scores(q_h[h], k_cur, k_prev, slopes[h], dilation, n > 0, row, col)
                m = jnp.maximum(jnp.max(s_cur, axis=1, keepdims=True), jnp.max(s_prev, axis=1, keepdims=True))
                p_cur = jnp.exp(s_cur - m)
                p_prev = jnp.exp(s_prev - m)
                den = jnp.sum(p_cur, axis=1, keepdims=True) + jnp.sum(p_prev, axis=1, keepdims=True)
                num = _dot(p_cur.astype(BF16), v_cur) + _dot(p_prev.astype(BF16), v_prev)
                outs.append(num / den)
                lses.append(m + jnp.log(den))
            o_ref[0, pl.ds(off, BLK), :] = jnp.where(lane0, outs[0], outs[1])
            lse_ref[0, pl.ds(off, BLK), :] = jnp.where(lane0, lses[0], lses[1])
            return 0

        lax.fori_loop(0, nblk, step, 0)

    spec = pl.BlockSpec((1, length, LANES), lambda b, c: (b, 0, c))
    out = jax.ShapeDtypeStruct((B_LOC, length, dilation * DIL_OUT), F32)
    return pl.pallas_call(
        body, name=f"dil_fwd_{group}", grid=(B_LOC, 2 * dilation),
        in_specs=[spec, spec, spec], out_specs=[spec, spec], out_shape=[out, out],
        compiler_params=_params(("parallel", "parallel")),
    )(q, k, v)


def _dil_bwd(q, k, v, d_o, lse, dsum, group):
    dilation = DIL_PAIRS[group][1]
    length = SEQ // dilation
    nblk = length // BLK

    def body(q_ref, k_ref, v_ref, do_ref, lse_ref, dsum_ref, dq_ref, dk_ref, dv_ref, dk_acc, dv_acc):
        row, col, lane0 = _sb_masks()
        lane1 = jnp.logical_not(lane0)
        slopes = _dil_slopes(group)
        dk_acc[...] = jnp.zeros_like(dk_acc)
        dv_acc[...] = jnp.zeros_like(dv_acc)

        def step(n, _):
            off = pl.multiple_of(n * BLK, BLK)
            off_prev = pl.multiple_of(jnp.maximum(n - 1, 0) * BLK, BLK)
            q_h = _two_heads(q_ref[0, pl.ds(off, BLK), :], lane0)
            do_h = _two_heads(do_ref[0, pl.ds(off, BLK), :].astype(BF16), lane0)
            k_cur = k_ref[0, pl.ds(off, BLK), :]
            v_cur = v_ref[0, pl.ds(off, BLK), :]
            k_prev = k_ref[0, pl.ds(off_prev, BLK), :]
            v_prev = v_ref[0, pl.ds(off_prev, BLK), :]
            kc_h = _two_heads(k_cur, lane0)
            kp_h = _two_heads(k_prev, lane0)
            lse_blk = lse_ref[0, pl.ds(off, BLK), :]
            dsum_blk = dsum_ref[0, pl.ds(off, BLK), :]
            dq = jnp.zeros((BLK, LANES), F32)
            dk_c = jnp.zeros((BLK, LANES), F32)
            dk_p = jnp.zeros((BLK, LANES), F32)
            dv_c = jnp.zeros((BLK, LANES), F32)
            dv_p = jnp.zeros((BLK, LANES), F32)
            for h, lanes in enumerate((lane0, lane1)):
                s_cur, s_prev = _dil_scores(q_h[h], k_cur, k_prev, slopes[h], dilation, n > 0, row, col)
                lse_h = _head_col(lse_blk, lanes)
                dsum_h = _head_col(dsum_blk, lanes)
                p_cur = jnp.exp(s_cur - lse_h)
                p_prev = jnp.exp(s_prev - lse_h)
                ds_cur = (p_cur * (_dot_nt(do_h[h], v_cur) - dsum_h) * QK_SCALE).astype(BF16)
                ds_prev = (p_prev * (_dot_nt(do_h[h], v_prev) - dsum_h) * QK_SCALE).astype(BF16)
                dq = dq + _dot(ds_cur, kc_h[h]) + _dot(ds_prev, kp_h[h])
                dk_c = dk_c + _dot_tn(ds_cur, q_h[h])
                dk_p = dk_p + _dot_tn(ds_prev, q_h[h])
                dv_c = dv_c + _dot_tn(p_cur.astype(BF16), do_h[h])
                dv_p = dv_p + _dot_tn(p_prev.astype(BF16), do_h[h])
            dq_ref[0, pl.ds(off, BLK), :] = dq.astype(BF16)
            dk_acc[pl.ds(off, BLK), :] += dk_c
            dv_acc[pl.ds(off, BLK), :] += dv_c
            dk_acc[pl.ds(off_prev, BLK), :] += dk_p
            dv_acc[pl.ds(off_prev, BLK), :] += dv_p
            return 0

        lax.fori_loop(0, nblk, step, 0)
        dk_ref[0] = dk_acc[...].astype(BF16)
        dv_ref[0] = dv_acc[...].astype(BF16)

    spec = pl.BlockSpec((1, length, LANES), lambda b, c: (b, 0, c))
    out = jax.ShapeDtypeStruct((B_LOC, length, dilation * DIL_OUT), BF16)
    return pl.pallas_call(
        body, name=f"dil_bwd_{group}", grid=(B_LOC, 2 * dilation),
        in_specs=[spec] * 6, out_specs=[spec] * 3, out_shape=[out] * 3,
        scratch_shapes=[pltpu.VMEM((length, LANES), F32), pltpu.VMEM((length, LANES), F32)],
        compiler_params=_params(("parallel", "parallel")),
    )(q, k, v, d_o, lse, dsum)


def _dil_combine(outs, lses):
    def body(o0, o1, o2, l0, l1, l2, o_ref, lse_ref):
        ls = (l0[...], l1[...], l2[...])
        m = jnp.maximum(jnp.maximum(ls[0], ls[1]), ls[2])
        w = [jnp.exp(l - m) for l in ls]
        den = w[0] + w[1] + w[2]
        o_ref[...] = (w[0] * o0[...] + w[1] * o1[...] + w[2] * o2[...]) / den
        lse_ref[...] = m + jnp.log(den)

    tile = pl.BlockSpec((512, DIL_OUT), lambda i: (i, 0))
    out = jax.ShapeDtypeStruct((TOK, DIL_OUT), F32)
    return pl.pallas_call(
        body, name="dil_combine", grid=(TOK // 512,),
        in_specs=[tile] * 6, out_specs=[tile, tile], out_shape=[out, out],
        compiler_params=_params(("parallel",)),
    )(*outs, *lses)


def _to_residues(t, dilation):
    return t.reshape(B_LOC, SEQ // dilation, dilation * DIL_OUT)


def _peers():
    x, y, c = lax.axis_index("x"), lax.axis_index("y"), lax.axis_index("c")
    me = 4 * x + 2 * y + c
    peers = []
    for mask in range(1, N_DEV):
        px = 1 - x if mask & 4 else x
        py = 1 - y if mask & 2 else y
        pc = 1 - c if mask & 1 else c
        peers.append(((px, py, pc), 4 * px + 2 * py + pc))
    return me, peers


def _all_gather(packed):
    rows = packed.shape[0]

    def body(src_ref, out_ref, send_sems, recv_sems, local_sem):
        me, peers = _peers()
        mine = pltpu.make_async_copy(src_ref, out_ref.at[me], local_sem)
        mine.start()
        sends = []
        for k, (peer, _) in enumerate(peers):
            cp = pltpu.make_async_remote_copy(
                src_ref=src_ref, dst_ref=out_ref.at[me], send_sem=send_sems.at[k], recv_sem=recv_sems.at[k],
                device_id=peer, device_id_type=pl.DeviceIdType.MESH)
            cp.start()
            sends.append(cp)
        for k, (peer, peer_idx) in enumerate(peers):
            pltpu.make_async_remote_copy(
                src_ref=src_ref, dst_ref=out_ref.at[peer_idx], send_sem=send_sems.at[k], recv_sem=recv_sems.at[k],
                device_id=peer, device_id_type=pl.DeviceIdType.MESH).wait_recv()
        for cp in sends:
            cp.wait_send()
        mine.wait()

    return pl.pallas_call(
        body, name="all_gather_weights",
        in_specs=[pl.BlockSpec(memory_space=pl.ANY)],
        out_specs=pl.BlockSpec(memory_space=pl.ANY),
        out_shape=jax.ShapeDtypeStruct((N_DEV, rows, LANES), packed.dtype),
        scratch_shapes=[pltpu.SemaphoreType.DMA((N_DEV - 1,)), pltpu.SemaphoreType.DMA((N_DEV - 1,)),
                        pltpu.SemaphoreType.DMA],
    )(packed)


def _exchange(blocks, gains):
    rows = blocks.shape[1]

    def body(blk_ref, gain_ref, blk_out, gain_out, send_sems, recv_sems, local_sems):
        me, peers = _peers()
        own = [pltpu.make_async_copy(blk_ref.at[me], blk_out.at[me], local_sems.at[0]),
               pltpu.make_async_copy(gain_ref, gain_out.at[me], local_sems.at[1])]
        for cp in own:
            cp.start()
        sends = []
        for k, (peer, peer_idx) in enumerate(peers):
            for a, (src, dst) in enumerate(((blk_ref.at[peer_idx], blk_out.at[me]), (gain_ref, gain_out.at[me]))):
                cp = pltpu.make_async_remote_copy(
                    src_ref=src, dst_ref=dst, send_sem=send_sems.at[a, k], recv_sem=recv_sems.at[a, k],
                    device_id=peer, device_id_type=pl.DeviceIdType.MESH)
                cp.start()
                sends.append(cp)
        for k, (peer, peer_idx) in enumerate(peers):
            for a, (src, dst) in enumerate(((blk_ref.at[peer_idx], blk_out.at[peer_idx]),
                                            (gain_ref, gain_out.at[peer_idx]))):
                pltpu.make_async_remote_copy(
                    src_ref=src, dst_ref=dst, send_sem=send_sems.at[a, k], recv_sem=recv_sems.at[a, k],
                    device_id=peer, device_id_type=pl.DeviceIdType.MESH).wait_recv()
        for cp in sends:
            cp.wait_send()
        for cp in own:
            cp.wait()

    return pl.pallas_call(
        body, name="exchange_grads",
        in_specs=[pl.BlockSpec(memory_space=pl.ANY), pl.BlockSpec(memory_space=pl.ANY)],
        out_specs=[pl.BlockSpec(memory_space=pl.ANY), pl.BlockSpec(memory_space=pl.ANY)],
        out_shape=[jax.ShapeDtypeStruct((N_DEV, rows, LANES), blocks.dtype),
                   jax.ShapeDtypeStruct((N_DEV,) + gains.shape, gains.dtype)],
        scratch_shapes=[pltpu.SemaphoreType.DMA((2, N_DEV - 1)), pltpu.SemaphoreType.DMA((2, N_DEV - 1)),
                        pltpu.SemaphoreType.DMA((2,))],
    )(blocks, gains)


_HBM = pl.BlockSpec(memory_space=pltpu.HBM)
_SEM = pl.BlockSpec(memory_space=pltpu.SEMAPHORE)
_ANY = pl.BlockSpec(memory_space=pl.ANY)
_EFFECT = pltpu.SideEffectType.DATAFLOW_SIDE_EFFECTING


def _peer_copy(src_ref, land_ref, send_sems, recv_sems, k, peer, src_idx, slot):
    return pltpu.make_async_remote_copy(
        src_ref=src_ref if src_idx is None else src_ref.at[src_idx], dst_ref=land_ref.at[slot],
        send_sem=send_sems.at[k], recv_sem=recv_sems.at[k], device_id=peer, device_id_type=pl.DeviceIdType.MESH)


def _send_start(src, name, scatter, after):
    rows = src.shape[-2]

    def body(src_ref, land_ref, after_ref, send_sems, recv_sems, src_thru, land_thru, token):
        me, peers = _peers()
        for k, (peer, peer_idx) in enumerate(peers):
            _peer_copy(src_ref, land_ref, send_sems, recv_sems, k, peer, peer_idx if scatter else None, me).start()
        token[...] = jnp.zeros_like(token)

    land = lax.empty((N_DEV, rows, LANES), src.dtype)
    return pl.pallas_call(
        body, name=name,
        out_shape=(pltpu.SemaphoreType.DMA((N_DEV - 1,)), pltpu.SemaphoreType.DMA((N_DEV - 1,)),
                   pltpu.HBM(src.shape, src.dtype), pltpu.HBM(land.shape, land.dtype),
                   jax.ShapeDtypeStruct((8, LANES), F32)),
        in_specs=(_HBM, _HBM, _ANY),
        out_specs=(_SEM, _SEM, _HBM, _HBM, pl.BlockSpec(memory_space=pltpu.VMEM)),
        input_output_aliases={0: 2, 1: 3},
        compiler_params=pltpu.CompilerParams(has_side_effects=_EFFECT),
    )(pltpu.with_memory_space_constraint(src, pltpu.HBM), pltpu.with_memory_space_constraint(land, pltpu.HBM), after)


def _send_wait(handles, name, scatter, afters):
    send_sems, recv_sems, src_thru, land_thru = handles

    def body(src_ref, land_ref, send_sems, recv_sems, *rest):
        me, peers = _peers()
        for k, (peer, peer_idx) in enumerate(peers):
            cp = _peer_copy(src_ref, land_ref, send_sems, recv_sems, k, peer, peer_idx if scatter else None, peer_idx)
            cp.wait_send()
            cp.wait_recv()

    return pl.pallas_call(
        body, name=name,
        out_shape=(pltpu.HBM(src_thru.shape, src_thru.dtype), pltpu.HBM(land_thru.shape, land_thru.dtype)),
        in_specs=(_HBM, _HBM, _SEM, _SEM) + (_ANY,) * len(afters),
        out_specs=(_HBM, _HBM),
        input_output_aliases={0: 0, 1: 1},
        compiler_params=pltpu.CompilerParams(has_side_effects=_EFFECT),
    )(src_thru, land_thru, send_sems, recv_sems, *afters)


def _sum_slots(slots, name, tile_rows):
    _, rows, cols = slots.shape

    def body(s_ref, o_ref):
        acc = s_ref[0].astype(F32)
        for j in range(1, N_DEV):
            acc = acc + s_ref[j].astype(F32)
        o_ref[...] = acc

    return pl.pallas_call(
        body, name=name, grid=(rows // tile_rows,),
        in_specs=[pl.BlockSpec((N_DEV, tile_rows, cols), lambda i: (0, i, 0))],
        out_specs=pl.BlockSpec((tile_rows, cols), lambda i: (i, 0)),
        out_shape=jax.ShapeDtypeStruct((rows, cols), F32),
        compiler_params=_params(("parallel",)),
    )(slots)


def _adamw(w, g, m, v, name):
    rows, cols = w.shape
    tile_rows = max(t for t in range(8, 257, 8) if rows % t == 0) if rows % 8 == 0 else rows
    c1 = 1.0 - ADAM_B1 ** ADAM_STEP
    c2 = 1.0 - ADAM_B2 ** ADAM_STEP

    def body(w_ref, g_ref, m_ref, v_ref, d_ref, nm_ref, nv_ref):
        g_v = g_ref[...]
        m_new = ADAM_B1 * m_ref[...] + (1.0 - ADAM_B1) * g_v
        v_new = ADAM_B2 * v_ref[...] + (1.0 - ADAM_B2) * (g_v * g_v)
        nm_ref[...] = m_new
        nv_ref[...] = v_new
        d_ref[...] = -ADAM_LR * ((m_new / c1) / (jnp.sqrt(v_new / c2) + ADAM_EPS) + ADAM_WD * w_ref[...])

    tile = pl.BlockSpec((tile_rows, cols), lambda i: (i, 0))
    out = jax.ShapeDtypeStruct((rows, cols), F32)
    return pl.pallas_call(
        body, name=name, grid=(rows // tile_rows,),
        in_specs=[tile] * 4, out_specs=[tile] * 3, out_shape=[out] * 3,
        compiler_params=_params(("parallel",)),
    )(w, g, m, v)


GROUP_FIRST = ("w_in",)
GROUP_REST = ("w_sb_up", "w_dil_up", "w_out", "w_ffn_in", "w_ffn_out")


def _group_shapes(names):
    return [(name, shape) for name, shape in SHARD_SHAPES if name in names]


def _pack_shards(shards, names):
    return jnp.concatenate([shards[name].reshape(-1, LANES) for name, _ in _group_shapes(names)], axis=0)


def _unpack_shards(packed, names):
    out, r0 = {}, 0
    for name, (r, c) in _group_shapes(names):
        n = r * c // LANES
        out[name] = packed[r0:r0 + n].reshape(r, c)
        r0 += n
    return out


def _unpack_full(gathered, names):
    out, r0 = {}, 0
    for name, (r, c) in _group_shapes(names):
        n = r * c // LANES
        blk = gathered[:, r0:r0 + n].reshape(N_DEV, r, c)
        if name in COL_SHARDED:
            out[name] = blk.transpose(1, 0, 2).reshape(r, N_DEV * c)
        else:
            out[name] = blk.reshape(N_DEV * r, c)
        r0 += n
    return out


def _pack_full(full, names):
    parts = []
    for name, (r, c) in _group_shapes(names):
        t = full[name]
        if name in COL_SHARDED:
            t = t.reshape(r, N_DEV, c).transpose(1, 0, 2)
        parts.append(t.reshape(N_DEV, r * c // LANES, LANES))
    return jnp.concatenate(parts, axis=1)


def _local_step(x, target, g_mix, g_ffn, g_fin, w_in, take_rest, give_rest):
    w = {"w_in": w_in}
    qkv, gates, u = _norm_proj(x, g_mix, w["w_in"])
    qkv3 = qkv.reshape(B_LOC, SEQ, QKV_WIDTH)
    o_sb, sb_tot = _sb_fwd(qkv3)
    o_sb = o_sb.reshape(TOK, SB_WIDTH)

    dil_in, dil_o, dil_lse = [], [], []
    for grp, (_, dilation) in enumerate(DIL_PAIRS):
        parts = []
        for part in range(3):
            c0 = 3 * SB_WIDTH + part * DIL_WIDTH + grp * DIL_OUT
            parts.append(_to_residues(qkv[:, c0:c0 + DIL_OUT], dilation))
        dil_in.append(parts)
        o_g, lse_g = _dil_fwd(*parts, grp)
        dil_o.append(o_g.reshape(TOK, DIL_OUT))
        dil_lse.append(lse_g.reshape(TOK, DIL_OUT))
    o_dl, lse = _dil_combine(dil_o, dil_lse)

    w.update(take_rest(o_sb, o_dl))
    x1, merged = _mix_out(x, o_sb, o_dl, gates, w["w_sb_up"], w["w_dil_up"], w["w_out"])
    loss, dx1, u2, act, dh, dx2, dg_fin, dg_ffn = _ffn_fwd_bwd(x1, target, g_ffn, g_fin, w["w_ffn_in"], w["w_ffn_out"])
    dgates, dy_sb, dy_dl, do_sb, do_dl, dsum = _mix_bwd(dx1, o_sb, o_dl, gates, w["w_sb_up"], w["w_dil_up"], w["w_out"])
    token = give_rest({
        "w_sb_up": _atb(o_sb, dy_sb, "grad_w_sb_up", SB_WIDTH, D_MODEL),
        "w_dil_up": _atb(o_dl.astype(BF16), dy_dl, "grad_w_dil_up", DIL_OUT, D_MODEL),
        "w_out": _atb(merged, dx1.astype(BF16), "grad_w_out", D_MODEL, D_MODEL),
        "w_ffn_in": _atb(u2, dh, "grad_w_ffn_in", D_MODEL, D_FF),
        "w_ffn_out": _atb(act, dx2, "grad_w_ffn_out", D_FF // 2, D_MODEL),
    })

    dq_sb, dk_sb, dv_sb = _sb_bwd(qkv3, do_sb.reshape(B_LOC, SEQ, SB_WIDTH), sb_tot, token)
    dq_dl, dk_dl, dv_dl = [], [], []
    for grp, (_, dilation) in enumerate(DIL_PAIRS):
        dq, dk, dv = _dil_bwd(*dil_in[grp], _to_residues(do_dl, dilation), _to_residues(lse, dilation),
                              _to_residues(dsum, dilation), grp)
        dq_dl.append(dq.reshape(TOK, DIL_OUT))
        dk_dl.append(dk.reshape(TOK, DIL_OUT))
        dv_dl.append(dv.reshape(TOK, DIL_OUT))
    flat = lambda t: t.reshape(TOK, SB_WIDTH)
    dproj = jnp.concatenate([flat(dq_sb), flat(dk_sb), flat(dv_sb), *dq_dl, *dk_dl, *dv_dl, dgates], axis=1)

    grad_x, dg_mix = _proj_bwd(dproj, dx1, x, g_mix, w["w_in"])
    grad_w_in = _atb(u, dproj, "grad_w_in", D_MODEL, IN_WIDTH // 2)
    gain_grads = jnp.concatenate([dg_mix, dg_ffn, dg_fin], axis=0)
    return loss, grad_x, grad_w_in, gain_grads


def kernel(x, norm_mix_g, w_in, w_sb_up, w_dil_up, w_out, norm_ffn_g, w_ffn_in, w_ffn_out, norm_final_g, loss_target, m_norm_mix_g, m_w_in, m_w_sb_up, m_w_dil_up, m_w_out, m_norm_ffn_g, m_w_ffn_in, m_w_ffn_out, m_norm_final_g, v_norm_mix_g, v_w_in, v_w_sb_up, v_w_dil_up, v_w_out, v_norm_ffn_g, v_w_ffn_in, v_w_ffn_out, v_norm_final_g):
    mats = {"w_in": w_in, "w_sb_up": w_sb_up, "w_dil_up": w_dil_up, "w_out": w_out,
            "w_ffn_in": w_ffn_in, "w_ffn_out": w_ffn_out}
    moments_m = {"w_in": m_w_in, "w_sb_up": m_w_sb_up, "w_dil_up": m_w_dil_up, "w_out": m_w_out,
                 "w_ffn_in": m_w_ffn_in, "w_ffn_out": m_w_ffn_out}
    moments_v = {"w_in": v_w_in, "w_sb_up": v_w_sb_up, "w_dil_up": v_w_dil_up, "w_out": v_w_out,
                 "w_ffn_in": v_w_ffn_in, "w_ffn_out": v_w_ffn_out}
    shards = {name: mats[name][0].astype(BF16) for name, _ in SHARD_SHAPES}
    me = 4 * lax.axis_index("x") + 2 * lax.axis_index("y") + lax.axis_index("c")

    def own_slot(landed, block):
        return lax.dynamic_update_slice(landed, block[None], (me, 0, 0))

    gathered_first = _all_gather(_pack_shards(shards, GROUP_FIRST))
    rest_handles = _send_start(_pack_shards(shards, GROUP_REST), "gather_rest_start", False, gathered_first)

    def take_rest(o_sb, o_dl):
        sent, landed = _send_wait(rest_handles[:4], "gather_rest_wait", False, (o_sb, o_dl))
        return _unpack_full(own_slot(landed, sent), GROUP_REST)

    grad_handles = []

    def give_rest(grads):
        handles = _send_start(_pack_full(grads, GROUP_REST).astype(BF16), "grads_rest_start", True, grads["w_out"])
        grad_handles.append(handles[:4])
        return handles[4]

    g_fin = norm_final_g.reshape(1, D_MODEL)
    loss, grad_x, grad_w_in, gain_grads = _local_step(
        x.reshape(TOK, D_MODEL), loss_target.reshape(TOK, D_MODEL), norm_mix_g + rest_handles[4][:1, :1],
        norm_ffn_g, g_fin, _unpack_full(gathered_first, GROUP_FIRST)["w_in"], take_rest, give_rest)

    sent, landed = _send_wait(grad_handles[0], "grads_rest_wait", True, (grad_w_in,))
    rest_slots = own_slot(landed, lax.dynamic_slice(sent, (me, 0, 0), (1,) + sent.shape[1:])[0])
    gain_rows = jnp.concatenate([gain_grads, jnp.zeros((8 - 3, D_MODEL), F32)], axis=0)
    first_slots, gain_slots = _exchange(_pack_full({"w_in": grad_w_in}, GROUP_FIRST).astype(BF16), gain_rows)
    g_shards = {**_unpack_shards(_sum_slots(first_slots, "sum_grad_w_in", 368), GROUP_FIRST),
                **_unpack_shards(_sum_slots(rest_slots, "sum_grad_rest", 512), GROUP_REST)}
    g_gains = _sum_slots(gain_slots, "sum_gain_grads", 8)

    out_g, out_d, out_m, out_v = {}, {}, {}, {}
    for name, g in g_shards.items():
        d, nm, nv = _adamw(mats[name][0], g, moments_m[name][0], moments_v[name][0], "adamw_" + name)
        out_g[name], out_d[name], out_m[name], out_v[name] = g[None], d[None], nm[None], nv[None]
    gain_w = jnp.concatenate([norm_mix_g, norm_ffn_g, g_fin], axis=0)
    gain_m = jnp.concatenate([m_norm_mix_g, m_norm_ffn_g, m_norm_final_g.reshape(1, D_MODEL)], axis=0)
    gain_v = jnp.concatenate([v_norm_mix_g, v_norm_ffn_g, v_norm_final_g.reshape(1, D_MODEL)], axis=0)
    gd, gm, gv = _adamw(gain_w, g_gains[:3], gain_m, gain_v, "adamw_gains")
    for idx, name in enumerate(("norm_mix_g", "norm_ffn_g", "norm_final_g")):
        shape = (D_MODEL,) if name == "norm_final_g" else (1, D_MODEL)
        out_g[name] = g_gains[idx].reshape(shape)
        out_d[name], out_m[name], out_v[name] = gd[idx].reshape(shape), gm[idx].reshape(shape), gv[idx].reshape(shape)

    order = ("norm_mix_g", "w_in", "w_sb_up", "w_dil_up", "w_out", "norm_ffn_g", "w_ffn_in", "w_ffn_out",
             "norm_final_g")
    total_loss = lax.psum(loss[0, 0], ("x", "y", "c"))
    return (total_loss, grad_x.reshape(B_LOC, SEQ, D_MODEL),
            *[out_g[n] for n in order], *[out_d[n] for n in order],
            *[out_m[n] for n in order], *[out_v[n] for n in order])
```

```python
import math

import jax
import jax.numpy as jnp
from jax import lax
from jax.experimental import pallas as pl
from jax.experimental.pallas import tpu as pltpu

F32 = jnp.float32
BF16 = jnp.bfloat16

N_DEV = 8
D_MODEL = 1024
SEQ = 2048
B_LOC = 2
TOK = B_LOC * SEQ
HEAD_DIM = 64
SB_WIDTH = 512
DIL_WIDTH = 768
DIL_OUT = 256
QKV_WIDTH = 3 * SB_WIDTH + 3 * DIL_WIDTH
IN_WIDTH = QKV_WIDTH + 2 * D_MODEL
D_FF = 2816
DIL_PAIRS = ((128, 1), (512, 4), (2048, 16))
DIL_HEADS = 12
RMS_EPS = 1e-6
ALIBI_MAX_BIAS = 8.0
QK_SCALE = 1.0 / math.sqrt(HEAD_DIM)
BLK = 128
LANES = 128
NEG_BIG = -1e30

ADAM_LR = 0.001
ADAM_B1 = 0.9
ADAM_B2 = 0.999
ADAM_EPS = 1e-08
ADAM_WD = 0.01
ADAM_STEP = 10

VMEM_LIMIT = 56 * 1024 * 1024

SHARD_SHAPES = (
    ("w_in", (D_MODEL, IN_WIDTH // N_DEV)),
    ("w_sb_up", (SB_WIDTH, D_MODEL // N_DEV)),
    ("w_dil_up", (DIL_OUT, D_MODEL // N_DEV)),
    ("w_out", (D_MODEL // N_DEV, D_MODEL)),
    ("w_ffn_in", (D_MODEL, 2 * D_FF // N_DEV)),
    ("w_ffn_out", (D_FF // N_DEV, D_MODEL)),
)
COL_SHARDED = ("w_in", "w_sb_up", "w_dil_up", "w_ffn_in")
PACK_ROWS = sum(r * c for _, (r, c) in SHARD_SHAPES) // LANES


def _dot(a, b):
    return jnp.dot(a, b, preferred_element_type=F32)


def _dot_nt(a, b):
    return lax.dot_general(a, b, (((1,), (1,)), ((), ())), preferred_element_type=F32)


def _dot_tn(a, b):
    return lax.dot_general(a, b, (((0,), (0,)), ((), ())), preferred_element_type=F32)


def _softplus(z):
    return jnp.maximum(z, 0.0) + jnp.log1p(jnp.exp(-jnp.abs(z)))


def _sigmoid(z):
    return 1.0 / (1.0 + jnp.exp(-z))


def _split_bf16(v):
    hi = v.astype(BF16)
    lo = (v - hi.astype(F32)).astype(BF16)
    return hi, lo


def _chunks(width, step=512):
    out, c = [], 0
    while c < width:
        w = min(step, width - c)
        out.append((c, w))
        c += w
    return out


def _resident(shape):
    nd = len(shape)
    return pl.BlockSpec(shape, lambda *_: (0,) * nd, pipeline_mode=pl.Buffered(1))


def _params(sem):
    return pltpu.CompilerParams(dimension_semantics=sem, vmem_limit_bytes=VMEM_LIMIT)


def _rms_fwd(x, g):
    r = lax.rsqrt(jnp.mean(x * x, axis=-1, keepdims=True) + RMS_EPS)
    n = x * r
    return n, r, n * g


def _rms_bwd(dy, n, r, g):
    dg = jnp.sum(dy * n, axis=0, keepdims=True)
    dn = dy * g
    dx = r * (dn - n * jnp.mean(dn * n, axis=-1, keepdims=True))
    return dx, dg


TM = 256


def _norm_proj(x, g, w_in):
    def body(x_ref, g_ref, w_ref, qkv_ref, gate_ref, u_ref):
        _, _, u = _rms_fwd(x_ref[...], g_ref[...])
        u = u.astype(BF16)
        u_ref[...] = u
        for c0, w in _chunks(QKV_WIDTH):
            qkv_ref[:, c0:c0 + w] = _dot(u, w_ref[:, c0:c0 + w]).astype(BF16)
        for c0, w in _chunks(2 * D_MODEL):
            gate_ref[:, c0:c0 + w] = _dot(u, w_ref[:, QKV_WIDTH + c0:QKV_WIDTH + c0 + w])

    return pl.pallas_call(
        body, name="norm_proj", grid=(TOK // TM,),
        in_specs=[pl.BlockSpec((TM, D_MODEL), lambda i: (i, 0)), _resident((1, D_MODEL)),
                  _resident((D_MODEL, IN_WIDTH))],
        out_specs=[pl.BlockSpec((TM, QKV_WIDTH), lambda i: (i, 0)),
                   pl.BlockSpec((TM, 2 * D_MODEL), lambda i: (i, 0)),
                   pl.BlockSpec((TM, D_MODEL), lambda i: (i, 0))],
        out_shape=[jax.ShapeDtypeStruct((TOK, QKV_WIDTH), BF16),
                   jax.ShapeDtypeStruct((TOK, 2 * D_MODEL), F32),
                   jax.ShapeDtypeStruct((TOK, D_MODEL), BF16)],
        compiler_params=_params(("parallel",)),
    )(x, g, w_in)


def _mix_out(x, o_sb, o_dl, gates, w_sb_up, w_dil_up, w_out):
    def body(x_ref, osb_ref, odl_ref, gate_ref, wsb_ref, wdl_ref, wout_ref, x1_ref, mg_ref):
        y_sb = _dot(osb_ref[...], wsb_ref[...])
        y_dl = _dot(odl_ref[...].astype(BF16), wdl_ref[...])
        merged = (_sigmoid(gate_ref[:, :D_MODEL]) * y_sb
                  + _sigmoid(gate_ref[:, D_MODEL:]) * y_dl).astype(BF16)
        mg_ref[...] = merged
        x1_ref[...] = x_ref[...] + _dot(merged, wout_ref[...])

    return pl.pallas_call(
        body, name="mix_out", grid=(TOK // TM,),
        in_specs=[pl.BlockSpec((TM, D_MODEL), lambda i: (i, 0)),
                  pl.BlockSpec((TM, SB_WIDTH), lambda i: (i, 0)),
                  pl.BlockSpec((TM, DIL_OUT), lambda i: (i, 0)),
                  pl.BlockSpec((TM, 2 * D_MODEL), lambda i: (i, 0)),
                  _resident((SB_WIDTH, D_MODEL)), _resident((DIL_OUT, D_MODEL)),
                  _resident((D_MODEL, D_MODEL))],
        out_specs=[pl.BlockSpec((TM, D_MODEL), lambda i: (i, 0)),
                   pl.BlockSpec((TM, D_MODEL), lambda i: (i, 0))],
        out_shape=[jax.ShapeDtypeStruct((TOK, D_MODEL), F32),
                   jax.ShapeDtypeStruct((TOK, D_MODEL), BF16)],
        compiler_params=_params(("parallel",)),
    )(x, o_sb, o_dl, gates, w_sb_up, w_dil_up, w_out)


FF_CHUNK = D_FF // 2


def _ffn_fwd_bwd(x1, target, g_ffn, g_fin, w_ffn_in, w_ffn_out):
    def body(x1_ref, t_ref, gffn_ref, gfin_ref, win_ref, wout_ref,
             loss_ref, dx1_ref, u2_ref, act_ref, dh_ref, dx2_ref, dgfin_ref, dgffn_ref, h_scr):
        i = pl.program_id(0)

        @pl.when(i == 0)
        def _():
            loss_ref[...] = jnp.zeros_like(loss_ref)
            dgfin_ref[...] = jnp.zeros_like(dgfin_ref)
            dgffn_ref[...] = jnp.zeros_like(dgffn_ref)

        x1 = x1_ref[...]
        g_ffn_v = gffn_ref[...]
        g_fin_v = gfin_ref[...]
        n2, r2, u2 = _rms_fwd(x1, g_ffn_v)
        u2 = u2.astype(BF16)
        u2_ref[...] = u2
        x2 = x1
        for c0 in range(0, D_FF, FF_CHUNK):
            gate = _dot(u2, win_ref[:, c0:c0 + FF_CHUNK])
            up = _dot(u2, win_ref[:, D_FF + c0:D_FF + c0 + FF_CHUNK])
            h_scr[:, c0:c0 + FF_CHUNK] = gate
            h_scr[:, D_FF + c0:D_FF + c0 + FF_CHUNK] = up
            act = (gate * _sigmoid(gate) * up).astype(BF16)
            act_ref[:, c0:c0 + FF_CHUNK] = act
            x2 = x2 + _dot(act, wout_ref[c0:c0 + FF_CHUNK, :])
        n3, r3, y = _rms_fwd(x2, g_fin_v)
        err = y - t_ref[...]
        sq = jnp.sum(jnp.sum(err * err, axis=1, keepdims=True), axis=0, keepdims=True)
        loss_ref[...] += sq * (0.5 / D_MODEL)
        dx2, dgfin = _rms_bwd(err * (1.0 / D_MODEL), n3, r3, g_fin_v)
        dgfin_ref[...] += dgfin
        dx2_b = dx2.astype(BF16)
        dx2_ref[...] = dx2_b
        du2 = jnp.zeros((TM, D_MODEL), F32)
        for c0 in range(0, D_FF, FF_CHUNK):
            gate = h_scr[:, c0:c0 + FF_CHUNK]
            up = h_scr[:, D_FF + c0:D_FF + c0 + FF_CHUNK]
            dact = _dot_nt(dx2_b, wout_ref[c0:c0 + FF_CHUNK, :])
            sg = _sigmoid(gate)
            dgate = (dact * up * (sg * (1.0 + gate * (1.0 - sg)))).astype(BF16)
            dup = (dact * (gate * sg)).astype(BF16)
            dh_ref[:, c0:c0 + FF_CHUNK] = dgate
            dh_ref[:, D_FF + c0:D_FF + c0 + FF_CHUNK] = dup
            du2 = du2 + _dot_nt(dgate, win_ref[:, c0:c0 + FF_CHUNK])
            du2 = du2 + _dot_nt(dup, win_ref[:, D_FF + c0:D_FF + c0 + FF_CHUNK])
        dx1_n, dgffn = _rms_bwd(du2, n2, r2, g_ffn_v)
        dgffn_ref[...] += dgffn
        dx1_ref[...] = dx2 + dx1_n

    tile = lambda w: pl.BlockSpec((TM, w), lambda i: (i, 0))
    acc = lambda w: pl.BlockSpec((1, w), lambda i: (0, 0))
    return pl.pallas_call(
        body, name="ffn_fwd_bwd", grid=(TOK // TM,),
        in_specs=[tile(D_MODEL), tile(D_MODEL), _resident((1, D_MODEL)), _resident((1, D_MODEL)),
                  _resident((D_MODEL, 2 * D_FF)), _resident((D_FF, D_MODEL))],
        out_specs=[acc(LANES), tile(D_MODEL), tile(D_MODEL), tile(D_FF), tile(2 * D_FF), tile(D_MODEL),
                   acc(D_MODEL), acc(D_MODEL)],
        out_shape=[jax.ShapeDtypeStruct((1, LANES), F32),
                   jax.ShapeDtypeStruct((TOK, D_MODEL), F32),
                   jax.ShapeDtypeStruct((TOK, D_MODEL), BF16),
                   jax.ShapeDtypeStruct((TOK, D_FF), BF16),
                   jax.ShapeDtypeStruct((TOK, 2 * D_FF), BF16),
                   jax.ShapeDtypeStruct((TOK, D_MODEL), BF16),
                   jax.ShapeDtypeStruct((1, D_MODEL), F32),
                   jax.ShapeDtypeStruct((1, D_MODEL), F32)],
        scratch_shapes=[pltpu.VMEM((TM, 2 * D_FF), F32)],
        compiler_params=_params(("arbitrary",)),
    )(x1, target, g_ffn, g_fin, w_ffn_in, w_ffn_out)


def _mix_bwd(dx1, o_sb, o_dl, gates, w_sb_up, w_dil_up, w_out):
    def body(dx1_ref, osb_ref, odl_ref, gate_ref, wsb_ref, wdl_ref, wout_ref,
             dgate_ref, dysb_ref, dydl_ref, dosb_ref, dodl_ref, dsum_ref):
        dmerged = _dot_nt(dx1_ref[...].astype(BF16), wout_ref[...])
        o_dl = odl_ref[...]
        y_sb = _dot(osb_ref[...], wsb_ref[...])
        y_dl = _dot(o_dl.astype(BF16), wdl_ref[...])
        s_sb = _sigmoid(gate_ref[:, :D_MODEL])
        s_dl = _sigmoid(gate_ref[:, D_MODEL:])
        dgate_ref[:, :D_MODEL] = (dmerged * y_sb * (s_sb * (1.0 - s_sb))).astype(BF16)
        dgate_ref[:, D_MODEL:] = (dmerged * y_dl * (s_dl * (1.0 - s_dl))).astype(BF16)
        dy_sb = (dmerged * s_sb).astype(BF16)
        dy_dl = (dmerged * s_dl).astype(BF16)
        dysb_ref[...] = dy_sb
        dydl_ref[...] = dy_dl
        dosb_ref[...] = _dot_nt(dy_sb, wsb_ref[...]).astype(BF16)
        do_dl = _dot_nt(dy_dl, wdl_ref[...])
        dodl_ref[...] = do_dl
        row = lax.broadcasted_iota(jnp.int32, (DIL_OUT, DIL_OUT), 0) // HEAD_DIM
        col = lax.broadcasted_iota(jnp.int32, (DIL_OUT, DIL_OUT), 1) // HEAD_DIM
        same_head = (row == col).astype(BF16)
        hi, lo = _split_bf16(do_dl * o_dl)
        dsum_ref[...] = _dot(hi, same_head) + _dot(lo, same_head)

    tile = lambda w: pl.BlockSpec((TM, w), lambda i: (i, 0))
    return pl.pallas_call(
        body, name="mix_bwd", grid=(TOK // TM,),
        in_specs=[tile(D_MODEL), tile(SB_WIDTH), tile(DIL_OUT), tile(2 * D_MODEL),
                  _resident((SB_WIDTH, D_MODEL)), _resident((DIL_OUT, D_MODEL)),
                  _resident((D_MODEL, D_MODEL))],
        out_specs=[tile(2 * D_MODEL), tile(D_MODEL), tile(D_MODEL), tile(SB_WIDTH), tile(DIL_OUT),
                   tile(DIL_OUT)],
        out_shape=[jax.ShapeDtypeStruct((TOK, 2 * D_MODEL), BF16),
                   jax.ShapeDtypeStruct((TOK, D_MODEL), BF16),
                   jax.ShapeDtypeStruct((TOK, D_MODEL), BF16),
                   jax.ShapeDtypeStruct((TOK, SB_WIDTH), BF16),
                   jax.ShapeDtypeStruct((TOK, DIL_OUT), F32),
                   jax.ShapeDtypeStruct((TOK, DIL_OUT), F32)],
        compiler_params=_params(("parallel",)),
    )(dx1, o_sb, o_dl, gates, w_sb_up, w_dil_up, w_out)


def _proj_bwd(dproj, dx1, x, g, w_in):
    def body(dp_ref, dx1_ref, x_ref, g_ref, w_ref, dx_ref, dg_ref):
        @pl.when(pl.program_id(0) == 0)
        def _():
            dg_ref[...] = jnp.zeros_like(dg_ref)

        du = jnp.zeros((TM, D_MODEL), F32)
        for c0, w in _chunks(IN_WIDTH, 1024):
            du = du + _dot_nt(dp_ref[:, c0:c0 + w], w_ref[:, c0:c0 + w])
        g_v = g_ref[...]
        n, r, _ = _rms_fwd(x_ref[...], g_v)
        dx, dg = _rms_bwd(du, n, r, g_v)
        dg_ref[...] += dg
        dx_ref[...] = dx1_ref[...] + dx

    tile = lambda w: pl.BlockSpec((TM, w), lambda i: (i, 0))
    return pl.pallas_call(
        body, name="proj_bwd", grid=(TOK // TM,),
        in_specs=[tile(IN_WIDTH), tile(D_MODEL), tile(D_MODEL), _resident((1, D_MODEL)),
                  _resident((D_MODEL, IN_WIDTH))],
        out_specs=[tile(D_MODEL), pl.BlockSpec((1, D_MODEL), lambda i: (0, 0))],
        out_shape=[jax.ShapeDtypeStruct((TOK, D_MODEL), F32),
                   jax.ShapeDtypeStruct((1, D_MODEL), F32)],
        compiler_params=_params(("arbitrary",)),
    )(dproj, dx1, x, g, w_in)


def _atb(a, b, name, tm, tn, tk=512):
    m, n = a.shape[1], b.shape[1]
    nk = TOK // tk

    def body(a_ref, b_ref, o_ref):
        @pl.when(pl.program_id(2) == 0)
        def _():
            o_ref[...] = jnp.zeros_like(o_ref)

        o_ref[...] += _dot_tn(a_ref[...], b_ref[...])

    return pl.pallas_call(
        body, name=name, grid=(m // tm, n // tn, nk),
        in_specs=[pl.BlockSpec((tk, tm), lambda i, j, k: (k, i)),
                  pl.BlockSpec((tk, tn), lambda i, j, k: (k, j))],
        out_specs=pl.BlockSpec((tm, tn), lambda i, j, k: (i, j)),
        out_shape=jax.ShapeDtypeStruct((m, n), F32),
        compiler_params=_params(("parallel", "parallel", "arbitrary")),
    )(a, b)


SB_PAIRS = SB_WIDTH // LANES


def _sb_masks():
    row = lax.broadcasted_iota(jnp.int32, (BLK, BLK), 0)
    col = lax.broadcasted_iota(jnp.int32, (BLK, BLK), 1)
    return row, col, col < HEAD_DIM


def _two_heads(v, lane0):
    zero = jnp.zeros_like(v)
    return jnp.where(lane0, v, zero), jnp.where(lane0, zero, v)


SB_QBLK = 256
N_SB_STEPS = SEQ // SB_QBLK


SB_KCHUNK = 2 * BLK
SB_ROWS = 2 * SB_QBLK


def _log_keep(z):
    neg_z = -z
    return jnp.minimum(neg_z, 0.0) - jnp.log(1.0 + jnp.exp(jnp.minimum(z, neg_z)))


def _stack_heads(v, lane0):
    return jnp.concatenate(_two_heads(v, lane0), axis=0)


def _block_sums(v, tri):
    halves = (v[:, :BLK], v[:, BLK:])
    hi, lo = _split_bf16(jnp.concatenate(halves, axis=0))
    prod = _dot(jnp.concatenate([hi, lo], axis=0), tri)
    tri_sum = prod[:2 * SB_ROWS] + prod[2 * SB_ROWS:]
    sums = tuple(jnp.sum(h, axis=1, keepdims=True) for h in halves)
    return (tri_sum[:SB_ROWS], tri_sum[SB_ROWS:]), sums


def _sb_diag_mask():
    row = lax.broadcasted_iota(jnp.int32, (SB_ROWS, SB_KCHUNK), 0)
    col = lax.broadcasted_iota(jnp.int32, (SB_ROWS, SB_KCHUNK), 1)
    return col < jnp.where(row >= SB_QBLK, row - SB_QBLK, row)


def _sb_fwd(qkv):
    def body(q_ref, k_ref, v_ref, o_ref, tot_ref):
        i = pl.program_id(2)
        krow = lax.broadcasted_iota(jnp.int32, (BLK, BLK), 0)
        kcol = lax.broadcasted_iota(jnp.int32, (BLK, BLK), 1)
        later = (krow > kcol).astype(BF16)
        lane0 = lax.broadcasted_iota(jnp.int32, (SB_QBLK, LANES), 1) < HEAD_DIM
        q2 = _stack_heads(q_ref[0] * QK_SCALE, lane0)

        def chunk(c, carry, causal):
            acc, run = carry
            off = pl.multiple_of(c * SB_KCHUNK, SB_KCHUNK)
            z = _dot_nt(q2, k_ref[0, pl.ds(off, SB_KCHUNK), :])
            log_keep = _log_keep(z)
            if causal is not None:
                log_keep = jnp.where(causal, log_keep, 0.0)
            suffix, sums = _block_sums(log_keep, later)
            log_after = jnp.concatenate([suffix[0] + (run + sums[1]), suffix[1] + run], axis=1)
            a = jnp.exp(log_keep + z + log_after)
            if causal is not None:
                a = jnp.where(causal, a, 0.0)
            acc = acc + _dot(a.astype(BF16), v_ref[0, pl.ds(off, SB_KCHUNK), :])
            return acc, run + (sums[0] + sums[1])

        carry = chunk(i, (jnp.zeros((SB_ROWS, LANES), F32), jnp.zeros((SB_ROWS, 1), F32)), _sb_diag_mask())
        acc, run = lax.fori_loop(0, i, lambda t, c: chunk(i - 1 - t, c, None), carry)
        o_ref[0] = jnp.where(lane0, acc[:SB_QBLK], acc[SB_QBLK:]).astype(BF16)
        tot_ref[0] = jnp.where(lane0, run[:SB_QBLK], run[SB_QBLK:])

    blk = pl.BlockSpec((1, SB_QBLK, LANES), lambda b, h, i: (b, i, h))
    return pl.pallas_call(
        body, name="sb_fwd", grid=(B_LOC, SB_PAIRS, N_SB_STEPS),
        in_specs=[blk,
                  pl.BlockSpec((1, SEQ, LANES), lambda b, h, i: (b, 0, SB_PAIRS + h)),
                  pl.BlockSpec((1, SEQ, LANES), lambda b, h, i: (b, 0, 2 * SB_PAIRS + h))],
        out_specs=[blk, blk],
        out_shape=[jax.ShapeDtypeStruct((B_LOC, SEQ, SB_WIDTH), BF16),
                   jax.ShapeDtypeStruct((B_LOC, SEQ, SB_WIDTH), F32)],
        compiler_params=_params(("parallel", "parallel", "arbitrary")),
    )(qkv, qkv, qkv)


def _sb_bwd(qkv, d_o, tot, token):
    def body(q_ref, k_ref, v_ref, do_ref, tot_ref, token_ref, dq_ref, dk_ref, dv_ref, dk_acc, dv_acc):
        i = pl.program_id(2)
        krow = lax.broadcasted_iota(jnp.int32, (BLK, BLK), 0)
        kcol = lax.broadcasted_iota(jnp.int32, (BLK, BLK), 1)
        upto = (krow <= kcol).astype(BF16)
        earlier = (krow < kcol).astype(BF16)
        lane0 = lax.broadcasted_iota(jnp.int32, (SB_QBLK, LANES), 1) < HEAD_DIM
        q2 = _stack_heads(q_ref[0] * QK_SCALE, lane0)
        do2 = _stack_heads(do_ref[0], lane0)
        tot = tot_ref[0]
        tot2 = jnp.concatenate([_head_col(tot, lane0), _head_col(tot, jnp.logical_not(lane0))], axis=0)

        @pl.when(i == 0)
        def _():
            dk_acc[...] = jnp.zeros_like(dk_acc)
            dv_acc[...] = jnp.zeros_like(dv_acc)

        def chunk(c, carry, causal):
            dq, pre_keep, pre_e = carry
            off = pl.multiple_of(c * SB_KCHUNK, SB_KCHUNK)
            k_c = k_ref[0, pl.ds(off, SB_KCHUNK), :]
            v_c = v_ref[0, pl.ds(off, SB_KCHUNK), :]
            z = _dot_nt(q2, k_c)
            d_a = _dot_nt(do2, v_c)
            log_keep = _log_keep(z)
            if causal is not None:
                log_keep = jnp.where(causal, log_keep, 0.0)
            log_beta = log_keep + z
            prefix, sums = _block_sums(log_keep, upto)
            inclusive = jnp.concatenate([prefix[0], prefix[1] + sums[0]], axis=1)
            a = jnp.exp(log_beta + ((tot2 - pre_keep) - inclusive))
            if causal is not None:
                a = jnp.where(causal, a, 0.0)
            e = d_a * a
            e_prefix, e_sums = _block_sums(e, earlier)
            before = jnp.concatenate([e_prefix[0] + pre_e, e_prefix[1] + (pre_e + e_sums[0])], axis=1)
            dz = e - (e + before) * jnp.exp(log_beta)
            if causal is not None:
                dz = jnp.where(causal, dz, 0.0)
            dz = dz.astype(BF16)
            dq = dq + _dot(dz, k_c)
            dk_acc[pl.ds(off, SB_KCHUNK), :] += _dot_tn(dz, q2)
            dv_acc[pl.ds(off, SB_KCHUNK), :] += _dot_tn(a.astype(BF16), do2)
            return dq, pre_keep + (sums[0] + sums[1]), pre_e + (e_sums[0] + e_sums[1])

        zero_col = jnp.zeros((SB_ROWS, 1), F32)
        carry = lax.fori_loop(0, i, lambda t, c: chunk(t, c, None),
                              (jnp.zeros((SB_ROWS, LANES), F32), zero_col, zero_col))
        dq, _, _ = chunk(i, carry, _sb_diag_mask())
        dq_ref[0] = (jnp.where(lane0, dq[:SB_QBLK], dq[SB_QBLK:]) * QK_SCALE).astype(BF16)

        @pl.when(i == N_SB_STEPS - 1)
        def _():
            dk_ref[0] = dk_acc[...].astype(BF16)
            dv_ref[0] = dv_acc[...].astype(BF16)

    blk = pl.BlockSpec((1, SB_QBLK, LANES), lambda b, h, i: (b, i, h))
    whole = lambda c: pl.BlockSpec((1, SEQ, LANES), lambda b, h, i: (b, 0, c * SB_PAIRS + h))
    out = jax.ShapeDtypeStruct((B_LOC, SEQ, SB_WIDTH), BF16)
    return pl.pallas_call(
        body, name="sb_bwd", grid=(B_LOC, SB_PAIRS, N_SB_STEPS),
        in_specs=[blk, whole(1), whole(2), blk, blk, pl.BlockSpec((8, LANES), lambda b, h, i: (0, 0))],
        out_specs=[blk, whole(0), whole(0)],
        out_shape=[out, out, out],
        scratch_shapes=[pltpu.VMEM((SEQ, LANES), F32), pltpu.VMEM((SEQ, LANES), F32)],
        compiler_params=_params(("parallel", "parallel", "arbitrary")),
    )(qkv, qkv, qkv, d_o, tot, token)


def _dil_scores(q_h, k_cur, k_prev, slope, dilation, has_prev, row, col):
    dist = (row - col).astype(F32) * float(dilation)
    s_cur = _dot_nt(q_h, k_cur) * QK_SCALE - slope * dist
    s_prev = _dot_nt(q_h, k_prev) * QK_SCALE - slope * (dist + float(BLK * dilation))
    s_cur = jnp.where(col <= row, s_cur, NEG_BIG)
    s_prev = jnp.where(jnp.logical_and(col >= row, has_prev), s_prev, NEG_BIG)
    return s_cur, s_prev


def _dil_slopes(group):
    pair = pl.program_id(1) % 2
    first = float(4 * group + 1) + 2.0 * pair.astype(F32)
    coef = -ALIBI_MAX_BIAS / DIL_HEADS * math.log(2.0)
    return jnp.exp(coef * first), jnp.exp(coef * (first + 1.0))


def _head_col(v, lane_mask):
    return jnp.max(jnp.where(lane_mask, v, NEG_BIG), axis=1, keepdims=True)


def _dil_fwd(q, k, v, group):
    dilation = DIL_PAIRS[group][1]
    length = SEQ // dilation
    nblk = length // BLK

    def body(q_ref, k_ref, v_ref, o_ref, lse_ref):
        row, col, lane0 = _sb_masks()
        slopes = _dil_slopes(group)

        def step(n, _):
            off = pl.multiple_of(n * BLK, BLK)
            off_prev = pl.multiple_of(jnp.maximum(n - 1, 0) * BLK, BLK)
            q_h = _two_heads(q_ref[0, pl.ds(off, BLK), :], lane0)
            k_cur = k_ref[0, pl.ds(off, BLK), :]
            v_cur = v_ref[0, pl.ds(off, BLK), :]
            k_prev = k_ref[0, pl.ds(off_prev, BLK), :]
            v_prev = v_ref[0, pl.ds(off_prev, BLK), :]
            outs, lses = [], []
            for h in range(2):
                s_cur, s_prev = _dil_scores(q_h[h], k_cur, k_prev, slopes[h], dilation, n > 0, row, col)
                m = jnp.maximum(jnp.max(s_cur, axis=1, keepdims=True), jnp.max(s_prev, axis=1, keepdims=True))
                p_cur = jnp.exp(s_cur - m)
                p_prev = jnp.exp(s_prev - m)
                den = jnp.sum(p_cur, axis=1, keepdims=True) + jnp.sum(p_prev, axis=1, keepdims=True)
                num = _dot(p_cur.astype(BF16), v_cur) + _dot(p_prev.astype(BF16), v_prev)
                outs.append(num / den)
                lses.append(m + jnp.log(den))
            o_ref[0, pl.ds(off, BLK), :] = jnp.where(lane0, outs[0], outs[1])
            lse_ref[0, pl.ds(off, BLK), :] = jnp.where(lane0, lses[0], lses[1])
            return 0

        lax.fori_loop(0, nblk, step, 0)

    spec = pl.BlockSpec((1, length, LANES), lambda b, c: (b, 0, c))
    out = jax.ShapeDtypeStruct((B_LOC, length, dilation * DIL_OUT), F32)
    return pl.pallas_call(
        body, name=f"dil_fwd_{group}", grid=(B_LOC, 2 * dilation),
        in_specs=[spec, spec, spec], out_specs=[spec, spec], out_shape=[out, out],
        compiler_params=_params(("parallel", "parallel")),
    )(q, k, v)


def _dil_bwd(q, k, v, d_o, lse, dsum, group):
    dilation = DIL_PAIRS[group][1]
    length = SEQ // dilation
    nblk = length // BLK

    def body(q_ref, k_ref, v_ref, do_ref, lse_ref, dsum_ref, dq_ref, dk_ref, dv_ref, dk_acc, dv_acc):
        row, col, lane0 = _sb_masks()
        lane1 = jnp.logical_not(lane0)
        slopes = _dil_slopes(group)
        dk_acc[...] = jnp.zeros_like(dk_acc)
        dv_acc[...] = jnp.zeros_like(dv_acc)

        def step(n, _):
            off = pl.multiple_of(n * BLK, BLK)
            off_prev = pl.multiple_of(jnp.maximum(n - 1, 0) * BLK, BLK)
            q_h = _two_heads(q_ref[0, pl.ds(off, BLK), :], lane0)
            do_h = _two_heads(do_ref[0, pl.ds(off, BLK), :].astype(BF16), lane0)
            k_cur = k_ref[0, pl.ds(off, BLK), :]
            v_cur = v_ref[0, pl.ds(off, BLK), :]
            k_prev = k_ref[0, pl.ds(off_prev, BLK), :]
            v_prev = v_ref[0, pl.ds(off_prev, BLK), :]
            kc_h = _two_heads(k_cur, lane0)
            kp_h = _two_heads(k_prev, lane0)
            lse_blk = lse_ref[0, pl.ds(off, BLK), :]
            dsum_blk = dsum_ref[0, pl.ds(off, BLK), :]
            dq = jnp.zeros((BLK, LANES), F32)
            dk_c = jnp.zeros((BLK, LANES), F32)
            dk_p = jnp.zeros((BLK, LANES), F32)
            dv_c = jnp.zeros((BLK, LANES), F32)
            dv_p = jnp.zeros((BLK, LANES), F32)
            for h, lanes in enumerate((lane0, lane1)):
                s_cur, s_prev = _dil_scores(q_h[h], k_cur, k_prev, slopes[h], dilation, n > 0, row, col)
                lse_h = _head_col(lse_blk, lanes)
                dsum_h = _head_col(dsum_blk, lanes)
                p_cur = jnp.exp(s_cur - lse_h)
                p_prev = jnp.exp(s_prev - lse_h)
                ds_cur = (p_cur * (_dot_nt(do_h[h], v_cur) - dsum_h) * QK_SCALE).astype(BF16)
                ds_prev = (p_prev * (_dot_nt(do_h[h], v_prev) - dsum_h) * QK_SCALE).astype(BF16)
                dq = dq + _dot(ds_cur, kc_h[h]) + _dot(ds_prev, kp_h[h])
                dk_c = dk_c + _dot_tn(ds_cur, q_h[h])
                dk_p = dk_p + _dot_tn(ds_prev, q_h[h])
                dv_c = dv_c + _dot_tn(p_cur.astype(BF16), do_h[h])
                dv_p = dv_p + _dot_tn(p_prev.astype(BF16), do_h[h])
            dq_ref[0, pl.ds(off, BLK), :] = dq.astype(BF16)
            dk_acc[pl.ds(off, BLK), :] += dk_c
            dv_acc[pl.ds(off, BLK), :] += dv_c
            dk_acc[pl.ds(off_prev, BLK), :] += dk_p
            dv_acc[pl.ds(off_prev, BLK), :] += dv_p
            return 0

        lax.fori_loop(0, nblk, step, 0)
        dk_ref[0] = dk_acc[...].astype(BF16)
        dv_ref[0] = dv_acc[...].astype(BF16)

    spec = pl.BlockSpec((1, length, LANES), lambda b, c: (b, 0, c))
    out = jax.ShapeDtypeStruct((B_LOC, length, dilation * DIL_OUT), BF16)
    return pl.pallas_call(
        body, name=f"dil_bwd_{group}", grid=(B_LOC, 2 * dilation),
        in_specs=[spec] * 6, out_specs=[spec] * 3, out_shape=[out] * 3,
        scratch_shapes=[pltpu.VMEM((length, LANES), F32), pltpu.VMEM((length, LANES), F32)],
        compiler_params=_params(("parallel", "parallel")),
    )(q, k, v, d_o, lse, dsum)


def _dil_combine(outs, lses):
    def body(o0, o1, o2, l0, l1, l2, o_ref, lse_ref):
        ls = (l0[...], l1[...], l2[...])
        m = jnp.maximum(jnp.maximum(ls[0], ls[1]), ls[2])
        w = [jnp.exp(l - m) for l in ls]
        den = w[0] + w[1] + w[2]
        o_ref[...] = (w[0] * o0[...] + w[1] * o1[...] + w[2] * o2[...]) / den
        lse_ref[...] = m + jnp.log(den)

    tile = pl.BlockSpec((512, DIL_OUT), lambda i: (i, 0))
    out = jax.ShapeDtypeStruct((TOK, DIL_OUT), F32)
    return pl.pallas_call(
        body, name="dil_combine", grid=(TOK // 512,),
        in_specs=[tile] * 6, out_specs=[tile, tile], out_shape=[out, out],
        compiler_params=_params(("parallel",)),
    )(*outs, *lses)


def _to_residues(t, dilation):
    return t.reshape(B_LOC, SEQ // dilation, dilation * DIL_OUT)


def _peers():
    x, y, c = lax.axis_index("x"), lax.axis_index("y"), lax.axis_index("c")
    me = 4 * x + 2 * y + c
    peers = []
    for mask in range(1, N_DEV):
        px = 1 - x if mask & 4 else x
        py = 1 - y if mask & 2 else y
        pc = 1 - c if mask & 1 else c
        peers.append(((px, py, pc), 4 * px + 2 * py + pc))
    return me, peers


def _all_gather(packed):
    rows = packed.shape[0]

    def body(src_ref, out_ref, send_sems, recv_sems, local_sem):
        me, peers = _peers()
        mine = pltpu.make_async_copy(src_ref, out_ref.at[me], local_sem)
        mine.start()
        sends = []
        for k, (peer, _) in enumerate(peers):
            cp = pltpu.make_async_remote_copy(
                src_ref=src_ref, dst_ref=out_ref.at[me], send_sem=send_sems.at[k], recv_sem=recv_sems.at[k],
                device_id=peer, device_id_type=pl.DeviceIdType.MESH)
            cp.start()
            sends.append(cp)
        for k, (peer, peer_idx) in enumerate(peers):
            pltpu.make_async_remote_copy(
                src_ref=src_ref, dst_ref=out_ref.at[peer_idx], send_sem=send_sems.at[k], recv_sem=recv_sems.at[k],
                device_id=peer, device_id_type=pl.DeviceIdType.MESH).wait_recv()
        for cp in sends:
            cp.wait_send()
        mine.wait()

    return pl.pallas_call(
        body, name="all_gather_weights",
        in_specs=[pl.BlockSpec(memory_space=pl.ANY)],
        out_specs=pl.BlockSpec(memory_space=pl.ANY),
        out_shape=jax.ShapeDtypeStruct((N_DEV, rows, LANES), packed.dtype),
        scratch_shapes=[pltpu.SemaphoreType.DMA((N_DEV - 1,)), pltpu.SemaphoreType.DMA((N_DEV - 1,)),
                        pltpu.SemaphoreType.DMA],
    )(packed)


_HBM = pl.BlockSpec(memory_space=pltpu.HBM)
_SEM = pl.BlockSpec(memory_space=pltpu.SEMAPHORE)
_ANY = pl.BlockSpec(memory_space=pl.ANY)
_EFFECT = pltpu.SideEffectType.DATAFLOW_SIDE_EFFECTING


def _peer_copy(src_ref, land_ref, send_sems, recv_sems, k, peer, src_idx, slot):
    return pltpu.make_async_remote_copy(
        src_ref=src_ref if src_idx is None else src_ref.at[src_idx], dst_ref=land_ref.at[slot],
        send_sem=send_sems.at[k], recv_sem=recv_sems.at[k], device_id=peer, device_id_type=pl.DeviceIdType.MESH)


def _send_start(src, name, scatter, after):
    def body(src_ref, land_ref, after_ref, send_sems, recv_sems, src_thru, land_thru, token):
        me, peers = _peers()
        for k, (peer, peer_idx) in enumerate(peers):
            _peer_copy(src_ref, land_ref, send_sems, recv_sems, k, peer, peer_idx if scatter else None, me).start()
        token[...] = jnp.zeros_like(token)

    land = lax.empty((N_DEV,) + src.shape[-2:], src.dtype)
    return pl.pallas_call(
        body, name=name,
        out_shape=(pltpu.SemaphoreType.DMA((N_DEV - 1,)), pltpu.SemaphoreType.DMA((N_DEV - 1,)),
                   pltpu.HBM(src.shape, src.dtype), pltpu.HBM(land.shape, land.dtype),
                   jax.ShapeDtypeStruct((8, LANES), F32)),
        in_specs=(_HBM, _HBM, _ANY),
        out_specs=(_SEM, _SEM, _HBM, _HBM, pl.BlockSpec(memory_space=pltpu.VMEM)),
        input_output_aliases={0: 2, 1: 3},
        compiler_params=pltpu.CompilerParams(has_side_effects=_EFFECT),
    )(pltpu.with_memory_space_constraint(src, pltpu.HBM), pltpu.with_memory_space_constraint(land, pltpu.HBM), after)


def _send_wait(handles, name, scatter, afters):
    send_sems, recv_sems, src_thru, land_thru = handles

    def body(src_ref, land_ref, send_sems, recv_sems, *rest):
        me, peers = _peers()
        for k, (peer, peer_idx) in enumerate(peers):
            cp = _peer_copy(src_ref, land_ref, send_sems, recv_sems, k, peer, peer_idx if scatter else None, peer_idx)
            cp.wait_send()
            cp.wait_recv()

    return pl.pallas_call(
        body, name=name,
        out_shape=(pltpu.HBM(src_thru.shape, src_thru.dtype), pltpu.HBM(land_thru.shape, land_thru.dtype)),
        in_specs=(_HBM, _HBM, _SEM, _SEM) + (_ANY,) * len(afters),
        out_specs=(_HBM, _HBM),
        input_output_aliases={0: 0, 1: 1},
        compiler_params=pltpu.CompilerParams(has_side_effects=_EFFECT),
    )(src_thru, land_thru, send_sems, recv_sems, *afters)


def _sum_slots(slots, name, tile_rows):
    _, rows, cols = slots.shape

    def body(s_ref, o_ref):
        acc = s_ref[0].astype(F32)
        for j in range(1, N_DEV):
            acc = acc + s_ref[j].astype(F32)
        o_ref[...] = acc

    return pl.pallas_call(
        body, name=name, grid=(rows // tile_rows,),
        in_specs=[pl.BlockSpec((N_DEV, tile_rows, cols), lambda i: (0, i, 0))],
        out_specs=pl.BlockSpec((tile_rows, cols), lambda i: (i, 0)),
        out_shape=jax.ShapeDtypeStruct((rows, cols), F32),
        compiler_params=_params(("parallel",)),
    )(slots)


def _adamw(w, g, m, v, name):
    rows, cols = w.shape
    tile_rows = max(t for t in range(8, 257, 8) if rows % t == 0) if rows % 8 == 0 else rows
    c1 = 1.0 - ADAM_B1 ** ADAM_STEP
    c2 = 1.0 - ADAM_B2 ** ADAM_STEP

    def body(w_ref, g_ref, m_ref, v_ref, d_ref, nm_ref, nv_ref):
        g_v = g_ref[...]
        m_new = ADAM_B1 * m_ref[...] + (1.0 - ADAM_B1) * g_v
        v_new = ADAM_B2 * v_ref[...] + (1.0 - ADAM_B2) * (g_v * g_v)
        nm_ref[...] = m_new
        nv_ref[...] = v_new
        d_ref[...] = -ADAM_LR * ((m_new / c1) / (jnp.sqrt(v_new / c2) + ADAM_EPS) + ADAM_WD * w_ref[...])

    tile = pl.BlockSpec((tile_rows, cols), lambda i: (i, 0))
    out = jax.ShapeDtypeStruct((rows, cols), F32)
    return pl.pallas_call(
        body, name=name, grid=(rows // tile_rows,),
        in_specs=[tile] * 4, out_specs=[tile] * 3, out_shape=[out] * 3,
        compiler_params=_params(("parallel",)),
    )(w, g, m, v)


GROUP_FIRST = ("w_in",)
GROUP_REST = ("w_sb_up", "w_dil_up", "w_out", "w_ffn_in", "w_ffn_out")


def _group_shapes(names):
    return [(name, shape) for name, shape in SHARD_SHAPES if name in names]


def _pack_shards(shards, names):
    return jnp.concatenate([shards[name].reshape(-1, LANES) for name, _ in _group_shapes(names)], axis=0)


def _unpack_shards(packed, names):
    out, r0 = {}, 0
    for name, (r, c) in _group_shapes(names):
        n = r * c // LANES
        out[name] = packed[r0:r0 + n].reshape(r, c)
        r0 += n
    return out


def _unpack_full(gathered, names):
    out, r0 = {}, 0
    for name, (r, c) in _group_shapes(names):
        n = r * c // LANES
        blk = gathered[:, r0:r0 + n].reshape(N_DEV, r, c)
        if name in COL_SHARDED:
            out[name] = blk.transpose(1, 0, 2).reshape(r, N_DEV * c)
        else:
            out[name] = blk.reshape(N_DEV * r, c)
        r0 += n
    return out


def _pack_full(full, names):
    parts = []
    for name, (r, c) in _group_shapes(names):
        t = full[name]
        if name in COL_SHARDED:
            t = t.reshape(r, N_DEV, c).transpose(1, 0, 2)
        parts.append(t.reshape(N_DEV, r * c // LANES, LANES))
    return jnp.concatenate(parts, axis=1)


def _local_step(x, target, g_mix, g_ffn, g_fin, w_in, take_rest, give_rest):
    w = {"w_in": w_in}
    qkv, gates, u = _norm_proj(x, g_mix, w["w_in"])
    qkv3 = qkv.reshape(B_LOC, SEQ, QKV_WIDTH)
    o_sb, sb_tot = _sb_fwd(qkv3)
    o_sb = o_sb.reshape(TOK, SB_WIDTH)

    dil_in, dil_o, dil_lse = [], [], []
    for grp, (_, dilation) in enumerate(DIL_PAIRS):
        parts = []
        for part in range(3):
            c0 = 3 * SB_WIDTH + part * DIL_WIDTH + grp * DIL_OUT
            parts.append(_to_residues(qkv[:, c0:c0 + DIL_OUT], dilation))
        dil_in.append(parts)
        o_g, lse_g = _dil_fwd(*parts, grp)
        dil_o.append(o_g.reshape(TOK, DIL_OUT))
        dil_lse.append(lse_g.reshape(TOK, DIL_OUT))
    o_dl, lse = _dil_combine(dil_o, dil_lse)

    w.update(take_rest(o_sb, o_dl))
    x1, merged = _mix_out(x, o_sb, o_dl, gates, w["w_sb_up"], w["w_dil_up"], w["w_out"])
    loss, dx1, u2, act, dh, dx2, dg_fin, dg_ffn = _ffn_fwd_bwd(x1, target, g_ffn, g_fin, w["w_ffn_in"], w["w_ffn_out"])
    dgates, dy_sb, dy_dl, do_sb, do_dl, dsum = _mix_bwd(dx1, o_sb, o_dl, gates, w["w_sb_up"], w["w_dil_up"], w["w_out"])
    token = give_rest({
        "w_sb_up": _atb(o_sb, dy_sb, "grad_w_sb_up", SB_WIDTH, D_MODEL),
        "w_dil_up": _atb(o_dl.astype(BF16), dy_dl, "grad_w_dil_up", DIL_OUT, D_MODEL),
        "w_out": _atb(merged, dx1.astype(BF16), "grad_w_out", D_MODEL, D_MODEL),
        "w_ffn_in": _atb(u2, dh, "grad_w_ffn_in", D_MODEL, D_FF),
        "w_ffn_out": _atb(act, dx2, "grad_w_ffn_out", D_FF // 2, D_MODEL),
    })

    dq_sb, dk_sb, dv_sb = _sb_bwd(qkv3, do_sb.reshape(B_LOC, SEQ, SB_WIDTH), sb_tot, token)
    dq_dl, dk_dl, dv_dl = [], [], []
    for grp, (_, dilation) in enumerate(DIL_PAIRS):
        dq, dk, dv = _dil_bwd(*dil_in[grp], _to_residues(do_dl, dilation), _to_residues(lse, dilation),
                              _to_residues(dsum, dilation), grp)
        dq_dl.append(dq.reshape(TOK, DIL_OUT))
        dk_dl.append(dk.reshape(TOK, DIL_OUT))
        dv_dl.append(dv.reshape(TOK, DIL_OUT))
    flat = lambda t: t.reshape(TOK, SB_WIDTH)
    dproj = jnp.concatenate([flat(dq_sb), flat(dk_sb), flat(dv_sb), *dq_dl, *dk_dl, *dv_dl, dgates], axis=1)

    grad_x, dg_mix = _proj_bwd(dproj, dx1, x, g_mix, w["w_in"])
    grad_w_in = _atb(u, dproj, "grad_w_in", D_MODEL, IN_WIDTH // 2)
    gain_grads = jnp.concatenate([dg_mix, dg_ffn, dg_fin], axis=0)
    return loss, grad_x, grad_w_in, gain_grads


def kernel(x, norm_mix_g, w_in, w_sb_up, w_dil_up, w_out, norm_ffn_g, w_ffn_in, w_ffn_out, norm_final_g, loss_target, m_norm_mix_g, m_w_in, m_w_sb_up, m_w_dil_up, m_w_out, m_norm_ffn_g, m_w_ffn_in, m_w_ffn_out, m_norm_final_g, v_norm_mix_g, v_w_in, v_w_sb_up, v_w_dil_up, v_w_out, v_norm_ffn_g, v_w_ffn_in, v_w_ffn_out, v_norm_final_g):
    mats = {"w_in": w_in, "w_sb_up": w_sb_up, "w_dil_up": w_dil_up, "w_out": w_out,
            "w_ffn_in": w_ffn_in, "w_ffn_out": w_ffn_out}
    moments_m = {"w_in": m_w_in, "w_sb_up": m_w_sb_up, "w_dil_up": m_w_dil_up, "w_out": m_w_out,
                 "w_ffn_in": m_w_ffn_in, "w_ffn_out": m_w_ffn_out}
    moments_v = {"w_in": v_w_in, "w_sb_up": v_w_sb_up, "w_dil_up": v_w_dil_up, "w_out": v_w_out,
                 "w_ffn_in": v_w_ffn_in, "w_ffn_out": v_w_ffn_out}
    shards = {name: mats[name][0].astype(BF16) for name, _ in SHARD_SHAPES}
    me = 4 * lax.axis_index("x") + 2 * lax.axis_index("y") + lax.axis_index("c")

    def own_slot(landed, block):
        return lax.dynamic_update_slice(landed, block[None], (me, 0, 0))

    gathered_first = _all_gather(_pack_shards(shards, GROUP_FIRST))
    rest_handles = _send_start(_pack_shards(shards, GROUP_REST), "gather_rest_start", False, gathered_first)

    def take_rest(o_sb, o_dl):
        sent, landed = _send_wait(rest_handles[:4], "gather_rest_wait", False, (o_sb, o_dl))
        return _unpack_full(own_slot(landed, sent), GROUP_REST)

    grad_handles = []

    def give_rest(grads):
        handles = _send_start(_pack_full(grads, GROUP_REST).astype(BF16), "grads_rest_start", True, grads["w_out"])
        grad_handles.append(handles[:4])
        return handles[4]

    g_fin = norm_final_g.reshape(1, D_MODEL)
    loss, grad_x, grad_w_in, gain_grads = _local_step(
        x.reshape(TOK, D_MODEL), loss_target.reshape(TOK, D_MODEL), norm_mix_g + rest_handles[4][:1, :1],
        norm_ffn_g, g_fin, _unpack_full(gathered_first, GROUP_FIRST)["w_in"], take_rest, give_rest)

    def own_block(blocks):
        return lax.dynamic_slice(blocks, (me, 0, 0), (1,) + blocks.shape[1:])[0]

    out_g, out_d, out_m, out_v = {}, {}, {}, {}

    def update(g_shards):
        for name, g in g_shards.items():
            d, nm, nv = _adamw(mats[name][0], g, moments_m[name][0], moments_v[name][0], "adamw_" + name)
            out_g[name], out_d[name], out_m[name], out_v[name] = g[None], d[None], nm[None], nv[None]

    sent, landed = _send_wait(grad_handles[0], "grads_rest_wait", True, (grad_w_in,))
    first_handles = _send_start(_pack_full({"w_in": grad_w_in}, GROUP_FIRST).astype(BF16), "grads_first_start",
                                True, landed)
    gain_rows = jnp.concatenate([gain_grads, jnp.tile(loss, (1, D_MODEL // LANES)),
                                 jnp.zeros((8 - 4, D_MODEL), F32)], axis=0)
    gain_handles = _send_start(gain_rows, "gains_start", False, landed)
    update(_unpack_shards(_sum_slots(own_slot(landed, own_block(sent)), "sum_grad_rest", 512), GROUP_REST))
    sent, landed = _send_wait(first_handles[:4], "grads_first_wait", True, (out_d["w_ffn_in"],))
    update(_unpack_shards(_sum_slots(own_slot(landed, own_block(sent)), "sum_grad_w_in", 368), GROUP_FIRST))
    sent, landed = _send_wait(gain_handles[:4], "gains_wait", False, (out_d["w_ffn_in"],))
    g_gains = _sum_slots(own_slot(landed, sent), "sum_gain_grads", 8)

    gain_w = jnp.concatenate([norm_mix_g, norm_ffn_g, g_fin], axis=0)
    gain_m = jnp.concatenate([m_norm_mix_g, m_norm_ffn_g, m_norm_final_g.reshape(1, D_MODEL)], axis=0)
    gain_v = jnp.concatenate([v_norm_mix_g, v_norm_ffn_g, v_norm_final_g.reshape(1, D_MODEL)], axis=0)
    gd, gm, gv = _adamw(gain_w, g_gains[:3], gain_m, gain_v, "adamw_gains")
    for idx, name in enumerate(("norm_mix_g", "norm_ffn_g", "norm_final_g")):
        shape = (D_MODEL,) if name == "norm_final_g" else (1, D_MODEL)
        out_g[name] = g_gains[idx].reshape(shape)
        out_d[name], out_m[name], out_v[name] = gd[idx].reshape(shape), gm[idx].reshape(shape), gv[idx].reshape(shape)

    order = ("norm_mix_g", "w_in", "w_sb_up", "w_dil_up", "w_out", "norm_ffn_g", "w_ffn_in", "w_ffn_out",
             "norm_final_g")
    return (g_gains[3, 0], grad_x.reshape(B_LOC, SEQ, D_MODEL),
            *[out_g[n] for n in order], *[out_d[n] for n in order],
            *[out_m[n] for n in order], *[out_v[n] for n in order])
```

```python
import math

import jax
import jax.numpy as jnp
from jax import lax
from jax.experimental import pallas as pl
from jax.experimental.pallas import tpu as pltpu

F32 = jnp.float32
BF16 = jnp.bfloat16

N_DEV = 8
D_MODEL = 1024
SEQ = 2048
B_LOC = 2
TOK = B_LOC * SEQ
HEAD_DIM = 64
SB_WIDTH = 512
DIL_WIDTH = 768
DIL_OUT = 256
QKV_WIDTH = 3 * SB_WIDTH + 3 * DIL_WIDTH
IN_WIDTH = QKV_WIDTH + 2 * D_MODEL
D_FF = 2816
DIL_PAIRS = ((128, 1), (512, 4), (2048, 16))
DIL_HEADS = 12
RMS_EPS = 1e-6
ALIBI_MAX_BIAS = 8.0
QK_SCALE = 1.0 / math.sqrt(HEAD_DIM)
BLK = 128
LANES = 128
NEG_BIG = -1e30

ADAM_LR = 0.001
ADAM_B1 = 0.9
ADAM_B2 = 0.999
ADAM_EPS = 1e-08
ADAM_WD = 0.01
ADAM_STEP = 10

VMEM_LIMIT = 56 * 1024 * 1024


def _dot(a, b):
    return jnp.dot(a, b, preferred_element_type=F32)


def _dot_nt(a, b):
    return lax.dot_general(a, b, (((1,), (1,)), ((), ())), preferred_element_type=F32)


def _dot_tn(a, b):
    return lax.dot_general(a, b, (((0,), (0,)), ((), ())), preferred_element_type=F32)


def _softplus(z):
    return jnp.maximum(z, 0.0) + jnp.log1p(jnp.exp(-jnp.abs(z)))


def _sigmoid(z):
    return 1.0 / (1.0 + jnp.exp(-z))


def _split_bf16(v):
    hi = v.astype(BF16)
    lo = (v - hi.astype(F32)).astype(BF16)
    return hi, lo


def _chunks(width, step=512):
    out, c = [], 0
    while c < width:
        w = min(step, width - c)
        out.append((c, w))
        c += w
    return out


def _resident(shape):
    nd = len(shape)
    return pl.BlockSpec(shape, lambda *_: (0,) * nd, pipeline_mode=pl.Buffered(1))


def _params(sem):
    return pltpu.CompilerParams(dimension_semantics=sem, vmem_limit_bytes=VMEM_LIMIT)


def _rms_fwd(x, g):
    r = lax.rsqrt(jnp.mean(x * x, axis=-1, keepdims=True) + RMS_EPS)
    n = x * r
    return n, r, n * g


def _rms_bwd(dy, n, r, g):
    dg = jnp.sum(dy * n, axis=0, keepdims=True)
    dn = dy * g
    dx = r * (dn - n * jnp.mean(dn * n, axis=-1, keepdims=True))
    return dx, dg


TM = 256


def _norm_proj(x, g, w_in):
    def body(x_ref, g_ref, w_ref, qkv_ref, gate_ref, u_ref):
        _, _, u = _rms_fwd(x_ref[...], g_ref[...])
        u = u.astype(BF16)
        u_ref[...] = u
        for c0, w in _chunks(QKV_WIDTH):
            qkv_ref[:, c0:c0 + w] = _dot(u, w_ref[:, c0:c0 + w]).astype(BF16)
        for c0, w in _chunks(2 * D_MODEL):
            gate_ref[:, c0:c0 + w] = _dot(u, w_ref[:, QKV_WIDTH + c0:QKV_WIDTH + c0 + w])

    return pl.pallas_call(
        body, name="norm_proj", grid=(TOK // TM,),
        in_specs=[pl.BlockSpec((TM, D_MODEL), lambda i: (i, 0)), _resident((1, D_MODEL)),
                  _resident((D_MODEL, IN_WIDTH))],
        out_specs=[pl.BlockSpec((TM, QKV_WIDTH), lambda i: (i, 0)),
                   pl.BlockSpec((TM, 2 * D_MODEL), lambda i: (i, 0)),
                   pl.BlockSpec((TM, D_MODEL), lambda i: (i, 0))],
        out_shape=[jax.ShapeDtypeStruct((TOK, QKV_WIDTH), BF16),
                   jax.ShapeDtypeStruct((TOK, 2 * D_MODEL), F32),
                   jax.ShapeDtypeStruct((TOK, D_MODEL), BF16)],
        compiler_params=_params(("parallel",)),
    )(x, g, w_in)


def _mix_out(x, o_sb, o_dl, gates, w_sb_up, w_dil_up, w_out):
    def body(x_ref, osb_ref, odl_ref, gate_ref, wsb_ref, wdl_ref, wout_ref, x1_ref, mg_ref):
        y_sb = _dot(osb_ref[...], wsb_ref[...])
        y_dl = _dot(odl_ref[...].astype(BF16), wdl_ref[...])
        merged = (_sigmoid(gate_ref[:, :D_MODEL]) * y_sb
                  + _sigmoid(gate_ref[:, D_MODEL:]) * y_dl).astype(BF16)
        mg_ref[...] = merged
        x1_ref[...] = x_ref[...] + _dot(merged, wout_ref[...])

    return pl.pallas_call(
        body, name="mix_out", grid=(TOK // TM,),
        in_specs=[pl.BlockSpec((TM, D_MODEL), lambda i: (i, 0)),
                  pl.BlockSpec((TM, SB_WIDTH), lambda i: (i, 0)),
                  pl.BlockSpec((TM, DIL_OUT), lambda i: (i, 0)),
                  pl.BlockSpec((TM, 2 * D_MODEL), lambda i: (i, 0)),
                  _resident((SB_WIDTH, D_MODEL)), _resident((DIL_OUT, D_MODEL)),
                  _resident((D_MODEL, D_MODEL))],
        out_specs=[pl.BlockSpec((TM, D_MODEL), lambda i: (i, 0)),
                   pl.BlockSpec((TM, D_MODEL), lambda i: (i, 0))],
        out_shape=[jax.ShapeDtypeStruct((TOK, D_MODEL), F32),
                   jax.ShapeDtypeStruct((TOK, D_MODEL), BF16)],
        compiler_params=_params(("parallel",)),
    )(x, o_sb, o_dl, gates, w_sb_up, w_dil_up, w_out)


FF_CHUNK = D_FF // 2


def _ffn_fwd_bwd(x1, target, g_ffn, g_fin, w_ffn_in, w_ffn_out):
    def body(x1_ref, t_ref, gffn_ref, gfin_ref, win_ref, wout_ref,
             loss_ref, dx1_ref, u2_ref, act_ref, dh_ref, dx2_ref, dgfin_ref, dgffn_ref, h_scr):
        i = pl.program_id(0)

        @pl.when(i == 0)
        def _():
            loss_ref[...] = jnp.zeros_like(loss_ref)
            dgfin_ref[...] = jnp.zeros_like(dgfin_ref)
            dgffn_ref[...] = jnp.zeros_like(dgffn_ref)

        x1 = x1_ref[...]
        g_ffn_v = gffn_ref[...]
        g_fin_v = gfin_ref[...]
        n2, r2, u2 = _rms_fwd(x1, g_ffn_v)
        u2 = u2.astype(BF16)
        u2_ref[...] = u2
        x2 = x1
        for c0 in range(0, D_FF, FF_CHUNK):
            gate = _dot(u2, win_ref[:, c0:c0 + FF_CHUNK])
            up = _dot(u2, win_ref[:, D_FF + c0:D_FF + c0 + FF_CHUNK])
            h_scr[:, c0:c0 + FF_CHUNK] = gate
            h_scr[:, D_FF + c0:D_FF + c0 + FF_CHUNK] = up
            act = (gate * _sigmoid(gate) * up).astype(BF16)
            act_ref[:, c0:c0 + FF_CHUNK] = act
            x2 = x2 + _dot(act, wout_ref[c0:c0 + FF_CHUNK, :])
        n3, r3, y = _rms_fwd(x2, g_fin_v)
        err = y - t_ref[...]
        sq = jnp.sum(jnp.sum(err * err, axis=1, keepdims=True), axis=0, keepdims=True)
        loss_ref[...] += sq * (0.5 / D_MODEL)
        dx2, dgfin = _rms_bwd(err * (1.0 / D_MODEL), n3, r3, g_fin_v)
        dgfin_ref[...] += dgfin
        dx2_b = dx2.astype(BF16)
        dx2_ref[...] = dx2_b
        du2 = jnp.zeros((TM, D_MODEL), F32)
        for c0 in range(0, D_FF, FF_CHUNK):
            gate = h_scr[:, c0:c0 + FF_CHUNK]
            up = h_scr[:, D_FF + c0:D_FF + c0 + FF_CHUNK]
            dact = _dot_nt(dx2_b, wout_ref[c0:c0 + FF_CHUNK, :])
            sg = _sigmoid(gate)
            dgate = (dact * up * (sg * (1.0 + gate * (1.0 - sg)))).astype(BF16)
            dup = (dact * (gate * sg)).astype(BF16)
            dh_ref[:, c0:c0 + FF_CHUNK] = dgate
            dh_ref[:, D_FF + c0:D_FF + c0 + FF_CHUNK] = dup
            du2 = du2 + _dot_nt(dgate, win_ref[:, c0:c0 + FF_CHUNK])
            du2 = du2 + _dot_nt(dup, win_ref[:, D_FF + c0:D_FF + c0 + FF_CHUNK])
        dx1_n, dgffn = _rms_bwd(du2, n2, r2, g_ffn_v)
        dgffn_ref[...] += dgffn
        dx1_ref[...] = dx2 + dx1_n

    tile = lambda w: pl.BlockSpec((TM, w), lambda i: (i, 0))
    acc = lambda w: pl.BlockSpec((1, w), lambda i: (0, 0))
    return pl.pallas_call(
        body, name="ffn_fwd_bwd", grid=(TOK // TM,),
        in_specs=[tile(D_MODEL), tile(D_MODEL), _resident((1, D_MODEL)), _resident((1, D_MODEL)),
                  _resident((D_MODEL, 2 * D_FF)), _resident((D_FF, D_MODEL))],
        out_specs=[acc(LANES), tile(D_MODEL), tile(D_MODEL), tile(D_FF), tile(2 * D_FF), tile(D_MODEL),
                   acc(D_MODEL), acc(D_MODEL)],
        out_shape=[jax.ShapeDtypeStruct((1, LANES), F32),
                   jax.ShapeDtypeStruct((TOK, D_MODEL), F32),
                   jax.ShapeDtypeStruct((TOK, D_MODEL), BF16),
                   jax.ShapeDtypeStruct((TOK, D_FF), BF16),
                   jax.ShapeDtypeStruct((TOK, 2 * D_FF), BF16),
                   jax.ShapeDtypeStruct((TOK, D_MODEL), BF16),
                   jax.ShapeDtypeStruct((1, D_MODEL), F32),
                   jax.ShapeDtypeStruct((1, D_MODEL), F32)],
        scratch_shapes=[pltpu.VMEM((TM, 2 * D_FF), F32)],
        compiler_params=_params(("arbitrary",)),
    )(x1, target, g_ffn, g_fin, w_ffn_in, w_ffn_out)


def _mix_bwd(dx1, o_sb, o_dl, gates, w_sb_up, w_dil_up, w_out):
    def body(dx1_ref, osb_ref, odl_ref, gate_ref, wsb_ref, wdl_ref, wout_ref,
             dgate_ref, dysb_ref, dydl_ref, dosb_ref, dodl_ref, dsum_ref):
        dmerged = _dot_nt(dx1_ref[...].astype(BF16), wout_ref[...])
        o_dl = odl_ref[...]
        y_sb = _dot(osb_ref[...], wsb_ref[...])
        y_dl = _dot(o_dl.astype(BF16), wdl_ref[...])
        s_sb = _sigmoid(gate_ref[:, :D_MODEL])
        s_dl = _sigmoid(gate_ref[:, D_MODEL:])
        dgate_ref[:, :D_MODEL] = (dmerged * y_sb * (s_sb * (1.0 - s_sb))).astype(BF16)
        dgate_ref[:, D_MODEL:] = (dmerged * y_dl * (s_dl * (1.0 - s_dl))).astype(BF16)
        dy_sb = (dmerged * s_sb).astype(BF16)
        dy_dl = (dmerged * s_dl).astype(BF16)
        dysb_ref[...] = dy_sb
        dydl_ref[...] = dy_dl
        dosb_ref[...] = _dot_nt(dy_sb, wsb_ref[...]).astype(BF16)
        do_dl = _dot_nt(dy_dl, wdl_ref[...])
        dodl_ref[...] = do_dl
        row = lax.broadcasted_iota(jnp.int32, (DIL_OUT, DIL_OUT), 0) // HEAD_DIM
        col = lax.broadcasted_iota(jnp.int32, (DIL_OUT, DIL_OUT), 1) // HEAD_DIM
        same_head = (row == col).astype(BF16)
        hi, lo = _split_bf16(do_dl * o_dl)
        dsum_ref[...] = _dot(hi, same_head) + _dot(lo, same_head)

    tile = lambda w: pl.BlockSpec((TM, w), lambda i: (i, 0))
    return pl.pallas_call(
        body, name="mix_bwd", grid=(TOK // TM,),
        in_specs=[tile(D_MODEL), tile(SB_WIDTH), tile(DIL_OUT), tile(2 * D_MODEL),
                  _resident((SB_WIDTH, D_MODEL)), _resident((DIL_OUT, D_MODEL)),
                  _resident((D_MODEL, D_MODEL))],
        out_specs=[tile(2 * D_MODEL), tile(D_MODEL), tile(D_MODEL), tile(SB_WIDTH), tile(DIL_OUT),
                   tile(DIL_OUT)],
        out_shape=[jax.ShapeDtypeStruct((TOK, 2 * D_MODEL), BF16),
                   jax.ShapeDtypeStruct((TOK, D_MODEL), BF16),
                   jax.ShapeDtypeStruct((TOK, D_MODEL), BF16),
                   jax.ShapeDtypeStruct((TOK, SB_WIDTH), BF16),
                   jax.ShapeDtypeStruct((TOK, DIL_OUT), F32),
                   jax.ShapeDtypeStruct((TOK, DIL_OUT), F32)],
        compiler_params=_params(("parallel",)),
    )(dx1, o_sb, o_dl, gates, w_sb_up, w_dil_up, w_out)


def _proj_bwd(dproj, dx1, x, g, w_in):
    def body(dp_ref, dx1_ref, x_ref, g_ref, w_ref, dx_ref, dg_ref):
        @pl.when(pl.program_id(0) == 0)
        def _():
            dg_ref[...] = jnp.zeros_like(dg_ref)

        du = jnp.zeros((TM, D_MODEL), F32)
        for c0, w in _chunks(IN_WIDTH, 1024):
            du = du + _dot_nt(dp_ref[:, c0:c0 + w], w_ref[:, c0:c0 + w])
        g_v = g_ref[...]
        n, r, _ = _rms_fwd(x_ref[...], g_v)
        dx, dg = _rms_bwd(du, n, r, g_v)
        dg_ref[...] += dg
        dx_ref[...] = dx1_ref[...] + dx

    tile = lambda w: pl.BlockSpec((TM, w), lambda i: (i, 0))
    return pl.pallas_call(
        body, name="proj_bwd", grid=(TOK // TM,),
        in_specs=[tile(IN_WIDTH), tile(D_MODEL), tile(D_MODEL), _resident((1, D_MODEL)),
                  _resident((D_MODEL, IN_WIDTH))],
        out_specs=[tile(D_MODEL), pl.BlockSpec((1, D_MODEL), lambda i: (0, 0))],
        out_shape=[jax.ShapeDtypeStruct((TOK, D_MODEL), F32),
                   jax.ShapeDtypeStruct((1, D_MODEL), F32)],
        compiler_params=_params(("arbitrary",)),
    )(dproj, dx1, x, g, w_in)


def _atb(a, b, name, tm, tn, col_blocks=False, tk=512):
    m, n = a.shape[1], b.shape[1]
    nk = TOK // tk

    def body(a_ref, b_ref, o_ref, acc_ref):
        k = pl.program_id(2)

        @pl.when(k == 0)
        def _():
            acc_ref[...] = jnp.zeros_like(acc_ref)

        acc_ref[...] += _dot_tn(a_ref[...], b_ref[...])

        @pl.when(k == nk - 1)
        def _():
            o_ref[...] = acc_ref[...].astype(BF16)

    if col_blocks:
        out_spec = pl.BlockSpec((None, tm, tn), lambda i, j, k: (j, i, 0))
        out_shape = jax.ShapeDtypeStruct((n // tn, m, tn), BF16)
    else:
        out_spec = pl.BlockSpec((tm, tn), lambda i, j, k: (i, j))
        out_shape = jax.ShapeDtypeStruct((m, n), BF16)
    return pl.pallas_call(
        body, name=name, grid=(m // tm, n // tn, nk),
        in_specs=[pl.BlockSpec((tk, tm), lambda i, j, k: (k, i)),
                  pl.BlockSpec((tk, tn), lambda i, j, k: (k, j))],
        out_specs=out_spec, out_shape=out_shape,
        scratch_shapes=[pltpu.VMEM((tm, tn), F32)],
        compiler_params=_params(("parallel", "parallel", "arbitrary")),
    )(a, b)


SB_PAIRS = SB_WIDTH // LANES


def _sb_masks():
    row = lax.broadcasted_iota(jnp.int32, (BLK, BLK), 0)
    col = lax.broadcasted_iota(jnp.int32, (BLK, BLK), 1)
    return row, col, col < HEAD_DIM


def _two_heads(v, lane0):
    zero = jnp.zeros_like(v)
    return jnp.where(lane0, v, zero), jnp.where(lane0, zero, v)


SB_QBLK = 256
N_SB_STEPS = SEQ // SB_QBLK


SB_KCHUNK = 2 * BLK
SB_ROWS = 2 * SB_QBLK


def _log_keep(z):
    neg_z = -z
    return jnp.minimum(neg_z, 0.0) - jnp.log(1.0 + jnp.exp(jnp.minimum(z, neg_z)))


def _stack_heads(v, lane0):
    return jnp.concatenate(_two_heads(v, lane0), axis=0)


def _block_sums(v, tri):
    halves = (v[:, :BLK], v[:, BLK:])
    hi, lo = _split_bf16(jnp.concatenate(halves, axis=0))
    prod = _dot(jnp.concatenate([hi, lo], axis=0), tri)
    tri_sum = prod[:2 * SB_ROWS] + prod[2 * SB_ROWS:]
    sums = tuple(jnp.sum(h, axis=1, keepdims=True) for h in halves)
    return (tri_sum[:SB_ROWS], tri_sum[SB_ROWS:]), sums


def _sb_diag_mask():
    row = lax.broadcasted_iota(jnp.int32, (SB_ROWS, SB_KCHUNK), 0)
    col = lax.broadcasted_iota(jnp.int32, (SB_ROWS, SB_KCHUNK), 1)
    return col < jnp.where(row >= SB_QBLK, row - SB_QBLK, row)


def _sb_fwd(qkv):
    def body(q_ref, k_ref, v_ref, o_ref, tot_ref):
        i = pl.program_id(2)
        krow = lax.broadcasted_iota(jnp.int32, (BLK, BLK), 0)
        kcol = lax.broadcasted_iota(jnp.int32, (BLK, BLK), 1)
        later = (krow > kcol).astype(BF16)
        lane0 = lax.broadcasted_iota(jnp.int32, (SB_QBLK, LANES), 1) < HEAD_DIM
        q2 = _stack_heads(q_ref[0] * QK_SCALE, lane0)

        def chunk(c, carry, causal):
            acc, run = carry
            off = pl.multiple_of(c * SB_KCHUNK, SB_KCHUNK)
            z = _dot_nt(q2, k_ref[0, pl.ds(off, SB_KCHUNK), :])
            log_keep = _log_keep(z)
            if causal is not None:
                log_keep = jnp.where(causal, log_keep, 0.0)
            suffix, sums = _block_sums(log_keep, later)
            log_after = jnp.concatenate([suffix[0] + (run + sums[1]), suffix[1] + run], axis=1)
            a = jnp.exp(log_keep + z + log_after)
            if causal is not None:
                a = jnp.where(causal, a, 0.0)
            acc = acc + _dot(a.astype(BF16), v_ref[0, pl.ds(off, SB_KCHUNK), :])
            return acc, run + (sums[0] + sums[1])

        carry = chunk(i, (jnp.zeros((SB_ROWS, LANES), F32), jnp.zeros((SB_ROWS, 1), F32)), _sb_diag_mask())
        acc, run = lax.fori_loop(0, i, lambda t, c: chunk(i - 1 - t, c, None), carry)
        o_ref[0] = jnp.where(lane0, acc[:SB_QBLK], acc[SB_QBLK:]).astype(BF16)
        tot_ref[0] = jnp.where(lane0, run[:SB_QBLK], run[SB_QBLK:])

    blk = pl.BlockSpec((1, SB_QBLK, LANES), lambda b, h, i: (b, i, h))
    return pl.pallas_call(
        body, name="sb_fwd", grid=(B_LOC, SB_PAIRS, N_SB_STEPS),
        in_specs=[blk,
                  pl.BlockSpec((1, SEQ, LANES), lambda b, h, i: (b, 0, SB_PAIRS + h)),
                  pl.BlockSpec((1, SEQ, LANES), lambda b, h, i: (b, 0, 2 * SB_PAIRS + h))],
        out_specs=[blk, blk],
        out_shape=[jax.ShapeDtypeStruct((B_LOC, SEQ, SB_WIDTH), BF16),
                   jax.ShapeDtypeStruct((B_LOC, SEQ, SB_WIDTH), F32)],
        compiler_params=_params(("parallel", "parallel", "arbitrary")),
    )(qkv, qkv, qkv)


def _sb_bwd(qkv, d_o, tot, token):
    def body(q_ref, k_ref, v_ref, do_ref, tot_ref, token_ref, dq_ref, dk_ref, dv_ref, dk_acc, dv_acc):
        i = pl.program_id(2)
        krow = lax.broadcasted_iota(jnp.int32, (BLK, BLK), 0)
        kcol = lax.broadcasted_iota(jnp.int32, (BLK, BLK), 1)
        upto = (krow <= kcol).astype(BF16)
        earlier = (krow < kcol).astype(BF16)
        lane0 = lax.broadcasted_iota(jnp.int32, (SB_QBLK, LANES), 1) < HEAD_DIM
        q2 = _stack_heads(q_ref[0] * QK_SCALE, lane0)
        do2 = _stack_heads(do_ref[0], lane0)
        tot = tot_ref[0]
        tot2 = jnp.concatenate([_head_col(tot, lane0), _head_col(tot, jnp.logical_not(lane0))], axis=0)

        @pl.when(i == 0)
        def _():
            dk_acc[...] = jnp.zeros_like(dk_acc)
            dv_acc[...] = jnp.zeros_like(dv_acc)

        def chunk(c, carry, causal):
            dq, pre_keep, pre_e = carry
            off = pl.multiple_of(c * SB_KCHUNK, SB_KCHUNK)
            k_c = k_ref[0, pl.ds(off, SB_KCHUNK), :]
            v_c = v_ref[0, pl.ds(off, SB_KCHUNK), :]
            z = _dot_nt(q2, k_c)
            d_a = _dot_nt(do2, v_c)
            log_keep = _log_keep(z)
            if causal is not None:
                log_keep = jnp.where(causal, log_keep, 0.0)
            log_beta = log_keep + z
            prefix, sums = _block_sums(log_keep, upto)
            inclusive = jnp.concatenate([prefix[0], prefix[1] + sums[0]], axis=1)
            a = jnp.exp(log_beta + ((tot2 - pre_keep) - inclusive))
            if causal is not None:
                a = jnp.where(causal, a, 0.0)
            e = d_a * a
            e_prefix, e_sums = _block_sums(e, earlier)
            before = jnp.concatenate([e_prefix[0] + pre_e, e_prefix[1] + (pre_e + e_sums[0])], axis=1)
            dz = e - (e + before) * jnp.exp(log_beta)
            if causal is not None:
                dz = jnp.where(causal, dz, 0.0)
            dz = dz.astype(BF16)
            dq = dq + _dot(dz, k_c)
            dk_acc[pl.ds(off, SB_KCHUNK), :] += _dot_tn(dz, q2)
            dv_acc[pl.ds(off, SB_KCHUNK), :] += _dot_tn(a.astype(BF16), do2)
            return dq, pre_keep + (sums[0] + sums[1]), pre_e + (e_sums[0] + e_sums[1])

        zero_col = jnp.zeros((SB_ROWS, 1), F32)
        carry = lax.fori_loop(0, i, lambda t, c: chunk(t, c, None),
                              (jnp.zeros((SB_ROWS, LANES), F32), zero_col, zero_col))
        dq, _, _ = chunk(i, carry, _sb_diag_mask())
        dq_ref[0] = (jnp.where(lane0, dq[:SB_QBLK], dq[SB_QBLK:]) * QK_SCALE).astype(BF16)

        @pl.when(i == N_SB_STEPS - 1)
        def _():
            dk_ref[0] = dk_acc[...].astype(BF16)
            dv_ref[0] = dv_acc[...].astype(BF16)

    blk = pl.BlockSpec((1, SB_QBLK, LANES), lambda b, h, i: (b, i, h))
    whole = lambda c: pl.BlockSpec((1, SEQ, LANES), lambda b, h, i: (b, 0, c * SB_PAIRS + h))
    out = jax.ShapeDtypeStruct((B_LOC, SEQ, SB_WIDTH), BF16)
    return pl.pallas_call(
        body, name="sb_bwd", grid=(B_LOC, SB_PAIRS, N_SB_STEPS),
        in_specs=[blk, whole(1), whole(2), blk, blk, pl.BlockSpec((8, LANES), lambda b, h, i: (0, 0))],
        out_specs=[blk, whole(0), whole(0)],
        out_shape=[out, out, out],
        scratch_shapes=[pltpu.VMEM((SEQ, LANES), F32), pltpu.VMEM((SEQ, LANES), F32)],
        compiler_params=_params(("parallel", "parallel", "arbitrary")),
    )(qkv, qkv, qkv, d_o, tot, token)


def _dil_scores(q_h, k_cur, k_prev, slope, dilation, has_prev, row, col):
    dist = (row - col).astype(F32) * float(dilation)
    s_cur = _dot_nt(q_h, k_cur) * QK_SCALE - slope * dist
    s_prev = _dot_nt(q_h, k_prev) * QK_SCALE - slope * (dist + float(BLK * dilation))
    s_cur = jnp.where(col <= row, s_cur, NEG_BIG)
    s_prev = jnp.where(jnp.logical_and(col >= row, has_prev), s_prev, NEG_BIG)
    return s_cur, s_prev


def _dil_slopes(group):
    pair = pl.program_id(1) % 2
    first = float(4 * group + 1) + 2.0 * pair.astype(F32)
    coef = -ALIBI_MAX_BIAS / DIL_HEADS * math.log(2.0)
    return jnp.exp(coef * first), jnp.exp(coef * (first + 1.0))


def _head_col(v, lane_mask):
    return jnp.max(jnp.where(lane_mask, v, NEG_BIG), axis=1, keepdims=True)


def _dil_fwd(q, k, v, group):
    dilation = DIL_PAIRS[group][1]
    length = SEQ // dilation
    nblk = length // BLK

    def body(q_ref, k_ref, v_ref, o_ref, lse_ref):
        row, col, lane0 = _sb_masks()
        slopes = _dil_slopes(group)

        def step(n, _):
            off = pl.multiple_of(n * BLK, BLK)
            off_prev = pl.multiple_of(jnp.maximum(n - 1, 0) * BLK, BLK)
            q_h = _two_heads(q_ref[0, pl.ds(off, BLK), :], lane0)
            k_cur = k_ref[0, pl.ds(off, BLK), :]
            v_cur = v_ref[0, pl.ds(off, BLK), :]
            k_prev = k_ref[0, pl.ds(off_prev, BLK), :]
            v_prev = v_ref[0, pl.ds(off_prev, BLK), :]
            outs, lses = [], []
            for h in range(2):
                s_cur, s_prev = _dil_scores(q_h[h], k_cur, k_prev, slopes[h], dilation, n > 0, row, col)
                m = jnp.maximum(jnp.max(s_cur, axis=1, keepdims=True), jnp.max(s_prev, axis=1, keepdims=True))
                p_cur = jnp.exp(s_cur - m)
                p_prev = jnp.exp(s_prev - m)
                den = jnp.sum(p_cur, axis=1, keepdims=True) + jnp.sum(p_prev, axis=1, keepdims=True)
                num = _dot(p_cur.astype(BF16), v_cur) + _dot(p_prev.astype(BF16), v_prev)
                outs.append(num / den)
                lses.append(m + jnp.log(den))
            o_ref[0, pl.ds(off, BLK), :] = jnp.where(lane0, outs[0], outs[1])
            lse_ref[0, pl.ds(off, BLK), :] = jnp.where(lane0, lses[0], lses[1])
            return 0

        lax.fori_loop(0, nblk, step, 0)

    spec = pl.BlockSpec((1, length, LANES), lambda b, c: (b, 0, c))
    out = jax.ShapeDtypeStruct((B_LOC, length, dilation * DIL_OUT), F32)
    return pl.pallas_call(
        body, name=f"dil_fwd_{group}", grid=(B_LOC, 2 * dilation),
        in_specs=[spec, spec, spec], out_specs=[spec, spec], out_shape=[out, out],
        compiler_params=_params(("parallel", "parallel")),
    )(q, k, v)


def _dil_bwd(q, k, v, d_o, lse, dsum, group):
    dilation = DIL_PAIRS[group][1]
    length = SEQ // dilation
    nblk = length // BLK

    def body(q_ref, k_ref, v_ref, do_ref, lse_ref, dsum_ref, dq_ref, dk_ref, dv_ref, dk_acc, dv_acc):
        row, col, lane0 = _sb_masks()
        lane1 = jnp.logical_not(lane0)
        slopes = _dil_slopes(group)
        dk_acc[...] = jnp.zeros_like(dk_acc)
        dv_acc[...] = jnp.zeros_like(dv_acc)

        def step(n, _):
            off = pl.multiple_of(n * BLK, BLK)
            off_prev = pl.multiple_of(jnp.maximum(n - 1, 0) * BLK, BLK)
            q_h = _two_heads(q_ref[0, pl.ds(off, BLK), :], lane0)
            do_h = _two_heads(do_ref[0, pl.ds(off, BLK), :].astype(BF16), lane0)
            k_cur = k_ref[0, pl.ds(off, BLK), :]
            v_cur = v_ref[0, pl.ds(off, BLK), :]
            k_prev = k_ref[0, pl.ds(off_prev, BLK), :]
            v_prev = v_ref[0, pl.ds(off_prev, BLK), :]
            kc_h = _two_heads(k_cur, lane0)
            kp_h = _two_heads(k_prev, lane0)
            lse_blk = lse_ref[0, pl.ds(off, BLK), :]
            dsum_blk = dsum_ref[0, pl.ds(off, BLK), :]
            dq = jnp.zeros((BLK, LANES), F32)
            dk_c = jnp.zeros((BLK, LANES), F32)
            dk_p = jnp.zeros((BLK, LANES), F32)
            dv_c = jnp.zeros((BLK, LANES), F32)
            dv_p = jnp.zeros((BLK, LANES), F32)
            for h, lanes in enumerate((lane0, lane1)):
                s_cur, s_prev = _dil_scores(q_h[h], k_cur, k_prev, slopes[h], dilation, n > 0, row, col)
                lse_h = _head_col(lse_blk, lanes)
                dsum_h = _head_col(dsum_blk, lanes)
                p_cur = jnp.exp(s_cur - lse_h)
                p_prev = jnp.exp(s_prev - lse_h)
                ds_cur = (p_cur * (_dot_nt(do_h[h], v_cur) - dsum_h) * QK_SCALE).astype(BF16)
                ds_prev = (p_prev * (_dot_nt(do_h[h], v_prev) - dsum_h) * QK_SCALE).astype(BF16)
                dq = dq + _dot(ds_cur, kc_h[h]) + _dot(ds_prev, kp_h[h])
                dk_c = dk_c + _dot_tn(ds_cur, q_h[h])
                dk_p = dk_p + _dot_tn(ds_prev, q_h[h])
                dv_c = dv_c + _dot_tn(p_cur.astype(BF16), do_h[h])
                dv_p = dv_p + _dot_tn(p_prev.astype(BF16), do_h[h])
            dq_ref[0, pl.ds(off, BLK), :] = dq.astype(BF16)
            dk_acc[pl.ds(off, BLK), :] += dk_c
            dv_acc[pl.ds(off, BLK), :] += dv_c
            dk_acc[pl.ds(off_prev, BLK), :] += dk_p
            dv_acc[pl.ds(off_prev, BLK), :] += dv_p
            return 0

        lax.fori_loop(0, nblk, step, 0)
        dk_ref[0] = dk_acc[...].astype(BF16)
        dv_ref[0] = dv_acc[...].astype(BF16)

    spec = pl.BlockSpec((1, length, LANES), lambda b, c: (b, 0, c))
    out = jax.ShapeDtypeStruct((B_LOC, length, dilation * DIL_OUT), BF16)
    return pl.pallas_call(
        body, name=f"dil_bwd_{group}", grid=(B_LOC, 2 * dilation),
        in_specs=[spec] * 6, out_specs=[spec] * 3, out_shape=[out] * 3,
        scratch_shapes=[pltpu.VMEM((length, LANES), F32), pltpu.VMEM((length, LANES), F32)],
        compiler_params=_params(("parallel", "parallel")),
    )(q, k, v, d_o, lse, dsum)


def _dil_combine(outs, lses):
    def body(o0, o1, o2, l0, l1, l2, o_ref, lse_ref):
        ls = (l0[...], l1[...], l2[...])
        m = jnp.maximum(jnp.maximum(ls[0], ls[1]), ls[2])
        w = [jnp.exp(l - m) for l in ls]
        den = w[0] + w[1] + w[2]
        o_ref[...] = (w[0] * o0[...] + w[1] * o1[...] + w[2] * o2[...]) / den
        lse_ref[...] = m + jnp.log(den)

    tile = pl.BlockSpec((512, DIL_OUT), lambda i: (i, 0))
    out = jax.ShapeDtypeStruct((TOK, DIL_OUT), F32)
    return pl.pallas_call(
        body, name="dil_combine", grid=(TOK // 512,),
        in_specs=[tile] * 6, out_specs=[tile, tile], out_shape=[out, out],
        compiler_params=_params(("parallel",)),
    )(*outs, *lses)


def _to_residues(t, dilation):
    return t.reshape(B_LOC, SEQ // dilation, dilation * DIL_OUT)


def _peers():
    x, y, c = lax.axis_index("x"), lax.axis_index("y"), lax.axis_index("c")
    me = 4 * x + 2 * y + c
    peers = []
    for mask in range(1, N_DEV):
        px = 1 - x if mask & 4 else x
        py = 1 - y if mask & 2 else y
        pc = 1 - c if mask & 1 else c
        peers.append(((px, py, pc), 4 * px + 2 * py + pc))
    return me, peers


def _all_gather(shard):
    def body(src_ref, out_ref, send_sems, recv_sems, local_sem):
        me, peers = _peers()
        mine = pltpu.make_async_copy(src_ref, out_ref.at[me], local_sem)
        mine.start()
        sends = []
        for k, (peer, _) in enumerate(peers):
            cp = pltpu.make_async_remote_copy(
                src_ref=src_ref, dst_ref=out_ref.at[me], send_sem=send_sems.at[k], recv_sem=recv_sems.at[k],
                device_id=peer, device_id_type=pl.DeviceIdType.MESH)
            cp.start()
            sends.append(cp)
        for k, (peer, peer_idx) in enumerate(peers):
            pltpu.make_async_remote_copy(
                src_ref=src_ref, dst_ref=out_ref.at[peer_idx], send_sem=send_sems.at[k], recv_sem=recv_sems.at[k],
                device_id=peer, device_id_type=pl.DeviceIdType.MESH).wait_recv()
        for cp in sends:
            cp.wait_send()
        mine.wait()

    return pl.pallas_call(
        body, name="all_gather_weights",
        in_specs=[pl.BlockSpec(memory_space=pl.ANY)],
        out_specs=pl.BlockSpec(memory_space=pl.ANY),
        out_shape=jax.ShapeDtypeStruct((N_DEV,) + shard.shape, shard.dtype),
        scratch_shapes=[pltpu.SemaphoreType.DMA((N_DEV - 1,)), pltpu.SemaphoreType.DMA((N_DEV - 1,)),
                        pltpu.SemaphoreType.DMA],
    )(shard)


_HBM = pl.BlockSpec(memory_space=pltpu.HBM)
_SEM = pl.BlockSpec(memory_space=pltpu.SEMAPHORE)
_ANY = pl.BlockSpec(memory_space=pl.ANY)
_EFFECT = pltpu.SideEffectType.DATAFLOW_SIDE_EFFECTING


def _peer_copy(src_ref, land_ref, send_sems, recv_sems, a, k, peer, src_idx, slot):
    return pltpu.make_async_remote_copy(
        src_ref=src_ref if src_idx is None else src_ref.at[src_idx], dst_ref=land_ref.at[slot],
        send_sem=send_sems.at[a * (N_DEV - 1) + k], recv_sem=recv_sems.at[a * (N_DEV - 1) + k],
        device_id=peer, device_id_type=pl.DeviceIdType.MESH)


def _send_start(srcs, name, scatter, after):
    n = len(srcs)

    def body(*refs):
        send_sems, recv_sems = refs[2 * n + 1], refs[2 * n + 2]
        me, peers = _peers()
        for a in range(n):
            for k, (peer, peer_idx) in enumerate(peers):
                _peer_copy(refs[a], refs[n + a], send_sems, recv_sems, a, k, peer,
                           peer_idx if scatter else None, me).start()
        refs[-1][...] = jnp.zeros_like(refs[-1])

    lands = [lax.empty((N_DEV,) + s.shape[-2:], s.dtype) for s in srcs]
    both = list(srcs) + lands
    out = pl.pallas_call(
        body, name=name,
        out_shape=(pltpu.SemaphoreType.DMA((n * (N_DEV - 1),)), pltpu.SemaphoreType.DMA((n * (N_DEV - 1),)),
                   *[pltpu.HBM(t.shape, t.dtype) for t in both], jax.ShapeDtypeStruct((8, LANES), F32)),
        in_specs=(_HBM,) * (2 * n) + (_ANY,),
        out_specs=(_SEM, _SEM) + (_HBM,) * (2 * n) + (pl.BlockSpec(memory_space=pltpu.VMEM),),
        input_output_aliases={i: 2 + i for i in range(2 * n)},
        compiler_params=pltpu.CompilerParams(has_side_effects=_EFFECT),
    )(*[pltpu.with_memory_space_constraint(t, pltpu.HBM) for t in both], after)
    return out[0], out[1], list(out[2:2 + n]), list(out[2 + n:2 + 2 * n]), out[-1]


def _send_wait(handles, name, scatter, afters):
    send_sems, recv_sems, srcs, lands = handles[:4]
    n = len(srcs)

    def body(*refs):
        send_sems, recv_sems = refs[2 * n], refs[2 * n + 1]
        me, peers = _peers()
        for a in range(n):
            for k, (peer, peer_idx) in enumerate(peers):
                cp = _peer_copy(refs[a], refs[n + a], send_sems, recv_sems, a, k, peer,
                                peer_idx if scatter else None, peer_idx)
                cp.wait_send()
                cp.wait_recv()

    both = list(srcs) + list(lands)
    out = pl.pallas_call(
        body, name=name,
        out_shape=tuple(pltpu.HBM(t.shape, t.dtype) for t in both),
        in_specs=(_HBM,) * (2 * n) + (_SEM, _SEM) + (_ANY,) * len(afters),
        out_specs=(_HBM,) * (2 * n),
        input_output_aliases={i: i for i in range(2 * n)},
        compiler_params=pltpu.CompilerParams(has_side_effects=_EFFECT),
    )(*both, send_sems, recv_sems, *afters)
    return list(out[:n]), list(out[n:])


def _sum_in_device_order(me, land_ref, own):
    acc = None
    for j in range(N_DEV):
        term = jnp.where(me == j, own, land_ref[j]).astype(F32)
        acc = term if acc is None else acc + term
    return acc


def _adam_math(w, g, m, v):
    c1 = 1.0 - ADAM_B1 ** ADAM_STEP
    c2 = 1.0 - ADAM_B2 ** ADAM_STEP
    m_new = ADAM_B1 * m + (1.0 - ADAM_B1) * g
    v_new = ADAM_B2 * v + (1.0 - ADAM_B2) * (g * g)
    delta = -ADAM_LR * ((m_new / c1) / (jnp.sqrt(v_new / c2) + ADAM_EPS) + ADAM_WD * w)
    return delta, m_new, v_new


def _row_tile(rows):
    return max(t for t in range(8, 257, 8) if rows % t == 0) if rows % 8 == 0 else rows


def _sum_update(land, blocks, me, w, m, v, name):
    _, rows, cols = land.shape
    tile_rows = _row_tile(rows)

    def body(me_ref, land_ref, own_ref, w_ref, m_ref, v_ref, g_ref, d_ref, nm_ref, nv_ref):
        g = _sum_in_device_order(me_ref[0], land_ref, own_ref[...])
        g_ref[...] = g
        d_ref[...], nm_ref[...], nv_ref[...] = _adam_math(w_ref[...], g, m_ref[...], v_ref[...])

    tile = pl.BlockSpec((tile_rows, cols), lambda i, me_ref: (i, 0))
    out = jax.ShapeDtypeStruct((rows, cols), F32)
    return pl.pallas_call(
        body, name=name,
        grid_spec=pltpu.PrefetchScalarGridSpec(
            num_scalar_prefetch=1, grid=(rows // tile_rows,),
            in_specs=[pl.BlockSpec((N_DEV, tile_rows, cols), lambda i, me_ref: (0, i, 0)),
                      pl.BlockSpec((None, tile_rows, cols), lambda i, me_ref: (me_ref[0], i, 0)),
                      tile, tile, tile],
            out_specs=[tile] * 4),
        out_shape=[out] * 4,
        compiler_params=_params(("parallel",)),
    )(me, land, blocks, w, m, v)


def _sum_gains(land, own, me):
    def body(me_ref, land_ref, own_ref, o_ref):
        o_ref[...] = _sum_in_device_order(me_ref[0], land_ref, own_ref[...])

    return pl.pallas_call(
        body, name="sum_gain_grads",
        grid_spec=pltpu.PrefetchScalarGridSpec(
            num_scalar_prefetch=1, grid=(1,),
            in_specs=[pl.BlockSpec(land.shape, lambda i, me_ref: (0, 0, 0)),
                      pl.BlockSpec(own.shape, lambda i, me_ref: (0, 0))],
            out_specs=pl.BlockSpec(own.shape, lambda i, me_ref: (0, 0))),
        out_shape=jax.ShapeDtypeStruct(own.shape, F32),
    )(me, land, own)


def _adamw(w, g, m, v, name):
    def body(w_ref, g_ref, m_ref, v_ref, d_ref, nm_ref, nv_ref):
        d_ref[...], nm_ref[...], nv_ref[...] = _adam_math(w_ref[...], g_ref[...], m_ref[...], v_ref[...])

    whole = pl.BlockSpec(w.shape, lambda i: (0, 0))
    out = jax.ShapeDtypeStruct(w.shape, F32)
    return pl.pallas_call(
        body, name=name, grid=(1,),
        in_specs=[whole] * 4, out_specs=[whole] * 3, out_shape=[out] * 3,
    )(w, g, m, v)


GROUP_REST = ("w_sb_up", "w_dil_up", "w_out", "w_ffn_in", "w_ffn_out")
COL_SHARDED = ("w_in", "w_sb_up", "w_dil_up", "w_ffn_in")


def _full_from_shards(name, slots):
    _, r, c = slots.shape
    if name in COL_SHARDED:
        return slots.transpose(1, 0, 2).reshape(r, N_DEV * c)
    return slots.reshape(N_DEV * r, c)


def _shards_from_full(name, full):
    rows, cols = full.shape
    if name in COL_SHARDED:
        return full.reshape(rows, N_DEV, cols // N_DEV).transpose(1, 0, 2)
    return full.reshape(N_DEV, rows // N_DEV, cols)


def _local_step(x, target, g_mix, g_ffn, g_fin, w_in, take_rest, give_rest):
    w = {"w_in": w_in}
    qkv, gates, u = _norm_proj(x, g_mix, w["w_in"])
    qkv3 = qkv.reshape(B_LOC, SEQ, QKV_WIDTH)
    o_sb, sb_tot = _sb_fwd(qkv3)
    o_sb = o_sb.reshape(TOK, SB_WIDTH)

    dil_in, dil_o, dil_lse = [], [], []
    for grp, (_, dilation) in enumerate(DIL_PAIRS):
        parts = []
        for part in range(3):
            c0 = 3 * SB_WIDTH + part * DIL_WIDTH + grp * DIL_OUT
            parts.append(_to_residues(qkv[:, c0:c0 + DIL_OUT], dilation))
        dil_in.append(parts)
        o_g, lse_g = _dil_fwd(*parts, grp)
        dil_o.append(o_g.reshape(TOK, DIL_OUT))
        dil_lse.append(lse_g.reshape(TOK, DIL_OUT))
    o_dl, lse = _dil_combine(dil_o, dil_lse)

    w.update(take_rest(o_sb, o_dl))
    x1, merged = _mix_out(x, o_sb, o_dl, gates, w["w_sb_up"], w["w_dil_up"], w["w_out"])
    loss, dx1, u2, act, dh, dx2, dg_fin, dg_ffn = _ffn_fwd_bwd(x1, target, g_ffn, g_fin, w["w_ffn_in"], w["w_ffn_out"])
    dgates, dy_sb, dy_dl, do_sb, do_dl, dsum = _mix_bwd(dx1, o_sb, o_dl, gates, w["w_sb_up"], w["w_dil_up"], w["w_out"])
    shard_cols = D_MODEL // N_DEV
    token = give_rest({
        "w_sb_up": _atb(o_sb, dy_sb, "grad_w_sb_up", SB_WIDTH, shard_cols, col_blocks=True),
        "w_dil_up": _atb(o_dl.astype(BF16), dy_dl, "grad_w_dil_up", DIL_OUT, shard_cols, col_blocks=True),
        "w_out": _shards_from_full("w_out", _atb(merged, dx1.astype(BF16), "grad_w_out", D_MODEL, D_MODEL)),
        "w_ffn_in": _shards_from_full("w_ffn_in", _atb(u2, dh, "grad_w_ffn_in", D_MODEL, D_FF)),
        "w_ffn_out": _shards_from_full("w_ffn_out", _atb(act, dx2, "grad_w_ffn_out", D_FF // 2, D_MODEL)),
    })

    dq_sb, dk_sb, dv_sb = _sb_bwd(qkv3, do_sb.reshape(B_LOC, SEQ, SB_WIDTH), sb_tot, token)
    dq_dl, dk_dl, dv_dl = [], [], []
    for grp, (_, dilation) in enumerate(DIL_PAIRS):
        dq, dk, dv = _dil_bwd(*dil_in[grp], _to_residues(do_dl, dilation), _to_residues(lse, dilation),
                              _to_residues(dsum, dilation), grp)
        dq_dl.append(dq.reshape(TOK, DIL_OUT))
        dk_dl.append(dk.reshape(TOK, DIL_OUT))
        dv_dl.append(dv.reshape(TOK, DIL_OUT))
    flat = lambda t: t.reshape(TOK, SB_WIDTH)
    dproj = jnp.concatenate([flat(dq_sb), flat(dk_sb), flat(dv_sb), *dq_dl, *dk_dl, *dv_dl, dgates], axis=1)

    grad_x, dg_mix = _proj_bwd(dproj, dx1, x, g_mix, w["w_in"])
    grad_w_in = _shards_from_full("w_in", _atb(u, dproj, "grad_w_in", D_MODEL, IN_WIDTH // 2))
    gain_grads = jnp.concatenate([dg_mix, dg_ffn, dg_fin], axis=0)
    return loss, grad_x, grad_w_in, gain_grads


def kernel(x, norm_mix_g, w_in, w_sb_up, w_dil_up, w_out, norm_ffn_g, w_ffn_in, w_ffn_out, norm_final_g, loss_target, m_norm_mix_g, m_w_in, m_w_sb_up, m_w_dil_up, m_w_out, m_norm_ffn_g, m_w_ffn_in, m_w_ffn_out, m_norm_final_g, v_norm_mix_g, v_w_in, v_w_sb_up, v_w_dil_up, v_w_out, v_norm_ffn_g, v_w_ffn_in, v_w_ffn_out, v_norm_final_g):
    mats = {"w_in": w_in, "w_sb_up": w_sb_up, "w_dil_up": w_dil_up, "w_out": w_out,
            "w_ffn_in": w_ffn_in, "w_ffn_out": w_ffn_out}
    moments_m = {"w_in": m_w_in, "w_sb_up": m_w_sb_up, "w_dil_up": m_w_dil_up, "w_out": m_w_out,
                 "w_ffn_in": m_w_ffn_in, "w_ffn_out": m_w_ffn_out}
    moments_v = {"w_in": v_w_in, "w_sb_up": v_w_sb_up, "w_dil_up": v_w_dil_up, "w_out": v_w_out,
                 "w_ffn_in": v_w_ffn_in, "w_ffn_out": v_w_ffn_out}
    me = 4 * lax.axis_index("x") + 2 * lax.axis_index("y") + lax.axis_index("c")
    me_arr = me.astype(jnp.int32).reshape(1)

    def own_slot(landed, block):
        return lax.dynamic_update_slice(landed, block[None], (me, 0, 0))

    gathered_w_in = _all_gather(w_in[0].astype(BF16))
    rest_handles = _send_start([mats[name][0].astype(BF16) for name in GROUP_REST], "gather_rest_start", False,
                               gathered_w_in)

    def take_rest(o_sb, o_dl):
        sent, landed = _send_wait(rest_handles, "gather_rest_wait", False, (o_sb, o_dl))
        return {name: _full_from_shards(name, own_slot(slots, shard))
                for name, shard, slots in zip(GROUP_REST, sent, landed)}

    grad_handles = []

    def give_rest(grads):
        handles = _send_start([grads[name] for name in GROUP_REST], "grads_rest_start", True, grads["w_out"])
        grad_handles.append(handles)
        return handles[4]

    g_fin = norm_final_g.reshape(1, D_MODEL)
    loss, grad_x, grad_w_in, gain_grads = _local_step(
        x.reshape(TOK, D_MODEL), loss_target.reshape(TOK, D_MODEL), norm_mix_g + rest_handles[4][:1, :1],
        norm_ffn_g, g_fin, _full_from_shards("w_in", gathered_w_in), take_rest, give_rest)

    out_g, out_d, out_m, out_v = {}, {}, {}, {}

    def update(names, sent, landed):
        for name, blocks, slots in zip(names, sent, landed):
            g, d, nm, nv = _sum_update(slots, blocks, me_arr, mats[name][0], moments_m[name][0], moments_v[name][0],
                                       "update_" + name)
            out_g[name], out_d[name], out_m[name], out_v[name] = g[None], d[None], nm[None], nv[None]

    sent, landed = _send_wait(grad_handles[0], "grads_rest_wait", True, (grad_w_in,))
    first_handles = _send_start([grad_w_in], "grads_first_start", True, landed[0])
    gain_rows = jnp.concatenate([gain_grads, jnp.tile(loss, (1, D_MODEL // LANES)),
                                 jnp.zeros((8 - 4, D_MODEL), F32)], axis=0)
    gain_handles = _send_start([gain_rows], "gains_start", False, landed[0])
    update(GROUP_REST, sent, landed)
    sent, landed = _send_wait(first_handles, "grads_first_wait", True, (out_d["w_ffn_in"],))
    update(("w_in",), sent, landed)
    sent, landed = _send_wait(gain_handles, "gains_wait", False, (out_d["w_ffn_in"],))
    g_gains = _sum_gains(landed[0], sent[0], me_arr)

    gain_w = jnp.concatenate([norm_mix_g, norm_ffn_g, g_fin], axis=0)
    gain_m = jnp.concatenate([m_norm_mix_g, m_norm_ffn_g, m_norm_final_g.reshape(1, D_MODEL)], axis=0)
    gain_v = jnp.concatenate([v_norm_mix_g, v_norm_ffn_g, v_norm_final_g.reshape(1, D_MODEL)], axis=0)
    gd, gm, gv = _adamw(gain_w, g_gains[:3], gain_m, gain_v, "adamw_gains")
    for idx, name in enumerate(("norm_mix_g", "norm_ffn_g", "norm_final_g")):
        shape = (D_MODEL,) if name == "norm_final_g" else (1, D_MODEL)
        out_g[name] = g_gains[idx].reshape(shape)
        out_d[name], out_m[name], out_v[name] = gd[idx].reshape(shape), gm[idx].reshape(shape), gv[idx].reshape(shape)

    order = ("norm_mix_g", "w_in", "w_sb_up", "w_dil_up", "w_out", "norm_ffn_g", "w_ffn_in", "w_ffn_out",
             "norm_final_g")
    return (g_gains[3, 0], grad_x.reshape(B_LOC, SEQ, D_MODEL),
            *[out_g[n] for n in order], *[out_d[n] for n in order],
            *[out_m[n] for n in order], *[out_v[n] for n in order])
```

```python
import math

import jax
import jax.numpy as jnp
from jax import lax
from jax.experimental import pallas as pl
from jax.experimental.pallas import tpu as pltpu

F32 = jnp.float32
BF16 = jnp.bfloat16

N_DEV = 8
D_MODEL = 1024
SEQ = 2048
B_LOC = 2
TOK = B_LOC * SEQ
HEAD_DIM = 64
SB_WIDTH = 512
DIL_WIDTH = 768
DIL_OUT = 256
QKV_WIDTH = 3 * SB_WIDTH + 3 * DIL_WIDTH
IN_WIDTH = QKV_WIDTH + 2 * D_MODEL
D_FF = 2816
DIL_PAIRS = ((128, 1), (512, 4), (2048, 16))
DIL_HEADS = 12
RMS_EPS = 1e-6
ALIBI_MAX_BIAS = 8.0
QK_SCALE = 1.0 / math.sqrt(HEAD_DIM)
BLK = 128
LANES = 128
NEG_BIG = -1e30

ADAM_LR = 0.001
ADAM_B1 = 0.9
ADAM_B2 = 0.999
ADAM_EPS = 1e-08
ADAM_WD = 0.01
ADAM_STEP = 10

VMEM_LIMIT = 56 * 1024 * 1024


def _dot(a, b):
    return jnp.dot(a, b, preferred_element_type=F32)


def _dot_nt(a, b):
    return lax.dot_general(a, b, (((1,), (1,)), ((), ())), preferred_element_type=F32)


def _dot_tn(a, b):
    return lax.dot_general(a, b, (((0,), (0,)), ((), ())), preferred_element_type=F32)


def _softplus(z):
    return jnp.maximum(z, 0.0) + jnp.log1p(jnp.exp(-jnp.abs(z)))


def _sigmoid(z):
    return 1.0 / (1.0 + jnp.exp(-z))


def _split_bf16(v):
    hi = v.astype(BF16)
    lo = (v - hi.astype(F32)).astype(BF16)
    return hi, lo


def _chunks(width, step=512):
    out, c = [], 0
    while c < width:
        w = min(step, width - c)
        out.append((c, w))
        c += w
    return out


def _resident(shape):
    nd = len(shape)
    return pl.BlockSpec(shape, lambda *_: (0,) * nd, pipeline_mode=pl.Buffered(1))


def _params(sem):
    return pltpu.CompilerParams(dimension_semantics=sem, vmem_limit_bytes=VMEM_LIMIT)


def _rms_fwd(x, g):
    r = lax.rsqrt(jnp.mean(x * x, axis=-1, keepdims=True) + RMS_EPS)
    n = x * r
    return n, r, n * g


def _rms_bwd(dy, n, r, g):
    dg = jnp.sum(dy * n, axis=0, keepdims=True)
    dn = dy * g
    dx = r * (dn - n * jnp.mean(dn * n, axis=-1, keepdims=True))
    return dx, dg


TM = 256


def _norm_proj(x, g, w_in):
    def body(x_ref, g_ref, w_ref, qkv_ref, gate_ref, u_ref):
        _, _, u = _rms_fwd(x_ref[...], g_ref[...])
        u = u.astype(BF16)
        u_ref[...] = u
        for c0, w in _chunks(QKV_WIDTH):
            qkv_ref[:, c0:c0 + w] = _dot(u, w_ref[:, c0:c0 + w]).astype(BF16)
        for c0, w in _chunks(2 * D_MODEL):
            gate_ref[:, c0:c0 + w] = _dot(u, w_ref[:, QKV_WIDTH + c0:QKV_WIDTH + c0 + w])

    return pl.pallas_call(
        body, name="norm_proj", grid=(TOK // TM,),
        in_specs=[pl.BlockSpec((TM, D_MODEL), lambda i: (i, 0)), _resident((1, D_MODEL)),
                  _resident((D_MODEL, IN_WIDTH))],
        out_specs=[pl.BlockSpec((TM, QKV_WIDTH), lambda i: (i, 0)),
                   pl.BlockSpec((TM, 2 * D_MODEL), lambda i: (i, 0)),
                   pl.BlockSpec((TM, D_MODEL), lambda i: (i, 0))],
        out_shape=[jax.ShapeDtypeStruct((TOK, QKV_WIDTH), BF16),
                   jax.ShapeDtypeStruct((TOK, 2 * D_MODEL), F32),
                   jax.ShapeDtypeStruct((TOK, D_MODEL), BF16)],
        compiler_params=_params(("parallel",)),
    )(x, g, w_in)


def _mix_out(x, o_sb, o_dl, gates, w_sb_up, w_dil_up, w_out):
    def body(x_ref, osb_ref, odl_ref, gate_ref, wsb_ref, wdl_ref, wout_ref, x1_ref, mg_ref):
        y_sb = _dot(osb_ref[...], wsb_ref[...])
        y_dl = _dot(odl_ref[...].astype(BF16), wdl_ref[...])
        merged = (_sigmoid(gate_ref[:, :D_MODEL]) * y_sb
                  + _sigmoid(gate_ref[:, D_MODEL:]) * y_dl).astype(BF16)
        mg_ref[...] = merged
        x1_ref[...] = x_ref[...] + _dot(merged, wout_ref[...])

    return pl.pallas_call(
        body, name="mix_out", grid=(TOK // TM,),
        in_specs=[pl.BlockSpec((TM, D_MODEL), lambda i: (i, 0)),
                  pl.BlockSpec((TM, SB_WIDTH), lambda i: (i, 0)),
                  pl.BlockSpec((TM, DIL_OUT), lambda i: (i, 0)),
                  pl.BlockSpec((TM, 2 * D_MODEL), lambda i: (i, 0)),
                  _resident((SB_WIDTH, D_MODEL)), _resident((DIL_OUT, D_MODEL)),
                  _resident((D_MODEL, D_MODEL))],
        out_specs=[pl.BlockSpec((TM, D_MODEL), lambda i: (i, 0)),
                   pl.BlockSpec((TM, D_MODEL), lambda i: (i, 0))],
        out_shape=[jax.ShapeDtypeStruct((TOK, D_MODEL), F32),
                   jax.ShapeDtypeStruct((TOK, D_MODEL), BF16)],
        compiler_params=_params(("parallel",)),
    )(x, o_sb, o_dl, gates, w_sb_up, w_dil_up, w_out)


FF_CHUNK = D_FF // 2


def _ffn_fwd_bwd(x1, target, g_ffn, g_fin, w_ffn_in, w_ffn_out):
    def body(x1_ref, t_ref, gffn_ref, gfin_ref, win_ref, wout_ref,
             loss_ref, dx1_ref, u2_ref, act_ref, dh_ref, dx2_ref, dgfin_ref, dgffn_ref, h_scr):
        i = pl.program_id(0)

        @pl.when(i == 0)
        def _():
            loss_ref[...] = jnp.zeros_like(loss_ref)
            dgfin_ref[...] = jnp.zeros_like(dgfin_ref)
            dgffn_ref[...] = jnp.zeros_like(dgffn_ref)

        x1 = x1_ref[...]
        g_ffn_v = gffn_ref[...]
        g_fin_v = gfin_ref[...]
        n2, r2, u2 = _rms_fwd(x1, g_ffn_v)
        u2 = u2.astype(BF16)
        u2_ref[...] = u2
        x2 = x1
        for c0 in range(0, D_FF, FF_CHUNK):
            gate = _dot(u2, win_ref[:, c0:c0 + FF_CHUNK])
            up = _dot(u2, win_ref[:, D_FF + c0:D_FF + c0 + FF_CHUNK])
            h_scr[:, c0:c0 + FF_CHUNK] = gate
            h_scr[:, D_FF + c0:D_FF + c0 + FF_CHUNK] = up
            act = (gate * _sigmoid(gate) * up).astype(BF16)
            act_ref[:, c0:c0 + FF_CHUNK] = act
            x2 = x2 + _dot(act, wout_ref[c0:c0 + FF_CHUNK, :])
        n3, r3, y = _rms_fwd(x2, g_fin_v)
        err = y - t_ref[...]
        sq = jnp.sum(jnp.sum(err * err, axis=1, keepdims=True), axis=0, keepdims=True)
        loss_ref[...] += sq * (0.5 / D_MODEL)
        dx2, dgfin = _rms_bwd(err * (1.0 / D_MODEL), n3, r3, g_fin_v)
        dgfin_ref[...] += dgfin
        dx2_b = dx2.astype(BF16)
        dx2_ref[...] = dx2_b
        du2 = jnp.zeros((TM, D_MODEL), F32)
        for c0 in range(0, D_FF, FF_CHUNK):
            gate = h_scr[:, c0:c0 + FF_CHUNK]
            up = h_scr[:, D_FF + c0:D_FF + c0 + FF_CHUNK]
            dact = _dot_nt(dx2_b, wout_ref[c0:c0 + FF_CHUNK, :])
            sg = _sigmoid(gate)
            dgate = (dact * up * (sg * (1.0 + gate * (1.0 - sg)))).astype(BF16)
            dup = (dact * (gate * sg)).astype(BF16)
            dh_ref[:, c0:c0 + FF_CHUNK] = dgate
            dh_ref[:, D_FF + c0:D_FF + c0 + FF_CHUNK] = dup
            du2 = du2 + _dot_nt(dgate, win_ref[:, c0:c0 + FF_CHUNK])
            du2 = du2 + _dot_nt(dup, win_ref[:, D_FF + c0:D_FF + c0 + FF_CHUNK])
        dx1_n, dgffn = _rms_bwd(du2, n2, r2, g_ffn_v)
        dgffn_ref[...] += dgffn
        dx1_ref[...] = dx2 + dx1_n

    tile = lambda w: pl.BlockSpec((TM, w), lambda i: (i, 0))
    acc = lambda w: pl.BlockSpec((1, w), lambda i: (0, 0))
    return pl.pallas_call(
        body, name="ffn_fwd_bwd", grid=(TOK // TM,),
        in_specs=[tile(D_MODEL), tile(D_MODEL), _resident((1, D_MODEL)), _resident((1, D_MODEL)),
                  _resident((D_MODEL, 2 * D_FF)), _resident((D_FF, D_MODEL))],
        out_specs=[acc(LANES), tile(D_MODEL), tile(D_MODEL), tile(D_FF), tile(2 * D_FF), tile(D_MODEL),
                   acc(D_MODEL), acc(D_MODEL)],
        out_shape=[jax.ShapeDtypeStruct((1, LANES), F32),
                   jax.ShapeDtypeStruct((TOK, D_MODEL), F32),
                   jax.ShapeDtypeStruct((TOK, D_MODEL), BF16),
                   jax.ShapeDtypeStruct((TOK, D_FF), BF16),
                   jax.ShapeDtypeStruct((TOK, 2 * D_FF), BF16),
                   jax.ShapeDtypeStruct((TOK, D_MODEL), BF16),
                   jax.ShapeDtypeStruct((1, D_MODEL), F32),
                   jax.ShapeDtypeStruct((1, D_MODEL), F32)],
        scratch_shapes=[pltpu.VMEM((TM, 2 * D_FF), F32)],
        compiler_params=_params(("arbitrary",)),
    )(x1, target, g_ffn, g_fin, w_ffn_in, w_ffn_out)


def _mix_bwd(dx1, o_sb, o_dl, gates, w_sb_up, w_dil_up, w_out):
    def body(dx1_ref, osb_ref, odl_ref, gate_ref, wsb_ref, wdl_ref, wout_ref,
             dgate_ref, dysb_ref, dydl_ref, dosb_ref, dodl_ref, dsum_ref):
        dmerged = _dot_nt(dx1_ref[...].astype(BF16), wout_ref[...])
        o_dl = odl_ref[...]
        y_sb = _dot(osb_ref[...], wsb_ref[...])
        y_dl = _dot(o_dl.astype(BF16), wdl_ref[...])
        s_sb = _sigmoid(gate_ref[:, :D_MODEL])
        s_dl = _sigmoid(gate_ref[:, D_MODEL:])
        dgate_ref[:, :D_MODEL] = (dmerged * y_sb * (s_sb * (1.0 - s_sb))).astype(BF16)
        dgate_ref[:, D_MODEL:] = (dmerged * y_dl * (s_dl * (1.0 - s_dl))).astype(BF16)
        dy_sb = (dmerged * s_sb).astype(BF16)
        dy_dl = (dmerged * s_dl).astype(BF16)
        dysb_ref[...] = dy_sb
        dydl_ref[...] = dy_dl
        dosb_ref[...] = _dot_nt(dy_sb, wsb_ref[...]).astype(BF16)
        do_dl = _dot_nt(dy_dl, wdl_ref[...])
        dodl_ref[...] = do_dl
        row = lax.broadcasted_iota(jnp.int32, (DIL_OUT, DIL_OUT), 0) // HEAD_DIM
        col = lax.broadcasted_iota(jnp.int32, (DIL_OUT, DIL_OUT), 1) // HEAD_DIM
        same_head = (row == col).astype(BF16)
        hi, lo = _split_bf16(do_dl * o_dl)
        dsum_ref[...] = _dot(hi, same_head) + _dot(lo, same_head)

    tile = lambda w: pl.BlockSpec((TM, w), lambda i: (i, 0))
    return pl.pallas_call(
        body, name="mix_bwd", grid=(TOK // TM,),
        in_specs=[tile(D_MODEL), tile(SB_WIDTH), tile(DIL_OUT), tile(2 * D_MODEL),
                  _resident((SB_WIDTH, D_MODEL)), _resident((DIL_OUT, D_MODEL)),
                  _resident((D_MODEL, D_MODEL))],
        out_specs=[tile(2 * D_MODEL), tile(D_MODEL), tile(D_MODEL), tile(SB_WIDTH), tile(DIL_OUT),
                   tile(DIL_OUT)],
        out_shape=[jax.ShapeDtypeStruct((TOK, 2 * D_MODEL), BF16),
                   jax.ShapeDtypeStruct((TOK, D_MODEL), BF16),
                   jax.ShapeDtypeStruct((TOK, D_MODEL), BF16),
                   jax.ShapeDtypeStruct((TOK, SB_WIDTH), BF16),
                   jax.ShapeDtypeStruct((TOK, DIL_OUT), F32),
                   jax.ShapeDtypeStruct((TOK, DIL_OUT), F32)],
        compiler_params=_params(("parallel",)),
    )(dx1, o_sb, o_dl, gates, w_sb_up, w_dil_up, w_out)


def _proj_bwd(dproj, dx1, x, g, w_in):
    def body(dp_ref, dx1_ref, x_ref, g_ref, w_ref, dx_ref, dg_ref):
        @pl.when(pl.program_id(0) == 0)
        def _():
            dg_ref[...] = jnp.zeros_like(dg_ref)

        du = jnp.zeros((TM, D_MODEL), F32)
        for c0, w in _chunks(IN_WIDTH, 1024):
            du = du + _dot_nt(dp_ref[:, c0:c0 + w], w_ref[:, c0:c0 + w])
        g_v = g_ref[...]
        n, r, _ = _rms_fwd(x_ref[...], g_v)
        dx, dg = _rms_bwd(du, n, r, g_v)
        dg_ref[...] += dg
        dx_ref[...] = dx1_ref[...] + dx

    tile = lambda w: pl.BlockSpec((TM, w), lambda i: (i, 0))
    return pl.pallas_call(
        body, name="proj_bwd", grid=(TOK // TM,),
        in_specs=[tile(IN_WIDTH), tile(D_MODEL), tile(D_MODEL), _resident((1, D_MODEL)),
                  _resident((D_MODEL, IN_WIDTH))],
        out_specs=[tile(D_MODEL), pl.BlockSpec((1, D_MODEL), lambda i: (0, 0))],
        out_shape=[jax.ShapeDtypeStruct((TOK, D_MODEL), F32),
                   jax.ShapeDtypeStruct((1, D_MODEL), F32)],
        compiler_params=_params(("arbitrary",)),
    )(dproj, dx1, x, g, w_in)


def _atb(a, b, name, tm, tn, col_blocks=0, tk=512):
    m, n = a.shape[1], b.shape[1]
    nk = TOK // tk

    def body(a_ref, b_ref, o_ref, acc_ref):
        k = pl.program_id(2)

        @pl.when(k == 0)
        def _():
            acc_ref[...] = jnp.zeros_like(acc_ref)

        acc_ref[...] += _dot_tn(a_ref[...], b_ref[...])

        @pl.when(k == nk - 1)
        def _():
            if col_blocks:
                width = n // col_blocks
                for blk in range(col_blocks):
                    o_ref[blk] = acc_ref[:, blk * width:(blk + 1) * width].astype(BF16)
            else:
                o_ref[...] = acc_ref[...].astype(BF16)

    if col_blocks:
        out_spec = pl.BlockSpec((col_blocks, tm, n // col_blocks), lambda i, j, k: (0, i, 0))
        out_shape = jax.ShapeDtypeStruct((col_blocks, m, n // col_blocks), BF16)
    else:
        out_spec = pl.BlockSpec((tm, tn), lambda i, j, k: (i, j))
        out_shape = jax.ShapeDtypeStruct((m, n), BF16)
    return pl.pallas_call(
        body, name=name, grid=(m // tm, n // tn, nk),
        in_specs=[pl.BlockSpec((tk, tm), lambda i, j, k: (k, i)),
                  pl.BlockSpec((tk, tn), lambda i, j, k: (k, j))],
        out_specs=out_spec, out_shape=out_shape,
        scratch_shapes=[pltpu.VMEM((tm, tn), F32)],
        compiler_params=_params(("parallel", "parallel", "arbitrary")),
    )(a, b)


SB_PAIRS = SB_WIDTH // LANES


def _sb_masks():
    row = lax.broadcasted_iota(jnp.int32, (BLK, BLK), 0)
    col = lax.broadcasted_iota(jnp.int32, (BLK, BLK), 1)
    return row, col, col < HEAD_DIM


def _two_heads(v, lane0):
    zero = jnp.zeros_like(v)
    return jnp.where(lane0, v, zero), jnp.where(lane0, zero, v)


SB_QBLK = 256
N_SB_STEPS = SEQ // SB_QBLK


SB_KCHUNK = 2 * BLK
SB_ROWS = 2 * SB_QBLK


def _log_keep(z):
    neg_z = -z
    return jnp.minimum(neg_z, 0.0) - jnp.log(1.0 + jnp.exp(jnp.minimum(z, neg_z)))


def _stack_heads(v, lane0):
    return jnp.concatenate(_two_heads(v, lane0), axis=0)


def _block_sums(v, tri):
    halves = (v[:, :BLK], v[:, BLK:])
    hi, lo = _split_bf16(jnp.concatenate(halves, axis=0))
    prod = _dot(jnp.concatenate([hi, lo], axis=0), tri)
    tri_sum = prod[:2 * SB_ROWS] + prod[2 * SB_ROWS:]
    sums = tuple(jnp.sum(h, axis=1, keepdims=True) for h in halves)
    return (tri_sum[:SB_ROWS], tri_sum[SB_ROWS:]), sums


def _sb_diag_mask():
    row = lax.broadcasted_iota(jnp.int32, (SB_ROWS, SB_KCHUNK), 0)
    col = lax.broadcasted_iota(jnp.int32, (SB_ROWS, SB_KCHUNK), 1)
    return col < jnp.where(row >= SB_QBLK, row - SB_QBLK, row)


def _sb_fwd(qkv):
    def body(q_ref, k_ref, v_ref, o_ref, tot_ref):
        i = pl.program_id(2)
        krow = lax.broadcasted_iota(jnp.int32, (BLK, BLK), 0)
        kcol = lax.broadcasted_iota(jnp.int32, (BLK, BLK), 1)
        later = (krow > kcol).astype(BF16)
        lane0 = lax.broadcasted_iota(jnp.int32, (SB_QBLK, LANES), 1) < HEAD_DIM
        q2 = _stack_heads(q_ref[0] * QK_SCALE, lane0)

        def chunk(c, carry, causal):
            acc, run = carry
            off = pl.multiple_of(c * SB_KCHUNK, SB_KCHUNK)
            z = _dot_nt(q2, k_ref[0, pl.ds(off, SB_KCHUNK), :])
            log_keep = _log_keep(z)
            if causal is not None:
                log_keep = jnp.where(causal, log_keep, 0.0)
            suffix, sums = _block_sums(log_keep, later)
            log_after = jnp.concatenate([suffix[0] + (run + sums[1]), suffix[1] + run], axis=1)
            a = jnp.exp(log_keep + z + log_after)
            if causal is not None:
                a = jnp.where(causal, a, 0.0)
            acc = acc + _dot(a.astype(BF16), v_ref[0, pl.ds(off, SB_KCHUNK), :])
            return acc, run + (sums[0] + sums[1])

        carry = chunk(i, (jnp.zeros((SB_ROWS, LANES), F32), jnp.zeros((SB_ROWS, 1), F32)), _sb_diag_mask())
        acc, run = lax.fori_loop(0, i, lambda t, c: chunk(i - 1 - t, c, None), carry)
        o_ref[0] = jnp.where(lane0, acc[:SB_QBLK], acc[SB_QBLK:]).astype(BF16)
        tot_ref[0] = jnp.where(lane0, run[:SB_QBLK], run[SB_QBLK:])

    blk = pl.BlockSpec((1, SB_QBLK, LANES), lambda b, h, i: (b, i, h))
    return pl.pallas_call(
        body, name="sb_fwd", grid=(B_LOC, SB_PAIRS, N_SB_STEPS),
        in_specs=[blk,
                  pl.BlockSpec((1, SEQ, LANES), lambda b, h, i: (b, 0, SB_PAIRS + h)),
                  pl.BlockSpec((1, SEQ, LANES), lambda b, h, i: (b, 0, 2 * SB_PAIRS + h))],
        out_specs=[blk, blk],
        out_shape=[jax.ShapeDtypeStruct((B_LOC, SEQ, SB_WIDTH), BF16),
                   jax.ShapeDtypeStruct((B_LOC, SEQ, SB_WIDTH), F32)],
        compiler_params=_params(("parallel", "parallel", "arbitrary")),
    )(qkv, qkv, qkv)


def _sb_bwd(qkv, d_o, tot, token):
    def body(q_ref, k_ref, v_ref, do_ref, tot_ref, token_ref, dq_ref, dk_ref, dv_ref, dk_acc, dv_acc):
        i = pl.program_id(2)
        krow = lax.broadcasted_iota(jnp.int32, (BLK, BLK), 0)
        kcol = lax.broadcasted_iota(jnp.int32, (BLK, BLK), 1)
        upto = (krow <= kcol).astype(BF16)
        earlier = (krow < kcol).astype(BF16)
        lane0 = lax.broadcasted_iota(jnp.int32, (SB_QBLK, LANES), 1) < HEAD_DIM
        q2 = _stack_heads(q_ref[0] * QK_SCALE, lane0)
        do2 = _stack_heads(do_ref[0], lane0)
        tot = tot_ref[0]
        tot2 = jnp.concatenate([_head_col(tot, lane0), _head_col(tot, jnp.logical_not(lane0))], axis=0)

        @pl.when(i == 0)
        def _():
            dk_acc[...] = jnp.zeros_like(dk_acc)
            dv_acc[...] = jnp.zeros_like(dv_acc)

        def chunk(c, carry, causal):
            dq, pre_keep, pre_e = carry
            off = pl.multiple_of(c * SB_KCHUNK, SB_KCHUNK)
            k_c = k_ref[0, pl.ds(off, SB_KCHUNK), :]
            v_c = v_ref[0, pl.ds(off, SB_KCHUNK), :]
            z = _dot_nt(q2, k_c)
            d_a = _dot_nt(do2, v_c)
            log_keep = _log_keep(z)
            if causal is not None:
                log_keep = jnp.where(causal, log_keep, 0.0)
            log_beta = log_keep + z
            prefix, sums = _block_sums(log_keep, upto)
            inclusive = jnp.concatenate([prefix[0], prefix[1] + sums[0]], axis=1)
            a = jnp.exp(log_beta + ((tot2 - pre_keep) - inclusive))
            if causal is not None:
                a = jnp.where(causal, a, 0.0)
            e = d_a * a
            e_prefix, e_sums = _block_sums(e, earlier)
            before = jnp.concatenate([e_prefix[0] + pre_e, e_prefix[1] + (pre_e + e_sums[0])], axis=1)
            dz = e - (e + before) * jnp.exp(log_beta)
            if causal is not None:
                dz = jnp.where(causal, dz, 0.0)
            dz = dz.astype(BF16)
            dq = dq + _dot(dz, k_c)
            dk_acc[pl.ds(off, SB_KCHUNK), :] += _dot_tn(dz, q2)
            dv_acc[pl.ds(off, SB_KCHUNK), :] += _dot_tn(a.astype(BF16), do2)
            return dq, pre_keep + (sums[0] + sums[1]), pre_e + (e_sums[0] + e_sums[1])

        zero_col = jnp.zeros((SB_ROWS, 1), F32)
        carry = lax.fori_loop(0, i, lambda t, c: chunk(t, c, None),
                              (jnp.zeros((SB_ROWS, LANES), F32), zero_col, zero_col))
        dq, _, _ = chunk(i, carry, _sb_diag_mask())
        dq_ref[0] = (jnp.where(lane0, dq[:SB_QBLK], dq[SB_QBLK:]) * QK_SCALE).astype(BF16)

        @pl.when(i == N_SB_STEPS - 1)
        def _():
            dk_ref[0] = dk_acc[...].astype(BF16)
            dv_ref[0] = dv_acc[...].astype(BF16)

    blk = pl.BlockSpec((1, SB_QBLK, LANES), lambda b, h, i: (b, i, h))
    whole = lambda c: pl.BlockSpec((1, SEQ, LANES), lambda b, h, i: (b, 0, c * SB_PAIRS + h))
    out = jax.ShapeDtypeStruct((B_LOC, SEQ, SB_WIDTH), BF16)
    return pl.pallas_call(
        body, name="sb_bwd", grid=(B_LOC, SB_PAIRS, N_SB_STEPS),
        in_specs=[blk, whole(1), whole(2), blk, blk, pl.BlockSpec((8, LANES), lambda b, h, i: (0, 0))],
        out_specs=[blk, whole(0), whole(0)],
        out_shape=[out, out, out],
        scratch_shapes=[pltpu.VMEM((SEQ, LANES), F32), pltpu.VMEM((SEQ, LANES), F32)],
        compiler_params=_params(("parallel", "parallel", "arbitrary")),
    )(qkv, qkv, qkv, d_o, tot, token)


def _dil_scores(q_h, k_cur, k_prev, slope, dilation, has_prev, row, col):
    dist = (row - col).astype(F32) * float(dilation)
    s_cur = _dot_nt(q_h, k_cur) * QK_SCALE - slope * dist
    s_prev = _dot_nt(q_h, k_prev) * QK_SCALE - slope * (dist + float(BLK * dilation))
    s_cur = jnp.where(col <= row, s_cur, NEG_BIG)
    s_prev = jnp.where(jnp.logical_and(col >= row, has_prev), s_prev, NEG_BIG)
    return s_cur, s_prev


def _dil_slopes(group):
    pair = pl.program_id(1) % 2
    first = float(4 * group + 1) + 2.0 * pair.astype(F32)
    coef = -ALIBI_MAX_BIAS / DIL_HEADS * math.log(2.0)
    return jnp.exp(coef * first), jnp.exp(coef * (first + 1.0))


def _head_col(v, lane_mask):
    return jnp.max(jnp.where(lane_mask, v, NEG_BIG), axis=1, keepdims=True)


def _dil_fwd(q, k, v, group):
    dilation = DIL_PAIRS[group][1]
    length = SEQ // dilation
    nblk = length // BLK

    def body(q_ref, k_ref, v_ref, o_ref, lse_ref):
        row, col, lane0 = _sb_masks()
        slopes = _dil_slopes(group)

        def step(n, _):
            off = pl.multiple_of(n * BLK, BLK)
            off_prev = pl.multiple_of(jnp.maximum(n - 1, 0) * BLK, BLK)
            q_h = _two_heads(q_ref[0, pl.ds(off, BLK), :], lane0)
            k_cur = k_ref[0, pl.ds(off, BLK), :]
            v_cur = v_ref[0, pl.ds(off, BLK), :]
            k_prev = k_ref[0, pl.ds(off_prev, BLK), :]
            v_prev = v_ref[0, pl.ds(off_prev, BLK), :]
            outs, lses = [], []
            for h in range(2):
                s_cur, s_prev = _dil_scores(q_h[h], k_cur, k_prev, slopes[h], dilation, n > 0, row, col)
                m = jnp.maximum(jnp.max(s_cur, axis=1, keepdims=True), jnp.max(s_prev, axis=1, keepdims=True))
                p_cur = jnp.exp(s_cur - m)
                p_prev = jnp.exp(s_prev - m)
                den = jnp.sum(p_cur, axis=1, keepdims=True) + jnp.sum(p_prev, axis=1, keepdims=True)
                num = _dot(p_cur.astype(BF16), v_cur) + _dot(p_prev.astype(BF16), v_prev)
                outs.append(num / den)
                lses.append(m + jnp.log(den))
            o_ref[0, pl.ds(off, BLK), :] = jnp.where(lane0, outs[0], outs[1])
            lse_ref[0, pl.ds(off, BLK), :] = jnp.where(lane0, lses[0], lses[1])
            return 0

        lax.fori_loop(0, nblk, step, 0)

    spec = pl.BlockSpec((1, length, LANES), lambda b, c: (b, 0, c))
    out = jax.ShapeDtypeStruct((B_LOC, length, dilation * DIL_OUT), F32)
    return pl.pallas_call(
        body, name=f"dil_fwd_{group}", grid=(B_LOC, 2 * dilation),
        in_specs=[spec, spec, spec], out_specs=[spec, spec], out_shape=[out, out],
        compiler_params=_params(("parallel", "parallel")),
    )(q, k, v)


def _dil_bwd(q, k, v, d_o, lse, dsum, group):
    dilation = DIL_PAIRS[group][1]
    length = SEQ // dilation
    nblk = length // BLK

    def body(q_ref, k_ref, v_ref, do_ref, lse_ref, dsum_ref, dq_ref, dk_ref, dv_ref, dk_acc, dv_acc):
        row, col, lane0 = _sb_masks()
        lane1 = jnp.logical_not(lane0)
        slopes = _dil_slopes(group)
        dk_acc[...] = jnp.zeros_like(dk_acc)
        dv_acc[...] = jnp.zeros_like(dv_acc)

        def step(n, _):
            off = pl.multiple_of(n * BLK, BLK)
            off_prev = pl.multiple_of(jnp.maximum(n - 1, 0) * BLK, BLK)
            q_h = _two_heads(q_ref[0, pl.ds(off, BLK), :], lane0)
            do_h = _two_heads(do_ref[0, pl.ds(off, BLK), :].astype(BF16), lane0)
            k_cur = k_ref[0, pl.ds(off, BLK), :]
            v_cur = v_ref[0, pl.ds(off, BLK), :]
            k_prev = k_ref[0, pl.ds(off_prev, BLK), :]
            v_prev = v_ref[0, pl.ds(off_prev, BLK), :]
            kc_h = _two_heads(k_cur, lane0)
            kp_h = _two_heads(k_prev, lane0)
            lse_blk = lse_ref[0, pl.ds(off, BLK), :]
            dsum_blk = dsum_ref[0, pl.ds(off, BLK), :]
            dq = jnp.zeros((BLK, LANES), F32)
            dk_c = jnp.zeros((BLK, LANES), F32)
            dk_p = jnp.zeros((BLK, LANES), F32)
            dv_c = jnp.zeros((BLK, LANES), F32)
            dv_p = jnp.zeros((BLK, LANES), F32)
            for h, lanes in enumerate((lane0, lane1)):
                s_cur, s_prev = _dil_scores(q_h[h], k_cur, k_prev, slopes[h], dilation, n > 0, row, col)
                lse_h = _head_col(lse_blk, lanes)
                dsum_h = _head_col(dsum_blk, lanes)
                p_cur = jnp.exp(s_cur - lse_h)
                p_prev = jnp.exp(s_prev - lse_h)
                ds_cur = (p_cur * (_dot_nt(do_h[h], v_cur) - dsum_h) * QK_SCALE).astype(BF16)
                ds_prev = (p_prev * (_dot_nt(do_h[h], v_prev) - dsum_h) * QK_SCALE).astype(BF16)
                dq = dq + _dot(ds_cur, kc_h[h]) + _dot(ds_prev, kp_h[h])
                dk_c = dk_c + _dot_tn(ds_cur, q_h[h])
                dk_p = dk_p + _dot_tn(ds_prev, q_h[h])
                dv_c = dv_c + _dot_tn(p_cur.astype(BF16), do_h[h])
                dv_p = dv_p + _dot_tn(p_prev.astype(BF16), do_h[h])
            dq_ref[0, pl.ds(off, BLK), :] = dq.astype(BF16)
            dk_acc[pl.ds(off, BLK), :] += dk_c
            dv_acc[pl.ds(off, BLK), :] += dv_c
            dk_acc[pl.ds(off_prev, BLK), :] += dk_p
            dv_acc[pl.ds(off_prev, BLK), :] += dv_p
            return 0

        lax.fori_loop(0, nblk, step, 0)
        dk_ref[0] = dk_acc[...].astype(BF16)
        dv_ref[0] = dv_acc[...].astype(BF16)

    spec = pl.BlockSpec((1, length, LANES), lambda b, c: (b, 0, c))
    out = jax.ShapeDtypeStruct((B_LOC, length, dilation * DIL_OUT), BF16)
    return pl.pallas_call(
        body, name=f"dil_bwd_{group}", grid=(B_LOC, 2 * dilation),
        in_specs=[spec] * 6, out_specs=[spec] * 3, out_shape=[out] * 3,
        scratch_shapes=[pltpu.VMEM((length, LANES), F32), pltpu.VMEM((length, LANES), F32)],
        compiler_params=_params(("parallel", "parallel")),
    )(q, k, v, d_o, lse, dsum)


def _dil_combine(outs, lses):
    def body(o0, o1, o2, l0, l1, l2, o_ref, lse_ref):
        ls = (l0[...], l1[...], l2[...])
        m = jnp.maximum(jnp.maximum(ls[0], ls[1]), ls[2])
        w = [jnp.exp(l - m) for l in ls]
        den = w[0] + w[1] + w[2]
        o_ref[...] = (w[0] * o0[...] + w[1] * o1[...] + w[2] * o2[...]) / den
        lse_ref[...] = m + jnp.log(den)

    tile = pl.BlockSpec((512, DIL_OUT), lambda i: (i, 0))
    out = jax.ShapeDtypeStruct((TOK, DIL_OUT), F32)
    return pl.pallas_call(
        body, name="dil_combine", grid=(TOK // 512,),
        in_specs=[tile] * 6, out_specs=[tile, tile], out_shape=[out, out],
        compiler_params=_params(("parallel",)),
    )(*outs, *lses)


def _to_residues(t, dilation):
    return t.reshape(B_LOC, SEQ // dilation, dilation * DIL_OUT)


def _peers():
    x, y, c = lax.axis_index("x"), lax.axis_index("y"), lax.axis_index("c")
    me = 4 * x + 2 * y + c
    peers = []
    for mask in range(1, N_DEV):
        px = 1 - x if mask & 4 else x
        py = 1 - y if mask & 2 else y
        pc = 1 - c if mask & 1 else c
        peers.append(((px, py, pc), 4 * px + 2 * py + pc))
    return me, peers


def _all_gather(shard):
    def body(src_ref, out_ref, send_sems, recv_sems, local_sem):
        me, peers = _peers()
        mine = pltpu.make_async_copy(src_ref, out_ref.at[me], local_sem)
        mine.start()
        sends = []
        for k, (peer, _) in enumerate(peers):
            cp = pltpu.make_async_remote_copy(
                src_ref=src_ref, dst_ref=out_ref.at[me], send_sem=send_sems.at[k], recv_sem=recv_sems.at[k],
                device_id=peer, device_id_type=pl.DeviceIdType.MESH)
            cp.start()
            sends.append(cp)
        for k, (peer, peer_idx) in enumerate(peers):
            pltpu.make_async_remote_copy(
                src_ref=src_ref, dst_ref=out_ref.at[peer_idx], send_sem=send_sems.at[k], recv_sem=recv_sems.at[k],
                device_id=peer, device_id_type=pl.DeviceIdType.MESH).wait_recv()
        for cp in sends:
            cp.wait_send()
        mine.wait()

    return pl.pallas_call(
        body, name="all_gather_weights",
        in_specs=[pl.BlockSpec(memory_space=pl.ANY)],
        out_specs=pl.BlockSpec(memory_space=pl.ANY),
        out_shape=jax.ShapeDtypeStruct((N_DEV,) + shard.shape, shard.dtype),
        scratch_shapes=[pltpu.SemaphoreType.DMA((N_DEV - 1,)), pltpu.SemaphoreType.DMA((N_DEV - 1,)),
                        pltpu.SemaphoreType.DMA],
    )(shard)


_HBM = pl.BlockSpec(memory_space=pltpu.HBM)
_SEM = pl.BlockSpec(memory_space=pltpu.SEMAPHORE)
_ANY = pl.BlockSpec(memory_space=pl.ANY)
_EFFECT = pltpu.SideEffectType.DATAFLOW_SIDE_EFFECTING


def _peer_copy(src_ref, land_ref, send_sems, recv_sems, a, k, peer, src_idx, slot):
    return pltpu.make_async_remote_copy(
        src_ref=src_ref if src_idx is None else src_ref.at[src_idx], dst_ref=land_ref.at[slot],
        send_sem=send_sems.at[a * (N_DEV - 1) + k], recv_sem=recv_sems.at[a * (N_DEV - 1) + k],
        device_id=peer, device_id_type=pl.DeviceIdType.MESH)


def _send_start(srcs, name, scatter, after):
    n = len(srcs)

    def body(*refs):
        send_sems, recv_sems = refs[2 * n + 1], refs[2 * n + 2]
        me, peers = _peers()
        for a in range(n):
            for k, (peer, peer_idx) in enumerate(peers):
                _peer_copy(refs[a], refs[n + a], send_sems, recv_sems, a, k, peer,
                           peer_idx if scatter else None, me).start()
        refs[-1][...] = jnp.zeros_like(refs[-1])

    lands = [lax.empty((N_DEV,) + s.shape[-2:], s.dtype) for s in srcs]
    both = list(srcs) + lands
    out = pl.pallas_call(
        body, name=name,
        out_shape=(pltpu.SemaphoreType.DMA((n * (N_DEV - 1),)), pltpu.SemaphoreType.DMA((n * (N_DEV - 1),)),
                   *[pltpu.HBM(t.shape, t.dtype) for t in both], jax.ShapeDtypeStruct((8, LANES), F32)),
        in_specs=(_HBM,) * (2 * n) + (_ANY,),
        out_specs=(_SEM, _SEM) + (_HBM,) * (2 * n) + (pl.BlockSpec(memory_space=pltpu.VMEM),),
        input_output_aliases={i: 2 + i for i in range(2 * n)},
        compiler_params=pltpu.CompilerParams(has_side_effects=_EFFECT),
    )(*[pltpu.with_memory_space_constraint(t, pltpu.HBM) for t in both], after)
    return out[0], out[1], list(out[2:2 + n]), list(out[2 + n:2 + 2 * n]), out[-1]


def _send_wait(handles, name, scatter, afters):
    send_sems, recv_sems, srcs, lands = handles[:4]
    n = len(srcs)

    def body(*refs):
        send_sems, recv_sems = refs[2 * n], refs[2 * n + 1]
        me, peers = _peers()
        for a in range(n):
            for k, (peer, peer_idx) in enumerate(peers):
                cp = _peer_copy(refs[a], refs[n + a], send_sems, recv_sems, a, k, peer,
                                peer_idx if scatter else None, peer_idx)
                cp.wait_send()
                cp.wait_recv()

    both = list(srcs) + list(lands)
    out = pl.pallas_call(
        body, name=name,
        out_shape=tuple(pltpu.HBM(t.shape, t.dtype) for t in both),
        in_specs=(_HBM,) * (2 * n) + (_SEM, _SEM) + (_ANY,) * len(afters),
        out_specs=(_HBM,) * (2 * n),
        input_output_aliases={i: i for i in range(2 * n)},
        compiler_params=pltpu.CompilerParams(has_side_effects=_EFFECT),
    )(*both, send_sems, recv_sems, *afters)
    return list(out[:n]), list(out[n:])


def _sum_in_device_order(me, land_ref, own):
    acc = None
    for j in range(N_DEV):
        term = jnp.where(me == j, own, land_ref[j]).astype(F32)
        acc = term if acc is None else acc + term
    return acc


def _adam_math(w, g, m, v):
    c1 = 1.0 - ADAM_B1 ** ADAM_STEP
    c2 = 1.0 - ADAM_B2 ** ADAM_STEP
    m_new = ADAM_B1 * m + (1.0 - ADAM_B1) * g
    v_new = ADAM_B2 * v + (1.0 - ADAM_B2) * (g * g)
    delta = -ADAM_LR * ((m_new / c1) / (jnp.sqrt(v_new / c2) + ADAM_EPS) + ADAM_WD * w)
    return delta, m_new, v_new


def _row_tile(rows):
    return max(t for t in range(8, 257, 8) if rows % t == 0) if rows % 8 == 0 else rows


def _sum_update(land, blocks, me, w, m, v, name):
    _, rows, cols = land.shape
    tile_rows = _row_tile(rows)

    def body(me_ref, land_ref, own_ref, w_ref, m_ref, v_ref, g_ref, d_ref, nm_ref, nv_ref):
        g = _sum_in_device_order(me_ref[0], land_ref, own_ref[...])
        g_ref[...] = g
        d_ref[...], nm_ref[...], nv_ref[...] = _adam_math(w_ref[...], g, m_ref[...], v_ref[...])

    tile = pl.BlockSpec((tile_rows, cols), lambda i, me_ref: (i, 0))
    out = jax.ShapeDtypeStruct((rows, cols), F32)
    return pl.pallas_call(
        body, name=name,
        grid_spec=pltpu.PrefetchScalarGridSpec(
            num_scalar_prefetch=1, grid=(rows // tile_rows,),
            in_specs=[pl.BlockSpec((N_DEV, tile_rows, cols), lambda i, me_ref: (0, i, 0)),
                      pl.BlockSpec((None, tile_rows, cols), lambda i, me_ref: (me_ref[0], i, 0)),
                      tile, tile, tile],
            out_specs=[tile] * 4),
        out_shape=[out] * 4,
        compiler_params=_params(("parallel",)),
    )(me, land, blocks, w, m, v)


def _sum_gains(land, own, me):
    def body(me_ref, land_ref, own_ref, o_ref):
        o_ref[...] = _sum_in_device_order(me_ref[0], land_ref, own_ref[...])

    return pl.pallas_call(
        body, name="sum_gain_grads",
        grid_spec=pltpu.PrefetchScalarGridSpec(
            num_scalar_prefetch=1, grid=(1,),
            in_specs=[pl.BlockSpec(land.shape, lambda i, me_ref: (0, 0, 0)),
                      pl.BlockSpec(own.shape, lambda i, me_ref: (0, 0))],
            out_specs=pl.BlockSpec(own.shape, lambda i, me_ref: (0, 0))),
        out_shape=jax.ShapeDtypeStruct(own.shape, F32),
    )(me, land, own)


def _adamw(w, g, m, v, name):
    def body(w_ref, g_ref, m_ref, v_ref, d_ref, nm_ref, nv_ref):
        d_ref[...], nm_ref[...], nv_ref[...] = _adam_math(w_ref[...], g_ref[...], m_ref[...], v_ref[...])

    whole = pl.BlockSpec(w.shape, lambda i: (0, 0))
    out = jax.ShapeDtypeStruct(w.shape, F32)
    return pl.pallas_call(
        body, name=name, grid=(1,),
        in_specs=[whole] * 4, out_specs=[whole] * 3, out_shape=[out] * 3,
    )(w, g, m, v)


GROUP_REST = ("w_sb_up", "w_dil_up", "w_out", "w_ffn_in", "w_ffn_out")
COL_SHARDED = ("w_in", "w_sb_up", "w_dil_up", "w_ffn_in")


def _full_from_shards(name, slots):
    _, r, c = slots.shape
    if name in COL_SHARDED:
        return slots.transpose(1, 0, 2).reshape(r, N_DEV * c)
    return slots.reshape(N_DEV * r, c)


def _shards_from_full(name, full):
    rows, cols = full.shape
    if name in COL_SHARDED:
        return full.reshape(rows, N_DEV, cols // N_DEV).transpose(1, 0, 2)
    return full.reshape(N_DEV, rows // N_DEV, cols)


def _local_step(x, target, g_mix, g_ffn, g_fin, w_in, take_rest, give_rest, give_first):
    w = {"w_in": w_in}
    qkv, gates, u = _norm_proj(x, g_mix, w["w_in"])
    qkv3 = qkv.reshape(B_LOC, SEQ, QKV_WIDTH)
    o_sb, sb_tot = _sb_fwd(qkv3)
    o_sb = o_sb.reshape(TOK, SB_WIDTH)

    dil_in, dil_o, dil_lse = [], [], []
    for grp, (_, dilation) in enumerate(DIL_PAIRS):
        parts = []
        for part in range(3):
            c0 = 3 * SB_WIDTH + part * DIL_WIDTH + grp * DIL_OUT
            parts.append(_to_residues(qkv[:, c0:c0 + DIL_OUT], dilation))
        dil_in.append(parts)
        o_g, lse_g = _dil_fwd(*parts, grp)
        dil_o.append(o_g.reshape(TOK, DIL_OUT))
        dil_lse.append(lse_g.reshape(TOK, DIL_OUT))
    o_dl, lse = _dil_combine(dil_o, dil_lse)

    w.update(take_rest(o_sb, o_dl))
    x1, merged = _mix_out(x, o_sb, o_dl, gates, w["w_sb_up"], w["w_dil_up"], w["w_out"])
    loss, dx1, u2, act, dh, dx2, dg_fin, dg_ffn = _ffn_fwd_bwd(x1, target, g_ffn, g_fin, w["w_ffn_in"], w["w_ffn_out"])
    dgates, dy_sb, dy_dl, do_sb, do_dl, dsum = _mix_bwd(dx1, o_sb, o_dl, gates, w["w_sb_up"], w["w_dil_up"], w["w_out"])
    token = give_rest({
        "w_sb_up": _atb(o_sb, dy_sb, "grad_w_sb_up", SB_WIDTH, D_MODEL, col_blocks=N_DEV),
        "w_dil_up": _atb(o_dl.astype(BF16), dy_dl, "grad_w_dil_up", DIL_OUT, D_MODEL, col_blocks=N_DEV),
        "w_out": _shards_from_full("w_out", _atb(merged, dx1.astype(BF16), "grad_w_out", D_MODEL, D_MODEL)),
        "w_ffn_in": _shards_from_full("w_ffn_in", _atb(u2, dh, "grad_w_ffn_in", D_MODEL, D_FF)),
        "w_ffn_out": _shards_from_full("w_ffn_out", _atb(act, dx2, "grad_w_ffn_out", D_FF // 2, D_MODEL)),
    })

    dq_sb, dk_sb, dv_sb = _sb_bwd(qkv3, do_sb.reshape(B_LOC, SEQ, SB_WIDTH), sb_tot, token)
    dq_dl, dk_dl, dv_dl = [], [], []
    for grp, (_, dilation) in enumerate(DIL_PAIRS):
        dq, dk, dv = _dil_bwd(*dil_in[grp], _to_residues(do_dl, dilation), _to_residues(lse, dilation),
                              _to_residues(dsum, dilation), grp)
        dq_dl.append(dq.reshape(TOK, DIL_OUT))
        dk_dl.append(dk.reshape(TOK, DIL_OUT))
        dv_dl.append(dv.reshape(TOK, DIL_OUT))
    flat = lambda t: t.reshape(TOK, SB_WIDTH)
    dproj = jnp.concatenate([flat(dq_sb), flat(dk_sb), flat(dv_sb), *dq_dl, *dk_dl, *dv_dl, dgates], axis=1)

    token = give_first(_shards_from_full("w_in", _atb(u, dproj, "grad_w_in", D_MODEL, IN_WIDTH // 2)))
    grad_x, dg_mix = _proj_bwd(dproj, dx1, x, g_mix + token[:1, :1], w["w_in"])
    gain_grads = jnp.concatenate([dg_mix, dg_ffn, dg_fin], axis=0)
    return loss, grad_x, gain_grads


def kernel(x, norm_mix_g, w_in, w_sb_up, w_dil_up, w_out, norm_ffn_g, w_ffn_in, w_ffn_out, norm_final_g, loss_target, m_norm_mix_g, m_w_in, m_w_sb_up, m_w_dil_up, m_w_out, m_norm_ffn_g, m_w_ffn_in, m_w_ffn_out, m_norm_final_g, v_norm_mix_g, v_w_in, v_w_sb_up, v_w_dil_up, v_w_out, v_norm_ffn_g, v_w_ffn_in, v_w_ffn_out, v_norm_final_g):
    mats = {"w_in": w_in, "w_sb_up": w_sb_up, "w_dil_up": w_dil_up, "w_out": w_out,
            "w_ffn_in": w_ffn_in, "w_ffn_out": w_ffn_out}
    moments_m = {"w_in": m_w_in, "w_sb_up": m_w_sb_up, "w_dil_up": m_w_dil_up, "w_out": m_w_out,
                 "w_ffn_in": m_w_ffn_in, "w_ffn_out": m_w_ffn_out}
    moments_v = {"w_in": v_w_in, "w_sb_up": v_w_sb_up, "w_dil_up": v_w_dil_up, "w_out": v_w_out,
                 "w_ffn_in": v_w_ffn_in, "w_ffn_out": v_w_ffn_out}
    me = 4 * lax.axis_index("x") + 2 * lax.axis_index("y") + lax.axis_index("c")
    me_arr = me.astype(jnp.int32).reshape(1)

    def own_slot(landed, block):
        return lax.dynamic_update_slice(landed, block[None], (me, 0, 0))

    gathered_w_in = _all_gather(w_in[0].astype(BF16))
    rest_handles = _send_start([mats[name][0].astype(BF16) for name in GROUP_REST], "gather_rest_start", False,
                               gathered_w_in)

    def take_rest(o_sb, o_dl):
        sent, landed = _send_wait(rest_handles, "gather_rest_wait", False, (o_sb, o_dl))
        return {name: _full_from_shards(name, own_slot(slots, shard))
                for name, shard, slots in zip(GROUP_REST, sent, landed)}

    grad_handles = []

    def give_rest(grads):
        handles = _send_start([grads[name] for name in GROUP_REST], "grads_rest_start", True, grads["w_out"])
        grad_handles.append(handles)
        return handles[4]

    rest_landed, first_handles = [], []

    def give_first(blocks):
        rest_landed.append(_send_wait(grad_handles[0], "grads_rest_wait", True, (blocks,)))
        handles = _send_start([blocks], "grads_first_start", True, rest_landed[0][1][0])
        first_handles.append(handles)
        return handles[4]

    g_fin = norm_final_g.reshape(1, D_MODEL)
    loss, grad_x, gain_grads = _local_step(
        x.reshape(TOK, D_MODEL), loss_target.reshape(TOK, D_MODEL), norm_mix_g + rest_handles[4][:1, :1],
        norm_ffn_g, g_fin, _full_from_shards("w_in", gathered_w_in), take_rest, give_rest, give_first)

    out_g, out_d, out_m, out_v = {}, {}, {}, {}

    def update(names, sent, landed):
        for name, blocks, slots in zip(names, sent, landed):
            g, d, nm, nv = _sum_update(slots, blocks, me_arr, mats[name][0], moments_m[name][0], moments_v[name][0],
                                       "update_" + name)
            out_g[name], out_d[name], out_m[name], out_v[name] = g[None], d[None], nm[None], nv[None]

    gain_rows = jnp.concatenate([gain_grads, jnp.tile(loss, (1, D_MODEL // LANES)),
                                 jnp.zeros((8 - 4, D_MODEL), F32)], axis=0)
    gain_handles = _send_start([gain_rows], "gains_start", False, grad_x)
    update(GROUP_REST, *rest_landed[0])
    sent, landed = _send_wait(gain_handles, "gains_wait", False, (out_d["w_ffn_in"],))
    g_gains = _sum_gains(landed[0], sent[0], me_arr)
    sent, landed = _send_wait(first_handles[0], "grads_first_wait", True, (out_d["w_ffn_in"], g_gains))
    update(("w_in",), sent, landed)

    gain_w = jnp.concatenate([norm_mix_g, norm_ffn_g, g_fin], axis=0)
    gain_m = jnp.concatenate([m_norm_mix_g, m_norm_ffn_g, m_norm_final_g.reshape(1, D_MODEL)], axis=0)
    gain_v = jnp.concatenate([v_norm_mix_g, v_norm_ffn_g, v_norm_final_g.reshape(1, D_MODEL)], axis=0)
    gd, gm, gv = _adamw(gain_w, g_gains[:3], gain_m, gain_v, "adamw_gains")
    for idx, name in enumerate(("norm_mix_g", "norm_ffn_g", "norm_final_g")):
        shape = (D_MODEL,) if name == "norm_final_g" else (1, D_MODEL)
        out_g[name] = g_gains[idx].reshape(shape)
        out_d[name], out_m[name], out_v[name] = gd[idx].reshape(shape), gm[idx].reshape(shape), gv[idx].reshape(shape)

    order = ("norm_mix_g", "w_in", "w_sb_up", "w_dil_up", "w_out", "norm_ffn_g", "w_ffn_in", "w_ffn_out",
             "norm_final_g")
    return (g_gains[3, 0], grad_x.reshape(B_LOC, SEQ, D_MODEL),
            *[out_g[n] for n in order], *[out_d[n] for n in order],
            *[out_m[n] for n in order], *[out_v[n] for n in order])
```

```python
import math

import jax
import jax.numpy as jnp
from jax import lax
from jax.experimental import pallas as pl
from jax.experimental.pallas import tpu as pltpu

F32 = jnp.float32
BF16 = jnp.bfloat16

N_DEV = 8
D_MODEL = 1024
SEQ = 2048
B_LOC = 2
TOK = B_LOC * SEQ
HEAD_DIM = 64
SB_WIDTH = 512
DIL_WIDTH = 768
DIL_OUT = 256
QKV_WIDTH = 3 * SB_WIDTH + 3 * DIL_WIDTH
IN_WIDTH = QKV_WIDTH + 2 * D_MODEL
D_FF = 2816
DIL_PAIRS = ((128, 1), (512, 4), (2048, 16))
DIL_HEADS = 12
RMS_EPS = 1e-6
ALIBI_MAX_BIAS = 8.0
QK_SCALE = 1.0 / math.sqrt(HEAD_DIM)
BLK = 128
LANES = 128
NEG_BIG = -1e30

ADAM_LR = 0.001
ADAM_B1 = 0.9
ADAM_B2 = 0.999
ADAM_EPS = 1e-08
ADAM_WD = 0.01
ADAM_STEP = 10

VMEM_LIMIT = 56 * 1024 * 1024


def _dot(a, b):
    return jnp.dot(a, b, preferred_element_type=F32)


def _dot_nt(a, b):
    return lax.dot_general(a, b, (((1,), (1,)), ((), ())), preferred_element_type=F32)


def _dot_tn(a, b):
    return lax.dot_general(a, b, (((0,), (0,)), ((), ())), preferred_element_type=F32)


def _softplus(z):
    return jnp.maximum(z, 0.0) + jnp.log1p(jnp.exp(-jnp.abs(z)))


def _sigmoid(z):
    return 1.0 / (1.0 + jnp.exp(-z))


def _split_bf16(v):
    hi = v.astype(BF16)
    lo = (v - hi.astype(F32)).astype(BF16)
    return hi, lo


def _chunks(width, step=512):
    out, c = [], 0
    while c < width:
        w = min(step, width - c)
        out.append((c, w))
        c += w
    return out


def _resident(shape):
    nd = len(shape)
    return pl.BlockSpec(shape, lambda *_: (0,) * nd, pipeline_mode=pl.Buffered(1))


def _params(sem):
    return pltpu.CompilerParams(dimension_semantics=sem, vmem_limit_bytes=VMEM_LIMIT)


def _rms_fwd(x, g):
    r = lax.rsqrt(jnp.mean(x * x, axis=-1, keepdims=True) + RMS_EPS)
    n = x * r
    return n, r, n * g


def _rms_bwd(dy, n, r, g):
    dg = jnp.sum(dy * n, axis=0, keepdims=True)
    dn = dy * g
    dx = r * (dn - n * jnp.mean(dn * n, axis=-1, keepdims=True))
    return dx, dg


TM = 256


def _norm_proj(x, g, w_in):
    def body(x_ref, g_ref, w_ref, sb_ref, dl_ref, gate_ref, u_ref):
        _, _, u = _rms_fwd(x_ref[...], g_ref[...])
        u = u.astype(BF16)
        u_ref[...] = u
        for c0, w in _chunks(3 * SB_WIDTH):
            sb_ref[:, c0:c0 + w] = _dot(u, w_ref[:, c0:c0 + w]).astype(BF16)
        for c0, w in _chunks(3 * DIL_WIDTH):
            dl_ref[:, c0:c0 + w] = _dot(u, w_ref[:, 3 * SB_WIDTH + c0:3 * SB_WIDTH + c0 + w])
        for c0, w in _chunks(2 * D_MODEL):
            gate_ref[:, c0:c0 + w] = _dot(u, w_ref[:, QKV_WIDTH + c0:QKV_WIDTH + c0 + w])

    return pl.pallas_call(
        body, name="norm_proj", grid=(TOK // TM,),
        in_specs=[pl.BlockSpec((TM, D_MODEL), lambda i: (i, 0)), _resident((1, D_MODEL)),
                  _resident((D_MODEL, IN_WIDTH))],
        out_specs=[pl.BlockSpec((TM, 3 * SB_WIDTH), lambda i: (i, 0)),
                   pl.BlockSpec((TM, 3 * DIL_WIDTH), lambda i: (i, 0)),
                   pl.BlockSpec((TM, 2 * D_MODEL), lambda i: (i, 0)),
                   pl.BlockSpec((TM, D_MODEL), lambda i: (i, 0))],
        out_shape=[jax.ShapeDtypeStruct((TOK, 3 * SB_WIDTH), BF16),
                   jax.ShapeDtypeStruct((TOK, 3 * DIL_WIDTH), F32),
                   jax.ShapeDtypeStruct((TOK, 2 * D_MODEL), F32),
                   jax.ShapeDtypeStruct((TOK, D_MODEL), BF16)],
        compiler_params=_params(("parallel",)),
    )(x, g, w_in)


def _mix_out(x, o_sb, o_dl, gates, w_sb_up, w_dil_up, w_out):
    def body(x_ref, osb_ref, odl_ref, gate_ref, wsb_ref, wdl_ref, wout_ref, x1_ref, mg_ref):
        y_sb = _dot(osb_ref[...], wsb_ref[...])
        y_dl = _dot(odl_ref[...].astype(BF16), wdl_ref[...])
        merged = (_sigmoid(gate_ref[:, :D_MODEL]) * y_sb
                  + _sigmoid(gate_ref[:, D_MODEL:]) * y_dl).astype(BF16)
        mg_ref[...] = merged
        x1_ref[...] = x_ref[...] + _dot(merged, wout_ref[...])

    return pl.pallas_call(
        body, name="mix_out", grid=(TOK // TM,),
        in_specs=[pl.BlockSpec((TM, D_MODEL), lambda i: (i, 0)),
                  pl.BlockSpec((TM, SB_WIDTH), lambda i: (i, 0)),
                  pl.BlockSpec((TM, DIL_OUT), lambda i: (i, 0)),
                  pl.BlockSpec((TM, 2 * D_MODEL), lambda i: (i, 0)),
                  _resident((SB_WIDTH, D_MODEL)), _resident((DIL_OUT, D_MODEL)),
                  _resident((D_MODEL, D_MODEL))],
        out_specs=[pl.BlockSpec((TM, D_MODEL), lambda i: (i, 0)),
                   pl.BlockSpec((TM, D_MODEL), lambda i: (i, 0))],
        out_shape=[jax.ShapeDtypeStruct((TOK, D_MODEL), F32),
                   jax.ShapeDtypeStruct((TOK, D_MODEL), BF16)],
        compiler_params=_params(("parallel",)),
    )(x, o_sb, o_dl, gates, w_sb_up, w_dil_up, w_out)


FF_CHUNK = D_FF // 2


def _ffn_fwd_bwd(x1, target, g_ffn, g_fin, w_ffn_in, w_ffn_out):
    def body(x1_ref, t_ref, gffn_ref, gfin_ref, win_ref, wout_ref,
             loss_ref, dx1_ref, u2_ref, act_ref, dh_ref, dx2_ref, dgfin_ref, dgffn_ref, h_scr):
        i = pl.program_id(0)

        @pl.when(i == 0)
        def _():
            loss_ref[...] = jnp.zeros_like(loss_ref)
            dgfin_ref[...] = jnp.zeros_like(dgfin_ref)
            dgffn_ref[...] = jnp.zeros_like(dgffn_ref)

        x1 = x1_ref[...]
        g_ffn_v = gffn_ref[...]
        g_fin_v = gfin_ref[...]
        n2, r2, u2 = _rms_fwd(x1, g_ffn_v)
        u2 = u2.astype(BF16)
        u2_ref[...] = u2
        x2 = x1
        for c0 in range(0, D_FF, FF_CHUNK):
            gate = _dot(u2, win_ref[:, c0:c0 + FF_CHUNK])
            up = _dot(u2, win_ref[:, D_FF + c0:D_FF + c0 + FF_CHUNK])
            h_scr[:, c0:c0 + FF_CHUNK] = gate
            h_scr[:, D_FF + c0:D_FF + c0 + FF_CHUNK] = up
            act = (gate * _sigmoid(gate) * up).astype(BF16)
            act_ref[:, c0:c0 + FF_CHUNK] = act
            x2 = x2 + _dot(act, wout_ref[c0:c0 + FF_CHUNK, :])
        n3, r3, y = _rms_fwd(x2, g_fin_v)
        err = y - t_ref[...]
        sq = jnp.sum(jnp.sum(err * err, axis=1, keepdims=True), axis=0, keepdims=True)
        loss_ref[...] += sq * (0.5 / D_MODEL)
        dx2, dgfin = _rms_bwd(err * (1.0 / D_MODEL), n3, r3, g_fin_v)
        dgfin_ref[...] += dgfin
        dx2_b = dx2.astype(BF16)
        dx2_ref[...] = dx2_b
        du2 = jnp.zeros((TM, D_MODEL), F32)
        for c0 in range(0, D_FF, FF_CHUNK):
            gate = h_scr[:, c0:c0 + FF_CHUNK]
            up = h_scr[:, D_FF + c0:D_FF + c0 + FF_CHUNK]
            dact = _dot_nt(dx2_b, wout_ref[c0:c0 + FF_CHUNK, :])
            sg = _sigmoid(gate)
            dgate = (dact * up * (sg * (1.0 + gate * (1.0 - sg)))).astype(BF16)
            dup = (dact * (gate * sg)).astype(BF16)
            dh_ref[:, c0:c0 + FF_CHUNK] = dgate
            dh_ref[:, D_FF + c0:D_FF + c0 + FF_CHUNK] = dup
            du2 = du2 + _dot_nt(dgate, win_ref[:, c0:c0 + FF_CHUNK])
            du2 = du2 + _dot_nt(dup, win_ref[:, D_FF + c0:D_FF + c0 + FF_CHUNK])
        dx1_n, dgffn = _rms_bwd(du2, n2, r2, g_ffn_v)
        dgffn_ref[...] += dgffn
        dx1_ref[...] = dx2 + dx1_n

    tile = lambda w: pl.BlockSpec((TM, w), lambda i: (i, 0))
    acc = lambda w: pl.BlockSpec((1, w), lambda i: (0, 0))
    return pl.pallas_call(
        body, name="ffn_fwd_bwd", grid=(TOK // TM,),
        in_specs=[tile(D_MODEL), tile(D_MODEL), _resident((1, D_MODEL)), _resident((1, D_MODEL)),
                  _resident((D_MODEL, 2 * D_FF)), _resident((D_FF, D_MODEL))],
        out_specs=[acc(LANES), tile(D_MODEL), tile(D_MODEL), tile(D_FF), tile(2 * D_FF), tile(D_MODEL),
                   acc(D_MODEL), acc(D_MODEL)],
        out_shape=[jax.ShapeDtypeStruct((1, LANES), F32),
                   jax.ShapeDtypeStruct((TOK, D_MODEL), F32),
                   jax.ShapeDtypeStruct((TOK, D_MODEL), BF16),
                   jax.ShapeDtypeStruct((TOK, D_FF), BF16),
                   jax.ShapeDtypeStruct((TOK, 2 * D_FF), BF16),
                   jax.ShapeDtypeStruct((TOK, D_MODEL), BF16),
                   jax.ShapeDtypeStruct((1, D_MODEL), F32),
                   jax.ShapeDtypeStruct((1, D_MODEL), F32)],
        scratch_shapes=[pltpu.VMEM((TM, 2 * D_FF), F32)],
        compiler_params=_params(("arbitrary",)),
    )(x1, target, g_ffn, g_fin, w_ffn_in, w_ffn_out)


def _mix_bwd(dx1, o_sb, o_dl, gates, w_sb_up, w_dil_up, w_out):
    def body(dx1_ref, osb_ref, odl_ref, gate_ref, wsb_ref, wdl_ref, wout_ref,
             dgate_ref, dysb_ref, dydl_ref, dosb_ref, dodl_ref, dsum_ref):
        dmerged = _dot_nt(dx1_ref[...].astype(BF16), wout_ref[...])
        o_dl = odl_ref[...]
        y_sb = _dot(osb_ref[...], wsb_ref[...])
        y_dl = _dot(o_dl.astype(BF16), wdl_ref[...])
        s_sb = _sigmoid(gate_ref[:, :D_MODEL])
        s_dl = _sigmoid(gate_ref[:, D_MODEL:])
        dgate_ref[:, :D_MODEL] = (dmerged * y_sb * (s_sb * (1.0 - s_sb))).astype(BF16)
        dgate_ref[:, D_MODEL:] = (dmerged * y_dl * (s_dl * (1.0 - s_dl))).astype(BF16)
        dy_sb = (dmerged * s_sb).astype(BF16)
        dy_dl = (dmerged * s_dl).astype(BF16)
        dysb_ref[...] = dy_sb
        dydl_ref[...] = dy_dl
        dosb_ref[...] = _dot_nt(dy_sb, wsb_ref[...]).astype(BF16)
        do_dl = _dot_nt(dy_dl, wdl_ref[...])
        dodl_ref[...] = do_dl
        row = lax.broadcasted_iota(jnp.int32, (DIL_OUT, DIL_OUT), 0) // HEAD_DIM
        col = lax.broadcasted_iota(jnp.int32, (DIL_OUT, DIL_OUT), 1) // HEAD_DIM
        same_head = (row == col).astype(BF16)
        hi, lo = _split_bf16(do_dl * o_dl)
        dsum_ref[...] = _dot(hi, same_head) + _dot(lo, same_head)

    tile = lambda w: pl.BlockSpec((TM, w), lambda i: (i, 0))
    return pl.pallas_call(
        body, name="mix_bwd", grid=(TOK // TM,),
        in_specs=[tile(D_MODEL), tile(SB_WIDTH), tile(DIL_OUT), tile(2 * D_MODEL),
                  _resident((SB_WIDTH, D_MODEL)), _resident((DIL_OUT, D_MODEL)),
                  _resident((D_MODEL, D_MODEL))],
        out_specs=[tile(2 * D_MODEL), tile(D_MODEL), tile(D_MODEL), tile(SB_WIDTH), tile(DIL_OUT),
                   tile(DIL_OUT)],
        out_shape=[jax.ShapeDtypeStruct((TOK, 2 * D_MODEL), BF16),
                   jax.ShapeDtypeStruct((TOK, D_MODEL), BF16),
                   jax.ShapeDtypeStruct((TOK, D_MODEL), BF16),
                   jax.ShapeDtypeStruct((TOK, SB_WIDTH), BF16),
                   jax.ShapeDtypeStruct((TOK, DIL_OUT), F32),
                   jax.ShapeDtypeStruct((TOK, DIL_OUT), F32)],
        compiler_params=_params(("parallel",)),
    )(dx1, o_sb, o_dl, gates, w_sb_up, w_dil_up, w_out)


def _proj_bwd(dproj, dx1, x, g, w_in):
    def body(dp_ref, dx1_ref, x_ref, g_ref, w_ref, dx_ref, dg_ref):
        @pl.when(pl.program_id(0) == 0)
        def _():
            dg_ref[...] = jnp.zeros_like(dg_ref)

        du = jnp.zeros((TM, D_MODEL), F32)
        for c0, w in _chunks(IN_WIDTH, 1024):
            du = du + _dot_nt(dp_ref[:, c0:c0 + w], w_ref[:, c0:c0 + w])
        g_v = g_ref[...]
        n, r, _ = _rms_fwd(x_ref[...], g_v)
        dx, dg = _rms_bwd(du, n, r, g_v)
        dg_ref[...] += dg
        dx_ref[...] = dx1_ref[...] + dx

    tile = lambda w: pl.BlockSpec((TM, w), lambda i: (i, 0))
    return pl.pallas_call(
        body, name="proj_bwd", grid=(TOK // TM,),
        in_specs=[tile(IN_WIDTH), tile(D_MODEL), tile(D_MODEL), _resident((1, D_MODEL)),
                  _resident((D_MODEL, IN_WIDTH))],
        out_specs=[tile(D_MODEL), pl.BlockSpec((1, D_MODEL), lambda i: (0, 0))],
        out_shape=[jax.ShapeDtypeStruct((TOK, D_MODEL), F32),
                   jax.ShapeDtypeStruct((1, D_MODEL), F32)],
        compiler_params=_params(("arbitrary",)),
    )(dproj, dx1, x, g, w_in)


def _atb(a, b, name, tm, tn, col_blocks=0, tk=512):
    m, n = a.shape[1], b.shape[1]
    nk = TOK // tk

    def body(a_ref, b_ref, o_ref, acc_ref):
        k = pl.program_id(2)

        @pl.when(k == 0)
        def _():
            acc_ref[...] = jnp.zeros_like(acc_ref)

        acc_ref[...] += _dot_tn(a_ref[...], b_ref[...])

        @pl.when(k == nk - 1)
        def _():
            if col_blocks:
                width = n // col_blocks
                for blk in range(col_blocks):
                    o_ref[blk] = acc_ref[:, blk * width:(blk + 1) * width].astype(BF16)
            else:
                o_ref[...] = acc_ref[...].astype(BF16)

    if col_blocks:
        out_spec = pl.BlockSpec((col_blocks, tm, n // col_blocks), lambda i, j, k: (0, i, 0))
        out_shape = jax.ShapeDtypeStruct((col_blocks, m, n // col_blocks), BF16)
    else:
        out_spec = pl.BlockSpec((tm, tn), lambda i, j, k: (i, j))
        out_shape = jax.ShapeDtypeStruct((m, n), BF16)
    return pl.pallas_call(
        body, name=name, grid=(m // tm, n // tn, nk),
        in_specs=[pl.BlockSpec((tk, tm), lambda i, j, k: (k, i)),
                  pl.BlockSpec((tk, tn), lambda i, j, k: (k, j))],
        out_specs=out_spec, out_shape=out_shape,
        scratch_shapes=[pltpu.VMEM((tm, tn), F32)],
        compiler_params=_params(("parallel", "parallel", "arbitrary")),
    )(a, b)


SB_PAIRS = SB_WIDTH // LANES


def _two_heads(v, lane0):
    zero = jnp.zeros_like(v)
    return jnp.where(lane0, v, zero), jnp.where(lane0, zero, v)


SB_QBLK = 256
N_SB_STEPS = SEQ // SB_QBLK


SB_KCHUNK = 2 * BLK
SB_ROWS = 2 * SB_QBLK


def _log_keep(z):
    neg_z = -z
    return jnp.minimum(neg_z, 0.0) - jnp.log(1.0 + jnp.exp(jnp.minimum(z, neg_z)))


def _stack_heads(v, lane0):
    return jnp.concatenate(_two_heads(v, lane0), axis=0)


def _block_sums(v, tri):
    halves = (v[:, :BLK], v[:, BLK:])
    hi, lo = _split_bf16(jnp.concatenate(halves, axis=0))
    prod = _dot(jnp.concatenate([hi, lo], axis=0), tri)
    tri_sum = prod[:2 * SB_ROWS] + prod[2 * SB_ROWS:]
    sums = tuple(jnp.sum(h, axis=1, keepdims=True) for h in halves)
    return (tri_sum[:SB_ROWS], tri_sum[SB_ROWS:]), sums


def _sb_diag_mask():
    row = lax.broadcasted_iota(jnp.int32, (SB_ROWS, SB_KCHUNK), 0)
    col = lax.broadcasted_iota(jnp.int32, (SB_ROWS, SB_KCHUNK), 1)
    return col < jnp.where(row >= SB_QBLK, row - SB_QBLK, row)


def _sb_fwd(qkv):
    def body(q_ref, k_ref, v_ref, o_ref, tot_ref):
        i = pl.program_id(2)
        krow = lax.broadcasted_iota(jnp.int32, (BLK, BLK), 0)
        kcol = lax.broadcasted_iota(jnp.int32, (BLK, BLK), 1)
        later = (krow > kcol).astype(BF16)
        lane0 = lax.broadcasted_iota(jnp.int32, (SB_QBLK, LANES), 1) < HEAD_DIM
        q2 = _stack_heads(q_ref[0] * QK_SCALE, lane0)

        def chunk(c, carry, causal):
            acc, run = carry
            off = pl.multiple_of(c * SB_KCHUNK, SB_KCHUNK)
            z = _dot_nt(q2, k_ref[0, pl.ds(off, SB_KCHUNK), :])
            log_keep = _log_keep(z)
            if causal is not None:
                log_keep = jnp.where(causal, log_keep, 0.0)
            suffix, sums = _block_sums(log_keep, later)
            log_after = jnp.concatenate([suffix[0] + (run + sums[1]), suffix[1] + run], axis=1)
            a = jnp.exp(log_keep + z + log_after)
            if causal is not None:
                a = jnp.where(causal, a, 0.0)
            acc = acc + _dot(a.astype(BF16), v_ref[0, pl.ds(off, SB_KCHUNK), :])
            return acc, run + (sums[0] + sums[1])

        carry = chunk(i, (jnp.zeros((SB_ROWS, LANES), F32), jnp.zeros((SB_ROWS, 1), F32)), _sb_diag_mask())
        acc, run = lax.fori_loop(0, i, lambda t, c: chunk(i - 1 - t, c, None), carry)
        o_ref[0] = jnp.where(lane0, acc[:SB_QBLK], acc[SB_QBLK:]).astype(BF16)
        tot_ref[0] = jnp.where(lane0, run[:SB_QBLK], run[SB_QBLK:])

    blk = pl.BlockSpec((1, SB_QBLK, LANES), lambda b, h, i: (b, i, h))
    return pl.pallas_call(
        body, name="sb_fwd", grid=(B_LOC, SB_PAIRS, N_SB_STEPS),
        in_specs=[blk,
                  pl.BlockSpec((1, SEQ, LANES), lambda b, h, i: (b, 0, SB_PAIRS + h)),
                  pl.BlockSpec((1, SEQ, LANES), lambda b, h, i: (b, 0, 2 * SB_PAIRS + h))],
        out_specs=[blk, blk],
        out_shape=[jax.ShapeDtypeStruct((B_LOC, SEQ, SB_WIDTH), BF16),
                   jax.ShapeDtypeStruct((B_LOC, SEQ, SB_WIDTH), F32)],
        compiler_params=_params(("parallel", "parallel", "arbitrary")),
    )(qkv, qkv, qkv)


def _sb_bwd(qkv, d_o, tot, token):
    def body(q_ref, k_ref, v_ref, do_ref, tot_ref, token_ref, dq_ref, dk_ref, dv_ref, dk_acc, dv_acc):
        i = pl.program_id(2)
        krow = lax.broadcasted_iota(jnp.int32, (BLK, BLK), 0)
        kcol = lax.broadcasted_iota(jnp.int32, (BLK, BLK), 1)
        upto = (krow <= kcol).astype(BF16)
        earlier = (krow < kcol).astype(BF16)
        lane0 = lax.broadcasted_iota(jnp.int32, (SB_QBLK, LANES), 1) < HEAD_DIM
        q2 = _stack_heads(q_ref[0] * QK_SCALE, lane0)
        do2 = _stack_heads(do_ref[0], lane0)
        tot = tot_ref[0]
        tot2 = jnp.concatenate([_head_col(tot, lane0), _head_col(tot, jnp.logical_not(lane0))], axis=0)

        @pl.when(i == 0)
        def _():
            dk_acc[...] = jnp.zeros_like(dk_acc)
            dv_acc[...] = jnp.zeros_like(dv_acc)

        def chunk(c, carry, causal):
            dq, pre_keep, pre_e = carry
            off = pl.multiple_of(c * SB_KCHUNK, SB_KCHUNK)
            k_c = k_ref[0, pl.ds(off, SB_KCHUNK), :]
            v_c = v_ref[0, pl.ds(off, SB_KCHUNK), :]
            z = _dot_nt(q2, k_c)
            d_a = _dot_nt(do2, v_c)
            log_keep = _log_keep(z)
            if causal is not None:
                log_keep = jnp.where(causal, log_keep, 0.0)
            log_beta = log_keep + z
            prefix, sums = _block_sums(log_keep, upto)
            inclusive = jnp.concatenate([prefix[0], prefix[1] + sums[0]], axis=1)
            a = jnp.exp(log_beta + ((tot2 - pre_keep) - inclusive))
            if causal is not None:
                a = jnp.where(causal, a, 0.0)
            e = d_a * a
            e_prefix, e_sums = _block_sums(e, earlier)
            before = jnp.concatenate([e_prefix[0] + pre_e, e_prefix[1] + (pre_e + e_sums[0])], axis=1)
            dz = e - (e + before) * jnp.exp(log_beta)
            if causal is not None:
                dz = jnp.where(causal, dz, 0.0)
            dz = dz.astype(BF16)
            dq = dq + _dot(dz, k_c)
            dk_acc[pl.ds(off, SB_KCHUNK), :] += _dot_tn(dz, q2)
            dv_acc[pl.ds(off, SB_KCHUNK), :] += _dot_tn(a.astype(BF16), do2)
            return dq, pre_keep + (sums[0] + sums[1]), pre_e + (e_sums[0] + e_sums[1])

        zero_col = jnp.zeros((SB_ROWS, 1), F32)
        carry = lax.fori_loop(0, i, lambda t, c: chunk(t, c, None),
                              (jnp.zeros((SB_ROWS, LANES), F32), zero_col, zero_col))
        dq, _, _ = chunk(i, carry, _sb_diag_mask())
        dq_ref[0] = (jnp.where(lane0, dq[:SB_QBLK], dq[SB_QBLK:]) * QK_SCALE).astype(BF16)

        @pl.when(i == N_SB_STEPS - 1)
        def _():
            dk_ref[0] = dk_acc[...].astype(BF16)
            dv_ref[0] = dv_acc[...].astype(BF16)

    blk = pl.BlockSpec((1, SB_QBLK, LANES), lambda b, h, i: (b, i, h))
    whole = lambda c: pl.BlockSpec((1, SEQ, LANES), lambda b, h, i: (b, 0, c * SB_PAIRS + h))
    out = jax.ShapeDtypeStruct((B_LOC, SEQ, SB_WIDTH), BF16)
    return pl.pallas_call(
        body, name="sb_bwd", grid=(B_LOC, SB_PAIRS, N_SB_STEPS),
        in_specs=[blk, whole(1), whole(2), blk, blk, pl.BlockSpec((8, LANES), lambda b, h, i: (0, 0))],
        out_specs=[blk, whole(0), whole(0)],
        out_shape=[out, out, out],
        scratch_shapes=[pltpu.VMEM((SEQ, LANES), F32), pltpu.VMEM((SEQ, LANES), F32)],
        compiler_params=_params(("parallel", "parallel", "arbitrary")),
    )(qkv, qkv, qkv, d_o, tot, token)


DIL_GROUPS = len(DIL_PAIRS)
DIL_QBLOCKS = SEQ // BLK


def _residue_rows(j, dilation):
    length = SEQ // dilation
    return pl.ds(j, length, stride=dilation) if dilation > 1 else pl.ds(0, length)


def _gather_residues(src_ref, dst_ref, dst_off, dilation, scale=None):
    length = SEQ // dilation
    for j in range(dilation):
        v = src_ref[_residue_rows(j, dilation), :]
        if scale is not None:
            v = v * scale
        dst_ref[dst_off + j * length:dst_off + (j + 1) * length, :] = v.astype(dst_ref.dtype)


def _scatter_residues(src_ref, src_off, dst_ref, dilation):
    length = SEQ // dilation
    for j in range(dilation):
        dst_ref[_residue_rows(j, dilation), :] = (
            src_ref[src_off + j * length:src_off + (j + 1) * length, :].astype(dst_ref.dtype))


def _dil_geometry(group, pair):
    dilation = DIL_PAIRS[group][1]
    row = lax.broadcasted_iota(jnp.int32, (2 * BLK, 2 * BLK), 0)
    col = lax.broadcasted_iota(jnp.int32, (2 * BLK, 2 * BLK), 1)
    second = row >= BLK
    steps = BLK + jnp.where(second, row - BLK, row) - col
    coef = -ALIBI_MAX_BIAS / DIL_HEADS * math.log(2.0)
    first_head = float(4 * group + 1) + 2.0 * pair.astype(F32)
    slope = jnp.exp(coef * (first_head + jnp.where(second, 1.0, 0.0)))
    bias = slope * (steps * dilation).astype(F32)
    valid = jnp.logical_and(steps >= 0, steps <= BLK)
    return bias, valid, col >= BLK


def _dil_tile_scores(q2, kk, geometry, has_prev):
    bias, valid, own = geometry
    ok = jnp.logical_and(valid, jnp.logical_or(own, has_prev))
    return jnp.where(ok, _dot_nt(q2, kk) - bias, NEG_BIG)


def _head_col(v, lane_mask):
    return jnp.max(jnp.where(lane_mask, v, NEG_BIG), axis=1, keepdims=True)


def _dil_fwd(qkv):
    def body(*refs):
        ins, (o_ref, lse_ref), (qs, ks, vs, o_res, lse_res) = refs[:9], refs[9:11], refs[11:16]
        o_grp, lse_grp = refs[16:19], refs[19:22]
        pair = pl.program_id(1)
        lane0 = lax.broadcasted_iota(jnp.int32, (BLK, LANES), 1) < HEAD_DIM
        ks[0:BLK, :] = jnp.zeros((BLK, LANES), BF16)
        vs[0:BLK, :] = jnp.zeros((BLK, LANES), BF16)
        for grp, (_, dilation) in enumerate(DIL_PAIRS):
            q_ref, k_ref, v_ref = ins[3 * grp:3 * grp + 3]
            per_residue = DIL_QBLOCKS // dilation
            _gather_residues(q_ref, qs, 0, dilation, QK_SCALE)
            _gather_residues(k_ref, ks, BLK, dilation)
            _gather_residues(v_ref, vs, BLK, dilation)
            geometry = _dil_geometry(grp, pair)

            def step(blk, _):
                off = pl.multiple_of(blk * BLK, BLK)
                q2 = _stack_heads(qs[pl.ds(off, BLK), :], lane0)
                s = _dil_tile_scores(q2, ks[pl.ds(off, 2 * BLK), :], geometry, blk % per_residue != 0)
                m = jnp.max(s, axis=1, keepdims=True)
                p = jnp.exp(s - m)
                den = jnp.sum(p, axis=1, keepdims=True)
                out = _dot(p.astype(BF16), vs[pl.ds(off, 2 * BLK), :]) / den
                lse = m + jnp.log(den)
                o_res[pl.ds(off, BLK), :] = jnp.where(lane0, out[:BLK], out[BLK:])
                lse_res[pl.ds(off, BLK), :] = jnp.where(lane0, lse[:BLK], lse[BLK:])
                return 0

            lax.fori_loop(0, DIL_QBLOCKS, step, 0, unroll=4)
            _scatter_residues(o_res, 0, o_grp[grp], dilation)
            _scatter_residues(lse_res, 0, lse_grp[grp], dilation)

        for r0 in range(0, SEQ, 2 * BLK):
            rows = slice(r0, r0 + 2 * BLK)
            ls = [lse_grp[g][rows, :] for g in range(DIL_GROUPS)]
            m = jnp.maximum(jnp.maximum(ls[0], ls[1]), ls[2])
            w = [jnp.exp(l - m) for l in ls]
            den = w[0] + w[1] + w[2]
            o_ref[rows, :] = (w[0] * o_grp[0][rows, :] + w[1] * o_grp[1][rows, :] + w[2] * o_grp[2][rows, :]) / den
            lse_ref[rows, :] = m + jnp.log(den)

    def col(part, grp):
        return pl.BlockSpec((None, SEQ, LANES), lambda b, p: (b, 0, 6 * part + 2 * grp + p))

    out_spec = pl.BlockSpec((None, SEQ, LANES), lambda b, p: (b, 0, p))
    out = jax.ShapeDtypeStruct((B_LOC, SEQ, DIL_OUT), F32)
    return pl.pallas_call(
        body, name="dil_fwd", grid=(B_LOC, DIL_OUT // LANES),
        in_specs=[col(part, grp) for grp in range(DIL_GROUPS) for part in range(3)],
        out_specs=[out_spec, out_spec], out_shape=[out, out],
        scratch_shapes=[pltpu.VMEM((SEQ, LANES), BF16), pltpu.VMEM((SEQ + BLK, LANES), BF16),
                        pltpu.VMEM((SEQ + BLK, LANES), BF16), pltpu.VMEM((SEQ, LANES), F32),
                        pltpu.VMEM((SEQ, LANES), F32)] + [pltpu.VMEM((SEQ, LANES), F32)] * (2 * DIL_GROUPS),
        compiler_params=_params(("parallel", "parallel")),
    )(*([qkv] * 9))


def _dil_bwd(qkv, d_o, lse, dsum):
    def body(*refs):
        ins, (do_ref, lse_ref, dsum_ref), outs = refs[:9], refs[9:12], refs[12:21]
        qs, ks, vs, dos, lse_res, dsum_res, dq_res, dk_acc, dv_acc = refs[21:]
        pair = pl.program_id(1)
        lane0 = lax.broadcasted_iota(jnp.int32, (BLK, LANES), 1) < HEAD_DIM
        lane1 = jnp.logical_not(lane0)
        ks[0:BLK, :] = jnp.zeros((BLK, LANES), BF16)
        vs[0:BLK, :] = jnp.zeros((BLK, LANES), BF16)
        for grp, (_, dilation) in enumerate(DIL_PAIRS):
            q_ref, k_ref, v_ref = ins[3 * grp:3 * grp + 3]
            dq_ref, dk_ref, dv_ref = outs[3 * grp:3 * grp + 3]
            per_residue = DIL_QBLOCKS // dilation
            _gather_residues(q_ref, qs, 0, dilation, QK_SCALE)
            _gather_residues(k_ref, ks, BLK, dilation)
            _gather_residues(v_ref, vs, BLK, dilation)
            _gather_residues(do_ref, dos, 0, dilation)
            _gather_residues(lse_ref, lse_res, 0, dilation)
            _gather_residues(dsum_ref, dsum_res, 0, dilation)
            dk_acc[...] = jnp.zeros_like(dk_acc)
            dv_acc[...] = jnp.zeros_like(dv_acc)
            geometry = _dil_geometry(grp, pair)

            def step(blk, _):
                off = pl.multiple_of(blk * BLK, BLK)
                q2 = _stack_heads(qs[pl.ds(off, BLK), :], lane0)
                do2 = _stack_heads(dos[pl.ds(off, BLK), :], lane0)
                kk = ks[pl.ds(off, 2 * BLK), :]
                vv = vs[pl.ds(off, 2 * BLK), :]
                lse_blk = lse_res[pl.ds(off, BLK), :]
                dsum_blk = dsum_res[pl.ds(off, BLK), :]
                lse2 = jnp.concatenate([_head_col(lse_blk, lane0), _head_col(lse_blk, lane1)], axis=0)
                dsum2 = jnp.concatenate([_head_col(dsum_blk, lane0), _head_col(dsum_blk, lane1)], axis=0)
                s = _dil_tile_scores(q2, kk, geometry, blk % per_residue != 0)
                p = jnp.exp(s - lse2)
                ds = (p * (_dot_nt(do2, vv) - dsum2)).astype(BF16)
                dq2 = _dot(ds, kk)
                dq_res[pl.ds(off, BLK), :] = jnp.where(lane0, dq2[:BLK], dq2[BLK:]) * QK_SCALE
                dk_acc[pl.ds(off, 2 * BLK), :] += _dot_tn(ds, q2)
                dv_acc[pl.ds(off, 2 * BLK), :] += _dot_tn(p.astype(BF16), do2)
                return 0

            lax.fori_loop(0, DIL_QBLOCKS, step, 0, unroll=4)
            _scatter_residues(dq_res, 0, dq_ref, dilation)
            _scatter_residues(dk_acc, BLK, dk_ref, dilation)
            _scatter_residues(dv_acc, BLK, dv_ref, dilation)

    def col(part, grp):
        return pl.BlockSpec((None, SEQ, LANES), lambda b, p: (b, 0, 6 * part + 2 * grp + p))

    slot = pl.BlockSpec((None, SEQ, LANES), lambda b, p: (b, 0, p))
    out = jax.ShapeDtypeStruct((B_LOC, SEQ, DIL_OUT), F32)
    return pl.pallas_call(
        body, name="dil_bwd", grid=(B_LOC, DIL_OUT // LANES),
        in_specs=[col(part, grp) for grp in range(DIL_GROUPS) for part in range(3)] + [slot] * 3,
        out_specs=[slot] * 9, out_shape=[out] * 9,
        scratch_shapes=[pltpu.VMEM((SEQ, LANES), BF16), pltpu.VMEM((SEQ + BLK, LANES), BF16),
                        pltpu.VMEM((SEQ + BLK, LANES), BF16), pltpu.VMEM((SEQ, LANES), BF16),
                        pltpu.VMEM((SEQ, LANES), F32), pltpu.VMEM((SEQ, LANES), F32),
                        pltpu.VMEM((SEQ, LANES), F32), pltpu.VMEM((SEQ + BLK, LANES), F32),
                        pltpu.VMEM((SEQ + BLK, LANES), F32)],
        compiler_params=_params(("parallel", "parallel")),
    )(*([qkv] * 9), d_o, lse, dsum)


def _peers():
    x, y, c = lax.axis_index("x"), lax.axis_index("y"), lax.axis_index("c")
    me = 4 * x + 2 * y + c
    peers = []
    for mask in range(1, N_DEV):
        px = 1 - x if mask & 4 else x
        py = 1 - y if mask & 2 else y
        pc = 1 - c if mask & 1 else c
        peers.append(((px, py, pc), 4 * px + 2 * py + pc))
    return me, peers


def _all_gather(shard):
    def body(src_ref, out_ref, send_sems, recv_sems, local_sem):
        me, peers = _peers()
        mine = pltpu.make_async_copy(src_ref, out_ref.at[me], local_sem)
        mine.start()
        sends = []
        for k, (peer, _) in enumerate(peers):
            cp = pltpu.make_async_remote_copy(
                src_ref=src_ref, dst_ref=out_ref.at[me], send_sem=send_sems.at[k], recv_sem=recv_sems.at[k],
                device_id=peer, device_id_type=pl.DeviceIdType.MESH)
            cp.start()
            sends.append(cp)
        for k, (peer, peer_idx) in enumerate(peers):
            pltpu.make_async_remote_copy(
                src_ref=src_ref, dst_ref=out_ref.at[peer_idx], send_sem=send_sems.at[k], recv_sem=recv_sems.at[k],
                device_id=peer, device_id_type=pl.DeviceIdType.MESH).wait_recv()
        for cp in sends:
            cp.wait_send()
        mine.wait()

    return pl.pallas_call(
        body, name="all_gather_weights",
        in_specs=[pl.BlockSpec(memory_space=pl.ANY)],
        out_specs=pl.BlockSpec(memory_space=pl.ANY),
        out_shape=jax.ShapeDtypeStruct((N_DEV,) + shard.shape, shard.dtype),
        scratch_shapes=[pltpu.SemaphoreType.DMA((N_DEV - 1,)), pltpu.SemaphoreType.DMA((N_DEV - 1,)),
                        pltpu.SemaphoreType.DMA],
    )(shard)


_HBM = pl.BlockSpec(memory_space=pltpu.HBM)
_SEM = pl.BlockSpec(memory_space=pltpu.SEMAPHORE)
_ANY = pl.BlockSpec(memory_space=pl.ANY)
_EFFECT = pltpu.SideEffectType.DATAFLOW_SIDE_EFFECTING


def _peer_copy(src_ref, land_ref, send_sems, recv_sems, a, k, peer, src_idx, slot):
    return pltpu.make_async_remote_copy(
        src_ref=src_ref if src_idx is None else src_ref.at[src_idx], dst_ref=land_ref.at[slot],
        send_sem=send_sems.at[a * (N_DEV - 1) + k], recv_sem=recv_sems.at[a * (N_DEV - 1) + k],
        device_id=peer, device_id_type=pl.DeviceIdType.MESH)


def _send_start(srcs, name, scatter, after):
    n = len(srcs)

    def body(*refs):
        send_sems, recv_sems = refs[2 * n + 1], refs[2 * n + 2]
        me, peers = _peers()
        for a in range(n):
            for k, (peer, peer_idx) in enumerate(peers):
                _peer_copy(refs[a], refs[n + a], send_sems, recv_sems, a, k, peer,
                           peer_idx if scatter else None, me).start()
        refs[-1][...] = jnp.zeros_like(refs[-1])

    lands = [lax.empty((N_DEV,) + s.shape[-2:], s.dtype) for s in srcs]
    both = list(srcs) + lands
    out = pl.pallas_call(
        body, name=name,
        out_shape=(pltpu.SemaphoreType.DMA((n * (N_DEV - 1),)), pltpu.SemaphoreType.DMA((n * (N_DEV - 1),)),
                   *[pltpu.HBM(t.shape, t.dtype) for t in both], jax.ShapeDtypeStruct((8, LANES), F32)),
        in_specs=(_HBM,) * (2 * n) + (_ANY,),
        out_specs=(_SEM, _SEM) + (_HBM,) * (2 * n) + (pl.BlockSpec(memory_space=pltpu.VMEM),),
        input_output_aliases={i: 2 + i for i in range(2 * n)},
        compiler_params=pltpu.CompilerParams(has_side_effects=_EFFECT),
    )(*[pltpu.with_memory_space_constraint(t, pltpu.HBM) for t in both], after)
    return out[0], out[1], list(out[2:2 + n]), list(out[2 + n:2 + 2 * n]), out[-1]


def _send_wait(handles, name, scatter, afters):
    send_sems, recv_sems, srcs, lands = handles[:4]
    n = len(srcs)

    def body(*refs):
        send_sems, recv_sems = refs[2 * n], refs[2 * n + 1]
        me, peers = _peers()
        for a in range(n):
            for k, (peer, peer_idx) in enumerate(peers):
                cp = _peer_copy(refs[a], refs[n + a], send_sems, recv_sems, a, k, peer,
                                peer_idx if scatter else None, peer_idx)
                cp.wait_send()
                cp.wait_recv()

    both = list(srcs) + list(lands)
    out = pl.pallas_call(
        body, name=name,
        out_shape=tuple(pltpu.HBM(t.shape, t.dtype) for t in both),
        in_specs=(_HBM,) * (2 * n) + (_SEM, _SEM) + (_ANY,) * len(afters),
        out_specs=(_HBM,) * (2 * n),
        input_output_aliases={i: i for i in range(2 * n)},
        compiler_params=pltpu.CompilerParams(has_side_effects=_EFFECT),
    )(*both, send_sems, recv_sems, *afters)
    return list(out[:n]), list(out[n:])


def _sum_in_device_order(me, land_ref, own):
    acc = None
    for j in range(N_DEV):
        term = jnp.where(me == j, own, land_ref[j]).astype(F32)
        acc = term if acc is None else acc + term
    return acc


def _adam_math(w, g, m, v):
    c1 = 1.0 - ADAM_B1 ** ADAM_STEP
    c2 = 1.0 - ADAM_B2 ** ADAM_STEP
    m_new = ADAM_B1 * m + (1.0 - ADAM_B1) * g
    v_new = ADAM_B2 * v + (1.0 - ADAM_B2) * (g * g)
    delta = -ADAM_LR * ((m_new / c1) / (jnp.sqrt(v_new / c2) + ADAM_EPS) + ADAM_WD * w)
    return delta, m_new, v_new


def _row_tile(rows):
    return max(t for t in range(8, 257, 8) if rows % t == 0) if rows % 8 == 0 else rows


def _sum_update(land, blocks, me, w, m, v, name):
    _, rows, cols = land.shape
    tile_rows = _row_tile(rows)

    def body(me_ref, land_ref, own_ref, w_ref, m_ref, v_ref, g_ref, d_ref, nm_ref, nv_ref):
        g = _sum_in_device_order(me_ref[0], land_ref, own_ref[...])
        g_ref[...] = g
        d_ref[...], nm_ref[...], nv_ref[...] = _adam_math(w_ref[...], g, m_ref[...], v_ref[...])

    tile = pl.BlockSpec((tile_rows, cols), lambda i, me_ref: (i, 0))
    out = jax.ShapeDtypeStruct((rows, cols), F32)
    return pl.pallas_call(
        body, name=name,
        grid_spec=pltpu.PrefetchScalarGridSpec(
            num_scalar_prefetch=1, grid=(rows // tile_rows,),
            in_specs=[pl.BlockSpec((N_DEV, tile_rows, cols), lambda i, me_ref: (0, i, 0)),
                      pl.BlockSpec((None, tile_rows, cols), lambda i, me_ref: (me_ref[0], i, 0)),
                      tile, tile, tile],
            out_specs=[tile] * 4),
        out_shape=[out] * 4,
        compiler_params=_params(("parallel",)),
    )(me, land, blocks, w, m, v)


def _sum_gains(land, own, me):
    def body(me_ref, land_ref, own_ref, o_ref):
        o_ref[...] = _sum_in_device_order(me_ref[0], land_ref, own_ref[...])

    return pl.pallas_call(
        body, name="sum_gain_grads",
        grid_spec=pltpu.PrefetchScalarGridSpec(
            num_scalar_prefetch=1, grid=(1,),
            in_specs=[pl.BlockSpec(land.shape, lambda i, me_ref: (0, 0, 0)),
                      pl.BlockSpec(own.shape, lambda i, me_ref: (0, 0))],
            out_specs=pl.BlockSpec(own.shape, lambda i, me_ref: (0, 0))),
        out_shape=jax.ShapeDtypeStruct(own.shape, F32),
    )(me, land, own)


def _adamw(w, g, m, v, name):
    def body(w_ref, g_ref, m_ref, v_ref, d_ref, nm_ref, nv_ref):
        d_ref[...], nm_ref[...], nv_ref[...] = _adam_math(w_ref[...], g_ref[...], m_ref[...], v_ref[...])

    whole = pl.BlockSpec(w.shape, lambda i: (0, 0))
    out = jax.ShapeDtypeStruct(w.shape, F32)
    return pl.pallas_call(
        body, name=name, grid=(1,),
        in_specs=[whole] * 4, out_specs=[whole] * 3, out_shape=[out] * 3,
    )(w, g, m, v)


GROUP_REST = ("w_sb_up", "w_dil_up", "w_out", "w_ffn_in", "w_ffn_out")
COL_SHARDED = ("w_in", "w_sb_up", "w_dil_up", "w_ffn_in")


def _full_from_shards(name, slots):
    _, r, c = slots.shape
    if name in COL_SHARDED:
        return slots.transpose(1, 0, 2).reshape(r, N_DEV * c)
    return slots.reshape(N_DEV * r, c)


def _shards_from_full(name, full):
    rows, cols = full.shape
    if name in COL_SHARDED:
        return full.reshape(rows, N_DEV, cols // N_DEV).transpose(1, 0, 2)
    return full.reshape(N_DEV, rows // N_DEV, cols)


def _local_step(x, target, g_mix, g_ffn, g_fin, w_in, take_rest, give_rest, give_first):
    w = {"w_in": w_in}
    qkv_sb, qkv_dl, gates, u = _norm_proj(x, g_mix, w["w_in"])
    qkv_sb = qkv_sb.reshape(B_LOC, SEQ, 3 * SB_WIDTH)
    qkv_dl = qkv_dl.reshape(B_LOC, SEQ, 3 * DIL_WIDTH)
    o_sb, sb_tot = _sb_fwd(qkv_sb)
    o_sb = o_sb.reshape(TOK, SB_WIDTH)
    o_dl, lse = _dil_fwd(qkv_dl)
    o_dl = o_dl.reshape(TOK, DIL_OUT)

    w.update(take_rest(o_sb, o_dl))
    x1, merged = _mix_out(x, o_sb, o_dl, gates, w["w_sb_up"], w["w_dil_up"], w["w_out"])
    loss, dx1, u2, act, dh, dx2, dg_fin, dg_ffn = _ffn_fwd_bwd(x1, target, g_ffn, g_fin, w["w_ffn_in"], w["w_ffn_out"])
    dgates, dy_sb, dy_dl, do_sb, do_dl, dsum = _mix_bwd(dx1, o_sb, o_dl, gates, w["w_sb_up"], w["w_dil_up"], w["w_out"])
    token = give_rest({
        "w_sb_up": _atb(o_sb, dy_sb, "grad_w_sb_up", SB_WIDTH, D_MODEL, col_blocks=N_DEV),
        "w_dil_up": _atb(o_dl.astype(BF16), dy_dl, "grad_w_dil_up", DIL_OUT, D_MODEL, col_blocks=N_DEV),
        "w_out": _shards_from_full("w_out", _atb(merged, dx1.astype(BF16), "grad_w_out", D_MODEL, D_MODEL)),
        "w_ffn_in": _shards_from_full("w_ffn_in", _atb(u2, dh, "grad_w_ffn_in", D_MODEL, D_FF)),
        "w_ffn_out": _shards_from_full("w_ffn_out", _atb(act, dx2, "grad_w_ffn_out", D_FF // 2, D_MODEL)),
    })

    dq_sb, dk_sb, dv_sb = _sb_bwd(qkv_sb, do_sb.reshape(B_LOC, SEQ, SB_WIDTH), sb_tot, token)
    as_batch = lambda t: t.reshape(B_LOC, SEQ, DIL_OUT)
    d_dl = _dil_bwd(qkv_dl, as_batch(do_dl), lse, as_batch(dsum))
    flat = lambda t: t.reshape(TOK, -1)
    dproj = jnp.concatenate(
        [flat(dq_sb), flat(dk_sb), flat(dv_sb)]
        + [flat(d_dl[3 * grp + part]).astype(BF16) for part in range(3) for grp in range(DIL_GROUPS)]
        + [dgates], axis=1)

    token = give_first(_shards_from_full("w_in", _atb(u, dproj, "grad_w_in", D_MODEL, IN_WIDTH // 2)))
    grad_x, dg_mix = _proj_bwd(dproj, dx1, x, g_mix + token[:1, :1], w["w_in"])
    gain_grads = jnp.concatenate([dg_mix, dg_ffn, dg_fin], axis=0)
    return loss, grad_x, gain_grads


def kernel(x, norm_mix_g, w_in, w_sb_up, w_dil_up, w_out, norm_ffn_g, w_ffn_in, w_ffn_out, norm_final_g, loss_target, m_norm_mix_g, m_w_in, m_w_sb_up, m_w_dil_up, m_w_out, m_norm_ffn_g, m_w_ffn_in, m_w_ffn_out, m_norm_final_g, v_norm_mix_g, v_w_in, v_w_sb_up, v_w_dil_up, v_w_out, v_norm_ffn_g, v_w_ffn_in, v_w_ffn_out, v_norm_final_g):
    mats = {"w_in": w_in, "w_sb_up": w_sb_up, "w_dil_up": w_dil_up, "w_out": w_out,
            "w_ffn_in": w_ffn_in, "w_ffn_out": w_ffn_out}
    moments_m = {"w_in": m_w_in, "w_sb_up": m_w_sb_up, "w_dil_up": m_w_dil_up, "w_out": m_w_out,
                 "w_ffn_in": m_w_ffn_in, "w_ffn_out": m_w_ffn_out}
    moments_v = {"w_in": v_w_in, "w_sb_up": v_w_sb_up, "w_dil_up": v_w_dil_up, "w_out": v_w_out,
                 "w_ffn_in": v_w_ffn_in, "w_ffn_out": v_w_ffn_out}
    me = 4 * lax.axis_index("x") + 2 * lax.axis_index("y") + lax.axis_index("c")
    me_arr = me.astype(jnp.int32).reshape(1)

    def own_slot(landed, block):
        return lax.dynamic_update_slice(landed, block[None], (me, 0, 0))

    gathered_w_in = _all_gather(w_in[0].astype(BF16))
    rest_handles = _send_start([mats[name][0].astype(BF16) for name in GROUP_REST], "gather_rest_start", False,
                               gathered_w_in)

    def take_rest(o_sb, o_dl):
        sent, landed = _send_wait(rest_handles, "gather_rest_wait", False, (o_sb, o_dl))
        return {name: _full_from_shards(name, own_slot(slots, shard))
                for name, shard, slots in zip(GROUP_REST, sent, landed)}

    grad_handles = []

    def give_rest(grads):
        handles = _send_start([grads[name] for name in GROUP_REST], "grads_rest_start", True, grads["w_out"])
        grad_handles.append(handles)
        return handles[4]

    rest_landed, first_handles = [], []

    def give_first(blocks):
        rest_landed.append(_send_wait(grad_handles[0], "grads_rest_wait", True, (blocks,)))
        handles = _send_start([blocks], "grads_first_start", True, rest_landed[0][1][0])
        first_handles.append(handles)
        return handles[4]

    g_fin = norm_final_g.reshape(1, D_MODEL)
    loss, grad_x, gain_grads = _local_step(
        x.reshape(TOK, D_MODEL), loss_target.reshape(TOK, D_MODEL), norm_mix_g + rest_handles[4][:1, :1],
        norm_ffn_g, g_fin, _full_from_shards("w_in", gathered_w_in), take_rest, give_rest, give_first)

    out_g, out_d, out_m, out_v = {}, {}, {}, {}

    def update(names, sent, landed):
        for name, blocks, slots in zip(names, sent, landed):
            g, d, nm, nv = _sum_update(slots, blocks, me_arr, mats[name][0], moments_m[name][0], moments_v[name][0],
                                       "update_" + name)
            out_g[name], out_d[name], out_m[name], out_v[name] = g[None], d[None], nm[None], nv[None]

    gain_rows = jnp.concatenate([gain_grads, jnp.tile(loss, (1, D_MODEL // LANES)),
                                 jnp.zeros((8 - 4, D_MODEL), F32)], axis=0)
    gain_handles = _send_start([gain_rows], "gains_start", False, grad_x)
    update(GROUP_REST, *rest_landed[0])
    sent, landed = _send_wait(gain_handles, "gains_wait", False, (out_d["w_ffn_in"],))
    g_gains = _sum_gains(landed[0], sent[0], me_arr)
    sent, landed = _send_wait(first_handles[0], "grads_first_wait", True, (out_d["w_ffn_in"], g_gains))
    update(("w_in",), sent, landed)

    gain_w = jnp.concatenate([norm_mix_g, norm_ffn_g, g_fin], axis=0)
    gain_m = jnp.concatenate([m_norm_mix_g, m_norm_ffn_g, m_norm_final_g.reshape(1, D_MODEL)], axis=0)
    gain_v = jnp.concatenate([v_norm_mix_g, v_norm_ffn_g, v_norm_final_g.reshape(1, D_MODEL)], axis=0)
    gd, gm, gv = _adamw(gain_w, g_gains[:3], gain_m, gain_v, "adamw_gains")
    for idx, name in enumerate(("norm_mix_g", "norm_ffn_g", "norm_final_g")):
        shape = (D_MODEL,) if name == "norm_final_g" else (1, D_MODEL)
        out_g[name] = g_gains[idx].reshape(shape)
        out_d[name], out_m[name], out_v[name] = gd[idx].reshape(shape), gm[idx].reshape(shape), gv[idx].reshape(shape)

    order = ("norm_mix_g", "w_in", "w_sb_up", "w_dil_up", "w_out", "norm_ffn_g", "w_ffn_in", "w_ffn_out",
             "norm_final_g")
    return (g_gains[3, 0], grad_x.reshape(B_LOC, SEQ, D_MODEL),
            *[out_g[n] for n in order], *[out_d[n] for n in order],
            *[out_m[n] for n in order], *[out_v[n] for n in order])
```

```python
import math

import jax
import jax.numpy as jnp
from jax import lax
from jax.experimental import pallas as pl
from jax.experimental.pallas import tpu as pltpu

F32 = jnp.float32
BF16 = jnp.bfloat16

N_DEV = 8
D_MODEL = 1024
SEQ = 2048
B_LOC = 2
TOK = B_LOC * SEQ
HEAD_DIM = 64
SB_WIDTH = 512
DIL_WIDTH = 768
DIL_OUT = 256
QKV_WIDTH = 3 * SB_WIDTH + 3 * DIL_WIDTH
IN_WIDTH = QKV_WIDTH + 2 * D_MODEL
D_FF = 2816
DIL_PAIRS = ((128, 1), (512, 4), (2048, 16))
DIL_HEADS = 12
RMS_EPS = 1e-6
ALIBI_MAX_BIAS = 8.0
QK_SCALE = 1.0 / math.sqrt(HEAD_DIM)
BLK = 128
LANES = 128
NEG_BIG = -1e30

ADAM_LR = 0.001
ADAM_B1 = 0.9
ADAM_B2 = 0.999
ADAM_EPS = 1e-08
ADAM_WD = 0.01
ADAM_STEP = 10

VMEM_LIMIT = 56 * 1024 * 1024


def _dot(a, b):
    return jnp.dot(a, b, preferred_element_type=F32)


def _dot_nt(a, b):
    return lax.dot_general(a, b, (((1,), (1,)), ((), ())), preferred_element_type=F32)


def _dot_tn(a, b):
    return lax.dot_general(a, b, (((0,), (0,)), ((), ())), preferred_element_type=F32)


def _softplus(z):
    return jnp.maximum(z, 0.0) + jnp.log1p(jnp.exp(-jnp.abs(z)))


def _sigmoid(z):
    return 1.0 / (1.0 + jnp.exp(-z))


def _split_bf16(v):
    hi = v.astype(BF16)
    lo = (v - hi.astype(F32)).astype(BF16)
    return hi, lo


def _chunks(width, step=512):
    out, c = [], 0
    while c < width:
        w = min(step, width - c)
        out.append((c, w))
        c += w
    return out


def _resident(shape):
    nd = len(shape)
    return pl.BlockSpec(shape, lambda *_: (0,) * nd, pipeline_mode=pl.Buffered(1))


def _params(sem):
    return pltpu.CompilerParams(dimension_semantics=sem, vmem_limit_bytes=VMEM_LIMIT)


def _rms_fwd(x, g):
    r = lax.rsqrt(jnp.mean(x * x, axis=-1, keepdims=True) + RMS_EPS)
    n = x * r
    return n, r, n * g


def _rms_bwd(dy, n, r, g):
    dg = jnp.sum(dy * n, axis=0, keepdims=True)
    dn = dy * g
    dx = r * (dn - n * jnp.mean(dn * n, axis=-1, keepdims=True))
    return dx, dg


TM = 256


def _norm_proj(x, g, w_in):
    def body(x_ref, g_ref, w_ref, sb_ref, dl_ref, gate_ref, u_ref):
        _, _, u = _rms_fwd(x_ref[...], g_ref[...])
        u = u.astype(BF16)
        u_ref[...] = u
        for c0, w in _chunks(3 * SB_WIDTH):
            sb_ref[:, c0:c0 + w] = _dot(u, w_ref[:, c0:c0 + w]).astype(BF16)
        for c0, w in _chunks(3 * DIL_WIDTH):
            dl_ref[:, c0:c0 + w] = _dot(u, w_ref[:, 3 * SB_WIDTH + c0:3 * SB_WIDTH + c0 + w])
        for c0, w in _chunks(2 * D_MODEL):
            gate_ref[:, c0:c0 + w] = _dot(u, w_ref[:, QKV_WIDTH + c0:QKV_WIDTH + c0 + w])

    return pl.pallas_call(
        body, name="norm_proj", grid=(TOK // TM,),
        in_specs=[pl.BlockSpec((TM, D_MODEL), lambda i: (i, 0)), _resident((1, D_MODEL)),
                  _resident((D_MODEL, IN_WIDTH))],
        out_specs=[pl.BlockSpec((TM, 3 * SB_WIDTH), lambda i: (i, 0)),
                   pl.BlockSpec((TM, 3 * DIL_WIDTH), lambda i: (i, 0)),
                   pl.BlockSpec((TM, 2 * D_MODEL), lambda i: (i, 0)),
                   pl.BlockSpec((TM, D_MODEL), lambda i: (i, 0))],
        out_shape=[jax.ShapeDtypeStruct((TOK, 3 * SB_WIDTH), BF16),
                   jax.ShapeDtypeStruct((TOK, 3 * DIL_WIDTH), F32),
                   jax.ShapeDtypeStruct((TOK, 2 * D_MODEL), F32),
                   jax.ShapeDtypeStruct((TOK, D_MODEL), BF16)],
        compiler_params=_params(("parallel",)),
    )(x, g, w_in)


def _mix_out(x, o_sb, o_dl, gates, w_sb_up, w_dil_up, w_out):
    def body(x_ref, osb_ref, odl_ref, gate_ref, wsb_ref, wdl_ref, wout_ref, x1_ref, mg_ref):
        y_sb = _dot(osb_ref[...], wsb_ref[...])
        y_dl = _dot(odl_ref[...].astype(BF16), wdl_ref[...])
        merged = (_sigmoid(gate_ref[:, :D_MODEL]) * y_sb
                  + _sigmoid(gate_ref[:, D_MODEL:]) * y_dl).astype(BF16)
        mg_ref[...] = merged
        x1_ref[...] = x_ref[...] + _dot(merged, wout_ref[...])

    return pl.pallas_call(
        body, name="mix_out", grid=(TOK // TM,),
        in_specs=[pl.BlockSpec((TM, D_MODEL), lambda i: (i, 0)),
                  pl.BlockSpec((TM, SB_WIDTH), lambda i: (i, 0)),
                  pl.BlockSpec((TM, DIL_OUT), lambda i: (i, 0)),
                  pl.BlockSpec((TM, 2 * D_MODEL), lambda i: (i, 0)),
                  _resident((SB_WIDTH, D_MODEL)), _resident((DIL_OUT, D_MODEL)),
                  _resident((D_MODEL, D_MODEL))],
        out_specs=[pl.BlockSpec((TM, D_MODEL), lambda i: (i, 0)),
                   pl.BlockSpec((TM, D_MODEL), lambda i: (i, 0))],
        out_shape=[jax.ShapeDtypeStruct((TOK, D_MODEL), F32),
                   jax.ShapeDtypeStruct((TOK, D_MODEL), BF16)],
        compiler_params=_params(("parallel",)),
    )(x, o_sb, o_dl, gates, w_sb_up, w_dil_up, w_out)


FF_CHUNK = D_FF // 2


def _ffn_fwd_bwd(x1, target, g_ffn, g_fin, w_ffn_in, w_ffn_out):
    def body(x1_ref, t_ref, gffn_ref, gfin_ref, win_ref, wout_ref,
             loss_ref, dx1_ref, u2_ref, act_ref, dh_ref, dx2_ref, dgfin_ref, dgffn_ref, h_scr):
        i = pl.program_id(0)

        @pl.when(i == 0)
        def _():
            loss_ref[...] = jnp.zeros_like(loss_ref)
            dgfin_ref[...] = jnp.zeros_like(dgfin_ref)
            dgffn_ref[...] = jnp.zeros_like(dgffn_ref)

        x1 = x1_ref[...]
        g_ffn_v = gffn_ref[...]
        g_fin_v = gfin_ref[...]
        n2, r2, u2 = _rms_fwd(x1, g_ffn_v)
        u2 = u2.astype(BF16)
        u2_ref[...] = u2
        x2 = x1
        for c0 in range(0, D_FF, FF_CHUNK):
            gate = _dot(u2, win_ref[:, c0:c0 + FF_CHUNK])
            up = _dot(u2, win_ref[:, D_FF + c0:D_FF + c0 + FF_CHUNK])
            h_scr[:, c0:c0 + FF_CHUNK] = gate
            h_scr[:, D_FF + c0:D_FF + c0 + FF_CHUNK] = up
            act = (gate * _sigmoid(gate) * up).astype(BF16)
            act_ref[:, c0:c0 + FF_CHUNK] = act
            x2 = x2 + _dot(act, wout_ref[c0:c0 + FF_CHUNK, :])
        n3, r3, y = _rms_fwd(x2, g_fin_v)
        err = y - t_ref[...]
        sq = jnp.sum(jnp.sum(err * err, axis=1, keepdims=True), axis=0, keepdims=True)
        loss_ref[...] += sq * (0.5 / D_MODEL)
        dx2, dgfin = _rms_bwd(err * (1.0 / D_MODEL), n3, r3, g_fin_v)
        dgfin_ref[...] += dgfin
        dx2_b = dx2.astype(BF16)
        dx2_ref[...] = dx2_b
        du2 = jnp.zeros((TM, D_MODEL), F32)
        for c0 in range(0, D_FF, FF_CHUNK):
            gate = h_scr[:, c0:c0 + FF_CHUNK]
            up = h_scr[:, D_FF + c0:D_FF + c0 + FF_CHUNK]
            dact = _dot_nt(dx2_b, wout_ref[c0:c0 + FF_CHUNK, :])
            sg = _sigmoid(gate)
            dgate = (dact * up * (sg * (1.0 + gate * (1.0 - sg)))).astype(BF16)
            dup = (dact * (gate * sg)).astype(BF16)
            dh_ref[:, c0:c0 + FF_CHUNK] = dgate
            dh_ref[:, D_FF + c0:D_FF + c0 + FF_CHUNK] = dup
            du2 = du2 + _dot_nt(dgate, win_ref[:, c0:c0 + FF_CHUNK])
            du2 = du2 + _dot_nt(dup, win_ref[:, D_FF + c0:D_FF + c0 + FF_CHUNK])
        dx1_n, dgffn = _rms_bwd(du2, n2, r2, g_ffn_v)
        dgffn_ref[...] += dgffn
        dx1_ref[...] = dx2 + dx1_n

    tile = lambda w: pl.BlockSpec((TM, w), lambda i: (i, 0))
    acc = lambda w: pl.BlockSpec((1, w), lambda i: (0, 0))
    return pl.pallas_call(
        body, name="ffn_fwd_bwd", grid=(TOK // TM,),
        in_specs=[tile(D_MODEL), tile(D_MODEL), _resident((1, D_MODEL)), _resident((1, D_MODEL)),
                  _resident((D_MODEL, 2 * D_FF)), _resident((D_FF, D_MODEL))],
        out_specs=[acc(LANES), tile(D_MODEL), tile(D_MODEL), tile(D_FF), tile(2 * D_FF), tile(D_MODEL),
                   acc(D_MODEL), acc(D_MODEL)],
        out_shape=[jax.ShapeDtypeStruct((1, LANES), F32),
                   jax.ShapeDtypeStruct((TOK, D_MODEL), F32),
                   jax.ShapeDtypeStruct((TOK, D_MODEL), BF16),
                   jax.ShapeDtypeStruct((TOK, D_FF), BF16),
                   jax.ShapeDtypeStruct((TOK, 2 * D_FF), BF16),
                   jax.ShapeDtypeStruct((TOK, D_MODEL), BF16),
                   jax.ShapeDtypeStruct((1, D_MODEL), F32),
                   jax.ShapeDtypeStruct((1, D_MODEL), F32)],
        scratch_shapes=[pltpu.VMEM((TM, 2 * D_FF), F32)],
        compiler_params=_params(("arbitrary",)),
    )(x1, target, g_ffn, g_fin, w_ffn_in, w_ffn_out)


def _mix_bwd(dx1, o_sb, o_dl, gates, w_sb_up, w_dil_up, w_out):
    def body(dx1_ref, osb_ref, odl_ref, gate_ref, wsb_ref, wdl_ref, wout_ref,
             dgate_ref, dysb_ref, dydl_ref, dosb_ref, dodl_ref, dsum_ref):
        dmerged = _dot_nt(dx1_ref[...].astype(BF16), wout_ref[...])
        o_dl = odl_ref[...]
        y_sb = _dot(osb_ref[...], wsb_ref[...])
        y_dl = _dot(o_dl.astype(BF16), wdl_ref[...])
        s_sb = _sigmoid(gate_ref[:, :D_MODEL])
        s_dl = _sigmoid(gate_ref[:, D_MODEL:])
        dgate_ref[:, :D_MODEL] = (dmerged * y_sb * (s_sb * (1.0 - s_sb))).astype(BF16)
        dgate_ref[:, D_MODEL:] = (dmerged * y_dl * (s_dl * (1.0 - s_dl))).astype(BF16)
        dy_sb = (dmerged * s_sb).astype(BF16)
        dy_dl = (dmerged * s_dl).astype(BF16)
        dysb_ref[...] = dy_sb
        dydl_ref[...] = dy_dl
        dosb_ref[...] = _dot_nt(dy_sb, wsb_ref[...]).astype(BF16)
        do_dl = _dot_nt(dy_dl, wdl_ref[...])
        dodl_ref[...] = do_dl
        row = lax.broadcasted_iota(jnp.int32, (DIL_OUT, DIL_OUT), 0) // HEAD_DIM
        col = lax.broadcasted_iota(jnp.int32, (DIL_OUT, DIL_OUT), 1) // HEAD_DIM
        same_head = (row == col).astype(BF16)
        hi, lo = _split_bf16(do_dl * o_dl)
        dsum_ref[...] = _dot(hi, same_head) + _dot(lo, same_head)

    tile = lambda w: pl.BlockSpec((TM, w), lambda i: (i, 0))
    return pl.pallas_call(
        body, name="mix_bwd", grid=(TOK // TM,),
        in_specs=[tile(D_MODEL), tile(SB_WIDTH), tile(DIL_OUT), tile(2 * D_MODEL),
                  _resident((SB_WIDTH, D_MODEL)), _resident((DIL_OUT, D_MODEL)),
                  _resident((D_MODEL, D_MODEL))],
        out_specs=[tile(2 * D_MODEL), tile(D_MODEL), tile(D_MODEL), tile(SB_WIDTH), tile(DIL_OUT),
                   tile(DIL_OUT)],
        out_shape=[jax.ShapeDtypeStruct((TOK, 2 * D_MODEL), BF16),
                   jax.ShapeDtypeStruct((TOK, D_MODEL), BF16),
                   jax.ShapeDtypeStruct((TOK, D_MODEL), BF16),
                   jax.ShapeDtypeStruct((TOK, SB_WIDTH), BF16),
                   jax.ShapeDtypeStruct((TOK, DIL_OUT), F32),
                   jax.ShapeDtypeStruct((TOK, DIL_OUT), F32)],
        compiler_params=_params(("parallel",)),
    )(dx1, o_sb, o_dl, gates, w_sb_up, w_dil_up, w_out)


def _proj_bwd(dproj, dx1, x, g, w_in):
    def body(dp_ref, dx1_ref, x_ref, g_ref, w_ref, dx_ref, dg_ref):
        @pl.when(pl.program_id(0) == 0)
        def _():
            dg_ref[...] = jnp.zeros_like(dg_ref)

        du = jnp.zeros((TM, D_MODEL), F32)
        for c0, w in _chunks(IN_WIDTH, 1024):
            du = du + _dot_nt(dp_ref[:, c0:c0 + w], w_ref[:, c0:c0 + w])
        g_v = g_ref[...]
        n, r, _ = _rms_fwd(x_ref[...], g_v)
        dx, dg = _rms_bwd(du, n, r, g_v)
        dg_ref[...] += dg
        dx_ref[...] = dx1_ref[...] + dx

    tile = lambda w: pl.BlockSpec((TM, w), lambda i: (i, 0))
    return pl.pallas_call(
        body, name="proj_bwd", grid=(TOK // TM,),
        in_specs=[tile(IN_WIDTH), tile(D_MODEL), tile(D_MODEL), _resident((1, D_MODEL)),
                  _resident((D_MODEL, IN_WIDTH))],
        out_specs=[tile(D_MODEL), pl.BlockSpec((1, D_MODEL), lambda i: (0, 0))],
        out_shape=[jax.ShapeDtypeStruct((TOK, D_MODEL), F32),
                   jax.ShapeDtypeStruct((1, D_MODEL), F32)],
        compiler_params=_params(("arbitrary",)),
    )(dproj, dx1, x, g, w_in)


def _atb(a, b, name, tm, tn, col_blocks=0, tk=512):
    m, n = a.shape[1], b.shape[1]
    nk = TOK // tk

    def body(a_ref, b_ref, o_ref, acc_ref):
        k = pl.program_id(2)

        @pl.when(k == 0)
        def _():
            acc_ref[...] = jnp.zeros_like(acc_ref)

        acc_ref[...] += _dot_tn(a_ref[...], b_ref[...])

        @pl.when(k == nk - 1)
        def _():
            if col_blocks:
                width = n // col_blocks
                for blk in range(col_blocks):
                    o_ref[blk] = acc_ref[:, blk * width:(blk + 1) * width].astype(BF16)
            else:
                o_ref[...] = acc_ref[...].astype(BF16)

    if col_blocks:
        out_spec = pl.BlockSpec((col_blocks, tm, n // col_blocks), lambda i, j, k: (0, i, 0))
        out_shape = jax.ShapeDtypeStruct((col_blocks, m, n // col_blocks), BF16)
    else:
        out_spec = pl.BlockSpec((tm, tn), lambda i, j, k: (i, j))
        out_shape = jax.ShapeDtypeStruct((m, n), BF16)
    return pl.pallas_call(
        body, name=name, grid=(m // tm, n // tn, nk),
        in_specs=[pl.BlockSpec((tk, tm), lambda i, j, k: (k, i)),
                  pl.BlockSpec((tk, tn), lambda i, j, k: (k, j))],
        out_specs=out_spec, out_shape=out_shape,
        scratch_shapes=[pltpu.VMEM((tm, tn), F32)],
        compiler_params=_params(("parallel", "parallel", "arbitrary")),
    )(a, b)


SB_PAIRS = SB_WIDTH // LANES


def _two_heads(v, lane0):
    zero = jnp.zeros_like(v)
    return jnp.where(lane0, v, zero), jnp.where(lane0, zero, v)


SB_QBLK = 256
N_SB_STEPS = SEQ // SB_QBLK


SB_KCHUNK = 2 * BLK
SB_ROWS = 2 * SB_QBLK


def _log_keep(z):
    neg_z = -z
    return jnp.minimum(neg_z, 0.0) - jnp.log(1.0 + jnp.exp(jnp.minimum(z, neg_z)))


def _stack_heads(v, lane0):
    return jnp.concatenate(_two_heads(v, lane0), axis=0)


def _block_sums(v, tri):
    halves = (v[:, :BLK], v[:, BLK:])
    hi, lo = _split_bf16(jnp.concatenate(halves, axis=0))
    prod = _dot(jnp.concatenate([hi, lo], axis=0), tri)
    tri_sum = prod[:2 * SB_ROWS] + prod[2 * SB_ROWS:]
    sums = tuple(jnp.sum(h, axis=1, keepdims=True) for h in halves)
    return (tri_sum[:SB_ROWS], tri_sum[SB_ROWS:]), sums


def _sb_diag_mask():
    row = lax.broadcasted_iota(jnp.int32, (SB_ROWS, SB_KCHUNK), 0)
    col = lax.broadcasted_iota(jnp.int32, (SB_ROWS, SB_KCHUNK), 1)
    return col < jnp.where(row >= SB_QBLK, row - SB_QBLK, row)


def _sb_fwd(qkv):
    def body(q_ref, k_ref, v_ref, o_ref, tot_ref):
        i = pl.program_id(2)
        krow = lax.broadcasted_iota(jnp.int32, (BLK, BLK), 0)
        kcol = lax.broadcasted_iota(jnp.int32, (BLK, BLK), 1)
        later = (krow > kcol).astype(BF16)
        lane0 = lax.broadcasted_iota(jnp.int32, (SB_QBLK, LANES), 1) < HEAD_DIM
        q2 = _stack_heads(q_ref[0] * QK_SCALE, lane0)

        def chunk(c, carry, causal):
            acc, run = carry
            off = pl.multiple_of(c * SB_KCHUNK, SB_KCHUNK)
            z = _dot_nt(q2, k_ref[0, pl.ds(off, SB_KCHUNK), :])
            log_keep = _log_keep(z)
            if causal is not None:
                log_keep = jnp.where(causal, log_keep, 0.0)
            suffix, sums = _block_sums(log_keep, later)
            log_after = jnp.concatenate([suffix[0] + (run + sums[1]), suffix[1] + run], axis=1)
            a = jnp.exp(log_keep + z + log_after)
            if causal is not None:
                a = jnp.where(causal, a, 0.0)
            acc = acc + _dot(a.astype(BF16), v_ref[0, pl.ds(off, SB_KCHUNK), :])
            return acc, run + (sums[0] + sums[1])

        carry = chunk(i, (jnp.zeros((SB_ROWS, LANES), F32), jnp.zeros((SB_ROWS, 1), F32)), _sb_diag_mask())
        acc, run = lax.fori_loop(0, i, lambda t, c: chunk(i - 1 - t, c, None), carry)
        o_ref[0] = jnp.where(lane0, acc[:SB_QBLK], acc[SB_QBLK:]).astype(BF16)
        tot_ref[0] = jnp.where(lane0, run[:SB_QBLK], run[SB_QBLK:])

    blk = pl.BlockSpec((1, SB_QBLK, LANES), lambda b, h, i: (b, i, h))
    return pl.pallas_call(
        body, name="sb_fwd", grid=(B_LOC, SB_PAIRS, N_SB_STEPS),
        in_specs=[blk,
                  pl.BlockSpec((1, SEQ, LANES), lambda b, h, i: (b, 0, SB_PAIRS + h)),
                  pl.BlockSpec((1, SEQ, LANES), lambda b, h, i: (b, 0, 2 * SB_PAIRS + h))],
        out_specs=[blk, blk],
        out_shape=[jax.ShapeDtypeStruct((B_LOC, SEQ, SB_WIDTH), BF16),
                   jax.ShapeDtypeStruct((B_LOC, SEQ, SB_WIDTH), F32)],
        compiler_params=_params(("parallel", "parallel", "arbitrary")),
    )(qkv, qkv, qkv)


def _sb_bwd(qkv, d_o, tot, token):
    def body(q_ref, k_ref, v_ref, do_ref, tot_ref, token_ref, dq_ref, dk_ref, dv_ref, dk_acc, dv_acc):
        i = pl.program_id(2)
        krow = lax.broadcasted_iota(jnp.int32, (BLK, BLK), 0)
        kcol = lax.broadcasted_iota(jnp.int32, (BLK, BLK), 1)
        upto = (krow <= kcol).astype(BF16)
        earlier = (krow < kcol).astype(BF16)
        lane0 = lax.broadcasted_iota(jnp.int32, (SB_QBLK, LANES), 1) < HEAD_DIM
        q2 = _stack_heads(q_ref[0] * QK_SCALE, lane0)
        do2 = _stack_heads(do_ref[0], lane0)
        tot = tot_ref[0]
        tot2 = jnp.concatenate([_head_col(tot, lane0), _head_col(tot, jnp.logical_not(lane0))], axis=0)

        @pl.when(i == 0)
        def _():
            dk_acc[...] = jnp.zeros_like(dk_acc)
            dv_acc[...] = jnp.zeros_like(dv_acc)

        def chunk(c, carry, causal):
            dq, pre_keep, pre_e = carry
            off = pl.multiple_of(c * SB_KCHUNK, SB_KCHUNK)
            k_c = k_ref[0, pl.ds(off, SB_KCHUNK), :]
            v_c = v_ref[0, pl.ds(off, SB_KCHUNK), :]
            z = _dot_nt(q2, k_c)
            d_a = _dot_nt(do2, v_c)
            log_keep = _log_keep(z)
            if causal is not None:
                log_keep = jnp.where(causal, log_keep, 0.0)
            log_beta = log_keep + z
            prefix, sums = _block_sums(log_keep, upto)
            inclusive = jnp.concatenate([prefix[0], prefix[1] + sums[0]], axis=1)
            a = jnp.exp(log_beta + ((tot2 - pre_keep) - inclusive))
            if causal is not None:
                a = jnp.where(causal, a, 0.0)
            e = d_a * a
            e_prefix, e_sums = _block_sums(e, earlier)
            before = jnp.concatenate([e_prefix[0] + pre_e, e_prefix[1] + (pre_e + e_sums[0])], axis=1)
            dz = e - (e + before) * jnp.exp(log_beta)
            if causal is not None:
                dz = jnp.where(causal, dz, 0.0)
            dz = dz.astype(BF16)
            dq = dq + _dot(dz, k_c)
            dk_acc[pl.ds(off, SB_KCHUNK), :] += _dot_tn(dz, q2)
            dv_acc[pl.ds(off, SB_KCHUNK), :] += _dot_tn(a.astype(BF16), do2)
            return dq, pre_keep + (sums[0] + sums[1]), pre_e + (e_sums[0] + e_sums[1])

        zero_col = jnp.zeros((SB_ROWS, 1), F32)
        carry = lax.fori_loop(0, i, lambda t, c: chunk(t, c, None),
                              (jnp.zeros((SB_ROWS, LANES), F32), zero_col, zero_col))
        dq, _, _ = chunk(i, carry, _sb_diag_mask())
        dq_ref[0] = (jnp.where(lane0, dq[:SB_QBLK], dq[SB_QBLK:]) * QK_SCALE).astype(BF16)

        @pl.when(i == N_SB_STEPS - 1)
        def _():
            dk_ref[0] = dk_acc[...].astype(BF16)
            dv_ref[0] = dv_acc[...].astype(BF16)

    blk = pl.BlockSpec((1, SB_QBLK, LANES), lambda b, h, i: (b, i, h))
    whole = lambda c: pl.BlockSpec((1, SEQ, LANES), lambda b, h, i: (b, 0, c * SB_PAIRS + h))
    out = jax.ShapeDtypeStruct((B_LOC, SEQ, SB_WIDTH), BF16)
    return pl.pallas_call(
        body, name="sb_bwd", grid=(B_LOC, SB_PAIRS, N_SB_STEPS),
        in_specs=[blk, whole(1), whole(2), blk, blk, pl.BlockSpec((8, LANES), lambda b, h, i: (0, 0))],
        out_specs=[blk, whole(0), whole(0)],
        out_shape=[out, out, out],
        scratch_shapes=[pltpu.VMEM((SEQ, LANES), F32), pltpu.VMEM((SEQ, LANES), F32)],
        compiler_params=_params(("parallel", "parallel", "arbitrary")),
    )(qkv, qkv, qkv, d_o, tot, token)


DIL_GROUPS = len(DIL_PAIRS)
DIL_QBLOCKS = SEQ // BLK


def _residue_rows(j, dilation):
    length = SEQ // dilation
    return pl.ds(j, length, stride=dilation) if dilation > 1 else pl.ds(0, length)


def _gather_residues(src_ref, dst_ref, dst_off, dilation, scale=None):
    length = SEQ // dilation
    for j in range(dilation):
        v = src_ref[_residue_rows(j, dilation), :]
        if scale is not None:
            v = v * scale
        dst_ref[dst_off + j * length:dst_off + (j + 1) * length, :] = v.astype(dst_ref.dtype)


def _scatter_residues(src_ref, src_off, dst_ref, dilation):
    length = SEQ // dilation
    for j in range(dilation):
        dst_ref[_residue_rows(j, dilation), :] = (
            src_ref[src_off + j * length:src_off + (j + 1) * length, :].astype(dst_ref.dtype))


def _dil_geometry(group, pair):
    dilation = DIL_PAIRS[group][1]
    row = lax.broadcasted_iota(jnp.int32, (2 * BLK, 2 * BLK), 0)
    col = lax.broadcasted_iota(jnp.int32, (2 * BLK, 2 * BLK), 1)
    second = row >= BLK
    steps = BLK + jnp.where(second, row - BLK, row) - col
    coef = -ALIBI_MAX_BIAS / DIL_HEADS * math.log(2.0)
    first_head = float(4 * group + 1) + 2.0 * pair.astype(F32)
    slope = jnp.exp(coef * (first_head + jnp.where(second, 1.0, 0.0)))
    bias = slope * (steps * dilation).astype(F32)
    valid = jnp.logical_and(steps >= 0, steps <= BLK)
    return bias, valid, col >= BLK


def _dil_tile_scores(q2, kk, geometry, has_prev):
    bias, valid, own = geometry
    ok = jnp.logical_and(valid, jnp.logical_or(own, has_prev))
    return jnp.where(ok, _dot_nt(q2, kk) - bias, NEG_BIG)


def _head_col(v, lane_mask):
    return jnp.max(jnp.where(lane_mask, v, NEG_BIG), axis=1, keepdims=True)


def _dil_fwd(qkv):
    def body(*refs):
        ins, (o_ref, lse_ref), (qs, ks, vs, o_res, lse_res) = refs[:9], refs[9:11], refs[11:16]
        o_grp, lse_grp = refs[16:19], refs[19:22]
        pair = pl.program_id(1)
        lane0 = lax.broadcasted_iota(jnp.int32, (BLK, LANES), 1) < HEAD_DIM
        ks[0:BLK, :] = jnp.zeros((BLK, LANES), BF16)
        vs[0:BLK, :] = jnp.zeros((BLK, LANES), BF16)
        for grp, (_, dilation) in enumerate(DIL_PAIRS):
            q_ref, k_ref, v_ref = ins[3 * grp:3 * grp + 3]
            per_residue = DIL_QBLOCKS // dilation
            _gather_residues(q_ref, qs, 0, dilation, QK_SCALE)
            _gather_residues(k_ref, ks, BLK, dilation)
            _gather_residues(v_ref, vs, BLK, dilation)
            geometry = _dil_geometry(grp, pair)

            def step(blk, _):
                off = pl.multiple_of(blk * BLK, BLK)
                q2 = _stack_heads(qs[pl.ds(off, BLK), :], lane0)
                s = _dil_tile_scores(q2, ks[pl.ds(off, 2 * BLK), :], geometry, blk % per_residue != 0)
                m = jnp.max(s, axis=1, keepdims=True)
                p = jnp.exp(s - m)
                den = jnp.sum(p, axis=1, keepdims=True)
                out = _dot(p.astype(BF16), vs[pl.ds(off, 2 * BLK), :]) / den
                lse = m + jnp.log(den)
                o_res[pl.ds(off, BLK), :] = jnp.where(lane0, out[:BLK], out[BLK:])
                lse_res[pl.ds(off, BLK), :] = jnp.where(lane0, lse[:BLK], lse[BLK:])
                return 0

            lax.fori_loop(0, DIL_QBLOCKS, step, 0, unroll=4)
            _scatter_residues(o_res, 0, o_grp[grp], dilation)
            _scatter_residues(lse_res, 0, lse_grp[grp], dilation)

        for r0 in range(0, SEQ, 2 * BLK):
            rows = slice(r0, r0 + 2 * BLK)
            ls = [lse_grp[g][rows, :] for g in range(DIL_GROUPS)]
            m = jnp.maximum(jnp.maximum(ls[0], ls[1]), ls[2])
            w = [jnp.exp(l - m) for l in ls]
            den = w[0] + w[1] + w[2]
            o_ref[rows, :] = (w[0] * o_grp[0][rows, :] + w[1] * o_grp[1][rows, :] + w[2] * o_grp[2][rows, :]) / den
            lse_ref[rows, :] = m + jnp.log(den)

    def col(part, grp):
        return pl.BlockSpec((None, SEQ, LANES), lambda b, p: (b, 0, 6 * part + 2 * grp + p))

    out_spec = pl.BlockSpec((None, SEQ, LANES), lambda b, p: (b, 0, p))
    out = jax.ShapeDtypeStruct((B_LOC, SEQ, DIL_OUT), F32)
    return pl.pallas_call(
        body, name="dil_fwd", grid=(B_LOC, DIL_OUT // LANES),
        in_specs=[col(part, grp) for grp in range(DIL_GROUPS) for part in range(3)],
        out_specs=[out_spec, out_spec], out_shape=[out, out],
        scratch_shapes=[pltpu.VMEM((SEQ, LANES), BF16), pltpu.VMEM((SEQ + BLK, LANES), BF16),
                        pltpu.VMEM((SEQ + BLK, LANES), BF16), pltpu.VMEM((SEQ, LANES), F32),
                        pltpu.VMEM((SEQ, LANES), F32)] + [pltpu.VMEM((SEQ, LANES), F32)] * (2 * DIL_GROUPS),
        compiler_params=_params(("parallel", "parallel")),
    )(*([qkv] * 9))


def _dil_bwd(qkv, d_o, lse, dsum):
    def body(*refs):
        ins, (do_ref, lse_ref, dsum_ref), outs = refs[:9], refs[9:12], refs[12:21]
        qs, ks, vs, dos, lse_res, dsum_res, dq_res, dk_acc, dv_acc = refs[21:]
        pair = pl.program_id(1)
        lane0 = lax.broadcasted_iota(jnp.int32, (BLK, LANES), 1) < HEAD_DIM
        lane1 = jnp.logical_not(lane0)
        ks[0:BLK, :] = jnp.zeros((BLK, LANES), BF16)
        vs[0:BLK, :] = jnp.zeros((BLK, LANES), BF16)
        for grp, (_, dilation) in enumerate(DIL_PAIRS):
            q_ref, k_ref, v_ref = ins[3 * grp:3 * grp + 3]
            dq_ref, dk_ref, dv_ref = outs[3 * grp:3 * grp + 3]
            per_residue = DIL_QBLOCKS // dilation
            _gather_residues(q_ref, qs, 0, dilation, QK_SCALE)
            _gather_residues(k_ref, ks, BLK, dilation)
            _gather_residues(v_ref, vs, BLK, dilation)
            _gather_residues(do_ref, dos, 0, dilation)
            _gather_residues(lse_ref, lse_res, 0, dilation)
            _gather_residues(dsum_ref, dsum_res, 0, dilation)
            dk_acc[...] = jnp.zeros_like(dk_acc)
            dv_acc[...] = jnp.zeros_like(dv_acc)
            geometry = _dil_geometry(grp, pair)

            def step(blk, _):
                off = pl.multiple_of(blk * BLK, BLK)
                q2 = _stack_heads(qs[pl.ds(off, BLK), :], lane0)
                do2 = _stack_heads(dos[pl.ds(off, BLK), :], lane0)
                kk = ks[pl.ds(off, 2 * BLK), :]
                vv = vs[pl.ds(off, 2 * BLK), :]
                lse_blk = lse_res[pl.ds(off, BLK), :]
                dsum_blk = dsum_res[pl.ds(off, BLK), :]
                lse2 = jnp.concatenate([_head_col(lse_blk, lane0), _head_col(lse_blk, lane1)], axis=0)
                dsum2 = jnp.concatenate([_head_col(dsum_blk, lane0), _head_col(dsum_blk, lane1)], axis=0)
                s = _dil_tile_scores(q2, kk, geometry, blk % per_residue != 0)
                p = jnp.exp(s - lse2)
                ds = (p * (_dot_nt(do2, vv) - dsum2)).astype(BF16)
                dq2 = _dot(ds, kk)
                dq_res[pl.ds(off, BLK), :] = jnp.where(lane0, dq2[:BLK], dq2[BLK:]) * QK_SCALE
                dk_acc[pl.ds(off, 2 * BLK), :] += _dot_tn(ds, q2)
                dv_acc[pl.ds(off, 2 * BLK), :] += _dot_tn(p.astype(BF16), do2)
                return 0

            lax.fori_loop(0, DIL_QBLOCKS, step, 0, unroll=4)
            _scatter_residues(dq_res, 0, dq_ref, dilation)
            _scatter_residues(dk_acc, BLK, dk_ref, dilation)
            _scatter_residues(dv_acc, BLK, dv_ref, dilation)

    def col(part, grp):
        return pl.BlockSpec((None, SEQ, LANES), lambda b, p: (b, 0, 6 * part + 2 * grp + p))

    slot = pl.BlockSpec((None, SEQ, LANES), lambda b, p: (b, 0, p))
    out = jax.ShapeDtypeStruct((B_LOC, SEQ, DIL_OUT), F32)
    return pl.pallas_call(
        body, name="dil_bwd", grid=(B_LOC, DIL_OUT // LANES),
        in_specs=[col(part, grp) for grp in range(DIL_GROUPS) for part in range(3)] + [slot] * 3,
        out_specs=[slot] * 9, out_shape=[out] * 9,
        scratch_shapes=[pltpu.VMEM((SEQ, LANES), BF16), pltpu.VMEM((SEQ + BLK, LANES), BF16),
                        pltpu.VMEM((SEQ + BLK, LANES), BF16), pltpu.VMEM((SEQ, LANES), BF16),
                        pltpu.VMEM((SEQ, LANES), F32), pltpu.VMEM((SEQ, LANES), F32),
                        pltpu.VMEM((SEQ, LANES), F32), pltpu.VMEM((SEQ + BLK, LANES), F32),
                        pltpu.VMEM((SEQ + BLK, LANES), F32)],
        compiler_params=_params(("parallel", "parallel")),
    )(*([qkv] * 9), d_o, lse, dsum)


def _peers():
    x, y, c = lax.axis_index("x"), lax.axis_index("y"), lax.axis_index("c")
    me = 4 * x + 2 * y + c
    peers = []
    for mask in range(1, N_DEV):
        px = 1 - x if mask & 4 else x
        py = 1 - y if mask & 2 else y
        pc = 1 - c if mask & 1 else c
        peers.append(((px, py, pc), 4 * px + 2 * py + pc))
    return me, peers


def _all_gather(shard):
    def body(src_ref, out_ref, send_sems, recv_sems, local_sem):
        x, y, c = lax.axis_index("x"), lax.axis_index("y"), lax.axis_index("c")
        sibling = (x, y, 1 - c)
        chips = [(1 - x, y), (x, 1 - y), (1 - x, 1 - y)]

        def slot(px, py, pc):
            return out_ref.at[4 * px + 2 * py + pc]

        def copy(k, block, to, src=None):
            return pltpu.make_async_remote_copy(
                src_ref=slot(*block) if src is None else src, dst_ref=slot(*block),
                send_sem=send_sems.at[k], recv_sem=recv_sems.at[k], device_id=to,
                device_id_type=pl.DeviceIdType.MESH)

        mine = pltpu.make_async_copy(src_ref, slot(x, y, c), local_sem)
        mine.start()
        first = [copy(0, (x, y, c), sibling, src=src_ref)]
        first += [copy(1 + j, (x, y, c), (*chip, c), src=src_ref) for j, chip in enumerate(chips)]
        for cp in first:
            cp.start()
        passed = [copy(4 + j, (*chip, c), sibling) for j, chip in enumerate(chips)]
        for j, chip in enumerate(chips):
            copy(1 + j, (*chip, c), (x, y, c)).wait_recv()
            passed[j].start()
        copy(0, sibling, (x, y, c)).wait_recv()
        for j, chip in enumerate(chips):
            copy(4 + j, (*chip, 1 - c), (x, y, c)).wait_recv()
        for cp in first + passed:
            cp.wait_send()
        mine.wait()

    return pl.pallas_call(
        body, name="all_gather_weights",
        in_specs=[pl.BlockSpec(memory_space=pl.ANY)],
        out_specs=pl.BlockSpec(memory_space=pl.ANY),
        out_shape=jax.ShapeDtypeStruct((N_DEV,) + shard.shape, shard.dtype),
        scratch_shapes=[pltpu.SemaphoreType.DMA((N_DEV - 1,)), pltpu.SemaphoreType.DMA((N_DEV - 1,)),
                        pltpu.SemaphoreType.DMA],
    )(shard)


_HBM = pl.BlockSpec(memory_space=pltpu.HBM)
_SEM = pl.BlockSpec(memory_space=pltpu.SEMAPHORE)
_ANY = pl.BlockSpec(memory_space=pl.ANY)
_EFFECT = pltpu.SideEffectType.DATAFLOW_SIDE_EFFECTING


def _peer_copy(src_ref, land_ref, send_sems, recv_sems, a, k, peer, src_idx, slot):
    return pltpu.make_async_remote_copy(
        src_ref=src_ref if src_idx is None else src_ref.at[src_idx], dst_ref=land_ref.at[slot],
        send_sem=send_sems.at[a * (N_DEV - 1) + k], recv_sem=recv_sems.at[a * (N_DEV - 1) + k],
        device_id=peer, device_id_type=pl.DeviceIdType.MESH)


def _send_start(srcs, name, scatter, after):
    n = len(srcs)

    def body(*refs):
        send_sems, recv_sems = refs[2 * n + 1], refs[2 * n + 2]
        me, peers = _peers()
        for a in range(n):
            for k, (peer, peer_idx) in enumerate(peers):
                _peer_copy(refs[a], refs[n + a], send_sems, recv_sems, a, k, peer,
                           peer_idx if scatter else None, me).start()
        refs[-1][...] = jnp.zeros_like(refs[-1])

    lands = [lax.empty((N_DEV,) + s.shape[-2:], s.dtype) for s in srcs]
    both = list(srcs) + lands
    out = pl.pallas_call(
        body, name=name,
        out_shape=(pltpu.SemaphoreType.DMA((n * (N_DEV - 1),)), pltpu.SemaphoreType.DMA((n * (N_DEV - 1),)),
                   *[pltpu.HBM(t.shape, t.dtype) for t in both], jax.ShapeDtypeStruct((8, LANES), F32)),
        in_specs=(_HBM,) * (2 * n) + (_ANY,),
        out_specs=(_SEM, _SEM) + (_HBM,) * (2 * n) + (pl.BlockSpec(memory_space=pltpu.VMEM),),
        input_output_aliases={i: 2 + i for i in range(2 * n)},
        compiler_params=pltpu.CompilerParams(has_side_effects=_EFFECT),
    )(*[pltpu.with_memory_space_constraint(t, pltpu.HBM) for t in both], after)
    return out[0], out[1], list(out[2:2 + n]), list(out[2 + n:2 + 2 * n]), out[-1]


def _send_wait(handles, name, scatter, afters):
    send_sems, recv_sems, srcs, lands = handles[:4]
    n = len(srcs)

    def body(*refs):
        send_sems, recv_sems = refs[2 * n], refs[2 * n + 1]
        me, peers = _peers()
        for a in range(n):
            for k, (peer, peer_idx) in enumerate(peers):
                cp = _peer_copy(refs[a], refs[n + a], send_sems, recv_sems, a, k, peer,
                                peer_idx if scatter else None, peer_idx)
                cp.wait_send()
                cp.wait_recv()

    both = list(srcs) + list(lands)
    out = pl.pallas_call(
        body, name=name,
        out_shape=tuple(pltpu.HBM(t.shape, t.dtype) for t in both),
        in_specs=(_HBM,) * (2 * n) + (_SEM, _SEM) + (_ANY,) * len(afters),
        out_specs=(_HBM,) * (2 * n),
        input_output_aliases={i: i for i in range(2 * n)},
        compiler_params=pltpu.CompilerParams(has_side_effects=_EFFECT),
    )(*both, send_sems, recv_sems, *afters)
    return list(out[:n]), list(out[n:])


def _sum_in_device_order(me, land_ref, own):
    acc = None
    for j in range(N_DEV):
        term = jnp.where(me == j, own, land_ref[j]).astype(F32)
        acc = term if acc is None else acc + term
    return acc


def _adam_math(w, g, m, v):
    c1 = 1.0 - ADAM_B1 ** ADAM_STEP
    c2 = 1.0 - ADAM_B2 ** ADAM_STEP
    m_new = ADAM_B1 * m + (1.0 - ADAM_B1) * g
    v_new = ADAM_B2 * v + (1.0 - ADAM_B2) * (g * g)
    delta = -ADAM_LR * ((m_new / c1) / (jnp.sqrt(v_new / c2) + ADAM_EPS) + ADAM_WD * w)
    return delta, m_new, v_new


def _row_tile(rows):
    return max(t for t in range(8, 257, 8) if rows % t == 0) if rows % 8 == 0 else rows


def _sum_update(land, blocks, me, w, m, v, name):
    _, rows, cols = land.shape
    tile_rows = _row_tile(rows)

    def body(me_ref, land_ref, own_ref, w_ref, m_ref, v_ref, g_ref, d_ref, nm_ref, nv_ref):
        g = _sum_in_device_order(me_ref[0], land_ref, own_ref[...])
        g_ref[...] = g
        d_ref[...], nm_ref[...], nv_ref[...] = _adam_math(w_ref[...], g, m_ref[...], v_ref[...])

    tile = pl.BlockSpec((tile_rows, cols), lambda i, me_ref: (i, 0))
    out = jax.ShapeDtypeStruct((rows, cols), F32)
    return pl.pallas_call(
        body, name=name,
        grid_spec=pltpu.PrefetchScalarGridSpec(
            num_scalar_prefetch=1, grid=(rows // tile_rows,),
            in_specs=[pl.BlockSpec((N_DEV, tile_rows, cols), lambda i, me_ref: (0, i, 0)),
                      pl.BlockSpec((None, tile_rows, cols), lambda i, me_ref: (me_ref[0], i, 0)),
                      tile, tile, tile],
            out_specs=[tile] * 4),
        out_shape=[out] * 4,
        compiler_params=_params(("parallel",)),
    )(me, land, blocks, w, m, v)


def _sum_gains(land, own, me):
    def body(me_ref, land_ref, own_ref, o_ref):
        o_ref[...] = _sum_in_device_order(me_ref[0], land_ref, own_ref[...])

    return pl.pallas_call(
        body, name="sum_gain_grads",
        grid_spec=pltpu.PrefetchScalarGridSpec(
            num_scalar_prefetch=1, grid=(1,),
            in_specs=[pl.BlockSpec(land.shape, lambda i, me_ref: (0, 0, 0)),
                      pl.BlockSpec(own.shape, lambda i, me_ref: (0, 0))],
            out_specs=pl.BlockSpec(own.shape, lambda i, me_ref: (0, 0))),
        out_shape=jax.ShapeDtypeStruct(own.shape, F32),
    )(me, land, own)


def _adamw(w, g, m, v, name):
    def body(w_ref, g_ref, m_ref, v_ref, d_ref, nm_ref, nv_ref):
        d_ref[...], nm_ref[...], nv_ref[...] = _adam_math(w_ref[...], g_ref[...], m_ref[...], v_ref[...])

    whole = pl.BlockSpec(w.shape, lambda i: (0, 0))
    out = jax.ShapeDtypeStruct(w.shape, F32)
    return pl.pallas_call(
        body, name=name, grid=(1,),
        in_specs=[whole] * 4, out_specs=[whole] * 3, out_shape=[out] * 3,
    )(w, g, m, v)


GROUP_REST = ("w_sb_up", "w_dil_up", "w_out", "w_ffn_in", "w_ffn_out")
COL_SHARDED = ("w_in", "w_sb_up", "w_dil_up", "w_ffn_in")


def _full_from_shards(name, slots):
    _, r, c = slots.shape
    if name in COL_SHARDED:
        return slots.transpose(1, 0, 2).reshape(r, N_DEV * c)
    return slots.reshape(N_DEV * r, c)


def _shards_from_full(name, full):
    rows, cols = full.shape
    if name in COL_SHARDED:
        return full.reshape(rows, N_DEV, cols // N_DEV).transpose(1, 0, 2)
    return full.reshape(N_DEV, rows // N_DEV, cols)


def _local_step(x, target, g_mix, g_ffn, g_fin, w_in, take_rest, give_rest, give_first):
    w = {"w_in": w_in}
    qkv_sb, qkv_dl, gates, u = _norm_proj(x, g_mix, w["w_in"])
    qkv_sb = qkv_sb.reshape(B_LOC, SEQ, 3 * SB_WIDTH)
    qkv_dl = qkv_dl.reshape(B_LOC, SEQ, 3 * DIL_WIDTH)
    o_sb, sb_tot = _sb_fwd(qkv_sb)
    o_sb = o_sb.reshape(TOK, SB_WIDTH)
    o_dl, lse = _dil_fwd(qkv_dl)
    o_dl = o_dl.reshape(TOK, DIL_OUT)

    w.update(take_rest(o_sb, o_dl))
    x1, merged = _mix_out(x, o_sb, o_dl, gates, w["w_sb_up"], w["w_dil_up"], w["w_out"])
    loss, dx1, u2, act, dh, dx2, dg_fin, dg_ffn = _ffn_fwd_bwd(x1, target, g_ffn, g_fin, w["w_ffn_in"], w["w_ffn_out"])
    dgates, dy_sb, dy_dl, do_sb, do_dl, dsum = _mix_bwd(dx1, o_sb, o_dl, gates, w["w_sb_up"], w["w_dil_up"], w["w_out"])
    token = give_rest({
        "w_sb_up": _atb(o_sb, dy_sb, "grad_w_sb_up", SB_WIDTH, D_MODEL, col_blocks=N_DEV),
        "w_dil_up": _atb(o_dl.astype(BF16), dy_dl, "grad_w_dil_up", DIL_OUT, D_MODEL, col_blocks=N_DEV),
        "w_out": _shards_from_full("w_out", _atb(merged, dx1.astype(BF16), "grad_w_out", D_MODEL, D_MODEL)),
        "w_ffn_in": _shards_from_full("w_ffn_in", _atb(u2, dh, "grad_w_ffn_in", D_MODEL, D_FF)),
        "w_ffn_out": _shards_from_full("w_ffn_out", _atb(act, dx2, "grad_w_ffn_out", D_FF // 2, D_MODEL)),
    })

    dq_sb, dk_sb, dv_sb = _sb_bwd(qkv_sb, do_sb.reshape(B_LOC, SEQ, SB_WIDTH), sb_tot, token)
    as_batch = lambda t: t.reshape(B_LOC, SEQ, DIL_OUT)
    d_dl = _dil_bwd(qkv_dl, as_batch(do_dl), lse, as_batch(dsum))
    flat = lambda t: t.reshape(TOK, -1)
    dproj = jnp.concatenate(
        [flat(dq_sb), flat(dk_sb), flat(dv_sb)]
        + [flat(d_dl[3 * grp + part]).astype(BF16) for part in range(3) for grp in range(DIL_GROUPS)]
        + [dgates], axis=1)

    token = give_first(_shards_from_full("w_in", _atb(u, dproj, "grad_w_in", D_MODEL, IN_WIDTH // 2)))
    grad_x, dg_mix = _proj_bwd(dproj, dx1, x, g_mix + token[:1, :1], w["w_in"])
    gain_grads = jnp.concatenate([dg_mix, dg_ffn, dg_fin], axis=0)
    return loss, grad_x, gain_grads


def kernel(x, norm_mix_g, w_in, w_sb_up, w_dil_up, w_out, norm_ffn_g, w_ffn_in, w_ffn_out, norm_final_g, loss_target, m_norm_mix_g, m_w_in, m_w_sb_up, m_w_dil_up, m_w_out, m_norm_ffn_g, m_w_ffn_in, m_w_ffn_out, m_norm_final_g, v_norm_mix_g, v_w_in, v_w_sb_up, v_w_dil_up, v_w_out, v_norm_ffn_g, v_w_ffn_in, v_w_ffn_out, v_norm_final_g):
    mats = {"w_in": w_in, "w_sb_up": w_sb_up, "w_dil_up": w_dil_up, "w_out": w_out,
            "w_ffn_in": w_ffn_in, "w_ffn_out": w_ffn_out}
    moments_m = {"w_in": m_w_in, "w_sb_up": m_w_sb_up, "w_dil_up": m_w_dil_up, "w_out": m_w_out,
                 "w_ffn_in": m_w_ffn_in, "w_ffn_out": m_w_ffn_out}
    moments_v = {"w_in": v_w_in, "w_sb_up": v_w_sb_up, "w_dil_up": v_w_dil_up, "w_out": v_w_out,
                 "w_ffn_in": v_w_ffn_in, "w_ffn_out": v_w_ffn_out}
    me = 4 * lax.axis_index("x") + 2 * lax.axis_index("y") + lax.axis_index("c")
    me_arr = me.astype(jnp.int32).reshape(1)

    def own_slot(landed, block):
        return lax.dynamic_update_slice(landed, block[None], (me, 0, 0))

    gathered_w_in = _all_gather(w_in[0].astype(BF16))
    rest_handles = _send_start([mats[name][0].astype(BF16) for name in GROUP_REST], "gather_rest_start", False,
                               gathered_w_in)

    def take_rest(o_sb, o_dl):
        sent, landed = _send_wait(rest_handles, "gather_rest_wait", False, (o_sb, o_dl))
        return {name: _full_from_shards(name, own_slot(slots, shard))
                for name, shard, slots in zip(GROUP_REST, sent, landed)}

    grad_handles = []

    def give_rest(grads):
        handles = _send_start([grads[name] for name in GROUP_REST], "grads_rest_start", True, grads["w_out"])
        grad_handles.append(handles)
        return handles[4]

    rest_landed, first_handles = [], []

    def give_first(blocks):
        rest_landed.append(_send_wait(grad_handles[0], "grads_rest_wait", True, (blocks,)))
        handles = _send_start([blocks], "grads_first_start", True, rest_landed[0][1][0])
        first_handles.append(handles)
        return handles[4]

    g_fin = norm_final_g.reshape(1, D_MODEL)
    loss, grad_x, gain_grads = _local_step(
        x.reshape(TOK, D_MODEL), loss_target.reshape(TOK, D_MODEL), norm_mix_g + rest_handles[4][:1, :1],
        norm_ffn_g, g_fin, _full_from_shards("w_in", gathered_w_in), take_rest, give_rest, give_first)

    out_g, out_d, out_m, out_v = {}, {}, {}, {}

    def update(names, sent, landed):
        for name, blocks, slots in zip(names, sent, landed):
            g, d, nm, nv = _sum_update(slots, blocks, me_arr, mats[name][0], moments_m[name][0], moments_v[name][0],
                                       "update_" + name)
            out_g[name], out_d[name], out_m[name], out_v[name] = g[None], d[None], nm[None], nv[None]

    gain_rows = jnp.concatenate([gain_grads, jnp.tile(loss, (1, D_MODEL // LANES)),
                                 jnp.zeros((8 - 4, D_MODEL), F32)], axis=0)
    gain_handles = _send_start([gain_rows], "gains_start", False, grad_x)
    update(GROUP_REST, *rest_landed[0])
    sent, landed = _send_wait(gain_handles, "gains_wait", False, (out_d["w_ffn_in"],))
    g_gains = _sum_gains(landed[0], sent[0], me_arr)
    sent, landed = _send_wait(first_handles[0], "grads_first_wait", True, (out_d["w_ffn_in"], g_gains))
    update(("w_in",), sent, landed)

    gain_w = jnp.concatenate([norm_mix_g, norm_ffn_g, g_fin], axis=0)
    gain_m = jnp.concatenate([m_norm_mix_g, m_norm_ffn_g, m_norm_final_g.reshape(1, D_MODEL)], axis=0)
    gain_v = jnp.concatenate([v_norm_mix_g, v_norm_ffn_g, v_norm_final_g.reshape(1, D_MODEL)], axis=0)
    gd, gm, gv = _adamw(gain_w, g_gains[:3], gain_m, gain_v, "adamw_gains")
    for idx, name in enumerate(("norm_mix_g", "norm_ffn_g", "norm_final_g")):
        shape = (D_MODEL,) if name == "norm_final_g" else (1, D_MODEL)
        out_g[name] = g_gains[idx].reshape(shape)
        out_d[name], out_m[name], out_v[name] = gd[idx].reshape(shape), gm[idx].reshape(shape), gv[idx].reshape(shape)

    order = ("norm_mix_g", "w_in", "w_sb_up", "w_dil_up", "w_out", "norm_ffn_g", "w_ffn_in", "w_ffn_out",
             "norm_final_g")
    return (g_gains[3, 0], grad_x.reshape(B_LOC, SEQ, D_MODEL),
            *[out_g[n] for n in order], *[out_d[n] for n in order],
            *[out_m[n] for n in order], *[out_v[n] for n in order])
```

```python
import math

import jax
import jax.numpy as jnp
from jax import lax
from jax.experimental import pallas as pl
from jax.experimental.pallas import tpu as pltpu

F32 = jnp.float32
BF16 = jnp.bfloat16

N_DEV = 8
D_MODEL = 1024
SEQ = 2048
B_LOC = 2
TOK = B_LOC * SEQ
HEAD_DIM = 64
SB_WIDTH = 512
DIL_WIDTH = 768
DIL_OUT = 256
QKV_WIDTH = 3 * SB_WIDTH + 3 * DIL_WIDTH
IN_WIDTH = QKV_WIDTH + 2 * D_MODEL
D_FF = 2816
DIL_PAIRS = ((128, 1), (512, 4), (2048, 16))
DIL_HEADS = 12
RMS_EPS = 1e-6
ALIBI_MAX_BIAS = 8.0
QK_SCALE = 1.0 / math.sqrt(HEAD_DIM)
BLK = 128
LANES = 128
NEG_BIG = -1e30

ADAM_LR = 0.001
ADAM_B1 = 0.9
ADAM_B2 = 0.999
ADAM_EPS = 1e-08
ADAM_WD = 0.01
ADAM_STEP = 10

VMEM_LIMIT = 56 * 1024 * 1024


def _dot(a, b):
    return jnp.dot(a, b, preferred_element_type=F32)


def _dot_nt(a, b):
    return lax.dot_general(a, b, (((1,), (1,)), ((), ())), preferred_element_type=F32)


def _dot_tn(a, b):
    return lax.dot_general(a, b, (((0,), (0,)), ((), ())), preferred_element_type=F32)


def _softplus(z):
    return jnp.maximum(z, 0.0) + jnp.log1p(jnp.exp(-jnp.abs(z)))


def _sigmoid(z):
    return 1.0 / (1.0 + jnp.exp(-z))


def _split_bf16(v):
    hi = v.astype(BF16)
    lo = (v - hi.astype(F32)).astype(BF16)
    return hi, lo


def _chunks(width, step=512):
    out, c = [], 0
    while c < width:
        w = min(step, width - c)
        out.append((c, w))
        c += w
    return out


def _resident(shape):
    nd = len(shape)
    return pl.BlockSpec(shape, lambda *_: (0,) * nd, pipeline_mode=pl.Buffered(1))


def _params(sem):
    return pltpu.CompilerParams(dimension_semantics=sem, vmem_limit_bytes=VMEM_LIMIT)


def _rms_fwd(x, g):
    r = lax.rsqrt(jnp.mean(x * x, axis=-1, keepdims=True) + RMS_EPS)
    n = x * r
    return n, r, n * g


def _rms_bwd(dy, n, r, g):
    dg = jnp.sum(dy * n, axis=0, keepdims=True)
    dn = dy * g
    dx = r * (dn - n * jnp.mean(dn * n, axis=-1, keepdims=True))
    return dx, dg


TM = 256


def _norm_proj(x, g, w_in):
    def body(x_ref, g_ref, w_ref, sb_ref, dl_ref, gate_ref, u_ref):
        _, _, u = _rms_fwd(x_ref[...], g_ref[...])
        u = u.astype(BF16)
        u_ref[...] = u
        for c0, w in _chunks(3 * SB_WIDTH):
            sb_ref[:, c0:c0 + w] = _dot(u, w_ref[:, c0:c0 + w]).astype(BF16)
        for c0, w in _chunks(3 * DIL_WIDTH):
            dl_ref[:, c0:c0 + w] = _dot(u, w_ref[:, 3 * SB_WIDTH + c0:3 * SB_WIDTH + c0 + w])
        for c0, w in _chunks(2 * D_MODEL):
            gate_ref[:, c0:c0 + w] = _dot(u, w_ref[:, QKV_WIDTH + c0:QKV_WIDTH + c0 + w])

    return pl.pallas_call(
        body, name="norm_proj", grid=(TOK // TM,),
        in_specs=[pl.BlockSpec((TM, D_MODEL), lambda i: (i, 0)), _resident((1, D_MODEL)),
                  _resident((D_MODEL, IN_WIDTH))],
        out_specs=[pl.BlockSpec((TM, 3 * SB_WIDTH), lambda i: (i, 0)),
                   pl.BlockSpec((TM, 3 * DIL_WIDTH), lambda i: (i, 0)),
                   pl.BlockSpec((TM, 2 * D_MODEL), lambda i: (i, 0)),
                   pl.BlockSpec((TM, D_MODEL), lambda i: (i, 0))],
        out_shape=[jax.ShapeDtypeStruct((TOK, 3 * SB_WIDTH), BF16),
                   jax.ShapeDtypeStruct((TOK, 3 * DIL_WIDTH), F32),
                   jax.ShapeDtypeStruct((TOK, 2 * D_MODEL), F32),
                   jax.ShapeDtypeStruct((TOK, D_MODEL), BF16)],
        compiler_params=_params(("parallel",)),
    )(x, g, w_in)


def _mix_out(x, o_sb, o_dl, gates, w_sb_up, w_dil_up, w_out):
    def body(x_ref, osb_ref, odl_ref, gate_ref, wsb_ref, wdl_ref, wout_ref, x1_ref, mg_ref):
        y_sb = _dot(osb_ref[...], wsb_ref[...])
        y_dl = _dot(odl_ref[...].astype(BF16), wdl_ref[...])
        merged = (_sigmoid(gate_ref[:, :D_MODEL]) * y_sb
                  + _sigmoid(gate_ref[:, D_MODEL:]) * y_dl).astype(BF16)
        mg_ref[...] = merged
        x1_ref[...] = x_ref[...] + _dot(merged, wout_ref[...])

    return pl.pallas_call(
        body, name="mix_out", grid=(TOK // TM,),
        in_specs=[pl.BlockSpec((TM, D_MODEL), lambda i: (i, 0)),
                  pl.BlockSpec((TM, SB_WIDTH), lambda i: (i, 0)),
                  pl.BlockSpec((TM, DIL_OUT), lambda i: (i, 0)),
                  pl.BlockSpec((TM, 2 * D_MODEL), lambda i: (i, 0)),
                  _resident((SB_WIDTH, D_MODEL)), _resident((DIL_OUT, D_MODEL)),
                  _resident((D_MODEL, D_MODEL))],
        out_specs=[pl.BlockSpec((TM, D_MODEL), lambda i: (i, 0)),
                   pl.BlockSpec((TM, D_MODEL), lambda i: (i, 0))],
        out_shape=[jax.ShapeDtypeStruct((TOK, D_MODEL), F32),
                   jax.ShapeDtypeStruct((TOK, D_MODEL), BF16)],
        compiler_params=_params(("parallel",)),
    )(x, o_sb, o_dl, gates, w_sb_up, w_dil_up, w_out)


FF_CHUNK = D_FF // 2


def _ffn_fwd_bwd(x1, target, g_ffn, g_fin, w_ffn_in, w_ffn_out):
    def body(x1_ref, t_ref, gffn_ref, gfin_ref, win_ref, wout_ref,
             loss_ref, dx1_ref, u2_ref, act_ref, dh_ref, dx2_ref, dgfin_ref, dgffn_ref, h_scr):
        i = pl.program_id(0)

        @pl.when(i == 0)
        def _():
            loss_ref[...] = jnp.zeros_like(loss_ref)
            dgfin_ref[...] = jnp.zeros_like(dgfin_ref)
            dgffn_ref[...] = jnp.zeros_like(dgffn_ref)

        x1 = x1_ref[...]
        g_ffn_v = gffn_ref[...]
        g_fin_v = gfin_ref[...]
        n2, r2, u2 = _rms_fwd(x1, g_ffn_v)
        u2 = u2.astype(BF16)
        u2_ref[...] = u2
        x2 = x1
        for c0 in range(0, D_FF, FF_CHUNK):
            gate = _dot(u2, win_ref[:, c0:c0 + FF_CHUNK])
            up = _dot(u2, win_ref[:, D_FF + c0:D_FF + c0 + FF_CHUNK])
            h_scr[:, c0:c0 + FF_CHUNK] = gate
            h_scr[:, D_FF + c0:D_FF + c0 + FF_CHUNK] = up
            act = (gate * _sigmoid(gate) * up).astype(BF16)
            act_ref[:, c0:c0 + FF_CHUNK] = act
            x2 = x2 + _dot(act, wout_ref[c0:c0 + FF_CHUNK, :])
        n3, r3, y = _rms_fwd(x2, g_fin_v)
        err = y - t_ref[...]
        sq = jnp.sum(jnp.sum(err * err, axis=1, keepdims=True), axis=0, keepdims=True)
        loss_ref[...] += sq * (0.5 / D_MODEL)
        dx2, dgfin = _rms_bwd(err * (1.0 / D_MODEL), n3, r3, g_fin_v)
        dgfin_ref[...] += dgfin
        dx2_b = dx2.astype(BF16)
        dx2_ref[...] = dx2_b
        du2 = jnp.zeros((TM, D_MODEL), F32)
        for c0 in range(0, D_FF, FF_CHUNK):
            gate = h_scr[:, c0:c0 + FF_CHUNK]
            up = h_scr[:, D_FF + c0:D_FF + c0 + FF_CHUNK]
            dact = _dot_nt(dx2_b, wout_ref[c0:c0 + FF_CHUNK, :])
            sg = _sigmoid(gate)
            dgate = (dact * up * (sg * (1.0 + gate * (1.0 - sg)))).astype(BF16)
            dup = (dact * (gate * sg)).astype(BF16)
            dh_ref[:, c0:c0 + FF_CHUNK] = dgate
            dh_ref[:, D_FF + c0:D_FF + c0 + FF_CHUNK] = dup
            du2 = du2 + _dot_nt(dgate, win_ref[:, c0:c0 + FF_CHUNK])
            du2 = du2 + _dot_nt(dup, win_ref[:, D_FF + c0:D_FF + c0 + FF_CHUNK])
        dx1_n, dgffn = _rms_bwd(du2, n2, r2, g_ffn_v)
        dgffn_ref[...] += dgffn
        dx1_ref[...] = dx2 + dx1_n

    tile = lambda w: pl.BlockSpec((TM, w), lambda i: (i, 0))
    acc = lambda w: pl.BlockSpec((1, w), lambda i: (0, 0))
    return pl.pallas_call(
        body, name="ffn_fwd_bwd", grid=(TOK // TM,),
        in_specs=[tile(D_MODEL), tile(D_MODEL), _resident((1, D_MODEL)), _resident((1, D_MODEL)),
                  _resident((D_MODEL, 2 * D_FF)), _resident((D_FF, D_MODEL))],
        out_specs=[acc(LANES), tile(D_MODEL), tile(D_MODEL), tile(D_FF), tile(2 * D_FF), tile(D_MODEL),
                   acc(D_MODEL), acc(D_MODEL)],
        out_shape=[jax.ShapeDtypeStruct((1, LANES), F32),
                   jax.ShapeDtypeStruct((TOK, D_MODEL), F32),
                   jax.ShapeDtypeStruct((TOK, D_MODEL), BF16),
                   jax.ShapeDtypeStruct((TOK, D_FF), BF16),
                   jax.ShapeDtypeStruct((TOK, 2 * D_FF), BF16),
                   jax.ShapeDtypeStruct((TOK, D_MODEL), BF16),
                   jax.ShapeDtypeStruct((1, D_MODEL), F32),
                   jax.ShapeDtypeStruct((1, D_MODEL), F32)],
        scratch_shapes=[pltpu.VMEM((TM, 2 * D_FF), F32)],
        compiler_params=_params(("arbitrary",)),
    )(x1, target, g_ffn, g_fin, w_ffn_in, w_ffn_out)


def _mix_bwd(dx1, o_sb, o_dl, gates, w_sb_up, w_dil_up, w_out):
    def body(dx1_ref, osb_ref, odl_ref, gate_ref, wsb_ref, wdl_ref, wout_ref,
             dgate_ref, dysb_ref, dydl_ref, dosb_ref, dodl_ref, dsum_ref):
        dmerged = _dot_nt(dx1_ref[...].astype(BF16), wout_ref[...])
        o_dl = odl_ref[...]
        y_sb = _dot(osb_ref[...], wsb_ref[...])
        y_dl = _dot(o_dl.astype(BF16), wdl_ref[...])
        s_sb = _sigmoid(gate_ref[:, :D_MODEL])
        s_dl = _sigmoid(gate_ref[:, D_MODEL:])
        dgate_ref[:, :D_MODEL] = (dmerged * y_sb * (s_sb * (1.0 - s_sb))).astype(BF16)
        dgate_ref[:, D_MODEL:] = (dmerged * y_dl * (s_dl * (1.0 - s_dl))).astype(BF16)
        dy_sb = (dmerged * s_sb).astype(BF16)
        dy_dl = (dmerged * s_dl).astype(BF16)
        dysb_ref[...] = dy_sb
        dydl_ref[...] = dy_dl
        dosb_ref[...] = _dot_nt(dy_sb, wsb_ref[...]).astype(BF16)
        do_dl = _dot_nt(dy_dl, wdl_ref[...])
        dodl_ref[...] = do_dl
        row = lax.broadcasted_iota(jnp.int32, (DIL_OUT, DIL_OUT), 0) // HEAD_DIM
        col = lax.broadcasted_iota(jnp.int32, (DIL_OUT, DIL_OUT), 1) // HEAD_DIM
        same_head = (row == col).astype(BF16)
        hi, lo = _split_bf16(do_dl * o_dl)
        dsum_ref[...] = _dot(hi, same_head) + _dot(lo, same_head)

    tile = lambda w: pl.BlockSpec((TM, w), lambda i: (i, 0))
    return pl.pallas_call(
        body, name="mix_bwd", grid=(TOK // TM,),
        in_specs=[tile(D_MODEL), tile(SB_WIDTH), tile(DIL_OUT), tile(2 * D_MODEL),
                  _resident((SB_WIDTH, D_MODEL)), _resident((DIL_OUT, D_MODEL)),
                  _resident((D_MODEL, D_MODEL))],
        out_specs=[tile(2 * D_MODEL), tile(D_MODEL), tile(D_MODEL), tile(SB_WIDTH), tile(DIL_OUT),
                   tile(DIL_OUT)],
        out_shape=[jax.ShapeDtypeStruct((TOK, 2 * D_MODEL), BF16),
                   jax.ShapeDtypeStruct((TOK, D_MODEL), BF16),
                   jax.ShapeDtypeStruct((TOK, D_MODEL), BF16),
                   jax.ShapeDtypeStruct((TOK, SB_WIDTH), BF16),
                   jax.ShapeDtypeStruct((TOK, DIL_OUT), F32),
                   jax.ShapeDtypeStruct((TOK, DIL_OUT), F32)],
        compiler_params=_params(("parallel",)),
    )(dx1, o_sb, o_dl, gates, w_sb_up, w_dil_up, w_out)


def _proj_bwd(dproj, dx1, x, g, w_in, send=None):
    def body(dp_ref, dx1_ref, x_ref, g_ref, w_ref, dx_ref, dg_ref):
        @pl.when(pl.program_id(0) == 0)
        def _():
            dg_ref[...] = jnp.zeros_like(dg_ref)

        du = jnp.zeros((TM, D_MODEL), F32)
        for c0, w in _chunks(IN_WIDTH, 1024):
            du = du + _dot_nt(dp_ref[:, c0:c0 + w], w_ref[:, c0:c0 + w])
        g_v = g_ref[...]
        n, r, _ = _rms_fwd(x_ref[...], g_v)
        dx, dg = _rms_bwd(du, n, r, g_v)
        dg_ref[...] += dg
        dx_ref[...] = dx1_ref[...] + dx

    tile = lambda w: pl.BlockSpec((TM, w), lambda i: (i, 0))
    return _call(
        body, send, name="proj_bwd", grid=(TOK // TM,),
        in_specs=[tile(IN_WIDTH), tile(D_MODEL), tile(D_MODEL), _resident((1, D_MODEL)),
                  _resident((D_MODEL, IN_WIDTH))],
        out_specs=[tile(D_MODEL), pl.BlockSpec((1, D_MODEL), lambda i: (0, 0))],
        out_shape=[jax.ShapeDtypeStruct((TOK, D_MODEL), F32),
                   jax.ShapeDtypeStruct((1, D_MODEL), F32)],
        scratch_shapes=[], semantics=("arbitrary",), operands=(dproj, dx1, x, g, w_in))


def _atb(a, b, name, tm, tn, col_blocks=0, tk=512):
    m, n = a.shape[1], b.shape[1]
    nk = TOK // tk

    def body(a_ref, b_ref, o_ref, acc_ref):
        k = pl.program_id(2)

        @pl.when(k == 0)
        def _():
            acc_ref[...] = jnp.zeros_like(acc_ref)

        acc_ref[...] += _dot_tn(a_ref[...], b_ref[...])

        @pl.when(k == nk - 1)
        def _():
            if col_blocks:
                width = n // col_blocks
                for blk in range(col_blocks):
                    o_ref[blk] = acc_ref[:, blk * width:(blk + 1) * width].astype(BF16)
            else:
                o_ref[...] = acc_ref[...].astype(BF16)

    if col_blocks:
        out_spec = pl.BlockSpec((col_blocks, tm, n // col_blocks), lambda i, j, k: (0, i, 0))
        out_shape = jax.ShapeDtypeStruct((col_blocks, m, n // col_blocks), BF16)
    else:
        out_spec = pl.BlockSpec((tm, tn), lambda i, j, k: (i, j))
        out_shape = jax.ShapeDtypeStruct((m, n), BF16)
    return pl.pallas_call(
        body, name=name, grid=(m // tm, n // tn, nk),
        in_specs=[pl.BlockSpec((tk, tm), lambda i, j, k: (k, i)),
                  pl.BlockSpec((tk, tn), lambda i, j, k: (k, j))],
        out_specs=out_spec, out_shape=out_shape,
        scratch_shapes=[pltpu.VMEM((tm, tn), F32)],
        compiler_params=_params(("parallel", "parallel", "arbitrary")),
    )(a, b)


SB_PAIRS = SB_WIDTH // LANES


def _two_heads(v, lane0):
    zero = jnp.zeros_like(v)
    return jnp.where(lane0, v, zero), jnp.where(lane0, zero, v)


SB_QBLK = 256
N_SB_STEPS = SEQ // SB_QBLK


SB_KCHUNK = 2 * BLK
SB_ROWS = 2 * SB_QBLK
SB_DEAD = -104.0


def _log_keep(z):
    neg_z = -z
    return jnp.minimum(neg_z, 0.0) - jnp.log(1.0 + jnp.exp(jnp.minimum(z, neg_z)))


def _stack_heads(v, lane0):
    return jnp.concatenate(_two_heads(v, lane0), axis=0)


def _block_sums(v, tri):
    halves = (v[:, :BLK], v[:, BLK:])
    hi, lo = _split_bf16(jnp.concatenate(halves, axis=0))
    prod = _dot(jnp.concatenate([hi, lo], axis=0), tri)
    tri_sum = prod[:2 * SB_ROWS] + prod[2 * SB_ROWS:]
    sums = tuple(jnp.sum(h, axis=1, keepdims=True) for h in halves)
    return (tri_sum[:SB_ROWS], tri_sum[SB_ROWS:]), sums


def _sb_diag_mask():
    row = lax.broadcasted_iota(jnp.int32, (SB_ROWS, SB_KCHUNK), 0)
    col = lax.broadcasted_iota(jnp.int32, (SB_ROWS, SB_KCHUNK), 1)
    return col < jnp.where(row >= SB_QBLK, row - SB_QBLK, row)


def _sb_fwd(qkv, send=None):
    def body(q_ref, k_ref, v_ref, o_ref):
        i = pl.program_id(2)
        krow = lax.broadcasted_iota(jnp.int32, (BLK, BLK), 0)
        kcol = lax.broadcasted_iota(jnp.int32, (BLK, BLK), 1)
        later = (krow > kcol).astype(BF16)
        lane0 = lax.broadcasted_iota(jnp.int32, (SB_QBLK, LANES), 1) < HEAD_DIM
        q2 = _stack_heads(q_ref[0] * QK_SCALE, lane0)

        def chunk(c, carry, causal):
            acc, run = carry
            off = pl.multiple_of(c * SB_KCHUNK, SB_KCHUNK)
            z = _dot_nt(q2, k_ref[0, pl.ds(off, SB_KCHUNK), :])
            log_keep = _log_keep(z)
            if causal is not None:
                log_keep = jnp.where(causal, log_keep, 0.0)
            suffix, sums = _block_sums(log_keep, later)
            log_after = jnp.concatenate([suffix[0] + (run + sums[1]), suffix[1] + run], axis=1)
            a = jnp.exp(log_keep + z + log_after)
            if causal is not None:
                a = jnp.where(causal, a, 0.0)
            acc = acc + _dot(a.astype(BF16), v_ref[0, pl.ds(off, SB_KCHUNK), :])
            return acc, run + (sums[0] + sums[1])

        acc, run = chunk(i, (jnp.zeros((SB_ROWS, LANES), F32), jnp.zeros((SB_ROWS, 1), F32)), _sb_diag_mask())

        def live(state):
            t, _, run = state
            return jnp.logical_and(t < i, jnp.max(run) > SB_DEAD)

        def trip(state):
            t, acc, run = state
            acc, run = chunk(i - 1 - t, (acc, run), None)
            return t + 1, acc, run

        _, acc, _ = lax.while_loop(live, trip, (jnp.int32(0), acc, run))
        o_ref[0] = jnp.where(lane0, acc[:SB_QBLK], acc[SB_QBLK:]).astype(BF16)

    blk = pl.BlockSpec((1, SB_QBLK, LANES), lambda b, h, i: (b, i, h))
    return _call(
        body, send, name="sb_fwd", grid=(B_LOC, SB_PAIRS, N_SB_STEPS),
        in_specs=[blk,
                  pl.BlockSpec((1, SEQ, LANES), lambda b, h, i: (b, 0, SB_PAIRS + h)),
                  pl.BlockSpec((1, SEQ, LANES), lambda b, h, i: (b, 0, 2 * SB_PAIRS + h))],
        out_specs=[blk], out_shape=[jax.ShapeDtypeStruct((B_LOC, SEQ, SB_WIDTH), BF16)],
        scratch_shapes=[], semantics=("parallel", "parallel", "arbitrary"), operands=(qkv, qkv, qkv))


def _sb_bwd(qkv, d_o, send=None):
    def body(q_ref, k_ref, v_ref, do_ref, dq_ref, dk_ref, dv_ref, dk_acc, dv_acc):
        i = pl.program_id(2)
        krow = lax.broadcasted_iota(jnp.int32, (BLK, BLK), 0)
        kcol = lax.broadcasted_iota(jnp.int32, (BLK, BLK), 1)
        upto = (krow <= kcol).astype(BF16)
        earlier = (krow < kcol).astype(BF16)
        lane0 = lax.broadcasted_iota(jnp.int32, (SB_QBLK, LANES), 1) < HEAD_DIM
        q2 = _stack_heads(q_ref[0] * QK_SCALE, lane0)
        do2 = _stack_heads(do_ref[0], lane0)

        def keep_sum(c, causal):
            off = pl.multiple_of(c * SB_KCHUNK, SB_KCHUNK)
            log_keep = _log_keep(_dot_nt(q2, k_ref[0, pl.ds(off, SB_KCHUNK), :]))
            if causal is not None:
                log_keep = jnp.where(causal, log_keep, 0.0)
            return jnp.sum(log_keep, axis=1, keepdims=True)

        def live(state):
            t, run = state
            return jnp.logical_and(t < i, jnp.max(run) > SB_DEAD)

        walked, tot2 = lax.while_loop(live, lambda s: (s[0] + 1, s[1] + keep_sum(i - 1 - s[0], None)),
                                      (jnp.int32(0), keep_sum(i, _sb_diag_mask())))
        first = i - walked

        @pl.when(i == 0)
        def _():
            dk_acc[...] = jnp.zeros_like(dk_acc)
            dv_acc[...] = jnp.zeros_like(dv_acc)

        def chunk(c, carry, causal):
            dq, pre_keep, pre_e = carry
            off = pl.multiple_of(c * SB_KCHUNK, SB_KCHUNK)
            k_c = k_ref[0, pl.ds(off, SB_KCHUNK), :]
            v_c = v_ref[0, pl.ds(off, SB_KCHUNK), :]
            z = _dot_nt(q2, k_c)
            d_a = _dot_nt(do2, v_c)
            log_keep = _log_keep(z)
            if causal is not None:
                log_keep = jnp.where(causal, log_keep, 0.0)
            log_beta = log_keep + z
            prefix, sums = _block_sums(log_keep, upto)
            inclusive = jnp.concatenate([prefix[0], prefix[1] + sums[0]], axis=1)
            a = jnp.exp(log_beta + ((tot2 - pre_keep) - inclusive))
            if causal is not None:
                a = jnp.where(causal, a, 0.0)
            e = d_a * a
            e_prefix, e_sums = _block_sums(e, earlier)
            before = jnp.concatenate([e_prefix[0] + pre_e, e_prefix[1] + (pre_e + e_sums[0])], axis=1)
            dz = e - (e + before) * jnp.exp(log_beta)
            if causal is not None:
                dz = jnp.where(causal, dz, 0.0)
            dz = dz.astype(BF16)
            dq = dq + _dot(dz, k_c)
            dk_acc[pl.ds(off, SB_KCHUNK), :] += _dot_tn(dz, q2)
            dv_acc[pl.ds(off, SB_KCHUNK), :] += _dot_tn(a.astype(BF16), do2)
            return dq, pre_keep + (sums[0] + sums[1]), pre_e + (e_sums[0] + e_sums[1])

        zero_col = jnp.zeros((SB_ROWS, 1), F32)
        carry = lax.fori_loop(first, i, lambda t, c: chunk(t, c, None),
                              (jnp.zeros((SB_ROWS, LANES), F32), zero_col, zero_col))
        dq, _, _ = chunk(i, carry, _sb_diag_mask())
        dq_ref[0] = (jnp.where(lane0, dq[:SB_QBLK], dq[SB_QBLK:]) * QK_SCALE).astype(BF16)

        @pl.when(i == N_SB_STEPS - 1)
        def _():
            dk_ref[0] = dk_acc[...].astype(BF16)
            dv_ref[0] = dv_acc[...].astype(BF16)

    blk = pl.BlockSpec((1, SB_QBLK, LANES), lambda b, h, i: (b, i, h))
    whole = lambda c: pl.BlockSpec((1, SEQ, LANES), lambda b, h, i: (b, 0, c * SB_PAIRS + h))
    out = jax.ShapeDtypeStruct((B_LOC, SEQ, SB_WIDTH), BF16)
    return _call(
        body, send, name="sb_bwd", grid=(B_LOC, SB_PAIRS, N_SB_STEPS),
        in_specs=[blk, whole(1), whole(2), blk],
        out_specs=[blk, whole(0), whole(0)],
        out_shape=[out, out, out],
        scratch_shapes=[pltpu.VMEM((SEQ, LANES), F32), pltpu.VMEM((SEQ, LANES), F32)],
        semantics=("parallel", "parallel", "arbitrary"), operands=(qkv, qkv, qkv, d_o))


DIL_GROUPS = len(DIL_PAIRS)
DIL_QBLOCKS = SEQ // BLK


def _residue_rows(j, dilation):
    length = SEQ // dilation
    return pl.ds(j, length, stride=dilation) if dilation > 1 else pl.ds(0, length)


def _gather_residues(src_ref, dst_ref, dst_off, dilation, scale=None):
    length = SEQ // dilation
    for j in range(dilation):
        v = src_ref[_residue_rows(j, dilation), :]
        if scale is not None:
            v = v * scale
        dst_ref[dst_off + j * length:dst_off + (j + 1) * length, :] = v.astype(dst_ref.dtype)


def _scatter_residues(src_ref, src_off, dst_ref, dilation):
    length = SEQ // dilation
    for j in range(dilation):
        dst_ref[_residue_rows(j, dilation), :] = (
            src_ref[src_off + j * length:src_off + (j + 1) * length, :].astype(dst_ref.dtype))


def _dil_geometry(group, pair):
    dilation = DIL_PAIRS[group][1]
    row = lax.broadcasted_iota(jnp.int32, (2 * BLK, 2 * BLK), 0)
    col = lax.broadcasted_iota(jnp.int32, (2 * BLK, 2 * BLK), 1)
    second = row >= BLK
    steps = BLK + jnp.where(second, row - BLK, row) - col
    coef = -ALIBI_MAX_BIAS / DIL_HEADS * math.log(2.0)
    first_head = float(4 * group + 1) + 2.0 * pair.astype(F32)
    slope = jnp.exp(coef * (first_head + jnp.where(second, 1.0, 0.0)))
    bias = slope * (steps * dilation).astype(F32)
    valid = jnp.logical_and(steps >= 0, steps <= BLK)
    return bias, valid, col >= BLK


def _dil_tile_scores(q2, kk, geometry, has_prev):
    bias, valid, own = geometry
    ok = jnp.logical_and(valid, jnp.logical_or(own, has_prev))
    return jnp.where(ok, _dot_nt(q2, kk) - bias, NEG_BIG)


def _head_col(v, lane_mask):
    return jnp.max(jnp.where(lane_mask, v, NEG_BIG), axis=1, keepdims=True)


def _dil_fwd(qkv, send=None):
    def body(*refs):
        ins, (o_ref, lse_ref), (qs, ks, vs, o_res, lse_res) = refs[:9], refs[9:11], refs[11:16]
        o_grp, lse_grp = refs[16:19], refs[19:22]
        pair = pl.program_id(1)
        lane0 = lax.broadcasted_iota(jnp.int32, (BLK, LANES), 1) < HEAD_DIM
        ks[0:BLK, :] = jnp.zeros((BLK, LANES), BF16)
        vs[0:BLK, :] = jnp.zeros((BLK, LANES), BF16)
        for grp, (_, dilation) in enumerate(DIL_PAIRS):
            q_ref, k_ref, v_ref = ins[3 * grp:3 * grp + 3]
            per_residue = DIL_QBLOCKS // dilation
            _gather_residues(q_ref, qs, 0, dilation, QK_SCALE)
            _gather_residues(k_ref, ks, BLK, dilation)
            _gather_residues(v_ref, vs, BLK, dilation)
            geometry = _dil_geometry(grp, pair)

            def step(blk, _):
                off = pl.multiple_of(blk * BLK, BLK)
                q2 = _stack_heads(qs[pl.ds(off, BLK), :], lane0)
                s = _dil_tile_scores(q2, ks[pl.ds(off, 2 * BLK), :], geometry, blk % per_residue != 0)
                m = jnp.max(s, axis=1, keepdims=True)
                p = jnp.exp(s - m)
                den = jnp.sum(p, axis=1, keepdims=True)
                out = _dot(p.astype(BF16), vs[pl.ds(off, 2 * BLK), :]) / den
                lse = m + jnp.log(den)
                o_res[pl.ds(off, BLK), :] = jnp.where(lane0, out[:BLK], out[BLK:])
                lse_res[pl.ds(off, BLK), :] = jnp.where(lane0, lse[:BLK], lse[BLK:])
                return 0

            lax.fori_loop(0, DIL_QBLOCKS, step, 0, unroll=4)
            _scatter_residues(o_res, 0, o_grp[grp], dilation)
            _scatter_residues(lse_res, 0, lse_grp[grp], dilation)

        for r0 in range(0, SEQ, 2 * BLK):
            rows = slice(r0, r0 + 2 * BLK)
            ls = [lse_grp[g][rows, :] for g in range(DIL_GROUPS)]
            m = jnp.maximum(jnp.maximum(ls[0], ls[1]), ls[2])
            w = [jnp.exp(l - m) for l in ls]
            den = w[0] + w[1] + w[2]
            o_ref[rows, :] = (w[0] * o_grp[0][rows, :] + w[1] * o_grp[1][rows, :] + w[2] * o_grp[2][rows, :]) / den
            lse_ref[rows, :] = m + jnp.log(den)

    def col(part, grp):
        return pl.BlockSpec((None, SEQ, LANES), lambda b, p: (b, 0, 6 * part + 2 * grp + p))

    out_spec = pl.BlockSpec((None, SEQ, LANES), lambda b, p: (b, 0, p))
    out = jax.ShapeDtypeStruct((B_LOC, SEQ, DIL_OUT), F32)
    return _call(
        body, send, name="dil_fwd", grid=(B_LOC, DIL_OUT // LANES),
        in_specs=[col(part, grp) for grp in range(DIL_GROUPS) for part in range(3)],
        out_specs=[out_spec, out_spec], out_shape=[out, out],
        scratch_shapes=[pltpu.VMEM((SEQ, LANES), BF16), pltpu.VMEM((SEQ + BLK, LANES), BF16),
                        pltpu.VMEM((SEQ + BLK, LANES), BF16), pltpu.VMEM((SEQ, LANES), F32),
                        pltpu.VMEM((SEQ, LANES), F32)] + [pltpu.VMEM((SEQ, LANES), F32)] * (2 * DIL_GROUPS),
        semantics=("parallel", "parallel"), operands=[qkv] * 9)


def _dil_bwd(qkv, d_o, lse, dsum, send=None):
    def body(*refs):
        ins, (do_ref, lse_ref, dsum_ref), outs = refs[:9], refs[9:12], refs[12:21]
        qs, ks, vs, dos, lse_res, dsum_res, dq_res, dk_acc, dv_acc = refs[21:]
        pair = pl.program_id(1)
        lane0 = lax.broadcasted_iota(jnp.int32, (BLK, LANES), 1) < HEAD_DIM
        lane1 = jnp.logical_not(lane0)
        ks[0:BLK, :] = jnp.zeros((BLK, LANES), BF16)
        vs[0:BLK, :] = jnp.zeros((BLK, LANES), BF16)
        for grp, (_, dilation) in enumerate(DIL_PAIRS):
            q_ref, k_ref, v_ref = ins[3 * grp:3 * grp + 3]
            dq_ref, dk_ref, dv_ref = outs[3 * grp:3 * grp + 3]
            per_residue = DIL_QBLOCKS // dilation
            _gather_residues(q_ref, qs, 0, dilation, QK_SCALE)
            _gather_residues(k_ref, ks, BLK, dilation)
            _gather_residues(v_ref, vs, BLK, dilation)
            _gather_residues(do_ref, dos, 0, dilation)
            _gather_residues(lse_ref, lse_res, 0, dilation)
            _gather_residues(dsum_ref, dsum_res, 0, dilation)
            dk_acc[...] = jnp.zeros_like(dk_acc)
            dv_acc[...] = jnp.zeros_like(dv_acc)
            geometry = _dil_geometry(grp, pair)

            def step(blk, _):
                off = pl.multiple_of(blk * BLK, BLK)
                q2 = _stack_heads(qs[pl.ds(off, BLK), :], lane0)
                do2 = _stack_heads(dos[pl.ds(off, BLK), :], lane0)
                kk = ks[pl.ds(off, 2 * BLK), :]
                vv = vs[pl.ds(off, 2 * BLK), :]
                lse_blk = lse_res[pl.ds(off, BLK), :]
                dsum_blk = dsum_res[pl.ds(off, BLK), :]
                lse2 = jnp.concatenate([_head_col(lse_blk, lane0), _head_col(lse_blk, lane1)], axis=0)
                dsum2 = jnp.concatenate([_head_col(dsum_blk, lane0), _head_col(dsum_blk, lane1)], axis=0)
                s = _dil_tile_scores(q2, kk, geometry, blk % per_residue != 0)
                p = jnp.exp(s - lse2)
                ds = (p * (_dot_nt(do2, vv) - dsum2)).astype(BF16)
                dq2 = _dot(ds, kk)
                dq_res[pl.ds(off, BLK), :] = jnp.where(lane0, dq2[:BLK], dq2[BLK:]) * QK_SCALE
                dk_acc[pl.ds(off, 2 * BLK), :] += _dot_tn(ds, q2)
                dv_acc[pl.ds(off, 2 * BLK), :] += _dot_tn(p.astype(BF16), do2)
                return 0

            lax.fori_loop(0, DIL_QBLOCKS, step, 0, unroll=4)
            _scatter_residues(dq_res, 0, dq_ref, dilation)
            _scatter_residues(dk_acc, BLK, dk_ref, dilation)
            _scatter_residues(dv_acc, BLK, dv_ref, dilation)

    def col(part, grp):
        return pl.BlockSpec((None, SEQ, LANES), lambda b, p: (b, 0, 6 * part + 2 * grp + p))

    slot = pl.BlockSpec((None, SEQ, LANES), lambda b, p: (b, 0, p))
    out = jax.ShapeDtypeStruct((B_LOC, SEQ, DIL_OUT), F32)
    return _call(
        body, send, name="dil_bwd", grid=(B_LOC, DIL_OUT // LANES),
        in_specs=[col(part, grp) for grp in range(DIL_GROUPS) for part in range(3)] + [slot] * 3,
        out_specs=[slot] * 9, out_shape=[out] * 9,
        scratch_shapes=[pltpu.VMEM((SEQ, LANES), BF16), pltpu.VMEM((SEQ + BLK, LANES), BF16),
                        pltpu.VMEM((SEQ + BLK, LANES), BF16), pltpu.VMEM((SEQ, LANES), BF16),
                        pltpu.VMEM((SEQ, LANES), F32), pltpu.VMEM((SEQ, LANES), F32),
                        pltpu.VMEM((SEQ, LANES), F32), pltpu.VMEM((SEQ + BLK, LANES), F32),
                        pltpu.VMEM((SEQ + BLK, LANES), F32)],
        semantics=("parallel", "parallel"), operands=[qkv] * 9 + [d_o, lse, dsum])


def _peers():
    x, y, c = lax.axis_index("x"), lax.axis_index("y"), lax.axis_index("c")
    me = 4 * x + 2 * y + c
    peers = []
    for mask in range(1, N_DEV):
        px = 1 - x if mask & 4 else x
        py = 1 - y if mask & 2 else y
        pc = 1 - c if mask & 1 else c
        peers.append(((px, py, pc), 4 * px + 2 * py + pc))
    return me, peers


def _all_gather(shard, name):
    def body(src_ref, out_ref, send_sems, recv_sems, local_sem):
        x, y, c = lax.axis_index("x"), lax.axis_index("y"), lax.axis_index("c")
        sibling = (x, y, 1 - c)
        chips = [(1 - x, y), (x, 1 - y), (1 - x, 1 - y)]

        def slot(px, py, pc):
            return out_ref.at[4 * px + 2 * py + pc]

        def copy(k, block, to, src=None):
            return pltpu.make_async_remote_copy(
                src_ref=slot(*block) if src is None else src, dst_ref=slot(*block),
                send_sem=send_sems.at[k], recv_sem=recv_sems.at[k], device_id=to,
                device_id_type=pl.DeviceIdType.MESH)

        mine = pltpu.make_async_copy(src_ref, slot(x, y, c), local_sem)
        mine.start()
        first = [copy(0, (x, y, c), sibling, src=src_ref)]
        first += [copy(1 + j, (x, y, c), (*chip, c), src=src_ref) for j, chip in enumerate(chips)]
        for cp in first:
            cp.start()
        passed = [copy(4 + j, (*chip, c), sibling) for j, chip in enumerate(chips)]
        for j, chip in enumerate(chips):
            copy(1 + j, (*chip, c), (x, y, c)).wait_recv()
            passed[j].start()
        copy(0, sibling, (x, y, c)).wait_recv()
        for j, chip in enumerate(chips):
            copy(4 + j, (*chip, 1 - c), (x, y, c)).wait_recv()
        for cp in first + passed:
            cp.wait_send()
        mine.wait()

    return pl.pallas_call(
        body, name=name,
        in_specs=[pl.BlockSpec(memory_space=pl.ANY)],
        out_specs=pl.BlockSpec(memory_space=pl.ANY),
        out_shape=jax.ShapeDtypeStruct((N_DEV,) + shard.shape, shard.dtype),
        scratch_shapes=[pltpu.SemaphoreType.DMA((N_DEV - 1,)), pltpu.SemaphoreType.DMA((N_DEV - 1,)),
                        pltpu.SemaphoreType.DMA],
    )(shard)


def _call(body, send, *, name, grid, in_specs, out_specs, out_shape, scratch_shapes, semantics, operands):
    if send is None:
        return pl.pallas_call(
            body, name=name, grid=grid, in_specs=in_specs, out_specs=out_specs, out_shape=out_shape,
            scratch_shapes=scratch_shapes, compiler_params=_params(semantics))(*operands), []
    srcs, scatter = send
    n, n_in, n_out, n_scr = len(srcs), len(in_specs), len(out_specs), len(scratch_shapes)

    def copies(refs, landing):
        src_refs, land_refs = refs[n_in:n_in + n], refs[n_in + n + n_out:n_in + 2 * n + n_out]
        send_sems, recv_sems, local_sems = refs[-3:]
        me, peers = _peers()
        out = []
        for a in range(n):
            for k, (peer, peer_idx) in enumerate(peers):
                out.append(pltpu.make_async_remote_copy(
                    src_ref=src_refs[a].at[peer_idx] if scatter else src_refs[a],
                    dst_ref=land_refs[a].at[peer_idx if landing else me],
                    send_sem=send_sems.at[a * (N_DEV - 1) + k], recv_sem=recv_sems.at[a * (N_DEV - 1) + k],
                    device_id=peer, device_id_type=pl.DeviceIdType.MESH))
        own = [pltpu.make_async_copy(src_refs[a].at[me] if scatter else src_refs[a], land_refs[a].at[me],
                                     local_sems.at[a]) for a in range(n)]
        return out, own

    def wrapped(*refs):
        step = 0
        for axis, size in enumerate(grid):
            step = step * size + pl.program_id(axis)

        @pl.when(step == 0)
        def _():
            remote, own = copies(refs, False)
            for cp in remote + own:
                cp.start()

        body(*refs[:n_in], *refs[n_in + n:n_in + n + n_out], *refs[n_in + 2 * n + n_out:n_in + 2 * n + n_out + n_scr])

        @pl.when(step == math.prod(grid) - 1)
        def _():
            remote, own = copies(refs, True)
            for cp in remote:
                cp.wait_send()
                cp.wait_recv()
            for cp in own:
                cp.wait()

    anywhere = pl.BlockSpec(memory_space=pl.ANY)
    lands = [jax.ShapeDtypeStruct((N_DEV,) + s.shape[-2:], s.dtype) for s in srcs]
    out = pl.pallas_call(
        wrapped, name=name, grid=grid,
        in_specs=list(in_specs) + [anywhere] * n, out_specs=list(out_specs) + [anywhere] * n,
        out_shape=list(out_shape) + lands,
        scratch_shapes=list(scratch_shapes) + [pltpu.SemaphoreType.DMA((n * (N_DEV - 1),)),
                                               pltpu.SemaphoreType.DMA((n * (N_DEV - 1),)),
                                               pltpu.SemaphoreType.DMA((n,))],
        compiler_params=_params(("arbitrary",) * len(grid)),
    )(*operands, *srcs)
    return out[:n_out], list(out[n_out:])


def _sum_in_device_order(land_ref):
    acc = land_ref[0].astype(F32)
    for j in range(1, N_DEV):
        acc = acc + land_ref[j].astype(F32)
    return acc


def _adam_math(w, g, m, v):
    c1 = 1.0 - ADAM_B1 ** ADAM_STEP
    c2 = 1.0 - ADAM_B2 ** ADAM_STEP
    m_new = ADAM_B1 * m + (1.0 - ADAM_B1) * g
    v_new = ADAM_B2 * v + (1.0 - ADAM_B2) * (g * g)
    delta = -ADAM_LR * ((m_new / c1) / (jnp.sqrt(v_new / c2) + ADAM_EPS) + ADAM_WD * w)
    return delta, m_new, v_new


def _row_tile(rows):
    return max(t for t in range(8, 257, 8) if rows % t == 0) if rows % 8 == 0 else rows


def _sum_update(land, w, m, v, name):
    _, rows, cols = land.shape
    tile_rows = _row_tile(rows)

    def body(land_ref, w_ref, m_ref, v_ref, g_ref, d_ref, nm_ref, nv_ref):
        g = _sum_in_device_order(land_ref)
        g_ref[...] = g
        d_ref[...], nm_ref[...], nv_ref[...] = _adam_math(w_ref[...], g, m_ref[...], v_ref[...])

    tile = pl.BlockSpec((tile_rows, cols), lambda i: (i, 0))
    out = jax.ShapeDtypeStruct((rows, cols), F32)
    return pl.pallas_call(
        body, name=name, grid=(rows // tile_rows,),
        in_specs=[pl.BlockSpec((N_DEV, tile_rows, cols), lambda i: (0, i, 0)), tile, tile, tile],
        out_specs=[tile] * 4, out_shape=[out] * 4,
        compiler_params=_params(("parallel",)),
    )(land, w, m, v)


def _sum_gains(land):
    def body(land_ref, o_ref):
        o_ref[...] = _sum_in_device_order(land_ref)

    return pl.pallas_call(
        body, name="sum_gain_grads", grid=(1,),
        in_specs=[pl.BlockSpec(land.shape, lambda i: (0, 0, 0))],
        out_specs=pl.BlockSpec(land.shape[1:], lambda i: (0, 0)),
        out_shape=jax.ShapeDtypeStruct(land.shape[1:], F32),
    )(land)


def _adamw(w, g, m, v, name):
    def body(w_ref, g_ref, m_ref, v_ref, d_ref, nm_ref, nv_ref):
        d_ref[...], nm_ref[...], nv_ref[...] = _adam_math(w_ref[...], g_ref[...], m_ref[...], v_ref[...])

    whole = pl.BlockSpec(w.shape, lambda i: (0, 0))
    out = jax.ShapeDtypeStruct(w.shape, F32)
    return pl.pallas_call(
        body, name=name, grid=(1,),
        in_specs=[whole] * 4, out_specs=[whole] * 3, out_shape=[out] * 3,
    )(w, g, m, v)


GROUP_FFN = ("w_ffn_in", "w_ffn_out")
GROUP_MIX = ("w_sb_up", "w_dil_up", "w_out")
COL_SHARDED = ("w_in", "w_sb_up", "w_dil_up", "w_ffn_in")


def _full_from_shards(name, slots):
    _, r, c = slots.shape
    if name in COL_SHARDED:
        return slots.transpose(1, 0, 2).reshape(r, N_DEV * c)
    return slots.reshape(N_DEV * r, c)


def _shards_from_full(name, full):
    rows, cols = full.shape
    if name in COL_SHARDED:
        return full.reshape(rows, N_DEV, cols // N_DEV).transpose(1, 0, 2)
    return full.reshape(N_DEV, rows // N_DEV, cols)


def _local_step(x, target, g_mix, g_ffn, g_fin, w_in, shards=None, rest=None):
    gather = lambda names: None if shards is None else ([shards[n] for n in names], False)
    scatter = lambda blocks: None if shards is None else (blocks, True)
    landed = lambda blocks, lands: lands if lands else blocks

    w = {"w_in": w_in}
    if shards is None:
        w.update(rest)
    qkv_sb, qkv_dl, gates, u = _norm_proj(x, g_mix, w["w_in"])
    qkv_sb = qkv_sb.reshape(B_LOC, SEQ, 3 * SB_WIDTH)
    qkv_dl = qkv_dl.reshape(B_LOC, SEQ, 3 * DIL_WIDTH)
    (o_sb,), lands = _sb_fwd(qkv_sb, gather(GROUP_FFN))
    w.update({n: _full_from_shards(n, t) for n, t in zip(GROUP_FFN, lands)})
    o_sb = o_sb.reshape(TOK, SB_WIDTH)
    (o_dl, lse), lands = _dil_fwd(qkv_dl, gather(GROUP_MIX))
    w.update({n: _full_from_shards(n, t) for n, t in zip(GROUP_MIX, lands)})
    o_dl = o_dl.reshape(TOK, DIL_OUT)

    x1, merged = _mix_out(x, o_sb, o_dl, gates, w["w_sb_up"], w["w_dil_up"], w["w_out"])
    loss, dx1, u2, act, dh, dx2, dg_fin, dg_ffn = _ffn_fwd_bwd(x1, target, g_ffn, g_fin, w["w_ffn_in"], w["w_ffn_out"])
    dgates, dy_sb, dy_dl, do_sb, do_dl, dsum = _mix_bwd(dx1, o_sb, o_dl, gates, w["w_sb_up"], w["w_dil_up"], w["w_out"])
    blocks = {
        "w_sb_up": _atb(o_sb, dy_sb, "grad_w_sb_up", SB_WIDTH, D_MODEL, col_blocks=N_DEV),
        "w_dil_up": _atb(o_dl.astype(BF16), dy_dl, "grad_w_dil_up", DIL_OUT, D_MODEL, col_blocks=N_DEV),
        "w_out": _shards_from_full("w_out", _atb(merged, dx1.astype(BF16), "grad_w_out", D_MODEL, D_MODEL)),
        "w_ffn_in": _shards_from_full("w_ffn_in", _atb(u2, dh, "grad_w_ffn_in", D_MODEL, D_FF)),
        "w_ffn_out": _shards_from_full("w_ffn_out", _atb(act, dx2, "grad_w_ffn_out", D_FF // 2, D_MODEL)),
    }
    grads = {}

    ffn_blocks = [blocks[n] for n in GROUP_FFN]
    (dq_sb, dk_sb, dv_sb), lands = _sb_bwd(qkv_sb, do_sb.reshape(B_LOC, SEQ, SB_WIDTH), scatter(ffn_blocks))
    grads.update(zip(GROUP_FFN, landed(ffn_blocks, lands)))
    as_batch = lambda t: t.reshape(B_LOC, SEQ, DIL_OUT)
    mix_blocks = [blocks[n] for n in GROUP_MIX]
    d_dl, lands = _dil_bwd(qkv_dl, as_batch(do_dl), lse, as_batch(dsum), scatter(mix_blocks))
    grads.update(zip(GROUP_MIX, landed(mix_blocks, lands)))
    flat = lambda t: t.reshape(TOK, -1)
    dproj = jnp.concatenate(
        [flat(dq_sb), flat(dk_sb), flat(dv_sb)]
        + [flat(d_dl[3 * grp + part]).astype(BF16) for part in range(3) for grp in range(DIL_GROUPS)]
        + [dgates], axis=1)

    w_in_blocks = [_shards_from_full("w_in", _atb(u, dproj, "grad_w_in", D_MODEL, IN_WIDTH // 2))]
    (grad_x, dg_mix), lands = _proj_bwd(dproj, dx1, x, g_mix, w["w_in"], scatter(w_in_blocks))
    grads["w_in"] = landed(w_in_blocks, lands)[0]
    gain_grads = jnp.concatenate([dg_mix, dg_ffn, dg_fin], axis=0)
    return loss, grad_x, gain_grads, grads


def kernel(x, norm_mix_g, w_in, w_sb_up, w_dil_up, w_out, norm_ffn_g, w_ffn_in, w_ffn_out, norm_final_g, loss_target, m_norm_mix_g, m_w_in, m_w_sb_up, m_w_dil_up, m_w_out, m_norm_ffn_g, m_w_ffn_in, m_w_ffn_out, m_norm_final_g, v_norm_mix_g, v_w_in, v_w_sb_up, v_w_dil_up, v_w_out, v_norm_ffn_g, v_w_ffn_in, v_w_ffn_out, v_norm_final_g):
    mats = {"w_in": w_in, "w_sb_up": w_sb_up, "w_dil_up": w_dil_up, "w_out": w_out,
            "w_ffn_in": w_ffn_in, "w_ffn_out": w_ffn_out}
    moments_m = {"w_in": m_w_in, "w_sb_up": m_w_sb_up, "w_dil_up": m_w_dil_up, "w_out": m_w_out,
                 "w_ffn_in": m_w_ffn_in, "w_ffn_out": m_w_ffn_out}
    moments_v = {"w_in": v_w_in, "w_sb_up": v_w_sb_up, "w_dil_up": v_w_dil_up, "w_out": v_w_out,
                 "w_ffn_in": v_w_ffn_in, "w_ffn_out": v_w_ffn_out}
    gathered_w_in = _all_gather(w_in[0].astype(BF16), "all_gather_w_in")
    g_fin = norm_final_g.reshape(1, D_MODEL)
    loss, grad_x, gain_grads, grad_slots = _local_step(
        x.reshape(TOK, D_MODEL), loss_target.reshape(TOK, D_MODEL), norm_mix_g, norm_ffn_g, g_fin,
        _full_from_shards("w_in", gathered_w_in),
        shards={name: mats[name][0].astype(BF16) for name in GROUP_FFN + GROUP_MIX})

    gain_rows = jnp.concatenate([gain_grads, jnp.tile(loss, (1, D_MODEL // LANES)),
                                 jnp.zeros((8 - 4, D_MODEL), F32)], axis=0)
    g_gains = _sum_gains(_all_gather(gain_rows, "all_gather_gains"))

    out_g, out_d, out_m, out_v = {}, {}, {}, {}
    for name, slots in grad_slots.items():
        g, d, nm, nv = _sum_update(slots, mats[name][0], moments_m[name][0], moments_v[name][0], "update_" + name)
        out_g[name], out_d[name], out_m[name], out_v[name] = g[None], d[None], nm[None], nv[None]

    gain_w = jnp.concatenate([norm_mix_g, norm_ffn_g, g_fin], axis=0)
    gain_m = jnp.concatenate([m_norm_mix_g, m_norm_ffn_g, m_norm_final_g.reshape(1, D_MODEL)], axis=0)
    gain_v = jnp.concatenate([v_norm_mix_g, v_norm_ffn_g, v_norm_final_g.reshape(1, D_MODEL)], axis=0)
    gd, gm, gv = _adamw(gain_w, g_gains[:3], gain_m, gain_v, "adamw_gains")
    for idx, name in enumerate(("norm_mix_g", "norm_ffn_g", "norm_final_g")):
        shape = (D_MODEL,) if name == "norm_final_g" else (1, D_MODEL)
        out_g[name] = g_gains[idx].reshape(shape)
        out_d[name], out_m[name], out_v[name] = gd[idx].reshape(shape), gm[idx].reshape(shape), gv[idx].reshape(shape)

    order = ("norm_mix_g", "w_in", "w_sb_up", "w_dil_up", "w_out", "norm_ffn_g", "w_ffn_in", "w_ffn_out",
             "norm_final_g")
    return (g_gains[3, 0], grad_x.reshape(B_LOC, SEQ, D_MODEL),
            *[out_g[n] for n in order], *[out_d[n] for n in order],
            *[out_m[n] for n in order], *[out_v[n] for n in order])
```

```python
import math

import jax
import jax.numpy as jnp
from jax import lax
from jax.experimental import pallas as pl
from jax.experimental.pallas import tpu as pltpu

F32 = jnp.float32
BF16 = jnp.bfloat16

N_DEV = 8
D_MODEL = 1024
SEQ = 2048
B_LOC = 2
TOK = B_LOC * SEQ
HEAD_DIM = 64
SB_WIDTH = 512
DIL_WIDTH = 768
DIL_OUT = 256
QKV_WIDTH = 3 * SB_WIDTH + 3 * DIL_WIDTH
IN_WIDTH = QKV_WIDTH + 2 * D_MODEL
D_FF = 2816
DIL_PAIRS = ((128, 1), (512, 4), (2048, 16))
DIL_HEADS = 12
RMS_EPS = 1e-6
ALIBI_MAX_BIAS = 8.0
QK_SCALE = 1.0 / math.sqrt(HEAD_DIM)
BLK = 128
LANES = 128
NEG_BIG = -1e30

ADAM_LR = 0.001
ADAM_B1 = 0.9
ADAM_B2 = 0.999
ADAM_EPS = 1e-08
ADAM_WD = 0.01
ADAM_STEP = 10

VMEM_LIMIT = 56 * 1024 * 1024


def _dot(a, b):
    return jnp.dot(a, b, preferred_element_type=F32)


def _dot_nt(a, b):
    return lax.dot_general(a, b, (((1,), (1,)), ((), ())), preferred_element_type=F32)


def _dot_tn(a, b):
    return lax.dot_general(a, b, (((0,), (0,)), ((), ())), preferred_element_type=F32)


def _softplus(z):
    return jnp.maximum(z, 0.0) + jnp.log1p(jnp.exp(-jnp.abs(z)))


def _sigmoid(z):
    return 1.0 / (1.0 + jnp.exp(-z))


def _split_bf16(v):
    hi = v.astype(BF16)
    lo = (v - hi.astype(F32)).astype(BF16)
    return hi, lo


def _chunks(width, step=512):
    out, c = [], 0
    while c < width:
        w = min(step, width - c)
        out.append((c, w))
        c += w
    return out


def _resident(shape):
    nd = len(shape)
    return pl.BlockSpec(shape, lambda *_: (0,) * nd, pipeline_mode=pl.Buffered(1))


def _params(sem):
    return pltpu.CompilerParams(dimension_semantics=sem, vmem_limit_bytes=VMEM_LIMIT)


def _rms_fwd(x, g):
    r = lax.rsqrt(jnp.mean(x * x, axis=-1, keepdims=True) + RMS_EPS)
    n = x * r
    return n, r, n * g


def _rms_bwd(dy, n, r, g):
    dg = jnp.sum(dy * n, axis=0, keepdims=True)
    dn = dy * g
    dx = r * (dn - n * jnp.mean(dn * n, axis=-1, keepdims=True))
    return dx, dg


TM = 256


def _norm_proj(x, g, w_in):
    def body(x_ref, g_ref, w_ref, sb_ref, dl_ref, gate_ref, u_ref):
        _, _, u = _rms_fwd(x_ref[...], g_ref[...])
        u = u.astype(BF16)
        u_ref[...] = u
        for c0, w in _chunks(3 * SB_WIDTH):
            sb_ref[:, c0:c0 + w] = _dot(u, w_ref[:, c0:c0 + w]).astype(BF16)
        for c0, w in _chunks(3 * DIL_WIDTH):
            dl_ref[:, c0:c0 + w] = _dot(u, w_ref[:, 3 * SB_WIDTH + c0:3 * SB_WIDTH + c0 + w])
        for c0, w in _chunks(2 * D_MODEL):
            gate_ref[:, c0:c0 + w] = _dot(u, w_ref[:, QKV_WIDTH + c0:QKV_WIDTH + c0 + w])

    return pl.pallas_call(
        body, name="norm_proj", grid=(TOK // TM,),
        in_specs=[pl.BlockSpec((TM, D_MODEL), lambda i: (i, 0)), _resident((1, D_MODEL)),
                  _resident((D_MODEL, IN_WIDTH))],
        out_specs=[pl.BlockSpec((TM, 3 * SB_WIDTH), lambda i: (i, 0)),
                   pl.BlockSpec((TM, 3 * DIL_WIDTH), lambda i: (i, 0)),
                   pl.BlockSpec((TM, 2 * D_MODEL), lambda i: (i, 0)),
                   pl.BlockSpec((TM, D_MODEL), lambda i: (i, 0))],
        out_shape=[jax.ShapeDtypeStruct((TOK, 3 * SB_WIDTH), BF16),
                   jax.ShapeDtypeStruct((TOK, 3 * DIL_WIDTH), F32),
                   jax.ShapeDtypeStruct((TOK, 2 * D_MODEL), F32),
                   jax.ShapeDtypeStruct((TOK, D_MODEL), BF16)],
        compiler_params=_params(("parallel",)),
    )(x, g, w_in)


def _mix_out(x, o_sb, o_dl, gates, w_sb_up, w_dil_up, w_out):
    def body(x_ref, osb_ref, odl_ref, gate_ref, wsb_ref, wdl_ref, wout_ref, x1_ref, mg_ref):
        y_sb = _dot(osb_ref[...], wsb_ref[...])
        y_dl = _dot(odl_ref[...].astype(BF16), wdl_ref[...])
        merged = (_sigmoid(gate_ref[:, :D_MODEL]) * y_sb
                  + _sigmoid(gate_ref[:, D_MODEL:]) * y_dl).astype(BF16)
        mg_ref[...] = merged
        x1_ref[...] = x_ref[...] + _dot(merged, wout_ref[...])

    return pl.pallas_call(
        body, name="mix_out", grid=(TOK // TM,),
        in_specs=[pl.BlockSpec((TM, D_MODEL), lambda i: (i, 0)),
                  pl.BlockSpec((TM, SB_WIDTH), lambda i: (i, 0)),
                  pl.BlockSpec((TM, DIL_OUT), lambda i: (i, 0)),
                  pl.BlockSpec((TM, 2 * D_MODEL), lambda i: (i, 0)),
                  _resident((SB_WIDTH, D_MODEL)), _resident((DIL_OUT, D_MODEL)),
                  _resident((D_MODEL, D_MODEL))],
        out_specs=[pl.BlockSpec((TM, D_MODEL), lambda i: (i, 0)),
                   pl.BlockSpec((TM, D_MODEL), lambda i: (i, 0))],
        out_shape=[jax.ShapeDtypeStruct((TOK, D_MODEL), F32),
                   jax.ShapeDtypeStruct((TOK, D_MODEL), BF16)],
        compiler_params=_params(("parallel",)),
    )(x, o_sb, o_dl, gates, w_sb_up, w_dil_up, w_out)


FF_CHUNK = D_FF // 2


def _ffn_fwd_bwd(x1, target, g_ffn, g_fin, w_ffn_in, w_ffn_out):
    def body(x1_ref, t_ref, gffn_ref, gfin_ref, win_ref, wout_ref,
             loss_ref, dx1_ref, u2_ref, act_ref, dh_ref, dx2_ref, dgfin_ref, dgffn_ref, h_scr):
        i = pl.program_id(0)

        @pl.when(i == 0)
        def _():
            loss_ref[...] = jnp.zeros_like(loss_ref)
            dgfin_ref[...] = jnp.zeros_like(dgfin_ref)
            dgffn_ref[...] = jnp.zeros_like(dgffn_ref)

        x1 = x1_ref[...]
        g_ffn_v = gffn_ref[...]
        g_fin_v = gfin_ref[...]
        n2, r2, u2 = _rms_fwd(x1, g_ffn_v)
        u2 = u2.astype(BF16)
        u2_ref[...] = u2
        x2 = x1
        for c0 in range(0, D_FF, FF_CHUNK):
            gate = _dot(u2, win_ref[:, c0:c0 + FF_CHUNK])
            up = _dot(u2, win_ref[:, D_FF + c0:D_FF + c0 + FF_CHUNK])
            h_scr[:, c0:c0 + FF_CHUNK] = gate
            h_scr[:, D_FF + c0:D_FF + c0 + FF_CHUNK] = up
            act = (gate * _sigmoid(gate) * up).astype(BF16)
            act_ref[:, c0:c0 + FF_CHUNK] = act
            x2 = x2 + _dot(act, wout_ref[c0:c0 + FF_CHUNK, :])
        n3, r3, y = _rms_fwd(x2, g_fin_v)
        err = y - t_ref[...]
        sq = jnp.sum(jnp.sum(err * err, axis=1, keepdims=True), axis=0, keepdims=True)
        loss_ref[...] += sq * (0.5 / D_MODEL)
        dx2, dgfin = _rms_bwd(err * (1.0 / D_MODEL), n3, r3, g_fin_v)
        dgfin_ref[...] += dgfin
        dx2_b = dx2.astype(BF16)
        dx2_ref[...] = dx2_b
        du2 = jnp.zeros((TM, D_MODEL), F32)
        for c0 in range(0, D_FF, FF_CHUNK):
            gate = h_scr[:, c0:c0 + FF_CHUNK]
            up = h_scr[:, D_FF + c0:D_FF + c0 + FF_CHUNK]
            dact = _dot_nt(dx2_b, wout_ref[c0:c0 + FF_CHUNK, :])
            sg = _sigmoid(gate)
            dgate = (dact * up * (sg * (1.0 + gate * (1.0 - sg)))).astype(BF16)
            dup = (dact * (gate * sg)).astype(BF16)
            dh_ref[:, c0:c0 + FF_CHUNK] = dgate
            dh_ref[:, D_FF + c0:D_FF + c0 + FF_CHUNK] = dup
            du2 = du2 + _dot_nt(dgate, win_ref[:, c0:c0 + FF_CHUNK])
            du2 = du2 + _dot_nt(dup, win_ref[:, D_FF + c0:D_FF + c0 + FF_CHUNK])
        dx1_n, dgffn = _rms_bwd(du2, n2, r2, g_ffn_v)
        dgffn_ref[...] += dgffn
        dx1_ref[...] = dx2 + dx1_n

    tile = lambda w: pl.BlockSpec((TM, w), lambda i: (i, 0))
    acc = lambda w: pl.BlockSpec((1, w), lambda i: (0, 0))
    return pl.pallas_call(
        body, name="ffn_fwd_bwd", grid=(TOK // TM,),
        in_specs=[tile(D_MODEL), tile(D_MODEL), _resident((1, D_MODEL)), _resident((1, D_MODEL)),
                  _resident((D_MODEL, 2 * D_FF)), _resident((D_FF, D_MODEL))],
        out_specs=[acc(LANES), tile(D_MODEL), tile(D_MODEL), tile(D_FF), tile(2 * D_FF), tile(D_MODEL),
                   acc(D_MODEL), acc(D_MODEL)],
        out_shape=[jax.ShapeDtypeStruct((1, LANES), F32),
                   jax.ShapeDtypeStruct((TOK, D_MODEL), F32),
                   jax.ShapeDtypeStruct((TOK, D_MODEL), BF16),
                   jax.ShapeDtypeStruct((TOK, D_FF), BF16),
                   jax.ShapeDtypeStruct((TOK, 2 * D_FF), BF16),
                   jax.ShapeDtypeStruct((TOK, D_MODEL), BF16),
                   jax.ShapeDtypeStruct((1, D_MODEL), F32),
                   jax.ShapeDtypeStruct((1, D_MODEL), F32)],
        scratch_shapes=[pltpu.VMEM((TM, 2 * D_FF), F32)],
        compiler_params=_params(("arbitrary",)),
    )(x1, target, g_ffn, g_fin, w_ffn_in, w_ffn_out)


def _mix_bwd(dx1, o_sb, o_dl, gates, w_sb_up, w_dil_up, w_out):
    def body(dx1_ref, osb_ref, odl_ref, gate_ref, wsb_ref, wdl_ref, wout_ref,
             dgate_ref, dysb_ref, dydl_ref, dosb_ref, dodl_ref, dsum_ref):
        dmerged = _dot_nt(dx1_ref[...].astype(BF16), wout_ref[...])
        o_dl = odl_ref[...]
        y_sb = _dot(osb_ref[...], wsb_ref[...])
        y_dl = _dot(o_dl.astype(BF16), wdl_ref[...])
        s_sb = _sigmoid(gate_ref[:, :D_MODEL])
        s_dl = _sigmoid(gate_ref[:, D_MODEL:])
        dgate_ref[:, :D_MODEL] = (dmerged * y_sb * (s_sb * (1.0 - s_sb))).astype(BF16)
        dgate_ref[:, D_MODEL:] = (dmerged * y_dl * (s_dl * (1.0 - s_dl))).astype(BF16)
        dy_sb = (dmerged * s_sb).astype(BF16)
        dy_dl = (dmerged * s_dl).astype(BF16)
        dysb_ref[...] = dy_sb
        dydl_ref[...] = dy_dl
        dosb_ref[...] = _dot_nt(dy_sb, wsb_ref[...]).astype(BF16)
        do_dl = _dot_nt(dy_dl, wdl_ref[...])
        dodl_ref[...] = do_dl
        row = lax.broadcasted_iota(jnp.int32, (DIL_OUT, DIL_OUT), 0) // HEAD_DIM
        col = lax.broadcasted_iota(jnp.int32, (DIL_OUT, DIL_OUT), 1) // HEAD_DIM
        same_head = (row == col).astype(BF16)
        hi, lo = _split_bf16(do_dl * o_dl)
        dsum_ref[...] = _dot(hi, same_head) + _dot(lo, same_head)

    tile = lambda w: pl.BlockSpec((TM, w), lambda i: (i, 0))
    return pl.pallas_call(
        body, name="mix_bwd", grid=(TOK // TM,),
        in_specs=[tile(D_MODEL), tile(SB_WIDTH), tile(DIL_OUT), tile(2 * D_MODEL),
                  _resident((SB_WIDTH, D_MODEL)), _resident((DIL_OUT, D_MODEL)),
                  _resident((D_MODEL, D_MODEL))],
        out_specs=[tile(2 * D_MODEL), tile(D_MODEL), tile(D_MODEL), tile(SB_WIDTH), tile(DIL_OUT),
                   tile(DIL_OUT)],
        out_shape=[jax.ShapeDtypeStruct((TOK, 2 * D_MODEL), BF16),
                   jax.ShapeDtypeStruct((TOK, D_MODEL), BF16),
                   jax.ShapeDtypeStruct((TOK, D_MODEL), BF16),
                   jax.ShapeDtypeStruct((TOK, SB_WIDTH), BF16),
                   jax.ShapeDtypeStruct((TOK, DIL_OUT), F32),
                   jax.ShapeDtypeStruct((TOK, DIL_OUT), F32)],
        compiler_params=_params(("parallel",)),
    )(dx1, o_sb, o_dl, gates, w_sb_up, w_dil_up, w_out)


def _proj_bwd(dproj, dx1, x, g, w_in, send=None):
    def body(dp_ref, dx1_ref, x_ref, g_ref, w_ref, dx_ref, dg_ref):
        @pl.when(pl.program_id(0) == 0)
        def _():
            dg_ref[...] = jnp.zeros_like(dg_ref)

        du = jnp.zeros((TM, D_MODEL), F32)
        for c0, w in _chunks(IN_WIDTH, 1024):
            du = du + _dot_nt(dp_ref[:, c0:c0 + w], w_ref[:, c0:c0 + w])
        g_v = g_ref[...]
        n, r, _ = _rms_fwd(x_ref[...], g_v)
        dx, dg = _rms_bwd(du, n, r, g_v)
        dg_ref[...] += dg
        dx_ref[...] = dx1_ref[...] + dx

    tile = lambda w: pl.BlockSpec((TM, w), lambda i: (i, 0))
    return _call(
        body, send, name="proj_bwd", grid=(TOK // TM,),
        in_specs=[tile(IN_WIDTH), tile(D_MODEL), tile(D_MODEL), _resident((1, D_MODEL)),
                  _resident((D_MODEL, IN_WIDTH))],
        out_specs=[tile(D_MODEL), pl.BlockSpec((1, D_MODEL), lambda i: (0, 0))],
        out_shape=[jax.ShapeDtypeStruct((TOK, D_MODEL), F32),
                   jax.ShapeDtypeStruct((1, D_MODEL), F32)],
        scratch_shapes=[], semantics=("arbitrary",), operands=(dproj, dx1, x, g, w_in))


def _atb(a, b, name, tm, tn, col_blocks=0, tk=512):
    m, n = a.shape[1], b.shape[1]
    nk = TOK // tk

    def body(a_ref, b_ref, o_ref, acc_ref):
        k = pl.program_id(2)

        @pl.when(k == 0)
        def _():
            acc_ref[...] = jnp.zeros_like(acc_ref)

        acc_ref[...] += _dot_tn(a_ref[...], b_ref[...])

        @pl.when(k == nk - 1)
        def _():
            if col_blocks:
                width = n // col_blocks
                for blk in range(col_blocks):
                    o_ref[blk] = acc_ref[:, blk * width:(blk + 1) * width].astype(BF16)
            else:
                o_ref[...] = acc_ref[...].astype(BF16)

    if col_blocks:
        out_spec = pl.BlockSpec((col_blocks, tm, n // col_blocks), lambda i, j, k: (0, i, 0))
        out_shape = jax.ShapeDtypeStruct((col_blocks, m, n // col_blocks), BF16)
    else:
        out_spec = pl.BlockSpec((tm, tn), lambda i, j, k: (i, j))
        out_shape = jax.ShapeDtypeStruct((m, n), BF16)
    return pl.pallas_call(
        body, name=name, grid=(m // tm, n // tn, nk),
        in_specs=[pl.BlockSpec((tk, tm), lambda i, j, k: (k, i)),
                  pl.BlockSpec((tk, tn), lambda i, j, k: (k, j))],
        out_specs=out_spec, out_shape=out_shape,
        scratch_shapes=[pltpu.VMEM((tm, tn), F32)],
        compiler_params=_params(("parallel", "parallel", "arbitrary")),
    )(a, b)


SB_PAIRS = SB_WIDTH // LANES


def _two_heads(v, lane0):
    zero = jnp.zeros_like(v)
    return jnp.where(lane0, v, zero), jnp.where(lane0, zero, v)


SB_QBLK = 256
N_SB_STEPS = SEQ // SB_QBLK


SB_KCHUNK = 2 * BLK
SB_ROWS = 2 * SB_QBLK
SB_DEAD = -104.0


def _log_keep(z):
    neg_z = -z
    return jnp.minimum(neg_z, 0.0) - jnp.log(1.0 + jnp.exp(jnp.minimum(z, neg_z)))


def _stack_heads(v, lane0):
    return jnp.concatenate(_two_heads(v, lane0), axis=0)


def _block_sums(v, tri):
    halves = (v[:, :BLK], v[:, BLK:])
    hi, lo = _split_bf16(jnp.concatenate(halves, axis=0))
    prod = _dot(jnp.concatenate([hi, lo], axis=0), tri)
    tri_sum = prod[:2 * SB_ROWS] + prod[2 * SB_ROWS:]
    sums = tuple(jnp.sum(h, axis=1, keepdims=True) for h in halves)
    return (tri_sum[:SB_ROWS], tri_sum[SB_ROWS:]), sums


def _sb_diag_mask():
    row = lax.broadcasted_iota(jnp.int32, (SB_ROWS, SB_KCHUNK), 0)
    col = lax.broadcasted_iota(jnp.int32, (SB_ROWS, SB_KCHUNK), 1)
    return col < jnp.where(row >= SB_QBLK, row - SB_QBLK, row)


def _sb_fwd(qkv, send=None):
    def body(q_ref, k_ref, v_ref, o_ref):
        i = pl.program_id(2)
        krow = lax.broadcasted_iota(jnp.int32, (BLK, BLK), 0)
        kcol = lax.broadcasted_iota(jnp.int32, (BLK, BLK), 1)
        later = (krow > kcol).astype(BF16)
        lane0 = lax.broadcasted_iota(jnp.int32, (SB_QBLK, LANES), 1) < HEAD_DIM
        q2 = _stack_heads(q_ref[0] * QK_SCALE, lane0)

        def chunk(c, carry, causal):
            acc, run = carry
            off = pl.multiple_of(c * SB_KCHUNK, SB_KCHUNK)
            z = _dot_nt(q2, k_ref[0, pl.ds(off, SB_KCHUNK), :])
            log_keep = _log_keep(z)
            if causal is not None:
                log_keep = jnp.where(causal, log_keep, 0.0)
            suffix, sums = _block_sums(log_keep, later)
            log_after = jnp.concatenate([suffix[0] + (run + sums[1]), suffix[1] + run], axis=1)
            a = jnp.exp(log_keep + z + log_after)
            if causal is not None:
                a = jnp.where(causal, a, 0.0)
            acc = acc + _dot(a.astype(BF16), v_ref[0, pl.ds(off, SB_KCHUNK), :])
            return acc, run + (sums[0] + sums[1])

        acc, run = chunk(i, (jnp.zeros((SB_ROWS, LANES), F32), jnp.zeros((SB_ROWS, 1), F32)), _sb_diag_mask())

        def live(state):
            t, _, run = state
            return jnp.logical_and(t < i, jnp.max(run) > SB_DEAD)

        def trip(state):
            t, acc, run = state
            acc, run = chunk(i - 1 - t, (acc, run), None)
            return t + 1, acc, run

        _, acc, _ = lax.while_loop(live, trip, (jnp.int32(0), acc, run))
        o_ref[0] = jnp.where(lane0, acc[:SB_QBLK], acc[SB_QBLK:]).astype(BF16)

    blk = pl.BlockSpec((1, SB_QBLK, LANES), lambda b, h, i: (b, i, h))
    return _call(
        body, send, name="sb_fwd", grid=(B_LOC, SB_PAIRS, N_SB_STEPS),
        in_specs=[blk,
                  pl.BlockSpec((1, SEQ, LANES), lambda b, h, i: (b, 0, SB_PAIRS + h)),
                  pl.BlockSpec((1, SEQ, LANES), lambda b, h, i: (b, 0, 2 * SB_PAIRS + h))],
        out_specs=[blk], out_shape=[jax.ShapeDtypeStruct((B_LOC, SEQ, SB_WIDTH), BF16)],
        scratch_shapes=[], semantics=("parallel", "parallel", "arbitrary"), operands=(qkv, qkv, qkv))


def _sb_bwd(qkv, d_o, send=None):
    def body(q_ref, k_ref, v_ref, do_ref, dq_ref, dk_ref, dv_ref, dk_acc, dv_acc):
        i = pl.program_id(2)
        krow = lax.broadcasted_iota(jnp.int32, (BLK, BLK), 0)
        kcol = lax.broadcasted_iota(jnp.int32, (BLK, BLK), 1)
        upto = (krow <= kcol).astype(BF16)
        earlier = (krow < kcol).astype(BF16)
        lane0 = lax.broadcasted_iota(jnp.int32, (SB_QBLK, LANES), 1) < HEAD_DIM
        q2 = _stack_heads(q_ref[0] * QK_SCALE, lane0)
        do2 = _stack_heads(do_ref[0], lane0)

        def keep_sum(c, causal):
            off = pl.multiple_of(c * SB_KCHUNK, SB_KCHUNK)
            log_keep = _log_keep(_dot_nt(q2, k_ref[0, pl.ds(off, SB_KCHUNK), :]))
            if causal is not None:
                log_keep = jnp.where(causal, log_keep, 0.0)
            return jnp.sum(log_keep, axis=1, keepdims=True)

        def live(state):
            t, run = state
            return jnp.logical_and(t < i, jnp.max(run) > SB_DEAD)

        walked, tot2 = lax.while_loop(live, lambda s: (s[0] + 1, s[1] + keep_sum(i - 1 - s[0], None)),
                                      (jnp.int32(0), keep_sum(i, _sb_diag_mask())))
        first = i - walked

        @pl.when(i == 0)
        def _():
            dk_acc[...] = jnp.zeros_like(dk_acc)
            dv_acc[...] = jnp.zeros_like(dv_acc)

        def chunk(c, carry, causal):
            dq, pre_keep, pre_e = carry
            off = pl.multiple_of(c * SB_KCHUNK, SB_KCHUNK)
            k_c = k_ref[0, pl.ds(off, SB_KCHUNK), :]
            v_c = v_ref[0, pl.ds(off, SB_KCHUNK), :]
            z = _dot_nt(q2, k_c)
            d_a = _dot_nt(do2, v_c)
            log_keep = _log_keep(z)
            if causal is not None:
                log_keep = jnp.where(causal, log_keep, 0.0)
            log_beta = log_keep + z
            prefix, sums = _block_sums(log_keep, upto)
            inclusive = jnp.concatenate([prefix[0], prefix[1] + sums[0]], axis=1)
            a = jnp.exp(log_beta + ((tot2 - pre_keep) - inclusive))
            if causal is not None:
                a = jnp.where(causal, a, 0.0)
            e = d_a * a
            e_prefix, e_sums = _block_sums(e, earlier)
            before = jnp.concatenate([e_prefix[0] + pre_e, e_prefix[1] + (pre_e + e_sums[0])], axis=1)
            dz = e - (e + before) * jnp.exp(log_beta)
            if causal is not None:
                dz = jnp.where(causal, dz, 0.0)
            dz = dz.astype(BF16)
            dq = dq + _dot(dz, k_c)
            dk_acc[pl.ds(off, SB_KCHUNK), :] += _dot_tn(dz, q2)
            dv_acc[pl.ds(off, SB_KCHUNK), :] += _dot_tn(a.astype(BF16), do2)
            return dq, pre_keep + (sums[0] + sums[1]), pre_e + (e_sums[0] + e_sums[1])

        zero_col = jnp.zeros((SB_ROWS, 1), F32)
        carry = lax.fori_loop(first, i, lambda t, c: chunk(t, c, None),
                              (jnp.zeros((SB_ROWS, LANES), F32), zero_col, zero_col))
        dq, _, _ = chunk(i, carry, _sb_diag_mask())
        dq_ref[0] = (jnp.where(lane0, dq[:SB_QBLK], dq[SB_QBLK:]) * QK_SCALE).astype(BF16)

        @pl.when(i == N_SB_STEPS - 1)
        def _():
            dk_ref[0] = dk_acc[...].astype(BF16)
            dv_ref[0] = dv_acc[...].astype(BF16)

    blk = pl.BlockSpec((1, SB_QBLK, LANES), lambda b, h, i: (b, i, h))
    whole = lambda c: pl.BlockSpec((1, SEQ, LANES), lambda b, h, i: (b, 0, c * SB_PAIRS + h))
    out = jax.ShapeDtypeStruct((B_LOC, SEQ, SB_WIDTH), BF16)
    return _call(
        body, send, name="sb_bwd", grid=(B_LOC, SB_PAIRS, N_SB_STEPS),
        in_specs=[blk, whole(1), whole(2), blk],
        out_specs=[blk, whole(0), whole(0)],
        out_shape=[out, out, out],
        scratch_shapes=[pltpu.VMEM((SEQ, LANES), F32), pltpu.VMEM((SEQ, LANES), F32)],
        semantics=("parallel", "parallel", "arbitrary"), operands=(qkv, qkv, qkv, d_o))


DIL_GROUPS = len(DIL_PAIRS)
DIL_QBLOCKS = SEQ // BLK


def _residue_rows(j, dilation):
    length = SEQ // dilation
    return pl.ds(j, length, stride=dilation) if dilation > 1 else pl.ds(0, length)


def _gather_residues(src_ref, dst_ref, dst_off, dilation, scale=None):
    length = SEQ // dilation
    for j in range(dilation):
        v = src_ref[_residue_rows(j, dilation), :]
        if scale is not None:
            v = v * scale
        dst_ref[dst_off + j * length:dst_off + (j + 1) * length, :] = v.astype(dst_ref.dtype)


def _scatter_residues(src_ref, src_off, dst_ref, dilation):
    length = SEQ // dilation
    for j in range(dilation):
        dst_ref[_residue_rows(j, dilation), :] = (
            src_ref[src_off + j * length:src_off + (j + 1) * length, :].astype(dst_ref.dtype))


def _dil_geometry(group, pair):
    dilation = DIL_PAIRS[group][1]
    row = lax.broadcasted_iota(jnp.int32, (2 * BLK, 2 * BLK), 0)
    col = lax.broadcasted_iota(jnp.int32, (2 * BLK, 2 * BLK), 1)
    second = row >= BLK
    steps = BLK + jnp.where(second, row - BLK, row) - col
    coef = -ALIBI_MAX_BIAS / DIL_HEADS * math.log(2.0)
    first_head = float(4 * group + 1) + 2.0 * pair.astype(F32)
    slope = jnp.exp(coef * (first_head + jnp.where(second, 1.0, 0.0)))
    bias = slope * (steps * dilation).astype(F32)
    valid = jnp.logical_and(steps >= 0, steps <= BLK)
    return bias, valid, col >= BLK


def _dil_tile_scores(q2, kk, geometry, has_prev):
    bias, valid, own = geometry
    ok = jnp.logical_and(valid, jnp.logical_or(own, has_prev))
    return jnp.where(ok, _dot_nt(q2, kk) - bias, NEG_BIG)


def _head_col(v, lane_mask):
    return jnp.max(jnp.where(lane_mask, v, NEG_BIG), axis=1, keepdims=True)


def _dil_fwd(qkv, send=None):
    def body(*refs):
        ins, (o_ref, lse_ref), (qs, ks, vs, o_res, lse_res) = refs[:9], refs[9:11], refs[11:16]
        o_grp, lse_grp = refs[16:19], refs[19:22]
        pair = pl.program_id(1)
        lane0 = lax.broadcasted_iota(jnp.int32, (BLK, LANES), 1) < HEAD_DIM
        ks[0:BLK, :] = jnp.zeros((BLK, LANES), BF16)
        vs[0:BLK, :] = jnp.zeros((BLK, LANES), BF16)
        for grp, (_, dilation) in enumerate(DIL_PAIRS):
            q_ref, k_ref, v_ref = ins[3 * grp:3 * grp + 3]
            per_residue = DIL_QBLOCKS // dilation
            _gather_residues(q_ref, qs, 0, dilation, QK_SCALE)
            _gather_residues(k_ref, ks, BLK, dilation)
            _gather_residues(v_ref, vs, BLK, dilation)
            geometry = _dil_geometry(grp, pair)

            def step(blk, _):
                off = pl.multiple_of(blk * BLK, BLK)
                q2 = _stack_heads(qs[pl.ds(off, BLK), :], lane0)
                s = _dil_tile_scores(q2, ks[pl.ds(off, 2 * BLK), :], geometry, blk % per_residue != 0)
                m = jnp.max(s, axis=1, keepdims=True)
                p = jnp.exp(s - m)
                den = jnp.sum(p, axis=1, keepdims=True)
                out = _dot(p.astype(BF16), vs[pl.ds(off, 2 * BLK), :]) / den
                lse = m + jnp.log(den)
                o_res[pl.ds(off, BLK), :] = jnp.where(lane0, out[:BLK], out[BLK:])
                lse_res[pl.ds(off, BLK), :] = jnp.where(lane0, lse[:BLK], lse[BLK:])
                return 0

            lax.fori_loop(0, DIL_QBLOCKS, step, 0, unroll=4)
            _scatter_residues(o_res, 0, o_grp[grp], dilation)
            _scatter_residues(lse_res, 0, lse_grp[grp], dilation)

        for r0 in range(0, SEQ, 2 * BLK):
            rows = slice(r0, r0 + 2 * BLK)
            ls = [lse_grp[g][rows, :] for g in range(DIL_GROUPS)]
            m = jnp.maximum(jnp.maximum(ls[0], ls[1]), ls[2])
            w = [jnp.exp(l - m) for l in ls]
            den = w[0] + w[1] + w[2]
            o_ref[rows, :] = (w[0] * o_grp[0][rows, :] + w[1] * o_grp[1][rows, :] + w[2] * o_grp[2][rows, :]) / den
            lse_ref[rows, :] = m + jnp.log(den)

    def col(part, grp):
        return pl.BlockSpec((None, SEQ, LANES), lambda b, p: (b, 0, 6 * part + 2 * grp + p))

    out_spec = pl.BlockSpec((None, SEQ, LANES), lambda b, p: (b, 0, p))
    out = jax.ShapeDtypeStruct((B_LOC, SEQ, DIL_OUT), F32)
    return _call(
        body, send, name="dil_fwd", grid=(B_LOC, DIL_OUT // LANES),
        in_specs=[col(part, grp) for grp in range(DIL_GROUPS) for part in range(3)],
        out_specs=[out_spec, out_spec], out_shape=[out, out],
        scratch_shapes=[pltpu.VMEM((SEQ, LANES), BF16), pltpu.VMEM((SEQ + BLK, LANES), BF16),
                        pltpu.VMEM((SEQ + BLK, LANES), BF16), pltpu.VMEM((SEQ, LANES), F32),
                        pltpu.VMEM((SEQ, LANES), F32)] + [pltpu.VMEM((SEQ, LANES), F32)] * (2 * DIL_GROUPS),
        semantics=("parallel", "parallel"), operands=[qkv] * 9)


def _dil_bwd(qkv, d_o, lse, dsum, send=None):
    def body(*refs):
        ins, (do_ref, lse_ref, dsum_ref), outs = refs[:9], refs[9:12], refs[12:21]
        qs, ks, vs, dos, lse_res, dsum_res, dq_res, dk_acc, dv_acc = refs[21:]
        pair = pl.program_id(1)
        lane0 = lax.broadcasted_iota(jnp.int32, (BLK, LANES), 1) < HEAD_DIM
        lane1 = jnp.logical_not(lane0)
        ks[0:BLK, :] = jnp.zeros((BLK, LANES), BF16)
        vs[0:BLK, :] = jnp.zeros((BLK, LANES), BF16)
        for grp, (_, dilation) in enumerate(DIL_PAIRS):
            q_ref, k_ref, v_ref = ins[3 * grp:3 * grp + 3]
            dq_ref, dk_ref, dv_ref = outs[3 * grp:3 * grp + 3]
            per_residue = DIL_QBLOCKS // dilation
            _gather_residues(q_ref, qs, 0, dilation, QK_SCALE)
            _gather_residues(k_ref, ks, BLK, dilation)
            _gather_residues(v_ref, vs, BLK, dilation)
            _gather_residues(do_ref, dos, 0, dilation)
            _gather_residues(lse_ref, lse_res, 0, dilation)
            _gather_residues(dsum_ref, dsum_res, 0, dilation)
            dk_acc[...] = jnp.zeros_like(dk_acc)
            dv_acc[...] = jnp.zeros_like(dv_acc)
            geometry = _dil_geometry(grp, pair)

            def step(blk, _):
                off = pl.multiple_of(blk * BLK, BLK)
                q2 = _stack_heads(qs[pl.ds(off, BLK), :], lane0)
                do2 = _stack_heads(dos[pl.ds(off, BLK), :], lane0)
                kk = ks[pl.ds(off, 2 * BLK), :]
                vv = vs[pl.ds(off, 2 * BLK), :]
                lse_blk = lse_res[pl.ds(off, BLK), :]
                dsum_blk = dsum_res[pl.ds(off, BLK), :]
                lse2 = jnp.concatenate([_head_col(lse_blk, lane0), _head_col(lse_blk, lane1)], axis=0)
                dsum2 = jnp.concatenate([_head_col(dsum_blk, lane0), _head_col(dsum_blk, lane1)], axis=0)
                s = _dil_tile_scores(q2, kk, geometry, blk % per_residue != 0)
                p = jnp.exp(s - lse2)
                ds = (p * (_dot_nt(do2, vv) - dsum2)).astype(BF16)
                dq2 = _dot(ds, kk)
                dq_res[pl.ds(off, BLK), :] = jnp.where(lane0, dq2[:BLK], dq2[BLK:]) * QK_SCALE
                dk_acc[pl.ds(off, 2 * BLK), :] += _dot_tn(ds, q2)
                dv_acc[pl.ds(off, 2 * BLK), :] += _dot_tn(p.astype(BF16), do2)
                return 0

            lax.fori_loop(0, DIL_QBLOCKS, step, 0, unroll=4)
            _scatter_residues(dq_res, 0, dq_ref, dilation)
            _scatter_residues(dk_acc, BLK, dk_ref, dilation)
            _scatter_residues(dv_acc, BLK, dv_ref, dilation)

    def col(part, grp):
        return pl.BlockSpec((None, SEQ, LANES), lambda b, p: (b, 0, 6 * part + 2 * grp + p))

    slot = pl.BlockSpec((None, SEQ, LANES), lambda b, p: (b, 0, p))
    out = jax.ShapeDtypeStruct((B_LOC, SEQ, DIL_OUT), F32)
    return _call(
        body, send, name="dil_bwd", grid=(B_LOC, DIL_OUT // LANES),
        in_specs=[col(part, grp) for grp in range(DIL_GROUPS) for part in range(3)] + [slot] * 3,
        out_specs=[slot] * 9, out_shape=[out] * 9,
        scratch_shapes=[pltpu.VMEM((SEQ, LANES), BF16), pltpu.VMEM((SEQ + BLK, LANES), BF16),
                        pltpu.VMEM((SEQ + BLK, LANES), BF16), pltpu.VMEM((SEQ, LANES), BF16),
                        pltpu.VMEM((SEQ, LANES), F32), pltpu.VMEM((SEQ, LANES), F32),
                        pltpu.VMEM((SEQ, LANES), F32), pltpu.VMEM((SEQ + BLK, LANES), F32),
                        pltpu.VMEM((SEQ + BLK, LANES), F32)],
        semantics=("parallel", "parallel"), operands=[qkv] * 9 + [d_o, lse, dsum])


def _peers():
    x, y, c = lax.axis_index("x"), lax.axis_index("y"), lax.axis_index("c")
    me = 4 * x + 2 * y + c
    peers = []
    for mask in range(1, N_DEV):
        px = 1 - x if mask & 4 else x
        py = 1 - y if mask & 2 else y
        pc = 1 - c if mask & 1 else c
        peers.append(((px, py, pc), 4 * px + 2 * py + pc))
    return me, peers


def _all_gather(shard, name):
    def body(src_ref, out_ref, send_sems, recv_sems, local_sem):
        x, y, c = lax.axis_index("x"), lax.axis_index("y"), lax.axis_index("c")
        sibling = (x, y, 1 - c)
        chips = [(1 - x, y), (x, 1 - y), (1 - x, 1 - y)]

        def slot(px, py, pc):
            return out_ref.at[4 * px + 2 * py + pc]

        def copy(k, block, to, src=None):
            return pltpu.make_async_remote_copy(
                src_ref=slot(*block) if src is None else src, dst_ref=slot(*block),
                send_sem=send_sems.at[k], recv_sem=recv_sems.at[k], device_id=to,
                device_id_type=pl.DeviceIdType.MESH)

        mine = pltpu.make_async_copy(src_ref, slot(x, y, c), local_sem)
        mine.start()
        first = [copy(0, (x, y, c), sibling, src=src_ref)]
        first += [copy(1 + j, (x, y, c), (*chip, c), src=src_ref) for j, chip in enumerate(chips)]
        for cp in first:
            cp.start()
        passed = [copy(4 + j, (*chip, c), sibling) for j, chip in enumerate(chips)]
        for j, chip in enumerate(chips):
            copy(1 + j, (*chip, c), (x, y, c)).wait_recv()
            passed[j].start()
        copy(0, sibling, (x, y, c)).wait_recv()
        for j, chip in enumerate(chips):
            copy(4 + j, (*chip, 1 - c), (x, y, c)).wait_recv()
        for cp in first + passed:
            cp.wait_send()
        mine.wait()

    return pl.pallas_call(
        body, name=name,
        in_specs=[pl.BlockSpec(memory_space=pl.ANY)],
        out_specs=pl.BlockSpec(memory_space=pl.ANY),
        out_shape=jax.ShapeDtypeStruct((N_DEV,) + shard.shape, shard.dtype),
        scratch_shapes=[pltpu.SemaphoreType.DMA((N_DEV - 1,)), pltpu.SemaphoreType.DMA((N_DEV - 1,)),
                        pltpu.SemaphoreType.DMA],
    )(shard)


def _call(body, send, *, name, grid, in_specs, out_specs, out_shape, scratch_shapes, semantics, operands):
    if send is None:
        return pl.pallas_call(
            body, name=name, grid=grid, in_specs=in_specs, out_specs=out_specs, out_shape=out_shape,
            scratch_shapes=scratch_shapes, compiler_params=_params(semantics))(*operands), []
    srcs, kind = send
    scatter = kind == "scatter"
    n, n_in, n_out, n_scr = len(srcs), len(in_specs), len(out_specs), len(scratch_shapes)
    steps = math.prod(grid)
    relay_step = (3 * steps) // 4

    def plan(refs):
        src_refs, land_refs = refs[n_in:n_in + n], refs[n_in + n + n_out:n_in + 2 * n + n_out]
        send_sems, recv_sems, local_sems = refs[-3:]
        x, y, c = lax.axis_index("x"), lax.axis_index("y"), lax.axis_index("c")
        me, peers = _peers()
        first, relayed_in, relayed_out, arrivals, sends, own = [], [], [], [], [], []
        for a in range(n):
            def copy(k, src, dst_slot, to):
                return pltpu.make_async_remote_copy(
                    src_ref=src, dst_ref=land_refs[a].at[dst_slot], send_sem=send_sems.at[a * (N_DEV - 1) + k],
                    recv_sem=recv_sems.at[a * (N_DEV - 1) + k], device_id=to, device_id_type=pl.DeviceIdType.MESH)

            if kind == "gather_by_chip":
                idx = lambda px, py, pc: 4 * px + 2 * py + pc
                chips = [(1 - x, y), (x, 1 - y), (1 - x, 1 - y)]
                first.append(copy(0, src_refs[a], me, (x, y, 1 - c)))
                arrivals.append(copy(0, src_refs[a], idx(x, y, 1 - c), (x, y, 1 - c)))
                for j, (px, py) in enumerate(chips):
                    first.append(copy(1 + j, src_refs[a], me, (px, py, c)))
                    relayed_in.append(copy(1 + j, src_refs[a], idx(px, py, c), (px, py, c)))
                    relayed_out.append(copy(4 + j, land_refs[a].at[idx(px, py, c)], idx(px, py, c), (x, y, 1 - c)))
                    arrivals.append(copy(4 + j, src_refs[a], idx(px, py, 1 - c), (x, y, 1 - c)))
                sends += [first[-4], first[-3], first[-2], first[-1]] + relayed_out[-3:]
            else:
                for k, (peer, peer_idx) in enumerate(peers):
                    src = src_refs[a].at[peer_idx] if scatter else src_refs[a]
                    first.append(copy(k, src, me, peer))
                    arrivals.append(copy(k, src, peer_idx, peer))
                sends += first[-(N_DEV - 1):]
            own.append(pltpu.make_async_copy(src_refs[a].at[me] if scatter else src_refs[a], land_refs[a].at[me],
                                             local_sems.at[a]))
        return first, relayed_in, relayed_out, arrivals, sends, own

    def wrapped(*refs):
        step = 0
        for axis, size in enumerate(grid):
            step = step * size + pl.program_id(axis)

        @pl.when(step == 0)
        def _():
            first, _, _, _, _, own = plan(refs)
            for cp in first + own:
                cp.start()

        if kind == "gather_by_chip":
            @pl.when(step == relay_step)
            def _():
                _, relayed_in, relayed_out, _, _, _ = plan(refs)
                for cp_in, cp_out in zip(relayed_in, relayed_out):
                    cp_in.wait_recv()
                    cp_out.start()

        body(*refs[:n_in], *refs[n_in + n:n_in + n + n_out], *refs[n_in + 2 * n + n_out:n_in + 2 * n + n_out + n_scr])

        @pl.when(step == steps - 1)
        def _():
            _, _, _, arrivals, sends, own = plan(refs)
            for cp in arrivals:
                cp.wait_recv()
            for cp in sends:
                cp.wait_send()
            for cp in own:
                cp.wait()

    anywhere = pl.BlockSpec(memory_space=pl.ANY)
    lands = [jax.ShapeDtypeStruct((N_DEV,) + s.shape[-2:], s.dtype) for s in srcs]
    out = pl.pallas_call(
        wrapped, name=name, grid=grid,
        in_specs=list(in_specs) + [anywhere] * n, out_specs=list(out_specs) + [anywhere] * n,
        out_shape=list(out_shape) + lands,
        scratch_shapes=list(scratch_shapes) + [pltpu.SemaphoreType.DMA((n * (N_DEV - 1),)),
                                               pltpu.SemaphoreType.DMA((n * (N_DEV - 1),)),
                                               pltpu.SemaphoreType.DMA((n,))],
        compiler_params=_params(("arbitrary",) * len(grid)),
    )(*operands, *srcs)
    return out[:n_out], list(out[n_out:])


def _sum_in_device_order(land_ref):
    acc = land_ref[0].astype(F32)
    for j in range(1, N_DEV):
        acc = acc + land_ref[j].astype(F32)
    return acc


def _adam_math(w, g, m, v):
    c1 = 1.0 - ADAM_B1 ** ADAM_STEP
    c2 = 1.0 - ADAM_B2 ** ADAM_STEP
    m_new = ADAM_B1 * m + (1.0 - ADAM_B1) * g
    v_new = ADAM_B2 * v + (1.0 - ADAM_B2) * (g * g)
    delta = -ADAM_LR * ((m_new / c1) / (jnp.sqrt(v_new / c2) + ADAM_EPS) + ADAM_WD * w)
    return delta, m_new, v_new


def _row_tile(rows):
    return max(t for t in range(8, 257, 8) if rows % t == 0) if rows % 8 == 0 else rows


def _sum_update(land, w, m, v, name):
    _, rows, cols = land.shape
    tile_rows = _row_tile(rows)

    def body(land_ref, w_ref, m_ref, v_ref, g_ref, d_ref, nm_ref, nv_ref):
        g = _sum_in_device_order(land_ref)
        g_ref[...] = g
        d_ref[...], nm_ref[...], nv_ref[...] = _adam_math(w_ref[...], g, m_ref[...], v_ref[...])

    tile = pl.BlockSpec((tile_rows, cols), lambda i: (i, 0))
    out = jax.ShapeDtypeStruct((rows, cols), F32)
    return pl.pallas_call(
        body, name=name, grid=(rows // tile_rows,),
        in_specs=[pl.BlockSpec((N_DEV, tile_rows, cols), lambda i: (0, i, 0)), tile, tile, tile],
        out_specs=[tile] * 4, out_shape=[out] * 4,
        compiler_params=_params(("parallel",)),
    )(land, w, m, v)


def _sum_gains(land):
    def body(land_ref, o_ref):
        o_ref[...] = _sum_in_device_order(land_ref)

    return pl.pallas_call(
        body, name="sum_gain_grads", grid=(1,),
        in_specs=[pl.BlockSpec(land.shape, lambda i: (0, 0, 0))],
        out_specs=pl.BlockSpec(land.shape[1:], lambda i: (0, 0)),
        out_shape=jax.ShapeDtypeStruct(land.shape[1:], F32),
    )(land)


def _adamw(w, g, m, v, name):
    def body(w_ref, g_ref, m_ref, v_ref, d_ref, nm_ref, nv_ref):
        d_ref[...], nm_ref[...], nv_ref[...] = _adam_math(w_ref[...], g_ref[...], m_ref[...], v_ref[...])

    whole = pl.BlockSpec(w.shape, lambda i: (0, 0))
    out = jax.ShapeDtypeStruct(w.shape, F32)
    return pl.pallas_call(
        body, name=name, grid=(1,),
        in_specs=[whole] * 4, out_specs=[whole] * 3, out_shape=[out] * 3,
    )(w, g, m, v)


GROUP_FFN = ("w_ffn_in", "w_ffn_out")
GROUP_MIX = ("w_sb_up", "w_dil_up", "w_out")
COL_SHARDED = ("w_in", "w_sb_up", "w_dil_up", "w_ffn_in")


def _full_from_shards(name, slots):
    _, r, c = slots.shape
    if name in COL_SHARDED:
        return slots.transpose(1, 0, 2).reshape(r, N_DEV * c)
    return slots.reshape(N_DEV * r, c)


def _shards_from_full(name, full):
    rows, cols = full.shape
    if name in COL_SHARDED:
        return full.reshape(rows, N_DEV, cols // N_DEV).transpose(1, 0, 2)
    return full.reshape(N_DEV, rows // N_DEV, cols)


def _local_step(x, target, g_mix, g_ffn, g_fin, w_in, shards=None, rest=None):
    gather = lambda names, kind: None if shards is None else ([shards[n] for n in names], kind)
    scatter = lambda blocks: None if shards is None else (blocks, "scatter")
    landed = lambda blocks, lands: lands if lands else blocks

    w = {"w_in": w_in}
    if shards is None:
        w.update(rest)
    qkv_sb, qkv_dl, gates, u = _norm_proj(x, g_mix, w["w_in"])
    qkv_sb = qkv_sb.reshape(B_LOC, SEQ, 3 * SB_WIDTH)
    qkv_dl = qkv_dl.reshape(B_LOC, SEQ, 3 * DIL_WIDTH)
    (o_sb,), lands = _sb_fwd(qkv_sb, gather(GROUP_FFN, "gather_by_chip"))
    w.update({n: _full_from_shards(n, t) for n, t in zip(GROUP_FFN, lands)})
    o_sb = o_sb.reshape(TOK, SB_WIDTH)
    (o_dl, lse), lands = _dil_fwd(qkv_dl, gather(GROUP_MIX, "gather"))
    w.update({n: _full_from_shards(n, t) for n, t in zip(GROUP_MIX, lands)})
    o_dl = o_dl.reshape(TOK, DIL_OUT)

    x1, merged = _mix_out(x, o_sb, o_dl, gates, w["w_sb_up"], w["w_dil_up"], w["w_out"])
    loss, dx1, u2, act, dh, dx2, dg_fin, dg_ffn = _ffn_fwd_bwd(x1, target, g_ffn, g_fin, w["w_ffn_in"], w["w_ffn_out"])
    dgates, dy_sb, dy_dl, do_sb, do_dl, dsum = _mix_bwd(dx1, o_sb, o_dl, gates, w["w_sb_up"], w["w_dil_up"], w["w_out"])
    blocks = {
        "w_sb_up": _atb(o_sb, dy_sb, "grad_w_sb_up", SB_WIDTH, D_MODEL, col_blocks=N_DEV),
        "w_dil_up": _atb(o_dl.astype(BF16), dy_dl, "grad_w_dil_up", DIL_OUT, D_MODEL, col_blocks=N_DEV),
        "w_out": _shards_from_full("w_out", _atb(merged, dx1.astype(BF16), "grad_w_out", D_MODEL, D_MODEL)),
        "w_ffn_in": _shards_from_full("w_ffn_in", _atb(u2, dh, "grad_w_ffn_in", D_MODEL, D_FF)),
        "w_ffn_out": _shards_from_full("w_ffn_out", _atb(act, dx2, "grad_w_ffn_out", D_FF // 2, D_MODEL)),
    }
    grads = {}

    early, late = ("w_ffn_in",), ("w_ffn_out",) + GROUP_MIX
    early_blocks = [blocks[n] for n in early]
    (dq_sb, dk_sb, dv_sb), lands = _sb_bwd(qkv_sb, do_sb.reshape(B_LOC, SEQ, SB_WIDTH), scatter(early_blocks))
    grads.update(zip(early, landed(early_blocks, lands)))
    as_batch = lambda t: t.reshape(B_LOC, SEQ, DIL_OUT)
    late_blocks = [blocks[n] for n in late]
    d_dl, lands = _dil_bwd(qkv_dl, as_batch(do_dl), lse, as_batch(dsum), scatter(late_blocks))
    grads.update(zip(late, landed(late_blocks, lands)))
    flat = lambda t: t.reshape(TOK, -1)
    dproj = jnp.concatenate(
        [flat(dq_sb), flat(dk_sb), flat(dv_sb)]
        + [flat(d_dl[3 * grp + part]).astype(BF16) for part in range(3) for grp in range(DIL_GROUPS)]
        + [dgates], axis=1)

    w_in_blocks = [_shards_from_full("w_in", _atb(u, dproj, "grad_w_in", D_MODEL, IN_WIDTH // 2))]
    (grad_x, dg_mix), lands = _proj_bwd(dproj, dx1, x, g_mix, w["w_in"], scatter(w_in_blocks))
    grads["w_in"] = landed(w_in_blocks, lands)[0]
    gain_grads = jnp.concatenate([dg_mix, dg_ffn, dg_fin], axis=0)
    return loss, grad_x, gain_grads, grads


def kernel(x, norm_mix_g, w_in, w_sb_up, w_dil_up, w_out, norm_ffn_g, w_ffn_in, w_ffn_out, norm_final_g, loss_target, m_norm_mix_g, m_w_in, m_w_sb_up, m_w_dil_up, m_w_out, m_norm_ffn_g, m_w_ffn_in, m_w_ffn_out, m_norm_final_g, v_norm_mix_g, v_w_in, v_w_sb_up, v_w_dil_up, v_w_out, v_norm_ffn_g, v_w_ffn_in, v_w_ffn_out, v_norm_final_g):
    mats = {"w_in": w_in, "w_sb_up": w_sb_up, "w_dil_up": w_dil_up, "w_out": w_out,
            "w_ffn_in": w_ffn_in, "w_ffn_out": w_ffn_out}
    moments_m = {"w_in": m_w_in, "w_sb_up": m_w_sb_up, "w_dil_up": m_w_dil_up, "w_out": m_w_out,
                 "w_ffn_in": m_w_ffn_in, "w_ffn_out": m_w_ffn_out}
    moments_v = {"w_in": v_w_in, "w_sb_up": v_w_sb_up, "w_dil_up": v_w_dil_up, "w_out": v_w_out,
                 "w_ffn_in": v_w_ffn_in, "w_ffn_out": v_w_ffn_out}
    gathered_w_in = _all_gather(w_in[0].astype(BF16), "all_gather_w_in")
    g_fin = norm_final_g.reshape(1, D_MODEL)
    loss, grad_x, gain_grads, grad_slots = _local_step(
        x.reshape(TOK, D_MODEL), loss_target.reshape(TOK, D_MODEL), norm_mix_g, norm_ffn_g, g_fin,
        _full_from_shards("w_in", gathered_w_in),
        shards={name: mats[name][0].astype(BF16) for name in GROUP_FFN + GROUP_MIX})

    gain_rows = jnp.concatenate([gain_grads, jnp.tile(loss, (1, D_MODEL // LANES)),
                                 jnp.zeros((8 - 4, D_MODEL), F32)], axis=0)
    g_gains = _sum_gains(_all_gather(gain_rows, "all_gather_gains"))

    out_g, out_d, out_m, out_v = {}, {}, {}, {}
    for name, slots in grad_slots.items():
        g, d, nm, nv = _sum_update(slots, mats[name][0], moments_m[name][0], moments_v[name][0], "update_" + name)
        out_g[name], out_d[name], out_m[name], out_v[name] = g[None], d[None], nm[None], nv[None]

    gain_w = jnp.concatenate([norm_mix_g, norm_ffn_g, g_fin], axis=0)
    gain_m = jnp.concatenate([m_norm_mix_g, m_norm_ffn_g, m_norm_final_g.reshape(1, D_MODEL)], axis=0)
    gain_v = jnp.concatenate([v_norm_mix_g, v_norm_ffn_g, v_norm_final_g.reshape(1, D_MODEL)], axis=0)
    gd, gm, gv = _adamw(gain_w, g_gains[:3], gain_m, gain_v, "adamw_gains")
    for idx, name in enumerate(("norm_mix_g", "norm_ffn_g", "norm_final_g")):
        shape = (D_MODEL,) if name == "norm_final_g" else (1, D_MODEL)
        out_g[name] = g_gains[idx].reshape(shape)
        out_d[name], out_m[name], out_v[name] = gd[idx].reshape(shape), gm[idx].reshape(shape), gv[idx].reshape(shape)

    order = ("norm_mix_g", "w_in", "w_sb_up", "w_dil_up", "w_out", "norm_ffn_g", "w_ffn_in", "w_ffn_out",
             "norm_final_g")
    return (g_gains[3, 0], grad_x.reshape(B_LOC, SEQ, D_MODEL),
            *[out_g[n] for n in order], *[out_d[n] for n in order],
            *[out_m[n] for n in order], *[out_v[n] for n in order])
```

```python
import math

import jax
import jax.numpy as jnp
from jax import lax
from jax.experimental import pallas as pl
from jax.experimental.pallas import tpu as pltpu

F32 = jnp.float32
BF16 = jnp.bfloat16

N_DEV = 8
D_MODEL = 1024
SEQ = 2048
B_LOC = 2
TOK = B_LOC * SEQ
HEAD_DIM = 64
SB_WIDTH = 512
DIL_WIDTH = 768
DIL_OUT = 256
QKV_WIDTH = 3 * SB_WIDTH + 3 * DIL_WIDTH
IN_WIDTH = QKV_WIDTH + 2 * D_MODEL
D_FF = 2816
DIL_PAIRS = ((128, 1), (512, 4), (2048, 16))
DIL_HEADS = 12
RMS_EPS = 1e-6
ALIBI_MAX_BIAS = 8.0
QK_SCALE = 1.0 / math.sqrt(HEAD_DIM)
BLK = 128
LANES = 128
NEG_BIG = -1e30

ADAM_LR = 0.001
ADAM_B1 = 0.9
ADAM_B2 = 0.999
ADAM_EPS = 1e-08
ADAM_WD = 0.01
ADAM_STEP = 10

VMEM_LIMIT = 56 * 1024 * 1024


def _dot(a, b):
    return jnp.dot(a, b, preferred_element_type=F32)


def _dot_nt(a, b):
    return lax.dot_general(a, b, (((1,), (1,)), ((), ())), preferred_element_type=F32)


def _dot_tn(a, b):
    return lax.dot_general(a, b, (((0,), (0,)), ((), ())), preferred_element_type=F32)


def _softplus(z):
    return jnp.maximum(z, 0.0) + jnp.log1p(jnp.exp(-jnp.abs(z)))


def _sigmoid(z):
    return 1.0 / (1.0 + jnp.exp(-z))


def _split_bf16(v):
    hi = v.astype(BF16)
    lo = (v - hi.astype(F32)).astype(BF16)
    return hi, lo


def _chunks(width, step=512):
    out, c = [], 0
    while c < width:
        w = min(step, width - c)
        out.append((c, w))
        c += w
    return out


def _resident(shape):
    nd = len(shape)
    return pl.BlockSpec(shape, lambda *_: (0,) * nd, pipeline_mode=pl.Buffered(1))


def _params(sem):
    return pltpu.CompilerParams(dimension_semantics=sem, vmem_limit_bytes=VMEM_LIMIT)


def _rms_fwd(x, g):
    r = lax.rsqrt(jnp.mean(x * x, axis=-1, keepdims=True) + RMS_EPS)
    n = x * r
    return n, r, n * g


def _rms_bwd(dy, n, r, g):
    dg = jnp.sum(dy * n, axis=0, keepdims=True)
    dn = dy * g
    dx = r * (dn - n * jnp.mean(dn * n, axis=-1, keepdims=True))
    return dx, dg


TM = 256


def _norm_proj(x, g, w_in):
    def body(x_ref, g_ref, w_ref, sb_ref, dl_ref, gate_ref, u_ref):
        _, _, u = _rms_fwd(x_ref[...], g_ref[...])
        u = u.astype(BF16)
        u_ref[...] = u
        for c0, w in _chunks(3 * SB_WIDTH):
            sb_ref[:, c0:c0 + w] = _dot(u, w_ref[:, c0:c0 + w]).astype(BF16)
        for c0, w in _chunks(3 * DIL_WIDTH):
            dl_ref[:, c0:c0 + w] = _dot(u, w_ref[:, 3 * SB_WIDTH + c0:3 * SB_WIDTH + c0 + w])
        for c0, w in _chunks(2 * D_MODEL):
            gate_ref[:, c0:c0 + w] = _dot(u, w_ref[:, QKV_WIDTH + c0:QKV_WIDTH + c0 + w])

    return pl.pallas_call(
        body, name="norm_proj", grid=(TOK // TM,),
        in_specs=[pl.BlockSpec((TM, D_MODEL), lambda i: (i, 0)), _resident((1, D_MODEL)),
                  _resident((D_MODEL, IN_WIDTH))],
        out_specs=[pl.BlockSpec((TM, 3 * SB_WIDTH), lambda i: (i, 0)),
                   pl.BlockSpec((TM, 3 * DIL_WIDTH), lambda i: (i, 0)),
                   pl.BlockSpec((TM, 2 * D_MODEL), lambda i: (i, 0)),
                   pl.BlockSpec((TM, D_MODEL), lambda i: (i, 0))],
        out_shape=[jax.ShapeDtypeStruct((TOK, 3 * SB_WIDTH), BF16),
                   jax.ShapeDtypeStruct((TOK, 3 * DIL_WIDTH), F32),
                   jax.ShapeDtypeStruct((TOK, 2 * D_MODEL), F32),
                   jax.ShapeDtypeStruct((TOK, D_MODEL), BF16)],
        compiler_params=_params(("parallel",)),
    )(x, g, w_in)


def _mix_out(x, o_sb, o_dl, gates, w_sb_up, w_dil_up, w_out):
    def body(x_ref, osb_ref, odl_ref, gate_ref, wsb_ref, wdl_ref, wout_ref, x1_ref, mg_ref):
        y_sb = _dot(osb_ref[...], wsb_ref[...])
        y_dl = _dot(odl_ref[...].astype(BF16), wdl_ref[...])
        merged = (_sigmoid(gate_ref[:, :D_MODEL]) * y_sb
                  + _sigmoid(gate_ref[:, D_MODEL:]) * y_dl).astype(BF16)
        mg_ref[...] = merged
        x1_ref[...] = x_ref[...] + _dot(merged, wout_ref[...])

    return pl.pallas_call(
        body, name="mix_out", grid=(TOK // TM,),
        in_specs=[pl.BlockSpec((TM, D_MODEL), lambda i: (i, 0)),
                  pl.BlockSpec((TM, SB_WIDTH), lambda i: (i, 0)),
                  pl.BlockSpec((TM, DIL_OUT), lambda i: (i, 0)),
                  pl.BlockSpec((TM, 2 * D_MODEL), lambda i: (i, 0)),
                  _resident((SB_WIDTH, D_MODEL)), _resident((DIL_OUT, D_MODEL)),
                  _resident((D_MODEL, D_MODEL))],
        out_specs=[pl.BlockSpec((TM, D_MODEL), lambda i: (i, 0)),
                   pl.BlockSpec((TM, D_MODEL), lambda i: (i, 0))],
        out_shape=[jax.ShapeDtypeStruct((TOK, D_MODEL), F32),
                   jax.ShapeDtypeStruct((TOK, D_MODEL), BF16)],
        compiler_params=_params(("parallel",)),
    )(x, o_sb, o_dl, gates, w_sb_up, w_dil_up, w_out)


FF_CHUNK = D_FF // 2


def _ffn_fwd_bwd(x1, target, g_ffn, g_fin, w_ffn_in, w_ffn_out):
    def body(x1_ref, t_ref, gffn_ref, gfin_ref, win_ref, wout_ref,
             loss_ref, dx1_ref, u2_ref, act_ref, dh_ref, dx2_ref, dgfin_ref, dgffn_ref, h_scr):
        i = pl.program_id(0)

        @pl.when(i == 0)
        def _():
            loss_ref[...] = jnp.zeros_like(loss_ref)
            dgfin_ref[...] = jnp.zeros_like(dgfin_ref)
            dgffn_ref[...] = jnp.zeros_like(dgffn_ref)

        x1 = x1_ref[...]
        g_ffn_v = gffn_ref[...]
        g_fin_v = gfin_ref[...]
        n2, r2, u2 = _rms_fwd(x1, g_ffn_v)
        u2 = u2.astype(BF16)
        u2_ref[...] = u2
        x2 = x1
        for c0 in range(0, D_FF, FF_CHUNK):
            gate = _dot(u2, win_ref[:, c0:c0 + FF_CHUNK])
            up = _dot(u2, win_ref[:, D_FF + c0:D_FF + c0 + FF_CHUNK])
            h_scr[:, c0:c0 + FF_CHUNK] = gate
            h_scr[:, D_FF + c0:D_FF + c0 + FF_CHUNK] = up
            act = (gate * _sigmoid(gate) * up).astype(BF16)
            act_ref[:, c0:c0 + FF_CHUNK] = act
            x2 = x2 + _dot(act, wout_ref[c0:c0 + FF_CHUNK, :])
        n3, r3, y = _rms_fwd(x2, g_fin_v)
        err = y - t_ref[...]
        sq = jnp.sum(jnp.sum(err * err, axis=1, keepdims=True), axis=0, keepdims=True)
        loss_ref[...] += sq * (0.5 / D_MODEL)
        dx2, dgfin = _rms_bwd(err * (1.0 / D_MODEL), n3, r3, g_fin_v)
        dgfin_ref[...] += dgfin
        dx2_b = dx2.astype(BF16)
        dx2_ref[...] = dx2_b
        du2 = jnp.zeros((TM, D_MODEL), F32)
        for c0 in range(0, D_FF, FF_CHUNK):
            gate = h_scr[:, c0:c0 + FF_CHUNK]
            up = h_scr[:, D_FF + c0:D_FF + c0 + FF_CHUNK]
            dact = _dot_nt(dx2_b, wout_ref[c0:c0 + FF_CHUNK, :])
            sg = _sigmoid(gate)
            dgate = (dact * up * (sg * (1.0 + gate * (1.0 - sg)))).astype(BF16)
            dup = (dact * (gate * sg)).astype(BF16)
            dh_ref[:, c0:c0 + FF_CHUNK] = dgate
            dh_ref[:, D_FF + c0:D_FF + c0 + FF_CHUNK] = dup
            du2 = du2 + _dot_nt(dgate, win_ref[:, c0:c0 + FF_CHUNK])
            du2 = du2 + _dot_nt(dup, win_ref[:, D_FF + c0:D_FF + c0 + FF_CHUNK])
        dx1_n, dgffn = _rms_bwd(du2, n2, r2, g_ffn_v)
        dgffn_ref[...] += dgffn
        dx1_ref[...] = dx2 + dx1_n

    tile = lambda w: pl.BlockSpec((TM, w), lambda i: (i, 0))
    acc = lambda w: pl.BlockSpec((1, w), lambda i: (0, 0))
    return pl.pallas_call(
        body, name="ffn_fwd_bwd", grid=(TOK // TM,),
        in_specs=[tile(D_MODEL), tile(D_MODEL), _resident((1, D_MODEL)), _resident((1, D_MODEL)),
                  _resident((D_MODEL, 2 * D_FF)), _resident((D_FF, D_MODEL))],
        out_specs=[acc(LANES), tile(D_MODEL), tile(D_MODEL), tile(D_FF), tile(2 * D_FF), tile(D_MODEL),
                   acc(D_MODEL), acc(D_MODEL)],
        out_shape=[jax.ShapeDtypeStruct((1, LANES), F32),
                   jax.ShapeDtypeStruct((TOK, D_MODEL), F32),
                   jax.ShapeDtypeStruct((TOK, D_MODEL), BF16),
                   jax.ShapeDtypeStruct((TOK, D_FF), BF16),
                   jax.ShapeDtypeStruct((TOK, 2 * D_FF), BF16),
                   jax.ShapeDtypeStruct((TOK, D_MODEL), BF16),
                   jax.ShapeDtypeStruct((1, D_MODEL), F32),
                   jax.ShapeDtypeStruct((1, D_MODEL), F32)],
        scratch_shapes=[pltpu.VMEM((TM, 2 * D_FF), F32)],
        compiler_params=_params(("arbitrary",)),
    )(x1, target, g_ffn, g_fin, w_ffn_in, w_ffn_out)


def _mix_bwd(dx1, o_sb, o_dl, gates, w_sb_up, w_dil_up, w_out):
    def body(dx1_ref, osb_ref, odl_ref, gate_ref, wsb_ref, wdl_ref, wout_ref,
             dgate_ref, dysb_ref, dydl_ref, dosb_ref, dodl_ref, dsum_ref):
        dmerged = _dot_nt(dx1_ref[...].astype(BF16), wout_ref[...])
        o_dl = odl_ref[...]
        y_sb = _dot(osb_ref[...], wsb_ref[...])
        y_dl = _dot(o_dl.astype(BF16), wdl_ref[...])
        s_sb = _sigmoid(gate_ref[:, :D_MODEL])
        s_dl = _sigmoid(gate_ref[:, D_MODEL:])
        dgate_ref[:, :D_MODEL] = (dmerged * y_sb * (s_sb * (1.0 - s_sb))).astype(BF16)
        dgate_ref[:, D_MODEL:] = (dmerged * y_dl * (s_dl * (1.0 - s_dl))).astype(BF16)
        dy_sb = (dmerged * s_sb).astype(BF16)
        dy_dl = (dmerged * s_dl).astype(BF16)
        dysb_ref[...] = dy_sb
        dydl_ref[...] = dy_dl
        dosb_ref[...] = _dot_nt(dy_sb, wsb_ref[...]).astype(BF16)
        do_dl = _dot_nt(dy_dl, wdl_ref[...])
        dodl_ref[...] = do_dl
        row = lax.broadcasted_iota(jnp.int32, (DIL_OUT, DIL_OUT), 0) // HEAD_DIM
        col = lax.broadcasted_iota(jnp.int32, (DIL_OUT, DIL_OUT), 1) // HEAD_DIM
        same_head = (row == col).astype(BF16)
        hi, lo = _split_bf16(do_dl * o_dl)
        dsum_ref[...] = _dot(hi, same_head) + _dot(lo, same_head)

    tile = lambda w: pl.BlockSpec((TM, w), lambda i: (i, 0))
    return pl.pallas_call(
        body, name="mix_bwd", grid=(TOK // TM,),
        in_specs=[tile(D_MODEL), tile(SB_WIDTH), tile(DIL_OUT), tile(2 * D_MODEL),
                  _resident((SB_WIDTH, D_MODEL)), _resident((DIL_OUT, D_MODEL)),
                  _resident((D_MODEL, D_MODEL))],
        out_specs=[tile(2 * D_MODEL), tile(D_MODEL), tile(D_MODEL), tile(SB_WIDTH), tile(DIL_OUT),
                   tile(DIL_OUT)],
        out_shape=[jax.ShapeDtypeStruct((TOK, 2 * D_MODEL), BF16),
                   jax.ShapeDtypeStruct((TOK, D_MODEL), BF16),
                   jax.ShapeDtypeStruct((TOK, D_MODEL), BF16),
                   jax.ShapeDtypeStruct((TOK, SB_WIDTH), BF16),
                   jax.ShapeDtypeStruct((TOK, DIL_OUT), F32),
                   jax.ShapeDtypeStruct((TOK, DIL_OUT), F32)],
        compiler_params=_params(("parallel",)),
    )(dx1, o_sb, o_dl, gates, w_sb_up, w_dil_up, w_out)


def _proj_bwd(dproj, dx1, x, g, w_in, send=None):
    def body(dp_ref, dx1_ref, x_ref, g_ref, w_ref, dx_ref, dg_ref):
        @pl.when(pl.program_id(0) == 0)
        def _():
            dg_ref[...] = jnp.zeros_like(dg_ref)

        du = jnp.zeros((TM, D_MODEL), F32)
        for c0, w in _chunks(IN_WIDTH, 1024):
            du = du + _dot_nt(dp_ref[:, c0:c0 + w], w_ref[:, c0:c0 + w])
        g_v = g_ref[...]
        n, r, _ = _rms_fwd(x_ref[...], g_v)
        dx, dg = _rms_bwd(du, n, r, g_v)
        dg_ref[...] += dg
        dx_ref[...] = dx1_ref[...] + dx

    tile = lambda w: pl.BlockSpec((TM, w), lambda i: (i, 0))
    return _call(
        body, send, name="proj_bwd", grid=(TOK // TM,),
        in_specs=[tile(IN_WIDTH), tile(D_MODEL), tile(D_MODEL), _resident((1, D_MODEL)),
                  _resident((D_MODEL, IN_WIDTH))],
        out_specs=[tile(D_MODEL), pl.BlockSpec((1, D_MODEL), lambda i: (0, 0))],
        out_shape=[jax.ShapeDtypeStruct((TOK, D_MODEL), F32),
                   jax.ShapeDtypeStruct((1, D_MODEL), F32)],
        scratch_shapes=[], semantics=("arbitrary",), operands=(dproj, dx1, x, g, w_in))


def _atb(a, b, name, tm, tn, col_blocks=0, tk=512):
    m, n = a.shape[1], b.shape[1]
    nk = TOK // tk

    def body(a_ref, b_ref, o_ref, acc_ref):
        k = pl.program_id(2)

        @pl.when(k == 0)
        def _():
            acc_ref[...] = jnp.zeros_like(acc_ref)

        acc_ref[...] += _dot_tn(a_ref[...], b_ref[...])

        @pl.when(k == nk - 1)
        def _():
            if col_blocks:
                width = n // col_blocks
                for blk in range(col_blocks):
                    o_ref[blk] = acc_ref[:, blk * width:(blk + 1) * width].astype(BF16)
            else:
                o_ref[...] = acc_ref[...].astype(BF16)

    if col_blocks:
        out_spec = pl.BlockSpec((col_blocks, tm, n // col_blocks), lambda i, j, k: (0, i, 0))
        out_shape = jax.ShapeDtypeStruct((col_blocks, m, n // col_blocks), BF16)
    else:
        out_spec = pl.BlockSpec((tm, tn), lambda i, j, k: (i, j))
        out_shape = jax.ShapeDtypeStruct((m, n), BF16)
    return pl.pallas_call(
        body, name=name, grid=(m // tm, n // tn, nk),
        in_specs=[pl.BlockSpec((tk, tm), lambda i, j, k: (k, i)),
                  pl.BlockSpec((tk, tn), lambda i, j, k: (k, j))],
        out_specs=out_spec, out_shape=out_shape,
        scratch_shapes=[pltpu.VMEM((tm, tn), F32)],
        compiler_params=_params(("parallel", "parallel", "arbitrary")),
    )(a, b)


SB_PAIRS = SB_WIDTH // LANES


def _two_heads(v, lane0):
    zero = jnp.zeros_like(v)
    return jnp.where(lane0, v, zero), jnp.where(lane0, zero, v)


SB_QBLK = 256
N_SB_STEPS = SEQ // SB_QBLK


SB_KCHUNK = 2 * BLK
SB_ROWS = 2 * SB_QBLK
SB_DEAD = -104.0


def _log_keep(z):
    neg_z = -z
    return jnp.minimum(neg_z, 0.0) - jnp.log(1.0 + jnp.exp(jnp.minimum(z, neg_z)))


def _stack_heads(v, lane0):
    return jnp.concatenate(_two_heads(v, lane0), axis=0)


def _block_sums(v, tri):
    halves = (v[:, :BLK], v[:, BLK:])
    hi, lo = _split_bf16(jnp.concatenate(halves, axis=0))
    prod = _dot(jnp.concatenate([hi, lo], axis=0), tri)
    tri_sum = prod[:2 * SB_ROWS] + prod[2 * SB_ROWS:]
    sums = tuple(jnp.sum(h, axis=1, keepdims=True) for h in halves)
    return (tri_sum[:SB_ROWS], tri_sum[SB_ROWS:]), sums


def _sb_diag_mask():
    row = lax.broadcasted_iota(jnp.int32, (SB_ROWS, SB_KCHUNK), 0)
    col = lax.broadcasted_iota(jnp.int32, (SB_ROWS, SB_KCHUNK), 1)
    return col < jnp.where(row >= SB_QBLK, row - SB_QBLK, row)


def _sb_fwd(qkv, send=None):
    def body(q_ref, k_ref, v_ref, o_ref):
        i = pl.program_id(2)
        krow = lax.broadcasted_iota(jnp.int32, (BLK, BLK), 0)
        kcol = lax.broadcasted_iota(jnp.int32, (BLK, BLK), 1)
        later = (krow > kcol).astype(BF16)
        lane0 = lax.broadcasted_iota(jnp.int32, (SB_QBLK, LANES), 1) < HEAD_DIM
        q2 = _stack_heads(q_ref[0] * QK_SCALE, lane0)

        def chunk(c, carry, causal):
            acc, run = carry
            off = pl.multiple_of(c * SB_KCHUNK, SB_KCHUNK)
            z = _dot_nt(q2, k_ref[0, pl.ds(off, SB_KCHUNK), :])
            log_keep = _log_keep(z)
            if causal is not None:
                log_keep = jnp.where(causal, log_keep, 0.0)
            suffix, sums = _block_sums(log_keep, later)
            log_after = jnp.concatenate([suffix[0] + (run + sums[1]), suffix[1] + run], axis=1)
            a = jnp.exp(log_keep + z + log_after)
            if causal is not None:
                a = jnp.where(causal, a, 0.0)
            acc = acc + _dot(a.astype(BF16), v_ref[0, pl.ds(off, SB_KCHUNK), :])
            return acc, run + (sums[0] + sums[1])

        acc, run = chunk(i, (jnp.zeros((SB_ROWS, LANES), F32), jnp.zeros((SB_ROWS, 1), F32)), _sb_diag_mask())

        def live(state):
            t, _, run = state
            return jnp.logical_and(t < i, jnp.max(run) > SB_DEAD)

        def trip(state):
            t, acc, run = state
            acc, run = chunk(i - 1 - t, (acc, run), None)
            return t + 1, acc, run

        _, acc, _ = lax.while_loop(live, trip, (jnp.int32(0), acc, run))
        o_ref[0] = jnp.where(lane0, acc[:SB_QBLK], acc[SB_QBLK:]).astype(BF16)

    blk = pl.BlockSpec((1, SB_QBLK, LANES), lambda b, h, i: (b, i, h))
    return _call(
        body, send, name="sb_fwd", grid=(B_LOC, SB_PAIRS, N_SB_STEPS),
        in_specs=[blk,
                  pl.BlockSpec((1, SEQ, LANES), lambda b, h, i: (b, 0, SB_PAIRS + h)),
                  pl.BlockSpec((1, SEQ, LANES), lambda b, h, i: (b, 0, 2 * SB_PAIRS + h))],
        out_specs=[blk], out_shape=[jax.ShapeDtypeStruct((B_LOC, SEQ, SB_WIDTH), BF16)],
        scratch_shapes=[], semantics=("parallel", "parallel", "arbitrary"), operands=(qkv, qkv, qkv))


def _sb_bwd(qkv, d_o, send=None):
    def body(q_ref, k_ref, v_ref, do_ref, dq_ref, dk_ref, dv_ref, dk_acc, dv_acc):
        i = pl.program_id(2)
        krow = lax.broadcasted_iota(jnp.int32, (BLK, BLK), 0)
        kcol = lax.broadcasted_iota(jnp.int32, (BLK, BLK), 1)
        upto = (krow <= kcol).astype(BF16)
        earlier = (krow < kcol).astype(BF16)
        lane0 = lax.broadcasted_iota(jnp.int32, (SB_QBLK, LANES), 1) < HEAD_DIM
        q2 = _stack_heads(q_ref[0] * QK_SCALE, lane0)
        do2 = _stack_heads(do_ref[0], lane0)

        def keep_sum(c, causal):
            off = pl.multiple_of(c * SB_KCHUNK, SB_KCHUNK)
            log_keep = _log_keep(_dot_nt(q2, k_ref[0, pl.ds(off, SB_KCHUNK), :]))
            if causal is not None:
                log_keep = jnp.where(causal, log_keep, 0.0)
            return jnp.sum(log_keep, axis=1, keepdims=True)

        def live(state):
            t, run = state
            return jnp.logical_and(t < i, jnp.max(run) > SB_DEAD)

        walked, tot2 = lax.while_loop(live, lambda s: (s[0] + 1, s[1] + keep_sum(i - 1 - s[0], None)),
                                      (jnp.int32(0), keep_sum(i, _sb_diag_mask())))
        first = i - walked

        @pl.when(i == 0)
        def _():
            dk_acc[...] = jnp.zeros_like(dk_acc)
            dv_acc[...] = jnp.zeros_like(dv_acc)

        def chunk(c, carry, causal):
            dq, pre_keep, pre_e = carry
            off = pl.multiple_of(c * SB_KCHUNK, SB_KCHUNK)
            k_c = k_ref[0, pl.ds(off, SB_KCHUNK), :]
            v_c = v_ref[0, pl.ds(off, SB_KCHUNK), :]
            z = _dot_nt(q2, k_c)
            d_a = _dot_nt(do2, v_c)
            log_keep = _log_keep(z)
            if causal is not None:
                log_keep = jnp.where(causal, log_keep, 0.0)
            log_beta = log_keep + z
            prefix, sums = _block_sums(log_keep, upto)
            inclusive = jnp.concatenate([prefix[0], prefix[1] + sums[0]], axis=1)
            a = jnp.exp(log_beta + ((tot2 - pre_keep) - inclusive))
            if causal is not None:
                a = jnp.where(causal, a, 0.0)
            e = d_a * a
            e_prefix, e_sums = _block_sums(e, earlier)
            before = jnp.concatenate([e_prefix[0] + pre_e, e_prefix[1] + (pre_e + e_sums[0])], axis=1)
            dz = e - (e + before) * jnp.exp(log_beta)
            if causal is not None:
                dz = jnp.where(causal, dz, 0.0)
            dz = dz.astype(BF16)
            dq = dq + _dot(dz, k_c)
            dk_acc[pl.ds(off, SB_KCHUNK), :] += _dot_tn(dz, q2)
            dv_acc[pl.ds(off, SB_KCHUNK), :] += _dot_tn(a.astype(BF16), do2)
            return dq, pre_keep + (sums[0] + sums[1]), pre_e + (e_sums[0] + e_sums[1])

        zero_col = jnp.zeros((SB_ROWS, 1), F32)
        carry = lax.fori_loop(first, i, lambda t, c: chunk(t, c, None),
                              (jnp.zeros((SB_ROWS, LANES), F32), zero_col, zero_col))
        dq, _, _ = chunk(i, carry, _sb_diag_mask())
        dq_ref[0] = (jnp.where(lane0, dq[:SB_QBLK], dq[SB_QBLK:]) * QK_SCALE).astype(BF16)

        @pl.when(i == N_SB_STEPS - 1)
        def _():
            dk_ref[0] = dk_acc[...].astype(BF16)
            dv_ref[0] = dv_acc[...].astype(BF16)

    blk = pl.BlockSpec((1, SB_QBLK, LANES), lambda b, h, i: (b, i, h))
    whole = lambda c: pl.BlockSpec((1, SEQ, LANES), lambda b, h, i: (b, 0, c * SB_PAIRS + h))
    out = jax.ShapeDtypeStruct((B_LOC, SEQ, SB_WIDTH), BF16)
    return _call(
        body, send, name="sb_bwd", grid=(B_LOC, SB_PAIRS, N_SB_STEPS),
        in_specs=[blk, whole(1), whole(2), blk],
        out_specs=[blk, whole(0), whole(0)],
        out_shape=[out, out, out],
        scratch_shapes=[pltpu.VMEM((SEQ, LANES), F32), pltpu.VMEM((SEQ, LANES), F32)],
        semantics=("parallel", "parallel", "arbitrary"), operands=(qkv, qkv, qkv, d_o))


DIL_GROUPS = len(DIL_PAIRS)
DIL_QBLOCKS = SEQ // BLK


def _residue_rows(j, dilation):
    length = SEQ // dilation
    return pl.ds(j, length, stride=dilation) if dilation > 1 else pl.ds(0, length)


def _gather_residues(src_ref, dst_ref, dst_off, dilation, scale=None):
    length = SEQ // dilation
    for j in range(dilation):
        v = src_ref[_residue_rows(j, dilation), :]
        if scale is not None:
            v = v * scale
        dst_ref[dst_off + j * length:dst_off + (j + 1) * length, :] = v.astype(dst_ref.dtype)


def _scatter_residues(src_ref, src_off, dst_ref, dilation):
    length = SEQ // dilation
    for j in range(dilation):
        dst_ref[_residue_rows(j, dilation), :] = (
            src_ref[src_off + j * length:src_off + (j + 1) * length, :].astype(dst_ref.dtype))


def _dil_geometry(group, pair):
    dilation = DIL_PAIRS[group][1]
    row = lax.broadcasted_iota(jnp.int32, (2 * BLK, 2 * BLK), 0)
    col = lax.broadcasted_iota(jnp.int32, (2 * BLK, 2 * BLK), 1)
    second = row >= BLK
    steps = BLK + jnp.where(second, row - BLK, row) - col
    coef = -ALIBI_MAX_BIAS / DIL_HEADS * math.log(2.0)
    first_head = float(4 * group + 1) + 2.0 * pair.astype(F32)
    slope = jnp.exp(coef * (first_head + jnp.where(second, 1.0, 0.0)))
    bias = slope * (steps * dilation).astype(F32)
    valid = jnp.logical_and(steps >= 0, steps <= BLK)
    return bias, valid, col >= BLK


def _dil_tile_scores(q2, kk, geometry, has_prev):
    bias, valid, own = geometry
    ok = jnp.logical_and(valid, jnp.logical_or(own, has_prev))
    return jnp.where(ok, _dot_nt(q2, kk) - bias, NEG_BIG)


def _head_col(v, lane_mask):
    return jnp.max(jnp.where(lane_mask, v, NEG_BIG), axis=1, keepdims=True)


def _dil_fwd(qkv, send=None):
    def body(*refs):
        ins, (o_ref, lse_ref), (qs, ks, vs, o_res, lse_res) = refs[:9], refs[9:11], refs[11:16]
        o_grp, lse_grp = refs[16:19], refs[19:22]
        pair = pl.program_id(1)
        lane0 = lax.broadcasted_iota(jnp.int32, (BLK, LANES), 1) < HEAD_DIM
        ks[0:BLK, :] = jnp.zeros((BLK, LANES), BF16)
        vs[0:BLK, :] = jnp.zeros((BLK, LANES), BF16)
        for grp, (_, dilation) in enumerate(DIL_PAIRS):
            q_ref, k_ref, v_ref = ins[3 * grp:3 * grp + 3]
            per_residue = DIL_QBLOCKS // dilation
            _gather_residues(q_ref, qs, 0, dilation, QK_SCALE)
            _gather_residues(k_ref, ks, BLK, dilation)
            _gather_residues(v_ref, vs, BLK, dilation)
            geometry = _dil_geometry(grp, pair)

            def step(blk, _):
                off = pl.multiple_of(blk * BLK, BLK)
                q2 = _stack_heads(qs[pl.ds(off, BLK), :], lane0)
                s = _dil_tile_scores(q2, ks[pl.ds(off, 2 * BLK), :], geometry, blk % per_residue != 0)
                m = jnp.max(s, axis=1, keepdims=True)
                p = jnp.exp(s - m)
                den = jnp.sum(p, axis=1, keepdims=True)
                out = _dot(p.astype(BF16), vs[pl.ds(off, 2 * BLK), :]) / den
                lse = m + jnp.log(den)
                o_res[pl.ds(off, BLK), :] = jnp.where(lane0, out[:BLK], out[BLK:])
                lse_res[pl.ds(off, BLK), :] = jnp.where(lane0, lse[:BLK], lse[BLK:])
                return 0

            lax.fori_loop(0, DIL_QBLOCKS, step, 0, unroll=4)
            _scatter_residues(o_res, 0, o_grp[grp], dilation)
            _scatter_residues(lse_res, 0, lse_grp[grp], dilation)

        for r0 in range(0, SEQ, 2 * BLK):
            rows = slice(r0, r0 + 2 * BLK)
            ls = [lse_grp[g][rows, :] for g in range(DIL_GROUPS)]
            m = jnp.maximum(jnp.maximum(ls[0], ls[1]), ls[2])
            w = [jnp.exp(l - m) for l in ls]
            den = w[0] + w[1] + w[2]
            o_ref[rows, :] = (w[0] * o_grp[0][rows, :] + w[1] * o_grp[1][rows, :] + w[2] * o_grp[2][rows, :]) / den
            lse_ref[rows, :] = m + jnp.log(den)

    def col(part, grp):
        return pl.BlockSpec((None, SEQ, LANES), lambda b, p: (b, 0, 6 * part + 2 * grp + p))

    out_spec = pl.BlockSpec((None, SEQ, LANES), lambda b, p: (b, 0, p))
    out = jax.ShapeDtypeStruct((B_LOC, SEQ, DIL_OUT), F32)
    return _call(
        body, send, name="dil_fwd", grid=(B_LOC, DIL_OUT // LANES),
        in_specs=[col(part, grp) for grp in range(DIL_GROUPS) for part in range(3)],
        out_specs=[out_spec, out_spec], out_shape=[out, out],
        scratch_shapes=[pltpu.VMEM((SEQ, LANES), BF16), pltpu.VMEM((SEQ + BLK, LANES), BF16),
                        pltpu.VMEM((SEQ + BLK, LANES), BF16), pltpu.VMEM((SEQ, LANES), F32),
                        pltpu.VMEM((SEQ, LANES), F32)] + [pltpu.VMEM((SEQ, LANES), F32)] * (2 * DIL_GROUPS),
        semantics=("parallel", "parallel"), operands=[qkv] * 9)


def _dil_bwd(qkv, d_o, lse, dsum, send=None):
    def body(*refs):
        ins, (do_ref, lse_ref, dsum_ref), outs = refs[:9], refs[9:12], refs[12:21]
        qs, ks, vs, dos, lse_res, dsum_res, dq_res, dk_acc, dv_acc = refs[21:]
        pair = pl.program_id(1)
        lane0 = lax.broadcasted_iota(jnp.int32, (BLK, LANES), 1) < HEAD_DIM
        lane1 = jnp.logical_not(lane0)
        ks[0:BLK, :] = jnp.zeros((BLK, LANES), BF16)
        vs[0:BLK, :] = jnp.zeros((BLK, LANES), BF16)
        for grp, (_, dilation) in enumerate(DIL_PAIRS):
            q_ref, k_ref, v_ref = ins[3 * grp:3 * grp + 3]
            dq_ref, dk_ref, dv_ref = outs[3 * grp:3 * grp + 3]
            per_residue = DIL_QBLOCKS // dilation
            _gather_residues(q_ref, qs, 0, dilation, QK_SCALE)
            _gather_residues(k_ref, ks, BLK, dilation)
            _gather_residues(v_ref, vs, BLK, dilation)
            _gather_residues(do_ref, dos, 0, dilation)
            _gather_residues(lse_ref, lse_res, 0, dilation)
            _gather_residues(dsum_ref, dsum_res, 0, dilation)
            dk_acc[...] = jnp.zeros_like(dk_acc)
            dv_acc[...] = jnp.zeros_like(dv_acc)
            geometry = _dil_geometry(grp, pair)

            def step(blk, _):
                off = pl.multiple_of(blk * BLK, BLK)
                q2 = _stack_heads(qs[pl.ds(off, BLK), :], lane0)
                do2 = _stack_heads(dos[pl.ds(off, BLK), :], lane0)
                kk = ks[pl.ds(off, 2 * BLK), :]
                vv = vs[pl.ds(off, 2 * BLK), :]
                lse_blk = lse_res[pl.ds(off, BLK), :]
                dsum_blk = dsum_res[pl.ds(off, BLK), :]
                lse2 = jnp.concatenate([_head_col(lse_blk, lane0), _head_col(lse_blk, lane1)], axis=0)
                dsum2 = jnp.concatenate([_head_col(dsum_blk, lane0), _head_col(dsum_blk, lane1)], axis=0)
                s = _dil_tile_scores(q2, kk, geometry, blk % per_residue != 0)
                p = jnp.exp(s - lse2)
                ds = (p * (_dot_nt(do2, vv) - dsum2)).astype(BF16)
                dq2 = _dot(ds, kk)
                dq_res[pl.ds(off, BLK), :] = jnp.where(lane0, dq2[:BLK], dq2[BLK:]) * QK_SCALE
                dk_acc[pl.ds(off, 2 * BLK), :] += _dot_tn(ds, q2)
                dv_acc[pl.ds(off, 2 * BLK), :] += _dot_tn(p.astype(BF16), do2)
                return 0

            lax.fori_loop(0, DIL_QBLOCKS, step, 0, unroll=4)
            _scatter_residues(dq_res, 0, dq_ref, dilation)
            _scatter_residues(dk_acc, BLK, dk_ref, dilation)
            _scatter_residues(dv_acc, BLK, dv_ref, dilation)

    def col(part, grp):
        return pl.BlockSpec((None, SEQ, LANES), lambda b, p: (b, 0, 6 * part + 2 * grp + p))

    slot = pl.BlockSpec((None, SEQ, LANES), lambda b, p: (b, 0, p))
    out = jax.ShapeDtypeStruct((B_LOC, SEQ, DIL_OUT), F32)
    return _call(
        body, send, name="dil_bwd", grid=(B_LOC, DIL_OUT // LANES),
        in_specs=[col(part, grp) for grp in range(DIL_GROUPS) for part in range(3)] + [slot] * 3,
        out_specs=[slot] * 9, out_shape=[out] * 9,
        scratch_shapes=[pltpu.VMEM((SEQ, LANES), BF16), pltpu.VMEM((SEQ + BLK, LANES), BF16),
                        pltpu.VMEM((SEQ + BLK, LANES), BF16), pltpu.VMEM((SEQ, LANES), BF16),
                        pltpu.VMEM((SEQ, LANES), F32), pltpu.VMEM((SEQ, LANES), F32),
                        pltpu.VMEM((SEQ, LANES), F32), pltpu.VMEM((SEQ + BLK, LANES), F32),
                        pltpu.VMEM((SEQ + BLK, LANES), F32)],
        semantics=("parallel", "parallel"), operands=[qkv] * 9 + [d_o, lse, dsum])


def _peers():
    x, y, c = lax.axis_index("x"), lax.axis_index("y"), lax.axis_index("c")
    me = 4 * x + 2 * y + c
    peers = []
    for mask in range(1, N_DEV):
        px = 1 - x if mask & 4 else x
        py = 1 - y if mask & 2 else y
        pc = 1 - c if mask & 1 else c
        peers.append(((px, py, pc), 4 * px + 2 * py + pc))
    return me, peers


def _all_gather(shard, name):
    def body(src_ref, out_ref, send_sems, recv_sems, local_sem):
        x, y, c = lax.axis_index("x"), lax.axis_index("y"), lax.axis_index("c")
        sibling = (x, y, 1 - c)
        chips = [(1 - x, y), (x, 1 - y), (1 - x, 1 - y)]

        def slot(px, py, pc):
            return out_ref.at[4 * px + 2 * py + pc]

        def copy(k, block, to, src=None):
            return pltpu.make_async_remote_copy(
                src_ref=slot(*block) if src is None else src, dst_ref=slot(*block),
                send_sem=send_sems.at[k], recv_sem=recv_sems.at[k], device_id=to,
                device_id_type=pl.DeviceIdType.MESH)

        mine = pltpu.make_async_copy(src_ref, slot(x, y, c), local_sem)
        mine.start()
        first = [copy(0, (x, y, c), sibling, src=src_ref)]
        first += [copy(1 + j, (x, y, c), (*chip, c), src=src_ref) for j, chip in enumerate(chips)]
        for cp in first:
            cp.start()
        passed = [copy(4 + j, (*chip, c), sibling) for j, chip in enumerate(chips)]
        for j, chip in enumerate(chips):
            copy(1 + j, (*chip, c), (x, y, c)).wait_recv()
            passed[j].start()
        copy(0, sibling, (x, y, c)).wait_recv()
        for j, chip in enumerate(chips):
            copy(4 + j, (*chip, 1 - c), (x, y, c)).wait_recv()
        for cp in first + passed:
            cp.wait_send()
        mine.wait()

    return pl.pallas_call(
        body, name=name,
        in_specs=[pl.BlockSpec(memory_space=pl.ANY)],
        out_specs=pl.BlockSpec(memory_space=pl.ANY),
        out_shape=jax.ShapeDtypeStruct((N_DEV,) + shard.shape, shard.dtype),
        scratch_shapes=[pltpu.SemaphoreType.DMA((N_DEV - 1,)), pltpu.SemaphoreType.DMA((N_DEV - 1,)),
                        pltpu.SemaphoreType.DMA],
    )(shard)


def _call(body, send, *, name, grid, in_specs, out_specs, out_shape, scratch_shapes, semantics, operands):
    if send is None:
        return pl.pallas_call(
            body, name=name, grid=grid, in_specs=in_specs, out_specs=out_specs, out_shape=out_shape,
            scratch_shapes=scratch_shapes, compiler_params=_params(semantics))(*operands), []
    srcs, kinds = [s for s, _ in send], [k for _, k in send]
    n, n_in, n_out, n_scr = len(srcs), len(in_specs), len(out_specs), len(scratch_shapes)
    steps = math.prod(grid)
    relay_step = (3 * steps) // 4

    def plan(refs):
        src_refs, land_refs = refs[n_in:n_in + n], refs[n_in + n + n_out:n_in + 2 * n + n_out]
        send_sems, recv_sems, local_sems = refs[-3:]
        x, y, c = lax.axis_index("x"), lax.axis_index("y"), lax.axis_index("c")
        me, peers = _peers()
        first, relayed_in, relayed_out, arrivals, sends, own = [], [], [], [], [], []
        for a, kind in enumerate(kinds):
            def copy(k, src, dst_slot, to):
                return pltpu.make_async_remote_copy(
                    src_ref=src, dst_ref=land_refs[a].at[dst_slot], send_sem=send_sems.at[a * (N_DEV - 1) + k],
                    recv_sem=recv_sems.at[a * (N_DEV - 1) + k], device_id=to, device_id_type=pl.DeviceIdType.MESH)

            if kind == "gather_by_chip":
                idx = lambda px, py, pc: 4 * px + 2 * py + pc
                chips = [(1 - x, y), (x, 1 - y), (1 - x, 1 - y)]
                mine = [copy(0, src_refs[a], me, (x, y, 1 - c))]
                arrivals.append((copy(0, src_refs[a], idx(x, y, 1 - c), (x, y, 1 - c)), None))
                for j, (px, py) in enumerate(chips):
                    mine.append(copy(1 + j, src_refs[a], me, (px, py, c)))
                    relayed_in.append(copy(1 + j, src_refs[a], idx(px, py, c), (px, py, c)))
                    relayed_out.append(copy(4 + j, land_refs[a].at[idx(px, py, c)], idx(px, py, c), (x, y, 1 - c)))
                    arrivals.append((copy(4 + j, src_refs[a], idx(px, py, 1 - c), (x, y, 1 - c)), None))
                first += [(cp, None) for cp in mine]
                sends += [(cp, None) for cp in mine + relayed_out[-3:]]
                own.append((pltpu.make_async_copy(src_refs[a], land_refs[a].at[me], local_sems.at[a]), None))
            elif kind == "gather":
                for k, (peer, peer_idx) in enumerate(peers):
                    first.append((copy(k, src_refs[a], me, peer), None))
                    arrivals.append((copy(k, src_refs[a], peer_idx, peer), None))
                sends += first[-(N_DEV - 1):]
                own.append((pltpu.make_async_copy(src_refs[a], land_refs[a].at[me], local_sems.at[a]), None))
            else:
                _, lo, hi = kind
                whole = lo == 0 and hi == N_DEV
                to_me = None if whole else jnp.logical_and(me >= lo, me < hi)
                for k, (peer, peer_idx) in enumerate(peers):
                    to_peer = None if whole else jnp.logical_and(peer_idx >= lo, peer_idx < hi)
                    cp = copy(k, src_refs[a].at[peer_idx - lo], me, peer)
                    first.append((cp, to_peer))
                    sends.append((cp, to_peer))
                    arrivals.append((copy(k, src_refs[a].at[0], peer_idx, peer), to_me))
                own.append((pltpu.make_async_copy(src_refs[a].at[me - lo],
                                                  land_refs[a].at[me], local_sems.at[a]), to_me))
        return first, relayed_in, relayed_out, arrivals, sends, own

    def each(pairs, act):
        for cp, cond in pairs:
            if cond is None:
                act(cp)
            else:
                pl.when(cond)(lambda cp=cp: act(cp))

    def wrapped(*refs):
        step = 0
        for axis, size in enumerate(grid):
            step = step * size + pl.program_id(axis)

        @pl.when(step == 0)
        def _():
            first, _, _, _, _, own = plan(refs)
            each(first + own, lambda cp: cp.start())

        if "gather_by_chip" in kinds:
            @pl.when(step == relay_step)
            def _():
                _, relayed_in, relayed_out, _, _, _ = plan(refs)
                for cp_in, cp_out in zip(relayed_in, relayed_out):
                    cp_in.wait_recv()
                    cp_out.start()

        body(*refs[:n_in], *refs[n_in + n:n_in + n + n_out], *refs[n_in + 2 * n + n_out:n_in + 2 * n + n_out + n_scr])

        @pl.when(step == steps - 1)
        def _():
            _, _, _, arrivals, sends, own = plan(refs)
            each(arrivals, lambda cp: cp.wait_recv())
            each(sends, lambda cp: cp.wait_send())
            each(own, lambda cp: cp.wait())

    anywhere = pl.BlockSpec(memory_space=pl.ANY)
    lands = [jax.ShapeDtypeStruct((N_DEV,) + s.shape[-2:], s.dtype) for s in srcs]
    out = pl.pallas_call(
        wrapped, name=name, grid=grid,
        in_specs=list(in_specs) + [anywhere] * n, out_specs=list(out_specs) + [anywhere] * n,
        out_shape=list(out_shape) + lands,
        scratch_shapes=list(scratch_shapes) + [pltpu.SemaphoreType.DMA((n * (N_DEV - 1),)),
                                               pltpu.SemaphoreType.DMA((n * (N_DEV - 1),)),
                                               pltpu.SemaphoreType.DMA((n,))],
        compiler_params=_params(("arbitrary",) * len(grid)),
    )(*operands, *srcs)
    return out[:n_out], list(out[n_out:])


def _sum_in_device_order(land_ref):
    acc = land_ref[0].astype(F32)
    for j in range(1, N_DEV):
        acc = acc + land_ref[j].astype(F32)
    return acc


def _adam_math(w, g, m, v):
    c1 = 1.0 - ADAM_B1 ** ADAM_STEP
    c2 = 1.0 - ADAM_B2 ** ADAM_STEP
    m_new = ADAM_B1 * m + (1.0 - ADAM_B1) * g
    v_new = ADAM_B2 * v + (1.0 - ADAM_B2) * (g * g)
    delta = -ADAM_LR * ((m_new / c1) / (jnp.sqrt(v_new / c2) + ADAM_EPS) + ADAM_WD * w)
    return delta, m_new, v_new


def _row_tile(rows):
    return max(t for t in range(8, 257, 8) if rows % t == 0) if rows % 8 == 0 else rows


def _sum_update(lands, which, w, m, v, name):
    _, rows, cols = lands[0].shape
    tile_rows = _row_tile(rows)
    n = len(lands)

    def body(which_ref, *refs):
        w_ref, m_ref, v_ref, g_ref, d_ref, nm_ref, nv_ref = refs[n:]
        g = _sum_in_device_order(refs[0])
        for alt in range(1, n):
            g = jnp.where(which_ref[0] == alt, _sum_in_device_order(refs[alt]), g)
        g_ref[...] = g
        d_ref[...], nm_ref[...], nv_ref[...] = _adam_math(w_ref[...], g, m_ref[...], v_ref[...])

    tile = pl.BlockSpec((tile_rows, cols), lambda i, which_ref: (i, 0))
    slots = pl.BlockSpec((N_DEV, tile_rows, cols), lambda i, which_ref: (0, i, 0))
    out = jax.ShapeDtypeStruct((rows, cols), F32)
    return pl.pallas_call(
        body, name=name,
        grid_spec=pltpu.PrefetchScalarGridSpec(
            num_scalar_prefetch=1, grid=(rows // tile_rows,),
            in_specs=[slots] * n + [tile, tile, tile], out_specs=[tile] * 4),
        out_shape=[out] * 4,
        compiler_params=_params(("parallel",)),
    )(which, *lands, w, m, v)


def _sum_gains(land):
    def body(land_ref, o_ref):
        o_ref[...] = _sum_in_device_order(land_ref)

    return pl.pallas_call(
        body, name="sum_gain_grads", grid=(1,),
        in_specs=[pl.BlockSpec(land.shape, lambda i: (0, 0, 0))],
        out_specs=pl.BlockSpec(land.shape[1:], lambda i: (0, 0)),
        out_shape=jax.ShapeDtypeStruct(land.shape[1:], F32),
    )(land)


def _adamw(w, g, m, v, name):
    def body(w_ref, g_ref, m_ref, v_ref, d_ref, nm_ref, nv_ref):
        d_ref[...], nm_ref[...], nv_ref[...] = _adam_math(w_ref[...], g_ref[...], m_ref[...], v_ref[...])

    whole = pl.BlockSpec(w.shape, lambda i: (0, 0))
    out = jax.ShapeDtypeStruct(w.shape, F32)
    return pl.pallas_call(
        body, name=name, grid=(1,),
        in_specs=[whole] * 4, out_specs=[whole] * 3, out_shape=[out] * 3,
    )(w, g, m, v)


GROUP_FFN = ("w_ffn_in", "w_ffn_out")
GROUP_MIX = ("w_sb_up", "w_dil_up", "w_out")
W_IN_SPLIT = 3
COL_SHARDED = ("w_in", "w_sb_up", "w_dil_up", "w_ffn_in")


def _full_from_shards(name, slots):
    _, r, c = slots.shape
    if name in COL_SHARDED:
        return slots.transpose(1, 0, 2).reshape(r, N_DEV * c)
    return slots.reshape(N_DEV * r, c)


def _shards_from_full(name, full, n_blocks=N_DEV):
    rows, cols = full.shape
    if name in COL_SHARDED:
        return full.reshape(rows, n_blocks, cols // n_blocks).transpose(1, 0, 2)
    return full.reshape(n_blocks, rows // n_blocks, cols)


def _local_step(x, target, g_mix, g_ffn, g_fin, w_in, shards=None, rest=None):
    gather = lambda names, kind: None if shards is None else [(shards[n], kind) for n in names]
    scatter = lambda blocks, lo, hi: None if shards is None else [(t, ("scatter", lo, hi)) for t in blocks]
    landed = lambda blocks, lands: lands if lands else blocks

    w = {"w_in": w_in}
    if shards is None:
        w.update(rest)
    qkv_sb, qkv_dl, gates, u = _norm_proj(x, g_mix, w["w_in"])
    qkv_sb = qkv_sb.reshape(B_LOC, SEQ, 3 * SB_WIDTH)
    qkv_dl = qkv_dl.reshape(B_LOC, SEQ, 3 * DIL_WIDTH)
    (o_sb,), lands = _sb_fwd(qkv_sb, gather(GROUP_FFN, "gather_by_chip"))
    w.update({n: _full_from_shards(n, t) for n, t in zip(GROUP_FFN, lands)})
    o_sb = o_sb.reshape(TOK, SB_WIDTH)
    (o_dl, lse), lands = _dil_fwd(qkv_dl, gather(GROUP_MIX, "gather"))
    w.update({n: _full_from_shards(n, t) for n, t in zip(GROUP_MIX, lands)})
    o_dl = o_dl.reshape(TOK, DIL_OUT)

    x1, merged = _mix_out(x, o_sb, o_dl, gates, w["w_sb_up"], w["w_dil_up"], w["w_out"])
    loss, dx1, u2, act, dh, dx2, dg_fin, dg_ffn = _ffn_fwd_bwd(x1, target, g_ffn, g_fin, w["w_ffn_in"], w["w_ffn_out"])
    dgates, dy_sb, dy_dl, do_sb, do_dl, dsum = _mix_bwd(dx1, o_sb, o_dl, gates, w["w_sb_up"], w["w_dil_up"], w["w_out"])
    blocks = {
        "w_sb_up": _atb(o_sb, dy_sb, "grad_w_sb_up", SB_WIDTH, D_MODEL, col_blocks=N_DEV),
        "w_dil_up": _atb(o_dl.astype(BF16), dy_dl, "grad_w_dil_up", DIL_OUT, D_MODEL, col_blocks=N_DEV),
        "w_out": _shards_from_full("w_out", _atb(merged, dx1.astype(BF16), "grad_w_out", D_MODEL, D_MODEL)),
        "w_ffn_in": _shards_from_full("w_ffn_in", _atb(u2, dh, "grad_w_ffn_in", D_MODEL, D_FF)),
        "w_ffn_out": _shards_from_full("w_ffn_out", _atb(act, dx2, "grad_w_ffn_out", D_FF // 2, D_MODEL)),
    }
    grads = {}

    with_dil = ("w_ffn_out",) + GROUP_MIX
    dil_blocks = [blocks[n] for n in with_dil]
    as_batch = lambda t: t.reshape(B_LOC, SEQ, DIL_OUT)
    flat = lambda t: t.reshape(TOK, -1)
    d_dl, lands = _dil_bwd(qkv_dl, as_batch(do_dl), lse, as_batch(dsum), scatter(dil_blocks, 0, N_DEV))
    grads.update({n: [t] for n, t in zip(with_dil, landed(dil_blocks, lands))})
    d_dil = [jnp.concatenate([flat(d_dl[3 * grp + part]).astype(BF16) for grp in range(DIL_GROUPS)], axis=1)
             for part in range(3)]
    split = W_IN_SPLIT * (IN_WIDTH // N_DEV) - 3 * SB_WIDTH
    dproj_hi = jnp.concatenate([d_dil[0][:, split:], d_dil[1], d_dil[2], dgates], axis=1)
    hi_blocks = _shards_from_full("w_in", _atb(u, dproj_hi, "grad_w_in_hi", D_MODEL, dproj_hi.shape[1]),
                                  N_DEV - W_IN_SPLIT)

    sb_blocks = [blocks["w_ffn_in"], hi_blocks]
    (dq_sb, dk_sb, dv_sb), lands = _sb_bwd(
        qkv_sb, do_sb.reshape(B_LOC, SEQ, SB_WIDTH),
        None if shards is None else [(sb_blocks[0], ("scatter", 0, N_DEV)), (hi_blocks, ("scatter", W_IN_SPLIT, N_DEV))])
    sb_landed = landed(sb_blocks, lands)
    grads["w_ffn_in"] = [sb_landed[0]]
    dproj_lo = jnp.concatenate([flat(dq_sb), flat(dk_sb), flat(dv_sb), d_dil[0][:, :split]], axis=1)
    lo_blocks = _shards_from_full("w_in", _atb(u, dproj_lo, "grad_w_in_lo", D_MODEL, dproj_lo.shape[1]), W_IN_SPLIT)

    dproj = jnp.concatenate([dproj_lo, dproj_hi], axis=1)
    (grad_x, dg_mix), lands = _proj_bwd(dproj, dx1, x, g_mix, w["w_in"], scatter([lo_blocks], 0, W_IN_SPLIT))
    grads["w_in"] = [landed([lo_blocks], lands)[0], sb_landed[1]]
    gain_grads = jnp.concatenate([dg_mix, dg_ffn, dg_fin], axis=0)
    return loss, grad_x, gain_grads, grads


def kernel(x, norm_mix_g, w_in, w_sb_up, w_dil_up, w_out, norm_ffn_g, w_ffn_in, w_ffn_out, norm_final_g, loss_target, m_norm_mix_g, m_w_in, m_w_sb_up, m_w_dil_up, m_w_out, m_norm_ffn_g, m_w_ffn_in, m_w_ffn_out, m_norm_final_g, v_norm_mix_g, v_w_in, v_w_sb_up, v_w_dil_up, v_w_out, v_norm_ffn_g, v_w_ffn_in, v_w_ffn_out, v_norm_final_g):
    mats = {"w_in": w_in, "w_sb_up": w_sb_up, "w_dil_up": w_dil_up, "w_out": w_out,
            "w_ffn_in": w_ffn_in, "w_ffn_out": w_ffn_out}
    moments_m = {"w_in": m_w_in, "w_sb_up": m_w_sb_up, "w_dil_up": m_w_dil_up, "w_out": m_w_out,
                 "w_ffn_in": m_w_ffn_in, "w_ffn_out": m_w_ffn_out}
    moments_v = {"w_in": v_w_in, "w_sb_up": v_w_sb_up, "w_dil_up": v_w_dil_up, "w_out": v_w_out,
                 "w_ffn_in": v_w_ffn_in, "w_ffn_out": v_w_ffn_out}
    gathered_w_in = _all_gather(w_in[0].astype(BF16), "all_gather_w_in")
    g_fin = norm_final_g.reshape(1, D_MODEL)
    loss, grad_x, gain_grads, grad_slots = _local_step(
        x.reshape(TOK, D_MODEL), loss_target.reshape(TOK, D_MODEL), norm_mix_g, norm_ffn_g, g_fin,
        _full_from_shards("w_in", gathered_w_in),
        shards={name: mats[name][0].astype(BF16) for name in GROUP_FFN + GROUP_MIX})

    gain_rows = jnp.concatenate([gain_grads, jnp.tile(loss, (1, D_MODEL // LANES)),
                                 jnp.zeros((8 - 4, D_MODEL), F32)], axis=0)
    g_gains = _sum_gains(_all_gather(gain_rows, "all_gather_gains"))

    me = 4 * lax.axis_index("x") + 2 * lax.axis_index("y") + lax.axis_index("c")
    out_g, out_d, out_m, out_v = {}, {}, {}, {}
    for name, lands in grad_slots.items():
        which = (me >= W_IN_SPLIT).astype(jnp.int32).reshape(1) if len(lands) > 1 else jnp.zeros((1,), jnp.int32)
        g, d, nm, nv = _sum_update(lands, which, mats[name][0], moments_m[name][0], moments_v[name][0],
                                   "update_" + name)
        out_g[name], out_d[name], out_m[name], out_v[name] = g[None], d[None], nm[None], nv[None]

    gain_w = jnp.concatenate([norm_mix_g, norm_ffn_g, g_fin], axis=0)
    gain_m = jnp.concatenate([m_norm_mix_g, m_norm_ffn_g, m_norm_final_g.reshape(1, D_MODEL)], axis=0)
    gain_v = jnp.concatenate([v_norm_mix_g, v_norm_ffn_g, v_norm_final_g.reshape(1, D_MODEL)], axis=0)
    gd, gm, gv = _adamw(gain_w, g_gains[:3], gain_m, gain_v, "adamw_gains")
    for idx, name in enumerate(("norm_mix_g", "norm_ffn_g", "norm_final_g")):
        shape = (D_MODEL,) if name == "norm_final_g" else (1, D_MODEL)
        out_g[name] = g_gains[idx].reshape(shape)
        out_d[name], out_m[name], out_v[name] = gd[idx].reshape(shape), gm[idx].reshape(shape), gv[idx].reshape(shape)

    order = ("norm_mix_g", "w_in", "w_sb_up", "w_dil_up", "w_out", "norm_ffn_g", "w_ffn_in", "w_ffn_out",
             "norm_final_g")
    return (g_gains[3, 0], grad_x.reshape(B_LOC, SEQ, D_MODEL),
            *[out_g[n] for n in order], *[out_d[n] for n in order],
            *[out_m[n] for n in order], *[out_v[n] for n in order])
```

```python
import math

import jax
import jax.numpy as jnp
from jax import lax
from jax.experimental import pallas as pl
from jax.experimental.pallas import tpu as pltpu

F32 = jnp.float32
BF16 = jnp.bfloat16

N_DEV = 8
D_MODEL = 1024
SEQ = 2048
B_LOC = 2
TOK = B_LOC * SEQ
HEAD_DIM = 64
SB_WIDTH = 512
DIL_WIDTH = 768
DIL_OUT = 256
QKV_WIDTH = 3 * SB_WIDTH + 3 * DIL_WIDTH
IN_WIDTH = QKV_WIDTH + 2 * D_MODEL
D_FF = 2816
DIL_PAIRS = ((128, 1), (512, 4), (2048, 16))
DIL_HEADS = 12
RMS_EPS = 1e-6
ALIBI_MAX_BIAS = 8.0
QK_SCALE = 1.0 / math.sqrt(HEAD_DIM)
BLK = 128
LANES = 128
NEG_BIG = -1e30

ADAM_LR = 0.001
ADAM_B1 = 0.9
ADAM_B2 = 0.999
ADAM_EPS = 1e-08
ADAM_WD = 0.01
ADAM_STEP = 10

VMEM_LIMIT = 56 * 1024 * 1024


def _dot(a, b):
    return jnp.dot(a, b, preferred_element_type=F32)


def _dot_nt(a, b):
    return lax.dot_general(a, b, (((1,), (1,)), ((), ())), preferred_element_type=F32)


def _dot_tn(a, b):
    return lax.dot_general(a, b, (((0,), (0,)), ((), ())), preferred_element_type=F32)


def _softplus(z):
    return jnp.maximum(z, 0.0) + jnp.log1p(jnp.exp(-jnp.abs(z)))


def _sigmoid(z):
    return 1.0 / (1.0 + jnp.exp(-z))


def _split_bf16(v):
    hi = v.astype(BF16)
    lo = (v - hi.astype(F32)).astype(BF16)
    return hi, lo


def _chunks(width, step=512):
    out, c = [], 0
    while c < width:
        w = min(step, width - c)
        out.append((c, w))
        c += w
    return out


def _resident(shape):
    nd = len(shape)
    return pl.BlockSpec(shape, lambda *_: (0,) * nd, pipeline_mode=pl.Buffered(1))


def _params(sem):
    return pltpu.CompilerParams(dimension_semantics=sem, vmem_limit_bytes=VMEM_LIMIT)


def _rms_fwd(x, g):
    r = lax.rsqrt(jnp.mean(x * x, axis=-1, keepdims=True) + RMS_EPS)
    n = x * r
    return n, r, n * g


def _rms_bwd(dy, n, r, g):
    dg = jnp.sum(dy * n, axis=0, keepdims=True)
    dn = dy * g
    dx = r * (dn - n * jnp.mean(dn * n, axis=-1, keepdims=True))
    return dx, dg


TM = 256


def _norm_proj(x, g, w_in):
    def body(x_ref, g_ref, w_ref, sb_ref, dl_ref, gate_ref, u_ref):
        _, _, u = _rms_fwd(x_ref[...], g_ref[...])
        u = u.astype(BF16)
        u_ref[...] = u
        for c0, w in _chunks(3 * SB_WIDTH):
            sb_ref[:, c0:c0 + w] = _dot(u, w_ref[:, c0:c0 + w]).astype(BF16)
        for c0, w in _chunks(3 * DIL_WIDTH):
            dl_ref[:, c0:c0 + w] = _dot(u, w_ref[:, 3 * SB_WIDTH + c0:3 * SB_WIDTH + c0 + w])
        for c0, w in _chunks(2 * D_MODEL):
            gate_ref[:, c0:c0 + w] = _dot(u, w_ref[:, QKV_WIDTH + c0:QKV_WIDTH + c0 + w])

    return pl.pallas_call(
        body, name="norm_proj", grid=(TOK // TM,),
        in_specs=[pl.BlockSpec((TM, D_MODEL), lambda i: (i, 0)), _resident((1, D_MODEL)),
                  _resident((D_MODEL, IN_WIDTH))],
        out_specs=[pl.BlockSpec((TM, 3 * SB_WIDTH), lambda i: (i, 0)),
                   pl.BlockSpec((TM, 3 * DIL_WIDTH), lambda i: (i, 0)),
                   pl.BlockSpec((TM, 2 * D_MODEL), lambda i: (i, 0)),
                   pl.BlockSpec((TM, D_MODEL), lambda i: (i, 0))],
        out_shape=[jax.ShapeDtypeStruct((TOK, 3 * SB_WIDTH), BF16),
                   jax.ShapeDtypeStruct((TOK, 3 * DIL_WIDTH), F32),
                   jax.ShapeDtypeStruct((TOK, 2 * D_MODEL), F32),
                   jax.ShapeDtypeStruct((TOK, D_MODEL), BF16)],
        compiler_params=_params(("parallel",)),
    )(x, g, w_in)


def _mix_out(x, o_sb, o_dl, gates, w_sb_up, w_dil_up, w_out):
    def body(x_ref, osb_ref, odl_ref, gate_ref, wsb_ref, wdl_ref, wout_ref, x1_ref, mg_ref):
        y_sb = _dot(osb_ref[...], wsb_ref[...])
        y_dl = _dot(odl_ref[...].astype(BF16), wdl_ref[...])
        merged = (_sigmoid(gate_ref[:, :D_MODEL]) * y_sb
                  + _sigmoid(gate_ref[:, D_MODEL:]) * y_dl).astype(BF16)
        mg_ref[...] = merged
        x1_ref[...] = x_ref[...] + _dot(merged, wout_ref[...])

    return pl.pallas_call(
        body, name="mix_out", grid=(TOK // TM,),
        in_specs=[pl.BlockSpec((TM, D_MODEL), lambda i: (i, 0)),
                  pl.BlockSpec((TM, SB_WIDTH), lambda i: (i, 0)),
                  pl.BlockSpec((TM, DIL_OUT), lambda i: (i, 0)),
                  pl.BlockSpec((TM, 2 * D_MODEL), lambda i: (i, 0)),
                  _resident((SB_WIDTH, D_MODEL)), _resident((DIL_OUT, D_MODEL)),
                  _resident((D_MODEL, D_MODEL))],
        out_specs=[pl.BlockSpec((TM, D_MODEL), lambda i: (i, 0)),
                   pl.BlockSpec((TM, D_MODEL), lambda i: (i, 0))],
        out_shape=[jax.ShapeDtypeStruct((TOK, D_MODEL), F32),
                   jax.ShapeDtypeStruct((TOK, D_MODEL), BF16)],
        compiler_params=_params(("parallel",)),
    )(x, o_sb, o_dl, gates, w_sb_up, w_dil_up, w_out)


FF_CHUNK = D_FF // 2


def _ffn_fwd_bwd(x1, target, g_ffn, g_fin, w_ffn_in, w_ffn_out):
    def body(x1_ref, t_ref, gffn_ref, gfin_ref, win_ref, wout_ref,
             loss_ref, dx1_ref, u2_ref, act_ref, dh_ref, dx2_ref, dgfin_ref, dgffn_ref, h_scr):
        i = pl.program_id(0)

        @pl.when(i == 0)
        def _():
            loss_ref[...] = jnp.zeros_like(loss_ref)
            dgfin_ref[...] = jnp.zeros_like(dgfin_ref)
            dgffn_ref[...] = jnp.zeros_like(dgffn_ref)

        x1 = x1_ref[...]
        g_ffn_v = gffn_ref[...]
        g_fin_v = gfin_ref[...]
        n2, r2, u2 = _rms_fwd(x1, g_ffn_v)
        u2 = u2.astype(BF16)
        u2_ref[...] = u2
        x2 = x1
        for c0 in range(0, D_FF, FF_CHUNK):
            gate = _dot(u2, win_ref[:, c0:c0 + FF_CHUNK])
            up = _dot(u2, win_ref[:, D_FF + c0:D_FF + c0 + FF_CHUNK])
            h_scr[:, c0:c0 + FF_CHUNK] = gate
            h_scr[:, D_FF + c0:D_FF + c0 + FF_CHUNK] = up
            act = (gate * _sigmoid(gate) * up).astype(BF16)
            act_ref[:, c0:c0 + FF_CHUNK] = act
            x2 = x2 + _dot(act, wout_ref[c0:c0 + FF_CHUNK, :])
        n3, r3, y = _rms_fwd(x2, g_fin_v)
        err = y - t_ref[...]
        sq = jnp.sum(jnp.sum(err * err, axis=1, keepdims=True), axis=0, keepdims=True)
        loss_ref[...] += sq * (0.5 / D_MODEL)
        dx2, dgfin = _rms_bwd(err * (1.0 / D_MODEL), n3, r3, g_fin_v)
        dgfin_ref[...] += dgfin
        dx2_b = dx2.astype(BF16)
        dx2_ref[...] = dx2_b
        du2 = jnp.zeros((TM, D_MODEL), F32)
        for c0 in range(0, D_FF, FF_CHUNK):
            gate = h_scr[:, c0:c0 + FF_CHUNK]
            up = h_scr[:, D_FF + c0:D_FF + c0 + FF_CHUNK]
            dact = _dot_nt(dx2_b, wout_ref[c0:c0 + FF_CHUNK, :])
            sg = _sigmoid(gate)
            dgate = (dact * up * (sg * (1.0 + gate * (1.0 - sg)))).astype(BF16)
            dup = (dact * (gate * sg)).astype(BF16)
            dh_ref[:, c0:c0 + FF_CHUNK] = dgate
            dh_ref[:, D_FF + c0:D_FF + c0 + FF_CHUNK] = dup
            du2 = du2 + _dot_nt(dgate, win_ref[:, c0:c0 + FF_CHUNK])
            du2 = du2 + _dot_nt(dup, win_ref[:, D_FF + c0:D_FF + c0 + FF_CHUNK])
        dx1_n, dgffn = _rms_bwd(du2, n2, r2, g_ffn_v)
        dgffn_ref[...] += dgffn
        dx1_ref[...] = dx2 + dx1_n

    tile = lambda w: pl.BlockSpec((TM, w), lambda i: (i, 0))
    acc = lambda w: pl.BlockSpec((1, w), lambda i: (0, 0))
    return pl.pallas_call(
        body, name="ffn_fwd_bwd", grid=(TOK // TM,),
        in_specs=[tile(D_MODEL), tile(D_MODEL), _resident((1, D_MODEL)), _resident((1, D_MODEL)),
                  _resident((D_MODEL, 2 * D_FF)), _resident((D_FF, D_MODEL))],
        out_specs=[acc(LANES), tile(D_MODEL), tile(D_MODEL), tile(D_FF), tile(2 * D_FF), tile(D_MODEL),
                   acc(D_MODEL), acc(D_MODEL)],
        out_shape=[jax.ShapeDtypeStruct((1, LANES), F32),
                   jax.ShapeDtypeStruct((TOK, D_MODEL), F32),
                   jax.ShapeDtypeStruct((TOK, D_MODEL), BF16),
                   jax.ShapeDtypeStruct((TOK, D_FF), BF16),
                   jax.ShapeDtypeStruct((TOK, 2 * D_FF), BF16),
                   jax.ShapeDtypeStruct((TOK, D_MODEL), BF16),
                   jax.ShapeDtypeStruct((1, D_MODEL), F32),
                   jax.ShapeDtypeStruct((1, D_MODEL), F32)],
        scratch_shapes=[pltpu.VMEM((TM, 2 * D_FF), F32)],
        compiler_params=_params(("arbitrary",)),
    )(x1, target, g_ffn, g_fin, w_ffn_in, w_ffn_out)


def _mix_bwd(dx1, o_sb, o_dl, gates, w_sb_up, w_dil_up, w_out):
    def body(dx1_ref, osb_ref, odl_ref, gate_ref, wsb_ref, wdl_ref, wout_ref,
             dgate_ref, dysb_ref, dydl_ref, dosb_ref, dodl_ref, dsum_ref):
        dmerged = _dot_nt(dx1_ref[...].astype(BF16), wout_ref[...])
        o_dl = odl_ref[...]
        y_sb = _dot(osb_ref[...], wsb_ref[...])
        y_dl = _dot(o_dl.astype(BF16), wdl_ref[...])
        s_sb = _sigmoid(gate_ref[:, :D_MODEL])
        s_dl = _sigmoid(gate_ref[:, D_MODEL:])
        dgate_ref[:, :D_MODEL] = (dmerged * y_sb * (s_sb * (1.0 - s_sb))).astype(BF16)
        dgate_ref[:, D_MODEL:] = (dmerged * y_dl * (s_dl * (1.0 - s_dl))).astype(BF16)
        dy_sb = (dmerged * s_sb).astype(BF16)
        dy_dl = (dmerged * s_dl).astype(BF16)
        dysb_ref[...] = dy_sb
        dydl_ref[...] = dy_dl
        dosb_ref[...] = _dot_nt(dy_sb, wsb_ref[...]).astype(BF16)
        do_dl = _dot_nt(dy_dl, wdl_ref[...])
        dodl_ref[...] = do_dl
        row = lax.broadcasted_iota(jnp.int32, (DIL_OUT, DIL_OUT), 0) // HEAD_DIM
        col = lax.broadcasted_iota(jnp.int32, (DIL_OUT, DIL_OUT), 1) // HEAD_DIM
        same_head = (row == col).astype(BF16)
        hi, lo = _split_bf16(do_dl * o_dl)
        dsum_ref[...] = _dot(hi, same_head) + _dot(lo, same_head)

    tile = lambda w: pl.BlockSpec((TM, w), lambda i: (i, 0))
    return pl.pallas_call(
        body, name="mix_bwd", grid=(TOK // TM,),
        in_specs=[tile(D_MODEL), tile(SB_WIDTH), tile(DIL_OUT), tile(2 * D_MODEL),
                  _resident((SB_WIDTH, D_MODEL)), _resident((DIL_OUT, D_MODEL)),
                  _resident((D_MODEL, D_MODEL))],
        out_specs=[tile(2 * D_MODEL), tile(D_MODEL), tile(D_MODEL), tile(SB_WIDTH), tile(DIL_OUT),
                   tile(DIL_OUT)],
        out_shape=[jax.ShapeDtypeStruct((TOK, 2 * D_MODEL), BF16),
                   jax.ShapeDtypeStruct((TOK, D_MODEL), BF16),
                   jax.ShapeDtypeStruct((TOK, D_MODEL), BF16),
                   jax.ShapeDtypeStruct((TOK, SB_WIDTH), BF16),
                   jax.ShapeDtypeStruct((TOK, DIL_OUT), F32),
                   jax.ShapeDtypeStruct((TOK, DIL_OUT), F32)],
        compiler_params=_params(("parallel",)),
    )(dx1, o_sb, o_dl, gates, w_sb_up, w_dil_up, w_out)


def _proj_bwd(dproj, dx1, x, g, w_in, send=None):
    def body(dp_ref, dx1_ref, x_ref, g_ref, w_ref, dx_ref, dg_ref):
        @pl.when(pl.program_id(0) == 0)
        def _():
            dg_ref[...] = jnp.zeros_like(dg_ref)

        du = jnp.zeros((TM, D_MODEL), F32)
        for c0, w in _chunks(IN_WIDTH, 1024):
            du = du + _dot_nt(dp_ref[:, c0:c0 + w], w_ref[:, c0:c0 + w])
        g_v = g_ref[...]
        n, r, _ = _rms_fwd(x_ref[...], g_v)
        dx, dg = _rms_bwd(du, n, r, g_v)
        dg_ref[...] += dg
        dx_ref[...] = dx1_ref[...] + dx

    tile = lambda w: pl.BlockSpec((TM, w), lambda i: (i, 0))
    return _call(
        body, send, name="proj_bwd", grid=(TOK // TM,),
        in_specs=[tile(IN_WIDTH), tile(D_MODEL), tile(D_MODEL), _resident((1, D_MODEL)),
                  _resident((D_MODEL, IN_WIDTH))],
        out_specs=[tile(D_MODEL), pl.BlockSpec((1, D_MODEL), lambda i: (0, 0))],
        out_shape=[jax.ShapeDtypeStruct((TOK, D_MODEL), F32),
                   jax.ShapeDtypeStruct((1, D_MODEL), F32)],
        scratch_shapes=[], semantics=("arbitrary",), operands=(dproj, dx1, x, g, w_in))


def _atb(a, b, name, tm, tn, col_blocks=0, tk=512):
    m, n = a.shape[1], b.shape[1]
    nk = TOK // tk

    def body(a_ref, b_ref, o_ref, acc_ref):
        k = pl.program_id(2)

        @pl.when(k == 0)
        def _():
            acc_ref[...] = jnp.zeros_like(acc_ref)

        acc_ref[...] += _dot_tn(a_ref[...], b_ref[...])

        @pl.when(k == nk - 1)
        def _():
            if col_blocks:
                width = n // col_blocks
                for blk in range(col_blocks):
                    o_ref[blk] = acc_ref[:, blk * width:(blk + 1) * width].astype(BF16)
            else:
                o_ref[...] = acc_ref[...].astype(BF16)

    if col_blocks:
        out_spec = pl.BlockSpec((col_blocks, tm, n // col_blocks), lambda i, j, k: (0, i, 0))
        out_shape = jax.ShapeDtypeStruct((col_blocks, m, n // col_blocks), BF16)
    else:
        out_spec = pl.BlockSpec((tm, tn), lambda i, j, k: (i, j))
        out_shape = jax.ShapeDtypeStruct((m, n), BF16)
    return pl.pallas_call(
        body, name=name, grid=(m // tm, n // tn, nk),
        in_specs=[pl.BlockSpec((tk, tm), lambda i, j, k: (k, i)),
                  pl.BlockSpec((tk, tn), lambda i, j, k: (k, j))],
        out_specs=out_spec, out_shape=out_shape,
        scratch_shapes=[pltpu.VMEM((tm, tn), F32)],
        compiler_params=_params(("parallel", "parallel", "arbitrary")),
    )(a, b)


SB_PAIRS = SB_WIDTH // LANES


def _two_heads(v, lane0):
    zero = jnp.zeros_like(v)
    return jnp.where(lane0, v, zero), jnp.where(lane0, zero, v)


SB_QBLK = 256
N_SB_STEPS = SEQ // SB_QBLK


SB_KCHUNK = 2 * BLK
SB_ROWS = 2 * SB_QBLK
SB_DEAD = -104.0


def _log_keep(z):
    neg_z = -z
    return jnp.minimum(neg_z, 0.0) - jnp.log(1.0 + jnp.exp(jnp.minimum(z, neg_z)))


def _stack_heads(v, lane0):
    return jnp.concatenate(_two_heads(v, lane0), axis=0)


def _block_sums(v, tri):
    halves = (v[:, :BLK], v[:, BLK:])
    hi, lo = _split_bf16(jnp.concatenate(halves, axis=0))
    prod = _dot(jnp.concatenate([hi, lo], axis=0), tri)
    tri_sum = prod[:2 * SB_ROWS] + prod[2 * SB_ROWS:]
    sums = tuple(jnp.sum(h, axis=1, keepdims=True) for h in halves)
    return (tri_sum[:SB_ROWS], tri_sum[SB_ROWS:]), sums


def _sb_diag_mask():
    row = lax.broadcasted_iota(jnp.int32, (SB_ROWS, SB_KCHUNK), 0)
    col = lax.broadcasted_iota(jnp.int32, (SB_ROWS, SB_KCHUNK), 1)
    return col < jnp.where(row >= SB_QBLK, row - SB_QBLK, row)


def _sb_fwd(qkv, send=None):
    def body(q_ref, k_ref, v_ref, o_ref):
        i = pl.program_id(2)
        krow = lax.broadcasted_iota(jnp.int32, (BLK, BLK), 0)
        kcol = lax.broadcasted_iota(jnp.int32, (BLK, BLK), 1)
        later = (krow > kcol).astype(BF16)
        lane0 = lax.broadcasted_iota(jnp.int32, (SB_QBLK, LANES), 1) < HEAD_DIM
        q2 = _stack_heads(q_ref[0] * QK_SCALE, lane0)

        def chunk(c, carry, causal):
            acc, run = carry
            off = pl.multiple_of(c * SB_KCHUNK, SB_KCHUNK)
            z = _dot_nt(q2, k_ref[0, pl.ds(off, SB_KCHUNK), :])
            log_keep = _log_keep(z)
            if causal is not None:
                log_keep = jnp.where(causal, log_keep, 0.0)
            suffix, sums = _block_sums(log_keep, later)
            log_after = jnp.concatenate([suffix[0] + (run + sums[1]), suffix[1] + run], axis=1)
            a = jnp.exp(log_keep + z + log_after)
            if causal is not None:
                a = jnp.where(causal, a, 0.0)
            acc = acc + _dot(a.astype(BF16), v_ref[0, pl.ds(off, SB_KCHUNK), :])
            return acc, run + (sums[0] + sums[1])

        acc, run = chunk(i, (jnp.zeros((SB_ROWS, LANES), F32), jnp.zeros((SB_ROWS, 1), F32)), _sb_diag_mask())

        def live(state):
            t, _, run = state
            return jnp.logical_and(t < i, jnp.max(run) > SB_DEAD)

        def trip(state):
            t, acc, run = state
            acc, run = chunk(i - 1 - t, (acc, run), None)
            return t + 1, acc, run

        _, acc, _ = lax.while_loop(live, trip, (jnp.int32(0), acc, run))
        o_ref[0] = jnp.where(lane0, acc[:SB_QBLK], acc[SB_QBLK:]).astype(BF16)

    blk = pl.BlockSpec((1, SB_QBLK, LANES), lambda b, h, i: (b, i, h))
    return _call(
        body, send, name="sb_fwd", grid=(B_LOC, SB_PAIRS, N_SB_STEPS),
        in_specs=[blk,
                  pl.BlockSpec((1, SEQ, LANES), lambda b, h, i: (b, 0, SB_PAIRS + h)),
                  pl.BlockSpec((1, SEQ, LANES), lambda b, h, i: (b, 0, 2 * SB_PAIRS + h))],
        out_specs=[blk], out_shape=[jax.ShapeDtypeStruct((B_LOC, SEQ, SB_WIDTH), BF16)],
        scratch_shapes=[], semantics=("parallel", "parallel", "arbitrary"), operands=(qkv, qkv, qkv))


def _sb_bwd(qkv, d_o, send=None):
    def body(q_ref, k_ref, v_ref, do_ref, dq_ref, dk_ref, dv_ref, dk_acc, dv_acc, z_scr, keep_scr):
        i = pl.program_id(2)
        krow = lax.broadcasted_iota(jnp.int32, (BLK, BLK), 0)
        kcol = lax.broadcasted_iota(jnp.int32, (BLK, BLK), 1)
        upto = (krow <= kcol).astype(BF16)
        earlier = (krow < kcol).astype(BF16)
        lane0 = lax.broadcasted_iota(jnp.int32, (SB_QBLK, LANES), 1) < HEAD_DIM
        q2 = _stack_heads(q_ref[0] * QK_SCALE, lane0)
        do2 = _stack_heads(do_ref[0], lane0)

        def keep_sum(c, causal):
            off = pl.multiple_of(c * SB_KCHUNK, SB_KCHUNK)
            z = _dot_nt(q2, k_ref[0, pl.ds(off, SB_KCHUNK), :])
            log_keep = _log_keep(z)
            if causal is not None:
                log_keep = jnp.where(causal, log_keep, 0.0)
            z_scr[c] = z
            keep_scr[c] = log_keep
            return jnp.sum(log_keep, axis=1, keepdims=True)

        def live(state):
            t, run = state
            return jnp.logical_and(t < i, jnp.max(run) > SB_DEAD)

        walked, tot2 = lax.while_loop(live, lambda s: (s[0] + 1, s[1] + keep_sum(i - 1 - s[0], None)),
                                      (jnp.int32(0), keep_sum(i, _sb_diag_mask())))
        first = i - walked

        @pl.when(i == 0)
        def _():
            dk_acc[...] = jnp.zeros_like(dk_acc)
            dv_acc[...] = jnp.zeros_like(dv_acc)

        def chunk(c, carry, causal):
            dq, pre_keep, pre_e = carry
            off = pl.multiple_of(c * SB_KCHUNK, SB_KCHUNK)
            k_c = k_ref[0, pl.ds(off, SB_KCHUNK), :]
            v_c = v_ref[0, pl.ds(off, SB_KCHUNK), :]
            d_a = _dot_nt(do2, v_c)
            log_keep = keep_scr[c]
            log_beta = log_keep + z_scr[c]
            prefix, sums = _block_sums(log_keep, upto)
            inclusive = jnp.concatenate([prefix[0], prefix[1] + sums[0]], axis=1)
            a = jnp.exp(log_beta + ((tot2 - pre_keep) - inclusive))
            if causal is not None:
                a = jnp.where(causal, a, 0.0)
            e = d_a * a
            e_prefix, e_sums = _block_sums(e, earlier)
            before = jnp.concatenate([e_prefix[0] + pre_e, e_prefix[1] + (pre_e + e_sums[0])], axis=1)
            dz = e - (e + before) * jnp.exp(log_beta)
            if causal is not None:
                dz = jnp.where(causal, dz, 0.0)
            dz = dz.astype(BF16)
            dq = dq + _dot(dz, k_c)
            dk_acc[pl.ds(off, SB_KCHUNK), :] += _dot_tn(dz, q2)
            dv_acc[pl.ds(off, SB_KCHUNK), :] += _dot_tn(a.astype(BF16), do2)
            return dq, pre_keep + (sums[0] + sums[1]), pre_e + (e_sums[0] + e_sums[1])

        zero_col = jnp.zeros((SB_ROWS, 1), F32)
        carry = lax.fori_loop(first, i, lambda t, c: chunk(t, c, None),
                              (jnp.zeros((SB_ROWS, LANES), F32), zero_col, zero_col))
        dq, _, _ = chunk(i, carry, _sb_diag_mask())
        dq_ref[0] = (jnp.where(lane0, dq[:SB_QBLK], dq[SB_QBLK:]) * QK_SCALE).astype(BF16)

        @pl.when(i == N_SB_STEPS - 1)
        def _():
            dk_ref[0] = dk_acc[...].astype(BF16)
            dv_ref[0] = dv_acc[...].astype(BF16)

    blk = pl.BlockSpec((1, SB_QBLK, LANES), lambda b, h, i: (b, i, h))
    whole = lambda c: pl.BlockSpec((1, SEQ, LANES), lambda b, h, i: (b, 0, c * SB_PAIRS + h))
    out = jax.ShapeDtypeStruct((B_LOC, SEQ, SB_WIDTH), BF16)
    return _call(
        body, send, name="sb_bwd", grid=(B_LOC, SB_PAIRS, N_SB_STEPS),
        in_specs=[blk, whole(1), whole(2), blk],
        out_specs=[blk, whole(0), whole(0)],
        out_shape=[out, out, out],
        scratch_shapes=[pltpu.VMEM((SEQ, LANES), F32), pltpu.VMEM((SEQ, LANES), F32),
                        pltpu.VMEM((N_SB_STEPS, SB_ROWS, SB_KCHUNK), F32),
                        pltpu.VMEM((N_SB_STEPS, SB_ROWS, SB_KCHUNK), F32)],
        semantics=("parallel", "parallel", "arbitrary"), operands=(qkv, qkv, qkv, d_o))


DIL_GROUPS = len(DIL_PAIRS)
DIL_QBLOCKS = SEQ // BLK


def _residue_rows(j, dilation):
    length = SEQ // dilation
    return pl.ds(j, length, stride=dilation) if dilation > 1 else pl.ds(0, length)


def _gather_residues(src_ref, dst_ref, dst_off, dilation, scale=None):
    length = SEQ // dilation
    for j in range(dilation):
        v = src_ref[_residue_rows(j, dilation), :]
        if scale is not None:
            v = v * scale
        dst_ref[dst_off + j * length:dst_off + (j + 1) * length, :] = v.astype(dst_ref.dtype)


def _scatter_residues(src_ref, src_off, dst_ref, dilation):
    length = SEQ // dilation
    for j in range(dilation):
        dst_ref[_residue_rows(j, dilation), :] = (
            src_ref[src_off + j * length:src_off + (j + 1) * length, :].astype(dst_ref.dtype))


def _dil_geometry(group, pair):
    dilation = DIL_PAIRS[group][1]
    row = lax.broadcasted_iota(jnp.int32, (2 * BLK, 2 * BLK), 0)
    col = lax.broadcasted_iota(jnp.int32, (2 * BLK, 2 * BLK), 1)
    second = row >= BLK
    steps = BLK + jnp.where(second, row - BLK, row) - col
    coef = -ALIBI_MAX_BIAS / DIL_HEADS * math.log(2.0)
    first_head = float(4 * group + 1) + 2.0 * pair.astype(F32)
    slope = jnp.exp(coef * (first_head + jnp.where(second, 1.0, 0.0)))
    bias = slope * (steps * dilation).astype(F32)
    valid = jnp.logical_and(steps >= 0, steps <= BLK)
    return bias, valid, col >= BLK


def _dil_tile_scores(q2, kk, geometry, has_prev):
    bias, valid, own = geometry
    ok = jnp.logical_and(valid, jnp.logical_or(own, has_prev))
    return jnp.where(ok, _dot_nt(q2, kk) - bias, NEG_BIG)


def _head_col(v, lane_mask):
    return jnp.max(jnp.where(lane_mask, v, NEG_BIG), axis=1, keepdims=True)


def _dil_fwd(qkv, send=None):
    def body(*refs):
        ins, (o_ref, lse_ref), (qs, ks, vs, o_res, lse_res) = refs[:9], refs[9:11], refs[11:16]
        o_grp, lse_grp = refs[16:19], refs[19:22]
        pair = pl.program_id(1)
        lane0 = lax.broadcasted_iota(jnp.int32, (BLK, LANES), 1) < HEAD_DIM
        ks[0:BLK, :] = jnp.zeros((BLK, LANES), BF16)
        vs[0:BLK, :] = jnp.zeros((BLK, LANES), BF16)
        for grp, (_, dilation) in enumerate(DIL_PAIRS):
            q_ref, k_ref, v_ref = ins[3 * grp:3 * grp + 3]
            per_residue = DIL_QBLOCKS // dilation
            _gather_residues(q_ref, qs, 0, dilation, QK_SCALE)
            _gather_residues(k_ref, ks, BLK, dilation)
            _gather_residues(v_ref, vs, BLK, dilation)
            geometry = _dil_geometry(grp, pair)

            def step(blk, _):
                off = pl.multiple_of(blk * BLK, BLK)
                q2 = _stack_heads(qs[pl.ds(off, BLK), :], lane0)
                s = _dil_tile_scores(q2, ks[pl.ds(off, 2 * BLK), :], geometry, blk % per_residue != 0)
                m = jnp.max(s, axis=1, keepdims=True)
                p = jnp.exp(s - m)
                den = jnp.sum(p, axis=1, keepdims=True)
                out = _dot(p.astype(BF16), vs[pl.ds(off, 2 * BLK), :]) / den
                lse = m + jnp.log(den)
                o_res[pl.ds(off, BLK), :] = jnp.where(lane0, out[:BLK], out[BLK:])
                lse_res[pl.ds(off, BLK), :] = jnp.where(lane0, lse[:BLK], lse[BLK:])
                return 0

            lax.fori_loop(0, DIL_QBLOCKS, step, 0, unroll=4)
            _scatter_residues(o_res, 0, o_grp[grp], dilation)
            _scatter_residues(lse_res, 0, lse_grp[grp], dilation)

        for r0 in range(0, SEQ, 2 * BLK):
            rows = slice(r0, r0 + 2 * BLK)
            ls = [lse_grp[g][rows, :] for g in range(DIL_GROUPS)]
            m = jnp.maximum(jnp.maximum(ls[0], ls[1]), ls[2])
            w = [jnp.exp(l - m) for l in ls]
            den = w[0] + w[1] + w[2]
            o_ref[rows, :] = (w[0] * o_grp[0][rows, :] + w[1] * o_grp[1][rows, :] + w[2] * o_grp[2][rows, :]) / den
            lse_ref[rows, :] = m + jnp.log(den)

    def col(part, grp):
        return pl.BlockSpec((None, SEQ, LANES), lambda b, p: (b, 0, 6 * part + 2 * grp + p))

    out_spec = pl.BlockSpec((None, SEQ, LANES), lambda b, p: (b, 0, p))
    out = jax.ShapeDtypeStruct((B_LOC, SEQ, DIL_OUT), F32)
    return _call(
        body, send, name="dil_fwd", grid=(B_LOC, DIL_OUT // LANES),
        in_specs=[col(part, grp) for grp in range(DIL_GROUPS) for part in range(3)],
        out_specs=[out_spec, out_spec], out_shape=[out, out],
        scratch_shapes=[pltpu.VMEM((SEQ, LANES), BF16), pltpu.VMEM((SEQ + BLK, LANES), BF16),
                        pltpu.VMEM((SEQ + BLK, LANES), BF16), pltpu.VMEM((SEQ, LANES), F32),
                        pltpu.VMEM((SEQ, LANES), F32)] + [pltpu.VMEM((SEQ, LANES), F32)] * (2 * DIL_GROUPS),
        semantics=("parallel", "parallel"), operands=[qkv] * 9)


def _dil_bwd(qkv, d_o, lse, dsum, send=None):
    def body(*refs):
        ins, (do_ref, lse_ref, dsum_ref), outs = refs[:9], refs[9:12], refs[12:21]
        qs, ks, vs, dos, lse_res, dsum_res, dq_res, dk_acc, dv_acc = refs[21:]
        pair = pl.program_id(1)
        lane0 = lax.broadcasted_iota(jnp.int32, (BLK, LANES), 1) < HEAD_DIM
        lane1 = jnp.logical_not(lane0)
        ks[0:BLK, :] = jnp.zeros((BLK, LANES), BF16)
        vs[0:BLK, :] = jnp.zeros((BLK, LANES), BF16)
        for grp, (_, dilation) in enumerate(DIL_PAIRS):
            q_ref, k_ref, v_ref = ins[3 * grp:3 * grp + 3]
            dq_ref, dk_ref, dv_ref = outs[3 * grp:3 * grp + 3]
            per_residue = DIL_QBLOCKS // dilation
            _gather_residues(q_ref, qs, 0, dilation, QK_SCALE)
            _gather_residues(k_ref, ks, BLK, dilation)
            _gather_residues(v_ref, vs, BLK, dilation)
            _gather_residues(do_ref, dos, 0, dilation)
            _gather_residues(lse_ref, lse_res, 0, dilation)
            _gather_residues(dsum_ref, dsum_res, 0, dilation)
            dk_acc[...] = jnp.zeros_like(dk_acc)
            dv_acc[...] = jnp.zeros_like(dv_acc)
            geometry = _dil_geometry(grp, pair)

            def step(blk, _):
                off = pl.multiple_of(blk * BLK, BLK)
                q2 = _stack_heads(qs[pl.ds(off, BLK), :], lane0)
                do2 = _stack_heads(dos[pl.ds(off, BLK), :], lane0)
                kk = ks[pl.ds(off, 2 * BLK), :]
                vv = vs[pl.ds(off, 2 * BLK), :]
                lse_blk = lse_res[pl.ds(off, BLK), :]
                dsum_blk = dsum_res[pl.ds(off, BLK), :]
                lse2 = jnp.concatenate([_head_col(lse_blk, lane0), _head_col(lse_blk, lane1)], axis=0)
                dsum2 = jnp.concatenate([_head_col(dsum_blk, lane0), _head_col(dsum_blk, lane1)], axis=0)
                s = _dil_tile_scores(q2, kk, geometry, blk % per_residue != 0)
                p = jnp.exp(s - lse2)
                ds = (p * (_dot_nt(do2, vv) - dsum2)).astype(BF16)
                dq2 = _dot(ds, kk)
                dq_res[pl.ds(off, BLK), :] = jnp.where(lane0, dq2[:BLK], dq2[BLK:]) * QK_SCALE
                dk_acc[pl.ds(off, 2 * BLK), :] += _dot_tn(ds, q2)
                dv_acc[pl.ds(off, 2 * BLK), :] += _dot_tn(p.astype(BF16), do2)
                return 0

            lax.fori_loop(0, DIL_QBLOCKS, step, 0, unroll=4)
            _scatter_residues(dq_res, 0, dq_ref, dilation)
            _scatter_residues(dk_acc, BLK, dk_ref, dilation)
            _scatter_residues(dv_acc, BLK, dv_ref, dilation)

    def col(part, grp):
        return pl.BlockSpec((None, SEQ, LANES), lambda b, p: (b, 0, 6 * part + 2 * grp + p))

    slot = pl.BlockSpec((None, SEQ, LANES), lambda b, p: (b, 0, p))
    out = jax.ShapeDtypeStruct((B_LOC, SEQ, DIL_OUT), F32)
    return _call(
        body, send, name="dil_bwd", grid=(B_LOC, DIL_OUT // LANES),
        in_specs=[col(part, grp) for grp in range(DIL_GROUPS) for part in range(3)] + [slot] * 3,
        out_specs=[slot] * 9, out_shape=[out] * 9,
        scratch_shapes=[pltpu.VMEM((SEQ, LANES), BF16), pltpu.VMEM((SEQ + BLK, LANES), BF16),
                        pltpu.VMEM((SEQ + BLK, LANES), BF16), pltpu.VMEM((SEQ, LANES), BF16),
                        pltpu.VMEM((SEQ, LANES), F32), pltpu.VMEM((SEQ, LANES), F32),
                        pltpu.VMEM((SEQ, LANES), F32), pltpu.VMEM((SEQ + BLK, LANES), F32),
                        pltpu.VMEM((SEQ + BLK, LANES), F32)],
        semantics=("parallel", "parallel"), operands=[qkv] * 9 + [d_o, lse, dsum])


def _peers():
    x, y, c = lax.axis_index("x"), lax.axis_index("y"), lax.axis_index("c")
    me = 4 * x + 2 * y + c
    peers = []
    for mask in range(1, N_DEV):
        px = 1 - x if mask & 4 else x
        py = 1 - y if mask & 2 else y
        pc = 1 - c if mask & 1 else c
        peers.append(((px, py, pc), 4 * px + 2 * py + pc))
    return me, peers


def _all_gather(shard, name):
    def body(src_ref, out_ref, send_sems, recv_sems, local_sem):
        x, y, c = lax.axis_index("x"), lax.axis_index("y"), lax.axis_index("c")
        sibling = (x, y, 1 - c)
        chips = [(1 - x, y), (x, 1 - y), (1 - x, 1 - y)]

        def slot(px, py, pc):
            return out_ref.at[4 * px + 2 * py + pc]

        def copy(k, block, to, src=None):
            return pltpu.make_async_remote_copy(
                src_ref=slot(*block) if src is None else src, dst_ref=slot(*block),
                send_sem=send_sems.at[k], recv_sem=recv_sems.at[k], device_id=to,
                device_id_type=pl.DeviceIdType.MESH)

        mine = pltpu.make_async_copy(src_ref, slot(x, y, c), local_sem)
        mine.start()
        first = [copy(0, (x, y, c), sibling, src=src_ref)]
        first += [copy(1 + j, (x, y, c), (*chip, c), src=src_ref) for j, chip in enumerate(chips)]
        for cp in first:
            cp.start()
        passed = [copy(4 + j, (*chip, c), sibling) for j, chip in enumerate(chips)]
        for j, chip in enumerate(chips):
            copy(1 + j, (*chip, c), (x, y, c)).wait_recv()
            passed[j].start()
        copy(0, sibling, (x, y, c)).wait_recv()
        for j, chip in enumerate(chips):
            copy(4 + j, (*chip, 1 - c), (x, y, c)).wait_recv()
        for cp in first + passed:
            cp.wait_send()
        mine.wait()

    return pl.pallas_call(
        body, name=name,
        in_specs=[pl.BlockSpec(memory_space=pl.ANY)],
        out_specs=pl.BlockSpec(memory_space=pl.ANY),
        out_shape=jax.ShapeDtypeStruct((N_DEV,) + shard.shape, shard.dtype),
        scratch_shapes=[pltpu.SemaphoreType.DMA((N_DEV - 1,)), pltpu.SemaphoreType.DMA((N_DEV - 1,)),
                        pltpu.SemaphoreType.DMA],
    )(shard)


def _call(body, send, *, name, grid, in_specs, out_specs, out_shape, scratch_shapes, semantics, operands):
    if send is None:
        return pl.pallas_call(
            body, name=name, grid=grid, in_specs=in_specs, out_specs=out_specs, out_shape=out_shape,
            scratch_shapes=scratch_shapes, compiler_params=_params(semantics))(*operands), []
    srcs, kinds = [s for s, _ in send], [k for _, k in send]
    n, n_in, n_out, n_scr = len(srcs), len(in_specs), len(out_specs), len(scratch_shapes)
    steps = math.prod(grid)
    relay_step = (3 * steps) // 4

    def plan(refs):
        src_refs, land_refs = refs[n_in:n_in + n], refs[n_in + n + n_out:n_in + 2 * n + n_out]
        send_sems, recv_sems, local_sems = refs[-3:]
        x, y, c = lax.axis_index("x"), lax.axis_index("y"), lax.axis_index("c")
        me, peers = _peers()
        first, relayed_in, relayed_out, arrivals, sends, own = [], [], [], [], [], []
        for a, kind in enumerate(kinds):
            def copy(k, src, dst_slot, to):
                return pltpu.make_async_remote_copy(
                    src_ref=src, dst_ref=land_refs[a].at[dst_slot], send_sem=send_sems.at[a * (N_DEV - 1) + k],
                    recv_sem=recv_sems.at[a * (N_DEV - 1) + k], device_id=to, device_id_type=pl.DeviceIdType.MESH)

            if kind == "gather_by_chip":
                idx = lambda px, py, pc: 4 * px + 2 * py + pc
                chips = [(1 - x, y), (x, 1 - y), (1 - x, 1 - y)]
                mine = [copy(0, src_refs[a], me, (x, y, 1 - c))]
                arrivals.append(copy(0, src_refs[a], idx(x, y, 1 - c), (x, y, 1 - c)))
                for j, (px, py) in enumerate(chips):
                    mine.append(copy(1 + j, src_refs[a], me, (px, py, c)))
                    relayed_in.append(copy(1 + j, src_refs[a], idx(px, py, c), (px, py, c)))
                    relayed_out.append(copy(4 + j, land_refs[a].at[idx(px, py, c)], idx(px, py, c), (x, y, 1 - c)))
                    arrivals.append(copy(4 + j, src_refs[a], idx(px, py, 1 - c), (x, y, 1 - c)))
                first += mine
                sends += mine + relayed_out[-3:]
                own.append(pltpu.make_async_copy(src_refs[a], land_refs[a].at[me], local_sems.at[a]))
            else:
                part = (lambda i: src_refs[a].at[i]) if kind == "scatter" else (lambda i: src_refs[a])
                for k, (peer, peer_idx) in enumerate(peers):
                    first.append(copy(k, part(peer_idx), me, peer))
                    arrivals.append(copy(k, part(peer_idx), peer_idx, peer))
                sends += first[-(N_DEV - 1):]
                own.append(pltpu.make_async_copy(part(me), land_refs[a].at[me], local_sems.at[a]))
        return first, relayed_in, relayed_out, arrivals, sends, own

    def wrapped(*refs):
        step = 0
        for axis, size in enumerate(grid):
            step = step * size + pl.program_id(axis)

        @pl.when(step == 0)
        def _():
            first, _, _, _, _, own = plan(refs)
            for cp in first + own:
                cp.start()

        if "gather_by_chip" in kinds:
            @pl.when(step == relay_step)
            def _():
                _, relayed_in, relayed_out, _, _, _ = plan(refs)
                for cp_in, cp_out in zip(relayed_in, relayed_out):
                    cp_in.wait_recv()
                    cp_out.start()

        body(*refs[:n_in], *refs[n_in + n:n_in + n + n_out], *refs[n_in + 2 * n + n_out:n_in + 2 * n + n_out + n_scr])

        @pl.when(step == steps - 1)
        def _():
            _, _, _, arrivals, sends, own = plan(refs)
            for cp in arrivals:
                cp.wait_recv()
            for cp in sends:
                cp.wait_send()
            for cp in own:
                cp.wait()

    anywhere = pl.BlockSpec(memory_space=pl.ANY)
    lands = [jax.ShapeDtypeStruct((N_DEV,) + s.shape[-2:], s.dtype) for s in srcs]
    out = pl.pallas_call(
        wrapped, name=name, grid=grid,
        in_specs=list(in_specs) + [anywhere] * n, out_specs=list(out_specs) + [anywhere] * n,
        out_shape=list(out_shape) + lands,
        scratch_shapes=list(scratch_shapes) + [pltpu.SemaphoreType.DMA((n * (N_DEV - 1),)),
                                               pltpu.SemaphoreType.DMA((n * (N_DEV - 1),)),
                                               pltpu.SemaphoreType.DMA((n,))],
        compiler_params=_params(("arbitrary",) * len(grid)),
    )(*operands, *srcs)
    return out[:n_out], list(out[n_out:])


def _sum_in_device_order(land_ref):
    acc = land_ref[0].astype(F32)
    for j in range(1, N_DEV):
        acc = acc + land_ref[j].astype(F32)
    return acc


def _adam_math(w, g, m, v):
    c1 = 1.0 - ADAM_B1 ** ADAM_STEP
    c2 = 1.0 - ADAM_B2 ** ADAM_STEP
    m_new = ADAM_B1 * m + (1.0 - ADAM_B1) * g
    v_new = ADAM_B2 * v + (1.0 - ADAM_B2) * (g * g)
    delta = -ADAM_LR * ((m_new / c1) / (jnp.sqrt(v_new / c2) + ADAM_EPS) + ADAM_WD * w)
    return delta, m_new, v_new


def _row_tile(rows):
    return max(t for t in range(8, 257, 8) if rows % t == 0) if rows % 8 == 0 else rows


def _sum_update(land, w, m, v, name):
    _, rows, cols = land.shape
    tile_rows = _row_tile(rows)

    def body(land_ref, w_ref, m_ref, v_ref, g_ref, d_ref, nm_ref, nv_ref):
        g = _sum_in_device_order(land_ref)
        g_ref[...] = g
        d_ref[...], nm_ref[...], nv_ref[...] = _adam_math(w_ref[...], g, m_ref[...], v_ref[...])

    tile = pl.BlockSpec((tile_rows, cols), lambda i: (i, 0))
    out = jax.ShapeDtypeStruct((rows, cols), F32)
    return pl.pallas_call(
        body, name=name, grid=(rows // tile_rows,),
        in_specs=[pl.BlockSpec((N_DEV, tile_rows, cols), lambda i: (0, i, 0)), tile, tile, tile],
        out_specs=[tile] * 4, out_shape=[out] * 4,
        compiler_params=_params(("parallel",)),
    )(land, w, m, v)


def _sum_gains(land):
    def body(land_ref, o_ref):
        o_ref[...] = _sum_in_device_order(land_ref)

    return pl.pallas_call(
        body, name="sum_gain_grads", grid=(1,),
        in_specs=[pl.BlockSpec(land.shape, lambda i: (0, 0, 0))],
        out_specs=pl.BlockSpec(land.shape[1:], lambda i: (0, 0)),
        out_shape=jax.ShapeDtypeStruct(land.shape[1:], F32),
    )(land)


def _adamw(w, g, m, v, name):
    def body(w_ref, g_ref, m_ref, v_ref, d_ref, nm_ref, nv_ref):
        d_ref[...], nm_ref[...], nv_ref[...] = _adam_math(w_ref[...], g_ref[...], m_ref[...], v_ref[...])

    whole = pl.BlockSpec(w.shape, lambda i: (0, 0))
    out = jax.ShapeDtypeStruct(w.shape, F32)
    return pl.pallas_call(
        body, name=name, grid=(1,),
        in_specs=[whole] * 4, out_specs=[whole] * 3, out_shape=[out] * 3,
    )(w, g, m, v)


GROUP_FFN = ("w_ffn_in", "w_ffn_out")
GROUP_MIX = ("w_sb_up", "w_dil_up", "w_out")
COL_SHARDED = ("w_in", "w_sb_up", "w_dil_up", "w_ffn_in")


def _full_from_shards(name, slots):
    _, r, c = slots.shape
    if name in COL_SHARDED:
        return slots.transpose(1, 0, 2).reshape(r, N_DEV * c)
    return slots.reshape(N_DEV * r, c)


def _shards_from_full(name, full):
    rows, cols = full.shape
    if name in COL_SHARDED:
        return full.reshape(rows, N_DEV, cols // N_DEV).transpose(1, 0, 2)
    return full.reshape(N_DEV, rows // N_DEV, cols)


def _local_step(x, target, g_mix, g_ffn, g_fin, w_in, shards=None, rest=None):
    gather = lambda names, kind: None if shards is None else [(shards[n], kind) for n in names]
    scatter = lambda blocks: None if shards is None else [(t, "scatter") for t in blocks]
    landed = lambda blocks, lands: lands if lands else blocks

    w = {"w_in": w_in}
    if shards is None:
        w.update(rest)
    qkv_sb, qkv_dl, gates, u = _norm_proj(x, g_mix, w["w_in"])
    qkv_sb = qkv_sb.reshape(B_LOC, SEQ, 3 * SB_WIDTH)
    qkv_dl = qkv_dl.reshape(B_LOC, SEQ, 3 * DIL_WIDTH)
    (o_sb,), lands = _sb_fwd(qkv_sb, gather(GROUP_FFN, "gather_by_chip"))
    w.update({n: _full_from_shards(n, t) for n, t in zip(GROUP_FFN, lands)})
    o_sb = o_sb.reshape(TOK, SB_WIDTH)
    (o_dl, lse), lands = _dil_fwd(qkv_dl, gather(GROUP_MIX, "gather"))
    w.update({n: _full_from_shards(n, t) for n, t in zip(GROUP_MIX, lands)})
    o_dl = o_dl.reshape(TOK, DIL_OUT)

    x1, merged = _mix_out(x, o_sb, o_dl, gates, w["w_sb_up"], w["w_dil_up"], w["w_out"])
    loss, dx1, u2, act, dh, dx2, dg_fin, dg_ffn = _ffn_fwd_bwd(x1, target, g_ffn, g_fin, w["w_ffn_in"], w["w_ffn_out"])
    dgates, dy_sb, dy_dl, do_sb, do_dl, dsum = _mix_bwd(dx1, o_sb, o_dl, gates, w["w_sb_up"], w["w_dil_up"], w["w_out"])
    blocks = {
        "w_sb_up": _atb(o_sb, dy_sb, "grad_w_sb_up", SB_WIDTH, D_MODEL, col_blocks=N_DEV),
        "w_dil_up": _atb(o_dl.astype(BF16), dy_dl, "grad_w_dil_up", DIL_OUT, D_MODEL, col_blocks=N_DEV),
        "w_out": _shards_from_full("w_out", _atb(merged, dx1.astype(BF16), "grad_w_out", D_MODEL, D_MODEL)),
        "w_ffn_in": _shards_from_full("w_ffn_in", _atb(u2, dh, "grad_w_ffn_in", D_MODEL, D_FF)),
        "w_ffn_out": _shards_from_full("w_ffn_out", _atb(act, dx2, "grad_w_ffn_out", D_FF // 2, D_MODEL)),
    }
    grads = {}

    early, late = ("w_ffn_in",), ("w_ffn_out",) + GROUP_MIX
    early_blocks = [blocks[n] for n in early]
    (dq_sb, dk_sb, dv_sb), lands = _sb_bwd(qkv_sb, do_sb.reshape(B_LOC, SEQ, SB_WIDTH), scatter(early_blocks))
    grads.update(zip(early, landed(early_blocks, lands)))
    as_batch = lambda t: t.reshape(B_LOC, SEQ, DIL_OUT)
    late_blocks = [blocks[n] for n in late]
    d_dl, lands = _dil_bwd(qkv_dl, as_batch(do_dl), lse, as_batch(dsum), scatter(late_blocks))
    grads.update(zip(late, landed(late_blocks, lands)))
    flat = lambda t: t.reshape(TOK, -1)
    dproj = jnp.concatenate(
        [flat(dq_sb), flat(dk_sb), flat(dv_sb)]
        + [flat(d_dl[3 * grp + part]).astype(BF16) for part in range(3) for grp in range(DIL_GROUPS)]
        + [dgates], axis=1)

    w_in_blocks = [_shards_from_full("w_in", _atb(u, dproj, "grad_w_in", D_MODEL, IN_WIDTH // 2))]
    (grad_x, dg_mix), lands = _proj_bwd(dproj, dx1, x, g_mix, w["w_in"], scatter(w_in_blocks))
    grads["w_in"] = landed(w_in_blocks, lands)[0]
    gain_grads = jnp.concatenate([dg_mix, dg_ffn, dg_fin], axis=0)
    return loss, grad_x, gain_grads, grads


def kernel(x, norm_mix_g, w_in, w_sb_up, w_dil_up, w_out, norm_ffn_g, w_ffn_in, w_ffn_out, norm_final_g, loss_target, m_norm_mix_g, m_w_in, m_w_sb_up, m_w_dil_up, m_w_out, m_norm_ffn_g, m_w_ffn_in, m_w_ffn_out, m_norm_final_g, v_norm_mix_g, v_w_in, v_w_sb_up, v_w_dil_up, v_w_out, v_norm_ffn_g, v_w_ffn_in, v_w_ffn_out, v_norm_final_g):
    mats = {"w_in": w_in, "w_sb_up": w_sb_up, "w_dil_up": w_dil_up, "w_out": w_out,
            "w_ffn_in": w_ffn_in, "w_ffn_out": w_ffn_out}
    moments_m = {"w_in": m_w_in, "w_sb_up": m_w_sb_up, "w_dil_up": m_w_dil_up, "w_out": m_w_out,
                 "w_ffn_in": m_w_ffn_in, "w_ffn_out": m_w_ffn_out}
    moments_v = {"w_in": v_w_in, "w_sb_up": v_w_sb_up, "w_dil_up": v_w_dil_up, "w_out": v_w_out,
                 "w_ffn_in": v_w_ffn_in, "w_ffn_out": v_w_ffn_out}
    gathered_w_in = _all_gather(w_in[0].astype(BF16), "all_gather_w_in")
    g_fin = norm_final_g.reshape(1, D_MODEL)
    loss, grad_x, gain_grads, grad_slots = _local_step(
        x.reshape(TOK, D_MODEL), loss_target.reshape(TOK, D_MODEL), norm_mix_g, norm_ffn_g, g_fin,
        _full_from_shards("w_in", gathered_w_in),
        shards={name: mats[name][0].astype(BF16) for name in GROUP_FFN + GROUP_MIX})

    gain_rows = jnp.concatenate([gain_grads, jnp.tile(loss, (1, D_MODEL // LANES)),
                                 jnp.zeros((8 - 4, D_MODEL), F32)], axis=0)
    g_gains = _sum_gains(_all_gather(gain_rows, "all_gather_gains"))

    out_g, out_d, out_m, out_v = {}, {}, {}, {}
    for name, slots in grad_slots.items():
        g, d, nm, nv = _sum_update(slots, mats[name][0], moments_m[name][0], moments_v[name][0], "update_" + name)
        out_g[name], out_d[name], out_m[name], out_v[name] = g[None], d[None], nm[None], nv[None]

    gain_w = jnp.concatenate([norm_mix_g, norm_ffn_g, g_fin], axis=0)
    gain_m = jnp.concatenate([m_norm_mix_g, m_norm_ffn_g, m_norm_final_g.reshape(1, D_MODEL)], axis=0)
    gain_v = jnp.concatenate([v_norm_mix_g, v_norm_ffn_g, v_norm_final_g.reshape(1, D_MODEL)], axis=0)
    gd, gm, gv = _adamw(gain_w, g_gains[:3], gain_m, gain_v, "adamw_gains")
    for idx, name in enumerate(("norm_mix_g", "norm_ffn_g", "norm_final_g")):
        shape = (D_MODEL,) if name == "norm_final_g" else (1, D_MODEL)
        out_g[name] = g_gains[idx].reshape(shape)
        out_d[name], out_m[name], out_v[name] = gd[idx].reshape(shape), gm[idx].reshape(shape), gv[idx].reshape(shape)

    order = ("norm_mix_g", "w_in", "w_sb_up", "w_dil_up", "w_out", "norm_ffn_g", "w_ffn_in", "w_ffn_out",
             "norm_final_g")
    return (g_gains[3, 0], grad_x.reshape(B_LOC, SEQ, D_MODEL),
            *[out_g[n] for n in order], *[out_d[n] for n in order],
            *[out_m[n] for n in order], *[out_v[n] for n in order])
```

```python
import math

import jax
import jax.numpy as jnp
from jax import lax
from jax.experimental import pallas as pl
from jax.experimental.pallas import tpu as pltpu

F32 = jnp.float32
BF16 = jnp.bfloat16

N_DEV = 8
D_MODEL = 1024
SEQ = 2048
B_LOC = 2
TOK = B_LOC * SEQ
HEAD_DIM = 64
SB_WIDTH = 512
DIL_WIDTH = 768
DIL_OUT = 256
QKV_WIDTH = 3 * SB_WIDTH + 3 * DIL_WIDTH
IN_WIDTH = QKV_WIDTH + 2 * D_MODEL
D_FF = 2816
DIL_PAIRS = ((128, 1), (512, 4), (2048, 16))
DIL_HEADS = 12
RMS_EPS = 1e-6
ALIBI_MAX_BIAS = 8.0
QK_SCALE = 1.0 / math.sqrt(HEAD_DIM)
BLK = 128
LANES = 128
NEG_BIG = -1e30

ADAM_LR = 0.001
ADAM_B1 = 0.9
ADAM_B2 = 0.999
ADAM_EPS = 1e-08
ADAM_WD = 0.01
ADAM_STEP = 10

VMEM_LIMIT = 56 * 1024 * 1024


def _dot(a, b):
    return jnp.dot(a, b, preferred_element_type=F32)


def _dot_nt(a, b):
    return lax.dot_general(a, b, (((1,), (1,)), ((), ())), preferred_element_type=F32)


def _dot_tn(a, b):
    return lax.dot_general(a, b, (((0,), (0,)), ((), ())), preferred_element_type=F32)


def _softplus(z):
    return jnp.maximum(z, 0.0) + jnp.log1p(jnp.exp(-jnp.abs(z)))


def _sigmoid(z):
    return 1.0 / (1.0 + jnp.exp(-z))


def _split_bf16(v):
    hi = v.astype(BF16)
    lo = (v - hi.astype(F32)).astype(BF16)
    return hi, lo


def _chunks(width, step=512):
    out, c = [], 0
    while c < width:
        w = min(step, width - c)
        out.append((c, w))
        c += w
    return out


def _resident(shape):
    nd = len(shape)
    return pl.BlockSpec(shape, lambda *_: (0,) * nd, pipeline_mode=pl.Buffered(1))


def _params(sem):
    return pltpu.CompilerParams(dimension_semantics=sem, vmem_limit_bytes=VMEM_LIMIT)


def _rms_fwd(x, g):
    r = lax.rsqrt(jnp.mean(x * x, axis=-1, keepdims=True) + RMS_EPS)
    n = x * r
    return n, r, n * g


def _rms_bwd(dy, n, r, g):
    dg = jnp.sum(dy * n, axis=0, keepdims=True)
    dn = dy * g
    dx = r * (dn - n * jnp.mean(dn * n, axis=-1, keepdims=True))
    return dx, dg


TM = 256


def _norm_proj(x, g, w_in):
    def body(x_ref, g_ref, w_ref, sb_ref, dl_ref, gate_ref, u_ref):
        _, _, u = _rms_fwd(x_ref[...], g_ref[...])
        u = u.astype(BF16)
        u_ref[...] = u
        for c0, w in _chunks(3 * SB_WIDTH):
            sb_ref[:, c0:c0 + w] = _dot(u, w_ref[:, c0:c0 + w]).astype(BF16)
        for c0, w in _chunks(3 * DIL_WIDTH):
            dl_ref[:, c0:c0 + w] = _dot(u, w_ref[:, 3 * SB_WIDTH + c0:3 * SB_WIDTH + c0 + w])
        for c0, w in _chunks(2 * D_MODEL):
            gate_ref[:, c0:c0 + w] = _dot(u, w_ref[:, QKV_WIDTH + c0:QKV_WIDTH + c0 + w])

    return pl.pallas_call(
        body, name="norm_proj", grid=(TOK // TM,),
        in_specs=[pl.BlockSpec((TM, D_MODEL), lambda i: (i, 0)), _resident((1, D_MODEL)),
                  _resident((D_MODEL, IN_WIDTH))],
        out_specs=[pl.BlockSpec((TM, 3 * SB_WIDTH), lambda i: (i, 0)),
                   pl.BlockSpec((TM, 3 * DIL_WIDTH), lambda i: (i, 0)),
                   pl.BlockSpec((TM, 2 * D_MODEL), lambda i: (i, 0)),
                   pl.BlockSpec((TM, D_MODEL), lambda i: (i, 0))],
        out_shape=[jax.ShapeDtypeStruct((TOK, 3 * SB_WIDTH), BF16),
                   jax.ShapeDtypeStruct((TOK, 3 * DIL_WIDTH), F32),
                   jax.ShapeDtypeStruct((TOK, 2 * D_MODEL), F32),
                   jax.ShapeDtypeStruct((TOK, D_MODEL), BF16)],
        compiler_params=_params(("parallel",)),
    )(x, g, w_in)


def _mix_out(x, o_sb, o_dl, gates, w_sb_up, w_dil_up, w_out):
    def body(x_ref, osb_ref, odl_ref, gate_ref, wsb_ref, wdl_ref, wout_ref, x1_ref, mg_ref):
        y_sb = _dot(osb_ref[...], wsb_ref[...])
        y_dl = _dot(odl_ref[...].astype(BF16), wdl_ref[...])
        merged = (_sigmoid(gate_ref[:, :D_MODEL]) * y_sb
                  + _sigmoid(gate_ref[:, D_MODEL:]) * y_dl).astype(BF16)
        mg_ref[...] = merged
        x1_ref[...] = x_ref[...] + _dot(merged, wout_ref[...])

    return pl.pallas_call(
        body, name="mix_out", grid=(TOK // TM,),
        in_specs=[pl.BlockSpec((TM, D_MODEL), lambda i: (i, 0)),
                  pl.BlockSpec((TM, SB_WIDTH), lambda i: (i, 0)),
                  pl.BlockSpec((TM, DIL_OUT), lambda i: (i, 0)),
                  pl.BlockSpec((TM, 2 * D_MODEL), lambda i: (i, 0)),
                  _resident((SB_WIDTH, D_MODEL)), _resident((DIL_OUT, D_MODEL)),
                  _resident((D_MODEL, D_MODEL))],
        out_specs=[pl.BlockSpec((TM, D_MODEL), lambda i: (i, 0)),
                   pl.BlockSpec((TM, D_MODEL), lambda i: (i, 0))],
        out_shape=[jax.ShapeDtypeStruct((TOK, D_MODEL), F32),
                   jax.ShapeDtypeStruct((TOK, D_MODEL), BF16)],
        compiler_params=_params(("parallel",)),
    )(x, o_sb, o_dl, gates, w_sb_up, w_dil_up, w_out)


FF_CHUNK = D_FF // 2


def _ffn_fwd_bwd(x1, target, g_ffn, g_fin, w_ffn_in, w_ffn_out):
    def body(x1_ref, t_ref, gffn_ref, gfin_ref, win_ref, wout_ref,
             loss_ref, dx1_ref, u2_ref, act_ref, dh_ref, dx2_ref, dgfin_ref, dgffn_ref, h_scr):
        i = pl.program_id(0)

        @pl.when(i == 0)
        def _():
            loss_ref[...] = jnp.zeros_like(loss_ref)
            dgfin_ref[...] = jnp.zeros_like(dgfin_ref)
            dgffn_ref[...] = jnp.zeros_like(dgffn_ref)

        x1 = x1_ref[...]
        g_ffn_v = gffn_ref[...]
        g_fin_v = gfin_ref[...]
        n2, r2, u2 = _rms_fwd(x1, g_ffn_v)
        u2 = u2.astype(BF16)
        u2_ref[...] = u2
        x2 = x1
        for c0 in range(0, D_FF, FF_CHUNK):
            gate = _dot(u2, win_ref[:, c0:c0 + FF_CHUNK])
            up = _dot(u2, win_ref[:, D_FF + c0:D_FF + c0 + FF_CHUNK])
            h_scr[:, c0:c0 + FF_CHUNK] = gate
            h_scr[:, D_FF + c0:D_FF + c0 + FF_CHUNK] = up
            act = (gate * _sigmoid(gate) * up).astype(BF16)
            act_ref[:, c0:c0 + FF_CHUNK] = act
            x2 = x2 + _dot(act, wout_ref[c0:c0 + FF_CHUNK, :])
        n3, r3, y = _rms_fwd(x2, g_fin_v)
        err = y - t_ref[...]
        sq = jnp.sum(jnp.sum(err * err, axis=1, keepdims=True), axis=0, keepdims=True)
        loss_ref[...] += sq * (0.5 / D_MODEL)
        dx2, dgfin = _rms_bwd(err * (1.0 / D_MODEL), n3, r3, g_fin_v)
        dgfin_ref[...] += dgfin
        dx2_b = dx2.astype(BF16)
        dx2_ref[...] = dx2_b
        du2 = jnp.zeros((TM, D_MODEL), F32)
        for c0 in range(0, D_FF, FF_CHUNK):
            gate = h_scr[:, c0:c0 + FF_CHUNK]
            up = h_scr[:, D_FF + c0:D_FF + c0 + FF_CHUNK]
            dact = _dot_nt(dx2_b, wout_ref[c0:c0 + FF_CHUNK, :])
            sg = _sigmoid(gate)
            dgate = (dact * up * (sg * (1.0 + gate * (1.0 - sg)))).astype(BF16)
            dup = (dact * (gate * sg)).astype(BF16)
            dh_ref[:, c0:c0 + FF_CHUNK] = dgate
            dh_ref[:, D_FF + c0:D_FF + c0 + FF_CHUNK] = dup
            du2 = du2 + _dot_nt(dgate, win_ref[:, c0:c0 + FF_CHUNK])
            du2 = du2 + _dot_nt(dup, win_ref[:, D_FF + c0:D_FF + c0 + FF_CHUNK])
        dx1_n, dgffn = _rms_bwd(du2, n2, r2, g_ffn_v)
        dgffn_ref[...] += dgffn
        dx1_ref[...] = dx2 + dx1_n

    tile = lambda w: pl.BlockSpec((TM, w), lambda i: (i, 0))
    acc = lambda w: pl.BlockSpec((1, w), lambda i: (0, 0))
    return pl.pallas_call(
        body, name="ffn_fwd_bwd", grid=(TOK // TM,),
        in_specs=[tile(D_MODEL), tile(D_MODEL), _resident((1, D_MODEL)), _resident((1, D_MODEL)),
                  _resident((D_MODEL, 2 * D_FF)), _resident((D_FF, D_MODEL))],
        out_specs=[acc(LANES), tile(D_MODEL), tile(D_MODEL), tile(D_FF), tile(2 * D_FF), tile(D_MODEL),
                   acc(D_MODEL), acc(D_MODEL)],
        out_shape=[jax.ShapeDtypeStruct((1, LANES), F32),
                   jax.ShapeDtypeStruct((TOK, D_MODEL), F32),
                   jax.ShapeDtypeStruct((TOK, D_MODEL), BF16),
                   jax.ShapeDtypeStruct((TOK, D_FF), BF16),
                   jax.ShapeDtypeStruct((TOK, 2 * D_FF), BF16),
                   jax.ShapeDtypeStruct((TOK, D_MODEL), BF16),
                   jax.ShapeDtypeStruct((1, D_MODEL), F32),
                   jax.ShapeDtypeStruct((1, D_MODEL), F32)],
        scratch_shapes=[pltpu.VMEM((TM, 2 * D_FF), F32)],
        compiler_params=_params(("arbitrary",)),
    )(x1, target, g_ffn, g_fin, w_ffn_in, w_ffn_out)


def _mix_bwd(dx1, o_sb, o_dl, gates, w_sb_up, w_dil_up, w_out):
    def body(dx1_ref, osb_ref, odl_ref, gate_ref, wsb_ref, wdl_ref, wout_ref,
             dgate_ref, dysb_ref, dydl_ref, dosb_ref, dodl_ref, dsum_ref):
        dmerged = _dot_nt(dx1_ref[...].astype(BF16), wout_ref[...])
        o_dl = odl_ref[...]
        y_sb = _dot(osb_ref[...], wsb_ref[...])
        y_dl = _dot(o_dl.astype(BF16), wdl_ref[...])
        s_sb = _sigmoid(gate_ref[:, :D_MODEL])
        s_dl = _sigmoid(gate_ref[:, D_MODEL:])
        dgate_ref[:, :D_MODEL] = (dmerged * y_sb * (s_sb * (1.0 - s_sb))).astype(BF16)
        dgate_ref[:, D_MODEL:] = (dmerged * y_dl * (s_dl * (1.0 - s_dl))).astype(BF16)
        dy_sb = (dmerged * s_sb).astype(BF16)
        dy_dl = (dmerged * s_dl).astype(BF16)
        dysb_ref[...] = dy_sb
        dydl_ref[...] = dy_dl
        dosb_ref[...] = _dot_nt(dy_sb, wsb_ref[...]).astype(BF16)
        do_dl = _dot_nt(dy_dl, wdl_ref[...])
        dodl_ref[...] = do_dl
        row = lax.broadcasted_iota(jnp.int32, (DIL_OUT, DIL_OUT), 0) // HEAD_DIM
        col = lax.broadcasted_iota(jnp.int32, (DIL_OUT, DIL_OUT), 1) // HEAD_DIM
        same_head = (row == col).astype(BF16)
        hi, lo = _split_bf16(do_dl * o_dl)
        dsum_ref[...] = _dot(hi, same_head) + _dot(lo, same_head)

    tile = lambda w: pl.BlockSpec((TM, w), lambda i: (i, 0))
    return pl.pallas_call(
        body, name="mix_bwd", grid=(TOK // TM,),
        in_specs=[tile(D_MODEL), tile(SB_WIDTH), tile(DIL_OUT), tile(2 * D_MODEL),
                  _resident((SB_WIDTH, D_MODEL)), _resident((DIL_OUT, D_MODEL)),
                  _resident((D_MODEL, D_MODEL))],
        out_specs=[tile(2 * D_MODEL), tile(D_MODEL), tile(D_MODEL), tile(SB_WIDTH), tile(DIL_OUT),
                   tile(DIL_OUT)],
        out_shape=[jax.ShapeDtypeStruct((TOK, 2 * D_MODEL), BF16),
                   jax.ShapeDtypeStruct((TOK, D_MODEL), BF16),
                   jax.ShapeDtypeStruct((TOK, D_MODEL), BF16),
                   jax.ShapeDtypeStruct((TOK, SB_WIDTH), BF16),
                   jax.ShapeDtypeStruct((TOK, DIL_OUT), F32),
                   jax.ShapeDtypeStruct((TOK, DIL_OUT), F32)],
        compiler_params=_params(("parallel",)),
    )(dx1, o_sb, o_dl, gates, w_sb_up, w_dil_up, w_out)


def _proj_bwd(dproj, dx1, x, g, w_in, send=None):
    widths = [p.shape[1] for p in dproj]

    def body(*refs):
        dx1_ref, x_ref, g_ref, w_ref, dx_ref, dg_ref = refs[len(widths):]

        @pl.when(pl.program_id(0) == 0)
        def _():
            dg_ref[...] = jnp.zeros_like(dg_ref)

        du = jnp.zeros((TM, D_MODEL), F32)
        c0 = 0
        for dp_ref, w in zip(refs, widths):
            du = du + _dot_nt(dp_ref[...].astype(BF16), w_ref[:, c0:c0 + w])
            c0 += w
        g_v = g_ref[...]
        n, r, _ = _rms_fwd(x_ref[...], g_v)
        dx, dg = _rms_bwd(du, n, r, g_v)
        dg_ref[...] += dg
        dx_ref[...] = dx1_ref[...] + dx

    tile = lambda w: pl.BlockSpec((TM, w), lambda i: (i, 0))
    return _call(
        body, send, name="proj_bwd", grid=(TOK // TM,),
        in_specs=[tile(w) for w in widths] + [tile(D_MODEL), tile(D_MODEL), _resident((1, D_MODEL)),
                                              _resident((D_MODEL, IN_WIDTH))],
        out_specs=[tile(D_MODEL), pl.BlockSpec((1, D_MODEL), lambda i: (0, 0))],
        out_shape=[jax.ShapeDtypeStruct((TOK, D_MODEL), F32),
                   jax.ShapeDtypeStruct((1, D_MODEL), F32)],
        scratch_shapes=[], semantics=("arbitrary",), operands=(*dproj, dx1, x, g, w_in))


def _atb_pieces(a, pieces, name, tm, tk=512):
    m = a.shape[1]
    widths = [p.shape[1] for p in pieces]
    n = sum(widths)
    nk = TOK // tk

    def body(a_ref, *refs):
        o_ref, acc_ref = refs[len(widths):]
        k = pl.program_id(1)

        @pl.when(k == 0)
        def _():
            acc_ref[...] = jnp.zeros_like(acc_ref)

        a_v = a_ref[...]
        c0 = 0
        for p_ref, w in zip(refs, widths):
            acc_ref[:, c0:c0 + w] += _dot_tn(a_v, p_ref[...].astype(BF16))
            c0 += w

        @pl.when(k == nk - 1)
        def _():
            o_ref[...] = acc_ref[...].astype(BF16)

    return pl.pallas_call(
        body, name=name, grid=(m // tm, nk),
        in_specs=[pl.BlockSpec((tk, tm), lambda i, k: (k, i))]
                 + [pl.BlockSpec((tk, w), lambda i, k: (k, 0)) for w in widths],
        out_specs=pl.BlockSpec((tm, n), lambda i, k: (i, 0)),
        out_shape=jax.ShapeDtypeStruct((m, n), BF16),
        scratch_shapes=[pltpu.VMEM((tm, n), F32)],
        compiler_params=_params(("parallel", "arbitrary")),
    )(a, *pieces)


def _atb(a, b, name, tm, tn, col_blocks=0, tk=512):
    m, n = a.shape[1], b.shape[1]
    nk = TOK // tk

    def body(a_ref, b_ref, o_ref, acc_ref):
        k = pl.program_id(2)

        @pl.when(k == 0)
        def _():
            acc_ref[...] = jnp.zeros_like(acc_ref)

        acc_ref[...] += _dot_tn(a_ref[...], b_ref[...])

        @pl.when(k == nk - 1)
        def _():
            if col_blocks:
                width = n // col_blocks
                for blk in range(col_blocks):
                    o_ref[blk] = acc_ref[:, blk * width:(blk + 1) * width].astype(BF16)
            else:
                o_ref[...] = acc_ref[...].astype(BF16)

    if col_blocks:
        out_spec = pl.BlockSpec((col_blocks, tm, n // col_blocks), lambda i, j, k: (0, i, 0))
        out_shape = jax.ShapeDtypeStruct((col_blocks, m, n // col_blocks), BF16)
    else:
        out_spec = pl.BlockSpec((tm, tn), lambda i, j, k: (i, j))
        out_shape = jax.ShapeDtypeStruct((m, n), BF16)
    return pl.pallas_call(
        body, name=name, grid=(m // tm, n // tn, nk),
        in_specs=[pl.BlockSpec((tk, tm), lambda i, j, k: (k, i)),
                  pl.BlockSpec((tk, tn), lambda i, j, k: (k, j))],
        out_specs=out_spec, out_shape=out_shape,
        scratch_shapes=[pltpu.VMEM((tm, tn), F32)],
        compiler_params=_params(("parallel", "parallel", "arbitrary")),
    )(a, b)


SB_PAIRS = SB_WIDTH // LANES


def _two_heads(v, lane0):
    zero = jnp.zeros_like(v)
    return jnp.where(lane0, v, zero), jnp.where(lane0, zero, v)


SB_QBLK = 256
N_SB_STEPS = SEQ // SB_QBLK


SB_KCHUNK = 2 * BLK
SB_ROWS = 2 * SB_QBLK
SB_DEAD = -104.0


def _log_keep(z):
    neg_z = -z
    return jnp.minimum(neg_z, 0.0) - jnp.log(1.0 + jnp.exp(jnp.minimum(z, neg_z)))


def _stack_heads(v, lane0):
    return jnp.concatenate(_two_heads(v, lane0), axis=0)


def _block_sums(v, tri):
    halves = (v[:, :BLK], v[:, BLK:])
    hi, lo = _split_bf16(jnp.concatenate(halves, axis=0))
    prod = _dot(jnp.concatenate([hi, lo], axis=0), tri)
    tri_sum = prod[:2 * SB_ROWS] + prod[2 * SB_ROWS:]
    sums = tuple(jnp.sum(h, axis=1, keepdims=True) for h in halves)
    return (tri_sum[:SB_ROWS], tri_sum[SB_ROWS:]), sums


def _sb_diag_mask():
    row = lax.broadcasted_iota(jnp.int32, (SB_ROWS, SB_KCHUNK), 0)
    col = lax.broadcasted_iota(jnp.int32, (SB_ROWS, SB_KCHUNK), 1)
    return col < jnp.where(row >= SB_QBLK, row - SB_QBLK, row)


def _sb_fwd(qkv, send=None):
    def body(q_ref, k_ref, v_ref, o_ref):
        i = pl.program_id(2)
        krow = lax.broadcasted_iota(jnp.int32, (BLK, BLK), 0)
        kcol = lax.broadcasted_iota(jnp.int32, (BLK, BLK), 1)
        later = (krow > kcol).astype(BF16)
        lane0 = lax.broadcasted_iota(jnp.int32, (SB_QBLK, LANES), 1) < HEAD_DIM
        q2 = _stack_heads(q_ref[0] * QK_SCALE, lane0)

        def chunk(c, carry, causal):
            acc, run = carry
            off = pl.multiple_of(c * SB_KCHUNK, SB_KCHUNK)
            z = _dot_nt(q2, k_ref[0, pl.ds(off, SB_KCHUNK), :])
            log_keep = _log_keep(z)
            if causal is not None:
                log_keep = jnp.where(causal, log_keep, 0.0)
            suffix, sums = _block_sums(log_keep, later)
            log_after = jnp.concatenate([suffix[0] + (run + sums[1]), suffix[1] + run], axis=1)
            a = jnp.exp(log_keep + z + log_after)
            if causal is not None:
                a = jnp.where(causal, a, 0.0)
            acc = acc + _dot(a.astype(BF16), v_ref[0, pl.ds(off, SB_KCHUNK), :])
            return acc, run + (sums[0] + sums[1])

        acc, run = chunk(i, (jnp.zeros((SB_ROWS, LANES), F32), jnp.zeros((SB_ROWS, 1), F32)), _sb_diag_mask())

        def live(state):
            t, _, run = state
            return jnp.logical_and(t < i, jnp.max(run) > SB_DEAD)

        def trip(state):
            t, acc, run = state
            acc, run = chunk(i - 1 - t, (acc, run), None)
            return t + 1, acc, run

        _, acc, _ = lax.while_loop(live, trip, (jnp.int32(0), acc, run))
        o_ref[0] = jnp.where(lane0, acc[:SB_QBLK], acc[SB_QBLK:]).astype(BF16)

    blk = pl.BlockSpec((1, SB_QBLK, LANES), lambda b, h, i: (b, i, h))
    return _call(
        body, send, name="sb_fwd", grid=(B_LOC, SB_PAIRS, N_SB_STEPS),
        in_specs=[blk,
                  pl.BlockSpec((1, SEQ, LANES), lambda b, h, i: (b, 0, SB_PAIRS + h)),
                  pl.BlockSpec((1, SEQ, LANES), lambda b, h, i: (b, 0, 2 * SB_PAIRS + h))],
        out_specs=[blk], out_shape=[jax.ShapeDtypeStruct((B_LOC, SEQ, SB_WIDTH), BF16)],
        scratch_shapes=[], semantics=("parallel", "parallel", "arbitrary"), operands=(qkv, qkv, qkv))


def _sb_bwd(qkv, d_o, send=None):
    def body(q_ref, k_ref, v_ref, do_ref, dq_ref, dk_ref, dv_ref, dk_acc, dv_acc, z_scr, keep_scr):
        i = pl.program_id(2)
        krow = lax.broadcasted_iota(jnp.int32, (BLK, BLK), 0)
        kcol = lax.broadcasted_iota(jnp.int32, (BLK, BLK), 1)
        upto = (krow <= kcol).astype(BF16)
        earlier = (krow < kcol).astype(BF16)
        lane0 = lax.broadcasted_iota(jnp.int32, (SB_QBLK, LANES), 1) < HEAD_DIM
        q2 = _stack_heads(q_ref[0] * QK_SCALE, lane0)
        do2 = _stack_heads(do_ref[0], lane0)

        def keep_sum(c, causal):
            off = pl.multiple_of(c * SB_KCHUNK, SB_KCHUNK)
            z = _dot_nt(q2, k_ref[0, pl.ds(off, SB_KCHUNK), :])
            log_keep = _log_keep(z)
            if causal is not None:
                log_keep = jnp.where(causal, log_keep, 0.0)
            z_scr[c] = z
            keep_scr[c] = log_keep
            return jnp.sum(log_keep, axis=1, keepdims=True)

        def live(state):
            t, run = state
            return jnp.logical_and(t < i, jnp.max(run) > SB_DEAD)

        walked, tot2 = lax.while_loop(live, lambda s: (s[0] + 1, s[1] + keep_sum(i - 1 - s[0], None)),
                                      (jnp.int32(0), keep_sum(i, _sb_diag_mask())))
        first = i - walked

        @pl.when(i == 0)
        def _():
            dk_acc[...] = jnp.zeros_like(dk_acc)
            dv_acc[...] = jnp.zeros_like(dv_acc)

        def chunk(c, carry, causal):
            dq, pre_keep, pre_e = carry
            off = pl.multiple_of(c * SB_KCHUNK, SB_KCHUNK)
            k_c = k_ref[0, pl.ds(off, SB_KCHUNK), :]
            v_c = v_ref[0, pl.ds(off, SB_KCHUNK), :]
            d_a = _dot_nt(do2, v_c)
            log_keep = keep_scr[c]
            log_beta = log_keep + z_scr[c]
            prefix, sums = _block_sums(log_keep, upto)
            inclusive = jnp.concatenate([prefix[0], prefix[1] + sums[0]], axis=1)
            a = jnp.exp(log_beta + ((tot2 - pre_keep) - inclusive))
            if causal is not None:
                a = jnp.where(causal, a, 0.0)
            e = d_a * a
            e_prefix, e_sums = _block_sums(e, earlier)
            before = jnp.concatenate([e_prefix[0] + pre_e, e_prefix[1] + (pre_e + e_sums[0])], axis=1)
            dz = e - (e + before) * jnp.exp(log_beta)
            if causal is not None:
                dz = jnp.where(causal, dz, 0.0)
            dz = dz.astype(BF16)
            dq = dq + _dot(dz, k_c)
            dk_acc[pl.ds(off, SB_KCHUNK), :] += _dot_tn(dz, q2)
            dv_acc[pl.ds(off, SB_KCHUNK), :] += _dot_tn(a.astype(BF16), do2)
            return dq, pre_keep + (sums[0] + sums[1]), pre_e + (e_sums[0] + e_sums[1])

        zero_col = jnp.zeros((SB_ROWS, 1), F32)
        carry = lax.fori_loop(first, i, lambda t, c: chunk(t, c, None),
                              (jnp.zeros((SB_ROWS, LANES), F32), zero_col, zero_col))
        dq, _, _ = chunk(i, carry, _sb_diag_mask())
        dq_ref[0] = (jnp.where(lane0, dq[:SB_QBLK], dq[SB_QBLK:]) * QK_SCALE).astype(BF16)

        @pl.when(i == N_SB_STEPS - 1)
        def _():
            dk_ref[0] = dk_acc[...].astype(BF16)
            dv_ref[0] = dv_acc[...].astype(BF16)

    blk = pl.BlockSpec((1, SB_QBLK, LANES), lambda b, h, i: (b, i, h))
    whole = lambda c: pl.BlockSpec((1, SEQ, LANES), lambda b, h, i: (b, 0, c * SB_PAIRS + h))
    out = jax.ShapeDtypeStruct((B_LOC, SEQ, SB_WIDTH), BF16)
    return _call(
        body, send, name="sb_bwd", grid=(B_LOC, SB_PAIRS, N_SB_STEPS),
        in_specs=[blk, whole(1), whole(2), blk],
        out_specs=[blk, whole(0), whole(0)],
        out_shape=[out, out, out],
        scratch_shapes=[pltpu.VMEM((SEQ, LANES), F32), pltpu.VMEM((SEQ, LANES), F32),
                        pltpu.VMEM((N_SB_STEPS, SB_ROWS, SB_KCHUNK), F32),
                        pltpu.VMEM((N_SB_STEPS, SB_ROWS, SB_KCHUNK), F32)],
        semantics=("parallel", "parallel", "arbitrary"), operands=(qkv, qkv, qkv, d_o))


DIL_GROUPS = len(DIL_PAIRS)
DIL_QBLOCKS = SEQ // BLK


def _residue_rows(j, dilation):
    length = SEQ // dilation
    return pl.ds(j, length, stride=dilation) if dilation > 1 else pl.ds(0, length)


def _gather_residues(src_ref, dst_ref, dst_off, dilation, scale=None):
    length = SEQ // dilation
    for j in range(dilation):
        v = src_ref[_residue_rows(j, dilation), :]
        if scale is not None:
            v = v * scale
        dst_ref[dst_off + j * length:dst_off + (j + 1) * length, :] = v.astype(dst_ref.dtype)


def _scatter_residues(src_ref, src_off, dst_ref, dilation):
    length = SEQ // dilation
    for j in range(dilation):
        dst_ref[_residue_rows(j, dilation), :] = (
            src_ref[src_off + j * length:src_off + (j + 1) * length, :].astype(dst_ref.dtype))


def _dil_geometry(group, pair):
    dilation = DIL_PAIRS[group][1]
    row = lax.broadcasted_iota(jnp.int32, (2 * BLK, 2 * BLK), 0)
    col = lax.broadcasted_iota(jnp.int32, (2 * BLK, 2 * BLK), 1)
    second = row >= BLK
    steps = BLK + jnp.where(second, row - BLK, row) - col
    coef = -ALIBI_MAX_BIAS / DIL_HEADS * math.log(2.0)
    first_head = float(4 * group + 1) + 2.0 * pair.astype(F32)
    slope = jnp.exp(coef * (first_head + jnp.where(second, 1.0, 0.0)))
    bias = slope * (steps * dilation).astype(F32)
    valid = jnp.logical_and(steps >= 0, steps <= BLK)
    return bias, valid, col >= BLK


def _dil_tile_scores(q2, kk, geometry, has_prev):
    bias, valid, own = geometry
    ok = jnp.logical_and(valid, jnp.logical_or(own, has_prev))
    return jnp.where(ok, _dot_nt(q2, kk) - bias, NEG_BIG)


def _head_col(v, lane_mask):
    return jnp.max(jnp.where(lane_mask, v, NEG_BIG), axis=1, keepdims=True)


def _dil_fwd(qkv, send=None):
    def body(*refs):
        ins, (o_ref, lse_ref), (qs, ks, vs, o_res, lse_res) = refs[:9], refs[9:11], refs[11:16]
        o_grp, lse_grp = refs[16:19], refs[19:22]
        pair = pl.program_id(1)
        lane0 = lax.broadcasted_iota(jnp.int32, (BLK, LANES), 1) < HEAD_DIM
        ks[0:BLK, :] = jnp.zeros((BLK, LANES), BF16)
        vs[0:BLK, :] = jnp.zeros((BLK, LANES), BF16)
        for grp, (_, dilation) in enumerate(DIL_PAIRS):
            q_ref, k_ref, v_ref = ins[3 * grp:3 * grp + 3]
            per_residue = DIL_QBLOCKS // dilation
            _gather_residues(q_ref, qs, 0, dilation, QK_SCALE)
            _gather_residues(k_ref, ks, BLK, dilation)
            _gather_residues(v_ref, vs, BLK, dilation)
            geometry = _dil_geometry(grp, pair)

            def step(blk, _):
                off = pl.multiple_of(blk * BLK, BLK)
                q2 = _stack_heads(qs[pl.ds(off, BLK), :], lane0)
                s = _dil_tile_scores(q2, ks[pl.ds(off, 2 * BLK), :], geometry, blk % per_residue != 0)
                m = jnp.max(s, axis=1, keepdims=True)
                p = jnp.exp(s - m)
                den = jnp.sum(p, axis=1, keepdims=True)
                out = _dot(p.astype(BF16), vs[pl.ds(off, 2 * BLK), :]) / den
                lse = m + jnp.log(den)
                o_res[pl.ds(off, BLK), :] = jnp.where(lane0, out[:BLK], out[BLK:])
                lse_res[pl.ds(off, BLK), :] = jnp.where(lane0, lse[:BLK], lse[BLK:])
                return 0

            lax.fori_loop(0, DIL_QBLOCKS, step, 0, unroll=4)
            _scatter_residues(o_res, 0, o_grp[grp], dilation)
            _scatter_residues(lse_res, 0, lse_grp[grp], dilation)

        for r0 in range(0, SEQ, 2 * BLK):
            rows = slice(r0, r0 + 2 * BLK)
            ls = [lse_grp[g][rows, :] for g in range(DIL_GROUPS)]
            m = jnp.maximum(jnp.maximum(ls[0], ls[1]), ls[2])
            w = [jnp.exp(l - m) for l in ls]
            den = w[0] + w[1] + w[2]
            o_ref[rows, :] = (w[0] * o_grp[0][rows, :] + w[1] * o_grp[1][rows, :] + w[2] * o_grp[2][rows, :]) / den
            lse_ref[rows, :] = m + jnp.log(den)

    def col(part, grp):
        return pl.BlockSpec((None, SEQ, LANES), lambda b, p: (b, 0, 6 * part + 2 * grp + p))

    out_spec = pl.BlockSpec((None, SEQ, LANES), lambda b, p: (b, 0, p))
    out = jax.ShapeDtypeStruct((B_LOC, SEQ, DIL_OUT), F32)
    return _call(
        body, send, name="dil_fwd", grid=(B_LOC, DIL_OUT // LANES),
        in_specs=[col(part, grp) for grp in range(DIL_GROUPS) for part in range(3)],
        out_specs=[out_spec, out_spec], out_shape=[out, out],
        scratch_shapes=[pltpu.VMEM((SEQ, LANES), BF16), pltpu.VMEM((SEQ + BLK, LANES), BF16),
                        pltpu.VMEM((SEQ + BLK, LANES), BF16), pltpu.VMEM((SEQ, LANES), F32),
                        pltpu.VMEM((SEQ, LANES), F32)] + [pltpu.VMEM((SEQ, LANES), F32)] * (2 * DIL_GROUPS),
        semantics=("parallel", "parallel"), operands=[qkv] * 9)


def _dil_bwd(qkv, d_o, lse, dsum, send=None):
    def body(*refs):
        ins, (do_ref, lse_ref, dsum_ref), outs = refs[:9], refs[9:12], refs[12:21]
        qs, ks, vs, dos, lse_res, dsum_res, dq_res, dk_acc, dv_acc = refs[21:]
        pair = pl.program_id(1)
        lane0 = lax.broadcasted_iota(jnp.int32, (BLK, LANES), 1) < HEAD_DIM
        lane1 = jnp.logical_not(lane0)
        ks[0:BLK, :] = jnp.zeros((BLK, LANES), BF16)
        vs[0:BLK, :] = jnp.zeros((BLK, LANES), BF16)
        for grp, (_, dilation) in enumerate(DIL_PAIRS):
            q_ref, k_ref, v_ref = ins[3 * grp:3 * grp + 3]
            dq_ref, dk_ref, dv_ref = outs[3 * grp:3 * grp + 3]
            per_residue = DIL_QBLOCKS // dilation
            _gather_residues(q_ref, qs, 0, dilation, QK_SCALE)
            _gather_residues(k_ref, ks, BLK, dilation)
            _gather_residues(v_ref, vs, BLK, dilation)
            _gather_residues(do_ref, dos, 0, dilation)
            _gather_residues(lse_ref, lse_res, 0, dilation)
            _gather_residues(dsum_ref, dsum_res, 0, dilation)
            dk_acc[...] = jnp.zeros_like(dk_acc)
            dv_acc[...] = jnp.zeros_like(dv_acc)
            geometry = _dil_geometry(grp, pair)

            def step(blk, _):
                off = pl.multiple_of(blk * BLK, BLK)
                q2 = _stack_heads(qs[pl.ds(off, BLK), :], lane0)
                do2 = _stack_heads(dos[pl.ds(off, BLK), :], lane0)
                kk = ks[pl.ds(off, 2 * BLK), :]
                vv = vs[pl.ds(off, 2 * BLK), :]
                lse_blk = lse_res[pl.ds(off, BLK), :]
                dsum_blk = dsum_res[pl.ds(off, BLK), :]
                lse2 = jnp.concatenate([_head_col(lse_blk, lane0), _head_col(lse_blk, lane1)], axis=0)
                dsum2 = jnp.concatenate([_head_col(dsum_blk, lane0), _head_col(dsum_blk, lane1)], axis=0)
                s = _dil_tile_scores(q2, kk, geometry, blk % per_residue != 0)
                p = jnp.exp(s - lse2)
                ds = (p * (_dot_nt(do2, vv) - dsum2)).astype(BF16)
                dq2 = _dot(ds, kk)
                dq_res[pl.ds(off, BLK), :] = jnp.where(lane0, dq2[:BLK], dq2[BLK:]) * QK_SCALE
                dk_acc[pl.ds(off, 2 * BLK), :] += _dot_tn(ds, q2)
                dv_acc[pl.ds(off, 2 * BLK), :] += _dot_tn(p.astype(BF16), do2)
                return 0

            lax.fori_loop(0, DIL_QBLOCKS, step, 0, unroll=4)
            _scatter_residues(dq_res, 0, dq_ref, dilation)
            _scatter_residues(dk_acc, BLK, dk_ref, dilation)
            _scatter_residues(dv_acc, BLK, dv_ref, dilation)

    def col(part, grp):
        return pl.BlockSpec((None, SEQ, LANES), lambda b, p: (b, 0, 6 * part + 2 * grp + p))

    slot = pl.BlockSpec((None, SEQ, LANES), lambda b, p: (b, 0, p))
    out = jax.ShapeDtypeStruct((B_LOC, SEQ, DIL_OUT), F32)
    return _call(
        body, send, name="dil_bwd", grid=(B_LOC, DIL_OUT // LANES),
        in_specs=[col(part, grp) for grp in range(DIL_GROUPS) for part in range(3)] + [slot] * 3,
        out_specs=[slot] * 9, out_shape=[out] * 9,
        scratch_shapes=[pltpu.VMEM((SEQ, LANES), BF16), pltpu.VMEM((SEQ + BLK, LANES), BF16),
                        pltpu.VMEM((SEQ + BLK, LANES), BF16), pltpu.VMEM((SEQ, LANES), BF16),
                        pltpu.VMEM((SEQ, LANES), F32), pltpu.VMEM((SEQ, LANES), F32),
                        pltpu.VMEM((SEQ, LANES), F32), pltpu.VMEM((SEQ + BLK, LANES), F32),
                        pltpu.VMEM((SEQ + BLK, LANES), F32)],
        semantics=("parallel", "parallel"), operands=[qkv] * 9 + [d_o, lse, dsum])


def _peers():
    x, y, c = lax.axis_index("x"), lax.axis_index("y"), lax.axis_index("c")
    me = 4 * x + 2 * y + c
    peers = []
    for mask in range(1, N_DEV):
        px = 1 - x if mask & 4 else x
        py = 1 - y if mask & 2 else y
        pc = 1 - c if mask & 1 else c
        peers.append(((px, py, pc), 4 * px + 2 * py + pc))
    return me, peers


def _all_gather(shard, name):
    def body(src_ref, out_ref, send_sems, recv_sems, local_sem):
        x, y, c = lax.axis_index("x"), lax.axis_index("y"), lax.axis_index("c")
        sibling = (x, y, 1 - c)
        chips = [(1 - x, y), (x, 1 - y), (1 - x, 1 - y)]

        def slot(px, py, pc):
            return out_ref.at[4 * px + 2 * py + pc]

        def copy(k, block, to, src=None):
            return pltpu.make_async_remote_copy(
                src_ref=slot(*block) if src is None else src, dst_ref=slot(*block),
                send_sem=send_sems.at[k], recv_sem=recv_sems.at[k], device_id=to,
                device_id_type=pl.DeviceIdType.MESH)

        mine = pltpu.make_async_copy(src_ref, slot(x, y, c), local_sem)
        mine.start()
        first = [copy(0, (x, y, c), sibling, src=src_ref)]
        first += [copy(1 + j, (x, y, c), (*chip, c), src=src_ref) for j, chip in enumerate(chips)]
        for cp in first:
            cp.start()
        passed = [copy(4 + j, (*chip, c), sibling) for j, chip in enumerate(chips)]
        for j, chip in enumerate(chips):
            copy(1 + j, (*chip, c), (x, y, c)).wait_recv()
            passed[j].start()
        copy(0, sibling, (x, y, c)).wait_recv()
        for j, chip in enumerate(chips):
            copy(4 + j, (*chip, 1 - c), (x, y, c)).wait_recv()
        for cp in first + passed:
            cp.wait_send()
        mine.wait()

    return pl.pallas_call(
        body, name=name,
        in_specs=[pl.BlockSpec(memory_space=pl.ANY)],
        out_specs=pl.BlockSpec(memory_space=pl.ANY),
        out_shape=jax.ShapeDtypeStruct((N_DEV,) + shard.shape, shard.dtype),
        scratch_shapes=[pltpu.SemaphoreType.DMA((N_DEV - 1,)), pltpu.SemaphoreType.DMA((N_DEV - 1,)),
                        pltpu.SemaphoreType.DMA],
    )(shard)


def _call(body, send, *, name, grid, in_specs, out_specs, out_shape, scratch_shapes, semantics, operands):
    if send is None:
        return pl.pallas_call(
            body, name=name, grid=grid, in_specs=in_specs, out_specs=out_specs, out_shape=out_shape,
            scratch_shapes=scratch_shapes, compiler_params=_params(semantics))(*operands), []
    srcs, kinds = [s for s, _ in send], [k for _, k in send]
    n, n_in, n_out, n_scr = len(srcs), len(in_specs), len(out_specs), len(scratch_shapes)
    steps = math.prod(grid)
    relay_step = (3 * steps) // 4

    def plan(refs):
        src_refs, land_refs = refs[n_in:n_in + n], refs[n_in + n + n_out:n_in + 2 * n + n_out]
        send_sems, recv_sems, local_sems = refs[-3:]
        x, y, c = lax.axis_index("x"), lax.axis_index("y"), lax.axis_index("c")
        me, peers = _peers()
        first, relayed_in, relayed_out, arrivals, sends, own = [], [], [], [], [], []
        for a, kind in enumerate(kinds):
            def copy(k, src, dst_slot, to):
                return pltpu.make_async_remote_copy(
                    src_ref=src, dst_ref=land_refs[a].at[dst_slot], send_sem=send_sems.at[a * (N_DEV - 1) + k],
                    recv_sem=recv_sems.at[a * (N_DEV - 1) + k], device_id=to, device_id_type=pl.DeviceIdType.MESH)

            if kind == "gather_by_chip":
                idx = lambda px, py, pc: 4 * px + 2 * py + pc
                chips = [(1 - x, y), (x, 1 - y), (1 - x, 1 - y)]
                mine = [copy(0, src_refs[a], me, (x, y, 1 - c))]
                arrivals.append(copy(0, src_refs[a], idx(x, y, 1 - c), (x, y, 1 - c)))
                for j, (px, py) in enumerate(chips):
                    mine.append(copy(1 + j, src_refs[a], me, (px, py, c)))
                    relayed_in.append(copy(1 + j, src_refs[a], idx(px, py, c), (px, py, c)))
                    relayed_out.append(copy(4 + j, land_refs[a].at[idx(px, py, c)], idx(px, py, c), (x, y, 1 - c)))
                    arrivals.append(copy(4 + j, src_refs[a], idx(px, py, 1 - c), (x, y, 1 - c)))
                first += mine
                sends += mine + relayed_out[-3:]
                own.append(pltpu.make_async_copy(src_refs[a], land_refs[a].at[me], local_sems.at[a]))
            else:
                part = (lambda i: src_refs[a].at[i]) if kind == "scatter" else (lambda i: src_refs[a])
                for k, (peer, peer_idx) in enumerate(peers):
                    first.append(copy(k, part(peer_idx), me, peer))
                    arrivals.append(copy(k, part(peer_idx), peer_idx, peer))
                sends += first[-(N_DEV - 1):]
                own.append(pltpu.make_async_copy(part(me), land_refs[a].at[me], local_sems.at[a]))
        return first, relayed_in, relayed_out, arrivals, sends, own

    def wrapped(*refs):
        step = 0
        for axis, size in enumerate(grid):
            step = step * size + pl.program_id(axis)

        @pl.when(step == 0)
        def _():
            first, _, _, _, _, own = plan(refs)
            for cp in first + own:
                cp.start()

        if "gather_by_chip" in kinds:
            @pl.when(step == relay_step)
            def _():
                _, relayed_in, relayed_out, _, _, _ = plan(refs)
                for cp_in, cp_out in zip(relayed_in, relayed_out):
                    cp_in.wait_recv()
                    cp_out.start()

        body(*refs[:n_in], *refs[n_in + n:n_in + n + n_out], *refs[n_in + 2 * n + n_out:n_in + 2 * n + n_out + n_scr])

        @pl.when(step == steps - 1)
        def _():
            _, _, _, arrivals, sends, own = plan(refs)
            for cp in arrivals:
                cp.wait_recv()
            for cp in sends:
                cp.wait_send()
            for cp in own:
                cp.wait()

    anywhere = pl.BlockSpec(memory_space=pl.ANY)
    lands = [jax.ShapeDtypeStruct((N_DEV,) + s.shape[-2:], s.dtype) for s in srcs]
    out = pl.pallas_call(
        wrapped, name=name, grid=grid,
        in_specs=list(in_specs) + [anywhere] * n, out_specs=list(out_specs) + [anywhere] * n,
        out_shape=list(out_shape) + lands,
        scratch_shapes=list(scratch_shapes) + [pltpu.SemaphoreType.DMA((n * (N_DEV - 1),)),
                                               pltpu.SemaphoreType.DMA((n * (N_DEV - 1),)),
                                               pltpu.SemaphoreType.DMA((n,))],
        compiler_params=_params(("arbitrary",) * len(grid)),
    )(*operands, *srcs)
    return out[:n_out], list(out[n_out:])


def _sum_in_device_order(land_ref):
    acc = land_ref[0].astype(F32)
    for j in range(1, N_DEV):
        acc = acc + land_ref[j].astype(F32)
    return acc


def _adam_math(w, g, m, v):
    c1 = 1.0 - ADAM_B1 ** ADAM_STEP
    c2 = 1.0 - ADAM_B2 ** ADAM_STEP
    m_new = ADAM_B1 * m + (1.0 - ADAM_B1) * g
    v_new = ADAM_B2 * v + (1.0 - ADAM_B2) * (g * g)
    delta = -ADAM_LR * ((m_new / c1) / (jnp.sqrt(v_new / c2) + ADAM_EPS) + ADAM_WD * w)
    return delta, m_new, v_new


def _row_tile(rows):
    return max(t for t in range(8, 257, 8) if rows % t == 0) if rows % 8 == 0 else rows


def _sum_update(land, w, m, v, name):
    _, rows, cols = land.shape
    tile_rows = _row_tile(rows)

    def body(land_ref, w_ref, m_ref, v_ref, g_ref, d_ref, nm_ref, nv_ref):
        g = _sum_in_device_order(land_ref)
        g_ref[...] = g
        d_ref[...], nm_ref[...], nv_ref[...] = _adam_math(w_ref[...], g, m_ref[...], v_ref[...])

    tile = pl.BlockSpec((tile_rows, cols), lambda i: (i, 0))
    out = jax.ShapeDtypeStruct((rows, cols), F32)
    return pl.pallas_call(
        body, name=name, grid=(rows // tile_rows,),
        in_specs=[pl.BlockSpec((N_DEV, tile_rows, cols), lambda i: (0, i, 0)), tile, tile, tile],
        out_specs=[tile] * 4, out_shape=[out] * 4,
        compiler_params=_params(("parallel",)),
    )(land, w, m, v)


def _sum_gains(land):
    def body(land_ref, o_ref):
        o_ref[...] = _sum_in_device_order(land_ref)

    return pl.pallas_call(
        body, name="sum_gain_grads", grid=(1,),
        in_specs=[pl.BlockSpec(land.shape, lambda i: (0, 0, 0))],
        out_specs=pl.BlockSpec(land.shape[1:], lambda i: (0, 0)),
        out_shape=jax.ShapeDtypeStruct(land.shape[1:], F32),
    )(land)


def _adamw(w, g, m, v, name):
    def body(w_ref, g_ref, m_ref, v_ref, d_ref, nm_ref, nv_ref):
        d_ref[...], nm_ref[...], nv_ref[...] = _adam_math(w_ref[...], g_ref[...], m_ref[...], v_ref[...])

    whole = pl.BlockSpec(w.shape, lambda i: (0, 0))
    out = jax.ShapeDtypeStruct(w.shape, F32)
    return pl.pallas_call(
        body, name=name, grid=(1,),
        in_specs=[whole] * 4, out_specs=[whole] * 3, out_shape=[out] * 3,
    )(w, g, m, v)


GROUP_FFN = ("w_ffn_in", "w_ffn_out")
GROUP_MIX = ("w_sb_up", "w_dil_up", "w_out")
COL_SHARDED = ("w_in", "w_sb_up", "w_dil_up", "w_ffn_in")


def _full_from_shards(name, slots):
    _, r, c = slots.shape
    if name in COL_SHARDED:
        return slots.transpose(1, 0, 2).reshape(r, N_DEV * c)
    return slots.reshape(N_DEV * r, c)


def _shards_from_full(name, full):
    rows, cols = full.shape
    if name in COL_SHARDED:
        return full.reshape(rows, N_DEV, cols // N_DEV).transpose(1, 0, 2)
    return full.reshape(N_DEV, rows // N_DEV, cols)


def _local_step(x, target, g_mix, g_ffn, g_fin, w_in, shards=None, rest=None):
    gather = lambda names, kind: None if shards is None else [(shards[n], kind) for n in names]
    scatter = lambda blocks: None if shards is None else [(t, "scatter") for t in blocks]
    landed = lambda blocks, lands: lands if lands else blocks

    w = {"w_in": w_in}
    if shards is None:
        w.update(rest)
    qkv_sb, qkv_dl, gates, u = _norm_proj(x, g_mix, w["w_in"])
    qkv_sb = qkv_sb.reshape(B_LOC, SEQ, 3 * SB_WIDTH)
    qkv_dl = qkv_dl.reshape(B_LOC, SEQ, 3 * DIL_WIDTH)
    (o_sb,), lands = _sb_fwd(qkv_sb, gather(GROUP_FFN, "gather_by_chip"))
    w.update({n: _full_from_shards(n, t) for n, t in zip(GROUP_FFN, lands)})
    o_sb = o_sb.reshape(TOK, SB_WIDTH)
    (o_dl, lse), lands = _dil_fwd(qkv_dl, gather(GROUP_MIX, "gather"))
    w.update({n: _full_from_shards(n, t) for n, t in zip(GROUP_MIX, lands)})
    o_dl = o_dl.reshape(TOK, DIL_OUT)

    x1, merged = _mix_out(x, o_sb, o_dl, gates, w["w_sb_up"], w["w_dil_up"], w["w_out"])
    loss, dx1, u2, act, dh, dx2, dg_fin, dg_ffn = _ffn_fwd_bwd(x1, target, g_ffn, g_fin, w["w_ffn_in"], w["w_ffn_out"])
    dgates, dy_sb, dy_dl, do_sb, do_dl, dsum = _mix_bwd(dx1, o_sb, o_dl, gates, w["w_sb_up"], w["w_dil_up"], w["w_out"])
    blocks = {
        "w_sb_up": _atb(o_sb, dy_sb, "grad_w_sb_up", SB_WIDTH, D_MODEL, col_blocks=N_DEV),
        "w_dil_up": _atb(o_dl.astype(BF16), dy_dl, "grad_w_dil_up", DIL_OUT, D_MODEL, col_blocks=N_DEV),
        "w_out": _shards_from_full("w_out", _atb(merged, dx1.astype(BF16), "grad_w_out", D_MODEL, D_MODEL)),
        "w_ffn_in": _shards_from_full("w_ffn_in", _atb(u2, dh, "grad_w_ffn_in", D_MODEL, D_FF)),
        "w_ffn_out": _shards_from_full("w_ffn_out", _atb(act, dx2, "grad_w_ffn_out", D_FF // 2, D_MODEL)),
    }
    grads = {}

    early, late = ("w_ffn_in",), ("w_ffn_out",) + GROUP_MIX
    early_blocks = [blocks[n] for n in early]
    (dq_sb, dk_sb, dv_sb), lands = _sb_bwd(qkv_sb, do_sb.reshape(B_LOC, SEQ, SB_WIDTH), scatter(early_blocks))
    grads.update(zip(early, landed(early_blocks, lands)))
    as_batch = lambda t: t.reshape(B_LOC, SEQ, DIL_OUT)
    late_blocks = [blocks[n] for n in late]
    d_dl, lands = _dil_bwd(qkv_dl, as_batch(do_dl), lse, as_batch(dsum), scatter(late_blocks))
    grads.update(zip(late, landed(late_blocks, lands)))
    flat = lambda t: t.reshape(TOK, -1)
    dproj = ([flat(dq_sb), flat(dk_sb), flat(dv_sb)]
             + [flat(d_dl[3 * grp + part]) for part in range(3) for grp in range(DIL_GROUPS)] + [dgates])

    w_in_blocks = [_shards_from_full("w_in", _atb_pieces(u, dproj, "grad_w_in", D_MODEL // 2))]
    (grad_x, dg_mix), lands = _proj_bwd(dproj, dx1, x, g_mix, w["w_in"], scatter(w_in_blocks))
    grads["w_in"] = landed(w_in_blocks, lands)[0]
    gain_grads = jnp.concatenate([dg_mix, dg_ffn, dg_fin], axis=0)
    return loss, grad_x, gain_grads, grads


def kernel(x, norm_mix_g, w_in, w_sb_up, w_dil_up, w_out, norm_ffn_g, w_ffn_in, w_ffn_out, norm_final_g, loss_target, m_norm_mix_g, m_w_in, m_w_sb_up, m_w_dil_up, m_w_out, m_norm_ffn_g, m_w_ffn_in, m_w_ffn_out, m_norm_final_g, v_norm_mix_g, v_w_in, v_w_sb_up, v_w_dil_up, v_w_out, v_norm_ffn_g, v_w_ffn_in, v_w_ffn_out, v_norm_final_g):
    mats = {"w_in": w_in, "w_sb_up": w_sb_up, "w_dil_up": w_dil_up, "w_out": w_out,
            "w_ffn_in": w_ffn_in, "w_ffn_out": w_ffn_out}
    moments_m = {"w_in": m_w_in, "w_sb_up": m_w_sb_up, "w_dil_up": m_w_dil_up, "w_out": m_w_out,
                 "w_ffn_in": m_w_ffn_in, "w_ffn_out": m_w_ffn_out}
    moments_v = {"w_in": v_w_in, "w_sb_up": v_w_sb_up, "w_dil_up": v_w_dil_up, "w_out": v_w_out,
                 "w_ffn_in": v_w_ffn_in, "w_ffn_out": v_w_ffn_out}
    gathered_w_in = _all_gather(w_in[0].astype(BF16), "all_gather_w_in")
    g_fin = norm_final_g.reshape(1, D_MODEL)
    loss, grad_x, gain_grads, grad_slots = _local_step(
        x.reshape(TOK, D_MODEL), loss_target.reshape(TOK, D_MODEL), norm_mix_g, norm_ffn_g, g_fin,
        _full_from_shards("w_in", gathered_w_in),
        shards={name: mats[name][0].astype(BF16) for name in GROUP_FFN + GROUP_MIX})

    gain_rows = jnp.concatenate([gain_grads, jnp.tile(loss, (1, D_MODEL // LANES)),
                                 jnp.zeros((8 - 4, D_MODEL), F32)], axis=0)
    g_gains = _sum_gains(_all_gather(gain_rows, "all_gather_gains"))

    out_g, out_d, out_m, out_v = {}, {}, {}, {}
    for name, slots in grad_slots.items():
        g, d, nm, nv = _sum_update(slots, mats[name][0], moments_m[name][0], moments_v[name][0], "update_" + name)
        out_g[name], out_d[name], out_m[name], out_v[name] = g[None], d[None], nm[None], nv[None]

    gain_w = jnp.concatenate([norm_mix_g, norm_ffn_g, g_fin], axis=0)
    gain_m = jnp.concatenate([m_norm_mix_g, m_norm_ffn_g, m_norm_final_g.reshape(1, D_MODEL)], axis=0)
    gain_v = jnp.concatenate([v_norm_mix_g, v_norm_ffn_g, v_norm_final_g.reshape(1, D_MODEL)], axis=0)
    gd, gm, gv = _adamw(gain_w, g_gains[:3], gain_m, gain_v, "adamw_gains")
    for idx, name in enumerate(("norm_mix_g", "norm_ffn_g", "norm_final_g")):
        shape = (D_MODEL,) if name == "norm_final_g" else (1, D_MODEL)
        out_g[name] = g_gains[idx].reshape(shape)
        out_d[name], out_m[name], out_v[name] = gd[idx].reshape(shape), gm[idx].reshape(shape), gv[idx].reshape(shape)

    order = ("norm_mix_g", "w_in", "w_sb_up", "w_dil_up", "w_out", "norm_ffn_g", "w_ffn_in", "w_ffn_out",
             "norm_final_g")
    return (g_gains[3, 0], grad_x.reshape(B_LOC, SEQ, D_MODEL),
            *[out_g[n] for n in order], *[out_d[n] for n in order],
            *[out_m[n] for n in order], *[out_v[n] for n in order])
```

```python
import math

import jax
import jax.numpy as jnp
from jax import lax
from jax.experimental import pallas as pl
from jax.experimental.pallas import tpu as pltpu

F32 = jnp.float32
BF16 = jnp.bfloat16

N_DEV = 8
D_MODEL = 1024
SEQ = 2048
B_LOC = 2
TOK = B_LOC * SEQ
HEAD_DIM = 64
SB_WIDTH = 512
DIL_WIDTH = 768
DIL_OUT = 256
QKV_WIDTH = 3 * SB_WIDTH + 3 * DIL_WIDTH
IN_WIDTH = QKV_WIDTH + 2 * D_MODEL
D_FF = 2816
DIL_PAIRS = ((128, 1), (512, 4), (2048, 16))
DIL_HEADS = 12
RMS_EPS = 1e-6
ALIBI_MAX_BIAS = 8.0
QK_SCALE = 1.0 / math.sqrt(HEAD_DIM)
BLK = 128
LANES = 128
NEG_BIG = -1e30

ADAM_LR = 0.001
ADAM_B1 = 0.9
ADAM_B2 = 0.999
ADAM_EPS = 1e-08
ADAM_WD = 0.01
ADAM_STEP = 10

VMEM_LIMIT = 56 * 1024 * 1024


def _dot(a, b):
    return jnp.dot(a, b, preferred_element_type=F32)


def _dot_nt(a, b):
    return lax.dot_general(a, b, (((1,), (1,)), ((), ())), preferred_element_type=F32)


def _dot_tn(a, b):
    return lax.dot_general(a, b, (((0,), (0,)), ((), ())), preferred_element_type=F32)


def _softplus(z):
    return jnp.maximum(z, 0.0) + jnp.log1p(jnp.exp(-jnp.abs(z)))


def _sigmoid(z):
    return 1.0 / (1.0 + jnp.exp(-z))


def _split_bf16(v):
    hi = v.astype(BF16)
    lo = (v - hi.astype(F32)).astype(BF16)
    return hi, lo


def _chunks(width, step=512):
    out, c = [], 0
    while c < width:
        w = min(step, width - c)
        out.append((c, w))
        c += w
    return out


def _resident(shape):
    nd = len(shape)
    return pl.BlockSpec(shape, lambda *_: (0,) * nd, pipeline_mode=pl.Buffered(1))


def _params(sem):
    return pltpu.CompilerParams(dimension_semantics=sem, vmem_limit_bytes=VMEM_LIMIT)


def _rms_fwd(x, g):
    r = lax.rsqrt(jnp.mean(x * x, axis=-1, keepdims=True) + RMS_EPS)
    n = x * r
    return n, r, n * g


def _rms_bwd(dy, n, r, g):
    dg = jnp.sum(dy * n, axis=0, keepdims=True)
    dn = dy * g
    dx = r * (dn - n * jnp.mean(dn * n, axis=-1, keepdims=True))
    return dx, dg


TM = 256


def _norm_proj(x, g, w_in):
    def body(x_ref, g_ref, w_ref, sb_ref, dl_ref, gate_ref, u_ref):
        _, _, u = _rms_fwd(x_ref[...], g_ref[...])
        u = u.astype(BF16)
        u_ref[...] = u
        for c0, w in _chunks(3 * SB_WIDTH):
            sb_ref[:, c0:c0 + w] = _dot(u, w_ref[:, c0:c0 + w]).astype(BF16)
        for c0, w in _chunks(3 * DIL_WIDTH):
            dl_ref[:, c0:c0 + w] = _dot(u, w_ref[:, 3 * SB_WIDTH + c0:3 * SB_WIDTH + c0 + w])
        for c0, w in _chunks(2 * D_MODEL):
            gate_ref[:, c0:c0 + w] = _dot(u, w_ref[:, QKV_WIDTH + c0:QKV_WIDTH + c0 + w])

    return pl.pallas_call(
        body, name="norm_proj", grid=(TOK // TM,),
        in_specs=[pl.BlockSpec((TM, D_MODEL), lambda i: (i, 0)), _resident((1, D_MODEL)),
                  _resident((D_MODEL, IN_WIDTH))],
        out_specs=[pl.BlockSpec((TM, 3 * SB_WIDTH), lambda i: (i, 0)),
                   pl.BlockSpec((TM, 3 * DIL_WIDTH), lambda i: (i, 0)),
                   pl.BlockSpec((TM, 2 * D_MODEL), lambda i: (i, 0)),
                   pl.BlockSpec((TM, D_MODEL), lambda i: (i, 0))],
        out_shape=[jax.ShapeDtypeStruct((TOK, 3 * SB_WIDTH), BF16),
                   jax.ShapeDtypeStruct((TOK, 3 * DIL_WIDTH), F32),
                   jax.ShapeDtypeStruct((TOK, 2 * D_MODEL), F32),
                   jax.ShapeDtypeStruct((TOK, D_MODEL), BF16)],
        compiler_params=_params(("parallel",)),
    )(x, g, w_in)


def _mix_out(x, o_sb, o_dl, gates, w_sb_up, w_dil_up, w_out):
    def body(x_ref, osb_ref, odl_ref, gate_ref, wsb_ref, wdl_ref, wout_ref, x1_ref, mg_ref):
        y_sb = _dot(osb_ref[...], wsb_ref[...])
        y_dl = _dot(odl_ref[...].astype(BF16), wdl_ref[...])
        merged = (_sigmoid(gate_ref[:, :D_MODEL]) * y_sb
                  + _sigmoid(gate_ref[:, D_MODEL:]) * y_dl).astype(BF16)
        mg_ref[...] = merged
        x1_ref[...] = x_ref[...] + _dot(merged, wout_ref[...])

    return pl.pallas_call(
        body, name="mix_out", grid=(TOK // TM,),
        in_specs=[pl.BlockSpec((TM, D_MODEL), lambda i: (i, 0)),
                  pl.BlockSpec((TM, SB_WIDTH), lambda i: (i, 0)),
                  pl.BlockSpec((TM, DIL_OUT), lambda i: (i, 0)),
                  pl.BlockSpec((TM, 2 * D_MODEL), lambda i: (i, 0)),
                  _resident((SB_WIDTH, D_MODEL)), _resident((DIL_OUT, D_MODEL)),
                  _resident((D_MODEL, D_MODEL))],
        out_specs=[pl.BlockSpec((TM, D_MODEL), lambda i: (i, 0)),
                   pl.BlockSpec((TM, D_MODEL), lambda i: (i, 0))],
        out_shape=[jax.ShapeDtypeStruct((TOK, D_MODEL), F32),
                   jax.ShapeDtypeStruct((TOK, D_MODEL), BF16)],
        compiler_params=_params(("parallel",)),
    )(x, o_sb, o_dl, gates, w_sb_up, w_dil_up, w_out)


FF_CHUNK = D_FF // 2


def _ffn_fwd_bwd(x1, target, g_ffn, g_fin, w_ffn_in, w_ffn_out):
    def body(x1_ref, t_ref, gffn_ref, gfin_ref, win_ref, wout_ref,
             loss_ref, dx1_ref, u2_ref, act_ref, dh_ref, dx2_ref, dgfin_ref, dgffn_ref, h_scr):
        i = pl.program_id(0)

        @pl.when(i == 0)
        def _():
            loss_ref[...] = jnp.zeros_like(loss_ref)
            dgfin_ref[...] = jnp.zeros_like(dgfin_ref)
            dgffn_ref[...] = jnp.zeros_like(dgffn_ref)

        x1 = x1_ref[...]
        g_ffn_v = gffn_ref[...]
        g_fin_v = gfin_ref[...]
        n2, r2, u2 = _rms_fwd(x1, g_ffn_v)
        u2 = u2.astype(BF16)
        u2_ref[...] = u2
        x2 = x1
        for c0 in range(0, D_FF, FF_CHUNK):
            gate = _dot(u2, win_ref[:, c0:c0 + FF_CHUNK])
            up = _dot(u2, win_ref[:, D_FF + c0:D_FF + c0 + FF_CHUNK])
            h_scr[:, c0:c0 + FF_CHUNK] = gate
            h_scr[:, D_FF + c0:D_FF + c0 + FF_CHUNK] = up
            act = (gate * _sigmoid(gate) * up).astype(BF16)
            act_ref[:, c0:c0 + FF_CHUNK] = act
            x2 = x2 + _dot(act, wout_ref[c0:c0 + FF_CHUNK, :])
        n3, r3, y = _rms_fwd(x2, g_fin_v)
        err = y - t_ref[...]
        sq = jnp.sum(jnp.sum(err * err, axis=1, keepdims=True), axis=0, keepdims=True)
        loss_ref[...] += sq * (0.5 / D_MODEL)
        dx2, dgfin = _rms_bwd(err * (1.0 / D_MODEL), n3, r3, g_fin_v)
        dgfin_ref[...] += dgfin
        dx2_b = dx2.astype(BF16)
        dx2_ref[...] = dx2_b
        du2 = jnp.zeros((TM, D_MODEL), F32)
        for c0 in range(0, D_FF, FF_CHUNK):
            gate = h_scr[:, c0:c0 + FF_CHUNK]
            up = h_scr[:, D_FF + c0:D_FF + c0 + FF_CHUNK]
            dact = _dot_nt(dx2_b, wout_ref[c0:c0 + FF_CHUNK, :])
            sg = _sigmoid(gate)
            dgate = (dact * up * (sg * (1.0 + gate * (1.0 - sg)))).astype(BF16)
            dup = (dact * (gate * sg)).astype(BF16)
            dh_ref[:, c0:c0 + FF_CHUNK] = dgate
            dh_ref[:, D_FF + c0:D_FF + c0 + FF_CHUNK] = dup
            du2 = du2 + _dot_nt(dgate, win_ref[:, c0:c0 + FF_CHUNK])
            du2 = du2 + _dot_nt(dup, win_ref[:, D_FF + c0:D_FF + c0 + FF_CHUNK])
        dx1_n, dgffn = _rms_bwd(du2, n2, r2, g_ffn_v)
        dgffn_ref[...] += dgffn
        dx1_ref[...] = dx2 + dx1_n

    tile = lambda w: pl.BlockSpec((TM, w), lambda i: (i, 0))
    acc = lambda w: pl.BlockSpec((1, w), lambda i: (0, 0))
    return pl.pallas_call(
        body, name="ffn_fwd_bwd", grid=(TOK // TM,),
        in_specs=[tile(D_MODEL), tile(D_MODEL), _resident((1, D_MODEL)), _resident((1, D_MODEL)),
                  _resident((D_MODEL, 2 * D_FF)), _resident((D_FF, D_MODEL))],
        out_specs=[acc(LANES), tile(D_MODEL), tile(D_MODEL), tile(D_FF), tile(2 * D_FF), tile(D_MODEL),
                   acc(D_MODEL), acc(D_MODEL)],
        out_shape=[jax.ShapeDtypeStruct((1, LANES), F32),
                   jax.ShapeDtypeStruct((TOK, D_MODEL), F32),
                   jax.ShapeDtypeStruct((TOK, D_MODEL), BF16),
                   jax.ShapeDtypeStruct((TOK, D_FF), BF16),
                   jax.ShapeDtypeStruct((TOK, 2 * D_FF), BF16),
                   jax.ShapeDtypeStruct((TOK, D_MODEL), BF16),
                   jax.ShapeDtypeStruct((1, D_MODEL), F32),
                   jax.ShapeDtypeStruct((1, D_MODEL), F32)],
        scratch_shapes=[pltpu.VMEM((TM, 2 * D_FF), F32)],
        compiler_params=_params(("arbitrary",)),
    )(x1, target, g_ffn, g_fin, w_ffn_in, w_ffn_out)


def _mix_bwd(dx1, o_sb, o_dl, gates, w_sb_up, w_dil_up, w_out):
    def body(dx1_ref, osb_ref, odl_ref, gate_ref, wsb_ref, wdl_ref, wout_ref,
             dgate_ref, dysb_ref, dydl_ref, dosb_ref, dodl_ref, dsum_ref):
        dmerged = _dot_nt(dx1_ref[...].astype(BF16), wout_ref[...])
        o_dl = odl_ref[...]
        y_sb = _dot(osb_ref[...], wsb_ref[...])
        y_dl = _dot(o_dl.astype(BF16), wdl_ref[...])
        s_sb = _sigmoid(gate_ref[:, :D_MODEL])
        s_dl = _sigmoid(gate_ref[:, D_MODEL:])
        dgate_ref[:, :D_MODEL] = (dmerged * y_sb * (s_sb * (1.0 - s_sb))).astype(BF16)
        dgate_ref[:, D_MODEL:] = (dmerged * y_dl * (s_dl * (1.0 - s_dl))).astype(BF16)
        dy_sb = (dmerged * s_sb).astype(BF16)
        dy_dl = (dmerged * s_dl).astype(BF16)
        dysb_ref[...] = dy_sb
        dydl_ref[...] = dy_dl
        dosb_ref[...] = _dot_nt(dy_sb, wsb_ref[...]).astype(BF16)
        do_dl = _dot_nt(dy_dl, wdl_ref[...])
        dodl_ref[...] = do_dl
        row = lax.broadcasted_iota(jnp.int32, (DIL_OUT, DIL_OUT), 0) // HEAD_DIM
        col = lax.broadcasted_iota(jnp.int32, (DIL_OUT, DIL_OUT), 1) // HEAD_DIM
        same_head = (row == col).astype(BF16)
        hi, lo = _split_bf16(do_dl * o_dl)
        dsum_ref[...] = _dot(hi, same_head) + _dot(lo, same_head)

    tile = lambda w: pl.BlockSpec((TM, w), lambda i: (i, 0))
    return pl.pallas_call(
        body, name="mix_bwd", grid=(TOK // TM,),
        in_specs=[tile(D_MODEL), tile(SB_WIDTH), tile(DIL_OUT), tile(2 * D_MODEL),
                  _resident((SB_WIDTH, D_MODEL)), _resident((DIL_OUT, D_MODEL)),
                  _resident((D_MODEL, D_MODEL))],
        out_specs=[tile(2 * D_MODEL), tile(D_MODEL), tile(D_MODEL), tile(SB_WIDTH), tile(DIL_OUT),
                   tile(DIL_OUT)],
        out_shape=[jax.ShapeDtypeStruct((TOK, 2 * D_MODEL), BF16),
                   jax.ShapeDtypeStruct((TOK, D_MODEL), BF16),
                   jax.ShapeDtypeStruct((TOK, D_MODEL), BF16),
                   jax.ShapeDtypeStruct((TOK, SB_WIDTH), BF16),
                   jax.ShapeDtypeStruct((TOK, DIL_OUT), F32),
                   jax.ShapeDtypeStruct((TOK, DIL_OUT), F32)],
        compiler_params=_params(("parallel",)),
    )(dx1, o_sb, o_dl, gates, w_sb_up, w_dil_up, w_out)


def _proj_bwd(dproj, dx1, x, g, w_in, send=None):
    widths = [p.shape[1] for p in dproj]

    def body(*refs):
        dx1_ref, x_ref, g_ref, w_ref, dx_ref, dg_ref = refs[len(widths):]

        @pl.when(pl.program_id(0) == 0)
        def _():
            dg_ref[...] = jnp.zeros_like(dg_ref)

        du = jnp.zeros((TM, D_MODEL), F32)
        c0 = 0
        for dp_ref, w in zip(refs, widths):
            du = du + _dot_nt(dp_ref[...].astype(BF16), w_ref[:, c0:c0 + w])
            c0 += w
        g_v = g_ref[...]
        n, r, _ = _rms_fwd(x_ref[...], g_v)
        dx, dg = _rms_bwd(du, n, r, g_v)
        dg_ref[...] += dg
        dx_ref[...] = dx1_ref[...] + dx

    tile = lambda w: pl.BlockSpec((TM, w), lambda i: (i, 0))
    return _call(
        body, send, name="proj_bwd", grid=(TOK // TM,),
        in_specs=[tile(w) for w in widths] + [tile(D_MODEL), tile(D_MODEL), _resident((1, D_MODEL)),
                                              _resident((D_MODEL, IN_WIDTH))],
        out_specs=[tile(D_MODEL), pl.BlockSpec((1, D_MODEL), lambda i: (0, 0))],
        out_shape=[jax.ShapeDtypeStruct((TOK, D_MODEL), F32),
                   jax.ShapeDtypeStruct((1, D_MODEL), F32)],
        scratch_shapes=[], semantics=("arbitrary",), operands=(*dproj, dx1, x, g, w_in))


def _atb_pieces(a, pieces, name, tm, tk=512):
    m = a.shape[1]
    widths = [p.shape[1] for p in pieces]
    n = sum(widths)
    nk = TOK // tk

    def body(a_ref, *refs):
        o_ref, acc_ref = refs[len(widths):]
        k = pl.program_id(1)

        @pl.when(k == 0)
        def _():
            acc_ref[...] = jnp.zeros_like(acc_ref)

        a_v = a_ref[...]
        c0 = 0
        for p_ref, w in zip(refs, widths):
            acc_ref[:, c0:c0 + w] += _dot_tn(a_v, p_ref[...].astype(BF16))
            c0 += w

        @pl.when(k == nk - 1)
        def _():
            o_ref[...] = acc_ref[...].astype(BF16)

    return pl.pallas_call(
        body, name=name, grid=(m // tm, nk),
        in_specs=[pl.BlockSpec((tk, tm), lambda i, k: (k, i))]
                 + [pl.BlockSpec((tk, w), lambda i, k: (k, 0)) for w in widths],
        out_specs=pl.BlockSpec((tm, n), lambda i, k: (i, 0)),
        out_shape=jax.ShapeDtypeStruct((m, n), BF16),
        scratch_shapes=[pltpu.VMEM((tm, n), F32)],
        compiler_params=_params(("parallel", "arbitrary")),
    )(a, *pieces)


def _atb(a, b, name, tm, tn, col_blocks=0, tk=512):
    m, n = a.shape[1], b.shape[1]
    nk = TOK // tk

    def body(a_ref, b_ref, o_ref, acc_ref):
        k = pl.program_id(2)

        @pl.when(k == 0)
        def _():
            acc_ref[...] = jnp.zeros_like(acc_ref)

        acc_ref[...] += _dot_tn(a_ref[...], b_ref[...])

        @pl.when(k == nk - 1)
        def _():
            if col_blocks:
                width = n // col_blocks
                for blk in range(col_blocks):
                    o_ref[blk] = acc_ref[:, blk * width:(blk + 1) * width].astype(BF16)
            else:
                o_ref[...] = acc_ref[...].astype(BF16)

    if col_blocks:
        out_spec = pl.BlockSpec((col_blocks, tm, n // col_blocks), lambda i, j, k: (0, i, 0))
        out_shape = jax.ShapeDtypeStruct((col_blocks, m, n // col_blocks), BF16)
    else:
        out_spec = pl.BlockSpec((tm, tn), lambda i, j, k: (i, j))
        out_shape = jax.ShapeDtypeStruct((m, n), BF16)
    return pl.pallas_call(
        body, name=name, grid=(m // tm, n // tn, nk),
        in_specs=[pl.BlockSpec((tk, tm), lambda i, j, k: (k, i)),
                  pl.BlockSpec((tk, tn), lambda i, j, k: (k, j))],
        out_specs=out_spec, out_shape=out_shape,
        scratch_shapes=[pltpu.VMEM((tm, tn), F32)],
        compiler_params=_params(("parallel", "parallel", "arbitrary")),
    )(a, b)


SB_PAIRS = SB_WIDTH // LANES


def _two_heads(v, lane0):
    zero = jnp.zeros_like(v)
    return jnp.where(lane0, v, zero), jnp.where(lane0, zero, v)


SB_QBLK = 256
N_SB_STEPS = SEQ // SB_QBLK


SB_KCHUNK = 2 * BLK
SB_ROWS = 2 * SB_QBLK
SB_DEAD = -104.0


def _log_keep(z):
    neg_z = -z
    return jnp.minimum(neg_z, 0.0) - jnp.log(1.0 + jnp.exp(jnp.minimum(z, neg_z)))


def _stack_heads(v, lane0):
    return jnp.concatenate(_two_heads(v, lane0), axis=0)


def _block_sums(v, tri):
    halves = (v[:, :BLK], v[:, BLK:])
    hi, lo = _split_bf16(jnp.concatenate(halves, axis=0))
    prod = _dot(jnp.concatenate([hi, lo], axis=0), tri)
    tri_sum = prod[:2 * SB_ROWS] + prod[2 * SB_ROWS:]
    sums = tuple(jnp.sum(h, axis=1, keepdims=True) for h in halves)
    return (tri_sum[:SB_ROWS], tri_sum[SB_ROWS:]), sums


def _sb_diag_mask():
    row = lax.broadcasted_iota(jnp.int32, (SB_ROWS, SB_KCHUNK), 0)
    col = lax.broadcasted_iota(jnp.int32, (SB_ROWS, SB_KCHUNK), 1)
    return col < jnp.where(row >= SB_QBLK, row - SB_QBLK, row)


def _sb_fwd(qkv, send=None):
    def body(q_ref, k_ref, v_ref, o_ref):
        i = pl.program_id(2)
        krow = lax.broadcasted_iota(jnp.int32, (BLK, BLK), 0)
        kcol = lax.broadcasted_iota(jnp.int32, (BLK, BLK), 1)
        later = (krow > kcol).astype(BF16)
        lane0 = lax.broadcasted_iota(jnp.int32, (SB_QBLK, LANES), 1) < HEAD_DIM
        q2 = _stack_heads(q_ref[0] * QK_SCALE, lane0)

        def chunk(c, carry, causal):
            acc, run = carry
            off = pl.multiple_of(c * SB_KCHUNK, SB_KCHUNK)
            z = _dot_nt(q2, k_ref[0, pl.ds(off, SB_KCHUNK), :])
            log_keep = _log_keep(z)
            if causal is not None:
                log_keep = jnp.where(causal, log_keep, 0.0)
            suffix, sums = _block_sums(log_keep, later)
            log_after = jnp.concatenate([suffix[0] + (run + sums[1]), suffix[1] + run], axis=1)
            a = jnp.exp(log_keep + z + log_after)
            if causal is not None:
                a = jnp.where(causal, a, 0.0)
            acc = acc + _dot(a.astype(BF16), v_ref[0, pl.ds(off, SB_KCHUNK), :])
            return acc, run + (sums[0] + sums[1])

        acc, run = chunk(i, (jnp.zeros((SB_ROWS, LANES), F32), jnp.zeros((SB_ROWS, 1), F32)), _sb_diag_mask())

        def live(state):
            t, _, run = state
            return jnp.logical_and(t < i, jnp.max(run) > SB_DEAD)

        def trip(state):
            t, acc, run = state
            acc, run = chunk(i - 1 - t, (acc, run), None)
            return t + 1, acc, run

        _, acc, _ = lax.while_loop(live, trip, (jnp.int32(0), acc, run))
        o_ref[0] = jnp.where(lane0, acc[:SB_QBLK], acc[SB_QBLK:]).astype(BF16)

    blk = pl.BlockSpec((1, SB_QBLK, LANES), lambda b, h, i: (b, i, h))
    return _call(
        body, send, name="sb_fwd", grid=(B_LOC, SB_PAIRS, N_SB_STEPS),
        in_specs=[blk,
                  pl.BlockSpec((1, SEQ, LANES), lambda b, h, i: (b, 0, SB_PAIRS + h)),
                  pl.BlockSpec((1, SEQ, LANES), lambda b, h, i: (b, 0, 2 * SB_PAIRS + h))],
        out_specs=[blk], out_shape=[jax.ShapeDtypeStruct((B_LOC, SEQ, SB_WIDTH), BF16)],
        scratch_shapes=[], semantics=("parallel", "parallel", "arbitrary"), operands=(qkv, qkv, qkv))


def _sb_bwd(qkv, d_o, send=None):
    def body(q_ref, k_ref, v_ref, do_ref, dq_ref, dk_ref, dv_ref, dk_acc, dv_acc, z_scr, keep_scr):
        i = pl.program_id(2)
        krow = lax.broadcasted_iota(jnp.int32, (BLK, BLK), 0)
        kcol = lax.broadcasted_iota(jnp.int32, (BLK, BLK), 1)
        upto = (krow <= kcol).astype(BF16)
        earlier = (krow < kcol).astype(BF16)
        lane0 = lax.broadcasted_iota(jnp.int32, (SB_QBLK, LANES), 1) < HEAD_DIM
        q2 = _stack_heads(q_ref[0] * QK_SCALE, lane0)
        do2 = _stack_heads(do_ref[0], lane0)

        def keep_sum(c, causal):
            off = pl.multiple_of(c * SB_KCHUNK, SB_KCHUNK)
            z = _dot_nt(q2, k_ref[0, pl.ds(off, SB_KCHUNK), :])
            log_keep = _log_keep(z)
            if causal is not None:
                log_keep = jnp.where(causal, log_keep, 0.0)
            z_scr[c] = z
            keep_scr[c] = log_keep
            return jnp.sum(log_keep, axis=1, keepdims=True)

        def live(state):
            t, run = state
            return jnp.logical_and(t < i, jnp.max(run) > SB_DEAD)

        walked, tot2 = lax.while_loop(live, lambda s: (s[0] + 1, s[1] + keep_sum(i - 1 - s[0], None)),
                                      (jnp.int32(0), keep_sum(i, _sb_diag_mask())))
        first = i - walked

        @pl.when(i == 0)
        def _():
            dk_acc[...] = jnp.zeros_like(dk_acc)
            dv_acc[...] = jnp.zeros_like(dv_acc)

        def chunk(c, carry, causal):
            dq, pre_keep, pre_e = carry
            off = pl.multiple_of(c * SB_KCHUNK, SB_KCHUNK)
            k_c = k_ref[0, pl.ds(off, SB_KCHUNK), :]
            v_c = v_ref[0, pl.ds(off, SB_KCHUNK), :]
            d_a = _dot_nt(do2, v_c)
            log_keep = keep_scr[c]
            log_beta = log_keep + z_scr[c]
            prefix, sums = _block_sums(log_keep, upto)
            inclusive = jnp.concatenate([prefix[0], prefix[1] + sums[0]], axis=1)
            a = jnp.exp(log_beta + ((tot2 - pre_keep) - inclusive))
            if causal is not None:
                a = jnp.where(causal, a, 0.0)
            e = d_a * a
            e_prefix, e_sums = _block_sums(e, earlier)
            before = jnp.concatenate([e_prefix[0] + pre_e, e_prefix[1] + (pre_e + e_sums[0])], axis=1)
            dz = e - (e + before) * jnp.exp(log_beta)
            if causal is not None:
                dz = jnp.where(causal, dz, 0.0)
            dz = dz.astype(BF16)
            dq = dq + _dot(dz, k_c)
            dk_acc[pl.ds(off, SB_KCHUNK), :] += _dot_tn(dz, q2)
            dv_acc[pl.ds(off, SB_KCHUNK), :] += _dot_tn(a.astype(BF16), do2)
            return dq, pre_keep + (sums[0] + sums[1]), pre_e + (e_sums[0] + e_sums[1])

        zero_col = jnp.zeros((SB_ROWS, 1), F32)
        carry = lax.fori_loop(first, i, lambda t, c: chunk(t, c, None),
                              (jnp.zeros((SB_ROWS, LANES), F32), zero_col, zero_col))
        dq, _, _ = chunk(i, carry, _sb_diag_mask())
        dq_ref[0] = (jnp.where(lane0, dq[:SB_QBLK], dq[SB_QBLK:]) * QK_SCALE).astype(BF16)

        @pl.when(i == N_SB_STEPS - 1)
        def _():
            dk_ref[0] = dk_acc[...].astype(BF16)
            dv_ref[0] = dv_acc[...].astype(BF16)

    blk = pl.BlockSpec((1, SB_QBLK, LANES), lambda b, h, i: (b, i, h))
    whole = lambda c: pl.BlockSpec((1, SEQ, LANES), lambda b, h, i: (b, 0, c * SB_PAIRS + h))
    out = jax.ShapeDtypeStruct((B_LOC, SEQ, SB_WIDTH), BF16)
    return _call(
        body, send, name="sb_bwd", grid=(B_LOC, SB_PAIRS, N_SB_STEPS),
        in_specs=[blk, whole(1), whole(2), blk],
        out_specs=[blk, whole(0), whole(0)],
        out_shape=[out, out, out],
        scratch_shapes=[pltpu.VMEM((SEQ, LANES), F32), pltpu.VMEM((SEQ, LANES), F32),
                        pltpu.VMEM((N_SB_STEPS, SB_ROWS, SB_KCHUNK), F32),
                        pltpu.VMEM((N_SB_STEPS, SB_ROWS, SB_KCHUNK), F32)],
        semantics=("parallel", "parallel", "arbitrary"), operands=(qkv, qkv, qkv, d_o))


DIL_GROUPS = len(DIL_PAIRS)
DIL_QBLOCKS = SEQ // BLK


def _residue_rows(j, dilation):
    length = SEQ // dilation
    return pl.ds(j, length, stride=dilation) if dilation > 1 else pl.ds(0, length)


def _gather_residues(src_ref, dst_ref, dst_off, dilation, scale=None):
    length = SEQ // dilation
    for j in range(dilation):
        v = src_ref[_residue_rows(j, dilation), :]
        if scale is not None:
            v = v * scale
        dst_ref[dst_off + j * length:dst_off + (j + 1) * length, :] = v.astype(dst_ref.dtype)


def _scatter_residues(src_ref, src_off, dst_ref, dilation):
    length = SEQ // dilation
    for j in range(dilation):
        dst_ref[_residue_rows(j, dilation), :] = (
            src_ref[src_off + j * length:src_off + (j + 1) * length, :].astype(dst_ref.dtype))


def _dil_geometry(group, pair):
    dilation = DIL_PAIRS[group][1]
    row = lax.broadcasted_iota(jnp.int32, (2 * BLK, 2 * BLK), 0)
    col = lax.broadcasted_iota(jnp.int32, (2 * BLK, 2 * BLK), 1)
    second = row >= BLK
    steps = BLK + jnp.where(second, row - BLK, row) - col
    coef = -ALIBI_MAX_BIAS / DIL_HEADS * math.log(2.0)
    first_head = float(4 * group + 1) + 2.0 * pair.astype(F32)
    slope = jnp.exp(coef * (first_head + jnp.where(second, 1.0, 0.0)))
    bias = slope * (steps * dilation).astype(F32)
    valid = jnp.logical_and(steps >= 0, steps <= BLK)
    return bias, valid, col >= BLK


def _dil_tile_scores(q2, kk, geometry, has_prev):
    bias, valid, own = geometry
    ok = jnp.logical_and(valid, jnp.logical_or(own, has_prev))
    return jnp.where(ok, _dot_nt(q2, kk) - bias, NEG_BIG)


def _head_col(v, lane_mask):
    return jnp.max(jnp.where(lane_mask, v, NEG_BIG), axis=1, keepdims=True)


def _dil_fwd(qkv, send=None):
    def body(*refs):
        ins, (o_ref, lse_ref), (qs, ks, vs, o_res, lse_res) = refs[:9], refs[9:11], refs[11:16]
        o_grp, lse_grp = refs[16:19], refs[19:22]
        pair = pl.program_id(1)
        lane0 = lax.broadcasted_iota(jnp.int32, (BLK, LANES), 1) < HEAD_DIM
        ks[0:BLK, :] = jnp.zeros((BLK, LANES), BF16)
        vs[0:BLK, :] = jnp.zeros((BLK, LANES), BF16)
        for grp, (_, dilation) in enumerate(DIL_PAIRS):
            q_ref, k_ref, v_ref = ins[3 * grp:3 * grp + 3]
            per_residue = DIL_QBLOCKS // dilation
            _gather_residues(q_ref, qs, 0, dilation, QK_SCALE)
            _gather_residues(k_ref, ks, BLK, dilation)
            _gather_residues(v_ref, vs, BLK, dilation)
            geometry = _dil_geometry(grp, pair)

            def step(blk, _):
                off = pl.multiple_of(blk * BLK, BLK)
                q2 = _stack_heads(qs[pl.ds(off, BLK), :], lane0)
                s = _dil_tile_scores(q2, ks[pl.ds(off, 2 * BLK), :], geometry, blk % per_residue != 0)
                m = jnp.max(s, axis=1, keepdims=True)
                p = jnp.exp(s - m)
                den = jnp.sum(p, axis=1, keepdims=True)
                out = _dot(p.astype(BF16), vs[pl.ds(off, 2 * BLK), :]) / den
                lse = m + jnp.log(den)
                o_res[pl.ds(off, BLK), :] = jnp.where(lane0, out[:BLK], out[BLK:])
                lse_res[pl.ds(off, BLK), :] = jnp.where(lane0, lse[:BLK], lse[BLK:])
                return 0

            lax.fori_loop(0, DIL_QBLOCKS, step, 0, unroll=4)
            _scatter_residues(o_res, 0, o_grp[grp], dilation)
            _scatter_residues(lse_res, 0, lse_grp[grp], dilation)

        for r0 in range(0, SEQ, 2 * BLK):
            rows = slice(r0, r0 + 2 * BLK)
            ls = [lse_grp[g][rows, :] for g in range(DIL_GROUPS)]
            m = jnp.maximum(jnp.maximum(ls[0], ls[1]), ls[2])
            w = [jnp.exp(l - m) for l in ls]
            den = w[0] + w[1] + w[2]
            o_ref[rows, :] = (w[0] * o_grp[0][rows, :] + w[1] * o_grp[1][rows, :] + w[2] * o_grp[2][rows, :]) / den
            lse_ref[rows, :] = m + jnp.log(den)

    def col(part, grp):
        return pl.BlockSpec((None, SEQ, LANES), lambda b, p: (b, 0, 6 * part + 2 * grp + p))

    out_spec = pl.BlockSpec((None, SEQ, LANES), lambda b, p: (b, 0, p))
    out = jax.ShapeDtypeStruct((B_LOC, SEQ, DIL_OUT), F32)
    return _call(
        body, send, name="dil_fwd", grid=(B_LOC, DIL_OUT // LANES),
        in_specs=[col(part, grp) for grp in range(DIL_GROUPS) for part in range(3)],
        out_specs=[out_spec, out_spec], out_shape=[out, out],
        scratch_shapes=[pltpu.VMEM((SEQ, LANES), BF16), pltpu.VMEM((SEQ + BLK, LANES), BF16),
                        pltpu.VMEM((SEQ + BLK, LANES), BF16), pltpu.VMEM((SEQ, LANES), F32),
                        pltpu.VMEM((SEQ, LANES), F32)] + [pltpu.VMEM((SEQ, LANES), F32)] * (2 * DIL_GROUPS),
        semantics=("parallel", "parallel"), operands=[qkv] * 9)


def _dil_bwd(qkv, d_o, lse, dsum, send=None):
    def body(*refs):
        ins, (do_ref, lse_ref, dsum_ref), outs = refs[:9], refs[9:12], refs[12:21]
        qs, ks, vs, dos, lse_res, dsum_res, dq_res, dk_acc, dv_acc = refs[21:]
        pair = pl.program_id(1)
        lane0 = lax.broadcasted_iota(jnp.int32, (BLK, LANES), 1) < HEAD_DIM
        lane1 = jnp.logical_not(lane0)
        ks[0:BLK, :] = jnp.zeros((BLK, LANES), BF16)
        vs[0:BLK, :] = jnp.zeros((BLK, LANES), BF16)
        for grp, (_, dilation) in enumerate(DIL_PAIRS):
            q_ref, k_ref, v_ref = ins[3 * grp:3 * grp + 3]
            dq_ref, dk_ref, dv_ref = outs[3 * grp:3 * grp + 3]
            per_residue = DIL_QBLOCKS // dilation
            _gather_residues(q_ref, qs, 0, dilation, QK_SCALE)
            _gather_residues(k_ref, ks, BLK, dilation)
            _gather_residues(v_ref, vs, BLK, dilation)
            _gather_residues(do_ref, dos, 0, dilation)
            _gather_residues(lse_ref, lse_res, 0, dilation)
            _gather_residues(dsum_ref, dsum_res, 0, dilation)
            dk_acc[...] = jnp.zeros_like(dk_acc)
            dv_acc[...] = jnp.zeros_like(dv_acc)
            geometry = _dil_geometry(grp, pair)

            def step(blk, _):
                off = pl.multiple_of(blk * BLK, BLK)
                q2 = _stack_heads(qs[pl.ds(off, BLK), :], lane0)
                do2 = _stack_heads(dos[pl.ds(off, BLK), :], lane0)
                kk = ks[pl.ds(off, 2 * BLK), :]
                vv = vs[pl.ds(off, 2 * BLK), :]
                lse_blk = lse_res[pl.ds(off, BLK), :]
                dsum_blk = dsum_res[pl.ds(off, BLK), :]
                lse2 = jnp.concatenate([_head_col(lse_blk, lane0), _head_col(lse_blk, lane1)], axis=0)
                dsum2 = jnp.concatenate([_head_col(dsum_blk, lane0), _head_col(dsum_blk, lane1)], axis=0)
                s = _dil_tile_scores(q2, kk, geometry, blk % per_residue != 0)
                p = jnp.exp(s - lse2)
                ds = (p * (_dot_nt(do2, vv) - dsum2)).astype(BF16)
                dq2 = _dot(ds, kk)
                dq_res[pl.ds(off, BLK), :] = jnp.where(lane0, dq2[:BLK], dq2[BLK:]) * QK_SCALE
                dk_acc[pl.ds(off, 2 * BLK), :] += _dot_tn(ds, q2)
                dv_acc[pl.ds(off, 2 * BLK), :] += _dot_tn(p.astype(BF16), do2)
                return 0

            lax.fori_loop(0, DIL_QBLOCKS, step, 0, unroll=4)
            _scatter_residues(dq_res, 0, dq_ref, dilation)
            _scatter_residues(dk_acc, BLK, dk_ref, dilation)
            _scatter_residues(dv_acc, BLK, dv_ref, dilation)

    def col(part, grp):
        return pl.BlockSpec((None, SEQ, LANES), lambda b, p: (b, 0, 6 * part + 2 * grp + p))

    slot = pl.BlockSpec((None, SEQ, LANES), lambda b, p: (b, 0, p))
    out = jax.ShapeDtypeStruct((B_LOC, SEQ, DIL_OUT), F32)
    return _call(
        body, send, name="dil_bwd", grid=(B_LOC, DIL_OUT // LANES),
        in_specs=[col(part, grp) for grp in range(DIL_GROUPS) for part in range(3)] + [slot] * 3,
        out_specs=[slot] * 9, out_shape=[out] * 9,
        scratch_shapes=[pltpu.VMEM((SEQ, LANES), BF16), pltpu.VMEM((SEQ + BLK, LANES), BF16),
                        pltpu.VMEM((SEQ + BLK, LANES), BF16), pltpu.VMEM((SEQ, LANES), BF16),
                        pltpu.VMEM((SEQ, LANES), F32), pltpu.VMEM((SEQ, LANES), F32),
                        pltpu.VMEM((SEQ, LANES), F32), pltpu.VMEM((SEQ + BLK, LANES), F32),
                        pltpu.VMEM((SEQ + BLK, LANES), F32)],
        semantics=("parallel", "parallel"), operands=[qkv] * 9 + [d_o, lse, dsum])


def _peers():
    x, y, c = lax.axis_index("x"), lax.axis_index("y"), lax.axis_index("c")
    me = 4 * x + 2 * y + c
    peers = []
    for mask in range(1, N_DEV):
        px = 1 - x if mask & 4 else x
        py = 1 - y if mask & 2 else y
        pc = 1 - c if mask & 1 else c
        peers.append(((px, py, pc), 4 * px + 2 * py + pc))
    return me, peers


def _all_gather(shard, name):
    def body(src_ref, out_ref, send_sems, recv_sems, local_sem):
        x, y, c = lax.axis_index("x"), lax.axis_index("y"), lax.axis_index("c")
        sibling = (x, y, 1 - c)
        chips = [(1 - x, y), (x, 1 - y), (1 - x, 1 - y)]

        def slot(px, py, pc):
            return out_ref.at[4 * px + 2 * py + pc]

        def copy(k, block, to, src=None):
            return pltpu.make_async_remote_copy(
                src_ref=slot(*block) if src is None else src, dst_ref=slot(*block),
                send_sem=send_sems.at[k], recv_sem=recv_sems.at[k], device_id=to,
                device_id_type=pl.DeviceIdType.MESH)

        mine = pltpu.make_async_copy(src_ref, slot(x, y, c), local_sem)
        mine.start()
        first = [copy(0, (x, y, c), sibling, src=src_ref)]
        first += [copy(1 + j, (x, y, c), (*chip, c), src=src_ref) for j, chip in enumerate(chips)]
        for cp in first:
            cp.start()
        passed = [copy(4 + j, (*chip, c), sibling) for j, chip in enumerate(chips)]
        for j, chip in enumerate(chips):
            copy(1 + j, (*chip, c), (x, y, c)).wait_recv()
            passed[j].start()
        copy(0, sibling, (x, y, c)).wait_recv()
        for j, chip in enumerate(chips):
            copy(4 + j, (*chip, 1 - c), (x, y, c)).wait_recv()
        for cp in first + passed:
            cp.wait_send()
        mine.wait()

    return pl.pallas_call(
        body, name=name,
        in_specs=[pl.BlockSpec(memory_space=pl.ANY)],
        out_specs=pl.BlockSpec(memory_space=pl.ANY),
        out_shape=jax.ShapeDtypeStruct((N_DEV,) + shard.shape, shard.dtype),
        scratch_shapes=[pltpu.SemaphoreType.DMA((N_DEV - 1,)), pltpu.SemaphoreType.DMA((N_DEV - 1,)),
                        pltpu.SemaphoreType.DMA],
    )(shard)


def _call(body, send, *, name, grid, in_specs, out_specs, out_shape, scratch_shapes, semantics, operands):
    if send is None:
        return pl.pallas_call(
            body, name=name, grid=grid, in_specs=in_specs, out_specs=out_specs, out_shape=out_shape,
            scratch_shapes=scratch_shapes, compiler_params=_params(semantics))(*operands), []
    srcs, kinds = [s for s, _ in send], [k for _, k in send]
    n, n_in, n_out, n_scr = len(srcs), len(in_specs), len(out_specs), len(scratch_shapes)
    steps = math.prod(grid)
    relay_step = (3 * steps) // 4

    def plan(refs):
        src_refs, land_refs = refs[n_in:n_in + n], refs[n_in + n + n_out:n_in + 2 * n + n_out]
        send_sems, recv_sems, local_sems = refs[-3:]
        x, y, c = lax.axis_index("x"), lax.axis_index("y"), lax.axis_index("c")
        me, peers = _peers()
        first, relayed_in, relayed_out, arrivals, sends, own = [], [], [], [], [], []
        for a, kind in enumerate(kinds):
            def copy(k, src, dst_slot, to):
                return pltpu.make_async_remote_copy(
                    src_ref=src, dst_ref=land_refs[a].at[dst_slot], send_sem=send_sems.at[a * (N_DEV - 1) + k],
                    recv_sem=recv_sems.at[a * (N_DEV - 1) + k], device_id=to, device_id_type=pl.DeviceIdType.MESH)

            if kind == "gather_by_chip":
                idx = lambda px, py, pc: 4 * px + 2 * py + pc
                chips = [(1 - x, y), (x, 1 - y), (1 - x, 1 - y)]
                mine = [copy(0, src_refs[a], me, (x, y, 1 - c))]
                arrivals.append(copy(0, src_refs[a], idx(x, y, 1 - c), (x, y, 1 - c)))
                for j, (px, py) in enumerate(chips):
                    mine.append(copy(1 + j, src_refs[a], me, (px, py, c)))
                    relayed_in.append(copy(1 + j, src_refs[a], idx(px, py, c), (px, py, c)))
                    relayed_out.append(copy(4 + j, land_refs[a].at[idx(px, py, c)], idx(px, py, c), (x, y, 1 - c)))
                    arrivals.append(copy(4 + j, src_refs[a], idx(px, py, 1 - c), (x, y, 1 - c)))
                first += mine
                sends += mine + relayed_out[-3:]
                own.append(pltpu.make_async_copy(src_refs[a], land_refs[a].at[me], local_sems.at[a]))
            elif kind == "scatter_by_chip":
                for k, (px, py) in enumerate([(1 - x, y), (x, 1 - y), (1 - x, 1 - y)]):
                    first.append(copy(k, src_refs[a].at[2 * px + py], 2 * x + y, (px, py, c)))
                    arrivals.append(copy(k, src_refs[a].at[2 * px + py], 2 * px + py, (px, py, c)))
                sends += first[-3:]
                own.append(pltpu.make_async_copy(src_refs[a].at[2 * x + y], land_refs[a].at[2 * x + y],
                                                 local_sems.at[a]))
            else:
                part = (lambda i: src_refs[a].at[i]) if kind == "scatter" else (lambda i: src_refs[a])
                for k, (peer, peer_idx) in enumerate(peers):
                    first.append(copy(k, part(peer_idx), me, peer))
                    arrivals.append(copy(k, part(peer_idx), peer_idx, peer))
                sends += first[-(N_DEV - 1):]
                own.append(pltpu.make_async_copy(part(me), land_refs[a].at[me], local_sems.at[a]))
        return first, relayed_in, relayed_out, arrivals, sends, own

    def wrapped(*refs):
        step = 0
        for axis, size in enumerate(grid):
            step = step * size + pl.program_id(axis)

        @pl.when(step == 0)
        def _():
            first, _, _, _, _, own = plan(refs)
            for cp in first + own:
                cp.start()

        if "gather_by_chip" in kinds:
            @pl.when(step == relay_step)
            def _():
                _, relayed_in, relayed_out, _, _, _ = plan(refs)
                for cp_in, cp_out in zip(relayed_in, relayed_out):
                    cp_in.wait_recv()
                    cp_out.start()

        body(*refs[:n_in], *refs[n_in + n:n_in + n + n_out], *refs[n_in + 2 * n + n_out:n_in + 2 * n + n_out + n_scr])

        @pl.when(step == steps - 1)
        def _():
            _, _, _, arrivals, sends, own = plan(refs)
            for cp in arrivals:
                cp.wait_recv()
            for cp in sends:
                cp.wait_send()
            for cp in own:
                cp.wait()

    anywhere = pl.BlockSpec(memory_space=pl.ANY)
    lands = [jax.ShapeDtypeStruct((N_DEV // 2 if k == "scatter_by_chip" else N_DEV,) + s.shape[-2:], s.dtype)
             for s, k in send]
    out = pl.pallas_call(
        wrapped, name=name, grid=grid,
        in_specs=list(in_specs) + [anywhere] * n, out_specs=list(out_specs) + [anywhere] * n,
        out_shape=list(out_shape) + lands,
        scratch_shapes=list(scratch_shapes) + [pltpu.SemaphoreType.DMA((n * (N_DEV - 1),)),
                                               pltpu.SemaphoreType.DMA((n * (N_DEV - 1),)),
                                               pltpu.SemaphoreType.DMA((n,))],
        compiler_params=_params(("arbitrary",) * len(grid)),
    )(*operands, *srcs)
    return out[:n_out], list(out[n_out:])


def _pair_swap(blocks):
    def body(src_ref, out_ref, send_sems, recv_sems):
        x, y, c = lax.axis_index("x"), lax.axis_index("y"), lax.axis_index("c")
        copies = [pltpu.make_async_remote_copy(
            src_ref=src_ref.at[2 * chip + (1 - c)], dst_ref=out_ref.at[chip], send_sem=send_sems.at[chip],
            recv_sem=recv_sems.at[chip], device_id=(x, y, 1 - c), device_id_type=pl.DeviceIdType.MESH)
            for chip in range(N_DEV // 2)]
        for cp in copies:
            cp.start()
        for cp in copies:
            cp.wait()

    return pl.pallas_call(
        body, name="pair_swap_grad_w_in",
        in_specs=[pl.BlockSpec(memory_space=pl.ANY)], out_specs=pl.BlockSpec(memory_space=pl.ANY),
        out_shape=jax.ShapeDtypeStruct((N_DEV // 2,) + blocks.shape[1:], blocks.dtype),
        scratch_shapes=[pltpu.SemaphoreType.DMA((N_DEV // 2,)), pltpu.SemaphoreType.DMA((N_DEV // 2,))],
    )(blocks)


def _pair_sum(blocks, swapped, core):
    _, rows, cols = swapped.shape
    tile_rows = _row_tile(rows)

    def body(core_ref, mine_ref, theirs_ref, o_ref):
        o_ref[...] = (mine_ref[...].astype(F32) + theirs_ref[...].astype(F32)).astype(o_ref.dtype)

    return pl.pallas_call(
        body, name="pair_sum_grad_w_in",
        grid_spec=pltpu.PrefetchScalarGridSpec(
            num_scalar_prefetch=1, grid=(N_DEV // 2, rows // tile_rows),
            in_specs=[pl.BlockSpec((None, tile_rows, cols), lambda j, i, core_ref: (2 * j + core_ref[0], i, 0)),
                      pl.BlockSpec((None, tile_rows, cols), lambda j, i, core_ref: (j, i, 0))],
            out_specs=pl.BlockSpec((None, tile_rows, cols), lambda j, i, core_ref: (j, i, 0))),
        out_shape=jax.ShapeDtypeStruct(swapped.shape, swapped.dtype),
        compiler_params=_params(("parallel", "parallel")),
    )(core, blocks, swapped)


def _sum_in_device_order(land_ref):
    acc = land_ref[0].astype(F32)
    for j in range(1, land_ref.shape[0]):
        acc = acc + land_ref[j].astype(F32)
    return acc


def _adam_math(w, g, m, v):
    c1 = 1.0 - ADAM_B1 ** ADAM_STEP
    c2 = 1.0 - ADAM_B2 ** ADAM_STEP
    m_new = ADAM_B1 * m + (1.0 - ADAM_B1) * g
    v_new = ADAM_B2 * v + (1.0 - ADAM_B2) * (g * g)
    delta = -ADAM_LR * ((m_new / c1) / (jnp.sqrt(v_new / c2) + ADAM_EPS) + ADAM_WD * w)
    return delta, m_new, v_new


def _row_tile(rows):
    return max(t for t in range(8, 257, 8) if rows % t == 0) if rows % 8 == 0 else rows


def _sum_update(land, w, m, v, name):
    slots, rows, cols = land.shape
    tile_rows = _row_tile(rows)

    def body(land_ref, w_ref, m_ref, v_ref, g_ref, d_ref, nm_ref, nv_ref):
        g = _sum_in_device_order(land_ref)
        g_ref[...] = g
        d_ref[...], nm_ref[...], nv_ref[...] = _adam_math(w_ref[...], g, m_ref[...], v_ref[...])

    tile = pl.BlockSpec((tile_rows, cols), lambda i: (i, 0))
    out = jax.ShapeDtypeStruct((rows, cols), F32)
    return pl.pallas_call(
        body, name=name, grid=(rows // tile_rows,),
        in_specs=[pl.BlockSpec((slots, tile_rows, cols), lambda i: (0, i, 0)), tile, tile, tile],
        out_specs=[tile] * 4, out_shape=[out] * 4,
        compiler_params=_params(("parallel",)),
    )(land, w, m, v)


def _sum_gains(land):
    def body(land_ref, o_ref):
        o_ref[...] = _sum_in_device_order(land_ref)

    return pl.pallas_call(
        body, name="sum_gain_grads", grid=(1,),
        in_specs=[pl.BlockSpec(land.shape, lambda i: (0, 0, 0))],
        out_specs=pl.BlockSpec(land.shape[1:], lambda i: (0, 0)),
        out_shape=jax.ShapeDtypeStruct(land.shape[1:], F32),
    )(land)


def _adamw(w, g, m, v, name):
    def body(w_ref, g_ref, m_ref, v_ref, d_ref, nm_ref, nv_ref):
        d_ref[...], nm_ref[...], nv_ref[...] = _adam_math(w_ref[...], g_ref[...], m_ref[...], v_ref[...])

    whole = pl.BlockSpec(w.shape, lambda i: (0, 0))
    out = jax.ShapeDtypeStruct(w.shape, F32)
    return pl.pallas_call(
        body, name=name, grid=(1,),
        in_specs=[whole] * 4, out_specs=[whole] * 3, out_shape=[out] * 3,
    )(w, g, m, v)


GROUP_FFN = ("w_ffn_in", "w_ffn_out")
GROUP_MIX = ("w_sb_up", "w_dil_up", "w_out")
COL_SHARDED = ("w_in", "w_sb_up", "w_dil_up", "w_ffn_in")


def _full_from_shards(name, slots):
    _, r, c = slots.shape
    if name in COL_SHARDED:
        return slots.transpose(1, 0, 2).reshape(r, N_DEV * c)
    return slots.reshape(N_DEV * r, c)


def _shards_from_full(name, full):
    rows, cols = full.shape
    if name in COL_SHARDED:
        return full.reshape(rows, N_DEV, cols // N_DEV).transpose(1, 0, 2)
    return full.reshape(N_DEV, rows // N_DEV, cols)


def _local_step(x, target, g_mix, g_ffn, g_fin, w_in, shards=None, rest=None):
    gather = lambda names, kind: None if shards is None else [(shards[n], kind) for n in names]
    scatter = lambda blocks: None if shards is None else [(t, "scatter") for t in blocks]
    landed = lambda blocks, lands: lands if lands else blocks

    w = {"w_in": w_in}
    if shards is None:
        w.update(rest)
    qkv_sb, qkv_dl, gates, u = _norm_proj(x, g_mix, w["w_in"])
    qkv_sb = qkv_sb.reshape(B_LOC, SEQ, 3 * SB_WIDTH)
    qkv_dl = qkv_dl.reshape(B_LOC, SEQ, 3 * DIL_WIDTH)
    (o_sb,), lands = _sb_fwd(qkv_sb, gather(GROUP_FFN, "gather_by_chip"))
    w.update({n: _full_from_shards(n, t) for n, t in zip(GROUP_FFN, lands)})
    o_sb = o_sb.reshape(TOK, SB_WIDTH)
    (o_dl, lse), lands = _dil_fwd(qkv_dl, gather(GROUP_MIX, "gather"))
    w.update({n: _full_from_shards(n, t) for n, t in zip(GROUP_MIX, lands)})
    o_dl = o_dl.reshape(TOK, DIL_OUT)

    x1, merged = _mix_out(x, o_sb, o_dl, gates, w["w_sb_up"], w["w_dil_up"], w["w_out"])
    loss, dx1, u2, act, dh, dx2, dg_fin, dg_ffn = _ffn_fwd_bwd(x1, target, g_ffn, g_fin, w["w_ffn_in"], w["w_ffn_out"])
    dgates, dy_sb, dy_dl, do_sb, do_dl, dsum = _mix_bwd(dx1, o_sb, o_dl, gates, w["w_sb_up"], w["w_dil_up"], w["w_out"])
    blocks = {
        "w_sb_up": _atb(o_sb, dy_sb, "grad_w_sb_up", SB_WIDTH, D_MODEL, col_blocks=N_DEV),
        "w_dil_up": _atb(o_dl.astype(BF16), dy_dl, "grad_w_dil_up", DIL_OUT, D_MODEL, col_blocks=N_DEV),
        "w_out": _shards_from_full("w_out", _atb(merged, dx1.astype(BF16), "grad_w_out", D_MODEL, D_MODEL)),
        "w_ffn_in": _shards_from_full("w_ffn_in", _atb(u2, dh, "grad_w_ffn_in", D_MODEL, D_FF)),
        "w_ffn_out": _shards_from_full("w_ffn_out", _atb(act, dx2, "grad_w_ffn_out", D_FF // 2, D_MODEL)),
    }
    grads = {}

    early, late = ("w_ffn_in",), ("w_ffn_out",) + GROUP_MIX
    early_blocks = [blocks[n] for n in early]
    (dq_sb, dk_sb, dv_sb), lands = _sb_bwd(qkv_sb, do_sb.reshape(B_LOC, SEQ, SB_WIDTH), scatter(early_blocks))
    grads.update(zip(early, landed(early_blocks, lands)))
    as_batch = lambda t: t.reshape(B_LOC, SEQ, DIL_OUT)
    late_blocks = [blocks[n] for n in late]
    d_dl, lands = _dil_bwd(qkv_dl, as_batch(do_dl), lse, as_batch(dsum), scatter(late_blocks))
    grads.update(zip(late, landed(late_blocks, lands)))
    flat = lambda t: t.reshape(TOK, -1)
    dproj = ([flat(dq_sb), flat(dk_sb), flat(dv_sb)]
             + [flat(d_dl[3 * grp + part]) for part in range(3) for grp in range(DIL_GROUPS)] + [dgates])

    w_in_blocks = _shards_from_full("w_in", _atb_pieces(u, dproj, "grad_w_in", D_MODEL // 2))
    if shards is None:
        grads["w_in"] = w_in_blocks
        send = None
    else:
        core = lax.axis_index("c").astype(jnp.int32).reshape(1)
        send = [(_pair_sum(w_in_blocks, _pair_swap(w_in_blocks), core), "scatter_by_chip")]
    (grad_x, dg_mix), lands = _proj_bwd(dproj, dx1, x, g_mix, w["w_in"], send)
    if lands:
        grads["w_in"] = lands[0]
    gain_grads = jnp.concatenate([dg_mix, dg_ffn, dg_fin], axis=0)
    return loss, grad_x, gain_grads, grads


def kernel(x, norm_mix_g, w_in, w_sb_up, w_dil_up, w_out, norm_ffn_g, w_ffn_in, w_ffn_out, norm_final_g, loss_target, m_norm_mix_g, m_w_in, m_w_sb_up, m_w_dil_up, m_w_out, m_norm_ffn_g, m_w_ffn_in, m_w_ffn_out, m_norm_final_g, v_norm_mix_g, v_w_in, v_w_sb_up, v_w_dil_up, v_w_out, v_norm_ffn_g, v_w_ffn_in, v_w_ffn_out, v_norm_final_g):
    mats = {"w_in": w_in, "w_sb_up": w_sb_up, "w_dil_up": w_dil_up, "w_out": w_out,
            "w_ffn_in": w_ffn_in, "w_ffn_out": w_ffn_out}
    moments_m = {"w_in": m_w_in, "w_sb_up": m_w_sb_up, "w_dil_up": m_w_dil_up, "w_out": m_w_out,
                 "w_ffn_in": m_w_ffn_in, "w_ffn_out": m_w_ffn_out}
    moments_v = {"w_in": v_w_in, "w_sb_up": v_w_sb_up, "w_dil_up": v_w_dil_up, "w_out": v_w_out,
                 "w_ffn_in": v_w_ffn_in, "w_ffn_out": v_w_ffn_out}
    gathered_w_in = _all_gather(w_in[0].astype(BF16), "all_gather_w_in")
    g_fin = norm_final_g.reshape(1, D_MODEL)
    loss, grad_x, gain_grads, grad_slots = _local_step(
        x.reshape(TOK, D_MODEL), loss_target.reshape(TOK, D_MODEL), norm_mix_g, norm_ffn_g, g_fin,
        _full_from_shards("w_in", gathered_w_in),
        shards={name: mats[name][0].astype(BF16) for name in GROUP_FFN + GROUP_MIX})

    gain_rows = jnp.concatenate([gain_grads, jnp.tile(loss, (1, D_MODEL // LANES)),
                                 jnp.zeros((8 - 4, D_MODEL), F32)], axis=0)
    g_gains = _sum_gains(_all_gather(gain_rows, "all_gather_gains"))

    out_g, out_d, out_m, out_v = {}, {}, {}, {}
    for name, slots in grad_slots.items():
        g, d, nm, nv = _sum_update(slots, mats[name][0], moments_m[name][0], moments_v[name][0], "update_" + name)
        out_g[name], out_d[name], out_m[name], out_v[name] = g[None], d[None], nm[None], nv[None]

    gain_w = jnp.concatenate([norm_mix_g, norm_ffn_g, g_fin], axis=0)
    gain_m = jnp.concatenate([m_norm_mix_g, m_norm_ffn_g, m_norm_final_g.reshape(1, D_MODEL)], axis=0)
    gain_v = jnp.concatenate([v_norm_mix_g, v_norm_ffn_g, v_norm_final_g.reshape(1, D_MODEL)], axis=0)
    gd, gm, gv = _adamw(gain_w, g_gains[:3], gain_m, gain_v, "adamw_gains")
    for idx, name in enumerate(("norm_mix_g", "norm_ffn_g", "norm_final_g")):
        shape = (D_MODEL,) if name == "norm_final_g" else (1, D_MODEL)
        out_g[name] = g_gains[idx].reshape(shape)
        out_d[name], out_m[name], out_v[name] = gd[idx].reshape(shape), gm[idx].reshape(shape), gv[idx].reshape(shape)

    order = ("norm_mix_g", "w_in", "w_sb_up", "w_dil_up", "w_out", "norm_ffn_g", "w_ffn_in", "w_ffn_out",
             "norm_final_g")
    return (g_gains[3, 0], grad_x.reshape(B_LOC, SEQ, D_MODEL),
            *[out_g[n] for n in order], *[out_d[n] for n in order],
            *[out_m[n] for n in order], *[out_v[n] for n in order])
```

```python
import math

import jax
import jax.numpy as jnp
from jax import lax
from jax.experimental import pallas as pl
from jax.experimental.pallas import tpu as pltpu

F32 = jnp.float32
BF16 = jnp.bfloat16

N_DEV = 8
D_MODEL = 1024
SEQ = 2048
B_LOC = 2
TOK = B_LOC * SEQ
HEAD_DIM = 64
SB_WIDTH = 512
DIL_WIDTH = 768
DIL_OUT = 256
QKV_WIDTH = 3 * SB_WIDTH + 3 * DIL_WIDTH
IN_WIDTH = QKV_WIDTH + 2 * D_MODEL
D_FF = 2816
DIL_PAIRS = ((128, 1), (512, 4), (2048, 16))
DIL_HEADS = 12
RMS_EPS = 1e-6
ALIBI_MAX_BIAS = 8.0
QK_SCALE = 1.0 / math.sqrt(HEAD_DIM)
BLK = 128
LANES = 128
NEG_BIG = -1e30

ADAM_LR = 0.001
ADAM_B1 = 0.9
ADAM_B2 = 0.999
ADAM_EPS = 1e-08
ADAM_WD = 0.01
ADAM_STEP = 10

VMEM_LIMIT = 56 * 1024 * 1024


def _dot(a, b):
    return jnp.dot(a, b, preferred_element_type=F32)


def _dot_nt(a, b):
    return lax.dot_general(a, b, (((1,), (1,)), ((), ())), preferred_element_type=F32)


def _dot_tn(a, b):
    return lax.dot_general(a, b, (((0,), (0,)), ((), ())), preferred_element_type=F32)


def _softplus(z):
    return jnp.maximum(z, 0.0) + jnp.log1p(jnp.exp(-jnp.abs(z)))


def _sigmoid(z):
    return 1.0 / (1.0 + jnp.exp(-z))


def _split_bf16(v):
    hi = v.astype(BF16)
    lo = (v - hi.astype(F32)).astype(BF16)
    return hi, lo


def _chunks(width, step=512):
    out, c = [], 0
    while c < width:
        w = min(step, width - c)
        out.append((c, w))
        c += w
    return out


def _resident(shape):
    nd = len(shape)
    return pl.BlockSpec(shape, lambda *_: (0,) * nd, pipeline_mode=pl.Buffered(1))


def _params(sem):
    return pltpu.CompilerParams(dimension_semantics=sem, vmem_limit_bytes=VMEM_LIMIT)


def _rms_fwd(x, g):
    r = lax.rsqrt(jnp.mean(x * x, axis=-1, keepdims=True) + RMS_EPS)
    n = x * r
    return n, r, n * g


def _rms_bwd(dy, n, r, g):
    dg = jnp.sum(dy * n, axis=0, keepdims=True)
    dn = dy * g
    dx = r * (dn - n * jnp.mean(dn * n, axis=-1, keepdims=True))
    return dx, dg


TM = 256


def _norm_proj(x, g, w_in):
    def body(x_ref, g_ref, w_ref, sb_ref, dl_ref, gate_ref, u_ref):
        _, _, u = _rms_fwd(x_ref[...], g_ref[...])
        u = u.astype(BF16)
        u_ref[...] = u
        for c0, w in _chunks(3 * SB_WIDTH):
            sb_ref[:, c0:c0 + w] = _dot(u, w_ref[:, c0:c0 + w]).astype(BF16)
        for c0, w in _chunks(3 * DIL_WIDTH):
            dl_ref[:, c0:c0 + w] = _dot(u, w_ref[:, 3 * SB_WIDTH + c0:3 * SB_WIDTH + c0 + w])
        for c0, w in _chunks(2 * D_MODEL):
            gate_ref[:, c0:c0 + w] = _dot(u, w_ref[:, QKV_WIDTH + c0:QKV_WIDTH + c0 + w])

    return pl.pallas_call(
        body, name="norm_proj", grid=(TOK // TM,),
        in_specs=[pl.BlockSpec((TM, D_MODEL), lambda i: (i, 0)), _resident((1, D_MODEL)),
                  _resident((D_MODEL, IN_WIDTH))],
        out_specs=[pl.BlockSpec((TM, 3 * SB_WIDTH), lambda i: (i, 0)),
                   pl.BlockSpec((TM, 3 * DIL_WIDTH), lambda i: (i, 0)),
                   pl.BlockSpec((TM, 2 * D_MODEL), lambda i: (i, 0)),
                   pl.BlockSpec((TM, D_MODEL), lambda i: (i, 0))],
        out_shape=[jax.ShapeDtypeStruct((TOK, 3 * SB_WIDTH), BF16),
                   jax.ShapeDtypeStruct((TOK, 3 * DIL_WIDTH), F32),
                   jax.ShapeDtypeStruct((TOK, 2 * D_MODEL), F32),
                   jax.ShapeDtypeStruct((TOK, D_MODEL), BF16)],
        compiler_params=_params(("parallel",)),
    )(x, g, w_in)


def _mix_out(x, o_sb, o_dl, gates, w_sb_up, w_dil_up, w_out):
    def body(x_ref, osb_ref, odl_ref, gate_ref, wsb_ref, wdl_ref, wout_ref, x1_ref, mg_ref):
        y_sb = _dot(osb_ref[...], wsb_ref[...])
        y_dl = _dot(odl_ref[...].astype(BF16), wdl_ref[...])
        merged = (_sigmoid(gate_ref[:, :D_MODEL]) * y_sb
                  + _sigmoid(gate_ref[:, D_MODEL:]) * y_dl).astype(BF16)
        mg_ref[...] = merged
        x1_ref[...] = x_ref[...] + _dot(merged, wout_ref[...])

    return pl.pallas_call(
        body, name="mix_out", grid=(TOK // TM,),
        in_specs=[pl.BlockSpec((TM, D_MODEL), lambda i: (i, 0)),
                  pl.BlockSpec((TM, SB_WIDTH), lambda i: (i, 0)),
                  pl.BlockSpec((TM, DIL_OUT), lambda i: (i, 0)),
                  pl.BlockSpec((TM, 2 * D_MODEL), lambda i: (i, 0)),
                  _resident((SB_WIDTH, D_MODEL)), _resident((DIL_OUT, D_MODEL)),
                  _resident((D_MODEL, D_MODEL))],
        out_specs=[pl.BlockSpec((TM, D_MODEL), lambda i: (i, 0)),
                   pl.BlockSpec((TM, D_MODEL), lambda i: (i, 0))],
        out_shape=[jax.ShapeDtypeStruct((TOK, D_MODEL), F32),
                   jax.ShapeDtypeStruct((TOK, D_MODEL), BF16)],
        compiler_params=_params(("parallel",)),
    )(x, o_sb, o_dl, gates, w_sb_up, w_dil_up, w_out)


FF_CHUNK = D_FF // 2


def _ffn_fwd_bwd(x1, target, g_ffn, g_fin, w_ffn_in, w_ffn_out):
    def body(x1_ref, t_ref, gffn_ref, gfin_ref, win_ref, wout_ref,
             loss_ref, dx1_ref, u2_ref, act_ref, dh_ref, dx2_ref, dgfin_ref, dgffn_ref, h_scr):
        i = pl.program_id(0)

        @pl.when(i == 0)
        def _():
            loss_ref[...] = jnp.zeros_like(loss_ref)
            dgfin_ref[...] = jnp.zeros_like(dgfin_ref)
            dgffn_ref[...] = jnp.zeros_like(dgffn_ref)

        x1 = x1_ref[...]
        g_ffn_v = gffn_ref[...]
        g_fin_v = gfin_ref[...]
        n2, r2, u2 = _rms_fwd(x1, g_ffn_v)
        u2 = u2.astype(BF16)
        u2_ref[...] = u2
        x2 = x1
        for c0 in range(0, D_FF, FF_CHUNK):
            gate = _dot(u2, win_ref[:, c0:c0 + FF_CHUNK])
            up = _dot(u2, win_ref[:, D_FF + c0:D_FF + c0 + FF_CHUNK])
            h_scr[:, c0:c0 + FF_CHUNK] = gate
            h_scr[:, D_FF + c0:D_FF + c0 + FF_CHUNK] = up
            act = (gate * _sigmoid(gate) * up).astype(BF16)
            act_ref[:, c0:c0 + FF_CHUNK] = act
            x2 = x2 + _dot(act, wout_ref[c0:c0 + FF_CHUNK, :])
        n3, r3, y = _rms_fwd(x2, g_fin_v)
        err = y - t_ref[...]
        sq = jnp.sum(jnp.sum(err * err, axis=1, keepdims=True), axis=0, keepdims=True)
        loss_ref[...] += sq * (0.5 / D_MODEL)
        dx2, dgfin = _rms_bwd(err * (1.0 / D_MODEL), n3, r3, g_fin_v)
        dgfin_ref[...] += dgfin
        dx2_b = dx2.astype(BF16)
        dx2_ref[...] = dx2_b
        du2 = jnp.zeros((TM, D_MODEL), F32)
        for c0 in range(0, D_FF, FF_CHUNK):
            gate = h_scr[:, c0:c0 + FF_CHUNK]
            up = h_scr[:, D_FF + c0:D_FF + c0 + FF_CHUNK]
            dact = _dot_nt(dx2_b, wout_ref[c0:c0 + FF_CHUNK, :])
            sg = _sigmoid(gate)
            dgate = (dact * up * (sg * (1.0 + gate * (1.0 - sg)))).astype(BF16)
            dup = (dact * (gate * sg)).astype(BF16)
            dh_ref[:, c0:c0 + FF_CHUNK] = dgate
            dh_ref[:, D_FF + c0:D_FF + c0 + FF_CHUNK] = dup
            du2 = du2 + _dot_nt(dgate, win_ref[:, c0:c0 + FF_CHUNK])
            du2 = du2 + _dot_nt(dup, win_ref[:, D_FF + c0:D_FF + c0 + FF_CHUNK])
        dx1_n, dgffn = _rms_bwd(du2, n2, r2, g_ffn_v)
        dgffn_ref[...] += dgffn
        dx1_ref[...] = dx2 + dx1_n

    tile = lambda w: pl.BlockSpec((TM, w), lambda i: (i, 0))
    acc = lambda w: pl.BlockSpec((1, w), lambda i: (0, 0))
    return pl.pallas_call(
        body, name="ffn_fwd_bwd", grid=(TOK // TM,),
        in_specs=[tile(D_MODEL), tile(D_MODEL), _resident((1, D_MODEL)), _resident((1, D_MODEL)),
                  _resident((D_MODEL, 2 * D_FF)), _resident((D_FF, D_MODEL))],
        out_specs=[acc(LANES), tile(D_MODEL), tile(D_MODEL), tile(D_FF), tile(2 * D_FF), tile(D_MODEL),
                   acc(D_MODEL), acc(D_MODEL)],
        out_shape=[jax.ShapeDtypeStruct((1, LANES), F32),
                   jax.ShapeDtypeStruct((TOK, D_MODEL), F32),
                   jax.ShapeDtypeStruct((TOK, D_MODEL), BF16),
                   jax.ShapeDtypeStruct((TOK, D_FF), BF16),
                   jax.ShapeDtypeStruct((TOK, 2 * D_FF), BF16),
                   jax.ShapeDtypeStruct((TOK, D_MODEL), BF16),
                   jax.ShapeDtypeStruct((1, D_MODEL), F32),
                   jax.ShapeDtypeStruct((1, D_MODEL), F32)],
        scratch_shapes=[pltpu.VMEM((TM, 2 * D_FF), F32)],
        compiler_params=_params(("arbitrary",)),
    )(x1, target, g_ffn, g_fin, w_ffn_in, w_ffn_out)


def _mix_bwd(dx1, o_sb, o_dl, gates, w_sb_up, w_dil_up, w_out):
    def body(dx1_ref, osb_ref, odl_ref, gate_ref, wsb_ref, wdl_ref, wout_ref,
             dgate_ref, dysb_ref, dydl_ref, dosb_ref, dodl_ref, dsum_ref):
        dmerged = _dot_nt(dx1_ref[...].astype(BF16), wout_ref[...])
        o_dl = odl_ref[...]
        y_sb = _dot(osb_ref[...], wsb_ref[...])
        y_dl = _dot(o_dl.astype(BF16), wdl_ref[...])
        s_sb = _sigmoid(gate_ref[:, :D_MODEL])
        s_dl = _sigmoid(gate_ref[:, D_MODEL:])
        dgate_ref[:, :D_MODEL] = (dmerged * y_sb * (s_sb * (1.0 - s_sb))).astype(BF16)
        dgate_ref[:, D_MODEL:] = (dmerged * y_dl * (s_dl * (1.0 - s_dl))).astype(BF16)
        dy_sb = (dmerged * s_sb).astype(BF16)
        dy_dl = (dmerged * s_dl).astype(BF16)
        dysb_ref[...] = dy_sb
        dydl_ref[...] = dy_dl
        dosb_ref[...] = _dot_nt(dy_sb, wsb_ref[...]).astype(BF16)
        do_dl = _dot_nt(dy_dl, wdl_ref[...])
        dodl_ref[...] = do_dl
        row = lax.broadcasted_iota(jnp.int32, (DIL_OUT, DIL_OUT), 0) // HEAD_DIM
        col = lax.broadcasted_iota(jnp.int32, (DIL_OUT, DIL_OUT), 1) // HEAD_DIM
        same_head = (row == col).astype(BF16)
        hi, lo = _split_bf16(do_dl * o_dl)
        dsum_ref[...] = _dot(hi, same_head) + _dot(lo, same_head)

    tile = lambda w: pl.BlockSpec((TM, w), lambda i: (i, 0))
    return pl.pallas_call(
        body, name="mix_bwd", grid=(TOK // TM,),
        in_specs=[tile(D_MODEL), tile(SB_WIDTH), tile(DIL_OUT), tile(2 * D_MODEL),
                  _resident((SB_WIDTH, D_MODEL)), _resident((DIL_OUT, D_MODEL)),
                  _resident((D_MODEL, D_MODEL))],
        out_specs=[tile(2 * D_MODEL), tile(D_MODEL), tile(D_MODEL), tile(SB_WIDTH), tile(DIL_OUT),
                   tile(DIL_OUT)],
        out_shape=[jax.ShapeDtypeStruct((TOK, 2 * D_MODEL), BF16),
                   jax.ShapeDtypeStruct((TOK, D_MODEL), BF16),
                   jax.ShapeDtypeStruct((TOK, D_MODEL), BF16),
                   jax.ShapeDtypeStruct((TOK, SB_WIDTH), BF16),
                   jax.ShapeDtypeStruct((TOK, DIL_OUT), F32),
                   jax.ShapeDtypeStruct((TOK, DIL_OUT), F32)],
        compiler_params=_params(("parallel",)),
    )(dx1, o_sb, o_dl, gates, w_sb_up, w_dil_up, w_out)


def _proj_bwd(dproj, dx1, x, g, w_in, send=None):
    widths = [p.shape[1] for p in dproj]

    def body(*refs):
        dx1_ref, x_ref, g_ref, w_ref, dx_ref, dg_ref = refs[len(widths):]

        @pl.when(pl.program_id(0) == 0)
        def _():
            dg_ref[...] = jnp.zeros_like(dg_ref)

        du = jnp.zeros((TM, D_MODEL), F32)
        c0 = 0
        for dp_ref, w in zip(refs, widths):
            du = du + _dot_nt(dp_ref[...].astype(BF16), w_ref[:, c0:c0 + w])
            c0 += w
        g_v = g_ref[...]
        n, r, _ = _rms_fwd(x_ref[...], g_v)
        dx, dg = _rms_bwd(du, n, r, g_v)
        dg_ref[...] += dg
        dx_ref[...] = dx1_ref[...] + dx

    tile = lambda w: pl.BlockSpec((TM, w), lambda i: (i, 0))
    return _call(
        body, send, name="proj_bwd", grid=(TOK // TM,),
        in_specs=[tile(w) for w in widths] + [tile(D_MODEL), tile(D_MODEL), _resident((1, D_MODEL)),
                                              _resident((D_MODEL, IN_WIDTH))],
        out_specs=[tile(D_MODEL), pl.BlockSpec((1, D_MODEL), lambda i: (0, 0))],
        out_shape=[jax.ShapeDtypeStruct((TOK, D_MODEL), F32),
                   jax.ShapeDtypeStruct((1, D_MODEL), F32)],
        scratch_shapes=[], semantics=("arbitrary",), operands=(*dproj, dx1, x, g, w_in))


def _atb_pieces(a, pieces, name, tm, tk=512):
    m = a.shape[1]
    widths = [p.shape[1] for p in pieces]
    n = sum(widths)
    nk = TOK // tk

    def body(a_ref, *refs):
        o_ref, acc_ref = refs[len(widths):]
        k = pl.program_id(1)

        @pl.when(k == 0)
        def _():
            acc_ref[...] = jnp.zeros_like(acc_ref)

        a_v = a_ref[...]
        c0 = 0
        for p_ref, w in zip(refs, widths):
            acc_ref[:, c0:c0 + w] += _dot_tn(a_v, p_ref[...].astype(BF16))
            c0 += w

        @pl.when(k == nk - 1)
        def _():
            o_ref[...] = acc_ref[...].astype(BF16)

    return pl.pallas_call(
        body, name=name, grid=(m // tm, nk),
        in_specs=[pl.BlockSpec((tk, tm), lambda i, k: (k, i))]
                 + [pl.BlockSpec((tk, w), lambda i, k: (k, 0)) for w in widths],
        out_specs=pl.BlockSpec((tm, n), lambda i, k: (i, 0)),
        out_shape=jax.ShapeDtypeStruct((m, n), BF16),
        scratch_shapes=[pltpu.VMEM((tm, n), F32)],
        compiler_params=_params(("parallel", "arbitrary")),
    )(a, *pieces)


def _atb(a, b, name, tm, tn, col_blocks=0, tk=512):
    m, n = a.shape[1], b.shape[1]
    nk = TOK // tk

    def body(a_ref, b_ref, o_ref, acc_ref):
        k = pl.program_id(2)

        @pl.when(k == 0)
        def _():
            acc_ref[...] = jnp.zeros_like(acc_ref)

        acc_ref[...] += _dot_tn(a_ref[...].astype(BF16), b_ref[...].astype(BF16))

        @pl.when(k == nk - 1)
        def _():
            if col_blocks:
                width = n // col_blocks
                for blk in range(col_blocks):
                    o_ref[blk] = acc_ref[:, blk * width:(blk + 1) * width].astype(BF16)
            else:
                o_ref[...] = acc_ref[...].astype(BF16)

    if col_blocks:
        out_spec = pl.BlockSpec((col_blocks, tm, n // col_blocks), lambda i, j, k: (0, i, 0))
        out_shape = jax.ShapeDtypeStruct((col_blocks, m, n // col_blocks), BF16)
    else:
        out_spec = pl.BlockSpec((tm, tn), lambda i, j, k: (i, j))
        out_shape = jax.ShapeDtypeStruct((m, n), BF16)
    return pl.pallas_call(
        body, name=name, grid=(m // tm, n // tn, nk),
        in_specs=[pl.BlockSpec((tk, tm), lambda i, j, k: (k, i)),
                  pl.BlockSpec((tk, tn), lambda i, j, k: (k, j))],
        out_specs=out_spec, out_shape=out_shape,
        scratch_shapes=[pltpu.VMEM((tm, tn), F32)],
        compiler_params=_params(("parallel", "parallel", "arbitrary")),
    )(a, b)


SB_PAIRS = SB_WIDTH // LANES


def _two_heads(v, lane0):
    zero = jnp.zeros_like(v)
    return jnp.where(lane0, v, zero), jnp.where(lane0, zero, v)


SB_QBLK = 256
N_SB_STEPS = SEQ // SB_QBLK


SB_KCHUNK = 2 * BLK
SB_ROWS = 2 * SB_QBLK
SB_DEAD = -104.0


def _log_keep(z):
    neg_z = -z
    return jnp.minimum(neg_z, 0.0) - jnp.log(1.0 + jnp.exp(jnp.minimum(z, neg_z)))


def _stack_heads(v, lane0):
    return jnp.concatenate(_two_heads(v, lane0), axis=0)


def _block_sums(v, tri):
    halves = (v[:, :BLK], v[:, BLK:])
    hi, lo = _split_bf16(jnp.concatenate(halves, axis=0))
    prod = _dot(jnp.concatenate([hi, lo], axis=0), tri)
    tri_sum = prod[:2 * SB_ROWS] + prod[2 * SB_ROWS:]
    sums = tuple(jnp.sum(h, axis=1, keepdims=True) for h in halves)
    return (tri_sum[:SB_ROWS], tri_sum[SB_ROWS:]), sums


def _sb_diag_mask():
    row = lax.broadcasted_iota(jnp.int32, (SB_ROWS, SB_KCHUNK), 0)
    col = lax.broadcasted_iota(jnp.int32, (SB_ROWS, SB_KCHUNK), 1)
    return col < jnp.where(row >= SB_QBLK, row - SB_QBLK, row)


def _sb_fwd(qkv, send=None):
    def body(q_ref, k_ref, v_ref, o_ref):
        i = pl.program_id(2)
        krow = lax.broadcasted_iota(jnp.int32, (BLK, BLK), 0)
        kcol = lax.broadcasted_iota(jnp.int32, (BLK, BLK), 1)
        later = (krow > kcol).astype(BF16)
        lane0 = lax.broadcasted_iota(jnp.int32, (SB_QBLK, LANES), 1) < HEAD_DIM
        q2 = _stack_heads(q_ref[0] * QK_SCALE, lane0)

        def chunk(c, carry, causal):
            acc, run = carry
            off = pl.multiple_of(c * SB_KCHUNK, SB_KCHUNK)
            z = _dot_nt(q2, k_ref[0, pl.ds(off, SB_KCHUNK), :])
            log_keep = _log_keep(z)
            if causal is not None:
                log_keep = jnp.where(causal, log_keep, 0.0)
            suffix, sums = _block_sums(log_keep, later)
            log_after = jnp.concatenate([suffix[0] + (run + sums[1]), suffix[1] + run], axis=1)
            a = jnp.exp(log_keep + z + log_after)
            if causal is not None:
                a = jnp.where(causal, a, 0.0)
            acc = acc + _dot(a.astype(BF16), v_ref[0, pl.ds(off, SB_KCHUNK), :])
            return acc, run + (sums[0] + sums[1])

        acc, run = chunk(i, (jnp.zeros((SB_ROWS, LANES), F32), jnp.zeros((SB_ROWS, 1), F32)), _sb_diag_mask())

        def live(state):
            t, _, run = state
            return jnp.logical_and(t < i, jnp.max(run) > SB_DEAD)

        def trip(state):
            t, acc, run = state
            acc, run = chunk(i - 1 - t, (acc, run), None)
            return t + 1, acc, run

        _, acc, _ = lax.while_loop(live, trip, (jnp.int32(0), acc, run))
        o_ref[0] = jnp.where(lane0, acc[:SB_QBLK], acc[SB_QBLK:]).astype(BF16)

    blk = pl.BlockSpec((1, SB_QBLK, LANES), lambda b, h, i: (b, i, h))
    return _call(
        body, send, name="sb_fwd", grid=(B_LOC, SB_PAIRS, N_SB_STEPS),
        in_specs=[blk,
                  pl.BlockSpec((1, SEQ, LANES), lambda b, h, i: (b, 0, SB_PAIRS + h)),
                  pl.BlockSpec((1, SEQ, LANES), lambda b, h, i: (b, 0, 2 * SB_PAIRS + h))],
        out_specs=[blk], out_shape=[jax.ShapeDtypeStruct((B_LOC, SEQ, SB_WIDTH), BF16)],
        scratch_shapes=[], semantics=("parallel", "parallel", "arbitrary"), operands=(qkv, qkv, qkv))


def _sb_bwd(qkv, d_o, send=None):
    def body(q_ref, k_ref, v_ref, do_ref, dq_ref, dk_ref, dv_ref, dk_acc, dv_acc, z_scr, keep_scr):
        i = pl.program_id(2)
        krow = lax.broadcasted_iota(jnp.int32, (BLK, BLK), 0)
        kcol = lax.broadcasted_iota(jnp.int32, (BLK, BLK), 1)
        upto = (krow <= kcol).astype(BF16)
        earlier = (krow < kcol).astype(BF16)
        lane0 = lax.broadcasted_iota(jnp.int32, (SB_QBLK, LANES), 1) < HEAD_DIM
        q2 = _stack_heads(q_ref[0] * QK_SCALE, lane0)
        do2 = _stack_heads(do_ref[0], lane0)

        def keep_sum(c, causal):
            off = pl.multiple_of(c * SB_KCHUNK, SB_KCHUNK)
            z = _dot_nt(q2, k_ref[0, pl.ds(off, SB_KCHUNK), :])
            log_keep = _log_keep(z)
            if causal is not None:
                log_keep = jnp.where(causal, log_keep, 0.0)
            z_scr[c] = z
            keep_scr[c] = log_keep
            return jnp.sum(log_keep, axis=1, keepdims=True)

        def live(state):
            t, run = state
            return jnp.logical_and(t < i, jnp.max(run) > SB_DEAD)

        walked, tot2 = lax.while_loop(live, lambda s: (s[0] + 1, s[1] + keep_sum(i - 1 - s[0], None)),
                                      (jnp.int32(0), keep_sum(i, _sb_diag_mask())))
        first = i - walked

        @pl.when(i == 0)
        def _():
            dk_acc[...] = jnp.zeros_like(dk_acc)
            dv_acc[...] = jnp.zeros_like(dv_acc)

        def chunk(c, carry, causal):
            dq, pre_keep, pre_e = carry
            off = pl.multiple_of(c * SB_KCHUNK, SB_KCHUNK)
            k_c = k_ref[0, pl.ds(off, SB_KCHUNK), :]
            v_c = v_ref[0, pl.ds(off, SB_KCHUNK), :]
            d_a = _dot_nt(do2, v_c)
            log_keep = keep_scr[c]
            log_beta = log_keep + z_scr[c]
            prefix, sums = _block_sums(log_keep, upto)
            inclusive = jnp.concatenate([prefix[0], prefix[1] + sums[0]], axis=1)
            a = jnp.exp(log_beta + ((tot2 - pre_keep) - inclusive))
            if causal is not None:
                a = jnp.where(causal, a, 0.0)
            e = d_a * a
            e_prefix, e_sums = _block_sums(e, earlier)
            before = jnp.concatenate([e_prefix[0] + pre_e, e_prefix[1] + (pre_e + e_sums[0])], axis=1)
            dz = e - (e + before) * jnp.exp(log_beta)
            if causal is not None:
                dz = jnp.where(causal, dz, 0.0)
            dz = dz.astype(BF16)
            dq = dq + _dot(dz, k_c)
            dk_acc[pl.ds(off, SB_KCHUNK), :] += _dot_tn(dz, q2)
            dv_acc[pl.ds(off, SB_KCHUNK), :] += _dot_tn(a.astype(BF16), do2)
            return dq, pre_keep + (sums[0] + sums[1]), pre_e + (e_sums[0] + e_sums[1])

        zero_col = jnp.zeros((SB_ROWS, 1), F32)
        carry = lax.fori_loop(first, i, lambda t, c: chunk(t, c, None),
                              (jnp.zeros((SB_ROWS, LANES), F32), zero_col, zero_col))
        dq, _, _ = chunk(i, carry, _sb_diag_mask())
        dq_ref[0] = (jnp.where(lane0, dq[:SB_QBLK], dq[SB_QBLK:]) * QK_SCALE).astype(BF16)

        @pl.when(i == N_SB_STEPS - 1)
        def _():
            dk_ref[0] = dk_acc[...].astype(BF16)
            dv_ref[0] = dv_acc[...].astype(BF16)

    blk = pl.BlockSpec((1, SB_QBLK, LANES), lambda b, h, i: (b, i, h))
    whole = lambda c: pl.BlockSpec((1, SEQ, LANES), lambda b, h, i: (b, 0, c * SB_PAIRS + h))
    out = jax.ShapeDtypeStruct((B_LOC, SEQ, SB_WIDTH), BF16)
    return _call(
        body, send, name="sb_bwd", grid=(B_LOC, SB_PAIRS, N_SB_STEPS),
        in_specs=[blk, whole(1), whole(2), blk],
        out_specs=[blk, whole(0), whole(0)],
        out_shape=[out, out, out],
        scratch_shapes=[pltpu.VMEM((SEQ, LANES), F32), pltpu.VMEM((SEQ, LANES), F32),
                        pltpu.VMEM((N_SB_STEPS, SB_ROWS, SB_KCHUNK), F32),
                        pltpu.VMEM((N_SB_STEPS, SB_ROWS, SB_KCHUNK), F32)],
        semantics=("parallel", "parallel", "arbitrary"), operands=(qkv, qkv, qkv, d_o))


DIL_GROUPS = len(DIL_PAIRS)
DIL_QBLOCKS = SEQ // BLK


def _residue_rows(j, dilation):
    length = SEQ // dilation
    return pl.ds(j, length, stride=dilation) if dilation > 1 else pl.ds(0, length)


def _gather_residues(src_ref, dst_ref, dst_off, dilation, scale=None):
    length = SEQ // dilation
    for j in range(dilation):
        v = src_ref[_residue_rows(j, dilation), :]
        if scale is not None:
            v = v * scale
        dst_ref[dst_off + j * length:dst_off + (j + 1) * length, :] = v.astype(dst_ref.dtype)


def _scatter_residues(src_ref, src_off, dst_ref, dilation):
    length = SEQ // dilation
    for j in range(dilation):
        dst_ref[_residue_rows(j, dilation), :] = (
            src_ref[src_off + j * length:src_off + (j + 1) * length, :].astype(dst_ref.dtype))


def _dil_geometry(group, pair):
    dilation = DIL_PAIRS[group][1]
    row = lax.broadcasted_iota(jnp.int32, (2 * BLK, 2 * BLK), 0)
    col = lax.broadcasted_iota(jnp.int32, (2 * BLK, 2 * BLK), 1)
    second = row >= BLK
    steps = BLK + jnp.where(second, row - BLK, row) - col
    coef = -ALIBI_MAX_BIAS / DIL_HEADS * math.log(2.0)
    first_head = float(4 * group + 1) + 2.0 * pair.astype(F32)
    slope = jnp.exp(coef * (first_head + jnp.where(second, 1.0, 0.0)))
    bias = slope * (steps * dilation).astype(F32)
    valid = jnp.logical_and(steps >= 0, steps <= BLK)
    return bias, valid, col >= BLK


def _dil_tile_scores(q2, kk, geometry, has_prev):
    bias, valid, own = geometry
    ok = jnp.logical_and(valid, jnp.logical_or(own, has_prev))
    return jnp.where(ok, _dot_nt(q2, kk) - bias, NEG_BIG)


def _head_col(v, lane_mask):
    return jnp.max(jnp.where(lane_mask, v, NEG_BIG), axis=1, keepdims=True)


def _dil_fwd(qkv, send=None):
    def body(*refs):
        ins, (o_ref, lse_ref), (qs, ks, vs, o_res, lse_res) = refs[:9], refs[9:11], refs[11:16]
        o_grp, lse_grp = refs[16:19], refs[19:22]
        pair = pl.program_id(1)
        lane0 = lax.broadcasted_iota(jnp.int32, (BLK, LANES), 1) < HEAD_DIM
        ks[0:BLK, :] = jnp.zeros((BLK, LANES), BF16)
        vs[0:BLK, :] = jnp.zeros((BLK, LANES), BF16)
        for grp, (_, dilation) in enumerate(DIL_PAIRS):
            q_ref, k_ref, v_ref = ins[3 * grp:3 * grp + 3]
            per_residue = DIL_QBLOCKS // dilation
            _gather_residues(q_ref, qs, 0, dilation, QK_SCALE)
            _gather_residues(k_ref, ks, BLK, dilation)
            _gather_residues(v_ref, vs, BLK, dilation)
            geometry = _dil_geometry(grp, pair)

            def step(blk, _):
                off = pl.multiple_of(blk * BLK, BLK)
                q2 = _stack_heads(qs[pl.ds(off, BLK), :], lane0)
                s = _dil_tile_scores(q2, ks[pl.ds(off, 2 * BLK), :], geometry, blk % per_residue != 0)
                m = jnp.max(s, axis=1, keepdims=True)
                p = jnp.exp(s - m)
                den = jnp.sum(p, axis=1, keepdims=True)
                out = _dot(p.astype(BF16), vs[pl.ds(off, 2 * BLK), :]) / den
                lse = m + jnp.log(den)
                o_res[pl.ds(off, BLK), :] = jnp.where(lane0, out[:BLK], out[BLK:])
                lse_res[pl.ds(off, BLK), :] = jnp.where(lane0, lse[:BLK], lse[BLK:])
                return 0

            lax.fori_loop(0, DIL_QBLOCKS, step, 0, unroll=4)
            _scatter_residues(o_res, 0, o_grp[grp], dilation)
            _scatter_residues(lse_res, 0, lse_grp[grp], dilation)

        for r0 in range(0, SEQ, 2 * BLK):
            rows = slice(r0, r0 + 2 * BLK)
            ls = [lse_grp[g][rows, :] for g in range(DIL_GROUPS)]
            m = jnp.maximum(jnp.maximum(ls[0], ls[1]), ls[2])
            w = [jnp.exp(l - m) for l in ls]
            den = w[0] + w[1] + w[2]
            o_ref[rows, :] = (w[0] * o_grp[0][rows, :] + w[1] * o_grp[1][rows, :] + w[2] * o_grp[2][rows, :]) / den
            lse_ref[rows, :] = m + jnp.log(den)

    def col(part, grp):
        return pl.BlockSpec((None, SEQ, LANES), lambda b, p: (b, 0, 6 * part + 2 * grp + p))

    out_spec = pl.BlockSpec((None, SEQ, LANES), lambda b, p: (b, 0, p))
    out = jax.ShapeDtypeStruct((B_LOC, SEQ, DIL_OUT), F32)
    return _call(
        body, send, name="dil_fwd", grid=(B_LOC, DIL_OUT // LANES),
        in_specs=[col(part, grp) for grp in range(DIL_GROUPS) for part in range(3)],
        out_specs=[out_spec, out_spec], out_shape=[out, out],
        scratch_shapes=[pltpu.VMEM((SEQ, LANES), BF16), pltpu.VMEM((SEQ + BLK, LANES), BF16),
                        pltpu.VMEM((SEQ + BLK, LANES), BF16), pltpu.VMEM((SEQ, LANES), F32),
                        pltpu.VMEM((SEQ, LANES), F32)] + [pltpu.VMEM((SEQ, LANES), F32)] * (2 * DIL_GROUPS),
        semantics=("parallel", "parallel"), operands=[qkv] * 9)


def _dil_bwd(qkv, d_o, lse, dsum, send=None):
    def body(*refs):
        ins, (do_ref, lse_ref, dsum_ref), outs = refs[:9], refs[9:12], refs[12:21]
        qs, ks, vs, dos, lse_res, dsum_res, dq_res, dk_acc, dv_acc = refs[21:]
        pair = pl.program_id(1)
        lane0 = lax.broadcasted_iota(jnp.int32, (BLK, LANES), 1) < HEAD_DIM
        lane1 = jnp.logical_not(lane0)
        ks[0:BLK, :] = jnp.zeros((BLK, LANES), BF16)
        vs[0:BLK, :] = jnp.zeros((BLK, LANES), BF16)
        for grp, (_, dilation) in enumerate(DIL_PAIRS):
            q_ref, k_ref, v_ref = ins[3 * grp:3 * grp + 3]
            dq_ref, dk_ref, dv_ref = outs[3 * grp:3 * grp + 3]
            per_residue = DIL_QBLOCKS // dilation
            _gather_residues(q_ref, qs, 0, dilation, QK_SCALE)
            _gather_residues(k_ref, ks, BLK, dilation)
            _gather_residues(v_ref, vs, BLK, dilation)
            _gather_residues(do_ref, dos, 0, dilation)
            _gather_residues(lse_ref, lse_res, 0, dilation)
            _gather_residues(dsum_ref, dsum_res, 0, dilation)
            dk_acc[...] = jnp.zeros_like(dk_acc)
            dv_acc[...] = jnp.zeros_like(dv_acc)
            geometry = _dil_geometry(grp, pair)

            def step(blk, _):
                off = pl.multiple_of(blk * BLK, BLK)
                q2 = _stack_heads(qs[pl.ds(off, BLK), :], lane0)
                do2 = _stack_heads(dos[pl.ds(off, BLK), :], lane0)
                kk = ks[pl.ds(off, 2 * BLK), :]
                vv = vs[pl.ds(off, 2 * BLK), :]
                lse_blk = lse_res[pl.ds(off, BLK), :]
                dsum_blk = dsum_res[pl.ds(off, BLK), :]
                lse2 = jnp.concatenate([_head_col(lse_blk, lane0), _head_col(lse_blk, lane1)], axis=0)
                dsum2 = jnp.concatenate([_head_col(dsum_blk, lane0), _head_col(dsum_blk, lane1)], axis=0)
                s = _dil_tile_scores(q2, kk, geometry, blk % per_residue != 0)
                p = jnp.exp(s - lse2)
                ds = (p * (_dot_nt(do2, vv) - dsum2)).astype(BF16)
                dq2 = _dot(ds, kk)
                dq_res[pl.ds(off, BLK), :] = jnp.where(lane0, dq2[:BLK], dq2[BLK:]) * QK_SCALE
                dk_acc[pl.ds(off, 2 * BLK), :] += _dot_tn(ds, q2)
                dv_acc[pl.ds(off, 2 * BLK), :] += _dot_tn(p.astype(BF16), do2)
                return 0

            lax.fori_loop(0, DIL_QBLOCKS, step, 0, unroll=4)
            _scatter_residues(dq_res, 0, dq_ref, dilation)
            _scatter_residues(dk_acc, BLK, dk_ref, dilation)
            _scatter_residues(dv_acc, BLK, dv_ref, dilation)

    def col(part, grp):
        return pl.BlockSpec((None, SEQ, LANES), lambda b, p: (b, 0, 6 * part + 2 * grp + p))

    slot = pl.BlockSpec((None, SEQ, LANES), lambda b, p: (b, 0, p))
    out = jax.ShapeDtypeStruct((B_LOC, SEQ, DIL_OUT), F32)
    return _call(
        body, send, name="dil_bwd", grid=(B_LOC, DIL_OUT // LANES),
        in_specs=[col(part, grp) for grp in range(DIL_GROUPS) for part in range(3)] + [slot] * 3,
        out_specs=[slot] * 9, out_shape=[out] * 9,
        scratch_shapes=[pltpu.VMEM((SEQ, LANES), BF16), pltpu.VMEM((SEQ + BLK, LANES), BF16),
                        pltpu.VMEM((SEQ + BLK, LANES), BF16), pltpu.VMEM((SEQ, LANES), BF16),
                        pltpu.VMEM((SEQ, LANES), F32), pltpu.VMEM((SEQ, LANES), F32),
                        pltpu.VMEM((SEQ, LANES), F32), pltpu.VMEM((SEQ + BLK, LANES), F32),
                        pltpu.VMEM((SEQ + BLK, LANES), F32)],
        semantics=("parallel", "parallel"), operands=[qkv] * 9 + [d_o, lse, dsum])


def _peers():
    x, y, c = lax.axis_index("x"), lax.axis_index("y"), lax.axis_index("c")
    me = 4 * x + 2 * y + c
    peers = []
    for mask in range(1, N_DEV):
        px = 1 - x if mask & 4 else x
        py = 1 - y if mask & 2 else y
        pc = 1 - c if mask & 1 else c
        peers.append(((px, py, pc), 4 * px + 2 * py + pc))
    return me, peers


def _all_gather(shard, name):
    def body(src_ref, out_ref, send_sems, recv_sems, local_sem):
        x, y, c = lax.axis_index("x"), lax.axis_index("y"), lax.axis_index("c")
        sibling = (x, y, 1 - c)
        chips = [(1 - x, y), (x, 1 - y), (1 - x, 1 - y)]

        def slot(px, py, pc):
            return out_ref.at[4 * px + 2 * py + pc]

        def copy(k, block, to, src=None):
            return pltpu.make_async_remote_copy(
                src_ref=slot(*block) if src is None else src, dst_ref=slot(*block),
                send_sem=send_sems.at[k], recv_sem=recv_sems.at[k], device_id=to,
                device_id_type=pl.DeviceIdType.MESH)

        mine = pltpu.make_async_copy(src_ref, slot(x, y, c), local_sem)
        mine.start()
        first = [copy(0, (x, y, c), sibling, src=src_ref)]
        first += [copy(1 + j, (x, y, c), (*chip, c), src=src_ref) for j, chip in enumerate(chips)]
        for cp in first:
            cp.start()
        passed = [copy(4 + j, (*chip, c), sibling) for j, chip in enumerate(chips)]
        for j, chip in enumerate(chips):
            copy(1 + j, (*chip, c), (x, y, c)).wait_recv()
            passed[j].start()
        copy(0, sibling, (x, y, c)).wait_recv()
        for j, chip in enumerate(chips):
            copy(4 + j, (*chip, 1 - c), (x, y, c)).wait_recv()
        for cp in first + passed:
            cp.wait_send()
        mine.wait()

    return pl.pallas_call(
        body, name=name,
        in_specs=[pl.BlockSpec(memory_space=pl.ANY)],
        out_specs=pl.BlockSpec(memory_space=pl.ANY),
        out_shape=jax.ShapeDtypeStruct((N_DEV,) + shard.shape, shard.dtype),
        scratch_shapes=[pltpu.SemaphoreType.DMA((N_DEV - 1,)), pltpu.SemaphoreType.DMA((N_DEV - 1,)),
                        pltpu.SemaphoreType.DMA],
    )(shard)


def _call(body, send, *, name, grid, in_specs, out_specs, out_shape, scratch_shapes, semantics, operands):
    if send is None:
        return pl.pallas_call(
            body, name=name, grid=grid, in_specs=in_specs, out_specs=out_specs, out_shape=out_shape,
            scratch_shapes=scratch_shapes, compiler_params=_params(semantics))(*operands), []
    srcs, kinds = [s for s, _ in send], [k for _, k in send]
    n, n_in, n_out, n_scr = len(srcs), len(in_specs), len(out_specs), len(scratch_shapes)
    steps = math.prod(grid)
    relay_step = (3 * steps) // 4

    def plan(refs):
        src_refs, land_refs = refs[n_in:n_in + n], refs[n_in + n + n_out:n_in + 2 * n + n_out]
        send_sems, recv_sems, local_sems = refs[-3:]
        x, y, c = lax.axis_index("x"), lax.axis_index("y"), lax.axis_index("c")
        me, peers = _peers()
        first, relayed_in, relayed_out, arrivals, sends, own = [], [], [], [], [], []
        for a, kind in enumerate(kinds):
            def copy(k, src, dst_slot, to):
                return pltpu.make_async_remote_copy(
                    src_ref=src, dst_ref=land_refs[a].at[dst_slot], send_sem=send_sems.at[a * (N_DEV - 1) + k],
                    recv_sem=recv_sems.at[a * (N_DEV - 1) + k], device_id=to, device_id_type=pl.DeviceIdType.MESH)

            if kind == "gather_by_chip":
                idx = lambda px, py, pc: 4 * px + 2 * py + pc
                chips = [(1 - x, y), (x, 1 - y), (1 - x, 1 - y)]
                mine = [copy(0, src_refs[a], me, (x, y, 1 - c))]
                arrivals.append(copy(0, src_refs[a], idx(x, y, 1 - c), (x, y, 1 - c)))
                for j, (px, py) in enumerate(chips):
                    mine.append(copy(1 + j, src_refs[a], me, (px, py, c)))
                    relayed_in.append(copy(1 + j, src_refs[a], idx(px, py, c), (px, py, c)))
                    relayed_out.append(copy(4 + j, land_refs[a].at[idx(px, py, c)], idx(px, py, c), (x, y, 1 - c)))
                    arrivals.append(copy(4 + j, src_refs[a], idx(px, py, 1 - c), (x, y, 1 - c)))
                first += mine
                sends += mine + relayed_out[-3:]
                own.append(pltpu.make_async_copy(src_refs[a], land_refs[a].at[me], local_sems.at[a]))
            elif kind == "scatter_by_chip":
                for k, (px, py) in enumerate([(1 - x, y), (x, 1 - y), (1 - x, 1 - y)]):
                    first.append(copy(k, src_refs[a].at[2 * px + py], 2 * x + y, (px, py, c)))
                    arrivals.append(copy(k, src_refs[a].at[2 * px + py], 2 * px + py, (px, py, c)))
                sends += first[-3:]
                own.append(pltpu.make_async_copy(src_refs[a].at[2 * x + y], land_refs[a].at[2 * x + y],
                                                 local_sems.at[a]))
            else:
                part = (lambda i: src_refs[a].at[i]) if kind == "scatter" else (lambda i: src_refs[a])
                for k, (peer, peer_idx) in enumerate(peers):
                    first.append(copy(k, part(peer_idx), me, peer))
                    arrivals.append(copy(k, part(peer_idx), peer_idx, peer))
                sends += first[-(N_DEV - 1):]
                own.append(pltpu.make_async_copy(part(me), land_refs[a].at[me], local_sems.at[a]))
        return first, relayed_in, relayed_out, arrivals, sends, own

    def wrapped(*refs):
        step = 0
        for axis, size in enumerate(grid):
            step = step * size + pl.program_id(axis)

        @pl.when(step == 0)
        def _():
            first, _, _, _, _, own = plan(refs)
            for cp in first + own:
                cp.start()

        if "gather_by_chip" in kinds:
            @pl.when(step == relay_step)
            def _():
                _, relayed_in, relayed_out, _, _, _ = plan(refs)
                for cp_in, cp_out in zip(relayed_in, relayed_out):
                    cp_in.wait_recv()
                    cp_out.start()

        body(*refs[:n_in], *refs[n_in + n:n_in + n + n_out], *refs[n_in + 2 * n + n_out:n_in + 2 * n + n_out + n_scr])

        @pl.when(step == steps - 1)
        def _():
            _, _, _, arrivals, sends, own = plan(refs)
            for cp in arrivals:
                cp.wait_recv()
            for cp in sends:
                cp.wait_send()
            for cp in own:
                cp.wait()

    anywhere = pl.BlockSpec(memory_space=pl.ANY)
    lands = [jax.ShapeDtypeStruct((N_DEV // 2 if k == "scatter_by_chip" else N_DEV,) + s.shape[-2:], s.dtype)
             for s, k in send]
    out = pl.pallas_call(
        wrapped, name=name, grid=grid,
        in_specs=list(in_specs) + [anywhere] * n, out_specs=list(out_specs) + [anywhere] * n,
        out_shape=list(out_shape) + lands,
        scratch_shapes=list(scratch_shapes) + [pltpu.SemaphoreType.DMA((n * (N_DEV - 1),)),
                                               pltpu.SemaphoreType.DMA((n * (N_DEV - 1),)),
                                               pltpu.SemaphoreType.DMA((n,))],
        compiler_params=_params(("arbitrary",) * len(grid)),
    )(*operands, *srcs)
    return out[:n_out], list(out[n_out:])


def _pair_swap(blocks):
    def body(src_ref, out_ref, send_sems, recv_sems):
        x, y, c = lax.axis_index("x"), lax.axis_index("y"), lax.axis_index("c")
        copies = [pltpu.make_async_remote_copy(
            src_ref=src_ref.at[2 * chip + (1 - c)], dst_ref=out_ref.at[chip], send_sem=send_sems.at[chip],
            recv_sem=recv_sems.at[chip], device_id=(x, y, 1 - c), device_id_type=pl.DeviceIdType.MESH)
            for chip in range(N_DEV // 2)]
        for cp in copies:
            cp.start()
        for cp in copies:
            cp.wait()

    return pl.pallas_call(
        body, name="pair_swap_grad_w_in",
        in_specs=[pl.BlockSpec(memory_space=pl.ANY)], out_specs=pl.BlockSpec(memory_space=pl.ANY),
        out_shape=jax.ShapeDtypeStruct((N_DEV // 2,) + blocks.shape[1:], blocks.dtype),
        scratch_shapes=[pltpu.SemaphoreType.DMA((N_DEV // 2,)), pltpu.SemaphoreType.DMA((N_DEV // 2,))],
    )(blocks)


def _pair_sum(blocks, swapped, core):
    _, rows, cols = swapped.shape
    tile_rows = _row_tile(rows)

    def body(core_ref, mine_ref, theirs_ref, o_ref):
        o_ref[...] = (mine_ref[...].astype(F32) + theirs_ref[...].astype(F32)).astype(o_ref.dtype)

    return pl.pallas_call(
        body, name="pair_sum_grad_w_in",
        grid_spec=pltpu.PrefetchScalarGridSpec(
            num_scalar_prefetch=1, grid=(N_DEV // 2, rows // tile_rows),
            in_specs=[pl.BlockSpec((None, tile_rows, cols), lambda j, i, core_ref: (2 * j + core_ref[0], i, 0)),
                      pl.BlockSpec((None, tile_rows, cols), lambda j, i, core_ref: (j, i, 0))],
            out_specs=pl.BlockSpec((None, tile_rows, cols), lambda j, i, core_ref: (j, i, 0))),
        out_shape=jax.ShapeDtypeStruct(swapped.shape, swapped.dtype),
        compiler_params=_params(("parallel", "parallel")),
    )(core, blocks, swapped)


def _sum_in_device_order(land_ref):
    acc = land_ref[0].astype(F32)
    for j in range(1, land_ref.shape[0]):
        acc = acc + land_ref[j].astype(F32)
    return acc


def _adam_math(w, g, m, v):
    c1 = 1.0 - ADAM_B1 ** ADAM_STEP
    c2 = 1.0 - ADAM_B2 ** ADAM_STEP
    m_new = ADAM_B1 * m + (1.0 - ADAM_B1) * g
    v_new = ADAM_B2 * v + (1.0 - ADAM_B2) * (g * g)
    delta = -ADAM_LR * ((m_new / c1) / (jnp.sqrt(v_new / c2) + ADAM_EPS) + ADAM_WD * w)
    return delta, m_new, v_new


def _row_tile(rows):
    return max(t for t in range(8, 257, 8) if rows % t == 0) if rows % 8 == 0 else rows


def _sum_update(land, w, m, v, name):
    slots, rows, cols = land.shape
    tile_rows = _row_tile(rows)

    def body(land_ref, w_ref, m_ref, v_ref, g_ref, d_ref, nm_ref, nv_ref):
        g = _sum_in_device_order(land_ref)
        g_ref[...] = g
        d_ref[...], nm_ref[...], nv_ref[...] = _adam_math(w_ref[...], g, m_ref[...], v_ref[...])

    tile = pl.BlockSpec((None, tile_rows, cols), lambda i: (0, i, 0))
    out = jax.ShapeDtypeStruct((1, rows, cols), F32)
    return pl.pallas_call(
        body, name=name, grid=(rows // tile_rows,),
        in_specs=[pl.BlockSpec((slots, tile_rows, cols), lambda i: (0, i, 0)), tile, tile, tile],
        out_specs=[tile] * 4, out_shape=[out] * 4,
        compiler_params=_params(("parallel",)),
    )(land, w, m, v)


def _sum_gains(land):
    def body(land_ref, o_ref):
        o_ref[...] = _sum_in_device_order(land_ref)

    return pl.pallas_call(
        body, name="sum_gain_grads", grid=(1,),
        in_specs=[pl.BlockSpec(land.shape, lambda i: (0, 0, 0))],
        out_specs=pl.BlockSpec(land.shape[1:], lambda i: (0, 0)),
        out_shape=jax.ShapeDtypeStruct(land.shape[1:], F32),
    )(land)


def _adamw(w, g, m, v, name):
    def body(w_ref, g_ref, m_ref, v_ref, d_ref, nm_ref, nv_ref):
        d_ref[...], nm_ref[...], nv_ref[...] = _adam_math(w_ref[...], g_ref[...], m_ref[...], v_ref[...])

    whole = pl.BlockSpec(w.shape, lambda i: (0, 0))
    out = jax.ShapeDtypeStruct(w.shape, F32)
    return pl.pallas_call(
        body, name=name, grid=(1,),
        in_specs=[whole] * 4, out_specs=[whole] * 3, out_shape=[out] * 3,
    )(w, g, m, v)


GROUP_FFN = ("w_ffn_in", "w_ffn_out")
GROUP_MIX = ("w_sb_up", "w_dil_up", "w_out")
COL_SHARDED = ("w_in", "w_sb_up", "w_dil_up", "w_ffn_in")


def _full_from_shards(name, slots):
    _, r, c = slots.shape
    if name in COL_SHARDED:
        return slots.transpose(1, 0, 2).reshape(r, N_DEV * c)
    return slots.reshape(N_DEV * r, c)


def _shards_from_full(name, full):
    rows, cols = full.shape
    if name in COL_SHARDED:
        return full.reshape(rows, N_DEV, cols // N_DEV).transpose(1, 0, 2)
    return full.reshape(N_DEV, rows // N_DEV, cols)


def _local_step(x, target, g_mix, g_ffn, g_fin, w_in, shards=None, rest=None):
    gather = lambda names, kind: None if shards is None else [(shards[n], kind) for n in names]
    scatter = lambda blocks: None if shards is None else [(t, "scatter") for t in blocks]
    landed = lambda blocks, lands: lands if lands else blocks

    w = {"w_in": w_in}
    if shards is None:
        w.update(rest)
    qkv_sb, qkv_dl, gates, u = _norm_proj(x, g_mix, w["w_in"])
    qkv_sb = qkv_sb.reshape(B_LOC, SEQ, 3 * SB_WIDTH)
    qkv_dl = qkv_dl.reshape(B_LOC, SEQ, 3 * DIL_WIDTH)
    (o_sb,), lands = _sb_fwd(qkv_sb, gather(GROUP_FFN, "gather_by_chip"))
    w.update({n: _full_from_shards(n, t) for n, t in zip(GROUP_FFN, lands)})
    o_sb = o_sb.reshape(TOK, SB_WIDTH)
    (o_dl, lse), lands = _dil_fwd(qkv_dl, gather(GROUP_MIX, "gather"))
    w.update({n: _full_from_shards(n, t) for n, t in zip(GROUP_MIX, lands)})
    o_dl = o_dl.reshape(TOK, DIL_OUT)

    x1, merged = _mix_out(x, o_sb, o_dl, gates, w["w_sb_up"], w["w_dil_up"], w["w_out"])
    loss, dx1, u2, act, dh, dx2, dg_fin, dg_ffn = _ffn_fwd_bwd(x1, target, g_ffn, g_fin, w["w_ffn_in"], w["w_ffn_out"])
    dgates, dy_sb, dy_dl, do_sb, do_dl, dsum = _mix_bwd(dx1, o_sb, o_dl, gates, w["w_sb_up"], w["w_dil_up"], w["w_out"])
    blocks = {
        "w_sb_up": _atb(o_sb, dy_sb, "grad_w_sb_up", SB_WIDTH, D_MODEL, col_blocks=N_DEV),
        "w_dil_up": _atb(o_dl, dy_dl, "grad_w_dil_up", DIL_OUT, D_MODEL, col_blocks=N_DEV),
        "w_out": _shards_from_full("w_out", _atb(merged, dx1, "grad_w_out", D_MODEL, D_MODEL)),
        "w_ffn_in": _shards_from_full("w_ffn_in", _atb(u2, dh, "grad_w_ffn_in", D_MODEL, D_FF)),
        "w_ffn_out": _shards_from_full("w_ffn_out", _atb(act, dx2, "grad_w_ffn_out", D_FF // 2, D_MODEL)),
    }
    grads = {}

    early, late = ("w_ffn_in",), ("w_ffn_out",) + GROUP_MIX
    early_blocks = [blocks[n] for n in early]
    (dq_sb, dk_sb, dv_sb), lands = _sb_bwd(qkv_sb, do_sb.reshape(B_LOC, SEQ, SB_WIDTH), scatter(early_blocks))
    grads.update(zip(early, landed(early_blocks, lands)))
    as_batch = lambda t: t.reshape(B_LOC, SEQ, DIL_OUT)
    late_blocks = [blocks[n] for n in late]
    d_dl, lands = _dil_bwd(qkv_dl, as_batch(do_dl), lse, as_batch(dsum), scatter(late_blocks))
    grads.update(zip(late, landed(late_blocks, lands)))
    flat = lambda t: t.reshape(TOK, -1)
    dproj = ([flat(dq_sb), flat(dk_sb), flat(dv_sb)]
             + [flat(d_dl[3 * grp + part]) for part in range(3) for grp in range(DIL_GROUPS)] + [dgates])

    w_in_blocks = _shards_from_full("w_in", _atb_pieces(u, dproj, "grad_w_in", D_MODEL // 2))
    if shards is None:
        grads["w_in"] = w_in_blocks
        send = None
    else:
        core = lax.axis_index("c").astype(jnp.int32).reshape(1)
        send = [(_pair_sum(w_in_blocks, _pair_swap(w_in_blocks), core), "scatter_by_chip")]
    (grad_x, dg_mix), lands = _proj_bwd(dproj, dx1, x, g_mix, w["w_in"], send)
    if lands:
        grads["w_in"] = lands[0]
    gain_grads = jnp.concatenate([dg_mix, dg_ffn, dg_fin], axis=0)
    return loss, grad_x, gain_grads, grads


def kernel(x, norm_mix_g, w_in, w_sb_up, w_dil_up, w_out, norm_ffn_g, w_ffn_in, w_ffn_out, norm_final_g, loss_target, m_norm_mix_g, m_w_in, m_w_sb_up, m_w_dil_up, m_w_out, m_norm_ffn_g, m_w_ffn_in, m_w_ffn_out, m_norm_final_g, v_norm_mix_g, v_w_in, v_w_sb_up, v_w_dil_up, v_w_out, v_norm_ffn_g, v_w_ffn_in, v_w_ffn_out, v_norm_final_g):
    mats = {"w_in": w_in, "w_sb_up": w_sb_up, "w_dil_up": w_dil_up, "w_out": w_out,
            "w_ffn_in": w_ffn_in, "w_ffn_out": w_ffn_out}
    moments_m = {"w_in": m_w_in, "w_sb_up": m_w_sb_up, "w_dil_up": m_w_dil_up, "w_out": m_w_out,
                 "w_ffn_in": m_w_ffn_in, "w_ffn_out": m_w_ffn_out}
    moments_v = {"w_in": v_w_in, "w_sb_up": v_w_sb_up, "w_dil_up": v_w_dil_up, "w_out": v_w_out,
                 "w_ffn_in": v_w_ffn_in, "w_ffn_out": v_w_ffn_out}
    gathered_w_in = _all_gather(w_in[0].astype(BF16), "all_gather_w_in")
    g_fin = norm_final_g.reshape(1, D_MODEL)
    loss, grad_x, gain_grads, grad_slots = _local_step(
        x.reshape(TOK, D_MODEL), loss_target.reshape(TOK, D_MODEL), norm_mix_g, norm_ffn_g, g_fin,
        _full_from_shards("w_in", gathered_w_in),
        shards={name: mats[name][0].astype(BF16) for name in GROUP_FFN + GROUP_MIX})

    gain_rows = jnp.concatenate([gain_grads, jnp.tile(loss, (1, D_MODEL // LANES)),
                                 jnp.zeros((8 - 4, D_MODEL), F32)], axis=0)
    g_gains = _sum_gains(_all_gather(gain_rows, "all_gather_gains"))

    out_g, out_d, out_m, out_v = {}, {}, {}, {}
    for name, slots in grad_slots.items():
        out_g[name], out_d[name], out_m[name], out_v[name] = _sum_update(
            slots, mats[name], moments_m[name], moments_v[name], "update_" + name)

    gain_w = jnp.concatenate([norm_mix_g, norm_ffn_g, g_fin], axis=0)
    gain_m = jnp.concatenate([m_norm_mix_g, m_norm_ffn_g, m_norm_final_g.reshape(1, D_MODEL)], axis=0)
    gain_v = jnp.concatenate([v_norm_mix_g, v_norm_ffn_g, v_norm_final_g.reshape(1, D_MODEL)], axis=0)
    gd, gm, gv = _adamw(gain_w, g_gains[:3], gain_m, gain_v, "adamw_gains")
    for idx, name in enumerate(("norm_mix_g", "norm_ffn_g", "norm_final_g")):
        shape = (D_MODEL,) if name == "norm_final_g" else (1, D_MODEL)
        out_g[name] = g_gains[idx].reshape(shape)
        out_d[name], out_m[name], out_v[name] = gd[idx].reshape(shape), gm[idx].reshape(shape), gv[idx].reshape(shape)

    order = ("norm_mix_g", "w_in", "w_sb_up", "w_dil_up", "w_out", "norm_ffn_g", "w_ffn_in", "w_ffn_out",
             "norm_final_g")
    return (g_gains[3, 0], grad_x.reshape(B_LOC, SEQ, D_MODEL),
            *[out_g[n] for n in order], *[out_d[n] for n in order],
            *[out_m[n] for n in order], *[out_v[n] for n in order])
```

```python
import math

import jax
import jax.numpy as jnp
from jax import lax
from jax.experimental import pallas as pl
from jax.experimental.pallas import tpu as pltpu

F32 = jnp.float32
BF16 = jnp.bfloat16

N_DEV = 8
D_MODEL = 1024
SEQ = 2048
B_LOC = 2
TOK = B_LOC * SEQ
HEAD_DIM = 64
SB_WIDTH = 512
DIL_WIDTH = 768
DIL_OUT = 256
QKV_WIDTH = 3 * SB_WIDTH + 3 * DIL_WIDTH
IN_WIDTH = QKV_WIDTH + 2 * D_MODEL
D_FF = 2816
DIL_PAIRS = ((128, 1), (512, 4), (2048, 16))
DIL_HEADS = 12
RMS_EPS = 1e-6
ALIBI_MAX_BIAS = 8.0
QK_SCALE = 1.0 / math.sqrt(HEAD_DIM)
BLK = 128
LANES = 128
NEG_BIG = -1e30

ADAM_LR = 0.001
ADAM_B1 = 0.9
ADAM_B2 = 0.999
ADAM_EPS = 1e-08
ADAM_WD = 0.01
ADAM_STEP = 10

VMEM_LIMIT = 56 * 1024 * 1024


def _dot(a, b):
    return jnp.dot(a, b, preferred_element_type=F32)


def _dot_nt(a, b):
    return lax.dot_general(a, b, (((1,), (1,)), ((), ())), preferred_element_type=F32)


def _dot_tn(a, b):
    return lax.dot_general(a, b, (((0,), (0,)), ((), ())), preferred_element_type=F32)


def _softplus(z):
    return jnp.maximum(z, 0.0) + jnp.log1p(jnp.exp(-jnp.abs(z)))


def _sigmoid(z):
    return 1.0 / (1.0 + jnp.exp(-z))


def _split_bf16(v):
    hi = v.astype(BF16)
    lo = (v - hi.astype(F32)).astype(BF16)
    return hi, lo


def _chunks(width, step=512):
    out, c = [], 0
    while c < width:
        w = min(step, width - c)
        out.append((c, w))
        c += w
    return out


def _resident(shape):
    nd = len(shape)
    return pl.BlockSpec(shape, lambda *_: (0,) * nd, pipeline_mode=pl.Buffered(1))


def _params(sem):
    return pltpu.CompilerParams(dimension_semantics=sem, vmem_limit_bytes=VMEM_LIMIT)


def _rms_fwd(x, g):
    r = lax.rsqrt(jnp.mean(x * x, axis=-1, keepdims=True) + RMS_EPS)
    n = x * r
    return n, r, n * g


def _rms_bwd(dy, n, r, g):
    dg = jnp.sum(dy * n, axis=0, keepdims=True)
    dn = dy * g
    dx = r * (dn - n * jnp.mean(dn * n, axis=-1, keepdims=True))
    return dx, dg


TM = 256


def _norm_proj(x, g, w_in):
    def body(x_ref, g_ref, w_ref, sb_ref, dl_ref, gate_ref, u_ref):
        _, _, u = _rms_fwd(x_ref[...], g_ref[...])
        u = u.astype(BF16)
        u_ref[...] = u
        for c0, w in _chunks(3 * SB_WIDTH):
            sb_ref[:, c0:c0 + w] = _dot(u, w_ref[:, c0:c0 + w]).astype(BF16)
        for c0, w in _chunks(3 * DIL_WIDTH):
            dl_ref[:, c0:c0 + w] = _dot(u, w_ref[:, 3 * SB_WIDTH + c0:3 * SB_WIDTH + c0 + w])
        for c0, w in _chunks(2 * D_MODEL):
            gate_ref[:, c0:c0 + w] = _dot(u, w_ref[:, QKV_WIDTH + c0:QKV_WIDTH + c0 + w])

    return pl.pallas_call(
        body, name="norm_proj", grid=(TOK // TM,),
        in_specs=[pl.BlockSpec((TM, D_MODEL), lambda i: (i, 0)), _resident((1, D_MODEL)),
                  _resident((D_MODEL, IN_WIDTH))],
        out_specs=[pl.BlockSpec((TM, 3 * SB_WIDTH), lambda i: (i, 0)),
                   pl.BlockSpec((TM, 3 * DIL_WIDTH), lambda i: (i, 0)),
                   pl.BlockSpec((TM, 2 * D_MODEL), lambda i: (i, 0)),
                   pl.BlockSpec((TM, D_MODEL), lambda i: (i, 0))],
        out_shape=[jax.ShapeDtypeStruct((TOK, 3 * SB_WIDTH), BF16),
                   jax.ShapeDtypeStruct((TOK, 3 * DIL_WIDTH), F32),
                   jax.ShapeDtypeStruct((TOK, 2 * D_MODEL), F32),
                   jax.ShapeDtypeStruct((TOK, D_MODEL), BF16)],
        compiler_params=_params(("parallel",)),
    )(x, g, w_in)


def _mix_out(x, o_sb, o_dl, gates, w_sb_up, w_dil_up, w_out):
    def body(x_ref, osb_ref, odl_ref, gate_ref, wsb_ref, wdl_ref, wout_ref, x1_ref, mg_ref):
        y_sb = _dot(osb_ref[...], wsb_ref[...])
        y_dl = _dot(odl_ref[...].astype(BF16), wdl_ref[...])
        merged = (_sigmoid(gate_ref[:, :D_MODEL]) * y_sb
                  + _sigmoid(gate_ref[:, D_MODEL:]) * y_dl).astype(BF16)
        mg_ref[...] = merged
        x1_ref[...] = x_ref[...] + _dot(merged, wout_ref[...])

    return pl.pallas_call(
        body, name="mix_out", grid=(TOK // TM,),
        in_specs=[pl.BlockSpec((TM, D_MODEL), lambda i: (i, 0)),
                  pl.BlockSpec((TM, SB_WIDTH), lambda i: (i, 0)),
                  pl.BlockSpec((TM, DIL_OUT), lambda i: (i, 0)),
                  pl.BlockSpec((TM, 2 * D_MODEL), lambda i: (i, 0)),
                  _resident((SB_WIDTH, D_MODEL)), _resident((DIL_OUT, D_MODEL)),
                  _resident((D_MODEL, D_MODEL))],
        out_specs=[pl.BlockSpec((TM, D_MODEL), lambda i: (i, 0)),
                   pl.BlockSpec((TM, D_MODEL), lambda i: (i, 0))],
        out_shape=[jax.ShapeDtypeStruct((TOK, D_MODEL), F32),
                   jax.ShapeDtypeStruct((TOK, D_MODEL), BF16)],
        compiler_params=_params(("parallel",)),
    )(x, o_sb, o_dl, gates, w_sb_up, w_dil_up, w_out)


FF_SHARD = 2 * D_FF // N_DEV
FF_PAIRS = N_DEV // 2


def _ffn_fwd_bwd(x1, target, g_ffn, g_fin, w_ffn_in, w_ffn_out):
    def body(x1_ref, t_ref, gffn_ref, gfin_ref, win_ref, wout_ref,
             loss_ref, dx1_ref, u2_ref, act_ref, dh_ref, dx2_ref, dgfin_ref, dgffn_ref, h_scr):
        i = pl.program_id(0)

        @pl.when(i == 0)
        def _():
            loss_ref[...] = jnp.zeros_like(loss_ref)
            dgfin_ref[...] = jnp.zeros_like(dgfin_ref)
            dgffn_ref[...] = jnp.zeros_like(dgffn_ref)

        x1 = x1_ref[...]
        g_ffn_v = gffn_ref[...]
        g_fin_v = gfin_ref[...]
        n2, r2, u2 = _rms_fwd(x1, g_ffn_v)
        u2 = u2.astype(BF16)
        u2_ref[...] = u2
        x2 = x1
        for r in range(FF_PAIRS):
            gate = _dot(u2, win_ref[r])
            up = _dot(u2, win_ref[r + FF_PAIRS])
            h_scr[r] = gate
            h_scr[r + FF_PAIRS] = up
            act = (gate * _sigmoid(gate) * up).astype(BF16)
            act_ref[r] = act
            x2 = x2 + _dot(act, wout_ref[r * FF_SHARD:(r + 1) * FF_SHARD, :])
        n3, r3, y = _rms_fwd(x2, g_fin_v)
        err = y - t_ref[...]
        sq = jnp.sum(jnp.sum(err * err, axis=1, keepdims=True), axis=0, keepdims=True)
        loss_ref[...] += sq * (0.5 / D_MODEL)
        dx2, dgfin = _rms_bwd(err * (1.0 / D_MODEL), n3, r3, g_fin_v)
        dgfin_ref[...] += dgfin
        dx2_b = dx2.astype(BF16)
        dx2_ref[...] = dx2_b
        du2 = jnp.zeros((TM, D_MODEL), F32)
        for r in range(FF_PAIRS):
            gate = h_scr[r]
            up = h_scr[r + FF_PAIRS]
            dact = _dot_nt(dx2_b, wout_ref[r * FF_SHARD:(r + 1) * FF_SHARD, :])
            sg = _sigmoid(gate)
            dgate = (dact * up * (sg * (1.0 + gate * (1.0 - sg)))).astype(BF16)
            dup = (dact * (gate * sg)).astype(BF16)
            dh_ref[r] = dgate
            dh_ref[r + FF_PAIRS] = dup
            du2 = du2 + _dot_nt(dgate, win_ref[r])
            du2 = du2 + _dot_nt(dup, win_ref[r + FF_PAIRS])
        dx1_n, dgffn = _rms_bwd(du2, n2, r2, g_ffn_v)
        dgffn_ref[...] += dgffn
        dx1_ref[...] = dx2 + dx1_n

    tile = lambda w: pl.BlockSpec((TM, w), lambda i: (i, 0))
    shards = lambda n: pl.BlockSpec((n, TM, FF_SHARD), lambda i: (0, i, 0))
    acc = lambda w: pl.BlockSpec((1, w), lambda i: (0, 0))
    return pl.pallas_call(
        body, name="ffn_fwd_bwd", grid=(TOK // TM,),
        in_specs=[tile(D_MODEL), tile(D_MODEL), _resident((1, D_MODEL)), _resident((1, D_MODEL)),
                  _resident((N_DEV, D_MODEL, FF_SHARD)), _resident((D_FF, D_MODEL))],
        out_specs=[acc(LANES), tile(D_MODEL), tile(D_MODEL), shards(FF_PAIRS), shards(N_DEV), tile(D_MODEL),
                   acc(D_MODEL), acc(D_MODEL)],
        out_shape=[jax.ShapeDtypeStruct((1, LANES), F32),
                   jax.ShapeDtypeStruct((TOK, D_MODEL), F32),
                   jax.ShapeDtypeStruct((TOK, D_MODEL), BF16),
                   jax.ShapeDtypeStruct((FF_PAIRS, TOK, FF_SHARD), BF16),
                   jax.ShapeDtypeStruct((N_DEV, TOK, FF_SHARD), BF16),
                   jax.ShapeDtypeStruct((TOK, D_MODEL), BF16),
                   jax.ShapeDtypeStruct((1, D_MODEL), F32),
                   jax.ShapeDtypeStruct((1, D_MODEL), F32)],
        scratch_shapes=[pltpu.VMEM((N_DEV, TM, FF_SHARD), F32)],
        compiler_params=_params(("arbitrary",)),
    )(x1, target, g_ffn, g_fin, w_ffn_in, w_ffn_out)


def _mix_bwd(dx1, o_sb, o_dl, gates, w_sb_up, w_dil_up, w_out):
    def body(dx1_ref, osb_ref, odl_ref, gate_ref, wsb_ref, wdl_ref, wout_ref,
             dgate_ref, dysb_ref, dydl_ref, dosb_ref, dodl_ref, dsum_ref):
        dmerged = _dot_nt(dx1_ref[...].astype(BF16), wout_ref[...])
        o_dl = odl_ref[...]
        y_sb = _dot(osb_ref[...], wsb_ref[...])
        y_dl = _dot(o_dl.astype(BF16), wdl_ref[...])
        s_sb = _sigmoid(gate_ref[:, :D_MODEL])
        s_dl = _sigmoid(gate_ref[:, D_MODEL:])
        dgate_ref[:, :D_MODEL] = (dmerged * y_sb * (s_sb * (1.0 - s_sb))).astype(BF16)
        dgate_ref[:, D_MODEL:] = (dmerged * y_dl * (s_dl * (1.0 - s_dl))).astype(BF16)
        dy_sb = (dmerged * s_sb).astype(BF16)
        dy_dl = (dmerged * s_dl).astype(BF16)
        dysb_ref[...] = dy_sb
        dydl_ref[...] = dy_dl
        dosb_ref[...] = _dot_nt(dy_sb, wsb_ref[...]).astype(BF16)
        do_dl = _dot_nt(dy_dl, wdl_ref[...])
        dodl_ref[...] = do_dl
        row = lax.broadcasted_iota(jnp.int32, (DIL_OUT, DIL_OUT), 0) // HEAD_DIM
        col = lax.broadcasted_iota(jnp.int32, (DIL_OUT, DIL_OUT), 1) // HEAD_DIM
        same_head = (row == col).astype(BF16)
        hi, lo = _split_bf16(do_dl * o_dl)
        dsum_ref[...] = _dot(hi, same_head) + _dot(lo, same_head)

    tile = lambda w: pl.BlockSpec((TM, w), lambda i: (i, 0))
    return pl.pallas_call(
        body, name="mix_bwd", grid=(TOK // TM,),
        in_specs=[tile(D_MODEL), tile(SB_WIDTH), tile(DIL_OUT), tile(2 * D_MODEL),
                  _resident((SB_WIDTH, D_MODEL)), _resident((DIL_OUT, D_MODEL)),
                  _resident((D_MODEL, D_MODEL))],
        out_specs=[tile(2 * D_MODEL), tile(D_MODEL), tile(D_MODEL), tile(SB_WIDTH), tile(DIL_OUT),
                   tile(DIL_OUT)],
        out_shape=[jax.ShapeDtypeStruct((TOK, 2 * D_MODEL), BF16),
                   jax.ShapeDtypeStruct((TOK, D_MODEL), BF16),
                   jax.ShapeDtypeStruct((TOK, D_MODEL), BF16),
                   jax.ShapeDtypeStruct((TOK, SB_WIDTH), BF16),
                   jax.ShapeDtypeStruct((TOK, DIL_OUT), F32),
                   jax.ShapeDtypeStruct((TOK, DIL_OUT), F32)],
        compiler_params=_params(("parallel",)),
    )(dx1, o_sb, o_dl, gates, w_sb_up, w_dil_up, w_out)


def _proj_bwd(dproj, dx1, x, g, w_in, send=None):
    widths = [p.shape[1] for p in dproj]

    def body(*refs):
        dx1_ref, x_ref, g_ref, w_ref, dx_ref, dg_ref = refs[len(widths):]

        @pl.when(pl.program_id(0) == 0)
        def _():
            dg_ref[...] = jnp.zeros_like(dg_ref)

        du = jnp.zeros((TM, D_MODEL), F32)
        c0 = 0
        for dp_ref, w in zip(refs, widths):
            du = du + _dot_nt(dp_ref[...].astype(BF16), w_ref[:, c0:c0 + w])
            c0 += w
        g_v = g_ref[...]
        n, r, _ = _rms_fwd(x_ref[...], g_v)
        dx, dg = _rms_bwd(du, n, r, g_v)
        dg_ref[...] += dg
        dx_ref[...] = dx1_ref[...] + dx

    tile = lambda w: pl.BlockSpec((TM, w), lambda i: (i, 0))
    return _call(
        body, send, name="proj_bwd", grid=(TOK // TM,),
        in_specs=[tile(w) for w in widths] + [tile(D_MODEL), tile(D_MODEL), _resident((1, D_MODEL)),
                                              _resident((D_MODEL, IN_WIDTH))],
        out_specs=[tile(D_MODEL), pl.BlockSpec((1, D_MODEL), lambda i: (0, 0))],
        out_shape=[jax.ShapeDtypeStruct((TOK, D_MODEL), F32),
                   jax.ShapeDtypeStruct((1, D_MODEL), F32)],
        scratch_shapes=[], semantics=("arbitrary",), operands=(*dproj, dx1, x, g, w_in))


def _atb_pieces(a, pieces, name, tm, tk=512):
    m = a.shape[1]
    widths = [p.shape[1] for p in pieces]
    n = sum(widths)
    nk = TOK // tk

    def body(a_ref, *refs):
        o_ref, acc_ref = refs[len(widths):]
        k = pl.program_id(1)

        @pl.when(k == 0)
        def _():
            acc_ref[...] = jnp.zeros_like(acc_ref)

        a_v = a_ref[...]
        c0 = 0
        for p_ref, w in zip(refs, widths):
            acc_ref[:, c0:c0 + w] += _dot_tn(a_v, p_ref[...].astype(BF16))
            c0 += w

        @pl.when(k == nk - 1)
        def _():
            o_ref[...] = acc_ref[...].astype(BF16)

    return pl.pallas_call(
        body, name=name, grid=(m // tm, nk),
        in_specs=[pl.BlockSpec((tk, tm), lambda i, k: (k, i))]
                 + [pl.BlockSpec((tk, w), lambda i, k: (k, 0)) for w in widths],
        out_specs=pl.BlockSpec((tm, n), lambda i, k: (i, 0)),
        out_shape=jax.ShapeDtypeStruct((m, n), BF16),
        scratch_shapes=[pltpu.VMEM((tm, n), F32)],
        compiler_params=_params(("parallel", "arbitrary")),
    )(a, *pieces)


def _atb_shards(a, b, name, tk=512):
    a_sharded = a.ndim == 3
    n, _, w = a.shape if a_sharded else b.shape
    other = (b if a_sharded else a).shape[1]
    nk = TOK // tk

    def body(a_ref, b_ref, o_ref, acc_ref):
        k = pl.program_id(1)

        @pl.when(k == 0)
        def _():
            acc_ref[...] = jnp.zeros_like(acc_ref)

        acc_ref[...] += _dot_tn(a_ref[...], b_ref[...])

        @pl.when(k == nk - 1)
        def _():
            o_ref[...] = acc_ref[...].astype(BF16)

    shard = pl.BlockSpec((None, tk, w), lambda r, k: (r, k, 0))
    whole = pl.BlockSpec((tk, other), lambda r, k: (k, 0))
    if a_sharded:
        out_spec, out_shape, acc = pl.BlockSpec((w, other), lambda r, k: (r, 0)), (n * w, other), (w, other)
    else:
        out_spec, out_shape, acc = pl.BlockSpec((None, other, w), lambda r, k: (r, 0, 0)), (n, other, w), (other, w)
    return pl.pallas_call(
        body, name=name, grid=(n, nk),
        in_specs=[shard, whole] if a_sharded else [whole, shard],
        out_specs=out_spec, out_shape=jax.ShapeDtypeStruct(out_shape, BF16),
        scratch_shapes=[pltpu.VMEM(acc, F32)],
        compiler_params=_params(("parallel", "arbitrary")),
    )(a, b)


def _atb(a, b, name, tm, tn, col_blocks=0, tk=512):
    m, n = a.shape[1], b.shape[1]
    nk = TOK // tk

    def body(a_ref, b_ref, o_ref, acc_ref):
        k = pl.program_id(2)

        @pl.when(k == 0)
        def _():
            acc_ref[...] = jnp.zeros_like(acc_ref)

        acc_ref[...] += _dot_tn(a_ref[...].astype(BF16), b_ref[...].astype(BF16))

        @pl.when(k == nk - 1)
        def _():
            if col_blocks:
                width = n // col_blocks
                for blk in range(col_blocks):
                    o_ref[blk] = acc_ref[:, blk * width:(blk + 1) * width].astype(BF16)
            else:
                o_ref[...] = acc_ref[...].astype(BF16)

    if col_blocks:
        out_spec = pl.BlockSpec((col_blocks, tm, n // col_blocks), lambda i, j, k: (0, i, 0))
        out_shape = jax.ShapeDtypeStruct((col_blocks, m, n // col_blocks), BF16)
    else:
        out_spec = pl.BlockSpec((tm, tn), lambda i, j, k: (i, j))
        out_shape = jax.ShapeDtypeStruct((m, n), BF16)
    return pl.pallas_call(
        body, name=name, grid=(m // tm, n // tn, nk),
        in_specs=[pl.BlockSpec((tk, tm), lambda i, j, k: (k, i)),
                  pl.BlockSpec((tk, tn), lambda i, j, k: (k, j))],
        out_specs=out_spec, out_shape=out_shape,
        scratch_shapes=[pltpu.VMEM((tm, tn), F32)],
        compiler_params=_params(("parallel", "parallel", "arbitrary")),
    )(a, b)


SB_PAIRS = SB_WIDTH // LANES


def _two_heads(v, lane0):
    zero = jnp.zeros_like(v)
    return jnp.where(lane0, v, zero), jnp.where(lane0, zero, v)


SB_QBLK = 256
N_SB_STEPS = SEQ // SB_QBLK


SB_KCHUNK = 2 * BLK
SB_ROWS = 2 * SB_QBLK
SB_DEAD = -104.0


def _log_keep(z):
    neg_z = -z
    return jnp.minimum(neg_z, 0.0) - jnp.log(1.0 + jnp.exp(jnp.minimum(z, neg_z)))


def _stack_heads(v, lane0):
    return jnp.concatenate(_two_heads(v, lane0), axis=0)


def _block_sums(v, tri):
    halves = (v[:, :BLK], v[:, BLK:])
    hi, lo = _split_bf16(jnp.concatenate(halves, axis=0))
    prod = _dot(jnp.concatenate([hi, lo], axis=0), tri)
    tri_sum = prod[:2 * SB_ROWS] + prod[2 * SB_ROWS:]
    sums = tuple(jnp.sum(h, axis=1, keepdims=True) for h in halves)
    return (tri_sum[:SB_ROWS], tri_sum[SB_ROWS:]), sums


def _sb_diag_mask():
    row = lax.broadcasted_iota(jnp.int32, (SB_ROWS, SB_KCHUNK), 0)
    col = lax.broadcasted_iota(jnp.int32, (SB_ROWS, SB_KCHUNK), 1)
    return col < jnp.where(row >= SB_QBLK, row - SB_QBLK, row)


def _sb_fwd(qkv, send=None):
    def body(q_ref, k_ref, v_ref, o_ref):
        i = pl.program_id(2)
        krow = lax.broadcasted_iota(jnp.int32, (BLK, BLK), 0)
        kcol = lax.broadcasted_iota(jnp.int32, (BLK, BLK), 1)
        later = (krow > kcol).astype(BF16)
        lane0 = lax.broadcasted_iota(jnp.int32, (SB_QBLK, LANES), 1) < HEAD_DIM
        q2 = _stack_heads(q_ref[0] * QK_SCALE, lane0)

        def chunk(c, carry, causal):
            acc, run = carry
            off = pl.multiple_of(c * SB_KCHUNK, SB_KCHUNK)
            z = _dot_nt(q2, k_ref[0, pl.ds(off, SB_KCHUNK), :])
            log_keep = _log_keep(z)
            if causal is not None:
                log_keep = jnp.where(causal, log_keep, 0.0)
            suffix, sums = _block_sums(log_keep, later)
            log_after = jnp.concatenate([suffix[0] + (run + sums[1]), suffix[1] + run], axis=1)
            a = jnp.exp(log_keep + z + log_after)
            if causal is not None:
                a = jnp.where(causal, a, 0.0)
            acc = acc + _dot(a.astype(BF16), v_ref[0, pl.ds(off, SB_KCHUNK), :])
            return acc, run + (sums[0] + sums[1])

        acc, run = chunk(i, (jnp.zeros((SB_ROWS, LANES), F32), jnp.zeros((SB_ROWS, 1), F32)), _sb_diag_mask())

        def live(state):
            t, _, run = state
            return jnp.logical_and(t < i, jnp.max(run) > SB_DEAD)

        def trip(state):
            t, acc, run = state
            acc, run = chunk(i - 1 - t, (acc, run), None)
            return t + 1, acc, run

        _, acc, _ = lax.while_loop(live, trip, (jnp.int32(0), acc, run))
        o_ref[0] = jnp.where(lane0, acc[:SB_QBLK], acc[SB_QBLK:]).astype(BF16)

    blk = pl.BlockSpec((1, SB_QBLK, LANES), lambda b, h, i: (b, i, h))
    return _call(
        body, send, name="sb_fwd", grid=(B_LOC, SB_PAIRS, N_SB_STEPS),
        in_specs=[blk,
                  pl.BlockSpec((1, SEQ, LANES), lambda b, h, i: (b, 0, SB_PAIRS + h)),
                  pl.BlockSpec((1, SEQ, LANES), lambda b, h, i: (b, 0, 2 * SB_PAIRS + h))],
        out_specs=[blk], out_shape=[jax.ShapeDtypeStruct((B_LOC, SEQ, SB_WIDTH), BF16)],
        scratch_shapes=[], semantics=("parallel", "parallel", "arbitrary"), operands=(qkv, qkv, qkv))


def _sb_bwd(qkv, d_o, send=None):
    def body(q_ref, k_ref, v_ref, do_ref, dq_ref, dk_ref, dv_ref, dk_acc, dv_acc, z_scr, keep_scr):
        i = pl.program_id(2)
        krow = lax.broadcasted_iota(jnp.int32, (BLK, BLK), 0)
        kcol = lax.broadcasted_iota(jnp.int32, (BLK, BLK), 1)
        upto = (krow <= kcol).astype(BF16)
        earlier = (krow < kcol).astype(BF16)
        lane0 = lax.broadcasted_iota(jnp.int32, (SB_QBLK, LANES), 1) < HEAD_DIM
        q2 = _stack_heads(q_ref[0] * QK_SCALE, lane0)
        do2 = _stack_heads(do_ref[0], lane0)

        def keep_sum(c, causal):
            off = pl.multiple_of(c * SB_KCHUNK, SB_KCHUNK)
            z = _dot_nt(q2, k_ref[0, pl.ds(off, SB_KCHUNK), :])
            log_keep = _log_keep(z)
            if causal is not None:
                log_keep = jnp.where(causal, log_keep, 0.0)
            z_scr[c] = z
            keep_scr[c] = log_keep
            return jnp.sum(log_keep, axis=1, keepdims=True)

        def live(state):
            t, run = state
            return jnp.logical_and(t < i, jnp.max(run) > SB_DEAD)

        walked, tot2 = lax.while_loop(live, lambda s: (s[0] + 1, s[1] + keep_sum(i - 1 - s[0], None)),
                                      (jnp.int32(0), keep_sum(i, _sb_diag_mask())))
        first = i - walked

        @pl.when(i == 0)
        def _():
            dk_acc[...] = jnp.zeros_like(dk_acc)
            dv_acc[...] = jnp.zeros_like(dv_acc)

        def chunk(c, carry, causal):
            dq, pre_keep, pre_e = carry
            off = pl.multiple_of(c * SB_KCHUNK, SB_KCHUNK)
            k_c = k_ref[0, pl.ds(off, SB_KCHUNK), :]
            v_c = v_ref[0, pl.ds(off, SB_KCHUNK), :]
            d_a = _dot_nt(do2, v_c)
            log_keep = keep_scr[c]
            log_beta = log_keep + z_scr[c]
            prefix, sums = _block_sums(log_keep, upto)
            inclusive = jnp.concatenate([prefix[0], prefix[1] + sums[0]], axis=1)
            a = jnp.exp(log_beta + ((tot2 - pre_keep) - inclusive))
            if causal is not None:
                a = jnp.where(causal, a, 0.0)
            e = d_a * a
            e_prefix, e_sums = _block_sums(e, earlier)
            before = jnp.concatenate([e_prefix[0] + pre_e, e_prefix[1] + (pre_e + e_sums[0])], axis=1)
            dz = e - (e + before) * jnp.exp(log_beta)
            if causal is not None:
                dz = jnp.where(causal, dz, 0.0)
            dz = dz.astype(BF16)
            dq = dq + _dot(dz, k_c)
            dk_acc[pl.ds(off, SB_KCHUNK), :] += _dot_tn(dz, q2)
            dv_acc[pl.ds(off, SB_KCHUNK), :] += _dot_tn(a.astype(BF16), do2)
            return dq, pre_keep + (sums[0] + sums[1]), pre_e + (e_sums[0] + e_sums[1])

        zero_col = jnp.zeros((SB_ROWS, 1), F32)
        carry = lax.fori_loop(first, i, lambda t, c: chunk(t, c, None),
                              (jnp.zeros((SB_ROWS, LANES), F32), zero_col, zero_col))
        dq, _, _ = chunk(i, carry, _sb_diag_mask())
        dq_ref[0] = (jnp.where(lane0, dq[:SB_QBLK], dq[SB_QBLK:]) * QK_SCALE).astype(BF16)

        @pl.when(i == N_SB_STEPS - 1)
        def _():
            dk_ref[0] = dk_acc[...].astype(BF16)
            dv_ref[0] = dv_acc[...].astype(BF16)

    blk = pl.BlockSpec((1, SB_QBLK, LANES), lambda b, h, i: (b, i, h))
    whole = lambda c: pl.BlockSpec((1, SEQ, LANES), lambda b, h, i: (b, 0, c * SB_PAIRS + h))
    out = jax.ShapeDtypeStruct((B_LOC, SEQ, SB_WIDTH), BF16)
    return _call(
        body, send, name="sb_bwd", grid=(B_LOC, SB_PAIRS, N_SB_STEPS),
        in_specs=[blk, whole(1), whole(2), blk],
        out_specs=[blk, whole(0), whole(0)],
        out_shape=[out, out, out],
        scratch_shapes=[pltpu.VMEM((SEQ, LANES), F32), pltpu.VMEM((SEQ, LANES), F32),
                        pltpu.VMEM((N_SB_STEPS, SB_ROWS, SB_KCHUNK), F32),
                        pltpu.VMEM((N_SB_STEPS, SB_ROWS, SB_KCHUNK), F32)],
        semantics=("parallel", "parallel", "arbitrary"), operands=(qkv, qkv, qkv, d_o))


DIL_GROUPS = len(DIL_PAIRS)
DIL_QBLOCKS = SEQ // BLK


def _residue_rows(j, dilation):
    length = SEQ // dilation
    return pl.ds(j, length, stride=dilation) if dilation > 1 else pl.ds(0, length)


def _gather_residues(src_ref, dst_ref, dst_off, dilation, scale=None):
    length = SEQ // dilation
    for j in range(dilation):
        v = src_ref[_residue_rows(j, dilation), :]
        if scale is not None:
            v = v * scale
        dst_ref[dst_off + j * length:dst_off + (j + 1) * length, :] = v.astype(dst_ref.dtype)


def _scatter_residues(src_ref, src_off, dst_ref, dilation):
    length = SEQ // dilation
    for j in range(dilation):
        dst_ref[_residue_rows(j, dilation), :] = (
            src_ref[src_off + j * length:src_off + (j + 1) * length, :].astype(dst_ref.dtype))


def _dil_geometry(group, pair):
    dilation = DIL_PAIRS[group][1]
    row = lax.broadcasted_iota(jnp.int32, (2 * BLK, 2 * BLK), 0)
    col = lax.broadcasted_iota(jnp.int32, (2 * BLK, 2 * BLK), 1)
    second = row >= BLK
    steps = BLK + jnp.where(second, row - BLK, row) - col
    coef = -ALIBI_MAX_BIAS / DIL_HEADS * math.log(2.0)
    first_head = float(4 * group + 1) + 2.0 * pair.astype(F32)
    slope = jnp.exp(coef * (first_head + jnp.where(second, 1.0, 0.0)))
    bias = slope * (steps * dilation).astype(F32)
    valid = jnp.logical_and(steps >= 0, steps <= BLK)
    return bias, valid, col >= BLK


def _dil_tile_scores(q2, kk, geometry, has_prev):
    bias, valid, own = geometry
    ok = jnp.logical_and(valid, jnp.logical_or(own, has_prev))
    return jnp.where(ok, _dot_nt(q2, kk) - bias, NEG_BIG)


def _head_col(v, lane_mask):
    return jnp.max(jnp.where(lane_mask, v, NEG_BIG), axis=1, keepdims=True)


def _dil_fwd(qkv, send=None):
    def body(*refs):
        ins, (o_ref, lse_ref), (qs, ks, vs, o_res, lse_res) = refs[:9], refs[9:11], refs[11:16]
        o_grp, lse_grp = refs[16:19], refs[19:22]
        pair = pl.program_id(1)
        lane0 = lax.broadcasted_iota(jnp.int32, (BLK, LANES), 1) < HEAD_DIM
        ks[0:BLK, :] = jnp.zeros((BLK, LANES), BF16)
        vs[0:BLK, :] = jnp.zeros((BLK, LANES), BF16)
        for grp, (_, dilation) in enumerate(DIL_PAIRS):
            q_ref, k_ref, v_ref = ins[3 * grp:3 * grp + 3]
            per_residue = DIL_QBLOCKS // dilation
            _gather_residues(q_ref, qs, 0, dilation, QK_SCALE)
            _gather_residues(k_ref, ks, BLK, dilation)
            _gather_residues(v_ref, vs, BLK, dilation)
            geometry = _dil_geometry(grp, pair)

            def step(blk, _):
                off = pl.multiple_of(blk * BLK, BLK)
                q2 = _stack_heads(qs[pl.ds(off, BLK), :], lane0)
                s = _dil_tile_scores(q2, ks[pl.ds(off, 2 * BLK), :], geometry, blk % per_residue != 0)
                m = jnp.max(s, axis=1, keepdims=True)
                p = jnp.exp(s - m)
                den = jnp.sum(p, axis=1, keepdims=True)
                out = _dot(p.astype(BF16), vs[pl.ds(off, 2 * BLK), :]) / den
                lse = m + jnp.log(den)
                o_res[pl.ds(off, BLK), :] = jnp.where(lane0, out[:BLK], out[BLK:])
                lse_res[pl.ds(off, BLK), :] = jnp.where(lane0, lse[:BLK], lse[BLK:])
                return 0

            lax.fori_loop(0, DIL_QBLOCKS, step, 0, unroll=4)
            _scatter_residues(o_res, 0, o_grp[grp], dilation)
            _scatter_residues(lse_res, 0, lse_grp[grp], dilation)

        for r0 in range(0, SEQ, 2 * BLK):
            rows = slice(r0, r0 + 2 * BLK)
            ls = [lse_grp[g][rows, :] for g in range(DIL_GROUPS)]
            m = jnp.maximum(jnp.maximum(ls[0], ls[1]), ls[2])
            w = [jnp.exp(l - m) for l in ls]
            den = w[0] + w[1] + w[2]
            o_ref[rows, :] = (w[0] * o_grp[0][rows, :] + w[1] * o_grp[1][rows, :] + w[2] * o_grp[2][rows, :]) / den
            lse_ref[rows, :] = m + jnp.log(den)

    def col(part, grp):
        return pl.BlockSpec((None, SEQ, LANES), lambda b, p: (b, 0, 6 * part + 2 * grp + p))

    out_spec = pl.BlockSpec((None, SEQ, LANES), lambda b, p: (b, 0, p))
    out = jax.ShapeDtypeStruct((B_LOC, SEQ, DIL_OUT), F32)
    return _call(
        body, send, name="dil_fwd", grid=(B_LOC, DIL_OUT // LANES),
        in_specs=[col(part, grp) for grp in range(DIL_GROUPS) for part in range(3)],
        out_specs=[out_spec, out_spec], out_shape=[out, out],
        scratch_shapes=[pltpu.VMEM((SEQ, LANES), BF16), pltpu.VMEM((SEQ + BLK, LANES), BF16),
                        pltpu.VMEM((SEQ + BLK, LANES), BF16), pltpu.VMEM((SEQ, LANES), F32),
                        pltpu.VMEM((SEQ, LANES), F32)] + [pltpu.VMEM((SEQ, LANES), F32)] * (2 * DIL_GROUPS),
        semantics=("parallel", "parallel"), operands=[qkv] * 9)


def _dil_bwd(qkv, d_o, lse, dsum, send=None):
    def body(*refs):
        ins, (do_ref, lse_ref, dsum_ref), outs = refs[:9], refs[9:12], refs[12:21]
        qs, ks, vs, dos, lse_res, dsum_res, dq_res, dk_acc, dv_acc = refs[21:]
        pair = pl.program_id(1)
        lane0 = lax.broadcasted_iota(jnp.int32, (BLK, LANES), 1) < HEAD_DIM
        lane1 = jnp.logical_not(lane0)
        ks[0:BLK, :] = jnp.zeros((BLK, LANES), BF16)
        vs[0:BLK, :] = jnp.zeros((BLK, LANES), BF16)
        for grp, (_, dilation) in enumerate(DIL_PAIRS):
            q_ref, k_ref, v_ref = ins[3 * grp:3 * grp + 3]
            dq_ref, dk_ref, dv_ref = outs[3 * grp:3 * grp + 3]
            per_residue = DIL_QBLOCKS // dilation
            _gather_residues(q_ref, qs, 0, dilation, QK_SCALE)
            _gather_residues(k_ref, ks, BLK, dilation)
            _gather_residues(v_ref, vs, BLK, dilation)
            _gather_residues(do_ref, dos, 0, dilation)
            _gather_residues(lse_ref, lse_res, 0, dilation)
            _gather_residues(dsum_ref, dsum_res, 0, dilation)
            dk_acc[...] = jnp.zeros_like(dk_acc)
            dv_acc[...] = jnp.zeros_like(dv_acc)
            geometry = _dil_geometry(grp, pair)

            def step(blk, _):
                off = pl.multiple_of(blk * BLK, BLK)
                q2 = _stack_heads(qs[pl.ds(off, BLK), :], lane0)
                do2 = _stack_heads(dos[pl.ds(off, BLK), :], lane0)
                kk = ks[pl.ds(off, 2 * BLK), :]
                vv = vs[pl.ds(off, 2 * BLK), :]
                lse_blk = lse_res[pl.ds(off, BLK), :]
                dsum_blk = dsum_res[pl.ds(off, BLK), :]
                lse2 = jnp.concatenate([_head_col(lse_blk, lane0), _head_col(lse_blk, lane1)], axis=0)
                dsum2 = jnp.concatenate([_head_col(dsum_blk, lane0), _head_col(dsum_blk, lane1)], axis=0)
                s = _dil_tile_scores(q2, kk, geometry, blk % per_residue != 0)
                p = jnp.exp(s - lse2)
                ds = (p * (_dot_nt(do2, vv) - dsum2)).astype(BF16)
                dq2 = _dot(ds, kk)
                dq_res[pl.ds(off, BLK), :] = jnp.where(lane0, dq2[:BLK], dq2[BLK:]) * QK_SCALE
                dk_acc[pl.ds(off, 2 * BLK), :] += _dot_tn(ds, q2)
                dv_acc[pl.ds(off, 2 * BLK), :] += _dot_tn(p.astype(BF16), do2)
                return 0

            lax.fori_loop(0, DIL_QBLOCKS, step, 0, unroll=4)
            _scatter_residues(dq_res, 0, dq_ref, dilation)
            _scatter_residues(dk_acc, BLK, dk_ref, dilation)
            _scatter_residues(dv_acc, BLK, dv_ref, dilation)

    def col(part, grp):
        return pl.BlockSpec((None, SEQ, LANES), lambda b, p: (b, 0, 6 * part + 2 * grp + p))

    slot = pl.BlockSpec((None, SEQ, LANES), lambda b, p: (b, 0, p))
    out = jax.ShapeDtypeStruct((B_LOC, SEQ, DIL_OUT), F32)
    return _call(
        body, send, name="dil_bwd", grid=(B_LOC, DIL_OUT // LANES),
        in_specs=[col(part, grp) for grp in range(DIL_GROUPS) for part in range(3)] + [slot] * 3,
        out_specs=[slot] * 9, out_shape=[out] * 9,
        scratch_shapes=[pltpu.VMEM((SEQ, LANES), BF16), pltpu.VMEM((SEQ + BLK, LANES), BF16),
                        pltpu.VMEM((SEQ + BLK, LANES), BF16), pltpu.VMEM((SEQ, LANES), BF16),
                        pltpu.VMEM((SEQ, LANES), F32), pltpu.VMEM((SEQ, LANES), F32),
                        pltpu.VMEM((SEQ, LANES), F32), pltpu.VMEM((SEQ + BLK, LANES), F32),
                        pltpu.VMEM((SEQ + BLK, LANES), F32)],
        semantics=("parallel", "parallel"), operands=[qkv] * 9 + [d_o, lse, dsum])


def _peers():
    x, y, c = lax.axis_index("x"), lax.axis_index("y"), lax.axis_index("c")
    me = 4 * x + 2 * y + c
    peers = []
    for mask in range(1, N_DEV):
        px = 1 - x if mask & 4 else x
        py = 1 - y if mask & 2 else y
        pc = 1 - c if mask & 1 else c
        peers.append(((px, py, pc), 4 * px + 2 * py + pc))
    return me, peers


def _all_gather(shard, name):
    def body(src_ref, out_ref, send_sems, recv_sems, local_sem):
        x, y, c = lax.axis_index("x"), lax.axis_index("y"), lax.axis_index("c")
        sibling = (x, y, 1 - c)
        chips = [(1 - x, y), (x, 1 - y), (1 - x, 1 - y)]

        def slot(px, py, pc):
            return out_ref.at[4 * px + 2 * py + pc]

        def copy(k, block, to, src=None):
            return pltpu.make_async_remote_copy(
                src_ref=slot(*block) if src is None else src, dst_ref=slot(*block),
                send_sem=send_sems.at[k], recv_sem=recv_sems.at[k], device_id=to,
                device_id_type=pl.DeviceIdType.MESH)

        mine = pltpu.make_async_copy(src_ref, slot(x, y, c), local_sem)
        mine.start()
        first = [copy(0, (x, y, c), sibling, src=src_ref)]
        first += [copy(1 + j, (x, y, c), (*chip, c), src=src_ref) for j, chip in enumerate(chips)]
        for cp in first:
            cp.start()
        passed = [copy(4 + j, (*chip, c), sibling) for j, chip in enumerate(chips)]
        for j, chip in enumerate(chips):
            copy(1 + j, (*chip, c), (x, y, c)).wait_recv()
            passed[j].start()
        copy(0, sibling, (x, y, c)).wait_recv()
        for j, chip in enumerate(chips):
            copy(4 + j, (*chip, 1 - c), (x, y, c)).wait_recv()
        for cp in first + passed:
            cp.wait_send()
        mine.wait()

    return pl.pallas_call(
        body, name=name,
        in_specs=[pl.BlockSpec(memory_space=pl.ANY)],
        out_specs=pl.BlockSpec(memory_space=pl.ANY),
        out_shape=jax.ShapeDtypeStruct((N_DEV,) + shard.shape, shard.dtype),
        scratch_shapes=[pltpu.SemaphoreType.DMA((N_DEV - 1,)), pltpu.SemaphoreType.DMA((N_DEV - 1,)),
                        pltpu.SemaphoreType.DMA],
    )(shard)


def _call(body, send, *, name, grid, in_specs, out_specs, out_shape, scratch_shapes, semantics, operands):
    if send is None:
        return pl.pallas_call(
            body, name=name, grid=grid, in_specs=in_specs, out_specs=out_specs, out_shape=out_shape,
            scratch_shapes=scratch_shapes, compiler_params=_params(semantics))(*operands), []
    srcs, kinds = [s for s, _ in send], [k for _, k in send]
    n, n_in, n_out, n_scr = len(srcs), len(in_specs), len(out_specs), len(scratch_shapes)
    steps = math.prod(grid)
    relay_step = (3 * steps) // 4

    def plan(refs):
        src_refs, land_refs = refs[n_in:n_in + n], refs[n_in + n + n_out:n_in + 2 * n + n_out]
        send_sems, recv_sems, local_sems = refs[-3:]
        x, y, c = lax.axis_index("x"), lax.axis_index("y"), lax.axis_index("c")
        me, peers = _peers()
        first, relayed_in, relayed_out, arrivals, sends, own = [], [], [], [], [], []
        for a, kind in enumerate(kinds):
            def copy(k, src, dst_slot, to):
                return pltpu.make_async_remote_copy(
                    src_ref=src, dst_ref=land_refs[a].at[dst_slot], send_sem=send_sems.at[a * (N_DEV - 1) + k],
                    recv_sem=recv_sems.at[a * (N_DEV - 1) + k], device_id=to, device_id_type=pl.DeviceIdType.MESH)

            if kind == "gather_by_chip":
                idx = lambda px, py, pc: 4 * px + 2 * py + pc
                chips = [(1 - x, y), (x, 1 - y), (1 - x, 1 - y)]
                mine = [copy(0, src_refs[a], me, (x, y, 1 - c))]
                arrivals.append(copy(0, src_refs[a], idx(x, y, 1 - c), (x, y, 1 - c)))
                for j, (px, py) in enumerate(chips):
                    mine.append(copy(1 + j, src_refs[a], me, (px, py, c)))
                    relayed_in.append(copy(1 + j, src_refs[a], idx(px, py, c), (px, py, c)))
                    relayed_out.append(copy(4 + j, land_refs[a].at[idx(px, py, c)], idx(px, py, c), (x, y, 1 - c)))
                    arrivals.append(copy(4 + j, src_refs[a], idx(px, py, 1 - c), (x, y, 1 - c)))
                first += mine
                sends += mine + relayed_out[-3:]
                own.append(pltpu.make_async_copy(src_refs[a], land_refs[a].at[me], local_sems.at[a]))
            elif kind == "scatter_by_chip":
                for k, (px, py) in enumerate([(1 - x, y), (x, 1 - y), (1 - x, 1 - y)]):
                    first.append(copy(k, src_refs[a].at[2 * px + py], 2 * x + y, (px, py, c)))
                    arrivals.append(copy(k, src_refs[a].at[2 * px + py], 2 * px + py, (px, py, c)))
                sends += first[-3:]
                own.append(pltpu.make_async_copy(src_refs[a].at[2 * x + y], land_refs[a].at[2 * x + y],
                                                 local_sems.at[a]))
            else:
                part = (lambda i: src_refs[a].at[i]) if kind == "scatter" else (lambda i: src_refs[a])
                for k, (peer, peer_idx) in enumerate(peers):
                    first.append(copy(k, part(peer_idx), me, peer))
                    arrivals.append(copy(k, part(peer_idx), peer_idx, peer))
                sends += first[-(N_DEV - 1):]
                own.append(pltpu.make_async_copy(part(me), land_refs[a].at[me], local_sems.at[a]))
        return first, relayed_in, relayed_out, arrivals, sends, own

    def wrapped(*refs):
        step = 0
        for axis, size in enumerate(grid):
            step = step * size + pl.program_id(axis)

        @pl.when(step == 0)
        def _():
            first, _, _, _, _, own = plan(refs)
            for cp in first + own:
                cp.start()

        if "gather_by_chip" in kinds:
            @pl.when(step == relay_step)
            def _():
                _, relayed_in, relayed_out, _, _, _ = plan(refs)
                for cp_in, cp_out in zip(relayed_in, relayed_out):
                    cp_in.wait_recv()
                    cp_out.start()

        body(*refs[:n_in], *refs[n_in + n:n_in + n + n_out], *refs[n_in + 2 * n + n_out:n_in + 2 * n + n_out + n_scr])

        @pl.when(step == steps - 1)
        def _():
            _, _, _, arrivals, sends, own = plan(refs)
            for cp in arrivals:
                cp.wait_recv()
            for cp in sends:
                cp.wait_send()
            for cp in own:
                cp.wait()

    anywhere = pl.BlockSpec(memory_space=pl.ANY)
    lands = [jax.ShapeDtypeStruct((N_DEV // 2 if k == "scatter_by_chip" else N_DEV,) + s.shape[-2:], s.dtype)
             for s, k in send]
    out = pl.pallas_call(
        wrapped, name=name, grid=grid,
        in_specs=list(in_specs) + [anywhere] * n, out_specs=list(out_specs) + [anywhere] * n,
        out_shape=list(out_shape) + lands,
        scratch_shapes=list(scratch_shapes) + [pltpu.SemaphoreType.DMA((n * (N_DEV - 1),)),
                                               pltpu.SemaphoreType.DMA((n * (N_DEV - 1),)),
                                               pltpu.SemaphoreType.DMA((n,))],
        compiler_params=_params(("arbitrary",) * len(grid)),
    )(*operands, *srcs)
    return out[:n_out], list(out[n_out:])


def _pair_swap(blocks):
    def body(src_ref, out_ref, send_sems, recv_sems):
        x, y, c = lax.axis_index("x"), lax.axis_index("y"), lax.axis_index("c")
        copies = [pltpu.make_async_remote_copy(
            src_ref=src_ref.at[2 * chip + (1 - c)], dst_ref=out_ref.at[chip], send_sem=send_sems.at[chip],
            recv_sem=recv_sems.at[chip], device_id=(x, y, 1 - c), device_id_type=pl.DeviceIdType.MESH)
            for chip in range(N_DEV // 2)]
        for cp in copies:
            cp.start()
        for cp in copies:
            cp.wait()

    return pl.pallas_call(
        body, name="pair_swap_grad_w_in",
        in_specs=[pl.BlockSpec(memory_space=pl.ANY)], out_specs=pl.BlockSpec(memory_space=pl.ANY),
        out_shape=jax.ShapeDtypeStruct((N_DEV // 2,) + blocks.shape[1:], blocks.dtype),
        scratch_shapes=[pltpu.SemaphoreType.DMA((N_DEV // 2,)), pltpu.SemaphoreType.DMA((N_DEV // 2,))],
    )(blocks)


def _pair_sum(blocks, swapped, core):
    _, rows, cols = swapped.shape
    tile_rows = _row_tile(rows)

    def body(core_ref, mine_ref, theirs_ref, o_ref):
        o_ref[...] = (mine_ref[...].astype(F32) + theirs_ref[...].astype(F32)).astype(o_ref.dtype)

    return pl.pallas_call(
        body, name="pair_sum_grad_w_in",
        grid_spec=pltpu.PrefetchScalarGridSpec(
            num_scalar_prefetch=1, grid=(N_DEV // 2, rows // tile_rows),
            in_specs=[pl.BlockSpec((None, tile_rows, cols), lambda j, i, core_ref: (2 * j + core_ref[0], i, 0)),
                      pl.BlockSpec((None, tile_rows, cols), lambda j, i, core_ref: (j, i, 0))],
            out_specs=pl.BlockSpec((None, tile_rows, cols), lambda j, i, core_ref: (j, i, 0))),
        out_shape=jax.ShapeDtypeStruct(swapped.shape, swapped.dtype),
        compiler_params=_params(("parallel", "parallel")),
    )(core, blocks, swapped)


def _sum_in_device_order(land_ref):
    acc = land_ref[0].astype(F32)
    for j in range(1, land_ref.shape[0]):
        acc = acc + land_ref[j].astype(F32)
    return acc


def _adam_math(w, g, m, v):
    c1 = 1.0 - ADAM_B1 ** ADAM_STEP
    c2 = 1.0 - ADAM_B2 ** ADAM_STEP
    m_new = ADAM_B1 * m + (1.0 - ADAM_B1) * g
    v_new = ADAM_B2 * v + (1.0 - ADAM_B2) * (g * g)
    delta = -ADAM_LR * ((m_new / c1) / (jnp.sqrt(v_new / c2) + ADAM_EPS) + ADAM_WD * w)
    return delta, m_new, v_new


def _row_tile(rows):
    return max(t for t in range(8, 257, 8) if rows % t == 0) if rows % 8 == 0 else rows


def _sum_update(land, w, m, v, name):
    slots, rows, cols = land.shape
    tile_rows = _row_tile(rows)

    def body(land_ref, w_ref, m_ref, v_ref, g_ref, d_ref, nm_ref, nv_ref):
        g = _sum_in_device_order(land_ref)
        g_ref[...] = g
        d_ref[...], nm_ref[...], nv_ref[...] = _adam_math(w_ref[...], g, m_ref[...], v_ref[...])

    tile = pl.BlockSpec((None, tile_rows, cols), lambda i: (0, i, 0))
    out = jax.ShapeDtypeStruct((1, rows, cols), F32)
    return pl.pallas_call(
        body, name=name, grid=(rows // tile_rows,),
        in_specs=[pl.BlockSpec((slots, tile_rows, cols), lambda i: (0, i, 0)), tile, tile, tile],
        out_specs=[tile] * 4, out_shape=[out] * 4,
        compiler_params=_params(("parallel",)),
    )(land, w, m, v)


def _sum_gains(land):
    def body(land_ref, o_ref):
        o_ref[...] = _sum_in_device_order(land_ref)

    return pl.pallas_call(
        body, name="sum_gain_grads", grid=(1,),
        in_specs=[pl.BlockSpec(land.shape, lambda i: (0, 0, 0))],
        out_specs=pl.BlockSpec(land.shape[1:], lambda i: (0, 0)),
        out_shape=jax.ShapeDtypeStruct(land.shape[1:], F32),
    )(land)


def _adamw(w, g, m, v, name):
    def body(w_ref, g_ref, m_ref, v_ref, d_ref, nm_ref, nv_ref):
        d_ref[...], nm_ref[...], nv_ref[...] = _adam_math(w_ref[...], g_ref[...], m_ref[...], v_ref[...])

    whole = pl.BlockSpec(w.shape, lambda i: (0, 0))
    out = jax.ShapeDtypeStruct(w.shape, F32)
    return pl.pallas_call(
        body, name=name, grid=(1,),
        in_specs=[whole] * 4, out_specs=[whole] * 3, out_shape=[out] * 3,
    )(w, g, m, v)


GROUP_FFN = ("w_ffn_in", "w_ffn_out")
GROUP_MIX = ("w_sb_up", "w_dil_up", "w_out")
COL_SHARDED = ("w_in", "w_sb_up", "w_dil_up", "w_ffn_in")


def _full_from_shards(name, slots):
    _, r, c = slots.shape
    if name in COL_SHARDED:
        return slots.transpose(1, 0, 2).reshape(r, N_DEV * c)
    return slots.reshape(N_DEV * r, c)


def _shards_from_full(name, full):
    rows, cols = full.shape
    if name in COL_SHARDED:
        return full.reshape(rows, N_DEV, cols // N_DEV).transpose(1, 0, 2)
    return full.reshape(N_DEV, rows // N_DEV, cols)


def _local_step(x, target, g_mix, g_ffn, g_fin, w_in, shards=None, rest=None):
    gather = lambda names, kind: None if shards is None else [(shards[n], kind) for n in names]
    scatter = lambda blocks: None if shards is None else [(t, "scatter") for t in blocks]
    landed = lambda blocks, lands: lands if lands else blocks

    w = {"w_in": w_in}
    if shards is None:
        w.update(rest)
        w["w_ffn_in"] = _shards_from_full("w_ffn_in", rest["w_ffn_in"])
    qkv_sb, qkv_dl, gates, u = _norm_proj(x, g_mix, w["w_in"])
    qkv_sb = qkv_sb.reshape(B_LOC, SEQ, 3 * SB_WIDTH)
    qkv_dl = qkv_dl.reshape(B_LOC, SEQ, 3 * DIL_WIDTH)
    (o_sb,), lands = _sb_fwd(qkv_sb, gather(GROUP_FFN, "gather_by_chip"))
    if lands:
        w["w_ffn_in"], w["w_ffn_out"] = lands[0], _full_from_shards("w_ffn_out", lands[1])
    o_sb = o_sb.reshape(TOK, SB_WIDTH)
    (o_dl, lse), lands = _dil_fwd(qkv_dl, gather(GROUP_MIX, "gather"))
    w.update({n: _full_from_shards(n, t) for n, t in zip(GROUP_MIX, lands)})
    o_dl = o_dl.reshape(TOK, DIL_OUT)

    x1, merged = _mix_out(x, o_sb, o_dl, gates, w["w_sb_up"], w["w_dil_up"], w["w_out"])
    loss, dx1, u2, act, dh, dx2, dg_fin, dg_ffn = _ffn_fwd_bwd(x1, target, g_ffn, g_fin, w["w_ffn_in"], w["w_ffn_out"])
    dgates, dy_sb, dy_dl, do_sb, do_dl, dsum = _mix_bwd(dx1, o_sb, o_dl, gates, w["w_sb_up"], w["w_dil_up"], w["w_out"])
    blocks = {
        "w_sb_up": _atb(o_sb, dy_sb, "grad_w_sb_up", SB_WIDTH, D_MODEL, col_blocks=N_DEV),
        "w_dil_up": _atb(o_dl, dy_dl, "grad_w_dil_up", DIL_OUT, D_MODEL, col_blocks=N_DEV),
        "w_out": _shards_from_full("w_out", _atb(merged, dx1, "grad_w_out", D_MODEL, D_MODEL)),
        "w_ffn_in": _atb_shards(u2, dh, "grad_w_ffn_in"),
        "w_ffn_out": _shards_from_full("w_ffn_out", _atb_shards(act, dx2, "grad_w_ffn_out")),
    }
    grads = {}

    early, late = ("w_ffn_in",), ("w_ffn_out",) + GROUP_MIX
    early_blocks = [blocks[n] for n in early]
    (dq_sb, dk_sb, dv_sb), lands = _sb_bwd(qkv_sb, do_sb.reshape(B_LOC, SEQ, SB_WIDTH), scatter(early_blocks))
    grads.update(zip(early, landed(early_blocks, lands)))
    as_batch = lambda t: t.reshape(B_LOC, SEQ, DIL_OUT)
    late_blocks = [blocks[n] for n in late]
    d_dl, lands = _dil_bwd(qkv_dl, as_batch(do_dl), lse, as_batch(dsum), scatter(late_blocks))
    grads.update(zip(late, landed(late_blocks, lands)))
    flat = lambda t: t.reshape(TOK, -1)
    dproj = ([flat(dq_sb), flat(dk_sb), flat(dv_sb)]
             + [flat(d_dl[3 * grp + part]) for part in range(3) for grp in range(DIL_GROUPS)] + [dgates])

    w_in_blocks = _shards_from_full("w_in", _atb_pieces(u, dproj, "grad_w_in", D_MODEL // 2))
    if shards is None:
        grads["w_in"] = w_in_blocks
        send = None
    else:
        core = lax.axis_index("c").astype(jnp.int32).reshape(1)
        send = [(_pair_sum(w_in_blocks, _pair_swap(w_in_blocks), core), "scatter_by_chip")]
    (grad_x, dg_mix), lands = _proj_bwd(dproj, dx1, x, g_mix, w["w_in"], send)
    if lands:
        grads["w_in"] = lands[0]
    gain_grads = jnp.concatenate([dg_mix, dg_ffn, dg_fin], axis=0)
    return loss, grad_x, gain_grads, grads


def kernel(x, norm_mix_g, w_in, w_sb_up, w_dil_up, w_out, norm_ffn_g, w_ffn_in, w_ffn_out, norm_final_g, loss_target, m_norm_mix_g, m_w_in, m_w_sb_up, m_w_dil_up, m_w_out, m_norm_ffn_g, m_w_ffn_in, m_w_ffn_out, m_norm_final_g, v_norm_mix_g, v_w_in, v_w_sb_up, v_w_dil_up, v_w_out, v_norm_ffn_g, v_w_ffn_in, v_w_ffn_out, v_norm_final_g):
    mats = {"w_in": w_in, "w_sb_up": w_sb_up, "w_dil_up": w_dil_up, "w_out": w_out,
            "w_ffn_in": w_ffn_in, "w_ffn_out": w_ffn_out}
    moments_m = {"w_in": m_w_in, "w_sb_up": m_w_sb_up, "w_dil_up": m_w_dil_up, "w_out": m_w_out,
                 "w_ffn_in": m_w_ffn_in, "w_ffn_out": m_w_ffn_out}
    moments_v = {"w_in": v_w_in, "w_sb_up": v_w_sb_up, "w_dil_up": v_w_dil_up, "w_out": v_w_out,
                 "w_ffn_in": v_w_ffn_in, "w_ffn_out": v_w_ffn_out}
    gathered_w_in = _all_gather(w_in[0].astype(BF16), "all_gather_w_in")
    g_fin = norm_final_g.reshape(1, D_MODEL)
    loss, grad_x, gain_grads, grad_slots = _local_step(
        x.reshape(TOK, D_MODEL), loss_target.reshape(TOK, D_MODEL), norm_mix_g, norm_ffn_g, g_fin,
        _full_from_shards("w_in", gathered_w_in),
        shards={name: mats[name][0].astype(BF16) for name in GROUP_FFN + GROUP_MIX})

    gain_rows = jnp.concatenate([gain_grads, jnp.tile(loss, (1, D_MODEL // LANES)),
                                 jnp.zeros((8 - 4, D_MODEL), F32)], axis=0)
    g_gains = _sum_gains(_all_gather(gain_rows, "all_gather_gains"))

    out_g, out_d, out_m, out_v = {}, {}, {}, {}
    for name, slots in grad_slots.items():
        out_g[name], out_d[name], out_m[name], out_v[name] = _sum_update(
            slots, mats[name], moments_m[name], moments_v[name], "update_" + name)

    gain_w = jnp.concatenate([norm_mix_g, norm_ffn_g, g_fin], axis=0)
    gain_m = jnp.concatenate([m_norm_mix_g, m_norm_ffn_g, m_norm_final_g.reshape(1, D_MODEL)], axis=0)
    gain_v = jnp.concatenate([v_norm_mix_g, v_norm_ffn_g, v_norm_final_g.reshape(1, D_MODEL)], axis=0)
    gd, gm, gv = _adamw(gain_w, g_gains[:3], gain_m, gain_v, "adamw_gains")
    for idx, name in enumerate(("norm_mix_g", "norm_ffn_g", "norm_final_g")):
        shape = (D_MODEL,) if name == "norm_final_g" else (1, D_MODEL)
        out_g[name] = g_gains[idx].reshape(shape)
        out_d[name], out_m[name], out_v[name] = gd[idx].reshape(shape), gm[idx].reshape(shape), gv[idx].reshape(shape)

    order = ("norm_mix_g", "w_in", "w_sb_up", "w_dil_up", "w_out", "norm_ffn_g", "w_ffn_in", "w_ffn_out",
             "norm_final_g")
    return (g_gains[3, 0], grad_x.reshape(B_LOC, SEQ, D_MODEL),
            *[out_g[n] for n in order], *[out_d[n] for n in order],
            *[out_m[n] for n in order], *[out_v[n] for n in order])
```

```python
import math

import jax
import jax.numpy as jnp
from jax import lax
from jax.experimental import pallas as pl
from jax.experimental.pallas import tpu as pltpu

F32 = jnp.float32
BF16 = jnp.bfloat16

N_DEV = 8
D_MODEL = 1024
SEQ = 2048
B_LOC = 2
TOK = B_LOC * SEQ
HEAD_DIM = 64
SB_WIDTH = 512
DIL_WIDTH = 768
DIL_OUT = 256
QKV_WIDTH = 3 * SB_WIDTH + 3 * DIL_WIDTH
IN_WIDTH = QKV_WIDTH + 2 * D_MODEL
D_FF = 2816
DIL_PAIRS = ((128, 1), (512, 4), (2048, 16))
DIL_HEADS = 12
RMS_EPS = 1e-6
ALIBI_MAX_BIAS = 8.0
QK_SCALE = 1.0 / math.sqrt(HEAD_DIM)
BLK = 128
LANES = 128
NEG_BIG = -1e30

ADAM_LR = 0.001
ADAM_B1 = 0.9
ADAM_B2 = 0.999
ADAM_EPS = 1e-08
ADAM_WD = 0.01
ADAM_STEP = 10

VMEM_LIMIT = 56 * 1024 * 1024


def _dot(a, b):
    return jnp.dot(a, b, preferred_element_type=F32)


def _dot_nt(a, b):
    return lax.dot_general(a, b, (((1,), (1,)), ((), ())), preferred_element_type=F32)


def _dot_tn(a, b):
    return lax.dot_general(a, b, (((0,), (0,)), ((), ())), preferred_element_type=F32)


def _softplus(z):
    return jnp.maximum(z, 0.0) + jnp.log1p(jnp.exp(-jnp.abs(z)))


def _sigmoid(z):
    return 1.0 / (1.0 + jnp.exp(-z))


def _split_bf16(v):
    hi = v.astype(BF16)
    lo = (v - hi.astype(F32)).astype(BF16)
    return hi, lo


def _chunks(width, step=512):
    out, c = [], 0
    while c < width:
        w = min(step, width - c)
        out.append((c, w))
        c += w
    return out


def _resident(shape):
    nd = len(shape)
    return pl.BlockSpec(shape, lambda *_: (0,) * nd, pipeline_mode=pl.Buffered(1))


def _params(sem):
    return pltpu.CompilerParams(dimension_semantics=sem, vmem_limit_bytes=VMEM_LIMIT)


def _rms_fwd(x, g):
    r = lax.rsqrt(jnp.mean(x * x, axis=-1, keepdims=True) + RMS_EPS)
    n = x * r
    return n, r, n * g


def _rms_bwd(dy, n, r, g):
    dg = jnp.sum(dy * n, axis=0, keepdims=True)
    dn = dy * g
    dx = r * (dn - n * jnp.mean(dn * n, axis=-1, keepdims=True))
    return dx, dg


TM = 256


def _norm_proj(x, g, w_in):
    def body(x_ref, g_ref, w_ref, sb_ref, dl_ref, gate_ref, u_ref):
        _, _, u = _rms_fwd(x_ref[...], g_ref[...])
        u = u.astype(BF16)
        u_ref[...] = u
        for c0, w in _chunks(3 * SB_WIDTH):
            sb_ref[:, c0:c0 + w] = _dot(u, w_ref[:, c0:c0 + w]).astype(BF16)
        for c0, w in _chunks(3 * DIL_WIDTH):
            dl_ref[:, c0:c0 + w] = _dot(u, w_ref[:, 3 * SB_WIDTH + c0:3 * SB_WIDTH + c0 + w])
        for c0, w in _chunks(2 * D_MODEL):
            gate_ref[:, c0:c0 + w] = _dot(u, w_ref[:, QKV_WIDTH + c0:QKV_WIDTH + c0 + w])

    return pl.pallas_call(
        body, name="norm_proj", grid=(TOK // TM,),
        in_specs=[pl.BlockSpec((TM, D_MODEL), lambda i: (i, 0)), _resident((1, D_MODEL)),
                  _resident((D_MODEL, IN_WIDTH))],
        out_specs=[pl.BlockSpec((TM, 3 * SB_WIDTH), lambda i: (i, 0)),
                   pl.BlockSpec((TM, 3 * DIL_WIDTH), lambda i: (i, 0)),
                   pl.BlockSpec((TM, 2 * D_MODEL), lambda i: (i, 0)),
                   pl.BlockSpec((TM, D_MODEL), lambda i: (i, 0))],
        out_shape=[jax.ShapeDtypeStruct((TOK, 3 * SB_WIDTH), BF16),
                   jax.ShapeDtypeStruct((TOK, 3 * DIL_WIDTH), F32),
                   jax.ShapeDtypeStruct((TOK, 2 * D_MODEL), F32),
                   jax.ShapeDtypeStruct((TOK, D_MODEL), BF16)],
        compiler_params=_params(("parallel",)),
    )(x, g, w_in)


def _mix_out(x, o_sb, o_dl, gates, w_sb_up, w_dil_up, w_out):
    def body(x_ref, osb_ref, odl_ref, gate_ref, wsb_ref, wdl_ref, wout_ref, x1_ref, mg_ref):
        y_sb = _dot(osb_ref[...], wsb_ref[...])
        y_dl = _dot(odl_ref[...].astype(BF16), wdl_ref[...])
        merged = (_sigmoid(gate_ref[:, :D_MODEL]) * y_sb
                  + _sigmoid(gate_ref[:, D_MODEL:]) * y_dl).astype(BF16)
        mg_ref[...] = merged
        x1_ref[...] = x_ref[...] + _dot(merged, wout_ref[...])

    return pl.pallas_call(
        body, name="mix_out", grid=(TOK // TM,),
        in_specs=[pl.BlockSpec((TM, D_MODEL), lambda i: (i, 0)),
                  pl.BlockSpec((TM, SB_WIDTH), lambda i: (i, 0)),
                  pl.BlockSpec((TM, DIL_OUT), lambda i: (i, 0)),
                  pl.BlockSpec((TM, 2 * D_MODEL), lambda i: (i, 0)),
                  _resident((SB_WIDTH, D_MODEL)), _resident((DIL_OUT, D_MODEL)),
                  _resident((D_MODEL, D_MODEL))],
        out_specs=[pl.BlockSpec((TM, D_MODEL), lambda i: (i, 0)),
                   pl.BlockSpec((TM, D_MODEL), lambda i: (i, 0))],
        out_shape=[jax.ShapeDtypeStruct((TOK, D_MODEL), F32),
                   jax.ShapeDtypeStruct((TOK, D_MODEL), BF16)],
        compiler_params=_params(("parallel",)),
    )(x, o_sb, o_dl, gates, w_sb_up, w_dil_up, w_out)


FF_SHARD = 2 * D_FF // N_DEV
FF_PAIRS = N_DEV // 2


def _ffn_fwd_bwd(x1, target, g_ffn, g_fin, w_ffn_in, w_ffn_out):
    def body(x1_ref, t_ref, gffn_ref, gfin_ref, win_ref, wout_ref,
             loss_ref, dx1_ref, u2_ref, act_ref, dh_ref, dx2_ref, dgfin_ref, dgffn_ref, h_scr):
        i = pl.program_id(0)

        @pl.when(i == 0)
        def _():
            loss_ref[...] = jnp.zeros_like(loss_ref)
            dgfin_ref[...] = jnp.zeros_like(dgfin_ref)
            dgffn_ref[...] = jnp.zeros_like(dgffn_ref)

        x1 = x1_ref[...]
        g_ffn_v = gffn_ref[...]
        g_fin_v = gfin_ref[...]
        n2, r2, u2 = _rms_fwd(x1, g_ffn_v)
        u2 = u2.astype(BF16)
        u2_ref[...] = u2
        x2 = x1
        for r in range(FF_PAIRS):
            gate = _dot(u2, win_ref[r])
            up = _dot(u2, win_ref[r + FF_PAIRS])
            h_scr[r] = gate
            h_scr[r + FF_PAIRS] = up
            act = (gate * _sigmoid(gate) * up).astype(BF16)
            act_ref[r] = act
            x2 = x2 + _dot(act, wout_ref[r * FF_SHARD:(r + 1) * FF_SHARD, :])
        n3, r3, y = _rms_fwd(x2, g_fin_v)
        err = y - t_ref[...]
        sq = jnp.sum(jnp.sum(err * err, axis=1, keepdims=True), axis=0, keepdims=True)
        loss_ref[...] += sq * (0.5 / D_MODEL)
        dx2, dgfin = _rms_bwd(err * (1.0 / D_MODEL), n3, r3, g_fin_v)
        dgfin_ref[...] += dgfin
        dx2_b = dx2.astype(BF16)
        dx2_ref[...] = dx2_b
        du2 = jnp.zeros((TM, D_MODEL), F32)
        for r in range(FF_PAIRS):
            gate = h_scr[r]
            up = h_scr[r + FF_PAIRS]
            dact = _dot_nt(dx2_b, wout_ref[r * FF_SHARD:(r + 1) * FF_SHARD, :])
            sg = _sigmoid(gate)
            dgate = (dact * up * (sg * (1.0 + gate * (1.0 - sg)))).astype(BF16)
            dup = (dact * (gate * sg)).astype(BF16)
            dh_ref[r] = dgate
            dh_ref[r + FF_PAIRS] = dup
            du2 = du2 + _dot_nt(dgate, win_ref[r])
            du2 = du2 + _dot_nt(dup, win_ref[r + FF_PAIRS])
        dx1_n, dgffn = _rms_bwd(du2, n2, r2, g_ffn_v)
        dgffn_ref[...] += dgffn
        dx1_ref[...] = dx2 + dx1_n

    tile = lambda w: pl.BlockSpec((TM, w), lambda i: (i, 0))
    shards = lambda n: pl.BlockSpec((n, TM, FF_SHARD), lambda i: (0, i, 0))
    acc = lambda w: pl.BlockSpec((1, w), lambda i: (0, 0))
    return pl.pallas_call(
        body, name="ffn_fwd_bwd", grid=(TOK // TM,),
        in_specs=[tile(D_MODEL), tile(D_MODEL), _resident((1, D_MODEL)), _resident((1, D_MODEL)),
                  _resident((N_DEV, D_MODEL, FF_SHARD)), _resident((D_FF, D_MODEL))],
        out_specs=[acc(LANES), tile(D_MODEL), tile(D_MODEL), shards(FF_PAIRS), shards(N_DEV), tile(D_MODEL),
                   acc(D_MODEL), acc(D_MODEL)],
        out_shape=[jax.ShapeDtypeStruct((1, LANES), F32),
                   jax.ShapeDtypeStruct((TOK, D_MODEL), F32),
                   jax.ShapeDtypeStruct((TOK, D_MODEL), BF16),
                   jax.ShapeDtypeStruct((FF_PAIRS, TOK, FF_SHARD), BF16),
                   jax.ShapeDtypeStruct((N_DEV, TOK, FF_SHARD), BF16),
                   jax.ShapeDtypeStruct((TOK, D_MODEL), BF16),
                   jax.ShapeDtypeStruct((1, D_MODEL), F32),
                   jax.ShapeDtypeStruct((1, D_MODEL), F32)],
        scratch_shapes=[pltpu.VMEM((N_DEV, TM, FF_SHARD), F32)],
        compiler_params=_params(("arbitrary",)),
    )(x1, target, g_ffn, g_fin, w_ffn_in, w_ffn_out)


def _mix_bwd(dx1, o_sb, o_dl, gates, w_sb_up, w_dil_up, w_out):
    def body(dx1_ref, osb_ref, odl_ref, gate_ref, wsb_ref, wdl_ref, wout_ref,
             dgate_ref, dysb_ref, dydl_ref, dosb_ref, dodl_ref, dsum_ref):
        dmerged = _dot_nt(dx1_ref[...].astype(BF16), wout_ref[...])
        o_dl = odl_ref[...]
        y_sb = _dot(osb_ref[...], wsb_ref[...])
        y_dl = _dot(o_dl.astype(BF16), wdl_ref[...])
        s_sb = _sigmoid(gate_ref[:, :D_MODEL])
        s_dl = _sigmoid(gate_ref[:, D_MODEL:])
        dgate_ref[:, :D_MODEL] = (dmerged * y_sb * (s_sb * (1.0 - s_sb))).astype(BF16)
        dgate_ref[:, D_MODEL:] = (dmerged * y_dl * (s_dl * (1.0 - s_dl))).astype(BF16)
        dy_sb = (dmerged * s_sb).astype(BF16)
        dy_dl = (dmerged * s_dl).astype(BF16)
        dysb_ref[...] = dy_sb
        dydl_ref[...] = dy_dl
        dosb_ref[...] = _dot_nt(dy_sb, wsb_ref[...]).astype(BF16)
        do_dl = _dot_nt(dy_dl, wdl_ref[...])
        dodl_ref[...] = do_dl
        row = lax.broadcasted_iota(jnp.int32, (DIL_OUT, DIL_OUT), 0) // HEAD_DIM
        col = lax.broadcasted_iota(jnp.int32, (DIL_OUT, DIL_OUT), 1) // HEAD_DIM
        same_head = (row == col).astype(BF16)
        hi, lo = _split_bf16(do_dl * o_dl)
        dsum_ref[...] = _dot(hi, same_head) + _dot(lo, same_head)

    tile = lambda w: pl.BlockSpec((TM, w), lambda i: (i, 0))
    return pl.pallas_call(
        body, name="mix_bwd", grid=(TOK // TM,),
        in_specs=[tile(D_MODEL), tile(SB_WIDTH), tile(DIL_OUT), tile(2 * D_MODEL),
                  _resident((SB_WIDTH, D_MODEL)), _resident((DIL_OUT, D_MODEL)),
                  _resident((D_MODEL, D_MODEL))],
        out_specs=[tile(2 * D_MODEL), tile(D_MODEL), tile(D_MODEL), tile(SB_WIDTH), tile(DIL_OUT),
                   tile(DIL_OUT)],
        out_shape=[jax.ShapeDtypeStruct((TOK, 2 * D_MODEL), BF16),
                   jax.ShapeDtypeStruct((TOK, D_MODEL), BF16),
                   jax.ShapeDtypeStruct((TOK, D_MODEL), BF16),
                   jax.ShapeDtypeStruct((TOK, SB_WIDTH), BF16),
                   jax.ShapeDtypeStruct((TOK, DIL_OUT), F32),
                   jax.ShapeDtypeStruct((TOK, DIL_OUT), F32)],
        compiler_params=_params(("parallel",)),
    )(dx1, o_sb, o_dl, gates, w_sb_up, w_dil_up, w_out)


def _proj_bwd(dproj, dx1, x, g, w_in, send=None):
    widths = [p.shape[1] for p in dproj]

    def body(*refs):
        dx1_ref, x_ref, g_ref, w_ref, dx_ref, dg_ref = refs[len(widths):]

        @pl.when(pl.program_id(0) == 0)
        def _():
            dg_ref[...] = jnp.zeros_like(dg_ref)

        du = jnp.zeros((TM, D_MODEL), F32)
        c0 = 0
        for dp_ref, w in zip(refs, widths):
            du = du + _dot_nt(dp_ref[...].astype(BF16), w_ref[:, c0:c0 + w])
            c0 += w
        g_v = g_ref[...]
        n, r, _ = _rms_fwd(x_ref[...], g_v)
        dx, dg = _rms_bwd(du, n, r, g_v)
        dg_ref[...] += dg
        dx_ref[...] = dx1_ref[...] + dx

    tile = lambda w: pl.BlockSpec((TM, w), lambda i: (i, 0))
    return _call(
        body, send, name="proj_bwd", grid=(TOK // TM,),
        in_specs=[tile(w) for w in widths] + [tile(D_MODEL), tile(D_MODEL), _resident((1, D_MODEL)),
                                              _resident((D_MODEL, IN_WIDTH))],
        out_specs=[tile(D_MODEL), pl.BlockSpec((1, D_MODEL), lambda i: (0, 0))],
        out_shape=[jax.ShapeDtypeStruct((TOK, D_MODEL), F32),
                   jax.ShapeDtypeStruct((1, D_MODEL), F32)],
        scratch_shapes=[], semantics=("arbitrary",), operands=(*dproj, dx1, x, g, w_in))


def _atb_pieces(a, pieces, name, tm, tk=512):
    m = a.shape[1]
    widths = [p.shape[1] for p in pieces]
    n = sum(widths)
    nk = TOK // tk

    def body(a_ref, *refs):
        o_ref, acc_ref = refs[len(widths):]
        k = pl.program_id(1)

        @pl.when(k == 0)
        def _():
            acc_ref[...] = jnp.zeros_like(acc_ref)

        a_v = a_ref[...]
        c0 = 0
        for p_ref, w in zip(refs, widths):
            acc_ref[:, c0:c0 + w] += _dot_tn(a_v, p_ref[...].astype(BF16))
            c0 += w

        @pl.when(k == nk - 1)
        def _():
            o_ref[...] = acc_ref[...].astype(BF16)

    return pl.pallas_call(
        body, name=name, grid=(m // tm, nk),
        in_specs=[pl.BlockSpec((tk, tm), lambda i, k: (k, i))]
                 + [pl.BlockSpec((tk, w), lambda i, k: (k, 0)) for w in widths],
        out_specs=pl.BlockSpec((tm, n), lambda i, k: (i, 0)),
        out_shape=jax.ShapeDtypeStruct((m, n), BF16),
        scratch_shapes=[pltpu.VMEM((tm, n), F32)],
        compiler_params=_params(("parallel", "arbitrary")),
    )(a, *pieces)


def _atb_shards(a, b, name, tk=512):
    a_sharded = a.ndim == 3
    n, _, w = a.shape if a_sharded else b.shape
    other = (b if a_sharded else a).shape[1]

    def body(a_ref, b_ref, o_ref):
        acc = jnp.zeros(o_ref.shape, F32)
        for k0 in range(0, TOK, tk):
            acc = acc + _dot_tn(a_ref[k0:k0 + tk, :], b_ref[k0:k0 + tk, :])
        o_ref[...] = acc.astype(BF16)

    shard = pl.BlockSpec((None, TOK, w), lambda r: (r, 0, 0))
    whole = _resident((TOK, other))
    if a_sharded:
        out_spec, out_shape = pl.BlockSpec((w, other), lambda r: (r, 0)), (n * w, other)
    else:
        out_spec, out_shape = pl.BlockSpec((None, other, w), lambda r: (r, 0, 0)), (n, other, w)
    return pl.pallas_call(
        body, name=name, grid=(n,),
        in_specs=[shard, whole] if a_sharded else [whole, shard],
        out_specs=out_spec, out_shape=jax.ShapeDtypeStruct(out_shape, BF16),
        compiler_params=_params(("parallel",)),
    )(a, b)


def _atb(a, b, name, tm, tn, col_blocks=0, tk=512):
    m, n = a.shape[1], b.shape[1]
    nk = TOK // tk

    def body(a_ref, b_ref, o_ref, acc_ref):
        k = pl.program_id(2)

        @pl.when(k == 0)
        def _():
            acc_ref[...] = jnp.zeros_like(acc_ref)

        acc_ref[...] += _dot_tn(a_ref[...].astype(BF16), b_ref[...].astype(BF16))

        @pl.when(k == nk - 1)
        def _():
            if col_blocks:
                width = n // col_blocks
                for blk in range(col_blocks):
                    o_ref[blk] = acc_ref[:, blk * width:(blk + 1) * width].astype(BF16)
            else:
                o_ref[...] = acc_ref[...].astype(BF16)

    if col_blocks:
        out_spec = pl.BlockSpec((col_blocks, tm, n // col_blocks), lambda i, j, k: (0, i, 0))
        out_shape = jax.ShapeDtypeStruct((col_blocks, m, n // col_blocks), BF16)
    else:
        out_spec = pl.BlockSpec((tm, tn), lambda i, j, k: (i, j))
        out_shape = jax.ShapeDtypeStruct((m, n), BF16)
    return pl.pallas_call(
        body, name=name, grid=(m // tm, n // tn, nk),
        in_specs=[pl.BlockSpec((tk, tm), lambda i, j, k: (k, i)),
                  pl.BlockSpec((tk, tn), lambda i, j, k: (k, j))],
        out_specs=out_spec, out_shape=out_shape,
        scratch_shapes=[pltpu.VMEM((tm, tn), F32)],
        compiler_params=_params(("parallel", "parallel", "arbitrary")),
    )(a, b)


SB_PAIRS = SB_WIDTH // LANES


def _two_heads(v, lane0):
    zero = jnp.zeros_like(v)
    return jnp.where(lane0, v, zero), jnp.where(lane0, zero, v)


SB_QBLK = 256
N_SB_STEPS = SEQ // SB_QBLK


SB_KCHUNK = 2 * BLK
SB_ROWS = 2 * SB_QBLK
SB_DEAD = -104.0


def _log_keep(z):
    neg_z = -z
    return jnp.minimum(neg_z, 0.0) - jnp.log(1.0 + jnp.exp(jnp.minimum(z, neg_z)))


def _stack_heads(v, lane0):
    return jnp.concatenate(_two_heads(v, lane0), axis=0)


def _block_sums(v, tri):
    halves = (v[:, :BLK], v[:, BLK:])
    hi, lo = _split_bf16(jnp.concatenate(halves, axis=0))
    prod = _dot(jnp.concatenate([hi, lo], axis=0), tri)
    tri_sum = prod[:2 * SB_ROWS] + prod[2 * SB_ROWS:]
    sums = tuple(jnp.sum(h, axis=1, keepdims=True) for h in halves)
    return (tri_sum[:SB_ROWS], tri_sum[SB_ROWS:]), sums


def _sb_diag_mask():
    row = lax.broadcasted_iota(jnp.int32, (SB_ROWS, SB_KCHUNK), 0)
    col = lax.broadcasted_iota(jnp.int32, (SB_ROWS, SB_KCHUNK), 1)
    return col < jnp.where(row >= SB_QBLK, row - SB_QBLK, row)


def _sb_fwd(qkv, send=None):
    def body(q_ref, k_ref, v_ref, o_ref):
        i = pl.program_id(2)
        krow = lax.broadcasted_iota(jnp.int32, (BLK, BLK), 0)
        kcol = lax.broadcasted_iota(jnp.int32, (BLK, BLK), 1)
        later = (krow > kcol).astype(BF16)
        lane0 = lax.broadcasted_iota(jnp.int32, (SB_QBLK, LANES), 1) < HEAD_DIM
        q2 = _stack_heads(q_ref[0] * QK_SCALE, lane0)

        def chunk(c, carry, causal):
            acc, run = carry
            off = pl.multiple_of(c * SB_KCHUNK, SB_KCHUNK)
            z = _dot_nt(q2, k_ref[0, pl.ds(off, SB_KCHUNK), :])
            log_keep = _log_keep(z)
            if causal is not None:
                log_keep = jnp.where(causal, log_keep, 0.0)
            suffix, sums = _block_sums(log_keep, later)
            log_after = jnp.concatenate([suffix[0] + (run + sums[1]), suffix[1] + run], axis=1)
            a = jnp.exp(log_keep + z + log_after)
            if causal is not None:
                a = jnp.where(causal, a, 0.0)
            acc = acc + _dot(a.astype(BF16), v_ref[0, pl.ds(off, SB_KCHUNK), :])
            return acc, run + (sums[0] + sums[1])

        acc, run = chunk(i, (jnp.zeros((SB_ROWS, LANES), F32), jnp.zeros((SB_ROWS, 1), F32)), _sb_diag_mask())

        def live(state):
            t, _, run = state
            return jnp.logical_and(t < i, jnp.max(run) > SB_DEAD)

        def trip(state):
            t, acc, run = state
            acc, run = chunk(i - 1 - t, (acc, run), None)
            return t + 1, acc, run

        _, acc, _ = lax.while_loop(live, trip, (jnp.int32(0), acc, run))
        o_ref[0] = jnp.where(lane0, acc[:SB_QBLK], acc[SB_QBLK:]).astype(BF16)

    blk = pl.BlockSpec((1, SB_QBLK, LANES), lambda b, h, i: (b, i, h))
    return _call(
        body, send, name="sb_fwd", grid=(B_LOC, SB_PAIRS, N_SB_STEPS),
        in_specs=[blk,
                  pl.BlockSpec((1, SEQ, LANES), lambda b, h, i: (b, 0, SB_PAIRS + h)),
                  pl.BlockSpec((1, SEQ, LANES), lambda b, h, i: (b, 0, 2 * SB_PAIRS + h))],
        out_specs=[blk], out_shape=[jax.ShapeDtypeStruct((B_LOC, SEQ, SB_WIDTH), BF16)],
        scratch_shapes=[], semantics=("parallel", "parallel", "arbitrary"), operands=(qkv, qkv, qkv))


def _sb_bwd(qkv, d_o, send=None):
    def body(q_ref, k_ref, v_ref, do_ref, dq_ref, dk_ref, dv_ref, dk_acc, dv_acc, z_scr, keep_scr):
        i = pl.program_id(2)
        krow = lax.broadcasted_iota(jnp.int32, (BLK, BLK), 0)
        kcol = lax.broadcasted_iota(jnp.int32, (BLK, BLK), 1)
        upto = (krow <= kcol).astype(BF16)
        earlier = (krow < kcol).astype(BF16)
        lane0 = lax.broadcasted_iota(jnp.int32, (SB_QBLK, LANES), 1) < HEAD_DIM
        q2 = _stack_heads(q_ref[0] * QK_SCALE, lane0)
        do2 = _stack_heads(do_ref[0], lane0)

        def keep_sum(c, causal):
            off = pl.multiple_of(c * SB_KCHUNK, SB_KCHUNK)
            z = _dot_nt(q2, k_ref[0, pl.ds(off, SB_KCHUNK), :])
            log_keep = _log_keep(z)
            if causal is not None:
                log_keep = jnp.where(causal, log_keep, 0.0)
            z_scr[c] = z
            keep_scr[c] = log_keep
            return jnp.sum(log_keep, axis=1, keepdims=True)

        def live(state):
            t, run = state
            return jnp.logical_and(t < i, jnp.max(run) > SB_DEAD)

        walked, tot2 = lax.while_loop(live, lambda s: (s[0] + 1, s[1] + keep_sum(i - 1 - s[0], None)),
                                      (jnp.int32(0), keep_sum(i, _sb_diag_mask())))
        first = i - walked

        @pl.when(i == 0)
        def _():
            dk_acc[...] = jnp.zeros_like(dk_acc)
            dv_acc[...] = jnp.zeros_like(dv_acc)

        def chunk(c, carry, causal):
            dq, pre_keep, pre_e = carry
            off = pl.multiple_of(c * SB_KCHUNK, SB_KCHUNK)
            k_c = k_ref[0, pl.ds(off, SB_KCHUNK), :]
            v_c = v_ref[0, pl.ds(off, SB_KCHUNK), :]
            d_a = _dot_nt(do2, v_c)
            log_keep = keep_scr[c]
            log_beta = log_keep + z_scr[c]
            prefix, sums = _block_sums(log_keep, upto)
            inclusive = jnp.concatenate([prefix[0], prefix[1] + sums[0]], axis=1)
            a = jnp.exp(log_beta + ((tot2 - pre_keep) - inclusive))
            if causal is not None:
                a = jnp.where(causal, a, 0.0)
            e = d_a * a
            e_prefix, e_sums = _block_sums(e, earlier)
            before = jnp.concatenate([e_prefix[0] + pre_e, e_prefix[1] + (pre_e + e_sums[0])], axis=1)
            dz = e - (e + before) * jnp.exp(log_beta)
            if causal is not None:
                dz = jnp.where(causal, dz, 0.0)
            dz = dz.astype(BF16)
            dq = dq + _dot(dz, k_c)
            dk_acc[pl.ds(off, SB_KCHUNK), :] += _dot_tn(dz, q2)
            dv_acc[pl.ds(off, SB_KCHUNK), :] += _dot_tn(a.astype(BF16), do2)
            return dq, pre_keep + (sums[0] + sums[1]), pre_e + (e_sums[0] + e_sums[1])

        zero_col = jnp.zeros((SB_ROWS, 1), F32)
        carry = lax.fori_loop(first, i, lambda t, c: chunk(t, c, None),
                              (jnp.zeros((SB_ROWS, LANES), F32), zero_col, zero_col))
        dq, _, _ = chunk(i, carry, _sb_diag_mask())
        dq_ref[0] = (jnp.where(lane0, dq[:SB_QBLK], dq[SB_QBLK:]) * QK_SCALE).astype(BF16)

        @pl.when(i == N_SB_STEPS - 1)
        def _():
            dk_ref[0] = dk_acc[...].astype(BF16)
            dv_ref[0] = dv_acc[...].astype(BF16)

    blk = pl.BlockSpec((1, SB_QBLK, LANES), lambda b, h, i: (b, i, h))
    whole = lambda c: pl.BlockSpec((1, SEQ, LANES), lambda b, h, i: (b, 0, c * SB_PAIRS + h))
    out = jax.ShapeDtypeStruct((B_LOC, SEQ, SB_WIDTH), BF16)
    return _call(
        body, send, name="sb_bwd", grid=(B_LOC, SB_PAIRS, N_SB_STEPS),
        in_specs=[blk, whole(1), whole(2), blk],
        out_specs=[blk, whole(0), whole(0)],
        out_shape=[out, out, out],
        scratch_shapes=[pltpu.VMEM((SEQ, LANES), F32), pltpu.VMEM((SEQ, LANES), F32),
                        pltpu.VMEM((N_SB_STEPS, SB_ROWS, SB_KCHUNK), F32),
                        pltpu.VMEM((N_SB_STEPS, SB_ROWS, SB_KCHUNK), F32)],
        semantics=("parallel", "parallel", "arbitrary"), operands=(qkv, qkv, qkv, d_o))


DIL_GROUPS = len(DIL_PAIRS)
DIL_QBLOCKS = SEQ // BLK


def _residue_rows(j, dilation):
    length = SEQ // dilation
    return pl.ds(j, length, stride=dilation) if dilation > 1 else pl.ds(0, length)


def _gather_residues(src_ref, dst_ref, dst_off, dilation, scale=None):
    length = SEQ // dilation
    for j in range(dilation):
        v = src_ref[_residue_rows(j, dilation), :]
        if scale is not None:
            v = v * scale
        dst_ref[dst_off + j * length:dst_off + (j + 1) * length, :] = v.astype(dst_ref.dtype)


def _scatter_residues(src_ref, src_off, dst_ref, dilation):
    length = SEQ // dilation
    for j in range(dilation):
        dst_ref[_residue_rows(j, dilation), :] = (
            src_ref[src_off + j * length:src_off + (j + 1) * length, :].astype(dst_ref.dtype))


def _dil_geometry(group, pair):
    dilation = DIL_PAIRS[group][1]
    row = lax.broadcasted_iota(jnp.int32, (2 * BLK, 2 * BLK), 0)
    col = lax.broadcasted_iota(jnp.int32, (2 * BLK, 2 * BLK), 1)
    second = row >= BLK
    steps = BLK + jnp.where(second, row - BLK, row) - col
    coef = -ALIBI_MAX_BIAS / DIL_HEADS * math.log(2.0)
    first_head = float(4 * group + 1) + 2.0 * pair.astype(F32)
    slope = jnp.exp(coef * (first_head + jnp.where(second, 1.0, 0.0)))
    bias = slope * (steps * dilation).astype(F32)
    valid = jnp.logical_and(steps >= 0, steps <= BLK)
    return bias, valid, col >= BLK


def _dil_tile_scores(q2, kk, geometry, has_prev):
    bias, valid, own = geometry
    ok = jnp.logical_and(valid, jnp.logical_or(own, has_prev))
    return jnp.where(ok, _dot_nt(q2, kk) - bias, NEG_BIG)


def _head_col(v, lane_mask):
    return jnp.max(jnp.where(lane_mask, v, NEG_BIG), axis=1, keepdims=True)


def _dil_fwd(qkv, send=None):
    def body(*refs):
        ins, (o_ref, lse_ref), (qs, ks, vs, o_res, lse_res) = refs[:9], refs[9:11], refs[11:16]
        o_grp, lse_grp = refs[16:19], refs[19:22]
        pair = pl.program_id(1)
        lane0 = lax.broadcasted_iota(jnp.int32, (BLK, LANES), 1) < HEAD_DIM
        ks[0:BLK, :] = jnp.zeros((BLK, LANES), BF16)
        vs[0:BLK, :] = jnp.zeros((BLK, LANES), BF16)
        for grp, (_, dilation) in enumerate(DIL_PAIRS):
            q_ref, k_ref, v_ref = ins[3 * grp:3 * grp + 3]
            per_residue = DIL_QBLOCKS // dilation
            _gather_residues(q_ref, qs, 0, dilation, QK_SCALE)
            _gather_residues(k_ref, ks, BLK, dilation)
            _gather_residues(v_ref, vs, BLK, dilation)
            geometry = _dil_geometry(grp, pair)

            def step(blk, _):
                off = pl.multiple_of(blk * BLK, BLK)
                q2 = _stack_heads(qs[pl.ds(off, BLK), :], lane0)
                s = _dil_tile_scores(q2, ks[pl.ds(off, 2 * BLK), :], geometry, blk % per_residue != 0)
                m = jnp.max(s, axis=1, keepdims=True)
                p = jnp.exp(s - m)
                den = jnp.sum(p, axis=1, keepdims=True)
                out = _dot(p.astype(BF16), vs[pl.ds(off, 2 * BLK), :]) / den
                lse = m + jnp.log(den)
                o_res[pl.ds(off, BLK), :] = jnp.where(lane0, out[:BLK], out[BLK:])
                lse_res[pl.ds(off, BLK), :] = jnp.where(lane0, lse[:BLK], lse[BLK:])
                return 0

            lax.fori_loop(0, DIL_QBLOCKS, step, 0, unroll=4)
            _scatter_residues(o_res, 0, o_grp[grp], dilation)
            _scatter_residues(lse_res, 0, lse_grp[grp], dilation)

        for r0 in range(0, SEQ, 2 * BLK):
            rows = slice(r0, r0 + 2 * BLK)
            ls = [lse_grp[g][rows, :] for g in range(DIL_GROUPS)]
            m = jnp.maximum(jnp.maximum(ls[0], ls[1]), ls[2])
            w = [jnp.exp(l - m) for l in ls]
            den = w[0] + w[1] + w[2]
            o_ref[rows, :] = (w[0] * o_grp[0][rows, :] + w[1] * o_grp[1][rows, :] + w[2] * o_grp[2][rows, :]) / den
            lse_ref[rows, :] = m + jnp.log(den)

    def col(part, grp):
        return pl.BlockSpec((None, SEQ, LANES), lambda b, p: (b, 0, 6 * part + 2 * grp + p))

    out_spec = pl.BlockSpec((None, SEQ, LANES), lambda b, p: (b, 0, p))
    out = jax.ShapeDtypeStruct((B_LOC, SEQ, DIL_OUT), F32)
    return _call(
        body, send, name="dil_fwd", grid=(B_LOC, DIL_OUT // LANES),
        in_specs=[col(part, grp) for grp in range(DIL_GROUPS) for part in range(3)],
        out_specs=[out_spec, out_spec], out_shape=[out, out],
        scratch_shapes=[pltpu.VMEM((SEQ, LANES), BF16), pltpu.VMEM((SEQ + BLK, LANES), BF16),
                        pltpu.VMEM((SEQ + BLK, LANES), BF16), pltpu.VMEM((SEQ, LANES), F32),
                        pltpu.VMEM((SEQ, LANES), F32)] + [pltpu.VMEM((SEQ, LANES), F32)] * (2 * DIL_GROUPS),
        semantics=("parallel", "parallel"), operands=[qkv] * 9)


def _dil_bwd(qkv, d_o, lse, dsum, send=None):
    def body(*refs):
        ins, (do_ref, lse_ref, dsum_ref), outs = refs[:9], refs[9:12], refs[12:21]
        qs, ks, vs, dos, lse_res, dsum_res, dq_res, dk_acc, dv_acc = refs[21:]
        pair = pl.program_id(1)
        lane0 = lax.broadcasted_iota(jnp.int32, (BLK, LANES), 1) < HEAD_DIM
        lane1 = jnp.logical_not(lane0)
        ks[0:BLK, :] = jnp.zeros((BLK, LANES), BF16)
        vs[0:BLK, :] = jnp.zeros((BLK, LANES), BF16)
        for grp, (_, dilation) in enumerate(DIL_PAIRS):
            q_ref, k_ref, v_ref = ins[3 * grp:3 * grp + 3]
            dq_ref, dk_ref, dv_ref = outs[3 * grp:3 * grp + 3]
            per_residue = DIL_QBLOCKS // dilation
            _gather_residues(q_ref, qs, 0, dilation, QK_SCALE)
            _gather_residues(k_ref, ks, BLK, dilation)
            _gather_residues(v_ref, vs, BLK, dilation)
            _gather_residues(do_ref, dos, 0, dilation)
            _gather_residues(lse_ref, lse_res, 0, dilation)
            _gather_residues(dsum_ref, dsum_res, 0, dilation)
            dk_acc[...] = jnp.zeros_like(dk_acc)
            dv_acc[...] = jnp.zeros_like(dv_acc)
            geometry = _dil_geometry(grp, pair)

            def step(blk, _):
                off = pl.multiple_of(blk * BLK, BLK)
                q2 = _stack_heads(qs[pl.ds(off, BLK), :], lane0)
                do2 = _stack_heads(dos[pl.ds(off, BLK), :], lane0)
                kk = ks[pl.ds(off, 2 * BLK), :]
                vv = vs[pl.ds(off, 2 * BLK), :]
                lse_blk = lse_res[pl.ds(off, BLK), :]
                dsum_blk = dsum_res[pl.ds(off, BLK), :]
                lse2 = jnp.concatenate([_head_col(lse_blk, lane0), _head_col(lse_blk, lane1)], axis=0)
                dsum2 = jnp.concatenate([_head_col(dsum_blk, lane0), _head_col(dsum_blk, lane1)], axis=0)
                s = _dil_tile_scores(q2, kk, geometry, blk % per_residue != 0)
                p = jnp.exp(s - lse2)
                ds = (p * (_dot_nt(do2, vv) - dsum2)).astype(BF16)
                dq2 = _dot(ds, kk)
                dq_res[pl.ds(off, BLK), :] = jnp.where(lane0, dq2[:BLK], dq2[BLK:]) * QK_SCALE
                dk_acc[pl.ds(off, 2 * BLK), :] += _dot_tn(ds, q2)
                dv_acc[pl.ds(off, 2 * BLK), :] += _dot_tn(p.astype(BF16), do2)
                return 0

            lax.fori_loop(0, DIL_QBLOCKS, step, 0, unroll=4)
            _scatter_residues(dq_res, 0, dq_ref, dilation)
            _scatter_residues(dk_acc, BLK, dk_ref, dilation)
            _scatter_residues(dv_acc, BLK, dv_ref, dilation)

    def col(part, grp):
        return pl.BlockSpec((None, SEQ, LANES), lambda b, p: (b, 0, 6 * part + 2 * grp + p))

    slot = pl.BlockSpec((None, SEQ, LANES), lambda b, p: (b, 0, p))
    out = jax.ShapeDtypeStruct((B_LOC, SEQ, DIL_OUT), F32)
    return _call(
        body, send, name="dil_bwd", grid=(B_LOC, DIL_OUT // LANES),
        in_specs=[col(part, grp) for grp in range(DIL_GROUPS) for part in range(3)] + [slot] * 3,
        out_specs=[slot] * 9, out_shape=[out] * 9,
        scratch_shapes=[pltpu.VMEM((SEQ, LANES), BF16), pltpu.VMEM((SEQ + BLK, LANES), BF16),
                        pltpu.VMEM((SEQ + BLK, LANES), BF16), pltpu.VMEM((SEQ, LANES), BF16),
                        pltpu.VMEM((SEQ, LANES), F32), pltpu.VMEM((SEQ, LANES), F32),
                        pltpu.VMEM((SEQ, LANES), F32), pltpu.VMEM((SEQ + BLK, LANES), F32),
                        pltpu.VMEM((SEQ + BLK, LANES), F32)],
        semantics=("parallel", "parallel"), operands=[qkv] * 9 + [d_o, lse, dsum])


def _peers():
    x, y, c = lax.axis_index("x"), lax.axis_index("y"), lax.axis_index("c")
    me = 4 * x + 2 * y + c
    peers = []
    for mask in range(1, N_DEV):
        px = 1 - x if mask & 4 else x
        py = 1 - y if mask & 2 else y
        pc = 1 - c if mask & 1 else c
        peers.append(((px, py, pc), 4 * px + 2 * py + pc))
    return me, peers


def _all_gather(shard, name):
    def body(src_ref, out_ref, send_sems, recv_sems, local_sem):
        x, y, c = lax.axis_index("x"), lax.axis_index("y"), lax.axis_index("c")
        sibling = (x, y, 1 - c)
        chips = [(1 - x, y), (x, 1 - y), (1 - x, 1 - y)]

        def slot(px, py, pc):
            return out_ref.at[4 * px + 2 * py + pc]

        def copy(k, block, to, src=None):
            return pltpu.make_async_remote_copy(
                src_ref=slot(*block) if src is None else src, dst_ref=slot(*block),
                send_sem=send_sems.at[k], recv_sem=recv_sems.at[k], device_id=to,
                device_id_type=pl.DeviceIdType.MESH)

        mine = pltpu.make_async_copy(src_ref, slot(x, y, c), local_sem)
        mine.start()
        first = [copy(0, (x, y, c), sibling, src=src_ref)]
        first += [copy(1 + j, (x, y, c), (*chip, c), src=src_ref) for j, chip in enumerate(chips)]
        for cp in first:
            cp.start()
        passed = [copy(4 + j, (*chip, c), sibling) for j, chip in enumerate(chips)]
        for j, chip in enumerate(chips):
            copy(1 + j, (*chip, c), (x, y, c)).wait_recv()
            passed[j].start()
        copy(0, sibling, (x, y, c)).wait_recv()
        for j, chip in enumerate(chips):
            copy(4 + j, (*chip, 1 - c), (x, y, c)).wait_recv()
        for cp in first + passed:
            cp.wait_send()
        mine.wait()

    return pl.pallas_call(
        body, name=name,
        in_specs=[pl.BlockSpec(memory_space=pl.ANY)],
        out_specs=pl.BlockSpec(memory_space=pl.ANY),
        out_shape=jax.ShapeDtypeStruct((N_DEV,) + shard.shape, shard.dtype),
        scratch_shapes=[pltpu.SemaphoreType.DMA((N_DEV - 1,)), pltpu.SemaphoreType.DMA((N_DEV - 1,)),
                        pltpu.SemaphoreType.DMA],
    )(shard)


def _call(body, send, *, name, grid, in_specs, out_specs, out_shape, scratch_shapes, semantics, operands):
    if send is None:
        return pl.pallas_call(
            body, name=name, grid=grid, in_specs=in_specs, out_specs=out_specs, out_shape=out_shape,
            scratch_shapes=scratch_shapes, compiler_params=_params(semantics))(*operands), []
    srcs, kinds = [s for s, _ in send], [k for _, k in send]
    n, n_in, n_out, n_scr = len(srcs), len(in_specs), len(out_specs), len(scratch_shapes)
    steps = math.prod(grid)
    relay_step = (3 * steps) // 4

    def plan(refs):
        src_refs, land_refs = refs[n_in:n_in + n], refs[n_in + n + n_out:n_in + 2 * n + n_out]
        send_sems, recv_sems, local_sems = refs[-3:]
        x, y, c = lax.axis_index("x"), lax.axis_index("y"), lax.axis_index("c")
        me, peers = _peers()
        first, relayed_in, relayed_out, arrivals, sends, own = [], [], [], [], [], []
        for a, kind in enumerate(kinds):
            def copy(k, src, dst_slot, to):
                return pltpu.make_async_remote_copy(
                    src_ref=src, dst_ref=land_refs[a].at[dst_slot], send_sem=send_sems.at[a * (N_DEV - 1) + k],
                    recv_sem=recv_sems.at[a * (N_DEV - 1) + k], device_id=to, device_id_type=pl.DeviceIdType.MESH)

            if kind == "gather_by_chip":
                idx = lambda px, py, pc: 4 * px + 2 * py + pc
                chips = [(1 - x, y), (x, 1 - y), (1 - x, 1 - y)]
                mine = [copy(0, src_refs[a], me, (x, y, 1 - c))]
                arrivals.append(copy(0, src_refs[a], idx(x, y, 1 - c), (x, y, 1 - c)))
                for j, (px, py) in enumerate(chips):
                    mine.append(copy(1 + j, src_refs[a], me, (px, py, c)))
                    relayed_in.append(copy(1 + j, src_refs[a], idx(px, py, c), (px, py, c)))
                    relayed_out.append(copy(4 + j, land_refs[a].at[idx(px, py, c)], idx(px, py, c), (x, y, 1 - c)))
                    arrivals.append(copy(4 + j, src_refs[a], idx(px, py, 1 - c), (x, y, 1 - c)))
                first += mine
                sends += mine + relayed_out[-3:]
                own.append(pltpu.make_async_copy(src_refs[a], land_refs[a].at[me], local_sems.at[a]))
            elif kind == "scatter_by_chip":
                for k, (px, py) in enumerate([(1 - x, y), (x, 1 - y), (1 - x, 1 - y)]):
                    first.append(copy(k, src_refs[a].at[2 * px + py], 2 * x + y, (px, py, c)))
                    arrivals.append(copy(k, src_refs[a].at[2 * px + py], 2 * px + py, (px, py, c)))
                sends += first[-3:]
                own.append(pltpu.make_async_copy(src_refs[a].at[2 * x + y], land_refs[a].at[2 * x + y],
                                                 local_sems.at[a]))
            else:
                part = (lambda i: src_refs[a].at[i]) if kind == "scatter" else (lambda i: src_refs[a])
                for k, (peer, peer_idx) in enumerate(peers):
                    first.append(copy(k, part(peer_idx), me, peer))
                    arrivals.append(copy(k, part(peer_idx), peer_idx, peer))
                sends += first[-(N_DEV - 1):]
                own.append(pltpu.make_async_copy(part(me), land_refs[a].at[me], local_sems.at[a]))
        return first, relayed_in, relayed_out, arrivals, sends, own

    def wrapped(*refs):
        step = 0
        for axis, size in enumerate(grid):
            step = step * size + pl.program_id(axis)

        @pl.when(step == 0)
        def _():
            first, _, _, _, _, own = plan(refs)
            for cp in first + own:
                cp.start()

        if "gather_by_chip" in kinds:
            @pl.when(step == relay_step)
            def _():
                _, relayed_in, relayed_out, _, _, _ = plan(refs)
                for cp_in, cp_out in zip(relayed_in, relayed_out):
                    cp_in.wait_recv()
                    cp_out.start()

        body(*refs[:n_in], *refs[n_in + n:n_in + n + n_out], *refs[n_in + 2 * n + n_out:n_in + 2 * n + n_out + n_scr])

        @pl.when(step == steps - 1)
        def _():
            _, _, _, arrivals, sends, own = plan(refs)
            for cp in arrivals:
                cp.wait_recv()
            for cp in sends:
                cp.wait_send()
            for cp in own:
                cp.wait()

    anywhere = pl.BlockSpec(memory_space=pl.ANY)
    lands = [jax.ShapeDtypeStruct((N_DEV // 2 if k == "scatter_by_chip" else N_DEV,) + s.shape[-2:], s.dtype)
             for s, k in send]
    out = pl.pallas_call(
        wrapped, name=name, grid=grid,
        in_specs=list(in_specs) + [anywhere] * n, out_specs=list(out_specs) + [anywhere] * n,
        out_shape=list(out_shape) + lands,
        scratch_shapes=list(scratch_shapes) + [pltpu.SemaphoreType.DMA((n * (N_DEV - 1),)),
                                               pltpu.SemaphoreType.DMA((n * (N_DEV - 1),)),
                                               pltpu.SemaphoreType.DMA((n,))],
        compiler_params=_params(("arbitrary",) * len(grid)),
    )(*operands, *srcs)
    return out[:n_out], list(out[n_out:])


def _pair_swap(blocks):
    def body(src_ref, out_ref, send_sems, recv_sems):
        x, y, c = lax.axis_index("x"), lax.axis_index("y"), lax.axis_index("c")
        copies = [pltpu.make_async_remote_copy(
            src_ref=src_ref.at[2 * chip + (1 - c)], dst_ref=out_ref.at[chip], send_sem=send_sems.at[chip],
            recv_sem=recv_sems.at[chip], device_id=(x, y, 1 - c), device_id_type=pl.DeviceIdType.MESH)
            for chip in range(N_DEV // 2)]
        for cp in copies:
            cp.start()
        for cp in copies:
            cp.wait()

    return pl.pallas_call(
        body, name="pair_swap_grad_w_in",
        in_specs=[pl.BlockSpec(memory_space=pl.ANY)], out_specs=pl.BlockSpec(memory_space=pl.ANY),
        out_shape=jax.ShapeDtypeStruct((N_DEV // 2,) + blocks.shape[1:], blocks.dtype),
        scratch_shapes=[pltpu.SemaphoreType.DMA((N_DEV // 2,)), pltpu.SemaphoreType.DMA((N_DEV // 2,))],
    )(blocks)


def _pair_sum(blocks, swapped, core):
    _, rows, cols = swapped.shape
    tile_rows = _row_tile(rows)

    def body(core_ref, mine_ref, theirs_ref, o_ref):
        o_ref[...] = (mine_ref[...].astype(F32) + theirs_ref[...].astype(F32)).astype(o_ref.dtype)

    return pl.pallas_call(
        body, name="pair_sum_grad_w_in",
        grid_spec=pltpu.PrefetchScalarGridSpec(
            num_scalar_prefetch=1, grid=(N_DEV // 2, rows // tile_rows),
            in_specs=[pl.BlockSpec((None, tile_rows, cols), lambda j, i, core_ref: (2 * j + core_ref[0], i, 0)),
                      pl.BlockSpec((None, tile_rows, cols), lambda j, i, core_ref: (j, i, 0))],
            out_specs=pl.BlockSpec((None, tile_rows, cols), lambda j, i, core_ref: (j, i, 0))),
        out_shape=jax.ShapeDtypeStruct(swapped.shape, swapped.dtype),
        compiler_params=_params(("parallel", "parallel")),
    )(core, blocks, swapped)


def _sum_in_device_order(land_ref):
    acc = land_ref[0].astype(F32)
    for j in range(1, land_ref.shape[0]):
        acc = acc + land_ref[j].astype(F32)
    return acc


def _adam_math(w, g, m, v):
    c1 = 1.0 - ADAM_B1 ** ADAM_STEP
    c2 = 1.0 - ADAM_B2 ** ADAM_STEP
    m_new = ADAM_B1 * m + (1.0 - ADAM_B1) * g
    v_new = ADAM_B2 * v + (1.0 - ADAM_B2) * (g * g)
    delta = -ADAM_LR * ((m_new / c1) / (jnp.sqrt(v_new / c2) + ADAM_EPS) + ADAM_WD * w)
    return delta, m_new, v_new


def _row_tile(rows):
    return max(t for t in range(8, 257, 8) if rows % t == 0) if rows % 8 == 0 else rows


def _sum_update(land, w, m, v, name):
    slots, rows, cols = land.shape
    tile_rows = _row_tile(rows)

    def body(land_ref, w_ref, m_ref, v_ref, g_ref, d_ref, nm_ref, nv_ref):
        g = _sum_in_device_order(land_ref)
        g_ref[...] = g
        d_ref[...], nm_ref[...], nv_ref[...] = _adam_math(w_ref[...], g, m_ref[...], v_ref[...])

    tile = pl.BlockSpec((None, tile_rows, cols), lambda i: (0, i, 0))
    out = jax.ShapeDtypeStruct((1, rows, cols), F32)
    return pl.pallas_call(
        body, name=name, grid=(rows // tile_rows,),
        in_specs=[pl.BlockSpec((slots, tile_rows, cols), lambda i: (0, i, 0)), tile, tile, tile],
        out_specs=[tile] * 4, out_shape=[out] * 4,
        compiler_params=_params(("parallel",)),
    )(land, w, m, v)


def _sum_gains(land):
    def body(land_ref, o_ref):
        o_ref[...] = _sum_in_device_order(land_ref)

    return pl.pallas_call(
        body, name="sum_gain_grads", grid=(1,),
        in_specs=[pl.BlockSpec(land.shape, lambda i: (0, 0, 0))],
        out_specs=pl.BlockSpec(land.shape[1:], lambda i: (0, 0)),
        out_shape=jax.ShapeDtypeStruct(land.shape[1:], F32),
    )(land)


def _adamw(w, g, m, v, name):
    def body(w_ref, g_ref, m_ref, v_ref, d_ref, nm_ref, nv_ref):
        d_ref[...], nm_ref[...], nv_ref[...] = _adam_math(w_ref[...], g_ref[...], m_ref[...], v_ref[...])

    whole = pl.BlockSpec(w.shape, lambda i: (0, 0))
    out = jax.ShapeDtypeStruct(w.shape, F32)
    return pl.pallas_call(
        body, name=name, grid=(1,),
        in_specs=[whole] * 4, out_specs=[whole] * 3, out_shape=[out] * 3,
    )(w, g, m, v)


GROUP_FFN = ("w_ffn_in", "w_ffn_out")
GROUP_MIX = ("w_sb_up", "w_dil_up", "w_out")
COL_SHARDED = ("w_in", "w_sb_up", "w_dil_up", "w_ffn_in")


def _full_from_shards(name, slots):
    _, r, c = slots.shape
    if name in COL_SHARDED:
        return slots.transpose(1, 0, 2).reshape(r, N_DEV * c)
    return slots.reshape(N_DEV * r, c)


def _shards_from_full(name, full):
    rows, cols = full.shape
    if name in COL_SHARDED:
        return full.reshape(rows, N_DEV, cols // N_DEV).transpose(1, 0, 2)
    return full.reshape(N_DEV, rows // N_DEV, cols)


def _local_step(x, target, g_mix, g_ffn, g_fin, w_in, shards=None, rest=None):
    gather = lambda names, kind: None if shards is None else [(shards[n], kind) for n in names]
    scatter = lambda blocks: None if shards is None else [(t, "scatter") for t in blocks]
    landed = lambda blocks, lands: lands if lands else blocks

    w = {"w_in": w_in}
    if shards is None:
        w.update(rest)
        w["w_ffn_in"] = _shards_from_full("w_ffn_in", rest["w_ffn_in"])
    qkv_sb, qkv_dl, gates, u = _norm_proj(x, g_mix, w["w_in"])
    qkv_sb = qkv_sb.reshape(B_LOC, SEQ, 3 * SB_WIDTH)
    qkv_dl = qkv_dl.reshape(B_LOC, SEQ, 3 * DIL_WIDTH)
    (o_sb,), lands = _sb_fwd(qkv_sb, gather(GROUP_FFN, "gather_by_chip"))
    if lands:
        w["w_ffn_in"], w["w_ffn_out"] = lands[0], _full_from_shards("w_ffn_out", lands[1])
    o_sb = o_sb.reshape(TOK, SB_WIDTH)
    (o_dl, lse), lands = _dil_fwd(qkv_dl, gather(GROUP_MIX, "gather"))
    w.update({n: _full_from_shards(n, t) for n, t in zip(GROUP_MIX, lands)})
    o_dl = o_dl.reshape(TOK, DIL_OUT)

    x1, merged = _mix_out(x, o_sb, o_dl, gates, w["w_sb_up"], w["w_dil_up"], w["w_out"])
    loss, dx1, u2, act, dh, dx2, dg_fin, dg_ffn = _ffn_fwd_bwd(x1, target, g_ffn, g_fin, w["w_ffn_in"], w["w_ffn_out"])
    dgates, dy_sb, dy_dl, do_sb, do_dl, dsum = _mix_bwd(dx1, o_sb, o_dl, gates, w["w_sb_up"], w["w_dil_up"], w["w_out"])
    blocks = {
        "w_sb_up": _atb(o_sb, dy_sb, "grad_w_sb_up", SB_WIDTH, D_MODEL, col_blocks=N_DEV),
        "w_dil_up": _atb(o_dl, dy_dl, "grad_w_dil_up", DIL_OUT, D_MODEL, col_blocks=N_DEV),
        "w_out": _shards_from_full("w_out", _atb(merged, dx1, "grad_w_out", D_MODEL, D_MODEL)),
        "w_ffn_in": _atb_shards(u2, dh, "grad_w_ffn_in"),
        "w_ffn_out": _shards_from_full("w_ffn_out", _atb_shards(act, dx2, "grad_w_ffn_out")),
    }
    grads = {}

    early, late = ("w_ffn_in",), ("w_ffn_out",) + GROUP_MIX
    early_blocks = [blocks[n] for n in early]
    (dq_sb, dk_sb, dv_sb), lands = _sb_bwd(qkv_sb, do_sb.reshape(B_LOC, SEQ, SB_WIDTH), scatter(early_blocks))
    grads.update(zip(early, landed(early_blocks, lands)))
    as_batch = lambda t: t.reshape(B_LOC, SEQ, DIL_OUT)
    late_blocks = [blocks[n] for n in late]
    d_dl, lands = _dil_bwd(qkv_dl, as_batch(do_dl), lse, as_batch(dsum), scatter(late_blocks))
    grads.update(zip(late, landed(late_blocks, lands)))
    flat = lambda t: t.reshape(TOK, -1)
    dproj = ([flat(dq_sb), flat(dk_sb), flat(dv_sb)]
             + [flat(d_dl[3 * grp + part]) for part in range(3) for grp in range(DIL_GROUPS)] + [dgates])

    w_in_blocks = _shards_from_full("w_in", _atb_pieces(u, dproj, "grad_w_in", D_MODEL // 2))
    if shards is None:
        grads["w_in"] = w_in_blocks
        send = None
    else:
        core = lax.axis_index("c").astype(jnp.int32).reshape(1)
        send = [(_pair_sum(w_in_blocks, _pair_swap(w_in_blocks), core), "scatter_by_chip")]
    (grad_x, dg_mix), lands = _proj_bwd(dproj, dx1, x, g_mix, w["w_in"], send)
    if lands:
        grads["w_in"] = lands[0]
    gain_grads = jnp.concatenate([dg_mix, dg_ffn, dg_fin], axis=0)
    return loss, grad_x, gain_grads, grads


def kernel(x, norm_mix_g, w_in, w_sb_up, w_dil_up, w_out, norm_ffn_g, w_ffn_in, w_ffn_out, norm_final_g, loss_target, m_norm_mix_g, m_w_in, m_w_sb_up, m_w_dil_up, m_w_out, m_norm_ffn_g, m_w_ffn_in, m_w_ffn_out, m_norm_final_g, v_norm_mix_g, v_w_in, v_w_sb_up, v_w_dil_up, v_w_out, v_norm_ffn_g, v_w_ffn_in, v_w_ffn_out, v_norm_final_g):
    mats = {"w_in": w_in, "w_sb_up": w_sb_up, "w_dil_up": w_dil_up, "w_out": w_out,
            "w_ffn_in": w_ffn_in, "w_ffn_out": w_ffn_out}
    moments_m = {"w_in": m_w_in, "w_sb_up": m_w_sb_up, "w_dil_up": m_w_dil_up, "w_out": m_w_out,
                 "w_ffn_in": m_w_ffn_in, "w_ffn_out": m_w_ffn_out}
    moments_v = {"w_in": v_w_in, "w_sb_up": v_w_sb_up, "w_dil_up": v_w_dil_up, "w_out": v_w_out,
                 "w_ffn_in": v_w_ffn_in, "w_ffn_out": v_w_ffn_out}
    gathered_w_in = _all_gather(w_in[0].astype(BF16), "all_gather_w_in")
    g_fin = norm_final_g.reshape(1, D_MODEL)
    loss, grad_x, gain_grads, grad_slots = _local_step(
        x.reshape(TOK, D_MODEL), loss_target.reshape(TOK, D_MODEL), norm_mix_g, norm_ffn_g, g_fin,
        _full_from_shards("w_in", gathered_w_in),
        shards={name: mats[name][0].astype(BF16) for name in GROUP_FFN + GROUP_MIX})

    gain_rows = jnp.concatenate([gain_grads, jnp.tile(loss, (1, D_MODEL // LANES)),
                                 jnp.zeros((8 - 4, D_MODEL), F32)], axis=0)
    g_gains = _sum_gains(_all_gather(gain_rows, "all_gather_gains"))

    out_g, out_d, out_m, out_v = {}, {}, {}, {}
    for name, slots in grad_slots.items():
        out_g[name], out_d[name], out_m[name], out_v[name] = _sum_update(
            slots, mats[name], moments_m[name], moments_v[name], "update_" + name)

    gain_w = jnp.concatenate([norm_mix_g, norm_ffn_g, g_fin], axis=0)
    gain_m = jnp.concatenate([m_norm_mix_g, m_norm_ffn_g, m_norm_final_g.reshape(1, D_MODEL)], axis=0)
    gain_v = jnp.concatenate([v_norm_mix_g, v_norm_ffn_g, v_norm_final_g.reshape(1, D_MODEL)], axis=0)
    gd, gm, gv = _adamw(gain_w, g_gains[:3], gain_m, gain_v, "adamw_gains")
    for idx, name in enumerate(("norm_mix_g", "norm_ffn_g", "norm_final_g")):
        shape = (D_MODEL,) if name == "norm_final_g" else (1, D_MODEL)
        out_g[name] = g_gains[idx].reshape(shape)
        out_d[name], out_m[name], out_v[name] = gd[idx].reshape(shape), gm[idx].reshape(shape), gv[idx].reshape(shape)

    order = ("norm_mix_g", "w_in", "w_sb_up", "w_dil_up", "w_out", "norm_ffn_g", "w_ffn_in", "w_ffn_out",
             "norm_final_g")
    return (g_gains[3, 0], grad_x.reshape(B_LOC, SEQ, D_MODEL),
            *[out_g[n] for n in order], *[out_d[n] for n in order],
            *[out_m[n] for n in order], *[out_v[n] for n in order])
```

```python
import math

import jax
import jax.numpy as jnp
from jax import lax
from jax.experimental import pallas as pl
from jax.experimental.pallas import tpu as pltpu

F32 = jnp.float32
BF16 = jnp.bfloat16

N_DEV = 8
D_MODEL = 1024
SEQ = 2048
B_LOC = 2
TOK = B_LOC * SEQ
HEAD_DIM = 64
SB_WIDTH = 512
DIL_WIDTH = 768
DIL_OUT = 256
QKV_WIDTH = 3 * SB_WIDTH + 3 * DIL_WIDTH
IN_WIDTH = QKV_WIDTH + 2 * D_MODEL
D_FF = 2816
DIL_PAIRS = ((128, 1), (512, 4), (2048, 16))
DIL_HEADS = 12
RMS_EPS = 1e-6
ALIBI_MAX_BIAS = 8.0
QK_SCALE = 1.0 / math.sqrt(HEAD_DIM)
BLK = 128
LANES = 128
NEG_BIG = -1e30

ADAM_LR = 0.001
ADAM_B1 = 0.9
ADAM_B2 = 0.999
ADAM_EPS = 1e-08
ADAM_WD = 0.01
ADAM_STEP = 10

VMEM_LIMIT = 56 * 1024 * 1024


def _dot(a, b):
    return jnp.dot(a, b, preferred_element_type=F32)


def _dot_nt(a, b):
    return lax.dot_general(a, b, (((1,), (1,)), ((), ())), preferred_element_type=F32)


def _dot_tn(a, b):
    return lax.dot_general(a, b, (((0,), (0,)), ((), ())), preferred_element_type=F32)


def _softplus(z):
    return jnp.maximum(z, 0.0) + jnp.log1p(jnp.exp(-jnp.abs(z)))


def _sigmoid(z):
    return 1.0 / (1.0 + jnp.exp(-z))


def _split_bf16(v):
    hi = v.astype(BF16)
    lo = (v - hi.astype(F32)).astype(BF16)
    return hi, lo


def _chunks(width, step=512):
    out, c = [], 0
    while c < width:
        w = min(step, width - c)
        out.append((c, w))
        c += w
    return out


def _resident(shape):
    nd = len(shape)
    return pl.BlockSpec(shape, lambda *_: (0,) * nd, pipeline_mode=pl.Buffered(1))


def _params(sem):
    return pltpu.CompilerParams(dimension_semantics=sem, vmem_limit_bytes=VMEM_LIMIT)


def _rms_fwd(x, g):
    r = lax.rsqrt(jnp.mean(x * x, axis=-1, keepdims=True) + RMS_EPS)
    n = x * r
    return n, r, n * g


def _rms_bwd(dy, n, r, g):
    dg = jnp.sum(dy * n, axis=0, keepdims=True)
    dn = dy * g
    dx = r * (dn - n * jnp.mean(dn * n, axis=-1, keepdims=True))
    return dx, dg


TM = 256


def _norm_proj(x, g, w_in):
    def body(x_ref, g_ref, w_ref, sb_ref, dl_ref, gate_ref, u_ref):
        _, _, u = _rms_fwd(x_ref[...], g_ref[...])
        u = u.astype(BF16)
        u_ref[...] = u
        for c0, w in _chunks(3 * SB_WIDTH):
            sb_ref[:, c0:c0 + w] = _dot(u, w_ref[:, c0:c0 + w]).astype(BF16)
        for c0, w in _chunks(3 * DIL_WIDTH):
            dl_ref[:, c0:c0 + w] = _dot(u, w_ref[:, 3 * SB_WIDTH + c0:3 * SB_WIDTH + c0 + w])
        for c0, w in _chunks(2 * D_MODEL):
            gate_ref[:, c0:c0 + w] = _dot(u, w_ref[:, QKV_WIDTH + c0:QKV_WIDTH + c0 + w])

    return pl.pallas_call(
        body, name="norm_proj", grid=(TOK // TM,),
        in_specs=[pl.BlockSpec((TM, D_MODEL), lambda i: (i, 0)), _resident((1, D_MODEL)),
                  _resident((D_MODEL, IN_WIDTH))],
        out_specs=[pl.BlockSpec((TM, 3 * SB_WIDTH), lambda i: (i, 0)),
                   pl.BlockSpec((TM, 3 * DIL_WIDTH), lambda i: (i, 0)),
                   pl.BlockSpec((TM, 2 * D_MODEL), lambda i: (i, 0)),
                   pl.BlockSpec((TM, D_MODEL), lambda i: (i, 0))],
        out_shape=[jax.ShapeDtypeStruct((TOK, 3 * SB_WIDTH), BF16),
                   jax.ShapeDtypeStruct((TOK, 3 * DIL_WIDTH), F32),
                   jax.ShapeDtypeStruct((TOK, 2 * D_MODEL), F32),
                   jax.ShapeDtypeStruct((TOK, D_MODEL), BF16)],
        compiler_params=_params(("parallel",)),
    )(x, g, w_in)


def _mix_out(x, o_sb, o_dl, gates, w_sb_up, w_dil_up, w_out):
    def body(x_ref, osb_ref, odl_ref, gate_ref, wsb_ref, wdl_ref, wout_ref, x1_ref, mg_ref):
        y_sb = _dot(osb_ref[...], wsb_ref[...])
        y_dl = _dot(odl_ref[...].astype(BF16), wdl_ref[...])
        merged = (_sigmoid(gate_ref[:, :D_MODEL]) * y_sb
                  + _sigmoid(gate_ref[:, D_MODEL:]) * y_dl).astype(BF16)
        mg_ref[...] = merged
        x1_ref[...] = x_ref[...] + _dot(merged, wout_ref[...])

    return pl.pallas_call(
        body, name="mix_out", grid=(TOK // TM,),
        in_specs=[pl.BlockSpec((TM, D_MODEL), lambda i: (i, 0)),
                  pl.BlockSpec((TM, SB_WIDTH), lambda i: (i, 0)),
                  pl.BlockSpec((TM, DIL_OUT), lambda i: (i, 0)),
                  pl.BlockSpec((TM, 2 * D_MODEL), lambda i: (i, 0)),
                  _resident((SB_WIDTH, D_MODEL)), _resident((DIL_OUT, D_MODEL)),
                  _resident((D_MODEL, D_MODEL))],
        out_specs=[pl.BlockSpec((TM, D_MODEL), lambda i: (i, 0)),
                   pl.BlockSpec((TM, D_MODEL), lambda i: (i, 0))],
        out_shape=[jax.ShapeDtypeStruct((TOK, D_MODEL), F32),
                   jax.ShapeDtypeStruct((TOK, D_MODEL), BF16)],
        compiler_params=_params(("parallel",)),
    )(x, o_sb, o_dl, gates, w_sb_up, w_dil_up, w_out)


FF_SHARD = 2 * D_FF // N_DEV
FF_PAIRS = N_DEV // 2


def _ffn_fwd_bwd(x1, target, g_ffn, g_fin, w_ffn_in, w_ffn_out):
    def body(x1_ref, t_ref, gffn_ref, gfin_ref, win_ref, wout_ref,
             loss_ref, dx1_ref, u2_ref, act_ref, dh_ref, dx2_ref, dgfin_ref, dgffn_ref, h_scr):
        i = pl.program_id(0)

        @pl.when(i == 0)
        def _():
            loss_ref[...] = jnp.zeros_like(loss_ref)
            dgfin_ref[...] = jnp.zeros_like(dgfin_ref)
            dgffn_ref[...] = jnp.zeros_like(dgffn_ref)

        x1 = x1_ref[...]
        g_ffn_v = gffn_ref[...]
        g_fin_v = gfin_ref[...]
        n2, r2, u2 = _rms_fwd(x1, g_ffn_v)
        u2 = u2.astype(BF16)
        u2_ref[...] = u2
        x2 = x1
        for r in range(FF_PAIRS):
            gate = _dot(u2, win_ref[r])
            up = _dot(u2, win_ref[r + FF_PAIRS])
            h_scr[r] = gate
            h_scr[r + FF_PAIRS] = up
            act = (gate * _sigmoid(gate) * up).astype(BF16)
            act_ref[r] = act
            x2 = x2 + _dot(act, wout_ref[r * FF_SHARD:(r + 1) * FF_SHARD, :])
        n3, r3, y = _rms_fwd(x2, g_fin_v)
        err = y - t_ref[...]
        sq = jnp.sum(jnp.sum(err * err, axis=1, keepdims=True), axis=0, keepdims=True)
        loss_ref[...] += sq * (0.5 / D_MODEL)
        dx2, dgfin = _rms_bwd(err * (1.0 / D_MODEL), n3, r3, g_fin_v)
        dgfin_ref[...] += dgfin
        dx2_b = dx2.astype(BF16)
        dx2_ref[...] = dx2_b
        du2 = jnp.zeros((TM, D_MODEL), F32)
        for r in range(FF_PAIRS):
            gate = h_scr[r]
            up = h_scr[r + FF_PAIRS]
            dact = _dot_nt(dx2_b, wout_ref[r * FF_SHARD:(r + 1) * FF_SHARD, :])
            sg = _sigmoid(gate)
            dgate = (dact * up * (sg * (1.0 + gate * (1.0 - sg)))).astype(BF16)
            dup = (dact * (gate * sg)).astype(BF16)
            dh_ref[r] = dgate
            dh_ref[r + FF_PAIRS] = dup
            du2 = du2 + _dot_nt(dgate, win_ref[r])
            du2 = du2 + _dot_nt(dup, win_ref[r + FF_PAIRS])
        dx1_n, dgffn = _rms_bwd(du2, n2, r2, g_ffn_v)
        dgffn_ref[...] += dgffn
        dx1_ref[...] = dx2 + dx1_n

    tile = lambda w: pl.BlockSpec((TM, w), lambda i: (i, 0))
    shards = lambda n: pl.BlockSpec((n, TM, FF_SHARD), lambda i: (0, i, 0))
    acc = lambda w: pl.BlockSpec((1, w), lambda i: (0, 0))
    return pl.pallas_call(
        body, name="ffn_fwd_bwd", grid=(TOK // TM,),
        in_specs=[tile(D_MODEL), tile(D_MODEL), _resident((1, D_MODEL)), _resident((1, D_MODEL)),
                  _resident((N_DEV, D_MODEL, FF_SHARD)), _resident((D_FF, D_MODEL))],
        out_specs=[acc(LANES), tile(D_MODEL), tile(D_MODEL), shards(FF_PAIRS), shards(N_DEV), tile(D_MODEL),
                   acc(D_MODEL), acc(D_MODEL)],
        out_shape=[jax.ShapeDtypeStruct((1, LANES), F32),
                   jax.ShapeDtypeStruct((TOK, D_MODEL), F32),
                   jax.ShapeDtypeStruct((TOK, D_MODEL), BF16),
                   jax.ShapeDtypeStruct((FF_PAIRS, TOK, FF_SHARD), BF16),
                   jax.ShapeDtypeStruct((N_DEV, TOK, FF_SHARD), BF16),
                   jax.ShapeDtypeStruct((TOK, D_MODEL), BF16),
                   jax.ShapeDtypeStruct((1, D_MODEL), F32),
                   jax.ShapeDtypeStruct((1, D_MODEL), F32)],
        scratch_shapes=[pltpu.VMEM((N_DEV, TM, FF_SHARD), F32)],
        compiler_params=_params(("arbitrary",)),
    )(x1, target, g_ffn, g_fin, w_ffn_in, w_ffn_out)


def _mix_bwd(dx1, o_sb, o_dl, gates, w_sb_up, w_dil_up, w_out):
    def body(dx1_ref, osb_ref, odl_ref, gate_ref, wsb_ref, wdl_ref, wout_ref,
             dgate_ref, dysb_ref, dydl_ref, dosb_ref, dodl_ref, dsum_ref):
        dmerged = _dot_nt(dx1_ref[...].astype(BF16), wout_ref[...])
        o_dl = odl_ref[...]
        y_sb = _dot(osb_ref[...], wsb_ref[...])
        y_dl = _dot(o_dl.astype(BF16), wdl_ref[...])
        s_sb = _sigmoid(gate_ref[:, :D_MODEL])
        s_dl = _sigmoid(gate_ref[:, D_MODEL:])
        dgate_ref[:, :D_MODEL] = (dmerged * y_sb * (s_sb * (1.0 - s_sb))).astype(BF16)
        dgate_ref[:, D_MODEL:] = (dmerged * y_dl * (s_dl * (1.0 - s_dl))).astype(BF16)
        dy_sb = (dmerged * s_sb).astype(BF16)
        dy_dl = (dmerged * s_dl).astype(BF16)
        dysb_ref[...] = dy_sb
        dydl_ref[...] = dy_dl
        dosb_ref[...] = _dot_nt(dy_sb, wsb_ref[...]).astype(BF16)
        do_dl = _dot_nt(dy_dl, wdl_ref[...])
        dodl_ref[...] = do_dl
        row = lax.broadcasted_iota(jnp.int32, (DIL_OUT, DIL_OUT), 0) // HEAD_DIM
        col = lax.broadcasted_iota(jnp.int32, (DIL_OUT, DIL_OUT), 1) // HEAD_DIM
        same_head = (row == col).astype(BF16)
        hi, lo = _split_bf16(do_dl * o_dl)
        dsum_ref[...] = _dot(hi, same_head) + _dot(lo, same_head)

    tile = lambda w: pl.BlockSpec((TM, w), lambda i: (i, 0))
    return pl.pallas_call(
        body, name="mix_bwd", grid=(TOK // TM,),
        in_specs=[tile(D_MODEL), tile(SB_WIDTH), tile(DIL_OUT), tile(2 * D_MODEL),
                  _resident((SB_WIDTH, D_MODEL)), _resident((DIL_OUT, D_MODEL)),
                  _resident((D_MODEL, D_MODEL))],
        out_specs=[tile(2 * D_MODEL), tile(D_MODEL), tile(D_MODEL), tile(SB_WIDTH), tile(DIL_OUT),
                   tile(DIL_OUT)],
        out_shape=[jax.ShapeDtypeStruct((TOK, 2 * D_MODEL), BF16),
                   jax.ShapeDtypeStruct((TOK, D_MODEL), BF16),
                   jax.ShapeDtypeStruct((TOK, D_MODEL), BF16),
                   jax.ShapeDtypeStruct((TOK, SB_WIDTH), BF16),
                   jax.ShapeDtypeStruct((TOK, DIL_OUT), F32),
                   jax.ShapeDtypeStruct((TOK, DIL_OUT), F32)],
        compiler_params=_params(("parallel",)),
    )(dx1, o_sb, o_dl, gates, w_sb_up, w_dil_up, w_out)


def _proj_bwd(dproj, dx1, x, g, w_in, send=None):
    widths = [p.shape[1] for p in dproj]

    def body(*refs):
        dx1_ref, x_ref, g_ref, w_ref, dx_ref, dg_ref = refs[len(widths):]

        @pl.when(pl.program_id(0) == 0)
        def _():
            dg_ref[...] = jnp.zeros_like(dg_ref)

        du = jnp.zeros((TM, D_MODEL), F32)
        c0 = 0
        for dp_ref, w in zip(refs, widths):
            du = du + _dot_nt(dp_ref[...].astype(BF16), w_ref[:, c0:c0 + w])
            c0 += w
        g_v = g_ref[...]
        n, r, _ = _rms_fwd(x_ref[...], g_v)
        dx, dg = _rms_bwd(du, n, r, g_v)
        dg_ref[...] += dg
        dx_ref[...] = dx1_ref[...] + dx

    tile = lambda w: pl.BlockSpec((TM, w), lambda i: (i, 0))
    return _call(
        body, send, name="proj_bwd", grid=(TOK // TM,),
        in_specs=[tile(w) for w in widths] + [tile(D_MODEL), tile(D_MODEL), _resident((1, D_MODEL)),
                                              _resident((D_MODEL, IN_WIDTH))],
        out_specs=[tile(D_MODEL), pl.BlockSpec((1, D_MODEL), lambda i: (0, 0))],
        out_shape=[jax.ShapeDtypeStruct((TOK, D_MODEL), F32),
                   jax.ShapeDtypeStruct((1, D_MODEL), F32)],
        scratch_shapes=[], semantics=("arbitrary",), operands=(*dproj, dx1, x, g, w_in))


def _atb_pieces(a, pieces, name, tm, tk=512):
    m = a.shape[1]
    widths = [p.shape[1] for p in pieces]
    n = sum(widths)
    nk = TOK // tk

    def body(a_ref, *refs):
        o_ref, acc_ref = refs[len(widths):]
        k = pl.program_id(1)

        @pl.when(k == 0)
        def _():
            acc_ref[...] = jnp.zeros_like(acc_ref)

        a_v = a_ref[...]
        c0 = 0
        for p_ref, w in zip(refs, widths):
            acc_ref[:, c0:c0 + w] += _dot_tn(a_v, p_ref[...].astype(BF16))
            c0 += w

        @pl.when(k == nk - 1)
        def _():
            o_ref[...] = acc_ref[...].astype(BF16)

    return pl.pallas_call(
        body, name=name, grid=(m // tm, nk),
        in_specs=[pl.BlockSpec((tk, tm), lambda i, k: (k, i))]
                 + [pl.BlockSpec((tk, w), lambda i, k: (k, 0)) for w in widths],
        out_specs=pl.BlockSpec((tm, n), lambda i, k: (i, 0)),
        out_shape=jax.ShapeDtypeStruct((m, n), BF16),
        scratch_shapes=[pltpu.VMEM((tm, n), F32)],
        compiler_params=_params(("parallel", "arbitrary")),
    )(a, *pieces)


def _atb_shards(a, b, name, tk=512):
    a_sharded = a.ndim == 3
    n, _, w = a.shape if a_sharded else b.shape
    other = (b if a_sharded else a).shape[1]

    def body(a_ref, b_ref, o_ref):
        acc = jnp.zeros(o_ref.shape, F32)
        for k0 in range(0, TOK, tk):
            acc = acc + _dot_tn(a_ref[k0:k0 + tk, :], b_ref[k0:k0 + tk, :])
        o_ref[...] = acc.astype(BF16)

    shard = pl.BlockSpec((None, TOK, w), lambda r: (r, 0, 0))
    whole = _resident((TOK, other))
    if a_sharded:
        out_spec, out_shape = pl.BlockSpec((w, other), lambda r: (r, 0)), (n * w, other)
    else:
        out_spec, out_shape = pl.BlockSpec((None, other, w), lambda r: (r, 0, 0)), (n, other, w)
    return pl.pallas_call(
        body, name=name, grid=(n,),
        in_specs=[shard, whole] if a_sharded else [whole, shard],
        out_specs=out_spec, out_shape=jax.ShapeDtypeStruct(out_shape, BF16),
        compiler_params=_params(("parallel",)),
    )(a, b)


def _atb(a, b, name, tm, tn, col_blocks=0, tk=512):
    m, n = a.shape[1], b.shape[1]
    nk = TOK // tk

    def body(a_ref, b_ref, o_ref, acc_ref):
        k = pl.program_id(2)

        @pl.when(k == 0)
        def _():
            acc_ref[...] = jnp.zeros_like(acc_ref)

        acc_ref[...] += _dot_tn(a_ref[...].astype(BF16), b_ref[...].astype(BF16))

        @pl.when(k == nk - 1)
        def _():
            if col_blocks:
                width = n // col_blocks
                for blk in range(col_blocks):
                    o_ref[blk] = acc_ref[:, blk * width:(blk + 1) * width].astype(BF16)
            else:
                o_ref[...] = acc_ref[...].astype(BF16)

    if col_blocks:
        out_spec = pl.BlockSpec((col_blocks, tm, n // col_blocks), lambda i, j, k: (0, i, 0))
        out_shape = jax.ShapeDtypeStruct((col_blocks, m, n // col_blocks), BF16)
    else:
        out_spec = pl.BlockSpec((tm, tn), lambda i, j, k: (i, j))
        out_shape = jax.ShapeDtypeStruct((m, n), BF16)
    return pl.pallas_call(
        body, name=name, grid=(m // tm, n // tn, nk),
        in_specs=[pl.BlockSpec((tk, tm), lambda i, j, k: (k, i)),
                  pl.BlockSpec((tk, tn), lambda i, j, k: (k, j))],
        out_specs=out_spec, out_shape=out_shape,
        scratch_shapes=[pltpu.VMEM((tm, tn), F32)],
        compiler_params=_params(("parallel", "parallel", "arbitrary")),
    )(a, b)


SB_PAIRS = SB_WIDTH // LANES


def _two_heads(v, lane0):
    zero = jnp.zeros_like(v)
    return jnp.where(lane0, v, zero), jnp.where(lane0, zero, v)


SB_QBLK = 256
N_SB_STEPS = SEQ // SB_QBLK


SB_KCHUNK = 2 * BLK
SB_ROWS = 2 * SB_QBLK
SB_DEAD = -104.0


def _log_keep(z):
    neg_z = -z
    return jnp.minimum(neg_z, 0.0) - jnp.log(1.0 + jnp.exp(jnp.minimum(z, neg_z)))


def _stack_heads(v, lane0):
    return jnp.concatenate(_two_heads(v, lane0), axis=0)


def _block_sums(v, tri):
    halves = (v[:, :BLK], v[:, BLK:])
    hi, lo = _split_bf16(jnp.concatenate(halves, axis=0))
    prod = _dot(jnp.concatenate([hi, lo], axis=0), tri)
    tri_sum = prod[:2 * SB_ROWS] + prod[2 * SB_ROWS:]
    sums = tuple(jnp.sum(h, axis=1, keepdims=True) for h in halves)
    return (tri_sum[:SB_ROWS], tri_sum[SB_ROWS:]), sums


def _sb_diag_mask():
    row = lax.broadcasted_iota(jnp.int32, (SB_ROWS, SB_KCHUNK), 0)
    col = lax.broadcasted_iota(jnp.int32, (SB_ROWS, SB_KCHUNK), 1)
    return col < jnp.where(row >= SB_QBLK, row - SB_QBLK, row)


def _sb_fwd(qkv, send=None):
    def body(q_ref, k_ref, v_ref, o_ref):
        i = pl.program_id(2)
        krow = lax.broadcasted_iota(jnp.int32, (BLK, BLK), 0)
        kcol = lax.broadcasted_iota(jnp.int32, (BLK, BLK), 1)
        later = (krow > kcol).astype(BF16)
        lane0 = lax.broadcasted_iota(jnp.int32, (SB_QBLK, LANES), 1) < HEAD_DIM
        q2 = _stack_heads(q_ref[0] * QK_SCALE, lane0)

        def chunk(c, carry, causal):
            acc, run = carry
            off = pl.multiple_of(c * SB_KCHUNK, SB_KCHUNK)
            z = _dot_nt(q2, k_ref[0, pl.ds(off, SB_KCHUNK), :])
            log_keep = _log_keep(z)
            if causal is not None:
                log_keep = jnp.where(causal, log_keep, 0.0)
            suffix, sums = _block_sums(log_keep, later)
            log_after = jnp.concatenate([suffix[0] + (run + sums[1]), suffix[1] + run], axis=1)
            a = jnp.exp(log_keep + z + log_after)
            if causal is not None:
                a = jnp.where(causal, a, 0.0)
            acc = acc + _dot(a.astype(BF16), v_ref[0, pl.ds(off, SB_KCHUNK), :])
            return acc, run + (sums[0] + sums[1])

        acc, run = chunk(i, (jnp.zeros((SB_ROWS, LANES), F32), jnp.zeros((SB_ROWS, 1), F32)), _sb_diag_mask())

        def live(state):
            t, _, run = state
            return jnp.logical_and(t < i, jnp.max(run) > SB_DEAD)

        def trip(state):
            t, acc, run = state
            acc, run = chunk(i - 1 - t, (acc, run), None)
            return t + 1, acc, run

        _, acc, _ = lax.while_loop(live, trip, (jnp.int32(0), acc, run))
        o_ref[0] = jnp.where(lane0, acc[:SB_QBLK], acc[SB_QBLK:]).astype(BF16)

    blk = pl.BlockSpec((1, SB_QBLK, LANES), lambda b, h, i: (b, i, h))
    return _call(
        body, send, name="sb_fwd", grid=(B_LOC, SB_PAIRS, N_SB_STEPS),
        in_specs=[blk,
                  pl.BlockSpec((1, SEQ, LANES), lambda b, h, i: (b, 0, SB_PAIRS + h)),
                  pl.BlockSpec((1, SEQ, LANES), lambda b, h, i: (b, 0, 2 * SB_PAIRS + h))],
        out_specs=[blk], out_shape=[jax.ShapeDtypeStruct((B_LOC, SEQ, SB_WIDTH), BF16)],
        scratch_shapes=[], semantics=("parallel", "parallel", "arbitrary"), operands=(qkv, qkv, qkv))


def _sb_bwd(qkv, d_o, send=None):
    def body(q_ref, k_ref, v_ref, do_ref, dq_ref, dk_ref, dv_ref, dk_acc, dv_acc, z_scr, keep_scr):
        i = pl.program_id(2)
        krow = lax.broadcasted_iota(jnp.int32, (BLK, BLK), 0)
        kcol = lax.broadcasted_iota(jnp.int32, (BLK, BLK), 1)
        upto = (krow <= kcol).astype(BF16)
        earlier = (krow < kcol).astype(BF16)
        lane0 = lax.broadcasted_iota(jnp.int32, (SB_QBLK, LANES), 1) < HEAD_DIM
        q2 = _stack_heads(q_ref[0] * QK_SCALE, lane0)
        do2 = _stack_heads(do_ref[0], lane0)

        def keep_sum(c, causal):
            off = pl.multiple_of(c * SB_KCHUNK, SB_KCHUNK)
            z = _dot_nt(q2, k_ref[0, pl.ds(off, SB_KCHUNK), :])
            log_keep = _log_keep(z)
            if causal is not None:
                log_keep = jnp.where(causal, log_keep, 0.0)
            z_scr[c] = z
            keep_scr[c] = log_keep
            return jnp.sum(log_keep, axis=1, keepdims=True)

        def live(state):
            t, run = state
            return jnp.logical_and(t < i, jnp.max(run) > SB_DEAD)

        walked, tot2 = lax.while_loop(live, lambda s: (s[0] + 1, s[1] + keep_sum(i - 1 - s[0], None)),
                                      (jnp.int32(0), keep_sum(i, _sb_diag_mask())))
        first = i - walked

        @pl.when(i == 0)
        def _():
            dk_acc[...] = jnp.zeros_like(dk_acc)
            dv_acc[...] = jnp.zeros_like(dv_acc)

        def chunk(c, carry, causal):
            dq, pre_keep, pre_e = carry
            off = pl.multiple_of(c * SB_KCHUNK, SB_KCHUNK)
            k_c = k_ref[0, pl.ds(off, SB_KCHUNK), :]
            v_c = v_ref[0, pl.ds(off, SB_KCHUNK), :]
            d_a = _dot_nt(do2, v_c)
            log_keep = keep_scr[c]
            log_beta = log_keep + z_scr[c]
            prefix, sums = _block_sums(log_keep, upto)
            inclusive = jnp.concatenate([prefix[0], prefix[1] + sums[0]], axis=1)
            a = jnp.exp(log_beta + ((tot2 - pre_keep) - inclusive))
            if causal is not None:
                a = jnp.where(causal, a, 0.0)
            e = d_a * a
            e_prefix, e_sums = _block_sums(e, earlier)
            before = jnp.concatenate([e_prefix[0] + pre_e, e_prefix[1] + (pre_e + e_sums[0])], axis=1)
            dz = e - (e + before) * jnp.exp(log_beta)
            if causal is not None:
                dz = jnp.where(causal, dz, 0.0)
            dz = dz.astype(BF16)
            dq = dq + _dot(dz, k_c)
            dk_acc[pl.ds(off, SB_KCHUNK), :] += _dot_tn(dz, q2)
            dv_acc[pl.ds(off, SB_KCHUNK), :] += _dot_tn(a.astype(BF16), do2)
            return dq, pre_keep + (sums[0] + sums[1]), pre_e + (e_sums[0] + e_sums[1])

        zero_col = jnp.zeros((SB_ROWS, 1), F32)
        carry = lax.fori_loop(first, i, lambda t, c: chunk(t, c, None),
                              (jnp.zeros((SB_ROWS, LANES), F32), zero_col, zero_col))
        dq, _, _ = chunk(i, carry, _sb_diag_mask())
        dq_ref[0] = (jnp.where(lane0, dq[:SB_QBLK], dq[SB_QBLK:]) * QK_SCALE).astype(BF16)

        @pl.when(i == N_SB_STEPS - 1)
        def _():
            dk_ref[0] = dk_acc[...].astype(BF16)
            dv_ref[0] = dv_acc[...].astype(BF16)

    blk = pl.BlockSpec((1, SB_QBLK, LANES), lambda b, h, i: (b, i, h))
    whole = lambda c: pl.BlockSpec((1, SEQ, LANES), lambda b, h, i: (b, 0, c * SB_PAIRS + h))
    out = jax.ShapeDtypeStruct((B_LOC, SEQ, SB_WIDTH), BF16)
    return _call(
        body, send, name="sb_bwd", grid=(B_LOC, SB_PAIRS, N_SB_STEPS),
        in_specs=[blk, whole(1), whole(2), blk],
        out_specs=[blk, whole(0), whole(0)],
        out_shape=[out, out, out],
        scratch_shapes=[pltpu.VMEM((SEQ, LANES), F32), pltpu.VMEM((SEQ, LANES), F32),
                        pltpu.VMEM((N_SB_STEPS, SB_ROWS, SB_KCHUNK), F32),
                        pltpu.VMEM((N_SB_STEPS, SB_ROWS, SB_KCHUNK), F32)],
        semantics=("parallel", "parallel", "arbitrary"), operands=(qkv, qkv, qkv, d_o))


DIL_GROUPS = len(DIL_PAIRS)
DIL_QBLOCKS = SEQ // BLK


def _residue_rows(j, dilation):
    length = SEQ // dilation
    return pl.ds(j, length, stride=dilation) if dilation > 1 else pl.ds(0, length)


def _gather_residues(src_ref, dst_ref, dst_off, dilation, scale=None):
    length = SEQ // dilation
    for j in range(dilation):
        v = src_ref[_residue_rows(j, dilation), :]
        if scale is not None:
            v = v * scale
        dst_ref[dst_off + j * length:dst_off + (j + 1) * length, :] = v.astype(dst_ref.dtype)


def _scatter_residues(src_ref, src_off, dst_ref, dilation):
    length = SEQ // dilation
    for j in range(dilation):
        dst_ref[_residue_rows(j, dilation), :] = (
            src_ref[src_off + j * length:src_off + (j + 1) * length, :].astype(dst_ref.dtype))


def _dil_geometry(group, pair):
    dilation = DIL_PAIRS[group][1]
    row = lax.broadcasted_iota(jnp.int32, (2 * BLK, 2 * BLK), 0)
    col = lax.broadcasted_iota(jnp.int32, (2 * BLK, 2 * BLK), 1)
    second = row >= BLK
    steps = BLK + jnp.where(second, row - BLK, row) - col
    coef = -ALIBI_MAX_BIAS / DIL_HEADS * math.log(2.0)
    first_head = float(4 * group + 1) + 2.0 * pair.astype(F32)
    slope = jnp.exp(coef * (first_head + jnp.where(second, 1.0, 0.0)))
    bias = slope * (steps * dilation).astype(F32)
    valid = jnp.logical_and(steps >= 0, steps <= BLK)
    return bias, valid, col >= BLK


def _dil_tile_scores(q2, kk, geometry, has_prev):
    bias, valid, own = geometry
    ok = jnp.logical_and(valid, jnp.logical_or(own, has_prev))
    return jnp.where(ok, _dot_nt(q2, kk) - bias, NEG_BIG)


def _head_col(v, lane_mask):
    return jnp.max(jnp.where(lane_mask, v, NEG_BIG), axis=1, keepdims=True)


def _dil_fwd(qkv, send=None):
    def body(*refs):
        ins, (o_ref, lse_ref), (qs, ks, vs, o_res, lse_res) = refs[:9], refs[9:11], refs[11:16]
        o_grp, lse_grp = refs[16:19], refs[19:22]
        pair = pl.program_id(1)
        lane0 = lax.broadcasted_iota(jnp.int32, (BLK, LANES), 1) < HEAD_DIM
        ks[0:BLK, :] = jnp.zeros((BLK, LANES), BF16)
        vs[0:BLK, :] = jnp.zeros((BLK, LANES), BF16)
        for grp, (_, dilation) in enumerate(DIL_PAIRS):
            q_ref, k_ref, v_ref = ins[3 * grp:3 * grp + 3]
            per_residue = DIL_QBLOCKS // dilation
            _gather_residues(q_ref, qs, 0, dilation, QK_SCALE)
            _gather_residues(k_ref, ks, BLK, dilation)
            _gather_residues(v_ref, vs, BLK, dilation)
            geometry = _dil_geometry(grp, pair)

            def step(blk, _):
                off = pl.multiple_of(blk * BLK, BLK)
                q2 = _stack_heads(qs[pl.ds(off, BLK), :], lane0)
                s = _dil_tile_scores(q2, ks[pl.ds(off, 2 * BLK), :], geometry, blk % per_residue != 0)
                m = jnp.max(s, axis=1, keepdims=True)
                p = jnp.exp(s - m)
                den = jnp.sum(p, axis=1, keepdims=True)
                out = _dot(p.astype(BF16), vs[pl.ds(off, 2 * BLK), :]) / den
                lse = m + jnp.log(den)
                o_res[pl.ds(off, BLK), :] = jnp.where(lane0, out[:BLK], out[BLK:])
                lse_res[pl.ds(off, BLK), :] = jnp.where(lane0, lse[:BLK], lse[BLK:])
                return 0

            lax.fori_loop(0, DIL_QBLOCKS, step, 0, unroll=8)
            _scatter_residues(o_res, 0, o_grp[grp], dilation)
            _scatter_residues(lse_res, 0, lse_grp[grp], dilation)

        for r0 in range(0, SEQ, 2 * BLK):
            rows = slice(r0, r0 + 2 * BLK)
            ls = [lse_grp[g][rows, :] for g in range(DIL_GROUPS)]
            m = jnp.maximum(jnp.maximum(ls[0], ls[1]), ls[2])
            w = [jnp.exp(l - m) for l in ls]
            den = w[0] + w[1] + w[2]
            o_ref[rows, :] = (w[0] * o_grp[0][rows, :] + w[1] * o_grp[1][rows, :] + w[2] * o_grp[2][rows, :]) / den
            lse_ref[rows, :] = m + jnp.log(den)

    def col(part, grp):
        return pl.BlockSpec((None, SEQ, LANES), lambda b, p: (b, 0, 6 * part + 2 * grp + p))

    out_spec = pl.BlockSpec((None, SEQ, LANES), lambda b, p: (b, 0, p))
    out = jax.ShapeDtypeStruct((B_LOC, SEQ, DIL_OUT), F32)
    return _call(
        body, send, name="dil_fwd", grid=(B_LOC, DIL_OUT // LANES),
        in_specs=[col(part, grp) for grp in range(DIL_GROUPS) for part in range(3)],
        out_specs=[out_spec, out_spec], out_shape=[out, out],
        scratch_shapes=[pltpu.VMEM((SEQ, LANES), BF16), pltpu.VMEM((SEQ + BLK, LANES), BF16),
                        pltpu.VMEM((SEQ + BLK, LANES), BF16), pltpu.VMEM((SEQ, LANES), F32),
                        pltpu.VMEM((SEQ, LANES), F32)] + [pltpu.VMEM((SEQ, LANES), F32)] * (2 * DIL_GROUPS),
        semantics=("parallel", "parallel"), operands=[qkv] * 9)


def _dil_bwd(qkv, d_o, lse, dsum, send=None):
    def body(*refs):
        ins, (do_ref, lse_ref, dsum_ref), outs = refs[:9], refs[9:12], refs[12:21]
        qs, ks, vs, dos, lse_res, dsum_res, dq_res, dk_acc, dv_acc = refs[21:]
        pair = pl.program_id(1)
        lane0 = lax.broadcasted_iota(jnp.int32, (BLK, LANES), 1) < HEAD_DIM
        lane1 = jnp.logical_not(lane0)
        ks[0:BLK, :] = jnp.zeros((BLK, LANES), BF16)
        vs[0:BLK, :] = jnp.zeros((BLK, LANES), BF16)
        for grp, (_, dilation) in enumerate(DIL_PAIRS):
            q_ref, k_ref, v_ref = ins[3 * grp:3 * grp + 3]
            dq_ref, dk_ref, dv_ref = outs[3 * grp:3 * grp + 3]
            per_residue = DIL_QBLOCKS // dilation
            _gather_residues(q_ref, qs, 0, dilation, QK_SCALE)
            _gather_residues(k_ref, ks, BLK, dilation)
            _gather_residues(v_ref, vs, BLK, dilation)
            _gather_residues(do_ref, dos, 0, dilation)
            _gather_residues(lse_ref, lse_res, 0, dilation)
            _gather_residues(dsum_ref, dsum_res, 0, dilation)
            dk_acc[...] = jnp.zeros_like(dk_acc)
            dv_acc[...] = jnp.zeros_like(dv_acc)
            geometry = _dil_geometry(grp, pair)

            def step(blk, _):
                off = pl.multiple_of(blk * BLK, BLK)
                q2 = _stack_heads(qs[pl.ds(off, BLK), :], lane0)
                do2 = _stack_heads(dos[pl.ds(off, BLK), :], lane0)
                kk = ks[pl.ds(off, 2 * BLK), :]
                vv = vs[pl.ds(off, 2 * BLK), :]
                lse_blk = lse_res[pl.ds(off, BLK), :]
                dsum_blk = dsum_res[pl.ds(off, BLK), :]
                lse2 = jnp.concatenate([_head_col(lse_blk, lane0), _head_col(lse_blk, lane1)], axis=0)
                dsum2 = jnp.concatenate([_head_col(dsum_blk, lane0), _head_col(dsum_blk, lane1)], axis=0)
                s = _dil_tile_scores(q2, kk, geometry, blk % per_residue != 0)
                p = jnp.exp(s - lse2)
                ds = (p * (_dot_nt(do2, vv) - dsum2)).astype(BF16)
                dq2 = _dot(ds, kk)
                dq_res[pl.ds(off, BLK), :] = jnp.where(lane0, dq2[:BLK], dq2[BLK:]) * QK_SCALE
                dk_acc[pl.ds(off, 2 * BLK), :] += _dot_tn(ds, q2)
                dv_acc[pl.ds(off, 2 * BLK), :] += _dot_tn(p.astype(BF16), do2)
                return 0

            lax.fori_loop(0, DIL_QBLOCKS, step, 0, unroll=8)
            _scatter_residues(dq_res, 0, dq_ref, dilation)
            _scatter_residues(dk_acc, BLK, dk_ref, dilation)
            _scatter_residues(dv_acc, BLK, dv_ref, dilation)

    def col(part, grp):
        return pl.BlockSpec((None, SEQ, LANES), lambda b, p: (b, 0, 6 * part + 2 * grp + p))

    slot = pl.BlockSpec((None, SEQ, LANES), lambda b, p: (b, 0, p))
    out = jax.ShapeDtypeStruct((B_LOC, SEQ, DIL_OUT), F32)
    return _call(
        body, send, name="dil_bwd", grid=(B_LOC, DIL_OUT // LANES),
        in_specs=[col(part, grp) for grp in range(DIL_GROUPS) for part in range(3)] + [slot] * 3,
        out_specs=[slot] * 9, out_shape=[out] * 9,
        scratch_shapes=[pltpu.VMEM((SEQ, LANES), BF16), pltpu.VMEM((SEQ + BLK, LANES), BF16),
                        pltpu.VMEM((SEQ + BLK, LANES), BF16), pltpu.VMEM((SEQ, LANES), BF16),
                        pltpu.VMEM((SEQ, LANES), F32), pltpu.VMEM((SEQ, LANES), F32),
                        pltpu.VMEM((SEQ, LANES), F32), pltpu.VMEM((SEQ + BLK, LANES), F32),
                        pltpu.VMEM((SEQ + BLK, LANES), F32)],
        semantics=("parallel", "parallel"), operands=[qkv] * 9 + [d_o, lse, dsum])


def _peers():
    x, y, c = lax.axis_index("x"), lax.axis_index("y"), lax.axis_index("c")
    me = 4 * x + 2 * y + c
    peers = []
    for mask in range(1, N_DEV):
        px = 1 - x if mask & 4 else x
        py = 1 - y if mask & 2 else y
        pc = 1 - c if mask & 1 else c
        peers.append(((px, py, pc), 4 * px + 2 * py + pc))
    return me, peers


def _all_gather(shard, name):
    def body(src_ref, out_ref, send_sems, recv_sems, local_sem):
        x, y, c = lax.axis_index("x"), lax.axis_index("y"), lax.axis_index("c")
        sibling = (x, y, 1 - c)
        chips = [(1 - x, y), (x, 1 - y), (1 - x, 1 - y)]

        def slot(px, py, pc):
            return out_ref.at[4 * px + 2 * py + pc]

        def copy(k, block, to, src=None):
            return pltpu.make_async_remote_copy(
                src_ref=slot(*block) if src is None else src, dst_ref=slot(*block),
                send_sem=send_sems.at[k], recv_sem=recv_sems.at[k], device_id=to,
                device_id_type=pl.DeviceIdType.MESH)

        mine = pltpu.make_async_copy(src_ref, slot(x, y, c), local_sem)
        mine.start()
        first = [copy(0, (x, y, c), sibling, src=src_ref)]
        first += [copy(1 + j, (x, y, c), (*chip, c), src=src_ref) for j, chip in enumerate(chips)]
        for cp in first:
            cp.start()
        passed = [copy(4 + j, (*chip, c), sibling) for j, chip in enumerate(chips)]
        for j, chip in enumerate(chips):
            copy(1 + j, (*chip, c), (x, y, c)).wait_recv()
            passed[j].start()
        copy(0, sibling, (x, y, c)).wait_recv()
        for j, chip in enumerate(chips):
            copy(4 + j, (*chip, 1 - c), (x, y, c)).wait_recv()
        for cp in first + passed:
            cp.wait_send()
        mine.wait()

    return pl.pallas_call(
        body, name=name,
        in_specs=[pl.BlockSpec(memory_space=pl.ANY)],
        out_specs=pl.BlockSpec(memory_space=pl.ANY),
        out_shape=jax.ShapeDtypeStruct((N_DEV,) + shard.shape, shard.dtype),
        scratch_shapes=[pltpu.SemaphoreType.DMA((N_DEV - 1,)), pltpu.SemaphoreType.DMA((N_DEV - 1,)),
                        pltpu.SemaphoreType.DMA],
    )(shard)


def _call(body, send, *, name, grid, in_specs, out_specs, out_shape, scratch_shapes, semantics, operands):
    if send is None:
        return pl.pallas_call(
            body, name=name, grid=grid, in_specs=in_specs, out_specs=out_specs, out_shape=out_shape,
            scratch_shapes=scratch_shapes, compiler_params=_params(semantics))(*operands), []
    srcs, kinds = [s for s, _ in send], [k for _, k in send]
    n, n_in, n_out, n_scr = len(srcs), len(in_specs), len(out_specs), len(scratch_shapes)
    steps = math.prod(grid)
    relay_step = (3 * steps) // 4

    def plan(refs):
        src_refs, land_refs = refs[n_in:n_in + n], refs[n_in + n + n_out:n_in + 2 * n + n_out]
        send_sems, recv_sems, local_sems = refs[-3:]
        x, y, c = lax.axis_index("x"), lax.axis_index("y"), lax.axis_index("c")
        me, peers = _peers()
        first, relayed_in, relayed_out, arrivals, sends, own = [], [], [], [], [], []
        for a, kind in enumerate(kinds):
            def copy(k, src, dst_slot, to):
                return pltpu.make_async_remote_copy(
                    src_ref=src, dst_ref=land_refs[a].at[dst_slot], send_sem=send_sems.at[a * (N_DEV - 1) + k],
                    recv_sem=recv_sems.at[a * (N_DEV - 1) + k], device_id=to, device_id_type=pl.DeviceIdType.MESH)

            if kind == "gather_by_chip":
                idx = lambda px, py, pc: 4 * px + 2 * py + pc
                chips = [(1 - x, y), (x, 1 - y), (1 - x, 1 - y)]
                mine = [copy(0, src_refs[a], me, (x, y, 1 - c))]
                arrivals.append(copy(0, src_refs[a], idx(x, y, 1 - c), (x, y, 1 - c)))
                for j, (px, py) in enumerate(chips):
                    mine.append(copy(1 + j, src_refs[a], me, (px, py, c)))
                    relayed_in.append(copy(1 + j, src_refs[a], idx(px, py, c), (px, py, c)))
                    relayed_out.append(copy(4 + j, land_refs[a].at[idx(px, py, c)], idx(px, py, c), (x, y, 1 - c)))
                    arrivals.append(copy(4 + j, src_refs[a], idx(px, py, 1 - c), (x, y, 1 - c)))
                first += mine
                sends += mine + relayed_out[-3:]
                own.append(pltpu.make_async_copy(src_refs[a], land_refs[a].at[me], local_sems.at[a]))
            elif kind == "scatter_by_chip":
                for k, (px, py) in enumerate([(1 - x, y), (x, 1 - y), (1 - x, 1 - y)]):
                    first.append(copy(k, src_refs[a].at[2 * px + py], 2 * x + y, (px, py, c)))
                    arrivals.append(copy(k, src_refs[a].at[2 * px + py], 2 * px + py, (px, py, c)))
                sends += first[-3:]
                own.append(pltpu.make_async_copy(src_refs[a].at[2 * x + y], land_refs[a].at[2 * x + y],
                                                 local_sems.at[a]))
            else:
                part = (lambda i: src_refs[a].at[i]) if kind == "scatter" else (lambda i: src_refs[a])
                for k, (peer, peer_idx) in enumerate(peers):
                    first.append(copy(k, part(peer_idx), me, peer))
                    arrivals.append(copy(k, part(peer_idx), peer_idx, peer))
                sends += first[-(N_DEV - 1):]
                own.append(pltpu.make_async_copy(part(me), land_refs[a].at[me], local_sems.at[a]))
        return first, relayed_in, relayed_out, arrivals, sends, own

    def wrapped(*refs):
        step = 0
        for axis, size in enumerate(grid):
            step = step * size + pl.program_id(axis)

        @pl.when(step == 0)
        def _():
            first, _, _, _, _, own = plan(refs)
            for cp in first + own:
                cp.start()

        if "gather_by_chip" in kinds:
            @pl.when(step == relay_step)
            def _():
                _, relayed_in, relayed_out, _, _, _ = plan(refs)
                for cp_in, cp_out in zip(relayed_in, relayed_out):
                    cp_in.wait_recv()
                    cp_out.start()

        body(*refs[:n_in], *refs[n_in + n:n_in + n + n_out], *refs[n_in + 2 * n + n_out:n_in + 2 * n + n_out + n_scr])

        @pl.when(step == steps - 1)
        def _():
            _, _, _, arrivals, sends, own = plan(refs)
            for cp in arrivals:
                cp.wait_recv()
            for cp in sends:
                cp.wait_send()
            for cp in own:
                cp.wait()

    anywhere = pl.BlockSpec(memory_space=pl.ANY)
    lands = [jax.ShapeDtypeStruct((N_DEV // 2 if k == "scatter_by_chip" else N_DEV,) + s.shape[-2:], s.dtype)
             for s, k in send]
    out = pl.pallas_call(
        wrapped, name=name, grid=grid,
        in_specs=list(in_specs) + [anywhere] * n, out_specs=list(out_specs) + [anywhere] * n,
        out_shape=list(out_shape) + lands,
        scratch_shapes=list(scratch_shapes) + [pltpu.SemaphoreType.DMA((n * (N_DEV - 1),)),
                                               pltpu.SemaphoreType.DMA((n * (N_DEV - 1),)),
                                               pltpu.SemaphoreType.DMA((n,))],
        compiler_params=_params(("arbitrary",) * len(grid)),
    )(*operands, *srcs)
    return out[:n_out], list(out[n_out:])


def _pair_swap(blocks):
    def body(src_ref, out_ref, send_sems, recv_sems):
        x, y, c = lax.axis_index("x"), lax.axis_index("y"), lax.axis_index("c")
        copies = [pltpu.make_async_remote_copy(
            src_ref=src_ref.at[2 * chip + (1 - c)], dst_ref=out_ref.at[chip], send_sem=send_sems.at[chip],
            recv_sem=recv_sems.at[chip], device_id=(x, y, 1 - c), device_id_type=pl.DeviceIdType.MESH)
            for chip in range(N_DEV // 2)]
        for cp in copies:
            cp.start()
        for cp in copies:
            cp.wait()

    return pl.pallas_call(
        body, name="pair_swap_grad_w_in",
        in_specs=[pl.BlockSpec(memory_space=pl.ANY)], out_specs=pl.BlockSpec(memory_space=pl.ANY),
        out_shape=jax.ShapeDtypeStruct((N_DEV // 2,) + blocks.shape[1:], blocks.dtype),
        scratch_shapes=[pltpu.SemaphoreType.DMA((N_DEV // 2,)), pltpu.SemaphoreType.DMA((N_DEV // 2,))],
    )(blocks)


def _pair_sum(blocks, swapped, core):
    _, rows, cols = swapped.shape
    tile_rows = _row_tile(rows)

    def body(core_ref, mine_ref, theirs_ref, o_ref):
        o_ref[...] = (mine_ref[...].astype(F32) + theirs_ref[...].astype(F32)).astype(o_ref.dtype)

    return pl.pallas_call(
        body, name="pair_sum_grad_w_in",
        grid_spec=pltpu.PrefetchScalarGridSpec(
            num_scalar_prefetch=1, grid=(N_DEV // 2, rows // tile_rows),
            in_specs=[pl.BlockSpec((None, tile_rows, cols), lambda j, i, core_ref: (2 * j + core_ref[0], i, 0)),
                      pl.BlockSpec((None, tile_rows, cols), lambda j, i, core_ref: (j, i, 0))],
            out_specs=pl.BlockSpec((None, tile_rows, cols), lambda j, i, core_ref: (j, i, 0))),
        out_shape=jax.ShapeDtypeStruct(swapped.shape, swapped.dtype),
        compiler_params=_params(("parallel", "parallel")),
    )(core, blocks, swapped)


def _sum_in_device_order(land_ref):
    acc = land_ref[0].astype(F32)
    for j in range(1, land_ref.shape[0]):
        acc = acc + land_ref[j].astype(F32)
    return acc


def _adam_math(w, g, m, v):
    c1 = 1.0 - ADAM_B1 ** ADAM_STEP
    c2 = 1.0 - ADAM_B2 ** ADAM_STEP
    m_new = ADAM_B1 * m + (1.0 - ADAM_B1) * g
    v_new = ADAM_B2 * v + (1.0 - ADAM_B2) * (g * g)
    delta = -ADAM_LR * ((m_new / c1) / (jnp.sqrt(v_new / c2) + ADAM_EPS) + ADAM_WD * w)
    return delta, m_new, v_new


def _row_tile(rows):
    return max(t for t in range(8, 257, 8) if rows % t == 0) if rows % 8 == 0 else rows


def _sum_update(land, w, m, v, name):
    slots, rows, cols = land.shape
    tile_rows = _row_tile(rows)

    def body(land_ref, w_ref, m_ref, v_ref, g_ref, d_ref, nm_ref, nv_ref):
        g = _sum_in_device_order(land_ref)
        g_ref[...] = g
        d_ref[...], nm_ref[...], nv_ref[...] = _adam_math(w_ref[...], g, m_ref[...], v_ref[...])

    tile = pl.BlockSpec((None, tile_rows, cols), lambda i: (0, i, 0))
    out = jax.ShapeDtypeStruct((1, rows, cols), F32)
    return pl.pallas_call(
        body, name=name, grid=(rows // tile_rows,),
        in_specs=[pl.BlockSpec((slots, tile_rows, cols), lambda i: (0, i, 0)), tile, tile, tile],
        out_specs=[tile] * 4, out_shape=[out] * 4,
        compiler_params=_params(("parallel",)),
    )(land, w, m, v)


def _sum_gains(land):
    def body(land_ref, o_ref):
        o_ref[...] = _sum_in_device_order(land_ref)

    return pl.pallas_call(
        body, name="sum_gain_grads", grid=(1,),
        in_specs=[pl.BlockSpec(land.shape, lambda i: (0, 0, 0))],
        out_specs=pl.BlockSpec(land.shape[1:], lambda i: (0, 0)),
        out_shape=jax.ShapeDtypeStruct(land.shape[1:], F32),
    )(land)


def _adamw(w, g, m, v, name):
    def body(w_ref, g_ref, m_ref, v_ref, d_ref, nm_ref, nv_ref):
        d_ref[...], nm_ref[...], nv_ref[...] = _adam_math(w_ref[...], g_ref[...], m_ref[...], v_ref[...])

    whole = pl.BlockSpec(w.shape, lambda i: (0, 0))
    out = jax.ShapeDtypeStruct(w.shape, F32)
    return pl.pallas_call(
        body, name=name, grid=(1,),
        in_specs=[whole] * 4, out_specs=[whole] * 3, out_shape=[out] * 3,
    )(w, g, m, v)


GROUP_FFN = ("w_ffn_in", "w_ffn_out")
GROUP_MIX = ("w_sb_up", "w_dil_up", "w_out")
COL_SHARDED = ("w_in", "w_sb_up", "w_dil_up", "w_ffn_in")


def _full_from_shards(name, slots):
    _, r, c = slots.shape
    if name in COL_SHARDED:
        return slots.transpose(1, 0, 2).reshape(r, N_DEV * c)
    return slots.reshape(N_DEV * r, c)


def _shards_from_full(name, full):
    rows, cols = full.shape
    if name in COL_SHARDED:
        return full.reshape(rows, N_DEV, cols // N_DEV).transpose(1, 0, 2)
    return full.reshape(N_DEV, rows // N_DEV, cols)


def _local_step(x, target, g_mix, g_ffn, g_fin, w_in, shards=None, rest=None):
    gather = lambda names, kind: None if shards is None else [(shards[n], kind) for n in names]
    scatter = lambda blocks: None if shards is None else [(t, "scatter") for t in blocks]
    landed = lambda blocks, lands: lands if lands else blocks

    w = {"w_in": w_in}
    if shards is None:
        w.update(rest)
        w["w_ffn_in"] = _shards_from_full("w_ffn_in", rest["w_ffn_in"])
    qkv_sb, qkv_dl, gates, u = _norm_proj(x, g_mix, w["w_in"])
    qkv_sb = qkv_sb.reshape(B_LOC, SEQ, 3 * SB_WIDTH)
    qkv_dl = qkv_dl.reshape(B_LOC, SEQ, 3 * DIL_WIDTH)
    (o_sb,), lands = _sb_fwd(qkv_sb, gather(GROUP_FFN, "gather_by_chip"))
    if lands:
        w["w_ffn_in"], w["w_ffn_out"] = lands[0], _full_from_shards("w_ffn_out", lands[1])
    o_sb = o_sb.reshape(TOK, SB_WIDTH)
    (o_dl, lse), lands = _dil_fwd(qkv_dl, gather(GROUP_MIX, "gather"))
    w.update({n: _full_from_shards(n, t) for n, t in zip(GROUP_MIX, lands)})
    o_dl = o_dl.reshape(TOK, DIL_OUT)

    x1, merged = _mix_out(x, o_sb, o_dl, gates, w["w_sb_up"], w["w_dil_up"], w["w_out"])
    loss, dx1, u2, act, dh, dx2, dg_fin, dg_ffn = _ffn_fwd_bwd(x1, target, g_ffn, g_fin, w["w_ffn_in"], w["w_ffn_out"])
    dgates, dy_sb, dy_dl, do_sb, do_dl, dsum = _mix_bwd(dx1, o_sb, o_dl, gates, w["w_sb_up"], w["w_dil_up"], w["w_out"])
    blocks = {
        "w_sb_up": _atb(o_sb, dy_sb, "grad_w_sb_up", SB_WIDTH, D_MODEL, col_blocks=N_DEV),
        "w_dil_up": _atb(o_dl, dy_dl, "grad_w_dil_up", DIL_OUT, D_MODEL, col_blocks=N_DEV),
        "w_out": _shards_from_full("w_out", _atb(merged, dx1, "grad_w_out", D_MODEL, D_MODEL)),
        "w_ffn_in": _atb_shards(u2, dh, "grad_w_ffn_in"),
        "w_ffn_out": _shards_from_full("w_ffn_out", _atb_shards(act, dx2, "grad_w_ffn_out")),
    }
    grads = {}

    early, late = ("w_ffn_in",), ("w_ffn_out",) + GROUP_MIX
    early_blocks = [blocks[n] for n in early]
    (dq_sb, dk_sb, dv_sb), lands = _sb_bwd(qkv_sb, do_sb.reshape(B_LOC, SEQ, SB_WIDTH), scatter(early_blocks))
    grads.update(zip(early, landed(early_blocks, lands)))
    as_batch = lambda t: t.reshape(B_LOC, SEQ, DIL_OUT)
    late_blocks = [blocks[n] for n in late]
    d_dl, lands = _dil_bwd(qkv_dl, as_batch(do_dl), lse, as_batch(dsum), scatter(late_blocks))
    grads.update(zip(late, landed(late_blocks, lands)))
    flat = lambda t: t.reshape(TOK, -1)
    dproj = ([flat(dq_sb), flat(dk_sb), flat(dv_sb)]
             + [flat(d_dl[3 * grp + part]) for part in range(3) for grp in range(DIL_GROUPS)] + [dgates])

    w_in_blocks = _shards_from_full("w_in", _atb_pieces(u, dproj, "grad_w_in", D_MODEL // 2))
    if shards is None:
        grads["w_in"] = w_in_blocks
        send = None
    else:
        core = lax.axis_index("c").astype(jnp.int32).reshape(1)
        send = [(_pair_sum(w_in_blocks, _pair_swap(w_in_blocks), core), "scatter_by_chip")]
    (grad_x, dg_mix), lands = _proj_bwd(dproj, dx1, x, g_mix, w["w_in"], send)
    if lands:
        grads["w_in"] = lands[0]
    gain_grads = jnp.concatenate([dg_mix, dg_ffn, dg_fin], axis=0)
    return loss, grad_x, gain_grads, grads


def kernel(x, norm_mix_g, w_in, w_sb_up, w_dil_up, w_out, norm_ffn_g, w_ffn_in, w_ffn_out, norm_final_g, loss_target, m_norm_mix_g, m_w_in, m_w_sb_up, m_w_dil_up, m_w_out, m_norm_ffn_g, m_w_ffn_in, m_w_ffn_out, m_norm_final_g, v_norm_mix_g, v_w_in, v_w_sb_up, v_w_dil_up, v_w_out, v_norm_ffn_g, v_w_ffn_in, v_w_ffn_out, v_norm_final_g):
    mats = {"w_in": w_in, "w_sb_up": w_sb_up, "w_dil_up": w_dil_up, "w_out": w_out,
            "w_ffn_in": w_ffn_in, "w_ffn_out": w_ffn_out}
    moments_m = {"w_in": m_w_in, "w_sb_up": m_w_sb_up, "w_dil_up": m_w_dil_up, "w_out": m_w_out,
                 "w_ffn_in": m_w_ffn_in, "w_ffn_out": m_w_ffn_out}
    moments_v = {"w_in": v_w_in, "w_sb_up": v_w_sb_up, "w_dil_up": v_w_dil_up, "w_out": v_w_out,
                 "w_ffn_in": v_w_ffn_in, "w_ffn_out": v_w_ffn_out}
    gathered_w_in = _all_gather(w_in[0].astype(BF16), "all_gather_w_in")
    g_fin = norm_final_g.reshape(1, D_MODEL)
    loss, grad_x, gain_grads, grad_slots = _local_step(
        x.reshape(TOK, D_MODEL), loss_target.reshape(TOK, D_MODEL), norm_mix_g, norm_ffn_g, g_fin,
        _full_from_shards("w_in", gathered_w_in),
        shards={name: mats[name][0].astype(BF16) for name in GROUP_FFN + GROUP_MIX})

    gain_rows = jnp.concatenate([gain_grads, jnp.tile(loss, (1, D_MODEL // LANES)),
                                 jnp.zeros((8 - 4, D_MODEL), F32)], axis=0)
    g_gains = _sum_gains(_all_gather(gain_rows, "all_gather_gains"))

    out_g, out_d, out_m, out_v = {}, {}, {}, {}
    for name, slots in grad_slots.items():
        out_g[name], out_d[name], out_m[name], out_v[name] = _sum_update(
            slots, mats[name], moments_m[name], moments_v[name], "update_" + name)

    gain_w = jnp.concatenate([norm_mix_g, norm_ffn_g, g_fin], axis=0)
    gain_m = jnp.concatenate([m_norm_mix_g, m_norm_ffn_g, m_norm_final_g.reshape(1, D_MODEL)], axis=0)
    gain_v = jnp.concatenate([v_norm_mix_g, v_norm_ffn_g, v_norm_final_g.reshape(1, D_MODEL)], axis=0)
    gd, gm, gv = _adamw(gain_w, g_gains[:3], gain_m, gain_v, "adamw_gains")
    for idx, name in enumerate(("norm_mix_g", "norm_ffn_g", "norm_final_g")):
        shape = (D_MODEL,) if name == "norm_final_g" else (1, D_MODEL)
        out_g[name] = g_gains[idx].reshape(shape)
        out_d[name], out_m[name], out_v[name] = gd[idx].reshape(shape), gm[idx].reshape(shape), gv[idx].reshape(shape)

    order = ("norm_mix_g", "w_in", "w_sb_up", "w_dil_up", "w_out", "norm_ffn_g", "w_ffn_in", "w_ffn_out",
             "norm_final_g")
    return (g_gains[3, 0], grad_x.reshape(B_LOC, SEQ, D_MODEL),
            *[out_g[n] for n in order], *[out_d[n] for n in order],
            *[out_m[n] for n in order], *[out_v[n] for n in order])
```

```python
import math

import jax
import jax.numpy as jnp
from jax import lax
from jax.experimental import pallas as pl
from jax.experimental.pallas import tpu as pltpu

F32 = jnp.float32
BF16 = jnp.bfloat16

N_DEV = 8
D_MODEL = 1024
SEQ = 2048
B_LOC = 2
TOK = B_LOC * SEQ
HEAD_DIM = 64
SB_WIDTH = 512
DIL_WIDTH = 768
DIL_OUT = 256
QKV_WIDTH = 3 * SB_WIDTH + 3 * DIL_WIDTH
IN_WIDTH = QKV_WIDTH + 2 * D_MODEL
D_FF = 2816
DIL_PAIRS = ((128, 1), (512, 4), (2048, 16))
DIL_HEADS = 12
RMS_EPS = 1e-6
ALIBI_MAX_BIAS = 8.0
QK_SCALE = 1.0 / math.sqrt(HEAD_DIM)
BLK = 128
LANES = 128
NEG_BIG = -1e30

ADAM_LR = 0.001
ADAM_B1 = 0.9
ADAM_B2 = 0.999
ADAM_EPS = 1e-08
ADAM_WD = 0.01
ADAM_STEP = 10

VMEM_LIMIT = 56 * 1024 * 1024


def _dot(a, b):
    return jnp.dot(a, b, preferred_element_type=F32)


def _dot_nt(a, b):
    return lax.dot_general(a, b, (((1,), (1,)), ((), ())), preferred_element_type=F32)


def _dot_tn(a, b):
    return lax.dot_general(a, b, (((0,), (0,)), ((), ())), preferred_element_type=F32)


def _sigmoid(z):
    return 1.0 / (1.0 + jnp.exp(-z))


def _split_bf16(v):
    hi = v.astype(BF16)
    lo = (v - hi.astype(F32)).astype(BF16)
    return hi, lo


def _chunks(width, step=512):
    out, c = [], 0
    while c < width:
        w = min(step, width - c)
        out.append((c, w))
        c += w
    return out


def _resident(shape):
    nd = len(shape)
    return pl.BlockSpec(shape, lambda *_: (0,) * nd, pipeline_mode=pl.Buffered(1))


def _params(sem):
    return pltpu.CompilerParams(dimension_semantics=sem, vmem_limit_bytes=VMEM_LIMIT)


def _rms_fwd(x, g):
    r = lax.rsqrt(jnp.mean(x * x, axis=-1, keepdims=True) + RMS_EPS)
    n = x * r
    return n, r, n * g


def _rms_bwd(dy, n, r, g):
    dg = jnp.sum(dy * n, axis=0, keepdims=True)
    dn = dy * g
    dx = r * (dn - n * jnp.mean(dn * n, axis=-1, keepdims=True))
    return dx, dg


TM = 256


def _norm_proj(x, g, w_in):
    def body(x_ref, g_ref, w_ref, sb_ref, dl_ref, gate_ref, u_ref):
        _, _, u = _rms_fwd(x_ref[...], g_ref[...])
        u = u.astype(BF16)
        u_ref[...] = u
        for c0, w in _chunks(3 * SB_WIDTH):
            sb_ref[:, c0:c0 + w] = _dot(u, w_ref[:, c0:c0 + w]).astype(BF16)
        for c0, w in _chunks(3 * DIL_WIDTH):
            dl_ref[:, c0:c0 + w] = _dot(u, w_ref[:, 3 * SB_WIDTH + c0:3 * SB_WIDTH + c0 + w])
        for c0, w in _chunks(2 * D_MODEL):
            gate_ref[:, c0:c0 + w] = _dot(u, w_ref[:, QKV_WIDTH + c0:QKV_WIDTH + c0 + w])

    return pl.pallas_call(
        body, name="norm_proj", grid=(TOK // TM,),
        in_specs=[pl.BlockSpec((TM, D_MODEL), lambda i: (i, 0)), _resident((1, D_MODEL)),
                  _resident((D_MODEL, IN_WIDTH))],
        out_specs=[pl.BlockSpec((TM, 3 * SB_WIDTH), lambda i: (i, 0)),
                   pl.BlockSpec((TM, 3 * DIL_WIDTH), lambda i: (i, 0)),
                   pl.BlockSpec((TM, 2 * D_MODEL), lambda i: (i, 0)),
                   pl.BlockSpec((TM, D_MODEL), lambda i: (i, 0))],
        out_shape=[jax.ShapeDtypeStruct((TOK, 3 * SB_WIDTH), BF16),
                   jax.ShapeDtypeStruct((TOK, 3 * DIL_WIDTH), F32),
                   jax.ShapeDtypeStruct((TOK, 2 * D_MODEL), F32),
                   jax.ShapeDtypeStruct((TOK, D_MODEL), BF16)],
        compiler_params=_params(("parallel",)),
    )(x, g, w_in)


def _mix_out(x, o_sb, o_dl, gates, w_sb_up, w_dil_up, w_out):
    def body(x_ref, osb_ref, odl_ref, gate_ref, wsb_ref, wdl_ref, wout_ref, x1_ref, mg_ref):
        y_sb = _dot(osb_ref[...], wsb_ref[...])
        y_dl = _dot(odl_ref[...].astype(BF16), wdl_ref[...])
        merged = (_sigmoid(gate_ref[:, :D_MODEL]) * y_sb
                  + _sigmoid(gate_ref[:, D_MODEL:]) * y_dl).astype(BF16)
        mg_ref[...] = merged
        x1_ref[...] = x_ref[...] + _dot(merged, wout_ref[...])

    return pl.pallas_call(
        body, name="mix_out", grid=(TOK // TM,),
        in_specs=[pl.BlockSpec((TM, D_MODEL), lambda i: (i, 0)),
                  pl.BlockSpec((TM, SB_WIDTH), lambda i: (i, 0)),
                  pl.BlockSpec((TM, DIL_OUT), lambda i: (i, 0)),
                  pl.BlockSpec((TM, 2 * D_MODEL), lambda i: (i, 0)),
                  _resident((SB_WIDTH, D_MODEL)), _resident((DIL_OUT, D_MODEL)),
                  _resident((D_MODEL, D_MODEL))],
        out_specs=[pl.BlockSpec((TM, D_MODEL), lambda i: (i, 0)),
                   pl.BlockSpec((TM, D_MODEL), lambda i: (i, 0))],
        out_shape=[jax.ShapeDtypeStruct((TOK, D_MODEL), F32),
                   jax.ShapeDtypeStruct((TOK, D_MODEL), BF16)],
        compiler_params=_params(("parallel",)),
    )(x, o_sb, o_dl, gates, w_sb_up, w_dil_up, w_out)


FF_SHARD = 2 * D_FF // N_DEV
FF_PAIRS = N_DEV // 2


def _ffn_fwd_bwd(x1, target, g_ffn, g_fin, w_ffn_in, w_ffn_out):
    def body(x1_ref, t_ref, gffn_ref, gfin_ref, win_ref, wout_ref,
             loss_ref, dx1_ref, u2_ref, act_ref, dh_ref, dx2_ref, dgfin_ref, dgffn_ref, h_scr):
        i = pl.program_id(0)

        @pl.when(i == 0)
        def _():
            loss_ref[...] = jnp.zeros_like(loss_ref)
            dgfin_ref[...] = jnp.zeros_like(dgfin_ref)
            dgffn_ref[...] = jnp.zeros_like(dgffn_ref)

        x1 = x1_ref[...]
        g_ffn_v = gffn_ref[...]
        g_fin_v = gfin_ref[...]
        n2, r2, u2 = _rms_fwd(x1, g_ffn_v)
        u2 = u2.astype(BF16)
        u2_ref[...] = u2
        x2 = x1
        for r in range(FF_PAIRS):
            gate = _dot(u2, win_ref[r])
            up = _dot(u2, win_ref[r + FF_PAIRS])
            h_scr[r] = gate
            h_scr[r + FF_PAIRS] = up
            act = (gate * _sigmoid(gate) * up).astype(BF16)
            act_ref[r] = act
            x2 = x2 + _dot(act, wout_ref[r * FF_SHARD:(r + 1) * FF_SHARD, :])
        n3, r3, y = _rms_fwd(x2, g_fin_v)
        err = y - t_ref[...]
        sq = jnp.sum(jnp.sum(err * err, axis=1, keepdims=True), axis=0, keepdims=True)
        loss_ref[...] += sq * (0.5 / D_MODEL)
        dx2, dgfin = _rms_bwd(err * (1.0 / D_MODEL), n3, r3, g_fin_v)
        dgfin_ref[...] += dgfin
        dx2_b = dx2.astype(BF16)
        dx2_ref[...] = dx2_b
        du2 = jnp.zeros((TM, D_MODEL), F32)
        for r in range(FF_PAIRS):
            gate = h_scr[r]
            up = h_scr[r + FF_PAIRS]
            dact = _dot_nt(dx2_b, wout_ref[r * FF_SHARD:(r + 1) * FF_SHARD, :])
            sg = _sigmoid(gate)
            dgate = (dact * up * (sg * (1.0 + gate * (1.0 - sg)))).astype(BF16)
            dup = (dact * (gate * sg)).astype(BF16)
            dh_ref[r] = dgate
            dh_ref[r + FF_PAIRS] = dup
            du2 = du2 + _dot_nt(dgate, win_ref[r])
            du2 = du2 + _dot_nt(dup, win_ref[r + FF_PAIRS])
        dx1_n, dgffn = _rms_bwd(du2, n2, r2, g_ffn_v)
        dgffn_ref[...] += dgffn
        dx1_ref[...] = dx2 + dx1_n

    tile = lambda w: pl.BlockSpec((TM, w), lambda i: (i, 0))
    shards = lambda n: pl.BlockSpec((n, TM, FF_SHARD), lambda i: (0, i, 0))
    acc = lambda w: pl.BlockSpec((1, w), lambda i: (0, 0))
    return pl.pallas_call(
        body, name="ffn_fwd_bwd", grid=(TOK // TM,),
        in_specs=[tile(D_MODEL), tile(D_MODEL), _resident((1, D_MODEL)), _resident((1, D_MODEL)),
                  _resident((N_DEV, D_MODEL, FF_SHARD)), _resident((D_FF, D_MODEL))],
        out_specs=[acc(LANES), tile(D_MODEL), tile(D_MODEL), shards(FF_PAIRS), shards(N_DEV), tile(D_MODEL),
                   acc(D_MODEL), acc(D_MODEL)],
        out_shape=[jax.ShapeDtypeStruct((1, LANES), F32),
                   jax.ShapeDtypeStruct((TOK, D_MODEL), F32),
                   jax.ShapeDtypeStruct((TOK, D_MODEL), BF16),
                   jax.ShapeDtypeStruct((FF_PAIRS, TOK, FF_SHARD), BF16),
                   jax.ShapeDtypeStruct((N_DEV, TOK, FF_SHARD), BF16),
                   jax.ShapeDtypeStruct((TOK, D_MODEL), BF16),
                   jax.ShapeDtypeStruct((1, D_MODEL), F32),
                   jax.ShapeDtypeStruct((1, D_MODEL), F32)],
        scratch_shapes=[pltpu.VMEM((N_DEV, TM, FF_SHARD), F32)],
        compiler_params=_params(("arbitrary",)),
    )(x1, target, g_ffn, g_fin, w_ffn_in, w_ffn_out)


def _mix_bwd(dx1, o_sb, o_dl, gates, w_sb_up, w_dil_up, w_out):
    def body(dx1_ref, osb_ref, odl_ref, gate_ref, wsb_ref, wdl_ref, wout_ref,
             dgate_ref, dysb_ref, dydl_ref, dosb_ref, dodl_ref, dsum_ref):
        dmerged = _dot_nt(dx1_ref[...].astype(BF16), wout_ref[...])
        o_dl = odl_ref[...]
        y_sb = _dot(osb_ref[...], wsb_ref[...])
        y_dl = _dot(o_dl.astype(BF16), wdl_ref[...])
        s_sb = _sigmoid(gate_ref[:, :D_MODEL])
        s_dl = _sigmoid(gate_ref[:, D_MODEL:])
        dgate_ref[:, :D_MODEL] = (dmerged * y_sb * (s_sb * (1.0 - s_sb))).astype(BF16)
        dgate_ref[:, D_MODEL:] = (dmerged * y_dl * (s_dl * (1.0 - s_dl))).astype(BF16)
        dy_sb = (dmerged * s_sb).astype(BF16)
        dy_dl = (dmerged * s_dl).astype(BF16)
        dysb_ref[...] = dy_sb
        dydl_ref[...] = dy_dl
        dosb_ref[...] = _dot_nt(dy_sb, wsb_ref[...]).astype(BF16)
        do_dl = _dot_nt(dy_dl, wdl_ref[...])
        dodl_ref[...] = do_dl
        row = lax.broadcasted_iota(jnp.int32, (DIL_OUT, DIL_OUT), 0) // HEAD_DIM
        col = lax.broadcasted_iota(jnp.int32, (DIL_OUT, DIL_OUT), 1) // HEAD_DIM
        same_head = (row == col).astype(BF16)
        hi, lo = _split_bf16(do_dl * o_dl)
        dsum_ref[...] = _dot(hi, same_head) + _dot(lo, same_head)

    tile = lambda w: pl.BlockSpec((TM, w), lambda i: (i, 0))
    return pl.pallas_call(
        body, name="mix_bwd", grid=(TOK // TM,),
        in_specs=[tile(D_MODEL), tile(SB_WIDTH), tile(DIL_OUT), tile(2 * D_MODEL),
                  _resident((SB_WIDTH, D_MODEL)), _resident((DIL_OUT, D_MODEL)),
                  _resident((D_MODEL, D_MODEL))],
        out_specs=[tile(2 * D_MODEL), tile(D_MODEL), tile(D_MODEL), tile(SB_WIDTH), tile(DIL_OUT),
                   tile(DIL_OUT)],
        out_shape=[jax.ShapeDtypeStruct((TOK, 2 * D_MODEL), BF16),
                   jax.ShapeDtypeStruct((TOK, D_MODEL), BF16),
                   jax.ShapeDtypeStruct((TOK, D_MODEL), BF16),
                   jax.ShapeDtypeStruct((TOK, SB_WIDTH), BF16),
                   jax.ShapeDtypeStruct((TOK, DIL_OUT), F32),
                   jax.ShapeDtypeStruct((TOK, DIL_OUT), F32)],
        compiler_params=_params(("parallel",)),
    )(dx1, o_sb, o_dl, gates, w_sb_up, w_dil_up, w_out)


def _proj_bwd(dproj, dx1, x, g, w_in, send=None):
    widths = [p.shape[1] for p in dproj]

    def body(*refs):
        dx1_ref, x_ref, g_ref, w_ref, dx_ref, dg_ref = refs[len(widths):]

        @pl.when(pl.program_id(0) == 0)
        def _():
            dg_ref[...] = jnp.zeros_like(dg_ref)

        du = jnp.zeros((TM, D_MODEL), F32)
        c0 = 0
        for dp_ref, w in zip(refs, widths):
            du = du + _dot_nt(dp_ref[...].astype(BF16), w_ref[:, c0:c0 + w])
            c0 += w
        g_v = g_ref[...]
        n, r, _ = _rms_fwd(x_ref[...], g_v)
        dx, dg = _rms_bwd(du, n, r, g_v)
        dg_ref[...] += dg
        dx_ref[...] = dx1_ref[...] + dx

    tile = lambda w: pl.BlockSpec((TM, w), lambda i: (i, 0))
    return _call(
        body, send, name="proj_bwd", grid=(TOK // TM,),
        in_specs=[tile(w) for w in widths] + [tile(D_MODEL), tile(D_MODEL), _resident((1, D_MODEL)),
                                              _resident((D_MODEL, IN_WIDTH))],
        out_specs=[tile(D_MODEL), pl.BlockSpec((1, D_MODEL), lambda i: (0, 0))],
        out_shape=[jax.ShapeDtypeStruct((TOK, D_MODEL), F32),
                   jax.ShapeDtypeStruct((1, D_MODEL), F32)],
        scratch_shapes=[], semantics=("arbitrary",), operands=(*dproj, dx1, x, g, w_in))


def _atb_pieces(a, pieces, name, tm, tk=512):
    m = a.shape[1]
    widths = [p.shape[1] for p in pieces]
    n = sum(widths)
    nk = TOK // tk

    def body(a_ref, *refs):
        o_ref, acc_ref = refs[len(widths):]
        k = pl.program_id(1)

        @pl.when(k == 0)
        def _():
            acc_ref[...] = jnp.zeros_like(acc_ref)

        a_v = a_ref[...]
        c0 = 0
        for p_ref, w in zip(refs, widths):
            acc_ref[:, c0:c0 + w] += _dot_tn(a_v, p_ref[...].astype(BF16))
            c0 += w

        @pl.when(k == nk - 1)
        def _():
            o_ref[...] = acc_ref[...].astype(BF16)

    return pl.pallas_call(
        body, name=name, grid=(m // tm, nk),
        in_specs=[pl.BlockSpec((tk, tm), lambda i, k: (k, i))]
                 + [pl.BlockSpec((tk, w), lambda i, k: (k, 0)) for w in widths],
        out_specs=pl.BlockSpec((tm, n), lambda i, k: (i, 0)),
        out_shape=jax.ShapeDtypeStruct((m, n), BF16),
        scratch_shapes=[pltpu.VMEM((tm, n), F32)],
        compiler_params=_params(("parallel", "arbitrary")),
    )(a, *pieces)


def _atb_shards(a, b, name, tk=512):
    a_sharded = a.ndim == 3
    n, _, w = a.shape if a_sharded else b.shape
    other = (b if a_sharded else a).shape[1]

    def body(a_ref, b_ref, o_ref):
        acc = jnp.zeros(o_ref.shape, F32)
        for k0 in range(0, TOK, tk):
            acc = acc + _dot_tn(a_ref[k0:k0 + tk, :], b_ref[k0:k0 + tk, :])
        o_ref[...] = acc.astype(BF16)

    shard = pl.BlockSpec((None, TOK, w), lambda r: (r, 0, 0))
    whole = _resident((TOK, other))
    if a_sharded:
        out_spec, out_shape = pl.BlockSpec((w, other), lambda r: (r, 0)), (n * w, other)
    else:
        out_spec, out_shape = pl.BlockSpec((None, other, w), lambda r: (r, 0, 0)), (n, other, w)
    return pl.pallas_call(
        body, name=name, grid=(n,),
        in_specs=[shard, whole] if a_sharded else [whole, shard],
        out_specs=out_spec, out_shape=jax.ShapeDtypeStruct(out_shape, BF16),
        compiler_params=_params(("parallel",)),
    )(a, b)


def _atb(a, b, name, tm, tn, col_blocks=0, tk=512):
    m, n = a.shape[1], b.shape[1]
    nk = TOK // tk

    def body(a_ref, b_ref, o_ref, acc_ref):
        k = pl.program_id(2)

        @pl.when(k == 0)
        def _():
            acc_ref[...] = jnp.zeros_like(acc_ref)

        acc_ref[...] += _dot_tn(a_ref[...].astype(BF16), b_ref[...].astype(BF16))

        @pl.when(k == nk - 1)
        def _():
            if col_blocks:
                width = n // col_blocks
                for blk in range(col_blocks):
                    o_ref[blk] = acc_ref[:, blk * width:(blk + 1) * width].astype(BF16)
            else:
                o_ref[...] = acc_ref[...].astype(BF16)

    if col_blocks:
        out_spec = pl.BlockSpec((col_blocks, tm, n // col_blocks), lambda i, j, k: (0, i, 0))
        out_shape = jax.ShapeDtypeStruct((col_blocks, m, n // col_blocks), BF16)
    else:
        out_spec = pl.BlockSpec((tm, tn), lambda i, j, k: (i, j))
        out_shape = jax.ShapeDtypeStruct((m, n), BF16)
    return pl.pallas_call(
        body, name=name, grid=(m // tm, n // tn, nk),
        in_specs=[pl.BlockSpec((tk, tm), lambda i, j, k: (k, i)),
                  pl.BlockSpec((tk, tn), lambda i, j, k: (k, j))],
        out_specs=out_spec, out_shape=out_shape,
        scratch_shapes=[pltpu.VMEM((tm, tn), F32)],
        compiler_params=_params(("parallel", "parallel", "arbitrary")),
    )(a, b)


SB_PAIRS = SB_WIDTH // LANES


def _two_heads(v, lane0):
    zero = jnp.zeros_like(v)
    return jnp.where(lane0, v, zero), jnp.where(lane0, zero, v)


SB_QBLK = 256
N_SB_STEPS = SEQ // SB_QBLK


SB_KCHUNK = 2 * BLK
SB_ROWS = 2 * SB_QBLK
SB_DEAD = -104.0


def _log_keep(z):
    neg_z = -z
    return jnp.minimum(neg_z, 0.0) - jnp.log(1.0 + jnp.exp(jnp.minimum(z, neg_z)))


def _stack_heads(v, lane0):
    return jnp.concatenate(_two_heads(v, lane0), axis=0)


def _block_sums(v, tri):
    halves = (v[:, :BLK], v[:, BLK:])
    hi, lo = _split_bf16(jnp.concatenate(halves, axis=0))
    prod = _dot(jnp.concatenate([hi, lo], axis=0), tri)
    tri_sum = prod[:2 * SB_ROWS] + prod[2 * SB_ROWS:]
    sums = tuple(jnp.sum(h, axis=1, keepdims=True) for h in halves)
    return (tri_sum[:SB_ROWS], tri_sum[SB_ROWS:]), sums


def _sb_diag_mask():
    row = lax.broadcasted_iota(jnp.int32, (SB_ROWS, SB_KCHUNK), 0)
    col = lax.broadcasted_iota(jnp.int32, (SB_ROWS, SB_KCHUNK), 1)
    return col < jnp.where(row >= SB_QBLK, row - SB_QBLK, row)


def _sb_fwd(qkv, send=None):
    def body(q_ref, k_ref, v_ref, o_ref):
        i = pl.program_id(2)
        krow = lax.broadcasted_iota(jnp.int32, (BLK, BLK), 0)
        kcol = lax.broadcasted_iota(jnp.int32, (BLK, BLK), 1)
        later = (krow > kcol).astype(BF16)
        lane0 = lax.broadcasted_iota(jnp.int32, (SB_QBLK, LANES), 1) < HEAD_DIM
        q2 = _stack_heads(q_ref[0] * QK_SCALE, lane0)

        def chunk(c, carry, causal):
            acc, run = carry
            off = pl.multiple_of(c * SB_KCHUNK, SB_KCHUNK)
            z = _dot_nt(q2, k_ref[0, pl.ds(off, SB_KCHUNK), :])
            log_keep = _log_keep(z)
            if causal is not None:
                log_keep = jnp.where(causal, log_keep, 0.0)
            suffix, sums = _block_sums(log_keep, later)
            log_after = jnp.concatenate([suffix[0] + (run + sums[1]), suffix[1] + run], axis=1)
            a = jnp.exp(log_keep + z + log_after)
            if causal is not None:
                a = jnp.where(causal, a, 0.0)
            acc = acc + _dot(a.astype(BF16), v_ref[0, pl.ds(off, SB_KCHUNK), :])
            return acc, run + (sums[0] + sums[1])

        acc, run = chunk(i, (jnp.zeros((SB_ROWS, LANES), F32), jnp.zeros((SB_ROWS, 1), F32)), _sb_diag_mask())

        def live(state):
            t, _, run = state
            return jnp.logical_and(t < i, jnp.max(run) > SB_DEAD)

        def trip(state):
            t, acc, run = state
            acc, run = chunk(i - 1 - t, (acc, run), None)
            return t + 1, acc, run

        _, acc, _ = lax.while_loop(live, trip, (jnp.int32(0), acc, run))
        o_ref[0] = jnp.where(lane0, acc[:SB_QBLK], acc[SB_QBLK:]).astype(BF16)

    blk = pl.BlockSpec((1, SB_QBLK, LANES), lambda b, h, i: (b, i, h))
    return _call(
        body, send, name="sb_fwd", grid=(B_LOC, SB_PAIRS, N_SB_STEPS),
        in_specs=[blk,
                  pl.BlockSpec((1, SEQ, LANES), lambda b, h, i: (b, 0, SB_PAIRS + h)),
                  pl.BlockSpec((1, SEQ, LANES), lambda b, h, i: (b, 0, 2 * SB_PAIRS + h))],
        out_specs=[blk], out_shape=[jax.ShapeDtypeStruct((B_LOC, SEQ, SB_WIDTH), BF16)],
        scratch_shapes=[], semantics=("parallel", "parallel", "arbitrary"), operands=(qkv, qkv, qkv))


def _sb_bwd(qkv, d_o, send=None):
    def body(q_ref, k_ref, v_ref, do_ref, dq_ref, dk_ref, dv_ref, dk_acc, dv_acc, z_scr, keep_scr):
        i = pl.program_id(2)
        krow = lax.broadcasted_iota(jnp.int32, (BLK, BLK), 0)
        kcol = lax.broadcasted_iota(jnp.int32, (BLK, BLK), 1)
        upto = (krow <= kcol).astype(BF16)
        earlier = (krow < kcol).astype(BF16)
        lane0 = lax.broadcasted_iota(jnp.int32, (SB_QBLK, LANES), 1) < HEAD_DIM
        q2 = _stack_heads(q_ref[0] * QK_SCALE, lane0)
        do2 = _stack_heads(do_ref[0], lane0)

        def keep_sum(c, causal):
            off = pl.multiple_of(c * SB_KCHUNK, SB_KCHUNK)
            z = _dot_nt(q2, k_ref[0, pl.ds(off, SB_KCHUNK), :])
            log_keep = _log_keep(z)
            if causal is not None:
                log_keep = jnp.where(causal, log_keep, 0.0)
            z_scr[c] = z
            keep_scr[c] = log_keep
            return jnp.sum(log_keep, axis=1, keepdims=True)

        def live(state):
            t, run = state
            return jnp.logical_and(t < i, jnp.max(run) > SB_DEAD)

        walked, tot2 = lax.while_loop(live, lambda s: (s[0] + 1, s[1] + keep_sum(i - 1 - s[0], None)),
                                      (jnp.int32(0), keep_sum(i, _sb_diag_mask())))
        first = i - walked

        @pl.when(i == 0)
        def _():
            dk_acc[...] = jnp.zeros_like(dk_acc)
            dv_acc[...] = jnp.zeros_like(dv_acc)

        def chunk(c, carry, causal):
            dq, pre_keep, pre_e = carry
            off = pl.multiple_of(c * SB_KCHUNK, SB_KCHUNK)
            k_c = k_ref[0, pl.ds(off, SB_KCHUNK), :]
            v_c = v_ref[0, pl.ds(off, SB_KCHUNK), :]
            d_a = _dot_nt(do2, v_c)
            log_keep = keep_scr[c]
            log_beta = log_keep + z_scr[c]
            prefix, sums = _block_sums(log_keep, upto)
            inclusive = jnp.concatenate([prefix[0], prefix[1] + sums[0]], axis=1)
            a = jnp.exp(log_beta + ((tot2 - pre_keep) - inclusive))
            if causal is not None:
                a = jnp.where(causal, a, 0.0)
            e = d_a * a
            e_prefix, e_sums = _block_sums(e, earlier)
            before = jnp.concatenate([e_prefix[0] + pre_e, e_prefix[1] + (pre_e + e_sums[0])], axis=1)
            dz = e - (e + before) * jnp.exp(log_beta)
            if causal is not None:
                dz = jnp.where(causal, dz, 0.0)
            dz = dz.astype(BF16)
            dq = dq + _dot(dz, k_c)
            dk_acc[pl.ds(off, SB_KCHUNK), :] += _dot_tn(dz, q2)
            dv_acc[pl.ds(off, SB_KCHUNK), :] += _dot_tn(a.astype(BF16), do2)
            return dq, pre_keep + (sums[0] + sums[1]), pre_e + (e_sums[0] + e_sums[1])

        zero_col = jnp.zeros((SB_ROWS, 1), F32)
        carry = lax.fori_loop(first, i, lambda t, c: chunk(t, c, None),
                              (jnp.zeros((SB_ROWS, LANES), F32), zero_col, zero_col))
        dq, _, _ = chunk(i, carry, _sb_diag_mask())
        dq_ref[0] = (jnp.where(lane0, dq[:SB_QBLK], dq[SB_QBLK:]) * QK_SCALE).astype(BF16)

        @pl.when(i == N_SB_STEPS - 1)
        def _():
            dk_ref[0] = dk_acc[...].astype(BF16)
            dv_ref[0] = dv_acc[...].astype(BF16)

    blk = pl.BlockSpec((1, SB_QBLK, LANES), lambda b, h, i: (b, i, h))
    whole = lambda c: pl.BlockSpec((1, SEQ, LANES), lambda b, h, i: (b, 0, c * SB_PAIRS + h))
    out = jax.ShapeDtypeStruct((B_LOC, SEQ, SB_WIDTH), BF16)
    return _call(
        body, send, name="sb_bwd", grid=(B_LOC, SB_PAIRS, N_SB_STEPS),
        in_specs=[blk, whole(1), whole(2), blk],
        out_specs=[blk, whole(0), whole(0)],
        out_shape=[out, out, out],
        scratch_shapes=[pltpu.VMEM((SEQ, LANES), F32), pltpu.VMEM((SEQ, LANES), F32),
                        pltpu.VMEM((N_SB_STEPS, SB_ROWS, SB_KCHUNK), F32),
                        pltpu.VMEM((N_SB_STEPS, SB_ROWS, SB_KCHUNK), F32)],
        semantics=("parallel", "parallel", "arbitrary"), operands=(qkv, qkv, qkv, d_o))


DIL_GROUPS = len(DIL_PAIRS)
DIL_QBLOCKS = SEQ // BLK


def _residue_rows(j, dilation):
    length = SEQ // dilation
    return pl.ds(j, length, stride=dilation) if dilation > 1 else pl.ds(0, length)


def _gather_residues(src_ref, dst_ref, dst_off, dilation, scale=None):
    length = SEQ // dilation
    for j in range(dilation):
        v = src_ref[_residue_rows(j, dilation), :]
        if scale is not None:
            v = v * scale
        dst_ref[dst_off + j * length:dst_off + (j + 1) * length, :] = v.astype(dst_ref.dtype)


def _scatter_residues(src_ref, src_off, dst_ref, dilation):
    length = SEQ // dilation
    for j in range(dilation):
        dst_ref[_residue_rows(j, dilation), :] = (
            src_ref[src_off + j * length:src_off + (j + 1) * length, :].astype(dst_ref.dtype))


def _dil_geometry(group, pair):
    dilation = DIL_PAIRS[group][1]
    row = lax.broadcasted_iota(jnp.int32, (2 * BLK, 2 * BLK), 0)
    col = lax.broadcasted_iota(jnp.int32, (2 * BLK, 2 * BLK), 1)
    second = row >= BLK
    steps = BLK + jnp.where(second, row - BLK, row) - col
    coef = -ALIBI_MAX_BIAS / DIL_HEADS * math.log(2.0)
    first_head = float(4 * group + 1) + 2.0 * pair.astype(F32)
    slope = jnp.exp(coef * (first_head + jnp.where(second, 1.0, 0.0)))
    bias = slope * (steps * dilation).astype(F32)
    valid = jnp.logical_and(steps >= 0, steps <= BLK)
    return bias, valid, col >= BLK


def _dil_tile_scores(q2, kk, geometry, has_prev):
    bias, valid, own = geometry
    ok = jnp.logical_and(valid, jnp.logical_or(own, has_prev))
    return jnp.where(ok, _dot_nt(q2, kk) - bias, NEG_BIG)


def _head_col(v, lane_mask):
    return jnp.max(jnp.where(lane_mask, v, NEG_BIG), axis=1, keepdims=True)


def _dil_fwd(qkv, send=None):
    def body(*refs):
        ins, (o_ref, lse_ref), (qs, ks, vs, o_res, lse_res) = refs[:9], refs[9:11], refs[11:16]
        o_grp, lse_grp = refs[16:19], refs[19:22]
        pair = pl.program_id(1)
        lane0 = lax.broadcasted_iota(jnp.int32, (BLK, LANES), 1) < HEAD_DIM
        ks[0:BLK, :] = jnp.zeros((BLK, LANES), BF16)
        vs[0:BLK, :] = jnp.zeros((BLK, LANES), BF16)
        for grp, (_, dilation) in enumerate(DIL_PAIRS):
            q_ref, k_ref, v_ref = ins[3 * grp:3 * grp + 3]
            per_residue = DIL_QBLOCKS // dilation
            _gather_residues(q_ref, qs, 0, dilation, QK_SCALE)
            _gather_residues(k_ref, ks, BLK, dilation)
            _gather_residues(v_ref, vs, BLK, dilation)
            geometry = _dil_geometry(grp, pair)

            def step(blk, _):
                off = pl.multiple_of(blk * BLK, BLK)
                q2 = _stack_heads(qs[pl.ds(off, BLK), :], lane0)
                s = _dil_tile_scores(q2, ks[pl.ds(off, 2 * BLK), :], geometry, blk % per_residue != 0)
                m = jnp.max(s, axis=1, keepdims=True)
                p = jnp.exp(s - m)
                den = jnp.sum(p, axis=1, keepdims=True)
                out = _dot(p.astype(BF16), vs[pl.ds(off, 2 * BLK), :]) / den
                lse = m + jnp.log(den)
                o_res[pl.ds(off, BLK), :] = jnp.where(lane0, out[:BLK], out[BLK:])
                lse_res[pl.ds(off, BLK), :] = jnp.where(lane0, lse[:BLK], lse[BLK:])
                return 0

            lax.fori_loop(0, DIL_QBLOCKS, step, 0, unroll=8)
            _scatter_residues(o_res, 0, o_grp[grp], dilation)
            _scatter_residues(lse_res, 0, lse_grp[grp], dilation)

        for r0 in range(0, SEQ, 2 * BLK):
            rows = slice(r0, r0 + 2 * BLK)
            ls = [lse_grp[g][rows, :] for g in range(DIL_GROUPS)]
            m = jnp.maximum(jnp.maximum(ls[0], ls[1]), ls[2])
            w = [jnp.exp(l - m) for l in ls]
            den = w[0] + w[1] + w[2]
            o_ref[rows, :] = (w[0] * o_grp[0][rows, :] + w[1] * o_grp[1][rows, :] + w[2] * o_grp[2][rows, :]) / den
            lse_ref[rows, :] = m + jnp.log(den)

    def col(part, grp):
        return pl.BlockSpec((None, SEQ, LANES), lambda b, p: (b, 0, 6 * part + 2 * grp + p))

    out_spec = pl.BlockSpec((None, SEQ, LANES), lambda b, p: (b, 0, p))
    out = jax.ShapeDtypeStruct((B_LOC, SEQ, DIL_OUT), F32)
    return _call(
        body, send, name="dil_fwd", grid=(B_LOC, DIL_OUT // LANES),
        in_specs=[col(part, grp) for grp in range(DIL_GROUPS) for part in range(3)],
        out_specs=[out_spec, out_spec], out_shape=[out, out],
        scratch_shapes=[pltpu.VMEM((SEQ, LANES), BF16), pltpu.VMEM((SEQ + BLK, LANES), BF16),
                        pltpu.VMEM((SEQ + BLK, LANES), BF16), pltpu.VMEM((SEQ, LANES), F32),
                        pltpu.VMEM((SEQ, LANES), F32)] + [pltpu.VMEM((SEQ, LANES), F32)] * (2 * DIL_GROUPS),
        semantics=("parallel", "parallel"), operands=[qkv] * 9)


def _dil_bwd(qkv, d_o, lse, dsum, send=None):
    def body(*refs):
        ins, (do_ref, lse_ref, dsum_ref), outs = refs[:9], refs[9:12], refs[12:21]
        qs, ks, vs, dos, lse_res, dsum_res, dq_res, dk_acc, dv_acc = refs[21:]
        pair = pl.program_id(1)
        lane0 = lax.broadcasted_iota(jnp.int32, (BLK, LANES), 1) < HEAD_DIM
        lane1 = jnp.logical_not(lane0)
        ks[0:BLK, :] = jnp.zeros((BLK, LANES), BF16)
        vs[0:BLK, :] = jnp.zeros((BLK, LANES), BF16)
        for grp, (_, dilation) in enumerate(DIL_PAIRS):
            q_ref, k_ref, v_ref = ins[3 * grp:3 * grp + 3]
            dq_ref, dk_ref, dv_ref = outs[3 * grp:3 * grp + 3]
            per_residue = DIL_QBLOCKS // dilation
            _gather_residues(q_ref, qs, 0, dilation, QK_SCALE)
            _gather_residues(k_ref, ks, BLK, dilation)
            _gather_residues(v_ref, vs, BLK, dilation)
            _gather_residues(do_ref, dos, 0, dilation)
            _gather_residues(lse_ref, lse_res, 0, dilation)
            _gather_residues(dsum_ref, dsum_res, 0, dilation)
            dk_acc[...] = jnp.zeros_like(dk_acc)
            dv_acc[...] = jnp.zeros_like(dv_acc)
            geometry = _dil_geometry(grp, pair)

            def step(blk, _):
                off = pl.multiple_of(blk * BLK, BLK)
                q2 = _stack_heads(qs[pl.ds(off, BLK), :], lane0)
                do2 = _stack_heads(dos[pl.ds(off, BLK), :], lane0)
                kk = ks[pl.ds(off, 2 * BLK), :]
                vv = vs[pl.ds(off, 2 * BLK), :]
                lse_blk = lse_res[pl.ds(off, BLK), :]
                dsum_blk = dsum_res[pl.ds(off, BLK), :]
                lse2 = jnp.concatenate([_head_col(lse_blk, lane0), _head_col(lse_blk, lane1)], axis=0)
                dsum2 = jnp.concatenate([_head_col(dsum_blk, lane0), _head_col(dsum_blk, lane1)], axis=0)
                s = _dil_tile_scores(q2, kk, geometry, blk % per_residue != 0)
                p = jnp.exp(s - lse2)
                ds = (p * (_dot_nt(do2, vv) - dsum2)).astype(BF16)
                dq2 = _dot(ds, kk)
                dq_res[pl.ds(off, BLK), :] = jnp.where(lane0, dq2[:BLK], dq2[BLK:]) * QK_SCALE
                dk_acc[pl.ds(off, 2 * BLK), :] += _dot_tn(ds, q2)
                dv_acc[pl.ds(off, 2 * BLK), :] += _dot_tn(p.astype(BF16), do2)
                return 0

            lax.fori_loop(0, DIL_QBLOCKS, step, 0, unroll=8)
            _scatter_residues(dq_res, 0, dq_ref, dilation)
            _scatter_residues(dk_acc, BLK, dk_ref, dilation)
            _scatter_residues(dv_acc, BLK, dv_ref, dilation)

    def col(part, grp):
        return pl.BlockSpec((None, SEQ, LANES), lambda b, p: (b, 0, 6 * part + 2 * grp + p))

    slot = pl.BlockSpec((None, SEQ, LANES), lambda b, p: (b, 0, p))
    out = jax.ShapeDtypeStruct((B_LOC, SEQ, DIL_OUT), F32)
    return _call(
        body, send, name="dil_bwd", grid=(B_LOC, DIL_OUT // LANES),
        in_specs=[col(part, grp) for grp in range(DIL_GROUPS) for part in range(3)] + [slot] * 3,
        out_specs=[slot] * 9, out_shape=[out] * 9,
        scratch_shapes=[pltpu.VMEM((SEQ, LANES), BF16), pltpu.VMEM((SEQ + BLK, LANES), BF16),
                        pltpu.VMEM((SEQ + BLK, LANES), BF16), pltpu.VMEM((SEQ, LANES), BF16),
                        pltpu.VMEM((SEQ, LANES), F32), pltpu.VMEM((SEQ, LANES), F32),
                        pltpu.VMEM((SEQ, LANES), F32), pltpu.VMEM((SEQ + BLK, LANES), F32),
                        pltpu.VMEM((SEQ + BLK, LANES), F32)],
        semantics=("parallel", "parallel"), operands=[qkv] * 9 + [d_o, lse, dsum])


def _peers():
    x, y, c = lax.axis_index("x"), lax.axis_index("y"), lax.axis_index("c")
    me = 4 * x + 2 * y + c
    peers = []
    for mask in range(1, N_DEV):
        px = 1 - x if mask & 4 else x
        py = 1 - y if mask & 2 else y
        pc = 1 - c if mask & 1 else c
        peers.append(((px, py, pc), 4 * px + 2 * py + pc))
    return me, peers


def _all_gather(shard, name):
    def body(src_ref, out_ref, send_sems, recv_sems, local_sem):
        x, y, c = lax.axis_index("x"), lax.axis_index("y"), lax.axis_index("c")
        sibling = (x, y, 1 - c)
        chips = [(1 - x, y), (x, 1 - y), (1 - x, 1 - y)]

        def slot(px, py, pc):
            return out_ref.at[4 * px + 2 * py + pc]

        def copy(k, block, to, src=None):
            return pltpu.make_async_remote_copy(
                src_ref=slot(*block) if src is None else src, dst_ref=slot(*block),
                send_sem=send_sems.at[k], recv_sem=recv_sems.at[k], device_id=to,
                device_id_type=pl.DeviceIdType.MESH)

        mine = pltpu.make_async_copy(src_ref, slot(x, y, c), local_sem)
        mine.start()
        first = [copy(0, (x, y, c), sibling, src=src_ref)]
        first += [copy(1 + j, (x, y, c), (*chip, c), src=src_ref) for j, chip in enumerate(chips)]
        for cp in first:
            cp.start()
        passed = [copy(4 + j, (*chip, c), sibling) for j, chip in enumerate(chips)]
        for j, chip in enumerate(chips):
            copy(1 + j, (*chip, c), (x, y, c)).wait_recv()
            passed[j].start()
        copy(0, sibling, (x, y, c)).wait_recv()
        for j, chip in enumerate(chips):
            copy(4 + j, (*chip, 1 - c), (x, y, c)).wait_recv()
        for cp in first + passed:
            cp.wait_send()
        mine.wait()

    return pl.pallas_call(
        body, name=name,
        in_specs=[pl.BlockSpec(memory_space=pl.ANY)],
        out_specs=pl.BlockSpec(memory_space=pl.ANY),
        out_shape=jax.ShapeDtypeStruct((N_DEV,) + shard.shape, shard.dtype),
        scratch_shapes=[pltpu.SemaphoreType.DMA((N_DEV - 1,)), pltpu.SemaphoreType.DMA((N_DEV - 1,)),
                        pltpu.SemaphoreType.DMA],
    )(shard)


def _call(body, send, *, name, grid, in_specs, out_specs, out_shape, scratch_shapes, semantics, operands):
    if send is None:
        return pl.pallas_call(
            body, name=name, grid=grid, in_specs=in_specs, out_specs=out_specs, out_shape=out_shape,
            scratch_shapes=scratch_shapes, compiler_params=_params(semantics))(*operands), []
    srcs, kinds = [s for s, _ in send], [k for _, k in send]
    n, n_in, n_out, n_scr = len(srcs), len(in_specs), len(out_specs), len(scratch_shapes)
    steps = math.prod(grid)
    relay_step = (3 * steps) // 4

    def plan(refs):
        src_refs, land_refs = refs[n_in:n_in + n], refs[n_in + n + n_out:n_in + 2 * n + n_out]
        send_sems, recv_sems, local_sems = refs[-3:]
        x, y, c = lax.axis_index("x"), lax.axis_index("y"), lax.axis_index("c")
        me, peers = _peers()
        first, relayed_in, relayed_out, arrivals, sends, own = [], [], [], [], [], []
        for a, kind in enumerate(kinds):
            def copy(k, src, dst_slot, to):
                return pltpu.make_async_remote_copy(
                    src_ref=src, dst_ref=land_refs[a].at[dst_slot], send_sem=send_sems.at[a * (N_DEV - 1) + k],
                    recv_sem=recv_sems.at[a * (N_DEV - 1) + k], device_id=to, device_id_type=pl.DeviceIdType.MESH)

            if kind == "gather_by_chip":
                idx = lambda px, py, pc: 4 * px + 2 * py + pc
                chips = [(1 - x, y), (x, 1 - y), (1 - x, 1 - y)]
                mine = [copy(0, src_refs[a], me, (x, y, 1 - c))]
                arrivals.append(copy(0, src_refs[a], idx(x, y, 1 - c), (x, y, 1 - c)))
                for j, (px, py) in enumerate(chips):
                    mine.append(copy(1 + j, src_refs[a], me, (px, py, c)))
                    relayed_in.append(copy(1 + j, src_refs[a], idx(px, py, c), (px, py, c)))
                    relayed_out.append(copy(4 + j, land_refs[a].at[idx(px, py, c)], idx(px, py, c), (x, y, 1 - c)))
                    arrivals.append(copy(4 + j, src_refs[a], idx(px, py, 1 - c), (x, y, 1 - c)))
                first += mine
                sends += mine + relayed_out[-3:]
                own.append(pltpu.make_async_copy(src_refs[a], land_refs[a].at[me], local_sems.at[a]))
            elif kind == "scatter_by_chip":
                for k, (px, py) in enumerate([(1 - x, y), (x, 1 - y), (1 - x, 1 - y)]):
                    first.append(copy(k, src_refs[a].at[2 * px + py], 2 * x + y, (px, py, c)))
                    arrivals.append(copy(k, src_refs[a].at[2 * px + py], 2 * px + py, (px, py, c)))
                sends += first[-3:]
                own.append(pltpu.make_async_copy(src_refs[a].at[2 * x + y], land_refs[a].at[2 * x + y],
                                                 local_sems.at[a]))
            else:
                part = (lambda i: src_refs[a].at[i]) if kind == "scatter" else (lambda i: src_refs[a])
                for k, (peer, peer_idx) in enumerate(peers):
                    first.append(copy(k, part(peer_idx), me, peer))
                    arrivals.append(copy(k, part(peer_idx), peer_idx, peer))
                sends += first[-(N_DEV - 1):]
                own.append(pltpu.make_async_copy(part(me), land_refs[a].at[me], local_sems.at[a]))
        return first, relayed_in, relayed_out, arrivals, sends, own

    def wrapped(*refs):
        step = 0
        for axis, size in enumerate(grid):
            step = step * size + pl.program_id(axis)

        @pl.when(step == 0)
        def _():
            first, _, _, _, _, own = plan(refs)
            for cp in first + own:
                cp.start()

        if "gather_by_chip" in kinds:
            @pl.when(step == relay_step)
            def _():
                _, relayed_in, relayed_out, _, _, _ = plan(refs)
                for cp_in, cp_out in zip(relayed_in, relayed_out):
                    cp_in.wait_recv()
                    cp_out.start()

        body(*refs[:n_in], *refs[n_in + n:n_in + n + n_out], *refs[n_in + 2 * n + n_out:n_in + 2 * n + n_out + n_scr])

        @pl.when(step == steps - 1)
        def _():
            _, _, _, arrivals, sends, own = plan(refs)
            for cp in arrivals:
                cp.wait_recv()
            for cp in sends:
                cp.wait_send()
            for cp in own:
                cp.wait()

    anywhere = pl.BlockSpec(memory_space=pl.ANY)
    lands = [jax.ShapeDtypeStruct((N_DEV // 2 if k == "scatter_by_chip" else N_DEV,) + s.shape[-2:], s.dtype)
             for s, k in send]
    out = pl.pallas_call(
        wrapped, name=name, grid=grid,
        in_specs=list(in_specs) + [anywhere] * n, out_specs=list(out_specs) + [anywhere] * n,
        out_shape=list(out_shape) + lands,
        scratch_shapes=list(scratch_shapes) + [pltpu.SemaphoreType.DMA((n * (N_DEV - 1),)),
                                               pltpu.SemaphoreType.DMA((n * (N_DEV - 1),)),
                                               pltpu.SemaphoreType.DMA((n,))],
        compiler_params=_params(("arbitrary",) * len(grid)),
    )(*operands, *srcs)
    return out[:n_out], list(out[n_out:])


def _pair_swap(blocks):
    def body(src_ref, out_ref, send_sems, recv_sems):
        x, y, c = lax.axis_index("x"), lax.axis_index("y"), lax.axis_index("c")
        copies = [pltpu.make_async_remote_copy(
            src_ref=src_ref.at[2 * chip + (1 - c)], dst_ref=out_ref.at[chip], send_sem=send_sems.at[chip],
            recv_sem=recv_sems.at[chip], device_id=(x, y, 1 - c), device_id_type=pl.DeviceIdType.MESH)
            for chip in range(N_DEV // 2)]
        for cp in copies:
            cp.start()
        for cp in copies:
            cp.wait()

    return pl.pallas_call(
        body, name="pair_swap_grad_w_in",
        in_specs=[pl.BlockSpec(memory_space=pl.ANY)], out_specs=pl.BlockSpec(memory_space=pl.ANY),
        out_shape=jax.ShapeDtypeStruct((N_DEV // 2,) + blocks.shape[1:], blocks.dtype),
        scratch_shapes=[pltpu.SemaphoreType.DMA((N_DEV // 2,)), pltpu.SemaphoreType.DMA((N_DEV // 2,))],
    )(blocks)


def _pair_sum(blocks, swapped, core):
    _, rows, cols = swapped.shape
    tile_rows = _row_tile(rows)

    def body(core_ref, mine_ref, theirs_ref, o_ref):
        o_ref[...] = (mine_ref[...].astype(F32) + theirs_ref[...].astype(F32)).astype(o_ref.dtype)

    return pl.pallas_call(
        body, name="pair_sum_grad_w_in",
        grid_spec=pltpu.PrefetchScalarGridSpec(
            num_scalar_prefetch=1, grid=(N_DEV // 2, rows // tile_rows),
            in_specs=[pl.BlockSpec((None, tile_rows, cols), lambda j, i, core_ref: (2 * j + core_ref[0], i, 0)),
                      pl.BlockSpec((None, tile_rows, cols), lambda j, i, core_ref: (j, i, 0))],
            out_specs=pl.BlockSpec((None, tile_rows, cols), lambda j, i, core_ref: (j, i, 0))),
        out_shape=jax.ShapeDtypeStruct(swapped.shape, swapped.dtype),
        compiler_params=_params(("parallel", "parallel")),
    )(core, blocks, swapped)


def _sum_in_device_order(land_ref):
    acc = land_ref[0].astype(F32)
    for j in range(1, land_ref.shape[0]):
        acc = acc + land_ref[j].astype(F32)
    return acc


def _adam_math(w, g, m, v):
    c1 = 1.0 - ADAM_B1 ** ADAM_STEP
    c2 = 1.0 - ADAM_B2 ** ADAM_STEP
    m_new = ADAM_B1 * m + (1.0 - ADAM_B1) * g
    v_new = ADAM_B2 * v + (1.0 - ADAM_B2) * (g * g)
    delta = -ADAM_LR * ((m_new / c1) / (jnp.sqrt(v_new / c2) + ADAM_EPS) + ADAM_WD * w)
    return delta, m_new, v_new


def _row_tile(rows):
    return max(t for t in range(8, 257, 8) if rows % t == 0) if rows % 8 == 0 else rows


def _sum_update(land, w, m, v, name):
    slots, rows, cols = land.shape
    tile_rows = _row_tile(rows)

    def body(land_ref, w_ref, m_ref, v_ref, g_ref, d_ref, nm_ref, nv_ref):
        g = _sum_in_device_order(land_ref)
        g_ref[...] = g
        d_ref[...], nm_ref[...], nv_ref[...] = _adam_math(w_ref[...], g, m_ref[...], v_ref[...])

    tile = pl.BlockSpec((None, tile_rows, cols), lambda i: (0, i, 0))
    out = jax.ShapeDtypeStruct((1, rows, cols), F32)
    return pl.pallas_call(
        body, name=name, grid=(rows // tile_rows,),
        in_specs=[pl.BlockSpec((slots, tile_rows, cols), lambda i: (0, i, 0)), tile, tile, tile],
        out_specs=[tile] * 4, out_shape=[out] * 4,
        compiler_params=_params(("parallel",)),
    )(land, w, m, v)


def _sum_gains(land):
    def body(land_ref, o_ref):
        o_ref[...] = _sum_in_device_order(land_ref)

    return pl.pallas_call(
        body, name="sum_gain_grads", grid=(1,),
        in_specs=[pl.BlockSpec(land.shape, lambda i: (0, 0, 0))],
        out_specs=pl.BlockSpec(land.shape[1:], lambda i: (0, 0)),
        out_shape=jax.ShapeDtypeStruct(land.shape[1:], F32),
    )(land)


def _adamw(w, g, m, v, name):
    def body(w_ref, g_ref, m_ref, v_ref, d_ref, nm_ref, nv_ref):
        d_ref[...], nm_ref[...], nv_ref[...] = _adam_math(w_ref[...], g_ref[...], m_ref[...], v_ref[...])

    whole = pl.BlockSpec(w.shape, lambda i: (0, 0))
    out = jax.ShapeDtypeStruct(w.shape, F32)
    return pl.pallas_call(
        body, name=name, grid=(1,),
        in_specs=[whole] * 4, out_specs=[whole] * 3, out_shape=[out] * 3,
    )(w, g, m, v)


GROUP_FFN = ("w_ffn_in", "w_ffn_out")
GROUP_MIX = ("w_sb_up", "w_dil_up", "w_out")
COL_SHARDED = ("w_in", "w_sb_up", "w_dil_up", "w_ffn_in")


def _full_from_shards(name, slots):
    _, r, c = slots.shape
    if name in COL_SHARDED:
        return slots.transpose(1, 0, 2).reshape(r, N_DEV * c)
    return slots.reshape(N_DEV * r, c)


def _shards_from_full(name, full):
    rows, cols = full.shape
    if name in COL_SHARDED:
        return full.reshape(rows, N_DEV, cols // N_DEV).transpose(1, 0, 2)
    return full.reshape(N_DEV, rows // N_DEV, cols)


def _local_step(x, target, g_mix, g_ffn, g_fin, w_in, shards=None, rest=None):
    gather = lambda names, kind: None if shards is None else [(shards[n], kind) for n in names]
    scatter = lambda blocks: None if shards is None else [(t, "scatter") for t in blocks]
    landed = lambda blocks, lands: lands if lands else blocks

    w = {"w_in": w_in}
    if shards is None:
        w.update(rest)
        w["w_ffn_in"] = _shards_from_full("w_ffn_in", rest["w_ffn_in"])
    qkv_sb, qkv_dl, gates, u = _norm_proj(x, g_mix, w["w_in"])
    qkv_sb = qkv_sb.reshape(B_LOC, SEQ, 3 * SB_WIDTH)
    qkv_dl = qkv_dl.reshape(B_LOC, SEQ, 3 * DIL_WIDTH)
    (o_sb,), lands = _sb_fwd(qkv_sb, gather(("w_ffn_in",), "gather_by_chip"))
    if lands:
        w["w_ffn_in"] = lands[0]
    o_sb = o_sb.reshape(TOK, SB_WIDTH)
    with_dil = ("w_ffn_out",) + GROUP_MIX
    (o_dl, lse), lands = _dil_fwd(qkv_dl, gather(with_dil, "gather_by_chip"))
    w.update({n: _full_from_shards(n, t) for n, t in zip(with_dil, lands)})
    o_dl = o_dl.reshape(TOK, DIL_OUT)

    x1, merged = _mix_out(x, o_sb, o_dl, gates, w["w_sb_up"], w["w_dil_up"], w["w_out"])
    loss, dx1, u2, act, dh, dx2, dg_fin, dg_ffn = _ffn_fwd_bwd(x1, target, g_ffn, g_fin, w["w_ffn_in"], w["w_ffn_out"])
    dgates, dy_sb, dy_dl, do_sb, do_dl, dsum = _mix_bwd(dx1, o_sb, o_dl, gates, w["w_sb_up"], w["w_dil_up"], w["w_out"])
    blocks = {
        "w_sb_up": _atb(o_sb, dy_sb, "grad_w_sb_up", SB_WIDTH, D_MODEL, col_blocks=N_DEV),
        "w_dil_up": _atb(o_dl, dy_dl, "grad_w_dil_up", DIL_OUT, D_MODEL, col_blocks=N_DEV),
        "w_out": _shards_from_full("w_out", _atb(merged, dx1, "grad_w_out", D_MODEL, D_MODEL)),
        "w_ffn_in": _atb_shards(u2, dh, "grad_w_ffn_in"),
        "w_ffn_out": _shards_from_full("w_ffn_out", _atb_shards(act, dx2, "grad_w_ffn_out")),
    }
    grads = {}

    early, late = ("w_ffn_in",), ("w_ffn_out",) + GROUP_MIX
    early_blocks = [blocks[n] for n in early]
    (dq_sb, dk_sb, dv_sb), lands = _sb_bwd(qkv_sb, do_sb.reshape(B_LOC, SEQ, SB_WIDTH), scatter(early_blocks))
    grads.update(zip(early, landed(early_blocks, lands)))
    as_batch = lambda t: t.reshape(B_LOC, SEQ, DIL_OUT)
    late_blocks = [blocks[n] for n in late]
    d_dl, lands = _dil_bwd(qkv_dl, as_batch(do_dl), lse, as_batch(dsum), scatter(late_blocks))
    grads.update(zip(late, landed(late_blocks, lands)))
    flat = lambda t: t.reshape(TOK, -1)
    dproj = ([flat(dq_sb), flat(dk_sb), flat(dv_sb)]
             + [flat(d_dl[3 * grp + part]) for part in range(3) for grp in range(DIL_GROUPS)] + [dgates])

    w_in_blocks = _shards_from_full("w_in", _atb_pieces(u, dproj, "grad_w_in", D_MODEL // 2))
    if shards is None:
        grads["w_in"] = w_in_blocks
        send = None
    else:
        core = lax.axis_index("c").astype(jnp.int32).reshape(1)
        send = [(_pair_sum(w_in_blocks, _pair_swap(w_in_blocks), core), "scatter_by_chip")]
    (grad_x, dg_mix), lands = _proj_bwd(dproj, dx1, x, g_mix, w["w_in"], send)
    if lands:
        grads["w_in"] = lands[0]
    gain_grads = jnp.concatenate([dg_mix, dg_ffn, dg_fin], axis=0)
    return loss, grad_x, gain_grads, grads


def kernel(x, norm_mix_g, w_in, w_sb_up, w_dil_up, w_out, norm_ffn_g, w_ffn_in, w_ffn_out, norm_final_g, loss_target, m_norm_mix_g, m_w_in, m_w_sb_up, m_w_dil_up, m_w_out, m_norm_ffn_g, m_w_ffn_in, m_w_ffn_out, m_norm_final_g, v_norm_mix_g, v_w_in, v_w_sb_up, v_w_dil_up, v_w_out, v_norm_ffn_g, v_w_ffn_in, v_w_ffn_out, v_norm_final_g):
    mats = {"w_in": w_in, "w_sb_up": w_sb_up, "w_dil_up": w_dil_up, "w_out": w_out,
            "w_ffn_in": w_ffn_in, "w_ffn_out": w_ffn_out}
    moments_m = {"w_in": m_w_in, "w_sb_up": m_w_sb_up, "w_dil_up": m_w_dil_up, "w_out": m_w_out,
                 "w_ffn_in": m_w_ffn_in, "w_ffn_out": m_w_ffn_out}
    moments_v = {"w_in": v_w_in, "w_sb_up": v_w_sb_up, "w_dil_up": v_w_dil_up, "w_out": v_w_out,
                 "w_ffn_in": v_w_ffn_in, "w_ffn_out": v_w_ffn_out}
    gathered_w_in = _all_gather(w_in[0].astype(BF16), "all_gather_w_in")
    g_fin = norm_final_g.reshape(1, D_MODEL)
    loss, grad_x, gain_grads, grad_slots = _local_step(
        x.reshape(TOK, D_MODEL), loss_target.reshape(TOK, D_MODEL), norm_mix_g, norm_ffn_g, g_fin,
        _full_from_shards("w_in", gathered_w_in),
        shards={name: mats[name][0].astype(BF16) for name in GROUP_FFN + GROUP_MIX})

    gain_rows = jnp.concatenate([gain_grads, jnp.tile(loss, (1, D_MODEL // LANES)),
                                 jnp.zeros((8 - 4, D_MODEL), F32)], axis=0)
    g_gains = _sum_gains(_all_gather(gain_rows, "all_gather_gains"))

    out_g, out_d, out_m, out_v = {}, {}, {}, {}
    for name, slots in grad_slots.items():
        out_g[name], out_d[name], out_m[name], out_v[name] = _sum_update(
            slots, mats[name], moments_m[name], moments_v[name], "update_" + name)

    gain_w = jnp.concatenate([norm_mix_g, norm_ffn_g, g_fin], axis=0)
    gain_m = jnp.concatenate([m_norm_mix_g, m_norm_ffn_g, m_norm_final_g.reshape(1, D_MODEL)], axis=0)
    gain_v = jnp.concatenate([v_norm_mix_g, v_norm_ffn_g, v_norm_final_g.reshape(1, D_MODEL)], axis=0)
    gd, gm, gv = _adamw(gain_w, g_gains[:3], gain_m, gain_v, "adamw_gains")
    for idx, name in enumerate(("norm_mix_g", "norm_ffn_g", "norm_final_g")):
        shape = (D_MODEL,) if name == "norm_final_g" else (1, D_MODEL)
        out_g[name] = g_gains[idx].reshape(shape)
        out_d[name], out_m[name], out_v[name] = gd[idx].reshape(shape), gm[idx].reshape(shape), gv[idx].reshape(shape)

    order = ("norm_mix_g", "w_in", "w_sb_up", "w_dil_up", "w_out", "norm_ffn_g", "w_ffn_in", "w_ffn_out",
             "norm_final_g")
    return (g_gains[3, 0], grad_x.reshape(B_LOC, SEQ, D_MODEL),
            *[out_g[n] for n in order], *[out_d[n] for n in order],
            *[out_m[n] for n in order], *[out_v[n] for n in order])
```

```python
import math

import jax
import jax.numpy as jnp
from jax import lax
from jax.experimental import pallas as pl
from jax.experimental.pallas import tpu as pltpu

F32 = jnp.float32
BF16 = jnp.bfloat16

N_DEV = 8
D_MODEL = 1024
SEQ = 2048
B_LOC = 2
TOK = B_LOC * SEQ
HEAD_DIM = 64
SB_WIDTH = 512
DIL_WIDTH = 768
DIL_OUT = 256
QKV_WIDTH = 3 * SB_WIDTH + 3 * DIL_WIDTH
IN_WIDTH = QKV_WIDTH + 2 * D_MODEL
D_FF = 2816
DIL_PAIRS = ((128, 1), (512, 4), (2048, 16))
DIL_HEADS = 12
RMS_EPS = 1e-6
ALIBI_MAX_BIAS = 8.0
QK_SCALE = 1.0 / math.sqrt(HEAD_DIM)
BLK = 128
LANES = 128
NEG_BIG = -1e30

ADAM_LR = 0.001
ADAM_B1 = 0.9
ADAM_B2 = 0.999
ADAM_EPS = 1e-08
ADAM_WD = 0.01
ADAM_STEP = 10

VMEM_LIMIT = 56 * 1024 * 1024


def _dot(a, b):
    return jnp.dot(a, b, preferred_element_type=F32)


def _dot_nt(a, b):
    return lax.dot_general(a, b, (((1,), (1,)), ((), ())), preferred_element_type=F32)


def _dot_tn(a, b):
    return lax.dot_general(a, b, (((0,), (0,)), ((), ())), preferred_element_type=F32)


def _sigmoid(z):
    return 1.0 / (1.0 + jnp.exp(-z))


def _split_bf16(v):
    hi = v.astype(BF16)
    lo = (v - hi.astype(F32)).astype(BF16)
    return hi, lo


def _chunks(width, step=512):
    out, c = [], 0
    while c < width:
        w = min(step, width - c)
        out.append((c, w))
        c += w
    return out


def _resident(shape):
    nd = len(shape)
    return pl.BlockSpec(shape, lambda *_: (0,) * nd, pipeline_mode=pl.Buffered(1))


def _params(sem):
    return pltpu.CompilerParams(dimension_semantics=sem, vmem_limit_bytes=VMEM_LIMIT)


def _rms_fwd(x, g):
    r = lax.rsqrt(jnp.mean(x * x, axis=-1, keepdims=True) + RMS_EPS)
    n = x * r
    return n, r, n * g


def _rms_bwd(dy, n, r, g):
    dg = jnp.sum(dy * n, axis=0, keepdims=True)
    dn = dy * g
    dx = r * (dn - n * jnp.mean(dn * n, axis=-1, keepdims=True))
    return dx, dg


TM = 256


def _norm_proj(x, g, w_in, send=None):
    def body(x_ref, g_ref, w_ref, sb_ref, dl_ref, gate_ref, u_ref):
        _, _, u = _rms_fwd(x_ref[...], g_ref[...])
        u = u.astype(BF16)
        u_ref[...] = u
        for c0, w in _chunks(3 * SB_WIDTH):
            sb_ref[:, c0:c0 + w] = _dot(u, w_ref[:, c0:c0 + w]).astype(BF16)
        for c0, w in _chunks(3 * DIL_WIDTH):
            dl_ref[:, c0:c0 + w] = _dot(u, w_ref[:, 3 * SB_WIDTH + c0:3 * SB_WIDTH + c0 + w])
        for c0, w in _chunks(2 * D_MODEL):
            gate_ref[:, c0:c0 + w] = _dot(u, w_ref[:, QKV_WIDTH + c0:QKV_WIDTH + c0 + w])

    return _call(
        body, send, name="norm_proj", grid=(TOK // TM,),
        in_specs=[pl.BlockSpec((TM, D_MODEL), lambda i: (i, 0)), _resident((1, D_MODEL)),
                  _resident((D_MODEL, IN_WIDTH))],
        out_specs=[pl.BlockSpec((TM, 3 * SB_WIDTH), lambda i: (i, 0)),
                   pl.BlockSpec((TM, 3 * DIL_WIDTH), lambda i: (i, 0)),
                   pl.BlockSpec((TM, 2 * D_MODEL), lambda i: (i, 0)),
                   pl.BlockSpec((TM, D_MODEL), lambda i: (i, 0))],
        out_shape=[jax.ShapeDtypeStruct((TOK, 3 * SB_WIDTH), BF16),
                   jax.ShapeDtypeStruct((TOK, 3 * DIL_WIDTH), F32),
                   jax.ShapeDtypeStruct((TOK, 2 * D_MODEL), F32),
                   jax.ShapeDtypeStruct((TOK, D_MODEL), BF16)],
        scratch_shapes=[], semantics=("parallel",), operands=(x, g, w_in))


def _mix_out(x, o_sb, o_dl, gates, w_sb_up, w_dil_up, w_out):
    def body(x_ref, osb_ref, odl_ref, gate_ref, wsb_ref, wdl_ref, wout_ref, x1_ref, mg_ref):
        y_sb = _dot(osb_ref[...], wsb_ref[...])
        y_dl = _dot(odl_ref[...].astype(BF16), wdl_ref[...])
        merged = (_sigmoid(gate_ref[:, :D_MODEL]) * y_sb
                  + _sigmoid(gate_ref[:, D_MODEL:]) * y_dl).astype(BF16)
        mg_ref[...] = merged
        x1_ref[...] = x_ref[...] + _dot(merged, wout_ref[...])

    return pl.pallas_call(
        body, name="mix_out", grid=(TOK // TM,),
        in_specs=[pl.BlockSpec((TM, D_MODEL), lambda i: (i, 0)),
                  pl.BlockSpec((TM, SB_WIDTH), lambda i: (i, 0)),
                  pl.BlockSpec((TM, DIL_OUT), lambda i: (i, 0)),
                  pl.BlockSpec((TM, 2 * D_MODEL), lambda i: (i, 0)),
                  _resident((SB_WIDTH, D_MODEL)), _resident((DIL_OUT, D_MODEL)),
                  _resident((D_MODEL, D_MODEL))],
        out_specs=[pl.BlockSpec((TM, D_MODEL), lambda i: (i, 0)),
                   pl.BlockSpec((TM, D_MODEL), lambda i: (i, 0))],
        out_shape=[jax.ShapeDtypeStruct((TOK, D_MODEL), F32),
                   jax.ShapeDtypeStruct((TOK, D_MODEL), BF16)],
        compiler_params=_params(("parallel",)),
    )(x, o_sb, o_dl, gates, w_sb_up, w_dil_up, w_out)


FF_SHARD = 2 * D_FF // N_DEV
FF_PAIRS = N_DEV // 2


def _ffn_fwd_bwd(x1, target, g_ffn, g_fin, w_ffn_in, w_ffn_out):
    def body(x1_ref, t_ref, gffn_ref, gfin_ref, win_ref, wout_ref,
             loss_ref, dx1_ref, u2_ref, act_ref, dh_ref, dx2_ref, dgfin_ref, dgffn_ref, h_scr):
        i = pl.program_id(0)

        @pl.when(i == 0)
        def _():
            loss_ref[...] = jnp.zeros_like(loss_ref)
            dgfin_ref[...] = jnp.zeros_like(dgfin_ref)
            dgffn_ref[...] = jnp.zeros_like(dgffn_ref)

        x1 = x1_ref[...]
        g_ffn_v = gffn_ref[...]
        g_fin_v = gfin_ref[...]
        n2, r2, u2 = _rms_fwd(x1, g_ffn_v)
        u2 = u2.astype(BF16)
        u2_ref[...] = u2
        x2 = x1
        for r in range(FF_PAIRS):
            gate = _dot(u2, win_ref[r])
            up = _dot(u2, win_ref[r + FF_PAIRS])
            h_scr[r] = gate
            h_scr[r + FF_PAIRS] = up
            act = (gate * _sigmoid(gate) * up).astype(BF16)
            act_ref[r] = act
            x2 = x2 + _dot(act, wout_ref[r * FF_SHARD:(r + 1) * FF_SHARD, :])
        n3, r3, y = _rms_fwd(x2, g_fin_v)
        err = y - t_ref[...]
        sq = jnp.sum(jnp.sum(err * err, axis=1, keepdims=True), axis=0, keepdims=True)
        loss_ref[...] += sq * (0.5 / D_MODEL)
        dx2, dgfin = _rms_bwd(err * (1.0 / D_MODEL), n3, r3, g_fin_v)
        dgfin_ref[...] += dgfin
        dx2_b = dx2.astype(BF16)
        dx2_ref[...] = dx2_b
        du2 = jnp.zeros((TM, D_MODEL), F32)
        for r in range(FF_PAIRS):
            gate = h_scr[r]
            up = h_scr[r + FF_PAIRS]
            dact = _dot_nt(dx2_b, wout_ref[r * FF_SHARD:(r + 1) * FF_SHARD, :])
            sg = _sigmoid(gate)
            dgate = (dact * up * (sg * (1.0 + gate * (1.0 - sg)))).astype(BF16)
            dup = (dact * (gate * sg)).astype(BF16)
            dh_ref[r] = dgate
            dh_ref[r + FF_PAIRS] = dup
            du2 = du2 + _dot_nt(dgate, win_ref[r])
            du2 = du2 + _dot_nt(dup, win_ref[r + FF_PAIRS])
        dx1_n, dgffn = _rms_bwd(du2, n2, r2, g_ffn_v)
        dgffn_ref[...] += dgffn
        dx1_ref[...] = dx2 + dx1_n

    tile = lambda w: pl.BlockSpec((TM, w), lambda i: (i, 0))
    shards = lambda n: pl.BlockSpec((n, TM, FF_SHARD), lambda i: (0, i, 0))
    acc = lambda w: pl.BlockSpec((1, w), lambda i: (0, 0))
    return pl.pallas_call(
        body, name="ffn_fwd_bwd", grid=(TOK // TM,),
        in_specs=[tile(D_MODEL), tile(D_MODEL), _resident((1, D_MODEL)), _resident((1, D_MODEL)),
                  _resident((N_DEV, D_MODEL, FF_SHARD)), _resident((D_FF, D_MODEL))],
        out_specs=[acc(LANES), tile(D_MODEL), tile(D_MODEL), shards(FF_PAIRS), shards(N_DEV), tile(D_MODEL),
                   acc(D_MODEL), acc(D_MODEL)],
        out_shape=[jax.ShapeDtypeStruct((1, LANES), F32),
                   jax.ShapeDtypeStruct((TOK, D_MODEL), F32),
                   jax.ShapeDtypeStruct((TOK, D_MODEL), BF16),
                   jax.ShapeDtypeStruct((FF_PAIRS, TOK, FF_SHARD), BF16),
                   jax.ShapeDtypeStruct((N_DEV, TOK, FF_SHARD), BF16),
                   jax.ShapeDtypeStruct((TOK, D_MODEL), BF16),
                   jax.ShapeDtypeStruct((1, D_MODEL), F32),
                   jax.ShapeDtypeStruct((1, D_MODEL), F32)],
        scratch_shapes=[pltpu.VMEM((N_DEV, TM, FF_SHARD), F32)],
        compiler_params=_params(("arbitrary",)),
    )(x1, target, g_ffn, g_fin, w_ffn_in, w_ffn_out)


def _mix_bwd(dx1, o_sb, o_dl, gates, w_sb_up, w_dil_up, w_out):
    def body(dx1_ref, osb_ref, odl_ref, gate_ref, wsb_ref, wdl_ref, wout_ref,
             dgate_ref, dysb_ref, dydl_ref, dosb_ref, dodl_ref, dsum_ref):
        dmerged = _dot_nt(dx1_ref[...].astype(BF16), wout_ref[...])
        o_dl = odl_ref[...]
        y_sb = _dot(osb_ref[...], wsb_ref[...])
        y_dl = _dot(o_dl.astype(BF16), wdl_ref[...])
        s_sb = _sigmoid(gate_ref[:, :D_MODEL])
        s_dl = _sigmoid(gate_ref[:, D_MODEL:])
        dgate_ref[:, :D_MODEL] = (dmerged * y_sb * (s_sb * (1.0 - s_sb))).astype(BF16)
        dgate_ref[:, D_MODEL:] = (dmerged * y_dl * (s_dl * (1.0 - s_dl))).astype(BF16)
        dy_sb = (dmerged * s_sb).astype(BF16)
        dy_dl = (dmerged * s_dl).astype(BF16)
        dysb_ref[...] = dy_sb
        dydl_ref[...] = dy_dl
        dosb_ref[...] = _dot_nt(dy_sb, wsb_ref[...]).astype(BF16)
        do_dl = _dot_nt(dy_dl, wdl_ref[...])
        dodl_ref[...] = do_dl
        row = lax.broadcasted_iota(jnp.int32, (DIL_OUT, DIL_OUT), 0) // HEAD_DIM
        col = lax.broadcasted_iota(jnp.int32, (DIL_OUT, DIL_OUT), 1) // HEAD_DIM
        same_head = (row == col).astype(BF16)
        hi, lo = _split_bf16(do_dl * o_dl)
        dsum_ref[...] = _dot(hi, same_head) + _dot(lo, same_head)

    tile = lambda w: pl.BlockSpec((TM, w), lambda i: (i, 0))
    return pl.pallas_call(
        body, name="mix_bwd", grid=(TOK // TM,),
        in_specs=[tile(D_MODEL), tile(SB_WIDTH), tile(DIL_OUT), tile(2 * D_MODEL),
                  _resident((SB_WIDTH, D_MODEL)), _resident((DIL_OUT, D_MODEL)),
                  _resident((D_MODEL, D_MODEL))],
        out_specs=[tile(2 * D_MODEL), tile(D_MODEL), tile(D_MODEL), tile(SB_WIDTH), tile(DIL_OUT),
                   tile(DIL_OUT)],
        out_shape=[jax.ShapeDtypeStruct((TOK, 2 * D_MODEL), BF16),
                   jax.ShapeDtypeStruct((TOK, D_MODEL), BF16),
                   jax.ShapeDtypeStruct((TOK, D_MODEL), BF16),
                   jax.ShapeDtypeStruct((TOK, SB_WIDTH), BF16),
                   jax.ShapeDtypeStruct((TOK, DIL_OUT), F32),
                   jax.ShapeDtypeStruct((TOK, DIL_OUT), F32)],
        compiler_params=_params(("parallel",)),
    )(dx1, o_sb, o_dl, gates, w_sb_up, w_dil_up, w_out)


def _proj_bwd(dproj, dx1, x, g, w_in, send=None):
    widths = [p.shape[1] for p in dproj]

    def body(*refs):
        dx1_ref, x_ref, g_ref, w_ref, dx_ref, dg_ref = refs[len(widths):]

        @pl.when(pl.program_id(0) == 0)
        def _():
            dg_ref[...] = jnp.zeros_like(dg_ref)

        du = jnp.zeros((TM, D_MODEL), F32)
        c0 = 0
        for dp_ref, w in zip(refs, widths):
            du = du + _dot_nt(dp_ref[...].astype(BF16), w_ref[:, c0:c0 + w])
            c0 += w
        g_v = g_ref[...]
        n, r, _ = _rms_fwd(x_ref[...], g_v)
        dx, dg = _rms_bwd(du, n, r, g_v)
        dg_ref[...] += dg
        dx_ref[...] = dx1_ref[...] + dx

    tile = lambda w: pl.BlockSpec((TM, w), lambda i: (i, 0))
    return _call(
        body, send, name="proj_bwd", grid=(TOK // TM,),
        in_specs=[tile(w) for w in widths] + [tile(D_MODEL), tile(D_MODEL), _resident((1, D_MODEL)),
                                              _resident((D_MODEL, IN_WIDTH))],
        out_specs=[tile(D_MODEL), pl.BlockSpec((1, D_MODEL), lambda i: (0, 0))],
        out_shape=[jax.ShapeDtypeStruct((TOK, D_MODEL), F32),
                   jax.ShapeDtypeStruct((1, D_MODEL), F32)],
        scratch_shapes=[], semantics=("arbitrary",), operands=(*dproj, dx1, x, g, w_in))


def _atb_pieces(a, pieces, name, tm, tk=512):
    m = a.shape[1]
    widths = [p.shape[1] for p in pieces]
    n = sum(widths)
    nk = TOK // tk

    def body(a_ref, *refs):
        o_ref, acc_ref = refs[len(widths):]
        k = pl.program_id(1)

        @pl.when(k == 0)
        def _():
            acc_ref[...] = jnp.zeros_like(acc_ref)

        a_v = a_ref[...]
        c0 = 0
        for p_ref, w in zip(refs, widths):
            acc_ref[:, c0:c0 + w] += _dot_tn(a_v, p_ref[...].astype(BF16))
            c0 += w

        @pl.when(k == nk - 1)
        def _():
            o_ref[...] = acc_ref[...].astype(BF16)

    return pl.pallas_call(
        body, name=name, grid=(m // tm, nk),
        in_specs=[pl.BlockSpec((tk, tm), lambda i, k: (k, i))]
                 + [pl.BlockSpec((tk, w), lambda i, k: (k, 0)) for w in widths],
        out_specs=pl.BlockSpec((tm, n), lambda i, k: (i, 0)),
        out_shape=jax.ShapeDtypeStruct((m, n), BF16),
        scratch_shapes=[pltpu.VMEM((tm, n), F32)],
        compiler_params=_params(("parallel", "arbitrary")),
    )(a, *pieces)


def _atb_shards(a, b, name, tk=512):
    a_sharded = a.ndim == 3
    n, _, w = a.shape if a_sharded else b.shape
    other = (b if a_sharded else a).shape[1]

    def body(a_ref, b_ref, o_ref):
        acc = jnp.zeros(o_ref.shape, F32)
        for k0 in range(0, TOK, tk):
            acc = acc + _dot_tn(a_ref[k0:k0 + tk, :], b_ref[k0:k0 + tk, :])
        o_ref[...] = acc.astype(BF16)

    shard = pl.BlockSpec((None, TOK, w), lambda r: (r, 0, 0))
    whole = _resident((TOK, other))
    if a_sharded:
        out_spec, out_shape = pl.BlockSpec((w, other), lambda r: (r, 0)), (n * w, other)
    else:
        out_spec, out_shape = pl.BlockSpec((None, other, w), lambda r: (r, 0, 0)), (n, other, w)
    return pl.pallas_call(
        body, name=name, grid=(n,),
        in_specs=[shard, whole] if a_sharded else [whole, shard],
        out_specs=out_spec, out_shape=jax.ShapeDtypeStruct(out_shape, BF16),
        compiler_params=_params(("parallel",)),
    )(a, b)


def _atb(a, b, name, tm, tn, col_blocks=0, tk=512):
    m, n = a.shape[1], b.shape[1]
    nk = TOK // tk

    def body(a_ref, b_ref, o_ref, acc_ref):
        k = pl.program_id(2)

        @pl.when(k == 0)
        def _():
            acc_ref[...] = jnp.zeros_like(acc_ref)

        acc_ref[...] += _dot_tn(a_ref[...].astype(BF16), b_ref[...].astype(BF16))

        @pl.when(k == nk - 1)
        def _():
            if col_blocks:
                width = n // col_blocks
                for blk in range(col_blocks):
                    o_ref[blk] = acc_ref[:, blk * width:(blk + 1) * width].astype(BF16)
            else:
                o_ref[...] = acc_ref[...].astype(BF16)

    if col_blocks:
        out_spec = pl.BlockSpec((col_blocks, tm, n // col_blocks), lambda i, j, k: (0, i, 0))
        out_shape = jax.ShapeDtypeStruct((col_blocks, m, n // col_blocks), BF16)
    else:
        out_spec = pl.BlockSpec((tm, tn), lambda i, j, k: (i, j))
        out_shape = jax.ShapeDtypeStruct((m, n), BF16)
    return pl.pallas_call(
        body, name=name, grid=(m // tm, n // tn, nk),
        in_specs=[pl.BlockSpec((tk, tm), lambda i, j, k: (k, i)),
                  pl.BlockSpec((tk, tn), lambda i, j, k: (k, j))],
        out_specs=out_spec, out_shape=out_shape,
        scratch_shapes=[pltpu.VMEM((tm, tn), F32)],
        compiler_params=_params(("parallel", "parallel", "arbitrary")),
    )(a, b)


SB_PAIRS = SB_WIDTH // LANES


def _two_heads(v, lane0):
    zero = jnp.zeros_like(v)
    return jnp.where(lane0, v, zero), jnp.where(lane0, zero, v)


SB_QBLK = 256
N_SB_STEPS = SEQ // SB_QBLK


SB_KCHUNK = 2 * BLK
SB_ROWS = 2 * SB_QBLK
SB_DEAD = -104.0


def _log_keep(z):
    neg_z = -z
    return jnp.minimum(neg_z, 0.0) - jnp.log(1.0 + jnp.exp(jnp.minimum(z, neg_z)))


def _stack_heads(v, lane0):
    return jnp.concatenate(_two_heads(v, lane0), axis=0)


def _block_sums(v, tri):
    halves = (v[:, :BLK], v[:, BLK:])
    hi, lo = _split_bf16(jnp.concatenate(halves, axis=0))
    prod = _dot(jnp.concatenate([hi, lo], axis=0), tri)
    tri_sum = prod[:2 * SB_ROWS] + prod[2 * SB_ROWS:]
    sums = tuple(jnp.sum(h, axis=1, keepdims=True) for h in halves)
    return (tri_sum[:SB_ROWS], tri_sum[SB_ROWS:]), sums


def _sb_diag_mask():
    row = lax.broadcasted_iota(jnp.int32, (SB_ROWS, SB_KCHUNK), 0)
    col = lax.broadcasted_iota(jnp.int32, (SB_ROWS, SB_KCHUNK), 1)
    return col < jnp.where(row >= SB_QBLK, row - SB_QBLK, row)


def _sb_fwd(qkv, send=None):
    def body(q_ref, k_ref, v_ref, o_ref):
        i = pl.program_id(2)
        krow = lax.broadcasted_iota(jnp.int32, (BLK, BLK), 0)
        kcol = lax.broadcasted_iota(jnp.int32, (BLK, BLK), 1)
        later = (krow > kcol).astype(BF16)
        lane0 = lax.broadcasted_iota(jnp.int32, (SB_QBLK, LANES), 1) < HEAD_DIM
        q2 = _stack_heads(q_ref[0] * QK_SCALE, lane0)

        def chunk(c, carry, causal):
            acc, run = carry
            off = pl.multiple_of(c * SB_KCHUNK, SB_KCHUNK)
            z = _dot_nt(q2, k_ref[0, pl.ds(off, SB_KCHUNK), :])
            log_keep = _log_keep(z)
            if causal is not None:
                log_keep = jnp.where(causal, log_keep, 0.0)
            suffix, sums = _block_sums(log_keep, later)
            log_after = jnp.concatenate([suffix[0] + (run + sums[1]), suffix[1] + run], axis=1)
            a = jnp.exp(log_keep + z + log_after)
            if causal is not None:
                a = jnp.where(causal, a, 0.0)
            acc = acc + _dot(a.astype(BF16), v_ref[0, pl.ds(off, SB_KCHUNK), :])
            return acc, run + (sums[0] + sums[1])

        acc, run = chunk(i, (jnp.zeros((SB_ROWS, LANES), F32), jnp.zeros((SB_ROWS, 1), F32)), _sb_diag_mask())

        def live(state):
            t, _, run = state
            return jnp.logical_and(t < i, jnp.max(run) > SB_DEAD)

        def trip(state):
            t, acc, run = state
            acc, run = chunk(i - 1 - t, (acc, run), None)
            return t + 1, acc, run

        _, acc, _ = lax.while_loop(live, trip, (jnp.int32(0), acc, run))
        o_ref[0] = jnp.where(lane0, acc[:SB_QBLK], acc[SB_QBLK:]).astype(BF16)

    blk = pl.BlockSpec((1, SB_QBLK, LANES), lambda b, h, i: (b, i, h))
    return _call(
        body, send, name="sb_fwd", grid=(B_LOC, SB_PAIRS, N_SB_STEPS),
        in_specs=[blk,
                  pl.BlockSpec((1, SEQ, LANES), lambda b, h, i: (b, 0, SB_PAIRS + h)),
                  pl.BlockSpec((1, SEQ, LANES), lambda b, h, i: (b, 0, 2 * SB_PAIRS + h))],
        out_specs=[blk], out_shape=[jax.ShapeDtypeStruct((B_LOC, SEQ, SB_WIDTH), BF16)],
        scratch_shapes=[], semantics=("parallel", "parallel", "arbitrary"), operands=(qkv, qkv, qkv))


def _sb_bwd(qkv, d_o, send=None):
    def body(q_ref, k_ref, v_ref, do_ref, dq_ref, dk_ref, dv_ref, dk_acc, dv_acc, z_scr, keep_scr):
        i = pl.program_id(2)
        krow = lax.broadcasted_iota(jnp.int32, (BLK, BLK), 0)
        kcol = lax.broadcasted_iota(jnp.int32, (BLK, BLK), 1)
        upto = (krow <= kcol).astype(BF16)
        earlier = (krow < kcol).astype(BF16)
        lane0 = lax.broadcasted_iota(jnp.int32, (SB_QBLK, LANES), 1) < HEAD_DIM
        q2 = _stack_heads(q_ref[0] * QK_SCALE, lane0)
        do2 = _stack_heads(do_ref[0], lane0)

        def keep_sum(c, causal):
            off = pl.multiple_of(c * SB_KCHUNK, SB_KCHUNK)
            z = _dot_nt(q2, k_ref[0, pl.ds(off, SB_KCHUNK), :])
            log_keep = _log_keep(z)
            if causal is not None:
                log_keep = jnp.where(causal, log_keep, 0.0)
            z_scr[c] = z
            keep_scr[c] = log_keep
            return jnp.sum(log_keep, axis=1, keepdims=True)

        def live(state):
            t, run = state
            return jnp.logical_and(t < i, jnp.max(run) > SB_DEAD)

        walked, tot2 = lax.while_loop(live, lambda s: (s[0] + 1, s[1] + keep_sum(i - 1 - s[0], None)),
                                      (jnp.int32(0), keep_sum(i, _sb_diag_mask())))
        first = i - walked

        @pl.when(i == 0)
        def _():
            dk_acc[...] = jnp.zeros_like(dk_acc)
            dv_acc[...] = jnp.zeros_like(dv_acc)

        def chunk(c, carry, causal):
            dq, pre_keep, pre_e = carry
            off = pl.multiple_of(c * SB_KCHUNK, SB_KCHUNK)
            k_c = k_ref[0, pl.ds(off, SB_KCHUNK), :]
            v_c = v_ref[0, pl.ds(off, SB_KCHUNK), :]
            d_a = _dot_nt(do2, v_c)
            log_keep = keep_scr[c]
            log_beta = log_keep + z_scr[c]
            prefix, sums = _block_sums(log_keep, upto)
            inclusive = jnp.concatenate([prefix[0], prefix[1] + sums[0]], axis=1)
            a = jnp.exp(log_beta + ((tot2 - pre_keep) - inclusive))
            if causal is not None:
                a = jnp.where(causal, a, 0.0)
            e = d_a * a
            e_prefix, e_sums = _block_sums(e, earlier)
            before = jnp.concatenate([e_prefix[0] + pre_e, e_prefix[1] + (pre_e + e_sums[0])], axis=1)
            dz = e - (e + before) * jnp.exp(log_beta)
            if causal is not None:
                dz = jnp.where(causal, dz, 0.0)
            dz = dz.astype(BF16)
            dq = dq + _dot(dz, k_c)
            dk_acc[pl.ds(off, SB_KCHUNK), :] += _dot_tn(dz, q2)
            dv_acc[pl.ds(off, SB_KCHUNK), :] += _dot_tn(a.astype(BF16), do2)
            return dq, pre_keep + (sums[0] + sums[1]), pre_e + (e_sums[0] + e_sums[1])

        zero_col = jnp.zeros((SB_ROWS, 1), F32)
        carry = lax.fori_loop(first, i, lambda t, c: chunk(t, c, None),
                              (jnp.zeros((SB_ROWS, LANES), F32), zero_col, zero_col))
        dq, _, _ = chunk(i, carry, _sb_diag_mask())
        dq_ref[0] = (jnp.where(lane0, dq[:SB_QBLK], dq[SB_QBLK:]) * QK_SCALE).astype(BF16)

        @pl.when(i == N_SB_STEPS - 1)
        def _():
            dk_ref[0] = dk_acc[...].astype(BF16)
            dv_ref[0] = dv_acc[...].astype(BF16)

    blk = pl.BlockSpec((1, SB_QBLK, LANES), lambda b, h, i: (b, i, h))
    whole = lambda c: pl.BlockSpec((1, SEQ, LANES), lambda b, h, i: (b, 0, c * SB_PAIRS + h))
    out = jax.ShapeDtypeStruct((B_LOC, SEQ, SB_WIDTH), BF16)
    return _call(
        body, send, name="sb_bwd", grid=(B_LOC, SB_PAIRS, N_SB_STEPS),
        in_specs=[blk, whole(1), whole(2), blk],
        out_specs=[blk, whole(0), whole(0)],
        out_shape=[out, out, out],
        scratch_shapes=[pltpu.VMEM((SEQ, LANES), F32), pltpu.VMEM((SEQ, LANES), F32),
                        pltpu.VMEM((N_SB_STEPS, SB_ROWS, SB_KCHUNK), F32),
                        pltpu.VMEM((N_SB_STEPS, SB_ROWS, SB_KCHUNK), F32)],
        semantics=("parallel", "parallel", "arbitrary"), operands=(qkv, qkv, qkv, d_o))


DIL_GROUPS = len(DIL_PAIRS)
DIL_QBLOCKS = SEQ // BLK


def _residue_rows(j, dilation):
    length = SEQ // dilation
    return pl.ds(j, length, stride=dilation) if dilation > 1 else pl.ds(0, length)


def _gather_residues(src_ref, dst_ref, dst_off, dilation, scale=None):
    length = SEQ // dilation
    for j in range(dilation):
        v = src_ref[_residue_rows(j, dilation), :]
        if scale is not None:
            v = v * scale
        dst_ref[dst_off + j * length:dst_off + (j + 1) * length, :] = v.astype(dst_ref.dtype)


def _scatter_residues(src_ref, src_off, dst_ref, dilation):
    length = SEQ // dilation
    for j in range(dilation):
        dst_ref[_residue_rows(j, dilation), :] = (
            src_ref[src_off + j * length:src_off + (j + 1) * length, :].astype(dst_ref.dtype))


def _dil_geometry(group, pair):
    dilation = DIL_PAIRS[group][1]
    row = lax.broadcasted_iota(jnp.int32, (2 * BLK, 2 * BLK), 0)
    col = lax.broadcasted_iota(jnp.int32, (2 * BLK, 2 * BLK), 1)
    second = row >= BLK
    steps = BLK + jnp.where(second, row - BLK, row) - col
    coef = -ALIBI_MAX_BIAS / DIL_HEADS * math.log(2.0)
    first_head = float(4 * group + 1) + 2.0 * pair.astype(F32)
    slope = jnp.exp(coef * (first_head + jnp.where(second, 1.0, 0.0)))
    bias = slope * (steps * dilation).astype(F32)
    valid = jnp.logical_and(steps >= 0, steps <= BLK)
    return bias, valid, col >= BLK


def _dil_tile_scores(q2, kk, geometry, has_prev):
    bias, valid, own = geometry
    ok = jnp.logical_and(valid, jnp.logical_or(own, has_prev))
    return jnp.where(ok, _dot_nt(q2, kk) - bias, NEG_BIG)


def _head_col(v, lane_mask):
    return jnp.max(jnp.where(lane_mask, v, NEG_BIG), axis=1, keepdims=True)


def _dil_fwd(qkv, send=None):
    def body(*refs):
        ins, (o_ref, lse_ref), (qs, ks, vs, o_res, lse_res) = refs[:9], refs[9:11], refs[11:16]
        o_grp, lse_grp = refs[16:19], refs[19:22]
        pair = pl.program_id(1)
        lane0 = lax.broadcasted_iota(jnp.int32, (BLK, LANES), 1) < HEAD_DIM
        ks[0:BLK, :] = jnp.zeros((BLK, LANES), BF16)
        vs[0:BLK, :] = jnp.zeros((BLK, LANES), BF16)
        for grp, (_, dilation) in enumerate(DIL_PAIRS):
            q_ref, k_ref, v_ref = ins[3 * grp:3 * grp + 3]
            per_residue = DIL_QBLOCKS // dilation
            _gather_residues(q_ref, qs, 0, dilation, QK_SCALE)
            _gather_residues(k_ref, ks, BLK, dilation)
            _gather_residues(v_ref, vs, BLK, dilation)
            geometry = _dil_geometry(grp, pair)

            def step(blk, _):
                off = pl.multiple_of(blk * BLK, BLK)
                q2 = _stack_heads(qs[pl.ds(off, BLK), :], lane0)
                s = _dil_tile_scores(q2, ks[pl.ds(off, 2 * BLK), :], geometry, blk % per_residue != 0)
                m = jnp.max(s, axis=1, keepdims=True)
                p = jnp.exp(s - m)
                den = jnp.sum(p, axis=1, keepdims=True)
                out = _dot(p.astype(BF16), vs[pl.ds(off, 2 * BLK), :]) / den
                lse = m + jnp.log(den)
                o_res[pl.ds(off, BLK), :] = jnp.where(lane0, out[:BLK], out[BLK:])
                lse_res[pl.ds(off, BLK), :] = jnp.where(lane0, lse[:BLK], lse[BLK:])
                return 0

            lax.fori_loop(0, DIL_QBLOCKS, step, 0, unroll=8)
            _scatter_residues(o_res, 0, o_grp[grp], dilation)
            _scatter_residues(lse_res, 0, lse_grp[grp], dilation)

        for r0 in range(0, SEQ, 2 * BLK):
            rows = slice(r0, r0 + 2 * BLK)
            ls = [lse_grp[g][rows, :] for g in range(DIL_GROUPS)]
            m = jnp.maximum(jnp.maximum(ls[0], ls[1]), ls[2])
            w = [jnp.exp(l - m) for l in ls]
            den = w[0] + w[1] + w[2]
            o_ref[rows, :] = (w[0] * o_grp[0][rows, :] + w[1] * o_grp[1][rows, :] + w[2] * o_grp[2][rows, :]) / den
            lse_ref[rows, :] = m + jnp.log(den)

    def col(part, grp):
        return pl.BlockSpec((None, SEQ, LANES), lambda b, p: (b, 0, 6 * part + 2 * grp + p))

    out_spec = pl.BlockSpec((None, SEQ, LANES), lambda b, p: (b, 0, p))
    out = jax.ShapeDtypeStruct((B_LOC, SEQ, DIL_OUT), F32)
    return _call(
        body, send, name="dil_fwd", grid=(B_LOC, DIL_OUT // LANES),
        in_specs=[col(part, grp) for grp in range(DIL_GROUPS) for part in range(3)],
        out_specs=[out_spec, out_spec], out_shape=[out, out],
        scratch_shapes=[pltpu.VMEM((SEQ, LANES), BF16), pltpu.VMEM((SEQ + BLK, LANES), BF16),
                        pltpu.VMEM((SEQ + BLK, LANES), BF16), pltpu.VMEM((SEQ, LANES), F32),
                        pltpu.VMEM((SEQ, LANES), F32)] + [pltpu.VMEM((SEQ, LANES), F32)] * (2 * DIL_GROUPS),
        semantics=("parallel", "parallel"), operands=[qkv] * 9)


def _dil_bwd(qkv, d_o, lse, dsum, send=None):
    def body(*refs):
        ins, (do_ref, lse_ref, dsum_ref), outs = refs[:9], refs[9:12], refs[12:21]
        qs, ks, vs, dos, lse_res, dsum_res, dq_res, dk_acc, dv_acc = refs[21:]
        pair = pl.program_id(1)
        lane0 = lax.broadcasted_iota(jnp.int32, (BLK, LANES), 1) < HEAD_DIM
        lane1 = jnp.logical_not(lane0)
        ks[0:BLK, :] = jnp.zeros((BLK, LANES), BF16)
        vs[0:BLK, :] = jnp.zeros((BLK, LANES), BF16)
        for grp, (_, dilation) in enumerate(DIL_PAIRS):
            q_ref, k_ref, v_ref = ins[3 * grp:3 * grp + 3]
            dq_ref, dk_ref, dv_ref = outs[3 * grp:3 * grp + 3]
            per_residue = DIL_QBLOCKS // dilation
            _gather_residues(q_ref, qs, 0, dilation, QK_SCALE)
            _gather_residues(k_ref, ks, BLK, dilation)
            _gather_residues(v_ref, vs, BLK, dilation)
            _gather_residues(do_ref, dos, 0, dilation)
            _gather_residues(lse_ref, lse_res, 0, dilation)
            _gather_residues(dsum_ref, dsum_res, 0, dilation)
            dk_acc[...] = jnp.zeros_like(dk_acc)
            dv_acc[...] = jnp.zeros_like(dv_acc)
            geometry = _dil_geometry(grp, pair)

            def step(blk, _):
                off = pl.multiple_of(blk * BLK, BLK)
                q2 = _stack_heads(qs[pl.ds(off, BLK), :], lane0)
                do2 = _stack_heads(dos[pl.ds(off, BLK), :], lane0)
                kk = ks[pl.ds(off, 2 * BLK), :]
                vv = vs[pl.ds(off, 2 * BLK), :]
                lse_blk = lse_res[pl.ds(off, BLK), :]
                dsum_blk = dsum_res[pl.ds(off, BLK), :]
                lse2 = jnp.concatenate([_head_col(lse_blk, lane0), _head_col(lse_blk, lane1)], axis=0)
                dsum2 = jnp.concatenate([_head_col(dsum_blk, lane0), _head_col(dsum_blk, lane1)], axis=0)
                s = _dil_tile_scores(q2, kk, geometry, blk % per_residue != 0)
                p = jnp.exp(s - lse2)
                ds = (p * (_dot_nt(do2, vv) - dsum2)).astype(BF16)
                dq2 = _dot(ds, kk)
                dq_res[pl.ds(off, BLK), :] = jnp.where(lane0, dq2[:BLK], dq2[BLK:]) * QK_SCALE
                dk_acc[pl.ds(off, 2 * BLK), :] += _dot_tn(ds, q2)
                dv_acc[pl.ds(off, 2 * BLK), :] += _dot_tn(p.astype(BF16), do2)
                return 0

            lax.fori_loop(0, DIL_QBLOCKS, step, 0, unroll=8)
            _scatter_residues(dq_res, 0, dq_ref, dilation)
            _scatter_residues(dk_acc, BLK, dk_ref, dilation)
            _scatter_residues(dv_acc, BLK, dv_ref, dilation)

    def col(part, grp):
        return pl.BlockSpec((None, SEQ, LANES), lambda b, p: (b, 0, 6 * part + 2 * grp + p))

    slot = pl.BlockSpec((None, SEQ, LANES), lambda b, p: (b, 0, p))
    out = jax.ShapeDtypeStruct((B_LOC, SEQ, DIL_OUT), F32)
    return _call(
        body, send, name="dil_bwd", grid=(B_LOC, DIL_OUT // LANES),
        in_specs=[col(part, grp) for grp in range(DIL_GROUPS) for part in range(3)] + [slot] * 3,
        out_specs=[slot] * 9, out_shape=[out] * 9,
        scratch_shapes=[pltpu.VMEM((SEQ, LANES), BF16), pltpu.VMEM((SEQ + BLK, LANES), BF16),
                        pltpu.VMEM((SEQ + BLK, LANES), BF16), pltpu.VMEM((SEQ, LANES), BF16),
                        pltpu.VMEM((SEQ, LANES), F32), pltpu.VMEM((SEQ, LANES), F32),
                        pltpu.VMEM((SEQ, LANES), F32), pltpu.VMEM((SEQ + BLK, LANES), F32),
                        pltpu.VMEM((SEQ + BLK, LANES), F32)],
        semantics=("parallel", "parallel"), operands=[qkv] * 9 + [d_o, lse, dsum])


def _peers():
    x, y, c = lax.axis_index("x"), lax.axis_index("y"), lax.axis_index("c")
    me = 4 * x + 2 * y + c
    peers = []
    for mask in range(1, N_DEV):
        px = 1 - x if mask & 4 else x
        py = 1 - y if mask & 2 else y
        pc = 1 - c if mask & 1 else c
        peers.append(((px, py, pc), 4 * px + 2 * py + pc))
    return me, peers


def _all_gather(shard, name):
    def body(src_ref, out_ref, send_sems, recv_sems, local_sem):
        x, y, c = lax.axis_index("x"), lax.axis_index("y"), lax.axis_index("c")
        sibling = (x, y, 1 - c)
        chips = [(1 - x, y), (x, 1 - y), (1 - x, 1 - y)]

        def slot(px, py, pc):
            return out_ref.at[4 * px + 2 * py + pc]

        def copy(k, block, to, src=None):
            return pltpu.make_async_remote_copy(
                src_ref=slot(*block) if src is None else src, dst_ref=slot(*block),
                send_sem=send_sems.at[k], recv_sem=recv_sems.at[k], device_id=to,
                device_id_type=pl.DeviceIdType.MESH)

        mine = pltpu.make_async_copy(src_ref, slot(x, y, c), local_sem)
        mine.start()
        first = [copy(0, (x, y, c), sibling, src=src_ref)]
        first += [copy(1 + j, (x, y, c), (*chip, c), src=src_ref) for j, chip in enumerate(chips)]
        for cp in first:
            cp.start()
        passed = [copy(4 + j, (*chip, c), sibling) for j, chip in enumerate(chips)]
        for j, chip in enumerate(chips):
            copy(1 + j, (*chip, c), (x, y, c)).wait_recv()
            passed[j].start()
        copy(0, sibling, (x, y, c)).wait_recv()
        for j, chip in enumerate(chips):
            copy(4 + j, (*chip, 1 - c), (x, y, c)).wait_recv()
        for cp in first + passed:
            cp.wait_send()
        mine.wait()

    return pl.pallas_call(
        body, name=name,
        in_specs=[pl.BlockSpec(memory_space=pl.ANY)],
        out_specs=pl.BlockSpec(memory_space=pl.ANY),
        out_shape=jax.ShapeDtypeStruct((N_DEV,) + shard.shape, shard.dtype),
        scratch_shapes=[pltpu.SemaphoreType.DMA((N_DEV - 1,)), pltpu.SemaphoreType.DMA((N_DEV - 1,)),
                        pltpu.SemaphoreType.DMA],
    )(shard)


def _call(body, send, *, name, grid, in_specs, out_specs, out_shape, scratch_shapes, semantics, operands):
    if send is None:
        return pl.pallas_call(
            body, name=name, grid=grid, in_specs=in_specs, out_specs=out_specs, out_shape=out_shape,
            scratch_shapes=scratch_shapes, compiler_params=_params(semantics))(*operands), []
    srcs, kinds = [s for s, _ in send], [k for _, k in send]
    n, n_in, n_out, n_scr = len(srcs), len(in_specs), len(out_specs), len(scratch_shapes)
    steps = math.prod(grid)
    relay_step = (3 * steps) // 4

    def plan(refs):
        src_refs, land_refs = refs[n_in:n_in + n], refs[n_in + n + n_out:n_in + 2 * n + n_out]
        send_sems, recv_sems, local_sems = refs[-3:]
        x, y, c = lax.axis_index("x"), lax.axis_index("y"), lax.axis_index("c")
        me, peers = _peers()
        first, relayed_in, relayed_out, arrivals, sends, own = [], [], [], [], [], []
        for a, kind in enumerate(kinds):
            def copy(k, src, dst_slot, to):
                return pltpu.make_async_remote_copy(
                    src_ref=src, dst_ref=land_refs[a].at[dst_slot], send_sem=send_sems.at[a * (N_DEV - 1) + k],
                    recv_sem=recv_sems.at[a * (N_DEV - 1) + k], device_id=to, device_id_type=pl.DeviceIdType.MESH)

            if kind == "gather_by_chip":
                idx = lambda px, py, pc: 4 * px + 2 * py + pc
                chips = [(1 - x, y), (x, 1 - y), (1 - x, 1 - y)]
                mine = [copy(0, src_refs[a], me, (x, y, 1 - c))]
                arrivals.append(copy(0, src_refs[a], idx(x, y, 1 - c), (x, y, 1 - c)))
                for j, (px, py) in enumerate(chips):
                    mine.append(copy(1 + j, src_refs[a], me, (px, py, c)))
                    relayed_in.append(copy(1 + j, src_refs[a], idx(px, py, c), (px, py, c)))
                    relayed_out.append(copy(4 + j, land_refs[a].at[idx(px, py, c)], idx(px, py, c), (x, y, 1 - c)))
                    arrivals.append(copy(4 + j, src_refs[a], idx(px, py, 1 - c), (x, y, 1 - c)))
                first += mine
                sends += mine + relayed_out[-3:]
                own.append(pltpu.make_async_copy(src_refs[a], land_refs[a].at[me], local_sems.at[a]))
            elif kind == "scatter_by_chip":
                for k, (px, py) in enumerate([(1 - x, y), (x, 1 - y), (1 - x, 1 - y)]):
                    first.append(copy(k, src_refs[a].at[2 * px + py], 2 * x + y, (px, py, c)))
                    arrivals.append(copy(k, src_refs[a].at[2 * px + py], 2 * px + py, (px, py, c)))
                sends += first[-3:]
                own.append(pltpu.make_async_copy(src_refs[a].at[2 * x + y], land_refs[a].at[2 * x + y],
                                                 local_sems.at[a]))
            else:
                part = (lambda i: src_refs[a].at[i]) if kind == "scatter" else (lambda i: src_refs[a])
                for k, (peer, peer_idx) in enumerate(peers):
                    first.append(copy(k, part(peer_idx), me, peer))
                    arrivals.append(copy(k, part(peer_idx), peer_idx, peer))
                sends += first[-(N_DEV - 1):]
                own.append(pltpu.make_async_copy(part(me), land_refs[a].at[me], local_sems.at[a]))
        return first, relayed_in, relayed_out, arrivals, sends, own

    def wrapped(*refs):
        step = 0
        for axis, size in enumerate(grid):
            step = step * size + pl.program_id(axis)

        @pl.when(step == 0)
        def _():
            first, _, _, _, _, own = plan(refs)
            for cp in first + own:
                cp.start()

        if "gather_by_chip" in kinds:
            @pl.when(step == relay_step)
            def _():
                _, relayed_in, relayed_out, _, _, _ = plan(refs)
                for cp_in, cp_out in zip(relayed_in, relayed_out):
                    cp_in.wait_recv()
                    cp_out.start()

        body(*refs[:n_in], *refs[n_in + n:n_in + n + n_out], *refs[n_in + 2 * n + n_out:n_in + 2 * n + n_out + n_scr])

        @pl.when(step == steps - 1)
        def _():
            _, _, _, arrivals, sends, own = plan(refs)
            for cp in arrivals:
                cp.wait_recv()
            for cp in sends:
                cp.wait_send()
            for cp in own:
                cp.wait()

    anywhere = pl.BlockSpec(memory_space=pl.ANY)
    lands = [jax.ShapeDtypeStruct((N_DEV // 2 if k == "scatter_by_chip" else N_DEV,) + s.shape[-2:], s.dtype)
             for s, k in send]
    out = pl.pallas_call(
        wrapped, name=name, grid=grid,
        in_specs=list(in_specs) + [anywhere] * n, out_specs=list(out_specs) + [anywhere] * n,
        out_shape=list(out_shape) + lands,
        scratch_shapes=list(scratch_shapes) + [pltpu.SemaphoreType.DMA((n * (N_DEV - 1),)),
                                               pltpu.SemaphoreType.DMA((n * (N_DEV - 1),)),
                                               pltpu.SemaphoreType.DMA((n,))],
        compiler_params=_params(("arbitrary",) * len(grid)),
    )(*operands, *srcs)
    return out[:n_out], list(out[n_out:])


def _pair_swap(blocks):
    def body(src_ref, out_ref, send_sems, recv_sems):
        x, y, c = lax.axis_index("x"), lax.axis_index("y"), lax.axis_index("c")
        copies = [pltpu.make_async_remote_copy(
            src_ref=src_ref.at[2 * chip + (1 - c)], dst_ref=out_ref.at[chip], send_sem=send_sems.at[chip],
            recv_sem=recv_sems.at[chip], device_id=(x, y, 1 - c), device_id_type=pl.DeviceIdType.MESH)
            for chip in range(N_DEV // 2)]
        for cp in copies:
            cp.start()
        for cp in copies:
            cp.wait()

    return pl.pallas_call(
        body, name="pair_swap_grad_w_in",
        in_specs=[pl.BlockSpec(memory_space=pl.ANY)], out_specs=pl.BlockSpec(memory_space=pl.ANY),
        out_shape=jax.ShapeDtypeStruct((N_DEV // 2,) + blocks.shape[1:], blocks.dtype),
        scratch_shapes=[pltpu.SemaphoreType.DMA((N_DEV // 2,)), pltpu.SemaphoreType.DMA((N_DEV // 2,))],
    )(blocks)


def _pair_sum(blocks, swapped, core):
    _, rows, cols = swapped.shape
    tile_rows = _row_tile(rows)

    def body(core_ref, mine_ref, theirs_ref, o_ref):
        o_ref[...] = (mine_ref[...].astype(F32) + theirs_ref[...].astype(F32)).astype(o_ref.dtype)

    return pl.pallas_call(
        body, name="pair_sum_grad_w_in",
        grid_spec=pltpu.PrefetchScalarGridSpec(
            num_scalar_prefetch=1, grid=(N_DEV // 2, rows // tile_rows),
            in_specs=[pl.BlockSpec((None, tile_rows, cols), lambda j, i, core_ref: (2 * j + core_ref[0], i, 0)),
                      pl.BlockSpec((None, tile_rows, cols), lambda j, i, core_ref: (j, i, 0))],
            out_specs=pl.BlockSpec((None, tile_rows, cols), lambda j, i, core_ref: (j, i, 0))),
        out_shape=jax.ShapeDtypeStruct(swapped.shape, swapped.dtype),
        compiler_params=_params(("parallel", "parallel")),
    )(core, blocks, swapped)


def _sum_in_device_order(land_ref):
    acc = land_ref[0].astype(F32)
    for j in range(1, land_ref.shape[0]):
        acc = acc + land_ref[j].astype(F32)
    return acc


def _adam_math(w, g, m, v):
    c1 = 1.0 - ADAM_B1 ** ADAM_STEP
    c2 = 1.0 - ADAM_B2 ** ADAM_STEP
    m_new = ADAM_B1 * m + (1.0 - ADAM_B1) * g
    v_new = ADAM_B2 * v + (1.0 - ADAM_B2) * (g * g)
    delta = -ADAM_LR * ((m_new / c1) / (jnp.sqrt(v_new / c2) + ADAM_EPS) + ADAM_WD * w)
    return delta, m_new, v_new


def _row_tile(rows):
    return max(t for t in range(8, 257, 8) if rows % t == 0) if rows % 8 == 0 else rows


def _sum_update(land, w, m, v, name):
    slots, rows, cols = land.shape
    tile_rows = _row_tile(rows)

    def body(land_ref, w_ref, m_ref, v_ref, g_ref, d_ref, nm_ref, nv_ref):
        g = _sum_in_device_order(land_ref)
        g_ref[...] = g
        d_ref[...], nm_ref[...], nv_ref[...] = _adam_math(w_ref[...], g, m_ref[...], v_ref[...])

    tile = pl.BlockSpec((None, tile_rows, cols), lambda i: (0, i, 0))
    out = jax.ShapeDtypeStruct((1, rows, cols), F32)
    return pl.pallas_call(
        body, name=name, grid=(rows // tile_rows,),
        in_specs=[pl.BlockSpec((slots, tile_rows, cols), lambda i: (0, i, 0)), tile, tile, tile],
        out_specs=[tile] * 4, out_shape=[out] * 4,
        compiler_params=_params(("parallel",)),
    )(land, w, m, v)


def _sum_gains(land):
    def body(land_ref, o_ref):
        o_ref[...] = _sum_in_device_order(land_ref)

    return pl.pallas_call(
        body, name="sum_gain_grads", grid=(1,),
        in_specs=[pl.BlockSpec(land.shape, lambda i: (0, 0, 0))],
        out_specs=pl.BlockSpec(land.shape[1:], lambda i: (0, 0)),
        out_shape=jax.ShapeDtypeStruct(land.shape[1:], F32),
    )(land)


def _adamw(w, g, m, v, name):
    def body(w_ref, g_ref, m_ref, v_ref, d_ref, nm_ref, nv_ref):
        d_ref[...], nm_ref[...], nv_ref[...] = _adam_math(w_ref[...], g_ref[...], m_ref[...], v_ref[...])

    whole = pl.BlockSpec(w.shape, lambda i: (0, 0))
    out = jax.ShapeDtypeStruct(w.shape, F32)
    return pl.pallas_call(
        body, name=name, grid=(1,),
        in_specs=[whole] * 4, out_specs=[whole] * 3, out_shape=[out] * 3,
    )(w, g, m, v)


GROUP_FFN = ("w_ffn_in", "w_ffn_out")
GROUP_MIX = ("w_sb_up", "w_dil_up", "w_out")
COL_SHARDED = ("w_in", "w_sb_up", "w_dil_up", "w_ffn_in")


def _full_from_shards(name, slots):
    _, r, c = slots.shape
    if name in COL_SHARDED:
        return slots.transpose(1, 0, 2).reshape(r, N_DEV * c)
    return slots.reshape(N_DEV * r, c)


def _shards_from_full(name, full):
    rows, cols = full.shape
    if name in COL_SHARDED:
        return full.reshape(rows, N_DEV, cols // N_DEV).transpose(1, 0, 2)
    return full.reshape(N_DEV, rows // N_DEV, cols)


def _local_step(x, target, g_mix, g_ffn, g_fin, w_in, shards=None, rest=None):
    gather = lambda names, kind: None if shards is None else [(shards[n], kind) for n in names]
    scatter = lambda blocks: None if shards is None else [(t, "scatter") for t in blocks]
    landed = lambda blocks, lands: lands if lands else blocks

    w = {"w_in": w_in}
    if shards is None:
        w.update(rest)
        w["w_ffn_in"] = _shards_from_full("w_ffn_in", rest["w_ffn_in"])
    (qkv_sb, qkv_dl, gates, u), lands = _norm_proj(x, g_mix, w["w_in"], gather(("w_ffn_out",), "gather"))
    if lands:
        w["w_ffn_out"] = _full_from_shards("w_ffn_out", lands[0])
    qkv_sb = qkv_sb.reshape(B_LOC, SEQ, 3 * SB_WIDTH)
    qkv_dl = qkv_dl.reshape(B_LOC, SEQ, 3 * DIL_WIDTH)
    (o_sb,), lands = _sb_fwd(qkv_sb, gather(("w_ffn_in",), "gather_by_chip"))
    if lands:
        w["w_ffn_in"] = lands[0]
    o_sb = o_sb.reshape(TOK, SB_WIDTH)
    (o_dl, lse), lands = _dil_fwd(qkv_dl, gather(GROUP_MIX, "gather"))
    w.update({n: _full_from_shards(n, t) for n, t in zip(GROUP_MIX, lands)})
    o_dl = o_dl.reshape(TOK, DIL_OUT)

    x1, merged = _mix_out(x, o_sb, o_dl, gates, w["w_sb_up"], w["w_dil_up"], w["w_out"])
    loss, dx1, u2, act, dh, dx2, dg_fin, dg_ffn = _ffn_fwd_bwd(x1, target, g_ffn, g_fin, w["w_ffn_in"], w["w_ffn_out"])
    dgates, dy_sb, dy_dl, do_sb, do_dl, dsum = _mix_bwd(dx1, o_sb, o_dl, gates, w["w_sb_up"], w["w_dil_up"], w["w_out"])
    blocks = {
        "w_sb_up": _atb(o_sb, dy_sb, "grad_w_sb_up", SB_WIDTH, D_MODEL, col_blocks=N_DEV),
        "w_dil_up": _atb(o_dl, dy_dl, "grad_w_dil_up", DIL_OUT, D_MODEL, col_blocks=N_DEV),
        "w_out": _shards_from_full("w_out", _atb(merged, dx1, "grad_w_out", D_MODEL, D_MODEL)),
        "w_ffn_in": _atb_shards(u2, dh, "grad_w_ffn_in"),
        "w_ffn_out": _shards_from_full("w_ffn_out", _atb_shards(act, dx2, "grad_w_ffn_out")),
    }
    grads = {}

    early, late = ("w_ffn_in",), ("w_ffn_out",) + GROUP_MIX
    early_blocks = [blocks[n] for n in early]
    (dq_sb, dk_sb, dv_sb), lands = _sb_bwd(qkv_sb, do_sb.reshape(B_LOC, SEQ, SB_WIDTH), scatter(early_blocks))
    grads.update(zip(early, landed(early_blocks, lands)))
    as_batch = lambda t: t.reshape(B_LOC, SEQ, DIL_OUT)
    late_blocks = [blocks[n] for n in late]
    d_dl, lands = _dil_bwd(qkv_dl, as_batch(do_dl), lse, as_batch(dsum), scatter(late_blocks))
    grads.update(zip(late, landed(late_blocks, lands)))
    flat = lambda t: t.reshape(TOK, -1)
    dproj = ([flat(dq_sb), flat(dk_sb), flat(dv_sb)]
             + [flat(d_dl[3 * grp + part]) for part in range(3) for grp in range(DIL_GROUPS)] + [dgates])

    w_in_blocks = _shards_from_full("w_in", _atb_pieces(u, dproj, "grad_w_in", D_MODEL // 2))
    if shards is None:
        grads["w_in"] = w_in_blocks
        send = None
    else:
        core = lax.axis_index("c").astype(jnp.int32).reshape(1)
        send = [(_pair_sum(w_in_blocks, _pair_swap(w_in_blocks), core), "scatter_by_chip")]
    (grad_x, dg_mix), lands = _proj_bwd(dproj, dx1, x, g_mix, w["w_in"], send)
    if lands:
        grads["w_in"] = lands[0]
    gain_grads = jnp.concatenate([dg_mix, dg_ffn, dg_fin], axis=0)
    return loss, grad_x, gain_grads, grads


def kernel(x, norm_mix_g, w_in, w_sb_up, w_dil_up, w_out, norm_ffn_g, w_ffn_in, w_ffn_out, norm_final_g, loss_target, m_norm_mix_g, m_w_in, m_w_sb_up, m_w_dil_up, m_w_out, m_norm_ffn_g, m_w_ffn_in, m_w_ffn_out, m_norm_final_g, v_norm_mix_g, v_w_in, v_w_sb_up, v_w_dil_up, v_w_out, v_norm_ffn_g, v_w_ffn_in, v_w_ffn_out, v_norm_final_g):
    mats = {"w_in": w_in, "w_sb_up": w_sb_up, "w_dil_up": w_dil_up, "w_out": w_out,
            "w_ffn_in": w_ffn_in, "w_ffn_out": w_ffn_out}
    moments_m = {"w_in": m_w_in, "w_sb_up": m_w_sb_up, "w_dil_up": m_w_dil_up, "w_out": m_w_out,
                 "w_ffn_in": m_w_ffn_in, "w_ffn_out": m_w_ffn_out}
    moments_v = {"w_in": v_w_in, "w_sb_up": v_w_sb_up, "w_dil_up": v_w_dil_up, "w_out": v_w_out,
                 "w_ffn_in": v_w_ffn_in, "w_ffn_out": v_w_ffn_out}
    gathered_w_in = _all_gather(w_in[0].astype(BF16), "all_gather_w_in")
    g_fin = norm_final_g.reshape(1, D_MODEL)
    loss, grad_x, gain_grads, grad_slots = _local_step(
        x.reshape(TOK, D_MODEL), loss_target.reshape(TOK, D_MODEL), norm_mix_g, norm_ffn_g, g_fin,
        _full_from_shards("w_in", gathered_w_in),
        shards={name: mats[name][0].astype(BF16) for name in GROUP_FFN + GROUP_MIX})

    gain_rows = jnp.concatenate([gain_grads, jnp.tile(loss, (1, D_MODEL // LANES)),
                                 jnp.zeros((8 - 4, D_MODEL), F32)], axis=0)
    g_gains = _sum_gains(_all_gather(gain_rows, "all_gather_gains"))

    out_g, out_d, out_m, out_v = {}, {}, {}, {}
    for name, slots in grad_slots.items():
        out_g[name], out_d[name], out_m[name], out_v[name] = _sum_update(
            slots, mats[name], moments_m[name], moments_v[name], "update_" + name)

    gain_w = jnp.concatenate([norm_mix_g, norm_ffn_g, g_fin], axis=0)
    gain_m = jnp.concatenate([m_norm_mix_g, m_norm_ffn_g, m_norm_final_g.reshape(1, D_MODEL)], axis=0)
    gain_v = jnp.concatenate([v_norm_mix_g, v_norm_ffn_g, v_norm_final_g.reshape(1, D_MODEL)], axis=0)
    gd, gm, gv = _adamw(gain_w, g_gains[:3], gain_m, gain_v, "adamw_gains")
    for idx, name in enumerate(("norm_mix_g", "norm_ffn_g", "norm_final_g")):
        shape = (D_MODEL,) if name == "norm_final_g" else (1, D_MODEL)
        out_g[name] = g_gains[idx].reshape(shape)
        out_d[name], out_m[name], out_v[name] = gd[idx].reshape(shape), gm[idx].reshape(shape), gv[idx].reshape(shape)

    order = ("norm_mix_g", "w_in", "w_sb_up", "w_dil_up", "w_out", "norm_ffn_g", "w_ffn_in", "w_ffn_out",
             "norm_final_g")
    return (g_gains[3, 0], grad_x.reshape(B_LOC, SEQ, D_MODEL),
            *[out_g[n] for n in order], *[out_d[n] for n in order],
            *[out_m[n] for n in order], *[out_v[n] for n in order])
```

```python
import math

import jax
import jax.numpy as jnp
from jax import lax
from jax.experimental import pallas as pl
from jax.experimental.pallas import tpu as pltpu

F32 = jnp.float32
BF16 = jnp.bfloat16

N_DEV = 8
D_MODEL = 1024
SEQ = 2048
B_LOC = 2
TOK = B_LOC * SEQ
HEAD_DIM = 64
SB_WIDTH = 512
DIL_WIDTH = 768
DIL_OUT = 256
QKV_WIDTH = 3 * SB_WIDTH + 3 * DIL_WIDTH
IN_WIDTH = QKV_WIDTH + 2 * D_MODEL
D_FF = 2816
DIL_PAIRS = ((128, 1), (512, 4), (2048, 16))
DIL_HEADS = 12
RMS_EPS = 1e-6
ALIBI_MAX_BIAS = 8.0
QK_SCALE = 1.0 / math.sqrt(HEAD_DIM)
BLK = 128
LANES = 128
NEG_BIG = -1e30

ADAM_LR = 0.001
ADAM_B1 = 0.9
ADAM_B2 = 0.999
ADAM_EPS = 1e-08
ADAM_WD = 0.01
ADAM_STEP = 10

VMEM_LIMIT = 56 * 1024 * 1024


def _dot(a, b):
    return jnp.dot(a, b, preferred_element_type=F32)


def _dot_nt(a, b):
    return lax.dot_general(a, b, (((1,), (1,)), ((), ())), preferred_element_type=F32)


def _dot_tn(a, b):
    return lax.dot_general(a, b, (((0,), (0,)), ((), ())), preferred_element_type=F32)


def _sigmoid(z):
    return 1.0 / (1.0 + jnp.exp(-z))


def _split_bf16(v):
    hi = v.astype(BF16)
    lo = (v - hi.astype(F32)).astype(BF16)
    return hi, lo


def _chunks(width, step=512):
    out, c = [], 0
    while c < width:
        w = min(step, width - c)
        out.append((c, w))
        c += w
    return out


def _resident(shape):
    nd = len(shape)
    return pl.BlockSpec(shape, lambda *_: (0,) * nd, pipeline_mode=pl.Buffered(1))


def _params(sem):
    return pltpu.CompilerParams(dimension_semantics=sem, vmem_limit_bytes=VMEM_LIMIT)


def _rms_fwd(x, g):
    r = lax.rsqrt(jnp.mean(x * x, axis=-1, keepdims=True) + RMS_EPS)
    n = x * r
    return n, r, n * g


def _rms_bwd(dy, n, r, g):
    dg = jnp.sum(dy * n, axis=0, keepdims=True)
    dn = dy * g
    dx = r * (dn - n * jnp.mean(dn * n, axis=-1, keepdims=True))
    return dx, dg


TM = 256


def _norm_proj(x, g, w_in, send=None):
    def body(x_ref, g_ref, w_ref, sb_ref, dl_ref, gate_ref, u_ref):
        _, _, u = _rms_fwd(x_ref[...], g_ref[...])
        u = u.astype(BF16)
        u_ref[...] = u
        for c0, w in _chunks(3 * SB_WIDTH):
            sb_ref[:, c0:c0 + w] = _dot(u, w_ref[:, c0:c0 + w]).astype(BF16)
        for c0, w in _chunks(3 * DIL_WIDTH):
            dl_ref[:, c0:c0 + w] = _dot(u, w_ref[:, 3 * SB_WIDTH + c0:3 * SB_WIDTH + c0 + w])
        for c0, w in _chunks(2 * D_MODEL):
            gate_ref[:, c0:c0 + w] = _dot(u, w_ref[:, QKV_WIDTH + c0:QKV_WIDTH + c0 + w])

    return _call(
        body, send, name="norm_proj", grid=(TOK // TM,),
        in_specs=[pl.BlockSpec((TM, D_MODEL), lambda i: (i, 0)), _resident((1, D_MODEL)),
                  _resident((D_MODEL, IN_WIDTH))],
        out_specs=[pl.BlockSpec((TM, 3 * SB_WIDTH), lambda i: (i, 0)),
                   pl.BlockSpec((TM, 3 * DIL_WIDTH), lambda i: (i, 0)),
                   pl.BlockSpec((TM, 2 * D_MODEL), lambda i: (i, 0)),
                   pl.BlockSpec((TM, D_MODEL), lambda i: (i, 0))],
        out_shape=[jax.ShapeDtypeStruct((TOK, 3 * SB_WIDTH), BF16),
                   jax.ShapeDtypeStruct((TOK, 3 * DIL_WIDTH), F32),
                   jax.ShapeDtypeStruct((TOK, 2 * D_MODEL), F32),
                   jax.ShapeDtypeStruct((TOK, D_MODEL), BF16)],
        scratch_shapes=[], semantics=("parallel",), operands=(x, g, w_in))


def _mix_out(x, o_sb, o_dl, gates, w_sb_up, w_dil_up, w_out):
    def body(x_ref, osb_ref, odl_ref, gate_ref, wsb_ref, wdl_ref, wout_ref, x1_ref, mg_ref):
        y_sb = _dot(osb_ref[...], wsb_ref[...])
        y_dl = _dot(odl_ref[...].astype(BF16), wdl_ref[...])
        merged = (_sigmoid(gate_ref[:, :D_MODEL]) * y_sb
                  + _sigmoid(gate_ref[:, D_MODEL:]) * y_dl).astype(BF16)
        mg_ref[...] = merged
        x1_ref[...] = x_ref[...] + _dot(merged, wout_ref[...])

    return pl.pallas_call(
        body, name="mix_out", grid=(TOK // TM,),
        in_specs=[pl.BlockSpec((TM, D_MODEL), lambda i: (i, 0)),
                  pl.BlockSpec((TM, SB_WIDTH), lambda i: (i, 0)),
                  pl.BlockSpec((TM, DIL_OUT), lambda i: (i, 0)),
                  pl.BlockSpec((TM, 2 * D_MODEL), lambda i: (i, 0)),
                  _resident((SB_WIDTH, D_MODEL)), _resident((DIL_OUT, D_MODEL)),
                  _resident((D_MODEL, D_MODEL))],
        out_specs=[pl.BlockSpec((TM, D_MODEL), lambda i: (i, 0)),
                   pl.BlockSpec((TM, D_MODEL), lambda i: (i, 0))],
        out_shape=[jax.ShapeDtypeStruct((TOK, D_MODEL), F32),
                   jax.ShapeDtypeStruct((TOK, D_MODEL), BF16)],
        compiler_params=_params(("parallel",)),
    )(x, o_sb, o_dl, gates, w_sb_up, w_dil_up, w_out)


FF_SHARD = 2 * D_FF // N_DEV
FF_PAIRS = N_DEV // 2


def _ffn_fwd_bwd(x1, target, g_ffn, g_fin, w_ffn_in, w_ffn_out):
    def body(x1_ref, t_ref, gffn_ref, gfin_ref, win_ref, wout_ref,
             loss_ref, dx1_ref, u2_ref, act_ref, dh_ref, dx2_ref, dgfin_ref, dgffn_ref, h_scr):
        i = pl.program_id(0)

        @pl.when(i == 0)
        def _():
            loss_ref[...] = jnp.zeros_like(loss_ref)
            dgfin_ref[...] = jnp.zeros_like(dgfin_ref)
            dgffn_ref[...] = jnp.zeros_like(dgffn_ref)

        x1 = x1_ref[...]
        g_ffn_v = gffn_ref[...]
        g_fin_v = gfin_ref[...]
        n2, r2, u2 = _rms_fwd(x1, g_ffn_v)
        u2 = u2.astype(BF16)
        u2_ref[...] = u2
        x2 = x1
        for r in range(FF_PAIRS):
            gate = _dot(u2, win_ref[r])
            up = _dot(u2, win_ref[r + FF_PAIRS])
            h_scr[r] = gate
            h_scr[r + FF_PAIRS] = up
            act = (gate * _sigmoid(gate) * up).astype(BF16)
            act_ref[r] = act
            x2 = x2 + _dot(act, wout_ref[r * FF_SHARD:(r + 1) * FF_SHARD, :])
        n3, r3, y = _rms_fwd(x2, g_fin_v)
        err = y - t_ref[...]
        sq = jnp.sum(jnp.sum(err * err, axis=1, keepdims=True), axis=0, keepdims=True)
        loss_ref[...] += sq * (0.5 / D_MODEL)
        dx2, dgfin = _rms_bwd(err * (1.0 / D_MODEL), n3, r3, g_fin_v)
        dgfin_ref[...] += dgfin
        dx2_b = dx2.astype(BF16)
        dx2_ref[...] = dx2_b
        du2 = jnp.zeros((TM, D_MODEL), F32)
        for r in range(FF_PAIRS):
            gate = h_scr[r]
            up = h_scr[r + FF_PAIRS]
            dact = _dot_nt(dx2_b, wout_ref[r * FF_SHARD:(r + 1) * FF_SHARD, :])
            sg = _sigmoid(gate)
            dgate = (dact * up * (sg * (1.0 + gate * (1.0 - sg)))).astype(BF16)
            dup = (dact * (gate * sg)).astype(BF16)
            dh_ref[r] = dgate
            dh_ref[r + FF_PAIRS] = dup
            du2 = du2 + _dot_nt(dgate, win_ref[r])
            du2 = du2 + _dot_nt(dup, win_ref[r + FF_PAIRS])
        dx1_n, dgffn = _rms_bwd(du2, n2, r2, g_ffn_v)
        dgffn_ref[...] += dgffn
        dx1_ref[...] = dx2 + dx1_n

    tile = lambda w: pl.BlockSpec((TM, w), lambda i: (i, 0))
    shards = lambda n: pl.BlockSpec((n, TM, FF_SHARD), lambda i: (0, i, 0))
    acc = lambda w: pl.BlockSpec((1, w), lambda i: (0, 0))
    return pl.pallas_call(
        body, name="ffn_fwd_bwd", grid=(TOK // TM,),
        in_specs=[tile(D_MODEL), tile(D_MODEL), _resident((1, D_MODEL)), _resident((1, D_MODEL)),
                  _resident((N_DEV, D_MODEL, FF_SHARD)), _resident((D_FF, D_MODEL))],
        out_specs=[acc(LANES), tile(D_MODEL), tile(D_MODEL), shards(FF_PAIRS), shards(N_DEV), tile(D_MODEL),
                   acc(D_MODEL), acc(D_MODEL)],
        out_shape=[jax.ShapeDtypeStruct((1, LANES), F32),
                   jax.ShapeDtypeStruct((TOK, D_MODEL), F32),
                   jax.ShapeDtypeStruct((TOK, D_MODEL), BF16),
                   jax.ShapeDtypeStruct((FF_PAIRS, TOK, FF_SHARD), BF16),
                   jax.ShapeDtypeStruct((N_DEV, TOK, FF_SHARD), BF16),
                   jax.ShapeDtypeStruct((TOK, D_MODEL), BF16),
                   jax.ShapeDtypeStruct((1, D_MODEL), F32),
                   jax.ShapeDtypeStruct((1, D_MODEL), F32)],
        scratch_shapes=[pltpu.VMEM((N_DEV, TM, FF_SHARD), F32)],
        compiler_params=_params(("arbitrary",)),
    )(x1, target, g_ffn, g_fin, w_ffn_in, w_ffn_out)


def _mix_bwd(dx1, o_sb, o_dl, gates, w_sb_up, w_dil_up, w_out):
    def body(dx1_ref, osb_ref, odl_ref, gate_ref, wsb_ref, wdl_ref, wout_ref,
             dgate_ref, dysb_ref, dydl_ref, dosb_ref, dodl_ref, dsum_ref):
        dmerged = _dot_nt(dx1_ref[...].astype(BF16), wout_ref[...])
        o_dl = odl_ref[...]
        y_sb = _dot(osb_ref[...], wsb_ref[...])
        y_dl = _dot(o_dl.astype(BF16), wdl_ref[...])
        s_sb = _sigmoid(gate_ref[:, :D_MODEL])
        s_dl = _sigmoid(gate_ref[:, D_MODEL:])
        dgate_ref[:, :D_MODEL] = (dmerged * y_sb * (s_sb * (1.0 - s_sb))).astype(BF16)
        dgate_ref[:, D_MODEL:] = (dmerged * y_dl * (s_dl * (1.0 - s_dl))).astype(BF16)
        dy_sb = (dmerged * s_sb).astype(BF16)
        dy_dl = (dmerged * s_dl).astype(BF16)
        dysb_ref[...] = dy_sb
        dydl_ref[...] = dy_dl
        dosb_ref[...] = _dot_nt(dy_sb, wsb_ref[...]).astype(BF16)
        do_dl = _dot_nt(dy_dl, wdl_ref[...])
        dodl_ref[...] = do_dl
        row = lax.broadcasted_iota(jnp.int32, (DIL_OUT, DIL_OUT), 0) // HEAD_DIM
        col = lax.broadcasted_iota(jnp.int32, (DIL_OUT, DIL_OUT), 1) // HEAD_DIM
        same_head = (row == col).astype(BF16)
        hi, lo = _split_bf16(do_dl * o_dl)
        dsum_ref[...] = _dot(hi, same_head) + _dot(lo, same_head)

    tile = lambda w: pl.BlockSpec((TM, w), lambda i: (i, 0))
    return pl.pallas_call(
        body, name="mix_bwd", grid=(TOK // TM,),
        in_specs=[tile(D_MODEL), tile(SB_WIDTH), tile(DIL_OUT), tile(2 * D_MODEL),
                  _resident((SB_WIDTH, D_MODEL)), _resident((DIL_OUT, D_MODEL)),
                  _resident((D_MODEL, D_MODEL))],
        out_specs=[tile(2 * D_MODEL), tile(D_MODEL), tile(D_MODEL), tile(SB_WIDTH), tile(DIL_OUT),
                   tile(DIL_OUT)],
        out_shape=[jax.ShapeDtypeStruct((TOK, 2 * D_MODEL), BF16),
                   jax.ShapeDtypeStruct((TOK, D_MODEL), BF16),
                   jax.ShapeDtypeStruct((TOK, D_MODEL), BF16),
                   jax.ShapeDtypeStruct((TOK, SB_WIDTH), BF16),
                   jax.ShapeDtypeStruct((TOK, DIL_OUT), F32),
                   jax.ShapeDtypeStruct((TOK, DIL_OUT), F32)],
        compiler_params=_params(("parallel",)),
    )(dx1, o_sb, o_dl, gates, w_sb_up, w_dil_up, w_out)


def _proj_bwd(dproj, dx1, x, g, w_in, send=None):
    widths = [p.shape[1] for p in dproj]

    def body(*refs):
        dx1_ref, x_ref, g_ref, w_ref, dx_ref, dg_ref = refs[len(widths):]

        @pl.when(pl.program_id(0) == 0)
        def _():
            dg_ref[...] = jnp.zeros_like(dg_ref)

        du = jnp.zeros((TM, D_MODEL), F32)
        c0 = 0
        for dp_ref, w in zip(refs, widths):
            du = du + _dot_nt(dp_ref[...].astype(BF16), w_ref[:, c0:c0 + w])
            c0 += w
        g_v = g_ref[...]
        n, r, _ = _rms_fwd(x_ref[...], g_v)
        dx, dg = _rms_bwd(du, n, r, g_v)
        dg_ref[...] += dg
        dx_ref[...] = dx1_ref[...] + dx

    tile = lambda w: pl.BlockSpec((TM, w), lambda i: (i, 0))
    return _call(
        body, send, name="proj_bwd", grid=(TOK // TM,),
        in_specs=[tile(w) for w in widths] + [tile(D_MODEL), tile(D_MODEL), _resident((1, D_MODEL)),
                                              _resident((D_MODEL, IN_WIDTH))],
        out_specs=[tile(D_MODEL), pl.BlockSpec((1, D_MODEL), lambda i: (0, 0))],
        out_shape=[jax.ShapeDtypeStruct((TOK, D_MODEL), F32),
                   jax.ShapeDtypeStruct((1, D_MODEL), F32)],
        scratch_shapes=[], semantics=("arbitrary",), operands=(*dproj, dx1, x, g, w_in))


def _atb_pieces(a, pieces, name, tm, tk=512):
    m = a.shape[1]
    widths = [p.shape[1] for p in pieces]
    n = sum(widths)
    nk = TOK // tk

    def body(a_ref, *refs):
        o_ref, acc_ref = refs[len(widths):]
        k = pl.program_id(1)

        @pl.when(k == 0)
        def _():
            acc_ref[...] = jnp.zeros_like(acc_ref)

        a_v = a_ref[...]
        c0 = 0
        for p_ref, w in zip(refs, widths):
            acc_ref[:, c0:c0 + w] += _dot_tn(a_v, p_ref[...].astype(BF16))
            c0 += w

        @pl.when(k == nk - 1)
        def _():
            o_ref[...] = acc_ref[...].astype(BF16)

    return pl.pallas_call(
        body, name=name, grid=(m // tm, nk),
        in_specs=[pl.BlockSpec((tk, tm), lambda i, k: (k, i))]
                 + [pl.BlockSpec((tk, w), lambda i, k: (k, 0)) for w in widths],
        out_specs=pl.BlockSpec((tm, n), lambda i, k: (i, 0)),
        out_shape=jax.ShapeDtypeStruct((m, n), BF16),
        scratch_shapes=[pltpu.VMEM((tm, n), F32)],
        compiler_params=_params(("parallel", "arbitrary")),
    )(a, *pieces)


def _atb_shards(a, b, name, tk=512):
    a_sharded = a.ndim == 3
    n, _, w = a.shape if a_sharded else b.shape
    other = (b if a_sharded else a).shape[1]

    def body(a_ref, b_ref, o_ref):
        acc = jnp.zeros(o_ref.shape, F32)
        for k0 in range(0, TOK, tk):
            acc = acc + _dot_tn(a_ref[k0:k0 + tk, :], b_ref[k0:k0 + tk, :])
        o_ref[...] = acc.astype(BF16)

    shard = pl.BlockSpec((None, TOK, w), lambda r: (r, 0, 0))
    whole = _resident((TOK, other))
    if a_sharded:
        out_spec, out_shape = pl.BlockSpec((w, other), lambda r: (r, 0)), (n * w, other)
    else:
        out_spec, out_shape = pl.BlockSpec((None, other, w), lambda r: (r, 0, 0)), (n, other, w)
    return pl.pallas_call(
        body, name=name, grid=(n,),
        in_specs=[shard, whole] if a_sharded else [whole, shard],
        out_specs=out_spec, out_shape=jax.ShapeDtypeStruct(out_shape, BF16),
        compiler_params=_params(("parallel",)),
    )(a, b)


def _atb(a, b, name, tm, tn, col_blocks=0, tk=512):
    m, n = a.shape[1], b.shape[1]
    nk = TOK // tk

    def body(a_ref, b_ref, o_ref, acc_ref):
        k = pl.program_id(2)

        @pl.when(k == 0)
        def _():
            acc_ref[...] = jnp.zeros_like(acc_ref)

        acc_ref[...] += _dot_tn(a_ref[...].astype(BF16), b_ref[...].astype(BF16))

        @pl.when(k == nk - 1)
        def _():
            if col_blocks:
                width = n // col_blocks
                for blk in range(col_blocks):
                    o_ref[blk] = acc_ref[:, blk * width:(blk + 1) * width].astype(BF16)
            else:
                o_ref[...] = acc_ref[...].astype(BF16)

    if col_blocks:
        out_spec = pl.BlockSpec((col_blocks, tm, n // col_blocks), lambda i, j, k: (0, i, 0))
        out_shape = jax.ShapeDtypeStruct((col_blocks, m, n // col_blocks), BF16)
    else:
        out_spec = pl.BlockSpec((tm, tn), lambda i, j, k: (i, j))
        out_shape = jax.ShapeDtypeStruct((m, n), BF16)
    return pl.pallas_call(
        body, name=name, grid=(m // tm, n // tn, nk),
        in_specs=[pl.BlockSpec((tk, tm), lambda i, j, k: (k, i)),
                  pl.BlockSpec((tk, tn), lambda i, j, k: (k, j))],
        out_specs=out_spec, out_shape=out_shape,
        scratch_shapes=[pltpu.VMEM((tm, tn), F32)],
        compiler_params=_params(("parallel", "parallel", "arbitrary")),
    )(a, b)


SB_PAIRS = SB_WIDTH // LANES


def _two_heads(v, lane0):
    zero = jnp.zeros_like(v)
    return jnp.where(lane0, v, zero), jnp.where(lane0, zero, v)


SB_QBLK = 256
N_SB_STEPS = SEQ // SB_QBLK


SB_KCHUNK = 2 * BLK
SB_ROWS = 2 * SB_QBLK
SB_DEAD = -104.0


def _log_keep(z):
    neg_z = -z
    return jnp.minimum(neg_z, 0.0) - jnp.log(1.0 + jnp.exp(jnp.minimum(z, neg_z)))


def _stack_heads(v, lane0):
    return jnp.concatenate(_two_heads(v, lane0), axis=0)


def _block_sums(v, tri, split=True):
    halves = (v[:, :BLK], v[:, BLK:])
    stacked = jnp.concatenate(halves, axis=0)
    if split:
        hi, lo = _split_bf16(stacked)
        prod = _dot(jnp.concatenate([hi, lo], axis=0), tri)
        tri_sum = prod[:2 * SB_ROWS] + prod[2 * SB_ROWS:]
    else:
        tri_sum = _dot(stacked.astype(BF16), tri)
    sums = tuple(jnp.sum(h, axis=1, keepdims=True) for h in halves)
    return (tri_sum[:SB_ROWS], tri_sum[SB_ROWS:]), sums


def _sb_diag_mask():
    row = lax.broadcasted_iota(jnp.int32, (SB_ROWS, SB_KCHUNK), 0)
    col = lax.broadcasted_iota(jnp.int32, (SB_ROWS, SB_KCHUNK), 1)
    return col < jnp.where(row >= SB_QBLK, row - SB_QBLK, row)


def _sb_fwd(qkv, send=None):
    def body(q_ref, k_ref, v_ref, o_ref):
        i = pl.program_id(2)
        krow = lax.broadcasted_iota(jnp.int32, (BLK, BLK), 0)
        kcol = lax.broadcasted_iota(jnp.int32, (BLK, BLK), 1)
        later = (krow > kcol).astype(BF16)
        lane0 = lax.broadcasted_iota(jnp.int32, (SB_QBLK, LANES), 1) < HEAD_DIM
        q2 = _stack_heads(q_ref[0] * QK_SCALE, lane0)

        def chunk(c, carry, causal):
            acc, run = carry
            off = pl.multiple_of(c * SB_KCHUNK, SB_KCHUNK)
            z = _dot_nt(q2, k_ref[0, pl.ds(off, SB_KCHUNK), :])
            log_keep = _log_keep(z)
            if causal is not None:
                log_keep = jnp.where(causal, log_keep, 0.0)
            suffix, sums = _block_sums(log_keep, later)
            log_after = jnp.concatenate([suffix[0] + (run + sums[1]), suffix[1] + run], axis=1)
            a = jnp.exp(log_keep + z + log_after)
            if causal is not None:
                a = jnp.where(causal, a, 0.0)
            acc = acc + _dot(a.astype(BF16), v_ref[0, pl.ds(off, SB_KCHUNK), :])
            return acc, run + (sums[0] + sums[1])

        acc, run = chunk(i, (jnp.zeros((SB_ROWS, LANES), F32), jnp.zeros((SB_ROWS, 1), F32)), _sb_diag_mask())

        def live(state):
            t, _, run = state
            return jnp.logical_and(t < i, jnp.max(run) > SB_DEAD)

        def trip(state):
            t, acc, run = state
            acc, run = chunk(i - 1 - t, (acc, run), None)
            return t + 1, acc, run

        _, acc, _ = lax.while_loop(live, trip, (jnp.int32(0), acc, run))
        o_ref[0] = jnp.where(lane0, acc[:SB_QBLK], acc[SB_QBLK:]).astype(BF16)

    blk = pl.BlockSpec((1, SB_QBLK, LANES), lambda b, h, i: (b, i, h))
    return _call(
        body, send, name="sb_fwd", grid=(B_LOC, SB_PAIRS, N_SB_STEPS),
        in_specs=[blk,
                  pl.BlockSpec((1, SEQ, LANES), lambda b, h, i: (b, 0, SB_PAIRS + h)),
                  pl.BlockSpec((1, SEQ, LANES), lambda b, h, i: (b, 0, 2 * SB_PAIRS + h))],
        out_specs=[blk], out_shape=[jax.ShapeDtypeStruct((B_LOC, SEQ, SB_WIDTH), BF16)],
        scratch_shapes=[], semantics=("parallel", "parallel", "arbitrary"), operands=(qkv, qkv, qkv))


def _sb_bwd(qkv, d_o, send=None):
    def body(q_ref, k_ref, v_ref, do_ref, dq_ref, dk_ref, dv_ref, dk_acc, dv_acc, z_scr, keep_scr):
        i = pl.program_id(2)
        krow = lax.broadcasted_iota(jnp.int32, (BLK, BLK), 0)
        kcol = lax.broadcasted_iota(jnp.int32, (BLK, BLK), 1)
        upto = (krow <= kcol).astype(BF16)
        earlier = (krow < kcol).astype(BF16)
        lane0 = lax.broadcasted_iota(jnp.int32, (SB_QBLK, LANES), 1) < HEAD_DIM
        q2 = _stack_heads(q_ref[0] * QK_SCALE, lane0)
        do2 = _stack_heads(do_ref[0], lane0)

        def keep_sum(c, causal):
            off = pl.multiple_of(c * SB_KCHUNK, SB_KCHUNK)
            z = _dot_nt(q2, k_ref[0, pl.ds(off, SB_KCHUNK), :])
            log_keep = _log_keep(z)
            if causal is not None:
                log_keep = jnp.where(causal, log_keep, 0.0)
            z_scr[c] = z
            keep_scr[c] = log_keep
            return jnp.sum(log_keep, axis=1, keepdims=True)

        def live(state):
            t, run = state
            return jnp.logical_and(t < i, jnp.max(run) > SB_DEAD)

        walked, tot2 = lax.while_loop(live, lambda s: (s[0] + 1, s[1] + keep_sum(i - 1 - s[0], None)),
                                      (jnp.int32(0), keep_sum(i, _sb_diag_mask())))
        first = i - walked

        @pl.when(i == 0)
        def _():
            dk_acc[...] = jnp.zeros_like(dk_acc)
            dv_acc[...] = jnp.zeros_like(dv_acc)

        def chunk(c, carry, causal):
            dq, pre_keep, pre_e = carry
            off = pl.multiple_of(c * SB_KCHUNK, SB_KCHUNK)
            k_c = k_ref[0, pl.ds(off, SB_KCHUNK), :]
            v_c = v_ref[0, pl.ds(off, SB_KCHUNK), :]
            d_a = _dot_nt(do2, v_c)
            log_keep = keep_scr[c]
            log_beta = log_keep + z_scr[c]
            prefix, sums = _block_sums(log_keep, upto)
            inclusive = jnp.concatenate([prefix[0], prefix[1] + sums[0]], axis=1)
            a = jnp.exp(log_beta + ((tot2 - pre_keep) - inclusive))
            if causal is not None:
                a = jnp.where(causal, a, 0.0)
            e = d_a * a
            e_prefix, e_sums = _block_sums(e, earlier, split=False)
            before = jnp.concatenate([e_prefix[0] + pre_e, e_prefix[1] + (pre_e + e_sums[0])], axis=1)
            dz = e - (e + before) * jnp.exp(log_beta)
            if causal is not None:
                dz = jnp.where(causal, dz, 0.0)
            dz = dz.astype(BF16)
            dq = dq + _dot(dz, k_c)
            dk_acc[pl.ds(off, SB_KCHUNK), :] += _dot_tn(dz, q2)
            dv_acc[pl.ds(off, SB_KCHUNK), :] += _dot_tn(a.astype(BF16), do2)
            return dq, pre_keep + (sums[0] + sums[1]), pre_e + (e_sums[0] + e_sums[1])

        zero_col = jnp.zeros((SB_ROWS, 1), F32)
        carry = lax.fori_loop(first, i, lambda t, c: chunk(t, c, None),
                              (jnp.zeros((SB_ROWS, LANES), F32), zero_col, zero_col))
        dq, _, _ = chunk(i, carry, _sb_diag_mask())
        dq_ref[0] = (jnp.where(lane0, dq[:SB_QBLK], dq[SB_QBLK:]) * QK_SCALE).astype(BF16)

        @pl.when(i == N_SB_STEPS - 1)
        def _():
            dk_ref[0] = dk_acc[...].astype(BF16)
            dv_ref[0] = dv_acc[...].astype(BF16)

    blk = pl.BlockSpec((1, SB_QBLK, LANES), lambda b, h, i: (b, i, h))
    whole = lambda c: pl.BlockSpec((1, SEQ, LANES), lambda b, h, i: (b, 0, c * SB_PAIRS + h))
    out = jax.ShapeDtypeStruct((B_LOC, SEQ, SB_WIDTH), BF16)
    return _call(
        body, send, name="sb_bwd", grid=(B_LOC, SB_PAIRS, N_SB_STEPS),
        in_specs=[blk, whole(1), whole(2), blk],
        out_specs=[blk, whole(0), whole(0)],
        out_shape=[out, out, out],
        scratch_shapes=[pltpu.VMEM((SEQ, LANES), F32), pltpu.VMEM((SEQ, LANES), F32),
                        pltpu.VMEM((N_SB_STEPS, SB_ROWS, SB_KCHUNK), F32),
                        pltpu.VMEM((N_SB_STEPS, SB_ROWS, SB_KCHUNK), F32)],
        semantics=("parallel", "parallel", "arbitrary"), operands=(qkv, qkv, qkv, d_o))


DIL_GROUPS = len(DIL_PAIRS)
DIL_QBLOCKS = SEQ // BLK


def _residue_rows(j, dilation):
    length = SEQ // dilation
    return pl.ds(j, length, stride=dilation) if dilation > 1 else pl.ds(0, length)


def _gather_residues(src_ref, dst_ref, dst_off, dilation, scale=None):
    length = SEQ // dilation
    for j in range(dilation):
        v = src_ref[_residue_rows(j, dilation), :]
        if scale is not None:
            v = v * scale
        dst_ref[dst_off + j * length:dst_off + (j + 1) * length, :] = v.astype(dst_ref.dtype)


def _scatter_residues(src_ref, src_off, dst_ref, dilation):
    length = SEQ // dilation
    for j in range(dilation):
        dst_ref[_residue_rows(j, dilation), :] = (
            src_ref[src_off + j * length:src_off + (j + 1) * length, :].astype(dst_ref.dtype))


def _dil_geometry(group, pair):
    dilation = DIL_PAIRS[group][1]
    row = lax.broadcasted_iota(jnp.int32, (2 * BLK, 2 * BLK), 0)
    col = lax.broadcasted_iota(jnp.int32, (2 * BLK, 2 * BLK), 1)
    second = row >= BLK
    steps = BLK + jnp.where(second, row - BLK, row) - col
    coef = -ALIBI_MAX_BIAS / DIL_HEADS * math.log(2.0)
    first_head = float(4 * group + 1) + 2.0 * pair.astype(F32)
    slope = jnp.exp(coef * (first_head + jnp.where(second, 1.0, 0.0)))
    bias = slope * (steps * dilation).astype(F32)
    valid = jnp.logical_and(steps >= 0, steps <= BLK)
    return bias, valid, col >= BLK


def _dil_tile_scores(q2, kk, geometry, has_prev):
    bias, valid, own = geometry
    ok = jnp.logical_and(valid, jnp.logical_or(own, has_prev))
    return jnp.where(ok, _dot_nt(q2, kk) - bias, NEG_BIG)


def _head_col(v, lane_mask):
    return jnp.max(jnp.where(lane_mask, v, NEG_BIG), axis=1, keepdims=True)


def _dil_fwd(qkv, send=None):
    def body(*refs):
        ins, (o_ref, lse_ref), (qs, ks, vs, o_res, lse_res) = refs[:9], refs[9:11], refs[11:16]
        o_grp, lse_grp = refs[16:19], refs[19:22]
        pair = pl.program_id(1)
        lane0 = lax.broadcasted_iota(jnp.int32, (BLK, LANES), 1) < HEAD_DIM
        ks[0:BLK, :] = jnp.zeros((BLK, LANES), BF16)
        vs[0:BLK, :] = jnp.zeros((BLK, LANES), BF16)
        for grp, (_, dilation) in enumerate(DIL_PAIRS):
            q_ref, k_ref, v_ref = ins[3 * grp:3 * grp + 3]
            per_residue = DIL_QBLOCKS // dilation
            _gather_residues(q_ref, qs, 0, dilation, QK_SCALE)
            _gather_residues(k_ref, ks, BLK, dilation)
            _gather_residues(v_ref, vs, BLK, dilation)
            geometry = _dil_geometry(grp, pair)

            def step(blk, _):
                off = pl.multiple_of(blk * BLK, BLK)
                q2 = _stack_heads(qs[pl.ds(off, BLK), :], lane0)
                s = _dil_tile_scores(q2, ks[pl.ds(off, 2 * BLK), :], geometry, blk % per_residue != 0)
                m = jnp.max(s, axis=1, keepdims=True)
                p = jnp.exp(s - m)
                den = jnp.sum(p, axis=1, keepdims=True)
                out = _dot(p.astype(BF16), vs[pl.ds(off, 2 * BLK), :]) / den
                lse = m + jnp.log(den)
                o_res[pl.ds(off, BLK), :] = jnp.where(lane0, out[:BLK], out[BLK:])
                lse_res[pl.ds(off, BLK), :] = jnp.where(lane0, lse[:BLK], lse[BLK:])
                return 0

            lax.fori_loop(0, DIL_QBLOCKS, step, 0, unroll=8)
            _scatter_residues(o_res, 0, o_grp[grp], dilation)
            _scatter_residues(lse_res, 0, lse_grp[grp], dilation)

        for r0 in range(0, SEQ, 2 * BLK):
            rows = slice(r0, r0 + 2 * BLK)
            ls = [lse_grp[g][rows, :] for g in range(DIL_GROUPS)]
            m = jnp.maximum(jnp.maximum(ls[0], ls[1]), ls[2])
            w = [jnp.exp(l - m) for l in ls]
            den = w[0] + w[1] + w[2]
            o_ref[rows, :] = (w[0] * o_grp[0][rows, :] + w[1] * o_grp[1][rows, :] + w[2] * o_grp[2][rows, :]) / den
            lse_ref[rows, :] = m + jnp.log(den)

    def col(part, grp):
        return pl.BlockSpec((None, SEQ, LANES), lambda b, p: (b, 0, 6 * part + 2 * grp + p))

    out_spec = pl.BlockSpec((None, SEQ, LANES), lambda b, p: (b, 0, p))
    out = jax.ShapeDtypeStruct((B_LOC, SEQ, DIL_OUT), F32)
    return _call(
        body, send, name="dil_fwd", grid=(B_LOC, DIL_OUT // LANES),
        in_specs=[col(part, grp) for grp in range(DIL_GROUPS) for part in range(3)],
        out_specs=[out_spec, out_spec], out_shape=[out, out],
        scratch_shapes=[pltpu.VMEM((SEQ, LANES), BF16), pltpu.VMEM((SEQ + BLK, LANES), BF16),
                        pltpu.VMEM((SEQ + BLK, LANES), BF16), pltpu.VMEM((SEQ, LANES), F32),
                        pltpu.VMEM((SEQ, LANES), F32)] + [pltpu.VMEM((SEQ, LANES), F32)] * (2 * DIL_GROUPS),
        semantics=("parallel", "parallel"), operands=[qkv] * 9)


def _dil_bwd(qkv, d_o, lse, dsum, send=None):
    def body(*refs):
        ins, (do_ref, lse_ref, dsum_ref), outs = refs[:9], refs[9:12], refs[12:21]
        qs, ks, vs, dos, lse_res, dsum_res, dq_res, dk_acc, dv_acc = refs[21:]
        pair = pl.program_id(1)
        lane0 = lax.broadcasted_iota(jnp.int32, (BLK, LANES), 1) < HEAD_DIM
        lane1 = jnp.logical_not(lane0)
        ks[0:BLK, :] = jnp.zeros((BLK, LANES), BF16)
        vs[0:BLK, :] = jnp.zeros((BLK, LANES), BF16)
        for grp, (_, dilation) in enumerate(DIL_PAIRS):
            q_ref, k_ref, v_ref = ins[3 * grp:3 * grp + 3]
            dq_ref, dk_ref, dv_ref = outs[3 * grp:3 * grp + 3]
            per_residue = DIL_QBLOCKS // dilation
            _gather_residues(q_ref, qs, 0, dilation, QK_SCALE)
            _gather_residues(k_ref, ks, BLK, dilation)
            _gather_residues(v_ref, vs, BLK, dilation)
            _gather_residues(do_ref, dos, 0, dilation)
            _gather_residues(lse_ref, lse_res, 0, dilation)
            _gather_residues(dsum_ref, dsum_res, 0, dilation)
            dk_acc[...] = jnp.zeros_like(dk_acc)
            dv_acc[...] = jnp.zeros_like(dv_acc)
            geometry = _dil_geometry(grp, pair)

            def step(blk, _):
                off = pl.multiple_of(blk * BLK, BLK)
                q2 = _stack_heads(qs[pl.ds(off, BLK), :], lane0)
                do2 = _stack_heads(dos[pl.ds(off, BLK), :], lane0)
                kk = ks[pl.ds(off, 2 * BLK), :]
                vv = vs[pl.ds(off, 2 * BLK), :]
                lse_blk = lse_res[pl.ds(off, BLK), :]
                dsum_blk = dsum_res[pl.ds(off, BLK), :]
                lse2 = jnp.concatenate([_head_col(lse_blk, lane0), _head_col(lse_blk, lane1)], axis=0)
                dsum2 = jnp.concatenate([_head_col(dsum_blk, lane0), _head_col(dsum_blk, lane1)], axis=0)
                s = _dil_tile_scores(q2, kk, geometry, blk % per_residue != 0)
                p = jnp.exp(s - lse2)
                ds = (p * (_dot_nt(do2, vv) - dsum2)).astype(BF16)
                dq2 = _dot(ds, kk)
                dq_res[pl.ds(off, BLK), :] = jnp.where(lane0, dq2[:BLK], dq2[BLK:]) * QK_SCALE
                dk_acc[pl.ds(off, 2 * BLK), :] += _dot_tn(ds, q2)
                dv_acc[pl.ds(off, 2 * BLK), :] += _dot_tn(p.astype(BF16), do2)
                return 0

            lax.fori_loop(0, DIL_QBLOCKS, step, 0, unroll=8)
            _scatter_residues(dq_res, 0, dq_ref, dilation)
            _scatter_residues(dk_acc, BLK, dk_ref, dilation)
            _scatter_residues(dv_acc, BLK, dv_ref, dilation)

    def col(part, grp):
        return pl.BlockSpec((None, SEQ, LANES), lambda b, p: (b, 0, 6 * part + 2 * grp + p))

    slot = pl.BlockSpec((None, SEQ, LANES), lambda b, p: (b, 0, p))
    out = jax.ShapeDtypeStruct((B_LOC, SEQ, DIL_OUT), F32)
    return _call(
        body, send, name="dil_bwd", grid=(B_LOC, DIL_OUT // LANES),
        in_specs=[col(part, grp) for grp in range(DIL_GROUPS) for part in range(3)] + [slot] * 3,
        out_specs=[slot] * 9, out_shape=[out] * 9,
        scratch_shapes=[pltpu.VMEM((SEQ, LANES), BF16), pltpu.VMEM((SEQ + BLK, LANES), BF16),
                        pltpu.VMEM((SEQ + BLK, LANES), BF16), pltpu.VMEM((SEQ, LANES), BF16),
                        pltpu.VMEM((SEQ, LANES), F32), pltpu.VMEM((SEQ, LANES), F32),
                        pltpu.VMEM((SEQ, LANES), F32), pltpu.VMEM((SEQ + BLK, LANES), F32),
                        pltpu.VMEM((SEQ + BLK, LANES), F32)],
        semantics=("parallel", "parallel"), operands=[qkv] * 9 + [d_o, lse, dsum])


def _peers():
    x, y, c = lax.axis_index("x"), lax.axis_index("y"), lax.axis_index("c")
    me = 4 * x + 2 * y + c
    peers = []
    for mask in range(1, N_DEV):
        px = 1 - x if mask & 4 else x
        py = 1 - y if mask & 2 else y
        pc = 1 - c if mask & 1 else c
        peers.append(((px, py, pc), 4 * px + 2 * py + pc))
    return me, peers


def _all_gather(shard, name):
    def body(src_ref, out_ref, send_sems, recv_sems, local_sem):
        x, y, c = lax.axis_index("x"), lax.axis_index("y"), lax.axis_index("c")
        sibling = (x, y, 1 - c)
        chips = [(1 - x, y), (x, 1 - y), (1 - x, 1 - y)]

        def slot(px, py, pc):
            return out_ref.at[4 * px + 2 * py + pc]

        def copy(k, block, to, src=None):
            return pltpu.make_async_remote_copy(
                src_ref=slot(*block) if src is None else src, dst_ref=slot(*block),
                send_sem=send_sems.at[k], recv_sem=recv_sems.at[k], device_id=to,
                device_id_type=pl.DeviceIdType.MESH)

        mine = pltpu.make_async_copy(src_ref, slot(x, y, c), local_sem)
        mine.start()
        first = [copy(0, (x, y, c), sibling, src=src_ref)]
        first += [copy(1 + j, (x, y, c), (*chip, c), src=src_ref) for j, chip in enumerate(chips)]
        for cp in first:
            cp.start()
        passed = [copy(4 + j, (*chip, c), sibling) for j, chip in enumerate(chips)]
        for j, chip in enumerate(chips):
            copy(1 + j, (*chip, c), (x, y, c)).wait_recv()
            passed[j].start()
        copy(0, sibling, (x, y, c)).wait_recv()
        for j, chip in enumerate(chips):
            copy(4 + j, (*chip, 1 - c), (x, y, c)).wait_recv()
        for cp in first + passed:
            cp.wait_send()
        mine.wait()

    return pl.pallas_call(
        body, name=name,
        in_specs=[pl.BlockSpec(memory_space=pl.ANY)],
        out_specs=pl.BlockSpec(memory_space=pl.ANY),
        out_shape=jax.ShapeDtypeStruct((N_DEV,) + shard.shape, shard.dtype),
        scratch_shapes=[pltpu.SemaphoreType.DMA((N_DEV - 1,)), pltpu.SemaphoreType.DMA((N_DEV - 1,)),
                        pltpu.SemaphoreType.DMA],
    )(shard)


def _call(body, send, *, name, grid, in_specs, out_specs, out_shape, scratch_shapes, semantics, operands):
    if send is None:
        return pl.pallas_call(
            body, name=name, grid=grid, in_specs=in_specs, out_specs=out_specs, out_shape=out_shape,
            scratch_shapes=scratch_shapes, compiler_params=_params(semantics))(*operands), []
    srcs, kinds = [s for s, _ in send], [k for _, k in send]
    n, n_in, n_out, n_scr = len(srcs), len(in_specs), len(out_specs), len(scratch_shapes)
    steps = math.prod(grid)
    relay_step = (13 * steps) // 16

    def plan(refs):
        src_refs, land_refs = refs[n_in:n_in + n], refs[n_in + n + n_out:n_in + 2 * n + n_out]
        send_sems, recv_sems, local_sems = refs[-3:]
        x, y, c = lax.axis_index("x"), lax.axis_index("y"), lax.axis_index("c")
        me, peers = _peers()
        first, relayed_in, relayed_out, arrivals, sends, own = [], [], [], [], [], []
        for a, kind in enumerate(kinds):
            def copy(k, src, dst_slot, to):
                return pltpu.make_async_remote_copy(
                    src_ref=src, dst_ref=land_refs[a].at[dst_slot], send_sem=send_sems.at[a * (N_DEV - 1) + k],
                    recv_sem=recv_sems.at[a * (N_DEV - 1) + k], device_id=to, device_id_type=pl.DeviceIdType.MESH)

            if kind == "gather_by_chip":
                idx = lambda px, py, pc: 4 * px + 2 * py + pc
                chips = [(1 - x, y), (x, 1 - y), (1 - x, 1 - y)]
                mine = [copy(0, src_refs[a], me, (x, y, 1 - c))]
                arrivals.append(copy(0, src_refs[a], idx(x, y, 1 - c), (x, y, 1 - c)))
                for j, (px, py) in enumerate(chips):
                    mine.append(copy(1 + j, src_refs[a], me, (px, py, c)))
                    relayed_in.append(copy(1 + j, src_refs[a], idx(px, py, c), (px, py, c)))
                    relayed_out.append(copy(4 + j, land_refs[a].at[idx(px, py, c)], idx(px, py, c), (x, y, 1 - c)))
                    arrivals.append(copy(4 + j, src_refs[a], idx(px, py, 1 - c), (x, y, 1 - c)))
                first += mine
                sends += mine + relayed_out[-3:]
                own.append(pltpu.make_async_copy(src_refs[a], land_refs[a].at[me], local_sems.at[a]))
            elif kind == "scatter_by_chip":
                for k, (px, py) in enumerate([(1 - x, y), (x, 1 - y), (1 - x, 1 - y)]):
                    first.append(copy(k, src_refs[a].at[2 * px + py], 2 * x + y, (px, py, c)))
                    arrivals.append(copy(k, src_refs[a].at[2 * px + py], 2 * px + py, (px, py, c)))
                sends += first[-3:]
                own.append(pltpu.make_async_copy(src_refs[a].at[2 * x + y], land_refs[a].at[2 * x + y],
                                                 local_sems.at[a]))
            else:
                part = (lambda i: src_refs[a].at[i]) if kind == "scatter" else (lambda i: src_refs[a])
                for k, (peer, peer_idx) in enumerate(peers):
                    first.append(copy(k, part(peer_idx), me, peer))
                    arrivals.append(copy(k, part(peer_idx), peer_idx, peer))
                sends += first[-(N_DEV - 1):]
                own.append(pltpu.make_async_copy(part(me), land_refs[a].at[me], local_sems.at[a]))
        return first, relayed_in, relayed_out, arrivals, sends, own

    def wrapped(*refs):
        step = 0
        for axis, size in enumerate(grid):
            step = step * size + pl.program_id(axis)

        @pl.when(step == 0)
        def _():
            first, _, _, _, _, own = plan(refs)
            for cp in first + own:
                cp.start()

        if "gather_by_chip" in kinds:
            @pl.when(step == relay_step)
            def _():
                _, relayed_in, relayed_out, _, _, _ = plan(refs)
                for cp_in, cp_out in zip(relayed_in, relayed_out):
                    cp_in.wait_recv()
                    cp_out.start()

        body(*refs[:n_in], *refs[n_in + n:n_in + n + n_out], *refs[n_in + 2 * n + n_out:n_in + 2 * n + n_out + n_scr])

        @pl.when(step == steps - 1)
        def _():
            _, _, _, arrivals, sends, own = plan(refs)
            for cp in arrivals:
                cp.wait_recv()
            for cp in sends:
                cp.wait_send()
            for cp in own:
                cp.wait()

    anywhere = pl.BlockSpec(memory_space=pl.ANY)
    lands = [jax.ShapeDtypeStruct((N_DEV // 2 if k == "scatter_by_chip" else N_DEV,) + s.shape[-2:], s.dtype)
             for s, k in send]
    out = pl.pallas_call(
        wrapped, name=name, grid=grid,
        in_specs=list(in_specs) + [anywhere] * n, out_specs=list(out_specs) + [anywhere] * n,
        out_shape=list(out_shape) + lands,
        scratch_shapes=list(scratch_shapes) + [pltpu.SemaphoreType.DMA((n * (N_DEV - 1),)),
                                               pltpu.SemaphoreType.DMA((n * (N_DEV - 1),)),
                                               pltpu.SemaphoreType.DMA((n,))],
        compiler_params=_params(("arbitrary",) * len(grid)),
    )(*operands, *srcs)
    return out[:n_out], list(out[n_out:])


def _pair_swap(blocks):
    def body(src_ref, out_ref, send_sems, recv_sems):
        x, y, c = lax.axis_index("x"), lax.axis_index("y"), lax.axis_index("c")
        copies = [pltpu.make_async_remote_copy(
            src_ref=src_ref.at[2 * chip + (1 - c)], dst_ref=out_ref.at[chip], send_sem=send_sems.at[chip],
            recv_sem=recv_sems.at[chip], device_id=(x, y, 1 - c), device_id_type=pl.DeviceIdType.MESH)
            for chip in range(N_DEV // 2)]
        for cp in copies:
            cp.start()
        for cp in copies:
            cp.wait()

    return pl.pallas_call(
        body, name="pair_swap_grad_w_in",
        in_specs=[pl.BlockSpec(memory_space=pl.ANY)], out_specs=pl.BlockSpec(memory_space=pl.ANY),
        out_shape=jax.ShapeDtypeStruct((N_DEV // 2,) + blocks.shape[1:], blocks.dtype),
        scratch_shapes=[pltpu.SemaphoreType.DMA((N_DEV // 2,)), pltpu.SemaphoreType.DMA((N_DEV // 2,))],
    )(blocks)


def _pair_sum(blocks, swapped, core):
    _, rows, cols = swapped.shape
    tile_rows = _row_tile(rows)

    def body(core_ref, mine_ref, theirs_ref, o_ref):
        o_ref[...] = (mine_ref[...].astype(F32) + theirs_ref[...].astype(F32)).astype(o_ref.dtype)

    return pl.pallas_call(
        body, name="pair_sum_grad_w_in",
        grid_spec=pltpu.PrefetchScalarGridSpec(
            num_scalar_prefetch=1, grid=(N_DEV // 2, rows // tile_rows),
            in_specs=[pl.BlockSpec((None, tile_rows, cols), lambda j, i, core_ref: (2 * j + core_ref[0], i, 0)),
                      pl.BlockSpec((None, tile_rows, cols), lambda j, i, core_ref: (j, i, 0))],
            out_specs=pl.BlockSpec((None, tile_rows, cols), lambda j, i, core_ref: (j, i, 0))),
        out_shape=jax.ShapeDtypeStruct(swapped.shape, swapped.dtype),
        compiler_params=_params(("parallel", "parallel")),
    )(core, blocks, swapped)


def _sum_in_device_order(land_ref):
    acc = land_ref[0].astype(F32)
    for j in range(1, land_ref.shape[0]):
        acc = acc + land_ref[j].astype(F32)
    return acc


def _adam_math(w, g, m, v):
    c1 = 1.0 - ADAM_B1 ** ADAM_STEP
    c2 = 1.0 - ADAM_B2 ** ADAM_STEP
    m_new = ADAM_B1 * m + (1.0 - ADAM_B1) * g
    v_new = ADAM_B2 * v + (1.0 - ADAM_B2) * (g * g)
    delta = -ADAM_LR * ((m_new / c1) / (jnp.sqrt(v_new / c2) + ADAM_EPS) + ADAM_WD * w)
    return delta, m_new, v_new


def _row_tile(rows):
    return max(t for t in range(8, 257, 8) if rows % t == 0) if rows % 8 == 0 else rows


def _sum_update(land, w, m, v, name):
    slots, rows, cols = land.shape
    tile_rows = _row_tile(rows)

    def body(land_ref, w_ref, m_ref, v_ref, g_ref, d_ref, nm_ref, nv_ref):
        g = _sum_in_device_order(land_ref)
        g_ref[...] = g
        d_ref[...], nm_ref[...], nv_ref[...] = _adam_math(w_ref[...], g, m_ref[...], v_ref[...])

    tile = pl.BlockSpec((None, tile_rows, cols), lambda i: (0, i, 0))
    out = jax.ShapeDtypeStruct((1, rows, cols), F32)
    return pl.pallas_call(
        body, name=name, grid=(rows // tile_rows,),
        in_specs=[pl.BlockSpec((slots, tile_rows, cols), lambda i: (0, i, 0)), tile, tile, tile],
        out_specs=[tile] * 4, out_shape=[out] * 4,
        compiler_params=_params(("parallel",)),
    )(land, w, m, v)


def _sum_gains(land):
    def body(land_ref, o_ref):
        o_ref[...] = _sum_in_device_order(land_ref)

    return pl.pallas_call(
        body, name="sum_gain_grads", grid=(1,),
        in_specs=[pl.BlockSpec(land.shape, lambda i: (0, 0, 0))],
        out_specs=pl.BlockSpec(land.shape[1:], lambda i: (0, 0)),
        out_shape=jax.ShapeDtypeStruct(land.shape[1:], F32),
    )(land)


def _adamw(w, g, m, v, name):
    def body(w_ref, g_ref, m_ref, v_ref, d_ref, nm_ref, nv_ref):
        d_ref[...], nm_ref[...], nv_ref[...] = _adam_math(w_ref[...], g_ref[...], m_ref[...], v_ref[...])

    whole = pl.BlockSpec(w.shape, lambda i: (0, 0))
    out = jax.ShapeDtypeStruct(w.shape, F32)
    return pl.pallas_call(
        body, name=name, grid=(1,),
        in_specs=[whole] * 4, out_specs=[whole] * 3, out_shape=[out] * 3,
    )(w, g, m, v)


GROUP_FFN = ("w_ffn_in", "w_ffn_out")
GROUP_MIX = ("w_sb_up", "w_dil_up", "w_out")
COL_SHARDED = ("w_in", "w_sb_up", "w_dil_up", "w_ffn_in")


def _full_from_shards(name, slots):
    _, r, c = slots.shape
    if name in COL_SHARDED:
        return slots.transpose(1, 0, 2).reshape(r, N_DEV * c)
    return slots.reshape(N_DEV * r, c)


def _shards_from_full(name, full):
    rows, cols = full.shape
    if name in COL_SHARDED:
        return full.reshape(rows, N_DEV, cols // N_DEV).transpose(1, 0, 2)
    return full.reshape(N_DEV, rows // N_DEV, cols)


def _local_step(x, target, g_mix, g_ffn, g_fin, w_in, shards=None, rest=None):
    gather = lambda names, kind: None if shards is None else [(shards[n], kind) for n in names]
    scatter = lambda blocks: None if shards is None else [(t, "scatter") for t in blocks]
    landed = lambda blocks, lands: lands if lands else blocks

    w = {"w_in": w_in}
    if shards is None:
        w.update(rest)
        w["w_ffn_in"] = _shards_from_full("w_ffn_in", rest["w_ffn_in"])
    (qkv_sb, qkv_dl, gates, u), _ = _norm_proj(x, g_mix, w["w_in"])
    qkv_sb = qkv_sb.reshape(B_LOC, SEQ, 3 * SB_WIDTH)
    qkv_dl = qkv_dl.reshape(B_LOC, SEQ, 3 * DIL_WIDTH)
    (o_sb,), lands = _sb_fwd(qkv_sb, gather(GROUP_FFN, "gather_by_chip"))
    if lands:
        w["w_ffn_in"], w["w_ffn_out"] = lands[0], _full_from_shards("w_ffn_out", lands[1])
    o_sb = o_sb.reshape(TOK, SB_WIDTH)
    (o_dl, lse), lands = _dil_fwd(qkv_dl, gather(GROUP_MIX, "gather"))
    w.update({n: _full_from_shards(n, t) for n, t in zip(GROUP_MIX, lands)})
    o_dl = o_dl.reshape(TOK, DIL_OUT)

    x1, merged = _mix_out(x, o_sb, o_dl, gates, w["w_sb_up"], w["w_dil_up"], w["w_out"])
    loss, dx1, u2, act, dh, dx2, dg_fin, dg_ffn = _ffn_fwd_bwd(x1, target, g_ffn, g_fin, w["w_ffn_in"], w["w_ffn_out"])
    dgates, dy_sb, dy_dl, do_sb, do_dl, dsum = _mix_bwd(dx1, o_sb, o_dl, gates, w["w_sb_up"], w["w_dil_up"], w["w_out"])
    blocks = {
        "w_sb_up": _atb(o_sb, dy_sb, "grad_w_sb_up", SB_WIDTH, D_MODEL, col_blocks=N_DEV),
        "w_dil_up": _atb(o_dl, dy_dl, "grad_w_dil_up", DIL_OUT, D_MODEL, col_blocks=N_DEV),
        "w_out": _shards_from_full("w_out", _atb(merged, dx1, "grad_w_out", D_MODEL, D_MODEL)),
        "w_ffn_in": _atb_shards(u2, dh, "grad_w_ffn_in"),
        "w_ffn_out": _shards_from_full("w_ffn_out", _atb_shards(act, dx2, "grad_w_ffn_out")),
    }
    grads = {}

    early, late = ("w_ffn_in",), ("w_ffn_out",) + GROUP_MIX
    early_blocks = [blocks[n] for n in early]
    (dq_sb, dk_sb, dv_sb), lands = _sb_bwd(qkv_sb, do_sb.reshape(B_LOC, SEQ, SB_WIDTH), scatter(early_blocks))
    grads.update(zip(early, landed(early_blocks, lands)))
    as_batch = lambda t: t.reshape(B_LOC, SEQ, DIL_OUT)
    late_blocks = [blocks[n] for n in late]
    d_dl, lands = _dil_bwd(qkv_dl, as_batch(do_dl), lse, as_batch(dsum), scatter(late_blocks))
    grads.update(zip(late, landed(late_blocks, lands)))
    flat = lambda t: t.reshape(TOK, -1)
    dproj = ([flat(dq_sb), flat(dk_sb), flat(dv_sb)]
             + [flat(d_dl[3 * grp + part]) for part in range(3) for grp in range(DIL_GROUPS)] + [dgates])

    w_in_blocks = _shards_from_full("w_in", _atb_pieces(u, dproj, "grad_w_in", D_MODEL // 2))
    if shards is None:
        grads["w_in"] = w_in_blocks
        send = None
    else:
        core = lax.axis_index("c").astype(jnp.int32).reshape(1)
        send = [(_pair_sum(w_in_blocks, _pair_swap(w_in_blocks), core), "scatter_by_chip")]
    (grad_x, dg_mix), lands = _proj_bwd(dproj, dx1, x, g_mix, w["w_in"], send)
    if lands:
        grads["w_in"] = lands[0]
    gain_grads = jnp.concatenate([dg_mix, dg_ffn, dg_fin], axis=0)
    return loss, grad_x, gain_grads, grads


def kernel(x, norm_mix_g, w_in, w_sb_up, w_dil_up, w_out, norm_ffn_g, w_ffn_in, w_ffn_out, norm_final_g, loss_target, m_norm_mix_g, m_w_in, m_w_sb_up, m_w_dil_up, m_w_out, m_norm_ffn_g, m_w_ffn_in, m_w_ffn_out, m_norm_final_g, v_norm_mix_g, v_w_in, v_w_sb_up, v_w_dil_up, v_w_out, v_norm_ffn_g, v_w_ffn_in, v_w_ffn_out, v_norm_final_g):
    mats = {"w_in": w_in, "w_sb_up": w_sb_up, "w_dil_up": w_dil_up, "w_out": w_out,
            "w_ffn_in": w_ffn_in, "w_ffn_out": w_ffn_out}
    moments_m = {"w_in": m_w_in, "w_sb_up": m_w_sb_up, "w_dil_up": m_w_dil_up, "w_out": m_w_out,
                 "w_ffn_in": m_w_ffn_in, "w_ffn_out": m_w_ffn_out}
    moments_v = {"w_in": v_w_in, "w_sb_up": v_w_sb_up, "w_dil_up": v_w_dil_up, "w_out": v_w_out,
                 "w_ffn_in": v_w_ffn_in, "w_ffn_out": v_w_ffn_out}
    gathered_w_in = _all_gather(w_in[0].astype(BF16), "all_gather_w_in")
    g_fin = norm_final_g.reshape(1, D_MODEL)
    loss, grad_x, gain_grads, grad_slots = _local_step(
        x.reshape(TOK, D_MODEL), loss_target.reshape(TOK, D_MODEL), norm_mix_g, norm_ffn_g, g_fin,
        _full_from_shards("w_in", gathered_w_in),
        shards={name: mats[name][0].astype(BF16) for name in GROUP_FFN + GROUP_MIX})

    gain_rows = jnp.concatenate([gain_grads, jnp.tile(loss, (1, D_MODEL // LANES)),
                                 jnp.zeros((8 - 4, D_MODEL), F32)], axis=0)
    g_gains = _sum_gains(_all_gather(gain_rows, "all_gather_gains"))

    out_g, out_d, out_m, out_v = {}, {}, {}, {}
    for name, slots in grad_slots.items():
        out_g[name], out_d[name], out_m[name], out_v[name] = _sum_update(
            slots, mats[name], moments_m[name], moments_v[name], "update_" + name)

    gain_w = jnp.concatenate([norm_mix_g, norm_ffn_g, g_fin], axis=0)
    gain_m = jnp.concatenate([m_norm_mix_g, m_norm_ffn_g, m_norm_final_g.reshape(1, D_MODEL)], axis=0)
    gain_v = jnp.concatenate([v_norm_mix_g, v_norm_ffn_g, v_norm_final_g.reshape(1, D_MODEL)], axis=0)
    gd, gm, gv = _adamw(gain_w, g_gains[:3], gain_m, gain_v, "adamw_gains")
    for idx, name in enumerate(("norm_mix_g", "norm_ffn_g", "norm_final_g")):
        shape = (D_MODEL,) if name == "norm_final_g" else (1, D_MODEL)
        out_g[name] = g_gains[idx].reshape(shape)
        out_d[name], out_m[name], out_v[name] = gd[idx].reshape(shape), gm[idx].reshape(shape), gv[idx].reshape(shape)

    order = ("norm_mix_g", "w_in", "w_sb_up", "w_dil_up", "w_out", "norm_ffn_g", "w_ffn_in", "w_ffn_out",
             "norm_final_g")
    return (g_gains[3, 0], grad_x.reshape(B_LOC, SEQ, D_MODEL),
            *[out_g[n] for n in order], *[out_d[n] for n in order],
            *[out_m[n] for n in order], *[out_v[n] for n in order])
```

```python
import math

import jax
import jax.numpy as jnp
from jax import lax
from jax.experimental import pallas as pl
from jax.experimental.pallas import tpu as pltpu

F32 = jnp.float32
BF16 = jnp.bfloat16

N_DEV = 8
D_MODEL = 1024
SEQ = 2048
B_LOC = 2
TOK = B_LOC * SEQ
HEAD_DIM = 64
SB_WIDTH = 512
DIL_WIDTH = 768
DIL_OUT = 256
QKV_WIDTH = 3 * SB_WIDTH + 3 * DIL_WIDTH
IN_WIDTH = QKV_WIDTH + 2 * D_MODEL
D_FF = 2816
DIL_PAIRS = ((128, 1), (512, 4), (2048, 16))
DIL_HEADS = 12
RMS_EPS = 1e-6
ALIBI_MAX_BIAS = 8.0
QK_SCALE = 1.0 / math.sqrt(HEAD_DIM)
BLK = 128
LANES = 128
NEG_BIG = -1e30

ADAM_LR = 0.001
ADAM_B1 = 0.9
ADAM_B2 = 0.999
ADAM_EPS = 1e-08
ADAM_WD = 0.01
ADAM_STEP = 10

VMEM_LIMIT = 56 * 1024 * 1024


def _dot(a, b):
    return jnp.dot(a, b, preferred_element_type=F32)


def _dot_nt(a, b):
    return lax.dot_general(a, b, (((1,), (1,)), ((), ())), preferred_element_type=F32)


def _dot_tn(a, b):
    return lax.dot_general(a, b, (((0,), (0,)), ((), ())), preferred_element_type=F32)


def _sigmoid(z):
    return 1.0 / (1.0 + jnp.exp(-z))


def _split_bf16(v):
    hi = v.astype(BF16)
    lo = (v - hi.astype(F32)).astype(BF16)
    return hi, lo


def _chunks(width, step=512):
    out, c = [], 0
    while c < width:
        w = min(step, width - c)
        out.append((c, w))
        c += w
    return out


def _resident(shape):
    nd = len(shape)
    return pl.BlockSpec(shape, lambda *_: (0,) * nd, pipeline_mode=pl.Buffered(1))


def _params(sem):
    return pltpu.CompilerParams(dimension_semantics=sem, vmem_limit_bytes=VMEM_LIMIT)


def _rms_fwd(x, g):
    r = lax.rsqrt(jnp.mean(x * x, axis=-1, keepdims=True) + RMS_EPS)
    n = x * r
    return n, r, n * g


def _rms_bwd(dy, n, r, g):
    dg = jnp.sum(dy * n, axis=0, keepdims=True)
    dn = dy * g
    dx = r * (dn - n * jnp.mean(dn * n, axis=-1, keepdims=True))
    return dx, dg


TM = 256


def _norm_proj(x, g, w_in, send=None):
    def body(x_ref, g_ref, w_ref, sb_ref, dl_ref, gate_ref, u_ref):
        _, _, u = _rms_fwd(x_ref[...], g_ref[...])
        u = u.astype(BF16)
        u_ref[...] = u
        for c0, w in _chunks(3 * SB_WIDTH):
            sb_ref[:, c0:c0 + w] = _dot(u, w_ref[:, c0:c0 + w]).astype(BF16)
        for c0, w in _chunks(3 * DIL_WIDTH):
            dl_ref[:, c0:c0 + w] = _dot(u, w_ref[:, 3 * SB_WIDTH + c0:3 * SB_WIDTH + c0 + w])
        for c0, w in _chunks(2 * D_MODEL):
            gate_ref[:, c0:c0 + w] = _dot(u, w_ref[:, QKV_WIDTH + c0:QKV_WIDTH + c0 + w])

    return _call(
        body, send, name="norm_proj", grid=(TOK // TM,),
        in_specs=[pl.BlockSpec((TM, D_MODEL), lambda i: (i, 0)), _resident((1, D_MODEL)),
                  _resident((D_MODEL, IN_WIDTH))],
        out_specs=[pl.BlockSpec((TM, 3 * SB_WIDTH), lambda i: (i, 0)),
                   pl.BlockSpec((TM, 3 * DIL_WIDTH), lambda i: (i, 0)),
                   pl.BlockSpec((TM, 2 * D_MODEL), lambda i: (i, 0)),
                   pl.BlockSpec((TM, D_MODEL), lambda i: (i, 0))],
        out_shape=[jax.ShapeDtypeStruct((TOK, 3 * SB_WIDTH), BF16),
                   jax.ShapeDtypeStruct((TOK, 3 * DIL_WIDTH), F32),
                   jax.ShapeDtypeStruct((TOK, 2 * D_MODEL), F32),
                   jax.ShapeDtypeStruct((TOK, D_MODEL), BF16)],
        scratch_shapes=[], semantics=("parallel",), operands=(x, g, w_in))


def _mix_out(x, o_sb, o_dl, gates, w_sb_up, w_dil_up, w_out):
    def body(x_ref, osb_ref, odl_ref, gate_ref, wsb_ref, wdl_ref, wout_ref, x1_ref, mg_ref):
        y_sb = _dot(osb_ref[...], wsb_ref[...])
        y_dl = _dot(odl_ref[...].astype(BF16), wdl_ref[...])
        merged = (_sigmoid(gate_ref[:, :D_MODEL]) * y_sb
                  + _sigmoid(gate_ref[:, D_MODEL:]) * y_dl).astype(BF16)
        mg_ref[...] = merged
        x1_ref[...] = x_ref[...] + _dot(merged, wout_ref[...])

    return pl.pallas_call(
        body, name="mix_out", grid=(TOK // TM,),
        in_specs=[pl.BlockSpec((TM, D_MODEL), lambda i: (i, 0)),
                  pl.BlockSpec((TM, SB_WIDTH), lambda i: (i, 0)),
                  pl.BlockSpec((TM, DIL_OUT), lambda i: (i, 0)),
                  pl.BlockSpec((TM, 2 * D_MODEL), lambda i: (i, 0)),
                  _resident((SB_WIDTH, D_MODEL)), _resident((DIL_OUT, D_MODEL)),
                  _resident((D_MODEL, D_MODEL))],
        out_specs=[pl.BlockSpec((TM, D_MODEL), lambda i: (i, 0)),
                   pl.BlockSpec((TM, D_MODEL), lambda i: (i, 0))],
        out_shape=[jax.ShapeDtypeStruct((TOK, D_MODEL), F32),
                   jax.ShapeDtypeStruct((TOK, D_MODEL), BF16)],
        compiler_params=_params(("parallel",)),
    )(x, o_sb, o_dl, gates, w_sb_up, w_dil_up, w_out)


FF_SHARD = 2 * D_FF // N_DEV
FF_PAIRS = N_DEV // 2


def _ffn_fwd_bwd(x1, target, g_ffn, g_fin, w_ffn_in, w_ffn_out):
    def body(x1_ref, t_ref, gffn_ref, gfin_ref, win_ref, wout_ref,
             loss_ref, dx1_ref, u2_ref, act_ref, dh_ref, dx2_ref, dgfin_ref, dgffn_ref, h_scr):
        i = pl.program_id(0)

        @pl.when(i == 0)
        def _():
            loss_ref[...] = jnp.zeros_like(loss_ref)
            dgfin_ref[...] = jnp.zeros_like(dgfin_ref)
            dgffn_ref[...] = jnp.zeros_like(dgffn_ref)

        x1 = x1_ref[...]
        g_ffn_v = gffn_ref[...]
        g_fin_v = gfin_ref[...]
        n2, r2, u2 = _rms_fwd(x1, g_ffn_v)
        u2 = u2.astype(BF16)
        u2_ref[...] = u2
        x2 = x1
        for r in range(FF_PAIRS):
            gate = _dot(u2, win_ref[r])
            up = _dot(u2, win_ref[r + FF_PAIRS])
            h_scr[r] = gate
            h_scr[r + FF_PAIRS] = up
            act = (gate * _sigmoid(gate) * up).astype(BF16)
            act_ref[r] = act
            x2 = x2 + _dot(act, wout_ref[r * FF_SHARD:(r + 1) * FF_SHARD, :])
        n3, r3, y = _rms_fwd(x2, g_fin_v)
        err = y - t_ref[...]
        sq = jnp.sum(jnp.sum(err * err, axis=1, keepdims=True), axis=0, keepdims=True)
        loss_ref[...] += sq * (0.5 / D_MODEL)
        dx2, dgfin = _rms_bwd(err * (1.0 / D_MODEL), n3, r3, g_fin_v)
        dgfin_ref[...] += dgfin
        dx2_b = dx2.astype(BF16)
        dx2_ref[...] = dx2_b
        du2 = jnp.zeros((TM, D_MODEL), F32)
        for r in range(FF_PAIRS):
            gate = h_scr[r]
            up = h_scr[r + FF_PAIRS]
            dact = _dot_nt(dx2_b, wout_ref[r * FF_SHARD:(r + 1) * FF_SHARD, :])
            sg = _sigmoid(gate)
            dgate = (dact * up * (sg * (1.0 + gate * (1.0 - sg)))).astype(BF16)
            dup = (dact * (gate * sg)).astype(BF16)
            dh_ref[r] = dgate
            dh_ref[r + FF_PAIRS] = dup
            du2 = du2 + _dot_nt(dgate, win_ref[r])
            du2 = du2 + _dot_nt(dup, win_ref[r + FF_PAIRS])
        dx1_n, dgffn = _rms_bwd(du2, n2, r2, g_ffn_v)
        dgffn_ref[...] += dgffn
        dx1_ref[...] = dx2 + dx1_n

    tile = lambda w: pl.BlockSpec((TM, w), lambda i: (i, 0))
    shards = lambda n: pl.BlockSpec((n, TM, FF_SHARD), lambda i: (0, i, 0))
    acc = lambda w: pl.BlockSpec((1, w), lambda i: (0, 0))
    return pl.pallas_call(
        body, name="ffn_fwd_bwd", grid=(TOK // TM,),
        in_specs=[tile(D_MODEL), tile(D_MODEL), _resident((1, D_MODEL)), _resident((1, D_MODEL)),
                  _resident((N_DEV, D_MODEL, FF_SHARD)), _resident((D_FF, D_MODEL))],
        out_specs=[acc(LANES), tile(D_MODEL), tile(D_MODEL), shards(FF_PAIRS), shards(N_DEV), tile(D_MODEL),
                   acc(D_MODEL), acc(D_MODEL)],
        out_shape=[jax.ShapeDtypeStruct((1, LANES), F32),
                   jax.ShapeDtypeStruct((TOK, D_MODEL), F32),
                   jax.ShapeDtypeStruct((TOK, D_MODEL), BF16),
                   jax.ShapeDtypeStruct((FF_PAIRS, TOK, FF_SHARD), BF16),
                   jax.ShapeDtypeStruct((N_DEV, TOK, FF_SHARD), BF16),
                   jax.ShapeDtypeStruct((TOK, D_MODEL), BF16),
                   jax.ShapeDtypeStruct((1, D_MODEL), F32),
                   jax.ShapeDtypeStruct((1, D_MODEL), F32)],
        scratch_shapes=[pltpu.VMEM((N_DEV, TM, FF_SHARD), F32)],
        compiler_params=_params(("arbitrary",)),
    )(x1, target, g_ffn, g_fin, w_ffn_in, w_ffn_out)


def _mix_bwd(dx1, o_sb, o_dl, gates, w_sb_up, w_dil_up, w_out):
    def body(dx1_ref, osb_ref, odl_ref, gate_ref, wsb_ref, wdl_ref, wout_ref,
             dgate_ref, dysb_ref, dydl_ref, dosb_ref, dodl_ref, dsum_ref):
        dmerged = _dot_nt(dx1_ref[...].astype(BF16), wout_ref[...])
        o_dl = odl_ref[...]
        y_sb = _dot(osb_ref[...], wsb_ref[...])
        y_dl = _dot(o_dl.astype(BF16), wdl_ref[...])
        s_sb = _sigmoid(gate_ref[:, :D_MODEL])
        s_dl = _sigmoid(gate_ref[:, D_MODEL:])
        dgate_ref[:, :D_MODEL] = (dmerged * y_sb * (s_sb * (1.0 - s_sb))).astype(BF16)
        dgate_ref[:, D_MODEL:] = (dmerged * y_dl * (s_dl * (1.0 - s_dl))).astype(BF16)
        dy_sb = (dmerged * s_sb).astype(BF16)
        dy_dl = (dmerged * s_dl).astype(BF16)
        dysb_ref[...] = dy_sb
        dydl_ref[...] = dy_dl
        dosb_ref[...] = _dot_nt(dy_sb, wsb_ref[...]).astype(BF16)
        do_dl = _dot_nt(dy_dl, wdl_ref[...])
        dodl_ref[...] = do_dl
        row = lax.broadcasted_iota(jnp.int32, (DIL_OUT, DIL_OUT), 0) // HEAD_DIM
        col = lax.broadcasted_iota(jnp.int32, (DIL_OUT, DIL_OUT), 1) // HEAD_DIM
        same_head = (row == col).astype(BF16)
        hi, lo = _split_bf16(do_dl * o_dl)
        dsum_ref[...] = _dot(hi, same_head) + _dot(lo, same_head)

    tile = lambda w: pl.BlockSpec((TM, w), lambda i: (i, 0))
    return pl.pallas_call(
        body, name="mix_bwd", grid=(TOK // TM,),
        in_specs=[tile(D_MODEL), tile(SB_WIDTH), tile(DIL_OUT), tile(2 * D_MODEL),
                  _resident((SB_WIDTH, D_MODEL)), _resident((DIL_OUT, D_MODEL)),
                  _resident((D_MODEL, D_MODEL))],
        out_specs=[tile(2 * D_MODEL), tile(D_MODEL), tile(D_MODEL), tile(SB_WIDTH), tile(DIL_OUT),
                   tile(DIL_OUT)],
        out_shape=[jax.ShapeDtypeStruct((TOK, 2 * D_MODEL), BF16),
                   jax.ShapeDtypeStruct((TOK, D_MODEL), BF16),
                   jax.ShapeDtypeStruct((TOK, D_MODEL), BF16),
                   jax.ShapeDtypeStruct((TOK, SB_WIDTH), BF16),
                   jax.ShapeDtypeStruct((TOK, DIL_OUT), F32),
                   jax.ShapeDtypeStruct((TOK, DIL_OUT), F32)],
        compiler_params=_params(("parallel",)),
    )(dx1, o_sb, o_dl, gates, w_sb_up, w_dil_up, w_out)


def _proj_bwd(dproj, dx1, x, g, w_in, send=None):
    widths = [p.shape[1] for p in dproj]

    def body(*refs):
        dx1_ref, x_ref, g_ref, w_ref, dx_ref, dg_ref = refs[len(widths):]

        @pl.when(pl.program_id(0) == 0)
        def _():
            dg_ref[...] = jnp.zeros_like(dg_ref)

        du = jnp.zeros((TM, D_MODEL), F32)
        c0 = 0
        for dp_ref, w in zip(refs, widths):
            du = du + _dot_nt(dp_ref[...].astype(BF16), w_ref[:, c0:c0 + w])
            c0 += w
        g_v = g_ref[...]
        n, r, _ = _rms_fwd(x_ref[...], g_v)
        dx, dg = _rms_bwd(du, n, r, g_v)
        dg_ref[...] += dg
        dx_ref[...] = dx1_ref[...] + dx

    tile = lambda w: pl.BlockSpec((TM, w), lambda i: (i, 0))
    return _call(
        body, send, name="proj_bwd", grid=(TOK // TM,),
        in_specs=[tile(w) for w in widths] + [tile(D_MODEL), tile(D_MODEL), _resident((1, D_MODEL)),
                                              _resident((D_MODEL, IN_WIDTH))],
        out_specs=[tile(D_MODEL), pl.BlockSpec((1, D_MODEL), lambda i: (0, 0))],
        out_shape=[jax.ShapeDtypeStruct((TOK, D_MODEL), F32),
                   jax.ShapeDtypeStruct((1, D_MODEL), F32)],
        scratch_shapes=[], semantics=("arbitrary",), operands=(*dproj, dx1, x, g, w_in))


def _atb_pieces(a, pieces, name, tm, tk=512):
    m = a.shape[1]
    widths = [p.shape[1] for p in pieces]
    n = sum(widths)
    nk = TOK // tk

    def body(a_ref, *refs):
        o_ref, acc_ref = refs[len(widths):]
        k = pl.program_id(1)

        @pl.when(k == 0)
        def _():
            acc_ref[...] = jnp.zeros_like(acc_ref)

        a_v = a_ref[...]
        c0 = 0
        for p_ref, w in zip(refs, widths):
            acc_ref[:, c0:c0 + w] += _dot_tn(a_v, p_ref[...].astype(BF16))
            c0 += w

        @pl.when(k == nk - 1)
        def _():
            o_ref[...] = acc_ref[...].astype(BF16)

    return pl.pallas_call(
        body, name=name, grid=(m // tm, nk),
        in_specs=[pl.BlockSpec((tk, tm), lambda i, k: (k, i))]
                 + [pl.BlockSpec((tk, w), lambda i, k: (k, 0)) for w in widths],
        out_specs=pl.BlockSpec((tm, n), lambda i, k: (i, 0)),
        out_shape=jax.ShapeDtypeStruct((m, n), BF16),
        scratch_shapes=[pltpu.VMEM((tm, n), F32)],
        compiler_params=_params(("parallel", "arbitrary")),
    )(a, *pieces)


def _atb_shards(a, b, name, tk=512):
    a_sharded = a.ndim == 3
    n, _, w = a.shape if a_sharded else b.shape
    other = (b if a_sharded else a).shape[1]

    def body(a_ref, b_ref, o_ref):
        acc = jnp.zeros(o_ref.shape, F32)
        for k0 in range(0, TOK, tk):
            acc = acc + _dot_tn(a_ref[k0:k0 + tk, :], b_ref[k0:k0 + tk, :])
        o_ref[...] = acc.astype(BF16)

    shard = pl.BlockSpec((None, TOK, w), lambda r: (r, 0, 0))
    whole = _resident((TOK, other))
    if a_sharded:
        out_spec, out_shape = pl.BlockSpec((w, other), lambda r: (r, 0)), (n * w, other)
    else:
        out_spec, out_shape = pl.BlockSpec((None, other, w), lambda r: (r, 0, 0)), (n, other, w)
    return pl.pallas_call(
        body, name=name, grid=(n,),
        in_specs=[shard, whole] if a_sharded else [whole, shard],
        out_specs=out_spec, out_shape=jax.ShapeDtypeStruct(out_shape, BF16),
        compiler_params=_params(("parallel",)),
    )(a, b)


def _atb(a, b, name, tm, tn, col_blocks=0, tk=512):
    m, n = a.shape[1], b.shape[1]
    nk = TOK // tk

    def body(a_ref, b_ref, o_ref, acc_ref):
        k = pl.program_id(2)

        @pl.when(k == 0)
        def _():
            acc_ref[...] = jnp.zeros_like(acc_ref)

        acc_ref[...] += _dot_tn(a_ref[...].astype(BF16), b_ref[...].astype(BF16))

        @pl.when(k == nk - 1)
        def _():
            if col_blocks:
                width = n // col_blocks
                for blk in range(col_blocks):
                    o_ref[blk] = acc_ref[:, blk * width:(blk + 1) * width].astype(BF16)
            else:
                o_ref[...] = acc_ref[...].astype(BF16)

    if col_blocks:
        out_spec = pl.BlockSpec((col_blocks, tm, n // col_blocks), lambda i, j, k: (0, i, 0))
        out_shape = jax.ShapeDtypeStruct((col_blocks, m, n // col_blocks), BF16)
    else:
        out_spec = pl.BlockSpec((tm, tn), lambda i, j, k: (i, j))
        out_shape = jax.ShapeDtypeStruct((m, n), BF16)
    return pl.pallas_call(
        body, name=name, grid=(m // tm, n // tn, nk),
        in_specs=[pl.BlockSpec((tk, tm), lambda i, j, k: (k, i)),
                  pl.BlockSpec((tk, tn), lambda i, j, k: (k, j))],
        out_specs=out_spec, out_shape=out_shape,
        scratch_shapes=[pltpu.VMEM((tm, tn), F32)],
        compiler_params=_params(("parallel", "parallel", "arbitrary")),
    )(a, b)


SB_PAIRS = SB_WIDTH // LANES


def _two_heads(v, lane0):
    zero = jnp.zeros_like(v)
    return jnp.where(lane0, v, zero), jnp.where(lane0, zero, v)


SB_QBLK = 256
N_SB_STEPS = SEQ // SB_QBLK


SB_KCHUNK = 2 * BLK
SB_ROWS = 2 * SB_QBLK
SB_DEAD = -104.0


def _log_keep(z):
    neg_z = -z
    return jnp.minimum(neg_z, 0.0) - jnp.log(1.0 + jnp.exp(jnp.minimum(z, neg_z)))


def _stack_heads(v, lane0):
    return jnp.concatenate(_two_heads(v, lane0), axis=0)


def _block_sums(v, tri, split=True):
    halves = (v[:, :BLK], v[:, BLK:])
    stacked = jnp.concatenate(halves, axis=0)
    if split:
        hi, lo = _split_bf16(stacked)
        prod = _dot(jnp.concatenate([hi, lo], axis=0), tri)
        tri_sum = prod[:2 * SB_ROWS] + prod[2 * SB_ROWS:]
    else:
        tri_sum = _dot(stacked.astype(BF16), tri)
    sums = tuple(jnp.sum(h, axis=1, keepdims=True) for h in halves)
    return (tri_sum[:SB_ROWS], tri_sum[SB_ROWS:]), sums


def _sb_diag_mask():
    row = lax.broadcasted_iota(jnp.int32, (SB_ROWS, SB_KCHUNK), 0)
    col = lax.broadcasted_iota(jnp.int32, (SB_ROWS, SB_KCHUNK), 1)
    return col < jnp.where(row >= SB_QBLK, row - SB_QBLK, row)


def _sb_fwd(qkv, send=None):
    def body(q_ref, k_ref, v_ref, o_ref):
        i = pl.program_id(2)
        krow = lax.broadcasted_iota(jnp.int32, (BLK, BLK), 0)
        kcol = lax.broadcasted_iota(jnp.int32, (BLK, BLK), 1)
        later = (krow > kcol).astype(BF16)
        lane0 = lax.broadcasted_iota(jnp.int32, (SB_QBLK, LANES), 1) < HEAD_DIM
        q2 = _stack_heads(q_ref[0] * QK_SCALE, lane0)

        def chunk(c, carry, causal):
            acc, run = carry
            off = pl.multiple_of(c * SB_KCHUNK, SB_KCHUNK)
            z = _dot_nt(q2, k_ref[0, pl.ds(off, SB_KCHUNK), :])
            log_keep = _log_keep(z)
            if causal is not None:
                log_keep = jnp.where(causal, log_keep, 0.0)
            suffix, sums = _block_sums(log_keep, later)
            log_after = jnp.concatenate([suffix[0] + (run + sums[1]), suffix[1] + run], axis=1)
            a = jnp.exp(log_keep + z + log_after)
            if causal is not None:
                a = jnp.where(causal, a, 0.0)
            acc = acc + _dot(a.astype(BF16), v_ref[0, pl.ds(off, SB_KCHUNK), :])
            return acc, run + (sums[0] + sums[1])

        acc, run = chunk(i, (jnp.zeros((SB_ROWS, LANES), F32), jnp.zeros((SB_ROWS, 1), F32)), _sb_diag_mask())

        def live(state):
            t, _, run = state
            return jnp.logical_and(t < i, jnp.max(run) > SB_DEAD)

        def trip(state):
            t, acc, run = state
            acc, run = chunk(i - 1 - t, (acc, run), None)
            return t + 1, acc, run

        _, acc, _ = lax.while_loop(live, trip, (jnp.int32(0), acc, run))
        o_ref[0] = jnp.where(lane0, acc[:SB_QBLK], acc[SB_QBLK:]).astype(BF16)

    blk = pl.BlockSpec((1, SB_QBLK, LANES), lambda b, h, i: (b, i, h))
    return _call(
        body, send, name="sb_fwd", grid=(B_LOC, SB_PAIRS, N_SB_STEPS),
        in_specs=[blk,
                  pl.BlockSpec((1, SEQ, LANES), lambda b, h, i: (b, 0, SB_PAIRS + h)),
                  pl.BlockSpec((1, SEQ, LANES), lambda b, h, i: (b, 0, 2 * SB_PAIRS + h))],
        out_specs=[blk], out_shape=[jax.ShapeDtypeStruct((B_LOC, SEQ, SB_WIDTH), BF16)],
        scratch_shapes=[], semantics=("parallel", "parallel", "arbitrary"), operands=(qkv, qkv, qkv))


def _sb_bwd(qkv, d_o, send=None):
    def body(q_ref, k_ref, v_ref, do_ref, dq_ref, dk_ref, dv_ref, dk_acc, dv_acc, z_scr, keep_scr):
        i = pl.program_id(2)
        krow = lax.broadcasted_iota(jnp.int32, (BLK, BLK), 0)
        kcol = lax.broadcasted_iota(jnp.int32, (BLK, BLK), 1)
        upto = (krow <= kcol).astype(BF16)
        earlier = (krow < kcol).astype(BF16)
        lane0 = lax.broadcasted_iota(jnp.int32, (SB_QBLK, LANES), 1) < HEAD_DIM
        q2 = _stack_heads(q_ref[0] * QK_SCALE, lane0)
        do2 = _stack_heads(do_ref[0], lane0)

        def keep_sum(c, causal):
            off = pl.multiple_of(c * SB_KCHUNK, SB_KCHUNK)
            z = _dot_nt(q2, k_ref[0, pl.ds(off, SB_KCHUNK), :])
            log_keep = _log_keep(z)
            if causal is not None:
                log_keep = jnp.where(causal, log_keep, 0.0)
            z_scr[c] = z
            keep_scr[c] = log_keep
            return jnp.sum(log_keep, axis=1, keepdims=True)

        def live(state):
            t, run = state
            return jnp.logical_and(t < i, jnp.max(run) > SB_DEAD)

        walked, tot2 = lax.while_loop(live, lambda s: (s[0] + 1, s[1] + keep_sum(i - 1 - s[0], None)),
                                      (jnp.int32(0), keep_sum(i, _sb_diag_mask())))
        first = i - walked

        @pl.when(i == 0)
        def _():
            dk_acc[...] = jnp.zeros_like(dk_acc)
            dv_acc[...] = jnp.zeros_like(dv_acc)

        def chunk(c, carry, causal):
            dq, pre_keep, pre_e = carry
            off = pl.multiple_of(c * SB_KCHUNK, SB_KCHUNK)
            k_c = k_ref[0, pl.ds(off, SB_KCHUNK), :]
            v_c = v_ref[0, pl.ds(off, SB_KCHUNK), :]
            d_a = _dot_nt(do2, v_c)
            log_keep = keep_scr[c]
            log_beta = log_keep + z_scr[c]
            prefix, sums = _block_sums(log_keep, upto)
            inclusive = jnp.concatenate([prefix[0], prefix[1] + sums[0]], axis=1)
            a = jnp.exp(log_beta + ((tot2 - pre_keep) - inclusive))
            if causal is not None:
                a = jnp.where(causal, a, 0.0)
            e = d_a * a
            e_prefix, e_sums = _block_sums(e, earlier, split=False)
            before = jnp.concatenate([e_prefix[0] + pre_e, e_prefix[1] + (pre_e + e_sums[0])], axis=1)
            dz = e - (e + before) * jnp.exp(log_beta)
            if causal is not None:
                dz = jnp.where(causal, dz, 0.0)
            dz = dz.astype(BF16)
            dq = dq + _dot(dz, k_c)
            dk_acc[pl.ds(off, SB_KCHUNK), :] += _dot_tn(dz, q2)
            dv_acc[pl.ds(off, SB_KCHUNK), :] += _dot_tn(a.astype(BF16), do2)
            return dq, pre_keep + (sums[0] + sums[1]), pre_e + (e_sums[0] + e_sums[1])

        zero_col = jnp.zeros((SB_ROWS, 1), F32)
        carry = lax.fori_loop(first, i, lambda t, c: chunk(t, c, None),
                              (jnp.zeros((SB_ROWS, LANES), F32), zero_col, zero_col))
        dq, _, _ = chunk(i, carry, _sb_diag_mask())
        dq_ref[0] = (jnp.where(lane0, dq[:SB_QBLK], dq[SB_QBLK:]) * QK_SCALE).astype(BF16)

        @pl.when(i == N_SB_STEPS - 1)
        def _():
            dk_ref[0] = dk_acc[...].astype(BF16)
            dv_ref[0] = dv_acc[...].astype(BF16)

    blk = pl.BlockSpec((1, SB_QBLK, LANES), lambda b, h, i: (b, i, h))
    whole = lambda c: pl.BlockSpec((1, SEQ, LANES), lambda b, h, i: (b, 0, c * SB_PAIRS + h))
    out = jax.ShapeDtypeStruct((B_LOC, SEQ, SB_WIDTH), BF16)
    return _call(
        body, send, name="sb_bwd", grid=(B_LOC, SB_PAIRS, N_SB_STEPS),
        in_specs=[blk, whole(1), whole(2), blk],
        out_specs=[blk, whole(0), whole(0)],
        out_shape=[out, out, out],
        scratch_shapes=[pltpu.VMEM((SEQ, LANES), F32), pltpu.VMEM((SEQ, LANES), F32),
                        pltpu.VMEM((N_SB_STEPS, SB_ROWS, SB_KCHUNK), F32),
                        pltpu.VMEM((N_SB_STEPS, SB_ROWS, SB_KCHUNK), F32)],
        semantics=("parallel", "parallel", "arbitrary"), operands=(qkv, qkv, qkv, d_o))


DIL_GROUPS = len(DIL_PAIRS)
DIL_QBLOCKS = SEQ // BLK


def _residue_rows(j, dilation):
    length = SEQ // dilation
    return pl.ds(j, length, stride=dilation) if dilation > 1 else pl.ds(0, length)


def _gather_residues(src_ref, dst_ref, dst_off, dilation, scale=None):
    length = SEQ // dilation
    for j in range(dilation):
        v = src_ref[_residue_rows(j, dilation), :]
        if scale is not None:
            v = v * scale
        dst_ref[dst_off + j * length:dst_off + (j + 1) * length, :] = v.astype(dst_ref.dtype)


def _scatter_residues(src_ref, src_off, dst_ref, dilation):
    length = SEQ // dilation
    for j in range(dilation):
        dst_ref[_residue_rows(j, dilation), :] = (
            src_ref[src_off + j * length:src_off + (j + 1) * length, :].astype(dst_ref.dtype))


def _dil_geometry(group, pair):
    dilation = DIL_PAIRS[group][1]
    row = lax.broadcasted_iota(jnp.int32, (2 * BLK, 2 * BLK), 0)
    col = lax.broadcasted_iota(jnp.int32, (2 * BLK, 2 * BLK), 1)
    second = row >= BLK
    steps = BLK + jnp.where(second, row - BLK, row) - col
    coef = -ALIBI_MAX_BIAS / DIL_HEADS * math.log(2.0)
    first_head = float(4 * group + 1) + 2.0 * pair.astype(F32)
    slope = jnp.exp(coef * (first_head + jnp.where(second, 1.0, 0.0)))
    bias = slope * (steps * dilation).astype(F32)
    valid = jnp.logical_and(steps >= 0, steps <= BLK)
    return bias, valid, col >= BLK


def _dil_tile_scores(q2, kk, geometry, has_prev):
    bias, valid, own = geometry
    ok = jnp.logical_and(valid, jnp.logical_or(own, has_prev))
    return jnp.where(ok, _dot_nt(q2, kk) - bias, NEG_BIG)


def _head_col(v, lane_mask):
    return jnp.max(jnp.where(lane_mask, v, NEG_BIG), axis=1, keepdims=True)


def _dil_fwd(qkv, send=None):
    def body(*refs):
        ins, (o_ref, lse_ref), (qs, ks, vs, o_res, lse_res) = refs[:9], refs[9:11], refs[11:16]
        o_grp, lse_grp = refs[16:19], refs[19:22]
        pair = pl.program_id(1)
        lane0 = lax.broadcasted_iota(jnp.int32, (BLK, LANES), 1) < HEAD_DIM
        ks[0:BLK, :] = jnp.zeros((BLK, LANES), BF16)
        vs[0:BLK, :] = jnp.zeros((BLK, LANES), BF16)
        for grp, (_, dilation) in enumerate(DIL_PAIRS):
            q_ref, k_ref, v_ref = ins[3 * grp:3 * grp + 3]
            per_residue = DIL_QBLOCKS // dilation
            _gather_residues(q_ref, qs, 0, dilation, QK_SCALE)
            _gather_residues(k_ref, ks, BLK, dilation)
            _gather_residues(v_ref, vs, BLK, dilation)
            geometry = _dil_geometry(grp, pair)

            def step(blk, _):
                off = pl.multiple_of(blk * BLK, BLK)
                q2 = _stack_heads(qs[pl.ds(off, BLK), :], lane0)
                s = _dil_tile_scores(q2, ks[pl.ds(off, 2 * BLK), :], geometry, blk % per_residue != 0)
                m = jnp.max(s, axis=1, keepdims=True)
                p = jnp.exp(s - m)
                den = jnp.sum(p, axis=1, keepdims=True)
                out = _dot(p.astype(BF16), vs[pl.ds(off, 2 * BLK), :]) / den
                lse = m + jnp.log(den)
                o_res[pl.ds(off, BLK), :] = jnp.where(lane0, out[:BLK], out[BLK:])
                lse_res[pl.ds(off, BLK), :] = jnp.where(lane0, lse[:BLK], lse[BLK:])
                return 0

            lax.fori_loop(0, DIL_QBLOCKS, step, 0, unroll=8)
            _scatter_residues(o_res, 0, o_grp[grp], dilation)
            _scatter_residues(lse_res, 0, lse_grp[grp], dilation)

        for r0 in range(0, SEQ, 2 * BLK):
            rows = slice(r0, r0 + 2 * BLK)
            ls = [lse_grp[g][rows, :] for g in range(DIL_GROUPS)]
            m = jnp.maximum(jnp.maximum(ls[0], ls[1]), ls[2])
            w = [jnp.exp(l - m) for l in ls]
            den = w[0] + w[1] + w[2]
            o_ref[rows, :] = (w[0] * o_grp[0][rows, :] + w[1] * o_grp[1][rows, :] + w[2] * o_grp[2][rows, :]) / den
            lse_ref[rows, :] = m + jnp.log(den)

    def col(part, grp):
        return pl.BlockSpec((None, SEQ, LANES), lambda b, p: (b, 0, 6 * part + 2 * grp + p))

    out_spec = pl.BlockSpec((None, SEQ, LANES), lambda b, p: (b, 0, p))
    out = jax.ShapeDtypeStruct((B_LOC, SEQ, DIL_OUT), F32)
    return _call(
        body, send, name="dil_fwd", grid=(B_LOC, DIL_OUT // LANES),
        in_specs=[col(part, grp) for grp in range(DIL_GROUPS) for part in range(3)],
        out_specs=[out_spec, out_spec], out_shape=[out, out],
        scratch_shapes=[pltpu.VMEM((SEQ, LANES), BF16), pltpu.VMEM((SEQ + BLK, LANES), BF16),
                        pltpu.VMEM((SEQ + BLK, LANES), BF16), pltpu.VMEM((SEQ, LANES), F32),
                        pltpu.VMEM((SEQ, LANES), F32)] + [pltpu.VMEM((SEQ, LANES), F32)] * (2 * DIL_GROUPS),
        semantics=("parallel", "parallel"), operands=[qkv] * 9)


def _dil_bwd(qkv, d_o, lse, dsum, send=None):
    def body(*refs):
        ins, (do_ref, lse_ref, dsum_ref), outs = refs[:9], refs[9:12], refs[12:21]
        qs, ks, vs, dos, lse_res, dsum_res, dq_res, dk_acc, dv_acc = refs[21:]
        pair = pl.program_id(1)
        lane0 = lax.broadcasted_iota(jnp.int32, (BLK, LANES), 1) < HEAD_DIM
        lane1 = jnp.logical_not(lane0)
        ks[0:BLK, :] = jnp.zeros((BLK, LANES), BF16)
        vs[0:BLK, :] = jnp.zeros((BLK, LANES), BF16)
        for grp, (_, dilation) in enumerate(DIL_PAIRS):
            q_ref, k_ref, v_ref = ins[3 * grp:3 * grp + 3]
            dq_ref, dk_ref, dv_ref = outs[3 * grp:3 * grp + 3]
            per_residue = DIL_QBLOCKS // dilation
            _gather_residues(q_ref, qs, 0, dilation, QK_SCALE)
            _gather_residues(k_ref, ks, BLK, dilation)
            _gather_residues(v_ref, vs, BLK, dilation)
            _gather_residues(do_ref, dos, 0, dilation)
            _gather_residues(lse_ref, lse_res, 0, dilation)
            _gather_residues(dsum_ref, dsum_res, 0, dilation)
            dk_acc[...] = jnp.zeros_like(dk_acc)
            dv_acc[...] = jnp.zeros_like(dv_acc)
            geometry = _dil_geometry(grp, pair)

            def step(blk, _):
                off = pl.multiple_of(blk * BLK, BLK)
                q2 = _stack_heads(qs[pl.ds(off, BLK), :], lane0)
                do2 = _stack_heads(dos[pl.ds(off, BLK), :], lane0)
                kk = ks[pl.ds(off, 2 * BLK), :]
                vv = vs[pl.ds(off, 2 * BLK), :]
                lse_blk = lse_res[pl.ds(off, BLK), :]
                dsum_blk = dsum_res[pl.ds(off, BLK), :]
                lse2 = jnp.concatenate([_head_col(lse_blk, lane0), _head_col(lse_blk, lane1)], axis=0)
                dsum2 = jnp.concatenate([_head_col(dsum_blk, lane0), _head_col(dsum_blk, lane1)], axis=0)
                s = _dil_tile_scores(q2, kk, geometry, blk % per_residue != 0)
                p = jnp.exp(s - lse2)
                ds = (p * (_dot_nt(do2, vv) - dsum2)).astype(BF16)
                dq2 = _dot(ds, kk)
                dq_res[pl.ds(off, BLK), :] = jnp.where(lane0, dq2[:BLK], dq2[BLK:]) * QK_SCALE
                dk_acc[pl.ds(off, 2 * BLK), :] += _dot_tn(ds, q2)
                dv_acc[pl.ds(off, 2 * BLK), :] += _dot_tn(p.astype(BF16), do2)
                return 0

            lax.fori_loop(0, DIL_QBLOCKS, step, 0, unroll=8)
            _scatter_residues(dq_res, 0, dq_ref, dilation)
            _scatter_residues(dk_acc, BLK, dk_ref, dilation)
            _scatter_residues(dv_acc, BLK, dv_ref, dilation)

    def col(part, grp):
        return pl.BlockSpec((None, SEQ, LANES), lambda b, p: (b, 0, 6 * part + 2 * grp + p))

    slot = pl.BlockSpec((None, SEQ, LANES), lambda b, p: (b, 0, p))
    out = jax.ShapeDtypeStruct((B_LOC, SEQ, DIL_OUT), F32)
    return _call(
        body, send, name="dil_bwd", grid=(B_LOC, DIL_OUT // LANES),
        in_specs=[col(part, grp) for grp in range(DIL_GROUPS) for part in range(3)] + [slot] * 3,
        out_specs=[slot] * 9, out_shape=[out] * 9,
        scratch_shapes=[pltpu.VMEM((SEQ, LANES), BF16), pltpu.VMEM((SEQ + BLK, LANES), BF16),
                        pltpu.VMEM((SEQ + BLK, LANES), BF16), pltpu.VMEM((SEQ, LANES), BF16),
                        pltpu.VMEM((SEQ, LANES), F32), pltpu.VMEM((SEQ, LANES), F32),
                        pltpu.VMEM((SEQ, LANES), F32), pltpu.VMEM((SEQ + BLK, LANES), F32),
                        pltpu.VMEM((SEQ + BLK, LANES), F32)],
        semantics=("parallel", "parallel"), operands=[qkv] * 9 + [d_o, lse, dsum])


def _peers():
    x, y, c = lax.axis_index("x"), lax.axis_index("y"), lax.axis_index("c")
    me = 4 * x + 2 * y + c
    peers = []
    for mask in range(1, N_DEV):
        px = 1 - x if mask & 4 else x
        py = 1 - y if mask & 2 else y
        pc = 1 - c if mask & 1 else c
        peers.append(((px, py, pc), 4 * px + 2 * py + pc))
    return me, peers


def _all_gather(shard, name):
    def body(src_ref, out_ref, send_sems, recv_sems, local_sem):
        x, y, c = lax.axis_index("x"), lax.axis_index("y"), lax.axis_index("c")
        sibling = (x, y, 1 - c)
        chips = [(1 - x, y), (x, 1 - y), (1 - x, 1 - y)]

        def slot(px, py, pc):
            return out_ref.at[4 * px + 2 * py + pc]

        def copy(k, block, to, src=None):
            return pltpu.make_async_remote_copy(
                src_ref=slot(*block) if src is None else src, dst_ref=slot(*block),
                send_sem=send_sems.at[k], recv_sem=recv_sems.at[k], device_id=to,
                device_id_type=pl.DeviceIdType.MESH)

        mine = pltpu.make_async_copy(src_ref, slot(x, y, c), local_sem)
        mine.start()
        first = [copy(0, (x, y, c), sibling, src=src_ref)]
        first += [copy(1 + j, (x, y, c), (*chip, c), src=src_ref) for j, chip in enumerate(chips)]
        for cp in first:
            cp.start()
        passed = [copy(4 + j, (*chip, c), sibling) for j, chip in enumerate(chips)]
        for j, chip in enumerate(chips):
            copy(1 + j, (*chip, c), (x, y, c)).wait_recv()
            passed[j].start()
        copy(0, sibling, (x, y, c)).wait_recv()
        for j, chip in enumerate(chips):
            copy(4 + j, (*chip, 1 - c), (x, y, c)).wait_recv()
        for cp in first + passed:
            cp.wait_send()
        mine.wait()

    return pl.pallas_call(
        body, name=name,
        in_specs=[pl.BlockSpec(memory_space=pl.ANY)],
        out_specs=pl.BlockSpec(memory_space=pl.ANY),
        out_shape=jax.ShapeDtypeStruct((N_DEV,) + shard.shape, shard.dtype),
        scratch_shapes=[pltpu.SemaphoreType.DMA((N_DEV - 1,)), pltpu.SemaphoreType.DMA((N_DEV - 1,)),
                        pltpu.SemaphoreType.DMA],
    )(shard)


def _call(body, send, *, name, grid, in_specs, out_specs, out_shape, scratch_shapes, semantics, operands):
    if send is None:
        return pl.pallas_call(
            body, name=name, grid=grid, in_specs=in_specs, out_specs=out_specs, out_shape=out_shape,
            scratch_shapes=scratch_shapes, compiler_params=_params(semantics))(*operands), []
    srcs, kinds = [s for s, _ in send], [k for _, k in send]
    n, n_in, n_out, n_scr = len(srcs), len(in_specs), len(out_specs), len(scratch_shapes)
    steps = math.prod(grid)
    relay_step = (13 * steps) // 16

    def plan(refs):
        src_refs, land_refs = refs[n_in:n_in + n], refs[n_in + n + n_out:n_in + 2 * n + n_out]
        send_sems, recv_sems, local_sems = refs[-3:]
        x, y, c = lax.axis_index("x"), lax.axis_index("y"), lax.axis_index("c")
        me, peers = _peers()
        first, relayed_in, relayed_out, arrivals, sends, own = [], [], [], [], [], []
        for a, kind in enumerate(kinds):
            def copy(k, src, dst_slot, to):
                return pltpu.make_async_remote_copy(
                    src_ref=src, dst_ref=land_refs[a].at[dst_slot], send_sem=send_sems.at[a * (N_DEV - 1) + k],
                    recv_sem=recv_sems.at[a * (N_DEV - 1) + k], device_id=to, device_id_type=pl.DeviceIdType.MESH)

            if kind == "gather_by_chip":
                idx = lambda px, py, pc: 4 * px + 2 * py + pc
                chips = [(1 - x, y), (x, 1 - y), (1 - x, 1 - y)]
                mine = [copy(0, src_refs[a], me, (x, y, 1 - c))]
                arrivals.append(copy(0, src_refs[a], idx(x, y, 1 - c), (x, y, 1 - c)))
                for j, (px, py) in enumerate(chips):
                    mine.append(copy(1 + j, src_refs[a], me, (px, py, c)))
                    relayed_in.append(copy(1 + j, src_refs[a], idx(px, py, c), (px, py, c)))
                    relayed_out.append(copy(4 + j, land_refs[a].at[idx(px, py, c)], idx(px, py, c), (x, y, 1 - c)))
                    arrivals.append(copy(4 + j, src_refs[a], idx(px, py, 1 - c), (x, y, 1 - c)))
                first += mine
                sends += mine + relayed_out[-3:]
                own.append(pltpu.make_async_copy(src_refs[a], land_refs[a].at[me], local_sems.at[a]))
            elif kind == "scatter_by_chip":
                for k, (px, py) in enumerate([(1 - x, y), (x, 1 - y), (1 - x, 1 - y)]):
                    first.append(copy(k, src_refs[a].at[2 * px + py], 2 * x + y, (px, py, c)))
                    arrivals.append(copy(k, src_refs[a].at[2 * px + py], 2 * px + py, (px, py, c)))
                sends += first[-3:]
                own.append(pltpu.make_async_copy(src_refs[a].at[2 * x + y], land_refs[a].at[2 * x + y],
                                                 local_sems.at[a]))
            else:
                part = (lambda i: src_refs[a].at[i]) if kind == "scatter" else (lambda i: src_refs[a])
                for k, (peer, peer_idx) in enumerate(peers):
                    first.append(copy(k, part(peer_idx), me, peer))
                    arrivals.append(copy(k, part(peer_idx), peer_idx, peer))
                sends += first[-(N_DEV - 1):]
                own.append(pltpu.make_async_copy(part(me), land_refs[a].at[me], local_sems.at[a]))
        return first, relayed_in, relayed_out, arrivals, sends, own

    def wrapped(*refs):
        step = 0
        for axis, size in enumerate(grid):
            step = step * size + pl.program_id(axis)

        @pl.when(step == 0)
        def _():
            first, _, _, _, _, own = plan(refs)
            for cp in first + own:
                cp.start()

        if "gather_by_chip" in kinds:
            @pl.when(step == relay_step)
            def _():
                _, relayed_in, relayed_out, _, _, _ = plan(refs)
                for cp_in, cp_out in zip(relayed_in, relayed_out):
                    cp_in.wait_recv()
                    cp_out.start()

        body(*refs[:n_in], *refs[n_in + n:n_in + n + n_out], *refs[n_in + 2 * n + n_out:n_in + 2 * n + n_out + n_scr])

        @pl.when(step == steps - 1)
        def _():
            _, _, _, arrivals, sends, own = plan(refs)
            for cp in arrivals:
                cp.wait_recv()
            for cp in sends:
                cp.wait_send()
            for cp in own:
                cp.wait()

    anywhere = pl.BlockSpec(memory_space=pl.ANY)
    lands = [jax.ShapeDtypeStruct((N_DEV // 2 if k == "scatter_by_chip" else N_DEV,) + s.shape[-2:], s.dtype)
             for s, k in send]
    out = pl.pallas_call(
        wrapped, name=name, grid=grid,
        in_specs=list(in_specs) + [anywhere] * n, out_specs=list(out_specs) + [anywhere] * n,
        out_shape=list(out_shape) + lands,
        scratch_shapes=list(scratch_shapes) + [pltpu.SemaphoreType.DMA((n * (N_DEV - 1),)),
                                               pltpu.SemaphoreType.DMA((n * (N_DEV - 1),)),
                                               pltpu.SemaphoreType.DMA((n,))],
        compiler_params=_params(("arbitrary",) * len(grid)),
    )(*operands, *srcs)
    return out[:n_out], list(out[n_out:])


def _pair_swap(blocks):
    def body(src_ref, out_ref, send_sems, recv_sems):
        x, y, c = lax.axis_index("x"), lax.axis_index("y"), lax.axis_index("c")
        copies = [pltpu.make_async_remote_copy(
            src_ref=src_ref.at[2 * chip + (1 - c)], dst_ref=out_ref.at[chip], send_sem=send_sems.at[chip],
            recv_sem=recv_sems.at[chip], device_id=(x, y, 1 - c), device_id_type=pl.DeviceIdType.MESH)
            for chip in range(N_DEV // 2)]
        for cp in copies:
            cp.start()
        for cp in copies:
            cp.wait()

    return pl.pallas_call(
        body, name="pair_swap_grad_w_in",
        in_specs=[pl.BlockSpec(memory_space=pl.ANY)], out_specs=pl.BlockSpec(memory_space=pl.ANY),
        out_shape=jax.ShapeDtypeStruct((N_DEV // 2,) + blocks.shape[1:], blocks.dtype),
        scratch_shapes=[pltpu.SemaphoreType.DMA((N_DEV // 2,)), pltpu.SemaphoreType.DMA((N_DEV // 2,))],
    )(blocks)


def _pair_sum(blocks, swapped, core):
    _, rows, cols = swapped.shape
    tile_rows = _row_tile(rows)

    def body(core_ref, mine_ref, theirs_ref, o_ref):
        o_ref[...] = (mine_ref[...].astype(F32) + theirs_ref[...].astype(F32)).astype(o_ref.dtype)

    return pl.pallas_call(
        body, name="pair_sum_grad_w_in",
        grid_spec=pltpu.PrefetchScalarGridSpec(
            num_scalar_prefetch=1, grid=(N_DEV // 2, rows // tile_rows),
            in_specs=[pl.BlockSpec((None, tile_rows, cols), lambda j, i, core_ref: (2 * j + core_ref[0], i, 0)),
                      pl.BlockSpec((None, tile_rows, cols), lambda j, i, core_ref: (j, i, 0))],
            out_specs=pl.BlockSpec((None, tile_rows, cols), lambda j, i, core_ref: (j, i, 0))),
        out_shape=jax.ShapeDtypeStruct(swapped.shape, swapped.dtype),
        compiler_params=_params(("parallel", "parallel")),
    )(core, blocks, swapped)


def _sum_in_device_order(land_ref):
    acc = land_ref[0].astype(F32)
    for j in range(1, land_ref.shape[0]):
        acc = acc + land_ref[j].astype(F32)
    return acc


def _adam_math(w, g, m, v):
    c1 = 1.0 - ADAM_B1 ** ADAM_STEP
    c2 = 1.0 - ADAM_B2 ** ADAM_STEP
    m_new = ADAM_B1 * m + (1.0 - ADAM_B1) * g
    v_new = ADAM_B2 * v + (1.0 - ADAM_B2) * (g * g)
    delta = -ADAM_LR * ((m_new / c1) / (jnp.sqrt(v_new / c2) + ADAM_EPS) + ADAM_WD * w)
    return delta, m_new, v_new


def _row_tile(rows):
    return max(t for t in range(8, 257, 8) if rows % t == 0) if rows % 8 == 0 else rows


def _sum_update(land, w, m, v, name):
    slots, rows, cols = land.shape
    tile_rows = _row_tile(rows)

    def body(land_ref, w_ref, m_ref, v_ref, g_ref, d_ref, nm_ref, nv_ref):
        g = _sum_in_device_order(land_ref)
        g_ref[...] = g
        d_ref[...], nm_ref[...], nv_ref[...] = _adam_math(w_ref[...], g, m_ref[...], v_ref[...])

    tile = pl.BlockSpec((None, tile_rows, cols), lambda i: (0, i, 0))
    out = jax.ShapeDtypeStruct((1, rows, cols), F32)
    return pl.pallas_call(
        body, name=name, grid=(rows // tile_rows,),
        in_specs=[pl.BlockSpec((slots, tile_rows, cols), lambda i: (0, i, 0)), tile, tile, tile],
        out_specs=[tile] * 4, out_shape=[out] * 4,
        compiler_params=_params(("parallel",)),
    )(land, w, m, v)


def _sum_gains(land):
    def body(land_ref, o_ref):
        o_ref[...] = _sum_in_device_order(land_ref)

    return pl.pallas_call(
        body, name="sum_gain_grads", grid=(1,),
        in_specs=[pl.BlockSpec(land.shape, lambda i: (0, 0, 0))],
        out_specs=pl.BlockSpec(land.shape[1:], lambda i: (0, 0)),
        out_shape=jax.ShapeDtypeStruct(land.shape[1:], F32),
    )(land)


def _adamw(w, g, m, v, name):
    def body(w_ref, g_ref, m_ref, v_ref, d_ref, nm_ref, nv_ref):
        d_ref[...], nm_ref[...], nv_ref[...] = _adam_math(w_ref[...], g_ref[...], m_ref[...], v_ref[...])

    whole = pl.BlockSpec(w.shape, lambda i: (0, 0))
    out = jax.ShapeDtypeStruct(w.shape, F32)
    return pl.pallas_call(
        body, name=name, grid=(1,),
        in_specs=[whole] * 4, out_specs=[whole] * 3, out_shape=[out] * 3,
    )(w, g, m, v)


GROUP_FFN = ("w_ffn_in", "w_ffn_out")
GROUP_MIX = ("w_sb_up", "w_dil_up", "w_out")
COL_SHARDED = ("w_in", "w_sb_up", "w_dil_up", "w_ffn_in")


def _full_from_shards(name, slots):
    _, r, c = slots.shape
    if name in COL_SHARDED:
        return slots.transpose(1, 0, 2).reshape(r, N_DEV * c)
    return slots.reshape(N_DEV * r, c)


def _shards_from_full(name, full):
    rows, cols = full.shape
    if name in COL_SHARDED:
        return full.reshape(rows, N_DEV, cols // N_DEV).transpose(1, 0, 2)
    return full.reshape(N_DEV, rows // N_DEV, cols)


def _local_step(x, target, g_mix, g_ffn, g_fin, w_in, shards=None, rest=None):
    gather = lambda names, kind: None if shards is None else [(shards[n], kind) for n in names]
    scatter = lambda blocks: None if shards is None else [(t, "scatter") for t in blocks]
    landed = lambda blocks, lands: lands if lands else blocks

    w = {"w_in": w_in}
    if shards is None:
        w.update(rest)
        w["w_ffn_in"] = _shards_from_full("w_ffn_in", rest["w_ffn_in"])
    (qkv_sb, qkv_dl, gates, u), _ = _norm_proj(x, g_mix, w["w_in"])
    qkv_sb = qkv_sb.reshape(B_LOC, SEQ, 3 * SB_WIDTH)
    qkv_dl = qkv_dl.reshape(B_LOC, SEQ, 3 * DIL_WIDTH)
    (o_sb,), lands = _sb_fwd(qkv_sb, gather(GROUP_FFN, "gather_by_chip"))
    if lands:
        w["w_ffn_in"], w["w_ffn_out"] = lands[0], _full_from_shards("w_ffn_out", lands[1])
    o_sb = o_sb.reshape(TOK, SB_WIDTH)
    (o_dl, lse), lands = _dil_fwd(qkv_dl, gather(GROUP_MIX, "gather"))
    w.update({n: _full_from_shards(n, t) for n, t in zip(GROUP_MIX, lands)})
    o_dl = o_dl.reshape(TOK, DIL_OUT)

    x1, merged = _mix_out(x, o_sb, o_dl, gates, w["w_sb_up"], w["w_dil_up"], w["w_out"])
    loss, dx1, u2, act, dh, dx2, dg_fin, dg_ffn = _ffn_fwd_bwd(x1, target, g_ffn, g_fin, w["w_ffn_in"], w["w_ffn_out"])
    dgates, dy_sb, dy_dl, do_sb, do_dl, dsum = _mix_bwd(dx1, o_sb, o_dl, gates, w["w_sb_up"], w["w_dil_up"], w["w_out"])
    blocks = {
        "w_sb_up": _atb(o_sb, dy_sb, "grad_w_sb_up", SB_WIDTH, D_MODEL, col_blocks=N_DEV),
        "w_dil_up": _atb(o_dl, dy_dl, "grad_w_dil_up", DIL_OUT, D_MODEL, col_blocks=N_DEV),
        "w_out": _shards_from_full("w_out", _atb(merged, dx1, "grad_w_out", D_MODEL, D_MODEL)),
        "w_ffn_in": _atb_shards(u2, dh, "grad_w_ffn_in"),
        "w_ffn_out": _shards_from_full("w_ffn_out", _atb_shards(act, dx2, "grad_w_ffn_out")),
    }
    grads = {}

    early, late = GROUP_FFN, GROUP_MIX
    early_blocks = [blocks[n] for n in early]
    (dq_sb, dk_sb, dv_sb), lands = _sb_bwd(qkv_sb, do_sb.reshape(B_LOC, SEQ, SB_WIDTH), scatter(early_blocks))
    grads.update(zip(early, landed(early_blocks, lands)))
    as_batch = lambda t: t.reshape(B_LOC, SEQ, DIL_OUT)
    late_blocks = [blocks[n] for n in late]
    d_dl, lands = _dil_bwd(qkv_dl, as_batch(do_dl), lse, as_batch(dsum), scatter(late_blocks))
    grads.update(zip(late, landed(late_blocks, lands)))
    flat = lambda t: t.reshape(TOK, -1)
    dproj = ([flat(dq_sb), flat(dk_sb), flat(dv_sb)]
             + [flat(d_dl[3 * grp + part]) for part in range(3) for grp in range(DIL_GROUPS)] + [dgates])

    w_in_blocks = _shards_from_full("w_in", _atb_pieces(u, dproj, "grad_w_in", D_MODEL // 2))
    if shards is None:
        grads["w_in"] = w_in_blocks
        send = None
    else:
        core = lax.axis_index("c").astype(jnp.int32).reshape(1)
        send = [(_pair_sum(w_in_blocks, _pair_swap(w_in_blocks), core), "scatter_by_chip")]
    (grad_x, dg_mix), lands = _proj_bwd(dproj, dx1, x, g_mix, w["w_in"], send)
    if lands:
        grads["w_in"] = lands[0]
    gain_grads = jnp.concatenate([dg_mix, dg_ffn, dg_fin], axis=0)
    return loss, grad_x, gain_grads, grads


def kernel(x, norm_mix_g, w_in, w_sb_up, w_dil_up, w_out, norm_ffn_g, w_ffn_in, w_ffn_out, norm_final_g, loss_target, m_norm_mix_g, m_w_in, m_w_sb_up, m_w_dil_up, m_w_out, m_norm_ffn_g, m_w_ffn_in, m_w_ffn_out, m_norm_final_g, v_norm_mix_g, v_w_in, v_w_sb_up, v_w_dil_up, v_w_out, v_norm_ffn_g, v_w_ffn_in, v_w_ffn_out, v_norm_final_g):
    mats = {"w_in": w_in, "w_sb_up": w_sb_up, "w_dil_up": w_dil_up, "w_out": w_out,
            "w_ffn_in": w_ffn_in, "w_ffn_out": w_ffn_out}
    moments_m = {"w_in": m_w_in, "w_sb_up": m_w_sb_up, "w_dil_up": m_w_dil_up, "w_out": m_w_out,
                 "w_ffn_in": m_w_ffn_in, "w_ffn_out": m_w_ffn_out}
    moments_v = {"w_in": v_w_in, "w_sb_up": v_w_sb_up, "w_dil_up": v_w_dil_up, "w_out": v_w_out,
                 "w_ffn_in": v_w_ffn_in, "w_ffn_out": v_w_ffn_out}
    gathered_w_in = _all_gather(w_in[0].astype(BF16), "all_gather_w_in")
    g_fin = norm_final_g.reshape(1, D_MODEL)
    loss, grad_x, gain_grads, grad_slots = _local_step(
        x.reshape(TOK, D_MODEL), loss_target.reshape(TOK, D_MODEL), norm_mix_g, norm_ffn_g, g_fin,
        _full_from_shards("w_in", gathered_w_in),
        shards={name: mats[name][0].astype(BF16) for name in GROUP_FFN + GROUP_MIX})

    gain_rows = jnp.concatenate([gain_grads, jnp.tile(loss, (1, D_MODEL // LANES)),
                                 jnp.zeros((8 - 4, D_MODEL), F32)], axis=0)
    g_gains = _sum_gains(_all_gather(gain_rows, "all_gather_gains"))

    out_g, out_d, out_m, out_v = {}, {}, {}, {}
    for name, slots in grad_slots.items():
        out_g[name], out_d[name], out_m[name], out_v[name] = _sum_update(
            slots, mats[name], moments_m[name], moments_v[name], "update_" + name)

    gain_w = jnp.concatenate([norm_mix_g, norm_ffn_g, g_fin], axis=0)
    gain_m = jnp.concatenate([m_norm_mix_g, m_norm_ffn_g, m_norm_final_g.reshape(1, D_MODEL)], axis=0)
    gain_v = jnp.concatenate([v_norm_mix_g, v_norm_ffn_g, v_norm_final_g.reshape(1, D_MODEL)], axis=0)
    gd, gm, gv = _adamw(gain_w, g_gains[:3], gain_m, gain_v, "adamw_gains")
    for idx, name in enumerate(("norm_mix_g", "norm_ffn_g", "norm_final_g")):
        shape = (D_MODEL,) if name == "norm_final_g" else (1, D_MODEL)
        out_g[name] = g_gains[idx].reshape(shape)
        out_d[name], out_m[name], out_v[name] = gd[idx].reshape(shape), gm[idx].reshape(shape), gv[idx].reshape(shape)

    order = ("norm_mix_g", "w_in", "w_sb_up", "w_dil_up", "w_out", "norm_ffn_g", "w_ffn_in", "w_ffn_out",
             "norm_final_g")
    return (g_gains[3, 0], grad_x.reshape(B_LOC, SEQ, D_MODEL),
            *[out_g[n] for n in order], *[out_d[n] for n in order],
            *[out_m[n] for n in order], *[out_v[n] for n in order])
```

```python
import math

import jax
import jax.numpy as jnp
from jax import lax
from jax.experimental import pallas as pl
from jax.experimental.pallas import tpu as pltpu

F32 = jnp.float32
BF16 = jnp.bfloat16

N_DEV = 8
D_MODEL = 1024
SEQ = 2048
B_LOC = 2
TOK = B_LOC * SEQ
HEAD_DIM = 64
SB_WIDTH = 512
DIL_WIDTH = 768
DIL_OUT = 256
QKV_WIDTH = 3 * SB_WIDTH + 3 * DIL_WIDTH
IN_WIDTH = QKV_WIDTH + 2 * D_MODEL
D_FF = 2816
DIL_PAIRS = ((128, 1), (512, 4), (2048, 16))
DIL_HEADS = 12
RMS_EPS = 1e-6
ALIBI_MAX_BIAS = 8.0
QK_SCALE = 1.0 / math.sqrt(HEAD_DIM)
BLK = 128
LANES = 128
NEG_BIG = -1e30

ADAM_LR = 0.001
ADAM_B1 = 0.9
ADAM_B2 = 0.999
ADAM_EPS = 1e-08
ADAM_WD = 0.01
ADAM_STEP = 10

VMEM_LIMIT = 56 * 1024 * 1024


def _dot(a, b):
    return jnp.dot(a, b, preferred_element_type=F32)


def _dot_nt(a, b):
    return lax.dot_general(a, b, (((1,), (1,)), ((), ())), preferred_element_type=F32)


def _dot_tn(a, b):
    return lax.dot_general(a, b, (((0,), (0,)), ((), ())), preferred_element_type=F32)


def _sigmoid(z):
    return 1.0 / (1.0 + jnp.exp(-z))


def _split_bf16(v):
    hi = v.astype(BF16)
    lo = (v - hi.astype(F32)).astype(BF16)
    return hi, lo


def _chunks(width, step=512):
    out, c = [], 0
    while c < width:
        w = min(step, width - c)
        out.append((c, w))
        c += w
    return out


def _resident(shape):
    nd = len(shape)
    return pl.BlockSpec(shape, lambda *_: (0,) * nd, pipeline_mode=pl.Buffered(1))


def _params(sem):
    return pltpu.CompilerParams(dimension_semantics=sem, vmem_limit_bytes=VMEM_LIMIT)


def _rms_fwd(x, g):
    r = lax.rsqrt(jnp.mean(x * x, axis=-1, keepdims=True) + RMS_EPS)
    n = x * r
    return n, r, n * g


def _rms_bwd(dy, n, r, g):
    dg = jnp.sum(dy * n, axis=0, keepdims=True)
    dn = dy * g
    dx = r * (dn - n * jnp.mean(dn * n, axis=-1, keepdims=True))
    return dx, dg


TM = 256


def _norm_proj(x, g, w_in, send=None):
    def body(x_ref, g_ref, w_ref, sb_ref, dl_ref, gate_ref, u_ref):
        _, _, u = _rms_fwd(x_ref[...], g_ref[...])
        u = u.astype(BF16)
        u_ref[...] = u
        for c0, w in _chunks(3 * SB_WIDTH):
            sb_ref[:, c0:c0 + w] = _dot(u, w_ref[:, c0:c0 + w]).astype(BF16)
        for c0, w in _chunks(3 * DIL_WIDTH):
            dl_ref[:, c0:c0 + w] = _dot(u, w_ref[:, 3 * SB_WIDTH + c0:3 * SB_WIDTH + c0 + w])
        for c0, w in _chunks(2 * D_MODEL):
            gate_ref[:, c0:c0 + w] = _dot(u, w_ref[:, QKV_WIDTH + c0:QKV_WIDTH + c0 + w])

    return _call(
        body, send, name="norm_proj", grid=(TOK // TM,),
        in_specs=[pl.BlockSpec((TM, D_MODEL), lambda i: (i, 0)), _resident((1, D_MODEL)),
                  _resident((D_MODEL, IN_WIDTH))],
        out_specs=[pl.BlockSpec((TM, 3 * SB_WIDTH), lambda i: (i, 0)),
                   pl.BlockSpec((TM, 3 * DIL_WIDTH), lambda i: (i, 0)),
                   pl.BlockSpec((TM, 2 * D_MODEL), lambda i: (i, 0)),
                   pl.BlockSpec((TM, D_MODEL), lambda i: (i, 0))],
        out_shape=[jax.ShapeDtypeStruct((TOK, 3 * SB_WIDTH), BF16),
                   jax.ShapeDtypeStruct((TOK, 3 * DIL_WIDTH), F32),
                   jax.ShapeDtypeStruct((TOK, 2 * D_MODEL), F32),
                   jax.ShapeDtypeStruct((TOK, D_MODEL), BF16)],
        scratch_shapes=[], semantics=("parallel",), operands=(x, g, w_in))


FF_SHARD = 2 * D_FF // N_DEV
FF_PAIRS = N_DEV // 2


def _mix_ffn_fwd_bwd(x, o_sb, o_dl, gates, w_sb_up, w_dil_up, w_out, target, g_ffn, g_fin, w_ffn_in, w_ffn_out):
    def body(x_ref, osb_ref, odl_ref, gate_ref, wsb_ref, wdl_ref, wo_ref, t_ref, gffn_ref, gfin_ref, win_ref, wout_ref,
             loss_ref, dx1_ref, mg_ref, u2_ref, act_ref, dh_ref, dx2_ref, dgfin_ref, dgffn_ref, h_scr):
        i = pl.program_id(0)

        @pl.when(i == 0)
        def _():
            loss_ref[...] = jnp.zeros_like(loss_ref)
            dgfin_ref[...] = jnp.zeros_like(dgfin_ref)
            dgffn_ref[...] = jnp.zeros_like(dgffn_ref)

        y_sb = _dot(osb_ref[...], wsb_ref[...])
        y_dl = _dot(odl_ref[...].astype(BF16), wdl_ref[...])
        merged = (_sigmoid(gate_ref[:, :D_MODEL]) * y_sb
                  + _sigmoid(gate_ref[:, D_MODEL:]) * y_dl).astype(BF16)
        mg_ref[...] = merged
        x1 = x_ref[...] + _dot(merged, wo_ref[...])
        g_ffn_v = gffn_ref[...]
        g_fin_v = gfin_ref[...]
        n2, r2, u2 = _rms_fwd(x1, g_ffn_v)
        u2 = u2.astype(BF16)
        u2_ref[...] = u2
        x2 = x1
        for r in range(FF_PAIRS):
            gate = _dot(u2, win_ref[r])
            up = _dot(u2, win_ref[r + FF_PAIRS])
            h_scr[r] = gate
            h_scr[r + FF_PAIRS] = up
            act = (gate * _sigmoid(gate) * up).astype(BF16)
            act_ref[r] = act
            x2 = x2 + _dot(act, wout_ref[r * FF_SHARD:(r + 1) * FF_SHARD, :])
        n3, r3, y = _rms_fwd(x2, g_fin_v)
        err = y - t_ref[...]
        sq = jnp.sum(jnp.sum(err * err, axis=1, keepdims=True), axis=0, keepdims=True)
        loss_ref[...] += sq * (0.5 / D_MODEL)
        dx2, dgfin = _rms_bwd(err * (1.0 / D_MODEL), n3, r3, g_fin_v)
        dgfin_ref[...] += dgfin
        dx2_b = dx2.astype(BF16)
        dx2_ref[...] = dx2_b
        du2 = jnp.zeros((TM, D_MODEL), F32)
        for r in range(FF_PAIRS):
            gate = h_scr[r]
            up = h_scr[r + FF_PAIRS]
            dact = _dot_nt(dx2_b, wout_ref[r * FF_SHARD:(r + 1) * FF_SHARD, :])
            sg = _sigmoid(gate)
            dgate = (dact * up * (sg * (1.0 + gate * (1.0 - sg)))).astype(BF16)
            dup = (dact * (gate * sg)).astype(BF16)
            dh_ref[r] = dgate
            dh_ref[r + FF_PAIRS] = dup
            du2 = du2 + _dot_nt(dgate, win_ref[r])
            du2 = du2 + _dot_nt(dup, win_ref[r + FF_PAIRS])
        dx1_n, dgffn = _rms_bwd(du2, n2, r2, g_ffn_v)
        dgffn_ref[...] += dgffn
        dx1_ref[...] = dx2 + dx1_n

    tile = lambda w: pl.BlockSpec((TM, w), lambda i: (i, 0))
    shards = lambda n: pl.BlockSpec((n, TM, FF_SHARD), lambda i: (0, i, 0))
    acc = lambda w: pl.BlockSpec((1, w), lambda i: (0, 0))
    return pl.pallas_call(
        body, name="mix_ffn_fwd_bwd", grid=(TOK // TM,),
        in_specs=[tile(D_MODEL), tile(SB_WIDTH), tile(DIL_OUT), tile(2 * D_MODEL),
                  _resident((SB_WIDTH, D_MODEL)), _resident((DIL_OUT, D_MODEL)), _resident((D_MODEL, D_MODEL)),
                  tile(D_MODEL), _resident((1, D_MODEL)), _resident((1, D_MODEL)),
                  _resident((N_DEV, D_MODEL, FF_SHARD)), _resident((D_FF, D_MODEL))],
        out_specs=[acc(LANES), tile(D_MODEL), tile(D_MODEL), tile(D_MODEL), shards(FF_PAIRS), shards(N_DEV),
                   tile(D_MODEL), acc(D_MODEL), acc(D_MODEL)],
        out_shape=[jax.ShapeDtypeStruct((1, LANES), F32),
                   jax.ShapeDtypeStruct((TOK, D_MODEL), F32),
                   jax.ShapeDtypeStruct((TOK, D_MODEL), BF16),
                   jax.ShapeDtypeStruct((TOK, D_MODEL), BF16),
                   jax.ShapeDtypeStruct((FF_PAIRS, TOK, FF_SHARD), BF16),
                   jax.ShapeDtypeStruct((N_DEV, TOK, FF_SHARD), BF16),
                   jax.ShapeDtypeStruct((TOK, D_MODEL), BF16),
                   jax.ShapeDtypeStruct((1, D_MODEL), F32),
                   jax.ShapeDtypeStruct((1, D_MODEL), F32)],
        scratch_shapes=[pltpu.VMEM((N_DEV, TM, FF_SHARD), F32)],
        compiler_params=_params(("arbitrary",)),
    )(x, o_sb, o_dl, gates, w_sb_up, w_dil_up, w_out, target, g_ffn, g_fin, w_ffn_in, w_ffn_out)


def _mix_bwd(dx1, o_sb, o_dl, gates, w_sb_up, w_dil_up, w_out):
    def body(dx1_ref, osb_ref, odl_ref, gate_ref, wsb_ref, wdl_ref, wout_ref,
             dgate_ref, dysb_ref, dydl_ref, dosb_ref, dodl_ref, dsum_ref):
        dmerged = _dot_nt(dx1_ref[...].astype(BF16), wout_ref[...])
        o_dl = odl_ref[...]
        y_sb = _dot(osb_ref[...], wsb_ref[...])
        y_dl = _dot(o_dl.astype(BF16), wdl_ref[...])
        s_sb = _sigmoid(gate_ref[:, :D_MODEL])
        s_dl = _sigmoid(gate_ref[:, D_MODEL:])
        dgate_ref[:, :D_MODEL] = (dmerged * y_sb * (s_sb * (1.0 - s_sb))).astype(BF16)
        dgate_ref[:, D_MODEL:] = (dmerged * y_dl * (s_dl * (1.0 - s_dl))).astype(BF16)
        dy_sb = (dmerged * s_sb).astype(BF16)
        dy_dl = (dmerged * s_dl).astype(BF16)
        dysb_ref[...] = dy_sb
        dydl_ref[...] = dy_dl
        dosb_ref[...] = _dot_nt(dy_sb, wsb_ref[...]).astype(BF16)
        do_dl = _dot_nt(dy_dl, wdl_ref[...])
        dodl_ref[...] = do_dl
        row = lax.broadcasted_iota(jnp.int32, (DIL_OUT, DIL_OUT), 0) // HEAD_DIM
        col = lax.broadcasted_iota(jnp.int32, (DIL_OUT, DIL_OUT), 1) // HEAD_DIM
        same_head = (row == col).astype(BF16)
        hi, lo = _split_bf16(do_dl * o_dl)
        dsum_ref[...] = _dot(hi, same_head) + _dot(lo, same_head)

    tile = lambda w: pl.BlockSpec((TM, w), lambda i: (i, 0))
    return pl.pallas_call(
        body, name="mix_bwd", grid=(TOK // TM,),
        in_specs=[tile(D_MODEL), tile(SB_WIDTH), tile(DIL_OUT), tile(2 * D_MODEL),
                  _resident((SB_WIDTH, D_MODEL)), _resident((DIL_OUT, D_MODEL)),
                  _resident((D_MODEL, D_MODEL))],
        out_specs=[tile(2 * D_MODEL), tile(D_MODEL), tile(D_MODEL), tile(SB_WIDTH), tile(DIL_OUT),
                   tile(DIL_OUT)],
        out_shape=[jax.ShapeDtypeStruct((TOK, 2 * D_MODEL), BF16),
                   jax.ShapeDtypeStruct((TOK, D_MODEL), BF16),
                   jax.ShapeDtypeStruct((TOK, D_MODEL), BF16),
                   jax.ShapeDtypeStruct((TOK, SB_WIDTH), BF16),
                   jax.ShapeDtypeStruct((TOK, DIL_OUT), F32),
                   jax.ShapeDtypeStruct((TOK, DIL_OUT), F32)],
        compiler_params=_params(("parallel",)),
    )(dx1, o_sb, o_dl, gates, w_sb_up, w_dil_up, w_out)


def _proj_bwd(dproj, dx1, x, g, w_in, send=None):
    widths = [p.shape[1] for p in dproj]

    def body(*refs):
        dx1_ref, x_ref, g_ref, w_ref, dx_ref, dg_ref = refs[len(widths):]

        @pl.when(pl.program_id(0) == 0)
        def _():
            dg_ref[...] = jnp.zeros_like(dg_ref)

        du = jnp.zeros((TM, D_MODEL), F32)
        c0 = 0
        for dp_ref, w in zip(refs, widths):
            du = du + _dot_nt(dp_ref[...].astype(BF16), w_ref[:, c0:c0 + w])
            c0 += w
        g_v = g_ref[...]
        n, r, _ = _rms_fwd(x_ref[...], g_v)
        dx, dg = _rms_bwd(du, n, r, g_v)
        dg_ref[...] += dg
        dx_ref[...] = dx1_ref[...] + dx

    tile = lambda w: pl.BlockSpec((TM, w), lambda i: (i, 0))
    return _call(
        body, send, name="proj_bwd", grid=(TOK // TM,),
        in_specs=[tile(w) for w in widths] + [tile(D_MODEL), tile(D_MODEL), _resident((1, D_MODEL)),
                                              _resident((D_MODEL, IN_WIDTH))],
        out_specs=[tile(D_MODEL), pl.BlockSpec((1, D_MODEL), lambda i: (0, 0))],
        out_shape=[jax.ShapeDtypeStruct((TOK, D_MODEL), F32),
                   jax.ShapeDtypeStruct((1, D_MODEL), F32)],
        scratch_shapes=[], semantics=("arbitrary",), operands=(*dproj, dx1, x, g, w_in))


def _atb_pieces(a, pieces, name, tm, tk=512):
    m = a.shape[1]
    widths = [p.shape[1] for p in pieces]
    n = sum(widths)
    nk = TOK // tk

    def body(a_ref, *refs):
        o_ref, acc_ref = refs[len(widths):]
        k = pl.program_id(1)

        @pl.when(k == 0)
        def _():
            acc_ref[...] = jnp.zeros_like(acc_ref)

        a_v = a_ref[...]
        c0 = 0
        for p_ref, w in zip(refs, widths):
            acc_ref[:, c0:c0 + w] += _dot_tn(a_v, p_ref[...].astype(BF16))
            c0 += w

        @pl.when(k == nk - 1)
        def _():
            o_ref[...] = acc_ref[...].astype(BF16)

    return pl.pallas_call(
        body, name=name, grid=(m // tm, nk),
        in_specs=[pl.BlockSpec((tk, tm), lambda i, k: (k, i))]
                 + [pl.BlockSpec((tk, w), lambda i, k: (k, 0)) for w in widths],
        out_specs=pl.BlockSpec((tm, n), lambda i, k: (i, 0)),
        out_shape=jax.ShapeDtypeStruct((m, n), BF16),
        scratch_shapes=[pltpu.VMEM((tm, n), F32)],
        compiler_params=_params(("parallel", "arbitrary")),
    )(a, *pieces)


def _atb_shards(a, b, name, tk=512):
    a_sharded = a.ndim == 3
    n, _, w = a.shape if a_sharded else b.shape
    other = (b if a_sharded else a).shape[1]

    def body(a_ref, b_ref, o_ref):
        acc = jnp.zeros(o_ref.shape, F32)
        for k0 in range(0, TOK, tk):
            acc = acc + _dot_tn(a_ref[k0:k0 + tk, :], b_ref[k0:k0 + tk, :])
        o_ref[...] = acc.astype(BF16)

    shard = pl.BlockSpec((None, TOK, w), lambda r: (r, 0, 0))
    whole = _resident((TOK, other))
    if a_sharded:
        out_spec, out_shape = pl.BlockSpec((w, other), lambda r: (r, 0)), (n * w, other)
    else:
        out_spec, out_shape = pl.BlockSpec((None, other, w), lambda r: (r, 0, 0)), (n, other, w)
    return pl.pallas_call(
        body, name=name, grid=(n,),
        in_specs=[shard, whole] if a_sharded else [whole, shard],
        out_specs=out_spec, out_shape=jax.ShapeDtypeStruct(out_shape, BF16),
        compiler_params=_params(("parallel",)),
    )(a, b)


def _atb(a, b, name, tm, tn, col_blocks=0, tk=512):
    m, n = a.shape[1], b.shape[1]
    nk = TOK // tk

    def body(a_ref, b_ref, o_ref, acc_ref):
        k = pl.program_id(2)

        @pl.when(k == 0)
        def _():
            acc_ref[...] = jnp.zeros_like(acc_ref)

        acc_ref[...] += _dot_tn(a_ref[...].astype(BF16), b_ref[...].astype(BF16))

        @pl.when(k == nk - 1)
        def _():
            if col_blocks:
                width = n // col_blocks
                for blk in range(col_blocks):
                    o_ref[blk] = acc_ref[:, blk * width:(blk + 1) * width].astype(BF16)
            else:
                o_ref[...] = acc_ref[...].astype(BF16)

    if col_blocks:
        out_spec = pl.BlockSpec((col_blocks, tm, n // col_blocks), lambda i, j, k: (0, i, 0))
        out_shape = jax.ShapeDtypeStruct((col_blocks, m, n // col_blocks), BF16)
    else:
        out_spec = pl.BlockSpec((tm, tn), lambda i, j, k: (i, j))
        out_shape = jax.ShapeDtypeStruct((m, n), BF16)
    return pl.pallas_call(
        body, name=name, grid=(m // tm, n // tn, nk),
        in_specs=[pl.BlockSpec((tk, tm), lambda i, j, k: (k, i)),
                  pl.BlockSpec((tk, tn), lambda i, j, k: (k, j))],
        out_specs=out_spec, out_shape=out_shape,
        scratch_shapes=[pltpu.VMEM((tm, tn), F32)],
        compiler_params=_params(("parallel", "parallel", "arbitrary")),
    )(a, b)


SB_PAIRS = SB_WIDTH // LANES


def _two_heads(v, lane0):
    zero = jnp.zeros_like(v)
    return jnp.where(lane0, v, zero), jnp.where(lane0, zero, v)


SB_QBLK = 256
N_SB_STEPS = SEQ // SB_QBLK


SB_KCHUNK = 2 * BLK
SB_ROWS = 2 * SB_QBLK
SB_DEAD = -104.0


def _log_keep(z):
    neg_z = -z
    return jnp.minimum(neg_z, 0.0) - jnp.log(1.0 + jnp.exp(jnp.minimum(z, neg_z)))


def _stack_heads(v, lane0):
    return jnp.concatenate(_two_heads(v, lane0), axis=0)


def _block_sums(v, tri, split=True):
    halves = (v[:, :BLK], v[:, BLK:])
    stacked = jnp.concatenate(halves, axis=0)
    if split:
        hi, lo = _split_bf16(stacked)
        prod = _dot(jnp.concatenate([hi, lo], axis=0), tri)
        tri_sum = prod[:2 * SB_ROWS] + prod[2 * SB_ROWS:]
    else:
        tri_sum = _dot(stacked.astype(BF16), tri)
    sums = tuple(jnp.sum(h, axis=1, keepdims=True) for h in halves)
    return (tri_sum[:SB_ROWS], tri_sum[SB_ROWS:]), sums


def _sb_diag_mask():
    row = lax.broadcasted_iota(jnp.int32, (SB_ROWS, SB_KCHUNK), 0)
    col = lax.broadcasted_iota(jnp.int32, (SB_ROWS, SB_KCHUNK), 1)
    return col < jnp.where(row >= SB_QBLK, row - SB_QBLK, row)


def _sb_fwd(qkv, send=None):
    def body(q_ref, k_ref, v_ref, o_ref):
        i = pl.program_id(2)
        krow = lax.broadcasted_iota(jnp.int32, (BLK, BLK), 0)
        kcol = lax.broadcasted_iota(jnp.int32, (BLK, BLK), 1)
        later = (krow > kcol).astype(BF16)
        lane0 = lax.broadcasted_iota(jnp.int32, (SB_QBLK, LANES), 1) < HEAD_DIM
        q2 = _stack_heads(q_ref[0] * QK_SCALE, lane0)

        def chunk(c, carry, causal):
            acc, run = carry
            off = pl.multiple_of(c * SB_KCHUNK, SB_KCHUNK)
            z = _dot_nt(q2, k_ref[0, pl.ds(off, SB_KCHUNK), :])
            log_keep = _log_keep(z)
            if causal is not None:
                log_keep = jnp.where(causal, log_keep, 0.0)
            suffix, sums = _block_sums(log_keep, later)
            log_after = jnp.concatenate([suffix[0] + (run + sums[1]), suffix[1] + run], axis=1)
            a = jnp.exp(log_keep + z + log_after)
            if causal is not None:
                a = jnp.where(causal, a, 0.0)
            acc = acc + _dot(a.astype(BF16), v_ref[0, pl.ds(off, SB_KCHUNK), :])
            return acc, run + (sums[0] + sums[1])

        acc, run = chunk(i, (jnp.zeros((SB_ROWS, LANES), F32), jnp.zeros((SB_ROWS, 1), F32)), _sb_diag_mask())

        def live(state):
            t, _, run = state
            return jnp.logical_and(t < i, jnp.max(run) > SB_DEAD)

        def trip(state):
            t, acc, run = state
            acc, run = chunk(i - 1 - t, (acc, run), None)
            return t + 1, acc, run

        _, acc, _ = lax.while_loop(live, trip, (jnp.int32(0), acc, run))
        o_ref[0] = jnp.where(lane0, acc[:SB_QBLK], acc[SB_QBLK:]).astype(BF16)

    blk = pl.BlockSpec((1, SB_QBLK, LANES), lambda b, h, i: (b, i, h))
    return _call(
        body, send, name="sb_fwd", grid=(B_LOC, SB_PAIRS, N_SB_STEPS),
        in_specs=[blk,
                  pl.BlockSpec((1, SEQ, LANES), lambda b, h, i: (b, 0, SB_PAIRS + h)),
                  pl.BlockSpec((1, SEQ, LANES), lambda b, h, i: (b, 0, 2 * SB_PAIRS + h))],
        out_specs=[blk], out_shape=[jax.ShapeDtypeStruct((B_LOC, SEQ, SB_WIDTH), BF16)],
        scratch_shapes=[], semantics=("parallel", "parallel", "arbitrary"), operands=(qkv, qkv, qkv))


def _sb_bwd(qkv, d_o, send=None):
    def body(q_ref, k_ref, v_ref, do_ref, dq_ref, dk_ref, dv_ref, dk_acc, dv_acc, z_scr, keep_scr):
        i = pl.program_id(2)
        krow = lax.broadcasted_iota(jnp.int32, (BLK, BLK), 0)
        kcol = lax.broadcasted_iota(jnp.int32, (BLK, BLK), 1)
        upto = (krow <= kcol).astype(BF16)
        earlier = (krow < kcol).astype(BF16)
        lane0 = lax.broadcasted_iota(jnp.int32, (SB_QBLK, LANES), 1) < HEAD_DIM
        q2 = _stack_heads(q_ref[0] * QK_SCALE, lane0)
        do2 = _stack_heads(do_ref[0], lane0)

        def keep_sum(c, causal):
            off = pl.multiple_of(c * SB_KCHUNK, SB_KCHUNK)
            z = _dot_nt(q2, k_ref[0, pl.ds(off, SB_KCHUNK), :])
            log_keep = _log_keep(z)
            if causal is not None:
                log_keep = jnp.where(causal, log_keep, 0.0)
            z_scr[c] = z
            keep_scr[c] = log_keep
            return jnp.sum(log_keep, axis=1, keepdims=True)

        def live(state):
            t, run = state
            return jnp.logical_and(t < i, jnp.max(run) > SB_DEAD)

        walked, tot2 = lax.while_loop(live, lambda s: (s[0] + 1, s[1] + keep_sum(i - 1 - s[0], None)),
                                      (jnp.int32(0), keep_sum(i, _sb_diag_mask())))
        first = i - walked

        @pl.when(i == 0)
        def _():
            dk_acc[...] = jnp.zeros_like(dk_acc)
            dv_acc[...] = jnp.zeros_like(dv_acc)

        def chunk(c, carry, causal):
            dq, pre_keep, pre_e = carry
            off = pl.multiple_of(c * SB_KCHUNK, SB_KCHUNK)
            k_c = k_ref[0, pl.ds(off, SB_KCHUNK), :]
            v_c = v_ref[0, pl.ds(off, SB_KCHUNK), :]
            d_a = _dot_nt(do2, v_c)
            log_keep = keep_scr[c]
            log_beta = log_keep + z_scr[c]
            prefix, sums = _block_sums(log_keep, upto)
            inclusive = jnp.concatenate([prefix[0], prefix[1] + sums[0]], axis=1)
            a = jnp.exp(log_beta + ((tot2 - pre_keep) - inclusive))
            if causal is not None:
                a = jnp.where(causal, a, 0.0)
            e = d_a * a
            e_prefix, e_sums = _block_sums(e, earlier, split=False)
            before = jnp.concatenate([e_prefix[0] + pre_e, e_prefix[1] + (pre_e + e_sums[0])], axis=1)
            dz = e - (e + before) * jnp.exp(log_beta)
            if causal is not None:
                dz = jnp.where(causal, dz, 0.0)
            dz = dz.astype(BF16)
            dq = dq + _dot(dz, k_c)
            dk_acc[pl.ds(off, SB_KCHUNK), :] += _dot_tn(dz, q2)
            dv_acc[pl.ds(off, SB_KCHUNK), :] += _dot_tn(a.astype(BF16), do2)
            return dq, pre_keep + (sums[0] + sums[1]), pre_e + (e_sums[0] + e_sums[1])

        zero_col = jnp.zeros((SB_ROWS, 1), F32)
        carry = lax.fori_loop(first, i, lambda t, c: chunk(t, c, None),
                              (jnp.zeros((SB_ROWS, LANES), F32), zero_col, zero_col))
        dq, _, _ = chunk(i, carry, _sb_diag_mask())
        dq_ref[0] = (jnp.where(lane0, dq[:SB_QBLK], dq[SB_QBLK:]) * QK_SCALE).astype(BF16)

        @pl.when(i == N_SB_STEPS - 1)
        def _():
            dk_ref[0] = dk_acc[...].astype(BF16)
            dv_ref[0] = dv_acc[...].astype(BF16)

    blk = pl.BlockSpec((1, SB_QBLK, LANES), lambda b, h, i: (b, i, h))
    whole = lambda c: pl.BlockSpec((1, SEQ, LANES), lambda b, h, i: (b, 0, c * SB_PAIRS + h))
    out = jax.ShapeDtypeStruct((B_LOC, SEQ, SB_WIDTH), BF16)
    return _call(
        body, send, name="sb_bwd", grid=(B_LOC, SB_PAIRS, N_SB_STEPS),
        in_specs=[blk, whole(1), whole(2), blk],
        out_specs=[blk, whole(0), whole(0)],
        out_shape=[out, out, out],
        scratch_shapes=[pltpu.VMEM((SEQ, LANES), F32), pltpu.VMEM((SEQ, LANES), F32),
                        pltpu.VMEM((N_SB_STEPS, SB_ROWS, SB_KCHUNK), F32),
                        pltpu.VMEM((N_SB_STEPS, SB_ROWS, SB_KCHUNK), F32)],
        semantics=("parallel", "parallel", "arbitrary"), operands=(qkv, qkv, qkv, d_o))


DIL_GROUPS = len(DIL_PAIRS)
DIL_QBLOCKS = SEQ // BLK


def _residue_rows(j, dilation):
    length = SEQ // dilation
    return pl.ds(j, length, stride=dilation) if dilation > 1 else pl.ds(0, length)


def _gather_residues(src_ref, dst_ref, dst_off, dilation, scale=None):
    length = SEQ // dilation
    for j in range(dilation):
        v = src_ref[_residue_rows(j, dilation), :]
        if scale is not None:
            v = v * scale
        dst_ref[dst_off + j * length:dst_off + (j + 1) * length, :] = v.astype(dst_ref.dtype)


def _scatter_residues(src_ref, src_off, dst_ref, dilation):
    length = SEQ // dilation
    for j in range(dilation):
        dst_ref[_residue_rows(j, dilation), :] = (
            src_ref[src_off + j * length:src_off + (j + 1) * length, :].astype(dst_ref.dtype))


def _dil_geometry(group, pair):
    dilation = DIL_PAIRS[group][1]
    row = lax.broadcasted_iota(jnp.int32, (2 * BLK, 2 * BLK), 0)
    col = lax.broadcasted_iota(jnp.int32, (2 * BLK, 2 * BLK), 1)
    second = row >= BLK
    steps = BLK + jnp.where(second, row - BLK, row) - col
    coef = -ALIBI_MAX_BIAS / DIL_HEADS * math.log(2.0)
    first_head = float(4 * group + 1) + 2.0 * pair.astype(F32)
    slope = jnp.exp(coef * (first_head + jnp.where(second, 1.0, 0.0)))
    bias = slope * (steps * dilation).astype(F32)
    valid = jnp.logical_and(steps >= 0, steps <= BLK)
    return bias, valid, col >= BLK


def _dil_tile_scores(q2, kk, geometry, has_prev):
    bias, valid, own = geometry
    ok = jnp.logical_and(valid, jnp.logical_or(own, has_prev))
    return jnp.where(ok, _dot_nt(q2, kk) - bias, NEG_BIG)


def _head_col(v, lane_mask):
    return jnp.max(jnp.where(lane_mask, v, NEG_BIG), axis=1, keepdims=True)


def _dil_fwd(qkv, send=None):
    def body(*refs):
        ins, (o_ref, lse_ref), (qs, ks, vs, o_res, lse_res) = refs[:9], refs[9:11], refs[11:16]
        o_grp, lse_grp = refs[16:19], refs[19:22]
        pair = pl.program_id(1)
        lane0 = lax.broadcasted_iota(jnp.int32, (BLK, LANES), 1) < HEAD_DIM
        ks[0:BLK, :] = jnp.zeros((BLK, LANES), BF16)
        vs[0:BLK, :] = jnp.zeros((BLK, LANES), BF16)
        for grp, (_, dilation) in enumerate(DIL_PAIRS):
            q_ref, k_ref, v_ref = ins[3 * grp:3 * grp + 3]
            per_residue = DIL_QBLOCKS // dilation
            _gather_residues(q_ref, qs, 0, dilation, QK_SCALE)
            _gather_residues(k_ref, ks, BLK, dilation)
            _gather_residues(v_ref, vs, BLK, dilation)
            geometry = _dil_geometry(grp, pair)

            def step(blk, _):
                off = pl.multiple_of(blk * BLK, BLK)
                q2 = _stack_heads(qs[pl.ds(off, BLK), :], lane0)
                s = _dil_tile_scores(q2, ks[pl.ds(off, 2 * BLK), :], geometry, blk % per_residue != 0)
                m = jnp.max(s, axis=1, keepdims=True)
                p = jnp.exp(s - m)
                den = jnp.sum(p, axis=1, keepdims=True)
                out = _dot(p.astype(BF16), vs[pl.ds(off, 2 * BLK), :]) / den
                lse = m + jnp.log(den)
                o_res[pl.ds(off, BLK), :] = jnp.where(lane0, out[:BLK], out[BLK:])
                lse_res[pl.ds(off, BLK), :] = jnp.where(lane0, lse[:BLK], lse[BLK:])
                return 0

            lax.fori_loop(0, DIL_QBLOCKS, step, 0, unroll=8)
            _scatter_residues(o_res, 0, o_grp[grp], dilation)
            _scatter_residues(lse_res, 0, lse_grp[grp], dilation)

        for r0 in range(0, SEQ, 2 * BLK):
            rows = slice(r0, r0 + 2 * BLK)
            ls = [lse_grp[g][rows, :] for g in range(DIL_GROUPS)]
            m = jnp.maximum(jnp.maximum(ls[0], ls[1]), ls[2])
            w = [jnp.exp(l - m) for l in ls]
            den = w[0] + w[1] + w[2]
            o_ref[rows, :] = (w[0] * o_grp[0][rows, :] + w[1] * o_grp[1][rows, :] + w[2] * o_grp[2][rows, :]) / den
            lse_ref[rows, :] = m + jnp.log(den)

    def col(part, grp):
        return pl.BlockSpec((None, SEQ, LANES), lambda b, p: (b, 0, 6 * part + 2 * grp + p))

    out_spec = pl.BlockSpec((None, SEQ, LANES), lambda b, p: (b, 0, p))
    out = jax.ShapeDtypeStruct((B_LOC, SEQ, DIL_OUT), F32)
    return _call(
        body, send, name="dil_fwd", grid=(B_LOC, DIL_OUT // LANES),
        in_specs=[col(part, grp) for grp in range(DIL_GROUPS) for part in range(3)],
        out_specs=[out_spec, out_spec], out_shape=[out, out],
        scratch_shapes=[pltpu.VMEM((SEQ, LANES), BF16), pltpu.VMEM((SEQ + BLK, LANES), BF16),
                        pltpu.VMEM((SEQ + BLK, LANES), BF16), pltpu.VMEM((SEQ, LANES), F32),
                        pltpu.VMEM((SEQ, LANES), F32)] + [pltpu.VMEM((SEQ, LANES), F32)] * (2 * DIL_GROUPS),
        semantics=("parallel", "parallel"), operands=[qkv] * 9)


def _dil_bwd(qkv, d_o, lse, dsum, send=None):
    def body(*refs):
        ins, (do_ref, lse_ref, dsum_ref), outs = refs[:9], refs[9:12], refs[12:21]
        qs, ks, vs, dos, lse_res, dsum_res, dq_res, dk_acc, dv_acc = refs[21:]
        pair = pl.program_id(1)
        lane0 = lax.broadcasted_iota(jnp.int32, (BLK, LANES), 1) < HEAD_DIM
        lane1 = jnp.logical_not(lane0)
        ks[0:BLK, :] = jnp.zeros((BLK, LANES), BF16)
        vs[0:BLK, :] = jnp.zeros((BLK, LANES), BF16)
        for grp, (_, dilation) in enumerate(DIL_PAIRS):
            q_ref, k_ref, v_ref = ins[3 * grp:3 * grp + 3]
            dq_ref, dk_ref, dv_ref = outs[3 * grp:3 * grp + 3]
            per_residue = DIL_QBLOCKS // dilation
            _gather_residues(q_ref, qs, 0, dilation, QK_SCALE)
            _gather_residues(k_ref, ks, BLK, dilation)
            _gather_residues(v_ref, vs, BLK, dilation)
            _gather_residues(do_ref, dos, 0, dilation)
            _gather_residues(lse_ref, lse_res, 0, dilation)
            _gather_residues(dsum_ref, dsum_res, 0, dilation)
            dk_acc[...] = jnp.zeros_like(dk_acc)
            dv_acc[...] = jnp.zeros_like(dv_acc)
            geometry = _dil_geometry(grp, pair)

            def step(blk, _):
                off = pl.multiple_of(blk * BLK, BLK)
                q2 = _stack_heads(qs[pl.ds(off, BLK), :], lane0)
                do2 = _stack_heads(dos[pl.ds(off, BLK), :], lane0)
                kk = ks[pl.ds(off, 2 * BLK), :]
                vv = vs[pl.ds(off, 2 * BLK), :]
                lse_blk = lse_res[pl.ds(off, BLK), :]
                dsum_blk = dsum_res[pl.ds(off, BLK), :]
                lse2 = jnp.concatenate([_head_col(lse_blk, lane0), _head_col(lse_blk, lane1)], axis=0)
                dsum2 = jnp.concatenate([_head_col(dsum_blk, lane0), _head_col(dsum_blk, lane1)], axis=0)
                s = _dil_tile_scores(q2, kk, geometry, blk % per_residue != 0)
                p = jnp.exp(s - lse2)
                ds = (p * (_dot_nt(do2, vv) - dsum2)).astype(BF16)
                dq2 = _dot(ds, kk)
                dq_res[pl.ds(off, BLK), :] = jnp.where(lane0, dq2[:BLK], dq2[BLK:]) * QK_SCALE
                dk_acc[pl.ds(off, 2 * BLK), :] += _dot_tn(ds, q2)
                dv_acc[pl.ds(off, 2 * BLK), :] += _dot_tn(p.astype(BF16), do2)
                return 0

            lax.fori_loop(0, DIL_QBLOCKS, step, 0, unroll=8)
            _scatter_residues(dq_res, 0, dq_ref, dilation)
            _scatter_residues(dk_acc, BLK, dk_ref, dilation)
            _scatter_residues(dv_acc, BLK, dv_ref, dilation)

    def col(part, grp):
        return pl.BlockSpec((None, SEQ, LANES), lambda b, p: (b, 0, 6 * part + 2 * grp + p))

    slot = pl.BlockSpec((None, SEQ, LANES), lambda b, p: (b, 0, p))
    out = jax.ShapeDtypeStruct((B_LOC, SEQ, DIL_OUT), F32)
    return _call(
        body, send, name="dil_bwd", grid=(B_LOC, DIL_OUT // LANES),
        in_specs=[col(part, grp) for grp in range(DIL_GROUPS) for part in range(3)] + [slot] * 3,
        out_specs=[slot] * 9, out_shape=[out] * 9,
        scratch_shapes=[pltpu.VMEM((SEQ, LANES), BF16), pltpu.VMEM((SEQ + BLK, LANES), BF16),
                        pltpu.VMEM((SEQ + BLK, LANES), BF16), pltpu.VMEM((SEQ, LANES), BF16),
                        pltpu.VMEM((SEQ, LANES), F32), pltpu.VMEM((SEQ, LANES), F32),
                        pltpu.VMEM((SEQ, LANES), F32), pltpu.VMEM((SEQ + BLK, LANES), F32),
                        pltpu.VMEM((SEQ + BLK, LANES), F32)],
        semantics=("parallel", "parallel"), operands=[qkv] * 9 + [d_o, lse, dsum])


def _peers():
    x, y, c = lax.axis_index("x"), lax.axis_index("y"), lax.axis_index("c")
    me = 4 * x + 2 * y + c
    peers = []
    for mask in range(1, N_DEV):
        px = 1 - x if mask & 4 else x
        py = 1 - y if mask & 2 else y
        pc = 1 - c if mask & 1 else c
        peers.append(((px, py, pc), 4 * px + 2 * py + pc))
    return me, peers


def _all_gather(shard, name):
    def body(src_ref, out_ref, send_sems, recv_sems, local_sem):
        x, y, c = lax.axis_index("x"), lax.axis_index("y"), lax.axis_index("c")
        sibling = (x, y, 1 - c)
        chips = [(1 - x, y), (x, 1 - y), (1 - x, 1 - y)]

        def slot(px, py, pc):
            return out_ref.at[4 * px + 2 * py + pc]

        def copy(k, block, to, src=None):
            return pltpu.make_async_remote_copy(
                src_ref=slot(*block) if src is None else src, dst_ref=slot(*block),
                send_sem=send_sems.at[k], recv_sem=recv_sems.at[k], device_id=to,
                device_id_type=pl.DeviceIdType.MESH)

        mine = pltpu.make_async_copy(src_ref, slot(x, y, c), local_sem)
        mine.start()
        first = [copy(0, (x, y, c), sibling, src=src_ref)]
        first += [copy(1 + j, (x, y, c), (*chip, c), src=src_ref) for j, chip in enumerate(chips)]
        for cp in first:
            cp.start()
        passed = [copy(4 + j, (*chip, c), sibling) for j, chip in enumerate(chips)]
        for j, chip in enumerate(chips):
            copy(1 + j, (*chip, c), (x, y, c)).wait_recv()
            passed[j].start()
        copy(0, sibling, (x, y, c)).wait_recv()
        for j, chip in enumerate(chips):
            copy(4 + j, (*chip, 1 - c), (x, y, c)).wait_recv()
        for cp in first + passed:
            cp.wait_send()
        mine.wait()

    return pl.pallas_call(
        body, name=name,
        in_specs=[pl.BlockSpec(memory_space=pl.ANY)],
        out_specs=pl.BlockSpec(memory_space=pl.ANY),
        out_shape=jax.ShapeDtypeStruct((N_DEV,) + shard.shape, shard.dtype),
        scratch_shapes=[pltpu.SemaphoreType.DMA((N_DEV - 1,)), pltpu.SemaphoreType.DMA((N_DEV - 1,)),
                        pltpu.SemaphoreType.DMA],
    )(shard)


def _call(body, send, *, name, grid, in_specs, out_specs, out_shape, scratch_shapes, semantics, operands):
    if send is None:
        return pl.pallas_call(
            body, name=name, grid=grid, in_specs=in_specs, out_specs=out_specs, out_shape=out_shape,
            scratch_shapes=scratch_shapes, compiler_params=_params(semantics))(*operands), []
    srcs, kinds = [s for s, _ in send], [k for _, k in send]
    n, n_in, n_out, n_scr = len(srcs), len(in_specs), len(out_specs), len(scratch_shapes)
    steps = math.prod(grid)
    relay_step = (13 * steps) // 16

    def plan(refs):
        src_refs, land_refs = refs[n_in:n_in + n], refs[n_in + n + n_out:n_in + 2 * n + n_out]
        send_sems, recv_sems, local_sems = refs[-3:]
        x, y, c = lax.axis_index("x"), lax.axis_index("y"), lax.axis_index("c")
        me, peers = _peers()
        first, relayed_in, relayed_out, arrivals, sends, own = [], [], [], [], [], []
        for a, kind in enumerate(kinds):
            def copy(k, src, dst_slot, to):
                return pltpu.make_async_remote_copy(
                    src_ref=src, dst_ref=land_refs[a].at[dst_slot], send_sem=send_sems.at[a * (N_DEV - 1) + k],
                    recv_sem=recv_sems.at[a * (N_DEV - 1) + k], device_id=to, device_id_type=pl.DeviceIdType.MESH)

            if kind == "gather_by_chip":
                idx = lambda px, py, pc: 4 * px + 2 * py + pc
                chips = [(1 - x, y), (x, 1 - y), (1 - x, 1 - y)]
                mine = [copy(0, src_refs[a], me, (x, y, 1 - c))]
                arrivals.append(copy(0, src_refs[a], idx(x, y, 1 - c), (x, y, 1 - c)))
                for j, (px, py) in enumerate(chips):
                    mine.append(copy(1 + j, src_refs[a], me, (px, py, c)))
                    relayed_in.append(copy(1 + j, src_refs[a], idx(px, py, c), (px, py, c)))
                    relayed_out.append(copy(4 + j, land_refs[a].at[idx(px, py, c)], idx(px, py, c), (x, y, 1 - c)))
                    arrivals.append(copy(4 + j, src_refs[a], idx(px, py, 1 - c), (x, y, 1 - c)))
                first += mine
                sends += mine + relayed_out[-3:]
                own.append(pltpu.make_async_copy(src_refs[a], land_refs[a].at[me], local_sems.at[a]))
            elif kind == "scatter_by_chip":
                for k, (px, py) in enumerate([(1 - x, y), (x, 1 - y), (1 - x, 1 - y)]):
                    first.append(copy(k, src_refs[a].at[2 * px + py], 2 * x + y, (px, py, c)))
                    arrivals.append(copy(k, src_refs[a].at[2 * px + py], 2 * px + py, (px, py, c)))
                sends += first[-3:]
                own.append(pltpu.make_async_copy(src_refs[a].at[2 * x + y], land_refs[a].at[2 * x + y],
                                                 local_sems.at[a]))
            else:
                part = (lambda i: src_refs[a].at[i]) if kind == "scatter" else (lambda i: src_refs[a])
                for k, (peer, peer_idx) in enumerate(peers):
                    first.append(copy(k, part(peer_idx), me, peer))
                    arrivals.append(copy(k, part(peer_idx), peer_idx, peer))
                sends += first[-(N_DEV - 1):]
                own.append(pltpu.make_async_copy(part(me), land_refs[a].at[me], local_sems.at[a]))
        return first, relayed_in, relayed_out, arrivals, sends, own

    def wrapped(*refs):
        step = 0
        for axis, size in enumerate(grid):
            step = step * size + pl.program_id(axis)

        @pl.when(step == 0)
        def _():
            first, _, _, _, _, own = plan(refs)
            for cp in first + own:
                cp.start()

        if "gather_by_chip" in kinds:
            @pl.when(step == relay_step)
            def _():
                _, relayed_in, relayed_out, _, _, _ = plan(refs)
                for cp_in, cp_out in zip(relayed_in, relayed_out):
                    cp_in.wait_recv()
                    cp_out.start()

        body(*refs[:n_in], *refs[n_in + n:n_in + n + n_out], *refs[n_in + 2 * n + n_out:n_in + 2 * n + n_out + n_scr])

        @pl.when(step == steps - 1)
        def _():
            _, _, _, arrivals, sends, own = plan(refs)
            for cp in arrivals:
                cp.wait_recv()
            for cp in sends:
                cp.wait_send()
            for cp in own:
                cp.wait()

    anywhere = pl.BlockSpec(memory_space=pl.ANY)
    lands = [jax.ShapeDtypeStruct((N_DEV // 2 if k == "scatter_by_chip" else N_DEV,) + s.shape[-2:], s.dtype)
             for s, k in send]
    out = pl.pallas_call(
        wrapped, name=name, grid=grid,
        in_specs=list(in_specs) + [anywhere] * n, out_specs=list(out_specs) + [anywhere] * n,
        out_shape=list(out_shape) + lands,
        scratch_shapes=list(scratch_shapes) + [pltpu.SemaphoreType.DMA((n * (N_DEV - 1),)),
                                               pltpu.SemaphoreType.DMA((n * (N_DEV - 1),)),
                                               pltpu.SemaphoreType.DMA((n,))],
        compiler_params=_params(("arbitrary",) * len(grid)),
    )(*operands, *srcs)
    return out[:n_out], list(out[n_out:])


def _pair_swap(blocks):
    def body(src_ref, out_ref, send_sems, recv_sems):
        x, y, c = lax.axis_index("x"), lax.axis_index("y"), lax.axis_index("c")
        copies = [pltpu.make_async_remote_copy(
            src_ref=src_ref.at[2 * chip + (1 - c)], dst_ref=out_ref.at[chip], send_sem=send_sems.at[chip],
            recv_sem=recv_sems.at[chip], device_id=(x, y, 1 - c), device_id_type=pl.DeviceIdType.MESH)
            for chip in range(N_DEV // 2)]
        for cp in copies:
            cp.start()
        for cp in copies:
            cp.wait()

    return pl.pallas_call(
        body, name="pair_swap_grad_w_in",
        in_specs=[pl.BlockSpec(memory_space=pl.ANY)], out_specs=pl.BlockSpec(memory_space=pl.ANY),
        out_shape=jax.ShapeDtypeStruct((N_DEV // 2,) + blocks.shape[1:], blocks.dtype),
        scratch_shapes=[pltpu.SemaphoreType.DMA((N_DEV // 2,)), pltpu.SemaphoreType.DMA((N_DEV // 2,))],
    )(blocks)


def _pair_sum(blocks, swapped, core):
    _, rows, cols = swapped.shape
    tile_rows = _row_tile(rows)

    def body(core_ref, mine_ref, theirs_ref, o_ref):
        o_ref[...] = (mine_ref[...].astype(F32) + theirs_ref[...].astype(F32)).astype(o_ref.dtype)

    return pl.pallas_call(
        body, name="pair_sum_grad_w_in",
        grid_spec=pltpu.PrefetchScalarGridSpec(
            num_scalar_prefetch=1, grid=(N_DEV // 2, rows // tile_rows),
            in_specs=[pl.BlockSpec((None, tile_rows, cols), lambda j, i, core_ref: (2 * j + core_ref[0], i, 0)),
                      pl.BlockSpec((None, tile_rows, cols), lambda j, i, core_ref: (j, i, 0))],
            out_specs=pl.BlockSpec((None, tile_rows, cols), lambda j, i, core_ref: (j, i, 0))),
        out_shape=jax.ShapeDtypeStruct(swapped.shape, swapped.dtype),
        compiler_params=_params(("parallel", "parallel")),
    )(core, blocks, swapped)


def _sum_in_device_order(land_ref):
    acc = land_ref[0].astype(F32)
    for j in range(1, land_ref.shape[0]):
        acc = acc + land_ref[j].astype(F32)
    return acc


def _adam_math(w, g, m, v):
    c1 = 1.0 - ADAM_B1 ** ADAM_STEP
    c2 = 1.0 - ADAM_B2 ** ADAM_STEP
    m_new = ADAM_B1 * m + (1.0 - ADAM_B1) * g
    v_new = ADAM_B2 * v + (1.0 - ADAM_B2) * (g * g)
    delta = -ADAM_LR * ((m_new / c1) / (jnp.sqrt(v_new / c2) + ADAM_EPS) + ADAM_WD * w)
    return delta, m_new, v_new


def _row_tile(rows):
    return max(t for t in range(8, 257, 8) if rows % t == 0) if rows % 8 == 0 else rows


def _sum_update(land, w, m, v, name):
    slots, rows, cols = land.shape
    tile_rows = _row_tile(rows)

    def body(land_ref, w_ref, m_ref, v_ref, g_ref, d_ref, nm_ref, nv_ref):
        g = _sum_in_device_order(land_ref)
        g_ref[...] = g
        d_ref[...], nm_ref[...], nv_ref[...] = _adam_math(w_ref[...], g, m_ref[...], v_ref[...])

    tile = pl.BlockSpec((None, tile_rows, cols), lambda i: (0, i, 0))
    out = jax.ShapeDtypeStruct((1, rows, cols), F32)
    return pl.pallas_call(
        body, name=name, grid=(rows // tile_rows,),
        in_specs=[pl.BlockSpec((slots, tile_rows, cols), lambda i: (0, i, 0)), tile, tile, tile],
        out_specs=[tile] * 4, out_shape=[out] * 4,
        compiler_params=_params(("parallel",)),
    )(land, w, m, v)


def _sum_gains(land):
    def body(land_ref, o_ref):
        o_ref[...] = _sum_in_device_order(land_ref)

    return pl.pallas_call(
        body, name="sum_gain_grads", grid=(1,),
        in_specs=[pl.BlockSpec(land.shape, lambda i: (0, 0, 0))],
        out_specs=pl.BlockSpec(land.shape[1:], lambda i: (0, 0)),
        out_shape=jax.ShapeDtypeStruct(land.shape[1:], F32),
    )(land)


def _adamw(w, g, m, v, name):
    def body(w_ref, g_ref, m_ref, v_ref, d_ref, nm_ref, nv_ref):
        d_ref[...], nm_ref[...], nv_ref[...] = _adam_math(w_ref[...], g_ref[...], m_ref[...], v_ref[...])

    whole = pl.BlockSpec(w.shape, lambda i: (0, 0))
    out = jax.ShapeDtypeStruct(w.shape, F32)
    return pl.pallas_call(
        body, name=name, grid=(1,),
        in_specs=[whole] * 4, out_specs=[whole] * 3, out_shape=[out] * 3,
    )(w, g, m, v)


GROUP_FFN = ("w_ffn_in", "w_ffn_out")
GROUP_MIX = ("w_sb_up", "w_dil_up", "w_out")
COL_SHARDED = ("w_in", "w_sb_up", "w_dil_up", "w_ffn_in")


def _full_from_shards(name, slots):
    _, r, c = slots.shape
    if name in COL_SHARDED:
        return slots.transpose(1, 0, 2).reshape(r, N_DEV * c)
    return slots.reshape(N_DEV * r, c)


def _shards_from_full(name, full):
    rows, cols = full.shape
    if name in COL_SHARDED:
        return full.reshape(rows, N_DEV, cols // N_DEV).transpose(1, 0, 2)
    return full.reshape(N_DEV, rows // N_DEV, cols)


def _local_step(x, target, g_mix, g_ffn, g_fin, w_in, shards=None, rest=None):
    gather = lambda names, kind: None if shards is None else [(shards[n], kind) for n in names]
    scatter = lambda blocks: None if shards is None else [(t, "scatter") for t in blocks]
    landed = lambda blocks, lands: lands if lands else blocks

    w = {"w_in": w_in}
    if shards is None:
        w.update(rest)
        w["w_ffn_in"] = _shards_from_full("w_ffn_in", rest["w_ffn_in"])
    (qkv_sb, qkv_dl, gates, u), _ = _norm_proj(x, g_mix, w["w_in"])
    qkv_sb = qkv_sb.reshape(B_LOC, SEQ, 3 * SB_WIDTH)
    qkv_dl = qkv_dl.reshape(B_LOC, SEQ, 3 * DIL_WIDTH)
    (o_sb,), lands = _sb_fwd(qkv_sb, gather(GROUP_FFN, "gather_by_chip"))
    if lands:
        w["w_ffn_in"], w["w_ffn_out"] = lands[0], _full_from_shards("w_ffn_out", lands[1])
    o_sb = o_sb.reshape(TOK, SB_WIDTH)
    (o_dl, lse), lands = _dil_fwd(qkv_dl, gather(GROUP_MIX, "gather"))
    w.update({n: _full_from_shards(n, t) for n, t in zip(GROUP_MIX, lands)})
    o_dl = o_dl.reshape(TOK, DIL_OUT)

    loss, dx1, merged, u2, act, dh, dx2, dg_fin, dg_ffn = _mix_ffn_fwd_bwd(
        x, o_sb, o_dl, gates, w["w_sb_up"], w["w_dil_up"], w["w_out"], target, g_ffn, g_fin,
        w["w_ffn_in"], w["w_ffn_out"])
    dgates, dy_sb, dy_dl, do_sb, do_dl, dsum = _mix_bwd(dx1, o_sb, o_dl, gates, w["w_sb_up"], w["w_dil_up"], w["w_out"])
    blocks = {
        "w_sb_up": _atb(o_sb, dy_sb, "grad_w_sb_up", SB_WIDTH, D_MODEL, col_blocks=N_DEV),
        "w_dil_up": _atb(o_dl, dy_dl, "grad_w_dil_up", DIL_OUT, D_MODEL, col_blocks=N_DEV),
        "w_out": _shards_from_full("w_out", _atb(merged, dx1, "grad_w_out", D_MODEL, D_MODEL)),
        "w_ffn_in": _atb_shards(u2, dh, "grad_w_ffn_in"),
        "w_ffn_out": _shards_from_full("w_ffn_out", _atb_shards(act, dx2, "grad_w_ffn_out")),
    }
    grads = {}

    early, late = GROUP_FFN, GROUP_MIX
    early_blocks = [blocks[n] for n in early]
    (dq_sb, dk_sb, dv_sb), lands = _sb_bwd(qkv_sb, do_sb.reshape(B_LOC, SEQ, SB_WIDTH), scatter(early_blocks))
    grads.update(zip(early, landed(early_blocks, lands)))
    as_batch = lambda t: t.reshape(B_LOC, SEQ, DIL_OUT)
    late_blocks = [blocks[n] for n in late]
    d_dl, lands = _dil_bwd(qkv_dl, as_batch(do_dl), lse, as_batch(dsum), scatter(late_blocks))
    grads.update(zip(late, landed(late_blocks, lands)))
    flat = lambda t: t.reshape(TOK, -1)
    dproj = ([flat(dq_sb), flat(dk_sb), flat(dv_sb)]
             + [flat(d_dl[3 * grp + part]) for part in range(3) for grp in range(DIL_GROUPS)] + [dgates])

    w_in_blocks = _shards_from_full("w_in", _atb_pieces(u, dproj, "grad_w_in", D_MODEL // 2))
    if shards is None:
        grads["w_in"] = w_in_blocks
        send = None
    else:
        core = lax.axis_index("c").astype(jnp.int32).reshape(1)
        send = [(_pair_sum(w_in_blocks, _pair_swap(w_in_blocks), core), "scatter_by_chip")]
    (grad_x, dg_mix), lands = _proj_bwd(dproj, dx1, x, g_mix, w["w_in"], send)
    if lands:
        grads["w_in"] = lands[0]
    gain_grads = jnp.concatenate([dg_mix, dg_ffn, dg_fin], axis=0)
    return loss, grad_x, gain_grads, grads


def kernel(x, norm_mix_g, w_in, w_sb_up, w_dil_up, w_out, norm_ffn_g, w_ffn_in, w_ffn_out, norm_final_g, loss_target, m_norm_mix_g, m_w_in, m_w_sb_up, m_w_dil_up, m_w_out, m_norm_ffn_g, m_w_ffn_in, m_w_ffn_out, m_norm_final_g, v_norm_mix_g, v_w_in, v_w_sb_up, v_w_dil_up, v_w_out, v_norm_ffn_g, v_w_ffn_in, v_w_ffn_out, v_norm_final_g):
    mats = {"w_in": w_in, "w_sb_up": w_sb_up, "w_dil_up": w_dil_up, "w_out": w_out,
            "w_ffn_in": w_ffn_in, "w_ffn_out": w_ffn_out}
    moments_m = {"w_in": m_w_in, "w_sb_up": m_w_sb_up, "w_dil_up": m_w_dil_up, "w_out": m_w_out,
                 "w_ffn_in": m_w_ffn_in, "w_ffn_out": m_w_ffn_out}
    moments_v = {"w_in": v_w_in, "w_sb_up": v_w_sb_up, "w_dil_up": v_w_dil_up, "w_out": v_w_out,
                 "w_ffn_in": v_w_ffn_in, "w_ffn_out": v_w_ffn_out}
    gathered_w_in = _all_gather(w_in[0].astype(BF16), "all_gather_w_in")
    g_fin = norm_final_g.reshape(1, D_MODEL)
    loss, grad_x, gain_grads, grad_slots = _local_step(
        x.reshape(TOK, D_MODEL), loss_target.reshape(TOK, D_MODEL), norm_mix_g, norm_ffn_g, g_fin,
        _full_from_shards("w_in", gathered_w_in),
        shards={name: mats[name][0].astype(BF16) for name in GROUP_FFN + GROUP_MIX})

    gain_rows = jnp.concatenate([gain_grads, jnp.tile(loss, (1, D_MODEL // LANES)),
                                 jnp.zeros((8 - 4, D_MODEL), F32)], axis=0)
    g_gains = _sum_gains(_all_gather(gain_rows, "all_gather_gains"))

    out_g, out_d, out_m, out_v = {}, {}, {}, {}
    for name, slots in grad_slots.items():
        out_g[name], out_d[name], out_m[name], out_v[name] = _sum_update(
            slots, mats[name], moments_m[name], moments_v[name], "update_" + name)

    gain_w = jnp.concatenate([norm_mix_g, norm_ffn_g, g_fin], axis=0)
    gain_m = jnp.concatenate([m_norm_mix_g, m_norm_ffn_g, m_norm_final_g.reshape(1, D_MODEL)], axis=0)
    gain_v = jnp.concatenate([v_norm_mix_g, v_norm_ffn_g, v_norm_final_g.reshape(1, D_MODEL)], axis=0)
    gd, gm, gv = _adamw(gain_w, g_gains[:3], gain_m, gain_v, "adamw_gains")
    for idx, name in enumerate(("norm_mix_g", "norm_ffn_g", "norm_final_g")):
        shape = (D_MODEL,) if name == "norm_final_g" else (1, D_MODEL)
        out_g[name] = g_gains[idx].reshape(shape)
        out_d[name], out_m[name], out_v[name] = gd[idx].reshape(shape), gm[idx].reshape(shape), gv[idx].reshape(shape)

    order = ("norm_mix_g", "w_in", "w_sb_up", "w_dil_up", "w_out", "norm_ffn_g", "w_ffn_in", "w_ffn_out",
             "norm_final_g")
    return (g_gains[3, 0], grad_x.reshape(B_LOC, SEQ, D_MODEL),
            *[out_g[n] for n in order], *[out_d[n] for n in order],
            *[out_m[n] for n in order], *[out_v[n] for n in order])
```

```python
import math

import jax
import jax.numpy as jnp
from jax import lax
from jax.experimental import pallas as pl
from jax.experimental.pallas import tpu as pltpu

F32 = jnp.float32
BF16 = jnp.bfloat16

N_DEV = 8
D_MODEL = 1024
SEQ = 2048
B_LOC = 2
TOK = B_LOC * SEQ
HEAD_DIM = 64
SB_WIDTH = 512
DIL_WIDTH = 768
DIL_OUT = 256
QKV_WIDTH = 3 * SB_WIDTH + 3 * DIL_WIDTH
IN_WIDTH = QKV_WIDTH + 2 * D_MODEL
D_FF = 2816
DIL_PAIRS = ((128, 1), (512, 4), (2048, 16))
DIL_HEADS = 12
RMS_EPS = 1e-6
ALIBI_MAX_BIAS = 8.0
QK_SCALE = 1.0 / math.sqrt(HEAD_DIM)
BLK = 128
LANES = 128
NEG_BIG = -1e30

ADAM_LR = 0.001
ADAM_B1 = 0.9
ADAM_B2 = 0.999
ADAM_EPS = 1e-08
ADAM_WD = 0.01
ADAM_STEP = 10

VMEM_LIMIT = 56 * 1024 * 1024


def _dot(a, b):
    return jnp.dot(a, b, preferred_element_type=F32)


def _dot_nt(a, b):
    return lax.dot_general(a, b, (((1,), (1,)), ((), ())), preferred_element_type=F32)


def _dot_tn(a, b):
    return lax.dot_general(a, b, (((0,), (0,)), ((), ())), preferred_element_type=F32)


def _sigmoid(z):
    return 1.0 / (1.0 + jnp.exp(-z))


def _split_bf16(v):
    hi = v.astype(BF16)
    lo = (v - hi.astype(F32)).astype(BF16)
    return hi, lo


def _chunks(width, step=512):
    out, c = [], 0
    while c < width:
        w = min(step, width - c)
        out.append((c, w))
        c += w
    return out


def _resident(shape):
    nd = len(shape)
    return pl.BlockSpec(shape, lambda *_: (0,) * nd, pipeline_mode=pl.Buffered(1))


def _params(sem):
    return pltpu.CompilerParams(dimension_semantics=sem, vmem_limit_bytes=VMEM_LIMIT)


def _rms_fwd(x, g):
    r = lax.rsqrt(jnp.mean(x * x, axis=-1, keepdims=True) + RMS_EPS)
    n = x * r
    return n, r, n * g


def _rms_bwd(dy, n, r, g):
    dg = jnp.sum(dy * n, axis=0, keepdims=True)
    dn = dy * g
    dx = r * (dn - n * jnp.mean(dn * n, axis=-1, keepdims=True))
    return dx, dg


TM = 256


def _norm_proj(x, g, w_in, send=None):
    def body(x_ref, g_ref, w_ref, sb_ref, dl_ref, gate_ref, u_ref):
        _, _, u = _rms_fwd(x_ref[...], g_ref[...])
        u = u.astype(BF16)
        u_ref[...] = u
        for c0, w in _chunks(3 * SB_WIDTH):
            sb_ref[:, c0:c0 + w] = _dot(u, w_ref[:, c0:c0 + w]).astype(BF16)
        for c0, w in _chunks(3 * DIL_WIDTH):
            dl_ref[:, c0:c0 + w] = _dot(u, w_ref[:, 3 * SB_WIDTH + c0:3 * SB_WIDTH + c0 + w])
        for c0, w in _chunks(2 * D_MODEL):
            gate_ref[:, c0:c0 + w] = _dot(u, w_ref[:, QKV_WIDTH + c0:QKV_WIDTH + c0 + w])

    return _call(
        body, send, name="norm_proj", grid=(TOK // TM,),
        in_specs=[pl.BlockSpec((TM, D_MODEL), lambda i: (i, 0)), _resident((1, D_MODEL)),
                  _resident((D_MODEL, IN_WIDTH))],
        out_specs=[pl.BlockSpec((TM, 3 * SB_WIDTH), lambda i: (i, 0)),
                   pl.BlockSpec((TM, 3 * DIL_WIDTH), lambda i: (i, 0)),
                   pl.BlockSpec((TM, 2 * D_MODEL), lambda i: (i, 0)),
                   pl.BlockSpec((TM, D_MODEL), lambda i: (i, 0))],
        out_shape=[jax.ShapeDtypeStruct((TOK, 3 * SB_WIDTH), BF16),
                   jax.ShapeDtypeStruct((TOK, 3 * DIL_WIDTH), F32),
                   jax.ShapeDtypeStruct((TOK, 2 * D_MODEL), F32),
                   jax.ShapeDtypeStruct((TOK, D_MODEL), BF16)],
        scratch_shapes=[], semantics=("parallel",), operands=(x, g, w_in))


FF_SHARD = 2 * D_FF // N_DEV
FF_PAIRS = N_DEV // 2


def _mix_ffn_fwd_bwd(x, o_sb, o_dl, gates, w_sb_up, w_dil_up, w_out, target, g_ffn, g_fin, w_ffn_in, w_ffn_out):
    def body(x_ref, osb_ref, odl_ref, gate_ref, wsb_ref, wdl_ref, wo_ref, t_ref, gffn_ref, gfin_ref, win_ref, wout_ref,
             loss_ref, dx1_ref, mg_ref, u2_ref, act_ref, dh_ref, dx2_ref, dgfin_ref, dgffn_ref, h_scr):
        i = pl.program_id(0)

        @pl.when(i == 0)
        def _():
            loss_ref[...] = jnp.zeros_like(loss_ref)
            dgfin_ref[...] = jnp.zeros_like(dgfin_ref)
            dgffn_ref[...] = jnp.zeros_like(dgffn_ref)

        y_sb = _dot(osb_ref[...], wsb_ref[...])
        y_dl = _dot(odl_ref[...].astype(BF16), wdl_ref[...])
        merged = (_sigmoid(gate_ref[:, :D_MODEL]) * y_sb
                  + _sigmoid(gate_ref[:, D_MODEL:]) * y_dl).astype(BF16)
        mg_ref[...] = merged
        x1 = x_ref[...] + _dot(merged, wo_ref[...])
        g_ffn_v = gffn_ref[...]
        g_fin_v = gfin_ref[...]
        n2, r2, u2 = _rms_fwd(x1, g_ffn_v)
        u2 = u2.astype(BF16)
        u2_ref[...] = u2
        x2 = x1
        for r in range(FF_PAIRS):
            gate = _dot(u2, win_ref[r])
            up = _dot(u2, win_ref[r + FF_PAIRS])
            h_scr[r] = gate
            h_scr[r + FF_PAIRS] = up
            act = (gate * _sigmoid(gate) * up).astype(BF16)
            act_ref[r] = act
            x2 = x2 + _dot(act, wout_ref[r * FF_SHARD:(r + 1) * FF_SHARD, :])
        n3, r3, y = _rms_fwd(x2, g_fin_v)
        err = y - t_ref[...]
        sq = jnp.sum(jnp.sum(err * err, axis=1, keepdims=True), axis=0, keepdims=True)
        loss_ref[...] += sq * (0.5 / D_MODEL)
        dx2, dgfin = _rms_bwd(err * (1.0 / D_MODEL), n3, r3, g_fin_v)
        dgfin_ref[...] += dgfin
        dx2_b = dx2.astype(BF16)
        dx2_ref[...] = dx2_b
        du2 = jnp.zeros((TM, D_MODEL), F32)
        for r in range(FF_PAIRS):
            gate = h_scr[r]
            up = h_scr[r + FF_PAIRS]
            dact = _dot_nt(dx2_b, wout_ref[r * FF_SHARD:(r + 1) * FF_SHARD, :])
            sg = _sigmoid(gate)
            dgate = (dact * up * (sg * (1.0 + gate * (1.0 - sg)))).astype(BF16)
            dup = (dact * (gate * sg)).astype(BF16)
            dh_ref[r] = dgate
            dh_ref[r + FF_PAIRS] = dup
            du2 = du2 + _dot_nt(dgate, win_ref[r])
            du2 = du2 + _dot_nt(dup, win_ref[r + FF_PAIRS])
        dx1_n, dgffn = _rms_bwd(du2, n2, r2, g_ffn_v)
        dgffn_ref[...] += dgffn
        dx1_ref[...] = dx2 + dx1_n

    tile = lambda w: pl.BlockSpec((TM, w), lambda i: (i, 0))
    shards = lambda n: pl.BlockSpec((n, TM, FF_SHARD), lambda i: (0, i, 0))
    acc = lambda w: pl.BlockSpec((1, w), lambda i: (0, 0))
    return pl.pallas_call(
        body, name="mix_ffn_fwd_bwd", grid=(TOK // TM,),
        in_specs=[tile(D_MODEL), tile(SB_WIDTH), tile(DIL_OUT), tile(2 * D_MODEL),
                  _resident((SB_WIDTH, D_MODEL)), _resident((DIL_OUT, D_MODEL)), _resident((D_MODEL, D_MODEL)),
                  tile(D_MODEL), _resident((1, D_MODEL)), _resident((1, D_MODEL)),
                  _resident((N_DEV, D_MODEL, FF_SHARD)), _resident((D_FF, D_MODEL))],
        out_specs=[acc(LANES), tile(D_MODEL), tile(D_MODEL), tile(D_MODEL), shards(FF_PAIRS), shards(N_DEV),
                   tile(D_MODEL), acc(D_MODEL), acc(D_MODEL)],
        out_shape=[jax.ShapeDtypeStruct((1, LANES), F32),
                   jax.ShapeDtypeStruct((TOK, D_MODEL), F32),
                   jax.ShapeDtypeStruct((TOK, D_MODEL), BF16),
                   jax.ShapeDtypeStruct((TOK, D_MODEL), BF16),
                   jax.ShapeDtypeStruct((FF_PAIRS, TOK, FF_SHARD), BF16),
                   jax.ShapeDtypeStruct((N_DEV, TOK, FF_SHARD), BF16),
                   jax.ShapeDtypeStruct((TOK, D_MODEL), BF16),
                   jax.ShapeDtypeStruct((1, D_MODEL), F32),
                   jax.ShapeDtypeStruct((1, D_MODEL), F32)],
        scratch_shapes=[pltpu.VMEM((N_DEV, TM, FF_SHARD), F32)],
        compiler_params=_params(("arbitrary",)),
    )(x, o_sb, o_dl, gates, w_sb_up, w_dil_up, w_out, target, g_ffn, g_fin, w_ffn_in, w_ffn_out)


def _mix_bwd(dx1, o_sb, o_dl, gates, w_sb_up, w_dil_up, w_out):
    def body(dx1_ref, osb_ref, odl_ref, gate_ref, wsb_ref, wdl_ref, wout_ref,
             dgate_ref, dysb_ref, dydl_ref, dosb_ref, dodl_ref, dsum_ref):
        dmerged = _dot_nt(dx1_ref[...].astype(BF16), wout_ref[...])
        o_dl = odl_ref[...]
        y_sb = _dot(osb_ref[...], wsb_ref[...])
        y_dl = _dot(o_dl.astype(BF16), wdl_ref[...])
        s_sb = _sigmoid(gate_ref[:, :D_MODEL])
        s_dl = _sigmoid(gate_ref[:, D_MODEL:])
        dgate_ref[:, :D_MODEL] = (dmerged * y_sb * (s_sb * (1.0 - s_sb))).astype(BF16)
        dgate_ref[:, D_MODEL:] = (dmerged * y_dl * (s_dl * (1.0 - s_dl))).astype(BF16)
        dy_sb = (dmerged * s_sb).astype(BF16)
        dy_dl = (dmerged * s_dl).astype(BF16)
        dysb_ref[...] = dy_sb
        dydl_ref[...] = dy_dl
        dosb_ref[...] = _dot_nt(dy_sb, wsb_ref[...]).astype(BF16)
        do_dl = _dot_nt(dy_dl, wdl_ref[...])
        dodl_ref[...] = do_dl
        row = lax.broadcasted_iota(jnp.int32, (DIL_OUT, DIL_OUT), 0) // HEAD_DIM
        col = lax.broadcasted_iota(jnp.int32, (DIL_OUT, DIL_OUT), 1) // HEAD_DIM
        same_head = (row == col).astype(BF16)
        hi, lo = _split_bf16(do_dl * o_dl)
        dsum_ref[...] = _dot(hi, same_head) + _dot(lo, same_head)

    tile = lambda w: pl.BlockSpec((TM, w), lambda i: (i, 0))
    return pl.pallas_call(
        body, name="mix_bwd", grid=(TOK // TM,),
        in_specs=[tile(D_MODEL), tile(SB_WIDTH), tile(DIL_OUT), tile(2 * D_MODEL),
                  _resident((SB_WIDTH, D_MODEL)), _resident((DIL_OUT, D_MODEL)),
                  _resident((D_MODEL, D_MODEL))],
        out_specs=[tile(2 * D_MODEL), tile(D_MODEL), tile(D_MODEL), tile(SB_WIDTH), tile(DIL_OUT),
                   tile(DIL_OUT)],
        out_shape=[jax.ShapeDtypeStruct((TOK, 2 * D_MODEL), BF16),
                   jax.ShapeDtypeStruct((TOK, D_MODEL), BF16),
                   jax.ShapeDtypeStruct((TOK, D_MODEL), BF16),
                   jax.ShapeDtypeStruct((TOK, SB_WIDTH), BF16),
                   jax.ShapeDtypeStruct((TOK, DIL_OUT), F32),
                   jax.ShapeDtypeStruct((TOK, DIL_OUT), F32)],
        compiler_params=_params(("parallel",)),
    )(dx1, o_sb, o_dl, gates, w_sb_up, w_dil_up, w_out)


def _proj_bwd(dproj, dx1, x, g, w_in, send=None):
    widths = [p.shape[1] for p in dproj]

    def body(*refs):
        dx1_ref, x_ref, g_ref, w_ref, dx_ref, dg_ref = refs[len(widths):]

        @pl.when(pl.program_id(0) == 0)
        def _():
            dg_ref[...] = jnp.zeros_like(dg_ref)

        du = jnp.zeros((TM, D_MODEL), F32)
        c0 = 0
        for dp_ref, w in zip(refs, widths):
            du = du + _dot_nt(dp_ref[...].astype(BF16), w_ref[:, c0:c0 + w])
            c0 += w
        g_v = g_ref[...]
        n, r, _ = _rms_fwd(x_ref[...], g_v)
        dx, dg = _rms_bwd(du, n, r, g_v)
        dg_ref[...] += dg
        dx_ref[...] = dx1_ref[...] + dx

    tile = lambda w: pl.BlockSpec((TM, w), lambda i: (i, 0))
    return _call(
        body, send, name="proj_bwd", grid=(TOK // TM,),
        in_specs=[tile(w) for w in widths] + [tile(D_MODEL), tile(D_MODEL), _resident((1, D_MODEL)),
                                              _resident((D_MODEL, IN_WIDTH))],
        out_specs=[tile(D_MODEL), pl.BlockSpec((1, D_MODEL), lambda i: (0, 0))],
        out_shape=[jax.ShapeDtypeStruct((TOK, D_MODEL), F32),
                   jax.ShapeDtypeStruct((1, D_MODEL), F32)],
        scratch_shapes=[], semantics=("arbitrary",), operands=(*dproj, dx1, x, g, w_in))


def _atb_pieces(a, pieces, name, tm, tk=512):
    m = a.shape[1]
    widths = [p.shape[1] for p in pieces]
    n = sum(widths)
    nk = TOK // tk

    def body(a_ref, *refs):
        o_ref, acc_ref = refs[len(widths):]
        k = pl.program_id(1)

        @pl.when(k == 0)
        def _():
            acc_ref[...] = jnp.zeros_like(acc_ref)

        a_v = a_ref[...]
        c0 = 0
        for p_ref, w in zip(refs, widths):
            acc_ref[:, c0:c0 + w] += _dot_tn(a_v, p_ref[...].astype(BF16))
            c0 += w

        @pl.when(k == nk - 1)
        def _():
            o_ref[...] = acc_ref[...].astype(BF16)

    return pl.pallas_call(
        body, name=name, grid=(m // tm, nk),
        in_specs=[pl.BlockSpec((tk, tm), lambda i, k: (k, i))]
                 + [pl.BlockSpec((tk, w), lambda i, k: (k, 0)) for w in widths],
        out_specs=pl.BlockSpec((tm, n), lambda i, k: (i, 0)),
        out_shape=jax.ShapeDtypeStruct((m, n), BF16),
        scratch_shapes=[pltpu.VMEM((tm, n), F32)],
        compiler_params=_params(("parallel", "arbitrary")),
    )(a, *pieces)


def _atb_shards(a, b, name, tk=512):
    a_sharded = a.ndim == 3
    n, _, w = a.shape if a_sharded else b.shape
    other = (b if a_sharded else a).shape[1]

    def body(a_ref, b_ref, o_ref):
        acc = jnp.zeros(o_ref.shape, F32)
        for k0 in range(0, TOK, tk):
            acc = acc + _dot_tn(a_ref[k0:k0 + tk, :], b_ref[k0:k0 + tk, :])
        o_ref[...] = acc.astype(BF16)

    shard = pl.BlockSpec((None, TOK, w), lambda r: (r, 0, 0))
    whole = _resident((TOK, other))
    if a_sharded:
        out_spec, out_shape = pl.BlockSpec((w, other), lambda r: (r, 0)), (n * w, other)
    else:
        out_spec, out_shape = pl.BlockSpec((None, other, w), lambda r: (r, 0, 0)), (n, other, w)
    return pl.pallas_call(
        body, name=name, grid=(n,),
        in_specs=[shard, whole] if a_sharded else [whole, shard],
        out_specs=out_spec, out_shape=jax.ShapeDtypeStruct(out_shape, BF16),
        compiler_params=_params(("parallel",)),
    )(a, b)


def _atb(a, b, name, tm, tn, col_blocks=0, tk=512):
    m, n = a.shape[1], b.shape[1]
    nk = TOK // tk

    def body(a_ref, b_ref, o_ref, acc_ref):
        k = pl.program_id(2)

        @pl.when(k == 0)
        def _():
            acc_ref[...] = jnp.zeros_like(acc_ref)

        acc_ref[...] += _dot_tn(a_ref[...].astype(BF16), b_ref[...].astype(BF16))

        @pl.when(k == nk - 1)
        def _():
            if col_blocks:
                width = n // col_blocks
                for blk in range(col_blocks):
                    o_ref[blk] = acc_ref[:, blk * width:(blk + 1) * width].astype(BF16)
            else:
                o_ref[...] = acc_ref[...].astype(BF16)

    if col_blocks:
        out_spec = pl.BlockSpec((col_blocks, tm, n // col_blocks), lambda i, j, k: (0, i, 0))
        out_shape = jax.ShapeDtypeStruct((col_blocks, m, n // col_blocks), BF16)
    else:
        out_spec = pl.BlockSpec((tm, tn), lambda i, j, k: (i, j))
        out_shape = jax.ShapeDtypeStruct((m, n), BF16)
    return pl.pallas_call(
        body, name=name, grid=(m // tm, n // tn, nk),
        in_specs=[pl.BlockSpec((tk, tm), lambda i, j, k: (k, i)),
                  pl.BlockSpec((tk, tn), lambda i, j, k: (k, j))],
        out_specs=out_spec, out_shape=out_shape,
        scratch_shapes=[pltpu.VMEM((tm, tn), F32)],
        compiler_params=_params(("parallel", "parallel", "arbitrary")),
    )(a, b)


SB_PAIRS = SB_WIDTH // LANES


def _two_heads(v, lane0):
    zero = jnp.zeros_like(v)
    return jnp.where(lane0, v, zero), jnp.where(lane0, zero, v)


SB_QBLK = 256
N_SB_STEPS = SEQ // SB_QBLK


SB_KCHUNK = 2 * BLK
SB_ROWS = 2 * SB_QBLK
SB_DEAD = -104.0


def _log_keep(z):
    neg_z = -z
    return jnp.minimum(neg_z, 0.0) - jnp.log(1.0 + jnp.exp(jnp.minimum(z, neg_z)))


def _stack_heads(v, lane0):
    return jnp.concatenate(_two_heads(v, lane0), axis=0)


def _block_sums(v, tri, split=True):
    halves = (v[:, :BLK], v[:, BLK:])
    stacked = jnp.concatenate(halves, axis=0)
    if split:
        hi, lo = _split_bf16(stacked)
        prod = _dot(jnp.concatenate([hi, lo], axis=0), tri)
        tri_sum = prod[:2 * SB_ROWS] + prod[2 * SB_ROWS:]
    else:
        tri_sum = _dot(stacked.astype(BF16), tri)
    sums = tuple(jnp.sum(h, axis=1, keepdims=True) for h in halves)
    return (tri_sum[:SB_ROWS], tri_sum[SB_ROWS:]), sums


def _sb_diag_mask():
    row = lax.broadcasted_iota(jnp.int32, (SB_ROWS, SB_KCHUNK), 0)
    col = lax.broadcasted_iota(jnp.int32, (SB_ROWS, SB_KCHUNK), 1)
    return col < jnp.where(row >= SB_QBLK, row - SB_QBLK, row)


def _sb_fwd(qkv, send=None):
    def body(q_ref, k_ref, v_ref, o_ref):
        i = pl.program_id(2)
        krow = lax.broadcasted_iota(jnp.int32, (BLK, BLK), 0)
        kcol = lax.broadcasted_iota(jnp.int32, (BLK, BLK), 1)
        later = (krow > kcol).astype(BF16)
        lane0 = lax.broadcasted_iota(jnp.int32, (SB_QBLK, LANES), 1) < HEAD_DIM
        q2 = _stack_heads(q_ref[0] * QK_SCALE, lane0)

        def chunk(c, carry, causal):
            acc, run = carry
            off = pl.multiple_of(c * SB_KCHUNK, SB_KCHUNK)
            z = _dot_nt(q2, k_ref[0, pl.ds(off, SB_KCHUNK), :])
            log_keep = _log_keep(z)
            if causal is not None:
                log_keep = jnp.where(causal, log_keep, 0.0)
            suffix, sums = _block_sums(log_keep, later)
            log_after = jnp.concatenate([suffix[0] + (run + sums[1]), suffix[1] + run], axis=1)
            a = jnp.exp(log_keep + z + log_after)
            if causal is not None:
                a = jnp.where(causal, a, 0.0)
            acc = acc + _dot(a.astype(BF16), v_ref[0, pl.ds(off, SB_KCHUNK), :])
            return acc, run + (sums[0] + sums[1])

        acc, run = chunk(i, (jnp.zeros((SB_ROWS, LANES), F32), jnp.zeros((SB_ROWS, 1), F32)), _sb_diag_mask())

        def some_alive(run):
            return (jnp.max(run) > SB_DEAD).astype(jnp.int32)

        def trip(state):
            t, _, acc, run = state
            acc, run = chunk(i - 1 - t, (acc, run), None)
            return t + 1, some_alive(run), acc, run

        _, _, acc, _ = lax.while_loop(lambda s: jnp.logical_and(s[0] < i, s[1] > 0), trip,
                                      (jnp.int32(0), some_alive(run), acc, run))
        o_ref[0] = jnp.where(lane0, acc[:SB_QBLK], acc[SB_QBLK:]).astype(BF16)

    blk = pl.BlockSpec((1, SB_QBLK, LANES), lambda b, h, i: (b, i, h))
    return _call(
        body, send, name="sb_fwd", grid=(B_LOC, SB_PAIRS, N_SB_STEPS),
        in_specs=[blk,
                  pl.BlockSpec((1, SEQ, LANES), lambda b, h, i: (b, 0, SB_PAIRS + h)),
                  pl.BlockSpec((1, SEQ, LANES), lambda b, h, i: (b, 0, 2 * SB_PAIRS + h))],
        out_specs=[blk], out_shape=[jax.ShapeDtypeStruct((B_LOC, SEQ, SB_WIDTH), BF16)],
        scratch_shapes=[], semantics=("parallel", "parallel", "arbitrary"), operands=(qkv, qkv, qkv))


def _sb_bwd(qkv, d_o, send=None):
    def body(q_ref, k_ref, v_ref, do_ref, dq_ref, dk_ref, dv_ref, dk_acc, dv_acc, z_scr, keep_scr):
        i = pl.program_id(2)
        krow = lax.broadcasted_iota(jnp.int32, (BLK, BLK), 0)
        kcol = lax.broadcasted_iota(jnp.int32, (BLK, BLK), 1)
        upto = (krow <= kcol).astype(BF16)
        earlier = (krow < kcol).astype(BF16)
        lane0 = lax.broadcasted_iota(jnp.int32, (SB_QBLK, LANES), 1) < HEAD_DIM
        q2 = _stack_heads(q_ref[0] * QK_SCALE, lane0)
        do2 = _stack_heads(do_ref[0], lane0)

        def keep_sum(c, causal):
            off = pl.multiple_of(c * SB_KCHUNK, SB_KCHUNK)
            z = _dot_nt(q2, k_ref[0, pl.ds(off, SB_KCHUNK), :])
            log_keep = _log_keep(z)
            if causal is not None:
                log_keep = jnp.where(causal, log_keep, 0.0)
            z_scr[c] = z
            keep_scr[c] = log_keep
            return jnp.sum(log_keep, axis=1, keepdims=True)

        def some_alive(run):
            return (jnp.max(run) > SB_DEAD).astype(jnp.int32)

        def scan(state):
            t, _, run = state
            run = run + keep_sum(i - 1 - t, None)
            return t + 1, some_alive(run), run

        diag_sum = keep_sum(i, _sb_diag_mask())
        walked, _, tot2 = lax.while_loop(lambda s: jnp.logical_and(s[0] < i, s[1] > 0), scan,
                                         (jnp.int32(0), some_alive(diag_sum), diag_sum))
        first = i - walked

        @pl.when(i == 0)
        def _():
            dk_acc[...] = jnp.zeros_like(dk_acc)
            dv_acc[...] = jnp.zeros_like(dv_acc)

        def chunk(c, carry, causal):
            dq, pre_keep, pre_e = carry
            off = pl.multiple_of(c * SB_KCHUNK, SB_KCHUNK)
            k_c = k_ref[0, pl.ds(off, SB_KCHUNK), :]
            v_c = v_ref[0, pl.ds(off, SB_KCHUNK), :]
            d_a = _dot_nt(do2, v_c)
            log_keep = keep_scr[c]
            log_beta = log_keep + z_scr[c]
            prefix, sums = _block_sums(log_keep, upto)
            inclusive = jnp.concatenate([prefix[0], prefix[1] + sums[0]], axis=1)
            a = jnp.exp(log_beta + ((tot2 - pre_keep) - inclusive))
            if causal is not None:
                a = jnp.where(causal, a, 0.0)
            e = d_a * a
            e_prefix, e_sums = _block_sums(e, earlier, split=False)
            before = jnp.concatenate([e_prefix[0] + pre_e, e_prefix[1] + (pre_e + e_sums[0])], axis=1)
            dz = e - (e + before) * jnp.exp(log_beta)
            if causal is not None:
                dz = jnp.where(causal, dz, 0.0)
            dz = dz.astype(BF16)
            dq = dq + _dot(dz, k_c)
            dk_acc[pl.ds(off, SB_KCHUNK), :] += _dot_tn(dz, q2)
            dv_acc[pl.ds(off, SB_KCHUNK), :] += _dot_tn(a.astype(BF16), do2)
            return dq, pre_keep + (sums[0] + sums[1]), pre_e + (e_sums[0] + e_sums[1])

        zero_col = jnp.zeros((SB_ROWS, 1), F32)
        carry = lax.fori_loop(first, i, lambda t, c: chunk(t, c, None),
                              (jnp.zeros((SB_ROWS, LANES), F32), zero_col, zero_col))
        dq, _, _ = chunk(i, carry, _sb_diag_mask())
        dq_ref[0] = (jnp.where(lane0, dq[:SB_QBLK], dq[SB_QBLK:]) * QK_SCALE).astype(BF16)

        @pl.when(i == N_SB_STEPS - 1)
        def _():
            dk_ref[0] = dk_acc[...].astype(BF16)
            dv_ref[0] = dv_acc[...].astype(BF16)

    blk = pl.BlockSpec((1, SB_QBLK, LANES), lambda b, h, i: (b, i, h))
    whole = lambda c: pl.BlockSpec((1, SEQ, LANES), lambda b, h, i: (b, 0, c * SB_PAIRS + h))
    out = jax.ShapeDtypeStruct((B_LOC, SEQ, SB_WIDTH), BF16)
    return _call(
        body, send, name="sb_bwd", grid=(B_LOC, SB_PAIRS, N_SB_STEPS),
        in_specs=[blk, whole(1), whole(2), blk],
        out_specs=[blk, whole(0), whole(0)],
        out_shape=[out, out, out],
        scratch_shapes=[pltpu.VMEM((SEQ, LANES), F32), pltpu.VMEM((SEQ, LANES), F32),
                        pltpu.VMEM((N_SB_STEPS, SB_ROWS, SB_KCHUNK), F32),
                        pltpu.VMEM((N_SB_STEPS, SB_ROWS, SB_KCHUNK), F32)],
        semantics=("parallel", "parallel", "arbitrary"), operands=(qkv, qkv, qkv, d_o))


DIL_GROUPS = len(DIL_PAIRS)
DIL_QBLOCKS = SEQ // BLK


def _residue_rows(j, dilation):
    length = SEQ // dilation
    return pl.ds(j, length, stride=dilation) if dilation > 1 else pl.ds(0, length)


def _gather_residues(src_ref, dst_ref, dst_off, dilation, scale=None):
    length = SEQ // dilation
    for j in range(dilation):
        v = src_ref[_residue_rows(j, dilation), :]
        if scale is not None:
            v = v * scale
        dst_ref[dst_off + j * length:dst_off + (j + 1) * length, :] = v.astype(dst_ref.dtype)


def _scatter_residues(src_ref, src_off, dst_ref, dilation):
    length = SEQ // dilation
    for j in range(dilation):
        dst_ref[_residue_rows(j, dilation), :] = (
            src_ref[src_off + j * length:src_off + (j + 1) * length, :].astype(dst_ref.dtype))


def _dil_geometry(group, pair):
    dilation = DIL_PAIRS[group][1]
    row = lax.broadcasted_iota(jnp.int32, (2 * BLK, 2 * BLK), 0)
    col = lax.broadcasted_iota(jnp.int32, (2 * BLK, 2 * BLK), 1)
    second = row >= BLK
    steps = BLK + jnp.where(second, row - BLK, row) - col
    coef = -ALIBI_MAX_BIAS / DIL_HEADS * math.log(2.0)
    first_head = float(4 * group + 1) + 2.0 * pair.astype(F32)
    slope = jnp.exp(coef * (first_head + jnp.where(second, 1.0, 0.0)))
    bias = slope * (steps * dilation).astype(F32)
    valid = jnp.logical_and(steps >= 0, steps <= BLK)
    return bias, valid, col >= BLK


def _dil_tile_scores(q2, kk, geometry, has_prev):
    bias, valid, own = geometry
    ok = jnp.logical_and(valid, jnp.logical_or(own, has_prev))
    return jnp.where(ok, _dot_nt(q2, kk) - bias, NEG_BIG)


def _head_col(v, lane_mask):
    return jnp.max(jnp.where(lane_mask, v, NEG_BIG), axis=1, keepdims=True)


def _dil_fwd(qkv, send=None):
    def body(*refs):
        ins, (o_ref, lse_ref), (qs, ks, vs, o_res, lse_res) = refs[:9], refs[9:11], refs[11:16]
        o_grp, lse_grp = refs[16:19], refs[19:22]
        pair = pl.program_id(1)
        lane0 = lax.broadcasted_iota(jnp.int32, (BLK, LANES), 1) < HEAD_DIM
        ks[0:BLK, :] = jnp.zeros((BLK, LANES), BF16)
        vs[0:BLK, :] = jnp.zeros((BLK, LANES), BF16)
        for grp, (_, dilation) in enumerate(DIL_PAIRS):
            q_ref, k_ref, v_ref = ins[3 * grp:3 * grp + 3]
            per_residue = DIL_QBLOCKS // dilation
            _gather_residues(q_ref, qs, 0, dilation, QK_SCALE)
            _gather_residues(k_ref, ks, BLK, dilation)
            _gather_residues(v_ref, vs, BLK, dilation)
            geometry = _dil_geometry(grp, pair)

            def step(blk, _):
                off = pl.multiple_of(blk * BLK, BLK)
                q2 = _stack_heads(qs[pl.ds(off, BLK), :], lane0)
                s = _dil_tile_scores(q2, ks[pl.ds(off, 2 * BLK), :], geometry, blk % per_residue != 0)
                m = jnp.max(s, axis=1, keepdims=True)
                p = jnp.exp(s - m)
                den = jnp.sum(p, axis=1, keepdims=True)
                out = _dot(p.astype(BF16), vs[pl.ds(off, 2 * BLK), :]) / den
                lse = m + jnp.log(den)
                o_res[pl.ds(off, BLK), :] = jnp.where(lane0, out[:BLK], out[BLK:])
                lse_res[pl.ds(off, BLK), :] = jnp.where(lane0, lse[:BLK], lse[BLK:])
                return 0

            lax.fori_loop(0, DIL_QBLOCKS, step, 0, unroll=8)
            _scatter_residues(o_res, 0, o_grp[grp], dilation)
            _scatter_residues(lse_res, 0, lse_grp[grp], dilation)

        for r0 in range(0, SEQ, 2 * BLK):
            rows = slice(r0, r0 + 2 * BLK)
            ls = [lse_grp[g][rows, :] for g in range(DIL_GROUPS)]
            m = jnp.maximum(jnp.maximum(ls[0], ls[1]), ls[2])
            w = [jnp.exp(l - m) for l in ls]
            den = w[0] + w[1] + w[2]
            o_ref[rows, :] = (w[0] * o_grp[0][rows, :] + w[1] * o_grp[1][rows, :] + w[2] * o_grp[2][rows, :]) / den
            lse_ref[rows, :] = m + jnp.log(den)

    def col(part, grp):
        return pl.BlockSpec((None, SEQ, LANES), lambda b, p: (b, 0, 6 * part + 2 * grp + p))

    out_spec = pl.BlockSpec((None, SEQ, LANES), lambda b, p: (b, 0, p))
    out = jax.ShapeDtypeStruct((B_LOC, SEQ, DIL_OUT), F32)
    return _call(
        body, send, name="dil_fwd", grid=(B_LOC, DIL_OUT // LANES),
        in_specs=[col(part, grp) for grp in range(DIL_GROUPS) for part in range(3)],
        out_specs=[out_spec, out_spec], out_shape=[out, out],
        scratch_shapes=[pltpu.VMEM((SEQ, LANES), BF16), pltpu.VMEM((SEQ + BLK, LANES), BF16),
                        pltpu.VMEM((SEQ + BLK, LANES), BF16), pltpu.VMEM((SEQ, LANES), F32),
                        pltpu.VMEM((SEQ, LANES), F32)] + [pltpu.VMEM((SEQ, LANES), F32)] * (2 * DIL_GROUPS),
        semantics=("parallel", "parallel"), operands=[qkv] * 9)


def _dil_bwd(qkv, d_o, lse, dsum, send=None):
    def body(*refs):
        ins, (do_ref, lse_ref, dsum_ref), outs = refs[:9], refs[9:12], refs[12:21]
        qs, ks, vs, dos, lse_res, dsum_res, dq_res, dk_acc, dv_acc = refs[21:]
        pair = pl.program_id(1)
        lane0 = lax.broadcasted_iota(jnp.int32, (BLK, LANES), 1) < HEAD_DIM
        lane1 = jnp.logical_not(lane0)
        ks[0:BLK, :] = jnp.zeros((BLK, LANES), BF16)
        vs[0:BLK, :] = jnp.zeros((BLK, LANES), BF16)
        for grp, (_, dilation) in enumerate(DIL_PAIRS):
            q_ref, k_ref, v_ref = ins[3 * grp:3 * grp + 3]
            dq_ref, dk_ref, dv_ref = outs[3 * grp:3 * grp + 3]
            per_residue = DIL_QBLOCKS // dilation
            _gather_residues(q_ref, qs, 0, dilation, QK_SCALE)
            _gather_residues(k_ref, ks, BLK, dilation)
            _gather_residues(v_ref, vs, BLK, dilation)
            _gather_residues(do_ref, dos, 0, dilation)
            _gather_residues(lse_ref, lse_res, 0, dilation)
            _gather_residues(dsum_ref, dsum_res, 0, dilation)
            dk_acc[...] = jnp.zeros_like(dk_acc)
            dv_acc[...] = jnp.zeros_like(dv_acc)
            geometry = _dil_geometry(grp, pair)

            def step(blk, _):
                off = pl.multiple_of(blk * BLK, BLK)
                q2 = _stack_heads(qs[pl.ds(off, BLK), :], lane0)
                do2 = _stack_heads(dos[pl.ds(off, BLK), :], lane0)
                kk = ks[pl.ds(off, 2 * BLK), :]
                vv = vs[pl.ds(off, 2 * BLK), :]
                lse_blk = lse_res[pl.ds(off, BLK), :]
                dsum_blk = dsum_res[pl.ds(off, BLK), :]
                lse2 = jnp.concatenate([_head_col(lse_blk, lane0), _head_col(lse_blk, lane1)], axis=0)
                dsum2 = jnp.concatenate([_head_col(dsum_blk, lane0), _head_col(dsum_blk, lane1)], axis=0)
                s = _dil_tile_scores(q2, kk, geometry, blk % per_residue != 0)
                p = jnp.exp(s - lse2)
                ds = (p * (_dot_nt(do2, vv) - dsum2)).astype(BF16)
                dq2 = _dot(ds, kk)
                dq_res[pl.ds(off, BLK), :] = jnp.where(lane0, dq2[:BLK], dq2[BLK:]) * QK_SCALE
                dk_acc[pl.ds(off, 2 * BLK), :] += _dot_tn(ds, q2)
                dv_acc[pl.ds(off, 2 * BLK), :] += _dot_tn(p.astype(BF16), do2)
                return 0

            lax.fori_loop(0, DIL_QBLOCKS, step, 0, unroll=8)
            _scatter_residues(dq_res, 0, dq_ref, dilation)
            _scatter_residues(dk_acc, BLK, dk_ref, dilation)
            _scatter_residues(dv_acc, BLK, dv_ref, dilation)

    def col(part, grp):
        return pl.BlockSpec((None, SEQ, LANES), lambda b, p: (b, 0, 6 * part + 2 * grp + p))

    slot = pl.BlockSpec((None, SEQ, LANES), lambda b, p: (b, 0, p))
    out = jax.ShapeDtypeStruct((B_LOC, SEQ, DIL_OUT), F32)
    return _call(
        body, send, name="dil_bwd", grid=(B_LOC, DIL_OUT // LANES),
        in_specs=[col(part, grp) for grp in range(DIL_GROUPS) for part in range(3)] + [slot] * 3,
        out_specs=[slot] * 9, out_shape=[out] * 9,
        scratch_shapes=[pltpu.VMEM((SEQ, LANES), BF16), pltpu.VMEM((SEQ + BLK, LANES), BF16),
                        pltpu.VMEM((SEQ + BLK, LANES), BF16), pltpu.VMEM((SEQ, LANES), BF16),
                        pltpu.VMEM((SEQ, LANES), F32), pltpu.VMEM((SEQ, LANES), F32),
                        pltpu.VMEM((SEQ, LANES), F32), pltpu.VMEM((SEQ + BLK, LANES), F32),
                        pltpu.VMEM((SEQ + BLK, LANES), F32)],
        semantics=("parallel", "parallel"), operands=[qkv] * 9 + [d_o, lse, dsum])


def _peers():
    x, y, c = lax.axis_index("x"), lax.axis_index("y"), lax.axis_index("c")
    me = 4 * x + 2 * y + c
    peers = []
    for mask in range(1, N_DEV):
        px = 1 - x if mask & 4 else x
        py = 1 - y if mask & 2 else y
        pc = 1 - c if mask & 1 else c
        peers.append(((px, py, pc), 4 * px + 2 * py + pc))
    return me, peers


def _all_gather(shard, name):
    def body(src_ref, out_ref, send_sems, recv_sems, local_sem):
        x, y, c = lax.axis_index("x"), lax.axis_index("y"), lax.axis_index("c")
        sibling = (x, y, 1 - c)
        chips = [(1 - x, y), (x, 1 - y), (1 - x, 1 - y)]

        def slot(px, py, pc):
            return out_ref.at[4 * px + 2 * py + pc]

        def copy(k, block, to, src=None):
            return pltpu.make_async_remote_copy(
                src_ref=slot(*block) if src is None else src, dst_ref=slot(*block),
                send_sem=send_sems.at[k], recv_sem=recv_sems.at[k], device_id=to,
                device_id_type=pl.DeviceIdType.MESH)

        mine = pltpu.make_async_copy(src_ref, slot(x, y, c), local_sem)
        mine.start()
        first = [copy(0, (x, y, c), sibling, src=src_ref)]
        first += [copy(1 + j, (x, y, c), (*chip, c), src=src_ref) for j, chip in enumerate(chips)]
        for cp in first:
            cp.start()
        passed = [copy(4 + j, (*chip, c), sibling) for j, chip in enumerate(chips)]
        for j, chip in enumerate(chips):
            copy(1 + j, (*chip, c), (x, y, c)).wait_recv()
            passed[j].start()
        copy(0, sibling, (x, y, c)).wait_recv()
        for j, chip in enumerate(chips):
            copy(4 + j, (*chip, 1 - c), (x, y, c)).wait_recv()
        for cp in first + passed:
            cp.wait_send()
        mine.wait()

    return pl.pallas_call(
        body, name=name,
        in_specs=[pl.BlockSpec(memory_space=pl.ANY)],
        out_specs=pl.BlockSpec(memory_space=pl.ANY),
        out_shape=jax.ShapeDtypeStruct((N_DEV,) + shard.shape, shard.dtype),
        scratch_shapes=[pltpu.SemaphoreType.DMA((N_DEV - 1,)), pltpu.SemaphoreType.DMA((N_DEV - 1,)),
                        pltpu.SemaphoreType.DMA],
    )(shard)


def _call(body, send, *, name, grid, in_specs, out_specs, out_shape, scratch_shapes, semantics, operands):
    if send is None:
        return pl.pallas_call(
            body, name=name, grid=grid, in_specs=in_specs, out_specs=out_specs, out_shape=out_shape,
            scratch_shapes=scratch_shapes, compiler_params=_params(semantics))(*operands), []
    srcs, kinds = [s for s, _ in send], [k for _, k in send]
    n, n_in, n_out, n_scr = len(srcs), len(in_specs), len(out_specs), len(scratch_shapes)
    steps = math.prod(grid)
    relay_step = (13 * steps) // 16

    def plan(refs):
        src_refs, land_refs = refs[n_in:n_in + n], refs[n_in + n + n_out:n_in + 2 * n + n_out]
        send_sems, recv_sems, local_sems = refs[-3:]
        x, y, c = lax.axis_index("x"), lax.axis_index("y"), lax.axis_index("c")
        me, peers = _peers()
        first, relayed_in, relayed_out, arrivals, sends, own = [], [], [], [], [], []
        for a, kind in enumerate(kinds):
            def copy(k, src, dst_slot, to):
                return pltpu.make_async_remote_copy(
                    src_ref=src, dst_ref=land_refs[a].at[dst_slot], send_sem=send_sems.at[a * (N_DEV - 1) + k],
                    recv_sem=recv_sems.at[a * (N_DEV - 1) + k], device_id=to, device_id_type=pl.DeviceIdType.MESH)

            if kind == "gather_by_chip":
                idx = lambda px, py, pc: 4 * px + 2 * py + pc
                chips = [(1 - x, y), (x, 1 - y), (1 - x, 1 - y)]
                mine = [copy(0, src_refs[a], me, (x, y, 1 - c))]
                arrivals.append(copy(0, src_refs[a], idx(x, y, 1 - c), (x, y, 1 - c)))
                for j, (px, py) in enumerate(chips):
                    mine.append(copy(1 + j, src_refs[a], me, (px, py, c)))
                    relayed_in.append(copy(1 + j, src_refs[a], idx(px, py, c), (px, py, c)))
                    relayed_out.append(copy(4 + j, land_refs[a].at[idx(px, py, c)], idx(px, py, c), (x, y, 1 - c)))
                    arrivals.append(copy(4 + j, src_refs[a], idx(px, py, 1 - c), (x, y, 1 - c)))
                first += mine
                sends += mine + relayed_out[-3:]
                own.append(pltpu.make_async_copy(src_refs[a], land_refs[a].at[me], local_sems.at[a]))
            elif kind == "scatter_by_chip":
                for k, (px, py) in enumerate([(1 - x, y), (x, 1 - y), (1 - x, 1 - y)]):
                    first.append(copy(k, src_refs[a].at[2 * px + py], 2 * x + y, (px, py, c)))
                    arrivals.append(copy(k, src_refs[a].at[2 * px + py], 2 * px + py, (px, py, c)))
                sends += first[-3:]
                own.append(pltpu.make_async_copy(src_refs[a].at[2 * x + y], land_refs[a].at[2 * x + y],
                                                 local_sems.at[a]))
            else:
                part = (lambda i: src_refs[a].at[i]) if kind == "scatter" else (lambda i: src_refs[a])
                for k, (peer, peer_idx) in enumerate(peers):
                    first.append(copy(k, part(peer_idx), me, peer))
                    arrivals.append(copy(k, part(peer_idx), peer_idx, peer))
                sends += first[-(N_DEV - 1):]
                own.append(pltpu.make_async_copy(part(me), land_refs[a].at[me], local_sems.at[a]))
        return first, relayed_in, relayed_out, arrivals, sends, own

    def wrapped(*refs):
        step = 0
        for axis, size in enumerate(grid):
            step = step * size + pl.program_id(axis)

        @pl.when(step == 0)
        def _():
            first, _, _, _, _, own = plan(refs)
            for cp in first + own:
                cp.start()

        if "gather_by_chip" in kinds:
            @pl.when(step == relay_step)
            def _():
                _, relayed_in, relayed_out, _, _, _ = plan(refs)
                for cp_in, cp_out in zip(relayed_in, relayed_out):
                    cp_in.wait_recv()
                    cp_out.start()

        body(*refs[:n_in], *refs[n_in + n:n_in + n + n_out], *refs[n_in + 2 * n + n_out:n_in + 2 * n + n_out + n_scr])

        @pl.when(step == steps - 1)
        def _():
            _, _, _, arrivals, sends, own = plan(refs)
            for cp in arrivals:
                cp.wait_recv()
            for cp in sends:
                cp.wait_send()
            for cp in own:
                cp.wait()

    anywhere = pl.BlockSpec(memory_space=pl.ANY)
    lands = [jax.ShapeDtypeStruct((N_DEV // 2 if k == "scatter_by_chip" else N_DEV,) + s.shape[-2:], s.dtype)
             for s, k in send]
    out = pl.pallas_call(
        wrapped, name=name, grid=grid,
        in_specs=list(in_specs) + [anywhere] * n, out_specs=list(out_specs) + [anywhere] * n,
        out_shape=list(out_shape) + lands,
        scratch_shapes=list(scratch_shapes) + [pltpu.SemaphoreType.DMA((n * (N_DEV - 1),)),
                                               pltpu.SemaphoreType.DMA((n * (N_DEV - 1),)),
                                               pltpu.SemaphoreType.DMA((n,))],
        compiler_params=_params(("arbitrary",) * len(grid)),
    )(*operands, *srcs)
    return out[:n_out], list(out[n_out:])


def _pair_swap(blocks):
    def body(src_ref, out_ref, send_sems, recv_sems):
        x, y, c = lax.axis_index("x"), lax.axis_index("y"), lax.axis_index("c")
        copies = [pltpu.make_async_remote_copy(
            src_ref=src_ref.at[2 * chip + (1 - c)], dst_ref=out_ref.at[chip], send_sem=send_sems.at[chip],
            recv_sem=recv_sems.at[chip], device_id=(x, y, 1 - c), device_id_type=pl.DeviceIdType.MESH)
            for chip in range(N_DEV // 2)]
        for cp in copies:
            cp.start()
        for cp in copies:
            cp.wait()

    return pl.pallas_call(
        body, name="pair_swap_grad_w_in",
        in_specs=[pl.BlockSpec(memory_space=pl.ANY)], out_specs=pl.BlockSpec(memory_space=pl.ANY),
        out_shape=jax.ShapeDtypeStruct((N_DEV // 2,) + blocks.shape[1:], blocks.dtype),
        scratch_shapes=[pltpu.SemaphoreType.DMA((N_DEV // 2,)), pltpu.SemaphoreType.DMA((N_DEV // 2,))],
    )(blocks)


def _pair_sum(blocks, swapped, core):
    _, rows, cols = swapped.shape
    tile_rows = _row_tile(rows)

    def body(core_ref, mine_ref, theirs_ref, o_ref):
        o_ref[...] = (mine_ref[...].astype(F32) + theirs_ref[...].astype(F32)).astype(o_ref.dtype)

    return pl.pallas_call(
        body, name="pair_sum_grad_w_in",
        grid_spec=pltpu.PrefetchScalarGridSpec(
            num_scalar_prefetch=1, grid=(N_DEV // 2, rows // tile_rows),
            in_specs=[pl.BlockSpec((None, tile_rows, cols), lambda j, i, core_ref: (2 * j + core_ref[0], i, 0)),
                      pl.BlockSpec((None, tile_rows, cols), lambda j, i, core_ref: (j, i, 0))],
            out_specs=pl.BlockSpec((None, tile_rows, cols), lambda j, i, core_ref: (j, i, 0))),
        out_shape=jax.ShapeDtypeStruct(swapped.shape, swapped.dtype),
        compiler_params=_params(("parallel", "parallel")),
    )(core, blocks, swapped)


def _sum_in_device_order(land_ref):
    acc = land_ref[0].astype(F32)
    for j in range(1, land_ref.shape[0]):
        acc = acc + land_ref[j].astype(F32)
    return acc


def _adam_math(w, g, m, v):
    c1 = 1.0 - ADAM_B1 ** ADAM_STEP
    c2 = 1.0 - ADAM_B2 ** ADAM_STEP
    m_new = ADAM_B1 * m + (1.0 - ADAM_B1) * g
    v_new = ADAM_B2 * v + (1.0 - ADAM_B2) * (g * g)
    delta = -ADAM_LR * ((m_new / c1) / (jnp.sqrt(v_new / c2) + ADAM_EPS) + ADAM_WD * w)
    return delta, m_new, v_new


def _row_tile(rows):
    return max(t for t in range(8, 257, 8) if rows % t == 0) if rows % 8 == 0 else rows


def _sum_update(land, w, m, v, name):
    slots, rows, cols = land.shape
    tile_rows = _row_tile(rows)

    def body(land_ref, w_ref, m_ref, v_ref, g_ref, d_ref, nm_ref, nv_ref):
        g = _sum_in_device_order(land_ref)
        g_ref[...] = g
        d_ref[...], nm_ref[...], nv_ref[...] = _adam_math(w_ref[...], g, m_ref[...], v_ref[...])

    tile = pl.BlockSpec((None, tile_rows, cols), lambda i: (0, i, 0))
    out = jax.ShapeDtypeStruct((1, rows, cols), F32)
    return pl.pallas_call(
        body, name=name, grid=(rows // tile_rows,),
        in_specs=[pl.BlockSpec((slots, tile_rows, cols), lambda i: (0, i, 0)), tile, tile, tile],
        out_specs=[tile] * 4, out_shape=[out] * 4,
        compiler_params=_params(("parallel",)),
    )(land, w, m, v)


def _sum_gains(land):
    def body(land_ref, o_ref):
        o_ref[...] = _sum_in_device_order(land_ref)

    return pl.pallas_call(
        body, name="sum_gain_grads", grid=(1,),
        in_specs=[pl.BlockSpec(land.shape, lambda i: (0, 0, 0))],
        out_specs=pl.BlockSpec(land.shape[1:], lambda i: (0, 0)),
        out_shape=jax.ShapeDtypeStruct(land.shape[1:], F32),
    )(land)


def _adamw(w, g, m, v, name):
    def body(w_ref, g_ref, m_ref, v_ref, d_ref, nm_ref, nv_ref):
        d_ref[...], nm_ref[...], nv_ref[...] = _adam_math(w_ref[...], g_ref[...], m_ref[...], v_ref[...])

    whole = pl.BlockSpec(w.shape, lambda i: (0, 0))
    out = jax.ShapeDtypeStruct(w.shape, F32)
    return pl.pallas_call(
        body, name=name, grid=(1,),
        in_specs=[whole] * 4, out_specs=[whole] * 3, out_shape=[out] * 3,
    )(w, g, m, v)


GROUP_FFN = ("w_ffn_in", "w_ffn_out")
GROUP_MIX = ("w_sb_up", "w_dil_up", "w_out")
COL_SHARDED = ("w_in", "w_sb_up", "w_dil_up", "w_ffn_in")


def _full_from_shards(name, slots):
    _, r, c = slots.shape
    if name in COL_SHARDED:
        return slots.transpose(1, 0, 2).reshape(r, N_DEV * c)
    return slots.reshape(N_DEV * r, c)


def _shards_from_full(name, full):
    rows, cols = full.shape
    if name in COL_SHARDED:
        return full.reshape(rows, N_DEV, cols // N_DEV).transpose(1, 0, 2)
    return full.reshape(N_DEV, rows // N_DEV, cols)


def _local_step(x, target, g_mix, g_ffn, g_fin, w_in, shards=None, rest=None):
    gather = lambda names, kind: None if shards is None else [(shards[n], kind) for n in names]
    scatter = lambda blocks: None if shards is None else [(t, "scatter") for t in blocks]
    landed = lambda blocks, lands: lands if lands else blocks

    w = {"w_in": w_in}
    if shards is None:
        w.update(rest)
        w["w_ffn_in"] = _shards_from_full("w_ffn_in", rest["w_ffn_in"])
    (qkv_sb, qkv_dl, gates, u), _ = _norm_proj(x, g_mix, w["w_in"])
    qkv_sb = qkv_sb.reshape(B_LOC, SEQ, 3 * SB_WIDTH)
    qkv_dl = qkv_dl.reshape(B_LOC, SEQ, 3 * DIL_WIDTH)
    (o_sb,), lands = _sb_fwd(qkv_sb, gather(GROUP_FFN, "gather_by_chip"))
    if lands:
        w["w_ffn_in"], w["w_ffn_out"] = lands[0], _full_from_shards("w_ffn_out", lands[1])
    o_sb = o_sb.reshape(TOK, SB_WIDTH)
    (o_dl, lse), lands = _dil_fwd(qkv_dl, gather(GROUP_MIX, "gather"))
    w.update({n: _full_from_shards(n, t) for n, t in zip(GROUP_MIX, lands)})
    o_dl = o_dl.reshape(TOK, DIL_OUT)

    loss, dx1, merged, u2, act, dh, dx2, dg_fin, dg_ffn = _mix_ffn_fwd_bwd(
        x, o_sb, o_dl, gates, w["w_sb_up"], w["w_dil_up"], w["w_out"], target, g_ffn, g_fin,
        w["w_ffn_in"], w["w_ffn_out"])
    dgates, dy_sb, dy_dl, do_sb, do_dl, dsum = _mix_bwd(dx1, o_sb, o_dl, gates, w["w_sb_up"], w["w_dil_up"], w["w_out"])
    blocks = {
        "w_sb_up": _atb(o_sb, dy_sb, "grad_w_sb_up", SB_WIDTH, D_MODEL, col_blocks=N_DEV),
        "w_dil_up": _atb(o_dl, dy_dl, "grad_w_dil_up", DIL_OUT, D_MODEL, col_blocks=N_DEV),
        "w_out": _shards_from_full("w_out", _atb(merged, dx1, "grad_w_out", D_MODEL, D_MODEL)),
        "w_ffn_in": _atb_shards(u2, dh, "grad_w_ffn_in"),
        "w_ffn_out": _shards_from_full("w_ffn_out", _atb_shards(act, dx2, "grad_w_ffn_out")),
    }
    grads = {}

    early, late = GROUP_FFN, GROUP_MIX
    early_blocks = [blocks[n] for n in early]
    (dq_sb, dk_sb, dv_sb), lands = _sb_bwd(qkv_sb, do_sb.reshape(B_LOC, SEQ, SB_WIDTH), scatter(early_blocks))
    grads.update(zip(early, landed(early_blocks, lands)))
    as_batch = lambda t: t.reshape(B_LOC, SEQ, DIL_OUT)
    late_blocks = [blocks[n] for n in late]
    d_dl, lands = _dil_bwd(qkv_dl, as_batch(do_dl), lse, as_batch(dsum), scatter(late_blocks))
    grads.update(zip(late, landed(late_blocks, lands)))
    flat = lambda t: t.reshape(TOK, -1)
    dproj = ([flat(dq_sb), flat(dk_sb), flat(dv_sb)]
             + [flat(d_dl[3 * grp + part]) for part in range(3) for grp in range(DIL_GROUPS)] + [dgates])

    w_in_blocks = _shards_from_full("w_in", _atb_pieces(u, dproj, "grad_w_in", D_MODEL // 2))
    if shards is None:
        grads["w_in"] = w_in_blocks
        send = None
    else:
        core = lax.axis_index("c").astype(jnp.int32).reshape(1)
        send = [(_pair_sum(w_in_blocks, _pair_swap(w_in_blocks), core), "scatter_by_chip")]
    (grad_x, dg_mix), lands = _proj_bwd(dproj, dx1, x, g_mix, w["w_in"], send)
    if lands:
        grads["w_in"] = lands[0]
    gain_grads = jnp.concatenate([dg_mix, dg_ffn, dg_fin], axis=0)
    return loss, grad_x, gain_grads, grads


def kernel(x, norm_mix_g, w_in, w_sb_up, w_dil_up, w_out, norm_ffn_g, w_ffn_in, w_ffn_out, norm_final_g, loss_target, m_norm_mix_g, m_w_in, m_w_sb_up, m_w_dil_up, m_w_out, m_norm_ffn_g, m_w_ffn_in, m_w_ffn_out, m_norm_final_g, v_norm_mix_g, v_w_in, v_w_sb_up, v_w_dil_up, v_w_out, v_norm_ffn_g, v_w_ffn_in, v_w_ffn_out, v_norm_final_g):
    mats = {"w_in": w_in, "w_sb_up": w_sb_up, "w_dil_up": w_dil_up, "w_out": w_out,
            "w_ffn_in": w_ffn_in, "w_ffn_out": w_ffn_out}
    moments_m = {"w_in": m_w_in, "w_sb_up": m_w_sb_up, "w_dil_up": m_w_dil_up, "w_out": m_w_out,
                 "w_ffn_in": m_w_ffn_in, "w_ffn_out": m_w_ffn_out}
    moments_v = {"w_in": v_w_in, "w_sb_up": v_w_sb_up, "w_dil_up": v_w_dil_up, "w_out": v_w_out,
                 "w_ffn_in": v_w_ffn_in, "w_ffn_out": v_w_ffn_out}
    gathered_w_in = _all_gather(w_in[0].astype(BF16), "all_gather_w_in")
    g_fin = norm_final_g.reshape(1, D_MODEL)
    loss, grad_x, gain_grads, grad_slots = _local_step(
        x.reshape(TOK, D_MODEL), loss_target.reshape(TOK, D_MODEL), norm_mix_g, norm_ffn_g, g_fin,
        _full_from_shards("w_in", gathered_w_in),
        shards={name: mats[name][0].astype(BF16) for name in GROUP_FFN + GROUP_MIX})

    gain_rows = jnp.concatenate([gain_grads, jnp.tile(loss, (1, D_MODEL // LANES)),
                                 jnp.zeros((8 - 4, D_MODEL), F32)], axis=0)
    g_gains = _sum_gains(_all_gather(gain_rows, "all_gather_gains"))

    out_g, out_d, out_m, out_v = {}, {}, {}, {}
    for name, slots in grad_slots.items():
        out_g[name], out_d[name], out_m[name], out_v[name] = _sum_update(
            slots, mats[name], moments_m[name], moments_v[name], "update_" + name)

    gain_w = jnp.concatenate([norm_mix_g, norm_ffn_g, g_fin], axis=0)
    gain_m = jnp.concatenate([m_norm_mix_g, m_norm_ffn_g, m_norm_final_g.reshape(1, D_MODEL)], axis=0)
    gain_v = jnp.concatenate([v_norm_mix_g, v_norm_ffn_g, v_norm_final_g.reshape(1, D_MODEL)], axis=0)
    gd, gm, gv = _adamw(gain_w, g_gains[:3], gain_m, gain_v, "adamw_gains")
    for idx, name in enumerate(("norm_mix_g", "norm_ffn_g", "norm_final_g")):
        shape = (D_MODEL,) if name == "norm_final_g" else (1, D_MODEL)
        out_g[name] = g_gains[idx].reshape(shape)
        out_d[name], out_m[name], out_v[name] = gd[idx].reshape(shape), gm[idx].reshape(shape), gv[idx].reshape(shape)

    order = ("norm_mix_g", "w_in", "w_sb_up", "w_dil_up", "w_out", "norm_ffn_g", "w_ffn_in", "w_ffn_out",
             "norm_final_g")
    return (g_gains[3, 0], grad_x.reshape(B_LOC, SEQ, D_MODEL),
            *[out_g[n] for n in order], *[out_d[n] for n in order],
            *[out_m[n] for n in order], *[out_v[n] for n in order])
```

```python
import math

import jax
import jax.numpy as jnp
from jax import lax
from jax.experimental import pallas as pl
from jax.experimental.pallas import tpu as pltpu

F32 = jnp.float32
BF16 = jnp.bfloat16

N_DEV = 8
D_MODEL = 1024
SEQ = 2048
B_LOC = 2
TOK = B_LOC * SEQ
HEAD_DIM = 64
SB_WIDTH = 512
DIL_WIDTH = 768
DIL_OUT = 256
QKV_WIDTH = 3 * SB_WIDTH + 3 * DIL_WIDTH
IN_WIDTH = QKV_WIDTH + 2 * D_MODEL
D_FF = 2816
DIL_PAIRS = ((128, 1), (512, 4), (2048, 16))
DIL_HEADS = 12
RMS_EPS = 1e-6
ALIBI_MAX_BIAS = 8.0
QK_SCALE = 1.0 / math.sqrt(HEAD_DIM)
BLK = 128
LANES = 128
NEG_BIG = -1e30

ADAM_LR = 0.001
ADAM_B1 = 0.9
ADAM_B2 = 0.999
ADAM_EPS = 1e-08
ADAM_WD = 0.01
ADAM_STEP = 10

VMEM_LIMIT = 56 * 1024 * 1024


def _dot(a, b):
    return jnp.dot(a, b, preferred_element_type=F32)


def _dot_nt(a, b):
    return lax.dot_general(a, b, (((1,), (1,)), ((), ())), preferred_element_type=F32)


def _dot_tn(a, b):
    return lax.dot_general(a, b, (((0,), (0,)), ((), ())), preferred_element_type=F32)


def _sigmoid(z):
    return 1.0 / (1.0 + jnp.exp(-z))


def _split_bf16(v):
    hi = v.astype(BF16)
    lo = (v - hi.astype(F32)).astype(BF16)
    return hi, lo


def _chunks(width, step=512):
    out, c = [], 0
    while c < width:
        w = min(step, width - c)
        out.append((c, w))
        c += w
    return out


def _resident(shape):
    nd = len(shape)
    return pl.BlockSpec(shape, lambda *_: (0,) * nd, pipeline_mode=pl.Buffered(1))


def _params(sem):
    return pltpu.CompilerParams(dimension_semantics=sem, vmem_limit_bytes=VMEM_LIMIT)


def _rms_fwd(x, g):
    r = lax.rsqrt(jnp.mean(x * x, axis=-1, keepdims=True) + RMS_EPS)
    n = x * r
    return n, r, n * g


def _rms_bwd(dy, n, r, g):
    dg = jnp.sum(dy * n, axis=0, keepdims=True)
    dn = dy * g
    dx = r * (dn - n * jnp.mean(dn * n, axis=-1, keepdims=True))
    return dx, dg


TM = 256


def _norm_proj(x, g, w_in, send=None):
    def body(x_ref, g_ref, w_ref, sb_ref, dl_ref, gate_ref, u_ref):
        _, _, u = _rms_fwd(x_ref[...], g_ref[...])
        u = u.astype(BF16)
        u_ref[...] = u
        for c0, w in _chunks(3 * SB_WIDTH):
            sb_ref[:, c0:c0 + w] = _dot(u, w_ref[:, c0:c0 + w]).astype(BF16)
        for c0, w in _chunks(3 * DIL_WIDTH):
            dl_ref[:, c0:c0 + w] = _dot(u, w_ref[:, 3 * SB_WIDTH + c0:3 * SB_WIDTH + c0 + w])
        for c0, w in _chunks(2 * D_MODEL):
            gate_ref[:, c0:c0 + w] = _dot(u, w_ref[:, QKV_WIDTH + c0:QKV_WIDTH + c0 + w])

    return _call(
        body, send, name="norm_proj", grid=(TOK // TM,),
        in_specs=[pl.BlockSpec((TM, D_MODEL), lambda i: (i, 0)), _resident((1, D_MODEL)),
                  _resident((D_MODEL, IN_WIDTH))],
        out_specs=[pl.BlockSpec((TM, 3 * SB_WIDTH), lambda i: (i, 0)),
                   pl.BlockSpec((TM, 3 * DIL_WIDTH), lambda i: (i, 0)),
                   pl.BlockSpec((TM, 2 * D_MODEL), lambda i: (i, 0)),
                   pl.BlockSpec((TM, D_MODEL), lambda i: (i, 0))],
        out_shape=[jax.ShapeDtypeStruct((TOK, 3 * SB_WIDTH), BF16),
                   jax.ShapeDtypeStruct((TOK, 3 * DIL_WIDTH), F32),
                   jax.ShapeDtypeStruct((TOK, 2 * D_MODEL), F32),
                   jax.ShapeDtypeStruct((TOK, D_MODEL), BF16)],
        scratch_shapes=[], semantics=("parallel",), operands=(x, g, w_in))


FF_SHARD = 2 * D_FF // N_DEV
FF_PAIRS = N_DEV // 2


def _mix_ffn_fwd_bwd(x, o_sb, o_dl, gates, w_sb_up, w_dil_up, w_out, target, g_ffn, g_fin, w_ffn_in, w_ffn_out):
    def body(x_ref, osb_ref, odl_ref, gate_ref, wsb_ref, wdl_ref, wo_ref, t_ref, gffn_ref, gfin_ref, win_ref, wout_ref,
             loss_ref, dx1_ref, mg_ref, u2_ref, act_ref, dh_ref, dx2_ref, dgfin_ref, dgffn_ref, h_scr):
        i = pl.program_id(0)

        @pl.when(i == 0)
        def _():
            loss_ref[...] = jnp.zeros_like(loss_ref)
            dgfin_ref[...] = jnp.zeros_like(dgfin_ref)
            dgffn_ref[...] = jnp.zeros_like(dgffn_ref)

        y_sb = _dot(osb_ref[...], wsb_ref[...])
        y_dl = _dot(odl_ref[...].astype(BF16), wdl_ref[...])
        merged = (_sigmoid(gate_ref[:, :D_MODEL]) * y_sb
                  + _sigmoid(gate_ref[:, D_MODEL:]) * y_dl).astype(BF16)
        mg_ref[...] = merged
        x1 = x_ref[...] + _dot(merged, wo_ref[...])
        g_ffn_v = gffn_ref[...]
        g_fin_v = gfin_ref[...]
        n2, r2, u2 = _rms_fwd(x1, g_ffn_v)
        u2 = u2.astype(BF16)
        u2_ref[...] = u2
        x2 = x1
        for r in range(FF_PAIRS):
            gate = _dot(u2, win_ref[r])
            up = _dot(u2, win_ref[r + FF_PAIRS])
            h_scr[r] = gate
            h_scr[r + FF_PAIRS] = up
            act = (gate * _sigmoid(gate) * up).astype(BF16)
            act_ref[r] = act
            x2 = x2 + _dot(act, wout_ref[r * FF_SHARD:(r + 1) * FF_SHARD, :])
        n3, r3, y = _rms_fwd(x2, g_fin_v)
        err = y - t_ref[...]
        sq = jnp.sum(jnp.sum(err * err, axis=1, keepdims=True), axis=0, keepdims=True)
        loss_ref[...] += sq * (0.5 / D_MODEL)
        dx2, dgfin = _rms_bwd(err * (1.0 / D_MODEL), n3, r3, g_fin_v)
        dgfin_ref[...] += dgfin
        dx2_b = dx2.astype(BF16)
        dx2_ref[...] = dx2_b
        du2 = jnp.zeros((TM, D_MODEL), F32)
        for r in range(FF_PAIRS):
            gate = h_scr[r]
            up = h_scr[r + FF_PAIRS]
            dact = _dot_nt(dx2_b, wout_ref[r * FF_SHARD:(r + 1) * FF_SHARD, :])
            sg = _sigmoid(gate)
            dgate = (dact * up * (sg * (1.0 + gate * (1.0 - sg)))).astype(BF16)
            dup = (dact * (gate * sg)).astype(BF16)
            dh_ref[r] = dgate
            dh_ref[r + FF_PAIRS] = dup
            du2 = du2 + _dot_nt(dgate, win_ref[r])
            du2 = du2 + _dot_nt(dup, win_ref[r + FF_PAIRS])
        dx1_n, dgffn = _rms_bwd(du2, n2, r2, g_ffn_v)
        dgffn_ref[...] += dgffn
        dx1_ref[...] = dx2 + dx1_n

    tile = lambda w: pl.BlockSpec((TM, w), lambda i: (i, 0))
    shards = lambda n: pl.BlockSpec((n, TM, FF_SHARD), lambda i: (0, i, 0))
    acc = lambda w: pl.BlockSpec((1, w), lambda i: (0, 0))
    return pl.pallas_call(
        body, name="mix_ffn_fwd_bwd", grid=(TOK // TM,),
        in_specs=[tile(D_MODEL), tile(SB_WIDTH), tile(DIL_OUT), tile(2 * D_MODEL),
                  _resident((SB_WIDTH, D_MODEL)), _resident((DIL_OUT, D_MODEL)), _resident((D_MODEL, D_MODEL)),
                  tile(D_MODEL), _resident((1, D_MODEL)), _resident((1, D_MODEL)),
                  _resident((N_DEV, D_MODEL, FF_SHARD)), _resident((D_FF, D_MODEL))],
        out_specs=[acc(LANES), tile(D_MODEL), tile(D_MODEL), tile(D_MODEL), shards(FF_PAIRS), shards(N_DEV),
                   tile(D_MODEL), acc(D_MODEL), acc(D_MODEL)],
        out_shape=[jax.ShapeDtypeStruct((1, LANES), F32),
                   jax.ShapeDtypeStruct((TOK, D_MODEL), F32),
                   jax.ShapeDtypeStruct((TOK, D_MODEL), BF16),
                   jax.ShapeDtypeStruct((TOK, D_MODEL), BF16),
                   jax.ShapeDtypeStruct((FF_PAIRS, TOK, FF_SHARD), BF16),
                   jax.ShapeDtypeStruct((N_DEV, TOK, FF_SHARD), BF16),
                   jax.ShapeDtypeStruct((TOK, D_MODEL), BF16),
                   jax.ShapeDtypeStruct((1, D_MODEL), F32),
                   jax.ShapeDtypeStruct((1, D_MODEL), F32)],
        scratch_shapes=[pltpu.VMEM((N_DEV, TM, FF_SHARD), F32)],
        compiler_params=_params(("arbitrary",)),
    )(x, o_sb, o_dl, gates, w_sb_up, w_dil_up, w_out, target, g_ffn, g_fin, w_ffn_in, w_ffn_out)


def _mix_bwd(dx1, o_sb, o_dl, gates, w_sb_up, w_dil_up, w_out):
    def body(dx1_ref, osb_ref, odl_ref, gate_ref, wsb_ref, wdl_ref, wout_ref,
             dgate_ref, dysb_ref, dydl_ref, dosb_ref, dodl_ref, dsum_ref):
        dmerged = _dot_nt(dx1_ref[...].astype(BF16), wout_ref[...])
        o_dl = odl_ref[...]
        y_sb = _dot(osb_ref[...], wsb_ref[...])
        y_dl = _dot(o_dl.astype(BF16), wdl_ref[...])
        s_sb = _sigmoid(gate_ref[:, :D_MODEL])
        s_dl = _sigmoid(gate_ref[:, D_MODEL:])
        dgate_ref[:, :D_MODEL] = (dmerged * y_sb * (s_sb * (1.0 - s_sb))).astype(BF16)
        dgate_ref[:, D_MODEL:] = (dmerged * y_dl * (s_dl * (1.0 - s_dl))).astype(BF16)
        dy_sb = (dmerged * s_sb).astype(BF16)
        dy_dl = (dmerged * s_dl).astype(BF16)
        dysb_ref[...] = dy_sb
        dydl_ref[...] = dy_dl
        dosb_ref[...] = _dot_nt(dy_sb, wsb_ref[...]).astype(BF16)
        do_dl = _dot_nt(dy_dl, wdl_ref[...])
        dodl_ref[...] = do_dl
        row = lax.broadcasted_iota(jnp.int32, (DIL_OUT, DIL_OUT), 0) // HEAD_DIM
        col = lax.broadcasted_iota(jnp.int32, (DIL_OUT, DIL_OUT), 1) // HEAD_DIM
        same_head = (row == col).astype(BF16)
        hi, lo = _split_bf16(do_dl * o_dl)
        dsum_ref[...] = _dot(hi, same_head) + _dot(lo, same_head)

    tile = lambda w: pl.BlockSpec((TM, w), lambda i: (i, 0))
    return pl.pallas_call(
        body, name="mix_bwd", grid=(TOK // TM,),
        in_specs=[tile(D_MODEL), tile(SB_WIDTH), tile(DIL_OUT), tile(2 * D_MODEL),
                  _resident((SB_WIDTH, D_MODEL)), _resident((DIL_OUT, D_MODEL)),
                  _resident((D_MODEL, D_MODEL))],
        out_specs=[tile(2 * D_MODEL), tile(D_MODEL), tile(D_MODEL), tile(SB_WIDTH), tile(DIL_OUT),
                   tile(DIL_OUT)],
        out_shape=[jax.ShapeDtypeStruct((TOK, 2 * D_MODEL), BF16),
                   jax.ShapeDtypeStruct((TOK, D_MODEL), BF16),
                   jax.ShapeDtypeStruct((TOK, D_MODEL), BF16),
                   jax.ShapeDtypeStruct((TOK, SB_WIDTH), BF16),
                   jax.ShapeDtypeStruct((TOK, DIL_OUT), F32),
                   jax.ShapeDtypeStruct((TOK, DIL_OUT), F32)],
        compiler_params=_params(("parallel",)),
    )(dx1, o_sb, o_dl, gates, w_sb_up, w_dil_up, w_out)


def _proj_bwd(dproj, dx1, x, g, w_in, send=None):
    widths = [p.shape[1] for p in dproj]

    def body(*refs):
        dx1_ref, x_ref, g_ref, w_ref, dx_ref, dg_ref = refs[len(widths):]

        @pl.when(pl.program_id(0) == 0)
        def _():
            dg_ref[...] = jnp.zeros_like(dg_ref)

        du = jnp.zeros((TM, D_MODEL), F32)
        c0 = 0
        for dp_ref, w in zip(refs, widths):
            du = du + _dot_nt(dp_ref[...].astype(BF16), w_ref[:, c0:c0 + w])
            c0 += w
        g_v = g_ref[...]
        n, r, _ = _rms_fwd(x_ref[...], g_v)
        dx, dg = _rms_bwd(du, n, r, g_v)
        dg_ref[...] += dg
        dx_ref[...] = dx1_ref[...] + dx

    tile = lambda w: pl.BlockSpec((TM, w), lambda i: (i, 0))
    return _call(
        body, send, name="proj_bwd", grid=(TOK // TM,),
        in_specs=[tile(w) for w in widths] + [tile(D_MODEL), tile(D_MODEL), _resident((1, D_MODEL)),
                                              _resident((D_MODEL, IN_WIDTH))],
        out_specs=[tile(D_MODEL), pl.BlockSpec((1, D_MODEL), lambda i: (0, 0))],
        out_shape=[jax.ShapeDtypeStruct((TOK, D_MODEL), F32),
                   jax.ShapeDtypeStruct((1, D_MODEL), F32)],
        scratch_shapes=[], semantics=("arbitrary",), operands=(*dproj, dx1, x, g, w_in))


def _atb_pieces(a, pieces, name, tm, tk=512):
    m = a.shape[1]
    widths = [p.shape[1] for p in pieces]
    n = sum(widths)
    nk = TOK // tk

    def body(a_ref, *refs):
        o_ref, acc_ref = refs[len(widths):]
        k = pl.program_id(1)

        @pl.when(k == 0)
        def _():
            acc_ref[...] = jnp.zeros_like(acc_ref)

        a_v = a_ref[...]
        c0 = 0
        for p_ref, w in zip(refs, widths):
            acc_ref[:, c0:c0 + w] += _dot_tn(a_v, p_ref[...].astype(BF16))
            c0 += w

        @pl.when(k == nk - 1)
        def _():
            o_ref[...] = acc_ref[...].astype(BF16)

    return pl.pallas_call(
        body, name=name, grid=(m // tm, nk),
        in_specs=[pl.BlockSpec((tk, tm), lambda i, k: (k, i))]
                 + [pl.BlockSpec((tk, w), lambda i, k: (k, 0)) for w in widths],
        out_specs=pl.BlockSpec((tm, n), lambda i, k: (i, 0)),
        out_shape=jax.ShapeDtypeStruct((m, n), BF16),
        scratch_shapes=[pltpu.VMEM((tm, n), F32)],
        compiler_params=_params(("parallel", "arbitrary")),
    )(a, *pieces)


def _atb_shards(a, b, name, tk=512):
    a_sharded = a.ndim == 3
    n, _, w = a.shape if a_sharded else b.shape
    other = (b if a_sharded else a).shape[1]

    def body(a_ref, b_ref, o_ref):
        acc = jnp.zeros(o_ref.shape, F32)
        for k0 in range(0, TOK, tk):
            acc = acc + _dot_tn(a_ref[k0:k0 + tk, :], b_ref[k0:k0 + tk, :])
        o_ref[...] = acc.astype(BF16)

    shard = pl.BlockSpec((None, TOK, w), lambda r: (r, 0, 0))
    whole = _resident((TOK, other))
    if a_sharded:
        out_spec, out_shape = pl.BlockSpec((w, other), lambda r: (r, 0)), (n * w, other)
    else:
        out_spec, out_shape = pl.BlockSpec((None, other, w), lambda r: (r, 0, 0)), (n, other, w)
    return pl.pallas_call(
        body, name=name, grid=(n,),
        in_specs=[shard, whole] if a_sharded else [whole, shard],
        out_specs=out_spec, out_shape=jax.ShapeDtypeStruct(out_shape, BF16),
        compiler_params=_params(("parallel",)),
    )(a, b)


def _atb(a, b, name, tm, tn, col_blocks=0, tk=512):
    m, n = a.shape[1], b.shape[1]
    nk = TOK // tk

    def body(a_ref, b_ref, o_ref, acc_ref):
        k = pl.program_id(2)

        @pl.when(k == 0)
        def _():
            acc_ref[...] = jnp.zeros_like(acc_ref)

        acc_ref[...] += _dot_tn(a_ref[...].astype(BF16), b_ref[...].astype(BF16))

        @pl.when(k == nk - 1)
        def _():
            if col_blocks:
                width = n // col_blocks
                for blk in range(col_blocks):
                    o_ref[blk] = acc_ref[:, blk * width:(blk + 1) * width].astype(BF16)
            else:
                o_ref[...] = acc_ref[...].astype(BF16)

    if col_blocks:
        out_spec = pl.BlockSpec((col_blocks, tm, n // col_blocks), lambda i, j, k: (0, i, 0))
        out_shape = jax.ShapeDtypeStruct((col_blocks, m, n // col_blocks), BF16)
    else:
        out_spec = pl.BlockSpec((tm, tn), lambda i, j, k: (i, j))
        out_shape = jax.ShapeDtypeStruct((m, n), BF16)
    return pl.pallas_call(
        body, name=name, grid=(m // tm, n // tn, nk),
        in_specs=[pl.BlockSpec((tk, tm), lambda i, j, k: (k, i)),
                  pl.BlockSpec((tk, tn), lambda i, j, k: (k, j))],
        out_specs=out_spec, out_shape=out_shape,
        scratch_shapes=[pltpu.VMEM((tm, tn), F32)],
        compiler_params=_params(("parallel", "parallel", "arbitrary")),
    )(a, b)


SB_PAIRS = SB_WIDTH // LANES


def _two_heads(v, lane0):
    zero = jnp.zeros_like(v)
    return jnp.where(lane0, v, zero), jnp.where(lane0, zero, v)


SB_QBLK = 256
N_SB_STEPS = SEQ // SB_QBLK


SB_KCHUNK = 2 * BLK
SB_ROWS = 2 * SB_QBLK
SB_DEAD = -104.0


def _log_keep(z):
    neg_z = -z
    return jnp.minimum(neg_z, 0.0) - jnp.log(1.0 + jnp.exp(jnp.minimum(z, neg_z)))


def _stack_heads(v, lane0):
    return jnp.concatenate(_two_heads(v, lane0), axis=0)


def _block_sums(v, tri, split=True):
    rows = v.shape[0]
    halves = (v[:, :BLK], v[:, BLK:])
    stacked = jnp.concatenate(halves, axis=0)
    if split:
        hi, lo = _split_bf16(stacked)
        prod = _dot(jnp.concatenate([hi, lo], axis=0), tri)
        tri_sum = prod[:2 * rows] + prod[2 * rows:]
    else:
        tri_sum = _dot(stacked.astype(BF16), tri)
    sums = tuple(jnp.sum(h, axis=1, keepdims=True) for h in halves)
    return (tri_sum[:rows], tri_sum[rows:]), sums


SB_EARLY = 160


def _early_rows(v):
    return jnp.concatenate([v[:SB_EARLY], v[SB_QBLK:SB_QBLK + SB_EARLY]], axis=0)


def _with_early_rows(early, v):
    return jnp.concatenate([early[:SB_EARLY], v[SB_EARLY:SB_QBLK], early[SB_EARLY:], v[SB_QBLK + SB_EARLY:]], axis=0)


def _sb_diag_mask():
    row = lax.broadcasted_iota(jnp.int32, (SB_ROWS, SB_KCHUNK), 0)
    col = lax.broadcasted_iota(jnp.int32, (SB_ROWS, SB_KCHUNK), 1)
    return col < jnp.where(row >= SB_QBLK, row - SB_QBLK, row)


def _sb_fwd(qkv, send=None):
    def body(q_ref, k_ref, v_ref, o_ref):
        i = pl.program_id(2)
        krow = lax.broadcasted_iota(jnp.int32, (BLK, BLK), 0)
        kcol = lax.broadcasted_iota(jnp.int32, (BLK, BLK), 1)
        later = (krow > kcol).astype(BF16)
        lane0 = lax.broadcasted_iota(jnp.int32, (SB_QBLK, LANES), 1) < HEAD_DIM
        q2 = _stack_heads(q_ref[0] * QK_SCALE, lane0)

        def chunk(c, q_rows, carry, causal):
            acc, run = carry
            off = pl.multiple_of(c * SB_KCHUNK, SB_KCHUNK)
            z = _dot_nt(q_rows, k_ref[0, pl.ds(off, SB_KCHUNK), :])
            log_keep = _log_keep(z)
            if causal is not None:
                log_keep = jnp.where(causal, log_keep, 0.0)
            suffix, sums = _block_sums(log_keep, later)
            log_after = jnp.concatenate([suffix[0] + (run + sums[1]), suffix[1] + run], axis=1)
            a = jnp.exp(log_keep + z + log_after)
            if causal is not None:
                a = jnp.where(causal, a, 0.0)
            acc = acc + _dot(a.astype(BF16), v_ref[0, pl.ds(off, SB_KCHUNK), :])
            return acc, run + (sums[0] + sums[1])

        acc, run = chunk(i, q2, (jnp.zeros((SB_ROWS, LANES), F32), jnp.zeros((SB_ROWS, 1), F32)), _sb_diag_mask())

        def some_alive(run):
            return (jnp.max(run) > SB_DEAD).astype(jnp.int32)

        q2_early = _early_rows(q2)

        def trip(state):
            t, _, acc, run = state
            early_run = _early_rows(run)
            late_dead = jnp.max(_with_early_rows(jnp.full_like(early_run, NEG_BIG), run)) <= SB_DEAD

            def early_only(_):
                acc_e, run_e = chunk(i - 1 - t, q2_early, (_early_rows(acc), early_run), None)
                return _with_early_rows(acc_e, acc), _with_early_rows(run_e, run)

            acc, run = lax.cond(late_dead, early_only, lambda _: chunk(i - 1 - t, q2, (acc, run), None), None)
            return t + 1, some_alive(run), acc, run

        _, _, acc, _ = lax.while_loop(lambda s: jnp.logical_and(s[0] < i, s[1] > 0), trip,
                                      (jnp.int32(0), some_alive(run), acc, run))
        o_ref[0] = jnp.where(lane0, acc[:SB_QBLK], acc[SB_QBLK:]).astype(BF16)

    blk = pl.BlockSpec((1, SB_QBLK, LANES), lambda b, h, i: (b, i, h))
    return _call(
        body, send, name="sb_fwd", grid=(B_LOC, SB_PAIRS, N_SB_STEPS),
        in_specs=[blk,
                  pl.BlockSpec((1, SEQ, LANES), lambda b, h, i: (b, 0, SB_PAIRS + h)),
                  pl.BlockSpec((1, SEQ, LANES), lambda b, h, i: (b, 0, 2 * SB_PAIRS + h))],
        out_specs=[blk], out_shape=[jax.ShapeDtypeStruct((B_LOC, SEQ, SB_WIDTH), BF16)],
        scratch_shapes=[], semantics=("parallel", "parallel", "arbitrary"), operands=(qkv, qkv, qkv))


def _sb_bwd(qkv, d_o, send=None):
    def body(q_ref, k_ref, v_ref, do_ref, dq_ref, dk_ref, dv_ref, dk_acc, dv_acc, z_scr, keep_scr):
        i = pl.program_id(2)
        krow = lax.broadcasted_iota(jnp.int32, (BLK, BLK), 0)
        kcol = lax.broadcasted_iota(jnp.int32, (BLK, BLK), 1)
        upto = (krow <= kcol).astype(BF16)
        earlier = (krow < kcol).astype(BF16)
        lane0 = lax.broadcasted_iota(jnp.int32, (SB_QBLK, LANES), 1) < HEAD_DIM
        q2 = _stack_heads(q_ref[0] * QK_SCALE, lane0)
        do2 = _stack_heads(do_ref[0], lane0)

        def keep_sum(c, causal):
            off = pl.multiple_of(c * SB_KCHUNK, SB_KCHUNK)
            z = _dot_nt(q2, k_ref[0, pl.ds(off, SB_KCHUNK), :])
            log_keep = _log_keep(z)
            if causal is not None:
                log_keep = jnp.where(causal, log_keep, 0.0)
            z_scr[c] = z
            keep_scr[c] = log_keep
            return jnp.sum(log_keep, axis=1, keepdims=True)

        def some_alive(run):
            return (jnp.max(run) > SB_DEAD).astype(jnp.int32)

        def scan(state):
            t, _, run = state
            run = run + keep_sum(i - 1 - t, None)
            return t + 1, some_alive(run), run

        diag_sum = keep_sum(i, _sb_diag_mask())
        walked, _, tot2 = lax.while_loop(lambda s: jnp.logical_and(s[0] < i, s[1] > 0), scan,
                                         (jnp.int32(0), some_alive(diag_sum), diag_sum))
        first = i - walked

        @pl.when(i == 0)
        def _():
            dk_acc[...] = jnp.zeros_like(dk_acc)
            dv_acc[...] = jnp.zeros_like(dv_acc)

        def chunk(c, carry, causal):
            dq, pre_keep, pre_e = carry
            off = pl.multiple_of(c * SB_KCHUNK, SB_KCHUNK)
            k_c = k_ref[0, pl.ds(off, SB_KCHUNK), :]
            v_c = v_ref[0, pl.ds(off, SB_KCHUNK), :]
            d_a = _dot_nt(do2, v_c)
            log_keep = keep_scr[c]
            log_beta = log_keep + z_scr[c]
            prefix, sums = _block_sums(log_keep, upto)
            inclusive = jnp.concatenate([prefix[0], prefix[1] + sums[0]], axis=1)
            a = jnp.exp(log_beta + ((tot2 - pre_keep) - inclusive))
            if causal is not None:
                a = jnp.where(causal, a, 0.0)
            e = d_a * a
            e_prefix, e_sums = _block_sums(e, earlier, split=False)
            before = jnp.concatenate([e_prefix[0] + pre_e, e_prefix[1] + (pre_e + e_sums[0])], axis=1)
            dz = e - (e + before) * jnp.exp(log_beta)
            if causal is not None:
                dz = jnp.where(causal, dz, 0.0)
            dz = dz.astype(BF16)
            dq = dq + _dot(dz, k_c)
            dk_acc[pl.ds(off, SB_KCHUNK), :] += _dot_tn(dz, q2)
            dv_acc[pl.ds(off, SB_KCHUNK), :] += _dot_tn(a.astype(BF16), do2)
            return dq, pre_keep + (sums[0] + sums[1]), pre_e + (e_sums[0] + e_sums[1])

        zero_col = jnp.zeros((SB_ROWS, 1), F32)
        carry = lax.fori_loop(first, i, lambda t, c: chunk(t, c, None),
                              (jnp.zeros((SB_ROWS, LANES), F32), zero_col, zero_col))
        dq, _, _ = chunk(i, carry, _sb_diag_mask())
        dq_ref[0] = (jnp.where(lane0, dq[:SB_QBLK], dq[SB_QBLK:]) * QK_SCALE).astype(BF16)

        @pl.when(i == N_SB_STEPS - 1)
        def _():
            dk_ref[0] = dk_acc[...].astype(BF16)
            dv_ref[0] = dv_acc[...].astype(BF16)

    blk = pl.BlockSpec((1, SB_QBLK, LANES), lambda b, h, i: (b, i, h))
    whole = lambda c: pl.BlockSpec((1, SEQ, LANES), lambda b, h, i: (b, 0, c * SB_PAIRS + h))
    out = jax.ShapeDtypeStruct((B_LOC, SEQ, SB_WIDTH), BF16)
    return _call(
        body, send, name="sb_bwd", grid=(B_LOC, SB_PAIRS, N_SB_STEPS),
        in_specs=[blk, whole(1), whole(2), blk],
        out_specs=[blk, whole(0), whole(0)],
        out_shape=[out, out, out],
        scratch_shapes=[pltpu.VMEM((SEQ, LANES), F32), pltpu.VMEM((SEQ, LANES), F32),
                        pltpu.VMEM((N_SB_STEPS, SB_ROWS, SB_KCHUNK), F32),
                        pltpu.VMEM((N_SB_STEPS, SB_ROWS, SB_KCHUNK), F32)],
        semantics=("parallel", "parallel", "arbitrary"), operands=(qkv, qkv, qkv, d_o))


DIL_GROUPS = len(DIL_PAIRS)
DIL_QBLOCKS = SEQ // BLK


def _residue_rows(j, dilation):
    length = SEQ // dilation
    return pl.ds(j, length, stride=dilation) if dilation > 1 else pl.ds(0, length)


def _gather_residues(src_ref, dst_ref, dst_off, dilation, scale=None):
    length = SEQ // dilation
    for j in range(dilation):
        v = src_ref[_residue_rows(j, dilation), :]
        if scale is not None:
            v = v * scale
        dst_ref[dst_off + j * length:dst_off + (j + 1) * length, :] = v.astype(dst_ref.dtype)


def _scatter_residues(src_ref, src_off, dst_ref, dilation):
    length = SEQ // dilation
    for j in range(dilation):
        dst_ref[_residue_rows(j, dilation), :] = (
            src_ref[src_off + j * length:src_off + (j + 1) * length, :].astype(dst_ref.dtype))


def _dil_geometry(group, pair):
    dilation = DIL_PAIRS[group][1]
    row = lax.broadcasted_iota(jnp.int32, (2 * BLK, 2 * BLK), 0)
    col = lax.broadcasted_iota(jnp.int32, (2 * BLK, 2 * BLK), 1)
    second = row >= BLK
    steps = BLK + jnp.where(second, row - BLK, row) - col
    coef = -ALIBI_MAX_BIAS / DIL_HEADS * math.log(2.0)
    first_head = float(4 * group + 1) + 2.0 * pair.astype(F32)
    slope = jnp.exp(coef * (first_head + jnp.where(second, 1.0, 0.0)))
    bias = slope * (steps * dilation).astype(F32)
    valid = jnp.logical_and(steps >= 0, steps <= BLK)
    return bias, valid, col >= BLK


def _dil_tile_scores(q2, kk, geometry, has_prev):
    bias, valid, own = geometry
    ok = jnp.logical_and(valid, jnp.logical_or(own, has_prev))
    return jnp.where(ok, _dot_nt(q2, kk) - bias, NEG_BIG)


def _head_col(v, lane_mask):
    return jnp.max(jnp.where(lane_mask, v, NEG_BIG), axis=1, keepdims=True)


def _dil_fwd(qkv, send=None):
    def body(*refs):
        ins, (o_ref, lse_ref), (qs, ks, vs, o_res, lse_res) = refs[:9], refs[9:11], refs[11:16]
        o_grp, lse_grp = refs[16:19], refs[19:22]
        pair = pl.program_id(1)
        lane0 = lax.broadcasted_iota(jnp.int32, (BLK, LANES), 1) < HEAD_DIM
        ks[0:BLK, :] = jnp.zeros((BLK, LANES), BF16)
        vs[0:BLK, :] = jnp.zeros((BLK, LANES), BF16)
        for grp, (_, dilation) in enumerate(DIL_PAIRS):
            q_ref, k_ref, v_ref = ins[3 * grp:3 * grp + 3]
            per_residue = DIL_QBLOCKS // dilation
            _gather_residues(q_ref, qs, 0, dilation, QK_SCALE)
            _gather_residues(k_ref, ks, BLK, dilation)
            _gather_residues(v_ref, vs, BLK, dilation)
            geometry = _dil_geometry(grp, pair)

            def step(blk, _):
                off = pl.multiple_of(blk * BLK, BLK)
                q2 = _stack_heads(qs[pl.ds(off, BLK), :], lane0)
                s = _dil_tile_scores(q2, ks[pl.ds(off, 2 * BLK), :], geometry, blk % per_residue != 0)
                m = jnp.max(s, axis=1, keepdims=True)
                p = jnp.exp(s - m)
                den = jnp.sum(p, axis=1, keepdims=True)
                out = _dot(p.astype(BF16), vs[pl.ds(off, 2 * BLK), :]) / den
                lse = m + jnp.log(den)
                o_res[pl.ds(off, BLK), :] = jnp.where(lane0, out[:BLK], out[BLK:])
                lse_res[pl.ds(off, BLK), :] = jnp.where(lane0, lse[:BLK], lse[BLK:])
                return 0

            lax.fori_loop(0, DIL_QBLOCKS, step, 0, unroll=8)
            _scatter_residues(o_res, 0, o_grp[grp], dilation)
            _scatter_residues(lse_res, 0, lse_grp[grp], dilation)

        for r0 in range(0, SEQ, 2 * BLK):
            rows = slice(r0, r0 + 2 * BLK)
            ls = [lse_grp[g][rows, :] for g in range(DIL_GROUPS)]
            m = jnp.maximum(jnp.maximum(ls[0], ls[1]), ls[2])
            w = [jnp.exp(l - m) for l in ls]
            den = w[0] + w[1] + w[2]
            o_ref[rows, :] = (w[0] * o_grp[0][rows, :] + w[1] * o_grp[1][rows, :] + w[2] * o_grp[2][rows, :]) / den
            lse_ref[rows, :] = m + jnp.log(den)

    def col(part, grp):
        return pl.BlockSpec((None, SEQ, LANES), lambda b, p: (b, 0, 6 * part + 2 * grp + p))

    out_spec = pl.BlockSpec((None, SEQ, LANES), lambda b, p: (b, 0, p))
    out = jax.ShapeDtypeStruct((B_LOC, SEQ, DIL_OUT), F32)
    return _call(
        body, send, name="dil_fwd", grid=(B_LOC, DIL_OUT // LANES),
        in_specs=[col(part, grp) for grp in range(DIL_GROUPS) for part in range(3)],
        out_specs=[out_spec, out_spec], out_shape=[out, out],
        scratch_shapes=[pltpu.VMEM((SEQ, LANES), BF16), pltpu.VMEM((SEQ + BLK, LANES), BF16),
                        pltpu.VMEM((SEQ + BLK, LANES), BF16), pltpu.VMEM((SEQ, LANES), F32),
                        pltpu.VMEM((SEQ, LANES), F32)] + [pltpu.VMEM((SEQ, LANES), F32)] * (2 * DIL_GROUPS),
        semantics=("parallel", "parallel"), operands=[qkv] * 9)


def _dil_bwd(qkv, d_o, lse, dsum, send=None):
    def body(*refs):
        ins, (do_ref, lse_ref, dsum_ref), outs = refs[:9], refs[9:12], refs[12:21]
        qs, ks, vs, dos, lse_res, dsum_res, dq_res, dk_acc, dv_acc = refs[21:]
        pair = pl.program_id(1)
        lane0 = lax.broadcasted_iota(jnp.int32, (BLK, LANES), 1) < HEAD_DIM
        lane1 = jnp.logical_not(lane0)
        ks[0:BLK, :] = jnp.zeros((BLK, LANES), BF16)
        vs[0:BLK, :] = jnp.zeros((BLK, LANES), BF16)
        for grp, (_, dilation) in enumerate(DIL_PAIRS):
            q_ref, k_ref, v_ref = ins[3 * grp:3 * grp + 3]
            dq_ref, dk_ref, dv_ref = outs[3 * grp:3 * grp + 3]
            per_residue = DIL_QBLOCKS // dilation
            _gather_residues(q_ref, qs, 0, dilation, QK_SCALE)
            _gather_residues(k_ref, ks, BLK, dilation)
            _gather_residues(v_ref, vs, BLK, dilation)
            _gather_residues(do_ref, dos, 0, dilation)
            _gather_residues(lse_ref, lse_res, 0, dilation)
            _gather_residues(dsum_ref, dsum_res, 0, dilation)
            dk_acc[...] = jnp.zeros_like(dk_acc)
            dv_acc[...] = jnp.zeros_like(dv_acc)
            geometry = _dil_geometry(grp, pair)

            def step(blk, _):
                off = pl.multiple_of(blk * BLK, BLK)
                q2 = _stack_heads(qs[pl.ds(off, BLK), :], lane0)
                do2 = _stack_heads(dos[pl.ds(off, BLK), :], lane0)
                kk = ks[pl.ds(off, 2 * BLK), :]
                vv = vs[pl.ds(off, 2 * BLK), :]
                lse_blk = lse_res[pl.ds(off, BLK), :]
                dsum_blk = dsum_res[pl.ds(off, BLK), :]
                lse2 = jnp.concatenate([_head_col(lse_blk, lane0), _head_col(lse_blk, lane1)], axis=0)
                dsum2 = jnp.concatenate([_head_col(dsum_blk, lane0), _head_col(dsum_blk, lane1)], axis=0)
                s = _dil_tile_scores(q2, kk, geometry, blk % per_residue != 0)
                p = jnp.exp(s - lse2)
                ds = (p * (_dot_nt(do2, vv) - dsum2)).astype(BF16)
                dq2 = _dot(ds, kk)
                dq_res[pl.ds(off, BLK), :] = jnp.where(lane0, dq2[:BLK], dq2[BLK:]) * QK_SCALE
                dk_acc[pl.ds(off, 2 * BLK), :] += _dot_tn(ds, q2)
                dv_acc[pl.ds(off, 2 * BLK), :] += _dot_tn(p.astype(BF16), do2)
                return 0

            lax.fori_loop(0, DIL_QBLOCKS, step, 0, unroll=8)
            _scatter_residues(dq_res, 0, dq_ref, dilation)
            _scatter_residues(dk_acc, BLK, dk_ref, dilation)
            _scatter_residues(dv_acc, BLK, dv_ref, dilation)

    def col(part, grp):
        return pl.BlockSpec((None, SEQ, LANES), lambda b, p: (b, 0, 6 * part + 2 * grp + p))

    slot = pl.BlockSpec((None, SEQ, LANES), lambda b, p: (b, 0, p))
    out = jax.ShapeDtypeStruct((B_LOC, SEQ, DIL_OUT), F32)
    return _call(
        body, send, name="dil_bwd", grid=(B_LOC, DIL_OUT // LANES),
        in_specs=[col(part, grp) for grp in range(DIL_GROUPS) for part in range(3)] + [slot] * 3,
        out_specs=[slot] * 9, out_shape=[out] * 9,
        scratch_shapes=[pltpu.VMEM((SEQ, LANES), BF16), pltpu.VMEM((SEQ + BLK, LANES), BF16),
                        pltpu.VMEM((SEQ + BLK, LANES), BF16), pltpu.VMEM((SEQ, LANES), BF16),
                        pltpu.VMEM((SEQ, LANES), F32), pltpu.VMEM((SEQ, LANES), F32),
                        pltpu.VMEM((SEQ, LANES), F32), pltpu.VMEM((SEQ + BLK, LANES), F32),
                        pltpu.VMEM((SEQ + BLK, LANES), F32)],
        semantics=("parallel", "parallel"), operands=[qkv] * 9 + [d_o, lse, dsum])


def _peers():
    x, y, c = lax.axis_index("x"), lax.axis_index("y"), lax.axis_index("c")
    me = 4 * x + 2 * y + c
    peers = []
    for mask in range(1, N_DEV):
        px = 1 - x if mask & 4 else x
        py = 1 - y if mask & 2 else y
        pc = 1 - c if mask & 1 else c
        peers.append(((px, py, pc), 4 * px + 2 * py + pc))
    return me, peers


def _all_gather(shard, name):
    def body(src_ref, out_ref, send_sems, recv_sems, local_sem):
        x, y, c = lax.axis_index("x"), lax.axis_index("y"), lax.axis_index("c")
        sibling = (x, y, 1 - c)
        chips = [(1 - x, y), (x, 1 - y), (1 - x, 1 - y)]

        def slot(px, py, pc):
            return out_ref.at[4 * px + 2 * py + pc]

        def copy(k, block, to, src=None):
            return pltpu.make_async_remote_copy(
                src_ref=slot(*block) if src is None else src, dst_ref=slot(*block),
                send_sem=send_sems.at[k], recv_sem=recv_sems.at[k], device_id=to,
                device_id_type=pl.DeviceIdType.MESH)

        mine = pltpu.make_async_copy(src_ref, slot(x, y, c), local_sem)
        mine.start()
        first = [copy(0, (x, y, c), sibling, src=src_ref)]
        first += [copy(1 + j, (x, y, c), (*chip, c), src=src_ref) for j, chip in enumerate(chips)]
        for cp in first:
            cp.start()
        passed = [copy(4 + j, (*chip, c), sibling) for j, chip in enumerate(chips)]
        for j, chip in enumerate(chips):
            copy(1 + j, (*chip, c), (x, y, c)).wait_recv()
            passed[j].start()
        copy(0, sibling, (x, y, c)).wait_recv()
        for j, chip in enumerate(chips):
            copy(4 + j, (*chip, 1 - c), (x, y, c)).wait_recv()
        for cp in first + passed:
            cp.wait_send()
        mine.wait()

    return pl.pallas_call(
        body, name=name,
        in_specs=[pl.BlockSpec(memory_space=pl.ANY)],
        out_specs=pl.BlockSpec(memory_space=pl.ANY),
        out_shape=jax.ShapeDtypeStruct((N_DEV,) + shard.shape, shard.dtype),
        scratch_shapes=[pltpu.SemaphoreType.DMA((N_DEV - 1,)), pltpu.SemaphoreType.DMA((N_DEV - 1,)),
                        pltpu.SemaphoreType.DMA],
    )(shard)


def _call(body, send, *, name, grid, in_specs, out_specs, out_shape, scratch_shapes, semantics, operands):
    if send is None:
        return pl.pallas_call(
            body, name=name, grid=grid, in_specs=in_specs, out_specs=out_specs, out_shape=out_shape,
            scratch_shapes=scratch_shapes, compiler_params=_params(semantics))(*operands), []
    srcs, kinds = [s for s, _ in send], [k for _, k in send]
    n, n_in, n_out, n_scr = len(srcs), len(in_specs), len(out_specs), len(scratch_shapes)
    steps = math.prod(grid)
    relay_step = (13 * steps) // 16

    def plan(refs):
        src_refs, land_refs = refs[n_in:n_in + n], refs[n_in + n + n_out:n_in + 2 * n + n_out]
        send_sems, recv_sems, local_sems = refs[-3:]
        x, y, c = lax.axis_index("x"), lax.axis_index("y"), lax.axis_index("c")
        me, peers = _peers()
        first, relayed_in, relayed_out, arrivals, sends, own = [], [], [], [], [], []
        for a, kind in enumerate(kinds):
            def copy(k, src, dst_slot, to):
                return pltpu.make_async_remote_copy(
                    src_ref=src, dst_ref=land_refs[a].at[dst_slot], send_sem=send_sems.at[a * (N_DEV - 1) + k],
                    recv_sem=recv_sems.at[a * (N_DEV - 1) + k], device_id=to, device_id_type=pl.DeviceIdType.MESH)

            if kind == "gather_by_chip":
                idx = lambda px, py, pc: 4 * px + 2 * py + pc
                chips = [(1 - x, y), (x, 1 - y), (1 - x, 1 - y)]
                mine = [copy(0, src_refs[a], me, (x, y, 1 - c))]
                arrivals.append(copy(0, src_refs[a], idx(x, y, 1 - c), (x, y, 1 - c)))
                for j, (px, py) in enumerate(chips):
                    mine.append(copy(1 + j, src_refs[a], me, (px, py, c)))
                    relayed_in.append(copy(1 + j, src_refs[a], idx(px, py, c), (px, py, c)))
                    relayed_out.append(copy(4 + j, land_refs[a].at[idx(px, py, c)], idx(px, py, c), (x, y, 1 - c)))
                    arrivals.append(copy(4 + j, src_refs[a], idx(px, py, 1 - c), (x, y, 1 - c)))
                first += mine
                sends += mine + relayed_out[-3:]
                own.append(pltpu.make_async_copy(src_refs[a], land_refs[a].at[me], local_sems.at[a]))
            elif kind == "scatter_by_chip":
                for k, (px, py) in enumerate([(1 - x, y), (x, 1 - y), (1 - x, 1 - y)]):
                    first.append(copy(k, src_refs[a].at[2 * px + py], 2 * x + y, (px, py, c)))
                    arrivals.append(copy(k, src_refs[a].at[2 * px + py], 2 * px + py, (px, py, c)))
                sends += first[-3:]
                own.append(pltpu.make_async_copy(src_refs[a].at[2 * x + y], land_refs[a].at[2 * x + y],
                                                 local_sems.at[a]))
            else:
                part = (lambda i: src_refs[a].at[i]) if kind == "scatter" else (lambda i: src_refs[a])
                for k, (peer, peer_idx) in enumerate(peers):
                    first.append(copy(k, part(peer_idx), me, peer))
                    arrivals.append(copy(k, part(peer_idx), peer_idx, peer))
                sends += first[-(N_DEV - 1):]
                own.append(pltpu.make_async_copy(part(me), land_refs[a].at[me], local_sems.at[a]))
        return first, relayed_in, relayed_out, arrivals, sends, own

    def wrapped(*refs):
        step = 0
        for axis, size in enumerate(grid):
            step = step * size + pl.program_id(axis)

        @pl.when(step == 0)
        def _():
            first, _, _, _, _, own = plan(refs)
            for cp in first + own:
                cp.start()

        if "gather_by_chip" in kinds:
            @pl.when(step == relay_step)
            def _():
                _, relayed_in, relayed_out, _, _, _ = plan(refs)
                for cp_in, cp_out in zip(relayed_in, relayed_out):
                    cp_in.wait_recv()
                    cp_out.start()

        body(*refs[:n_in], *refs[n_in + n:n_in + n + n_out], *refs[n_in + 2 * n + n_out:n_in + 2 * n + n_out + n_scr])

        @pl.when(step == steps - 1)
        def _():
            _, _, _, arrivals, sends, own = plan(refs)
            for cp in arrivals:
                cp.wait_recv()
            for cp in sends:
                cp.wait_send()
            for cp in own:
                cp.wait()

    anywhere = pl.BlockSpec(memory_space=pl.ANY)
    lands = [jax.ShapeDtypeStruct((N_DEV // 2 if k == "scatter_by_chip" else N_DEV,) + s.shape[-2:], s.dtype)
             for s, k in send]
    out = pl.pallas_call(
        wrapped, name=name, grid=grid,
        in_specs=list(in_specs) + [anywhere] * n, out_specs=list(out_specs) + [anywhere] * n,
        out_shape=list(out_shape) + lands,
        scratch_shapes=list(scratch_shapes) + [pltpu.SemaphoreType.DMA((n * (N_DEV - 1),)),
                                               pltpu.SemaphoreType.DMA((n * (N_DEV - 1),)),
                                               pltpu.SemaphoreType.DMA((n,))],
        compiler_params=_params(("arbitrary",) * len(grid)),
    )(*operands, *srcs)
    return out[:n_out], list(out[n_out:])


def _pair_swap(blocks):
    def body(src_ref, out_ref, send_sems, recv_sems):
        x, y, c = lax.axis_index("x"), lax.axis_index("y"), lax.axis_index("c")
        copies = [pltpu.make_async_remote_copy(
            src_ref=src_ref.at[2 * chip + (1 - c)], dst_ref=out_ref.at[chip], send_sem=send_sems.at[chip],
            recv_sem=recv_sems.at[chip], device_id=(x, y, 1 - c), device_id_type=pl.DeviceIdType.MESH)
            for chip in range(N_DEV // 2)]
        for cp in copies:
            cp.start()
        for cp in copies:
            cp.wait()

    return pl.pallas_call(
        body, name="pair_swap_grad_w_in",
        in_specs=[pl.BlockSpec(memory_space=pl.ANY)], out_specs=pl.BlockSpec(memory_space=pl.ANY),
        out_shape=jax.ShapeDtypeStruct((N_DEV // 2,) + blocks.shape[1:], blocks.dtype),
        scratch_shapes=[pltpu.SemaphoreType.DMA((N_DEV // 2,)), pltpu.SemaphoreType.DMA((N_DEV // 2,))],
    )(blocks)


def _pair_sum(blocks, swapped, core):
    _, rows, cols = swapped.shape
    tile_rows = _row_tile(rows)

    def body(core_ref, mine_ref, theirs_ref, o_ref):
        o_ref[...] = (mine_ref[...].astype(F32) + theirs_ref[...].astype(F32)).astype(o_ref.dtype)

    return pl.pallas_call(
        body, name="pair_sum_grad_w_in",
        grid_spec=pltpu.PrefetchScalarGridSpec(
            num_scalar_prefetch=1, grid=(N_DEV // 2, rows // tile_rows),
            in_specs=[pl.BlockSpec((None, tile_rows, cols), lambda j, i, core_ref: (2 * j + core_ref[0], i, 0)),
                      pl.BlockSpec((None, tile_rows, cols), lambda j, i, core_ref: (j, i, 0))],
            out_specs=pl.BlockSpec((None, tile_rows, cols), lambda j, i, core_ref: (j, i, 0))),
        out_shape=jax.ShapeDtypeStruct(swapped.shape, swapped.dtype),
        compiler_params=_params(("parallel", "parallel")),
    )(core, blocks, swapped)


def _sum_in_device_order(land_ref):
    acc = land_ref[0].astype(F32)
    for j in range(1, land_ref.shape[0]):
        acc = acc + land_ref[j].astype(F32)
    return acc


def _adam_math(w, g, m, v):
    c1 = 1.0 - ADAM_B1 ** ADAM_STEP
    c2 = 1.0 - ADAM_B2 ** ADAM_STEP
    m_new = ADAM_B1 * m + (1.0 - ADAM_B1) * g
    v_new = ADAM_B2 * v + (1.0 - ADAM_B2) * (g * g)
    delta = -ADAM_LR * ((m_new / c1) / (jnp.sqrt(v_new / c2) + ADAM_EPS) + ADAM_WD * w)
    return delta, m_new, v_new


def _row_tile(rows):
    return max(t for t in range(8, 257, 8) if rows % t == 0) if rows % 8 == 0 else rows


def _sum_update(land, w, m, v, name):
    slots, rows, cols = land.shape
    tile_rows = _row_tile(rows)

    def body(land_ref, w_ref, m_ref, v_ref, g_ref, d_ref, nm_ref, nv_ref):
        g = _sum_in_device_order(land_ref)
        g_ref[...] = g
        d_ref[...], nm_ref[...], nv_ref[...] = _adam_math(w_ref[...], g, m_ref[...], v_ref[...])

    tile = pl.BlockSpec((None, tile_rows, cols), lambda i: (0, i, 0))
    out = jax.ShapeDtypeStruct((1, rows, cols), F32)
    return pl.pallas_call(
        body, name=name, grid=(rows // tile_rows,),
        in_specs=[pl.BlockSpec((slots, tile_rows, cols), lambda i: (0, i, 0)), tile, tile, tile],
        out_specs=[tile] * 4, out_shape=[out] * 4,
        compiler_params=_params(("parallel",)),
    )(land, w, m, v)


def _sum_gains(land):
    def body(land_ref, o_ref):
        o_ref[...] = _sum_in_device_order(land_ref)

    return pl.pallas_call(
        body, name="sum_gain_grads", grid=(1,),
        in_specs=[pl.BlockSpec(land.shape, lambda i: (0, 0, 0))],
        out_specs=pl.BlockSpec(land.shape[1:], lambda i: (0, 0)),
        out_shape=jax.ShapeDtypeStruct(land.shape[1:], F32),
    )(land)


def _adamw(w, g, m, v, name):
    def body(w_ref, g_ref, m_ref, v_ref, d_ref, nm_ref, nv_ref):
        d_ref[...], nm_ref[...], nv_ref[...] = _adam_math(w_ref[...], g_ref[...], m_ref[...], v_ref[...])

    whole = pl.BlockSpec(w.shape, lambda i: (0, 0))
    out = jax.ShapeDtypeStruct(w.shape, F32)
    return pl.pallas_call(
        body, name=name, grid=(1,),
        in_specs=[whole] * 4, out_specs=[whole] * 3, out_shape=[out] * 3,
    )(w, g, m, v)


GROUP_FFN = ("w_ffn_in", "w_ffn_out")
GROUP_MIX = ("w_sb_up", "w_dil_up", "w_out")
COL_SHARDED = ("w_in", "w_sb_up", "w_dil_up", "w_ffn_in")


def _full_from_shards(name, slots):
    _, r, c = slots.shape
    if name in COL_SHARDED:
        return slots.transpose(1, 0, 2).reshape(r, N_DEV * c)
    return slots.reshape(N_DEV * r, c)


def _shards_from_full(name, full):
    rows, cols = full.shape
    if name in COL_SHARDED:
        return full.reshape(rows, N_DEV, cols // N_DEV).transpose(1, 0, 2)
    return full.reshape(N_DEV, rows // N_DEV, cols)


def _local_step(x, target, g_mix, g_ffn, g_fin, w_in, shards=None, rest=None):
    gather = lambda names, kind: None if shards is None else [(shards[n], kind) for n in names]
    scatter = lambda blocks: None if shards is None else [(t, "scatter") for t in blocks]
    landed = lambda blocks, lands: lands if lands else blocks

    w = {"w_in": w_in}
    if shards is None:
        w.update(rest)
        w["w_ffn_in"] = _shards_from_full("w_ffn_in", rest["w_ffn_in"])
    (qkv_sb, qkv_dl, gates, u), _ = _norm_proj(x, g_mix, w["w_in"])
    qkv_sb = qkv_sb.reshape(B_LOC, SEQ, 3 * SB_WIDTH)
    qkv_dl = qkv_dl.reshape(B_LOC, SEQ, 3 * DIL_WIDTH)
    (o_sb,), lands = _sb_fwd(qkv_sb, gather(GROUP_FFN, "gather_by_chip"))
    if lands:
        w["w_ffn_in"], w["w_ffn_out"] = lands[0], _full_from_shards("w_ffn_out", lands[1])
    o_sb = o_sb.reshape(TOK, SB_WIDTH)
    (o_dl, lse), lands = _dil_fwd(qkv_dl, gather(GROUP_MIX, "gather"))
    w.update({n: _full_from_shards(n, t) for n, t in zip(GROUP_MIX, lands)})
    o_dl = o_dl.reshape(TOK, DIL_OUT)

    loss, dx1, merged, u2, act, dh, dx2, dg_fin, dg_ffn = _mix_ffn_fwd_bwd(
        x, o_sb, o_dl, gates, w["w_sb_up"], w["w_dil_up"], w["w_out"], target, g_ffn, g_fin,
        w["w_ffn_in"], w["w_ffn_out"])
    dgates, dy_sb, dy_dl, do_sb, do_dl, dsum = _mix_bwd(dx1, o_sb, o_dl, gates, w["w_sb_up"], w["w_dil_up"], w["w_out"])
    blocks = {
        "w_sb_up": _atb(o_sb, dy_sb, "grad_w_sb_up", SB_WIDTH, D_MODEL, col_blocks=N_DEV),
        "w_dil_up": _atb(o_dl, dy_dl, "grad_w_dil_up", DIL_OUT, D_MODEL, col_blocks=N_DEV),
        "w_out": _shards_from_full("w_out", _atb(merged, dx1, "grad_w_out", D_MODEL, D_MODEL)),
        "w_ffn_in": _atb_shards(u2, dh, "grad_w_ffn_in"),
        "w_ffn_out": _shards_from_full("w_ffn_out", _atb_shards(act, dx2, "grad_w_ffn_out")),
    }
    grads = {}

    early, late = GROUP_FFN, GROUP_MIX
    early_blocks = [blocks[n] for n in early]
    (dq_sb, dk_sb, dv_sb), lands = _sb_bwd(qkv_sb, do_sb.reshape(B_LOC, SEQ, SB_WIDTH), scatter(early_blocks))
    grads.update(zip(early, landed(early_blocks, lands)))
    as_batch = lambda t: t.reshape(B_LOC, SEQ, DIL_OUT)
    late_blocks = [blocks[n] for n in late]
    d_dl, lands = _dil_bwd(qkv_dl, as_batch(do_dl), lse, as_batch(dsum), scatter(late_blocks))
    grads.update(zip(late, landed(late_blocks, lands)))
    flat = lambda t: t.reshape(TOK, -1)
    dproj = ([flat(dq_sb), flat(dk_sb), flat(dv_sb)]
             + [flat(d_dl[3 * grp + part]) for part in range(3) for grp in range(DIL_GROUPS)] + [dgates])

    w_in_blocks = _shards_from_full("w_in", _atb_pieces(u, dproj, "grad_w_in", D_MODEL // 2))
    if shards is None:
        grads["w_in"] = w_in_blocks
        send = None
    else:
        core = lax.axis_index("c").astype(jnp.int32).reshape(1)
        send = [(_pair_sum(w_in_blocks, _pair_swap(w_in_blocks), core), "scatter_by_chip")]
    (grad_x, dg_mix), lands = _proj_bwd(dproj, dx1, x, g_mix, w["w_in"], send)
    if lands:
        grads["w_in"] = lands[0]
    gain_grads = jnp.concatenate([dg_mix, dg_ffn, dg_fin], axis=0)
    return loss, grad_x, gain_grads, grads


def kernel(x, norm_mix_g, w_in, w_sb_up, w_dil_up, w_out, norm_ffn_g, w_ffn_in, w_ffn_out, norm_final_g, loss_target, m_norm_mix_g, m_w_in, m_w_sb_up, m_w_dil_up, m_w_out, m_norm_ffn_g, m_w_ffn_in, m_w_ffn_out, m_norm_final_g, v_norm_mix_g, v_w_in, v_w_sb_up, v_w_dil_up, v_w_out, v_norm_ffn_g, v_w_ffn_in, v_w_ffn_out, v_norm_final_g):
    mats = {"w_in": w_in, "w_sb_up": w_sb_up, "w_dil_up": w_dil_up, "w_out": w_out,
            "w_ffn_in": w_ffn_in, "w_ffn_out": w_ffn_out}
    moments_m = {"w_in": m_w_in, "w_sb_up": m_w_sb_up, "w_dil_up": m_w_dil_up, "w_out": m_w_out,
                 "w_ffn_in": m_w_ffn_in, "w_ffn_out": m_w_ffn_out}
    moments_v = {"w_in": v_w_in, "w_sb_up": v_w_sb_up, "w_dil_up": v_w_dil_up, "w_out": v_w_out,
                 "w_ffn_in": v_w_ffn_in, "w_ffn_out": v_w_ffn_out}
    gathered_w_in = _all_gather(w_in[0].astype(BF16), "all_gather_w_in")
    g_fin = norm_final_g.reshape(1, D_MODEL)
    loss, grad_x, gain_grads, grad_slots = _local_step(
        x.reshape(TOK, D_MODEL), loss_target.reshape(TOK, D_MODEL), norm_mix_g, norm_ffn_g, g_fin,
        _full_from_shards("w_in", gathered_w_in),
        shards={name: mats[name][0].astype(BF16) for name in GROUP_FFN + GROUP_MIX})

    gain_rows = jnp.concatenate([gain_grads, jnp.tile(loss, (1, D_MODEL // LANES)),
                                 jnp.zeros((8 - 4, D_MODEL), F32)], axis=0)
    g_gains = _sum_gains(_all_gather(gain_rows, "all_gather_gains"))

    out_g, out_d, out_m, out_v = {}, {}, {}, {}
    for name, slots in grad_slots.items():
        out_g[name], out_d[name], out_m[name], out_v[name] = _sum_update(
            slots, mats[name], moments_m[name], moments_v[name], "update_" + name)

    gain_w = jnp.concatenate([norm_mix_g, norm_ffn_g, g_fin], axis=0)
    gain_m = jnp.concatenate([m_norm_mix_g, m_norm_ffn_g, m_norm_final_g.reshape(1, D_MODEL)], axis=0)
    gain_v = jnp.concatenate([v_norm_mix_g, v_norm_ffn_g, v_norm_final_g.reshape(1, D_MODEL)], axis=0)
    gd, gm, gv = _adamw(gain_w, g_gains[:3], gain_m, gain_v, "adamw_gains")
    for idx, name in enumerate(("norm_mix_g", "norm_ffn_g", "norm_final_g")):
        shape = (D_MODEL,) if name == "norm_final_g" else (1, D_MODEL)
        out_g[name] = g_gains[idx].reshape(shape)
        out_d[name], out_m[name], out_v[name] = gd[idx].reshape(shape), gm[idx].reshape(shape), gv[idx].reshape(shape)

    order = ("norm_mix_g", "w_in", "w_sb_up", "w_dil_up", "w_out", "norm_ffn_g", "w_ffn_in", "w_ffn_out",
             "norm_final_g")
    return (g_gains[3, 0], grad_x.reshape(B_LOC, SEQ, D_MODEL),
            *[out_g[n] for n in order], *[out_d[n] for n in order],
            *[out_m[n] for n in order], *[out_v[n] for n in order])
```

```python
import math

import jax
import jax.numpy as jnp
from jax import lax
from jax.experimental import pallas as pl
from jax.experimental.pallas import tpu as pltpu

F32 = jnp.float32
BF16 = jnp.bfloat16

N_DEV = 8
D_MODEL = 1024
SEQ = 2048
B_LOC = 2
TOK = B_LOC * SEQ
HEAD_DIM = 64
SB_WIDTH = 512
DIL_WIDTH = 768
DIL_OUT = 256
QKV_WIDTH = 3 * SB_WIDTH + 3 * DIL_WIDTH
IN_WIDTH = QKV_WIDTH + 2 * D_MODEL
D_FF = 2816
DIL_PAIRS = ((128, 1), (512, 4), (2048, 16))
DIL_HEADS = 12
RMS_EPS = 1e-6
ALIBI_MAX_BIAS = 8.0
QK_SCALE = 1.0 / math.sqrt(HEAD_DIM)
BLK = 128
LANES = 128
NEG_BIG = -1e30

ADAM_LR = 0.001
ADAM_B1 = 0.9
ADAM_B2 = 0.999
ADAM_EPS = 1e-08
ADAM_WD = 0.01
ADAM_STEP = 10

VMEM_LIMIT = 56 * 1024 * 1024


def _dot(a, b):
    return jnp.dot(a, b, preferred_element_type=F32)


def _dot_nt(a, b):
    return lax.dot_general(a, b, (((1,), (1,)), ((), ())), preferred_element_type=F32)


def _dot_tn(a, b):
    return lax.dot_general(a, b, (((0,), (0,)), ((), ())), preferred_element_type=F32)


def _sigmoid(z):
    return 1.0 / (1.0 + jnp.exp(-z))


def _split_bf16(v):
    hi = v.astype(BF16)
    lo = (v - hi.astype(F32)).astype(BF16)
    return hi, lo


def _chunks(width, step=512):
    out, c = [], 0
    while c < width:
        w = min(step, width - c)
        out.append((c, w))
        c += w
    return out


def _resident(shape):
    nd = len(shape)
    return pl.BlockSpec(shape, lambda *_: (0,) * nd, pipeline_mode=pl.Buffered(1))


def _params(sem):
    return pltpu.CompilerParams(dimension_semantics=sem, vmem_limit_bytes=VMEM_LIMIT)


def _rms_fwd(x, g):
    r = lax.rsqrt(jnp.mean(x * x, axis=-1, keepdims=True) + RMS_EPS)
    n = x * r
    return n, r, n * g


def _rms_bwd(dy, n, r, g):
    dg = jnp.sum(dy * n, axis=0, keepdims=True)
    dn = dy * g
    dx = r * (dn - n * jnp.mean(dn * n, axis=-1, keepdims=True))
    return dx, dg


TM = 256


def _norm_proj(x, g, w_in, send=None):
    def body(x_ref, g_ref, w_ref, sb_ref, dl_ref, gate_ref, u_ref):
        _, _, u = _rms_fwd(x_ref[...], g_ref[...])
        u = u.astype(BF16)
        u_ref[...] = u
        for c0, w in _chunks(3 * SB_WIDTH):
            sb_ref[:, c0:c0 + w] = _dot(u, w_ref[:, c0:c0 + w]).astype(BF16)
        for c0, w in _chunks(3 * DIL_WIDTH):
            dl_ref[:, c0:c0 + w] = _dot(u, w_ref[:, 3 * SB_WIDTH + c0:3 * SB_WIDTH + c0 + w])
        for c0, w in _chunks(2 * D_MODEL):
            gate_ref[:, c0:c0 + w] = _dot(u, w_ref[:, QKV_WIDTH + c0:QKV_WIDTH + c0 + w])

    return _call(
        body, send, name="norm_proj", grid=(TOK // TM,),
        in_specs=[pl.BlockSpec((TM, D_MODEL), lambda i: (i, 0)), _resident((1, D_MODEL)),
                  _resident((D_MODEL, IN_WIDTH))],
        out_specs=[pl.BlockSpec((TM, 3 * SB_WIDTH), lambda i: (i, 0)),
                   pl.BlockSpec((TM, 3 * DIL_WIDTH), lambda i: (i, 0)),
                   pl.BlockSpec((TM, 2 * D_MODEL), lambda i: (i, 0)),
                   pl.BlockSpec((TM, D_MODEL), lambda i: (i, 0))],
        out_shape=[jax.ShapeDtypeStruct((TOK, 3 * SB_WIDTH), BF16),
                   jax.ShapeDtypeStruct((TOK, 3 * DIL_WIDTH), F32),
                   jax.ShapeDtypeStruct((TOK, 2 * D_MODEL), F32),
                   jax.ShapeDtypeStruct((TOK, D_MODEL), BF16)],
        scratch_shapes=[], semantics=("parallel",), operands=(x, g, w_in))


FF_SHARD = 2 * D_FF // N_DEV
FF_PAIRS = N_DEV // 2


def _mix_ffn_fwd_bwd(x, o_sb, o_dl, gates, w_sb_up, w_dil_up, w_out, target, g_ffn, g_fin, w_ffn_in, w_ffn_out):
    def body(x_ref, osb_ref, odl_ref, gate_ref, wsb_ref, wdl_ref, wo_ref, t_ref, gffn_ref, gfin_ref, win_ref, wout_ref,
             loss_ref, dx1_ref, mg_ref, u2_ref, act_ref, dh_ref, dx2_ref, dgfin_ref, dgffn_ref, h_scr):
        i = pl.program_id(0)

        @pl.when(i == 0)
        def _():
            loss_ref[...] = jnp.zeros_like(loss_ref)
            dgfin_ref[...] = jnp.zeros_like(dgfin_ref)
            dgffn_ref[...] = jnp.zeros_like(dgffn_ref)

        y_sb = _dot(osb_ref[...], wsb_ref[...])
        y_dl = _dot(odl_ref[...].astype(BF16), wdl_ref[...])
        merged = (_sigmoid(gate_ref[:, :D_MODEL]) * y_sb
                  + _sigmoid(gate_ref[:, D_MODEL:]) * y_dl).astype(BF16)
        mg_ref[...] = merged
        x1 = x_ref[...] + _dot(merged, wo_ref[...])
        g_ffn_v = gffn_ref[...]
        g_fin_v = gfin_ref[...]
        n2, r2, u2 = _rms_fwd(x1, g_ffn_v)
        u2 = u2.astype(BF16)
        u2_ref[...] = u2
        x2 = x1
        for r in range(FF_PAIRS):
            gate = _dot(u2, win_ref[r])
            up = _dot(u2, win_ref[r + FF_PAIRS])
            h_scr[r] = gate
            h_scr[r + FF_PAIRS] = up
            act = (gate * _sigmoid(gate) * up).astype(BF16)
            act_ref[r] = act
            x2 = x2 + _dot(act, wout_ref[r * FF_SHARD:(r + 1) * FF_SHARD, :])
        n3, r3, y = _rms_fwd(x2, g_fin_v)
        err = y - t_ref[...]
        sq = jnp.sum(jnp.sum(err * err, axis=1, keepdims=True), axis=0, keepdims=True)
        loss_ref[...] += sq * (0.5 / D_MODEL)
        dx2, dgfin = _rms_bwd(err * (1.0 / D_MODEL), n3, r3, g_fin_v)
        dgfin_ref[...] += dgfin
        dx2_b = dx2.astype(BF16)
        dx2_ref[...] = dx2_b
        du2 = jnp.zeros((TM, D_MODEL), F32)
        for r in range(FF_PAIRS):
            gate = h_scr[r]
            up = h_scr[r + FF_PAIRS]
            dact = _dot_nt(dx2_b, wout_ref[r * FF_SHARD:(r + 1) * FF_SHARD, :])
            sg = _sigmoid(gate)
            dgate = (dact * up * (sg * (1.0 + gate * (1.0 - sg)))).astype(BF16)
            dup = (dact * (gate * sg)).astype(BF16)
            dh_ref[r] = dgate
            dh_ref[r + FF_PAIRS] = dup
            du2 = du2 + _dot_nt(dgate, win_ref[r])
            du2 = du2 + _dot_nt(dup, win_ref[r + FF_PAIRS])
        dx1_n, dgffn = _rms_bwd(du2, n2, r2, g_ffn_v)
        dgffn_ref[...] += dgffn
        dx1_ref[...] = dx2 + dx1_n

    tile = lambda w: pl.BlockSpec((TM, w), lambda i: (i, 0))
    shards = lambda n: pl.BlockSpec((n, TM, FF_SHARD), lambda i: (0, i, 0))
    acc = lambda w: pl.BlockSpec((1, w), lambda i: (0, 0))
    return pl.pallas_call(
        body, name="mix_ffn_fwd_bwd", grid=(TOK // TM,),
        in_specs=[tile(D_MODEL), tile(SB_WIDTH), tile(DIL_OUT), tile(2 * D_MODEL),
                  _resident((SB_WIDTH, D_MODEL)), _resident((DIL_OUT, D_MODEL)), _resident((D_MODEL, D_MODEL)),
                  tile(D_MODEL), _resident((1, D_MODEL)), _resident((1, D_MODEL)),
                  _resident((N_DEV, D_MODEL, FF_SHARD)), _resident((D_FF, D_MODEL))],
        out_specs=[acc(LANES), tile(D_MODEL), tile(D_MODEL), tile(D_MODEL), shards(FF_PAIRS), shards(N_DEV),
                   tile(D_MODEL), acc(D_MODEL), acc(D_MODEL)],
        out_shape=[jax.ShapeDtypeStruct((1, LANES), F32),
                   jax.ShapeDtypeStruct((TOK, D_MODEL), F32),
                   jax.ShapeDtypeStruct((TOK, D_MODEL), BF16),
                   jax.ShapeDtypeStruct((TOK, D_MODEL), BF16),
                   jax.ShapeDtypeStruct((FF_PAIRS, TOK, FF_SHARD), BF16),
                   jax.ShapeDtypeStruct((N_DEV, TOK, FF_SHARD), BF16),
                   jax.ShapeDtypeStruct((TOK, D_MODEL), BF16),
                   jax.ShapeDtypeStruct((1, D_MODEL), F32),
                   jax.ShapeDtypeStruct((1, D_MODEL), F32)],
        scratch_shapes=[pltpu.VMEM((N_DEV, TM, FF_SHARD), F32)],
        compiler_params=_params(("arbitrary",)),
    )(x, o_sb, o_dl, gates, w_sb_up, w_dil_up, w_out, target, g_ffn, g_fin, w_ffn_in, w_ffn_out)


def _mix_bwd(dx1, o_sb, o_dl, gates, w_sb_up, w_dil_up, w_out):
    def body(dx1_ref, osb_ref, odl_ref, gate_ref, wsb_ref, wdl_ref, wout_ref,
             dgate_ref, dysb_ref, dydl_ref, dosb_ref, dodl_ref, dsum_ref):
        dmerged = _dot_nt(dx1_ref[...].astype(BF16), wout_ref[...])
        o_dl = odl_ref[...]
        y_sb = _dot(osb_ref[...], wsb_ref[...])
        y_dl = _dot(o_dl.astype(BF16), wdl_ref[...])
        s_sb = _sigmoid(gate_ref[:, :D_MODEL])
        s_dl = _sigmoid(gate_ref[:, D_MODEL:])
        dgate_ref[:, :D_MODEL] = (dmerged * y_sb * (s_sb * (1.0 - s_sb))).astype(BF16)
        dgate_ref[:, D_MODEL:] = (dmerged * y_dl * (s_dl * (1.0 - s_dl))).astype(BF16)
        dy_sb = (dmerged * s_sb).astype(BF16)
        dy_dl = (dmerged * s_dl).astype(BF16)
        dysb_ref[...] = dy_sb
        dydl_ref[...] = dy_dl
        dosb_ref[...] = _dot_nt(dy_sb, wsb_ref[...]).astype(BF16)
        do_dl = _dot_nt(dy_dl, wdl_ref[...])
        dodl_ref[...] = do_dl
        row = lax.broadcasted_iota(jnp.int32, (DIL_OUT, DIL_OUT), 0) // HEAD_DIM
        col = lax.broadcasted_iota(jnp.int32, (DIL_OUT, DIL_OUT), 1) // HEAD_DIM
        same_head = (row == col).astype(BF16)
        hi, lo = _split_bf16(do_dl * o_dl)
        dsum_ref[...] = _dot(hi, same_head) + _dot(lo, same_head)

    tile = lambda w: pl.BlockSpec((TM, w), lambda i: (i, 0))
    return pl.pallas_call(
        body, name="mix_bwd", grid=(TOK // TM,),
        in_specs=[tile(D_MODEL), tile(SB_WIDTH), tile(DIL_OUT), tile(2 * D_MODEL),
                  _resident((SB_WIDTH, D_MODEL)), _resident((DIL_OUT, D_MODEL)),
                  _resident((D_MODEL, D_MODEL))],
        out_specs=[tile(2 * D_MODEL), tile(D_MODEL), tile(D_MODEL), tile(SB_WIDTH), tile(DIL_OUT),
                   tile(DIL_OUT)],
        out_shape=[jax.ShapeDtypeStruct((TOK, 2 * D_MODEL), BF16),
                   jax.ShapeDtypeStruct((TOK, D_MODEL), BF16),
                   jax.ShapeDtypeStruct((TOK, D_MODEL), BF16),
                   jax.ShapeDtypeStruct((TOK, SB_WIDTH), BF16),
                   jax.ShapeDtypeStruct((TOK, DIL_OUT), F32),
                   jax.ShapeDtypeStruct((TOK, DIL_OUT), F32)],
        compiler_params=_params(("parallel",)),
    )(dx1, o_sb, o_dl, gates, w_sb_up, w_dil_up, w_out)


def _proj_bwd(dproj, dx1, x, g, w_in, send=None):
    widths = [p.shape[1] for p in dproj]

    def body(*refs):
        dx1_ref, x_ref, g_ref, w_ref, dx_ref, dg_ref = refs[len(widths):]

        @pl.when(pl.program_id(0) == 0)
        def _():
            dg_ref[...] = jnp.zeros_like(dg_ref)

        du = jnp.zeros((TM, D_MODEL), F32)
        c0 = 0
        for dp_ref, w in zip(refs, widths):
            du = du + _dot_nt(dp_ref[...].astype(BF16), w_ref[:, c0:c0 + w])
            c0 += w
        g_v = g_ref[...]
        n, r, _ = _rms_fwd(x_ref[...], g_v)
        dx, dg = _rms_bwd(du, n, r, g_v)
        dg_ref[...] += dg
        dx_ref[...] = dx1_ref[...] + dx

    tile = lambda w: pl.BlockSpec((TM, w), lambda i: (i, 0))
    return _call(
        body, send, name="proj_bwd", grid=(TOK // TM,),
        in_specs=[tile(w) for w in widths] + [tile(D_MODEL), tile(D_MODEL), _resident((1, D_MODEL)),
                                              _resident((D_MODEL, IN_WIDTH))],
        out_specs=[tile(D_MODEL), pl.BlockSpec((1, D_MODEL), lambda i: (0, 0))],
        out_shape=[jax.ShapeDtypeStruct((TOK, D_MODEL), F32),
                   jax.ShapeDtypeStruct((1, D_MODEL), F32)],
        scratch_shapes=[], semantics=("arbitrary",), operands=(*dproj, dx1, x, g, w_in))


def _atb_pieces(a, pieces, name, tm, tk=512):
    m = a.shape[1]
    widths = [p.shape[1] for p in pieces]
    n = sum(widths)
    nk = TOK // tk

    def body(a_ref, *refs):
        o_ref, acc_ref = refs[len(widths):]
        k = pl.program_id(1)

        @pl.when(k == 0)
        def _():
            acc_ref[...] = jnp.zeros_like(acc_ref)

        a_v = a_ref[...]
        c0 = 0
        for p_ref, w in zip(refs, widths):
            acc_ref[:, c0:c0 + w] += _dot_tn(a_v, p_ref[...].astype(BF16))
            c0 += w

        @pl.when(k == nk - 1)
        def _():
            o_ref[...] = acc_ref[...].astype(BF16)

    return pl.pallas_call(
        body, name=name, grid=(m // tm, nk),
        in_specs=[pl.BlockSpec((tk, tm), lambda i, k: (k, i))]
                 + [pl.BlockSpec((tk, w), lambda i, k: (k, 0)) for w in widths],
        out_specs=pl.BlockSpec((tm, n), lambda i, k: (i, 0)),
        out_shape=jax.ShapeDtypeStruct((m, n), BF16),
        scratch_shapes=[pltpu.VMEM((tm, n), F32)],
        compiler_params=_params(("parallel", "arbitrary")),
    )(a, *pieces)


def _atb_shards(a, b, name, tk=512):
    a_sharded = a.ndim == 3
    n, _, w = a.shape if a_sharded else b.shape
    other = (b if a_sharded else a).shape[1]

    def body(a_ref, b_ref, o_ref):
        acc = jnp.zeros(o_ref.shape, F32)
        for k0 in range(0, TOK, tk):
            acc = acc + _dot_tn(a_ref[k0:k0 + tk, :], b_ref[k0:k0 + tk, :])
        o_ref[...] = acc.astype(BF16)

    shard = pl.BlockSpec((None, TOK, w), lambda r: (r, 0, 0))
    whole = _resident((TOK, other))
    if a_sharded:
        out_spec, out_shape = pl.BlockSpec((w, other), lambda r: (r, 0)), (n * w, other)
    else:
        out_spec, out_shape = pl.BlockSpec((None, other, w), lambda r: (r, 0, 0)), (n, other, w)
    return pl.pallas_call(
        body, name=name, grid=(n,),
        in_specs=[shard, whole] if a_sharded else [whole, shard],
        out_specs=out_spec, out_shape=jax.ShapeDtypeStruct(out_shape, BF16),
        compiler_params=_params(("parallel",)),
    )(a, b)


def _atb(a, b, name, tm, tn, col_blocks=0, tk=512):
    m, n = a.shape[1], b.shape[1]
    nk = TOK // tk

    def body(a_ref, b_ref, o_ref, acc_ref):
        k = pl.program_id(2)

        @pl.when(k == 0)
        def _():
            acc_ref[...] = jnp.zeros_like(acc_ref)

        acc_ref[...] += _dot_tn(a_ref[...].astype(BF16), b_ref[...].astype(BF16))

        @pl.when(k == nk - 1)
        def _():
            if col_blocks:
                width = n // col_blocks
                for blk in range(col_blocks):
                    o_ref[blk] = acc_ref[:, blk * width:(blk + 1) * width].astype(BF16)
            else:
                o_ref[...] = acc_ref[...].astype(BF16)

    if col_blocks:
        out_spec = pl.BlockSpec((col_blocks, tm, n // col_blocks), lambda i, j, k: (0, i, 0))
        out_shape = jax.ShapeDtypeStruct((col_blocks, m, n // col_blocks), BF16)
    else:
        out_spec = pl.BlockSpec((tm, tn), lambda i, j, k: (i, j))
        out_shape = jax.ShapeDtypeStruct((m, n), BF16)
    return pl.pallas_call(
        body, name=name, grid=(m // tm, n // tn, nk),
        in_specs=[pl.BlockSpec((tk, tm), lambda i, j, k: (k, i)),
                  pl.BlockSpec((tk, tn), lambda i, j, k: (k, j))],
        out_specs=out_spec, out_shape=out_shape,
        scratch_shapes=[pltpu.VMEM((tm, tn), F32)],
        compiler_params=_params(("parallel", "parallel", "arbitrary")),
    )(a, b)


SB_PAIRS = SB_WIDTH // LANES


def _two_heads(v, lane0):
    zero = jnp.zeros_like(v)
    return jnp.where(lane0, v, zero), jnp.where(lane0, zero, v)


SB_QBLK = 256
N_SB_STEPS = SEQ // SB_QBLK


SB_KCHUNK = 2 * BLK
SB_ROWS = 2 * SB_QBLK
SB_DEAD = -104.0


def _log_keep(z):
    neg_z = -z
    return jnp.minimum(neg_z, 0.0) - jnp.log(1.0 + jnp.exp(jnp.minimum(z, neg_z)))


def _stack_heads(v, lane0):
    return jnp.concatenate(_two_heads(v, lane0), axis=0)


def _block_sums(v, tri, split=True):
    rows = v.shape[0]
    halves = (v[:, :BLK], v[:, BLK:])
    stacked = jnp.concatenate(halves, axis=0)
    if split:
        hi, lo = _split_bf16(stacked)
        prod = _dot(jnp.concatenate([hi, lo], axis=0), tri)
        tri_sum = prod[:2 * rows] + prod[2 * rows:]
    else:
        tri_sum = _dot(stacked.astype(BF16), tri)
    sums = tuple(jnp.sum(h, axis=1, keepdims=True) for h in halves)
    return (tri_sum[:rows], tri_sum[rows:]), sums


SB_EARLY = 160


def _early_rows(v):
    return jnp.concatenate([v[:SB_EARLY], v[SB_QBLK:SB_QBLK + SB_EARLY]], axis=0)


def _with_early_rows(early, v):
    return jnp.concatenate([early[:SB_EARLY], v[SB_EARLY:SB_QBLK], early[SB_EARLY:], v[SB_QBLK + SB_EARLY:]], axis=0)


def _sb_diag_mask():
    row = lax.broadcasted_iota(jnp.int32, (SB_ROWS, SB_KCHUNK), 0)
    col = lax.broadcasted_iota(jnp.int32, (SB_ROWS, SB_KCHUNK), 1)
    return col < jnp.where(row >= SB_QBLK, row - SB_QBLK, row)


def _sb_fwd(qkv, send=None):
    def body(q_ref, k_ref, v_ref, o_ref):
        i = pl.program_id(2)
        krow = lax.broadcasted_iota(jnp.int32, (BLK, BLK), 0)
        kcol = lax.broadcasted_iota(jnp.int32, (BLK, BLK), 1)
        later = (krow > kcol).astype(BF16)
        lane0 = lax.broadcasted_iota(jnp.int32, (SB_QBLK, LANES), 1) < HEAD_DIM
        q2 = _stack_heads(q_ref[0] * QK_SCALE, lane0)

        def chunk(c, q_rows, carry, causal):
            acc, run = carry
            off = pl.multiple_of(c * SB_KCHUNK, SB_KCHUNK)
            z = _dot_nt(q_rows, k_ref[0, pl.ds(off, SB_KCHUNK), :])
            log_keep = _log_keep(z)
            if causal is not None:
                log_keep = jnp.where(causal, log_keep, 0.0)
            suffix, sums = _block_sums(log_keep, later)
            log_after = jnp.concatenate([suffix[0] + (run + sums[1]), suffix[1] + run], axis=1)
            a = jnp.exp(log_keep + z + log_after)
            if causal is not None:
                a = jnp.where(causal, a, 0.0)
            acc = acc + _dot(a.astype(BF16), v_ref[0, pl.ds(off, SB_KCHUNK), :])
            return acc, run + (sums[0] + sums[1])

        acc, run = chunk(i, q2, (jnp.zeros((SB_ROWS, LANES), F32), jnp.zeros((SB_ROWS, 1), F32)), _sb_diag_mask())

        def some_alive(run):
            return (jnp.max(run) > SB_DEAD).astype(jnp.int32)

        q2_early = _early_rows(q2)

        def trip(state):
            t, _, acc, run = state
            early_run = _early_rows(run)
            late_dead = jnp.max(_with_early_rows(jnp.full_like(early_run, NEG_BIG), run)) <= SB_DEAD

            def early_only(_):
                acc_e, run_e = chunk(i - 1 - t, q2_early, (_early_rows(acc), early_run), None)
                return _with_early_rows(acc_e, acc), _with_early_rows(run_e, run)

            acc, run = lax.cond(late_dead, early_only, lambda _: chunk(i - 1 - t, q2, (acc, run), None), None)
            return t + 1, some_alive(run), acc, run

        _, _, acc, _ = lax.while_loop(lambda s: jnp.logical_and(s[0] < i, s[1] > 0), trip,
                                      (jnp.int32(0), some_alive(run), acc, run))
        o_ref[0] = jnp.where(lane0, acc[:SB_QBLK], acc[SB_QBLK:]).astype(BF16)

    blk = pl.BlockSpec((1, SB_QBLK, LANES), lambda b, h, i: (b, i, h))
    return _call(
        body, send, name="sb_fwd", grid=(B_LOC, SB_PAIRS, N_SB_STEPS),
        in_specs=[blk,
                  pl.BlockSpec((1, SEQ, LANES), lambda b, h, i: (b, 0, SB_PAIRS + h)),
                  pl.BlockSpec((1, SEQ, LANES), lambda b, h, i: (b, 0, 2 * SB_PAIRS + h))],
        out_specs=[blk], out_shape=[jax.ShapeDtypeStruct((B_LOC, SEQ, SB_WIDTH), BF16)],
        scratch_shapes=[], semantics=("parallel", "parallel", "arbitrary"), operands=(qkv, qkv, qkv))


def _sb_bwd(qkv, d_o, send=None):
    def body(q_ref, k_ref, v_ref, do_ref, dq_ref, dk_ref, dv_ref, dk_acc, dv_acc, z_scr, keep_scr):
        i = pl.program_id(2)
        krow = lax.broadcasted_iota(jnp.int32, (BLK, BLK), 0)
        kcol = lax.broadcasted_iota(jnp.int32, (BLK, BLK), 1)
        upto = (krow <= kcol).astype(BF16)
        earlier = (krow < kcol).astype(BF16)
        lane0 = lax.broadcasted_iota(jnp.int32, (SB_QBLK, LANES), 1) < HEAD_DIM
        q2 = _stack_heads(q_ref[0] * QK_SCALE, lane0)
        do2 = _stack_heads(do_ref[0], lane0)

        def keep_sum(c, causal):
            off = pl.multiple_of(c * SB_KCHUNK, SB_KCHUNK)
            z = _dot_nt(q2, k_ref[0, pl.ds(off, SB_KCHUNK), :])
            log_keep = _log_keep(z)
            if causal is not None:
                log_keep = jnp.where(causal, log_keep, 0.0)
            z_scr[c] = z
            keep_scr[c] = log_keep
            return jnp.sum(log_keep, axis=1, keepdims=True)

        def some_alive(run):
            return (jnp.max(run) > SB_DEAD).astype(jnp.int32)

        def scan(state):
            t, _, run = state
            run = run + keep_sum(i - 1 - t, None)
            return t + 1, some_alive(run), run

        diag_sum = keep_sum(i, _sb_diag_mask())
        walked, _, tot2 = lax.while_loop(lambda s: jnp.logical_and(s[0] < i, s[1] > 0), scan,
                                         (jnp.int32(0), some_alive(diag_sum), diag_sum))
        first = i - walked

        @pl.when(i == 0)
        def _():
            dk_acc[...] = jnp.zeros_like(dk_acc)
            dv_acc[...] = jnp.zeros_like(dv_acc)

        def chunk(c, carry, causal):
            dq, pre_keep, pre_e = carry
            off = pl.multiple_of(c * SB_KCHUNK, SB_KCHUNK)
            k_c = k_ref[0, pl.ds(off, SB_KCHUNK), :]
            v_c = v_ref[0, pl.ds(off, SB_KCHUNK), :]
            d_a = _dot_nt(do2, v_c)
            log_keep = keep_scr[c]
            log_beta = log_keep + z_scr[c]
            prefix, sums = _block_sums(log_keep, upto)
            inclusive = jnp.concatenate([prefix[0], prefix[1] + sums[0]], axis=1)
            a = jnp.exp(log_beta + ((tot2 - pre_keep) - inclusive))
            if causal is not None:
                a = jnp.where(causal, a, 0.0)
            e = d_a * a
            e_prefix, e_sums = _block_sums(e, earlier, split=False)
            before = jnp.concatenate([e_prefix[0] + pre_e, e_prefix[1] + (pre_e + e_sums[0])], axis=1)
            dz = e - (e + before) * jnp.exp(log_beta)
            if causal is not None:
                dz = jnp.where(causal, dz, 0.0)
            dz = dz.astype(BF16)
            dq = dq + _dot(dz, k_c)
            dk_acc[pl.ds(off, SB_KCHUNK), :] += _dot_tn(dz, q2)
            dv_acc[pl.ds(off, SB_KCHUNK), :] += _dot_tn(a.astype(BF16), do2)
            return dq, pre_keep + (sums[0] + sums[1]), pre_e + (e_sums[0] + e_sums[1])

        zero_col = jnp.zeros((SB_ROWS, 1), F32)
        carry = lax.fori_loop(first, i, lambda t, c: chunk(t, c, None),
                              (jnp.zeros((SB_ROWS, LANES), F32), zero_col, zero_col))
        dq, _, _ = chunk(i, carry, _sb_diag_mask())
        dq_ref[0] = (jnp.where(lane0, dq[:SB_QBLK], dq[SB_QBLK:]) * QK_SCALE).astype(BF16)

        @pl.when(i == N_SB_STEPS - 1)
        def _():
            dk_ref[0] = dk_acc[...].astype(BF16)
            dv_ref[0] = dv_acc[...].astype(BF16)

    blk = pl.BlockSpec((1, SB_QBLK, LANES), lambda b, h, i: (b, i, h))
    whole = lambda c: pl.BlockSpec((1, SEQ, LANES), lambda b, h, i: (b, 0, c * SB_PAIRS + h))
    out = jax.ShapeDtypeStruct((B_LOC, SEQ, SB_WIDTH), BF16)
    return _call(
        body, send, name="sb_bwd", grid=(B_LOC, SB_PAIRS, N_SB_STEPS),
        in_specs=[blk, whole(1), whole(2), blk],
        out_specs=[blk, whole(0), whole(0)],
        out_shape=[out, out, out],
        scratch_shapes=[pltpu.VMEM((SEQ, LANES), F32), pltpu.VMEM((SEQ, LANES), F32),
                        pltpu.VMEM((N_SB_STEPS, SB_ROWS, SB_KCHUNK), F32),
                        pltpu.VMEM((N_SB_STEPS, SB_ROWS, SB_KCHUNK), F32)],
        semantics=("parallel", "parallel", "arbitrary"), operands=(qkv, qkv, qkv, d_o))


DIL_GROUPS = len(DIL_PAIRS)
DIL_QBLOCKS = SEQ // BLK


def _residue_rows(j, dilation):
    length = SEQ // dilation
    return pl.ds(j, length, stride=dilation) if dilation > 1 else pl.ds(0, length)


def _gather_residues(src_ref, dst_ref, dst_off, dilation, scale=None):
    length = SEQ // dilation
    for j in range(dilation):
        v = src_ref[_residue_rows(j, dilation), :]
        if scale is not None:
            v = v * scale
        dst_ref[dst_off + j * length:dst_off + (j + 1) * length, :] = v.astype(dst_ref.dtype)


def _scatter_residues(src_ref, src_off, dst_ref, dilation):
    length = SEQ // dilation
    for j in range(dilation):
        dst_ref[_residue_rows(j, dilation), :] = (
            src_ref[src_off + j * length:src_off + (j + 1) * length, :].astype(dst_ref.dtype))


def _dil_geometry(group, pair):
    dilation = DIL_PAIRS[group][1]
    row = lax.broadcasted_iota(jnp.int32, (2 * BLK, 2 * BLK), 0)
    col = lax.broadcasted_iota(jnp.int32, (2 * BLK, 2 * BLK), 1)
    second = row >= BLK
    steps = BLK + jnp.where(second, row - BLK, row) - col
    coef = -ALIBI_MAX_BIAS / DIL_HEADS * math.log(2.0)
    first_head = float(4 * group + 1) + 2.0 * pair.astype(F32)
    slope = jnp.exp(coef * (first_head + jnp.where(second, 1.0, 0.0)))
    bias = slope * (steps * dilation).astype(F32)
    valid = jnp.logical_and(steps >= 0, steps <= BLK)
    return bias, valid, col >= BLK


def _dil_tile_scores(q2, kk, geometry, has_prev):
    bias, valid, own = geometry
    ok = jnp.logical_and(valid, jnp.logical_or(own, has_prev))
    return jnp.where(ok, _dot_nt(q2, kk) - bias, NEG_BIG)


def _head_col(v, lane_mask):
    return jnp.max(jnp.where(lane_mask, v, NEG_BIG), axis=1, keepdims=True)


def _dil_fwd(qkv, send=None):
    def body(*refs):
        ins, (o_ref, lse_ref), (qs, ks, vs, o_res, lse_res) = refs[:9], refs[9:11], refs[11:16]
        o_grp, lse_grp = refs[16:19], refs[19:22]
        pair = pl.program_id(1)
        lane0 = lax.broadcasted_iota(jnp.int32, (BLK, LANES), 1) < HEAD_DIM
        ks[0:BLK, :] = jnp.zeros((BLK, LANES), BF16)
        vs[0:BLK, :] = jnp.zeros((BLK, LANES), BF16)
        for grp, (_, dilation) in enumerate(DIL_PAIRS):
            q_ref, k_ref, v_ref = ins[3 * grp:3 * grp + 3]
            per_residue = DIL_QBLOCKS // dilation
            _gather_residues(q_ref, qs, 0, dilation, QK_SCALE)
            _gather_residues(k_ref, ks, BLK, dilation)
            _gather_residues(v_ref, vs, BLK, dilation)
            geometry = _dil_geometry(grp, pair)

            def step(blk, _):
                off = pl.multiple_of(blk * BLK, BLK)
                q2 = _stack_heads(qs[pl.ds(off, BLK), :], lane0)
                s = _dil_tile_scores(q2, ks[pl.ds(off, 2 * BLK), :], geometry, blk % per_residue != 0)
                m = jnp.max(s, axis=1, keepdims=True)
                p = jnp.exp(s - m)
                den = jnp.sum(p, axis=1, keepdims=True)
                out = _dot(p.astype(BF16), vs[pl.ds(off, 2 * BLK), :]) / den
                lse = m + jnp.log(den)
                o_res[pl.ds(off, BLK), :] = jnp.where(lane0, out[:BLK], out[BLK:])
                lse_res[pl.ds(off, BLK), :] = jnp.where(lane0, lse[:BLK], lse[BLK:])
                return 0

            lax.fori_loop(0, DIL_QBLOCKS, step, 0, unroll=8)
            _scatter_residues(o_res, 0, o_grp[grp], dilation)
            _scatter_residues(lse_res, 0, lse_grp[grp], dilation)

        for r0 in range(0, SEQ, 2 * BLK):
            rows = slice(r0, r0 + 2 * BLK)
            ls = [lse_grp[g][rows, :] for g in range(DIL_GROUPS)]
            m = jnp.maximum(jnp.maximum(ls[0], ls[1]), ls[2])
            w = [jnp.exp(l - m) for l in ls]
            den = w[0] + w[1] + w[2]
            o_ref[rows, :] = (w[0] * o_grp[0][rows, :] + w[1] * o_grp[1][rows, :] + w[2] * o_grp[2][rows, :]) / den
            lse_ref[rows, :] = m + jnp.log(den)

    def col(part, grp):
        return pl.BlockSpec((None, SEQ, LANES), lambda b, p: (b, 0, 6 * part + 2 * grp + p))

    out_spec = pl.BlockSpec((None, SEQ, LANES), lambda b, p: (b, 0, p))
    out = jax.ShapeDtypeStruct((B_LOC, SEQ, DIL_OUT), F32)
    return _call(
        body, send, name="dil_fwd", grid=(B_LOC, DIL_OUT // LANES),
        in_specs=[col(part, grp) for grp in range(DIL_GROUPS) for part in range(3)],
        out_specs=[out_spec, out_spec], out_shape=[out, out],
        scratch_shapes=[pltpu.VMEM((SEQ, LANES), BF16), pltpu.VMEM((SEQ + BLK, LANES), BF16),
                        pltpu.VMEM((SEQ + BLK, LANES), BF16), pltpu.VMEM((SEQ, LANES), F32),
                        pltpu.VMEM((SEQ, LANES), F32)] + [pltpu.VMEM((SEQ, LANES), F32)] * (2 * DIL_GROUPS),
        semantics=("parallel", "parallel"), operands=[qkv] * 9)


def _dil_bwd(qkv, d_o, lse, dsum, send=None):
    def body(*refs):
        ins, (do_ref, lse_ref, dsum_ref), outs = refs[:9], refs[9:12], refs[12:21]
        qs, ks, vs, dos, lse_res, dsum_res, dq_res, dk_acc, dv_acc = refs[21:]
        pair = pl.program_id(1)
        lane0 = lax.broadcasted_iota(jnp.int32, (BLK, LANES), 1) < HEAD_DIM
        lane1 = jnp.logical_not(lane0)
        ks[0:BLK, :] = jnp.zeros((BLK, LANES), BF16)
        vs[0:BLK, :] = jnp.zeros((BLK, LANES), BF16)
        for grp, (_, dilation) in enumerate(DIL_PAIRS):
            q_ref, k_ref, v_ref = ins[3 * grp:3 * grp + 3]
            dq_ref, dk_ref, dv_ref = outs[3 * grp:3 * grp + 3]
            per_residue = DIL_QBLOCKS // dilation
            _gather_residues(q_ref, qs, 0, dilation, QK_SCALE)
            _gather_residues(k_ref, ks, BLK, dilation)
            _gather_residues(v_ref, vs, BLK, dilation)
            _gather_residues(do_ref, dos, 0, dilation)
            _gather_residues(lse_ref, lse_res, 0, dilation)
            _gather_residues(dsum_ref, dsum_res, 0, dilation)
            dk_acc[...] = jnp.zeros_like(dk_acc)
            dv_acc[...] = jnp.zeros_like(dv_acc)
            geometry = _dil_geometry(grp, pair)

            def step(blk, _):
                off = pl.multiple_of(blk * BLK, BLK)
                q2 = _stack_heads(qs[pl.ds(off, BLK), :], lane0)
                do2 = _stack_heads(dos[pl.ds(off, BLK), :], lane0)
                kk = ks[pl.ds(off, 2 * BLK), :]
                vv = vs[pl.ds(off, 2 * BLK), :]
                lse_blk = lse_res[pl.ds(off, BLK), :]
                dsum_blk = dsum_res[pl.ds(off, BLK), :]
                lse2 = jnp.concatenate([_head_col(lse_blk, lane0), _head_col(lse_blk, lane1)], axis=0)
                dsum2 = jnp.concatenate([_head_col(dsum_blk, lane0), _head_col(dsum_blk, lane1)], axis=0)
                s = _dil_tile_scores(q2, kk, geometry, blk % per_residue != 0)
                p = jnp.exp(s - lse2)
                ds = (p * (_dot_nt(do2, vv) - dsum2)).astype(BF16)
                dq2 = _dot(ds, kk)
                dq_res[pl.ds(off, BLK), :] = jnp.where(lane0, dq2[:BLK], dq2[BLK:]) * QK_SCALE
                dk_acc[pl.ds(off, 2 * BLK), :] += _dot_tn(ds, q2)
                dv_acc[pl.ds(off, 2 * BLK), :] += _dot_tn(p.astype(BF16), do2)
                return 0

            lax.fori_loop(0, DIL_QBLOCKS, step, 0, unroll=8)
            _scatter_residues(dq_res, 0, dq_ref, dilation)
            _scatter_residues(dk_acc, BLK, dk_ref, dilation)
            _scatter_residues(dv_acc, BLK, dv_ref, dilation)

    def col(part, grp):
        return pl.BlockSpec((None, SEQ, LANES), lambda b, p: (b, 0, 6 * part + 2 * grp + p))

    slot = pl.BlockSpec((None, SEQ, LANES), lambda b, p: (b, 0, p))
    out = jax.ShapeDtypeStruct((B_LOC, SEQ, DIL_OUT), F32)
    return _call(
        body, send, name="dil_bwd", grid=(B_LOC, DIL_OUT // LANES),
        in_specs=[col(part, grp) for grp in range(DIL_GROUPS) for part in range(3)] + [slot] * 3,
        out_specs=[slot] * 9, out_shape=[out] * 9,
        scratch_shapes=[pltpu.VMEM((SEQ, LANES), BF16), pltpu.VMEM((SEQ + BLK, LANES), BF16),
                        pltpu.VMEM((SEQ + BLK, LANES), BF16), pltpu.VMEM((SEQ, LANES), BF16),
                        pltpu.VMEM((SEQ, LANES), F32), pltpu.VMEM((SEQ, LANES), F32),
                        pltpu.VMEM((SEQ, LANES), F32), pltpu.VMEM((SEQ + BLK, LANES), F32),
                        pltpu.VMEM((SEQ + BLK, LANES), F32)],
        semantics=("parallel", "parallel"), operands=[qkv] * 9 + [d_o, lse, dsum])


def _peers():
    x, y, c = lax.axis_index("x"), lax.axis_index("y"), lax.axis_index("c")
    me = 4 * x + 2 * y + c
    peers = []
    for mask in range(1, N_DEV):
        px = 1 - x if mask & 4 else x
        py = 1 - y if mask & 2 else y
        pc = 1 - c if mask & 1 else c
        peers.append(((px, py, pc), 4 * px + 2 * py + pc))
    return me, peers


def _all_gather(shard, name):
    def body(src_ref, out_ref, send_sems, recv_sems, local_sem):
        x, y, c = lax.axis_index("x"), lax.axis_index("y"), lax.axis_index("c")
        sibling = (x, y, 1 - c)
        chips = [(1 - x, y), (x, 1 - y), (1 - x, 1 - y)]

        def slot(px, py, pc):
            return out_ref.at[4 * px + 2 * py + pc]

        def copy(k, block, to, src=None):
            return pltpu.make_async_remote_copy(
                src_ref=slot(*block) if src is None else src, dst_ref=slot(*block),
                send_sem=send_sems.at[k], recv_sem=recv_sems.at[k], device_id=to,
                device_id_type=pl.DeviceIdType.MESH)

        mine = pltpu.make_async_copy(src_ref, slot(x, y, c), local_sem)
        mine.start()
        first = [copy(0, (x, y, c), sibling, src=src_ref)]
        first += [copy(1 + j, (x, y, c), (*chip, c), src=src_ref) for j, chip in enumerate(chips)]
        for cp in first:
            cp.start()
        passed = [copy(4 + j, (*chip, c), sibling) for j, chip in enumerate(chips)]
        for j, chip in enumerate(chips):
            copy(1 + j, (*chip, c), (x, y, c)).wait_recv()
            passed[j].start()
        copy(0, sibling, (x, y, c)).wait_recv()
        for j, chip in enumerate(chips):
            copy(4 + j, (*chip, 1 - c), (x, y, c)).wait_recv()
        for cp in first + passed:
            cp.wait_send()
        mine.wait()

    return pl.pallas_call(
        body, name=name,
        in_specs=[pl.BlockSpec(memory_space=pl.ANY)],
        out_specs=pl.BlockSpec(memory_space=pl.ANY),
        out_shape=jax.ShapeDtypeStruct((N_DEV,) + shard.shape, shard.dtype),
        scratch_shapes=[pltpu.SemaphoreType.DMA((N_DEV - 1,)), pltpu.SemaphoreType.DMA((N_DEV - 1,)),
                        pltpu.SemaphoreType.DMA],
    )(shard)


def _call(body, send, *, name, grid, in_specs, out_specs, out_shape, scratch_shapes, semantics, operands):
    if send is None:
        return pl.pallas_call(
            body, name=name, grid=grid, in_specs=in_specs, out_specs=out_specs, out_shape=out_shape,
            scratch_shapes=scratch_shapes, compiler_params=_params(semantics))(*operands), []
    srcs, kinds = [s for s, _ in send], [k for _, k in send]
    n, n_in, n_out, n_scr = len(srcs), len(in_specs), len(out_specs), len(scratch_shapes)
    steps = math.prod(grid)
    relay_step = (13 * steps) // 16

    def plan(refs):
        src_refs, land_refs = refs[n_in:n_in + n], refs[n_in + n + n_out:n_in + 2 * n + n_out]
        send_sems, recv_sems, local_sems = refs[-3:]
        x, y, c = lax.axis_index("x"), lax.axis_index("y"), lax.axis_index("c")
        me, peers = _peers()
        first, relayed_in, relayed_out, arrivals, sends, own = [], [], [], [], [], []
        for a, kind in enumerate(kinds):
            def copy(k, src, dst_slot, to):
                return pltpu.make_async_remote_copy(
                    src_ref=src, dst_ref=land_refs[a].at[dst_slot], send_sem=send_sems.at[a * (N_DEV - 1) + k],
                    recv_sem=recv_sems.at[a * (N_DEV - 1) + k], device_id=to, device_id_type=pl.DeviceIdType.MESH)

            if kind == "gather_by_chip":
                idx = lambda px, py, pc: 4 * px + 2 * py + pc
                chips = [(1 - x, y), (x, 1 - y), (1 - x, 1 - y)]
                mine = [copy(0, src_refs[a], me, (x, y, 1 - c))]
                arrivals.append(copy(0, src_refs[a], idx(x, y, 1 - c), (x, y, 1 - c)))
                for j, (px, py) in enumerate(chips):
                    mine.append(copy(1 + j, src_refs[a], me, (px, py, c)))
                    relayed_in.append(copy(1 + j, src_refs[a], idx(px, py, c), (px, py, c)))
                    relayed_out.append(copy(4 + j, land_refs[a].at[idx(px, py, c)], idx(px, py, c), (x, y, 1 - c)))
                    arrivals.append(copy(4 + j, src_refs[a], idx(px, py, 1 - c), (x, y, 1 - c)))
                first += mine
                sends += mine + relayed_out[-3:]
                own.append(pltpu.make_async_copy(src_refs[a], land_refs[a].at[me], local_sems.at[a]))
            elif kind == "scatter_by_chip":
                for k, (px, py) in enumerate([(1 - x, y), (x, 1 - y), (1 - x, 1 - y)]):
                    first.append(copy(k, src_refs[a].at[2 * px + py], 2 * x + y, (px, py, c)))
                    arrivals.append(copy(k, src_refs[a].at[2 * px + py], 2 * px + py, (px, py, c)))
                sends += first[-3:]
                own.append(pltpu.make_async_copy(src_refs[a].at[2 * x + y], land_refs[a].at[2 * x + y],
                                                 local_sems.at[a]))
            else:
                part = (lambda i: src_refs[a].at[i]) if kind == "scatter" else (lambda i: src_refs[a])
                for k, (peer, peer_idx) in enumerate(peers):
                    first.append(copy(k, part(peer_idx), me, peer))
                    arrivals.append(copy(k, part(peer_idx), peer_idx, peer))
                sends += first[-(N_DEV - 1):]
                own.append(pltpu.make_async_copy(part(me), land_refs[a].at[me], local_sems.at[a]))
        return first, relayed_in, relayed_out, arrivals, sends, own

    def wrapped(*refs):
        step = 0
        for axis, size in enumerate(grid):
            step = step * size + pl.program_id(axis)

        @pl.when(step == 0)
        def _():
            first, _, _, _, _, own = plan(refs)
            for cp in first + own:
                cp.start()

        if "gather_by_chip" in kinds:
            @pl.when(step == relay_step)
            def _():
                _, relayed_in, relayed_out, _, _, _ = plan(refs)
                for cp_in, cp_out in zip(relayed_in, relayed_out):
                    cp_in.wait_recv()
                    cp_out.start()

        body(*refs[:n_in], *refs[n_in + n:n_in + n + n_out], *refs[n_in + 2 * n + n_out:n_in + 2 * n + n_out + n_scr])

        @pl.when(step == steps - 1)
        def _():
            _, _, _, arrivals, sends, own = plan(refs)
            for cp in arrivals:
                cp.wait_recv()
            for cp in sends:
                cp.wait_send()
            for cp in own:
                cp.wait()

    anywhere = pl.BlockSpec(memory_space=pl.ANY)
    lands = [jax.ShapeDtypeStruct((N_DEV // 2 if k == "scatter_by_chip" else N_DEV,) + s.shape[-2:], s.dtype)
             for s, k in send]
    out = pl.pallas_call(
        wrapped, name=name, grid=grid,
        in_specs=list(in_specs) + [anywhere] * n, out_specs=list(out_specs) + [anywhere] * n,
        out_shape=list(out_shape) + lands,
        scratch_shapes=list(scratch_shapes) + [pltpu.SemaphoreType.DMA((n * (N_DEV - 1),)),
                                               pltpu.SemaphoreType.DMA((n * (N_DEV - 1),)),
                                               pltpu.SemaphoreType.DMA((n,))],
        compiler_params=_params(("arbitrary",) * len(grid)),
    )(*operands, *srcs)
    return out[:n_out], list(out[n_out:])


def _pair_swap(blocks):
    def body(src_ref, out_ref, send_sems, recv_sems):
        x, y, c = lax.axis_index("x"), lax.axis_index("y"), lax.axis_index("c")
        copies = [pltpu.make_async_remote_copy(
            src_ref=src_ref.at[2 * chip + (1 - c)], dst_ref=out_ref.at[chip], send_sem=send_sems.at[chip],
            recv_sem=recv_sems.at[chip], device_id=(x, y, 1 - c), device_id_type=pl.DeviceIdType.MESH)
            for chip in range(N_DEV // 2)]
        for cp in copies:
            cp.start()
        for cp in copies:
            cp.wait()

    return pl.pallas_call(
        body, name="pair_swap_grad_w_in",
        in_specs=[pl.BlockSpec(memory_space=pl.ANY)], out_specs=pl.BlockSpec(memory_space=pl.ANY),
        out_shape=jax.ShapeDtypeStruct((N_DEV // 2,) + blocks.shape[1:], blocks.dtype),
        scratch_shapes=[pltpu.SemaphoreType.DMA((N_DEV // 2,)), pltpu.SemaphoreType.DMA((N_DEV // 2,))],
    )(blocks)


def _pair_sum(blocks, swapped, core):
    _, rows, cols = swapped.shape
    tile_rows = _row_tile(rows)

    def body(core_ref, mine_ref, theirs_ref, o_ref):
        o_ref[...] = (mine_ref[...].astype(F32) + theirs_ref[...].astype(F32)).astype(o_ref.dtype)

    return pl.pallas_call(
        body, name="pair_sum_grad_w_in",
        grid_spec=pltpu.PrefetchScalarGridSpec(
            num_scalar_prefetch=1, grid=(N_DEV // 2, rows // tile_rows),
            in_specs=[pl.BlockSpec((None, tile_rows, cols), lambda j, i, core_ref: (2 * j + core_ref[0], i, 0)),
                      pl.BlockSpec((None, tile_rows, cols), lambda j, i, core_ref: (j, i, 0))],
            out_specs=pl.BlockSpec((None, tile_rows, cols), lambda j, i, core_ref: (j, i, 0))),
        out_shape=jax.ShapeDtypeStruct(swapped.shape, swapped.dtype),
        compiler_params=_params(("parallel", "parallel")),
    )(core, blocks, swapped)


def _sum_in_device_order(land_ref):
    acc = land_ref[0].astype(F32)
    for j in range(1, land_ref.shape[0]):
        acc = acc + land_ref[j].astype(F32)
    return acc


def _adam_math(w, g, m, v):
    c1 = 1.0 - ADAM_B1 ** ADAM_STEP
    c2 = 1.0 - ADAM_B2 ** ADAM_STEP
    m_new = ADAM_B1 * m + (1.0 - ADAM_B1) * g
    v_new = ADAM_B2 * v + (1.0 - ADAM_B2) * (g * g)
    delta = -ADAM_LR * ((m_new / c1) / (jnp.sqrt(v_new / c2) + ADAM_EPS) + ADAM_WD * w)
    return delta, m_new, v_new


def _row_tile(rows):
    return max(t for t in range(8, 257, 8) if rows % t == 0) if rows % 8 == 0 else rows


def _sum_update(land, w, m, v, name):
    slots, rows, cols = land.shape
    tile_rows = _row_tile(rows)

    def body(land_ref, w_ref, m_ref, v_ref, g_ref, d_ref, nm_ref, nv_ref):
        g = _sum_in_device_order(land_ref)
        g_ref[...] = g
        d_ref[...], nm_ref[...], nv_ref[...] = _adam_math(w_ref[...], g, m_ref[...], v_ref[...])

    tile = pl.BlockSpec((None, tile_rows, cols), lambda i: (0, i, 0))
    out = jax.ShapeDtypeStruct((1, rows, cols), F32)
    return pl.pallas_call(
        body, name=name, grid=(rows // tile_rows,),
        in_specs=[pl.BlockSpec((slots, tile_rows, cols), lambda i: (0, i, 0)), tile, tile, tile],
        out_specs=[tile] * 4, out_shape=[out] * 4,
        compiler_params=_params(("parallel",)),
    )(land, w, m, v)


def _sum_gains(land):
    def body(land_ref, o_ref):
        o_ref[...] = _sum_in_device_order(land_ref)

    return pl.pallas_call(
        body, name="sum_gain_grads", grid=(1,),
        in_specs=[pl.BlockSpec(land.shape, lambda i: (0, 0, 0))],
        out_specs=pl.BlockSpec(land.shape[1:], lambda i: (0, 0)),
        out_shape=jax.ShapeDtypeStruct(land.shape[1:], F32),
    )(land)


def _adamw(w, g, m, v, name):
    def body(w_ref, g_ref, m_ref, v_ref, d_ref, nm_ref, nv_ref):
        d_ref[...], nm_ref[...], nv_ref[...] = _adam_math(w_ref[...], g_ref[...], m_ref[...], v_ref[...])

    whole = pl.BlockSpec(w.shape, lambda i: (0, 0))
    out = jax.ShapeDtypeStruct(w.shape, F32)
    return pl.pallas_call(
        body, name=name, grid=(1,),
        in_specs=[whole] * 4, out_specs=[whole] * 3, out_shape=[out] * 3,
    )(w, g, m, v)


GROUP_FFN = ("w_ffn_in", "w_ffn_out")
GROUP_MIX = ("w_sb_up", "w_dil_up", "w_out")
COL_SHARDED = ("w_in", "w_sb_up", "w_dil_up", "w_ffn_in")


def _full_from_shards(name, slots):
    _, r, c = slots.shape
    if name in COL_SHARDED:
        return slots.transpose(1, 0, 2).reshape(r, N_DEV * c)
    return slots.reshape(N_DEV * r, c)


def _shards_from_full(name, full):
    rows, cols = full.shape
    if name in COL_SHARDED:
        return full.reshape(rows, N_DEV, cols // N_DEV).transpose(1, 0, 2)
    return full.reshape(N_DEV, rows // N_DEV, cols)


def _local_step(x, target, g_mix, g_ffn, g_fin, w_in, shards=None, rest=None):
    gather = lambda names, kind: None if shards is None else [(shards[n], kind) for n in names]
    scatter = lambda blocks: None if shards is None else [(t, "scatter") for t in blocks]
    landed = lambda blocks, lands: lands if lands else blocks

    w = {"w_in": w_in}
    if shards is None:
        w.update(rest)
        w["w_ffn_in"] = _shards_from_full("w_ffn_in", rest["w_ffn_in"])
    (qkv_sb, qkv_dl, gates, u), lands = _norm_proj(x, g_mix, w["w_in"], gather(("w_ffn_out",), "gather"))
    if lands:
        w["w_ffn_out"] = _full_from_shards("w_ffn_out", lands[0])
    qkv_sb = qkv_sb.reshape(B_LOC, SEQ, 3 * SB_WIDTH)
    qkv_dl = qkv_dl.reshape(B_LOC, SEQ, 3 * DIL_WIDTH)
    (o_sb,), lands = _sb_fwd(qkv_sb, gather(("w_ffn_in",), "gather_by_chip"))
    if lands:
        w["w_ffn_in"] = lands[0]
    o_sb = o_sb.reshape(TOK, SB_WIDTH)
    (o_dl, lse), lands = _dil_fwd(qkv_dl, gather(GROUP_MIX, "gather"))
    w.update({n: _full_from_shards(n, t) for n, t in zip(GROUP_MIX, lands)})
    o_dl = o_dl.reshape(TOK, DIL_OUT)

    loss, dx1, merged, u2, act, dh, dx2, dg_fin, dg_ffn = _mix_ffn_fwd_bwd(
        x, o_sb, o_dl, gates, w["w_sb_up"], w["w_dil_up"], w["w_out"], target, g_ffn, g_fin,
        w["w_ffn_in"], w["w_ffn_out"])
    dgates, dy_sb, dy_dl, do_sb, do_dl, dsum = _mix_bwd(dx1, o_sb, o_dl, gates, w["w_sb_up"], w["w_dil_up"], w["w_out"])
    blocks = {
        "w_sb_up": _atb(o_sb, dy_sb, "grad_w_sb_up", SB_WIDTH, D_MODEL, col_blocks=N_DEV),
        "w_dil_up": _atb(o_dl, dy_dl, "grad_w_dil_up", DIL_OUT, D_MODEL, col_blocks=N_DEV),
        "w_out": _shards_from_full("w_out", _atb(merged, dx1, "grad_w_out", D_MODEL, D_MODEL)),
        "w_ffn_in": _atb_shards(u2, dh, "grad_w_ffn_in"),
        "w_ffn_out": _shards_from_full("w_ffn_out", _atb_shards(act, dx2, "grad_w_ffn_out")),
    }
    grads = {}

    early, late = GROUP_FFN, GROUP_MIX
    early_blocks = [blocks[n] for n in early]
    (dq_sb, dk_sb, dv_sb), lands = _sb_bwd(qkv_sb, do_sb.reshape(B_LOC, SEQ, SB_WIDTH), scatter(early_blocks))
    grads.update(zip(early, landed(early_blocks, lands)))
    as_batch = lambda t: t.reshape(B_LOC, SEQ, DIL_OUT)
    late_blocks = [blocks[n] for n in late]
    d_dl, lands = _dil_bwd(qkv_dl, as_batch(do_dl), lse, as_batch(dsum), scatter(late_blocks))
    grads.update(zip(late, landed(late_blocks, lands)))
    flat = lambda t: t.reshape(TOK, -1)
    dproj = ([flat(dq_sb), flat(dk_sb), flat(dv_sb)]
             + [flat(d_dl[3 * grp + part]) for part in range(3) for grp in range(DIL_GROUPS)] + [dgates])

    w_in_blocks = _shards_from_full("w_in", _atb_pieces(u, dproj, "grad_w_in", D_MODEL // 2))
    if shards is None:
        grads["w_in"] = w_in_blocks
        send = None
    else:
        core = lax.axis_index("c").astype(jnp.int32).reshape(1)
        send = [(_pair_sum(w_in_blocks, _pair_swap(w_in_blocks), core), "scatter_by_chip")]
    (grad_x, dg_mix), lands = _proj_bwd(dproj, dx1, x, g_mix, w["w_in"], send)
    if lands:
        grads["w_in"] = lands[0]
    gain_grads = jnp.concatenate([dg_mix, dg_ffn, dg_fin], axis=0)
    return loss, grad_x, gain_grads, grads


def kernel(x, norm_mix_g, w_in, w_sb_up, w_dil_up, w_out, norm_ffn_g, w_ffn_in, w_ffn_out, norm_final_g, loss_target, m_norm_mix_g, m_w_in, m_w_sb_up, m_w_dil_up, m_w_out, m_norm_ffn_g, m_w_ffn_in, m_w_ffn_out, m_norm_final_g, v_norm_mix_g, v_w_in, v_w_sb_up, v_w_dil_up, v_w_out, v_norm_ffn_g, v_w_ffn_in, v_w_ffn_out, v_norm_final_g):
    mats = {"w_in": w_in, "w_sb_up": w_sb_up, "w_dil_up": w_dil_up, "w_out": w_out,
            "w_ffn_in": w_ffn_in, "w_ffn_out": w_ffn_out}
    moments_m = {"w_in": m_w_in, "w_sb_up": m_w_sb_up, "w_dil_up": m_w_dil_up, "w_out": m_w_out,
                 "w_ffn_in": m_w_ffn_in, "w_ffn_out": m_w_ffn_out}
    moments_v = {"w_in": v_w_in, "w_sb_up": v_w_sb_up, "w_dil_up": v_w_dil_up, "w_out": v_w_out,
                 "w_ffn_in": v_w_ffn_in, "w_ffn_out": v_w_ffn_out}
    gathered_w_in = _all_gather(w_in[0].astype(BF16), "all_gather_w_in")
    g_fin = norm_final_g.reshape(1, D_MODEL)
    loss, grad_x, gain_grads, grad_slots = _local_step(
        x.reshape(TOK, D_MODEL), loss_target.reshape(TOK, D_MODEL), norm_mix_g, norm_ffn_g, g_fin,
        _full_from_shards("w_in", gathered_w_in),
        shards={name: mats[name][0].astype(BF16) for name in GROUP_FFN + GROUP_MIX})

    gain_rows = jnp.concatenate([gain_grads, jnp.tile(loss, (1, D_MODEL // LANES)),
                                 jnp.zeros((8 - 4, D_MODEL), F32)], axis=0)
    g_gains = _sum_gains(_all_gather(gain_rows, "all_gather_gains"))

    out_g, out_d, out_m, out_v = {}, {}, {}, {}
    for name, slots in grad_slots.items():
        out_g[name], out_d[name], out_m[name], out_v[name] = _sum_update(
            slots, mats[name], moments_m[name], moments_v[name], "update_" + name)

    gain_w = jnp.concatenate([norm_mix_g, norm_ffn_g, g_fin], axis=0)
    gain_m = jnp.concatenate([m_norm_mix_g, m_norm_ffn_g, m_norm_final_g.reshape(1, D_MODEL)], axis=0)
    gain_v = jnp.concatenate([v_norm_mix_g, v_norm_ffn_g, v_norm_final_g.reshape(1, D_MODEL)], axis=0)
    gd, gm, gv = _adamw(gain_w, g_gains[:3], gain_m, gain_v, "adamw_gains")
    for idx, name in enumerate(("norm_mix_g", "norm_ffn_g", "norm_final_g")):
        shape = (D_MODEL,) if name == "norm_final_g" else (1, D_MODEL)
        out_g[name] = g_gains[idx].reshape(shape)
        out_d[name], out_m[name], out_v[name] = gd[idx].reshape(shape), gm[idx].reshape(shape), gv[idx].reshape(shape)

    order = ("norm_mix_g", "w_in", "w_sb_up", "w_dil_up", "w_out", "norm_ffn_g", "w_ffn_in", "w_ffn_out",
             "norm_final_g")
    return (g_gains[3, 0], grad_x.reshape(B_LOC, SEQ, D_MODEL),
            *[out_g[n] for n in order], *[out_d[n] for n in order],
            *[out_m[n] for n in order], *[out_v[n] for n in order])
```

```python
import math

import jax
import jax.numpy as jnp
from jax import lax
from jax.experimental import pallas as pl
from jax.experimental.pallas import tpu as pltpu

F32 = jnp.float32
BF16 = jnp.bfloat16

N_DEV = 8
D_MODEL = 1024
SEQ = 2048
B_LOC = 2
TOK = B_LOC * SEQ
HEAD_DIM = 64
SB_WIDTH = 512
DIL_WIDTH = 768
DIL_OUT = 256
QKV_WIDTH = 3 * SB_WIDTH + 3 * DIL_WIDTH
IN_WIDTH = QKV_WIDTH + 2 * D_MODEL
D_FF = 2816
DIL_PAIRS = ((128, 1), (512, 4), (2048, 16))
DIL_HEADS = 12
RMS_EPS = 1e-6
ALIBI_MAX_BIAS = 8.0
QK_SCALE = 1.0 / math.sqrt(HEAD_DIM)
BLK = 128
LANES = 128
NEG_BIG = -1e30

ADAM_LR = 0.001
ADAM_B1 = 0.9
ADAM_B2 = 0.999
ADAM_EPS = 1e-08
ADAM_WD = 0.01
ADAM_STEP = 10

VMEM_LIMIT = 56 * 1024 * 1024


def _dot(a, b):
    return jnp.dot(a, b, preferred_element_type=F32)


def _dot_nt(a, b):
    return lax.dot_general(a, b, (((1,), (1,)), ((), ())), preferred_element_type=F32)


def _dot_tn(a, b):
    return lax.dot_general(a, b, (((0,), (0,)), ((), ())), preferred_element_type=F32)


def _sigmoid(z):
    return 1.0 / (1.0 + jnp.exp(-z))


def _split_bf16(v):
    hi = v.astype(BF16)
    lo = (v - hi.astype(F32)).astype(BF16)
    return hi, lo


def _chunks(width, step=512):
    out, c = [], 0
    while c < width:
        w = min(step, width - c)
        out.append((c, w))
        c += w
    return out


def _resident(shape):
    nd = len(shape)
    return pl.BlockSpec(shape, lambda *_: (0,) * nd, pipeline_mode=pl.Buffered(1))


def _params(sem):
    return pltpu.CompilerParams(dimension_semantics=sem, vmem_limit_bytes=VMEM_LIMIT)


def _rms_fwd(x, g):
    r = lax.rsqrt(jnp.mean(x * x, axis=-1, keepdims=True) + RMS_EPS)
    n = x * r
    return n, r, n * g


def _rms_bwd(dy, n, r, g):
    dg = jnp.sum(dy * n, axis=0, keepdims=True)
    dn = dy * g
    dx = r * (dn - n * jnp.mean(dn * n, axis=-1, keepdims=True))
    return dx, dg


TM = 256


def _norm_proj(x, g, w_in_t, send=None):
    def body(x_ref, g_ref, w_ref, sb_ref, dl_ref, gate_ref, u_ref):
        _, _, u = _rms_fwd(x_ref[...], g_ref[...])
        u = u.astype(BF16)
        u_ref[...] = u
        for c0, w in _chunks(3 * SB_WIDTH):
            sb_ref[:, c0:c0 + w] = _dot_nt(u, w_ref[c0:c0 + w, :]).astype(BF16)
        for c0, w in _chunks(3 * DIL_WIDTH):
            dl_ref[:, c0:c0 + w] = _dot_nt(u, w_ref[3 * SB_WIDTH + c0:3 * SB_WIDTH + c0 + w, :])
        for c0, w in _chunks(2 * D_MODEL):
            gate_ref[:, c0:c0 + w] = _dot_nt(u, w_ref[QKV_WIDTH + c0:QKV_WIDTH + c0 + w, :])

    return _call(
        body, send, name="norm_proj", grid=(TOK // TM,),
        in_specs=[pl.BlockSpec((TM, D_MODEL), lambda i: (i, 0)), _resident((1, D_MODEL)),
                  _resident((IN_WIDTH, D_MODEL))],
        out_specs=[pl.BlockSpec((TM, 3 * SB_WIDTH), lambda i: (i, 0)),
                   pl.BlockSpec((TM, 3 * DIL_WIDTH), lambda i: (i, 0)),
                   pl.BlockSpec((TM, 2 * D_MODEL), lambda i: (i, 0)),
                   pl.BlockSpec((TM, D_MODEL), lambda i: (i, 0))],
        out_shape=[jax.ShapeDtypeStruct((TOK, 3 * SB_WIDTH), BF16),
                   jax.ShapeDtypeStruct((TOK, 3 * DIL_WIDTH), F32),
                   jax.ShapeDtypeStruct((TOK, 2 * D_MODEL), F32),
                   jax.ShapeDtypeStruct((TOK, D_MODEL), BF16)],
        scratch_shapes=[], semantics=("parallel",), operands=(x, g, w_in_t))


FF_SHARD = 2 * D_FF // N_DEV
FF_PAIRS = N_DEV // 2


def _mix_ffn_fwd_bwd(x, o_sb, o_dl, gates, w_sb_up, w_dil_up, w_out, target, g_ffn, g_fin, w_ffn_in, w_ffn_out):
    def body(x_ref, osb_ref, odl_ref, gate_ref, wsb_ref, wdl_ref, wo_ref, t_ref, gffn_ref, gfin_ref, win_ref, wout_ref,
             loss_ref, dx1_ref, mg_ref, u2_ref, act_ref, dh_ref, dx2_ref, dgfin_ref, dgffn_ref, h_scr):
        i = pl.program_id(0)

        @pl.when(i == 0)
        def _():
            loss_ref[...] = jnp.zeros_like(loss_ref)
            dgfin_ref[...] = jnp.zeros_like(dgfin_ref)
            dgffn_ref[...] = jnp.zeros_like(dgffn_ref)

        y_sb = _dot(osb_ref[...], wsb_ref[...])
        y_dl = _dot(odl_ref[...].astype(BF16), wdl_ref[...])
        merged = (_sigmoid(gate_ref[:, :D_MODEL]) * y_sb
                  + _sigmoid(gate_ref[:, D_MODEL:]) * y_dl).astype(BF16)
        mg_ref[...] = merged
        x1 = x_ref[...] + _dot(merged, wo_ref[...])
        g_ffn_v = gffn_ref[...]
        g_fin_v = gfin_ref[...]
        n2, r2, u2 = _rms_fwd(x1, g_ffn_v)
        u2 = u2.astype(BF16)
        u2_ref[...] = u2
        x2 = x1
        for r in range(FF_PAIRS):
            gate = _dot_nt(u2, win_ref[r])
            up = _dot_nt(u2, win_ref[r + FF_PAIRS])
            h_scr[r] = gate
            h_scr[r + FF_PAIRS] = up
            act = (gate * _sigmoid(gate) * up).astype(BF16)
            act_ref[r] = act
            x2 = x2 + _dot(act, wout_ref[r * FF_SHARD:(r + 1) * FF_SHARD, :])
        n3, r3, y = _rms_fwd(x2, g_fin_v)
        err = y - t_ref[...]
        sq = jnp.sum(jnp.sum(err * err, axis=1, keepdims=True), axis=0, keepdims=True)
        loss_ref[...] += sq * (0.5 / D_MODEL)
        dx2, dgfin = _rms_bwd(err * (1.0 / D_MODEL), n3, r3, g_fin_v)
        dgfin_ref[...] += dgfin
        dx2_b = dx2.astype(BF16)
        dx2_ref[...] = dx2_b
        du2 = jnp.zeros((TM, D_MODEL), F32)
        for r in range(FF_PAIRS):
            gate = h_scr[r]
            up = h_scr[r + FF_PAIRS]
            dact = _dot_nt(dx2_b, wout_ref[r * FF_SHARD:(r + 1) * FF_SHARD, :])
            sg = _sigmoid(gate)
            dgate = (dact * up * (sg * (1.0 + gate * (1.0 - sg)))).astype(BF16)
            dup = (dact * (gate * sg)).astype(BF16)
            dh_ref[r] = dgate
            dh_ref[r + FF_PAIRS] = dup
            du2 = du2 + _dot(dgate, win_ref[r])
            du2 = du2 + _dot(dup, win_ref[r + FF_PAIRS])
        dx1_n, dgffn = _rms_bwd(du2, n2, r2, g_ffn_v)
        dgffn_ref[...] += dgffn
        dx1_ref[...] = dx2 + dx1_n

    tile = lambda w: pl.BlockSpec((TM, w), lambda i: (i, 0))
    shards = lambda n: pl.BlockSpec((n, TM, FF_SHARD), lambda i: (0, i, 0))
    acc = lambda w: pl.BlockSpec((1, w), lambda i: (0, 0))
    return pl.pallas_call(
        body, name="mix_ffn_fwd_bwd", grid=(TOK // TM,),
        in_specs=[tile(D_MODEL), tile(SB_WIDTH), tile(DIL_OUT), tile(2 * D_MODEL),
                  _resident((SB_WIDTH, D_MODEL)), _resident((DIL_OUT, D_MODEL)), _resident((D_MODEL, D_MODEL)),
                  tile(D_MODEL), _resident((1, D_MODEL)), _resident((1, D_MODEL)),
                  _resident((N_DEV, FF_SHARD, D_MODEL)), _resident((D_FF, D_MODEL))],
        out_specs=[acc(LANES), tile(D_MODEL), tile(D_MODEL), tile(D_MODEL), shards(FF_PAIRS), shards(N_DEV),
                   tile(D_MODEL), acc(D_MODEL), acc(D_MODEL)],
        out_shape=[jax.ShapeDtypeStruct((1, LANES), F32),
                   jax.ShapeDtypeStruct((TOK, D_MODEL), F32),
                   jax.ShapeDtypeStruct((TOK, D_MODEL), BF16),
                   jax.ShapeDtypeStruct((TOK, D_MODEL), BF16),
                   jax.ShapeDtypeStruct((FF_PAIRS, TOK, FF_SHARD), BF16),
                   jax.ShapeDtypeStruct((N_DEV, TOK, FF_SHARD), BF16),
                   jax.ShapeDtypeStruct((TOK, D_MODEL), BF16),
                   jax.ShapeDtypeStruct((1, D_MODEL), F32),
                   jax.ShapeDtypeStruct((1, D_MODEL), F32)],
        scratch_shapes=[pltpu.VMEM((N_DEV, TM, FF_SHARD), F32)],
        compiler_params=_params(("arbitrary",)),
    )(x, o_sb, o_dl, gates, w_sb_up, w_dil_up, w_out, target, g_ffn, g_fin, w_ffn_in, w_ffn_out)


def _mix_bwd(dx1, o_sb, o_dl, gates, w_sb_up, w_dil_up, w_out):
    def body(dx1_ref, osb_ref, odl_ref, gate_ref, wsb_ref, wdl_ref, wout_ref,
             dgate_ref, dysb_ref, dydl_ref, dosb_ref, dodl_ref, dsum_ref):
        dmerged = _dot_nt(dx1_ref[...].astype(BF16), wout_ref[...])
        o_dl = odl_ref[...]
        y_sb = _dot(osb_ref[...], wsb_ref[...])
        y_dl = _dot(o_dl.astype(BF16), wdl_ref[...])
        s_sb = _sigmoid(gate_ref[:, :D_MODEL])
        s_dl = _sigmoid(gate_ref[:, D_MODEL:])
        dgate_ref[:, :D_MODEL] = (dmerged * y_sb * (s_sb * (1.0 - s_sb))).astype(BF16)
        dgate_ref[:, D_MODEL:] = (dmerged * y_dl * (s_dl * (1.0 - s_dl))).astype(BF16)
        dy_sb = (dmerged * s_sb).astype(BF16)
        dy_dl = (dmerged * s_dl).astype(BF16)
        dysb_ref[...] = dy_sb
        dydl_ref[...] = dy_dl
        dosb_ref[...] = _dot_nt(dy_sb, wsb_ref[...]).astype(BF16)
        do_dl = _dot_nt(dy_dl, wdl_ref[...])
        dodl_ref[...] = do_dl
        row = lax.broadcasted_iota(jnp.int32, (DIL_OUT, DIL_OUT), 0) // HEAD_DIM
        col = lax.broadcasted_iota(jnp.int32, (DIL_OUT, DIL_OUT), 1) // HEAD_DIM
        same_head = (row == col).astype(BF16)
        hi, lo = _split_bf16(do_dl * o_dl)
        dsum_ref[...] = _dot(hi, same_head) + _dot(lo, same_head)

    tile = lambda w: pl.BlockSpec((TM, w), lambda i: (i, 0))
    return pl.pallas_call(
        body, name="mix_bwd", grid=(TOK // TM,),
        in_specs=[tile(D_MODEL), tile(SB_WIDTH), tile(DIL_OUT), tile(2 * D_MODEL),
                  _resident((SB_WIDTH, D_MODEL)), _resident((DIL_OUT, D_MODEL)),
                  _resident((D_MODEL, D_MODEL))],
        out_specs=[tile(2 * D_MODEL), tile(D_MODEL), tile(D_MODEL), tile(SB_WIDTH), tile(DIL_OUT),
                   tile(DIL_OUT)],
        out_shape=[jax.ShapeDtypeStruct((TOK, 2 * D_MODEL), BF16),
                   jax.ShapeDtypeStruct((TOK, D_MODEL), BF16),
                   jax.ShapeDtypeStruct((TOK, D_MODEL), BF16),
                   jax.ShapeDtypeStruct((TOK, SB_WIDTH), BF16),
                   jax.ShapeDtypeStruct((TOK, DIL_OUT), F32),
                   jax.ShapeDtypeStruct((TOK, DIL_OUT), F32)],
        compiler_params=_params(("parallel",)),
    )(dx1, o_sb, o_dl, gates, w_sb_up, w_dil_up, w_out)


def _proj_bwd(dproj, dx1, x, g, w_in_t, send=None):
    widths = [p.shape[1] for p in dproj]

    def body(*refs):
        dx1_ref, x_ref, g_ref, w_ref, dx_ref, dg_ref = refs[len(widths):]

        @pl.when(pl.program_id(0) == 0)
        def _():
            dg_ref[...] = jnp.zeros_like(dg_ref)

        du = jnp.zeros((TM, D_MODEL), F32)
        c0 = 0
        for dp_ref, w in zip(refs, widths):
            du = du + _dot(dp_ref[...].astype(BF16), w_ref[c0:c0 + w, :])
            c0 += w
        g_v = g_ref[...]
        n, r, _ = _rms_fwd(x_ref[...], g_v)
        dx, dg = _rms_bwd(du, n, r, g_v)
        dg_ref[...] += dg
        dx_ref[...] = dx1_ref[...] + dx

    tile = lambda w: pl.BlockSpec((TM, w), lambda i: (i, 0))
    return _call(
        body, send, name="proj_bwd", grid=(TOK // TM,),
        in_specs=[tile(w) for w in widths] + [tile(D_MODEL), tile(D_MODEL), _resident((1, D_MODEL)),
                                              _resident((IN_WIDTH, D_MODEL))],
        out_specs=[tile(D_MODEL), pl.BlockSpec((1, D_MODEL), lambda i: (0, 0))],
        out_shape=[jax.ShapeDtypeStruct((TOK, D_MODEL), F32),
                   jax.ShapeDtypeStruct((1, D_MODEL), F32)],
        scratch_shapes=[], semantics=("arbitrary",), operands=(*dproj, dx1, x, g, w_in_t))


def _atb_pieces(a, pieces, name, tm, tk=512):
    m = a.shape[1]
    widths = [p.shape[1] for p in pieces]
    n = sum(widths)
    nk = TOK // tk

    def body(a_ref, *refs):
        o_ref, acc_ref = refs[len(widths):]
        k = pl.program_id(1)

        @pl.when(k == 0)
        def _():
            acc_ref[...] = jnp.zeros_like(acc_ref)

        a_v = a_ref[...]
        c0 = 0
        for p_ref, w in zip(refs, widths):
            acc_ref[:, c0:c0 + w] += _dot_tn(a_v, p_ref[...].astype(BF16))
            c0 += w

        @pl.when(k == nk - 1)
        def _():
            for c0, w in _chunks(n):
                o_ref[c0:c0 + w, :] = acc_ref[:, c0:c0 + w].T.astype(BF16)

    return pl.pallas_call(
        body, name=name, grid=(m // tm, nk),
        in_specs=[pl.BlockSpec((tk, tm), lambda i, k: (k, i))]
                 + [pl.BlockSpec((tk, w), lambda i, k: (k, 0)) for w in widths],
        out_specs=pl.BlockSpec((n, tm), lambda i, k: (0, i)),
        out_shape=jax.ShapeDtypeStruct((n, m), BF16),
        scratch_shapes=[pltpu.VMEM((tm, n), F32)],
        compiler_params=_params(("parallel", "arbitrary")),
    )(a, *pieces)


def _atb_shards(a, b, name, tk=512):
    a_sharded = a.ndim == 3
    n, _, w = a.shape if a_sharded else b.shape
    other = (b if a_sharded else a).shape[1]

    def body(a_ref, b_ref, o_ref):
        acc = jnp.zeros(o_ref.shape, F32)
        for k0 in range(0, TOK, tk):
            acc = acc + _dot_tn(a_ref[k0:k0 + tk, :], b_ref[k0:k0 + tk, :])
        o_ref[...] = acc.astype(BF16)

    shard = pl.BlockSpec((None, TOK, w), lambda r: (r, 0, 0))
    whole = _resident((TOK, other))
    if a_sharded:
        out_spec, out_shape = pl.BlockSpec((w, other), lambda r: (r, 0)), (n * w, other)
    else:
        out_spec, out_shape = pl.BlockSpec((None, other, w), lambda r: (r, 0, 0)), (n, other, w)
    return pl.pallas_call(
        body, name=name, grid=(n,),
        in_specs=[shard, whole] if a_sharded else [whole, shard],
        out_specs=out_spec, out_shape=jax.ShapeDtypeStruct(out_shape, BF16),
        compiler_params=_params(("parallel",)),
    )(a, b)


def _atb(a, b, name, tm, tn, col_blocks=0, tk=512):
    m, n = a.shape[1], b.shape[1]
    nk = TOK // tk

    def body(a_ref, b_ref, o_ref, acc_ref):
        k = pl.program_id(2)

        @pl.when(k == 0)
        def _():
            acc_ref[...] = jnp.zeros_like(acc_ref)

        acc_ref[...] += _dot_tn(a_ref[...].astype(BF16), b_ref[...].astype(BF16))

        @pl.when(k == nk - 1)
        def _():
            if col_blocks:
                width = n // col_blocks
                for blk in range(col_blocks):
                    o_ref[blk] = acc_ref[:, blk * width:(blk + 1) * width].astype(BF16)
            else:
                o_ref[...] = acc_ref[...].astype(BF16)

    if col_blocks:
        out_spec = pl.BlockSpec((col_blocks, tm, n // col_blocks), lambda i, j, k: (0, i, 0))
        out_shape = jax.ShapeDtypeStruct((col_blocks, m, n // col_blocks), BF16)
    else:
        out_spec = pl.BlockSpec((tm, tn), lambda i, j, k: (i, j))
        out_shape = jax.ShapeDtypeStruct((m, n), BF16)
    return pl.pallas_call(
        body, name=name, grid=(m // tm, n // tn, nk),
        in_specs=[pl.BlockSpec((tk, tm), lambda i, j, k: (k, i)),
                  pl.BlockSpec((tk, tn), lambda i, j, k: (k, j))],
        out_specs=out_spec, out_shape=out_shape,
        scratch_shapes=[pltpu.VMEM((tm, tn), F32)],
        compiler_params=_params(("parallel", "parallel", "arbitrary")),
    )(a, b)


SB_PAIRS = SB_WIDTH // LANES


def _two_heads(v, lane0):
    zero = jnp.zeros_like(v)
    return jnp.where(lane0, v, zero), jnp.where(lane0, zero, v)


SB_QBLK = 256
N_SB_STEPS = SEQ // SB_QBLK


SB_KCHUNK = 2 * BLK
SB_ROWS = 2 * SB_QBLK
SB_DEAD = -104.0


def _log_keep(z):
    neg_z = -z
    return jnp.minimum(neg_z, 0.0) - jnp.log(1.0 + jnp.exp(jnp.minimum(z, neg_z)))


def _stack_heads(v, lane0):
    return jnp.concatenate(_two_heads(v, lane0), axis=0)


def _block_sums(v, tri, split=True):
    halves = (v[:, :BLK], v[:, BLK:])
    stacked = jnp.concatenate(halves, axis=0)
    if split:
        hi, lo = _split_bf16(stacked)
        prod = _dot(jnp.concatenate([hi, lo], axis=0), tri)
        tri_sum = prod[:2 * SB_ROWS] + prod[2 * SB_ROWS:]
    else:
        tri_sum = _dot(stacked.astype(BF16), tri)
    sums = tuple(jnp.sum(h, axis=1, keepdims=True) for h in halves)
    return (tri_sum[:SB_ROWS], tri_sum[SB_ROWS:]), sums


def _sb_diag_mask():
    row = lax.broadcasted_iota(jnp.int32, (SB_ROWS, SB_KCHUNK), 0)
    col = lax.broadcasted_iota(jnp.int32, (SB_ROWS, SB_KCHUNK), 1)
    return col < jnp.where(row >= SB_QBLK, row - SB_QBLK, row)


def _sb_fwd(qkv, send=None):
    def body(q_ref, k_ref, v_ref, o_ref):
        i = pl.program_id(2)
        krow = lax.broadcasted_iota(jnp.int32, (BLK, BLK), 0)
        kcol = lax.broadcasted_iota(jnp.int32, (BLK, BLK), 1)
        later = (krow > kcol).astype(BF16)
        lane0 = lax.broadcasted_iota(jnp.int32, (SB_QBLK, LANES), 1) < HEAD_DIM
        q2 = _stack_heads(q_ref[0] * QK_SCALE, lane0)

        def chunk(c, carry, causal):
            acc, run = carry
            off = pl.multiple_of(c * SB_KCHUNK, SB_KCHUNK)
            z = _dot_nt(q2, k_ref[0, pl.ds(off, SB_KCHUNK), :])
            log_keep = _log_keep(z)
            if causal is not None:
                log_keep = jnp.where(causal, log_keep, 0.0)
            suffix, sums = _block_sums(log_keep, later)
            log_after = jnp.concatenate([suffix[0] + (run + sums[1]), suffix[1] + run], axis=1)
            a = jnp.exp(log_keep + z + log_after)
            if causal is not None:
                a = jnp.where(causal, a, 0.0)
            acc = acc + _dot(a.astype(BF16), v_ref[0, pl.ds(off, SB_KCHUNK), :])
            return acc, run + (sums[0] + sums[1])

        acc, run = chunk(i, (jnp.zeros((SB_ROWS, LANES), F32), jnp.zeros((SB_ROWS, 1), F32)), _sb_diag_mask())

        def some_alive(run):
            return (jnp.max(run) > SB_DEAD).astype(jnp.int32)

        def trip(state):
            t, _, acc, run = state
            acc, run = chunk(i - 1 - t, (acc, run), None)
            return t + 1, some_alive(run), acc, run

        _, _, acc, _ = lax.while_loop(lambda s: jnp.logical_and(s[0] < i, s[1] > 0), trip,
                                      (jnp.int32(0), some_alive(run), acc, run))
        o_ref[0] = jnp.where(lane0, acc[:SB_QBLK], acc[SB_QBLK:]).astype(BF16)

    blk = pl.BlockSpec((1, SB_QBLK, LANES), lambda b, h, i: (b, i, h))
    return _call(
        body, send, name="sb_fwd", grid=(B_LOC, SB_PAIRS, N_SB_STEPS),
        in_specs=[blk,
                  pl.BlockSpec((1, SEQ, LANES), lambda b, h, i: (b, 0, SB_PAIRS + h)),
                  pl.BlockSpec((1, SEQ, LANES), lambda b, h, i: (b, 0, 2 * SB_PAIRS + h))],
        out_specs=[blk], out_shape=[jax.ShapeDtypeStruct((B_LOC, SEQ, SB_WIDTH), BF16)],
        scratch_shapes=[], semantics=("parallel", "parallel", "arbitrary"), operands=(qkv, qkv, qkv))


def _sb_bwd(qkv, d_o, send=None):
    def body(q_ref, k_ref, v_ref, do_ref, dq_ref, dk_ref, dv_ref, dk_acc, dv_acc, z_scr, keep_scr):
        i = pl.program_id(2)
        krow = lax.broadcasted_iota(jnp.int32, (BLK, BLK), 0)
        kcol = lax.broadcasted_iota(jnp.int32, (BLK, BLK), 1)
        upto = (krow <= kcol).astype(BF16)
        earlier = (krow < kcol).astype(BF16)
        lane0 = lax.broadcasted_iota(jnp.int32, (SB_QBLK, LANES), 1) < HEAD_DIM
        q2 = _stack_heads(q_ref[0] * QK_SCALE, lane0)
        do2 = _stack_heads(do_ref[0], lane0)

        def keep_sum(c, causal):
            off = pl.multiple_of(c * SB_KCHUNK, SB_KCHUNK)
            z = _dot_nt(q2, k_ref[0, pl.ds(off, SB_KCHUNK), :])
            log_keep = _log_keep(z)
            if causal is not None:
                log_keep = jnp.where(causal, log_keep, 0.0)
            z_scr[c] = z
            keep_scr[c] = log_keep
            return jnp.sum(log_keep, axis=1, keepdims=True)

        def some_alive(run):
            return (jnp.max(run) > SB_DEAD).astype(jnp.int32)

        def scan(state):
            t, _, run = state
            run = run + keep_sum(i - 1 - t, None)
            return t + 1, some_alive(run), run

        diag_sum = keep_sum(i, _sb_diag_mask())
        walked, _, tot2 = lax.while_loop(lambda s: jnp.logical_and(s[0] < i, s[1] > 0), scan,
                                         (jnp.int32(0), some_alive(diag_sum), diag_sum))
        first = i - walked

        @pl.when(i == 0)
        def _():
            dk_acc[...] = jnp.zeros_like(dk_acc)
            dv_acc[...] = jnp.zeros_like(dv_acc)

        def chunk(c, carry, causal):
            dq, pre_keep, pre_e = carry
            off = pl.multiple_of(c * SB_KCHUNK, SB_KCHUNK)
            k_c = k_ref[0, pl.ds(off, SB_KCHUNK), :]
            v_c = v_ref[0, pl.ds(off, SB_KCHUNK), :]
            d_a = _dot_nt(do2, v_c)
            log_keep = keep_scr[c]
            log_beta = log_keep + z_scr[c]
            prefix, sums = _block_sums(log_keep, upto)
            inclusive = jnp.concatenate([prefix[0], prefix[1] + sums[0]], axis=1)
            a = jnp.exp(log_beta + ((tot2 - pre_keep) - inclusive))
            if causal is not None:
                a = jnp.where(causal, a, 0.0)
            e = d_a * a
            e_prefix, e_sums = _block_sums(e, earlier, split=False)
            before = jnp.concatenate([e_prefix[0] + pre_e, e_prefix[1] + (pre_e + e_sums[0])], axis=1)
            dz = e - (e + before) * jnp.exp(log_beta)
            if causal is not None:
                dz = jnp.where(causal, dz, 0.0)
            dz = dz.astype(BF16)
            dq = dq + _dot(dz, k_c)
            dk_acc[pl.ds(off, SB_KCHUNK), :] += _dot_tn(dz, q2)
            dv_acc[pl.ds(off, SB_KCHUNK), :] += _dot_tn(a.astype(BF16), do2)
            return dq, pre_keep + (sums[0] + sums[1]), pre_e + (e_sums[0] + e_sums[1])

        zero_col = jnp.zeros((SB_ROWS, 1), F32)
        carry = lax.fori_loop(first, i, lambda t, c: chunk(t, c, None),
                              (jnp.zeros((SB_ROWS, LANES), F32), zero_col, zero_col))
        dq, _, _ = chunk(i, carry, _sb_diag_mask())
        dq_ref[0] = (jnp.where(lane0, dq[:SB_QBLK], dq[SB_QBLK:]) * QK_SCALE).astype(BF16)

        @pl.when(i == N_SB_STEPS - 1)
        def _():
            dk_ref[0] = dk_acc[...].astype(BF16)
            dv_ref[0] = dv_acc[...].astype(BF16)

    blk = pl.BlockSpec((1, SB_QBLK, LANES), lambda b, h, i: (b, i, h))
    whole = lambda c: pl.BlockSpec((1, SEQ, LANES), lambda b, h, i: (b, 0, c * SB_PAIRS + h))
    out = jax.ShapeDtypeStruct((B_LOC, SEQ, SB_WIDTH), BF16)
    return _call(
        body, send, name="sb_bwd", grid=(B_LOC, SB_PAIRS, N_SB_STEPS),
        in_specs=[blk, whole(1), whole(2), blk],
        out_specs=[blk, whole(0), whole(0)],
        out_shape=[out, out, out],
        scratch_shapes=[pltpu.VMEM((SEQ, LANES), F32), pltpu.VMEM((SEQ, LANES), F32),
                        pltpu.VMEM((N_SB_STEPS, SB_ROWS, SB_KCHUNK), F32),
                        pltpu.VMEM((N_SB_STEPS, SB_ROWS, SB_KCHUNK), F32)],
        semantics=("parallel", "parallel", "arbitrary"), operands=(qkv, qkv, qkv, d_o))


DIL_GROUPS = len(DIL_PAIRS)
DIL_QBLOCKS = SEQ // BLK


def _residue_rows(j, dilation):
    length = SEQ // dilation
    return pl.ds(j, length, stride=dilation) if dilation > 1 else pl.ds(0, length)


def _gather_residues(src_ref, dst_ref, dst_off, dilation, scale=None):
    length = SEQ // dilation
    for j in range(dilation):
        v = src_ref[_residue_rows(j, dilation), :]
        if scale is not None:
            v = v * scale
        dst_ref[dst_off + j * length:dst_off + (j + 1) * length, :] = v.astype(dst_ref.dtype)


def _scatter_residues(src_ref, src_off, dst_ref, dilation):
    length = SEQ // dilation
    for j in range(dilation):
        dst_ref[_residue_rows(j, dilation), :] = (
            src_ref[src_off + j * length:src_off + (j + 1) * length, :].astype(dst_ref.dtype))


def _dil_geometry(group, pair):
    dilation = DIL_PAIRS[group][1]
    row = lax.broadcasted_iota(jnp.int32, (2 * BLK, 2 * BLK), 0)
    col = lax.broadcasted_iota(jnp.int32, (2 * BLK, 2 * BLK), 1)
    second = row >= BLK
    steps = BLK + jnp.where(second, row - BLK, row) - col
    coef = -ALIBI_MAX_BIAS / DIL_HEADS * math.log(2.0)
    first_head = float(4 * group + 1) + 2.0 * pair.astype(F32)
    slope = jnp.exp(coef * (first_head + jnp.where(second, 1.0, 0.0)))
    bias = slope * (steps * dilation).astype(F32)
    valid = jnp.logical_and(steps >= 0, steps <= BLK)
    return bias, valid, col >= BLK


def _dil_tile_scores(q2, kk, geometry, has_prev):
    bias, valid, own = geometry
    ok = jnp.logical_and(valid, jnp.logical_or(own, has_prev))
    return jnp.where(ok, _dot_nt(q2, kk) - bias, NEG_BIG)


def _head_col(v, lane_mask):
    return jnp.max(jnp.where(lane_mask, v, NEG_BIG), axis=1, keepdims=True)


def _dil_fwd(qkv, send=None):
    def body(*refs):
        ins, (o_ref, lse_ref), (qs, ks, vs, o_res, lse_res) = refs[:9], refs[9:11], refs[11:16]
        o_grp, lse_grp = refs[16:19], refs[19:22]
        pair = pl.program_id(1)
        lane0 = lax.broadcasted_iota(jnp.int32, (BLK, LANES), 1) < HEAD_DIM
        ks[0:BLK, :] = jnp.zeros((BLK, LANES), BF16)
        vs[0:BLK, :] = jnp.zeros((BLK, LANES), BF16)
        for grp, (_, dilation) in enumerate(DIL_PAIRS):
            q_ref, k_ref, v_ref = ins[3 * grp:3 * grp + 3]
            per_residue = DIL_QBLOCKS // dilation
            _gather_residues(q_ref, qs, 0, dilation, QK_SCALE)
            _gather_residues(k_ref, ks, BLK, dilation)
            _gather_residues(v_ref, vs, BLK, dilation)
            geometry = _dil_geometry(grp, pair)

            def step(blk, _):
                off = pl.multiple_of(blk * BLK, BLK)
                q2 = _stack_heads(qs[pl.ds(off, BLK), :], lane0)
                s = _dil_tile_scores(q2, ks[pl.ds(off, 2 * BLK), :], geometry, blk % per_residue != 0)
                m = jnp.max(s, axis=1, keepdims=True)
                p = jnp.exp(s - m)
                den = jnp.sum(p, axis=1, keepdims=True)
                out = _dot(p.astype(BF16), vs[pl.ds(off, 2 * BLK), :]) / den
                lse = m + jnp.log(den)
                o_res[pl.ds(off, BLK), :] = jnp.where(lane0, out[:BLK], out[BLK:])
                lse_res[pl.ds(off, BLK), :] = jnp.where(lane0, lse[:BLK], lse[BLK:])
                return 0

            lax.fori_loop(0, DIL_QBLOCKS, step, 0, unroll=8)
            _scatter_residues(o_res, 0, o_grp[grp], dilation)
            _scatter_residues(lse_res, 0, lse_grp[grp], dilation)

        for r0 in range(0, SEQ, 2 * BLK):
            rows = slice(r0, r0 + 2 * BLK)
            ls = [lse_grp[g][rows, :] for g in range(DIL_GROUPS)]
            m = jnp.maximum(jnp.maximum(ls[0], ls[1]), ls[2])
            w = [jnp.exp(l - m) for l in ls]
            den = w[0] + w[1] + w[2]
            o_ref[rows, :] = (w[0] * o_grp[0][rows, :] + w[1] * o_grp[1][rows, :] + w[2] * o_grp[2][rows, :]) / den
            lse_ref[rows, :] = m + jnp.log(den)

    def col(part, grp):
        return pl.BlockSpec((None, SEQ, LANES), lambda b, p: (b, 0, 6 * part + 2 * grp + p))

    out_spec = pl.BlockSpec((None, SEQ, LANES), lambda b, p: (b, 0, p))
    out = jax.ShapeDtypeStruct((B_LOC, SEQ, DIL_OUT), F32)
    return _call(
        body, send, name="dil_fwd", grid=(B_LOC, DIL_OUT // LANES),
        in_specs=[col(part, grp) for grp in range(DIL_GROUPS) for part in range(3)],
        out_specs=[out_spec, out_spec], out_shape=[out, out],
        scratch_shapes=[pltpu.VMEM((SEQ, LANES), BF16), pltpu.VMEM((SEQ + BLK, LANES), BF16),
                        pltpu.VMEM((SEQ + BLK, LANES), BF16), pltpu.VMEM((SEQ, LANES), F32),
                        pltpu.VMEM((SEQ, LANES), F32)] + [pltpu.VMEM((SEQ, LANES), F32)] * (2 * DIL_GROUPS),
        semantics=("parallel", "parallel"), operands=[qkv] * 9)


def _dil_bwd(qkv, d_o, lse, dsum, send=None):
    def body(*refs):
        ins, (do_ref, lse_ref, dsum_ref), outs = refs[:9], refs[9:12], refs[12:21]
        qs, ks, vs, dos, lse_res, dsum_res, dq_res, dk_acc, dv_acc = refs[21:]
        pair = pl.program_id(1)
        lane0 = lax.broadcasted_iota(jnp.int32, (BLK, LANES), 1) < HEAD_DIM
        lane1 = jnp.logical_not(lane0)
        ks[0:BLK, :] = jnp.zeros((BLK, LANES), BF16)
        vs[0:BLK, :] = jnp.zeros((BLK, LANES), BF16)
        for grp, (_, dilation) in enumerate(DIL_PAIRS):
            q_ref, k_ref, v_ref = ins[3 * grp:3 * grp + 3]
            dq_ref, dk_ref, dv_ref = outs[3 * grp:3 * grp + 3]
            per_residue = DIL_QBLOCKS // dilation
            _gather_residues(q_ref, qs, 0, dilation, QK_SCALE)
            _gather_residues(k_ref, ks, BLK, dilation)
            _gather_residues(v_ref, vs, BLK, dilation)
            _gather_residues(do_ref, dos, 0, dilation)
            _gather_residues(lse_ref, lse_res, 0, dilation)
            _gather_residues(dsum_ref, dsum_res, 0, dilation)
            dk_acc[...] = jnp.zeros_like(dk_acc)
            dv_acc[...] = jnp.zeros_like(dv_acc)
            geometry = _dil_geometry(grp, pair)

            def step(blk, _):
                off = pl.multiple_of(blk * BLK, BLK)
                q2 = _stack_heads(qs[pl.ds(off, BLK), :], lane0)
                do2 = _stack_heads(dos[pl.ds(off, BLK), :], lane0)
                kk = ks[pl.ds(off, 2 * BLK), :]
                vv = vs[pl.ds(off, 2 * BLK), :]
                lse_blk = lse_res[pl.ds(off, BLK), :]
                dsum_blk = dsum_res[pl.ds(off, BLK), :]
                lse2 = jnp.concatenate([_head_col(lse_blk, lane0), _head_col(lse_blk, lane1)], axis=0)
                dsum2 = jnp.concatenate([_head_col(dsum_blk, lane0), _head_col(dsum_blk, lane1)], axis=0)
                s = _dil_tile_scores(q2, kk, geometry, blk % per_residue != 0)
                p = jnp.exp(s - lse2)
                ds = (p * (_dot_nt(do2, vv) - dsum2)).astype(BF16)
                dq2 = _dot(ds, kk)
                dq_res[pl.ds(off, BLK), :] = jnp.where(lane0, dq2[:BLK], dq2[BLK:]) * QK_SCALE
                dk_acc[pl.ds(off, 2 * BLK), :] += _dot_tn(ds, q2)
                dv_acc[pl.ds(off, 2 * BLK), :] += _dot_tn(p.astype(BF16), do2)
                return 0

            lax.fori_loop(0, DIL_QBLOCKS, step, 0, unroll=8)
            _scatter_residues(dq_res, 0, dq_ref, dilation)
            _scatter_residues(dk_acc, BLK, dk_ref, dilation)
            _scatter_residues(dv_acc, BLK, dv_ref, dilation)

    def col(part, grp):
        return pl.BlockSpec((None, SEQ, LANES), lambda b, p: (b, 0, 6 * part + 2 * grp + p))

    slot = pl.BlockSpec((None, SEQ, LANES), lambda b, p: (b, 0, p))
    out = jax.ShapeDtypeStruct((B_LOC, SEQ, DIL_OUT), F32)
    return _call(
        body, send, name="dil_bwd", grid=(B_LOC, DIL_OUT // LANES),
        in_specs=[col(part, grp) for grp in range(DIL_GROUPS) for part in range(3)] + [slot] * 3,
        out_specs=[slot] * 9, out_shape=[out] * 9,
        scratch_shapes=[pltpu.VMEM((SEQ, LANES), BF16), pltpu.VMEM((SEQ + BLK, LANES), BF16),
                        pltpu.VMEM((SEQ + BLK, LANES), BF16), pltpu.VMEM((SEQ, LANES), BF16),
                        pltpu.VMEM((SEQ, LANES), F32), pltpu.VMEM((SEQ, LANES), F32),
                        pltpu.VMEM((SEQ, LANES), F32), pltpu.VMEM((SEQ + BLK, LANES), F32),
                        pltpu.VMEM((SEQ + BLK, LANES), F32)],
        semantics=("parallel", "parallel"), operands=[qkv] * 9 + [d_o, lse, dsum])


def _peers():
    x, y, c = lax.axis_index("x"), lax.axis_index("y"), lax.axis_index("c")
    me = 4 * x + 2 * y + c
    peers = []
    for mask in range(1, N_DEV):
        px = 1 - x if mask & 4 else x
        py = 1 - y if mask & 2 else y
        pc = 1 - c if mask & 1 else c
        peers.append(((px, py, pc), 4 * px + 2 * py + pc))
    return me, peers


def _all_gather(shard, name):
    def body(src_ref, out_ref, send_sems, recv_sems, local_sem):
        x, y, c = lax.axis_index("x"), lax.axis_index("y"), lax.axis_index("c")
        sibling = (x, y, 1 - c)
        chips = [(1 - x, y), (x, 1 - y), (1 - x, 1 - y)]

        def slot(px, py, pc):
            return out_ref.at[4 * px + 2 * py + pc]

        def copy(k, block, to, src=None):
            return pltpu.make_async_remote_copy(
                src_ref=slot(*block) if src is None else src, dst_ref=slot(*block),
                send_sem=send_sems.at[k], recv_sem=recv_sems.at[k], device_id=to,
                device_id_type=pl.DeviceIdType.MESH)

        mine = pltpu.make_async_copy(src_ref, slot(x, y, c), local_sem)
        mine.start()
        first = [copy(0, (x, y, c), sibling, src=src_ref)]
        first += [copy(1 + j, (x, y, c), (*chip, c), src=src_ref) for j, chip in enumerate(chips)]
        for cp in first:
            cp.start()
        passed = [copy(4 + j, (*chip, c), sibling) for j, chip in enumerate(chips)]
        for j, chip in enumerate(chips):
            copy(1 + j, (*chip, c), (x, y, c)).wait_recv()
            passed[j].start()
        copy(0, sibling, (x, y, c)).wait_recv()
        for j, chip in enumerate(chips):
            copy(4 + j, (*chip, 1 - c), (x, y, c)).wait_recv()
        for cp in first + passed:
            cp.wait_send()
        mine.wait()

    return pl.pallas_call(
        body, name=name,
        in_specs=[pl.BlockSpec(memory_space=pl.ANY)],
        out_specs=pl.BlockSpec(memory_space=pl.ANY),
        out_shape=jax.ShapeDtypeStruct((N_DEV,) + shard.shape, shard.dtype),
        scratch_shapes=[pltpu.SemaphoreType.DMA((N_DEV - 1,)), pltpu.SemaphoreType.DMA((N_DEV - 1,)),
                        pltpu.SemaphoreType.DMA],
    )(shard)


def _call(body, send, *, name, grid, in_specs, out_specs, out_shape, scratch_shapes, semantics, operands):
    if send is None:
        return pl.pallas_call(
            body, name=name, grid=grid, in_specs=in_specs, out_specs=out_specs, out_shape=out_shape,
            scratch_shapes=scratch_shapes, compiler_params=_params(semantics))(*operands), []
    srcs, kinds = [s for s, _ in send], [k for _, k in send]
    n, n_in, n_out, n_scr = len(srcs), len(in_specs), len(out_specs), len(scratch_shapes)
    steps = math.prod(grid)
    relay_step = (13 * steps) // 16

    def plan(refs):
        src_refs, land_refs = refs[n_in:n_in + n], refs[n_in + n + n_out:n_in + 2 * n + n_out]
        send_sems, recv_sems, local_sems = refs[-3:]
        x, y, c = lax.axis_index("x"), lax.axis_index("y"), lax.axis_index("c")
        me, peers = _peers()
        first, relayed_in, relayed_out, arrivals, sends, own = [], [], [], [], [], []
        for a, kind in enumerate(kinds):
            def copy(k, src, dst_slot, to):
                return pltpu.make_async_remote_copy(
                    src_ref=src, dst_ref=land_refs[a].at[dst_slot], send_sem=send_sems.at[a * (N_DEV - 1) + k],
                    recv_sem=recv_sems.at[a * (N_DEV - 1) + k], device_id=to, device_id_type=pl.DeviceIdType.MESH)

            if kind == "gather_by_chip":
                idx = lambda px, py, pc: 4 * px + 2 * py + pc
                chips = [(1 - x, y), (x, 1 - y), (1 - x, 1 - y)]
                mine = [copy(0, src_refs[a], me, (x, y, 1 - c))]
                arrivals.append(copy(0, src_refs[a], idx(x, y, 1 - c), (x, y, 1 - c)))
                for j, (px, py) in enumerate(chips):
                    mine.append(copy(1 + j, src_refs[a], me, (px, py, c)))
                    relayed_in.append(copy(1 + j, src_refs[a], idx(px, py, c), (px, py, c)))
                    relayed_out.append(copy(4 + j, land_refs[a].at[idx(px, py, c)], idx(px, py, c), (x, y, 1 - c)))
                    arrivals.append(copy(4 + j, src_refs[a], idx(px, py, 1 - c), (x, y, 1 - c)))
                first += mine
                sends += mine + relayed_out[-3:]
                own.append(pltpu.make_async_copy(src_refs[a], land_refs[a].at[me], local_sems.at[a]))
            elif kind == "scatter_by_chip":
                for k, (px, py) in enumerate([(1 - x, y), (x, 1 - y), (1 - x, 1 - y)]):
                    first.append(copy(k, src_refs[a].at[2 * px + py], 2 * x + y, (px, py, c)))
                    arrivals.append(copy(k, src_refs[a].at[2 * px + py], 2 * px + py, (px, py, c)))
                sends += first[-3:]
                own.append(pltpu.make_async_copy(src_refs[a].at[2 * x + y], land_refs[a].at[2 * x + y],
                                                 local_sems.at[a]))
            else:
                part = (lambda i: src_refs[a].at[i]) if kind == "scatter" else (lambda i: src_refs[a])
                for k, (peer, peer_idx) in enumerate(peers):
                    first.append(copy(k, part(peer_idx), me, peer))
                    arrivals.append(copy(k, part(peer_idx), peer_idx, peer))
                sends += first[-(N_DEV - 1):]
                own.append(pltpu.make_async_copy(part(me), land_refs[a].at[me], local_sems.at[a]))
        return first, relayed_in, relayed_out, arrivals, sends, own

    def wrapped(*refs):
        step = 0
        for axis, size in enumerate(grid):
            step = step * size + pl.program_id(axis)

        @pl.when(step == 0)
        def _():
            first, _, _, _, _, own = plan(refs)
            for cp in first + own:
                cp.start()

        if "gather_by_chip" in kinds:
            @pl.when(step == relay_step)
            def _():
                _, relayed_in, relayed_out, _, _, _ = plan(refs)
                for cp_in, cp_out in zip(relayed_in, relayed_out):
                    cp_in.wait_recv()
                    cp_out.start()

        body(*refs[:n_in], *refs[n_in + n:n_in + n + n_out], *refs[n_in + 2 * n + n_out:n_in + 2 * n + n_out + n_scr])

        @pl.when(step == steps - 1)
        def _():
            _, _, _, arrivals, sends, own = plan(refs)
            for cp in arrivals:
                cp.wait_recv()
            for cp in sends:
                cp.wait_send()
            for cp in own:
                cp.wait()

    anywhere = pl.BlockSpec(memory_space=pl.ANY)
    lands = [jax.ShapeDtypeStruct((N_DEV // 2 if k == "scatter_by_chip" else N_DEV,) + s.shape[-2:], s.dtype)
             for s, k in send]
    out = pl.pallas_call(
        wrapped, name=name, grid=grid,
        in_specs=list(in_specs) + [anywhere] * n, out_specs=list(out_specs) + [anywhere] * n,
        out_shape=list(out_shape) + lands,
        scratch_shapes=list(scratch_shapes) + [pltpu.SemaphoreType.DMA((n * (N_DEV - 1),)),
                                               pltpu.SemaphoreType.DMA((n * (N_DEV - 1),)),
                                               pltpu.SemaphoreType.DMA((n,))],
        compiler_params=_params(("arbitrary",) * len(grid)),
    )(*operands, *srcs)
    return out[:n_out], list(out[n_out:])


def _pair_swap(blocks):
    def body(src_ref, out_ref, send_sems, recv_sems):
        x, y, c = lax.axis_index("x"), lax.axis_index("y"), lax.axis_index("c")
        copies = [pltpu.make_async_remote_copy(
            src_ref=src_ref.at[2 * chip + (1 - c)], dst_ref=out_ref.at[chip], send_sem=send_sems.at[chip],
            recv_sem=recv_sems.at[chip], device_id=(x, y, 1 - c), device_id_type=pl.DeviceIdType.MESH)
            for chip in range(N_DEV // 2)]
        for cp in copies:
            cp.start()
        for cp in copies:
            cp.wait()

    return pl.pallas_call(
        body, name="pair_swap_grad_w_in",
        in_specs=[pl.BlockSpec(memory_space=pl.ANY)], out_specs=pl.BlockSpec(memory_space=pl.ANY),
        out_shape=jax.ShapeDtypeStruct((N_DEV // 2,) + blocks.shape[1:], blocks.dtype),
        scratch_shapes=[pltpu.SemaphoreType.DMA((N_DEV // 2,)), pltpu.SemaphoreType.DMA((N_DEV // 2,))],
    )(blocks)


def _pair_sum(blocks, swapped, core):
    _, rows, cols = swapped.shape
    tile_rows = _row_tile(rows)

    def body(core_ref, mine_ref, theirs_ref, o_ref):
        o_ref[...] = (mine_ref[...].astype(F32) + theirs_ref[...].astype(F32)).astype(o_ref.dtype)

    return pl.pallas_call(
        body, name="pair_sum_grad_w_in",
        grid_spec=pltpu.PrefetchScalarGridSpec(
            num_scalar_prefetch=1, grid=(N_DEV // 2, rows // tile_rows),
            in_specs=[pl.BlockSpec((None, tile_rows, cols), lambda j, i, core_ref: (2 * j + core_ref[0], i, 0)),
                      pl.BlockSpec((None, tile_rows, cols), lambda j, i, core_ref: (j, i, 0))],
            out_specs=pl.BlockSpec((None, tile_rows, cols), lambda j, i, core_ref: (j, i, 0))),
        out_shape=jax.ShapeDtypeStruct(swapped.shape, swapped.dtype),
        compiler_params=_params(("parallel", "parallel")),
    )(core, blocks, swapped)


def _sum_in_device_order(land_ref):
    acc = land_ref[0].astype(F32)
    for j in range(1, land_ref.shape[0]):
        acc = acc + land_ref[j].astype(F32)
    return acc


def _adam_math(w, g, m, v):
    c1 = 1.0 - ADAM_B1 ** ADAM_STEP
    c2 = 1.0 - ADAM_B2 ** ADAM_STEP
    m_new = ADAM_B1 * m + (1.0 - ADAM_B1) * g
    v_new = ADAM_B2 * v + (1.0 - ADAM_B2) * (g * g)
    delta = -ADAM_LR * ((m_new / c1) / (jnp.sqrt(v_new / c2) + ADAM_EPS) + ADAM_WD * w)
    return delta, m_new, v_new


def _row_tile(rows):
    return max(t for t in range(16, 385, 16) if rows % t == 0) if rows % 16 == 0 else rows


def _sum_update(land, w, m, v, name):
    slots, rows, cols = land.shape
    tile_rows = _row_tile(rows)

    def body(land_ref, w_ref, m_ref, v_ref, g_ref, d_ref, nm_ref, nv_ref):
        g = _sum_in_device_order(land_ref)
        g_ref[...] = g
        d_ref[...], nm_ref[...], nv_ref[...] = _adam_math(w_ref[...], g, m_ref[...], v_ref[...])

    tile = pl.BlockSpec((None, tile_rows, cols), lambda i: (0, i, 0))
    out = jax.ShapeDtypeStruct((1, rows, cols), F32)
    return pl.pallas_call(
        body, name=name, grid=(rows // tile_rows,),
        in_specs=[pl.BlockSpec((slots, tile_rows, cols), lambda i: (0, i, 0)), tile, tile, tile],
        out_specs=[tile] * 4, out_shape=[out] * 4,
        compiler_params=_params(("parallel",)),
    )(land, w, m, v)


def _sum_gains(land):
    def body(land_ref, o_ref):
        o_ref[...] = _sum_in_device_order(land_ref)

    return pl.pallas_call(
        body, name="sum_gain_grads", grid=(1,),
        in_specs=[pl.BlockSpec(land.shape, lambda i: (0, 0, 0))],
        out_specs=pl.BlockSpec(land.shape[1:], lambda i: (0, 0)),
        out_shape=jax.ShapeDtypeStruct(land.shape[1:], F32),
    )(land)


def _adamw(w, g, m, v, name):
    def body(w_ref, g_ref, m_ref, v_ref, d_ref, nm_ref, nv_ref):
        d_ref[...], nm_ref[...], nv_ref[...] = _adam_math(w_ref[...], g_ref[...], m_ref[...], v_ref[...])

    whole = pl.BlockSpec(w.shape, lambda i: (0, 0))
    out = jax.ShapeDtypeStruct(w.shape, F32)
    return pl.pallas_call(
        body, name=name, grid=(1,),
        in_specs=[whole] * 4, out_specs=[whole] * 3, out_shape=[out] * 3,
    )(w, g, m, v)


GROUP_FFN = ("w_ffn_in", "w_ffn_out")
GROUP_MIX = ("w_sb_up", "w_dil_up", "w_out")
COL_SHARDED = ("w_in", "w_sb_up", "w_dil_up", "w_ffn_in")
TRANSPOSED = ("w_in", "w_ffn_in")


def _full_from_shards(name, slots):
    _, r, c = slots.shape
    if name in TRANSPOSED:
        return slots.reshape(N_DEV * r, c).T
    if name in COL_SHARDED:
        return slots.transpose(1, 0, 2).reshape(r, N_DEV * c)
    return slots.reshape(N_DEV * r, c)


def _row_shards(full):
    rows, cols = full.shape
    return full.reshape(N_DEV, rows // N_DEV, cols)


def _local_step(x, target, g_mix, g_ffn, g_fin, w_in_t, shards=None, rest=None):
    gather = lambda names, kind: None if shards is None else [(shards[n], kind) for n in names]
    scatter = lambda blocks: None if shards is None else [(t, "scatter") for t in blocks]
    landed = lambda blocks, lands: lands if lands else blocks

    w = {"w_in": w_in_t}
    if shards is None:
        w.update(rest)
        w["w_ffn_in"] = _row_shards(rest["w_ffn_in"].T)
    (qkv_sb, qkv_dl, gates, u), _ = _norm_proj(x, g_mix, w["w_in"])
    qkv_sb = qkv_sb.reshape(B_LOC, SEQ, 3 * SB_WIDTH)
    qkv_dl = qkv_dl.reshape(B_LOC, SEQ, 3 * DIL_WIDTH)
    (o_sb,), lands = _sb_fwd(qkv_sb, gather(GROUP_FFN, "gather_by_chip"))
    if lands:
        w["w_ffn_in"], w["w_ffn_out"] = lands[0], _full_from_shards("w_ffn_out", lands[1])
    o_sb = o_sb.reshape(TOK, SB_WIDTH)
    (o_dl, lse), lands = _dil_fwd(qkv_dl, gather(GROUP_MIX, "gather"))
    w.update({n: _full_from_shards(n, t) for n, t in zip(GROUP_MIX, lands)})
    o_dl = o_dl.reshape(TOK, DIL_OUT)

    loss, dx1, merged, u2, act, dh, dx2, dg_fin, dg_ffn = _mix_ffn_fwd_bwd(
        x, o_sb, o_dl, gates, w["w_sb_up"], w["w_dil_up"], w["w_out"], target, g_ffn, g_fin,
        w["w_ffn_in"], w["w_ffn_out"])
    dgates, dy_sb, dy_dl, do_sb, do_dl, dsum = _mix_bwd(dx1, o_sb, o_dl, gates, w["w_sb_up"], w["w_dil_up"], w["w_out"])
    blocks = {
        "w_sb_up": _atb(o_sb, dy_sb, "grad_w_sb_up", SB_WIDTH, D_MODEL, col_blocks=N_DEV),
        "w_dil_up": _atb(o_dl, dy_dl, "grad_w_dil_up", DIL_OUT, D_MODEL, col_blocks=N_DEV),
        "w_out": _row_shards(_atb(merged, dx1, "grad_w_out", D_MODEL, D_MODEL)),
        "w_ffn_in": _row_shards(_atb_shards(dh, u2, "grad_w_ffn_in")),
        "w_ffn_out": _row_shards(_atb_shards(act, dx2, "grad_w_ffn_out")),
    }
    grads = {}

    early, late = GROUP_FFN, GROUP_MIX
    early_blocks = [blocks[n] for n in early]
    (dq_sb, dk_sb, dv_sb), lands = _sb_bwd(qkv_sb, do_sb.reshape(B_LOC, SEQ, SB_WIDTH), scatter(early_blocks))
    grads.update(zip(early, landed(early_blocks, lands)))
    as_batch = lambda t: t.reshape(B_LOC, SEQ, DIL_OUT)
    late_blocks = [blocks[n] for n in late]
    d_dl, lands = _dil_bwd(qkv_dl, as_batch(do_dl), lse, as_batch(dsum), scatter(late_blocks))
    grads.update(zip(late, landed(late_blocks, lands)))
    flat = lambda t: t.reshape(TOK, -1)
    dproj = ([flat(dq_sb), flat(dk_sb), flat(dv_sb)]
             + [flat(d_dl[3 * grp + part]) for part in range(3) for grp in range(DIL_GROUPS)] + [dgates])

    w_in_blocks = _row_shards(_atb_pieces(u, dproj, "grad_w_in", D_MODEL // 2))
    if shards is None:
        grads["w_in"] = w_in_blocks
        send = None
    else:
        core = lax.axis_index("c").astype(jnp.int32).reshape(1)
        send = [(_pair_sum(w_in_blocks, _pair_swap(w_in_blocks), core), "scatter_by_chip")]
    (grad_x, dg_mix), lands = _proj_bwd(dproj, dx1, x, g_mix, w["w_in"], send)
    if lands:
        grads["w_in"] = lands[0]
    gain_grads = jnp.concatenate([dg_mix, dg_ffn, dg_fin], axis=0)
    return loss, grad_x, gain_grads, grads


def kernel(x, norm_mix_g, w_in, w_sb_up, w_dil_up, w_out, norm_ffn_g, w_ffn_in, w_ffn_out, norm_final_g, loss_target, m_norm_mix_g, m_w_in, m_w_sb_up, m_w_dil_up, m_w_out, m_norm_ffn_g, m_w_ffn_in, m_w_ffn_out, m_norm_final_g, v_norm_mix_g, v_w_in, v_w_sb_up, v_w_dil_up, v_w_out, v_norm_ffn_g, v_w_ffn_in, v_w_ffn_out, v_norm_final_g):
    mats = {"w_in": w_in, "w_sb_up": w_sb_up, "w_dil_up": w_dil_up, "w_out": w_out,
            "w_ffn_in": w_ffn_in, "w_ffn_out": w_ffn_out}
    moments_m = {"w_in": m_w_in, "w_sb_up": m_w_sb_up, "w_dil_up": m_w_dil_up, "w_out": m_w_out,
                 "w_ffn_in": m_w_ffn_in, "w_ffn_out": m_w_ffn_out}
    moments_v = {"w_in": v_w_in, "w_sb_up": v_w_sb_up, "w_dil_up": v_w_dil_up, "w_out": v_w_out,
                 "w_ffn_in": v_w_ffn_in, "w_ffn_out": v_w_ffn_out}
    stored = lambda t, name: t.transpose(0, 2, 1) if name in TRANSPOSED else t
    mats = {name: stored(t, name) for name, t in mats.items()}
    gathered_w_in = _all_gather(mats["w_in"][0].astype(BF16), "all_gather_w_in")
    g_fin = norm_final_g.reshape(1, D_MODEL)
    loss, grad_x, gain_grads, grad_slots = _local_step(
        x.reshape(TOK, D_MODEL), loss_target.reshape(TOK, D_MODEL), norm_mix_g, norm_ffn_g, g_fin,
        gathered_w_in.reshape(IN_WIDTH, D_MODEL),
        shards={name: mats[name][0].astype(BF16) for name in GROUP_FFN + GROUP_MIX})

    gain_rows = jnp.concatenate([gain_grads, jnp.tile(loss, (1, D_MODEL // LANES)),
                                 jnp.zeros((8 - 4, D_MODEL), F32)], axis=0)
    g_gains = _sum_gains(_all_gather(gain_rows, "all_gather_gains"))

    out_g, out_d, out_m, out_v = {}, {}, {}, {}
    for name, slots in grad_slots.items():
        out_g[name], out_d[name], out_m[name], out_v[name] = [stored(t, name) for t in _sum_update(
            slots, mats[name], stored(moments_m[name], name), stored(moments_v[name], name), "update_" + name)]

    gain_w = jnp.concatenate([norm_mix_g, norm_ffn_g, g_fin], axis=0)
    gain_m = jnp.concatenate([m_norm_mix_g, m_norm_ffn_g, m_norm_final_g.reshape(1, D_MODEL)], axis=0)
    gain_v = jnp.concatenate([v_norm_mix_g, v_norm_ffn_g, v_norm_final_g.reshape(1, D_MODEL)], axis=0)
    gd, gm, gv = _adamw(gain_w, g_gains[:3], gain_m, gain_v, "adamw_gains")
    for idx, name in enumerate(("norm_mix_g", "norm_ffn_g", "norm_final_g")):
        shape = (D_MODEL,) if name == "norm_final_g" else (1, D_MODEL)
        out_g[name] = g_gains[idx].reshape(shape)
        out_d[name], out_m[name], out_v[name] = gd[idx].reshape(shape), gm[idx].reshape(shape), gv[idx].reshape(shape)

    order = ("norm_mix_g", "w_in", "w_sb_up", "w_dil_up", "w_out", "norm_ffn_g", "w_ffn_in", "w_ffn_out",
             "norm_final_g")
    return (g_gains[3, 0], grad_x.reshape(B_LOC, SEQ, D_MODEL),
            *[out_g[n] for n in order], *[out_d[n] for n in order],
            *[out_m[n] for n in order], *[out_v[n] for n in order])
```

```python
import math

import jax
import jax.numpy as jnp
from jax import lax
from jax.experimental import pallas as pl
from jax.experimental.pallas import tpu as pltpu

F32 = jnp.float32
BF16 = jnp.bfloat16

N_DEV = 8
D_MODEL = 1024
SEQ = 2048
B_LOC = 2
TOK = B_LOC * SEQ
HEAD_DIM = 64
SB_WIDTH = 512
DIL_WIDTH = 768
DIL_OUT = 256
QKV_WIDTH = 3 * SB_WIDTH + 3 * DIL_WIDTH
IN_WIDTH = QKV_WIDTH + 2 * D_MODEL
D_FF = 2816
DIL_PAIRS = ((128, 1), (512, 4), (2048, 16))
DIL_HEADS = 12
RMS_EPS = 1e-6
ALIBI_MAX_BIAS = 8.0
QK_SCALE = 1.0 / math.sqrt(HEAD_DIM)
BLK = 128
LANES = 128
NEG_BIG = -1e30

ADAM_LR = 0.001
ADAM_B1 = 0.9
ADAM_B2 = 0.999
ADAM_EPS = 1e-08
ADAM_WD = 0.01
ADAM_STEP = 10

VMEM_LIMIT = 56 * 1024 * 1024


def _dot(a, b):
    return jnp.dot(a, b, preferred_element_type=F32)


def _dot_nt(a, b):
    return lax.dot_general(a, b, (((1,), (1,)), ((), ())), preferred_element_type=F32)


def _dot_tn(a, b):
    return lax.dot_general(a, b, (((0,), (0,)), ((), ())), preferred_element_type=F32)


def _sigmoid(z):
    return 1.0 / (1.0 + jnp.exp(-z))


def _split_bf16(v):
    hi = v.astype(BF16)
    lo = (v - hi.astype(F32)).astype(BF16)
    return hi, lo


def _chunks(width, step=512):
    out, c = [], 0
    while c < width:
        w = min(step, width - c)
        out.append((c, w))
        c += w
    return out


def _resident(shape):
    nd = len(shape)
    return pl.BlockSpec(shape, lambda *_: (0,) * nd, pipeline_mode=pl.Buffered(1))


def _params(sem):
    return pltpu.CompilerParams(dimension_semantics=sem, vmem_limit_bytes=VMEM_LIMIT)


def _rms_fwd(x, g):
    r = lax.rsqrt(jnp.mean(x * x, axis=-1, keepdims=True) + RMS_EPS)
    n = x * r
    return n, r, n * g


def _rms_bwd(dy, n, r, g):
    dg = jnp.sum(dy * n, axis=0, keepdims=True)
    dn = dy * g
    dx = r * (dn - n * jnp.mean(dn * n, axis=-1, keepdims=True))
    return dx, dg


TM = 256


def _norm_proj(x, g, w_in_t, send=None):
    def body(x_ref, g_ref, w_ref, sb_ref, dl_ref, gate_ref, u_ref):
        _, _, u = _rms_fwd(x_ref[...], g_ref[...])
        u = u.astype(BF16)
        u_ref[...] = u
        for c0, w in _chunks(3 * SB_WIDTH):
            sb_ref[:, c0:c0 + w] = _dot_nt(u, w_ref[c0:c0 + w, :]).astype(BF16)
        for c0, w in _chunks(3 * DIL_WIDTH):
            dl_ref[:, c0:c0 + w] = _dot_nt(u, w_ref[3 * SB_WIDTH + c0:3 * SB_WIDTH + c0 + w, :])
        for c0, w in _chunks(2 * D_MODEL):
            gate_ref[:, c0:c0 + w] = _dot_nt(u, w_ref[QKV_WIDTH + c0:QKV_WIDTH + c0 + w, :])

    return _call(
        body, send, name="norm_proj", grid=(TOK // TM,),
        in_specs=[pl.BlockSpec((TM, D_MODEL), lambda i: (i, 0)), _resident((1, D_MODEL)),
                  _resident((IN_WIDTH, D_MODEL))],
        out_specs=[pl.BlockSpec((TM, 3 * SB_WIDTH), lambda i: (i, 0)),
                   pl.BlockSpec((TM, 3 * DIL_WIDTH), lambda i: (i, 0)),
                   pl.BlockSpec((TM, 2 * D_MODEL), lambda i: (i, 0)),
                   pl.BlockSpec((TM, D_MODEL), lambda i: (i, 0))],
        out_shape=[jax.ShapeDtypeStruct((TOK, 3 * SB_WIDTH), BF16),
                   jax.ShapeDtypeStruct((TOK, 3 * DIL_WIDTH), F32),
                   jax.ShapeDtypeStruct((TOK, 2 * D_MODEL), F32),
                   jax.ShapeDtypeStruct((TOK, D_MODEL), BF16)],
        scratch_shapes=[], semantics=("parallel",), operands=(x, g, w_in_t))


FF_SHARD = 2 * D_FF // N_DEV
FF_PAIRS = N_DEV // 2


def _mix_ffn_fwd_bwd(x, o_sb, o_dl, gates, w_sb_up, w_dil_up, w_out, target, g_ffn, g_fin, w_ffn_in, w_ffn_out):
    def body(x_ref, osb_ref, odl_ref, gate_ref, wsb_ref, wdl_ref, wo_ref, t_ref, gffn_ref, gfin_ref, win_ref, wout_ref,
             loss_ref, dx1_ref, mg_ref, u2_ref, act_ref, dh_ref, dx2_ref, dgfin_ref, dgffn_ref, h_scr):
        i = pl.program_id(0)

        @pl.when(i == 0)
        def _():
            loss_ref[...] = jnp.zeros_like(loss_ref)
            dgfin_ref[...] = jnp.zeros_like(dgfin_ref)
            dgffn_ref[...] = jnp.zeros_like(dgffn_ref)

        y_sb = _dot(osb_ref[...], wsb_ref[...])
        y_dl = _dot(odl_ref[...].astype(BF16), wdl_ref[...])
        merged = (_sigmoid(gate_ref[:, :D_MODEL]) * y_sb
                  + _sigmoid(gate_ref[:, D_MODEL:]) * y_dl).astype(BF16)
        mg_ref[...] = merged
        x1 = x_ref[...] + _dot(merged, wo_ref[...])
        g_ffn_v = gffn_ref[...]
        g_fin_v = gfin_ref[...]
        n2, r2, u2 = _rms_fwd(x1, g_ffn_v)
        u2 = u2.astype(BF16)
        u2_ref[...] = u2
        x2 = x1
        for r in range(FF_PAIRS):
            gate = _dot_nt(u2, win_ref[r])
            up = _dot_nt(u2, win_ref[r + FF_PAIRS])
            h_scr[r] = gate
            h_scr[r + FF_PAIRS] = up
            act = (gate * _sigmoid(gate) * up).astype(BF16)
            act_ref[r] = act
            x2 = x2 + _dot(act, wout_ref[r * FF_SHARD:(r + 1) * FF_SHARD, :])
        n3, r3, y = _rms_fwd(x2, g_fin_v)
        err = y - t_ref[...]
        sq = jnp.sum(jnp.sum(err * err, axis=1, keepdims=True), axis=0, keepdims=True)
        loss_ref[...] += sq * (0.5 / D_MODEL)
        dx2, dgfin = _rms_bwd(err * (1.0 / D_MODEL), n3, r3, g_fin_v)
        dgfin_ref[...] += dgfin
        dx2_b = dx2.astype(BF16)
        dx2_ref[...] = dx2_b
        du2 = jnp.zeros((TM, D_MODEL), F32)
        for r in range(FF_PAIRS):
            gate = h_scr[r]
            up = h_scr[r + FF_PAIRS]
            dact = _dot_nt(dx2_b, wout_ref[r * FF_SHARD:(r + 1) * FF_SHARD, :])
            sg = _sigmoid(gate)
            dgate = (dact * up * (sg * (1.0 + gate * (1.0 - sg)))).astype(BF16)
            dup = (dact * (gate * sg)).astype(BF16)
            dh_ref[r] = dgate
            dh_ref[r + FF_PAIRS] = dup
            du2 = du2 + _dot(dgate, win_ref[r])
            du2 = du2 + _dot(dup, win_ref[r + FF_PAIRS])
        dx1_n, dgffn = _rms_bwd(du2, n2, r2, g_ffn_v)
        dgffn_ref[...] += dgffn
        dx1_ref[...] = dx2 + dx1_n

    tile = lambda w: pl.BlockSpec((TM, w), lambda i: (i, 0))
    shards = lambda n: pl.BlockSpec((n, TM, FF_SHARD), lambda i: (0, i, 0))
    acc = lambda w: pl.BlockSpec((1, w), lambda i: (0, 0))
    return pl.pallas_call(
        body, name="mix_ffn_fwd_bwd", grid=(TOK // TM,),
        in_specs=[tile(D_MODEL), tile(SB_WIDTH), tile(DIL_OUT), tile(2 * D_MODEL),
                  _resident((SB_WIDTH, D_MODEL)), _resident((DIL_OUT, D_MODEL)), _resident((D_MODEL, D_MODEL)),
                  tile(D_MODEL), _resident((1, D_MODEL)), _resident((1, D_MODEL)),
                  _resident((N_DEV, FF_SHARD, D_MODEL)), _resident((D_FF, D_MODEL))],
        out_specs=[acc(LANES), tile(D_MODEL), tile(D_MODEL), tile(D_MODEL), shards(FF_PAIRS), shards(N_DEV),
                   tile(D_MODEL), acc(D_MODEL), acc(D_MODEL)],
        out_shape=[jax.ShapeDtypeStruct((1, LANES), F32),
                   jax.ShapeDtypeStruct((TOK, D_MODEL), F32),
                   jax.ShapeDtypeStruct((TOK, D_MODEL), BF16),
                   jax.ShapeDtypeStruct((TOK, D_MODEL), BF16),
                   jax.ShapeDtypeStruct((FF_PAIRS, TOK, FF_SHARD), BF16),
                   jax.ShapeDtypeStruct((N_DEV, TOK, FF_SHARD), BF16),
                   jax.ShapeDtypeStruct((TOK, D_MODEL), BF16),
                   jax.ShapeDtypeStruct((1, D_MODEL), F32),
                   jax.ShapeDtypeStruct((1, D_MODEL), F32)],
        scratch_shapes=[pltpu.VMEM((N_DEV, TM, FF_SHARD), F32)],
        compiler_params=_params(("arbitrary",)),
    )(x, o_sb, o_dl, gates, w_sb_up, w_dil_up, w_out, target, g_ffn, g_fin, w_ffn_in, w_ffn_out)


def _mix_bwd(dx1, o_sb, o_dl, gates, w_sb_up, w_dil_up, w_out):
    def body(dx1_ref, osb_ref, odl_ref, gate_ref, wsb_ref, wdl_ref, wout_ref,
             dgate_ref, dysb_ref, dydl_ref, dosb_ref, dodl_ref, dsum_ref):
        dmerged = _dot_nt(dx1_ref[...].astype(BF16), wout_ref[...])
        o_dl = odl_ref[...]
        y_sb = _dot(osb_ref[...], wsb_ref[...])
        y_dl = _dot(o_dl.astype(BF16), wdl_ref[...])
        s_sb = _sigmoid(gate_ref[:, :D_MODEL])
        s_dl = _sigmoid(gate_ref[:, D_MODEL:])
        dgate_ref[:, :D_MODEL] = (dmerged * y_sb * (s_sb * (1.0 - s_sb))).astype(BF16)
        dgate_ref[:, D_MODEL:] = (dmerged * y_dl * (s_dl * (1.0 - s_dl))).astype(BF16)
        dy_sb = (dmerged * s_sb).astype(BF16)
        dy_dl = (dmerged * s_dl).astype(BF16)
        dysb_ref[...] = dy_sb
        dydl_ref[...] = dy_dl
        dosb_ref[...] = _dot_nt(dy_sb, wsb_ref[...]).astype(BF16)
        do_dl = _dot_nt(dy_dl, wdl_ref[...])
        dodl_ref[...] = do_dl
        row = lax.broadcasted_iota(jnp.int32, (DIL_OUT, DIL_OUT), 0) // HEAD_DIM
        col = lax.broadcasted_iota(jnp.int32, (DIL_OUT, DIL_OUT), 1) // HEAD_DIM
        same_head = (row == col).astype(BF16)
        hi, lo = _split_bf16(do_dl * o_dl)
        dsum_ref[...] = _dot(hi, same_head) + _dot(lo, same_head)

    tile = lambda w: pl.BlockSpec((TM, w), lambda i: (i, 0))
    return pl.pallas_call(
        body, name="mix_bwd", grid=(TOK // TM,),
        in_specs=[tile(D_MODEL), tile(SB_WIDTH), tile(DIL_OUT), tile(2 * D_MODEL),
                  _resident((SB_WIDTH, D_MODEL)), _resident((DIL_OUT, D_MODEL)),
                  _resident((D_MODEL, D_MODEL))],
        out_specs=[tile(2 * D_MODEL), tile(D_MODEL), tile(D_MODEL), tile(SB_WIDTH), tile(DIL_OUT),
                   tile(DIL_OUT)],
        out_shape=[jax.ShapeDtypeStruct((TOK, 2 * D_MODEL), BF16),
                   jax.ShapeDtypeStruct((TOK, D_MODEL), BF16),
                   jax.ShapeDtypeStruct((TOK, D_MODEL), BF16),
                   jax.ShapeDtypeStruct((TOK, SB_WIDTH), BF16),
                   jax.ShapeDtypeStruct((TOK, DIL_OUT), F32),
                   jax.ShapeDtypeStruct((TOK, DIL_OUT), F32)],
        compiler_params=_params(("parallel",)),
    )(dx1, o_sb, o_dl, gates, w_sb_up, w_dil_up, w_out)


def _proj_bwd(dproj, dx1, x, g, w_in_t, send=None):
    widths = [p.shape[1] for p in dproj]

    def body(*refs):
        dx1_ref, x_ref, g_ref, w_ref, dx_ref, dg_ref = refs[len(widths):]

        @pl.when(pl.program_id(0) == 0)
        def _():
            dg_ref[...] = jnp.zeros_like(dg_ref)

        du = jnp.zeros((TM, D_MODEL), F32)
        c0 = 0
        for dp_ref, w in zip(refs, widths):
            du = du + _dot(dp_ref[...].astype(BF16), w_ref[c0:c0 + w, :])
            c0 += w
        g_v = g_ref[...]
        n, r, _ = _rms_fwd(x_ref[...], g_v)
        dx, dg = _rms_bwd(du, n, r, g_v)
        dg_ref[...] += dg
        dx_ref[...] = dx1_ref[...] + dx

    tile = lambda w: pl.BlockSpec((TM, w), lambda i: (i, 0))
    return _call(
        body, send, name="proj_bwd", grid=(TOK // TM,),
        in_specs=[tile(w) for w in widths] + [tile(D_MODEL), tile(D_MODEL), _resident((1, D_MODEL)),
                                              _resident((IN_WIDTH, D_MODEL))],
        out_specs=[tile(D_MODEL), pl.BlockSpec((1, D_MODEL), lambda i: (0, 0))],
        out_shape=[jax.ShapeDtypeStruct((TOK, D_MODEL), F32),
                   jax.ShapeDtypeStruct((1, D_MODEL), F32)],
        scratch_shapes=[], semantics=("arbitrary",), operands=(*dproj, dx1, x, g, w_in_t))


def _atb_pieces(a, pieces, name, tm, tk=512):
    m = a.shape[1]
    widths = [p.shape[1] for p in pieces]
    n = sum(widths)
    nk = TOK // tk

    def body(a_ref, *refs):
        o_ref, acc_ref = refs[len(widths):]
        k = pl.program_id(1)

        @pl.when(k == 0)
        def _():
            acc_ref[...] = jnp.zeros_like(acc_ref)

        a_v = a_ref[...]
        c0 = 0
        for p_ref, w in zip(refs, widths):
            acc_ref[:, c0:c0 + w] += _dot_tn(a_v, p_ref[...].astype(BF16))
            c0 += w

        @pl.when(k == nk - 1)
        def _():
            for c0, w in _chunks(n):
                o_ref[c0:c0 + w, :] = acc_ref[:, c0:c0 + w].T.astype(BF16)

    return pl.pallas_call(
        body, name=name, grid=(m // tm, nk),
        in_specs=[pl.BlockSpec((tk, tm), lambda i, k: (k, i))]
                 + [pl.BlockSpec((tk, w), lambda i, k: (k, 0)) for w in widths],
        out_specs=pl.BlockSpec((n, tm), lambda i, k: (0, i)),
        out_shape=jax.ShapeDtypeStruct((n, m), BF16),
        scratch_shapes=[pltpu.VMEM((tm, n), F32)],
        compiler_params=_params(("parallel", "arbitrary")),
    )(a, *pieces)


def _atb_shards(a, b, name, tk=512):
    a_sharded = a.ndim == 3
    n, _, w = a.shape if a_sharded else b.shape
    other = (b if a_sharded else a).shape[1]

    def body(a_ref, b_ref, o_ref):
        acc = jnp.zeros(o_ref.shape, F32)
        for k0 in range(0, TOK, tk):
            acc = acc + _dot_tn(a_ref[k0:k0 + tk, :], b_ref[k0:k0 + tk, :])
        o_ref[...] = acc.astype(BF16)

    shard = pl.BlockSpec((None, TOK, w), lambda r: (r, 0, 0))
    whole = _resident((TOK, other))
    if a_sharded:
        out_spec, out_shape = pl.BlockSpec((w, other), lambda r: (r, 0)), (n * w, other)
    else:
        out_spec, out_shape = pl.BlockSpec((None, other, w), lambda r: (r, 0, 0)), (n, other, w)
    return pl.pallas_call(
        body, name=name, grid=(n,),
        in_specs=[shard, whole] if a_sharded else [whole, shard],
        out_specs=out_spec, out_shape=jax.ShapeDtypeStruct(out_shape, BF16),
        compiler_params=_params(("parallel",)),
    )(a, b)


def _atb(a, b, name, tm, tn, col_blocks=0, tk=512):
    m, n = a.shape[1], b.shape[1]
    nk = TOK // tk

    def body(a_ref, b_ref, o_ref, acc_ref):
        k = pl.program_id(2)

        @pl.when(k == 0)
        def _():
            acc_ref[...] = jnp.zeros_like(acc_ref)

        acc_ref[...] += _dot_tn(a_ref[...].astype(BF16), b_ref[...].astype(BF16))

        @pl.when(k == nk - 1)
        def _():
            if col_blocks:
                width = n // col_blocks
                for blk in range(col_blocks):
                    o_ref[blk] = acc_ref[:, blk * width:(blk + 1) * width].astype(BF16)
            else:
                o_ref[...] = acc_ref[...].astype(BF16)

    if col_blocks:
        out_spec = pl.BlockSpec((col_blocks, tm, n // col_blocks), lambda i, j, k: (0, i, 0))
        out_shape = jax.ShapeDtypeStruct((col_blocks, m, n // col_blocks), BF16)
    else:
        out_spec = pl.BlockSpec((tm, tn), lambda i, j, k: (i, j))
        out_shape = jax.ShapeDtypeStruct((m, n), BF16)
    return pl.pallas_call(
        body, name=name, grid=(m // tm, n // tn, nk),
        in_specs=[pl.BlockSpec((tk, tm), lambda i, j, k: (k, i)),
                  pl.BlockSpec((tk, tn), lambda i, j, k: (k, j))],
        out_specs=out_spec, out_shape=out_shape,
        scratch_shapes=[pltpu.VMEM((tm, tn), F32)],
        compiler_params=_params(("parallel", "parallel", "arbitrary")),
    )(a, b)


SB_PAIRS = SB_WIDTH // LANES


def _two_heads(v, lane0):
    zero = jnp.zeros_like(v)
    return jnp.where(lane0, v, zero), jnp.where(lane0, zero, v)


SB_QBLK = 256
N_SB_STEPS = SEQ // SB_QBLK


SB_KCHUNK = 2 * BLK
SB_ROWS = 2 * SB_QBLK
SB_DEAD = -104.0


def _log_keep(z):
    neg_z = -z
    return jnp.minimum(neg_z, 0.0) - jnp.log(1.0 + jnp.exp(jnp.minimum(z, neg_z)))


def _stack_heads(v, lane0):
    return jnp.concatenate(_two_heads(v, lane0), axis=0)


def _block_sums(v, tri, split=True):
    halves = (v[:, :BLK], v[:, BLK:])
    stacked = jnp.concatenate(halves, axis=0)
    if split:
        hi, lo = _split_bf16(stacked)
        prod = _dot(jnp.concatenate([hi, lo], axis=0), tri)
        tri_sum = prod[:2 * SB_ROWS] + prod[2 * SB_ROWS:]
    else:
        tri_sum = _dot(stacked.astype(BF16), tri)
    sums = tuple(jnp.sum(h, axis=1, keepdims=True) for h in halves)
    return (tri_sum[:SB_ROWS], tri_sum[SB_ROWS:]), sums


def _sb_diag_mask():
    row = lax.broadcasted_iota(jnp.int32, (SB_ROWS, SB_KCHUNK), 0)
    col = lax.broadcasted_iota(jnp.int32, (SB_ROWS, SB_KCHUNK), 1)
    return col < jnp.where(row >= SB_QBLK, row - SB_QBLK, row)


def _sb_fwd(qkv, send=None):
    def body(q_ref, k_ref, v_ref, o_ref):
        i = pl.program_id(2)
        krow = lax.broadcasted_iota(jnp.int32, (BLK, BLK), 0)
        kcol = lax.broadcasted_iota(jnp.int32, (BLK, BLK), 1)
        later = (krow > kcol).astype(BF16)
        lane0 = lax.broadcasted_iota(jnp.int32, (SB_QBLK, LANES), 1) < HEAD_DIM
        q2 = _stack_heads(q_ref[0] * QK_SCALE, lane0)

        def chunk(c, carry, causal):
            acc, run = carry
            off = pl.multiple_of(c * SB_KCHUNK, SB_KCHUNK)
            z = _dot_nt(q2, k_ref[0, pl.ds(off, SB_KCHUNK), :])
            log_keep = _log_keep(z)
            if causal is not None:
                log_keep = jnp.where(causal, log_keep, 0.0)
            suffix, sums = _block_sums(log_keep, later)
            log_after = jnp.concatenate([suffix[0] + (run + sums[1]), suffix[1] + run], axis=1)
            a = jnp.exp(log_keep + z + log_after)
            if causal is not None:
                a = jnp.where(causal, a, 0.0)
            acc = acc + _dot(a.astype(BF16), v_ref[0, pl.ds(off, SB_KCHUNK), :])
            return acc, run + (sums[0] + sums[1])

        acc, run = chunk(i, (jnp.zeros((SB_ROWS, LANES), F32), jnp.zeros((SB_ROWS, 1), F32)), _sb_diag_mask())

        def some_alive(run):
            return (jnp.max(run) > SB_DEAD).astype(jnp.int32)

        def trip(state):
            t, _, acc, run = state
            acc, run = chunk(i - 1 - t, (acc, run), None)
            return t + 1, some_alive(run), acc, run

        _, _, acc, _ = lax.while_loop(lambda s: jnp.logical_and(s[0] < i, s[1] > 0), trip,
                                      (jnp.int32(0), some_alive(run), acc, run))
        o_ref[0] = jnp.where(lane0, acc[:SB_QBLK], acc[SB_QBLK:]).astype(BF16)

    blk = pl.BlockSpec((1, SB_QBLK, LANES), lambda b, h, i: (b, i, h))
    return _call(
        body, send, name="sb_fwd", grid=(B_LOC, SB_PAIRS, N_SB_STEPS),
        in_specs=[blk,
                  pl.BlockSpec((1, SEQ, LANES), lambda b, h, i: (b, 0, SB_PAIRS + h)),
                  pl.BlockSpec((1, SEQ, LANES), lambda b, h, i: (b, 0, 2 * SB_PAIRS + h))],
        out_specs=[blk], out_shape=[jax.ShapeDtypeStruct((B_LOC, SEQ, SB_WIDTH), BF16)],
        scratch_shapes=[], semantics=("parallel", "parallel", "arbitrary"), operands=(qkv, qkv, qkv))


def _sb_bwd(qkv, d_o, send=None):
    def body(q_ref, k_ref, v_ref, do_ref, dq_ref, dk_ref, dv_ref, dk_acc, dv_acc, z_scr, keep_scr):
        i = pl.program_id(2)
        krow = lax.broadcasted_iota(jnp.int32, (BLK, BLK), 0)
        kcol = lax.broadcasted_iota(jnp.int32, (BLK, BLK), 1)
        upto = (krow <= kcol).astype(BF16)
        earlier = (krow < kcol).astype(BF16)
        lane0 = lax.broadcasted_iota(jnp.int32, (SB_QBLK, LANES), 1) < HEAD_DIM
        q2 = _stack_heads(q_ref[0] * QK_SCALE, lane0)
        do2 = _stack_heads(do_ref[0], lane0)

        def keep_sum(c, causal):
            off = pl.multiple_of(c * SB_KCHUNK, SB_KCHUNK)
            z = _dot_nt(q2, k_ref[0, pl.ds(off, SB_KCHUNK), :])
            log_keep = _log_keep(z)
            if causal is not None:
                log_keep = jnp.where(causal, log_keep, 0.0)
            z_scr[c] = z
            keep_scr[c] = log_keep
            return jnp.sum(log_keep, axis=1, keepdims=True)

        def some_alive(run):
            return (jnp.max(run) > SB_DEAD).astype(jnp.int32)

        def scan(state):
            t, _, run = state
            run = run + keep_sum(i - 1 - t, None)
            return t + 1, some_alive(run), run

        diag_sum = keep_sum(i, _sb_diag_mask())
        walked, _, tot2 = lax.while_loop(lambda s: jnp.logical_and(s[0] < i, s[1] > 0), scan,
                                         (jnp.int32(0), some_alive(diag_sum), diag_sum))
        first = i - walked

        @pl.when(i == 0)
        def _():
            dk_acc[...] = jnp.zeros_like(dk_acc)
            dv_acc[...] = jnp.zeros_like(dv_acc)

        def chunk(c, carry, causal):
            dq, pre_keep, pre_e = carry
            off = pl.multiple_of(c * SB_KCHUNK, SB_KCHUNK)
            k_c = k_ref[0, pl.ds(off, SB_KCHUNK), :]
            v_c = v_ref[0, pl.ds(off, SB_KCHUNK), :]
            d_a = _dot_nt(do2, v_c)
            log_keep = keep_scr[c]
            log_beta = log_keep + z_scr[c]
            prefix, sums = _block_sums(log_keep, upto)
            inclusive = jnp.concatenate([prefix[0], prefix[1] + sums[0]], axis=1)
            a = jnp.exp(log_beta + ((tot2 - pre_keep) - inclusive))
            if causal is not None:
                a = jnp.where(causal, a, 0.0)
            e = d_a * a
            e_prefix, e_sums = _block_sums(e, earlier, split=False)
            before = jnp.concatenate([e_prefix[0] + pre_e, e_prefix[1] + (pre_e + e_sums[0])], axis=1)
            dz = e - (e + before) * jnp.exp(log_beta)
            if causal is not None:
                dz = jnp.where(causal, dz, 0.0)
            dz = dz.astype(BF16)
            dq = dq + _dot(dz, k_c)
            dk_acc[pl.ds(off, SB_KCHUNK), :] += _dot_tn(dz, q2)
            dv_acc[pl.ds(off, SB_KCHUNK), :] += _dot_tn(a.astype(BF16), do2)
            return dq, pre_keep + (sums[0] + sums[1]), pre_e + (e_sums[0] + e_sums[1])

        zero_col = jnp.zeros((SB_ROWS, 1), F32)
        carry = lax.fori_loop(first, i, lambda t, c: chunk(t, c, None),
                              (jnp.zeros((SB_ROWS, LANES), F32), zero_col, zero_col))
        dq, _, _ = chunk(i, carry, _sb_diag_mask())
        dq_ref[0] = (jnp.where(lane0, dq[:SB_QBLK], dq[SB_QBLK:]) * QK_SCALE).astype(BF16)

        @pl.when(i == N_SB_STEPS - 1)
        def _():
            dk_ref[0] = dk_acc[...].astype(BF16)
            dv_ref[0] = dv_acc[...].astype(BF16)

    blk = pl.BlockSpec((1, SB_QBLK, LANES), lambda b, h, i: (b, i, h))
    whole = lambda c: pl.BlockSpec((1, SEQ, LANES), lambda b, h, i: (b, 0, c * SB_PAIRS + h))
    out = jax.ShapeDtypeStruct((B_LOC, SEQ, SB_WIDTH), BF16)
    return _call(
        body, send, name="sb_bwd", grid=(B_LOC, SB_PAIRS, N_SB_STEPS),
        in_specs=[blk, whole(1), whole(2), blk],
        out_specs=[blk, whole(0), whole(0)],
        out_shape=[out, out, out],
        scratch_shapes=[pltpu.VMEM((SEQ, LANES), F32), pltpu.VMEM((SEQ, LANES), F32),
                        pltpu.VMEM((N_SB_STEPS, SB_ROWS, SB_KCHUNK), F32),
                        pltpu.VMEM((N_SB_STEPS, SB_ROWS, SB_KCHUNK), F32)],
        semantics=("parallel", "parallel", "arbitrary"), operands=(qkv, qkv, qkv, d_o))


DIL_GROUPS = len(DIL_PAIRS)
DIL_QBLOCKS = SEQ // BLK


def _residue_rows(j, dilation):
    length = SEQ // dilation
    return pl.ds(j, length, stride=dilation) if dilation > 1 else pl.ds(0, length)


def _gather_residues(src_ref, dst_ref, dst_off, dilation, scale=None):
    length = SEQ // dilation
    for j in range(dilation):
        v = src_ref[_residue_rows(j, dilation), :]
        if scale is not None:
            v = v * scale
        dst_ref[dst_off + j * length:dst_off + (j + 1) * length, :] = v.astype(dst_ref.dtype)


def _scatter_residues(src_ref, src_off, dst_ref, dilation):
    length = SEQ // dilation
    for j in range(dilation):
        dst_ref[_residue_rows(j, dilation), :] = (
            src_ref[src_off + j * length:src_off + (j + 1) * length, :].astype(dst_ref.dtype))


def _dil_geometry(group, pair):
    dilation = DIL_PAIRS[group][1]
    row = lax.broadcasted_iota(jnp.int32, (2 * BLK, 2 * BLK), 0)
    col = lax.broadcasted_iota(jnp.int32, (2 * BLK, 2 * BLK), 1)
    second = row >= BLK
    steps = BLK + jnp.where(second, row - BLK, row) - col
    coef = -ALIBI_MAX_BIAS / DIL_HEADS * math.log(2.0)
    first_head = float(4 * group + 1) + 2.0 * pair.astype(F32)
    slope = jnp.exp(coef * (first_head + jnp.where(second, 1.0, 0.0)))
    bias = slope * (steps * dilation).astype(F32)
    valid = jnp.logical_and(steps >= 0, steps <= BLK)
    return bias, valid, col >= BLK


def _dil_tile_scores(q2, kk, geometry, has_prev):
    bias, valid, own = geometry
    ok = jnp.logical_and(valid, jnp.logical_or(own, has_prev))
    return jnp.where(ok, _dot_nt(q2, kk) - bias, NEG_BIG)


def _head_col(v, lane_mask):
    return jnp.max(jnp.where(lane_mask, v, NEG_BIG), axis=1, keepdims=True)


def _dil_fwd(qkv, send=None):
    def body(*refs):
        ins, (o_ref, lse_ref), (qs, ks, vs, o_res, lse_res) = refs[:9], refs[9:11], refs[11:16]
        o_grp, lse_grp = refs[16:19], refs[19:22]
        pair = pl.program_id(1)
        lane0 = lax.broadcasted_iota(jnp.int32, (BLK, LANES), 1) < HEAD_DIM
        ks[0:BLK, :] = jnp.zeros((BLK, LANES), BF16)
        vs[0:BLK, :] = jnp.zeros((BLK, LANES), BF16)
        for grp, (_, dilation) in enumerate(DIL_PAIRS):
            q_ref, k_ref, v_ref = ins[3 * grp:3 * grp + 3]
            per_residue = DIL_QBLOCKS // dilation
            _gather_residues(q_ref, qs, 0, dilation, QK_SCALE)
            _gather_residues(k_ref, ks, BLK, dilation)
            _gather_residues(v_ref, vs, BLK, dilation)
            geometry = _dil_geometry(grp, pair)

            def step(blk, _):
                off = pl.multiple_of(blk * BLK, BLK)
                q2 = _stack_heads(qs[pl.ds(off, BLK), :], lane0)
                s = _dil_tile_scores(q2, ks[pl.ds(off, 2 * BLK), :], geometry, blk % per_residue != 0)
                m = jnp.max(s, axis=1, keepdims=True)
                p = jnp.exp(s - m)
                den = jnp.sum(p, axis=1, keepdims=True)
                out = _dot(p.astype(BF16), vs[pl.ds(off, 2 * BLK), :]) / den
                lse = m + jnp.log(den)
                o_res[pl.ds(off, BLK), :] = jnp.where(lane0, out[:BLK], out[BLK:])
                lse_res[pl.ds(off, BLK), :] = jnp.where(lane0, lse[:BLK], lse[BLK:])
                return 0

            lax.fori_loop(0, DIL_QBLOCKS, step, 0, unroll=8)
            _scatter_residues(o_res, 0, o_grp[grp], dilation)
            _scatter_residues(lse_res, 0, lse_grp[grp], dilation)

        for r0 in range(0, SEQ, 2 * BLK):
            rows = slice(r0, r0 + 2 * BLK)
            ls = [lse_grp[g][rows, :] for g in range(DIL_GROUPS)]
            m = jnp.maximum(jnp.maximum(ls[0], ls[1]), ls[2])
            w = [jnp.exp(l - m) for l in ls]
            den = w[0] + w[1] + w[2]
            o_ref[rows, :] = (w[0] * o_grp[0][rows, :] + w[1] * o_grp[1][rows, :] + w[2] * o_grp[2][rows, :]) / den
            lse_ref[rows, :] = m + jnp.log(den)

    def col(part, grp):
        return pl.BlockSpec((None, SEQ, LANES), lambda b, p: (b, 0, 6 * part + 2 * grp + p))

    out_spec = pl.BlockSpec((None, SEQ, LANES), lambda b, p: (b, 0, p))
    out = jax.ShapeDtypeStruct((B_LOC, SEQ, DIL_OUT), F32)
    return _call(
        body, send, name="dil_fwd", grid=(B_LOC, DIL_OUT // LANES),
        in_specs=[col(part, grp) for grp in range(DIL_GROUPS) for part in range(3)],
        out_specs=[out_spec, out_spec], out_shape=[out, out],
        scratch_shapes=[pltpu.VMEM((SEQ, LANES), BF16), pltpu.VMEM((SEQ + BLK, LANES), BF16),
                        pltpu.VMEM((SEQ + BLK, LANES), BF16), pltpu.VMEM((SEQ, LANES), F32),
                        pltpu.VMEM((SEQ, LANES), F32)] + [pltpu.VMEM((SEQ, LANES), F32)] * (2 * DIL_GROUPS),
        semantics=("parallel", "parallel"), operands=[qkv] * 9)


def _dil_bwd(qkv, d_o, lse, dsum, send=None):
    def body(*refs):
        ins, (do_ref, lse_ref, dsum_ref), outs = refs[:9], refs[9:12], refs[12:21]
        qs, ks, vs, dos, lse_res, dsum_res, dq_res, dk_acc, dv_acc = refs[21:]
        pair = pl.program_id(1)
        lane0 = lax.broadcasted_iota(jnp.int32, (BLK, LANES), 1) < HEAD_DIM
        lane1 = jnp.logical_not(lane0)
        ks[0:BLK, :] = jnp.zeros((BLK, LANES), BF16)
        vs[0:BLK, :] = jnp.zeros((BLK, LANES), BF16)
        for grp, (_, dilation) in enumerate(DIL_PAIRS):
            q_ref, k_ref, v_ref = ins[3 * grp:3 * grp + 3]
            dq_ref, dk_ref, dv_ref = outs[3 * grp:3 * grp + 3]
            per_residue = DIL_QBLOCKS // dilation
            _gather_residues(q_ref, qs, 0, dilation, QK_SCALE)
            _gather_residues(k_ref, ks, BLK, dilation)
            _gather_residues(v_ref, vs, BLK, dilation)
            _gather_residues(do_ref, dos, 0, dilation)
            _gather_residues(lse_ref, lse_res, 0, dilation)
            _gather_residues(dsum_ref, dsum_res, 0, dilation)
            dk_acc[...] = jnp.zeros_like(dk_acc)
            dv_acc[...] = jnp.zeros_like(dv_acc)
            geometry = _dil_geometry(grp, pair)

            def step(blk, _):
                off = pl.multiple_of(blk * BLK, BLK)
                q2 = _stack_heads(qs[pl.ds(off, BLK), :], lane0)
                do2 = _stack_heads(dos[pl.ds(off, BLK), :], lane0)
                kk = ks[pl.ds(off, 2 * BLK), :]
                vv = vs[pl.ds(off, 2 * BLK), :]
                lse_blk = lse_res[pl.ds(off, BLK), :]
                dsum_blk = dsum_res[pl.ds(off, BLK), :]
                lse2 = jnp.concatenate([_head_col(lse_blk, lane0), _head_col(lse_blk, lane1)], axis=0)
                dsum2 = jnp.concatenate([_head_col(dsum_blk, lane0), _head_col(dsum_blk, lane1)], axis=0)
                s = _dil_tile_scores(q2, kk, geometry, blk % per_residue != 0)
                p = jnp.exp(s - lse2)
                ds = (p * (_dot_nt(do2, vv) - dsum2)).astype(BF16)
                dq2 = _dot(ds, kk)
                dq_res[pl.ds(off, BLK), :] = jnp.where(lane0, dq2[:BLK], dq2[BLK:]) * QK_SCALE
                dk_acc[pl.ds(off, 2 * BLK), :] += _dot_tn(ds, q2)
                dv_acc[pl.ds(off, 2 * BLK), :] += _dot_tn(p.astype(BF16), do2)
                return 0

            lax.fori_loop(0, DIL_QBLOCKS, step, 0, unroll=8)
            _scatter_residues(dq_res, 0, dq_ref, dilation)
            _scatter_residues(dk_acc, BLK, dk_ref, dilation)
            _scatter_residues(dv_acc, BLK, dv_ref, dilation)

    def col(part, grp):
        return pl.BlockSpec((None, SEQ, LANES), lambda b, p: (b, 0, 6 * part + 2 * grp + p))

    slot = pl.BlockSpec((None, SEQ, LANES), lambda b, p: (b, 0, p))
    out = jax.ShapeDtypeStruct((B_LOC, SEQ, DIL_OUT), F32)
    return _call(
        body, send, name="dil_bwd", grid=(B_LOC, DIL_OUT // LANES),
        in_specs=[col(part, grp) for grp in range(DIL_GROUPS) for part in range(3)] + [slot] * 3,
        out_specs=[slot] * 9, out_shape=[out] * 9,
        scratch_shapes=[pltpu.VMEM((SEQ, LANES), BF16), pltpu.VMEM((SEQ + BLK, LANES), BF16),
                        pltpu.VMEM((SEQ + BLK, LANES), BF16), pltpu.VMEM((SEQ, LANES), BF16),
                        pltpu.VMEM((SEQ, LANES), F32), pltpu.VMEM((SEQ, LANES), F32),
                        pltpu.VMEM((SEQ, LANES), F32), pltpu.VMEM((SEQ + BLK, LANES), F32),
                        pltpu.VMEM((SEQ + BLK, LANES), F32)],
        semantics=("parallel", "parallel"), operands=[qkv] * 9 + [d_o, lse, dsum])


def _peers():
    x, y, c = lax.axis_index("x"), lax.axis_index("y"), lax.axis_index("c")
    me = 4 * x + 2 * y + c
    peers = []
    for mask in range(1, N_DEV):
        px = 1 - x if mask & 4 else x
        py = 1 - y if mask & 2 else y
        pc = 1 - c if mask & 1 else c
        peers.append(((px, py, pc), 4 * px + 2 * py + pc))
    return me, peers


def _all_gather(shard, name):
    def body(src_ref, out_ref, send_sems, recv_sems, local_sem):
        x, y, c = lax.axis_index("x"), lax.axis_index("y"), lax.axis_index("c")
        sibling = (x, y, 1 - c)
        chips = [(1 - x, y), (x, 1 - y), (1 - x, 1 - y)]

        def slot(px, py, pc):
            return out_ref.at[4 * px + 2 * py + pc]

        def copy(k, block, to, src=None):
            return pltpu.make_async_remote_copy(
                src_ref=slot(*block) if src is None else src, dst_ref=slot(*block),
                send_sem=send_sems.at[k], recv_sem=recv_sems.at[k], device_id=to,
                device_id_type=pl.DeviceIdType.MESH)

        mine = pltpu.make_async_copy(src_ref, slot(x, y, c), local_sem)
        mine.start()
        first = [copy(0, (x, y, c), sibling, src=src_ref)]
        first += [copy(1 + j, (x, y, c), (*chip, c), src=src_ref) for j, chip in enumerate(chips)]
        for cp in first:
            cp.start()
        passed = [copy(4 + j, (*chip, c), sibling) for j, chip in enumerate(chips)]
        for j, chip in enumerate(chips):
            copy(1 + j, (*chip, c), (x, y, c)).wait_recv()
            passed[j].start()
        copy(0, sibling, (x, y, c)).wait_recv()
        for j, chip in enumerate(chips):
            copy(4 + j, (*chip, 1 - c), (x, y, c)).wait_recv()
        for cp in first + passed:
            cp.wait_send()
        mine.wait()

    return pl.pallas_call(
        body, name=name,
        in_specs=[pl.BlockSpec(memory_space=pl.ANY)],
        out_specs=pl.BlockSpec(memory_space=pl.ANY),
        out_shape=jax.ShapeDtypeStruct((N_DEV,) + shard.shape, shard.dtype),
        scratch_shapes=[pltpu.SemaphoreType.DMA((N_DEV - 1,)), pltpu.SemaphoreType.DMA((N_DEV - 1,)),
                        pltpu.SemaphoreType.DMA],
    )(shard)


def _call(body, send, *, name, grid, in_specs, out_specs, out_shape, scratch_shapes, semantics, operands):
    if send is None:
        return pl.pallas_call(
            body, name=name, grid=grid, in_specs=in_specs, out_specs=out_specs, out_shape=out_shape,
            scratch_shapes=scratch_shapes, compiler_params=_params(semantics))(*operands), []
    srcs, kinds = [s for s, _ in send], [k for _, k in send]
    n, n_in, n_out, n_scr = len(srcs), len(in_specs), len(out_specs), len(scratch_shapes)
    steps = math.prod(grid)
    relay_step = (13 * steps) // 16

    def plan(refs):
        src_refs, land_refs = refs[n_in:n_in + n], refs[n_in + n + n_out:n_in + 2 * n + n_out]
        send_sems, recv_sems, local_sems = refs[-3:]
        x, y, c = lax.axis_index("x"), lax.axis_index("y"), lax.axis_index("c")
        me, peers = _peers()
        first, relayed_in, relayed_out, arrivals, sends, own = [], [], [], [], [], []
        for a, kind in enumerate(kinds):
            def copy(k, src, dst_slot, to):
                return pltpu.make_async_remote_copy(
                    src_ref=src, dst_ref=land_refs[a].at[dst_slot], send_sem=send_sems.at[a * (N_DEV - 1) + k],
                    recv_sem=recv_sems.at[a * (N_DEV - 1) + k], device_id=to, device_id_type=pl.DeviceIdType.MESH)

            if kind == "gather_by_chip":
                idx = lambda px, py, pc: 4 * px + 2 * py + pc
                chips = [(1 - x, y), (x, 1 - y), (1 - x, 1 - y)]
                mine = [copy(0, src_refs[a], me, (x, y, 1 - c))]
                arrivals.append(copy(0, src_refs[a], idx(x, y, 1 - c), (x, y, 1 - c)))
                for j, (px, py) in enumerate(chips):
                    mine.append(copy(1 + j, src_refs[a], me, (px, py, c)))
                    relayed_in.append(copy(1 + j, src_refs[a], idx(px, py, c), (px, py, c)))
                    relayed_out.append(copy(4 + j, land_refs[a].at[idx(px, py, c)], idx(px, py, c), (x, y, 1 - c)))
                    arrivals.append(copy(4 + j, src_refs[a], idx(px, py, 1 - c), (x, y, 1 - c)))
                first += mine
                sends += mine + relayed_out[-3:]
                own.append(pltpu.make_async_copy(src_refs[a], land_refs[a].at[me], local_sems.at[a]))
            elif kind == "scatter_by_chip":
                for k, (px, py) in enumerate([(1 - x, y), (x, 1 - y), (1 - x, 1 - y)]):
                    first.append(copy(k, src_refs[a].at[2 * px + py], 2 * x + y, (px, py, c)))
                    arrivals.append(copy(k, src_refs[a].at[2 * px + py], 2 * px + py, (px, py, c)))
                sends += first[-3:]
                own.append(pltpu.make_async_copy(src_refs[a].at[2 * x + y], land_refs[a].at[2 * x + y],
                                                 local_sems.at[a]))
            else:
                part = (lambda i: src_refs[a].at[i]) if kind == "scatter" else (lambda i: src_refs[a])
                for k, (peer, peer_idx) in enumerate(peers):
                    first.append(copy(k, part(peer_idx), me, peer))
                    arrivals.append(copy(k, part(peer_idx), peer_idx, peer))
                sends += first[-(N_DEV - 1):]
                own.append(pltpu.make_async_copy(part(me), land_refs[a].at[me], local_sems.at[a]))
        return first, relayed_in, relayed_out, arrivals, sends, own

    def wrapped(*refs):
        step = 0
        for axis, size in enumerate(grid):
            step = step * size + pl.program_id(axis)

        @pl.when(step == 0)
        def _():
            first, _, _, _, _, own = plan(refs)
            for cp in first + own:
                cp.start()

        if "gather_by_chip" in kinds:
            @pl.when(step == relay_step)
            def _():
                _, relayed_in, relayed_out, _, _, _ = plan(refs)
                for cp_in, cp_out in zip(relayed_in, relayed_out):
                    cp_in.wait_recv()
                    cp_out.start()

        body(*refs[:n_in], *refs[n_in + n:n_in + n + n_out], *refs[n_in + 2 * n + n_out:n_in + 2 * n + n_out + n_scr])

        @pl.when(step == steps - 1)
        def _():
            _, _, _, arrivals, sends, own = plan(refs)
            for cp in arrivals:
                cp.wait_recv()
            for cp in sends:
                cp.wait_send()
            for cp in own:
                cp.wait()

    anywhere = pl.BlockSpec(memory_space=pl.ANY)
    lands = [jax.ShapeDtypeStruct((N_DEV // 2 if k == "scatter_by_chip" else N_DEV,) + s.shape[-2:], s.dtype)
             for s, k in send]
    out = pl.pallas_call(
        wrapped, name=name, grid=grid,
        in_specs=list(in_specs) + [anywhere] * n, out_specs=list(out_specs) + [anywhere] * n,
        out_shape=list(out_shape) + lands,
        scratch_shapes=list(scratch_shapes) + [pltpu.SemaphoreType.DMA((n * (N_DEV - 1),)),
                                               pltpu.SemaphoreType.DMA((n * (N_DEV - 1),)),
                                               pltpu.SemaphoreType.DMA((n,))],
        compiler_params=_params(("arbitrary",) * len(grid)),
    )(*operands, *srcs)
    return out[:n_out], list(out[n_out:])


def _pair_swap(blocks):
    def body(src_ref, out_ref, send_sems, recv_sems):
        x, y, c = lax.axis_index("x"), lax.axis_index("y"), lax.axis_index("c")
        copies = [pltpu.make_async_remote_copy(
            src_ref=src_ref.at[2 * chip + (1 - c)], dst_ref=out_ref.at[chip], send_sem=send_sems.at[chip],
            recv_sem=recv_sems.at[chip], device_id=(x, y, 1 - c), device_id_type=pl.DeviceIdType.MESH)
            for chip in range(N_DEV // 2)]
        for cp in copies:
            cp.start()
        for cp in copies:
            cp.wait()

    return pl.pallas_call(
        body, name="pair_swap_grad_w_in",
        in_specs=[pl.BlockSpec(memory_space=pl.ANY)], out_specs=pl.BlockSpec(memory_space=pl.ANY),
        out_shape=jax.ShapeDtypeStruct((N_DEV // 2,) + blocks.shape[1:], blocks.dtype),
        scratch_shapes=[pltpu.SemaphoreType.DMA((N_DEV // 2,)), pltpu.SemaphoreType.DMA((N_DEV // 2,))],
    )(blocks)


def _pair_sum(blocks, swapped, core):
    _, rows, cols = swapped.shape
    tile_rows = _row_tile(rows)

    def body(core_ref, mine_ref, theirs_ref, o_ref):
        o_ref[...] = (mine_ref[...].astype(F32) + theirs_ref[...].astype(F32)).astype(o_ref.dtype)

    return pl.pallas_call(
        body, name="pair_sum_grad_w_in",
        grid_spec=pltpu.PrefetchScalarGridSpec(
            num_scalar_prefetch=1, grid=(N_DEV // 2, rows // tile_rows),
            in_specs=[pl.BlockSpec((None, tile_rows, cols), lambda j, i, core_ref: (2 * j + core_ref[0], i, 0)),
                      pl.BlockSpec((None, tile_rows, cols), lambda j, i, core_ref: (j, i, 0))],
            out_specs=pl.BlockSpec((None, tile_rows, cols), lambda j, i, core_ref: (j, i, 0))),
        out_shape=jax.ShapeDtypeStruct(swapped.shape, swapped.dtype),
        compiler_params=_params(("parallel", "parallel")),
    )(core, blocks, swapped)


def _sum_in_device_order(land_ref):
    acc = land_ref[0].astype(F32)
    for j in range(1, land_ref.shape[0]):
        acc = acc + land_ref[j].astype(F32)
    return acc


def _adam_math(w, g, m, v):
    c1 = 1.0 - ADAM_B1 ** ADAM_STEP
    c2 = 1.0 - ADAM_B2 ** ADAM_STEP
    m_new = ADAM_B1 * m + (1.0 - ADAM_B1) * g
    v_new = ADAM_B2 * v + (1.0 - ADAM_B2) * (g * g)
    delta = -ADAM_LR * ((m_new / c1) / (jnp.sqrt(v_new / c2) + ADAM_EPS) + ADAM_WD * w)
    return delta, m_new, v_new


def _row_tile(rows):
    return max(t for t in range(16, 385, 16) if rows % t == 0) if rows % 16 == 0 else rows


def _sum_update(land, w, m, v, name):
    slots, rows, cols = land.shape
    tile_rows = _row_tile(rows)

    def body(land_ref, w_ref, m_ref, v_ref, g_ref, d_ref, nm_ref, nv_ref):
        g = _sum_in_device_order(land_ref)
        g_ref[...] = g
        d_ref[...], nm_ref[...], nv_ref[...] = _adam_math(w_ref[...], g, m_ref[...], v_ref[...])

    tile = pl.BlockSpec((None, tile_rows, cols), lambda i: (0, i, 0))
    out = jax.ShapeDtypeStruct((1, rows, cols), F32)
    return pl.pallas_call(
        body, name=name, grid=(rows // tile_rows,),
        in_specs=[pl.BlockSpec((slots, tile_rows, cols), lambda i: (0, i, 0)), tile, tile, tile],
        out_specs=[tile] * 4, out_shape=[out] * 4,
        compiler_params=_params(("parallel",)),
    )(land, w, m, v)


def _to_bf16(arrays):
    def body(*refs):
        for src_ref, dst_ref in zip(refs[:len(arrays)], refs[len(arrays):]):
            dst_ref[...] = src_ref[...].astype(BF16)

    whole = [pl.BlockSpec(a.shape, lambda i: (0, 0)) for a in arrays]
    return pl.pallas_call(
        body, name="shards_to_bf16", grid=(1,), in_specs=whole, out_specs=whole,
        out_shape=[jax.ShapeDtypeStruct(a.shape, BF16) for a in arrays],
        compiler_params=_params(("arbitrary",)),
    )(*arrays)


N_GAINS = 3


def _sum_update_gains(land, ws, ms, vs):
    def body(land_ref, *refs):
        w_refs, m_refs, v_refs = (refs[k * N_GAINS:(k + 1) * N_GAINS] for k in range(3))
        loss_ref, out_refs = refs[3 * N_GAINS], refs[3 * N_GAINS + 1:]
        rows = _sum_in_device_order(land_ref)
        loss_ref[...] = rows[N_GAINS:N_GAINS + 1]
        for k in range(N_GAINS):
            g = rows[k:k + 1]
            g_ref, d_ref, nm_ref, nv_ref = out_refs[4 * k:4 * k + 4]
            g_ref[...] = g
            d_ref[...], nm_ref[...], nv_ref[...] = _adam_math(w_refs[k][...], g, m_refs[k][...], v_refs[k][...])

    row = pl.BlockSpec((1, D_MODEL), lambda i: (0, 0))
    out = jax.ShapeDtypeStruct((1, D_MODEL), F32)
    n_out = 1 + 4 * N_GAINS
    return pl.pallas_call(
        body, name="update_gains", grid=(1,),
        in_specs=[pl.BlockSpec(land.shape, lambda i: (0, 0, 0))] + [row] * (3 * N_GAINS),
        out_specs=[row] * n_out, out_shape=[out] * n_out,
    )(land, *ws, *ms, *vs)


GROUP_FFN = ("w_ffn_in", "w_ffn_out")
GROUP_MIX = ("w_sb_up", "w_dil_up", "w_out")
COL_SHARDED = ("w_in", "w_sb_up", "w_dil_up", "w_ffn_in")
TRANSPOSED = ("w_in", "w_ffn_in")


def _full_from_shards(name, slots):
    _, r, c = slots.shape
    if name in TRANSPOSED:
        return slots.reshape(N_DEV * r, c).T
    if name in COL_SHARDED:
        return slots.transpose(1, 0, 2).reshape(r, N_DEV * c)
    return slots.reshape(N_DEV * r, c)


def _row_shards(full):
    rows, cols = full.shape
    return full.reshape(N_DEV, rows // N_DEV, cols)


def _local_step(x, target, g_mix, g_ffn, g_fin, w_in_t, shards=None, rest=None):
    gather = lambda names, kind: None if shards is None else [(shards[n], kind) for n in names]
    scatter = lambda blocks: None if shards is None else [(t, "scatter") for t in blocks]
    landed = lambda blocks, lands: lands if lands else blocks

    w = {"w_in": w_in_t}
    if shards is None:
        w.update(rest)
        w["w_ffn_in"] = _row_shards(rest["w_ffn_in"].T)
    (qkv_sb, qkv_dl, gates, u), _ = _norm_proj(x, g_mix, w["w_in"])
    qkv_sb = qkv_sb.reshape(B_LOC, SEQ, 3 * SB_WIDTH)
    qkv_dl = qkv_dl.reshape(B_LOC, SEQ, 3 * DIL_WIDTH)
    (o_sb,), lands = _sb_fwd(qkv_sb, gather(GROUP_FFN, "gather_by_chip"))
    if lands:
        w["w_ffn_in"], w["w_ffn_out"] = lands[0], _full_from_shards("w_ffn_out", lands[1])
    o_sb = o_sb.reshape(TOK, SB_WIDTH)
    (o_dl, lse), lands = _dil_fwd(qkv_dl, gather(GROUP_MIX, "gather"))
    w.update({n: _full_from_shards(n, t) for n, t in zip(GROUP_MIX, lands)})
    o_dl = o_dl.reshape(TOK, DIL_OUT)

    loss, dx1, merged, u2, act, dh, dx2, dg_fin, dg_ffn = _mix_ffn_fwd_bwd(
        x, o_sb, o_dl, gates, w["w_sb_up"], w["w_dil_up"], w["w_out"], target, g_ffn, g_fin,
        w["w_ffn_in"], w["w_ffn_out"])
    dgates, dy_sb, dy_dl, do_sb, do_dl, dsum = _mix_bwd(dx1, o_sb, o_dl, gates, w["w_sb_up"], w["w_dil_up"], w["w_out"])
    blocks = {
        "w_sb_up": _atb(o_sb, dy_sb, "grad_w_sb_up", SB_WIDTH, D_MODEL, col_blocks=N_DEV),
        "w_dil_up": _atb(o_dl, dy_dl, "grad_w_dil_up", DIL_OUT, D_MODEL, col_blocks=N_DEV),
        "w_out": _row_shards(_atb(merged, dx1, "grad_w_out", D_MODEL, D_MODEL)),
        "w_ffn_in": _row_shards(_atb_shards(dh, u2, "grad_w_ffn_in")),
        "w_ffn_out": _row_shards(_atb_shards(act, dx2, "grad_w_ffn_out")),
    }
    grads = {}

    early, late = GROUP_FFN, GROUP_MIX
    early_blocks = [blocks[n] for n in early]
    (dq_sb, dk_sb, dv_sb), lands = _sb_bwd(qkv_sb, do_sb.reshape(B_LOC, SEQ, SB_WIDTH), scatter(early_blocks))
    grads.update(zip(early, landed(early_blocks, lands)))
    as_batch = lambda t: t.reshape(B_LOC, SEQ, DIL_OUT)
    late_blocks = [blocks[n] for n in late]
    d_dl, lands = _dil_bwd(qkv_dl, as_batch(do_dl), lse, as_batch(dsum), scatter(late_blocks))
    grads.update(zip(late, landed(late_blocks, lands)))
    flat = lambda t: t.reshape(TOK, -1)
    dproj = ([flat(dq_sb), flat(dk_sb), flat(dv_sb)]
             + [flat(d_dl[3 * grp + part]) for part in range(3) for grp in range(DIL_GROUPS)] + [dgates])

    w_in_blocks = _row_shards(_atb_pieces(u, dproj, "grad_w_in", D_MODEL // 2))
    if shards is None:
        grads["w_in"] = w_in_blocks
        send = None
    else:
        core = lax.axis_index("c").astype(jnp.int32).reshape(1)
        send = [(_pair_sum(w_in_blocks, _pair_swap(w_in_blocks), core), "scatter_by_chip")]
    (grad_x, dg_mix), lands = _proj_bwd(dproj, dx1, x, g_mix, w["w_in"], send)
    if lands:
        grads["w_in"] = lands[0]
    gain_grads = jnp.concatenate([dg_mix, dg_ffn, dg_fin], axis=0)
    return loss, grad_x, gain_grads, grads


def kernel(x, norm_mix_g, w_in, w_sb_up, w_dil_up, w_out, norm_ffn_g, w_ffn_in, w_ffn_out, norm_final_g, loss_target, m_norm_mix_g, m_w_in, m_w_sb_up, m_w_dil_up, m_w_out, m_norm_ffn_g, m_w_ffn_in, m_w_ffn_out, m_norm_final_g, v_norm_mix_g, v_w_in, v_w_sb_up, v_w_dil_up, v_w_out, v_norm_ffn_g, v_w_ffn_in, v_w_ffn_out, v_norm_final_g):
    mats = {"w_in": w_in, "w_sb_up": w_sb_up, "w_dil_up": w_dil_up, "w_out": w_out,
            "w_ffn_in": w_ffn_in, "w_ffn_out": w_ffn_out}
    moments_m = {"w_in": m_w_in, "w_sb_up": m_w_sb_up, "w_dil_up": m_w_dil_up, "w_out": m_w_out,
                 "w_ffn_in": m_w_ffn_in, "w_ffn_out": m_w_ffn_out}
    moments_v = {"w_in": v_w_in, "w_sb_up": v_w_sb_up, "w_dil_up": v_w_dil_up, "w_out": v_w_out,
                 "w_ffn_in": v_w_ffn_in, "w_ffn_out": v_w_ffn_out}
    stored = lambda t, name: t.transpose(0, 2, 1) if name in TRANSPOSED else t
    mats = {name: stored(t, name) for name, t in mats.items()}
    shards = dict(zip(mats, _to_bf16([t[0] for t in mats.values()])))
    gathered_w_in = _all_gather(shards.pop("w_in"), "all_gather_w_in")
    g_fin = norm_final_g.reshape(1, D_MODEL)
    loss, grad_x, gain_grads, grad_slots = _local_step(
        x.reshape(TOK, D_MODEL), loss_target.reshape(TOK, D_MODEL), norm_mix_g, norm_ffn_g, g_fin,
        gathered_w_in.reshape(IN_WIDTH, D_MODEL), shards=shards)

    gain_rows = jnp.concatenate([gain_grads, jnp.tile(loss, (1, D_MODEL // LANES)),
                                 jnp.zeros((8 - 4, D_MODEL), F32)], axis=0)
    row = lambda t: t.reshape(1, D_MODEL)
    loss_row, *gain_outs = _sum_update_gains(
        _all_gather(gain_rows, "all_gather_gains"),
        [norm_mix_g, norm_ffn_g, g_fin], [m_norm_mix_g, m_norm_ffn_g, row(m_norm_final_g)],
        [v_norm_mix_g, v_norm_ffn_g, row(v_norm_final_g)])

    out_g, out_d, out_m, out_v = {}, {}, {}, {}
    for name, slots in grad_slots.items():
        out_g[name], out_d[name], out_m[name], out_v[name] = [stored(t, name) for t in _sum_update(
            slots, mats[name], stored(moments_m[name], name), stored(moments_v[name], name), "update_" + name)]
    for idx, name in enumerate(("norm_mix_g", "norm_ffn_g", "norm_final_g")):
        shape = (D_MODEL,) if name == "norm_final_g" else (1, D_MODEL)
        out_g[name], out_d[name], out_m[name], out_v[name] = [t.reshape(shape) for t in gain_outs[4 * idx:4 * idx + 4]]

    order = ("norm_mix_g", "w_in", "w_sb_up", "w_dil_up", "w_out", "norm_ffn_g", "w_ffn_in", "w_ffn_out",
             "norm_final_g")
    return (loss_row[0, 0], grad_x.reshape(B_LOC, SEQ, D_MODEL),
            *[out_g[n] for n in order], *[out_d[n] for n in order],
            *[out_m[n] for n in order], *[out_v[n] for n in order])
```

```python
import math

import jax
import jax.numpy as jnp
from jax import lax
from jax.experimental import pallas as pl
from jax.experimental.pallas import tpu as pltpu

F32 = jnp.float32
BF16 = jnp.bfloat16

N_DEV = 8
D_MODEL = 1024
SEQ = 2048
B_LOC = 2
TOK = B_LOC * SEQ
HEAD_DIM = 64
SB_WIDTH = 512
DIL_WIDTH = 768
DIL_OUT = 256
QKV_WIDTH = 3 * SB_WIDTH + 3 * DIL_WIDTH
IN_WIDTH = QKV_WIDTH + 2 * D_MODEL
D_FF = 2816
DIL_PAIRS = ((128, 1), (512, 4), (2048, 16))
DIL_HEADS = 12
RMS_EPS = 1e-6
ALIBI_MAX_BIAS = 8.0
QK_SCALE = 1.0 / math.sqrt(HEAD_DIM)
BLK = 128
LANES = 128
NEG_BIG = -1e30

ADAM_LR = 0.001
ADAM_B1 = 0.9
ADAM_B2 = 0.999
ADAM_EPS = 1e-08
ADAM_WD = 0.01
ADAM_STEP = 10

VMEM_LIMIT = 58 * 1024 * 1024


def _dot(a, b):
    return jnp.dot(a, b, preferred_element_type=F32)


def _dot_nt(a, b):
    return lax.dot_general(a, b, (((1,), (1,)), ((), ())), preferred_element_type=F32)


def _dot_tn(a, b):
    return lax.dot_general(a, b, (((0,), (0,)), ((), ())), preferred_element_type=F32)


def _sigmoid(z):
    return 1.0 / (1.0 + jnp.exp(-z))


def _split_bf16(v):
    hi = v.astype(BF16)
    lo = (v - hi.astype(F32)).astype(BF16)
    return hi, lo


def _chunks(width, step=512):
    out, c = [], 0
    while c < width:
        w = min(step, width - c)
        out.append((c, w))
        c += w
    return out


def _resident(shape):
    nd = len(shape)
    return pl.BlockSpec(shape, lambda *_: (0,) * nd, pipeline_mode=pl.Buffered(1))


def _params(sem):
    return pltpu.CompilerParams(dimension_semantics=sem, vmem_limit_bytes=VMEM_LIMIT)


def _rms_fwd(x, g):
    r = lax.rsqrt(jnp.mean(x * x, axis=-1, keepdims=True) + RMS_EPS)
    n = x * r
    return n, r, n * g


def _rms_bwd(dy, n, r, g):
    dg = jnp.sum(dy * n, axis=0, keepdims=True)
    dn = dy * g
    dx = r * (dn - n * jnp.mean(dn * n, axis=-1, keepdims=True))
    return dx, dg


TM = 256


def _norm_proj(x, g, w_in_t, send=None):
    def body(x_ref, g_ref, w_ref, sb_ref, dl_ref, gate_ref, u_ref):
        _, _, u = _rms_fwd(x_ref[...], g_ref[...])
        u = u.astype(BF16)
        u_ref[...] = u
        for c0, w in _chunks(3 * SB_WIDTH):
            sb_ref[:, c0:c0 + w] = _dot_nt(u, w_ref[c0:c0 + w, :]).astype(BF16)
        for c0, w in _chunks(3 * DIL_WIDTH):
            dl_ref[:, c0:c0 + w] = _dot_nt(u, w_ref[3 * SB_WIDTH + c0:3 * SB_WIDTH + c0 + w, :])
        for c0, w in _chunks(2 * D_MODEL):
            gate_ref[:, c0:c0 + w] = _dot_nt(u, w_ref[QKV_WIDTH + c0:QKV_WIDTH + c0 + w, :])

    return _call(
        body, send, name="norm_proj", grid=(TOK // TM,),
        in_specs=[pl.BlockSpec((TM, D_MODEL), lambda i: (i, 0)), _resident((1, D_MODEL)),
                  _resident((IN_WIDTH, D_MODEL))],
        out_specs=[pl.BlockSpec((TM, 3 * SB_WIDTH), lambda i: (i, 0)),
                   pl.BlockSpec((TM, 3 * DIL_WIDTH), lambda i: (i, 0)),
                   pl.BlockSpec((TM, 2 * D_MODEL), lambda i: (i, 0)),
                   pl.BlockSpec((TM, D_MODEL), lambda i: (i, 0))],
        out_shape=[jax.ShapeDtypeStruct((TOK, 3 * SB_WIDTH), BF16),
                   jax.ShapeDtypeStruct((TOK, 3 * DIL_WIDTH), F32),
                   jax.ShapeDtypeStruct((TOK, 2 * D_MODEL), F32),
                   jax.ShapeDtypeStruct((TOK, D_MODEL), BF16)],
        scratch_shapes=[], semantics=("parallel",), operands=(x, g, w_in_t))


FF_CHUNK = 1024


def _mix_ffn_fwd_bwd(x, o_sb, o_dl, gates, w_sb_up, w_dil_up, w_out, target, g_ffn, g_fin, w_ffn_in_t, w_ffn_out):
    def body(x_ref, osb_ref, odl_ref, gate_ref, wsb_ref, wdl_ref, wo_ref, t_ref, gffn_ref, gfin_ref, win_ref, wout_ref,
             loss_ref, dx1_ref, mg_ref, u2_ref, act_ref, dh_ref, dx2_ref, dgfin_ref, dgffn_ref, h_scr):
        i = pl.program_id(0)

        @pl.when(i == 0)
        def _():
            loss_ref[...] = jnp.zeros_like(loss_ref)
            dgfin_ref[...] = jnp.zeros_like(dgfin_ref)
            dgffn_ref[...] = jnp.zeros_like(dgffn_ref)

        y_sb = _dot(osb_ref[...], wsb_ref[...])
        y_dl = _dot(odl_ref[...].astype(BF16), wdl_ref[...])
        merged = (_sigmoid(gate_ref[:, :D_MODEL]) * y_sb
                  + _sigmoid(gate_ref[:, D_MODEL:]) * y_dl).astype(BF16)
        mg_ref[...] = merged
        x1 = x_ref[...] + _dot(merged, wo_ref[...])
        g_ffn_v = gffn_ref[...]
        g_fin_v = gfin_ref[...]
        n2, r2, u2 = _rms_fwd(x1, g_ffn_v)
        u2 = u2.astype(BF16)
        u2_ref[...] = u2
        x2 = x1
        for c0, w in _chunks(D_FF, FF_CHUNK):
            gate = _dot_nt(u2, win_ref[c0:c0 + w, :])
            up = _dot_nt(u2, win_ref[D_FF + c0:D_FF + c0 + w, :])
            h_scr[:, c0:c0 + w] = gate
            h_scr[:, D_FF + c0:D_FF + c0 + w] = up
            act = (gate * _sigmoid(gate) * up).astype(BF16)
            act_ref[:, c0:c0 + w] = act
            x2 = x2 + _dot(act, wout_ref[c0:c0 + w, :])
        n3, r3, y = _rms_fwd(x2, g_fin_v)
        err = y - t_ref[...]
        sq = jnp.sum(jnp.sum(err * err, axis=1, keepdims=True), axis=0, keepdims=True)
        loss_ref[...] += sq * (0.5 / D_MODEL)
        dx2, dgfin = _rms_bwd(err * (1.0 / D_MODEL), n3, r3, g_fin_v)
        dgfin_ref[...] += dgfin
        dx2_b = dx2.astype(BF16)
        dx2_ref[...] = dx2_b
        du2 = jnp.zeros((TM, D_MODEL), F32)
        for c0, w in _chunks(D_FF, FF_CHUNK):
            gate = h_scr[:, c0:c0 + w]
            up = h_scr[:, D_FF + c0:D_FF + c0 + w]
            dact = _dot_nt(dx2_b, wout_ref[c0:c0 + w, :])
            sg = _sigmoid(gate)
            dgate = (dact * up * (sg * (1.0 + gate * (1.0 - sg)))).astype(BF16)
            dup = (dact * (gate * sg)).astype(BF16)
            dh_ref[:, c0:c0 + w] = dgate
            dh_ref[:, D_FF + c0:D_FF + c0 + w] = dup
            du2 = du2 + _dot(dgate, win_ref[c0:c0 + w, :])
            du2 = du2 + _dot(dup, win_ref[D_FF + c0:D_FF + c0 + w, :])
        dx1_n, dgffn = _rms_bwd(du2, n2, r2, g_ffn_v)
        dgffn_ref[...] += dgffn
        dx1_ref[...] = dx2 + dx1_n

    tile = lambda w: pl.BlockSpec((TM, w), lambda i: (i, 0))
    acc = lambda w: pl.BlockSpec((1, w), lambda i: (0, 0))
    return pl.pallas_call(
        body, name="mix_ffn_fwd_bwd", grid=(TOK // TM,),
        in_specs=[tile(D_MODEL), tile(SB_WIDTH), tile(DIL_OUT), tile(2 * D_MODEL),
                  _resident((SB_WIDTH, D_MODEL)), _resident((DIL_OUT, D_MODEL)), _resident((D_MODEL, D_MODEL)),
                  tile(D_MODEL), _resident((1, D_MODEL)), _resident((1, D_MODEL)),
                  _resident((2 * D_FF, D_MODEL)), _resident((D_FF, D_MODEL))],
        out_specs=[acc(LANES), tile(D_MODEL), tile(D_MODEL), tile(D_MODEL), tile(D_FF), tile(2 * D_FF),
                   tile(D_MODEL), acc(D_MODEL), acc(D_MODEL)],
        out_shape=[jax.ShapeDtypeStruct((1, LANES), F32),
                   jax.ShapeDtypeStruct((TOK, D_MODEL), F32),
                   jax.ShapeDtypeStruct((TOK, D_MODEL), BF16),
                   jax.ShapeDtypeStruct((TOK, D_MODEL), BF16),
                   jax.ShapeDtypeStruct((TOK, D_FF), BF16),
                   jax.ShapeDtypeStruct((TOK, 2 * D_FF), BF16),
                   jax.ShapeDtypeStruct((TOK, D_MODEL), BF16),
                   jax.ShapeDtypeStruct((1, D_MODEL), F32),
                   jax.ShapeDtypeStruct((1, D_MODEL), F32)],
        scratch_shapes=[pltpu.VMEM((TM, 2 * D_FF), F32)],
        compiler_params=_params(("arbitrary",)),
    )(x, o_sb, o_dl, gates, w_sb_up, w_dil_up, w_out, target, g_ffn, g_fin, w_ffn_in_t, w_ffn_out)


def _mix_bwd(dx1, o_sb, o_dl, gates, w_sb_up, w_dil_up, w_out):
    def body(dx1_ref, osb_ref, odl_ref, gate_ref, wsb_ref, wdl_ref, wout_ref,
             dgate_ref, dysb_ref, dydl_ref, dosb_ref, dodl_ref, dsum_ref):
        dmerged = _dot_nt(dx1_ref[...].astype(BF16), wout_ref[...])
        o_dl = odl_ref[...]
        y_sb = _dot(osb_ref[...], wsb_ref[...])
        y_dl = _dot(o_dl.astype(BF16), wdl_ref[...])
        s_sb = _sigmoid(gate_ref[:, :D_MODEL])
        s_dl = _sigmoid(gate_ref[:, D_MODEL:])
        dgate_ref[:, :D_MODEL] = (dmerged * y_sb * (s_sb * (1.0 - s_sb))).astype(BF16)
        dgate_ref[:, D_MODEL:] = (dmerged * y_dl * (s_dl * (1.0 - s_dl))).astype(BF16)
        dy_sb = (dmerged * s_sb).astype(BF16)
        dy_dl = (dmerged * s_dl).astype(BF16)
        dysb_ref[...] = dy_sb
        dydl_ref[...] = dy_dl
        dosb_ref[...] = _dot_nt(dy_sb, wsb_ref[...]).astype(BF16)
        do_dl = _dot_nt(dy_dl, wdl_ref[...])
        dodl_ref[...] = do_dl
        row = lax.broadcasted_iota(jnp.int32, (DIL_OUT, DIL_OUT), 0) // HEAD_DIM
        col = lax.broadcasted_iota(jnp.int32, (DIL_OUT, DIL_OUT), 1) // HEAD_DIM
        same_head = (row == col).astype(BF16)
        hi, lo = _split_bf16(do_dl * o_dl)
        dsum_ref[...] = _dot(hi, same_head) + _dot(lo, same_head)

    tile = lambda w: pl.BlockSpec((TM, w), lambda i: (i, 0))
    return pl.pallas_call(
        body, name="mix_bwd", grid=(TOK // TM,),
        in_specs=[tile(D_MODEL), tile(SB_WIDTH), tile(DIL_OUT), tile(2 * D_MODEL),
                  _resident((SB_WIDTH, D_MODEL)), _resident((DIL_OUT, D_MODEL)),
                  _resident((D_MODEL, D_MODEL))],
        out_specs=[tile(2 * D_MODEL), tile(D_MODEL), tile(D_MODEL), tile(SB_WIDTH), tile(DIL_OUT),
                   tile(DIL_OUT)],
        out_shape=[jax.ShapeDtypeStruct((TOK, 2 * D_MODEL), BF16),
                   jax.ShapeDtypeStruct((TOK, D_MODEL), BF16),
                   jax.ShapeDtypeStruct((TOK, D_MODEL), BF16),
                   jax.ShapeDtypeStruct((TOK, SB_WIDTH), BF16),
                   jax.ShapeDtypeStruct((TOK, DIL_OUT), F32),
                   jax.ShapeDtypeStruct((TOK, DIL_OUT), F32)],
        compiler_params=_params(("parallel",)),
    )(dx1, o_sb, o_dl, gates, w_sb_up, w_dil_up, w_out)


def _proj_bwd(dproj, dx1, x, g, w_in_t, send=None):
    widths = [p.shape[1] for p in dproj]

    def body(*refs):
        dx1_ref, x_ref, g_ref, w_ref, dx_ref, dg_ref = refs[len(widths):]

        @pl.when(pl.program_id(0) == 0)
        def _():
            dg_ref[...] = jnp.zeros_like(dg_ref)

        du = jnp.zeros((TM, D_MODEL), F32)
        c0 = 0
        for dp_ref, w in zip(refs, widths):
            du = du + _dot(dp_ref[...].astype(BF16), w_ref[c0:c0 + w, :])
            c0 += w
        g_v = g_ref[...]
        n, r, _ = _rms_fwd(x_ref[...], g_v)
        dx, dg = _rms_bwd(du, n, r, g_v)
        dg_ref[...] += dg
        dx_ref[...] = dx1_ref[...] + dx

    tile = lambda w: pl.BlockSpec((TM, w), lambda i: (i, 0))
    return _call(
        body, send, name="proj_bwd", grid=(TOK // TM,),
        in_specs=[tile(w) for w in widths] + [tile(D_MODEL), tile(D_MODEL), _resident((1, D_MODEL)),
                                              _resident((IN_WIDTH, D_MODEL))],
        out_specs=[tile(D_MODEL), pl.BlockSpec((1, D_MODEL), lambda i: (0, 0))],
        out_shape=[jax.ShapeDtypeStruct((TOK, D_MODEL), F32),
                   jax.ShapeDtypeStruct((1, D_MODEL), F32)],
        scratch_shapes=[], semantics=("arbitrary",), operands=(*dproj, dx1, x, g, w_in_t))


def _atb_pieces(a, pieces, name, tm, tk=512):
    m = a.shape[1]
    widths = [p.shape[1] for p in pieces]
    n = sum(widths)
    nk = TOK // tk

    def body(a_ref, *refs):
        o_ref, acc_ref = refs[len(widths):]
        k = pl.program_id(1)

        @pl.when(k == 0)
        def _():
            acc_ref[...] = jnp.zeros_like(acc_ref)

        a_v = a_ref[...]
        c0 = 0
        for p_ref, w in zip(refs, widths):
            acc_ref[:, c0:c0 + w] += _dot_tn(a_v, p_ref[...].astype(BF16))
            c0 += w

        @pl.when(k == nk - 1)
        def _():
            for c0, w in _chunks(n):
                o_ref[c0:c0 + w, :] = acc_ref[:, c0:c0 + w].T.astype(BF16)

    return pl.pallas_call(
        body, name=name, grid=(m // tm, nk),
        in_specs=[pl.BlockSpec((tk, tm), lambda i, k: (k, i))]
                 + [pl.BlockSpec((tk, w), lambda i, k: (k, 0)) for w in widths],
        out_specs=pl.BlockSpec((n, tm), lambda i, k: (0, i)),
        out_shape=jax.ShapeDtypeStruct((n, m), BF16),
        scratch_shapes=[pltpu.VMEM((tm, n), F32)],
        compiler_params=_params(("parallel", "arbitrary")),
    )(a, *pieces)


def _atb_cols(a, b, name, cols, tk=512):
    m, other = a.shape[1], b.shape[1]

    def body(a_ref, b_ref, o_ref):
        acc = jnp.zeros(o_ref.shape, F32)
        for k0 in range(0, TOK, tk):
            acc = acc + _dot_tn(a_ref[k0:k0 + tk, :], b_ref[k0:k0 + tk, :])
        o_ref[...] = acc.astype(BF16)

    assert m % cols == 0
    return pl.pallas_call(
        body, name=name, grid=(m // cols,),
        in_specs=[pl.BlockSpec((TOK, cols), lambda r: (0, r)), _resident((TOK, other))],
        out_specs=pl.BlockSpec((cols, other), lambda r: (r, 0)),
        out_shape=jax.ShapeDtypeStruct((m, other), BF16),
        compiler_params=_params(("parallel",)),
    )(a, b)


def _atb(a, b, name, tm, tn, col_blocks=0, tk=512):
    m, n = a.shape[1], b.shape[1]
    nk = TOK // tk

    def body(a_ref, b_ref, o_ref, acc_ref):
        k = pl.program_id(2)

        @pl.when(k == 0)
        def _():
            acc_ref[...] = jnp.zeros_like(acc_ref)

        acc_ref[...] += _dot_tn(a_ref[...].astype(BF16), b_ref[...].astype(BF16))

        @pl.when(k == nk - 1)
        def _():
            if col_blocks:
                width = n // col_blocks
                for blk in range(col_blocks):
                    o_ref[blk] = acc_ref[:, blk * width:(blk + 1) * width].astype(BF16)
            else:
                o_ref[...] = acc_ref[...].astype(BF16)

    if col_blocks:
        out_spec = pl.BlockSpec((col_blocks, tm, n // col_blocks), lambda i, j, k: (0, i, 0))
        out_shape = jax.ShapeDtypeStruct((col_blocks, m, n // col_blocks), BF16)
    else:
        out_spec = pl.BlockSpec((tm, tn), lambda i, j, k: (i, j))
        out_shape = jax.ShapeDtypeStruct((m, n), BF16)
    return pl.pallas_call(
        body, name=name, grid=(m // tm, n // tn, nk),
        in_specs=[pl.BlockSpec((tk, tm), lambda i, j, k: (k, i)),
                  pl.BlockSpec((tk, tn), lambda i, j, k: (k, j))],
        out_specs=out_spec, out_shape=out_shape,
        scratch_shapes=[pltpu.VMEM((tm, tn), F32)],
        compiler_params=_params(("parallel", "parallel", "arbitrary")),
    )(a, b)


SB_PAIRS = SB_WIDTH // LANES


def _two_heads(v, lane0):
    zero = jnp.zeros_like(v)
    return jnp.where(lane0, v, zero), jnp.where(lane0, zero, v)


SB_QBLK = 256
N_SB_STEPS = SEQ // SB_QBLK


SB_KCHUNK = 2 * BLK
SB_ROWS = 2 * SB_QBLK
SB_DEAD = -104.0


def _log_keep(z):
    neg_z = -z
    return jnp.minimum(neg_z, 0.0) - jnp.log(1.0 + jnp.exp(jnp.minimum(z, neg_z)))


def _stack_heads(v, lane0):
    return jnp.concatenate(_two_heads(v, lane0), axis=0)


def _block_sums(v, tri, split=True):
    halves = (v[:, :BLK], v[:, BLK:])
    stacked = jnp.concatenate(halves, axis=0)
    if split:
        hi, lo = _split_bf16(stacked)
        prod = _dot(jnp.concatenate([hi, lo], axis=0), tri)
        tri_sum = prod[:2 * SB_ROWS] + prod[2 * SB_ROWS:]
    else:
        tri_sum = _dot(stacked.astype(BF16), tri)
    sums = tuple(jnp.sum(h, axis=1, keepdims=True) for h in halves)
    return (tri_sum[:SB_ROWS], tri_sum[SB_ROWS:]), sums


def _sb_diag_mask():
    row = lax.broadcasted_iota(jnp.int32, (SB_ROWS, SB_KCHUNK), 0)
    col = lax.broadcasted_iota(jnp.int32, (SB_ROWS, SB_KCHUNK), 1)
    return col < jnp.where(row >= SB_QBLK, row - SB_QBLK, row)


def _sb_fwd(qkv, send=None):
    def body(q_ref, k_ref, v_ref, o_ref):
        i = pl.program_id(2)
        krow = lax.broadcasted_iota(jnp.int32, (BLK, BLK), 0)
        kcol = lax.broadcasted_iota(jnp.int32, (BLK, BLK), 1)
        later = (krow > kcol).astype(BF16)
        lane0 = lax.broadcasted_iota(jnp.int32, (SB_QBLK, LANES), 1) < HEAD_DIM
        q2 = _stack_heads(q_ref[0] * QK_SCALE, lane0)

        def chunk(c, carry, causal):
            acc, run = carry
            off = pl.multiple_of(c * SB_KCHUNK, SB_KCHUNK)
            z = _dot_nt(q2, k_ref[0, pl.ds(off, SB_KCHUNK), :])
            log_keep = _log_keep(z)
            if causal is not None:
                log_keep = jnp.where(causal, log_keep, 0.0)
            suffix, sums = _block_sums(log_keep, later)
            log_after = jnp.concatenate([suffix[0] + (run + sums[1]), suffix[1] + run], axis=1)
            a = jnp.exp(log_keep + z + log_after)
            if causal is not None:
                a = jnp.where(causal, a, 0.0)
            acc = acc + _dot(a.astype(BF16), v_ref[0, pl.ds(off, SB_KCHUNK), :])
            return acc, run + (sums[0] + sums[1])

        acc, run = chunk(i, (jnp.zeros((SB_ROWS, LANES), F32), jnp.zeros((SB_ROWS, 1), F32)), _sb_diag_mask())

        def some_alive(run):
            return (jnp.max(run) > SB_DEAD).astype(jnp.int32)

        def trip(state):
            t, _, acc, run = state
            acc, run = chunk(i - 1 - t, (acc, run), None)
            return t + 1, some_alive(run), acc, run

        _, _, acc, _ = lax.while_loop(lambda s: jnp.logical_and(s[0] < i, s[1] > 0), trip,
                                      (jnp.int32(0), some_alive(run), acc, run))
        o_ref[0] = jnp.where(lane0, acc[:SB_QBLK], acc[SB_QBLK:]).astype(BF16)

    blk = pl.BlockSpec((1, SB_QBLK, LANES), lambda b, h, i: (b, i, h))
    return _call(
        body, send, name="sb_fwd", grid=(B_LOC, SB_PAIRS, N_SB_STEPS),
        in_specs=[blk,
                  pl.BlockSpec((1, SEQ, LANES), lambda b, h, i: (b, 0, SB_PAIRS + h)),
                  pl.BlockSpec((1, SEQ, LANES), lambda b, h, i: (b, 0, 2 * SB_PAIRS + h))],
        out_specs=[blk], out_shape=[jax.ShapeDtypeStruct((B_LOC, SEQ, SB_WIDTH), BF16)],
        scratch_shapes=[], semantics=("parallel", "parallel", "arbitrary"), operands=(qkv, qkv, qkv))


def _sb_bwd(qkv, d_o, send=None):
    def body(q_ref, k_ref, v_ref, do_ref, dq_ref, dk_ref, dv_ref, dk_acc, dv_acc, z_scr, keep_scr):
        i = pl.program_id(2)
        krow = lax.broadcasted_iota(jnp.int32, (BLK, BLK), 0)
        kcol = lax.broadcasted_iota(jnp.int32, (BLK, BLK), 1)
        upto = (krow <= kcol).astype(BF16)
        earlier = (krow < kcol).astype(BF16)
        lane0 = lax.broadcasted_iota(jnp.int32, (SB_QBLK, LANES), 1) < HEAD_DIM
        q2 = _stack_heads(q_ref[0] * QK_SCALE, lane0)
        do2 = _stack_heads(do_ref[0], lane0)

        def keep_sum(c, causal):
            off = pl.multiple_of(c * SB_KCHUNK, SB_KCHUNK)
            z = _dot_nt(q2, k_ref[0, pl.ds(off, SB_KCHUNK), :])
            log_keep = _log_keep(z)
            if causal is not None:
                log_keep = jnp.where(causal, log_keep, 0.0)
            z_scr[c] = z
            keep_scr[c] = log_keep
            return jnp.sum(log_keep, axis=1, keepdims=True)

        def some_alive(run):
            return (jnp.max(run) > SB_DEAD).astype(jnp.int32)

        def scan(state):
            t, _, run = state
            run = run + keep_sum(i - 1 - t, None)
            return t + 1, some_alive(run), run

        diag_sum = keep_sum(i, _sb_diag_mask())
        walked, _, tot2 = lax.while_loop(lambda s: jnp.logical_and(s[0] < i, s[1] > 0), scan,
                                         (jnp.int32(0), some_alive(diag_sum), diag_sum))
        first = i - walked

        @pl.when(i == 0)
        def _():
            dk_acc[...] = jnp.zeros_like(dk_acc)
            dv_acc[...] = jnp.zeros_like(dv_acc)

        def chunk(c, carry, causal):
            dq, pre_keep, pre_e = carry
            off = pl.multiple_of(c * SB_KCHUNK, SB_KCHUNK)
            k_c = k_ref[0, pl.ds(off, SB_KCHUNK), :]
            v_c = v_ref[0, pl.ds(off, SB_KCHUNK), :]
            d_a = _dot_nt(do2, v_c)
            log_keep = keep_scr[c]
            log_beta = log_keep + z_scr[c]
            prefix, sums = _block_sums(log_keep, upto)
            inclusive = jnp.concatenate([prefix[0], prefix[1] + sums[0]], axis=1)
            a = jnp.exp(log_beta + ((tot2 - pre_keep) - inclusive))
            if causal is not None:
                a = jnp.where(causal, a, 0.0)
            e = d_a * a
            e_prefix, e_sums = _block_sums(e, earlier, split=False)
            before = jnp.concatenate([e_prefix[0] + pre_e, e_prefix[1] + (pre_e + e_sums[0])], axis=1)
            dz = e - (e + before) * jnp.exp(log_beta)
            if causal is not None:
                dz = jnp.where(causal, dz, 0.0)
            dz = dz.astype(BF16)
            dq = dq + _dot(dz, k_c)
            dk_acc[pl.ds(off, SB_KCHUNK), :] += _dot_tn(dz, q2)
            dv_acc[pl.ds(off, SB_KCHUNK), :] += _dot_tn(a.astype(BF16), do2)
            return dq, pre_keep + (sums[0] + sums[1]), pre_e + (e_sums[0] + e_sums[1])

        zero_col = jnp.zeros((SB_ROWS, 1), F32)
        carry = lax.fori_loop(first, i, lambda t, c: chunk(t, c, None),
                              (jnp.zeros((SB_ROWS, LANES), F32), zero_col, zero_col))
        dq, _, _ = chunk(i, carry, _sb_diag_mask())
        dq_ref[0] = (jnp.where(lane0, dq[:SB_QBLK], dq[SB_QBLK:]) * QK_SCALE).astype(BF16)

        @pl.when(i == N_SB_STEPS - 1)
        def _():
            dk_ref[0] = dk_acc[...].astype(BF16)
            dv_ref[0] = dv_acc[...].astype(BF16)

    blk = pl.BlockSpec((1, SB_QBLK, LANES), lambda b, h, i: (b, i, h))
    whole = lambda c: pl.BlockSpec((1, SEQ, LANES), lambda b, h, i: (b, 0, c * SB_PAIRS + h))
    out = jax.ShapeDtypeStruct((B_LOC, SEQ, SB_WIDTH), BF16)
    return _call(
        body, send, name="sb_bwd", grid=(B_LOC, SB_PAIRS, N_SB_STEPS),
        in_specs=[blk, whole(1), whole(2), blk],
        out_specs=[blk, whole(0), whole(0)],
        out_shape=[out, out, out],
        scratch_shapes=[pltpu.VMEM((SEQ, LANES), F32), pltpu.VMEM((SEQ, LANES), F32),
                        pltpu.VMEM((N_SB_STEPS, SB_ROWS, SB_KCHUNK), F32),
                        pltpu.VMEM((N_SB_STEPS, SB_ROWS, SB_KCHUNK), F32)],
        semantics=("parallel", "parallel", "arbitrary"), operands=(qkv, qkv, qkv, d_o))


DIL_GROUPS = len(DIL_PAIRS)
DIL_QBLOCKS = SEQ // BLK


def _residue_rows(j, dilation):
    length = SEQ // dilation
    return pl.ds(j, length, stride=dilation) if dilation > 1 else pl.ds(0, length)


def _gather_residues(src_ref, dst_ref, dst_off, dilation, scale=None):
    length = SEQ // dilation
    for j in range(dilation):
        v = src_ref[_residue_rows(j, dilation), :]
        if scale is not None:
            v = v * scale
        dst_ref[dst_off + j * length:dst_off + (j + 1) * length, :] = v.astype(dst_ref.dtype)


def _scatter_residues(src_ref, src_off, dst_ref, dilation):
    length = SEQ // dilation
    for j in range(dilation):
        dst_ref[_residue_rows(j, dilation), :] = (
            src_ref[src_off + j * length:src_off + (j + 1) * length, :].astype(dst_ref.dtype))


def _dil_geometry(group, pair):
    dilation = DIL_PAIRS[group][1]
    row = lax.broadcasted_iota(jnp.int32, (2 * BLK, 2 * BLK), 0)
    col = lax.broadcasted_iota(jnp.int32, (2 * BLK, 2 * BLK), 1)
    second = row >= BLK
    steps = BLK + jnp.where(second, row - BLK, row) - col
    coef = -ALIBI_MAX_BIAS / DIL_HEADS * math.log(2.0)
    first_head = float(4 * group + 1) + 2.0 * pair.astype(F32)
    slope = jnp.exp(coef * (first_head + jnp.where(second, 1.0, 0.0)))
    bias = slope * (steps * dilation).astype(F32)
    valid = jnp.logical_and(steps >= 0, steps <= BLK)
    return bias, valid, col >= BLK


def _dil_tile_scores(q2, kk, geometry, has_prev):
    bias, valid, own = geometry
    ok = jnp.logical_and(valid, jnp.logical_or(own, has_prev))
    return jnp.where(ok, _dot_nt(q2, kk) - bias, NEG_BIG)


def _head_col(v, lane_mask):
    return jnp.max(jnp.where(lane_mask, v, NEG_BIG), axis=1, keepdims=True)


def _dil_fwd(qkv, send=None):
    def body(*refs):
        ins, (o_ref, lse_ref), (qs, ks, vs, o_res, lse_res) = refs[:9], refs[9:11], refs[11:16]
        o_grp, lse_grp = refs[16:19], refs[19:22]
        pair = pl.program_id(1)
        lane0 = lax.broadcasted_iota(jnp.int32, (BLK, LANES), 1) < HEAD_DIM
        ks[0:BLK, :] = jnp.zeros((BLK, LANES), BF16)
        vs[0:BLK, :] = jnp.zeros((BLK, LANES), BF16)
        for grp, (_, dilation) in enumerate(DIL_PAIRS):
            q_ref, k_ref, v_ref = ins[3 * grp:3 * grp + 3]
            per_residue = DIL_QBLOCKS // dilation
            _gather_residues(q_ref, qs, 0, dilation, QK_SCALE)
            _gather_residues(k_ref, ks, BLK, dilation)
            _gather_residues(v_ref, vs, BLK, dilation)
            geometry = _dil_geometry(grp, pair)

            def step(blk, _):
                off = pl.multiple_of(blk * BLK, BLK)
                q2 = _stack_heads(qs[pl.ds(off, BLK), :], lane0)
                s = _dil_tile_scores(q2, ks[pl.ds(off, 2 * BLK), :], geometry, blk % per_residue != 0)
                m = jnp.max(s, axis=1, keepdims=True)
                p = jnp.exp(s - m)
                den = jnp.sum(p, axis=1, keepdims=True)
                out = _dot(p.astype(BF16), vs[pl.ds(off, 2 * BLK), :]) / den
                lse = m + jnp.log(den)
                o_res[pl.ds(off, BLK), :] = jnp.where(lane0, out[:BLK], out[BLK:])
                lse_res[pl.ds(off, BLK), :] = jnp.where(lane0, lse[:BLK], lse[BLK:])
                return 0

            lax.fori_loop(0, DIL_QBLOCKS, step, 0, unroll=8)
            _scatter_residues(o_res, 0, o_grp[grp], dilation)
            _scatter_residues(lse_res, 0, lse_grp[grp], dilation)

        for r0 in range(0, SEQ, 2 * BLK):
            rows = slice(r0, r0 + 2 * BLK)
            ls = [lse_grp[g][rows, :] for g in range(DIL_GROUPS)]
            m = jnp.maximum(jnp.maximum(ls[0], ls[1]), ls[2])
            w = [jnp.exp(l - m) for l in ls]
            den = w[0] + w[1] + w[2]
            o_ref[rows, :] = (w[0] * o_grp[0][rows, :] + w[1] * o_grp[1][rows, :] + w[2] * o_grp[2][rows, :]) / den
            lse_ref[rows, :] = m + jnp.log(den)

    def col(part, grp):
        return pl.BlockSpec((None, SEQ, LANES), lambda b, p: (b, 0, 6 * part + 2 * grp + p))

    out_spec = pl.BlockSpec((None, SEQ, LANES), lambda b, p: (b, 0, p))
    out = jax.ShapeDtypeStruct((B_LOC, SEQ, DIL_OUT), F32)
    return _call(
        body, send, name="dil_fwd", grid=(B_LOC, DIL_OUT // LANES),
        in_specs=[col(part, grp) for grp in range(DIL_GROUPS) for part in range(3)],
        out_specs=[out_spec, out_spec], out_shape=[out, out],
        scratch_shapes=[pltpu.VMEM((SEQ, LANES), BF16), pltpu.VMEM((SEQ + BLK, LANES), BF16),
                        pltpu.VMEM((SEQ + BLK, LANES), BF16), pltpu.VMEM((SEQ, LANES), F32),
                        pltpu.VMEM((SEQ, LANES), F32)] + [pltpu.VMEM((SEQ, LANES), F32)] * (2 * DIL_GROUPS),
        semantics=("parallel", "parallel"), operands=[qkv] * 9)


def _dil_bwd(qkv, d_o, lse, dsum, send=None):
    def body(*refs):
        ins, (do_ref, lse_ref, dsum_ref), outs = refs[:9], refs[9:12], refs[12:21]
        qs, ks, vs, dos, lse_res, dsum_res, dq_res, dk_acc, dv_acc = refs[21:]
        pair = pl.program_id(1)
        lane0 = lax.broadcasted_iota(jnp.int32, (BLK, LANES), 1) < HEAD_DIM
        lane1 = jnp.logical_not(lane0)
        ks[0:BLK, :] = jnp.zeros((BLK, LANES), BF16)
        vs[0:BLK, :] = jnp.zeros((BLK, LANES), BF16)
        for grp, (_, dilation) in enumerate(DIL_PAIRS):
            q_ref, k_ref, v_ref = ins[3 * grp:3 * grp + 3]
            dq_ref, dk_ref, dv_ref = outs[3 * grp:3 * grp + 3]
            per_residue = DIL_QBLOCKS // dilation
            _gather_residues(q_ref, qs, 0, dilation, QK_SCALE)
            _gather_residues(k_ref, ks, BLK, dilation)
            _gather_residues(v_ref, vs, BLK, dilation)
            _gather_residues(do_ref, dos, 0, dilation)
            _gather_residues(lse_ref, lse_res, 0, dilation)
            _gather_residues(dsum_ref, dsum_res, 0, dilation)
            dk_acc[...] = jnp.zeros_like(dk_acc)
            dv_acc[...] = jnp.zeros_like(dv_acc)
            geometry = _dil_geometry(grp, pair)

            def step(blk, _):
                off = pl.multiple_of(blk * BLK, BLK)
                q2 = _stack_heads(qs[pl.ds(off, BLK), :], lane0)
                do2 = _stack_heads(dos[pl.ds(off, BLK), :], lane0)
                kk = ks[pl.ds(off, 2 * BLK), :]
                vv = vs[pl.ds(off, 2 * BLK), :]
                lse_blk = lse_res[pl.ds(off, BLK), :]
                dsum_blk = dsum_res[pl.ds(off, BLK), :]
                lse2 = jnp.concatenate([_head_col(lse_blk, lane0), _head_col(lse_blk, lane1)], axis=0)
                dsum2 = jnp.concatenate([_head_col(dsum_blk, lane0), _head_col(dsum_blk, lane1)], axis=0)
                s = _dil_tile_scores(q2, kk, geometry, blk % per_residue != 0)
                p = jnp.exp(s - lse2)
                ds = (p * (_dot_nt(do2, vv) - dsum2)).astype(BF16)
                dq2 = _dot(ds, kk)
                dq_res[pl.ds(off, BLK), :] = jnp.where(lane0, dq2[:BLK], dq2[BLK:]) * QK_SCALE
                dk_acc[pl.ds(off, 2 * BLK), :] += _dot_tn(ds, q2)
                dv_acc[pl.ds(off, 2 * BLK), :] += _dot_tn(p.astype(BF16), do2)
                return 0

            lax.fori_loop(0, DIL_QBLOCKS, step, 0, unroll=8)
            _scatter_residues(dq_res, 0, dq_ref, dilation)
            _scatter_residues(dk_acc, BLK, dk_ref, dilation)
            _scatter_residues(dv_acc, BLK, dv_ref, dilation)

    def col(part, grp):
        return pl.BlockSpec((None, SEQ, LANES), lambda b, p: (b, 0, 6 * part + 2 * grp + p))

    slot = pl.BlockSpec((None, SEQ, LANES), lambda b, p: (b, 0, p))
    out = jax.ShapeDtypeStruct((B_LOC, SEQ, DIL_OUT), F32)
    return _call(
        body, send, name="dil_bwd", grid=(B_LOC, DIL_OUT // LANES),
        in_specs=[col(part, grp) for grp in range(DIL_GROUPS) for part in range(3)] + [slot] * 3,
        out_specs=[slot] * 9, out_shape=[out] * 9,
        scratch_shapes=[pltpu.VMEM((SEQ, LANES), BF16), pltpu.VMEM((SEQ + BLK, LANES), BF16),
                        pltpu.VMEM((SEQ + BLK, LANES), BF16), pltpu.VMEM((SEQ, LANES), BF16),
                        pltpu.VMEM((SEQ, LANES), F32), pltpu.VMEM((SEQ, LANES), F32),
                        pltpu.VMEM((SEQ, LANES), F32), pltpu.VMEM((SEQ + BLK, LANES), F32),
                        pltpu.VMEM((SEQ + BLK, LANES), F32)],
        semantics=("parallel", "parallel"), operands=[qkv] * 9 + [d_o, lse, dsum])


def _peers():
    x, y, c = lax.axis_index("x"), lax.axis_index("y"), lax.axis_index("c")
    me = 4 * x + 2 * y + c
    peers = []
    for mask in range(1, N_DEV):
        px = 1 - x if mask & 4 else x
        py = 1 - y if mask & 2 else y
        pc = 1 - c if mask & 1 else c
        peers.append(((px, py, pc), 4 * px + 2 * py + pc))
    return me, peers


def _all_gather(shard, name):
    def body(src_ref, out_ref, send_sems, recv_sems, local_sem):
        x, y, c = lax.axis_index("x"), lax.axis_index("y"), lax.axis_index("c")
        sibling = (x, y, 1 - c)
        chips = [(1 - x, y), (x, 1 - y), (1 - x, 1 - y)]

        def slot(px, py, pc):
            return out_ref.at[4 * px + 2 * py + pc]

        def copy(k, block, to, src=None):
            return pltpu.make_async_remote_copy(
                src_ref=slot(*block) if src is None else src, dst_ref=slot(*block),
                send_sem=send_sems.at[k], recv_sem=recv_sems.at[k], device_id=to,
                device_id_type=pl.DeviceIdType.MESH)

        mine = pltpu.make_async_copy(src_ref, slot(x, y, c), local_sem)
        mine.start()
        first = [copy(0, (x, y, c), sibling, src=src_ref)]
        first += [copy(1 + j, (x, y, c), (*chip, c), src=src_ref) for j, chip in enumerate(chips)]
        for cp in first:
            cp.start()
        passed = [copy(4 + j, (*chip, c), sibling) for j, chip in enumerate(chips)]
        for j, chip in enumerate(chips):
            copy(1 + j, (*chip, c), (x, y, c)).wait_recv()
            passed[j].start()
        copy(0, sibling, (x, y, c)).wait_recv()
        for j, chip in enumerate(chips):
            copy(4 + j, (*chip, 1 - c), (x, y, c)).wait_recv()
        for cp in first + passed:
            cp.wait_send()
        mine.wait()

    return pl.pallas_call(
        body, name=name,
        in_specs=[pl.BlockSpec(memory_space=pl.ANY)],
        out_specs=pl.BlockSpec(memory_space=pl.ANY),
        out_shape=jax.ShapeDtypeStruct((N_DEV,) + shard.shape, shard.dtype),
        scratch_shapes=[pltpu.SemaphoreType.DMA((N_DEV - 1,)), pltpu.SemaphoreType.DMA((N_DEV - 1,)),
                        pltpu.SemaphoreType.DMA],
    )(shard)


def _call(body, send, *, name, grid, in_specs, out_specs, out_shape, scratch_shapes, semantics, operands):
    if send is None:
        return pl.pallas_call(
            body, name=name, grid=grid, in_specs=in_specs, out_specs=out_specs, out_shape=out_shape,
            scratch_shapes=scratch_shapes, compiler_params=_params(semantics))(*operands), []
    srcs, kinds = [s for s, _ in send], [k for _, k in send]
    n, n_in, n_out, n_scr = len(srcs), len(in_specs), len(out_specs), len(scratch_shapes)
    steps = math.prod(grid)
    relay_step = (13 * steps) // 16

    def plan(refs):
        src_refs, land_refs = refs[n_in:n_in + n], refs[n_in + n + n_out:n_in + 2 * n + n_out]
        send_sems, recv_sems, local_sems = refs[-3:]
        x, y, c = lax.axis_index("x"), lax.axis_index("y"), lax.axis_index("c")
        me, peers = _peers()
        first, relayed_in, relayed_out, arrivals, sends, own = [], [], [], [], [], []
        for a, kind in enumerate(kinds):
            def copy(k, src, dst_slot, to):
                return pltpu.make_async_remote_copy(
                    src_ref=src, dst_ref=land_refs[a].at[dst_slot], send_sem=send_sems.at[a * (N_DEV - 1) + k],
                    recv_sem=recv_sems.at[a * (N_DEV - 1) + k], device_id=to, device_id_type=pl.DeviceIdType.MESH)

            if kind == "gather_by_chip":
                idx = lambda px, py, pc: 4 * px + 2 * py + pc
                chips = [(1 - x, y), (x, 1 - y), (1 - x, 1 - y)]
                mine = [copy(0, src_refs[a], me, (x, y, 1 - c))]
                arrivals.append(copy(0, src_refs[a], idx(x, y, 1 - c), (x, y, 1 - c)))
                for j, (px, py) in enumerate(chips):
                    mine.append(copy(1 + j, src_refs[a], me, (px, py, c)))
                    relayed_in.append(copy(1 + j, src_refs[a], idx(px, py, c), (px, py, c)))
                    relayed_out.append(copy(4 + j, land_refs[a].at[idx(px, py, c)], idx(px, py, c), (x, y, 1 - c)))
                    arrivals.append(copy(4 + j, src_refs[a], idx(px, py, 1 - c), (x, y, 1 - c)))
                first += mine
                sends += mine + relayed_out[-3:]
                own.append(pltpu.make_async_copy(src_refs[a], land_refs[a].at[me], local_sems.at[a]))
            elif kind == "scatter_by_chip":
                for k, (px, py) in enumerate([(1 - x, y), (x, 1 - y), (1 - x, 1 - y)]):
                    first.append(copy(k, src_refs[a].at[2 * px + py], 2 * x + y, (px, py, c)))
                    arrivals.append(copy(k, src_refs[a].at[2 * px + py], 2 * px + py, (px, py, c)))
                sends += first[-3:]
                own.append(pltpu.make_async_copy(src_refs[a].at[2 * x + y], land_refs[a].at[2 * x + y],
                                                 local_sems.at[a]))
            else:
                part = (lambda i: src_refs[a].at[i]) if kind == "scatter" else (lambda i: src_refs[a])
                for k, (peer, peer_idx) in enumerate(peers):
                    first.append(copy(k, part(peer_idx), me, peer))
                    arrivals.append(copy(k, part(peer_idx), peer_idx, peer))
                sends += first[-(N_DEV - 1):]
                own.append(pltpu.make_async_copy(part(me), land_refs[a].at[me], local_sems.at[a]))
        return first, relayed_in, relayed_out, arrivals, sends, own

    def wrapped(*refs):
        step = 0
        for axis, size in enumerate(grid):
            step = step * size + pl.program_id(axis)

        @pl.when(step == 0)
        def _():
            first, _, _, _, _, own = plan(refs)
            for cp in first + own:
                cp.start()

        if "gather_by_chip" in kinds:
            @pl.when(step == relay_step)
            def _():
                _, relayed_in, relayed_out, _, _, _ = plan(refs)
                for cp_in, cp_out in zip(relayed_in, relayed_out):
                    cp_in.wait_recv()
                    cp_out.start()

        body(*refs[:n_in], *refs[n_in + n:n_in + n + n_out], *refs[n_in + 2 * n + n_out:n_in + 2 * n + n_out + n_scr])

        @pl.when(step == steps - 1)
        def _():
            _, _, _, arrivals, sends, own = plan(refs)
            for cp in arrivals:
                cp.wait_recv()
            for cp in sends:
                cp.wait_send()
            for cp in own:
                cp.wait()

    anywhere = pl.BlockSpec(memory_space=pl.ANY)
    lands = [jax.ShapeDtypeStruct((N_DEV // 2 if k == "scatter_by_chip" else N_DEV,) + s.shape[-2:], s.dtype)
             for s, k in send]
    out = pl.pallas_call(
        wrapped, name=name, grid=grid,
        in_specs=list(in_specs) + [anywhere] * n, out_specs=list(out_specs) + [anywhere] * n,
        out_shape=list(out_shape) + lands,
        scratch_shapes=list(scratch_shapes) + [pltpu.SemaphoreType.DMA((n * (N_DEV - 1),)),
                                               pltpu.SemaphoreType.DMA((n * (N_DEV - 1),)),
                                               pltpu.SemaphoreType.DMA((n,))],
        compiler_params=_params(("arbitrary",) * len(grid)),
    )(*operands, *srcs)
    return out[:n_out], list(out[n_out:])


def _pair_swap(blocks):
    def body(src_ref, out_ref, send_sems, recv_sems):
        x, y, c = lax.axis_index("x"), lax.axis_index("y"), lax.axis_index("c")
        copies = [pltpu.make_async_remote_copy(
            src_ref=src_ref.at[2 * chip + (1 - c)], dst_ref=out_ref.at[chip], send_sem=send_sems.at[chip],
            recv_sem=recv_sems.at[chip], device_id=(x, y, 1 - c), device_id_type=pl.DeviceIdType.MESH)
            for chip in range(N_DEV // 2)]
        for cp in copies:
            cp.start()
        for cp in copies:
            cp.wait()

    return pl.pallas_call(
        body, name="pair_swap_grad_w_in",
        in_specs=[pl.BlockSpec(memory_space=pl.ANY)], out_specs=pl.BlockSpec(memory_space=pl.ANY),
        out_shape=jax.ShapeDtypeStruct((N_DEV // 2,) + blocks.shape[1:], blocks.dtype),
        scratch_shapes=[pltpu.SemaphoreType.DMA((N_DEV // 2,)), pltpu.SemaphoreType.DMA((N_DEV // 2,))],
    )(blocks)


def _pair_sum(blocks, swapped, core):
    _, rows, cols = swapped.shape
    tile_rows = _row_tile(rows)

    def body(core_ref, mine_ref, theirs_ref, o_ref):
        o_ref[...] = (mine_ref[...].astype(F32) + theirs_ref[...].astype(F32)).astype(o_ref.dtype)

    return pl.pallas_call(
        body, name="pair_sum_grad_w_in",
        grid_spec=pltpu.PrefetchScalarGridSpec(
            num_scalar_prefetch=1, grid=(N_DEV // 2, rows // tile_rows),
            in_specs=[pl.BlockSpec((None, tile_rows, cols), lambda j, i, core_ref: (2 * j + core_ref[0], i, 0)),
                      pl.BlockSpec((None, tile_rows, cols), lambda j, i, core_ref: (j, i, 0))],
            out_specs=pl.BlockSpec((None, tile_rows, cols), lambda j, i, core_ref: (j, i, 0))),
        out_shape=jax.ShapeDtypeStruct(swapped.shape, swapped.dtype),
        compiler_params=_params(("parallel", "parallel")),
    )(core, blocks, swapped)


def _sum_in_device_order(land_ref):
    acc = land_ref[0].astype(F32)
    for j in range(1, land_ref.shape[0]):
        acc = acc + land_ref[j].astype(F32)
    return acc


def _adam_math(w, g, m, v):
    c1 = 1.0 - ADAM_B1 ** ADAM_STEP
    c2 = 1.0 - ADAM_B2 ** ADAM_STEP
    m_new = ADAM_B1 * m + (1.0 - ADAM_B1) * g
    v_new = ADAM_B2 * v + (1.0 - ADAM_B2) * (g * g)
    delta = -ADAM_LR * ((m_new / c1) / (jnp.sqrt(v_new / c2) + ADAM_EPS) + ADAM_WD * w)
    return delta, m_new, v_new


def _row_tile(rows):
    return max(t for t in range(16, 385, 16) if rows % t == 0) if rows % 16 == 0 else rows


def _sum_update(land, w, m, v, name):
    slots, rows, cols = land.shape
    tile_rows = _row_tile(rows)

    def body(land_ref, w_ref, m_ref, v_ref, g_ref, d_ref, nm_ref, nv_ref):
        g = _sum_in_device_order(land_ref)
        g_ref[...] = g
        d_ref[...], nm_ref[...], nv_ref[...] = _adam_math(w_ref[...], g, m_ref[...], v_ref[...])

    tile = pl.BlockSpec((None, tile_rows, cols), lambda i: (0, i, 0))
    out = jax.ShapeDtypeStruct((1, rows, cols), F32)
    return pl.pallas_call(
        body, name=name, grid=(rows // tile_rows,),
        in_specs=[pl.BlockSpec((slots, tile_rows, cols), lambda i: (0, i, 0)), tile, tile, tile],
        out_specs=[tile] * 4, out_shape=[out] * 4,
        compiler_params=_params(("parallel",)),
    )(land, w, m, v)


def _to_bf16(arrays):
    def body(*refs):
        for src_ref, dst_ref in zip(refs[:len(arrays)], refs[len(arrays):]):
            dst_ref[...] = src_ref[...].astype(BF16)

    whole = [pl.BlockSpec(a.shape, lambda i: (0, 0)) for a in arrays]
    return pl.pallas_call(
        body, name="shards_to_bf16", grid=(1,), in_specs=whole, out_specs=whole,
        out_shape=[jax.ShapeDtypeStruct(a.shape, BF16) for a in arrays],
        compiler_params=_params(("arbitrary",)),
    )(*arrays)


N_GAINS = 3


def _sum_update_gains(land, ws, ms, vs):
    def body(land_ref, *refs):
        w_refs, m_refs, v_refs = (refs[k * N_GAINS:(k + 1) * N_GAINS] for k in range(3))
        loss_ref, out_refs = refs[3 * N_GAINS], refs[3 * N_GAINS + 1:]
        rows = _sum_in_device_order(land_ref)
        loss_ref[...] = rows[N_GAINS:N_GAINS + 1]
        for k in range(N_GAINS):
            g = rows[k:k + 1]
            g_ref, d_ref, nm_ref, nv_ref = out_refs[4 * k:4 * k + 4]
            g_ref[...] = g
            d_ref[...], nm_ref[...], nv_ref[...] = _adam_math(w_refs[k][...], g, m_refs[k][...], v_refs[k][...])

    row = pl.BlockSpec((1, D_MODEL), lambda i: (0, 0))
    out = jax.ShapeDtypeStruct((1, D_MODEL), F32)
    n_out = 1 + 4 * N_GAINS
    return pl.pallas_call(
        body, name="update_gains", grid=(1,),
        in_specs=[pl.BlockSpec(land.shape, lambda i: (0, 0, 0))] + [row] * (3 * N_GAINS),
        out_specs=[row] * n_out, out_shape=[out] * n_out,
    )(land, *ws, *ms, *vs)


GROUP_FFN = ("w_ffn_in", "w_ffn_out")
GROUP_MIX = ("w_sb_up", "w_dil_up", "w_out")
COL_SHARDED = ("w_in", "w_sb_up", "w_dil_up", "w_ffn_in")
TRANSPOSED = ("w_in", "w_ffn_in")


def _full_from_shards(name, slots):
    _, r, c = slots.shape
    if name in TRANSPOSED:
        return slots.reshape(N_DEV * r, c).T
    if name in COL_SHARDED:
        return slots.transpose(1, 0, 2).reshape(r, N_DEV * c)
    return slots.reshape(N_DEV * r, c)


def _row_shards(full):
    rows, cols = full.shape
    return full.reshape(N_DEV, rows // N_DEV, cols)


def _local_step(x, target, g_mix, g_ffn, g_fin, w_in_t, shards=None, rest=None):
    gather = lambda names, kind: None if shards is None else [(shards[n], kind) for n in names]
    scatter = lambda blocks: None if shards is None else [(t, "scatter") for t in blocks]
    landed = lambda blocks, lands: lands if lands else blocks

    w = {"w_in": w_in_t}
    if shards is None:
        w.update(rest)
        w["w_ffn_in"] = rest["w_ffn_in"].T
    (qkv_sb, qkv_dl, gates, u), _ = _norm_proj(x, g_mix, w["w_in"])
    qkv_sb = qkv_sb.reshape(B_LOC, SEQ, 3 * SB_WIDTH)
    qkv_dl = qkv_dl.reshape(B_LOC, SEQ, 3 * DIL_WIDTH)
    (o_sb,), lands = _sb_fwd(qkv_sb, gather(GROUP_FFN, "gather_by_chip"))
    if lands:
        w["w_ffn_in"], w["w_ffn_out"] = lands[0].reshape(2 * D_FF, D_MODEL), _full_from_shards("w_ffn_out", lands[1])
    o_sb = o_sb.reshape(TOK, SB_WIDTH)
    (o_dl, lse), lands = _dil_fwd(qkv_dl, gather(GROUP_MIX, "gather"))
    w.update({n: _full_from_shards(n, t) for n, t in zip(GROUP_MIX, lands)})
    o_dl = o_dl.reshape(TOK, DIL_OUT)

    loss, dx1, merged, u2, act, dh, dx2, dg_fin, dg_ffn = _mix_ffn_fwd_bwd(
        x, o_sb, o_dl, gates, w["w_sb_up"], w["w_dil_up"], w["w_out"], target, g_ffn, g_fin,
        w["w_ffn_in"], w["w_ffn_out"])
    dgates, dy_sb, dy_dl, do_sb, do_dl, dsum = _mix_bwd(dx1, o_sb, o_dl, gates, w["w_sb_up"], w["w_dil_up"], w["w_out"])
    blocks = {
        "w_sb_up": _atb(o_sb, dy_sb, "grad_w_sb_up", SB_WIDTH, D_MODEL, col_blocks=N_DEV),
        "w_dil_up": _atb(o_dl, dy_dl, "grad_w_dil_up", DIL_OUT, D_MODEL, col_blocks=N_DEV),
        "w_out": _row_shards(_atb(merged, dx1, "grad_w_out", D_MODEL, D_MODEL)),
        "w_ffn_in": _row_shards(_atb_cols(dh, u2, "grad_w_ffn_in", 512)),
        "w_ffn_out": _row_shards(_atb_cols(act, dx2, "grad_w_ffn_out", 256)),
    }
    grads = {}

    early, late = GROUP_FFN, GROUP_MIX
    early_blocks = [blocks[n] for n in early]
    (dq_sb, dk_sb, dv_sb), lands = _sb_bwd(qkv_sb, do_sb.reshape(B_LOC, SEQ, SB_WIDTH), scatter(early_blocks))
    grads.update(zip(early, landed(early_blocks, lands)))
    as_batch = lambda t: t.reshape(B_LOC, SEQ, DIL_OUT)
    late_blocks = [blocks[n] for n in late]
    d_dl, lands = _dil_bwd(qkv_dl, as_batch(do_dl), lse, as_batch(dsum), scatter(late_blocks))
    grads.update(zip(late, landed(late_blocks, lands)))
    flat = lambda t: t.reshape(TOK, -1)
    dproj = ([flat(dq_sb), flat(dk_sb), flat(dv_sb)]
             + [flat(d_dl[3 * grp + part]) for part in range(3) for grp in range(DIL_GROUPS)] + [dgates])

    w_in_blocks = _row_shards(_atb_pieces(u, dproj, "grad_w_in", D_MODEL // 2))
    if shards is None:
        grads["w_in"] = w_in_blocks
        send = None
    else:
        core = lax.axis_index("c").astype(jnp.int32).reshape(1)
        send = [(_pair_sum(w_in_blocks, _pair_swap(w_in_blocks), core), "scatter_by_chip")]
    (grad_x, dg_mix), lands = _proj_bwd(dproj, dx1, x, g_mix, w["w_in"], send)
    if lands:
        grads["w_in"] = lands[0]
    gain_grads = jnp.concatenate([dg_mix, dg_ffn, dg_fin], axis=0)
    return loss, grad_x, gain_grads, grads


def kernel(x, norm_mix_g, w_in, w_sb_up, w_dil_up, w_out, norm_ffn_g, w_ffn_in, w_ffn_out, norm_final_g, loss_target, m_norm_mix_g, m_w_in, m_w_sb_up, m_w_dil_up, m_w_out, m_norm_ffn_g, m_w_ffn_in, m_w_ffn_out, m_norm_final_g, v_norm_mix_g, v_w_in, v_w_sb_up, v_w_dil_up, v_w_out, v_norm_ffn_g, v_w_ffn_in, v_w_ffn_out, v_norm_final_g):
    mats = {"w_in": w_in, "w_sb_up": w_sb_up, "w_dil_up": w_dil_up, "w_out": w_out,
            "w_ffn_in": w_ffn_in, "w_ffn_out": w_ffn_out}
    moments_m = {"w_in": m_w_in, "w_sb_up": m_w_sb_up, "w_dil_up": m_w_dil_up, "w_out": m_w_out,
                 "w_ffn_in": m_w_ffn_in, "w_ffn_out": m_w_ffn_out}
    moments_v = {"w_in": v_w_in, "w_sb_up": v_w_sb_up, "w_dil_up": v_w_dil_up, "w_out": v_w_out,
                 "w_ffn_in": v_w_ffn_in, "w_ffn_out": v_w_ffn_out}
    stored = lambda t, name: t.transpose(0, 2, 1) if name in TRANSPOSED else t
    mats = {name: stored(t, name) for name, t in mats.items()}
    shards = dict(zip(mats, _to_bf16([t[0] for t in mats.values()])))
    gathered_w_in = _all_gather(shards.pop("w_in"), "all_gather_w_in")
    g_fin = norm_final_g.reshape(1, D_MODEL)
    loss, grad_x, gain_grads, grad_slots = _local_step(
        x.reshape(TOK, D_MODEL), loss_target.reshape(TOK, D_MODEL), norm_mix_g, norm_ffn_g, g_fin,
        gathered_w_in.reshape(IN_WIDTH, D_MODEL), shards=shards)

    gain_rows = jnp.concatenate([gain_grads, jnp.tile(loss, (1, D_MODEL // LANES)),
                                 jnp.zeros((8 - 4, D_MODEL), F32)], axis=0)
    row = lambda t: t.reshape(1, D_MODEL)
    loss_row, *gain_outs = _sum_update_gains(
        _all_gather(gain_rows, "all_gather_gains"),
        [norm_mix_g, norm_ffn_g, g_fin], [m_norm_mix_g, m_norm_ffn_g, row(m_norm_final_g)],
        [v_norm_mix_g, v_norm_ffn_g, row(v_norm_final_g)])

    out_g, out_d, out_m, out_v = {}, {}, {}, {}
    for name, slots in grad_slots.items():
        out_g[name], out_d[name], out_m[name], out_v[name] = [stored(t, name) for t in _sum_update(
            slots, mats[name], stored(moments_m[name], name), stored(moments_v[name], name), "update_" + name)]
    for idx, name in enumerate(("norm_mix_g", "norm_ffn_g", "norm_final_g")):
        shape = (D_MODEL,) if name == "norm_final_g" else (1, D_MODEL)
        out_g[name], out_d[name], out_m[name], out_v[name] = [t.reshape(shape) for t in gain_outs[4 * idx:4 * idx + 4]]

    order = ("norm_mix_g", "w_in", "w_sb_up", "w_dil_up", "w_out", "norm_ffn_g", "w_ffn_in", "w_ffn_out",
             "norm_final_g")
    return (loss_row[0, 0], grad_x.reshape(B_LOC, SEQ, D_MODEL),
            *[out_g[n] for n in order], *[out_d[n] for n in order],
            *[out_m[n] for n in order], *[out_v[n] for n in order])
```

```python
import math

import jax
import jax.numpy as jnp
from jax import lax
from jax.experimental import pallas as pl
from jax.experimental.pallas import tpu as pltpu

F32 = jnp.float32
BF16 = jnp.bfloat16

N_DEV = 8
D_MODEL = 1024
SEQ = 2048
B_LOC = 2
TOK = B_LOC * SEQ
HEAD_DIM = 64
SB_WIDTH = 512
DIL_WIDTH = 768
DIL_OUT = 256
QKV_WIDTH = 3 * SB_WIDTH + 3 * DIL_WIDTH
IN_WIDTH = QKV_WIDTH + 2 * D_MODEL
D_FF = 2816
DIL_PAIRS = ((128, 1), (512, 4), (2048, 16))
DIL_HEADS = 12
RMS_EPS = 1e-6
ALIBI_MAX_BIAS = 8.0
QK_SCALE = 1.0 / math.sqrt(HEAD_DIM)
BLK = 128
LANES = 128
NEG_BIG = -1e30

ADAM_LR = 0.001
ADAM_B1 = 0.9
ADAM_B2 = 0.999
ADAM_EPS = 1e-08
ADAM_WD = 0.01
ADAM_STEP = 10

VMEM_LIMIT = 58 * 1024 * 1024


def _dot(a, b):
    return jnp.dot(a, b, preferred_element_type=F32)


def _dot_nt(a, b):
    return lax.dot_general(a, b, (((1,), (1,)), ((), ())), preferred_element_type=F32)


def _dot_tn(a, b):
    return lax.dot_general(a, b, (((0,), (0,)), ((), ())), preferred_element_type=F32)


def _sigmoid(z):
    return 1.0 / (1.0 + jnp.exp(-z))


def _split_bf16(v):
    hi = v.astype(BF16)
    lo = (v - hi.astype(F32)).astype(BF16)
    return hi, lo


def _chunks(width, step=512):
    out, c = [], 0
    while c < width:
        w = min(step, width - c)
        out.append((c, w))
        c += w
    return out


def _resident(shape):
    nd = len(shape)
    return pl.BlockSpec(shape, lambda *_: (0,) * nd, pipeline_mode=pl.Buffered(1))


def _params(sem):
    return pltpu.CompilerParams(dimension_semantics=sem, vmem_limit_bytes=VMEM_LIMIT)


def _rms_fwd(x, g):
    r = lax.rsqrt(jnp.mean(x * x, axis=-1, keepdims=True) + RMS_EPS)
    n = x * r
    return n, r, n * g


def _rms_bwd(dy, n, r, g):
    dg = jnp.sum(dy * n, axis=0, keepdims=True)
    dn = dy * g
    dx = r * (dn - n * jnp.mean(dn * n, axis=-1, keepdims=True))
    return dx, dg


TM = 256


def _norm_proj(x, g, w_in_t, send=None):
    def body(x_ref, g_ref, w_ref, sb_ref, dl_ref, gate_ref, u_ref):
        _, _, u = _rms_fwd(x_ref[...], g_ref[...])
        u = u.astype(BF16)
        u_ref[...] = u
        for c0, w in _chunks(3 * SB_WIDTH):
            sb_ref[:, c0:c0 + w] = _dot_nt(u, w_ref[c0:c0 + w, :]).astype(BF16)
        for c0, w in _chunks(3 * DIL_WIDTH):
            dl_ref[:, c0:c0 + w] = _dot_nt(u, w_ref[3 * SB_WIDTH + c0:3 * SB_WIDTH + c0 + w, :])
        for c0, w in _chunks(2 * D_MODEL):
            gate_ref[:, c0:c0 + w] = _dot_nt(u, w_ref[QKV_WIDTH + c0:QKV_WIDTH + c0 + w, :])

    return _call(
        body, send, name="norm_proj", grid=(TOK // TM,),
        in_specs=[pl.BlockSpec((TM, D_MODEL), lambda i: (i, 0)), _resident((1, D_MODEL)),
                  _resident((IN_WIDTH, D_MODEL))],
        out_specs=[pl.BlockSpec((TM, 3 * SB_WIDTH), lambda i: (i, 0)),
                   pl.BlockSpec((TM, 3 * DIL_WIDTH), lambda i: (i, 0)),
                   pl.BlockSpec((TM, 2 * D_MODEL), lambda i: (i, 0)),
                   pl.BlockSpec((TM, D_MODEL), lambda i: (i, 0))],
        out_shape=[jax.ShapeDtypeStruct((TOK, 3 * SB_WIDTH), BF16),
                   jax.ShapeDtypeStruct((TOK, 3 * DIL_WIDTH), F32),
                   jax.ShapeDtypeStruct((TOK, 2 * D_MODEL), F32),
                   jax.ShapeDtypeStruct((TOK, D_MODEL), BF16)],
        scratch_shapes=[], semantics=("parallel",), operands=(x, g, w_in_t))


FF_CHUNK = 1024


def _mix_ffn_fwd_bwd(x, o_sb, o_dl, gates, w_sb_up, w_dil_up, w_out, target, g_ffn, g_fin, w_ffn_in_t, w_ffn_out):
    def body(x_ref, osb_ref, odl_ref, gate_ref, wsb_ref, wdl_ref, wo_ref, t_ref, gffn_ref, gfin_ref, win_ref, wout_ref,
             loss_ref, dx1_ref, mg_ref, u2_ref, act_ref, dh_ref, dx2_ref, dgfin_ref, dgffn_ref, h_scr):
        i = pl.program_id(0)

        @pl.when(i == 0)
        def _():
            loss_ref[...] = jnp.zeros_like(loss_ref)
            dgfin_ref[...] = jnp.zeros_like(dgfin_ref)
            dgffn_ref[...] = jnp.zeros_like(dgffn_ref)

        y_sb = _dot(osb_ref[...], wsb_ref[...])
        y_dl = _dot(odl_ref[...].astype(BF16), wdl_ref[...])
        merged = (_sigmoid(gate_ref[:, :D_MODEL]) * y_sb
                  + _sigmoid(gate_ref[:, D_MODEL:]) * y_dl).astype(BF16)
        mg_ref[...] = merged
        x1 = x_ref[...] + _dot(merged, wo_ref[...])
        g_ffn_v = gffn_ref[...]
        g_fin_v = gfin_ref[...]
        n2, r2, u2 = _rms_fwd(x1, g_ffn_v)
        u2 = u2.astype(BF16)
        u2_ref[...] = u2
        x2 = x1
        for c0, w in _chunks(D_FF, FF_CHUNK):
            gate = _dot_nt(u2, win_ref[c0:c0 + w, :])
            up = _dot_nt(u2, win_ref[D_FF + c0:D_FF + c0 + w, :])
            h_scr[:, c0:c0 + w] = gate
            h_scr[:, D_FF + c0:D_FF + c0 + w] = up
            act = (gate * _sigmoid(gate) * up).astype(BF16)
            act_ref[:, c0:c0 + w] = act
            x2 = x2 + _dot(act, wout_ref[c0:c0 + w, :])
        n3, r3, y = _rms_fwd(x2, g_fin_v)
        err = y - t_ref[...]
        sq = jnp.sum(jnp.sum(err * err, axis=1, keepdims=True), axis=0, keepdims=True)
        loss_ref[...] += sq * (0.5 / D_MODEL)
        dx2, dgfin = _rms_bwd(err * (1.0 / D_MODEL), n3, r3, g_fin_v)
        dgfin_ref[...] += dgfin
        dx2_b = dx2.astype(BF16)
        dx2_ref[...] = dx2_b
        du2 = jnp.zeros((TM, D_MODEL), F32)
        for c0, w in _chunks(D_FF, FF_CHUNK):
            gate = h_scr[:, c0:c0 + w]
            up = h_scr[:, D_FF + c0:D_FF + c0 + w]
            dact = _dot_nt(dx2_b, wout_ref[c0:c0 + w, :])
            sg = _sigmoid(gate)
            dgate = (dact * up * (sg * (1.0 + gate * (1.0 - sg)))).astype(BF16)
            dup = (dact * (gate * sg)).astype(BF16)
            dh_ref[:, c0:c0 + w] = dgate
            dh_ref[:, D_FF + c0:D_FF + c0 + w] = dup
            du2 = du2 + _dot(dgate, win_ref[c0:c0 + w, :])
            du2 = du2 + _dot(dup, win_ref[D_FF + c0:D_FF + c0 + w, :])
        dx1_n, dgffn = _rms_bwd(du2, n2, r2, g_ffn_v)
        dgffn_ref[...] += dgffn
        dx1_ref[...] = dx2 + dx1_n

    tile = lambda w: pl.BlockSpec((TM, w), lambda i: (i, 0))
    acc = lambda w: pl.BlockSpec((1, w), lambda i: (0, 0))
    return pl.pallas_call(
        body, name="mix_ffn_fwd_bwd", grid=(TOK // TM,),
        in_specs=[tile(D_MODEL), tile(SB_WIDTH), tile(DIL_OUT), tile(2 * D_MODEL),
                  _resident((SB_WIDTH, D_MODEL)), _resident((DIL_OUT, D_MODEL)), _resident((D_MODEL, D_MODEL)),
                  tile(D_MODEL), _resident((1, D_MODEL)), _resident((1, D_MODEL)),
                  _resident((2 * D_FF, D_MODEL)), _resident((D_FF, D_MODEL))],
        out_specs=[acc(LANES), tile(D_MODEL), tile(D_MODEL), tile(D_MODEL), tile(D_FF), tile(2 * D_FF),
                   tile(D_MODEL), acc(D_MODEL), acc(D_MODEL)],
        out_shape=[jax.ShapeDtypeStruct((1, LANES), F32),
                   jax.ShapeDtypeStruct((TOK, D_MODEL), F32),
                   jax.ShapeDtypeStruct((TOK, D_MODEL), BF16),
                   jax.ShapeDtypeStruct((TOK, D_MODEL), BF16),
                   jax.ShapeDtypeStruct((TOK, D_FF), BF16),
                   jax.ShapeDtypeStruct((TOK, 2 * D_FF), BF16),
                   jax.ShapeDtypeStruct((TOK, D_MODEL), BF16),
                   jax.ShapeDtypeStruct((1, D_MODEL), F32),
                   jax.ShapeDtypeStruct((1, D_MODEL), F32)],
        scratch_shapes=[pltpu.VMEM((TM, 2 * D_FF), F32)],
        compiler_params=_params(("arbitrary",)),
    )(x, o_sb, o_dl, gates, w_sb_up, w_dil_up, w_out, target, g_ffn, g_fin, w_ffn_in_t, w_ffn_out)


def _mix_bwd(dx1, o_sb, o_dl, gates, w_sb_up, w_dil_up, w_out):
    def body(dx1_ref, osb_ref, odl_ref, gate_ref, wsb_ref, wdl_ref, wout_ref,
             dgate_ref, dysb_ref, dydl_ref, dosb_ref, dodl_ref, dsum_ref):
        dmerged = _dot_nt(dx1_ref[...].astype(BF16), wout_ref[...])
        o_dl = odl_ref[...]
        y_sb = _dot(osb_ref[...], wsb_ref[...])
        y_dl = _dot(o_dl.astype(BF16), wdl_ref[...])
        s_sb = _sigmoid(gate_ref[:, :D_MODEL])
        s_dl = _sigmoid(gate_ref[:, D_MODEL:])
        dgate_ref[:, :D_MODEL] = (dmerged * y_sb * (s_sb * (1.0 - s_sb))).astype(BF16)
        dgate_ref[:, D_MODEL:] = (dmerged * y_dl * (s_dl * (1.0 - s_dl))).astype(BF16)
        dy_sb = (dmerged * s_sb).astype(BF16)
        dy_dl = (dmerged * s_dl).astype(BF16)
        dysb_ref[...] = dy_sb
        dydl_ref[...] = dy_dl
        dosb_ref[...] = _dot_nt(dy_sb, wsb_ref[...]).astype(BF16)
        do_dl = _dot_nt(dy_dl, wdl_ref[...])
        dodl_ref[...] = do_dl
        row = lax.broadcasted_iota(jnp.int32, (DIL_OUT, DIL_OUT), 0) // HEAD_DIM
        col = lax.broadcasted_iota(jnp.int32, (DIL_OUT, DIL_OUT), 1) // HEAD_DIM
        same_head = (row == col).astype(BF16)
        hi, lo = _split_bf16(do_dl * o_dl)
        dsum_ref[...] = _dot(hi, same_head) + _dot(lo, same_head)

    tile = lambda w: pl.BlockSpec((TM, w), lambda i: (i, 0))
    return pl.pallas_call(
        body, name="mix_bwd", grid=(TOK // TM,),
        in_specs=[tile(D_MODEL), tile(SB_WIDTH), tile(DIL_OUT), tile(2 * D_MODEL),
                  _resident((SB_WIDTH, D_MODEL)), _resident((DIL_OUT, D_MODEL)),
                  _resident((D_MODEL, D_MODEL))],
        out_specs=[tile(2 * D_MODEL), tile(D_MODEL), tile(D_MODEL), tile(SB_WIDTH), tile(DIL_OUT),
                   tile(DIL_OUT)],
        out_shape=[jax.ShapeDtypeStruct((TOK, 2 * D_MODEL), BF16),
                   jax.ShapeDtypeStruct((TOK, D_MODEL), BF16),
                   jax.ShapeDtypeStruct((TOK, D_MODEL), BF16),
                   jax.ShapeDtypeStruct((TOK, SB_WIDTH), BF16),
                   jax.ShapeDtypeStruct((TOK, DIL_OUT), F32),
                   jax.ShapeDtypeStruct((TOK, DIL_OUT), F32)],
        compiler_params=_params(("parallel",)),
    )(dx1, o_sb, o_dl, gates, w_sb_up, w_dil_up, w_out)


def _proj_bwd(dproj, dx1, x, g, w_in_t, send=None):
    widths = [p.shape[1] for p in dproj]

    def body(*refs):
        dx1_ref, x_ref, g_ref, w_ref, dx_ref, dg_ref = refs[len(widths):]

        @pl.when(pl.program_id(0) == 0)
        def _():
            dg_ref[...] = jnp.zeros_like(dg_ref)

        du = jnp.zeros((TM, D_MODEL), F32)
        c0 = 0
        for dp_ref, w in zip(refs, widths):
            du = du + _dot(dp_ref[...].astype(BF16), w_ref[c0:c0 + w, :])
            c0 += w
        g_v = g_ref[...]
        n, r, _ = _rms_fwd(x_ref[...], g_v)
        dx, dg = _rms_bwd(du, n, r, g_v)
        dg_ref[...] += dg
        dx_ref[...] = dx1_ref[...] + dx

    tile = lambda w: pl.BlockSpec((TM, w), lambda i: (i, 0))
    return _call(
        body, send, name="proj_bwd", grid=(TOK // TM,),
        in_specs=[tile(w) for w in widths] + [tile(D_MODEL), tile(D_MODEL), _resident((1, D_MODEL)),
                                              _resident((IN_WIDTH, D_MODEL))],
        out_specs=[tile(D_MODEL), pl.BlockSpec((1, D_MODEL), lambda i: (0, 0))],
        out_shape=[jax.ShapeDtypeStruct((TOK, D_MODEL), F32),
                   jax.ShapeDtypeStruct((1, D_MODEL), F32)],
        scratch_shapes=[], semantics=("arbitrary",), operands=(*dproj, dx1, x, g, w_in_t))


def _atb_pieces(a, pieces, name, tm, tk=512):
    m = a.shape[1]
    widths = [p.shape[1] for p in pieces]
    n = sum(widths)
    nk = TOK // tk

    def body(a_ref, *refs):
        o_ref, acc_ref = refs[len(widths):]
        k = pl.program_id(1)

        @pl.when(k == 0)
        def _():
            acc_ref[...] = jnp.zeros_like(acc_ref)

        a_v = a_ref[...]
        c0 = 0
        for p_ref, w in zip(refs, widths):
            acc_ref[:, c0:c0 + w] += _dot_tn(a_v, p_ref[...].astype(BF16))
            c0 += w

        @pl.when(k == nk - 1)
        def _():
            for c0, w in _chunks(n):
                o_ref[c0:c0 + w, :] = acc_ref[:, c0:c0 + w].T.astype(BF16)

    return pl.pallas_call(
        body, name=name, grid=(m // tm, nk),
        in_specs=[pl.BlockSpec((tk, tm), lambda i, k: (k, i))]
                 + [pl.BlockSpec((tk, w), lambda i, k: (k, 0)) for w in widths],
        out_specs=pl.BlockSpec((n, tm), lambda i, k: (0, i)),
        out_shape=jax.ShapeDtypeStruct((n, m), BF16),
        scratch_shapes=[pltpu.VMEM((tm, n), F32)],
        compiler_params=_params(("parallel", "arbitrary")),
    )(a, *pieces)


def _atb_cols(a, b, name, cols, tk=512):
    m, other = a.shape[1], b.shape[1]

    def body(a_ref, b_ref, o_ref):
        acc = jnp.zeros(o_ref.shape, F32)
        for k0 in range(0, TOK, tk):
            acc = acc + _dot_tn(a_ref[k0:k0 + tk, :], b_ref[k0:k0 + tk, :])
        o_ref[...] = acc.astype(BF16)

    assert m % cols == 0
    return pl.pallas_call(
        body, name=name, grid=(m // cols,),
        in_specs=[pl.BlockSpec((TOK, cols), lambda r: (0, r)), _resident((TOK, other))],
        out_specs=pl.BlockSpec((cols, other), lambda r: (r, 0)),
        out_shape=jax.ShapeDtypeStruct((m, other), BF16),
        compiler_params=_params(("parallel",)),
    )(a, b)


def _atb(a, b, name, tm, tn, col_blocks=0, tk=512):
    m, n = a.shape[1], b.shape[1]
    nk = TOK // tk

    def body(a_ref, b_ref, o_ref, acc_ref):
        k = pl.program_id(2)

        @pl.when(k == 0)
        def _():
            acc_ref[...] = jnp.zeros_like(acc_ref)

        acc_ref[...] += _dot_tn(a_ref[...].astype(BF16), b_ref[...].astype(BF16))

        @pl.when(k == nk - 1)
        def _():
            if col_blocks:
                width = n // col_blocks
                for blk in range(col_blocks):
                    o_ref[blk] = acc_ref[:, blk * width:(blk + 1) * width].astype(BF16)
            else:
                o_ref[...] = acc_ref[...].astype(BF16)

    if col_blocks:
        out_spec = pl.BlockSpec((col_blocks, tm, n // col_blocks), lambda i, j, k: (0, i, 0))
        out_shape = jax.ShapeDtypeStruct((col_blocks, m, n // col_blocks), BF16)
    else:
        out_spec = pl.BlockSpec((tm, tn), lambda i, j, k: (i, j))
        out_shape = jax.ShapeDtypeStruct((m, n), BF16)
    return pl.pallas_call(
        body, name=name, grid=(m // tm, n // tn, nk),
        in_specs=[pl.BlockSpec((tk, tm), lambda i, j, k: (k, i)),
                  pl.BlockSpec((tk, tn), lambda i, j, k: (k, j))],
        out_specs=out_spec, out_shape=out_shape,
        scratch_shapes=[pltpu.VMEM((tm, tn), F32)],
        compiler_params=_params(("parallel", "parallel", "arbitrary")),
    )(a, b)


SB_PAIRS = SB_WIDTH // LANES


def _two_heads(v, lane0):
    zero = jnp.zeros_like(v)
    return jnp.where(lane0, v, zero), jnp.where(lane0, zero, v)


SB_QBLK = 256
N_SB_STEPS = SEQ // SB_QBLK


SB_KCHUNK = 2 * BLK
SB_ROWS = 2 * SB_QBLK
SB_DEAD = -104.0


def _log_keep(z):
    neg_z = -z
    return jnp.minimum(neg_z, 0.0) - jnp.log(1.0 + jnp.exp(jnp.minimum(z, neg_z)))


def _stack_heads(v, lane0):
    return jnp.concatenate(_two_heads(v, lane0), axis=0)


def _block_sums(v, tri, split=True):
    halves = (v[:, :BLK], v[:, BLK:])
    stacked = jnp.concatenate(halves, axis=0)
    if split:
        hi, lo = _split_bf16(stacked)
        prod = _dot(jnp.concatenate([hi, lo], axis=0), tri)
        tri_sum = prod[:2 * SB_ROWS] + prod[2 * SB_ROWS:]
    else:
        tri_sum = _dot(stacked.astype(BF16), tri)
    sums = tuple(jnp.sum(h, axis=1, keepdims=True) for h in halves)
    return (tri_sum[:SB_ROWS], tri_sum[SB_ROWS:]), sums


def _sb_diag_mask():
    row = lax.broadcasted_iota(jnp.int32, (SB_ROWS, SB_KCHUNK), 0)
    col = lax.broadcasted_iota(jnp.int32, (SB_ROWS, SB_KCHUNK), 1)
    return col < jnp.where(row >= SB_QBLK, row - SB_QBLK, row)


def _sb_fwd(qkv, send=None):
    def body(q_ref, k_ref, v_ref, o_ref):
        i = pl.program_id(2)
        krow = lax.broadcasted_iota(jnp.int32, (BLK, BLK), 0)
        kcol = lax.broadcasted_iota(jnp.int32, (BLK, BLK), 1)
        later = (krow > kcol).astype(BF16)
        lane0 = lax.broadcasted_iota(jnp.int32, (SB_QBLK, LANES), 1) < HEAD_DIM
        q2 = _stack_heads(q_ref[0] * QK_SCALE, lane0)

        def chunk(c, carry, causal):
            acc, run = carry
            off = pl.multiple_of(c * SB_KCHUNK, SB_KCHUNK)
            z = _dot_nt(q2, k_ref[0, pl.ds(off, SB_KCHUNK), :])
            log_keep = _log_keep(z)
            if causal is not None:
                log_keep = jnp.where(causal, log_keep, 0.0)
            suffix, sums = _block_sums(log_keep, later)
            log_after = jnp.concatenate([suffix[0] + (run + sums[1]), suffix[1] + run], axis=1)
            a = jnp.exp(log_keep + z + log_after)
            if causal is not None:
                a = jnp.where(causal, a, 0.0)
            acc = acc + _dot(a.astype(BF16), v_ref[0, pl.ds(off, SB_KCHUNK), :])
            return acc, run + (sums[0] + sums[1])

        acc, run = chunk(i, (jnp.zeros((SB_ROWS, LANES), F32), jnp.zeros((SB_ROWS, 1), F32)), _sb_diag_mask())

        def some_alive(run):
            return (jnp.max(run) > SB_DEAD).astype(jnp.int32)

        def trip(state):
            t, _, acc, run = state
            acc, run = chunk(i - 1 - t, (acc, run), None)
            return t + 1, some_alive(run), acc, run

        _, _, acc, _ = lax.while_loop(lambda s: jnp.logical_and(s[0] < i, s[1] > 0), trip,
                                      (jnp.int32(0), some_alive(run), acc, run))
        o_ref[0] = jnp.where(lane0, acc[:SB_QBLK], acc[SB_QBLK:]).astype(BF16)

    blk = pl.BlockSpec((1, SB_QBLK, LANES), lambda b, h, i: (b, i, h))
    return _call(
        body, send, name="sb_fwd", grid=(B_LOC, SB_PAIRS, N_SB_STEPS),
        in_specs=[blk,
                  pl.BlockSpec((1, SEQ, LANES), lambda b, h, i: (b, 0, SB_PAIRS + h)),
                  pl.BlockSpec((1, SEQ, LANES), lambda b, h, i: (b, 0, 2 * SB_PAIRS + h))],
        out_specs=[blk], out_shape=[jax.ShapeDtypeStruct((B_LOC, SEQ, SB_WIDTH), BF16)],
        scratch_shapes=[], semantics=("parallel", "parallel", "arbitrary"), operands=(qkv, qkv, qkv))


def _sb_bwd(qkv, d_o, send=None):
    def body(q_ref, k_ref, v_ref, do_ref, dq_ref, dk_ref, dv_ref, dk_acc, dv_acc, z_scr, keep_scr):
        i = pl.program_id(2)
        krow = lax.broadcasted_iota(jnp.int32, (BLK, BLK), 0)
        kcol = lax.broadcasted_iota(jnp.int32, (BLK, BLK), 1)
        upto = (krow <= kcol).astype(BF16)
        earlier = (krow < kcol).astype(BF16)
        lane0 = lax.broadcasted_iota(jnp.int32, (SB_QBLK, LANES), 1) < HEAD_DIM
        q2 = _stack_heads(q_ref[0] * QK_SCALE, lane0)
        do2 = _stack_heads(do_ref[0], lane0)

        def keep_sum(c, causal):
            off = pl.multiple_of(c * SB_KCHUNK, SB_KCHUNK)
            z = _dot_nt(q2, k_ref[0, pl.ds(off, SB_KCHUNK), :])
            log_keep = _log_keep(z)
            if causal is not None:
                log_keep = jnp.where(causal, log_keep, 0.0)
            z_scr[c] = z
            keep_scr[c] = log_keep
            return jnp.sum(log_keep, axis=1, keepdims=True)

        def some_alive(run):
            return (jnp.max(run) > SB_DEAD).astype(jnp.int32)

        def scan(state):
            t, _, run = state
            run = run + keep_sum(i - 1 - t, None)
            return t + 1, some_alive(run), run

        diag_sum = keep_sum(i, _sb_diag_mask())
        walked, _, tot2 = lax.while_loop(lambda s: jnp.logical_and(s[0] < i, s[1] > 0), scan,
                                         (jnp.int32(0), some_alive(diag_sum), diag_sum))
        first = i - walked

        @pl.when(i == 0)
        def _():
            dk_acc[...] = jnp.zeros_like(dk_acc)
            dv_acc[...] = jnp.zeros_like(dv_acc)

        def chunk(c, carry, causal):
            dq, pre_keep, pre_e = carry
            off = pl.multiple_of(c * SB_KCHUNK, SB_KCHUNK)
            k_c = k_ref[0, pl.ds(off, SB_KCHUNK), :]
            v_c = v_ref[0, pl.ds(off, SB_KCHUNK), :]
            d_a = _dot_nt(do2, v_c)
            log_keep = keep_scr[c]
            log_beta = log_keep + z_scr[c]
            prefix, sums = _block_sums(log_keep, upto)
            inclusive = jnp.concatenate([prefix[0], prefix[1] + sums[0]], axis=1)
            a = jnp.exp(log_beta + ((tot2 - pre_keep) - inclusive))
            if causal is not None:
                a = jnp.where(causal, a, 0.0)
            e = d_a * a
            e_prefix, e_sums = _block_sums(e, earlier, split=False)
            before = jnp.concatenate([e_prefix[0] + pre_e, e_prefix[1] + (pre_e + e_sums[0])], axis=1)
            dz = e - (e + before) * jnp.exp(log_beta)
            if causal is not None:
                dz = jnp.where(causal, dz, 0.0)
            dz = dz.astype(BF16)
            dq = dq + _dot(dz, k_c)
            dk_acc[pl.ds(off, SB_KCHUNK), :] += _dot_tn(dz, q2)
            dv_acc[pl.ds(off, SB_KCHUNK), :] += _dot_tn(a.astype(BF16), do2)
            return dq, pre_keep + (sums[0] + sums[1]), pre_e + (e_sums[0] + e_sums[1])

        zero_col = jnp.zeros((SB_ROWS, 1), F32)
        carry = lax.fori_loop(first, i, lambda t, c: chunk(t, c, None),
                              (jnp.zeros((SB_ROWS, LANES), F32), zero_col, zero_col))
        dq, _, _ = chunk(i, carry, _sb_diag_mask())
        dq_ref[0] = (jnp.where(lane0, dq[:SB_QBLK], dq[SB_QBLK:]) * QK_SCALE).astype(BF16)

        @pl.when(i == N_SB_STEPS - 1)
        def _():
            dk_ref[0] = dk_acc[...].astype(BF16)
            dv_ref[0] = dv_acc[...].astype(BF16)

    blk = pl.BlockSpec((1, SB_QBLK, LANES), lambda b, h, i: (b, i, h))
    whole = lambda c: pl.BlockSpec((1, SEQ, LANES), lambda b, h, i: (b, 0, c * SB_PAIRS + h))
    out = jax.ShapeDtypeStruct((B_LOC, SEQ, SB_WIDTH), BF16)
    return _call(
        body, send, name="sb_bwd", grid=(B_LOC, SB_PAIRS, N_SB_STEPS),
        in_specs=[blk, whole(1), whole(2), blk],
        out_specs=[blk, whole(0), whole(0)],
        out_shape=[out, out, out],
        scratch_shapes=[pltpu.VMEM((SEQ, LANES), F32), pltpu.VMEM((SEQ, LANES), F32),
                        pltpu.VMEM((N_SB_STEPS, SB_ROWS, SB_KCHUNK), F32),
                        pltpu.VMEM((N_SB_STEPS, SB_ROWS, SB_KCHUNK), F32)],
        semantics=("parallel", "parallel", "arbitrary"), operands=(qkv, qkv, qkv, d_o))


DIL_GROUPS = len(DIL_PAIRS)
DIL_QBLOCKS = SEQ // BLK


def _residue_rows(j, dilation):
    length = SEQ // dilation
    return pl.ds(j, length, stride=dilation) if dilation > 1 else pl.ds(0, length)


def _gather_residues(src_ref, dst_ref, dst_off, dilation, scale=None):
    length = SEQ // dilation
    for j in range(dilation):
        v = src_ref[_residue_rows(j, dilation), :]
        if scale is not None:
            v = v * scale
        dst_ref[dst_off + j * length:dst_off + (j + 1) * length, :] = v.astype(dst_ref.dtype)


def _scatter_residues(src_ref, src_off, dst_ref, dilation):
    length = SEQ // dilation
    for j in range(dilation):
        dst_ref[_residue_rows(j, dilation), :] = (
            src_ref[src_off + j * length:src_off + (j + 1) * length, :].astype(dst_ref.dtype))


def _dil_geometry(group, pair):
    dilation = DIL_PAIRS[group][1]
    row = lax.broadcasted_iota(jnp.int32, (2 * BLK, 2 * BLK), 0)
    col = lax.broadcasted_iota(jnp.int32, (2 * BLK, 2 * BLK), 1)
    second = row >= BLK
    steps = BLK + jnp.where(second, row - BLK, row) - col
    coef = -ALIBI_MAX_BIAS / DIL_HEADS * math.log(2.0)
    first_head = float(4 * group + 1) + 2.0 * pair.astype(F32)
    slope = jnp.exp(coef * (first_head + jnp.where(second, 1.0, 0.0)))
    bias = slope * (steps * dilation).astype(F32)
    valid = jnp.logical_and(steps >= 0, steps <= BLK)
    return bias, valid, col >= BLK


def _dil_tile_scores(q2, kk, geometry, has_prev):
    bias, valid, own = geometry
    ok = jnp.logical_and(valid, jnp.logical_or(own, has_prev))
    return jnp.where(ok, _dot_nt(q2, kk) - bias, NEG_BIG)


def _head_col(v, lane_mask):
    return jnp.max(jnp.where(lane_mask, v, NEG_BIG), axis=1, keepdims=True)


def _dil_fwd(qkv, send=None):
    def body(*refs):
        ins, (o_ref, lse_ref), (qs, ks, vs, o_res, lse_res) = refs[:9], refs[9:11], refs[11:16]
        o_grp, lse_grp = refs[16:19], refs[19:22]
        pair = pl.program_id(1)
        lane0 = lax.broadcasted_iota(jnp.int32, (BLK, LANES), 1) < HEAD_DIM
        ks[0:BLK, :] = jnp.zeros((BLK, LANES), BF16)
        vs[0:BLK, :] = jnp.zeros((BLK, LANES), BF16)
        for grp, (_, dilation) in enumerate(DIL_PAIRS):
            q_ref, k_ref, v_ref = ins[3 * grp:3 * grp + 3]
            per_residue = DIL_QBLOCKS // dilation
            _gather_residues(q_ref, qs, 0, dilation, QK_SCALE)
            _gather_residues(k_ref, ks, BLK, dilation)
            _gather_residues(v_ref, vs, BLK, dilation)
            geometry = _dil_geometry(grp, pair)

            def step(blk, _):
                off = pl.multiple_of(blk * BLK, BLK)
                q2 = _stack_heads(qs[pl.ds(off, BLK), :], lane0)
                s = _dil_tile_scores(q2, ks[pl.ds(off, 2 * BLK), :], geometry, blk % per_residue != 0)
                m = jnp.max(s, axis=1, keepdims=True)
                p = jnp.exp(s - m)
                den = jnp.sum(p, axis=1, keepdims=True)
                out = _dot(p.astype(BF16), vs[pl.ds(off, 2 * BLK), :]) / den
                lse = m + jnp.log(den)
                o_res[pl.ds(off, BLK), :] = jnp.where(lane0, out[:BLK], out[BLK:])
                lse_res[pl.ds(off, BLK), :] = jnp.where(lane0, lse[:BLK], lse[BLK:])
                return 0

            lax.fori_loop(0, DIL_QBLOCKS, step, 0, unroll=8)
            _scatter_residues(o_res, 0, o_grp[grp], dilation)
            _scatter_residues(lse_res, 0, lse_grp[grp], dilation)

        for r0 in range(0, SEQ, 2 * BLK):
            rows = slice(r0, r0 + 2 * BLK)
            ls = [lse_grp[g][rows, :] for g in range(DIL_GROUPS)]
            m = jnp.maximum(jnp.maximum(ls[0], ls[1]), ls[2])
            w = [jnp.exp(l - m) for l in ls]
            den = w[0] + w[1] + w[2]
            o_ref[rows, :] = (w[0] * o_grp[0][rows, :] + w[1] * o_grp[1][rows, :] + w[2] * o_grp[2][rows, :]) / den
            lse_ref[rows, :] = m + jnp.log(den)

    def col(part, grp):
        return pl.BlockSpec((None, SEQ, LANES), lambda b, p: (b, 0, 6 * part + 2 * grp + p))

    out_spec = pl.BlockSpec((None, SEQ, LANES), lambda b, p: (b, 0, p))
    out = jax.ShapeDtypeStruct((B_LOC, SEQ, DIL_OUT), F32)
    return _call(
        body, send, name="dil_fwd", grid=(B_LOC, DIL_OUT // LANES),
        in_specs=[col(part, grp) for grp in range(DIL_GROUPS) for part in range(3)],
        out_specs=[out_spec, out_spec], out_shape=[out, out],
        scratch_shapes=[pltpu.VMEM((SEQ, LANES), BF16), pltpu.VMEM((SEQ + BLK, LANES), BF16),
                        pltpu.VMEM((SEQ + BLK, LANES), BF16), pltpu.VMEM((SEQ, LANES), F32),
                        pltpu.VMEM((SEQ, LANES), F32)] + [pltpu.VMEM((SEQ, LANES), F32)] * (2 * DIL_GROUPS),
        semantics=("parallel", "parallel"), operands=[qkv] * 9)


def _dil_bwd(qkv, d_o, lse, dsum, send=None):
    def body(*refs):
        ins, (do_ref, lse_ref, dsum_ref), outs = refs[:9], refs[9:12], refs[12:21]
        qs, ks, vs, dos, lse_res, dsum_res, dq_res, dk_acc, dv_acc = refs[21:]
        pair = pl.program_id(1)
        lane0 = lax.broadcasted_iota(jnp.int32, (BLK, LANES), 1) < HEAD_DIM
        lane1 = jnp.logical_not(lane0)
        ks[0:BLK, :] = jnp.zeros((BLK, LANES), BF16)
        vs[0:BLK, :] = jnp.zeros((BLK, LANES), BF16)
        for grp, (_, dilation) in enumerate(DIL_PAIRS):
            q_ref, k_ref, v_ref = ins[3 * grp:3 * grp + 3]
            dq_ref, dk_ref, dv_ref = outs[3 * grp:3 * grp + 3]
            per_residue = DIL_QBLOCKS // dilation
            _gather_residues(q_ref, qs, 0, dilation, QK_SCALE)
            _gather_residues(k_ref, ks, BLK, dilation)
            _gather_residues(v_ref, vs, BLK, dilation)
            _gather_residues(do_ref, dos, 0, dilation)
            _gather_residues(lse_ref, lse_res, 0, dilation)
            _gather_residues(dsum_ref, dsum_res, 0, dilation)
            dk_acc[...] = jnp.zeros_like(dk_acc)
            dv_acc[...] = jnp.zeros_like(dv_acc)
            geometry = _dil_geometry(grp, pair)

            def step(blk, _):
                off = pl.multiple_of(blk * BLK, BLK)
                q2 = _stack_heads(qs[pl.ds(off, BLK), :], lane0)
                do2 = _stack_heads(dos[pl.ds(off, BLK), :], lane0)
                kk = ks[pl.ds(off, 2 * BLK), :]
                vv = vs[pl.ds(off, 2 * BLK), :]
                lse_blk = lse_res[pl.ds(off, BLK), :]
                dsum_blk = dsum_res[pl.ds(off, BLK), :]
                lse2 = jnp.concatenate([_head_col(lse_blk, lane0), _head_col(lse_blk, lane1)], axis=0)
                dsum2 = jnp.concatenate([_head_col(dsum_blk, lane0), _head_col(dsum_blk, lane1)], axis=0)
                s = _dil_tile_scores(q2, kk, geometry, blk % per_residue != 0)
                p = jnp.exp(s - lse2)
                ds = (p * (_dot_nt(do2, vv) - dsum2)).astype(BF16)
                dq2 = _dot(ds, kk)
                dq_res[pl.ds(off, BLK), :] = jnp.where(lane0, dq2[:BLK], dq2[BLK:]) * QK_SCALE
                dk_acc[pl.ds(off, 2 * BLK), :] += _dot_tn(ds, q2)
                dv_acc[pl.ds(off, 2 * BLK), :] += _dot_tn(p.astype(BF16), do2)
                return 0

            lax.fori_loop(0, DIL_QBLOCKS, step, 0, unroll=8)
            _scatter_residues(dq_res, 0, dq_ref, dilation)
            _scatter_residues(dk_acc, BLK, dk_ref, dilation)
            _scatter_residues(dv_acc, BLK, dv_ref, dilation)

    def col(part, grp):
        return pl.BlockSpec((None, SEQ, LANES), lambda b, p: (b, 0, 6 * part + 2 * grp + p))

    slot = pl.BlockSpec((None, SEQ, LANES), lambda b, p: (b, 0, p))
    out = jax.ShapeDtypeStruct((B_LOC, SEQ, DIL_OUT), F32)
    return _call(
        body, send, name="dil_bwd", grid=(B_LOC, DIL_OUT // LANES),
        in_specs=[col(part, grp) for grp in range(DIL_GROUPS) for part in range(3)] + [slot] * 3,
        out_specs=[slot] * 9, out_shape=[out] * 9,
        scratch_shapes=[pltpu.VMEM((SEQ, LANES), BF16), pltpu.VMEM((SEQ + BLK, LANES), BF16),
                        pltpu.VMEM((SEQ + BLK, LANES), BF16), pltpu.VMEM((SEQ, LANES), BF16),
                        pltpu.VMEM((SEQ, LANES), F32), pltpu.VMEM((SEQ, LANES), F32),
                        pltpu.VMEM((SEQ, LANES), F32), pltpu.VMEM((SEQ + BLK, LANES), F32),
                        pltpu.VMEM((SEQ + BLK, LANES), F32)],
        semantics=("parallel", "parallel"), operands=[qkv] * 9 + [d_o, lse, dsum])


def _peers():
    x, y, c = lax.axis_index("x"), lax.axis_index("y"), lax.axis_index("c")
    me = 4 * x + 2 * y + c
    peers = []
    for mask in range(1, N_DEV):
        px = 1 - x if mask & 4 else x
        py = 1 - y if mask & 2 else y
        pc = 1 - c if mask & 1 else c
        peers.append(((px, py, pc), 4 * px + 2 * py + pc))
    return me, peers


def _all_gather(shard, name):
    rows, cols = shard.shape
    by_rows = rows % 32 == 0

    def body(src_ref, out_ref, send_sems, recv_sems, local_sem):
        x, y, c = lax.axis_index("x"), lax.axis_index("y"), lax.axis_index("c")
        me, sibling = (x, y, c), (x, y, 1 - c)
        x_chip, y_chip, across = (1 - x, y), (x, 1 - y), (1 - x, 1 - y)

        def slot(block, half=None):
            ref = out_ref.at[4 * block[0] + 2 * block[1] + block[2]]
            if half is None:
                return ref
            return ref.at[pl.ds(half * (rows // 2), rows // 2)] if by_rows else \
                ref.at[:, pl.ds(half * (cols // 2), cols // 2)]

        def copy(k, block, to, src=None, half=None):
            return pltpu.make_async_remote_copy(
                src_ref=slot(block, half) if src is None else src, dst_ref=slot(block, half),
                send_sem=send_sems.at[k], recv_sem=recv_sems.at[k], device_id=to,
                device_id_type=pl.DeviceIdType.MESH)

        mine = pltpu.make_async_copy(src_ref, slot(me), local_sem)
        mine.start()
        sent = [copy(0, me, sibling, src=src_ref), copy(1, me, (*x_chip, c), src=src_ref),
                copy(2, me, (*y_chip, c), src=src_ref)]
        for cp in sent:
            cp.start()

        def arrived(k, block, half=None):
            copy(k, block, me, half=half).wait_recv()
            onward = {1: [copy(3, block, (*y_chip, c), half=0), copy(5, block, sibling)],
                      2: [copy(4, block, (*x_chip, c), half=1), copy(6, block, sibling)],
                      3: [copy(7, block, sibling, half=0)],
                      4: [copy(8, block, sibling, half=1)]}.get(k, [])
            for cp in onward:
                cp.start()
            sent.extend(onward)

        arrived(1, (*x_chip, c))
        arrived(2, (*y_chip, c))
        arrived(3, (*across, c), half=0)
        arrived(4, (*across, c), half=1)
        arrived(0, sibling)
        arrived(5, (*x_chip, 1 - c))
        arrived(6, (*y_chip, 1 - c))
        arrived(7, (*across, 1 - c), half=0)
        arrived(8, (*across, 1 - c), half=1)
        for cp in sent:
            cp.wait_send()
        mine.wait()

    n_sems = 9
    return pl.pallas_call(
        body, name=name,
        in_specs=[pl.BlockSpec(memory_space=pl.ANY)],
        out_specs=pl.BlockSpec(memory_space=pl.ANY),
        out_shape=jax.ShapeDtypeStruct((N_DEV,) + shard.shape, shard.dtype),
        scratch_shapes=[pltpu.SemaphoreType.DMA((n_sems,)), pltpu.SemaphoreType.DMA((n_sems,)),
                        pltpu.SemaphoreType.DMA],
    )(shard)


def _call(body, send, *, name, grid, in_specs, out_specs, out_shape, scratch_shapes, semantics, operands):
    if send is None:
        return pl.pallas_call(
            body, name=name, grid=grid, in_specs=in_specs, out_specs=out_specs, out_shape=out_shape,
            scratch_shapes=scratch_shapes, compiler_params=_params(semantics))(*operands), []
    srcs, kinds = [s for s, _ in send], [k for _, k in send]
    n, n_in, n_out, n_scr = len(srcs), len(in_specs), len(out_specs), len(scratch_shapes)
    steps = math.prod(grid)
    relay_step = (13 * steps) // 16

    def plan(refs):
        src_refs, land_refs = refs[n_in:n_in + n], refs[n_in + n + n_out:n_in + 2 * n + n_out]
        send_sems, recv_sems, local_sems = refs[-3:]
        x, y, c = lax.axis_index("x"), lax.axis_index("y"), lax.axis_index("c")
        me, peers = _peers()
        first, relayed_in, relayed_out, arrivals, sends, own = [], [], [], [], [], []
        for a, kind in enumerate(kinds):
            def copy(k, src, dst_slot, to):
                return pltpu.make_async_remote_copy(
                    src_ref=src, dst_ref=land_refs[a].at[dst_slot], send_sem=send_sems.at[a * (N_DEV - 1) + k],
                    recv_sem=recv_sems.at[a * (N_DEV - 1) + k], device_id=to, device_id_type=pl.DeviceIdType.MESH)

            if kind == "gather_by_chip":
                idx = lambda px, py, pc: 4 * px + 2 * py + pc
                chips = [(1 - x, y), (x, 1 - y), (1 - x, 1 - y)]
                mine = [copy(0, src_refs[a], me, (x, y, 1 - c))]
                arrivals.append(copy(0, src_refs[a], idx(x, y, 1 - c), (x, y, 1 - c)))
                for j, (px, py) in enumerate(chips):
                    mine.append(copy(1 + j, src_refs[a], me, (px, py, c)))
                    relayed_in.append(copy(1 + j, src_refs[a], idx(px, py, c), (px, py, c)))
                    relayed_out.append(copy(4 + j, land_refs[a].at[idx(px, py, c)], idx(px, py, c), (x, y, 1 - c)))
                    arrivals.append(copy(4 + j, src_refs[a], idx(px, py, 1 - c), (x, y, 1 - c)))
                first += mine
                sends += mine + relayed_out[-3:]
                own.append(pltpu.make_async_copy(src_refs[a], land_refs[a].at[me], local_sems.at[a]))
            elif kind == "scatter_by_chip":
                for k, (px, py) in enumerate([(1 - x, y), (x, 1 - y), (1 - x, 1 - y)]):
                    first.append(copy(k, src_refs[a].at[2 * px + py], 2 * x + y, (px, py, c)))
                    arrivals.append(copy(k, src_refs[a].at[2 * px + py], 2 * px + py, (px, py, c)))
                sends += first[-3:]
                own.append(pltpu.make_async_copy(src_refs[a].at[2 * x + y], land_refs[a].at[2 * x + y],
                                                 local_sems.at[a]))
            else:
                part = (lambda i: src_refs[a].at[i]) if kind == "scatter" else (lambda i: src_refs[a])
                for k, (peer, peer_idx) in enumerate(peers):
                    first.append(copy(k, part(peer_idx), me, peer))
                    arrivals.append(copy(k, part(peer_idx), peer_idx, peer))
                sends += first[-(N_DEV - 1):]
                own.append(pltpu.make_async_copy(part(me), land_refs[a].at[me], local_sems.at[a]))
        return first, relayed_in, relayed_out, arrivals, sends, own

    def wrapped(*refs):
        step = 0
        for axis, size in enumerate(grid):
            step = step * size + pl.program_id(axis)

        @pl.when(step == 0)
        def _():
            first, _, _, _, _, own = plan(refs)
            for cp in first + own:
                cp.start()

        if "gather_by_chip" in kinds:
            @pl.when(step == relay_step)
            def _():
                _, relayed_in, relayed_out, _, _, _ = plan(refs)
                for cp_in, cp_out in zip(relayed_in, relayed_out):
                    cp_in.wait_recv()
                    cp_out.start()

        body(*refs[:n_in], *refs[n_in + n:n_in + n + n_out], *refs[n_in + 2 * n + n_out:n_in + 2 * n + n_out + n_scr])

        @pl.when(step == steps - 1)
        def _():
            _, _, _, arrivals, sends, own = plan(refs)
            for cp in arrivals:
                cp.wait_recv()
            for cp in sends:
                cp.wait_send()
            for cp in own:
                cp.wait()

    anywhere = pl.BlockSpec(memory_space=pl.ANY)
    lands = [jax.ShapeDtypeStruct((N_DEV // 2 if k == "scatter_by_chip" else N_DEV,) + s.shape[-2:], s.dtype)
             for s, k in send]
    out = pl.pallas_call(
        wrapped, name=name, grid=grid,
        in_specs=list(in_specs) + [anywhere] * n, out_specs=list(out_specs) + [anywhere] * n,
        out_shape=list(out_shape) + lands,
        scratch_shapes=list(scratch_shapes) + [pltpu.SemaphoreType.DMA((n * (N_DEV - 1),)),
                                               pltpu.SemaphoreType.DMA((n * (N_DEV - 1),)),
                                               pltpu.SemaphoreType.DMA((n,))],
        compiler_params=_params(("arbitrary",) * len(grid)),
    )(*operands, *srcs)
    return out[:n_out], list(out[n_out:])


def _pair_swap(blocks):
    def body(src_ref, out_ref, send_sems, recv_sems):
        x, y, c = lax.axis_index("x"), lax.axis_index("y"), lax.axis_index("c")
        copies = [pltpu.make_async_remote_copy(
            src_ref=src_ref.at[2 * chip + (1 - c)], dst_ref=out_ref.at[chip], send_sem=send_sems.at[chip],
            recv_sem=recv_sems.at[chip], device_id=(x, y, 1 - c), device_id_type=pl.DeviceIdType.MESH)
            for chip in range(N_DEV // 2)]
        for cp in copies:
            cp.start()
        for cp in copies:
            cp.wait()

    return pl.pallas_call(
        body, name="pair_swap_grad_w_in",
        in_specs=[pl.BlockSpec(memory_space=pl.ANY)], out_specs=pl.BlockSpec(memory_space=pl.ANY),
        out_shape=jax.ShapeDtypeStruct((N_DEV // 2,) + blocks.shape[1:], blocks.dtype),
        scratch_shapes=[pltpu.SemaphoreType.DMA((N_DEV // 2,)), pltpu.SemaphoreType.DMA((N_DEV // 2,))],
    )(blocks)


def _pair_sum(blocks, swapped, core):
    _, rows, cols = swapped.shape
    tile_rows = _row_tile(rows)

    def body(core_ref, mine_ref, theirs_ref, o_ref):
        o_ref[...] = (mine_ref[...].astype(F32) + theirs_ref[...].astype(F32)).astype(o_ref.dtype)

    return pl.pallas_call(
        body, name="pair_sum_grad_w_in",
        grid_spec=pltpu.PrefetchScalarGridSpec(
            num_scalar_prefetch=1, grid=(N_DEV // 2, rows // tile_rows),
            in_specs=[pl.BlockSpec((None, tile_rows, cols), lambda j, i, core_ref: (2 * j + core_ref[0], i, 0)),
                      pl.BlockSpec((None, tile_rows, cols), lambda j, i, core_ref: (j, i, 0))],
            out_specs=pl.BlockSpec((None, tile_rows, cols), lambda j, i, core_ref: (j, i, 0))),
        out_shape=jax.ShapeDtypeStruct(swapped.shape, swapped.dtype),
        compiler_params=_params(("parallel", "parallel")),
    )(core, blocks, swapped)


def _sum_in_device_order(land_ref):
    acc = land_ref[0].astype(F32)
    for j in range(1, land_ref.shape[0]):
        acc = acc + land_ref[j].astype(F32)
    return acc


def _adam_math(w, g, m, v):
    c1 = 1.0 - ADAM_B1 ** ADAM_STEP
    c2 = 1.0 - ADAM_B2 ** ADAM_STEP
    m_new = ADAM_B1 * m + (1.0 - ADAM_B1) * g
    v_new = ADAM_B2 * v + (1.0 - ADAM_B2) * (g * g)
    delta = -ADAM_LR * ((m_new / c1) / (jnp.sqrt(v_new / c2) + ADAM_EPS) + ADAM_WD * w)
    return delta, m_new, v_new


def _row_tile(rows):
    return max(t for t in range(16, 385, 16) if rows % t == 0) if rows % 16 == 0 else rows


def _sum_update(land, w, m, v, name):
    slots, rows, cols = land.shape
    tile_rows = _row_tile(rows)

    def body(land_ref, w_ref, m_ref, v_ref, g_ref, d_ref, nm_ref, nv_ref):
        g = _sum_in_device_order(land_ref)
        g_ref[...] = g
        d_ref[...], nm_ref[...], nv_ref[...] = _adam_math(w_ref[...], g, m_ref[...], v_ref[...])

    tile = pl.BlockSpec((None, tile_rows, cols), lambda i: (0, i, 0))
    out = jax.ShapeDtypeStruct((1, rows, cols), F32)
    return pl.pallas_call(
        body, name=name, grid=(rows // tile_rows,),
        in_specs=[pl.BlockSpec((slots, tile_rows, cols), lambda i: (0, i, 0)), tile, tile, tile],
        out_specs=[tile] * 4, out_shape=[out] * 4,
        compiler_params=_params(("parallel",)),
    )(land, w, m, v)


def _to_bf16(arrays):
    def body(*refs):
        for src_ref, dst_ref in zip(refs[:len(arrays)], refs[len(arrays):]):
            dst_ref[...] = src_ref[...].astype(BF16)

    whole = [pl.BlockSpec(a.shape, lambda i: (0, 0)) for a in arrays]
    return pl.pallas_call(
        body, name="shards_to_bf16", grid=(1,), in_specs=whole, out_specs=whole,
        out_shape=[jax.ShapeDtypeStruct(a.shape, BF16) for a in arrays],
        compiler_params=_params(("arbitrary",)),
    )(*arrays)


N_GAINS = 3


def _sum_update_gains(land, ws, ms, vs):
    def body(land_ref, *refs):
        w_refs, m_refs, v_refs = (refs[k * N_GAINS:(k + 1) * N_GAINS] for k in range(3))
        loss_ref, out_refs = refs[3 * N_GAINS], refs[3 * N_GAINS + 1:]
        rows = _sum_in_device_order(land_ref)
        loss_ref[...] = rows[N_GAINS:N_GAINS + 1]
        for k in range(N_GAINS):
            g = rows[k:k + 1]
            g_ref, d_ref, nm_ref, nv_ref = out_refs[4 * k:4 * k + 4]
            g_ref[...] = g
            d_ref[...], nm_ref[...], nv_ref[...] = _adam_math(w_refs[k][...], g, m_refs[k][...], v_refs[k][...])

    row = pl.BlockSpec((1, D_MODEL), lambda i: (0, 0))
    out = jax.ShapeDtypeStruct((1, D_MODEL), F32)
    n_out = 1 + 4 * N_GAINS
    return pl.pallas_call(
        body, name="update_gains", grid=(1,),
        in_specs=[pl.BlockSpec(land.shape, lambda i: (0, 0, 0))] + [row] * (3 * N_GAINS),
        out_specs=[row] * n_out, out_shape=[out] * n_out,
    )(land, *ws, *ms, *vs)


GROUP_FFN = ("w_ffn_in", "w_ffn_out")
GROUP_MIX = ("w_sb_up", "w_dil_up", "w_out")
COL_SHARDED = ("w_in", "w_sb_up", "w_dil_up", "w_ffn_in")
TRANSPOSED = ("w_in", "w_ffn_in")


def _full_from_shards(name, slots):
    _, r, c = slots.shape
    if name in TRANSPOSED:
        return slots.reshape(N_DEV * r, c).T
    if name in COL_SHARDED:
        return slots.transpose(1, 0, 2).reshape(r, N_DEV * c)
    return slots.reshape(N_DEV * r, c)


def _row_shards(full):
    rows, cols = full.shape
    return full.reshape(N_DEV, rows // N_DEV, cols)


def _local_step(x, target, g_mix, g_ffn, g_fin, w_in_t, shards=None, rest=None):
    gather = lambda names, kind: None if shards is None else [(shards[n], kind) for n in names]
    scatter = lambda blocks: None if shards is None else [(t, "scatter") for t in blocks]
    landed = lambda blocks, lands: lands if lands else blocks

    w = {"w_in": w_in_t}
    if shards is None:
        w.update(rest)
        w["w_ffn_in"] = rest["w_ffn_in"].T
    (qkv_sb, qkv_dl, gates, u), _ = _norm_proj(x, g_mix, w["w_in"])
    qkv_sb = qkv_sb.reshape(B_LOC, SEQ, 3 * SB_WIDTH)
    qkv_dl = qkv_dl.reshape(B_LOC, SEQ, 3 * DIL_WIDTH)
    (o_sb,), lands = _sb_fwd(qkv_sb, gather(GROUP_FFN, "gather_by_chip"))
    if lands:
        w["w_ffn_in"], w["w_ffn_out"] = lands[0].reshape(2 * D_FF, D_MODEL), _full_from_shards("w_ffn_out", lands[1])
    o_sb = o_sb.reshape(TOK, SB_WIDTH)
    (o_dl, lse), lands = _dil_fwd(qkv_dl, gather(GROUP_MIX, "gather"))
    w.update({n: _full_from_shards(n, t) for n, t in zip(GROUP_MIX, lands)})
    o_dl = o_dl.reshape(TOK, DIL_OUT)

    loss, dx1, merged, u2, act, dh, dx2, dg_fin, dg_ffn = _mix_ffn_fwd_bwd(
        x, o_sb, o_dl, gates, w["w_sb_up"], w["w_dil_up"], w["w_out"], target, g_ffn, g_fin,
        w["w_ffn_in"], w["w_ffn_out"])
    dgates, dy_sb, dy_dl, do_sb, do_dl, dsum = _mix_bwd(dx1, o_sb, o_dl, gates, w["w_sb_up"], w["w_dil_up"], w["w_out"])
    blocks = {
        "w_sb_up": _atb(o_sb, dy_sb, "grad_w_sb_up", SB_WIDTH, D_MODEL, col_blocks=N_DEV),
        "w_dil_up": _atb(o_dl, dy_dl, "grad_w_dil_up", DIL_OUT, D_MODEL, col_blocks=N_DEV),
        "w_out": _row_shards(_atb(merged, dx1, "grad_w_out", D_MODEL, D_MODEL)),
        "w_ffn_in": _row_shards(_atb_cols(dh, u2, "grad_w_ffn_in", 512)),
        "w_ffn_out": _row_shards(_atb_cols(act, dx2, "grad_w_ffn_out", 256)),
    }
    grads = {}

    early, late = GROUP_FFN, GROUP_MIX
    early_blocks = [blocks[n] for n in early]
    (dq_sb, dk_sb, dv_sb), lands = _sb_bwd(qkv_sb, do_sb.reshape(B_LOC, SEQ, SB_WIDTH), scatter(early_blocks))
    grads.update(zip(early, landed(early_blocks, lands)))
    as_batch = lambda t: t.reshape(B_LOC, SEQ, DIL_OUT)
    late_blocks = [blocks[n] for n in late]
    d_dl, lands = _dil_bwd(qkv_dl, as_batch(do_dl), lse, as_batch(dsum), scatter(late_blocks))
    grads.update(zip(late, landed(late_blocks, lands)))
    flat = lambda t: t.reshape(TOK, -1)
    dproj = ([flat(dq_sb), flat(dk_sb), flat(dv_sb)]
             + [flat(d_dl[3 * grp + part]) for part in range(3) for grp in range(DIL_GROUPS)] + [dgates])

    w_in_blocks = _row_shards(_atb_pieces(u, dproj, "grad_w_in", D_MODEL // 2))
    if shards is None:
        grads["w_in"] = w_in_blocks
        send = None
    else:
        core = lax.axis_index("c").astype(jnp.int32).reshape(1)
        send = [(_pair_sum(w_in_blocks, _pair_swap(w_in_blocks), core), "scatter_by_chip")]
    (grad_x, dg_mix), lands = _proj_bwd(dproj, dx1, x, g_mix, w["w_in"], send)
    if lands:
        grads["w_in"] = lands[0]
    gain_grads = jnp.concatenate([dg_mix, dg_ffn, dg_fin], axis=0)
    return loss, grad_x, gain_grads, grads


def kernel(x, norm_mix_g, w_in, w_sb_up, w_dil_up, w_out, norm_ffn_g, w_ffn_in, w_ffn_out, norm_final_g, loss_target, m_norm_mix_g, m_w_in, m_w_sb_up, m_w_dil_up, m_w_out, m_norm_ffn_g, m_w_ffn_in, m_w_ffn_out, m_norm_final_g, v_norm_mix_g, v_w_in, v_w_sb_up, v_w_dil_up, v_w_out, v_norm_ffn_g, v_w_ffn_in, v_w_ffn_out, v_norm_final_g):
    mats = {"w_in": w_in, "w_sb_up": w_sb_up, "w_dil_up": w_dil_up, "w_out": w_out,
            "w_ffn_in": w_ffn_in, "w_ffn_out": w_ffn_out}
    moments_m = {"w_in": m_w_in, "w_sb_up": m_w_sb_up, "w_dil_up": m_w_dil_up, "w_out": m_w_out,
                 "w_ffn_in": m_w_ffn_in, "w_ffn_out": m_w_ffn_out}
    moments_v = {"w_in": v_w_in, "w_sb_up": v_w_sb_up, "w_dil_up": v_w_dil_up, "w_out": v_w_out,
                 "w_ffn_in": v_w_ffn_in, "w_ffn_out": v_w_ffn_out}
    stored = lambda t, name: t.transpose(0, 2, 1) if name in TRANSPOSED else t
    mats = {name: stored(t, name) for name, t in mats.items()}
    shards = dict(zip(mats, _to_bf16([t[0] for t in mats.values()])))
    gathered_w_in = _all_gather(shards.pop("w_in"), "all_gather_w_in")
    g_fin = norm_final_g.reshape(1, D_MODEL)
    loss, grad_x, gain_grads, grad_slots = _local_step(
        x.reshape(TOK, D_MODEL), loss_target.reshape(TOK, D_MODEL), norm_mix_g, norm_ffn_g, g_fin,
        gathered_w_in.reshape(IN_WIDTH, D_MODEL), shards=shards)

    gain_rows = jnp.concatenate([gain_grads, jnp.tile(loss, (1, D_MODEL // LANES)),
                                 jnp.zeros((8 - 4, D_MODEL), F32)], axis=0)
    row = lambda t: t.reshape(1, D_MODEL)
    loss_row, *gain_outs = _sum_update_gains(
        _all_gather(gain_rows, "all_gather_gains"),
        [norm_mix_g, norm_ffn_g, g_fin], [m_norm_mix_g, m_norm_ffn_g, row(m_norm_final_g)],
        [v_norm_mix_g, v_norm_ffn_g, row(v_norm_final_g)])

    out_g, out_d, out_m, out_v = {}, {}, {}, {}
    for name, slots in grad_slots.items():
        out_g[name], out_d[name], out_m[name], out_v[name] = [stored(t, name) for t in _sum_update(
            slots, mats[name], stored(moments_m[name], name), stored(moments_v[name], name), "update_" + name)]
    for idx, name in enumerate(("norm_mix_g", "norm_ffn_g", "norm_final_g")):
        shape = (D_MODEL,) if name == "norm_final_g" else (1, D_MODEL)
        out_g[name], out_d[name], out_m[name], out_v[name] = [t.reshape(shape) for t in gain_outs[4 * idx:4 * idx + 4]]

    order = ("norm_mix_g", "w_in", "w_sb_up", "w_dil_up", "w_out", "norm_ffn_g", "w_ffn_in", "w_ffn_out",
             "norm_final_g")
    return (loss_row[0, 0], grad_x.reshape(B_LOC, SEQ, D_MODEL),
            *[out_g[n] for n in order], *[out_d[n] for n in order],
            *[out_m[n] for n in order], *[out_v[n] for n in order])
```

```python
import math

import jax
import jax.numpy as jnp
from jax import lax
from jax.experimental import pallas as pl
from jax.experimental.pallas import tpu as pltpu

F32 = jnp.float32
BF16 = jnp.bfloat16

N_DEV = 8
D_MODEL = 1024
SEQ = 2048
B_LOC = 2
TOK = B_LOC * SEQ
HEAD_DIM = 64
SB_WIDTH = 512
DIL_WIDTH = 768
DIL_OUT = 256
QKV_WIDTH = 3 * SB_WIDTH + 3 * DIL_WIDTH
IN_WIDTH = QKV_WIDTH + 2 * D_MODEL
D_FF = 2816
DIL_PAIRS = ((128, 1), (512, 4), (2048, 16))
DIL_HEADS = 12
RMS_EPS = 1e-6
ALIBI_MAX_BIAS = 8.0
QK_SCALE = 1.0 / math.sqrt(HEAD_DIM)
BLK = 128
LANES = 128
NEG_BIG = -1e30

ADAM_LR = 0.001
ADAM_B1 = 0.9
ADAM_B2 = 0.999
ADAM_EPS = 1e-08
ADAM_WD = 0.01
ADAM_STEP = 10

VMEM_LIMIT = 58 * 1024 * 1024


def _dot(a, b):
    return jnp.dot(a, b, preferred_element_type=F32)


def _dot_nt(a, b):
    return lax.dot_general(a, b, (((1,), (1,)), ((), ())), preferred_element_type=F32)


def _dot_tn(a, b):
    return lax.dot_general(a, b, (((0,), (0,)), ((), ())), preferred_element_type=F32)


def _sigmoid(z):
    return 1.0 / (1.0 + jnp.exp(-z))


def _split_bf16(v):
    hi = v.astype(BF16)
    lo = (v - hi.astype(F32)).astype(BF16)
    return hi, lo


def _chunks(width, step=512):
    out, c = [], 0
    while c < width:
        w = min(step, width - c)
        out.append((c, w))
        c += w
    return out


def _resident(shape):
    nd = len(shape)
    return pl.BlockSpec(shape, lambda *_: (0,) * nd, pipeline_mode=pl.Buffered(1))


def _params(sem):
    return pltpu.CompilerParams(dimension_semantics=sem, vmem_limit_bytes=VMEM_LIMIT)


def _rms_fwd(x, g):
    r = lax.rsqrt(jnp.mean(x * x, axis=-1, keepdims=True) + RMS_EPS)
    n = x * r
    return n, r, n * g


def _rms_bwd(dy, n, r, g):
    dg = jnp.sum(dy * n, axis=0, keepdims=True)
    dn = dy * g
    dx = r * (dn - n * jnp.mean(dn * n, axis=-1, keepdims=True))
    return dx, dg


TM = 256


def _norm_proj(x, g, w_in_t, send=None):
    def body(x_ref, g_ref, w_ref, sb_ref, dl_ref, gate_ref, u_ref):
        _, _, u = _rms_fwd(x_ref[...], g_ref[...])
        u = u.astype(BF16)
        u_ref[...] = u
        for c0, w in _chunks(3 * SB_WIDTH):
            sb_ref[:, c0:c0 + w] = _dot_nt(u, w_ref[c0:c0 + w, :]).astype(BF16)
        for c0, w in _chunks(3 * DIL_WIDTH):
            dl_ref[:, c0:c0 + w] = _dot_nt(u, w_ref[3 * SB_WIDTH + c0:3 * SB_WIDTH + c0 + w, :])
        for c0, w in _chunks(2 * D_MODEL):
            gate_ref[:, c0:c0 + w] = _dot_nt(u, w_ref[QKV_WIDTH + c0:QKV_WIDTH + c0 + w, :])

    return _call(
        body, send, name="norm_proj", grid=(TOK // TM,),
        in_specs=[pl.BlockSpec((TM, D_MODEL), lambda i: (i, 0)), _resident((1, D_MODEL)),
                  _resident((IN_WIDTH, D_MODEL))],
        out_specs=[pl.BlockSpec((TM, 3 * SB_WIDTH), lambda i: (i, 0)),
                   pl.BlockSpec((TM, 3 * DIL_WIDTH), lambda i: (i, 0)),
                   pl.BlockSpec((TM, 2 * D_MODEL), lambda i: (i, 0)),
                   pl.BlockSpec((TM, D_MODEL), lambda i: (i, 0))],
        out_shape=[jax.ShapeDtypeStruct((TOK, 3 * SB_WIDTH), BF16),
                   jax.ShapeDtypeStruct((TOK, 3 * DIL_WIDTH), F32),
                   jax.ShapeDtypeStruct((TOK, 2 * D_MODEL), F32),
                   jax.ShapeDtypeStruct((TOK, D_MODEL), BF16)],
        scratch_shapes=[], semantics=("parallel",), operands=(x, g, w_in_t))


FF_CHUNK = 1024


def _mix_ffn_fwd_bwd(x, o_sb, o_dl, gates, w_sb_up, w_dil_up, w_out, target, g_ffn, g_fin, w_ffn_in_t, w_ffn_out):
    def body(x_ref, osb_ref, odl_ref, gate_ref, wsb_ref, wdl_ref, wo_ref, t_ref, gffn_ref, gfin_ref, win_ref, wout_ref,
             loss_ref, dx1_ref, mg_ref, u2_ref, act_ref, dh_ref, dx2_ref, dgfin_ref, dgffn_ref, h_scr):
        i = pl.program_id(0)

        @pl.when(i == 0)
        def _():
            loss_ref[...] = jnp.zeros_like(loss_ref)
            dgfin_ref[...] = jnp.zeros_like(dgfin_ref)
            dgffn_ref[...] = jnp.zeros_like(dgffn_ref)

        y_sb = _dot(osb_ref[...], wsb_ref[...])
        y_dl = _dot(odl_ref[...].astype(BF16), wdl_ref[...])
        merged = (_sigmoid(gate_ref[:, :D_MODEL]) * y_sb
                  + _sigmoid(gate_ref[:, D_MODEL:]) * y_dl).astype(BF16)
        mg_ref[...] = merged
        x1 = x_ref[...] + _dot(merged, wo_ref[...])
        g_ffn_v = gffn_ref[...]
        g_fin_v = gfin_ref[...]
        n2, r2, u2 = _rms_fwd(x1, g_ffn_v)
        u2 = u2.astype(BF16)
        u2_ref[...] = u2
        x2 = x1
        for c0, w in _chunks(D_FF, FF_CHUNK):
            gate = _dot_nt(u2, win_ref[c0:c0 + w, :])
            up = _dot_nt(u2, win_ref[D_FF + c0:D_FF + c0 + w, :])
            h_scr[:, c0:c0 + w] = gate
            h_scr[:, D_FF + c0:D_FF + c0 + w] = up
            act = (gate * _sigmoid(gate) * up).astype(BF16)
            act_ref[:, c0:c0 + w] = act
            x2 = x2 + _dot(act, wout_ref[c0:c0 + w, :])
        n3, r3, y = _rms_fwd(x2, g_fin_v)
        err = y - t_ref[...]
        sq = jnp.sum(jnp.sum(err * err, axis=1, keepdims=True), axis=0, keepdims=True)
        loss_ref[...] += sq * (0.5 / D_MODEL)
        dx2, dgfin = _rms_bwd(err * (1.0 / D_MODEL), n3, r3, g_fin_v)
        dgfin_ref[...] += dgfin
        dx2_b = dx2.astype(BF16)
        dx2_ref[...] = dx2_b
        du2 = jnp.zeros((TM, D_MODEL), F32)
        for c0, w in _chunks(D_FF, FF_CHUNK):
            gate = h_scr[:, c0:c0 + w]
            up = h_scr[:, D_FF + c0:D_FF + c0 + w]
            dact = _dot_nt(dx2_b, wout_ref[c0:c0 + w, :])
            sg = _sigmoid(gate)
            dgate = (dact * up * (sg * (1.0 + gate * (1.0 - sg)))).astype(BF16)
            dup = (dact * (gate * sg)).astype(BF16)
            dh_ref[:, c0:c0 + w] = dgate
            dh_ref[:, D_FF + c0:D_FF + c0 + w] = dup
            du2 = du2 + _dot(dgate, win_ref[c0:c0 + w, :])
            du2 = du2 + _dot(dup, win_ref[D_FF + c0:D_FF + c0 + w, :])
        dx1_n, dgffn = _rms_bwd(du2, n2, r2, g_ffn_v)
        dgffn_ref[...] += dgffn
        dx1_ref[...] = dx2 + dx1_n

    tile = lambda w: pl.BlockSpec((TM, w), lambda i: (i, 0))
    acc = lambda w: pl.BlockSpec((1, w), lambda i: (0, 0))
    return pl.pallas_call(
        body, name="mix_ffn_fwd_bwd", grid=(TOK // TM,),
        in_specs=[tile(D_MODEL), tile(SB_WIDTH), tile(DIL_OUT), tile(2 * D_MODEL),
                  _resident((SB_WIDTH, D_MODEL)), _resident((DIL_OUT, D_MODEL)), _resident((D_MODEL, D_MODEL)),
                  tile(D_MODEL), _resident((1, D_MODEL)), _resident((1, D_MODEL)),
                  _resident((2 * D_FF, D_MODEL)), _resident((D_FF, D_MODEL))],
        out_specs=[acc(LANES), tile(D_MODEL), tile(D_MODEL), tile(D_MODEL), tile(D_FF), tile(2 * D_FF),
                   tile(D_MODEL), acc(D_MODEL), acc(D_MODEL)],
        out_shape=[jax.ShapeDtypeStruct((1, LANES), F32),
                   jax.ShapeDtypeStruct((TOK, D_MODEL), F32),
                   jax.ShapeDtypeStruct((TOK, D_MODEL), BF16),
                   jax.ShapeDtypeStruct((TOK, D_MODEL), BF16),
                   jax.ShapeDtypeStruct((TOK, D_FF), BF16),
                   jax.ShapeDtypeStruct((TOK, 2 * D_FF), BF16),
                   jax.ShapeDtypeStruct((TOK, D_MODEL), BF16),
                   jax.ShapeDtypeStruct((1, D_MODEL), F32),
                   jax.ShapeDtypeStruct((1, D_MODEL), F32)],
        scratch_shapes=[pltpu.VMEM((TM, 2 * D_FF), F32)],
        compiler_params=_params(("arbitrary",)),
    )(x, o_sb, o_dl, gates, w_sb_up, w_dil_up, w_out, target, g_ffn, g_fin, w_ffn_in_t, w_ffn_out)


def _mix_bwd(dx1, o_sb, o_dl, gates, w_sb_up, w_dil_up, w_out):
    def body(dx1_ref, osb_ref, odl_ref, gate_ref, wsb_ref, wdl_ref, wout_ref,
             dgate_ref, dysb_ref, dydl_ref, dosb_ref, dodl_ref, dsum_ref):
        dmerged = _dot_nt(dx1_ref[...].astype(BF16), wout_ref[...])
        o_dl = odl_ref[...]
        y_sb = _dot(osb_ref[...], wsb_ref[...])
        y_dl = _dot(o_dl.astype(BF16), wdl_ref[...])
        s_sb = _sigmoid(gate_ref[:, :D_MODEL])
        s_dl = _sigmoid(gate_ref[:, D_MODEL:])
        dgate_ref[:, :D_MODEL] = (dmerged * y_sb * (s_sb * (1.0 - s_sb))).astype(BF16)
        dgate_ref[:, D_MODEL:] = (dmerged * y_dl * (s_dl * (1.0 - s_dl))).astype(BF16)
        dy_sb = (dmerged * s_sb).astype(BF16)
        dy_dl = (dmerged * s_dl).astype(BF16)
        dysb_ref[...] = dy_sb
        dydl_ref[...] = dy_dl
        dosb_ref[...] = _dot_nt(dy_sb, wsb_ref[...]).astype(BF16)
        do_dl = _dot_nt(dy_dl, wdl_ref[...])
        dodl_ref[...] = do_dl
        row = lax.broadcasted_iota(jnp.int32, (DIL_OUT, DIL_OUT), 0) // HEAD_DIM
        col = lax.broadcasted_iota(jnp.int32, (DIL_OUT, DIL_OUT), 1) // HEAD_DIM
        same_head = (row == col).astype(BF16)
        hi, lo = _split_bf16(do_dl * o_dl)
        dsum_ref[...] = _dot(hi, same_head) + _dot(lo, same_head)

    tile = lambda w: pl.BlockSpec((TM, w), lambda i: (i, 0))
    return pl.pallas_call(
        body, name="mix_bwd", grid=(TOK // TM,),
        in_specs=[tile(D_MODEL), tile(SB_WIDTH), tile(DIL_OUT), tile(2 * D_MODEL),
                  _resident((SB_WIDTH, D_MODEL)), _resident((DIL_OUT, D_MODEL)),
                  _resident((D_MODEL, D_MODEL))],
        out_specs=[tile(2 * D_MODEL), tile(D_MODEL), tile(D_MODEL), tile(SB_WIDTH), tile(DIL_OUT),
                   tile(DIL_OUT)],
        out_shape=[jax.ShapeDtypeStruct((TOK, 2 * D_MODEL), BF16),
                   jax.ShapeDtypeStruct((TOK, D_MODEL), BF16),
                   jax.ShapeDtypeStruct((TOK, D_MODEL), BF16),
                   jax.ShapeDtypeStruct((TOK, SB_WIDTH), BF16),
                   jax.ShapeDtypeStruct((TOK, DIL_OUT), F32),
                   jax.ShapeDtypeStruct((TOK, DIL_OUT), F32)],
        compiler_params=_params(("parallel",)),
    )(dx1, o_sb, o_dl, gates, w_sb_up, w_dil_up, w_out)


def _proj_bwd(dproj, dx1, x, g, w_in_t, send=None):
    widths = [p.shape[1] for p in dproj]

    def body(*refs):
        dx1_ref, x_ref, g_ref, w_ref, dx_ref, dg_ref = refs[len(widths):]

        @pl.when(pl.program_id(0) == 0)
        def _():
            dg_ref[...] = jnp.zeros_like(dg_ref)

        du = jnp.zeros((TM, D_MODEL), F32)
        c0 = 0
        for dp_ref, w in zip(refs, widths):
            du = du + _dot(dp_ref[...].astype(BF16), w_ref[c0:c0 + w, :])
            c0 += w
        g_v = g_ref[...]
        n, r, _ = _rms_fwd(x_ref[...], g_v)
        dx, dg = _rms_bwd(du, n, r, g_v)
        dg_ref[...] += dg
        dx_ref[...] = dx1_ref[...] + dx

    tile = lambda w: pl.BlockSpec((TM, w), lambda i: (i, 0))
    return _call(
        body, send, name="proj_bwd", grid=(TOK // TM,),
        in_specs=[tile(w) for w in widths] + [tile(D_MODEL), tile(D_MODEL), _resident((1, D_MODEL)),
                                              _resident((IN_WIDTH, D_MODEL))],
        out_specs=[tile(D_MODEL), pl.BlockSpec((1, D_MODEL), lambda i: (0, 0))],
        out_shape=[jax.ShapeDtypeStruct((TOK, D_MODEL), F32),
                   jax.ShapeDtypeStruct((1, D_MODEL), F32)],
        scratch_shapes=[], semantics=("arbitrary",), operands=(*dproj, dx1, x, g, w_in_t))


def _atb_pieces(a, pieces, name, tm, tk=512):
    m = a.shape[1]
    widths = [p.shape[1] for p in pieces]
    n = sum(widths)
    nk = TOK // tk

    def body(a_ref, *refs):
        o_ref, acc_ref = refs[len(widths):]
        k = pl.program_id(1)

        @pl.when(k == 0)
        def _():
            acc_ref[...] = jnp.zeros_like(acc_ref)

        a_v = a_ref[...]
        c0 = 0
        for p_ref, w in zip(refs, widths):
            acc_ref[:, c0:c0 + w] += _dot_tn(a_v, p_ref[...].astype(BF16))
            c0 += w

        @pl.when(k == nk - 1)
        def _():
            for c0, w in _chunks(n):
                o_ref[c0:c0 + w, :] = acc_ref[:, c0:c0 + w].T.astype(BF16)

    return pl.pallas_call(
        body, name=name, grid=(m // tm, nk),
        in_specs=[pl.BlockSpec((tk, tm), lambda i, k: (k, i))]
                 + [pl.BlockSpec((tk, w), lambda i, k: (k, 0)) for w in widths],
        out_specs=pl.BlockSpec((n, tm), lambda i, k: (0, i)),
        out_shape=jax.ShapeDtypeStruct((n, m), BF16),
        scratch_shapes=[pltpu.VMEM((tm, n), F32)],
        compiler_params=_params(("parallel", "arbitrary")),
    )(a, *pieces)


def _atb_cols(a, b, name, cols, tk=512):
    m, other = a.shape[1], b.shape[1]

    def body(a_ref, b_ref, o_ref):
        acc = jnp.zeros(o_ref.shape, F32)
        for k0 in range(0, TOK, tk):
            acc = acc + _dot_tn(a_ref[k0:k0 + tk, :], b_ref[k0:k0 + tk, :])
        o_ref[...] = acc.astype(BF16)

    assert m % cols == 0
    return pl.pallas_call(
        body, name=name, grid=(m // cols,),
        in_specs=[pl.BlockSpec((TOK, cols), lambda r: (0, r)), _resident((TOK, other))],
        out_specs=pl.BlockSpec((cols, other), lambda r: (r, 0)),
        out_shape=jax.ShapeDtypeStruct((m, other), BF16),
        compiler_params=_params(("parallel",)),
    )(a, b)


def _atb(a, b, name, tm, tn, col_blocks=0, tk=512):
    m, n = a.shape[1], b.shape[1]
    nk = TOK // tk

    def body(a_ref, b_ref, o_ref, acc_ref):
        k = pl.program_id(2)

        @pl.when(k == 0)
        def _():
            acc_ref[...] = jnp.zeros_like(acc_ref)

        acc_ref[...] += _dot_tn(a_ref[...].astype(BF16), b_ref[...].astype(BF16))

        @pl.when(k == nk - 1)
        def _():
            if col_blocks:
                width = n // col_blocks
                for blk in range(col_blocks):
                    o_ref[blk] = acc_ref[:, blk * width:(blk + 1) * width].astype(BF16)
            else:
                o_ref[...] = acc_ref[...].astype(BF16)

    if col_blocks:
        out_spec = pl.BlockSpec((col_blocks, tm, n // col_blocks), lambda i, j, k: (0, i, 0))
        out_shape = jax.ShapeDtypeStruct((col_blocks, m, n // col_blocks), BF16)
    else:
        out_spec = pl.BlockSpec((tm, tn), lambda i, j, k: (i, j))
        out_shape = jax.ShapeDtypeStruct((m, n), BF16)
    return pl.pallas_call(
        body, name=name, grid=(m // tm, n // tn, nk),
        in_specs=[pl.BlockSpec((tk, tm), lambda i, j, k: (k, i)),
                  pl.BlockSpec((tk, tn), lambda i, j, k: (k, j))],
        out_specs=out_spec, out_shape=out_shape,
        scratch_shapes=[pltpu.VMEM((tm, tn), F32)],
        compiler_params=_params(("parallel", "parallel", "arbitrary")),
    )(a, b)


SB_PAIRS = SB_WIDTH // LANES


def _two_heads(v, lane0):
    zero = jnp.zeros_like(v)
    return jnp.where(lane0, v, zero), jnp.where(lane0, zero, v)


SB_QBLK = 512
N_SB_STEPS = SEQ // SB_QBLK


SB_KCHUNK = 2 * BLK
SB_ROWS = 2 * SB_QBLK
SB_DIAG = SB_QBLK // SB_KCHUNK
SB_DEAD = -104.0


def _log_keep(z):
    neg_z = -z
    return jnp.minimum(neg_z, 0.0) - jnp.log(1.0 + jnp.exp(jnp.minimum(z, neg_z)))


def _stack_heads(v, lane0):
    return jnp.concatenate(_two_heads(v, lane0), axis=0)


def _block_sums(v, tri, split=True):
    halves = (v[:, :BLK], v[:, BLK:])
    stacked = jnp.concatenate(halves, axis=0)
    if split:
        hi, lo = _split_bf16(stacked)
        prod = _dot(jnp.concatenate([hi, lo], axis=0), tri)
        tri_sum = prod[:2 * SB_ROWS] + prod[2 * SB_ROWS:]
    else:
        tri_sum = _dot(stacked.astype(BF16), tri)
    sums = tuple(jnp.sum(h, axis=1, keepdims=True) for h in halves)
    return (tri_sum[:SB_ROWS], tri_sum[SB_ROWS:]), sums


def _sb_diag_mask(d):
    row = lax.broadcasted_iota(jnp.int32, (SB_ROWS, SB_KCHUNK), 0)
    col = lax.broadcasted_iota(jnp.int32, (SB_ROWS, SB_KCHUNK), 1)
    return col + d * SB_KCHUNK < jnp.where(row >= SB_QBLK, row - SB_QBLK, row)


def _sb_fwd(qkv, send=None):
    def body(q_ref, k_ref, v_ref, o_ref):
        i = pl.program_id(2)
        krow = lax.broadcasted_iota(jnp.int32, (BLK, BLK), 0)
        kcol = lax.broadcasted_iota(jnp.int32, (BLK, BLK), 1)
        later = (krow > kcol).astype(BF16)
        lane0 = lax.broadcasted_iota(jnp.int32, (SB_QBLK, LANES), 1) < HEAD_DIM
        q2 = _stack_heads(q_ref[0] * QK_SCALE, lane0)

        def chunk(c, carry, causal):
            acc, run = carry
            off = pl.multiple_of(c * SB_KCHUNK, SB_KCHUNK)
            z = _dot_nt(q2, k_ref[0, pl.ds(off, SB_KCHUNK), :])
            log_keep = _log_keep(z)
            if causal is not None:
                log_keep = jnp.where(causal, log_keep, 0.0)
            suffix, sums = _block_sums(log_keep, later)
            log_after = jnp.concatenate([suffix[0] + (run + sums[1]), suffix[1] + run], axis=1)
            a = jnp.exp(log_keep + z + log_after)
            if causal is not None:
                a = jnp.where(causal, a, 0.0)
            acc = acc + _dot(a.astype(BF16), v_ref[0, pl.ds(off, SB_KCHUNK), :])
            return acc, run + (sums[0] + sums[1])

        below = i * SB_DIAG
        carry = (jnp.zeros((SB_ROWS, LANES), F32), jnp.zeros((SB_ROWS, 1), F32))
        for d in reversed(range(SB_DIAG)):
            carry = chunk(below + d, carry, _sb_diag_mask(d))
        acc, run = carry

        def some_alive(run):
            return (jnp.max(run) > SB_DEAD).astype(jnp.int32)

        def trip(state):
            t, _, acc, run = state
            acc, run = chunk(below - 1 - t, (acc, run), None)
            return t + 1, some_alive(run), acc, run

        _, _, acc, _ = lax.while_loop(lambda s: jnp.logical_and(s[0] < below, s[1] > 0), trip,
                                      (jnp.int32(0), some_alive(run), acc, run))
        o_ref[0] = jnp.where(lane0, acc[:SB_QBLK], acc[SB_QBLK:]).astype(BF16)

    blk = pl.BlockSpec((1, SB_QBLK, LANES), lambda b, h, i: (b, i, h))
    return _call(
        body, send, name="sb_fwd", grid=(B_LOC, SB_PAIRS, N_SB_STEPS),
        in_specs=[blk,
                  pl.BlockSpec((1, SEQ, LANES), lambda b, h, i: (b, 0, SB_PAIRS + h)),
                  pl.BlockSpec((1, SEQ, LANES), lambda b, h, i: (b, 0, 2 * SB_PAIRS + h))],
        out_specs=[blk], out_shape=[jax.ShapeDtypeStruct((B_LOC, SEQ, SB_WIDTH), BF16)],
        scratch_shapes=[], semantics=("parallel", "parallel", "arbitrary"), operands=(qkv, qkv, qkv))


def _sb_bwd(qkv, d_o, send=None):
    def body(q_ref, k_ref, v_ref, do_ref, dq_ref, dk_ref, dv_ref, dk_acc, dv_acc, z_scr, keep_scr):
        i = pl.program_id(2)
        krow = lax.broadcasted_iota(jnp.int32, (BLK, BLK), 0)
        kcol = lax.broadcasted_iota(jnp.int32, (BLK, BLK), 1)
        upto = (krow <= kcol).astype(BF16)
        earlier = (krow < kcol).astype(BF16)
        lane0 = lax.broadcasted_iota(jnp.int32, (SB_QBLK, LANES), 1) < HEAD_DIM
        q2 = _stack_heads(q_ref[0] * QK_SCALE, lane0)
        do2 = _stack_heads(do_ref[0], lane0)

        def keep_sum(c, causal):
            off = pl.multiple_of(c * SB_KCHUNK, SB_KCHUNK)
            z = _dot_nt(q2, k_ref[0, pl.ds(off, SB_KCHUNK), :])
            log_keep = _log_keep(z)
            if causal is not None:
                log_keep = jnp.where(causal, log_keep, 0.0)
            z_scr[c] = z
            keep_scr[c] = log_keep
            return jnp.sum(log_keep, axis=1, keepdims=True)

        def some_alive(run):
            return (jnp.max(run) > SB_DEAD).astype(jnp.int32)

        below = i * SB_DIAG

        def scan(state):
            t, _, run = state
            run = run + keep_sum(below - 1 - t, None)
            return t + 1, some_alive(run), run

        diag_sum = keep_sum(below + SB_DIAG - 1, _sb_diag_mask(SB_DIAG - 1))
        for d in reversed(range(SB_DIAG - 1)):
            diag_sum = diag_sum + keep_sum(below + d, _sb_diag_mask(d))
        walked, _, tot2 = lax.while_loop(lambda s: jnp.logical_and(s[0] < below, s[1] > 0), scan,
                                         (jnp.int32(0), some_alive(diag_sum), diag_sum))
        first = below - walked

        @pl.when(i == 0)
        def _():
            dk_acc[...] = jnp.zeros_like(dk_acc)
            dv_acc[...] = jnp.zeros_like(dv_acc)

        def chunk(c, carry, causal):
            dq, pre_keep, pre_e = carry
            off = pl.multiple_of(c * SB_KCHUNK, SB_KCHUNK)
            k_c = k_ref[0, pl.ds(off, SB_KCHUNK), :]
            v_c = v_ref[0, pl.ds(off, SB_KCHUNK), :]
            d_a = _dot_nt(do2, v_c)
            log_keep = keep_scr[c]
            log_beta = log_keep + z_scr[c]
            prefix, sums = _block_sums(log_keep, upto)
            inclusive = jnp.concatenate([prefix[0], prefix[1] + sums[0]], axis=1)
            a = jnp.exp(log_beta + ((tot2 - pre_keep) - inclusive))
            if causal is not None:
                a = jnp.where(causal, a, 0.0)
            e = d_a * a
            e_prefix, e_sums = _block_sums(e, earlier, split=False)
            before = jnp.concatenate([e_prefix[0] + pre_e, e_prefix[1] + (pre_e + e_sums[0])], axis=1)
            dz = e - (e + before) * jnp.exp(log_beta)
            if causal is not None:
                dz = jnp.where(causal, dz, 0.0)
            dz = dz.astype(BF16)
            dq = dq + _dot(dz, k_c)
            dk_acc[pl.ds(off, SB_KCHUNK), :] += _dot_tn(dz, q2)
            dv_acc[pl.ds(off, SB_KCHUNK), :] += _dot_tn(a.astype(BF16), do2)
            return dq, pre_keep + (sums[0] + sums[1]), pre_e + (e_sums[0] + e_sums[1])

        zero_col = jnp.zeros((SB_ROWS, 1), F32)
        carry = lax.fori_loop(first, below, lambda t, c: chunk(t, c, None),
                              (jnp.zeros((SB_ROWS, LANES), F32), zero_col, zero_col))
        for d in range(SB_DIAG):
            carry = chunk(below + d, carry, _sb_diag_mask(d))
        dq = carry[0]
        dq_ref[0] = (jnp.where(lane0, dq[:SB_QBLK], dq[SB_QBLK:]) * QK_SCALE).astype(BF16)

        @pl.when(i == N_SB_STEPS - 1)
        def _():
            dk_ref[0] = dk_acc[...].astype(BF16)
            dv_ref[0] = dv_acc[...].astype(BF16)

    blk = pl.BlockSpec((1, SB_QBLK, LANES), lambda b, h, i: (b, i, h))
    whole = lambda c: pl.BlockSpec((1, SEQ, LANES), lambda b, h, i: (b, 0, c * SB_PAIRS + h))
    out = jax.ShapeDtypeStruct((B_LOC, SEQ, SB_WIDTH), BF16)
    return _call(
        body, send, name="sb_bwd", grid=(B_LOC, SB_PAIRS, N_SB_STEPS),
        in_specs=[blk, whole(1), whole(2), blk],
        out_specs=[blk, whole(0), whole(0)],
        out_shape=[out, out, out],
        scratch_shapes=[pltpu.VMEM((SEQ, LANES), F32), pltpu.VMEM((SEQ, LANES), F32),
                        pltpu.VMEM((SEQ // SB_KCHUNK, SB_ROWS, SB_KCHUNK), F32),
                        pltpu.VMEM((SEQ // SB_KCHUNK, SB_ROWS, SB_KCHUNK), F32)],
        semantics=("parallel", "parallel", "arbitrary"), operands=(qkv, qkv, qkv, d_o))


DIL_GROUPS = len(DIL_PAIRS)
DIL_QBLOCKS = SEQ // BLK


def _residue_rows(j, dilation):
    length = SEQ // dilation
    return pl.ds(j, length, stride=dilation) if dilation > 1 else pl.ds(0, length)


def _gather_residues(src_ref, dst_ref, dst_off, dilation, scale=None):
    length = SEQ // dilation
    for j in range(dilation):
        v = src_ref[_residue_rows(j, dilation), :]
        if scale is not None:
            v = v * scale
        dst_ref[dst_off + j * length:dst_off + (j + 1) * length, :] = v.astype(dst_ref.dtype)


def _scatter_residues(src_ref, src_off, dst_ref, dilation):
    length = SEQ // dilation
    for j in range(dilation):
        dst_ref[_residue_rows(j, dilation), :] = (
            src_ref[src_off + j * length:src_off + (j + 1) * length, :].astype(dst_ref.dtype))


def _dil_geometry(group, pair):
    dilation = DIL_PAIRS[group][1]
    row = lax.broadcasted_iota(jnp.int32, (2 * BLK, 2 * BLK), 0)
    col = lax.broadcasted_iota(jnp.int32, (2 * BLK, 2 * BLK), 1)
    second = row >= BLK
    steps = BLK + jnp.where(second, row - BLK, row) - col
    coef = -ALIBI_MAX_BIAS / DIL_HEADS * math.log(2.0)
    first_head = float(4 * group + 1) + 2.0 * pair.astype(F32)
    slope = jnp.exp(coef * (first_head + jnp.where(second, 1.0, 0.0)))
    bias = slope * (steps * dilation).astype(F32)
    valid = jnp.logical_and(steps >= 0, steps <= BLK)
    return bias, valid, col >= BLK


def _dil_tile_scores(q2, kk, geometry, has_prev):
    bias, valid, own = geometry
    ok = jnp.logical_and(valid, jnp.logical_or(own, has_prev))
    return jnp.where(ok, _dot_nt(q2, kk) - bias, NEG_BIG)


def _head_col(v, lane_mask):
    return jnp.max(jnp.where(lane_mask, v, NEG_BIG), axis=1, keepdims=True)


def _dil_fwd(qkv, send=None):
    def body(*refs):
        ins, (o_ref, lse_ref), (qs, ks, vs, o_res, lse_res) = refs[:9], refs[9:11], refs[11:16]
        o_grp, lse_grp = refs[16:19], refs[19:22]
        pair = pl.program_id(1)
        lane0 = lax.broadcasted_iota(jnp.int32, (BLK, LANES), 1) < HEAD_DIM
        ks[0:BLK, :] = jnp.zeros((BLK, LANES), BF16)
        vs[0:BLK, :] = jnp.zeros((BLK, LANES), BF16)
        for grp, (_, dilation) in enumerate(DIL_PAIRS):
            q_ref, k_ref, v_ref = ins[3 * grp:3 * grp + 3]
            per_residue = DIL_QBLOCKS // dilation
            _gather_residues(q_ref, qs, 0, dilation, QK_SCALE)
            _gather_residues(k_ref, ks, BLK, dilation)
            _gather_residues(v_ref, vs, BLK, dilation)
            geometry = _dil_geometry(grp, pair)

            def step(blk, _):
                off = pl.multiple_of(blk * BLK, BLK)
                q2 = _stack_heads(qs[pl.ds(off, BLK), :], lane0)
                s = _dil_tile_scores(q2, ks[pl.ds(off, 2 * BLK), :], geometry, blk % per_residue != 0)
                m = jnp.max(s, axis=1, keepdims=True)
                p = jnp.exp(s - m)
                den = jnp.sum(p, axis=1, keepdims=True)
                out = _dot(p.astype(BF16), vs[pl.ds(off, 2 * BLK), :]) / den
                lse = m + jnp.log(den)
                o_res[pl.ds(off, BLK), :] = jnp.where(lane0, out[:BLK], out[BLK:])
                lse_res[pl.ds(off, BLK), :] = jnp.where(lane0, lse[:BLK], lse[BLK:])
                return 0

            lax.fori_loop(0, DIL_QBLOCKS, step, 0, unroll=8)
            _scatter_residues(o_res, 0, o_grp[grp], dilation)
            _scatter_residues(lse_res, 0, lse_grp[grp], dilation)

        for r0 in range(0, SEQ, 2 * BLK):
            rows = slice(r0, r0 + 2 * BLK)
            ls = [lse_grp[g][rows, :] for g in range(DIL_GROUPS)]
            m = jnp.maximum(jnp.maximum(ls[0], ls[1]), ls[2])
            w = [jnp.exp(l - m) for l in ls]
            den = w[0] + w[1] + w[2]
            o_ref[rows, :] = (w[0] * o_grp[0][rows, :] + w[1] * o_grp[1][rows, :] + w[2] * o_grp[2][rows, :]) / den
            lse_ref[rows, :] = m + jnp.log(den)

    def col(part, grp):
        return pl.BlockSpec((None, SEQ, LANES), lambda b, p: (b, 0, 6 * part + 2 * grp + p))

    out_spec = pl.BlockSpec((None, SEQ, LANES), lambda b, p: (b, 0, p))
    out = jax.ShapeDtypeStruct((B_LOC, SEQ, DIL_OUT), F32)
    return _call(
        body, send, name="dil_fwd", grid=(B_LOC, DIL_OUT // LANES),
        in_specs=[col(part, grp) for grp in range(DIL_GROUPS) for part in range(3)],
        out_specs=[out_spec, out_spec], out_shape=[out, out],
        scratch_shapes=[pltpu.VMEM((SEQ, LANES), BF16), pltpu.VMEM((SEQ + BLK, LANES), BF16),
                        pltpu.VMEM((SEQ + BLK, LANES), BF16), pltpu.VMEM((SEQ, LANES), F32),
                        pltpu.VMEM((SEQ, LANES), F32)] + [pltpu.VMEM((SEQ, LANES), F32)] * (2 * DIL_GROUPS),
        semantics=("parallel", "parallel"), operands=[qkv] * 9)


def _dil_bwd(qkv, d_o, lse, dsum, send=None):
    def body(*refs):
        ins, (do_ref, lse_ref, dsum_ref), outs = refs[:9], refs[9:12], refs[12:21]
        qs, ks, vs, dos, lse_res, dsum_res, dq_res, dk_acc, dv_acc = refs[21:]
        pair = pl.program_id(1)
        lane0 = lax.broadcasted_iota(jnp.int32, (BLK, LANES), 1) < HEAD_DIM
        lane1 = jnp.logical_not(lane0)
        ks[0:BLK, :] = jnp.zeros((BLK, LANES), BF16)
        vs[0:BLK, :] = jnp.zeros((BLK, LANES), BF16)
        for grp, (_, dilation) in enumerate(DIL_PAIRS):
            q_ref, k_ref, v_ref = ins[3 * grp:3 * grp + 3]
            dq_ref, dk_ref, dv_ref = outs[3 * grp:3 * grp + 3]
            per_residue = DIL_QBLOCKS // dilation
            _gather_residues(q_ref, qs, 0, dilation, QK_SCALE)
            _gather_residues(k_ref, ks, BLK, dilation)
            _gather_residues(v_ref, vs, BLK, dilation)
            _gather_residues(do_ref, dos, 0, dilation)
            _gather_residues(lse_ref, lse_res, 0, dilation)
            _gather_residues(dsum_ref, dsum_res, 0, dilation)
            dk_acc[...] = jnp.zeros_like(dk_acc)
            dv_acc[...] = jnp.zeros_like(dv_acc)
            geometry = _dil_geometry(grp, pair)

            def step(blk, _):
                off = pl.multiple_of(blk * BLK, BLK)
                q2 = _stack_heads(qs[pl.ds(off, BLK), :], lane0)
                do2 = _stack_heads(dos[pl.ds(off, BLK), :], lane0)
                kk = ks[pl.ds(off, 2 * BLK), :]
                vv = vs[pl.ds(off, 2 * BLK), :]
                lse_blk = lse_res[pl.ds(off, BLK), :]
                dsum_blk = dsum_res[pl.ds(off, BLK), :]
                lse2 = jnp.concatenate([_head_col(lse_blk, lane0), _head_col(lse_blk, lane1)], axis=0)
                dsum2 = jnp.concatenate([_head_col(dsum_blk, lane0), _head_col(dsum_blk, lane1)], axis=0)
                s = _dil_tile_scores(q2, kk, geometry, blk % per_residue != 0)
                p = jnp.exp(s - lse2)
                ds = (p * (_dot_nt(do2, vv) - dsum2)).astype(BF16)
                dq2 = _dot(ds, kk)
                dq_res[pl.ds(off, BLK), :] = jnp.where(lane0, dq2[:BLK], dq2[BLK:]) * QK_SCALE
                dk_acc[pl.ds(off, 2 * BLK), :] += _dot_tn(ds, q2)
                dv_acc[pl.ds(off, 2 * BLK), :] += _dot_tn(p.astype(BF16), do2)
                return 0

            lax.fori_loop(0, DIL_QBLOCKS, step, 0, unroll=8)
            _scatter_residues(dq_res, 0, dq_ref, dilation)
            _scatter_residues(dk_acc, BLK, dk_ref, dilation)
            _scatter_residues(dv_acc, BLK, dv_ref, dilation)

    def col(part, grp):
        return pl.BlockSpec((None, SEQ, LANES), lambda b, p: (b, 0, 6 * part + 2 * grp + p))

    slot = pl.BlockSpec((None, SEQ, LANES), lambda b, p: (b, 0, p))
    out = jax.ShapeDtypeStruct((B_LOC, SEQ, DIL_OUT), F32)
    return _call(
        body, send, name="dil_bwd", grid=(B_LOC, DIL_OUT // LANES),
        in_specs=[col(part, grp) for grp in range(DIL_GROUPS) for part in range(3)] + [slot] * 3,
        out_specs=[slot] * 9, out_shape=[out] * 9,
        scratch_shapes=[pltpu.VMEM((SEQ, LANES), BF16), pltpu.VMEM((SEQ + BLK, LANES), BF16),
                        pltpu.VMEM((SEQ + BLK, LANES), BF16), pltpu.VMEM((SEQ, LANES), BF16),
                        pltpu.VMEM((SEQ, LANES), F32), pltpu.VMEM((SEQ, LANES), F32),
                        pltpu.VMEM((SEQ, LANES), F32), pltpu.VMEM((SEQ + BLK, LANES), F32),
                        pltpu.VMEM((SEQ + BLK, LANES), F32)],
        semantics=("parallel", "parallel"), operands=[qkv] * 9 + [d_o, lse, dsum])


def _peers():
    x, y, c = lax.axis_index("x"), lax.axis_index("y"), lax.axis_index("c")
    me = 4 * x + 2 * y + c
    peers = []
    for mask in range(1, N_DEV):
        px = 1 - x if mask & 4 else x
        py = 1 - y if mask & 2 else y
        pc = 1 - c if mask & 1 else c
        peers.append(((px, py, pc), 4 * px + 2 * py + pc))
    return me, peers


def _all_gather(shard, name):
    rows, cols = shard.shape
    by_rows = rows % 32 == 0

    def body(src_ref, out_ref, send_sems, recv_sems, local_sem):
        x, y, c = lax.axis_index("x"), lax.axis_index("y"), lax.axis_index("c")
        me, sibling = (x, y, c), (x, y, 1 - c)
        x_chip, y_chip, across = (1 - x, y), (x, 1 - y), (1 - x, 1 - y)

        def slot(block, half=None):
            ref = out_ref.at[4 * block[0] + 2 * block[1] + block[2]]
            if half is None:
                return ref
            return ref.at[pl.ds(half * (rows // 2), rows // 2)] if by_rows else \
                ref.at[:, pl.ds(half * (cols // 2), cols // 2)]

        def copy(k, block, to, src=None, half=None):
            return pltpu.make_async_remote_copy(
                src_ref=slot(block, half) if src is None else src, dst_ref=slot(block, half),
                send_sem=send_sems.at[k], recv_sem=recv_sems.at[k], device_id=to,
                device_id_type=pl.DeviceIdType.MESH)

        mine = pltpu.make_async_copy(src_ref, slot(me), local_sem)
        mine.start()
        sent = [copy(0, me, sibling, src=src_ref), copy(1, me, (*x_chip, c), src=src_ref),
                copy(2, me, (*y_chip, c), src=src_ref)]
        for cp in sent:
            cp.start()

        def arrived(k, block, half=None):
            copy(k, block, me, half=half).wait_recv()
            onward = {1: [copy(3, block, (*y_chip, c), half=0), copy(5, block, sibling)],
                      2: [copy(4, block, (*x_chip, c), half=1), copy(6, block, sibling)],
                      3: [copy(7, block, sibling, half=0)],
                      4: [copy(8, block, sibling, half=1)]}.get(k, [])
            for cp in onward:
                cp.start()
            sent.extend(onward)

        arrived(1, (*x_chip, c))
        arrived(2, (*y_chip, c))
        arrived(3, (*across, c), half=0)
        arrived(4, (*across, c), half=1)
        arrived(0, sibling)
        arrived(5, (*x_chip, 1 - c))
        arrived(6, (*y_chip, 1 - c))
        arrived(7, (*across, 1 - c), half=0)
        arrived(8, (*across, 1 - c), half=1)
        for cp in sent:
            cp.wait_send()
        mine.wait()

    n_sems = 9
    return pl.pallas_call(
        body, name=name,
        in_specs=[pl.BlockSpec(memory_space=pl.ANY)],
        out_specs=pl.BlockSpec(memory_space=pl.ANY),
        out_shape=jax.ShapeDtypeStruct((N_DEV,) + shard.shape, shard.dtype),
        scratch_shapes=[pltpu.SemaphoreType.DMA((n_sems,)), pltpu.SemaphoreType.DMA((n_sems,)),
                        pltpu.SemaphoreType.DMA],
    )(shard)


def _call(body, send, *, name, grid, in_specs, out_specs, out_shape, scratch_shapes, semantics, operands):
    if send is None:
        return pl.pallas_call(
            body, name=name, grid=grid, in_specs=in_specs, out_specs=out_specs, out_shape=out_shape,
            scratch_shapes=scratch_shapes, compiler_params=_params(semantics))(*operands), []
    srcs, kinds = [s for s, _ in send], [k for _, k in send]
    n, n_in, n_out, n_scr = len(srcs), len(in_specs), len(out_specs), len(scratch_shapes)
    steps = math.prod(grid)
    relay_step = (13 * steps) // 16

    def plan(refs):
        src_refs, land_refs = refs[n_in:n_in + n], refs[n_in + n + n_out:n_in + 2 * n + n_out]
        send_sems, recv_sems, local_sems = refs[-3:]
        x, y, c = lax.axis_index("x"), lax.axis_index("y"), lax.axis_index("c")
        me, peers = _peers()
        first, relayed_in, relayed_out, arrivals, sends, own = [], [], [], [], [], []
        for a, kind in enumerate(kinds):
            def copy(k, src, dst_slot, to):
                return pltpu.make_async_remote_copy(
                    src_ref=src, dst_ref=land_refs[a].at[dst_slot], send_sem=send_sems.at[a * (N_DEV - 1) + k],
                    recv_sem=recv_sems.at[a * (N_DEV - 1) + k], device_id=to, device_id_type=pl.DeviceIdType.MESH)

            if kind == "gather_by_chip":
                idx = lambda px, py, pc: 4 * px + 2 * py + pc
                chips = [(1 - x, y), (x, 1 - y), (1 - x, 1 - y)]
                mine = [copy(0, src_refs[a], me, (x, y, 1 - c))]
                arrivals.append(copy(0, src_refs[a], idx(x, y, 1 - c), (x, y, 1 - c)))
                for j, (px, py) in enumerate(chips):
                    mine.append(copy(1 + j, src_refs[a], me, (px, py, c)))
                    relayed_in.append(copy(1 + j, src_refs[a], idx(px, py, c), (px, py, c)))
                    relayed_out.append(copy(4 + j, land_refs[a].at[idx(px, py, c)], idx(px, py, c), (x, y, 1 - c)))
                    arrivals.append(copy(4 + j, src_refs[a], idx(px, py, 1 - c), (x, y, 1 - c)))
                first += mine
                sends += mine + relayed_out[-3:]
                own.append(pltpu.make_async_copy(src_refs[a], land_refs[a].at[me], local_sems.at[a]))
            elif kind == "scatter_by_chip":
                for k, (px, py) in enumerate([(1 - x, y), (x, 1 - y), (1 - x, 1 - y)]):
                    first.append(copy(k, src_refs[a].at[2 * px + py], 2 * x + y, (px, py, c)))
                    arrivals.append(copy(k, src_refs[a].at[2 * px + py], 2 * px + py, (px, py, c)))
                sends += first[-3:]
                own.append(pltpu.make_async_copy(src_refs[a].at[2 * x + y], land_refs[a].at[2 * x + y],
                                                 local_sems.at[a]))
            else:
                part = (lambda i: src_refs[a].at[i]) if kind == "scatter" else (lambda i: src_refs[a])
                for k, (peer, peer_idx) in enumerate(peers):
                    first.append(copy(k, part(peer_idx), me, peer))
                    arrivals.append(copy(k, part(peer_idx), peer_idx, peer))
                sends += first[-(N_DEV - 1):]
                own.append(pltpu.make_async_copy(part(me), land_refs[a].at[me], local_sems.at[a]))
        return first, relayed_in, relayed_out, arrivals, sends, own

    def wrapped(*refs):
        step = 0
        for axis, size in enumerate(grid):
            step = step * size + pl.program_id(axis)

        @pl.when(step == 0)
        def _():
            first, _, _, _, _, own = plan(refs)
            for cp in first + own:
                cp.start()

        if "gather_by_chip" in kinds:
            @pl.when(step == relay_step)
            def _():
                _, relayed_in, relayed_out, _, _, _ = plan(refs)
                for cp_in, cp_out in zip(relayed_in, relayed_out):
                    cp_in.wait_recv()
                    cp_out.start()

        body(*refs[:n_in], *refs[n_in + n:n_in + n + n_out], *refs[n_in + 2 * n + n_out:n_in + 2 * n + n_out + n_scr])

        @pl.when(step == steps - 1)
        def _():
            _, _, _, arrivals, sends, own = plan(refs)
            for cp in arrivals:
                cp.wait_recv()
            for cp in sends:
                cp.wait_send()
            for cp in own:
                cp.wait()

    anywhere = pl.BlockSpec(memory_space=pl.ANY)
    lands = [jax.ShapeDtypeStruct((N_DEV // 2 if k == "scatter_by_chip" else N_DEV,) + s.shape[-2:], s.dtype)
             for s, k in send]
    out = pl.pallas_call(
        wrapped, name=name, grid=grid,
        in_specs=list(in_specs) + [anywhere] * n, out_specs=list(out_specs) + [anywhere] * n,
        out_shape=list(out_shape) + lands,
        scratch_shapes=list(scratch_shapes) + [pltpu.SemaphoreType.DMA((n * (N_DEV - 1),)),
                                               pltpu.SemaphoreType.DMA((n * (N_DEV - 1),)),
                                               pltpu.SemaphoreType.DMA((n,))],
        compiler_params=_params(("arbitrary",) * len(grid)),
    )(*operands, *srcs)
    return out[:n_out], list(out[n_out:])


def _pair_swap(blocks):
    def body(src_ref, out_ref, send_sems, recv_sems):
        x, y, c = lax.axis_index("x"), lax.axis_index("y"), lax.axis_index("c")
        copies = [pltpu.make_async_remote_copy(
            src_ref=src_ref.at[2 * chip + (1 - c)], dst_ref=out_ref.at[chip], send_sem=send_sems.at[chip],
            recv_sem=recv_sems.at[chip], device_id=(x, y, 1 - c), device_id_type=pl.DeviceIdType.MESH)
            for chip in range(N_DEV // 2)]
        for cp in copies:
            cp.start()
        for cp in copies:
            cp.wait()

    return pl.pallas_call(
        body, name="pair_swap_grad_w_in",
        in_specs=[pl.BlockSpec(memory_space=pl.ANY)], out_specs=pl.BlockSpec(memory_space=pl.ANY),
        out_shape=jax.ShapeDtypeStruct((N_DEV // 2,) + blocks.shape[1:], blocks.dtype),
        scratch_shapes=[pltpu.SemaphoreType.DMA((N_DEV // 2,)), pltpu.SemaphoreType.DMA((N_DEV // 2,))],
    )(blocks)


def _pair_sum(blocks, swapped, core):
    _, rows, cols = swapped.shape
    tile_rows = _row_tile(rows)

    def body(core_ref, mine_ref, theirs_ref, o_ref):
        o_ref[...] = (mine_ref[...].astype(F32) + theirs_ref[...].astype(F32)).astype(o_ref.dtype)

    return pl.pallas_call(
        body, name="pair_sum_grad_w_in",
        grid_spec=pltpu.PrefetchScalarGridSpec(
            num_scalar_prefetch=1, grid=(N_DEV // 2, rows // tile_rows),
            in_specs=[pl.BlockSpec((None, tile_rows, cols), lambda j, i, core_ref: (2 * j + core_ref[0], i, 0)),
                      pl.BlockSpec((None, tile_rows, cols), lambda j, i, core_ref: (j, i, 0))],
            out_specs=pl.BlockSpec((None, tile_rows, cols), lambda j, i, core_ref: (j, i, 0))),
        out_shape=jax.ShapeDtypeStruct(swapped.shape, swapped.dtype),
        compiler_params=_params(("parallel", "parallel")),
    )(core, blocks, swapped)


def _sum_in_device_order(land_ref):
    acc = land_ref[0].astype(F32)
    for j in range(1, land_ref.shape[0]):
        acc = acc + land_ref[j].astype(F32)
    return acc


def _adam_math(w, g, m, v):
    c1 = 1.0 - ADAM_B1 ** ADAM_STEP
    c2 = 1.0 - ADAM_B2 ** ADAM_STEP
    m_new = ADAM_B1 * m + (1.0 - ADAM_B1) * g
    v_new = ADAM_B2 * v + (1.0 - ADAM_B2) * (g * g)
    delta = -ADAM_LR * ((m_new / c1) / (jnp.sqrt(v_new / c2) + ADAM_EPS) + ADAM_WD * w)
    return delta, m_new, v_new


def _row_tile(rows):
    return max(t for t in range(16, 385, 16) if rows % t == 0) if rows % 16 == 0 else rows


def _sum_update(land, w, m, v, name):
    slots, rows, cols = land.shape
    tile_rows = _row_tile(rows)

    def body(land_ref, w_ref, m_ref, v_ref, g_ref, d_ref, nm_ref, nv_ref):
        g = _sum_in_device_order(land_ref)
        g_ref[...] = g
        d_ref[...], nm_ref[...], nv_ref[...] = _adam_math(w_ref[...], g, m_ref[...], v_ref[...])

    tile = pl.BlockSpec((None, tile_rows, cols), lambda i: (0, i, 0))
    out = jax.ShapeDtypeStruct((1, rows, cols), F32)
    return pl.pallas_call(
        body, name=name, grid=(rows // tile_rows,),
        in_specs=[pl.BlockSpec((slots, tile_rows, cols), lambda i: (0, i, 0)), tile, tile, tile],
        out_specs=[tile] * 4, out_shape=[out] * 4,
        compiler_params=_params(("parallel",)),
    )(land, w, m, v)


def _to_bf16(arrays):
    def body(*refs):
        for src_ref, dst_ref in zip(refs[:len(arrays)], refs[len(arrays):]):
            dst_ref[...] = src_ref[...].astype(BF16)

    whole = [pl.BlockSpec(a.shape, lambda i: (0, 0)) for a in arrays]
    return pl.pallas_call(
        body, name="shards_to_bf16", grid=(1,), in_specs=whole, out_specs=whole,
        out_shape=[jax.ShapeDtypeStruct(a.shape, BF16) for a in arrays],
        compiler_params=_params(("arbitrary",)),
    )(*arrays)


N_GAINS = 3


def _sum_update_gains(land, ws, ms, vs):
    def body(land_ref, *refs):
        w_refs, m_refs, v_refs = (refs[k * N_GAINS:(k + 1) * N_GAINS] for k in range(3))
        loss_ref, out_refs = refs[3 * N_GAINS], refs[3 * N_GAINS + 1:]
        rows = _sum_in_device_order(land_ref)
        loss_ref[...] = rows[N_GAINS:N_GAINS + 1]
        for k in range(N_GAINS):
            g = rows[k:k + 1]
            g_ref, d_ref, nm_ref, nv_ref = out_refs[4 * k:4 * k + 4]
            g_ref[...] = g
            d_ref[...], nm_ref[...], nv_ref[...] = _adam_math(w_refs[k][...], g, m_refs[k][...], v_refs[k][...])

    row = pl.BlockSpec((1, D_MODEL), lambda i: (0, 0))
    out = jax.ShapeDtypeStruct((1, D_MODEL), F32)
    n_out = 1 + 4 * N_GAINS
    return pl.pallas_call(
        body, name="update_gains", grid=(1,),
        in_specs=[pl.BlockSpec(land.shape, lambda i: (0, 0, 0))] + [row] * (3 * N_GAINS),
        out_specs=[row] * n_out, out_shape=[out] * n_out,
    )(land, *ws, *ms, *vs)


GROUP_FFN = ("w_ffn_in", "w_ffn_out")
GROUP_MIX = ("w_sb_up", "w_dil_up", "w_out")
COL_SHARDED = ("w_in", "w_sb_up", "w_dil_up", "w_ffn_in")
TRANSPOSED = ("w_in", "w_ffn_in")


def _full_from_shards(name, slots):
    _, r, c = slots.shape
    if name in TRANSPOSED:
        return slots.reshape(N_DEV * r, c).T
    if name in COL_SHARDED:
        return slots.transpose(1, 0, 2).reshape(r, N_DEV * c)
    return slots.reshape(N_DEV * r, c)


def _row_shards(full):
    rows, cols = full.shape
    return full.reshape(N_DEV, rows // N_DEV, cols)


def _local_step(x, target, g_mix, g_ffn, g_fin, w_in_t, shards=None, rest=None):
    gather = lambda names, kind: None if shards is None else [(shards[n], kind) for n in names]
    scatter = lambda blocks: None if shards is None else [(t, "scatter") for t in blocks]
    landed = lambda blocks, lands: lands if lands else blocks

    w = {"w_in": w_in_t}
    if shards is None:
        w.update(rest)
        w["w_ffn_in"] = rest["w_ffn_in"].T
    (qkv_sb, qkv_dl, gates, u), _ = _norm_proj(x, g_mix, w["w_in"])
    qkv_sb = qkv_sb.reshape(B_LOC, SEQ, 3 * SB_WIDTH)
    qkv_dl = qkv_dl.reshape(B_LOC, SEQ, 3 * DIL_WIDTH)
    (o_sb,), lands = _sb_fwd(qkv_sb, gather(GROUP_FFN, "gather_by_chip"))
    if lands:
        w["w_ffn_in"], w["w_ffn_out"] = lands[0].reshape(2 * D_FF, D_MODEL), _full_from_shards("w_ffn_out", lands[1])
    o_sb = o_sb.reshape(TOK, SB_WIDTH)
    (o_dl, lse), lands = _dil_fwd(qkv_dl, gather(GROUP_MIX, "gather"))
    w.update({n: _full_from_shards(n, t) for n, t in zip(GROUP_MIX, lands)})
    o_dl = o_dl.reshape(TOK, DIL_OUT)

    loss, dx1, merged, u2, act, dh, dx2, dg_fin, dg_ffn = _mix_ffn_fwd_bwd(
        x, o_sb, o_dl, gates, w["w_sb_up"], w["w_dil_up"], w["w_out"], target, g_ffn, g_fin,
        w["w_ffn_in"], w["w_ffn_out"])
    dgates, dy_sb, dy_dl, do_sb, do_dl, dsum = _mix_bwd(dx1, o_sb, o_dl, gates, w["w_sb_up"], w["w_dil_up"], w["w_out"])
    blocks = {
        "w_sb_up": _atb(o_sb, dy_sb, "grad_w_sb_up", SB_WIDTH, D_MODEL, col_blocks=N_DEV),
        "w_dil_up": _atb(o_dl, dy_dl, "grad_w_dil_up", DIL_OUT, D_MODEL, col_blocks=N_DEV),
        "w_out": _row_shards(_atb(merged, dx1, "grad_w_out", D_MODEL, D_MODEL)),
        "w_ffn_in": _row_shards(_atb_cols(dh, u2, "grad_w_ffn_in", 512)),
        "w_ffn_out": _row_shards(_atb_cols(act, dx2, "grad_w_ffn_out", 256)),
    }
    grads = {}

    early, late = GROUP_FFN, GROUP_MIX
    early_blocks = [blocks[n] for n in early]
    (dq_sb, dk_sb, dv_sb), lands = _sb_bwd(qkv_sb, do_sb.reshape(B_LOC, SEQ, SB_WIDTH), scatter(early_blocks))
    grads.update(zip(early, landed(early_blocks, lands)))
    as_batch = lambda t: t.reshape(B_LOC, SEQ, DIL_OUT)
    late_blocks = [blocks[n] for n in late]
    d_dl, lands = _dil_bwd(qkv_dl, as_batch(do_dl), lse, as_batch(dsum), scatter(late_blocks))
    grads.update(zip(late, landed(late_blocks, lands)))
    flat = lambda t: t.reshape(TOK, -1)
    dproj = ([flat(dq_sb), flat(dk_sb), flat(dv_sb)]
             + [flat(d_dl[3 * grp + part]) for part in range(3) for grp in range(DIL_GROUPS)] + [dgates])

    w_in_blocks = _row_shards(_atb_pieces(u, dproj, "grad_w_in", D_MODEL // 2))
    if shards is None:
        grads["w_in"] = w_in_blocks
        send = None
    else:
        core = lax.axis_index("c").astype(jnp.int32).reshape(1)
        send = [(_pair_sum(w_in_blocks, _pair_swap(w_in_blocks), core), "scatter_by_chip")]
    (grad_x, dg_mix), lands = _proj_bwd(dproj, dx1, x, g_mix, w["w_in"], send)
    if lands:
        grads["w_in"] = lands[0]
    gain_grads = jnp.concatenate([dg_mix, dg_ffn, dg_fin], axis=0)
    return loss, grad_x, gain_grads, grads


def kernel(x, norm_mix_g, w_in, w_sb_up, w_dil_up, w_out, norm_ffn_g, w_ffn_in, w_ffn_out, norm_final_g, loss_target, m_norm_mix_g, m_w_in, m_w_sb_up, m_w_dil_up, m_w_out, m_norm_ffn_g, m_w_ffn_in, m_w_ffn_out, m_norm_final_g, v_norm_mix_g, v_w_in, v_w_sb_up, v_w_dil_up, v_w_out, v_norm_ffn_g, v_w_ffn_in, v_w_ffn_out, v_norm_final_g):
    mats = {"w_in": w_in, "w_sb_up": w_sb_up, "w_dil_up": w_dil_up, "w_out": w_out,
            "w_ffn_in": w_ffn_in, "w_ffn_out": w_ffn_out}
    moments_m = {"w_in": m_w_in, "w_sb_up": m_w_sb_up, "w_dil_up": m_w_dil_up, "w_out": m_w_out,
                 "w_ffn_in": m_w_ffn_in, "w_ffn_out": m_w_ffn_out}
    moments_v = {"w_in": v_w_in, "w_sb_up": v_w_sb_up, "w_dil_up": v_w_dil_up, "w_out": v_w_out,
                 "w_ffn_in": v_w_ffn_in, "w_ffn_out": v_w_ffn_out}
    stored = lambda t, name: t.transpose(0, 2, 1) if name in TRANSPOSED else t
    mats = {name: stored(t, name) for name, t in mats.items()}
    shards = dict(zip(mats, _to_bf16([t[0] for t in mats.values()])))
    gathered_w_in = _all_gather(shards.pop("w_in"), "all_gather_w_in")
    g_fin = norm_final_g.reshape(1, D_MODEL)
    loss, grad_x, gain_grads, grad_slots = _local_step(
        x.reshape(TOK, D_MODEL), loss_target.reshape(TOK, D_MODEL), norm_mix_g, norm_ffn_g, g_fin,
        gathered_w_in.reshape(IN_WIDTH, D_MODEL), shards=shards)

    gain_rows = jnp.concatenate([gain_grads, jnp.tile(loss, (1, D_MODEL // LANES)),
                                 jnp.zeros((8 - 4, D_MODEL), F32)], axis=0)
    row = lambda t: t.reshape(1, D_MODEL)
    loss_row, *gain_outs = _sum_update_gains(
        _all_gather(gain_rows, "all_gather_gains"),
        [norm_mix_g, norm_ffn_g, g_fin], [m_norm_mix_g, m_norm_ffn_g, row(m_norm_final_g)],
        [v_norm_mix_g, v_norm_ffn_g, row(v_norm_final_g)])

    out_g, out_d, out_m, out_v = {}, {}, {}, {}
    for name, slots in grad_slots.items():
        out_g[name], out_d[name], out_m[name], out_v[name] = [stored(t, name) for t in _sum_update(
            slots, mats[name], stored(moments_m[name], name), stored(moments_v[name], name), "update_" + name)]
    for idx, name in enumerate(("norm_mix_g", "norm_ffn_g", "norm_final_g")):
        shape = (D_MODEL,) if name == "norm_final_g" else (1, D_MODEL)
        out_g[name], out_d[name], out_m[name], out_v[name] = [t.reshape(shape) for t in gain_outs[4 * idx:4 * idx + 4]]

    order = ("norm_mix_g", "w_in", "w_sb_up", "w_dil_up", "w_out", "norm_ffn_g", "w_ffn_in", "w_ffn_out",
             "norm_final_g")
    return (loss_row[0, 0], grad_x.reshape(B_LOC, SEQ, D_MODEL),
            *[out_g[n] for n in order], *[out_d[n] for n in order],
            *[out_m[n] for n in order], *[out_v[n] for n in order])
```

```python
import math

import jax
import jax.numpy as jnp
from jax import lax
from jax.experimental import pallas as pl
from jax.experimental.pallas import tpu as pltpu

F32 = jnp.float32
BF16 = jnp.bfloat16

N_DEV = 8
D_MODEL = 1024
SEQ = 2048
B_LOC = 2
TOK = B_LOC * SEQ
HEAD_DIM = 64
SB_WIDTH = 512
DIL_WIDTH = 768
DIL_OUT = 256
QKV_WIDTH = 3 * SB_WIDTH + 3 * DIL_WIDTH
IN_WIDTH = QKV_WIDTH + 2 * D_MODEL
D_FF = 2816
DIL_PAIRS = ((128, 1), (512, 4), (2048, 16))
DIL_HEADS = 12
RMS_EPS = 1e-6
ALIBI_MAX_BIAS = 8.0
QK_SCALE = 1.0 / math.sqrt(HEAD_DIM)
BLK = 128
LANES = 128
NEG_BIG = -1e30

ADAM_LR = 0.001
ADAM_B1 = 0.9
ADAM_B2 = 0.999
ADAM_EPS = 1e-08
ADAM_WD = 0.01
ADAM_STEP = 10

VMEM_LIMIT = 58 * 1024 * 1024


def _dot(a, b):
    return jnp.dot(a, b, preferred_element_type=F32)


def _dot_nt(a, b):
    return lax.dot_general(a, b, (((1,), (1,)), ((), ())), preferred_element_type=F32)


def _dot_tn(a, b):
    return lax.dot_general(a, b, (((0,), (0,)), ((), ())), preferred_element_type=F32)


def _sigmoid(z):
    return 1.0 / (1.0 + jnp.exp(-z))


def _split_bf16(v):
    hi = v.astype(BF16)
    lo = (v - hi.astype(F32)).astype(BF16)
    return hi, lo


def _chunks(width, step=512):
    out, c = [], 0
    while c < width:
        w = min(step, width - c)
        out.append((c, w))
        c += w
    return out


def _resident(shape):
    nd = len(shape)
    return pl.BlockSpec(shape, lambda *_: (0,) * nd, pipeline_mode=pl.Buffered(1))


def _params(sem):
    return pltpu.CompilerParams(dimension_semantics=sem, vmem_limit_bytes=VMEM_LIMIT)


def _rms_fwd(x, g):
    r = lax.rsqrt(jnp.mean(x * x, axis=-1, keepdims=True) + RMS_EPS)
    n = x * r
    return n, r, n * g


def _rms_bwd(dy, n, r, g):
    dg = jnp.sum(dy * n, axis=0, keepdims=True)
    dn = dy * g
    dx = r * (dn - n * jnp.mean(dn * n, axis=-1, keepdims=True))
    return dx, dg


TM = 256


def _norm_proj(x, g, w_in_t, send=None):
    def body(x_ref, g_ref, w_ref, sb_ref, dl_ref, gate_ref, u_ref):
        _, _, u = _rms_fwd(x_ref[...], g_ref[...])
        u = u.astype(BF16)
        u_ref[...] = u
        for c0, w in _chunks(3 * SB_WIDTH):
            sb_ref[:, c0:c0 + w] = _dot_nt(u, w_ref[c0:c0 + w, :]).astype(BF16)
        for c0, w in _chunks(3 * DIL_WIDTH):
            dl_ref[:, c0:c0 + w] = _dot_nt(u, w_ref[3 * SB_WIDTH + c0:3 * SB_WIDTH + c0 + w, :])
        for c0, w in _chunks(2 * D_MODEL):
            gate_ref[:, c0:c0 + w] = _dot_nt(u, w_ref[QKV_WIDTH + c0:QKV_WIDTH + c0 + w, :])

    return _call(
        body, send, name="norm_proj", grid=(TOK // TM,),
        in_specs=[pl.BlockSpec((TM, D_MODEL), lambda i: (i, 0)), _resident((1, D_MODEL)),
                  _resident((IN_WIDTH, D_MODEL))],
        out_specs=[pl.BlockSpec((TM, 3 * SB_WIDTH), lambda i: (i, 0)),
                   pl.BlockSpec((TM, 3 * DIL_WIDTH), lambda i: (i, 0)),
                   pl.BlockSpec((TM, 2 * D_MODEL), lambda i: (i, 0)),
                   pl.BlockSpec((TM, D_MODEL), lambda i: (i, 0))],
        out_shape=[jax.ShapeDtypeStruct((TOK, 3 * SB_WIDTH), BF16),
                   jax.ShapeDtypeStruct((TOK, 3 * DIL_WIDTH), F32),
                   jax.ShapeDtypeStruct((TOK, 2 * D_MODEL), F32),
                   jax.ShapeDtypeStruct((TOK, D_MODEL), BF16)],
        scratch_shapes=[], semantics=("parallel",), operands=(x, g, w_in_t))


FF_CHUNK = 1024


def _mix_ffn_fwd_bwd(x, o_sb, o_dl, gates, w_sb_up, w_dil_up, w_out, target, g_ffn, g_fin, w_ffn_in_t, w_ffn_out):
    def body(x_ref, osb_ref, odl_ref, gate_ref, wsb_ref, wdl_ref, wo_ref, t_ref, gffn_ref, gfin_ref, win_ref, wout_ref,
             loss_ref, dx1_ref, mg_ref, u2_ref, act_ref, dh_ref, dx2_ref, dgfin_ref, dgffn_ref, h_scr):
        i = pl.program_id(0)

        @pl.when(i == 0)
        def _():
            loss_ref[...] = jnp.zeros_like(loss_ref)
            dgfin_ref[...] = jnp.zeros_like(dgfin_ref)
            dgffn_ref[...] = jnp.zeros_like(dgffn_ref)

        y_sb = _dot(osb_ref[...], wsb_ref[...])
        y_dl = _dot(odl_ref[...].astype(BF16), wdl_ref[...])
        merged = (_sigmoid(gate_ref[:, :D_MODEL]) * y_sb
                  + _sigmoid(gate_ref[:, D_MODEL:]) * y_dl).astype(BF16)
        mg_ref[...] = merged
        x1 = x_ref[...] + _dot(merged, wo_ref[...])
        g_ffn_v = gffn_ref[...]
        g_fin_v = gfin_ref[...]
        n2, r2, u2 = _rms_fwd(x1, g_ffn_v)
        u2 = u2.astype(BF16)
        u2_ref[...] = u2
        x2 = x1
        for c0, w in _chunks(D_FF, FF_CHUNK):
            gate = _dot_nt(u2, win_ref[c0:c0 + w, :])
            up = _dot_nt(u2, win_ref[D_FF + c0:D_FF + c0 + w, :])
            h_scr[:, c0:c0 + w] = gate
            h_scr[:, D_FF + c0:D_FF + c0 + w] = up
            act = (gate * _sigmoid(gate) * up).astype(BF16)
            act_ref[:, c0:c0 + w] = act
            x2 = x2 + _dot(act, wout_ref[c0:c0 + w, :])
        n3, r3, y = _rms_fwd(x2, g_fin_v)
        err = y - t_ref[...]
        sq = jnp.sum(jnp.sum(err * err, axis=1, keepdims=True), axis=0, keepdims=True)
        loss_ref[...] += sq * (0.5 / D_MODEL)
        dx2, dgfin = _rms_bwd(err * (1.0 / D_MODEL), n3, r3, g_fin_v)
        dgfin_ref[...] += dgfin
        dx2_b = dx2.astype(BF16)
        dx2_ref[...] = dx2_b
        du2 = jnp.zeros((TM, D_MODEL), F32)
        for c0, w in _chunks(D_FF, FF_CHUNK):
            gate = h_scr[:, c0:c0 + w]
            up = h_scr[:, D_FF + c0:D_FF + c0 + w]
            dact = _dot_nt(dx2_b, wout_ref[c0:c0 + w, :])
            sg = _sigmoid(gate)
            dgate = (dact * up * (sg * (1.0 + gate * (1.0 - sg)))).astype(BF16)
            dup = (dact * (gate * sg)).astype(BF16)
            dh_ref[:, c0:c0 + w] = dgate
            dh_ref[:, D_FF + c0:D_FF + c0 + w] = dup
            du2 = du2 + _dot(dgate, win_ref[c0:c0 + w, :])
            du2 = du2 + _dot(dup, win_ref[D_FF + c0:D_FF + c0 + w, :])
        dx1_n, dgffn = _rms_bwd(du2, n2, r2, g_ffn_v)
        dgffn_ref[...] += dgffn
        dx1_ref[...] = dx2 + dx1_n

    tile = lambda w: pl.BlockSpec((TM, w), lambda i: (i, 0))
    acc = lambda w: pl.BlockSpec((1, w), lambda i: (0, 0))
    return pl.pallas_call(
        body, name="mix_ffn_fwd_bwd", grid=(TOK // TM,),
        in_specs=[tile(D_MODEL), tile(SB_WIDTH), tile(DIL_OUT), tile(2 * D_MODEL),
                  _resident((SB_WIDTH, D_MODEL)), _resident((DIL_OUT, D_MODEL)), _resident((D_MODEL, D_MODEL)),
                  tile(D_MODEL), _resident((1, D_MODEL)), _resident((1, D_MODEL)),
                  _resident((2 * D_FF, D_MODEL)), _resident((D_FF, D_MODEL))],
        out_specs=[acc(LANES), tile(D_MODEL), tile(D_MODEL), tile(D_MODEL), tile(D_FF), tile(2 * D_FF),
                   tile(D_MODEL), acc(D_MODEL), acc(D_MODEL)],
        out_shape=[jax.ShapeDtypeStruct((1, LANES), F32),
                   jax.ShapeDtypeStruct((TOK, D_MODEL), F32),
                   jax.ShapeDtypeStruct((TOK, D_MODEL), BF16),
                   jax.ShapeDtypeStruct((TOK, D_MODEL), BF16),
                   jax.ShapeDtypeStruct((TOK, D_FF), BF16),
                   jax.ShapeDtypeStruct((TOK, 2 * D_FF), BF16),
                   jax.ShapeDtypeStruct((TOK, D_MODEL), BF16),
                   jax.ShapeDtypeStruct((1, D_MODEL), F32),
                   jax.ShapeDtypeStruct((1, D_MODEL), F32)],
        scratch_shapes=[pltpu.VMEM((TM, 2 * D_FF), F32)],
        compiler_params=_params(("arbitrary",)),
    )(x, o_sb, o_dl, gates, w_sb_up, w_dil_up, w_out, target, g_ffn, g_fin, w_ffn_in_t, w_ffn_out)


def _mix_bwd(dx1, o_sb, o_dl, gates, w_sb_up, w_dil_up, w_out):
    def body(dx1_ref, osb_ref, odl_ref, gate_ref, wsb_ref, wdl_ref, wout_ref,
             dgate_ref, dysb_ref, dydl_ref, dosb_ref, dodl_ref, dsum_ref):
        dmerged = _dot_nt(dx1_ref[...].astype(BF16), wout_ref[...])
        o_dl = odl_ref[...]
        y_sb = _dot(osb_ref[...], wsb_ref[...])
        y_dl = _dot(o_dl.astype(BF16), wdl_ref[...])
        s_sb = _sigmoid(gate_ref[:, :D_MODEL])
        s_dl = _sigmoid(gate_ref[:, D_MODEL:])
        dgate_ref[:, :D_MODEL] = (dmerged * y_sb * (s_sb * (1.0 - s_sb))).astype(BF16)
        dgate_ref[:, D_MODEL:] = (dmerged * y_dl * (s_dl * (1.0 - s_dl))).astype(BF16)
        dy_sb = (dmerged * s_sb).astype(BF16)
        dy_dl = (dmerged * s_dl).astype(BF16)
        dysb_ref[...] = dy_sb
        dydl_ref[...] = dy_dl
        dosb_ref[...] = _dot_nt(dy_sb, wsb_ref[...]).astype(BF16)
        do_dl = _dot_nt(dy_dl, wdl_ref[...])
        dodl_ref[...] = do_dl
        row = lax.broadcasted_iota(jnp.int32, (DIL_OUT, DIL_OUT), 0) // HEAD_DIM
        col = lax.broadcasted_iota(jnp.int32, (DIL_OUT, DIL_OUT), 1) // HEAD_DIM
        same_head = (row == col).astype(BF16)
        hi, lo = _split_bf16(do_dl * o_dl)
        dsum_ref[...] = _dot(hi, same_head) + _dot(lo, same_head)

    tile = lambda w: pl.BlockSpec((TM, w), lambda i: (i, 0))
    return pl.pallas_call(
        body, name="mix_bwd", grid=(TOK // TM,),
        in_specs=[tile(D_MODEL), tile(SB_WIDTH), tile(DIL_OUT), tile(2 * D_MODEL),
                  _resident((SB_WIDTH, D_MODEL)), _resident((DIL_OUT, D_MODEL)),
                  _resident((D_MODEL, D_MODEL))],
        out_specs=[tile(2 * D_MODEL), tile(D_MODEL), tile(D_MODEL), tile(SB_WIDTH), tile(DIL_OUT),
                   tile(DIL_OUT)],
        out_shape=[jax.ShapeDtypeStruct((TOK, 2 * D_MODEL), BF16),
                   jax.ShapeDtypeStruct((TOK, D_MODEL), BF16),
                   jax.ShapeDtypeStruct((TOK, D_MODEL), BF16),
                   jax.ShapeDtypeStruct((TOK, SB_WIDTH), BF16),
                   jax.ShapeDtypeStruct((TOK, DIL_OUT), F32),
                   jax.ShapeDtypeStruct((TOK, DIL_OUT), F32)],
        compiler_params=_params(("parallel",)),
    )(dx1, o_sb, o_dl, gates, w_sb_up, w_dil_up, w_out)


def _proj_bwd(dproj, dx1, x, g, w_in_t, send=None):
    widths = [p.shape[1] for p in dproj]

    def body(*refs):
        dx1_ref, x_ref, g_ref, w_ref, dx_ref, dg_ref = refs[len(widths):]

        @pl.when(pl.program_id(0) == 0)
        def _():
            dg_ref[...] = jnp.zeros_like(dg_ref)

        du = jnp.zeros((TM, D_MODEL), F32)
        c0 = 0
        for dp_ref, w in zip(refs, widths):
            du = du + _dot(dp_ref[...], w_ref[c0:c0 + w, :])
            c0 += w
        g_v = g_ref[...]
        n, r, _ = _rms_fwd(x_ref[...], g_v)
        dx, dg = _rms_bwd(du, n, r, g_v)
        dg_ref[...] += dg
        dx_ref[...] = dx1_ref[...] + dx

    tile = lambda w: pl.BlockSpec((TM, w), lambda i: (i, 0))
    return _call(
        body, send, name="proj_bwd", grid=(TOK // TM,),
        in_specs=[tile(w) for w in widths] + [tile(D_MODEL), tile(D_MODEL), _resident((1, D_MODEL)),
                                              _resident((IN_WIDTH, D_MODEL))],
        out_specs=[tile(D_MODEL), pl.BlockSpec((1, D_MODEL), lambda i: (0, 0))],
        out_shape=[jax.ShapeDtypeStruct((TOK, D_MODEL), F32),
                   jax.ShapeDtypeStruct((1, D_MODEL), F32)],
        scratch_shapes=[], semantics=("arbitrary",), operands=(*dproj, dx1, x, g, w_in_t))


def _atb_pieces(a, pieces, name, tm, tk=512):
    m = a.shape[1]
    widths = [p.shape[1] for p in pieces]
    n = sum(widths)
    nk = TOK // tk

    def body(a_ref, *refs):
        o_ref, acc_ref = refs[len(widths):]
        k = pl.program_id(1)

        @pl.when(k == 0)
        def _():
            acc_ref[...] = jnp.zeros_like(acc_ref)

        a_v = a_ref[...]
        c0 = 0
        for p_ref, w in zip(refs, widths):
            acc_ref[:, c0:c0 + w] += _dot_tn(a_v, p_ref[...])
            c0 += w

        @pl.when(k == nk - 1)
        def _():
            for c0, w in _chunks(n):
                o_ref[c0:c0 + w, :] = acc_ref[:, c0:c0 + w].T.astype(BF16)

    return pl.pallas_call(
        body, name=name, grid=(m // tm, nk),
        in_specs=[pl.BlockSpec((tk, tm), lambda i, k: (k, i))]
                 + [pl.BlockSpec((tk, w), lambda i, k: (k, 0)) for w in widths],
        out_specs=pl.BlockSpec((n, tm), lambda i, k: (0, i)),
        out_shape=jax.ShapeDtypeStruct((n, m), BF16),
        scratch_shapes=[pltpu.VMEM((tm, n), F32)],
        compiler_params=_params(("parallel", "arbitrary")),
    )(a, *pieces)


def _atb_cols(a, b, name, cols, tk=512):
    m, other = a.shape[1], b.shape[1]

    def body(a_ref, b_ref, o_ref):
        acc = jnp.zeros(o_ref.shape, F32)
        for k0 in range(0, TOK, tk):
            acc = acc + _dot_tn(a_ref[k0:k0 + tk, :], b_ref[k0:k0 + tk, :])
        o_ref[...] = acc.astype(BF16)

    assert m % cols == 0
    return pl.pallas_call(
        body, name=name, grid=(m // cols,),
        in_specs=[pl.BlockSpec((TOK, cols), lambda r: (0, r)), _resident((TOK, other))],
        out_specs=pl.BlockSpec((cols, other), lambda r: (r, 0)),
        out_shape=jax.ShapeDtypeStruct((m, other), BF16),
        compiler_params=_params(("parallel",)),
    )(a, b)


def _atb(a, b, name, tm, tn, col_blocks=0, tk=512):
    m, n = a.shape[1], b.shape[1]
    nk = TOK // tk

    def body(a_ref, b_ref, o_ref, acc_ref):
        k = pl.program_id(2)

        @pl.when(k == 0)
        def _():
            acc_ref[...] = jnp.zeros_like(acc_ref)

        acc_ref[...] += _dot_tn(a_ref[...].astype(BF16), b_ref[...].astype(BF16))

        @pl.when(k == nk - 1)
        def _():
            if col_blocks:
                width = n // col_blocks
                for blk in range(col_blocks):
                    o_ref[blk] = acc_ref[:, blk * width:(blk + 1) * width].astype(BF16)
            else:
                o_ref[...] = acc_ref[...].astype(BF16)

    if col_blocks:
        out_spec = pl.BlockSpec((col_blocks, tm, n // col_blocks), lambda i, j, k: (0, i, 0))
        out_shape = jax.ShapeDtypeStruct((col_blocks, m, n // col_blocks), BF16)
    else:
        out_spec = pl.BlockSpec((tm, tn), lambda i, j, k: (i, j))
        out_shape = jax.ShapeDtypeStruct((m, n), BF16)
    return pl.pallas_call(
        body, name=name, grid=(m // tm, n // tn, nk),
        in_specs=[pl.BlockSpec((tk, tm), lambda i, j, k: (k, i)),
                  pl.BlockSpec((tk, tn), lambda i, j, k: (k, j))],
        out_specs=out_spec, out_shape=out_shape,
        scratch_shapes=[pltpu.VMEM((tm, tn), F32)],
        compiler_params=_params(("parallel", "parallel", "arbitrary")),
    )(a, b)


SB_PAIRS = SB_WIDTH // LANES


def _two_heads(v, lane0):
    zero = jnp.zeros_like(v)
    return jnp.where(lane0, v, zero), jnp.where(lane0, zero, v)


SB_QBLK = 256
N_SB_STEPS = SEQ // SB_QBLK


SB_KCHUNK = 2 * BLK
SB_ROWS = 2 * SB_QBLK
SB_DEAD = -104.0


def _log_keep(z):
    neg_z = -z
    return jnp.minimum(neg_z, 0.0) - jnp.log(1.0 + jnp.exp(jnp.minimum(z, neg_z)))


def _stack_heads(v, lane0):
    return jnp.concatenate(_two_heads(v, lane0), axis=0)


def _block_sums(v, tri, split=True):
    halves = (v[:, :BLK], v[:, BLK:])
    stacked = jnp.concatenate(halves, axis=0)
    if split:
        hi, lo = _split_bf16(stacked)
        prod = _dot(jnp.concatenate([hi, lo], axis=0), tri)
        tri_sum = prod[:2 * SB_ROWS] + prod[2 * SB_ROWS:]
    else:
        tri_sum = _dot(stacked.astype(BF16), tri)
    sums = tuple(jnp.sum(h, axis=1, keepdims=True) for h in halves)
    return (tri_sum[:SB_ROWS], tri_sum[SB_ROWS:]), sums


def _sb_diag_mask():
    row = lax.broadcasted_iota(jnp.int32, (SB_ROWS, SB_KCHUNK), 0)
    col = lax.broadcasted_iota(jnp.int32, (SB_ROWS, SB_KCHUNK), 1)
    return col < jnp.where(row >= SB_QBLK, row - SB_QBLK, row)


def _sb_fwd(qkv, send=None):
    def body(q_ref, k_ref, v_ref, o_ref):
        i = pl.program_id(2)
        krow = lax.broadcasted_iota(jnp.int32, (BLK, BLK), 0)
        kcol = lax.broadcasted_iota(jnp.int32, (BLK, BLK), 1)
        later = (krow > kcol).astype(BF16)
        lane0 = lax.broadcasted_iota(jnp.int32, (SB_QBLK, LANES), 1) < HEAD_DIM
        q2 = _stack_heads(q_ref[0] * QK_SCALE, lane0)

        def chunk(c, carry, causal):
            acc, run = carry
            off = pl.multiple_of(c * SB_KCHUNK, SB_KCHUNK)
            z = _dot_nt(q2, k_ref[0, pl.ds(off, SB_KCHUNK), :])
            log_keep = _log_keep(z)
            if causal is not None:
                log_keep = jnp.where(causal, log_keep, 0.0)
            suffix, sums = _block_sums(log_keep, later)
            log_after = jnp.concatenate([suffix[0] + (run + sums[1]), suffix[1] + run], axis=1)
            a = jnp.exp(log_keep + z + log_after)
            if causal is not None:
                a = jnp.where(causal, a, 0.0)
            acc = acc + _dot(a.astype(BF16), v_ref[0, pl.ds(off, SB_KCHUNK), :])
            return acc, run + (sums[0] + sums[1])

        acc, run = chunk(i, (jnp.zeros((SB_ROWS, LANES), F32), jnp.zeros((SB_ROWS, 1), F32)), _sb_diag_mask())

        def some_alive(run):
            return (jnp.max(run) > SB_DEAD).astype(jnp.int32)

        def trip(state):
            t, _, acc, run = state
            acc, run = chunk(i - 1 - t, (acc, run), None)
            return t + 1, some_alive(run), acc, run

        _, _, acc, _ = lax.while_loop(lambda s: jnp.logical_and(s[0] < i, s[1] > 0), trip,
                                      (jnp.int32(0), some_alive(run), acc, run))
        o_ref[0] = jnp.where(lane0, acc[:SB_QBLK], acc[SB_QBLK:]).astype(BF16)

    blk = pl.BlockSpec((1, SB_QBLK, LANES), lambda b, h, i: (b, i, h))
    return _call(
        body, send, name="sb_fwd", grid=(B_LOC, SB_PAIRS, N_SB_STEPS),
        in_specs=[blk,
                  pl.BlockSpec((1, SEQ, LANES), lambda b, h, i: (b, 0, SB_PAIRS + h)),
                  pl.BlockSpec((1, SEQ, LANES), lambda b, h, i: (b, 0, 2 * SB_PAIRS + h))],
        out_specs=[blk], out_shape=[jax.ShapeDtypeStruct((B_LOC, SEQ, SB_WIDTH), BF16)],
        scratch_shapes=[], semantics=("parallel", "parallel", "arbitrary"), operands=(qkv, qkv, qkv))


def _sb_bwd(qkv, d_o, send=None):
    def body(q_ref, k_ref, v_ref, do_ref, dq_ref, dk_ref, dv_ref, dk_acc, dv_acc, z_scr, keep_scr):
        i = pl.program_id(2)
        krow = lax.broadcasted_iota(jnp.int32, (BLK, BLK), 0)
        kcol = lax.broadcasted_iota(jnp.int32, (BLK, BLK), 1)
        upto = (krow <= kcol).astype(BF16)
        earlier = (krow < kcol).astype(BF16)
        lane0 = lax.broadcasted_iota(jnp.int32, (SB_QBLK, LANES), 1) < HEAD_DIM
        q2 = _stack_heads(q_ref[0] * QK_SCALE, lane0)
        do2 = _stack_heads(do_ref[0], lane0)

        def keep_sum(c, causal):
            off = pl.multiple_of(c * SB_KCHUNK, SB_KCHUNK)
            z = _dot_nt(q2, k_ref[0, pl.ds(off, SB_KCHUNK), :])
            log_keep = _log_keep(z)
            if causal is not None:
                log_keep = jnp.where(causal, log_keep, 0.0)
            z_scr[c] = z
            keep_scr[c] = log_keep
            return jnp.sum(log_keep, axis=1, keepdims=True)

        def some_alive(run):
            return (jnp.max(run) > SB_DEAD).astype(jnp.int32)

        def scan(state):
            t, _, run = state
            run = run + keep_sum(i - 1 - t, None)
            return t + 1, some_alive(run), run

        diag_sum = keep_sum(i, _sb_diag_mask())
        walked, _, tot2 = lax.while_loop(lambda s: jnp.logical_and(s[0] < i, s[1] > 0), scan,
                                         (jnp.int32(0), some_alive(diag_sum), diag_sum))
        first = i - walked

        @pl.when(i == 0)
        def _():
            dk_acc[...] = jnp.zeros_like(dk_acc)
            dv_acc[...] = jnp.zeros_like(dv_acc)

        def chunk(c, carry, causal):
            dq, pre_keep, pre_e = carry
            off = pl.multiple_of(c * SB_KCHUNK, SB_KCHUNK)
            k_c = k_ref[0, pl.ds(off, SB_KCHUNK), :]
            v_c = v_ref[0, pl.ds(off, SB_KCHUNK), :]
            d_a = _dot_nt(do2, v_c)
            log_keep = keep_scr[c]
            log_beta = log_keep + z_scr[c]
            prefix, sums = _block_sums(log_keep, upto)
            inclusive = jnp.concatenate([prefix[0], prefix[1] + sums[0]], axis=1)
            a = jnp.exp(log_beta + ((tot2 - pre_keep) - inclusive))
            if causal is not None:
                a = jnp.where(causal, a, 0.0)
            e = d_a * a
            e_prefix, e_sums = _block_sums(e, earlier, split=False)
            before = jnp.concatenate([e_prefix[0] + pre_e, e_prefix[1] + (pre_e + e_sums[0])], axis=1)
            dz = e - (e + before) * jnp.exp(log_beta)
            if causal is not None:
                dz = jnp.where(causal, dz, 0.0)
            dz = dz.astype(BF16)
            dq = dq + _dot(dz, k_c)
            dk_acc[pl.ds(off, SB_KCHUNK), :] += _dot_tn(dz, q2)
            dv_acc[pl.ds(off, SB_KCHUNK), :] += _dot_tn(a.astype(BF16), do2)
            return dq, pre_keep + (sums[0] + sums[1]), pre_e + (e_sums[0] + e_sums[1])

        zero_col = jnp.zeros((SB_ROWS, 1), F32)
        carry = lax.fori_loop(first, i, lambda t, c: chunk(t, c, None),
                              (jnp.zeros((SB_ROWS, LANES), F32), zero_col, zero_col))
        dq, _, _ = chunk(i, carry, _sb_diag_mask())
        dq_ref[0] = (jnp.where(lane0, dq[:SB_QBLK], dq[SB_QBLK:]) * QK_SCALE).astype(BF16)

        @pl.when(i == N_SB_STEPS - 1)
        def _():
            dk_ref[0] = dk_acc[...].astype(BF16)
            dv_ref[0] = dv_acc[...].astype(BF16)

    blk = pl.BlockSpec((1, SB_QBLK, LANES), lambda b, h, i: (b, i, h))
    whole = lambda c: pl.BlockSpec((1, SEQ, LANES), lambda b, h, i: (b, 0, c * SB_PAIRS + h))
    out = jax.ShapeDtypeStruct((B_LOC, SEQ, SB_WIDTH), BF16)
    return _call(
        body, send, name="sb_bwd", grid=(B_LOC, SB_PAIRS, N_SB_STEPS),
        in_specs=[blk, whole(1), whole(2), blk],
        out_specs=[blk, whole(0), whole(0)],
        out_shape=[out, out, out],
        scratch_shapes=[pltpu.VMEM((SEQ, LANES), F32), pltpu.VMEM((SEQ, LANES), F32),
                        pltpu.VMEM((N_SB_STEPS, SB_ROWS, SB_KCHUNK), F32),
                        pltpu.VMEM((N_SB_STEPS, SB_ROWS, SB_KCHUNK), F32)],
        semantics=("parallel", "parallel", "arbitrary"), operands=(qkv, qkv, qkv, d_o))


DIL_GROUPS = len(DIL_PAIRS)
DIL_QBLOCKS = SEQ // BLK


def _residue_rows(j, dilation):
    length = SEQ // dilation
    return pl.ds(j, length, stride=dilation) if dilation > 1 else pl.ds(0, length)


def _gather_residues(src_ref, dst_ref, dst_off, dilation, scale=None):
    length = SEQ // dilation
    for j in range(dilation):
        v = src_ref[_residue_rows(j, dilation), :]
        if scale is not None:
            v = v * scale
        dst_ref[dst_off + j * length:dst_off + (j + 1) * length, :] = v.astype(dst_ref.dtype)


def _scatter_residues(src_ref, src_off, dst_ref, dilation):
    length = SEQ // dilation
    for j in range(dilation):
        dst_ref[_residue_rows(j, dilation), :] = (
            src_ref[src_off + j * length:src_off + (j + 1) * length, :].astype(dst_ref.dtype))


def _dil_geometry(group, pair):
    dilation = DIL_PAIRS[group][1]
    row = lax.broadcasted_iota(jnp.int32, (2 * BLK, 2 * BLK), 0)
    col = lax.broadcasted_iota(jnp.int32, (2 * BLK, 2 * BLK), 1)
    second = row >= BLK
    steps = BLK + jnp.where(second, row - BLK, row) - col
    coef = -ALIBI_MAX_BIAS / DIL_HEADS * math.log(2.0)
    first_head = float(4 * group + 1) + 2.0 * pair.astype(F32)
    slope = jnp.exp(coef * (first_head + jnp.where(second, 1.0, 0.0)))
    bias = slope * (steps * dilation).astype(F32)
    valid = jnp.logical_and(steps >= 0, steps <= BLK)
    return bias, valid, col >= BLK


def _dil_tile_scores(q2, kk, geometry, has_prev):
    bias, valid, own = geometry
    ok = jnp.logical_and(valid, jnp.logical_or(own, has_prev))
    return jnp.where(ok, _dot_nt(q2, kk) - bias, NEG_BIG)


def _head_col(v, lane_mask):
    return jnp.max(jnp.where(lane_mask, v, NEG_BIG), axis=1, keepdims=True)


def _dil_fwd(qkv, send=None):
    def body(*refs):
        ins, (o_ref, lse_ref), (qs, ks, vs, o_res, lse_res) = refs[:9], refs[9:11], refs[11:16]
        o_grp, lse_grp = refs[16:19], refs[19:22]
        pair = pl.program_id(1)
        lane0 = lax.broadcasted_iota(jnp.int32, (BLK, LANES), 1) < HEAD_DIM
        ks[0:BLK, :] = jnp.zeros((BLK, LANES), BF16)
        vs[0:BLK, :] = jnp.zeros((BLK, LANES), BF16)
        for grp, (_, dilation) in enumerate(DIL_PAIRS):
            q_ref, k_ref, v_ref = ins[3 * grp:3 * grp + 3]
            per_residue = DIL_QBLOCKS // dilation
            _gather_residues(q_ref, qs, 0, dilation, QK_SCALE)
            _gather_residues(k_ref, ks, BLK, dilation)
            _gather_residues(v_ref, vs, BLK, dilation)
            geometry = _dil_geometry(grp, pair)

            def step(blk, _):
                off = pl.multiple_of(blk * BLK, BLK)
                q2 = _stack_heads(qs[pl.ds(off, BLK), :], lane0)
                s = _dil_tile_scores(q2, ks[pl.ds(off, 2 * BLK), :], geometry, blk % per_residue != 0)
                m = jnp.max(s, axis=1, keepdims=True)
                p = jnp.exp(s - m)
                den = jnp.sum(p, axis=1, keepdims=True)
                out = _dot(p.astype(BF16), vs[pl.ds(off, 2 * BLK), :]) / den
                lse = m + jnp.log(den)
                o_res[pl.ds(off, BLK), :] = jnp.where(lane0, out[:BLK], out[BLK:])
                lse_res[pl.ds(off, BLK), :] = jnp.where(lane0, lse[:BLK], lse[BLK:])
                return 0

            lax.fori_loop(0, DIL_QBLOCKS, step, 0, unroll=8)
            _scatter_residues(o_res, 0, o_grp[grp], dilation)
            _scatter_residues(lse_res, 0, lse_grp[grp], dilation)

        for r0 in range(0, SEQ, 2 * BLK):
            rows = slice(r0, r0 + 2 * BLK)
            ls = [lse_grp[g][rows, :] for g in range(DIL_GROUPS)]
            m = jnp.maximum(jnp.maximum(ls[0], ls[1]), ls[2])
            w = [jnp.exp(l - m) for l in ls]
            den = w[0] + w[1] + w[2]
            o_ref[rows, :] = (w[0] * o_grp[0][rows, :] + w[1] * o_grp[1][rows, :] + w[2] * o_grp[2][rows, :]) / den
            lse_ref[rows, :] = m + jnp.log(den)

    def col(part, grp):
        return pl.BlockSpec((None, SEQ, LANES), lambda b, p: (b, 0, 6 * part + 2 * grp + p))

    out_spec = pl.BlockSpec((None, SEQ, LANES), lambda b, p: (b, 0, p))
    out = jax.ShapeDtypeStruct((B_LOC, SEQ, DIL_OUT), F32)
    return _call(
        body, send, name="dil_fwd", grid=(B_LOC, DIL_OUT // LANES),
        in_specs=[col(part, grp) for grp in range(DIL_GROUPS) for part in range(3)],
        out_specs=[out_spec, out_spec], out_shape=[out, out],
        scratch_shapes=[pltpu.VMEM((SEQ, LANES), BF16), pltpu.VMEM((SEQ + BLK, LANES), BF16),
                        pltpu.VMEM((SEQ + BLK, LANES), BF16), pltpu.VMEM((SEQ, LANES), F32),
                        pltpu.VMEM((SEQ, LANES), F32)] + [pltpu.VMEM((SEQ, LANES), F32)] * (2 * DIL_GROUPS),
        semantics=("parallel", "parallel"), operands=[qkv] * 9)


def _dil_bwd(qkv, d_o, lse, dsum, send=None):
    def body(*refs):
        ins, (do_ref, lse_ref, dsum_ref), outs = refs[:9], refs[9:12], refs[12:21]
        qs, ks, vs, dos, lse_res, dsum_res, dq_res, dk_acc, dv_acc, tok_order = refs[21:]
        pair = pl.program_id(1)
        lane0 = lax.broadcasted_iota(jnp.int32, (BLK, LANES), 1) < HEAD_DIM
        lane1 = jnp.logical_not(lane0)
        ks[0:BLK, :] = jnp.zeros((BLK, LANES), BF16)
        vs[0:BLK, :] = jnp.zeros((BLK, LANES), BF16)
        for grp, (_, dilation) in enumerate(DIL_PAIRS):
            q_ref, k_ref, v_ref = ins[3 * grp:3 * grp + 3]
            dq_ref, dk_ref, dv_ref = outs[3 * grp:3 * grp + 3]
            per_residue = DIL_QBLOCKS // dilation
            _gather_residues(q_ref, qs, 0, dilation, QK_SCALE)
            _gather_residues(k_ref, ks, BLK, dilation)
            _gather_residues(v_ref, vs, BLK, dilation)
            _gather_residues(do_ref, dos, 0, dilation)
            _gather_residues(lse_ref, lse_res, 0, dilation)
            _gather_residues(dsum_ref, dsum_res, 0, dilation)
            dk_acc[...] = jnp.zeros_like(dk_acc)
            dv_acc[...] = jnp.zeros_like(dv_acc)
            geometry = _dil_geometry(grp, pair)

            def step(blk, _):
                off = pl.multiple_of(blk * BLK, BLK)
                q2 = _stack_heads(qs[pl.ds(off, BLK), :], lane0)
                do2 = _stack_heads(dos[pl.ds(off, BLK), :], lane0)
                kk = ks[pl.ds(off, 2 * BLK), :]
                vv = vs[pl.ds(off, 2 * BLK), :]
                lse_blk = lse_res[pl.ds(off, BLK), :]
                dsum_blk = dsum_res[pl.ds(off, BLK), :]
                lse2 = jnp.concatenate([_head_col(lse_blk, lane0), _head_col(lse_blk, lane1)], axis=0)
                dsum2 = jnp.concatenate([_head_col(dsum_blk, lane0), _head_col(dsum_blk, lane1)], axis=0)
                s = _dil_tile_scores(q2, kk, geometry, blk % per_residue != 0)
                p = jnp.exp(s - lse2)
                ds = (p * (_dot_nt(do2, vv) - dsum2)).astype(BF16)
                dq2 = _dot(ds, kk)
                dq_res[pl.ds(off, BLK), :] = jnp.where(lane0, dq2[:BLK], dq2[BLK:]) * QK_SCALE
                dk_acc[pl.ds(off, 2 * BLK), :] += _dot_tn(ds, q2)
                dv_acc[pl.ds(off, 2 * BLK), :] += _dot_tn(p.astype(BF16), do2)
                return 0

            lax.fori_loop(0, DIL_QBLOCKS, step, 0, unroll=8)
            for res, off, d_ref in ((dq_res, 0, dq_ref), (dk_acc, BLK, dk_ref), (dv_acc, BLK, dv_ref)):
                _scatter_residues(res, off, tok_order, dilation)
                d_ref[...] = tok_order[...].astype(BF16)

    def col(part, grp):
        return pl.BlockSpec((None, SEQ, LANES), lambda b, p: (b, 0, 6 * part + 2 * grp + p))

    slot = pl.BlockSpec((None, SEQ, LANES), lambda b, p: (b, 0, p))
    out = jax.ShapeDtypeStruct((B_LOC, SEQ, DIL_OUT), BF16)
    return _call(
        body, send, name="dil_bwd", grid=(B_LOC, DIL_OUT // LANES),
        in_specs=[col(part, grp) for grp in range(DIL_GROUPS) for part in range(3)] + [slot] * 3,
        out_specs=[slot] * 9, out_shape=[out] * 9,
        scratch_shapes=[pltpu.VMEM((SEQ, LANES), BF16), pltpu.VMEM((SEQ + BLK, LANES), BF16),
                        pltpu.VMEM((SEQ + BLK, LANES), BF16), pltpu.VMEM((SEQ, LANES), BF16),
                        pltpu.VMEM((SEQ, LANES), F32), pltpu.VMEM((SEQ, LANES), F32),
                        pltpu.VMEM((SEQ, LANES), F32), pltpu.VMEM((SEQ + BLK, LANES), F32),
                        pltpu.VMEM((SEQ + BLK, LANES), F32), pltpu.VMEM((SEQ, LANES), F32)],
        semantics=("parallel", "parallel"), operands=[qkv] * 9 + [d_o, lse, dsum])


def _peers():
    x, y, c = lax.axis_index("x"), lax.axis_index("y"), lax.axis_index("c")
    me = 4 * x + 2 * y + c
    peers = []
    for mask in range(1, N_DEV):
        px = 1 - x if mask & 4 else x
        py = 1 - y if mask & 2 else y
        pc = 1 - c if mask & 1 else c
        peers.append(((px, py, pc), 4 * px + 2 * py + pc))
    return me, peers


def _all_gather(shard, name):
    rows, cols = shard.shape
    by_rows = rows % 32 == 0

    def body(src_ref, out_ref, send_sems, recv_sems, local_sem):
        x, y, c = lax.axis_index("x"), lax.axis_index("y"), lax.axis_index("c")
        me, sibling = (x, y, c), (x, y, 1 - c)
        x_chip, y_chip, across = (1 - x, y), (x, 1 - y), (1 - x, 1 - y)

        def slot(block, half=None):
            ref = out_ref.at[4 * block[0] + 2 * block[1] + block[2]]
            if half is None:
                return ref
            return ref.at[pl.ds(half * (rows // 2), rows // 2)] if by_rows else \
                ref.at[:, pl.ds(half * (cols // 2), cols // 2)]

        def copy(k, block, to, src=None, half=None):
            return pltpu.make_async_remote_copy(
                src_ref=slot(block, half) if src is None else src, dst_ref=slot(block, half),
                send_sem=send_sems.at[k], recv_sem=recv_sems.at[k], device_id=to,
                device_id_type=pl.DeviceIdType.MESH)

        mine = pltpu.make_async_copy(src_ref, slot(me), local_sem)
        mine.start()
        sent = [copy(0, me, sibling, src=src_ref), copy(1, me, (*x_chip, c), src=src_ref),
                copy(2, me, (*y_chip, c), src=src_ref)]
        for cp in sent:
            cp.start()

        def arrived(k, block, half=None):
            copy(k, block, me, half=half).wait_recv()
            onward = {1: [copy(3, block, (*y_chip, c), half=0), copy(5, block, sibling)],
                      2: [copy(4, block, (*x_chip, c), half=1), copy(6, block, sibling)],
                      3: [copy(7, block, sibling, half=0)],
                      4: [copy(8, block, sibling, half=1)]}.get(k, [])
            for cp in onward:
                cp.start()
            sent.extend(onward)

        arrived(1, (*x_chip, c))
        arrived(2, (*y_chip, c))
        arrived(3, (*across, c), half=0)
        arrived(4, (*across, c), half=1)
        arrived(0, sibling)
        arrived(5, (*x_chip, 1 - c))
        arrived(6, (*y_chip, 1 - c))
        arrived(7, (*across, 1 - c), half=0)
        arrived(8, (*across, 1 - c), half=1)
        for cp in sent:
            cp.wait_send()
        mine.wait()

    n_sems = 9
    return pl.pallas_call(
        body, name=name,
        in_specs=[pl.BlockSpec(memory_space=pl.ANY)],
        out_specs=pl.BlockSpec(memory_space=pl.ANY),
        out_shape=jax.ShapeDtypeStruct((N_DEV,) + shard.shape, shard.dtype),
        scratch_shapes=[pltpu.SemaphoreType.DMA((n_sems,)), pltpu.SemaphoreType.DMA((n_sems,)),
                        pltpu.SemaphoreType.DMA],
    )(shard)


def _call(body, send, *, name, grid, in_specs, out_specs, out_shape, scratch_shapes, semantics, operands):
    if send is None:
        return pl.pallas_call(
            body, name=name, grid=grid, in_specs=in_specs, out_specs=out_specs, out_shape=out_shape,
            scratch_shapes=scratch_shapes, compiler_params=_params(semantics))(*operands), []
    srcs, kinds = [s for s, _ in send], [k for _, k in send]
    n, n_in, n_out, n_scr = len(srcs), len(in_specs), len(out_specs), len(scratch_shapes)
    steps = math.prod(grid)
    relay_step = (13 * steps) // 16

    def plan(refs):
        src_refs, land_refs = refs[n_in:n_in + n], refs[n_in + n + n_out:n_in + 2 * n + n_out]
        send_sems, recv_sems, local_sems = refs[-3:]
        x, y, c = lax.axis_index("x"), lax.axis_index("y"), lax.axis_index("c")
        me, peers = _peers()
        first, relayed_in, relayed_out, arrivals, sends, own = [], [], [], [], [], []
        for a, kind in enumerate(kinds):
            def copy(k, src, dst_slot, to):
                return pltpu.make_async_remote_copy(
                    src_ref=src, dst_ref=land_refs[a].at[dst_slot], send_sem=send_sems.at[a * (N_DEV - 1) + k],
                    recv_sem=recv_sems.at[a * (N_DEV - 1) + k], device_id=to, device_id_type=pl.DeviceIdType.MESH)

            if kind == "gather_by_chip":
                idx = lambda px, py, pc: 4 * px + 2 * py + pc
                chips = [(1 - x, y), (x, 1 - y), (1 - x, 1 - y)]
                mine = [copy(0, src_refs[a], me, (x, y, 1 - c))]
                arrivals.append(copy(0, src_refs[a], idx(x, y, 1 - c), (x, y, 1 - c)))
                for j, (px, py) in enumerate(chips):
                    mine.append(copy(1 + j, src_refs[a], me, (px, py, c)))
                    relayed_in.append(copy(1 + j, src_refs[a], idx(px, py, c), (px, py, c)))
                    relayed_out.append(copy(4 + j, land_refs[a].at[idx(px, py, c)], idx(px, py, c), (x, y, 1 - c)))
                    arrivals.append(copy(4 + j, src_refs[a], idx(px, py, 1 - c), (x, y, 1 - c)))
                first += mine
                sends += mine + relayed_out[-3:]
                own.append(pltpu.make_async_copy(src_refs[a], land_refs[a].at[me], local_sems.at[a]))
            elif kind == "scatter_by_chip":
                for k, (px, py) in enumerate([(1 - x, y), (x, 1 - y), (1 - x, 1 - y)]):
                    first.append(copy(k, src_refs[a].at[2 * px + py], 2 * x + y, (px, py, c)))
                    arrivals.append(copy(k, src_refs[a].at[2 * px + py], 2 * px + py, (px, py, c)))
                sends += first[-3:]
                own.append(pltpu.make_async_copy(src_refs[a].at[2 * x + y], land_refs[a].at[2 * x + y],
                                                 local_sems.at[a]))
            else:
                part = (lambda i: src_refs[a].at[i]) if kind == "scatter" else (lambda i: src_refs[a])
                for k, (peer, peer_idx) in enumerate(peers):
                    first.append(copy(k, part(peer_idx), me, peer))
                    arrivals.append(copy(k, part(peer_idx), peer_idx, peer))
                sends += first[-(N_DEV - 1):]
                own.append(pltpu.make_async_copy(part(me), land_refs[a].at[me], local_sems.at[a]))
        return first, relayed_in, relayed_out, arrivals, sends, own

    def wrapped(*refs):
        step = 0
        for axis, size in enumerate(grid):
            step = step * size + pl.program_id(axis)

        @pl.when(step == 0)
        def _():
            first, _, _, _, _, own = plan(refs)
            for cp in first + own:
                cp.start()

        if "gather_by_chip" in kinds:
            @pl.when(step == relay_step)
            def _():
                _, relayed_in, relayed_out, _, _, _ = plan(refs)
                for cp_in, cp_out in zip(relayed_in, relayed_out):
                    cp_in.wait_recv()
                    cp_out.start()

        body(*refs[:n_in], *refs[n_in + n:n_in + n + n_out], *refs[n_in + 2 * n + n_out:n_in + 2 * n + n_out + n_scr])

        @pl.when(step == steps - 1)
        def _():
            _, _, _, arrivals, sends, own = plan(refs)
            for cp in arrivals:
                cp.wait_recv()
            for cp in sends:
                cp.wait_send()
            for cp in own:
                cp.wait()

    anywhere = pl.BlockSpec(memory_space=pl.ANY)
    lands = [jax.ShapeDtypeStruct((N_DEV // 2 if k == "scatter_by_chip" else N_DEV,) + s.shape[-2:], s.dtype)
             for s, k in send]
    out = pl.pallas_call(
        wrapped, name=name, grid=grid,
        in_specs=list(in_specs) + [anywhere] * n, out_specs=list(out_specs) + [anywhere] * n,
        out_shape=list(out_shape) + lands,
        scratch_shapes=list(scratch_shapes) + [pltpu.SemaphoreType.DMA((n * (N_DEV - 1),)),
                                               pltpu.SemaphoreType.DMA((n * (N_DEV - 1),)),
                                               pltpu.SemaphoreType.DMA((n,))],
        compiler_params=_params(("arbitrary",) * len(grid)),
    )(*operands, *srcs)
    return out[:n_out], list(out[n_out:])


def _pair_swap(blocks):
    def body(src_ref, out_ref, send_sems, recv_sems):
        x, y, c = lax.axis_index("x"), lax.axis_index("y"), lax.axis_index("c")
        copies = [pltpu.make_async_remote_copy(
            src_ref=src_ref.at[2 * chip + (1 - c)], dst_ref=out_ref.at[chip], send_sem=send_sems.at[chip],
            recv_sem=recv_sems.at[chip], device_id=(x, y, 1 - c), device_id_type=pl.DeviceIdType.MESH)
            for chip in range(N_DEV // 2)]
        for cp in copies:
            cp.start()
        for cp in copies:
            cp.wait()

    return pl.pallas_call(
        body, name="pair_swap_grad_w_in",
        in_specs=[pl.BlockSpec(memory_space=pl.ANY)], out_specs=pl.BlockSpec(memory_space=pl.ANY),
        out_shape=jax.ShapeDtypeStruct((N_DEV // 2,) + blocks.shape[1:], blocks.dtype),
        scratch_shapes=[pltpu.SemaphoreType.DMA((N_DEV // 2,)), pltpu.SemaphoreType.DMA((N_DEV // 2,))],
    )(blocks)


def _pair_sum(blocks, swapped, core):
    _, rows, cols = swapped.shape
    tile_rows = _row_tile(rows)

    def body(core_ref, mine_ref, theirs_ref, o_ref):
        o_ref[...] = (mine_ref[...].astype(F32) + theirs_ref[...].astype(F32)).astype(o_ref.dtype)

    return pl.pallas_call(
        body, name="pair_sum_grad_w_in",
        grid_spec=pltpu.PrefetchScalarGridSpec(
            num_scalar_prefetch=1, grid=(N_DEV // 2, rows // tile_rows),
            in_specs=[pl.BlockSpec((None, tile_rows, cols), lambda j, i, core_ref: (2 * j + core_ref[0], i, 0)),
                      pl.BlockSpec((None, tile_rows, cols), lambda j, i, core_ref: (j, i, 0))],
            out_specs=pl.BlockSpec((None, tile_rows, cols), lambda j, i, core_ref: (j, i, 0))),
        out_shape=jax.ShapeDtypeStruct(swapped.shape, swapped.dtype),
        compiler_params=_params(("parallel", "parallel")),
    )(core, blocks, swapped)


def _sum_in_device_order(land_ref):
    acc = land_ref[0].astype(F32)
    for j in range(1, land_ref.shape[0]):
        acc = acc + land_ref[j].astype(F32)
    return acc


def _adam_math(w, g, m, v):
    c1 = 1.0 - ADAM_B1 ** ADAM_STEP
    c2 = 1.0 - ADAM_B2 ** ADAM_STEP
    m_new = ADAM_B1 * m + (1.0 - ADAM_B1) * g
    v_new = ADAM_B2 * v + (1.0 - ADAM_B2) * (g * g)
    delta = -ADAM_LR * ((m_new / c1) / (jnp.sqrt(v_new / c2) + ADAM_EPS) + ADAM_WD * w)
    return delta, m_new, v_new


def _row_tile(rows):
    return max(t for t in range(16, 385, 16) if rows % t == 0) if rows % 16 == 0 else rows


def _sum_update(land, w, m, v, name):
    slots, rows, cols = land.shape
    tile_rows = _row_tile(rows)

    def body(land_ref, w_ref, m_ref, v_ref, g_ref, d_ref, nm_ref, nv_ref):
        g = _sum_in_device_order(land_ref)
        g_ref[...] = g
        d_ref[...], nm_ref[...], nv_ref[...] = _adam_math(w_ref[...], g, m_ref[...], v_ref[...])

    tile = pl.BlockSpec((None, tile_rows, cols), lambda i: (0, i, 0))
    out = jax.ShapeDtypeStruct((1, rows, cols), F32)
    return pl.pallas_call(
        body, name=name, grid=(rows // tile_rows,),
        in_specs=[pl.BlockSpec((slots, tile_rows, cols), lambda i: (0, i, 0)), tile, tile, tile],
        out_specs=[tile] * 4, out_shape=[out] * 4,
        compiler_params=_params(("parallel",)),
    )(land, w, m, v)


def _to_bf16(arrays):
    def body(*refs):
        for src_ref, dst_ref in zip(refs[:len(arrays)], refs[len(arrays):]):
            dst_ref[...] = src_ref[...].astype(BF16)

    whole = [pl.BlockSpec(a.shape, lambda i: (0, 0)) for a in arrays]
    return pl.pallas_call(
        body, name="shards_to_bf16", grid=(1,), in_specs=whole, out_specs=whole,
        out_shape=[jax.ShapeDtypeStruct(a.shape, BF16) for a in arrays],
        compiler_params=_params(("arbitrary",)),
    )(*arrays)


N_GAINS = 3


def _sum_update_gains(land, ws, ms, vs):
    def body(land_ref, *refs):
        w_refs, m_refs, v_refs = (refs[k * N_GAINS:(k + 1) * N_GAINS] for k in range(3))
        loss_ref, out_refs = refs[3 * N_GAINS], refs[3 * N_GAINS + 1:]
        rows = _sum_in_device_order(land_ref)
        loss_ref[...] = rows[N_GAINS:N_GAINS + 1]
        for k in range(N_GAINS):
            g = rows[k:k + 1]
            g_ref, d_ref, nm_ref, nv_ref = out_refs[4 * k:4 * k + 4]
            g_ref[...] = g
            d_ref[...], nm_ref[...], nv_ref[...] = _adam_math(w_refs[k][...], g, m_refs[k][...], v_refs[k][...])

    row = pl.BlockSpec((1, D_MODEL), lambda i: (0, 0))
    out = jax.ShapeDtypeStruct((1, D_MODEL), F32)
    n_out = 1 + 4 * N_GAINS
    return pl.pallas_call(
        body, name="update_gains", grid=(1,),
        in_specs=[pl.BlockSpec(land.shape, lambda i: (0, 0, 0))] + [row] * (3 * N_GAINS),
        out_specs=[row] * n_out, out_shape=[out] * n_out,
    )(land, *ws, *ms, *vs)


GROUP_FFN = ("w_ffn_in", "w_ffn_out")
GROUP_MIX = ("w_sb_up", "w_dil_up", "w_out")
COL_SHARDED = ("w_in", "w_sb_up", "w_dil_up", "w_ffn_in")
TRANSPOSED = ("w_in", "w_ffn_in")


def _full_from_shards(name, slots):
    _, r, c = slots.shape
    if name in TRANSPOSED:
        return slots.reshape(N_DEV * r, c).T
    if name in COL_SHARDED:
        return slots.transpose(1, 0, 2).reshape(r, N_DEV * c)
    return slots.reshape(N_DEV * r, c)


def _row_shards(full):
    rows, cols = full.shape
    return full.reshape(N_DEV, rows // N_DEV, cols)


def _local_step(x, target, g_mix, g_ffn, g_fin, w_in_t, shards=None, rest=None):
    gather = lambda names, kind: None if shards is None else [(shards[n], kind) for n in names]
    scatter = lambda blocks: None if shards is None else [(t, "scatter") for t in blocks]
    landed = lambda blocks, lands: lands if lands else blocks

    w = {"w_in": w_in_t}
    if shards is None:
        w.update(rest)
        w["w_ffn_in"] = rest["w_ffn_in"].T
    (qkv_sb, qkv_dl, gates, u), _ = _norm_proj(x, g_mix, w["w_in"])
    qkv_sb = qkv_sb.reshape(B_LOC, SEQ, 3 * SB_WIDTH)
    qkv_dl = qkv_dl.reshape(B_LOC, SEQ, 3 * DIL_WIDTH)
    (o_sb,), lands = _sb_fwd(qkv_sb, gather(GROUP_FFN, "gather_by_chip"))
    if lands:
        w["w_ffn_in"], w["w_ffn_out"] = lands[0].reshape(2 * D_FF, D_MODEL), _full_from_shards("w_ffn_out", lands[1])
    o_sb = o_sb.reshape(TOK, SB_WIDTH)
    (o_dl, lse), lands = _dil_fwd(qkv_dl, gather(GROUP_MIX, "gather"))
    w.update({n: _full_from_shards(n, t) for n, t in zip(GROUP_MIX, lands)})
    o_dl = o_dl.reshape(TOK, DIL_OUT)

    loss, dx1, merged, u2, act, dh, dx2, dg_fin, dg_ffn = _mix_ffn_fwd_bwd(
        x, o_sb, o_dl, gates, w["w_sb_up"], w["w_dil_up"], w["w_out"], target, g_ffn, g_fin,
        w["w_ffn_in"], w["w_ffn_out"])
    dgates, dy_sb, dy_dl, do_sb, do_dl, dsum = _mix_bwd(dx1, o_sb, o_dl, gates, w["w_sb_up"], w["w_dil_up"], w["w_out"])
    blocks = {
        "w_sb_up": _atb(o_sb, dy_sb, "grad_w_sb_up", SB_WIDTH, D_MODEL, col_blocks=N_DEV),
        "w_dil_up": _atb(o_dl, dy_dl, "grad_w_dil_up", DIL_OUT, D_MODEL, col_blocks=N_DEV),
        "w_out": _row_shards(_atb(merged, dx1, "grad_w_out", D_MODEL, D_MODEL)),
        "w_ffn_in": _row_shards(_atb_cols(dh, u2, "grad_w_ffn_in", 512)),
        "w_ffn_out": _row_shards(_atb_cols(act, dx2, "grad_w_ffn_out", 256)),
    }
    grads = {}

    early, late = GROUP_FFN, GROUP_MIX
    early_blocks = [blocks[n] for n in early]
    (dq_sb, dk_sb, dv_sb), lands = _sb_bwd(qkv_sb, do_sb.reshape(B_LOC, SEQ, SB_WIDTH), scatter(early_blocks))
    grads.update(zip(early, landed(early_blocks, lands)))
    as_batch = lambda t: t.reshape(B_LOC, SEQ, DIL_OUT)
    late_blocks = [blocks[n] for n in late]
    d_dl, lands = _dil_bwd(qkv_dl, as_batch(do_dl), lse, as_batch(dsum), scatter(late_blocks))
    grads.update(zip(late, landed(late_blocks, lands)))
    flat = lambda t: t.reshape(TOK, -1)
    dproj = ([flat(dq_sb), flat(dk_sb), flat(dv_sb)]
             + [flat(d_dl[3 * grp + part]) for part in range(3) for grp in range(DIL_GROUPS)] + [dgates])

    w_in_blocks = _row_shards(_atb_pieces(u, dproj, "grad_w_in", D_MODEL // 2))
    if shards is None:
        grads["w_in"] = w_in_blocks
        send = None
    else:
        core = lax.axis_index("c").astype(jnp.int32).reshape(1)
        send = [(_pair_sum(w_in_blocks, _pair_swap(w_in_blocks), core), "scatter_by_chip")]
    (grad_x, dg_mix), lands = _proj_bwd(dproj, dx1, x, g_mix, w["w_in"], send)
    if lands:
        grads["w_in"] = lands[0]
    gain_grads = jnp.concatenate([dg_mix, dg_ffn, dg_fin], axis=0)
    return loss, grad_x, gain_grads, grads


def kernel(x, norm_mix_g, w_in, w_sb_up, w_dil_up, w_out, norm_ffn_g, w_ffn_in, w_ffn_out, norm_final_g, loss_target, m_norm_mix_g, m_w_in, m_w_sb_up, m_w_dil_up, m_w_out, m_norm_ffn_g, m_w_ffn_in, m_w_ffn_out, m_norm_final_g, v_norm_mix_g, v_w_in, v_w_sb_up, v_w_dil_up, v_w_out, v_norm_ffn_g, v_w_ffn_in, v_w_ffn_out, v_norm_final_g):
    mats = {"w_in": w_in, "w_sb_up": w_sb_up, "w_dil_up": w_dil_up, "w_out": w_out,
            "w_ffn_in": w_ffn_in, "w_ffn_out": w_ffn_out}
    moments_m = {"w_in": m_w_in, "w_sb_up": m_w_sb_up, "w_dil_up": m_w_dil_up, "w_out": m_w_out,
                 "w_ffn_in": m_w_ffn_in, "w_ffn_out": m_w_ffn_out}
    moments_v = {"w_in": v_w_in, "w_sb_up": v_w_sb_up, "w_dil_up": v_w_dil_up, "w_out": v_w_out,
                 "w_ffn_in": v_w_ffn_in, "w_ffn_out": v_w_ffn_out}
    stored = lambda t, name: t.transpose(0, 2, 1) if name in TRANSPOSED else t
    mats = {name: stored(t, name) for name, t in mats.items()}
    shards = dict(zip(mats, _to_bf16([t[0] for t in mats.values()])))
    gathered_w_in = _all_gather(shards.pop("w_in"), "all_gather_w_in")
    g_fin = norm_final_g.reshape(1, D_MODEL)
    loss, grad_x, gain_grads, grad_slots = _local_step(
        x.reshape(TOK, D_MODEL), loss_target.reshape(TOK, D_MODEL), norm_mix_g, norm_ffn_g, g_fin,
        gathered_w_in.reshape(IN_WIDTH, D_MODEL), shards=shards)

    gain_rows = jnp.concatenate([gain_grads, jnp.tile(loss, (1, D_MODEL // LANES)),
                                 jnp.zeros((8 - 4, D_MODEL), F32)], axis=0)
    row = lambda t: t.reshape(1, D_MODEL)
    loss_row, *gain_outs = _sum_update_gains(
        _all_gather(gain_rows, "all_gather_gains"),
        [norm_mix_g, norm_ffn_g, g_fin], [m_norm_mix_g, m_norm_ffn_g, row(m_norm_final_g)],
        [v_norm_mix_g, v_norm_ffn_g, row(v_norm_final_g)])

    out_g, out_d, out_m, out_v = {}, {}, {}, {}
    for name, slots in grad_slots.items():
        out_g[name], out_d[name], out_m[name], out_v[name] = [stored(t, name) for t in _sum_update(
            slots, mats[name], stored(moments_m[name], name), stored(moments_v[name], name), "update_" + name)]
    for idx, name in enumerate(("norm_mix_g", "norm_ffn_g", "norm_final_g")):
        shape = (D_MODEL,) if name == "norm_final_g" else (1, D_MODEL)
        out_g[name], out_d[name], out_m[name], out_v[name] = [t.reshape(shape) for t in gain_outs[4 * idx:4 * idx + 4]]

    order = ("norm_mix_g", "w_in", "w_sb_up", "w_dil_up", "w_out", "norm_ffn_g", "w_ffn_in", "w_ffn_out",
             "norm_final_g")
    return (loss_row[0, 0], grad_x.reshape(B_LOC, SEQ, D_MODEL),
            *[out_g[n] for n in order], *[out_d[n] for n in order],
            *[out_m[n] for n in order], *[out_v[n] for n in order])
```

```python
import math

import jax
import jax.numpy as jnp
from jax import lax
from jax.experimental import pallas as pl
from jax.experimental.pallas import tpu as pltpu

F32 = jnp.float32
BF16 = jnp.bfloat16

N_DEV = 8
D_MODEL = 1024
SEQ = 2048
B_LOC = 2
TOK = B_LOC * SEQ
HEAD_DIM = 64
SB_WIDTH = 512
DIL_WIDTH = 768
DIL_OUT = 256
QKV_WIDTH = 3 * SB_WIDTH + 3 * DIL_WIDTH
IN_WIDTH = QKV_WIDTH + 2 * D_MODEL
D_FF = 2816
DIL_PAIRS = ((128, 1), (512, 4), (2048, 16))
DIL_HEADS = 12
RMS_EPS = 1e-6
ALIBI_MAX_BIAS = 8.0
QK_SCALE = 1.0 / math.sqrt(HEAD_DIM)
BLK = 128
LANES = 128
NEG_BIG = -1e30

ADAM_LR = 0.001
ADAM_B1 = 0.9
ADAM_B2 = 0.999
ADAM_EPS = 1e-08
ADAM_WD = 0.01
ADAM_STEP = 10

VMEM_LIMIT = 58 * 1024 * 1024


def _dot(a, b):
    return jnp.dot(a, b, preferred_element_type=F32)


def _dot_nt(a, b):
    return lax.dot_general(a, b, (((1,), (1,)), ((), ())), preferred_element_type=F32)


def _dot_tn(a, b):
    return lax.dot_general(a, b, (((0,), (0,)), ((), ())), preferred_element_type=F32)


def _sigmoid(z):
    return 1.0 / (1.0 + jnp.exp(-z))


def _split_bf16(v):
    hi = v.astype(BF16)
    lo = (v - hi.astype(F32)).astype(BF16)
    return hi, lo


def _chunks(width, step=512):
    out, c = [], 0
    while c < width:
        w = min(step, width - c)
        out.append((c, w))
        c += w
    return out


def _resident(shape):
    nd = len(shape)
    return pl.BlockSpec(shape, lambda *_: (0,) * nd, pipeline_mode=pl.Buffered(1))


def _params(sem):
    return pltpu.CompilerParams(dimension_semantics=sem, vmem_limit_bytes=VMEM_LIMIT)


def _rms_fwd(x, g):
    r = lax.rsqrt(jnp.mean(x * x, axis=-1, keepdims=True) + RMS_EPS)
    n = x * r
    return n, r, n * g


def _rms_bwd(dy, n, r, g):
    dg = jnp.sum(dy * n, axis=0, keepdims=True)
    dn = dy * g
    dx = r * (dn - n * jnp.mean(dn * n, axis=-1, keepdims=True))
    return dx, dg


TM = 256
TM_WIDE = 512


def _norm_proj(x, g, w_in_t, send=None):
    def body(x_ref, g_ref, w_ref, sb_ref, dl_ref, gate_ref, u_ref):
        _, _, u = _rms_fwd(x_ref[...], g_ref[...])
        u = u.astype(BF16)
        u_ref[...] = u
        for c0, w in _chunks(3 * SB_WIDTH):
            sb_ref[:, c0:c0 + w] = _dot_nt(u, w_ref[c0:c0 + w, :]).astype(BF16)
        for c0, w in _chunks(3 * DIL_WIDTH):
            dl_ref[:, c0:c0 + w] = _dot_nt(u, w_ref[3 * SB_WIDTH + c0:3 * SB_WIDTH + c0 + w, :])
        for c0, w in _chunks(2 * D_MODEL):
            gate_ref[:, c0:c0 + w] = _dot_nt(u, w_ref[QKV_WIDTH + c0:QKV_WIDTH + c0 + w, :])

    return _call(
        body, send, name="norm_proj", grid=(TOK // TM_WIDE,),
        in_specs=[pl.BlockSpec((TM_WIDE, D_MODEL), lambda i: (i, 0)), _resident((1, D_MODEL)),
                  _resident((IN_WIDTH, D_MODEL))],
        out_specs=[pl.BlockSpec((TM_WIDE, 3 * SB_WIDTH), lambda i: (i, 0)),
                   pl.BlockSpec((TM_WIDE, 3 * DIL_WIDTH), lambda i: (i, 0)),
                   pl.BlockSpec((TM_WIDE, 2 * D_MODEL), lambda i: (i, 0)),
                   pl.BlockSpec((TM_WIDE, D_MODEL), lambda i: (i, 0))],
        out_shape=[jax.ShapeDtypeStruct((TOK, 3 * SB_WIDTH), BF16),
                   jax.ShapeDtypeStruct((TOK, 3 * DIL_WIDTH), F32),
                   jax.ShapeDtypeStruct((TOK, 2 * D_MODEL), F32),
                   jax.ShapeDtypeStruct((TOK, D_MODEL), BF16)],
        scratch_shapes=[], semantics=("parallel",), operands=(x, g, w_in_t))


FF_CHUNK = 1024


def _mix_ffn_fwd_bwd(x, o_sb, o_dl, gates, w_sb_up, w_dil_up, w_out, target, g_ffn, g_fin, w_ffn_in_t, w_ffn_out):
    def body(x_ref, osb_ref, odl_ref, gate_ref, wsb_ref, wdl_ref, wo_ref, t_ref, gffn_ref, gfin_ref, win_ref, wout_ref,
             loss_ref, dx1_ref, mg_ref, u2_ref, act_ref, dh_ref, dx2_ref, dgfin_ref, dgffn_ref, h_scr):
        i = pl.program_id(0)

        @pl.when(i == 0)
        def _():
            loss_ref[...] = jnp.zeros_like(loss_ref)
            dgfin_ref[...] = jnp.zeros_like(dgfin_ref)
            dgffn_ref[...] = jnp.zeros_like(dgffn_ref)

        y_sb = _dot(osb_ref[...], wsb_ref[...])
        y_dl = _dot(odl_ref[...].astype(BF16), wdl_ref[...])
        merged = (_sigmoid(gate_ref[:, :D_MODEL]) * y_sb
                  + _sigmoid(gate_ref[:, D_MODEL:]) * y_dl).astype(BF16)
        mg_ref[...] = merged
        x1 = x_ref[...] + _dot(merged, wo_ref[...])
        g_ffn_v = gffn_ref[...]
        g_fin_v = gfin_ref[...]
        n2, r2, u2 = _rms_fwd(x1, g_ffn_v)
        u2 = u2.astype(BF16)
        u2_ref[...] = u2
        x2 = x1
        for c0, w in _chunks(D_FF, FF_CHUNK):
            gate = _dot_nt(u2, win_ref[c0:c0 + w, :])
            up = _dot_nt(u2, win_ref[D_FF + c0:D_FF + c0 + w, :])
            h_scr[:, c0:c0 + w] = gate
            h_scr[:, D_FF + c0:D_FF + c0 + w] = up
            act = (gate * _sigmoid(gate) * up).astype(BF16)
            act_ref[:, c0:c0 + w] = act
            x2 = x2 + _dot(act, wout_ref[c0:c0 + w, :])
        n3, r3, y = _rms_fwd(x2, g_fin_v)
        err = y - t_ref[...]
        sq = jnp.sum(jnp.sum(err * err, axis=1, keepdims=True), axis=0, keepdims=True)
        loss_ref[...] += sq * (0.5 / D_MODEL)
        dx2, dgfin = _rms_bwd(err * (1.0 / D_MODEL), n3, r3, g_fin_v)
        dgfin_ref[...] += dgfin
        dx2_b = dx2.astype(BF16)
        dx2_ref[...] = dx2_b
        du2 = jnp.zeros((TM, D_MODEL), F32)
        for c0, w in _chunks(D_FF, FF_CHUNK):
            gate = h_scr[:, c0:c0 + w]
            up = h_scr[:, D_FF + c0:D_FF + c0 + w]
            dact = _dot_nt(dx2_b, wout_ref[c0:c0 + w, :])
            sg = _sigmoid(gate)
            dgate = (dact * up * (sg * (1.0 + gate * (1.0 - sg)))).astype(BF16)
            dup = (dact * (gate * sg)).astype(BF16)
            dh_ref[:, c0:c0 + w] = dgate
            dh_ref[:, D_FF + c0:D_FF + c0 + w] = dup
            du2 = du2 + _dot(dgate, win_ref[c0:c0 + w, :])
            du2 = du2 + _dot(dup, win_ref[D_FF + c0:D_FF + c0 + w, :])
        dx1_n, dgffn = _rms_bwd(du2, n2, r2, g_ffn_v)
        dgffn_ref[...] += dgffn
        dx1_ref[...] = dx2 + dx1_n

    tile = lambda w: pl.BlockSpec((TM, w), lambda i: (i, 0))
    acc = lambda w: pl.BlockSpec((1, w), lambda i: (0, 0))
    return pl.pallas_call(
        body, name="mix_ffn_fwd_bwd", grid=(TOK // TM,),
        in_specs=[tile(D_MODEL), tile(SB_WIDTH), tile(DIL_OUT), tile(2 * D_MODEL),
                  _resident((SB_WIDTH, D_MODEL)), _resident((DIL_OUT, D_MODEL)), _resident((D_MODEL, D_MODEL)),
                  tile(D_MODEL), _resident((1, D_MODEL)), _resident((1, D_MODEL)),
                  _resident((2 * D_FF, D_MODEL)), _resident((D_FF, D_MODEL))],
        out_specs=[acc(LANES), tile(D_MODEL), tile(D_MODEL), tile(D_MODEL), tile(D_FF), tile(2 * D_FF),
                   tile(D_MODEL), acc(D_MODEL), acc(D_MODEL)],
        out_shape=[jax.ShapeDtypeStruct((1, LANES), F32),
                   jax.ShapeDtypeStruct((TOK, D_MODEL), F32),
                   jax.ShapeDtypeStruct((TOK, D_MODEL), BF16),
                   jax.ShapeDtypeStruct((TOK, D_MODEL), BF16),
                   jax.ShapeDtypeStruct((TOK, D_FF), BF16),
                   jax.ShapeDtypeStruct((TOK, 2 * D_FF), BF16),
                   jax.ShapeDtypeStruct((TOK, D_MODEL), BF16),
                   jax.ShapeDtypeStruct((1, D_MODEL), F32),
                   jax.ShapeDtypeStruct((1, D_MODEL), F32)],
        scratch_shapes=[pltpu.VMEM((TM, 2 * D_FF), F32)],
        compiler_params=_params(("arbitrary",)),
    )(x, o_sb, o_dl, gates, w_sb_up, w_dil_up, w_out, target, g_ffn, g_fin, w_ffn_in_t, w_ffn_out)


def _mix_bwd(dx1, o_sb, o_dl, gates, w_sb_up, w_dil_up, w_out):
    def body(dx1_ref, osb_ref, odl_ref, gate_ref, wsb_ref, wdl_ref, wout_ref,
             dgate_ref, dysb_ref, dydl_ref, dosb_ref, dodl_ref, dsum_ref):
        dmerged = _dot_nt(dx1_ref[...].astype(BF16), wout_ref[...])
        o_dl = odl_ref[...]
        y_sb = _dot(osb_ref[...], wsb_ref[...])
        y_dl = _dot(o_dl.astype(BF16), wdl_ref[...])
        s_sb = _sigmoid(gate_ref[:, :D_MODEL])
        s_dl = _sigmoid(gate_ref[:, D_MODEL:])
        dgate_ref[:, :D_MODEL] = (dmerged * y_sb * (s_sb * (1.0 - s_sb))).astype(BF16)
        dgate_ref[:, D_MODEL:] = (dmerged * y_dl * (s_dl * (1.0 - s_dl))).astype(BF16)
        dy_sb = (dmerged * s_sb).astype(BF16)
        dy_dl = (dmerged * s_dl).astype(BF16)
        dysb_ref[...] = dy_sb
        dydl_ref[...] = dy_dl
        dosb_ref[...] = _dot_nt(dy_sb, wsb_ref[...]).astype(BF16)
        do_dl = _dot_nt(dy_dl, wdl_ref[...])
        dodl_ref[...] = do_dl
        row = lax.broadcasted_iota(jnp.int32, (DIL_OUT, DIL_OUT), 0) // HEAD_DIM
        col = lax.broadcasted_iota(jnp.int32, (DIL_OUT, DIL_OUT), 1) // HEAD_DIM
        same_head = (row == col).astype(BF16)
        hi, lo = _split_bf16(do_dl * o_dl)
        dsum_ref[...] = _dot(hi, same_head) + _dot(lo, same_head)

    tile = lambda w: pl.BlockSpec((TM_WIDE, w), lambda i: (i, 0))
    return pl.pallas_call(
        body, name="mix_bwd", grid=(TOK // TM_WIDE,),
        in_specs=[tile(D_MODEL), tile(SB_WIDTH), tile(DIL_OUT), tile(2 * D_MODEL),
                  _resident((SB_WIDTH, D_MODEL)), _resident((DIL_OUT, D_MODEL)),
                  _resident((D_MODEL, D_MODEL))],
        out_specs=[tile(2 * D_MODEL), tile(D_MODEL), tile(D_MODEL), tile(SB_WIDTH), tile(DIL_OUT),
                   tile(DIL_OUT)],
        out_shape=[jax.ShapeDtypeStruct((TOK, 2 * D_MODEL), BF16),
                   jax.ShapeDtypeStruct((TOK, D_MODEL), BF16),
                   jax.ShapeDtypeStruct((TOK, D_MODEL), BF16),
                   jax.ShapeDtypeStruct((TOK, SB_WIDTH), BF16),
                   jax.ShapeDtypeStruct((TOK, DIL_OUT), F32),
                   jax.ShapeDtypeStruct((TOK, DIL_OUT), F32)],
        compiler_params=_params(("parallel",)),
    )(dx1, o_sb, o_dl, gates, w_sb_up, w_dil_up, w_out)


def _proj_bwd(dproj, dx1, x, g, w_in_t, send=None):
    widths = [p.shape[1] for p in dproj]

    def body(*refs):
        dx1_ref, x_ref, g_ref, w_ref, dx_ref, dg_ref = refs[len(widths):]

        @pl.when(pl.program_id(0) == 0)
        def _():
            dg_ref[...] = jnp.zeros_like(dg_ref)

        du = jnp.zeros((TM_WIDE, D_MODEL), F32)
        c0 = 0
        for dp_ref, w in zip(refs, widths):
            du = du + _dot(dp_ref[...].astype(BF16), w_ref[c0:c0 + w, :])
            c0 += w
        g_v = g_ref[...]
        n, r, _ = _rms_fwd(x_ref[...], g_v)
        dx, dg = _rms_bwd(du, n, r, g_v)
        dg_ref[...] += dg
        dx_ref[...] = dx1_ref[...] + dx

    tile = lambda w: pl.BlockSpec((TM_WIDE, w), lambda i: (i, 0))
    return _call(
        body, send, name="proj_bwd", grid=(TOK // TM_WIDE,),
        in_specs=[tile(w) for w in widths] + [tile(D_MODEL), tile(D_MODEL), _resident((1, D_MODEL)),
                                              _resident((IN_WIDTH, D_MODEL))],
        out_specs=[tile(D_MODEL), pl.BlockSpec((1, D_MODEL), lambda i: (0, 0))],
        out_shape=[jax.ShapeDtypeStruct((TOK, D_MODEL), F32),
                   jax.ShapeDtypeStruct((1, D_MODEL), F32)],
        scratch_shapes=[], semantics=("arbitrary",), operands=(*dproj, dx1, x, g, w_in_t))


def _atb_pieces(a, pieces, name, tm, tk=512):
    m = a.shape[1]
    widths = [p.shape[1] for p in pieces]
    n = sum(widths)
    nk = TOK // tk

    def body(a_ref, *refs):
        o_ref, acc_ref = refs[len(widths):]
        k = pl.program_id(1)

        @pl.when(k == 0)
        def _():
            acc_ref[...] = jnp.zeros_like(acc_ref)

        a_v = a_ref[...]
        c0 = 0
        for p_ref, w in zip(refs, widths):
            acc_ref[:, c0:c0 + w] += _dot_tn(a_v, p_ref[...].astype(BF16))
            c0 += w

        @pl.when(k == nk - 1)
        def _():
            for c0, w in _chunks(n):
                o_ref[c0:c0 + w, :] = acc_ref[:, c0:c0 + w].T.astype(BF16)

    return pl.pallas_call(
        body, name=name, grid=(m // tm, nk),
        in_specs=[pl.BlockSpec((tk, tm), lambda i, k: (k, i))]
                 + [pl.BlockSpec((tk, w), lambda i, k: (k, 0)) for w in widths],
        out_specs=pl.BlockSpec((n, tm), lambda i, k: (0, i)),
        out_shape=jax.ShapeDtypeStruct((n, m), BF16),
        scratch_shapes=[pltpu.VMEM((tm, n), F32)],
        compiler_params=_params(("parallel", "arbitrary")),
    )(a, *pieces)


def _atb_cols(a, b, name, cols, tk=512):
    m, other = a.shape[1], b.shape[1]

    def body(a_ref, b_ref, o_ref):
        acc = jnp.zeros(o_ref.shape, F32)
        for k0 in range(0, TOK, tk):
            acc = acc + _dot_tn(a_ref[k0:k0 + tk, :], b_ref[k0:k0 + tk, :])
        o_ref[...] = acc.astype(BF16)

    assert m % cols == 0
    return pl.pallas_call(
        body, name=name, grid=(m // cols,),
        in_specs=[pl.BlockSpec((TOK, cols), lambda r: (0, r)), _resident((TOK, other))],
        out_specs=pl.BlockSpec((cols, other), lambda r: (r, 0)),
        out_shape=jax.ShapeDtypeStruct((m, other), BF16),
        compiler_params=_params(("parallel",)),
    )(a, b)


def _atb(a, b, name, tm, tn, col_blocks=0, tk=512):
    m, n = a.shape[1], b.shape[1]
    nk = TOK // tk

    def body(a_ref, b_ref, o_ref, acc_ref):
        k = pl.program_id(2)

        @pl.when(k == 0)
        def _():
            acc_ref[...] = jnp.zeros_like(acc_ref)

        acc_ref[...] += _dot_tn(a_ref[...].astype(BF16), b_ref[...].astype(BF16))

        @pl.when(k == nk - 1)
        def _():
            if col_blocks:
                width = n // col_blocks
                for blk in range(col_blocks):
                    o_ref[blk] = acc_ref[:, blk * width:(blk + 1) * width].astype(BF16)
            else:
                o_ref[...] = acc_ref[...].astype(BF16)

    if col_blocks:
        out_spec = pl.BlockSpec((col_blocks, tm, n // col_blocks), lambda i, j, k: (0, i, 0))
        out_shape = jax.ShapeDtypeStruct((col_blocks, m, n // col_blocks), BF16)
    else:
        out_spec = pl.BlockSpec((tm, tn), lambda i, j, k: (i, j))
        out_shape = jax.ShapeDtypeStruct((m, n), BF16)
    return pl.pallas_call(
        body, name=name, grid=(m // tm, n // tn, nk),
        in_specs=[pl.BlockSpec((tk, tm), lambda i, j, k: (k, i)),
                  pl.BlockSpec((tk, tn), lambda i, j, k: (k, j))],
        out_specs=out_spec, out_shape=out_shape,
        scratch_shapes=[pltpu.VMEM((tm, tn), F32)],
        compiler_params=_params(("parallel", "parallel", "arbitrary")),
    )(a, b)


SB_PAIRS = SB_WIDTH // LANES


def _two_heads(v, lane0):
    zero = jnp.zeros_like(v)
    return jnp.where(lane0, v, zero), jnp.where(lane0, zero, v)


SB_QBLK = 256
N_SB_STEPS = SEQ // SB_QBLK


SB_KCHUNK = 2 * BLK
SB_ROWS = 2 * SB_QBLK
SB_DEAD = -104.0


def _log_keep(z):
    neg_z = -z
    return jnp.minimum(neg_z, 0.0) - jnp.log(1.0 + jnp.exp(jnp.minimum(z, neg_z)))


def _stack_heads(v, lane0):
    return jnp.concatenate(_two_heads(v, lane0), axis=0)


def _block_sums(v, tri, split=True):
    halves = (v[:, :BLK], v[:, BLK:])
    stacked = jnp.concatenate(halves, axis=0)
    if split:
        hi, lo = _split_bf16(stacked)
        prod = _dot(jnp.concatenate([hi, lo], axis=0), tri)
        tri_sum = prod[:2 * SB_ROWS] + prod[2 * SB_ROWS:]
    else:
        tri_sum = _dot(stacked.astype(BF16), tri)
    sums = tuple(jnp.sum(h, axis=1, keepdims=True) for h in halves)
    return (tri_sum[:SB_ROWS], tri_sum[SB_ROWS:]), sums


def _sb_diag_mask():
    row = lax.broadcasted_iota(jnp.int32, (SB_ROWS, SB_KCHUNK), 0)
    col = lax.broadcasted_iota(jnp.int32, (SB_ROWS, SB_KCHUNK), 1)
    return col < jnp.where(row >= SB_QBLK, row - SB_QBLK, row)


def _sb_fwd(qkv, send=None):
    def body(q_ref, k_ref, v_ref, o_ref):
        i = pl.program_id(2)
        krow = lax.broadcasted_iota(jnp.int32, (BLK, BLK), 0)
        kcol = lax.broadcasted_iota(jnp.int32, (BLK, BLK), 1)
        later = (krow > kcol).astype(BF16)
        lane0 = lax.broadcasted_iota(jnp.int32, (SB_QBLK, LANES), 1) < HEAD_DIM
        q2 = _stack_heads(q_ref[0] * QK_SCALE, lane0)

        def chunk(c, carry, causal):
            acc, run = carry
            off = pl.multiple_of(c * SB_KCHUNK, SB_KCHUNK)
            z = _dot_nt(q2, k_ref[0, pl.ds(off, SB_KCHUNK), :])
            log_keep = _log_keep(z)
            if causal is not None:
                log_keep = jnp.where(causal, log_keep, 0.0)
            suffix, sums = _block_sums(log_keep, later)
            log_after = jnp.concatenate([suffix[0] + (run + sums[1]), suffix[1] + run], axis=1)
            a = jnp.exp(log_keep + z + log_after)
            if causal is not None:
                a = jnp.where(causal, a, 0.0)
            acc = acc + _dot(a.astype(BF16), v_ref[0, pl.ds(off, SB_KCHUNK), :])
            return acc, run + (sums[0] + sums[1])

        acc, run = chunk(i, (jnp.zeros((SB_ROWS, LANES), F32), jnp.zeros((SB_ROWS, 1), F32)), _sb_diag_mask())

        def some_alive(run):
            return (jnp.max(run) > SB_DEAD).astype(jnp.int32)

        def trip(state):
            t, _, acc, run = state
            acc, run = chunk(i - 1 - t, (acc, run), None)
            return t + 1, some_alive(run), acc, run

        _, _, acc, _ = lax.while_loop(lambda s: jnp.logical_and(s[0] < i, s[1] > 0), trip,
                                      (jnp.int32(0), some_alive(run), acc, run))
        o_ref[0] = jnp.where(lane0, acc[:SB_QBLK], acc[SB_QBLK:]).astype(BF16)

    blk = pl.BlockSpec((1, SB_QBLK, LANES), lambda b, h, i: (b, i, h))
    return _call(
        body, send, name="sb_fwd", grid=(B_LOC, SB_PAIRS, N_SB_STEPS),
        in_specs=[blk,
                  pl.BlockSpec((1, SEQ, LANES), lambda b, h, i: (b, 0, SB_PAIRS + h)),
                  pl.BlockSpec((1, SEQ, LANES), lambda b, h, i: (b, 0, 2 * SB_PAIRS + h))],
        out_specs=[blk], out_shape=[jax.ShapeDtypeStruct((B_LOC, SEQ, SB_WIDTH), BF16)],
        scratch_shapes=[], semantics=("parallel", "parallel", "arbitrary"), operands=(qkv, qkv, qkv))


def _sb_bwd(qkv, d_o, send=None):
    def body(q_ref, k_ref, v_ref, do_ref, dq_ref, dk_ref, dv_ref, dk_acc, dv_acc, z_scr, keep_scr):
        i = pl.program_id(2)
        krow = lax.broadcasted_iota(jnp.int32, (BLK, BLK), 0)
        kcol = lax.broadcasted_iota(jnp.int32, (BLK, BLK), 1)
        upto = (krow <= kcol).astype(BF16)
        earlier = (krow < kcol).astype(BF16)
        lane0 = lax.broadcasted_iota(jnp.int32, (SB_QBLK, LANES), 1) < HEAD_DIM
        q2 = _stack_heads(q_ref[0] * QK_SCALE, lane0)
        do2 = _stack_heads(do_ref[0], lane0)

        def keep_sum(c, causal):
            off = pl.multiple_of(c * SB_KCHUNK, SB_KCHUNK)
            z = _dot_nt(q2, k_ref[0, pl.ds(off, SB_KCHUNK), :])
            log_keep = _log_keep(z)
            if causal is not None:
                log_keep = jnp.where(causal, log_keep, 0.0)
            z_scr[c] = z
            keep_scr[c] = log_keep
            return jnp.sum(log_keep, axis=1, keepdims=True)

        def some_alive(run):
            return (jnp.max(run) > SB_DEAD).astype(jnp.int32)

        def scan(state):
            t, _, run = state
            run = run + keep_sum(i - 1 - t, None)
            return t + 1, some_alive(run), run

        diag_sum = keep_sum(i, _sb_diag_mask())
        walked, _, tot2 = lax.while_loop(lambda s: jnp.logical_and(s[0] < i, s[1] > 0), scan,
                                         (jnp.int32(0), some_alive(diag_sum), diag_sum))
        first = i - walked

        @pl.when(i == 0)
        def _():
            dk_acc[...] = jnp.zeros_like(dk_acc)
            dv_acc[...] = jnp.zeros_like(dv_acc)

        def chunk(c, carry, causal):
            dq, pre_keep, pre_e = carry
            off = pl.multiple_of(c * SB_KCHUNK, SB_KCHUNK)
            k_c = k_ref[0, pl.ds(off, SB_KCHUNK), :]
            v_c = v_ref[0, pl.ds(off, SB_KCHUNK), :]
            d_a = _dot_nt(do2, v_c)
            log_keep = keep_scr[c]
            log_beta = log_keep + z_scr[c]
            prefix, sums = _block_sums(log_keep, upto)
            inclusive = jnp.concatenate([prefix[0], prefix[1] + sums[0]], axis=1)
            a = jnp.exp(log_beta + ((tot2 - pre_keep) - inclusive))
            if causal is not None:
                a = jnp.where(causal, a, 0.0)
            e = d_a * a
            e_prefix, e_sums = _block_sums(e, earlier, split=False)
            before = jnp.concatenate([e_prefix[0] + pre_e, e_prefix[1] + (pre_e + e_sums[0])], axis=1)
            dz = e - (e + before) * jnp.exp(log_beta)
            if causal is not None:
                dz = jnp.where(causal, dz, 0.0)
            dz = dz.astype(BF16)
            dq = dq + _dot(dz, k_c)
            dk_acc[pl.ds(off, SB_KCHUNK), :] += _dot_tn(dz, q2)
            dv_acc[pl.ds(off, SB_KCHUNK), :] += _dot_tn(a.astype(BF16), do2)
            return dq, pre_keep + (sums[0] + sums[1]), pre_e + (e_sums[0] + e_sums[1])

        zero_col = jnp.zeros((SB_ROWS, 1), F32)
        carry = lax.fori_loop(first, i, lambda t, c: chunk(t, c, None),
                              (jnp.zeros((SB_ROWS, LANES), F32), zero_col, zero_col))
        dq, _, _ = chunk(i, carry, _sb_diag_mask())
        dq_ref[0] = (jnp.where(lane0, dq[:SB_QBLK], dq[SB_QBLK:]) * QK_SCALE).astype(BF16)

        @pl.when(i == N_SB_STEPS - 1)
        def _():
            dk_ref[0] = dk_acc[...].astype(BF16)
            dv_ref[0] = dv_acc[...].astype(BF16)

    blk = pl.BlockSpec((1, SB_QBLK, LANES), lambda b, h, i: (b, i, h))
    whole = lambda c: pl.BlockSpec((1, SEQ, LANES), lambda b, h, i: (b, 0, c * SB_PAIRS + h))
    out = jax.ShapeDtypeStruct((B_LOC, SEQ, SB_WIDTH), BF16)
    return _call(
        body, send, name="sb_bwd", grid=(B_LOC, SB_PAIRS, N_SB_STEPS),
        in_specs=[blk, whole(1), whole(2), blk],
        out_specs=[blk, whole(0), whole(0)],
        out_shape=[out, out, out],
        scratch_shapes=[pltpu.VMEM((SEQ, LANES), F32), pltpu.VMEM((SEQ, LANES), F32),
                        pltpu.VMEM((N_SB_STEPS, SB_ROWS, SB_KCHUNK), F32),
                        pltpu.VMEM((N_SB_STEPS, SB_ROWS, SB_KCHUNK), F32)],
        semantics=("parallel", "parallel", "arbitrary"), operands=(qkv, qkv, qkv, d_o))


DIL_GROUPS = len(DIL_PAIRS)
DIL_QBLOCKS = SEQ // BLK


def _residue_rows(j, dilation):
    length = SEQ // dilation
    return pl.ds(j, length, stride=dilation) if dilation > 1 else pl.ds(0, length)


def _gather_residues(src_ref, dst_ref, dst_off, dilation, scale=None):
    length = SEQ // dilation
    for j in range(dilation):
        v = src_ref[_residue_rows(j, dilation), :]
        if scale is not None:
            v = v * scale
        dst_ref[dst_off + j * length:dst_off + (j + 1) * length, :] = v.astype(dst_ref.dtype)


def _scatter_residues(src_ref, src_off, dst_ref, dilation):
    length = SEQ // dilation
    for j in range(dilation):
        dst_ref[_residue_rows(j, dilation), :] = (
            src_ref[src_off + j * length:src_off + (j + 1) * length, :].astype(dst_ref.dtype))


def _dil_geometry(group, pair):
    dilation = DIL_PAIRS[group][1]
    row = lax.broadcasted_iota(jnp.int32, (2 * BLK, 2 * BLK), 0)
    col = lax.broadcasted_iota(jnp.int32, (2 * BLK, 2 * BLK), 1)
    second = row >= BLK
    steps = BLK + jnp.where(second, row - BLK, row) - col
    coef = -ALIBI_MAX_BIAS / DIL_HEADS * math.log(2.0)
    first_head = float(4 * group + 1) + 2.0 * pair.astype(F32)
    slope = jnp.exp(coef * (first_head + jnp.where(second, 1.0, 0.0)))
    bias = slope * (steps * dilation).astype(F32)
    valid = jnp.logical_and(steps >= 0, steps <= BLK)
    return bias, valid, col >= BLK


def _dil_tile_scores(q2, kk, geometry, has_prev):
    bias, valid, own = geometry
    ok = jnp.logical_and(valid, jnp.logical_or(own, has_prev))
    return jnp.where(ok, _dot_nt(q2, kk) - bias, NEG_BIG)


def _head_col(v, lane_mask):
    return jnp.max(jnp.where(lane_mask, v, NEG_BIG), axis=1, keepdims=True)


def _dil_fwd(qkv, send=None):
    def body(*refs):
        ins, (o_ref, lse_ref), (qs, ks, vs, o_res, lse_res) = refs[:9], refs[9:11], refs[11:16]
        o_grp, lse_grp = refs[16:19], refs[19:22]
        pair = pl.program_id(1)
        lane0 = lax.broadcasted_iota(jnp.int32, (BLK, LANES), 1) < HEAD_DIM
        ks[0:BLK, :] = jnp.zeros((BLK, LANES), BF16)
        vs[0:BLK, :] = jnp.zeros((BLK, LANES), BF16)
        for grp, (_, dilation) in enumerate(DIL_PAIRS):
            q_ref, k_ref, v_ref = ins[3 * grp:3 * grp + 3]
            per_residue = DIL_QBLOCKS // dilation
            _gather_residues(q_ref, qs, 0, dilation, QK_SCALE)
            _gather_residues(k_ref, ks, BLK, dilation)
            _gather_residues(v_ref, vs, BLK, dilation)
            geometry = _dil_geometry(grp, pair)

            def step(blk, _):
                off = pl.multiple_of(blk * BLK, BLK)
                q2 = _stack_heads(qs[pl.ds(off, BLK), :], lane0)
                s = _dil_tile_scores(q2, ks[pl.ds(off, 2 * BLK), :], geometry, blk % per_residue != 0)
                m = jnp.max(s, axis=1, keepdims=True)
                p = jnp.exp(s - m)
                den = jnp.sum(p, axis=1, keepdims=True)
                out = _dot(p.astype(BF16), vs[pl.ds(off, 2 * BLK), :]) / den
                lse = m + jnp.log(den)
                o_res[pl.ds(off, BLK), :] = jnp.where(lane0, out[:BLK], out[BLK:])
                lse_res[pl.ds(off, BLK), :] = jnp.where(lane0, lse[:BLK], lse[BLK:])
                return 0

            lax.fori_loop(0, DIL_QBLOCKS, step, 0, unroll=8)
            _scatter_residues(o_res, 0, o_grp[grp], dilation)
            _scatter_residues(lse_res, 0, lse_grp[grp], dilation)

        for r0 in range(0, SEQ, 2 * BLK):
            rows = slice(r0, r0 + 2 * BLK)
            ls = [lse_grp[g][rows, :] for g in range(DIL_GROUPS)]
            m = jnp.maximum(jnp.maximum(ls[0], ls[1]), ls[2])
            w = [jnp.exp(l - m) for l in ls]
            den = w[0] + w[1] + w[2]
            o_ref[rows, :] = (w[0] * o_grp[0][rows, :] + w[1] * o_grp[1][rows, :] + w[2] * o_grp[2][rows, :]) / den
            lse_ref[rows, :] = m + jnp.log(den)

    def col(part, grp):
        return pl.BlockSpec((None, SEQ, LANES), lambda b, p: (b, 0, 6 * part + 2 * grp + p))

    out_spec = pl.BlockSpec((None, SEQ, LANES), lambda b, p: (b, 0, p))
    out = jax.ShapeDtypeStruct((B_LOC, SEQ, DIL_OUT), F32)
    return _call(
        body, send, name="dil_fwd", grid=(B_LOC, DIL_OUT // LANES),
        in_specs=[col(part, grp) for grp in range(DIL_GROUPS) for part in range(3)],
        out_specs=[out_spec, out_spec], out_shape=[out, out],
        scratch_shapes=[pltpu.VMEM((SEQ, LANES), BF16), pltpu.VMEM((SEQ + BLK, LANES), BF16),
                        pltpu.VMEM((SEQ + BLK, LANES), BF16), pltpu.VMEM((SEQ, LANES), F32),
                        pltpu.VMEM((SEQ, LANES), F32)] + [pltpu.VMEM((SEQ, LANES), F32)] * (2 * DIL_GROUPS),
        semantics=("parallel", "parallel"), operands=[qkv] * 9)


def _dil_bwd(qkv, d_o, lse, dsum, send=None):
    def body(*refs):
        ins, (do_ref, lse_ref, dsum_ref), outs = refs[:9], refs[9:12], refs[12:21]
        qs, ks, vs, dos, lse_res, dsum_res, dq_res, dk_acc, dv_acc = refs[21:]
        pair = pl.program_id(1)
        lane0 = lax.broadcasted_iota(jnp.int32, (BLK, LANES), 1) < HEAD_DIM
        lane1 = jnp.logical_not(lane0)
        ks[0:BLK, :] = jnp.zeros((BLK, LANES), BF16)
        vs[0:BLK, :] = jnp.zeros((BLK, LANES), BF16)
        for grp, (_, dilation) in enumerate(DIL_PAIRS):
            q_ref, k_ref, v_ref = ins[3 * grp:3 * grp + 3]
            dq_ref, dk_ref, dv_ref = outs[3 * grp:3 * grp + 3]
            per_residue = DIL_QBLOCKS // dilation
            _gather_residues(q_ref, qs, 0, dilation, QK_SCALE)
            _gather_residues(k_ref, ks, BLK, dilation)
            _gather_residues(v_ref, vs, BLK, dilation)
            _gather_residues(do_ref, dos, 0, dilation)
            _gather_residues(lse_ref, lse_res, 0, dilation)
            _gather_residues(dsum_ref, dsum_res, 0, dilation)
            dk_acc[...] = jnp.zeros_like(dk_acc)
            dv_acc[...] = jnp.zeros_like(dv_acc)
            geometry = _dil_geometry(grp, pair)

            def step(blk, _):
                off = pl.multiple_of(blk * BLK, BLK)
                q2 = _stack_heads(qs[pl.ds(off, BLK), :], lane0)
                do2 = _stack_heads(dos[pl.ds(off, BLK), :], lane0)
                kk = ks[pl.ds(off, 2 * BLK), :]
                vv = vs[pl.ds(off, 2 * BLK), :]
                lse_blk = lse_res[pl.ds(off, BLK), :]
                dsum_blk = dsum_res[pl.ds(off, BLK), :]
                lse2 = jnp.concatenate([_head_col(lse_blk, lane0), _head_col(lse_blk, lane1)], axis=0)
                dsum2 = jnp.concatenate([_head_col(dsum_blk, lane0), _head_col(dsum_blk, lane1)], axis=0)
                s = _dil_tile_scores(q2, kk, geometry, blk % per_residue != 0)
                p = jnp.exp(s - lse2)
                ds = (p * (_dot_nt(do2, vv) - dsum2)).astype(BF16)
                dq2 = _dot(ds, kk)
                dq_res[pl.ds(off, BLK), :] = jnp.where(lane0, dq2[:BLK], dq2[BLK:]) * QK_SCALE
                dk_acc[pl.ds(off, 2 * BLK), :] += _dot_tn(ds, q2)
                dv_acc[pl.ds(off, 2 * BLK), :] += _dot_tn(p.astype(BF16), do2)
                return 0

            lax.fori_loop(0, DIL_QBLOCKS, step, 0, unroll=8)
            _scatter_residues(dq_res, 0, dq_ref, dilation)
            _scatter_residues(dk_acc, BLK, dk_ref, dilation)
            _scatter_residues(dv_acc, BLK, dv_ref, dilation)

    def col(part, grp):
        return pl.BlockSpec((None, SEQ, LANES), lambda b, p: (b, 0, 6 * part + 2 * grp + p))

    slot = pl.BlockSpec((None, SEQ, LANES), lambda b, p: (b, 0, p))
    out = jax.ShapeDtypeStruct((B_LOC, SEQ, DIL_OUT), F32)
    return _call(
        body, send, name="dil_bwd", grid=(B_LOC, DIL_OUT // LANES),
        in_specs=[col(part, grp) for grp in range(DIL_GROUPS) for part in range(3)] + [slot] * 3,
        out_specs=[slot] * 9, out_shape=[out] * 9,
        scratch_shapes=[pltpu.VMEM((SEQ, LANES), BF16), pltpu.VMEM((SEQ + BLK, LANES), BF16),
                        pltpu.VMEM((SEQ + BLK, LANES), BF16), pltpu.VMEM((SEQ, LANES), BF16),
                        pltpu.VMEM((SEQ, LANES), F32), pltpu.VMEM((SEQ, LANES), F32),
                        pltpu.VMEM((SEQ, LANES), F32), pltpu.VMEM((SEQ + BLK, LANES), F32),
                        pltpu.VMEM((SEQ + BLK, LANES), F32)],
        semantics=("parallel", "parallel"), operands=[qkv] * 9 + [d_o, lse, dsum])


def _peers():
    x, y, c = lax.axis_index("x"), lax.axis_index("y"), lax.axis_index("c")
    me = 4 * x + 2 * y + c
    peers = []
    for mask in range(1, N_DEV):
        px = 1 - x if mask & 4 else x
        py = 1 - y if mask & 2 else y
        pc = 1 - c if mask & 1 else c
        peers.append(((px, py, pc), 4 * px + 2 * py + pc))
    return me, peers


def _all_gather(shard, name):
    rows, cols = shard.shape
    by_rows = rows % 32 == 0

    def body(src_ref, out_ref, send_sems, recv_sems, local_sem):
        x, y, c = lax.axis_index("x"), lax.axis_index("y"), lax.axis_index("c")
        me, sibling = (x, y, c), (x, y, 1 - c)
        x_chip, y_chip, across = (1 - x, y), (x, 1 - y), (1 - x, 1 - y)

        def slot(block, half=None):
            ref = out_ref.at[4 * block[0] + 2 * block[1] + block[2]]
            if half is None:
                return ref
            return ref.at[pl.ds(half * (rows // 2), rows // 2)] if by_rows else \
                ref.at[:, pl.ds(half * (cols // 2), cols // 2)]

        def copy(k, block, to, src=None, half=None):
            return pltpu.make_async_remote_copy(
                src_ref=slot(block, half) if src is None else src, dst_ref=slot(block, half),
                send_sem=send_sems.at[k], recv_sem=recv_sems.at[k], device_id=to,
                device_id_type=pl.DeviceIdType.MESH)

        mine = pltpu.make_async_copy(src_ref, slot(me), local_sem)
        mine.start()
        sent = [copy(0, me, sibling, src=src_ref), copy(1, me, (*x_chip, c), src=src_ref),
                copy(2, me, (*y_chip, c), src=src_ref)]
        for cp in sent:
            cp.start()

        def arrived(k, block, half=None):
            copy(k, block, me, half=half).wait_recv()
            onward = {1: [copy(3, block, (*y_chip, c), half=0), copy(5, block, sibling)],
                      2: [copy(4, block, (*x_chip, c), half=1), copy(6, block, sibling)],
                      3: [copy(7, block, sibling, half=0)],
                      4: [copy(8, block, sibling, half=1)]}.get(k, [])
            for cp in onward:
                cp.start()
            sent.extend(onward)

        arrived(1, (*x_chip, c))
        arrived(2, (*y_chip, c))
        arrived(3, (*across, c), half=0)
        arrived(4, (*across, c), half=1)
        arrived(0, sibling)
        arrived(5, (*x_chip, 1 - c))
        arrived(6, (*y_chip, 1 - c))
        arrived(7, (*across, 1 - c), half=0)
        arrived(8, (*across, 1 - c), half=1)
        for cp in sent:
            cp.wait_send()
        mine.wait()

    n_sems = 9
    return pl.pallas_call(
        body, name=name,
        in_specs=[pl.BlockSpec(memory_space=pl.ANY)],
        out_specs=pl.BlockSpec(memory_space=pl.ANY),
        out_shape=jax.ShapeDtypeStruct((N_DEV,) + shard.shape, shard.dtype),
        scratch_shapes=[pltpu.SemaphoreType.DMA((n_sems,)), pltpu.SemaphoreType.DMA((n_sems,)),
                        pltpu.SemaphoreType.DMA],
    )(shard)


def _call(body, send, *, name, grid, in_specs, out_specs, out_shape, scratch_shapes, semantics, operands):
    if send is None:
        return pl.pallas_call(
            body, name=name, grid=grid, in_specs=in_specs, out_specs=out_specs, out_shape=out_shape,
            scratch_shapes=scratch_shapes, compiler_params=_params(semantics))(*operands), []
    srcs, kinds = [s for s, _ in send], [k for _, k in send]
    n, n_in, n_out, n_scr = len(srcs), len(in_specs), len(out_specs), len(scratch_shapes)
    steps = math.prod(grid)
    relay_step = (13 * steps) // 16

    def plan(refs):
        src_refs, land_refs = refs[n_in:n_in + n], refs[n_in + n + n_out:n_in + 2 * n + n_out]
        send_sems, recv_sems, local_sems = refs[-3:]
        x, y, c = lax.axis_index("x"), lax.axis_index("y"), lax.axis_index("c")
        me, peers = _peers()
        first, relayed_in, relayed_out, arrivals, sends, own = [], [], [], [], [], []
        for a, kind in enumerate(kinds):
            def copy(k, src, dst_slot, to):
                return pltpu.make_async_remote_copy(
                    src_ref=src, dst_ref=land_refs[a].at[dst_slot], send_sem=send_sems.at[a * (N_DEV - 1) + k],
                    recv_sem=recv_sems.at[a * (N_DEV - 1) + k], device_id=to, device_id_type=pl.DeviceIdType.MESH)

            if kind == "gather_by_chip":
                idx = lambda px, py, pc: 4 * px + 2 * py + pc
                chips = [(1 - x, y), (x, 1 - y), (1 - x, 1 - y)]
                mine = [copy(0, src_refs[a], me, (x, y, 1 - c))]
                arrivals.append(copy(0, src_refs[a], idx(x, y, 1 - c), (x, y, 1 - c)))
                for j, (px, py) in enumerate(chips):
                    mine.append(copy(1 + j, src_refs[a], me, (px, py, c)))
                    relayed_in.append(copy(1 + j, src_refs[a], idx(px, py, c), (px, py, c)))
                    relayed_out.append(copy(4 + j, land_refs[a].at[idx(px, py, c)], idx(px, py, c), (x, y, 1 - c)))
                    arrivals.append(copy(4 + j, src_refs[a], idx(px, py, 1 - c), (x, y, 1 - c)))
                first += mine
                sends += mine + relayed_out[-3:]
                own.append(pltpu.make_async_copy(src_refs[a], land_refs[a].at[me], local_sems.at[a]))
            elif kind == "scatter_by_chip":
                for k, (px, py) in enumerate([(1 - x, y), (x, 1 - y), (1 - x, 1 - y)]):
                    first.append(copy(k, src_refs[a].at[2 * px + py], 2 * x + y, (px, py, c)))
                    arrivals.append(copy(k, src_refs[a].at[2 * px + py], 2 * px + py, (px, py, c)))
                sends += first[-3:]
                own.append(pltpu.make_async_copy(src_refs[a].at[2 * x + y], land_refs[a].at[2 * x + y],
                                                 local_sems.at[a]))
            else:
                part = (lambda i: src_refs[a].at[i]) if kind == "scatter" else (lambda i: src_refs[a])
                for k, (peer, peer_idx) in enumerate(peers):
                    first.append(copy(k, part(peer_idx), me, peer))
                    arrivals.append(copy(k, part(peer_idx), peer_idx, peer))
                sends += first[-(N_DEV - 1):]
                own.append(pltpu.make_async_copy(part(me), land_refs[a].at[me], local_sems.at[a]))
        return first, relayed_in, relayed_out, arrivals, sends, own

    def wrapped(*refs):
        step = 0
        for axis, size in enumerate(grid):
            step = step * size + pl.program_id(axis)

        @pl.when(step == 0)
        def _():
            first, _, _, _, _, own = plan(refs)
            for cp in first + own:
                cp.start()

        if "gather_by_chip" in kinds:
            @pl.when(step == relay_step)
            def _():
                _, relayed_in, relayed_out, _, _, _ = plan(refs)
                for cp_in, cp_out in zip(relayed_in, relayed_out):
                    cp_in.wait_recv()
                    cp_out.start()

        body(*refs[:n_in], *refs[n_in + n:n_in + n + n_out], *refs[n_in + 2 * n + n_out:n_in + 2 * n + n_out + n_scr])

        @pl.when(step == steps - 1)
        def _():
            _, _, _, arrivals, sends, own = plan(refs)
            for cp in arrivals:
                cp.wait_recv()
            for cp in sends:
                cp.wait_send()
            for cp in own:
                cp.wait()

    anywhere = pl.BlockSpec(memory_space=pl.ANY)
    lands = [jax.ShapeDtypeStruct((N_DEV // 2 if k == "scatter_by_chip" else N_DEV,) + s.shape[-2:], s.dtype)
             for s, k in send]
    out = pl.pallas_call(
        wrapped, name=name, grid=grid,
        in_specs=list(in_specs) + [anywhere] * n, out_specs=list(out_specs) + [anywhere] * n,
        out_shape=list(out_shape) + lands,
        scratch_shapes=list(scratch_shapes) + [pltpu.SemaphoreType.DMA((n * (N_DEV - 1),)),
                                               pltpu.SemaphoreType.DMA((n * (N_DEV - 1),)),
                                               pltpu.SemaphoreType.DMA((n,))],
        compiler_params=_params(("arbitrary",) * len(grid)),
    )(*operands, *srcs)
    return out[:n_out], list(out[n_out:])


def _pair_swap(blocks):
    def body(src_ref, out_ref, send_sems, recv_sems):
        x, y, c = lax.axis_index("x"), lax.axis_index("y"), lax.axis_index("c")
        copies = [pltpu.make_async_remote_copy(
            src_ref=src_ref.at[2 * chip + (1 - c)], dst_ref=out_ref.at[chip], send_sem=send_sems.at[chip],
            recv_sem=recv_sems.at[chip], device_id=(x, y, 1 - c), device_id_type=pl.DeviceIdType.MESH)
            for chip in range(N_DEV // 2)]
        for cp in copies:
            cp.start()
        for cp in copies:
            cp.wait()

    return pl.pallas_call(
        body, name="pair_swap_grad_w_in",
        in_specs=[pl.BlockSpec(memory_space=pl.ANY)], out_specs=pl.BlockSpec(memory_space=pl.ANY),
        out_shape=jax.ShapeDtypeStruct((N_DEV // 2,) + blocks.shape[1:], blocks.dtype),
        scratch_shapes=[pltpu.SemaphoreType.DMA((N_DEV // 2,)), pltpu.SemaphoreType.DMA((N_DEV // 2,))],
    )(blocks)


def _pair_sum(blocks, swapped, core):
    _, rows, cols = swapped.shape
    tile_rows = _row_tile(rows)

    def body(core_ref, mine_ref, theirs_ref, o_ref):
        o_ref[...] = (mine_ref[...].astype(F32) + theirs_ref[...].astype(F32)).astype(o_ref.dtype)

    return pl.pallas_call(
        body, name="pair_sum_grad_w_in",
        grid_spec=pltpu.PrefetchScalarGridSpec(
            num_scalar_prefetch=1, grid=(N_DEV // 2, rows // tile_rows),
            in_specs=[pl.BlockSpec((None, tile_rows, cols), lambda j, i, core_ref: (2 * j + core_ref[0], i, 0)),
                      pl.BlockSpec((None, tile_rows, cols), lambda j, i, core_ref: (j, i, 0))],
            out_specs=pl.BlockSpec((None, tile_rows, cols), lambda j, i, core_ref: (j, i, 0))),
        out_shape=jax.ShapeDtypeStruct(swapped.shape, swapped.dtype),
        compiler_params=_params(("parallel", "parallel")),
    )(core, blocks, swapped)


def _sum_in_device_order(land_ref):
    acc = land_ref[0].astype(F32)
    for j in range(1, land_ref.shape[0]):
        acc = acc + land_ref[j].astype(F32)
    return acc


def _adam_math(w, g, m, v):
    c1 = 1.0 - ADAM_B1 ** ADAM_STEP
    c2 = 1.0 - ADAM_B2 ** ADAM_STEP
    m_new = ADAM_B1 * m + (1.0 - ADAM_B1) * g
    v_new = ADAM_B2 * v + (1.0 - ADAM_B2) * (g * g)
    delta = -ADAM_LR * ((m_new / c1) / (jnp.sqrt(v_new / c2) + ADAM_EPS) + ADAM_WD * w)
    return delta, m_new, v_new


def _row_tile(rows):
    return max(t for t in range(16, 385, 16) if rows % t == 0) if rows % 16 == 0 else rows


def _sum_update(land, w, m, v, name):
    slots, rows, cols = land.shape
    tile_rows = _row_tile(rows)

    def body(land_ref, w_ref, m_ref, v_ref, g_ref, d_ref, nm_ref, nv_ref):
        g = _sum_in_device_order(land_ref)
        g_ref[...] = g
        d_ref[...], nm_ref[...], nv_ref[...] = _adam_math(w_ref[...], g, m_ref[...], v_ref[...])

    tile = pl.BlockSpec((None, tile_rows, cols), lambda i: (0, i, 0))
    out = jax.ShapeDtypeStruct((1, rows, cols), F32)
    return pl.pallas_call(
        body, name=name, grid=(rows // tile_rows,),
        in_specs=[pl.BlockSpec((slots, tile_rows, cols), lambda i: (0, i, 0)), tile, tile, tile],
        out_specs=[tile] * 4, out_shape=[out] * 4,
        compiler_params=_params(("parallel",)),
    )(land, w, m, v)


def _to_bf16(arrays):
    def body(*refs):
        for src_ref, dst_ref in zip(refs[:len(arrays)], refs[len(arrays):]):
            dst_ref[...] = src_ref[...].astype(BF16)

    whole = [pl.BlockSpec(a.shape, lambda i: (0, 0)) for a in arrays]
    return pl.pallas_call(
        body, name="shards_to_bf16", grid=(1,), in_specs=whole, out_specs=whole,
        out_shape=[jax.ShapeDtypeStruct(a.shape, BF16) for a in arrays],
        compiler_params=_params(("arbitrary",)),
    )(*arrays)


N_GAINS = 3


def _sum_update_gains(land, ws, ms, vs):
    def body(land_ref, *refs):
        w_refs, m_refs, v_refs = (refs[k * N_GAINS:(k + 1) * N_GAINS] for k in range(3))
        loss_ref, out_refs = refs[3 * N_GAINS], refs[3 * N_GAINS + 1:]
        rows = _sum_in_device_order(land_ref)
        loss_ref[...] = rows[N_GAINS:N_GAINS + 1]
        for k in range(N_GAINS):
            g = rows[k:k + 1]
            g_ref, d_ref, nm_ref, nv_ref = out_refs[4 * k:4 * k + 4]
            g_ref[...] = g
            d_ref[...], nm_ref[...], nv_ref[...] = _adam_math(w_refs[k][...], g, m_refs[k][...], v_refs[k][...])

    row = pl.BlockSpec((1, D_MODEL), lambda i: (0, 0))
    out = jax.ShapeDtypeStruct((1, D_MODEL), F32)
    n_out = 1 + 4 * N_GAINS
    return pl.pallas_call(
        body, name="update_gains", grid=(1,),
        in_specs=[pl.BlockSpec(land.shape, lambda i: (0, 0, 0))] + [row] * (3 * N_GAINS),
        out_specs=[row] * n_out, out_shape=[out] * n_out,
    )(land, *ws, *ms, *vs)


GROUP_FFN = ("w_ffn_in", "w_ffn_out")
GROUP_MIX = ("w_sb_up", "w_dil_up", "w_out")
COL_SHARDED = ("w_in", "w_sb_up", "w_dil_up", "w_ffn_in")
TRANSPOSED = ("w_in", "w_ffn_in")


def _full_from_shards(name, slots):
    _, r, c = slots.shape
    if name in TRANSPOSED:
        return slots.reshape(N_DEV * r, c).T
    if name in COL_SHARDED:
        return slots.transpose(1, 0, 2).reshape(r, N_DEV * c)
    return slots.reshape(N_DEV * r, c)


def _row_shards(full):
    rows, cols = full.shape
    return full.reshape(N_DEV, rows // N_DEV, cols)


def _local_step(x, target, g_mix, g_ffn, g_fin, w_in_t, shards=None, rest=None):
    gather = lambda names, kind: None if shards is None else [(shards[n], kind) for n in names]
    scatter = lambda blocks: None if shards is None else [(t, "scatter") for t in blocks]
    landed = lambda blocks, lands: lands if lands else blocks

    w = {"w_in": w_in_t}
    if shards is None:
        w.update(rest)
        w["w_ffn_in"] = rest["w_ffn_in"].T
    (qkv_sb, qkv_dl, gates, u), _ = _norm_proj(x, g_mix, w["w_in"])
    qkv_sb = qkv_sb.reshape(B_LOC, SEQ, 3 * SB_WIDTH)
    qkv_dl = qkv_dl.reshape(B_LOC, SEQ, 3 * DIL_WIDTH)
    (o_sb,), lands = _sb_fwd(qkv_sb, gather(GROUP_FFN, "gather_by_chip"))
    if lands:
        w["w_ffn_in"], w["w_ffn_out"] = lands[0].reshape(2 * D_FF, D_MODEL), _full_from_shards("w_ffn_out", lands[1])
    o_sb = o_sb.reshape(TOK, SB_WIDTH)
    (o_dl, lse), lands = _dil_fwd(qkv_dl, gather(GROUP_MIX, "gather"))
    w.update({n: _full_from_shards(n, t) for n, t in zip(GROUP_MIX, lands)})
    o_dl = o_dl.reshape(TOK, DIL_OUT)

    loss, dx1, merged, u2, act, dh, dx2, dg_fin, dg_ffn = _mix_ffn_fwd_bwd(
        x, o_sb, o_dl, gates, w["w_sb_up"], w["w_dil_up"], w["w_out"], target, g_ffn, g_fin,
        w["w_ffn_in"], w["w_ffn_out"])
    dgates, dy_sb, dy_dl, do_sb, do_dl, dsum = _mix_bwd(dx1, o_sb, o_dl, gates, w["w_sb_up"], w["w_dil_up"], w["w_out"])
    blocks = {
        "w_sb_up": _atb(o_sb, dy_sb, "grad_w_sb_up", SB_WIDTH, D_MODEL, col_blocks=N_DEV),
        "w_dil_up": _atb(o_dl, dy_dl, "grad_w_dil_up", DIL_OUT, D_MODEL, col_blocks=N_DEV),
        "w_out": _row_shards(_atb(merged, dx1, "grad_w_out", D_MODEL, D_MODEL)),
        "w_ffn_in": _row_shards(_atb_cols(dh, u2, "grad_w_ffn_in", 512)),
        "w_ffn_out": _row_shards(_atb_cols(act, dx2, "grad_w_ffn_out", 256)),
    }
    grads = {}

    early, late = GROUP_FFN, GROUP_MIX
    early_blocks = [blocks[n] for n in early]
    (dq_sb, dk_sb, dv_sb), lands = _sb_bwd(qkv_sb, do_sb.reshape(B_LOC, SEQ, SB_WIDTH), scatter(early_blocks))
    grads.update(zip(early, landed(early_blocks, lands)))
    as_batch = lambda t: t.reshape(B_LOC, SEQ, DIL_OUT)
    late_blocks = [blocks[n] for n in late]
    d_dl, lands = _dil_bwd(qkv_dl, as_batch(do_dl), lse, as_batch(dsum), scatter(late_blocks))
    grads.update(zip(late, landed(late_blocks, lands)))
    flat = lambda t: t.reshape(TOK, -1)
    dproj = ([flat(dq_sb), flat(dk_sb), flat(dv_sb)]
             + [flat(d_dl[3 * grp + part]) for part in range(3) for grp in range(DIL_GROUPS)] + [dgates])

    w_in_blocks = _row_shards(_atb_pieces(u, dproj, "grad_w_in", D_MODEL // 2))
    if shards is None:
        grads["w_in"] = w_in_blocks
        send = None
    else:
        core = lax.axis_index("c").astype(jnp.int32).reshape(1)
        send = [(_pair_sum(w_in_blocks, _pair_swap(w_in_blocks), core), "scatter_by_chip")]
    (grad_x, dg_mix), lands = _proj_bwd(dproj, dx1, x, g_mix, w["w_in"], send)
    if lands:
        grads["w_in"] = lands[0]
    gain_grads = jnp.concatenate([dg_mix, dg_ffn, dg_fin], axis=0)
    return loss, grad_x, gain_grads, grads


def kernel(x, norm_mix_g, w_in, w_sb_up, w_dil_up, w_out, norm_ffn_g, w_ffn_in, w_ffn_out, norm_final_g, loss_target, m_norm_mix_g, m_w_in, m_w_sb_up, m_w_dil_up, m_w_out, m_norm_ffn_g, m_w_ffn_in, m_w_ffn_out, m_norm_final_g, v_norm_mix_g, v_w_in, v_w_sb_up, v_w_dil_up, v_w_out, v_norm_ffn_g, v_w_ffn_in, v_w_ffn_out, v_norm_final_g):
    mats = {"w_in": w_in, "w_sb_up": w_sb_up, "w_dil_up": w_dil_up, "w_out": w_out,
            "w_ffn_in": w_ffn_in, "w_ffn_out": w_ffn_out}
    moments_m = {"w_in": m_w_in, "w_sb_up": m_w_sb_up, "w_dil_up": m_w_dil_up, "w_out": m_w_out,
                 "w_ffn_in": m_w_ffn_in, "w_ffn_out": m_w_ffn_out}
    moments_v = {"w_in": v_w_in, "w_sb_up": v_w_sb_up, "w_dil_up": v_w_dil_up, "w_out": v_w_out,
                 "w_ffn_in": v_w_ffn_in, "w_ffn_out": v_w_ffn_out}
    stored = lambda t, name: t.transpose(0, 2, 1) if name in TRANSPOSED else t
    mats = {name: stored(t, name) for name, t in mats.items()}
    shards = dict(zip(mats, _to_bf16([t[0] for t in mats.values()])))
    gathered_w_in = _all_gather(shards.pop("w_in"), "all_gather_w_in")
    g_fin = norm_final_g.reshape(1, D_MODEL)
    loss, grad_x, gain_grads, grad_slots = _local_step(
        x.reshape(TOK, D_MODEL), loss_target.reshape(TOK, D_MODEL), norm_mix_g, norm_ffn_g, g_fin,
        gathered_w_in.reshape(IN_WIDTH, D_MODEL), shards=shards)

    gain_rows = jnp.concatenate([gain_grads, jnp.tile(loss, (1, D_MODEL // LANES)),
                                 jnp.zeros((8 - 4, D_MODEL), F32)], axis=0)
    row = lambda t: t.reshape(1, D_MODEL)
    loss_row, *gain_outs = _sum_update_gains(
        _all_gather(gain_rows, "all_gather_gains"),
        [norm_mix_g, norm_ffn_g, g_fin], [m_norm_mix_g, m_norm_ffn_g, row(m_norm_final_g)],
        [v_norm_mix_g, v_norm_ffn_g, row(v_norm_final_g)])

    out_g, out_d, out_m, out_v = {}, {}, {}, {}
    for name, slots in grad_slots.items():
        out_g[name], out_d[name], out_m[name], out_v[name] = [stored(t, name) for t in _sum_update(
            slots, mats[name], stored(moments_m[name], name), stored(moments_v[name], name), "update_" + name)]
    for idx, name in enumerate(("norm_mix_g", "norm_ffn_g", "norm_final_g")):
        shape = (D_MODEL,) if name == "norm_final_g" else (1, D_MODEL)
        out_g[name], out_d[name], out_m[name], out_v[name] = [t.reshape(shape) for t in gain_outs[4 * idx:4 * idx + 4]]

    order = ("norm_mix_g", "w_in", "w_sb_up", "w_dil_up", "w_out", "norm_ffn_g", "w_ffn_in", "w_ffn_out",
             "norm_final_g")
    return (loss_row[0, 0], grad_x.reshape(B_LOC, SEQ, D_MODEL),
            *[out_g[n] for n in order], *[out_d[n] for n in order],
            *[out_m[n] for n in order], *[out_v[n] for n in order])
```

```python
import math

import jax
import jax.numpy as jnp
from jax import lax
from jax.experimental import pallas as pl
from jax.experimental.pallas import tpu as pltpu

F32 = jnp.float32
BF16 = jnp.bfloat16

N_DEV = 8
D_MODEL = 1024
SEQ = 2048
B_LOC = 2
TOK = B_LOC * SEQ
HEAD_DIM = 64
SB_WIDTH = 512
DIL_WIDTH = 768
DIL_OUT = 256
QKV_WIDTH = 3 * SB_WIDTH + 3 * DIL_WIDTH
IN_WIDTH = QKV_WIDTH + 2 * D_MODEL
D_FF = 2816
DIL_PAIRS = ((128, 1), (512, 4), (2048, 16))
DIL_HEADS = 12
RMS_EPS = 1e-6
ALIBI_MAX_BIAS = 8.0
QK_SCALE = 1.0 / math.sqrt(HEAD_DIM)
BLK = 128
LANES = 128
NEG_BIG = -1e30

ADAM_LR = 0.001
ADAM_B1 = 0.9
ADAM_B2 = 0.999
ADAM_EPS = 1e-08
ADAM_WD = 0.01
ADAM_STEP = 10

VMEM_LIMIT = 58 * 1024 * 1024


def _dot(a, b):
    return jnp.dot(a, b, preferred_element_type=F32)


def _dot_nt(a, b):
    return lax.dot_general(a, b, (((1,), (1,)), ((), ())), preferred_element_type=F32)


def _dot_tn(a, b):
    return lax.dot_general(a, b, (((0,), (0,)), ((), ())), preferred_element_type=F32)


def _sigmoid(z):
    return 1.0 / (1.0 + jnp.exp(-z))


def _split_bf16(v):
    hi = v.astype(BF16)
    lo = (v - hi.astype(F32)).astype(BF16)
    return hi, lo


def _chunks(width, step=512):
    out, c = [], 0
    while c < width:
        w = min(step, width - c)
        out.append((c, w))
        c += w
    return out


def _resident(shape):
    nd = len(shape)
    return pl.BlockSpec(shape, lambda *_: (0,) * nd, pipeline_mode=pl.Buffered(1))


def _params(sem):
    return pltpu.CompilerParams(dimension_semantics=sem, vmem_limit_bytes=VMEM_LIMIT)


def _rms_fwd(x, g):
    r = lax.rsqrt(jnp.mean(x * x, axis=-1, keepdims=True) + RMS_EPS)
    n = x * r
    return n, r, n * g


def _rms_bwd(dy, n, r, g):
    dg = jnp.sum(dy * n, axis=0, keepdims=True)
    dn = dy * g
    dx = r * (dn - n * jnp.mean(dn * n, axis=-1, keepdims=True))
    return dx, dg


TM = 256
TM_WIDE = 512


def _norm_proj(x, g, w_in_t, send=None):
    def body(x_ref, g_ref, w_ref, sb_ref, dl_ref, gate_ref, u_ref):
        _, _, u = _rms_fwd(x_ref[...], g_ref[...])
        u = u.astype(BF16)
        u_ref[...] = u
        for c0, w in _chunks(3 * SB_WIDTH):
            sb_ref[:, c0:c0 + w] = _dot_nt(u, w_ref[c0:c0 + w, :]).astype(BF16)
        for c0, w in _chunks(3 * DIL_WIDTH):
            dl_ref[:, c0:c0 + w] = _dot_nt(u, w_ref[3 * SB_WIDTH + c0:3 * SB_WIDTH + c0 + w, :])
        for c0, w in _chunks(2 * D_MODEL):
            gate_ref[:, c0:c0 + w] = _dot_nt(u, w_ref[QKV_WIDTH + c0:QKV_WIDTH + c0 + w, :])

    return _call(
        body, send, name="norm_proj", grid=(TOK // TM_WIDE,),
        in_specs=[pl.BlockSpec((TM_WIDE, D_MODEL), lambda i: (i, 0)), _resident((1, D_MODEL)),
                  _resident((IN_WIDTH, D_MODEL))],
        out_specs=[pl.BlockSpec((TM_WIDE, 3 * SB_WIDTH), lambda i: (i, 0)),
                   pl.BlockSpec((TM_WIDE, 3 * DIL_WIDTH), lambda i: (i, 0)),
                   pl.BlockSpec((TM_WIDE, 2 * D_MODEL), lambda i: (i, 0)),
                   pl.BlockSpec((TM_WIDE, D_MODEL), lambda i: (i, 0))],
        out_shape=[jax.ShapeDtypeStruct((TOK, 3 * SB_WIDTH), BF16),
                   jax.ShapeDtypeStruct((TOK, 3 * DIL_WIDTH), F32),
                   jax.ShapeDtypeStruct((TOK, 2 * D_MODEL), F32),
                   jax.ShapeDtypeStruct((TOK, D_MODEL), BF16)],
        scratch_shapes=[], semantics=("parallel",), operands=(x, g, w_in_t))


FF_CHUNK = 1024


def _mix_ffn_fwd_bwd(x, o_sb, o_dl, gates, w_sb_up, w_dil_up, w_out, target, g_ffn, g_fin, w_ffn_in_t, w_ffn_out):
    def body(x_ref, osb_ref, odl_ref, gate_ref, wsb_ref, wdl_ref, wo_ref, t_ref, gffn_ref, gfin_ref, win_ref, wout_ref,
             loss_ref, dx1_ref, mg_ref, u2_ref, act_ref, dh_ref, dx2_ref, dgfin_ref, dgffn_ref, h_scr):
        i = pl.program_id(0)

        @pl.when(i == 0)
        def _():
            loss_ref[...] = jnp.zeros_like(loss_ref)
            dgfin_ref[...] = jnp.zeros_like(dgfin_ref)
            dgffn_ref[...] = jnp.zeros_like(dgffn_ref)

        y_sb = _dot(osb_ref[...], wsb_ref[...])
        y_dl = _dot(odl_ref[...].astype(BF16), wdl_ref[...])
        merged = (_sigmoid(gate_ref[:, :D_MODEL]) * y_sb
                  + _sigmoid(gate_ref[:, D_MODEL:]) * y_dl).astype(BF16)
        mg_ref[...] = merged
        x1 = x_ref[...] + _dot(merged, wo_ref[...])
        g_ffn_v = gffn_ref[...]
        g_fin_v = gfin_ref[...]
        n2, r2, u2 = _rms_fwd(x1, g_ffn_v)
        u2 = u2.astype(BF16)
        u2_ref[...] = u2
        x2 = x1
        for c0, w in _chunks(D_FF, FF_CHUNK):
            gate = _dot_nt(u2, win_ref[c0:c0 + w, :])
            up = _dot_nt(u2, win_ref[D_FF + c0:D_FF + c0 + w, :])
            h_scr[:, c0:c0 + w] = gate
            h_scr[:, D_FF + c0:D_FF + c0 + w] = up
            act = (gate * _sigmoid(gate) * up).astype(BF16)
            act_ref[:, c0:c0 + w] = act
            x2 = x2 + _dot(act, wout_ref[c0:c0 + w, :])
        n3, r3, y = _rms_fwd(x2, g_fin_v)
        err = y - t_ref[...]
        sq = jnp.sum(jnp.sum(err * err, axis=1, keepdims=True), axis=0, keepdims=True)
        loss_ref[...] += sq * (0.5 / D_MODEL)
        dx2, dgfin = _rms_bwd(err * (1.0 / D_MODEL), n3, r3, g_fin_v)
        dgfin_ref[...] += dgfin
        dx2_b = dx2.astype(BF16)
        dx2_ref[...] = dx2_b
        du2 = jnp.zeros((TM, D_MODEL), F32)
        for c0, w in _chunks(D_FF, FF_CHUNK):
            gate = h_scr[:, c0:c0 + w]
            up = h_scr[:, D_FF + c0:D_FF + c0 + w]
            dact = _dot_nt(dx2_b, wout_ref[c0:c0 + w, :])
            sg = _sigmoid(gate)
            dgate = (dact * up * (sg * (1.0 + gate * (1.0 - sg)))).astype(BF16)
            dup = (dact * (gate * sg)).astype(BF16)
            dh_ref[:, c0:c0 + w] = dgate
            dh_ref[:, D_FF + c0:D_FF + c0 + w] = dup
            du2 = du2 + _dot(dgate, win_ref[c0:c0 + w, :])
            du2 = du2 + _dot(dup, win_ref[D_FF + c0:D_FF + c0 + w, :])
        dx1_n, dgffn = _rms_bwd(du2, n2, r2, g_ffn_v)
        dgffn_ref[...] += dgffn
        dx1_ref[...] = dx2 + dx1_n

    tile = lambda w: pl.BlockSpec((TM, w), lambda i: (i, 0))
    acc = lambda w: pl.BlockSpec((1, w), lambda i: (0, 0))
    return pl.pallas_call(
        body, name="mix_ffn_fwd_bwd", grid=(TOK // TM,),
        in_specs=[tile(D_MODEL), tile(SB_WIDTH), tile(DIL_OUT), tile(2 * D_MODEL),
                  _resident((SB_WIDTH, D_MODEL)), _resident((DIL_OUT, D_MODEL)), _resident((D_MODEL, D_MODEL)),
                  tile(D_MODEL), _resident((1, D_MODEL)), _resident((1, D_MODEL)),
                  _resident((2 * D_FF, D_MODEL)), _resident((D_FF, D_MODEL))],
        out_specs=[acc(LANES), tile(D_MODEL), tile(D_MODEL), tile(D_MODEL), tile(D_FF), tile(2 * D_FF),
                   tile(D_MODEL), acc(D_MODEL), acc(D_MODEL)],
        out_shape=[jax.ShapeDtypeStruct((1, LANES), F32),
                   jax.ShapeDtypeStruct((TOK, D_MODEL), F32),
                   jax.ShapeDtypeStruct((TOK, D_MODEL), BF16),
                   jax.ShapeDtypeStruct((TOK, D_MODEL), BF16),
                   jax.ShapeDtypeStruct((TOK, D_FF), BF16),
                   jax.ShapeDtypeStruct((TOK, 2 * D_FF), BF16),
                   jax.ShapeDtypeStruct((TOK, D_MODEL), BF16),
                   jax.ShapeDtypeStruct((1, D_MODEL), F32),
                   jax.ShapeDtypeStruct((1, D_MODEL), F32)],
        scratch_shapes=[pltpu.VMEM((TM, 2 * D_FF), F32)],
        compiler_params=_params(("arbitrary",)),
    )(x, o_sb, o_dl, gates, w_sb_up, w_dil_up, w_out, target, g_ffn, g_fin, w_ffn_in_t, w_ffn_out)


def _mix_bwd(dx1, o_sb, o_dl, gates, w_sb_up, w_dil_up, w_out):
    def body(dx1_ref, osb_ref, odl_ref, gate_ref, wsb_ref, wdl_ref, wout_ref,
             dgate_ref, dysb_ref, dydl_ref, dosb_ref, dodl_ref, dsum_ref):
        dmerged = _dot_nt(dx1_ref[...].astype(BF16), wout_ref[...])
        o_dl = odl_ref[...]
        y_sb = _dot(osb_ref[...], wsb_ref[...])
        y_dl = _dot(o_dl.astype(BF16), wdl_ref[...])
        s_sb = _sigmoid(gate_ref[:, :D_MODEL])
        s_dl = _sigmoid(gate_ref[:, D_MODEL:])
        dgate_ref[:, :D_MODEL] = (dmerged * y_sb * (s_sb * (1.0 - s_sb))).astype(BF16)
        dgate_ref[:, D_MODEL:] = (dmerged * y_dl * (s_dl * (1.0 - s_dl))).astype(BF16)
        dy_sb = (dmerged * s_sb).astype(BF16)
        dy_dl = (dmerged * s_dl).astype(BF16)
        dysb_ref[...] = dy_sb
        dydl_ref[...] = dy_dl
        dosb_ref[...] = _dot_nt(dy_sb, wsb_ref[...]).astype(BF16)
        do_dl = _dot_nt(dy_dl, wdl_ref[...])
        dodl_ref[...] = do_dl
        row = lax.broadcasted_iota(jnp.int32, (DIL_OUT, DIL_OUT), 0) // HEAD_DIM
        col = lax.broadcasted_iota(jnp.int32, (DIL_OUT, DIL_OUT), 1) // HEAD_DIM
        same_head = (row == col).astype(BF16)
        hi, lo = _split_bf16(do_dl * o_dl)
        dsum_ref[...] = _dot(hi, same_head) + _dot(lo, same_head)

    tile = lambda w: pl.BlockSpec((TM_WIDE, w), lambda i: (i, 0))
    return pl.pallas_call(
        body, name="mix_bwd", grid=(TOK // TM_WIDE,),
        in_specs=[tile(D_MODEL), tile(SB_WIDTH), tile(DIL_OUT), tile(2 * D_MODEL),
                  _resident((SB_WIDTH, D_MODEL)), _resident((DIL_OUT, D_MODEL)),
                  _resident((D_MODEL, D_MODEL))],
        out_specs=[tile(2 * D_MODEL), tile(D_MODEL), tile(D_MODEL), tile(SB_WIDTH), tile(DIL_OUT),
                   tile(DIL_OUT)],
        out_shape=[jax.ShapeDtypeStruct((TOK, 2 * D_MODEL), BF16),
                   jax.ShapeDtypeStruct((TOK, D_MODEL), BF16),
                   jax.ShapeDtypeStruct((TOK, D_MODEL), BF16),
                   jax.ShapeDtypeStruct((TOK, SB_WIDTH), BF16),
                   jax.ShapeDtypeStruct((TOK, DIL_OUT), F32),
                   jax.ShapeDtypeStruct((TOK, DIL_OUT), F32)],
        compiler_params=_params(("parallel",)),
    )(dx1, o_sb, o_dl, gates, w_sb_up, w_dil_up, w_out)


def _proj_bwd(dproj, dx1, x, g, w_in_t, send=None):
    widths = [p.shape[1] for p in dproj]

    def body(*refs):
        dx1_ref, x_ref, g_ref, w_ref, dx_ref, dg_ref = refs[len(widths):]

        @pl.when(pl.program_id(0) == 0)
        def _():
            dg_ref[...] = jnp.zeros_like(dg_ref)

        du = jnp.zeros((TM, D_MODEL), F32)
        c0 = 0
        for dp_ref, w in zip(refs, widths):
            du = du + _dot(dp_ref[...].astype(BF16), w_ref[c0:c0 + w, :])
            c0 += w
        g_v = g_ref[...]
        n, r, _ = _rms_fwd(x_ref[...], g_v)
        dx, dg = _rms_bwd(du, n, r, g_v)
        dg_ref[...] += dg
        dx_ref[...] = dx1_ref[...] + dx

    tile = lambda w: pl.BlockSpec((TM, w), lambda i: (i, 0))
    return _call(
        body, send, name="proj_bwd", grid=(TOK // TM,),
        in_specs=[tile(w) for w in widths] + [tile(D_MODEL), tile(D_MODEL), _resident((1, D_MODEL)),
                                              _resident((IN_WIDTH, D_MODEL))],
        out_specs=[tile(D_MODEL), pl.BlockSpec((1, D_MODEL), lambda i: (0, 0))],
        out_shape=[jax.ShapeDtypeStruct((TOK, D_MODEL), F32),
                   jax.ShapeDtypeStruct((1, D_MODEL), F32)],
        scratch_shapes=[], semantics=("arbitrary",), operands=(*dproj, dx1, x, g, w_in_t))


def _atb_pieces(a, pieces, name, tm, tk=512):
    m = a.shape[1]
    widths = [p.shape[1] for p in pieces]
    n = sum(widths)
    nk = TOK // tk

    def body(a_ref, *refs):
        o_ref, acc_ref = refs[len(widths):]
        k = pl.program_id(1)

        @pl.when(k == 0)
        def _():
            acc_ref[...] = jnp.zeros_like(acc_ref)

        a_v = a_ref[...]
        c0 = 0
        for p_ref, w in zip(refs, widths):
            acc_ref[:, c0:c0 + w] += _dot_tn(a_v, p_ref[...].astype(BF16))
            c0 += w

        @pl.when(k == nk - 1)
        def _():
            for c0, w in _chunks(n):
                o_ref[c0:c0 + w, :] = acc_ref[:, c0:c0 + w].T.astype(BF16)

    return pl.pallas_call(
        body, name=name, grid=(m // tm, nk),
        in_specs=[pl.BlockSpec((tk, tm), lambda i, k: (k, i))]
                 + [pl.BlockSpec((tk, w), lambda i, k: (k, 0)) for w in widths],
        out_specs=pl.BlockSpec((n, tm), lambda i, k: (0, i)),
        out_shape=jax.ShapeDtypeStruct((n, m), BF16),
        scratch_shapes=[pltpu.VMEM((tm, n), F32)],
        compiler_params=_params(("parallel", "arbitrary")),
    )(a, *pieces)


def _atb_cols(a, b, name, cols, tk=512):
    m, other = a.shape[1], b.shape[1]

    def body(a_ref, b_ref, o_ref):
        acc = jnp.zeros(o_ref.shape, F32)
        for k0 in range(0, TOK, tk):
            acc = acc + _dot_tn(a_ref[k0:k0 + tk, :], b_ref[k0:k0 + tk, :])
        o_ref[...] = acc.astype(BF16)

    assert m % cols == 0
    return pl.pallas_call(
        body, name=name, grid=(m // cols,),
        in_specs=[pl.BlockSpec((TOK, cols), lambda r: (0, r)), _resident((TOK, other))],
        out_specs=pl.BlockSpec((cols, other), lambda r: (r, 0)),
        out_shape=jax.ShapeDtypeStruct((m, other), BF16),
        compiler_params=_params(("parallel",)),
    )(a, b)


def _atb(a, b, name, tm, tn, col_blocks=0, tk=512):
    m, n = a.shape[1], b.shape[1]
    nk = TOK // tk

    def body(a_ref, b_ref, o_ref, acc_ref):
        k = pl.program_id(2)

        @pl.when(k == 0)
        def _():
            acc_ref[...] = jnp.zeros_like(acc_ref)

        acc_ref[...] += _dot_tn(a_ref[...].astype(BF16), b_ref[...].astype(BF16))

        @pl.when(k == nk - 1)
        def _():
            if col_blocks:
                width = n // col_blocks
                for blk in range(col_blocks):
                    o_ref[blk] = acc_ref[:, blk * width:(blk + 1) * width].astype(BF16)
            else:
                o_ref[...] = acc_ref[...].astype(BF16)

    if col_blocks:
        out_spec = pl.BlockSpec((col_blocks, tm, n // col_blocks), lambda i, j, k: (0, i, 0))
        out_shape = jax.ShapeDtypeStruct((col_blocks, m, n // col_blocks), BF16)
    else:
        out_spec = pl.BlockSpec((tm, tn), lambda i, j, k: (i, j))
        out_shape = jax.ShapeDtypeStruct((m, n), BF16)
    return pl.pallas_call(
        body, name=name, grid=(m // tm, n // tn, nk),
        in_specs=[pl.BlockSpec((tk, tm), lambda i, j, k: (k, i)),
                  pl.BlockSpec((tk, tn), lambda i, j, k: (k, j))],
        out_specs=out_spec, out_shape=out_shape,
        scratch_shapes=[pltpu.VMEM((tm, tn), F32)],
        compiler_params=_params(("parallel", "parallel", "arbitrary")),
    )(a, b)


SB_PAIRS = SB_WIDTH // LANES


def _two_heads(v, lane0):
    zero = jnp.zeros_like(v)
    return jnp.where(lane0, v, zero), jnp.where(lane0, zero, v)


SB_QBLK = 256
N_SB_STEPS = SEQ // SB_QBLK


SB_KCHUNK = 2 * BLK
SB_ROWS = 2 * SB_QBLK
SB_DEAD = -104.0


def _log_keep(z):
    neg_z = -z
    return jnp.minimum(neg_z, 0.0) - jnp.log(1.0 + jnp.exp(jnp.minimum(z, neg_z)))


def _stack_heads(v, lane0):
    return jnp.concatenate(_two_heads(v, lane0), axis=0)


def _block_sums(v, tri, split=True):
    halves = (v[:, :BLK], v[:, BLK:])
    stacked = jnp.concatenate(halves, axis=0)
    if split:
        hi, lo = _split_bf16(stacked)
        prod = _dot(jnp.concatenate([hi, lo], axis=0), tri)
        tri_sum = prod[:2 * SB_ROWS] + prod[2 * SB_ROWS:]
    else:
        tri_sum = _dot(stacked.astype(BF16), tri)
    sums = tuple(jnp.sum(h, axis=1, keepdims=True) for h in halves)
    return (tri_sum[:SB_ROWS], tri_sum[SB_ROWS:]), sums


def _sb_diag_mask():
    row = lax.broadcasted_iota(jnp.int32, (SB_ROWS, SB_KCHUNK), 0)
    col = lax.broadcasted_iota(jnp.int32, (SB_ROWS, SB_KCHUNK), 1)
    return col < jnp.where(row >= SB_QBLK, row - SB_QBLK, row)


def _sb_fwd(qkv, send=None):
    def body(q_ref, k_ref, v_ref, o_ref):
        i = pl.program_id(2)
        krow = lax.broadcasted_iota(jnp.int32, (BLK, BLK), 0)
        kcol = lax.broadcasted_iota(jnp.int32, (BLK, BLK), 1)
        later = (krow > kcol).astype(BF16)
        lane0 = lax.broadcasted_iota(jnp.int32, (SB_QBLK, LANES), 1) < HEAD_DIM
        q2 = _stack_heads(q_ref[0] * QK_SCALE, lane0)

        def chunk(c, carry, causal):
            acc, run = carry
            off = pl.multiple_of(c * SB_KCHUNK, SB_KCHUNK)
            z = _dot_nt(q2, k_ref[0, pl.ds(off, SB_KCHUNK), :])
            log_keep = _log_keep(z)
            if causal is not None:
                log_keep = jnp.where(causal, log_keep, 0.0)
            suffix, sums = _block_sums(log_keep, later)
            log_after = jnp.concatenate([suffix[0] + (run + sums[1]), suffix[1] + run], axis=1)
            a = jnp.exp(log_keep + z + log_after)
            if causal is not None:
                a = jnp.where(causal, a, 0.0)
            acc = acc + _dot(a.astype(BF16), v_ref[0, pl.ds(off, SB_KCHUNK), :])
            return acc, run + (sums[0] + sums[1])

        acc, run = chunk(i, (jnp.zeros((SB_ROWS, LANES), F32), jnp.zeros((SB_ROWS, 1), F32)), _sb_diag_mask())

        def some_alive(run):
            return (jnp.max(run) > SB_DEAD).astype(jnp.int32)

        def trip(state):
            t, _, acc, run = state
            acc, run = chunk(i - 1 - t, (acc, run), None)
            return t + 1, some_alive(run), acc, run

        _, _, acc, _ = lax.while_loop(lambda s: jnp.logical_and(s[0] < i, s[1] > 0), trip,
                                      (jnp.int32(0), some_alive(run), acc, run))
        o_ref[0] = jnp.where(lane0, acc[:SB_QBLK], acc[SB_QBLK:]).astype(BF16)

    blk = pl.BlockSpec((1, SB_QBLK, LANES), lambda b, h, i: (b, i, h))
    return _call(
        body, send, name="sb_fwd", grid=(B_LOC, SB_PAIRS, N_SB_STEPS),
        in_specs=[blk,
                  pl.BlockSpec((1, SEQ, LANES), lambda b, h, i: (b, 0, SB_PAIRS + h)),
                  pl.BlockSpec((1, SEQ, LANES), lambda b, h, i: (b, 0, 2 * SB_PAIRS + h))],
        out_specs=[blk], out_shape=[jax.ShapeDtypeStruct((B_LOC, SEQ, SB_WIDTH), BF16)],
        scratch_shapes=[], semantics=("parallel", "parallel", "arbitrary"), operands=(qkv, qkv, qkv))


def _sb_bwd(qkv, d_o, send=None):
    def body(q_ref, k_ref, v_ref, do_ref, dq_ref, dk_ref, dv_ref, dk_acc, dv_acc, z_scr, keep_scr):
        i = pl.program_id(2)
        krow = lax.broadcasted_iota(jnp.int32, (BLK, BLK), 0)
        kcol = lax.broadcasted_iota(jnp.int32, (BLK, BLK), 1)
        upto = (krow <= kcol).astype(BF16)
        earlier = (krow < kcol).astype(BF16)
        lane0 = lax.broadcasted_iota(jnp.int32, (SB_QBLK, LANES), 1) < HEAD_DIM
        q2 = _stack_heads(q_ref[0] * QK_SCALE, lane0)
        do2 = _stack_heads(do_ref[0], lane0)

        def keep_sum(c, causal):
            off = pl.multiple_of(c * SB_KCHUNK, SB_KCHUNK)
            z = _dot_nt(q2, k_ref[0, pl.ds(off, SB_KCHUNK), :])
            log_keep = _log_keep(z)
            if causal is not None:
                log_keep = jnp.where(causal, log_keep, 0.0)
            z_scr[c] = z
            keep_scr[c] = log_keep
            return jnp.sum(log_keep, axis=1, keepdims=True)

        def some_alive(run):
            return (jnp.max(run) > SB_DEAD).astype(jnp.int32)

        def scan(state):
            t, _, run = state
            run = run + keep_sum(i - 1 - t, None)
            return t + 1, some_alive(run), run

        diag_sum = keep_sum(i, _sb_diag_mask())
        walked, _, tot2 = lax.while_loop(lambda s: jnp.logical_and(s[0] < i, s[1] > 0), scan,
                                         (jnp.int32(0), some_alive(diag_sum), diag_sum))
        first = i - walked

        @pl.when(i == 0)
        def _():
            dk_acc[...] = jnp.zeros_like(dk_acc)
            dv_acc[...] = jnp.zeros_like(dv_acc)

        def chunk(c, carry, causal):
            dq, pre_keep, pre_e = carry
            off = pl.multiple_of(c * SB_KCHUNK, SB_KCHUNK)
            k_c = k_ref[0, pl.ds(off, SB_KCHUNK), :]
            v_c = v_ref[0, pl.ds(off, SB_KCHUNK), :]
            d_a = _dot_nt(do2, v_c)
            log_keep = keep_scr[c]
            log_beta = log_keep + z_scr[c]
            prefix, sums = _block_sums(log_keep, upto)
            inclusive = jnp.concatenate([prefix[0], prefix[1] + sums[0]], axis=1)
            a = jnp.exp(log_beta + ((tot2 - pre_keep) - inclusive))
            if causal is not None:
                a = jnp.where(causal, a, 0.0)
            e = d_a * a
            e_prefix, e_sums = _block_sums(e, earlier, split=False)
            before = jnp.concatenate([e_prefix[0] + pre_e, e_prefix[1] + (pre_e + e_sums[0])], axis=1)
            dz = e - (e + before) * jnp.exp(log_beta)
            if causal is not None:
                dz = jnp.where(causal, dz, 0.0)
            dz = dz.astype(BF16)
            dq = dq + _dot(dz, k_c)
            dk_acc[pl.ds(off, SB_KCHUNK), :] += _dot_tn(dz, q2)
            dv_acc[pl.ds(off, SB_KCHUNK), :] += _dot_tn(a.astype(BF16), do2)
            return dq, pre_keep + (sums[0] + sums[1]), pre_e + (e_sums[0] + e_sums[1])

        zero_col = jnp.zeros((SB_ROWS, 1), F32)
        carry = lax.fori_loop(first, i, lambda t, c: chunk(t, c, None),
                              (jnp.zeros((SB_ROWS, LANES), F32), zero_col, zero_col))
        dq, _, _ = chunk(i, carry, _sb_diag_mask())
        dq_ref[0] = (jnp.where(lane0, dq[:SB_QBLK], dq[SB_QBLK:]) * QK_SCALE).astype(BF16)

        @pl.when(i == N_SB_STEPS - 1)
        def _():
            dk_ref[0] = dk_acc[...].astype(BF16)
            dv_ref[0] = dv_acc[...].astype(BF16)

    blk = pl.BlockSpec((1, SB_QBLK, LANES), lambda b, h, i: (b, i, h))
    whole = lambda c: pl.BlockSpec((1, SEQ, LANES), lambda b, h, i: (b, 0, c * SB_PAIRS + h))
    out = jax.ShapeDtypeStruct((B_LOC, SEQ, SB_WIDTH), BF16)
    return _call(
        body, send, name="sb_bwd", grid=(B_LOC, SB_PAIRS, N_SB_STEPS),
        in_specs=[blk, whole(1), whole(2), blk],
        out_specs=[blk, whole(0), whole(0)],
        out_shape=[out, out, out],
        scratch_shapes=[pltpu.VMEM((SEQ, LANES), F32), pltpu.VMEM((SEQ, LANES), F32),
                        pltpu.VMEM((N_SB_STEPS, SB_ROWS, SB_KCHUNK), F32),
                        pltpu.VMEM((N_SB_STEPS, SB_ROWS, SB_KCHUNK), F32)],
        semantics=("parallel", "parallel", "arbitrary"), operands=(qkv, qkv, qkv, d_o))


DIL_GROUPS = len(DIL_PAIRS)
DIL_QBLOCKS = SEQ // BLK


def _residue_rows(j, dilation):
    length = SEQ // dilation
    return pl.ds(j, length, stride=dilation) if dilation > 1 else pl.ds(0, length)


def _gather_residues(src_ref, dst_ref, dst_off, dilation, scale=None):
    length = SEQ // dilation
    for j in range(dilation):
        v = src_ref[_residue_rows(j, dilation), :]
        if scale is not None:
            v = v * scale
        dst_ref[dst_off + j * length:dst_off + (j + 1) * length, :] = v.astype(dst_ref.dtype)


def _scatter_residues(src_ref, src_off, dst_ref, dilation):
    length = SEQ // dilation
    for j in range(dilation):
        dst_ref[_residue_rows(j, dilation), :] = (
            src_ref[src_off + j * length:src_off + (j + 1) * length, :].astype(dst_ref.dtype))


def _dil_geometry(group, pair):
    dilation = DIL_PAIRS[group][1]
    row = lax.broadcasted_iota(jnp.int32, (2 * BLK, 2 * BLK), 0)
    col = lax.broadcasted_iota(jnp.int32, (2 * BLK, 2 * BLK), 1)
    second = row >= BLK
    steps = BLK + jnp.where(second, row - BLK, row) - col
    coef = -ALIBI_MAX_BIAS / DIL_HEADS * math.log(2.0)
    first_head = float(4 * group + 1) + 2.0 * pair.astype(F32)
    slope = jnp.exp(coef * (first_head + jnp.where(second, 1.0, 0.0)))
    bias = slope * (steps * dilation).astype(F32)
    valid = jnp.logical_and(steps >= 0, steps <= BLK)
    return bias, valid, col >= BLK


def _dil_tile_scores(q2, kk, geometry, has_prev):
    bias, valid, own = geometry
    ok = jnp.logical_and(valid, jnp.logical_or(own, has_prev))
    return jnp.where(ok, _dot_nt(q2, kk) - bias, NEG_BIG)


def _head_col(v, lane_mask):
    return jnp.max(jnp.where(lane_mask, v, NEG_BIG), axis=1, keepdims=True)


def _dil_fwd(qkv, send=None):
    def body(*refs):
        ins, (o_ref, lse_ref), (qs, ks, vs, o_res, lse_res) = refs[:9], refs[9:11], refs[11:16]
        o_grp, lse_grp = refs[16:19], refs[19:22]
        pair = pl.program_id(1)
        lane0 = lax.broadcasted_iota(jnp.int32, (BLK, LANES), 1) < HEAD_DIM
        ks[0:BLK, :] = jnp.zeros((BLK, LANES), BF16)
        vs[0:BLK, :] = jnp.zeros((BLK, LANES), BF16)
        for grp, (_, dilation) in enumerate(DIL_PAIRS):
            q_ref, k_ref, v_ref = ins[3 * grp:3 * grp + 3]
            per_residue = DIL_QBLOCKS // dilation
            _gather_residues(q_ref, qs, 0, dilation, QK_SCALE)
            _gather_residues(k_ref, ks, BLK, dilation)
            _gather_residues(v_ref, vs, BLK, dilation)
            geometry = _dil_geometry(grp, pair)

            def step(blk, _):
                off = pl.multiple_of(blk * BLK, BLK)
                q2 = _stack_heads(qs[pl.ds(off, BLK), :], lane0)
                s = _dil_tile_scores(q2, ks[pl.ds(off, 2 * BLK), :], geometry, blk % per_residue != 0)
                m = jnp.max(s, axis=1, keepdims=True)
                p = jnp.exp(s - m)
                den = jnp.sum(p, axis=1, keepdims=True)
                out = _dot(p.astype(BF16), vs[pl.ds(off, 2 * BLK), :]) / den
                lse = m + jnp.log(den)
                o_res[pl.ds(off, BLK), :] = jnp.where(lane0, out[:BLK], out[BLK:])
                lse_res[pl.ds(off, BLK), :] = jnp.where(lane0, lse[:BLK], lse[BLK:])
                return 0

            lax.fori_loop(0, DIL_QBLOCKS, step, 0, unroll=8)
            _scatter_residues(o_res, 0, o_grp[grp], dilation)
            _scatter_residues(lse_res, 0, lse_grp[grp], dilation)

        for r0 in range(0, SEQ, 2 * BLK):
            rows = slice(r0, r0 + 2 * BLK)
            ls = [lse_grp[g][rows, :] for g in range(DIL_GROUPS)]
            m = jnp.maximum(jnp.maximum(ls[0], ls[1]), ls[2])
            w = [jnp.exp(l - m) for l in ls]
            den = w[0] + w[1] + w[2]
            o_ref[rows, :] = (w[0] * o_grp[0][rows, :] + w[1] * o_grp[1][rows, :] + w[2] * o_grp[2][rows, :]) / den
            lse_ref[rows, :] = m + jnp.log(den)

    def col(part, grp):
        return pl.BlockSpec((None, SEQ, LANES), lambda b, p: (b, 0, 6 * part + 2 * grp + p))

    out_spec = pl.BlockSpec((None, SEQ, LANES), lambda b, p: (b, 0, p))
    out = jax.ShapeDtypeStruct((B_LOC, SEQ, DIL_OUT), F32)
    return _call(
        body, send, name="dil_fwd", grid=(B_LOC, DIL_OUT // LANES),
        in_specs=[col(part, grp) for grp in range(DIL_GROUPS) for part in range(3)],
        out_specs=[out_spec, out_spec], out_shape=[out, out],
        scratch_shapes=[pltpu.VMEM((SEQ, LANES), BF16), pltpu.VMEM((SEQ + BLK, LANES), BF16),
                        pltpu.VMEM((SEQ + BLK, LANES), BF16), pltpu.VMEM((SEQ, LANES), F32),
                        pltpu.VMEM((SEQ, LANES), F32)] + [pltpu.VMEM((SEQ, LANES), F32)] * (2 * DIL_GROUPS),
        semantics=("parallel", "parallel"), operands=[qkv] * 9)


def _dil_bwd(qkv, d_o, lse, dsum, send=None):
    def body(*refs):
        ins, (do_ref, lse_ref, dsum_ref), outs = refs[:9], refs[9:12], refs[12:21]
        qs, ks, vs, dos, lse_res, dsum_res, dq_res, dk_acc, dv_acc = refs[21:]
        pair = pl.program_id(1)
        lane0 = lax.broadcasted_iota(jnp.int32, (BLK, LANES), 1) < HEAD_DIM
        lane1 = jnp.logical_not(lane0)
        ks[0:BLK, :] = jnp.zeros((BLK, LANES), BF16)
        vs[0:BLK, :] = jnp.zeros((BLK, LANES), BF16)
        for grp, (_, dilation) in enumerate(DIL_PAIRS):
            q_ref, k_ref, v_ref = ins[3 * grp:3 * grp + 3]
            dq_ref, dk_ref, dv_ref = outs[3 * grp:3 * grp + 3]
            per_residue = DIL_QBLOCKS // dilation
            _gather_residues(q_ref, qs, 0, dilation, QK_SCALE)
            _gather_residues(k_ref, ks, BLK, dilation)
            _gather_residues(v_ref, vs, BLK, dilation)
            _gather_residues(do_ref, dos, 0, dilation)
            _gather_residues(lse_ref, lse_res, 0, dilation)
            _gather_residues(dsum_ref, dsum_res, 0, dilation)
            dk_acc[...] = jnp.zeros_like(dk_acc)
            dv_acc[...] = jnp.zeros_like(dv_acc)
            geometry = _dil_geometry(grp, pair)

            def step(blk, _):
                off = pl.multiple_of(blk * BLK, BLK)
                q2 = _stack_heads(qs[pl.ds(off, BLK), :], lane0)
                do2 = _stack_heads(dos[pl.ds(off, BLK), :], lane0)
                kk = ks[pl.ds(off, 2 * BLK), :]
                vv = vs[pl.ds(off, 2 * BLK), :]
                lse_blk = lse_res[pl.ds(off, BLK), :]
                dsum_blk = dsum_res[pl.ds(off, BLK), :]
                lse2 = jnp.concatenate([_head_col(lse_blk, lane0), _head_col(lse_blk, lane1)], axis=0)
                dsum2 = jnp.concatenate([_head_col(dsum_blk, lane0), _head_col(dsum_blk, lane1)], axis=0)
                s = _dil_tile_scores(q2, kk, geometry, blk % per_residue != 0)
                p = jnp.exp(s - lse2)
                ds = (p * (_dot_nt(do2, vv) - dsum2)).astype(BF16)
                dq2 = _dot(ds, kk)
                dq_res[pl.ds(off, BLK), :] = jnp.where(lane0, dq2[:BLK], dq2[BLK:]) * QK_SCALE
                dk_acc[pl.ds(off, 2 * BLK), :] += _dot_tn(ds, q2)
                dv_acc[pl.ds(off, 2 * BLK), :] += _dot_tn(p.astype(BF16), do2)
                return 0

            lax.fori_loop(0, DIL_QBLOCKS, step, 0, unroll=8)
            _scatter_residues(dq_res, 0, dq_ref, dilation)
            _scatter_residues(dk_acc, BLK, dk_ref, dilation)
            _scatter_residues(dv_acc, BLK, dv_ref, dilation)

    def col(part, grp):
        return pl.BlockSpec((None, SEQ, LANES), lambda b, p: (b, 0, 6 * part + 2 * grp + p))

    slot = pl.BlockSpec((None, SEQ, LANES), lambda b, p: (b, 0, p))
    out = jax.ShapeDtypeStruct((B_LOC, SEQ, DIL_OUT), F32)
    return _call(
        body, send, name="dil_bwd", grid=(B_LOC, DIL_OUT // LANES),
        in_specs=[col(part, grp) for grp in range(DIL_GROUPS) for part in range(3)] + [slot] * 3,
        out_specs=[slot] * 9, out_shape=[out] * 9,
        scratch_shapes=[pltpu.VMEM((SEQ, LANES), BF16), pltpu.VMEM((SEQ + BLK, LANES), BF16),
                        pltpu.VMEM((SEQ + BLK, LANES), BF16), pltpu.VMEM((SEQ, LANES), BF16),
                        pltpu.VMEM((SEQ, LANES), F32), pltpu.VMEM((SEQ, LANES), F32),
                        pltpu.VMEM((SEQ, LANES), F32), pltpu.VMEM((SEQ + BLK, LANES), F32),
                        pltpu.VMEM((SEQ + BLK, LANES), F32)],
        semantics=("parallel", "parallel"), operands=[qkv] * 9 + [d_o, lse, dsum])


def _peers():
    x, y, c = lax.axis_index("x"), lax.axis_index("y"), lax.axis_index("c")
    me = 4 * x + 2 * y + c
    peers = []
    for mask in range(1, N_DEV):
        px = 1 - x if mask & 4 else x
        py = 1 - y if mask & 2 else y
        pc = 1 - c if mask & 1 else c
        peers.append(((px, py, pc), 4 * px + 2 * py + pc))
    return me, peers


def _all_gather(shard, name):
    rows, cols = shard.shape
    by_rows = rows % 32 == 0

    def body(src_ref, out_ref, send_sems, recv_sems, local_sem):
        x, y, c = lax.axis_index("x"), lax.axis_index("y"), lax.axis_index("c")
        me, sibling = (x, y, c), (x, y, 1 - c)
        x_chip, y_chip, across = (1 - x, y), (x, 1 - y), (1 - x, 1 - y)

        def slot(block, half=None):
            ref = out_ref.at[4 * block[0] + 2 * block[1] + block[2]]
            if half is None:
                return ref
            return ref.at[pl.ds(half * (rows // 2), rows // 2)] if by_rows else \
                ref.at[:, pl.ds(half * (cols // 2), cols // 2)]

        def copy(k, block, to, src=None, half=None):
            return pltpu.make_async_remote_copy(
                src_ref=slot(block, half) if src is None else src, dst_ref=slot(block, half),
                send_sem=send_sems.at[k], recv_sem=recv_sems.at[k], device_id=to,
                device_id_type=pl.DeviceIdType.MESH)

        mine = pltpu.make_async_copy(src_ref, slot(me), local_sem)
        mine.start()
        sent = [copy(0, me, sibling, src=src_ref), copy(1, me, (*x_chip, c), src=src_ref),
                copy(2, me, (*y_chip, c), src=src_ref)]
        for cp in sent:
            cp.start()

        def arrived(k, block, half=None):
            copy(k, block, me, half=half).wait_recv()
            onward = {1: [copy(3, block, (*y_chip, c), half=0), copy(5, block, sibling)],
                      2: [copy(4, block, (*x_chip, c), half=1), copy(6, block, sibling)],
                      3: [copy(7, block, sibling, half=0)],
                      4: [copy(8, block, sibling, half=1)]}.get(k, [])
            for cp in onward:
                cp.start()
            sent.extend(onward)

        arrived(1, (*x_chip, c))
        arrived(2, (*y_chip, c))
        arrived(3, (*across, c), half=0)
        arrived(4, (*across, c), half=1)
        arrived(0, sibling)
        arrived(5, (*x_chip, 1 - c))
        arrived(6, (*y_chip, 1 - c))
        arrived(7, (*across, 1 - c), half=0)
        arrived(8, (*across, 1 - c), half=1)
        for cp in sent:
            cp.wait_send()
        mine.wait()

    n_sems = 9
    return pl.pallas_call(
        body, name=name,
        in_specs=[pl.BlockSpec(memory_space=pl.ANY)],
        out_specs=pl.BlockSpec(memory_space=pl.ANY),
        out_shape=jax.ShapeDtypeStruct((N_DEV,) + shard.shape, shard.dtype),
        scratch_shapes=[pltpu.SemaphoreType.DMA((n_sems,)), pltpu.SemaphoreType.DMA((n_sems,)),
                        pltpu.SemaphoreType.DMA],
    )(shard)


def _call(body, send, *, name, grid, in_specs, out_specs, out_shape, scratch_shapes, semantics, operands):
    if send is None:
        return pl.pallas_call(
            body, name=name, grid=grid, in_specs=in_specs, out_specs=out_specs, out_shape=out_shape,
            scratch_shapes=scratch_shapes, compiler_params=_params(semantics))(*operands), []
    srcs, kinds = [s for s, _ in send], [k for _, k in send]
    n, n_in, n_out, n_scr = len(srcs), len(in_specs), len(out_specs), len(scratch_shapes)
    steps = math.prod(grid)
    relay_step = (13 * steps) // 16

    def plan(refs):
        src_refs, land_refs = refs[n_in:n_in + n], refs[n_in + n + n_out:n_in + 2 * n + n_out]
        send_sems, recv_sems, local_sems = refs[-3:]
        x, y, c = lax.axis_index("x"), lax.axis_index("y"), lax.axis_index("c")
        me, peers = _peers()
        first, relayed_in, relayed_out, arrivals, sends, own = [], [], [], [], [], []
        for a, kind in enumerate(kinds):
            def copy(k, src, dst_slot, to):
                return pltpu.make_async_remote_copy(
                    src_ref=src, dst_ref=land_refs[a].at[dst_slot], send_sem=send_sems.at[a * (N_DEV - 1) + k],
                    recv_sem=recv_sems.at[a * (N_DEV - 1) + k], device_id=to, device_id_type=pl.DeviceIdType.MESH)

            if kind == "gather_by_chip":
                idx = lambda px, py, pc: 4 * px + 2 * py + pc
                chips = [(1 - x, y), (x, 1 - y), (1 - x, 1 - y)]
                mine = [copy(0, src_refs[a], me, (x, y, 1 - c))]
                arrivals.append(copy(0, src_refs[a], idx(x, y, 1 - c), (x, y, 1 - c)))
                for j, (px, py) in enumerate(chips):
                    mine.append(copy(1 + j, src_refs[a], me, (px, py, c)))
                    relayed_in.append(copy(1 + j, src_refs[a], idx(px, py, c), (px, py, c)))
                    relayed_out.append(copy(4 + j, land_refs[a].at[idx(px, py, c)], idx(px, py, c), (x, y, 1 - c)))
                    arrivals.append(copy(4 + j, src_refs[a], idx(px, py, 1 - c), (x, y, 1 - c)))
                first += mine
                sends += mine + relayed_out[-3:]
                own.append(pltpu.make_async_copy(src_refs[a], land_refs[a].at[me], local_sems.at[a]))
            elif kind == "scatter_by_chip":
                for k, (px, py) in enumerate([(1 - x, y), (x, 1 - y), (1 - x, 1 - y)]):
                    first.append(copy(k, src_refs[a].at[2 * px + py], 2 * x + y, (px, py, c)))
                    arrivals.append(copy(k, src_refs[a].at[2 * px + py], 2 * px + py, (px, py, c)))
                sends += first[-3:]
                own.append(pltpu.make_async_copy(src_refs[a].at[2 * x + y], land_refs[a].at[2 * x + y],
                                                 local_sems.at[a]))
            else:
                part = (lambda i: src_refs[a].at[i]) if kind == "scatter" else (lambda i: src_refs[a])
                for k, (peer, peer_idx) in enumerate(peers):
                    first.append(copy(k, part(peer_idx), me, peer))
                    arrivals.append(copy(k, part(peer_idx), peer_idx, peer))
                sends += first[-(N_DEV - 1):]
                own.append(pltpu.make_async_copy(part(me), land_refs[a].at[me], local_sems.at[a]))
        return first, relayed_in, relayed_out, arrivals, sends, own

    def wrapped(*refs):
        step = 0
        for axis, size in enumerate(grid):
            step = step * size + pl.program_id(axis)

        @pl.when(step == 0)
        def _():
            first, _, _, _, _, own = plan(refs)
            for cp in first + own:
                cp.start()

        if "gather_by_chip" in kinds:
            @pl.when(step == relay_step)
            def _():
                _, relayed_in, relayed_out, _, _, _ = plan(refs)
                for cp_in, cp_out in zip(relayed_in, relayed_out):
                    cp_in.wait_recv()
                    cp_out.start()

        body(*refs[:n_in], *refs[n_in + n:n_in + n + n_out], *refs[n_in + 2 * n + n_out:n_in + 2 * n + n_out + n_scr])

        @pl.when(step == steps - 1)
        def _():
            _, _, _, arrivals, sends, own = plan(refs)
            for cp in arrivals:
                cp.wait_recv()
            for cp in sends:
                cp.wait_send()
            for cp in own:
                cp.wait()

    anywhere = pl.BlockSpec(memory_space=pl.ANY)
    lands = [jax.ShapeDtypeStruct((N_DEV // 2 if k == "scatter_by_chip" else N_DEV,) + s.shape[-2:], s.dtype)
             for s, k in send]
    out = pl.pallas_call(
        wrapped, name=name, grid=grid,
        in_specs=list(in_specs) + [anywhere] * n, out_specs=list(out_specs) + [anywhere] * n,
        out_shape=list(out_shape) + lands,
        scratch_shapes=list(scratch_shapes) + [pltpu.SemaphoreType.DMA((n * (N_DEV - 1),)),
                                               pltpu.SemaphoreType.DMA((n * (N_DEV - 1),)),
                                               pltpu.SemaphoreType.DMA((n,))],
        compiler_params=_params(("arbitrary",) * len(grid)),
    )(*operands, *srcs)
    return out[:n_out], list(out[n_out:])


def _pair_swap(blocks):
    def body(src_ref, out_ref, send_sems, recv_sems):
        x, y, c = lax.axis_index("x"), lax.axis_index("y"), lax.axis_index("c")
        copies = [pltpu.make_async_remote_copy(
            src_ref=src_ref.at[2 * chip + (1 - c)], dst_ref=out_ref.at[chip], send_sem=send_sems.at[chip],
            recv_sem=recv_sems.at[chip], device_id=(x, y, 1 - c), device_id_type=pl.DeviceIdType.MESH)
            for chip in range(N_DEV // 2)]
        for cp in copies:
            cp.start()
        for cp in copies:
            cp.wait()

    return pl.pallas_call(
        body, name="pair_swap_grad_w_in",
        in_specs=[pl.BlockSpec(memory_space=pl.ANY)], out_specs=pl.BlockSpec(memory_space=pl.ANY),
        out_shape=jax.ShapeDtypeStruct((N_DEV // 2,) + blocks.shape[1:], blocks.dtype),
        scratch_shapes=[pltpu.SemaphoreType.DMA((N_DEV // 2,)), pltpu.SemaphoreType.DMA((N_DEV // 2,))],
    )(blocks)


def _pair_sum(blocks, swapped, core):
    _, rows, cols = swapped.shape
    tile_rows = _row_tile(rows)

    def body(core_ref, mine_ref, theirs_ref, o_ref):
        o_ref[...] = (mine_ref[...].astype(F32) + theirs_ref[...].astype(F32)).astype(o_ref.dtype)

    return pl.pallas_call(
        body, name="pair_sum_grad_w_in",
        grid_spec=pltpu.PrefetchScalarGridSpec(
            num_scalar_prefetch=1, grid=(N_DEV // 2, rows // tile_rows),
            in_specs=[pl.BlockSpec((None, tile_rows, cols), lambda j, i, core_ref: (2 * j + core_ref[0], i, 0)),
                      pl.BlockSpec((None, tile_rows, cols), lambda j, i, core_ref: (j, i, 0))],
            out_specs=pl.BlockSpec((None, tile_rows, cols), lambda j, i, core_ref: (j, i, 0))),
        out_shape=jax.ShapeDtypeStruct(swapped.shape, swapped.dtype),
        compiler_params=_params(("parallel", "parallel")),
    )(core, blocks, swapped)


def _sum_in_device_order(land_ref):
    acc = land_ref[0].astype(F32)
    for j in range(1, land_ref.shape[0]):
        acc = acc + land_ref[j].astype(F32)
    return acc


def _adam_math(w, g, m, v):
    c1 = 1.0 - ADAM_B1 ** ADAM_STEP
    c2 = 1.0 - ADAM_B2 ** ADAM_STEP
    m_new = ADAM_B1 * m + (1.0 - ADAM_B1) * g
    v_new = ADAM_B2 * v + (1.0 - ADAM_B2) * (g * g)
    delta = -ADAM_LR * ((m_new / c1) / (jnp.sqrt(v_new / c2) + ADAM_EPS) + ADAM_WD * w)
    return delta, m_new, v_new


def _row_tile(rows):
    return max(t for t in range(16, 385, 16) if rows % t == 0) if rows % 16 == 0 else rows


def _sum_update(land, w, m, v, name):
    slots, rows, cols = land.shape
    tile_rows = _row_tile(rows)

    def body(land_ref, w_ref, m_ref, v_ref, g_ref, d_ref, nm_ref, nv_ref):
        g = _sum_in_device_order(land_ref)
        g_ref[...] = g
        d_ref[...], nm_ref[...], nv_ref[...] = _adam_math(w_ref[...], g, m_ref[...], v_ref[...])

    tile = pl.BlockSpec((None, tile_rows, cols), lambda i: (0, i, 0))
    out = jax.ShapeDtypeStruct((1, rows, cols), F32)
    return pl.pallas_call(
        body, name=name, grid=(rows // tile_rows,),
        in_specs=[pl.BlockSpec((slots, tile_rows, cols), lambda i: (0, i, 0)), tile, tile, tile],
        out_specs=[tile] * 4, out_shape=[out] * 4,
        compiler_params=_params(("parallel",)),
    )(land, w, m, v)


def _to_bf16(arrays):
    def body(*refs):
        for src_ref, dst_ref in zip(refs[:len(arrays)], refs[len(arrays):]):
            dst_ref[...] = src_ref[...].astype(BF16)

    whole = [pl.BlockSpec(a.shape, lambda i: (0, 0)) for a in arrays]
    return pl.pallas_call(
        body, name="shards_to_bf16", grid=(1,), in_specs=whole, out_specs=whole,
        out_shape=[jax.ShapeDtypeStruct(a.shape, BF16) for a in arrays],
        compiler_params=_params(("arbitrary",)),
    )(*arrays)


N_GAINS = 3


def _sum_update_gains(land, ws, ms, vs):
    def body(land_ref, *refs):
        w_refs, m_refs, v_refs = (refs[k * N_GAINS:(k + 1) * N_GAINS] for k in range(3))
        loss_ref, out_refs = refs[3 * N_GAINS], refs[3 * N_GAINS + 1:]
        rows = _sum_in_device_order(land_ref)
        loss_ref[...] = rows[N_GAINS:N_GAINS + 1]
        for k in range(N_GAINS):
            g = rows[k:k + 1]
            g_ref, d_ref, nm_ref, nv_ref = out_refs[4 * k:4 * k + 4]
            g_ref[...] = g
            d_ref[...], nm_ref[...], nv_ref[...] = _adam_math(w_refs[k][...], g, m_refs[k][...], v_refs[k][...])

    row = pl.BlockSpec((1, D_MODEL), lambda i: (0, 0))
    out = jax.ShapeDtypeStruct((1, D_MODEL), F32)
    n_out = 1 + 4 * N_GAINS
    return pl.pallas_call(
        body, name="update_gains", grid=(1,),
        in_specs=[pl.BlockSpec(land.shape, lambda i: (0, 0, 0))] + [row] * (3 * N_GAINS),
        out_specs=[row] * n_out, out_shape=[out] * n_out,
    )(land, *ws, *ms, *vs)


GROUP_FFN = ("w_ffn_in", "w_ffn_out")
GROUP_MIX = ("w_sb_up", "w_dil_up", "w_out")
COL_SHARDED = ("w_in", "w_sb_up", "w_dil_up", "w_ffn_in")
TRANSPOSED = ("w_in", "w_ffn_in")


def _full_from_shards(name, slots):
    _, r, c = slots.shape
    if name in TRANSPOSED:
        return slots.reshape(N_DEV * r, c).T
    if name in COL_SHARDED:
        return slots.transpose(1, 0, 2).reshape(r, N_DEV * c)
    return slots.reshape(N_DEV * r, c)


def _row_shards(full):
    rows, cols = full.shape
    return full.reshape(N_DEV, rows // N_DEV, cols)


def _local_step(x, target, g_mix, g_ffn, g_fin, w_in_t, shards=None, rest=None):
    gather = lambda names, kind: None if shards is None else [(shards[n], kind) for n in names]
    scatter = lambda blocks: None if shards is None else [(t, "scatter") for t in blocks]
    landed = lambda blocks, lands: lands if lands else blocks

    w = {"w_in": w_in_t}
    if shards is None:
        w.update(rest)
        w["w_ffn_in"] = rest["w_ffn_in"].T
    (qkv_sb, qkv_dl, gates, u), _ = _norm_proj(x, g_mix, w["w_in"])
    qkv_sb = qkv_sb.reshape(B_LOC, SEQ, 3 * SB_WIDTH)
    qkv_dl = qkv_dl.reshape(B_LOC, SEQ, 3 * DIL_WIDTH)
    (o_sb,), lands = _sb_fwd(qkv_sb, gather(GROUP_FFN, "gather_by_chip"))
    if lands:
        w["w_ffn_in"], w["w_ffn_out"] = lands[0].reshape(2 * D_FF, D_MODEL), _full_from_shards("w_ffn_out", lands[1])
    o_sb = o_sb.reshape(TOK, SB_WIDTH)
    (o_dl, lse), lands = _dil_fwd(qkv_dl, gather(GROUP_MIX, "gather"))
    w.update({n: _full_from_shards(n, t) for n, t in zip(GROUP_MIX, lands)})
    o_dl = o_dl.reshape(TOK, DIL_OUT)

    loss, dx1, merged, u2, act, dh, dx2, dg_fin, dg_ffn = _mix_ffn_fwd_bwd(
        x, o_sb, o_dl, gates, w["w_sb_up"], w["w_dil_up"], w["w_out"], target, g_ffn, g_fin,
        w["w_ffn_in"], w["w_ffn_out"])
    dgates, dy_sb, dy_dl, do_sb, do_dl, dsum = _mix_bwd(dx1, o_sb, o_dl, gates, w["w_sb_up"], w["w_dil_up"], w["w_out"])
    blocks = {
        "w_sb_up": _atb(o_sb, dy_sb, "grad_w_sb_up", SB_WIDTH, D_MODEL, col_blocks=N_DEV),
        "w_dil_up": _atb(o_dl, dy_dl, "grad_w_dil_up", DIL_OUT, D_MODEL, col_blocks=N_DEV),
        "w_out": _row_shards(_atb(merged, dx1, "grad_w_out", D_MODEL, D_MODEL)),
        "w_ffn_in": _row_shards(_atb_cols(dh, u2, "grad_w_ffn_in", 512)),
        "w_ffn_out": _row_shards(_atb_cols(act, dx2, "grad_w_ffn_out", 256)),
    }
    grads = {}

    early, late = GROUP_FFN, GROUP_MIX
    early_blocks = [blocks[n] for n in early]
    (dq_sb, dk_sb, dv_sb), lands = _sb_bwd(qkv_sb, do_sb.reshape(B_LOC, SEQ, SB_WIDTH), scatter(early_blocks))
    grads.update(zip(early, landed(early_blocks, lands)))
    as_batch = lambda t: t.reshape(B_LOC, SEQ, DIL_OUT)
    late_blocks = [blocks[n] for n in late]
    d_dl, lands = _dil_bwd(qkv_dl, as_batch(do_dl), lse, as_batch(dsum), scatter(late_blocks))
    grads.update(zip(late, landed(late_blocks, lands)))
    flat = lambda t: t.reshape(TOK, -1)
    dproj = ([flat(dq_sb), flat(dk_sb), flat(dv_sb)]
             + [flat(d_dl[3 * grp + part]) for part in range(3) for grp in range(DIL_GROUPS)] + [dgates])

    w_in_blocks = _row_shards(_atb_pieces(u, dproj, "grad_w_in", D_MODEL // 2))
    if shards is None:
        grads["w_in"] = w_in_blocks
        send = None
    else:
        core = lax.axis_index("c").astype(jnp.int32).reshape(1)
        send = [(_pair_sum(w_in_blocks, _pair_swap(w_in_blocks), core), "scatter_by_chip")]
    (grad_x, dg_mix), lands = _proj_bwd(dproj, dx1, x, g_mix, w["w_in"], send)
    if lands:
        grads["w_in"] = lands[0]
    gain_grads = jnp.concatenate([dg_mix, dg_ffn, dg_fin], axis=0)
    return loss, grad_x, gain_grads, grads


def kernel(x, norm_mix_g, w_in, w_sb_up, w_dil_up, w_out, norm_ffn_g, w_ffn_in, w_ffn_out, norm_final_g, loss_target, m_norm_mix_g, m_w_in, m_w_sb_up, m_w_dil_up, m_w_out, m_norm_ffn_g, m_w_ffn_in, m_w_ffn_out, m_norm_final_g, v_norm_mix_g, v_w_in, v_w_sb_up, v_w_dil_up, v_w_out, v_norm_ffn_g, v_w_ffn_in, v_w_ffn_out, v_norm_final_g):
    mats = {"w_in": w_in, "w_sb_up": w_sb_up, "w_dil_up": w_dil_up, "w_out": w_out,
            "w_ffn_in": w_ffn_in, "w_ffn_out": w_ffn_out}
    moments_m = {"w_in": m_w_in, "w_sb_up": m_w_sb_up, "w_dil_up": m_w_dil_up, "w_out": m_w_out,
                 "w_ffn_in": m_w_ffn_in, "w_ffn_out": m_w_ffn_out}
    moments_v = {"w_in": v_w_in, "w_sb_up": v_w_sb_up, "w_dil_up": v_w_dil_up, "w_out": v_w_out,
                 "w_ffn_in": v_w_ffn_in, "w_ffn_out": v_w_ffn_out}
    stored = lambda t, name: t.transpose(0, 2, 1) if name in TRANSPOSED else t
    mats = {name: stored(t, name) for name, t in mats.items()}
    shards = dict(zip(mats, _to_bf16([t[0] for t in mats.values()])))
    gathered_w_in = _all_gather(shards.pop("w_in"), "all_gather_w_in")
    g_fin = norm_final_g.reshape(1, D_MODEL)
    loss, grad_x, gain_grads, grad_slots = _local_step(
        x.reshape(TOK, D_MODEL), loss_target.reshape(TOK, D_MODEL), norm_mix_g, norm_ffn_g, g_fin,
        gathered_w_in.reshape(IN_WIDTH, D_MODEL), shards=shards)

    gain_rows = jnp.concatenate([gain_grads, jnp.tile(loss, (1, D_MODEL // LANES)),
                                 jnp.zeros((8 - 4, D_MODEL), F32)], axis=0)
    row = lambda t: t.reshape(1, D_MODEL)
    loss_row, *gain_outs = _sum_update_gains(
        _all_gather(gain_rows, "all_gather_gains"),
        [norm_mix_g, norm_ffn_g, g_fin], [m_norm_mix_g, m_norm_ffn_g, row(m_norm_final_g)],
        [v_norm_mix_g, v_norm_ffn_g, row(v_norm_final_g)])

    out_g, out_d, out_m, out_v = {}, {}, {}, {}
    for name, slots in grad_slots.items():
        out_g[name], out_d[name], out_m[name], out_v[name] = [stored(t, name) for t in _sum_update(
            slots, mats[name], stored(moments_m[name], name), stored(moments_v[name], name), "update_" + name)]
    for idx, name in enumerate(("norm_mix_g", "norm_ffn_g", "norm_final_g")):
        shape = (D_MODEL,) if name == "norm_final_g" else (1, D_MODEL)
        out_g[name], out_d[name], out_m[name], out_v[name] = [t.reshape(shape) for t in gain_outs[4 * idx:4 * idx + 4]]

    order = ("norm_mix_g", "w_in", "w_sb_up", "w_dil_up", "w_out", "norm_ffn_g", "w_ffn_in", "w_ffn_out",
             "norm_final_g")
    return (loss_row[0, 0], grad_x.reshape(B_LOC, SEQ, D_MODEL),
            *[out_g[n] for n in order], *[out_d[n] for n in order],
            *[out_m[n] for n in order], *[out_v[n] for n in order])
```

```python
import math

import jax
import jax.numpy as jnp
from jax import lax
from jax.experimental import pallas as pl
from jax.experimental.pallas import tpu as pltpu

F32 = jnp.float32
BF16 = jnp.bfloat16

N_DEV = 8
D_MODEL = 1024
SEQ = 2048
B_LOC = 2
TOK = B_LOC * SEQ
HEAD_DIM = 64
SB_WIDTH = 512
DIL_WIDTH = 768
DIL_OUT = 256
QKV_WIDTH = 3 * SB_WIDTH + 3 * DIL_WIDTH
IN_WIDTH = QKV_WIDTH + 2 * D_MODEL
D_FF = 2816
DIL_PAIRS = ((128, 1), (512, 4), (2048, 16))
DIL_HEADS = 12
RMS_EPS = 1e-6
ALIBI_MAX_BIAS = 8.0
QK_SCALE = 1.0 / math.sqrt(HEAD_DIM)
BLK = 128
LANES = 128
NEG_BIG = -1e30

ADAM_LR = 0.001
ADAM_B1 = 0.9
ADAM_B2 = 0.999
ADAM_EPS = 1e-08
ADAM_WD = 0.01
ADAM_STEP = 10

VMEM_LIMIT = 58 * 1024 * 1024


def _dot(a, b):
    return jnp.dot(a, b, preferred_element_type=F32)


def _dot_nt(a, b):
    return lax.dot_general(a, b, (((1,), (1,)), ((), ())), preferred_element_type=F32)


def _dot_tn(a, b):
    return lax.dot_general(a, b, (((0,), (0,)), ((), ())), preferred_element_type=F32)


def _sigmoid(z):
    return 1.0 / (1.0 + jnp.exp(-z))


def _split_bf16(v):
    hi = v.astype(BF16)
    lo = (v - hi.astype(F32)).astype(BF16)
    return hi, lo


def _chunks(width, step=512):
    out, c = [], 0
    while c < width:
        w = min(step, width - c)
        out.append((c, w))
        c += w
    return out


def _resident(shape):
    nd = len(shape)
    return pl.BlockSpec(shape, lambda *_: (0,) * nd, pipeline_mode=pl.Buffered(1))


def _params(sem):
    return pltpu.CompilerParams(dimension_semantics=sem, vmem_limit_bytes=VMEM_LIMIT)


def _rms_fwd(x, g):
    r = lax.rsqrt(jnp.mean(x * x, axis=-1, keepdims=True) + RMS_EPS)
    n = x * r
    return n, r, n * g


def _rms_bwd(dy, n, r, g):
    dg = jnp.sum(dy * n, axis=0, keepdims=True)
    dn = dy * g
    dx = r * (dn - n * jnp.mean(dn * n, axis=-1, keepdims=True))
    return dx, dg


TM = 256
TM_WIDE = 512


def _norm_proj(x, g, w_in_t, send=None):
    def body(x_ref, g_ref, w_ref, sb_ref, dl_ref, gate_ref, u_ref):
        _, _, u = _rms_fwd(x_ref[...], g_ref[...])
        u = u.astype(BF16)
        u_ref[...] = u
        for c0, w in _chunks(3 * SB_WIDTH):
            sb_ref[:, c0:c0 + w] = _dot_nt(u, w_ref[c0:c0 + w, :]).astype(BF16)
        for c0, w in _chunks(3 * DIL_WIDTH):
            dl_ref[:, c0:c0 + w] = _dot_nt(u, w_ref[3 * SB_WIDTH + c0:3 * SB_WIDTH + c0 + w, :])
        for c0, w in _chunks(2 * D_MODEL):
            gate_ref[:, c0:c0 + w] = _dot_nt(u, w_ref[QKV_WIDTH + c0:QKV_WIDTH + c0 + w, :])

    return _call(
        body, send, name="norm_proj", grid=(TOK // TM_WIDE,),
        in_specs=[pl.BlockSpec((TM_WIDE, D_MODEL), lambda i: (i, 0)), _resident((1, D_MODEL)),
                  _resident((IN_WIDTH, D_MODEL))],
        out_specs=[pl.BlockSpec((TM_WIDE, 3 * SB_WIDTH), lambda i: (i, 0)),
                   pl.BlockSpec((TM_WIDE, 3 * DIL_WIDTH), lambda i: (i, 0)),
                   pl.BlockSpec((TM_WIDE, 2 * D_MODEL), lambda i: (i, 0)),
                   pl.BlockSpec((TM_WIDE, D_MODEL), lambda i: (i, 0))],
        out_shape=[jax.ShapeDtypeStruct((TOK, 3 * SB_WIDTH), BF16),
                   jax.ShapeDtypeStruct((TOK, 3 * DIL_WIDTH), F32),
                   jax.ShapeDtypeStruct((TOK, 2 * D_MODEL), F32),
                   jax.ShapeDtypeStruct((TOK, D_MODEL), BF16)],
        scratch_shapes=[], semantics=("parallel",), operands=(x, g, w_in_t))


FF_CHUNK = 1024


def _mix_ffn_fwd_bwd(x, o_sb, o_dl, gates, w_sb_up, w_dil_up, w_out, target, g_ffn, g_fin, w_ffn_in_t, w_ffn_out):
    def body(x_ref, osb_ref, odl_ref, gate_ref, wsb_ref, wdl_ref, wo_ref, t_ref, gffn_ref, gfin_ref, win_ref, wout_ref,
             loss_ref, dx1_ref, mg_ref, u2_ref, act_ref, dh_ref, dx2_ref, dgfin_ref, dgffn_ref, h_scr):
        i = pl.program_id(0)

        @pl.when(i == 0)
        def _():
            loss_ref[...] = jnp.zeros_like(loss_ref)
            dgfin_ref[...] = jnp.zeros_like(dgfin_ref)
            dgffn_ref[...] = jnp.zeros_like(dgffn_ref)

        y_sb = _dot(osb_ref[...], wsb_ref[...])
        y_dl = _dot(odl_ref[...].astype(BF16), wdl_ref[...])
        merged = (_sigmoid(gate_ref[:, :D_MODEL]) * y_sb
                  + _sigmoid(gate_ref[:, D_MODEL:]) * y_dl).astype(BF16)
        mg_ref[...] = merged
        x1 = x_ref[...] + _dot(merged, wo_ref[...])
        g_ffn_v = gffn_ref[...]
        g_fin_v = gfin_ref[...]
        n2, r2, u2 = _rms_fwd(x1, g_ffn_v)
        u2 = u2.astype(BF16)
        u2_ref[...] = u2
        x2 = x1
        for c0, w in _chunks(D_FF, FF_CHUNK):
            gate = _dot_nt(u2, win_ref[c0:c0 + w, :])
            up = _dot_nt(u2, win_ref[D_FF + c0:D_FF + c0 + w, :])
            h_scr[:, c0:c0 + w] = gate
            h_scr[:, D_FF + c0:D_FF + c0 + w] = up
            act = (gate * _sigmoid(gate) * up).astype(BF16)
            act_ref[:, c0:c0 + w] = act
            x2 = x2 + _dot(act, wout_ref[c0:c0 + w, :])
        n3, r3, y = _rms_fwd(x2, g_fin_v)
        err = y - t_ref[...]
        sq = jnp.sum(jnp.sum(err * err, axis=1, keepdims=True), axis=0, keepdims=True)
        loss_ref[...] += sq * (0.5 / D_MODEL)
        dx2, dgfin = _rms_bwd(err * (1.0 / D_MODEL), n3, r3, g_fin_v)
        dgfin_ref[...] += dgfin
        dx2_b = dx2.astype(BF16)
        dx2_ref[...] = dx2_b
        du2 = jnp.zeros((TM, D_MODEL), F32)
        for c0, w in _chunks(D_FF, FF_CHUNK):
            gate = h_scr[:, c0:c0 + w]
            up = h_scr[:, D_FF + c0:D_FF + c0 + w]
            dact = _dot_nt(dx2_b, wout_ref[c0:c0 + w, :])
            sg = _sigmoid(gate)
            dgate = (dact * up * (sg * (1.0 + gate * (1.0 - sg)))).astype(BF16)
            dup = (dact * (gate * sg)).astype(BF16)
            dh_ref[:, c0:c0 + w] = dgate
            dh_ref[:, D_FF + c0:D_FF + c0 + w] = dup
            du2 = du2 + _dot(dgate, win_ref[c0:c0 + w, :])
            du2 = du2 + _dot(dup, win_ref[D_FF + c0:D_FF + c0 + w, :])
        dx1_n, dgffn = _rms_bwd(du2, n2, r2, g_ffn_v)
        dgffn_ref[...] += dgffn
        dx1_ref[...] = dx2 + dx1_n

    tile = lambda w: pl.BlockSpec((TM, w), lambda i: (i, 0))
    acc = lambda w: pl.BlockSpec((1, w), lambda i: (0, 0))
    return pl.pallas_call(
        body, name="mix_ffn_fwd_bwd", grid=(TOK // TM,),
        in_specs=[tile(D_MODEL), tile(SB_WIDTH), tile(DIL_OUT), tile(2 * D_MODEL),
                  _resident((SB_WIDTH, D_MODEL)), _resident((DIL_OUT, D_MODEL)), _resident((D_MODEL, D_MODEL)),
                  tile(D_MODEL), _resident((1, D_MODEL)), _resident((1, D_MODEL)),
                  _resident((2 * D_FF, D_MODEL)), _resident((D_FF, D_MODEL))],
        out_specs=[acc(LANES), tile(D_MODEL), tile(D_MODEL), tile(D_MODEL), tile(D_FF), tile(2 * D_FF),
                   tile(D_MODEL), acc(D_MODEL), acc(D_MODEL)],
        out_shape=[jax.ShapeDtypeStruct((1, LANES), F32),
                   jax.ShapeDtypeStruct((TOK, D_MODEL), F32),
                   jax.ShapeDtypeStruct((TOK, D_MODEL), BF16),
                   jax.ShapeDtypeStruct((TOK, D_MODEL), BF16),
                   jax.ShapeDtypeStruct((TOK, D_FF), BF16),
                   jax.ShapeDtypeStruct((TOK, 2 * D_FF), BF16),
                   jax.ShapeDtypeStruct((TOK, D_MODEL), BF16),
                   jax.ShapeDtypeStruct((1, D_MODEL), F32),
                   jax.ShapeDtypeStruct((1, D_MODEL), F32)],
        scratch_shapes=[pltpu.VMEM((TM, 2 * D_FF), F32)],
        compiler_params=_params(("arbitrary",)),
    )(x, o_sb, o_dl, gates, w_sb_up, w_dil_up, w_out, target, g_ffn, g_fin, w_ffn_in_t, w_ffn_out)


GATE_SLOTS = 3


def _mix_bwd(dx1, o_sb, o_dl, gates, w_sb_up, w_dil_up, w_out):
    steps = TOK // TM_WIDE

    def body(dx1_ref, osb_ref, odl_ref, gates_hbm, wsb_ref, wdl_ref, wout_ref,
             dgate_ref, dysb_ref, dydl_ref, dosb_ref, dodl_ref, dsum_ref, gate_ring, ring_sems):
        s = pl.program_id(0)

        def fetch(step):
            start = step * TM_WIDE if isinstance(step, int) else pl.multiple_of(step * TM_WIDE, TM_WIDE)
            slot = step % GATE_SLOTS
            return pltpu.make_async_copy(gates_hbm.at[pl.ds(start, TM_WIDE)], gate_ring.at[slot],
                                         ring_sems.at[slot])

        @pl.when(s == 0)
        def _():
            for step in range(GATE_SLOTS - 1):
                fetch(step).start()

        @pl.when(s + (GATE_SLOTS - 1) < steps)
        def _():
            fetch(s + (GATE_SLOTS - 1)).start()

        fetch(s).wait()
        gate_ref = gate_ring.at[s % GATE_SLOTS]
        dmerged = _dot_nt(dx1_ref[...].astype(BF16), wout_ref[...])
        o_dl = odl_ref[...]
        y_sb = _dot(osb_ref[...], wsb_ref[...])
        y_dl = _dot(o_dl.astype(BF16), wdl_ref[...])
        s_sb = _sigmoid(gate_ref[:, :D_MODEL])
        s_dl = _sigmoid(gate_ref[:, D_MODEL:])
        dgate_ref[:, :D_MODEL] = (dmerged * y_sb * (s_sb * (1.0 - s_sb))).astype(BF16)
        dgate_ref[:, D_MODEL:] = (dmerged * y_dl * (s_dl * (1.0 - s_dl))).astype(BF16)
        dy_sb = (dmerged * s_sb).astype(BF16)
        dy_dl = (dmerged * s_dl).astype(BF16)
        dysb_ref[...] = dy_sb
        dydl_ref[...] = dy_dl
        dosb_ref[...] = _dot_nt(dy_sb, wsb_ref[...]).astype(BF16)
        do_dl = _dot_nt(dy_dl, wdl_ref[...])
        dodl_ref[...] = do_dl
        row = lax.broadcasted_iota(jnp.int32, (DIL_OUT, DIL_OUT), 0) // HEAD_DIM
        col = lax.broadcasted_iota(jnp.int32, (DIL_OUT, DIL_OUT), 1) // HEAD_DIM
        same_head = (row == col).astype(BF16)
        hi, lo = _split_bf16(do_dl * o_dl)
        dsum_ref[...] = _dot(hi, same_head) + _dot(lo, same_head)

    tile = lambda w: pl.BlockSpec((TM_WIDE, w), lambda i: (i, 0))
    return pl.pallas_call(
        body, name="mix_bwd", grid=(steps,),
        in_specs=[tile(D_MODEL), tile(SB_WIDTH), tile(DIL_OUT), pl.BlockSpec(memory_space=pl.ANY),
                  _resident((SB_WIDTH, D_MODEL)), _resident((DIL_OUT, D_MODEL)),
                  _resident((D_MODEL, D_MODEL))],
        out_specs=[tile(2 * D_MODEL), tile(D_MODEL), tile(D_MODEL), tile(SB_WIDTH), tile(DIL_OUT),
                   tile(DIL_OUT)],
        out_shape=[jax.ShapeDtypeStruct((TOK, 2 * D_MODEL), BF16),
                   jax.ShapeDtypeStruct((TOK, D_MODEL), BF16),
                   jax.ShapeDtypeStruct((TOK, D_MODEL), BF16),
                   jax.ShapeDtypeStruct((TOK, SB_WIDTH), BF16),
                   jax.ShapeDtypeStruct((TOK, DIL_OUT), F32),
                   jax.ShapeDtypeStruct((TOK, DIL_OUT), F32)],
        scratch_shapes=[pltpu.VMEM((GATE_SLOTS, TM_WIDE, 2 * D_MODEL), F32),
                        pltpu.SemaphoreType.DMA((GATE_SLOTS,))],
        compiler_params=_params(("arbitrary",)),
    )(dx1, o_sb, o_dl, gates, w_sb_up, w_dil_up, w_out)


def _proj_bwd(dproj, dx1, x, g, w_in_t, send=None):
    widths = [p.shape[1] for p in dproj]

    def body(*refs):
        dx1_ref, x_ref, g_ref, w_ref, dx_ref, dg_ref = refs[len(widths):]

        @pl.when(pl.program_id(0) == 0)
        def _():
            dg_ref[...] = jnp.zeros_like(dg_ref)

        du = jnp.zeros((TM, D_MODEL), F32)
        c0 = 0
        for dp_ref, w in zip(refs, widths):
            du = du + _dot(dp_ref[...].astype(BF16), w_ref[c0:c0 + w, :])
            c0 += w
        g_v = g_ref[...]
        n, r, _ = _rms_fwd(x_ref[...], g_v)
        dx, dg = _rms_bwd(du, n, r, g_v)
        dg_ref[...] += dg
        dx_ref[...] = dx1_ref[...] + dx

    tile = lambda w: pl.BlockSpec((TM, w), lambda i: (i, 0))
    return _call(
        body, send, name="proj_bwd", grid=(TOK // TM,),
        in_specs=[tile(w) for w in widths] + [tile(D_MODEL), tile(D_MODEL), _resident((1, D_MODEL)),
                                              _resident((IN_WIDTH, D_MODEL))],
        out_specs=[tile(D_MODEL), pl.BlockSpec((1, D_MODEL), lambda i: (0, 0))],
        out_shape=[jax.ShapeDtypeStruct((TOK, D_MODEL), F32),
                   jax.ShapeDtypeStruct((1, D_MODEL), F32)],
        scratch_shapes=[], semantics=("arbitrary",), operands=(*dproj, dx1, x, g, w_in_t))


def _atb_pieces(a, pieces, name, tm, tk=512):
    m = a.shape[1]
    widths = [p.shape[1] for p in pieces]
    n = sum(widths)
    nk = TOK // tk

    def body(a_ref, *refs):
        o_ref, acc_ref = refs[len(widths):]
        k = pl.program_id(1)

        @pl.when(k == 0)
        def _():
            acc_ref[...] = jnp.zeros_like(acc_ref)

        a_v = a_ref[...]
        c0 = 0
        for p_ref, w in zip(refs, widths):
            acc_ref[:, c0:c0 + w] += _dot_tn(a_v, p_ref[...].astype(BF16))
            c0 += w

        @pl.when(k == nk - 1)
        def _():
            for c0, w in _chunks(n):
                o_ref[c0:c0 + w, :] = acc_ref[:, c0:c0 + w].T.astype(BF16)

    return pl.pallas_call(
        body, name=name, grid=(m // tm, nk),
        in_specs=[pl.BlockSpec((tk, tm), lambda i, k: (k, i))]
                 + [pl.BlockSpec((tk, w), lambda i, k: (k, 0)) for w in widths],
        out_specs=pl.BlockSpec((n, tm), lambda i, k: (0, i)),
        out_shape=jax.ShapeDtypeStruct((n, m), BF16),
        scratch_shapes=[pltpu.VMEM((tm, n), F32)],
        compiler_params=_params(("parallel", "arbitrary")),
    )(a, *pieces)


def _atb_cols(a, b, name, cols, tk=512):
    m, other = a.shape[1], b.shape[1]

    def body(a_ref, b_ref, o_ref):
        acc = jnp.zeros(o_ref.shape, F32)
        for k0 in range(0, TOK, tk):
            acc = acc + _dot_tn(a_ref[k0:k0 + tk, :], b_ref[k0:k0 + tk, :])
        o_ref[...] = acc.astype(BF16)

    assert m % cols == 0
    return pl.pallas_call(
        body, name=name, grid=(m // cols,),
        in_specs=[pl.BlockSpec((TOK, cols), lambda r: (0, r)), _resident((TOK, other))],
        out_specs=pl.BlockSpec((cols, other), lambda r: (r, 0)),
        out_shape=jax.ShapeDtypeStruct((m, other), BF16),
        compiler_params=_params(("parallel",)),
    )(a, b)


def _atb(a, b, name, tm, tn, col_blocks=0, tk=512):
    m, n = a.shape[1], b.shape[1]
    nk = TOK // tk

    def body(a_ref, b_ref, o_ref, acc_ref):
        k = pl.program_id(2)

        @pl.when(k == 0)
        def _():
            acc_ref[...] = jnp.zeros_like(acc_ref)

        acc_ref[...] += _dot_tn(a_ref[...].astype(BF16), b_ref[...].astype(BF16))

        @pl.when(k == nk - 1)
        def _():
            if col_blocks:
                width = n // col_blocks
                for blk in range(col_blocks):
                    o_ref[blk] = acc_ref[:, blk * width:(blk + 1) * width].astype(BF16)
            else:
                o_ref[...] = acc_ref[...].astype(BF16)

    if col_blocks:
        out_spec = pl.BlockSpec((col_blocks, tm, n // col_blocks), lambda i, j, k: (0, i, 0))
        out_shape = jax.ShapeDtypeStruct((col_blocks, m, n // col_blocks), BF16)
    else:
        out_spec = pl.BlockSpec((tm, tn), lambda i, j, k: (i, j))
        out_shape = jax.ShapeDtypeStruct((m, n), BF16)
    return pl.pallas_call(
        body, name=name, grid=(m // tm, n // tn, nk),
        in_specs=[pl.BlockSpec((tk, tm), lambda i, j, k: (k, i)),
                  pl.BlockSpec((tk, tn), lambda i, j, k: (k, j))],
        out_specs=out_spec, out_shape=out_shape,
        scratch_shapes=[pltpu.VMEM((tm, tn), F32)],
        compiler_params=_params(("parallel", "parallel", "arbitrary")),
    )(a, b)


SB_PAIRS = SB_WIDTH // LANES


def _two_heads(v, lane0):
    zero = jnp.zeros_like(v)
    return jnp.where(lane0, v, zero), jnp.where(lane0, zero, v)


SB_QBLK = 256
N_SB_STEPS = SEQ // SB_QBLK


SB_KCHUNK = 2 * BLK
SB_ROWS = 2 * SB_QBLK
SB_DEAD = -104.0


def _log_keep(z):
    neg_z = -z
    return jnp.minimum(neg_z, 0.0) - jnp.log(1.0 + jnp.exp(jnp.minimum(z, neg_z)))


def _stack_heads(v, lane0):
    return jnp.concatenate(_two_heads(v, lane0), axis=0)


def _block_sums(v, tri, split=True):
    halves = (v[:, :BLK], v[:, BLK:])
    stacked = jnp.concatenate(halves, axis=0)
    if split:
        hi, lo = _split_bf16(stacked)
        prod = _dot(jnp.concatenate([hi, lo], axis=0), tri)
        tri_sum = prod[:2 * SB_ROWS] + prod[2 * SB_ROWS:]
    else:
        tri_sum = _dot(stacked.astype(BF16), tri)
    sums = tuple(jnp.sum(h, axis=1, keepdims=True) for h in halves)
    return (tri_sum[:SB_ROWS], tri_sum[SB_ROWS:]), sums


def _sb_diag_mask():
    row = lax.broadcasted_iota(jnp.int32, (SB_ROWS, SB_KCHUNK), 0)
    col = lax.broadcasted_iota(jnp.int32, (SB_ROWS, SB_KCHUNK), 1)
    return col < jnp.where(row >= SB_QBLK, row - SB_QBLK, row)


def _sb_fwd(qkv, send=None):
    def body(q_ref, k_ref, v_ref, o_ref):
        i = pl.program_id(2)
        krow = lax.broadcasted_iota(jnp.int32, (BLK, BLK), 0)
        kcol = lax.broadcasted_iota(jnp.int32, (BLK, BLK), 1)
        later = (krow > kcol).astype(BF16)
        lane0 = lax.broadcasted_iota(jnp.int32, (SB_QBLK, LANES), 1) < HEAD_DIM
        q2 = _stack_heads(q_ref[0] * QK_SCALE, lane0)

        def chunk(c, carry, causal):
            acc, run = carry
            off = pl.multiple_of(c * SB_KCHUNK, SB_KCHUNK)
            z = _dot_nt(q2, k_ref[0, pl.ds(off, SB_KCHUNK), :])
            log_keep = _log_keep(z)
            if causal is not None:
                log_keep = jnp.where(causal, log_keep, 0.0)
            suffix, sums = _block_sums(log_keep, later)
            log_after = jnp.concatenate([suffix[0] + (run + sums[1]), suffix[1] + run], axis=1)
            a = jnp.exp(log_keep + z + log_after)
            if causal is not None:
                a = jnp.where(causal, a, 0.0)
            acc = acc + _dot(a.astype(BF16), v_ref[0, pl.ds(off, SB_KCHUNK), :])
            return acc, run + (sums[0] + sums[1])

        acc, run = chunk(i, (jnp.zeros((SB_ROWS, LANES), F32), jnp.zeros((SB_ROWS, 1), F32)), _sb_diag_mask())

        def some_alive(run):
            return (jnp.max(run) > SB_DEAD).astype(jnp.int32)

        def trip(state):
            t, _, acc, run = state
            acc, run = chunk(i - 1 - t, (acc, run), None)
            return t + 1, some_alive(run), acc, run

        _, _, acc, _ = lax.while_loop(lambda s: jnp.logical_and(s[0] < i, s[1] > 0), trip,
                                      (jnp.int32(0), some_alive(run), acc, run))
        o_ref[0] = jnp.where(lane0, acc[:SB_QBLK], acc[SB_QBLK:]).astype(BF16)

    blk = pl.BlockSpec((1, SB_QBLK, LANES), lambda b, h, i: (b, i, h))
    return _call(
        body, send, name="sb_fwd", grid=(B_LOC, SB_PAIRS, N_SB_STEPS),
        in_specs=[blk,
                  pl.BlockSpec((1, SEQ, LANES), lambda b, h, i: (b, 0, SB_PAIRS + h)),
                  pl.BlockSpec((1, SEQ, LANES), lambda b, h, i: (b, 0, 2 * SB_PAIRS + h))],
        out_specs=[blk], out_shape=[jax.ShapeDtypeStruct((B_LOC, SEQ, SB_WIDTH), BF16)],
        scratch_shapes=[], semantics=("parallel", "parallel", "arbitrary"), operands=(qkv, qkv, qkv))


def _sb_bwd(qkv, d_o, send=None):
    def body(q_ref, k_ref, v_ref, do_ref, dq_ref, dk_ref, dv_ref, dk_acc, dv_acc, z_scr, keep_scr):
        i = pl.program_id(2)
        krow = lax.broadcasted_iota(jnp.int32, (BLK, BLK), 0)
        kcol = lax.broadcasted_iota(jnp.int32, (BLK, BLK), 1)
        upto = (krow <= kcol).astype(BF16)
        earlier = (krow < kcol).astype(BF16)
        lane0 = lax.broadcasted_iota(jnp.int32, (SB_QBLK, LANES), 1) < HEAD_DIM
        q2 = _stack_heads(q_ref[0] * QK_SCALE, lane0)
        do2 = _stack_heads(do_ref[0], lane0)

        def keep_sum(c, causal):
            off = pl.multiple_of(c * SB_KCHUNK, SB_KCHUNK)
            z = _dot_nt(q2, k_ref[0, pl.ds(off, SB_KCHUNK), :])
            log_keep = _log_keep(z)
            if causal is not None:
                log_keep = jnp.where(causal, log_keep, 0.0)
            z_scr[c] = z
            keep_scr[c] = log_keep
            return jnp.sum(log_keep, axis=1, keepdims=True)

        def some_alive(run):
            return (jnp.max(run) > SB_DEAD).astype(jnp.int32)

        def scan(state):
            t, _, run = state
            run = run + keep_sum(i - 1 - t, None)
            return t + 1, some_alive(run), run

        diag_sum = keep_sum(i, _sb_diag_mask())
        walked, _, tot2 = lax.while_loop(lambda s: jnp.logical_and(s[0] < i, s[1] > 0), scan,
                                         (jnp.int32(0), some_alive(diag_sum), diag_sum))
        first = i - walked

        @pl.when(i == 0)
        def _():
            dk_acc[...] = jnp.zeros_like(dk_acc)
            dv_acc[...] = jnp.zeros_like(dv_acc)

        def chunk(c, carry, causal):
            dq, pre_keep, pre_e = carry
            off = pl.multiple_of(c * SB_KCHUNK, SB_KCHUNK)
            k_c = k_ref[0, pl.ds(off, SB_KCHUNK), :]
            v_c = v_ref[0, pl.ds(off, SB_KCHUNK), :]
            d_a = _dot_nt(do2, v_c)
            log_keep = keep_scr[c]
            log_beta = log_keep + z_scr[c]
            prefix, sums = _block_sums(log_keep, upto)
            inclusive = jnp.concatenate([prefix[0], prefix[1] + sums[0]], axis=1)
            a = jnp.exp(log_beta + ((tot2 - pre_keep) - inclusive))
            if causal is not None:
                a = jnp.where(causal, a, 0.0)
            e = d_a * a
            e_prefix, e_sums = _block_sums(e, earlier, split=False)
            before = jnp.concatenate([e_prefix[0] + pre_e, e_prefix[1] + (pre_e + e_sums[0])], axis=1)
            dz = e - (e + before) * jnp.exp(log_beta)
            if causal is not None:
                dz = jnp.where(causal, dz, 0.0)
            dz = dz.astype(BF16)
            dq = dq + _dot(dz, k_c)
            dk_acc[pl.ds(off, SB_KCHUNK), :] += _dot_tn(dz, q2)
            dv_acc[pl.ds(off, SB_KCHUNK), :] += _dot_tn(a.astype(BF16), do2)
            return dq, pre_keep + (sums[0] + sums[1]), pre_e + (e_sums[0] + e_sums[1])

        zero_col = jnp.zeros((SB_ROWS, 1), F32)
        carry = lax.fori_loop(first, i, lambda t, c: chunk(t, c, None),
                              (jnp.zeros((SB_ROWS, LANES), F32), zero_col, zero_col))
        dq, _, _ = chunk(i, carry, _sb_diag_mask())
        dq_ref[0] = (jnp.where(lane0, dq[:SB_QBLK], dq[SB_QBLK:]) * QK_SCALE).astype(BF16)

        @pl.when(i == N_SB_STEPS - 1)
        def _():
            dk_ref[0] = dk_acc[...].astype(BF16)
            dv_ref[0] = dv_acc[...].astype(BF16)

    blk = pl.BlockSpec((1, SB_QBLK, LANES), lambda b, h, i: (b, i, h))
    whole = lambda c: pl.BlockSpec((1, SEQ, LANES), lambda b, h, i: (b, 0, c * SB_PAIRS + h))
    out = jax.ShapeDtypeStruct((B_LOC, SEQ, SB_WIDTH), BF16)
    return _call(
        body, send, name="sb_bwd", grid=(B_LOC, SB_PAIRS, N_SB_STEPS),
        in_specs=[blk, whole(1), whole(2), blk],
        out_specs=[blk, whole(0), whole(0)],
        out_shape=[out, out, out],
        scratch_shapes=[pltpu.VMEM((SEQ, LANES), F32), pltpu.VMEM((SEQ, LANES), F32),
                        pltpu.VMEM((N_SB_STEPS, SB_ROWS, SB_KCHUNK), F32),
                        pltpu.VMEM((N_SB_STEPS, SB_ROWS, SB_KCHUNK), F32)],
        semantics=("parallel", "parallel", "arbitrary"), operands=(qkv, qkv, qkv, d_o))


DIL_GROUPS = len(DIL_PAIRS)
DIL_QBLOCKS = SEQ // BLK


def _residue_rows(j, dilation):
    length = SEQ // dilation
    return pl.ds(j, length, stride=dilation) if dilation > 1 else pl.ds(0, length)


def _gather_residues(src_ref, dst_ref, dst_off, dilation, scale=None):
    length = SEQ // dilation
    for j in range(dilation):
        v = src_ref[_residue_rows(j, dilation), :]
        if scale is not None:
            v = v * scale
        dst_ref[dst_off + j * length:dst_off + (j + 1) * length, :] = v.astype(dst_ref.dtype)


def _scatter_residues(src_ref, src_off, dst_ref, dilation):
    length = SEQ // dilation
    for j in range(dilation):
        dst_ref[_residue_rows(j, dilation), :] = (
            src_ref[src_off + j * length:src_off + (j + 1) * length, :].astype(dst_ref.dtype))


def _dil_geometry(group, pair):
    dilation = DIL_PAIRS[group][1]
    row = lax.broadcasted_iota(jnp.int32, (2 * BLK, 2 * BLK), 0)
    col = lax.broadcasted_iota(jnp.int32, (2 * BLK, 2 * BLK), 1)
    second = row >= BLK
    steps = BLK + jnp.where(second, row - BLK, row) - col
    coef = -ALIBI_MAX_BIAS / DIL_HEADS * math.log(2.0)
    first_head = float(4 * group + 1) + 2.0 * pair.astype(F32)
    slope = jnp.exp(coef * (first_head + jnp.where(second, 1.0, 0.0)))
    bias = slope * (steps * dilation).astype(F32)
    valid = jnp.logical_and(steps >= 0, steps <= BLK)
    return bias, valid, col >= BLK


def _dil_tile_scores(q2, kk, geometry, has_prev):
    bias, valid, own = geometry
    ok = jnp.logical_and(valid, jnp.logical_or(own, has_prev))
    return jnp.where(ok, _dot_nt(q2, kk) - bias, NEG_BIG)


def _head_col(v, lane_mask):
    return jnp.max(jnp.where(lane_mask, v, NEG_BIG), axis=1, keepdims=True)


def _dil_fwd(qkv, send=None):
    def body(*refs):
        ins, (o_ref, lse_ref), (qs, ks, vs, o_res, lse_res) = refs[:9], refs[9:11], refs[11:16]
        o_grp, lse_grp = refs[16:19], refs[19:22]
        pair = pl.program_id(1)
        lane0 = lax.broadcasted_iota(jnp.int32, (BLK, LANES), 1) < HEAD_DIM
        ks[0:BLK, :] = jnp.zeros((BLK, LANES), BF16)
        vs[0:BLK, :] = jnp.zeros((BLK, LANES), BF16)
        for grp, (_, dilation) in enumerate(DIL_PAIRS):
            q_ref, k_ref, v_ref = ins[3 * grp:3 * grp + 3]
            per_residue = DIL_QBLOCKS // dilation
            _gather_residues(q_ref, qs, 0, dilation, QK_SCALE)
            _gather_residues(k_ref, ks, BLK, dilation)
            _gather_residues(v_ref, vs, BLK, dilation)
            geometry = _dil_geometry(grp, pair)

            def step(blk, _):
                off = pl.multiple_of(blk * BLK, BLK)
                q2 = _stack_heads(qs[pl.ds(off, BLK), :], lane0)
                s = _dil_tile_scores(q2, ks[pl.ds(off, 2 * BLK), :], geometry, blk % per_residue != 0)
                m = jnp.max(s, axis=1, keepdims=True)
                p = jnp.exp(s - m)
                den = jnp.sum(p, axis=1, keepdims=True)
                out = _dot(p.astype(BF16), vs[pl.ds(off, 2 * BLK), :]) / den
                lse = m + jnp.log(den)
                o_res[pl.ds(off, BLK), :] = jnp.where(lane0, out[:BLK], out[BLK:])
                lse_res[pl.ds(off, BLK), :] = jnp.where(lane0, lse[:BLK], lse[BLK:])
                return 0

            lax.fori_loop(0, DIL_QBLOCKS, step, 0, unroll=8)
            _scatter_residues(o_res, 0, o_grp[grp], dilation)
            _scatter_residues(lse_res, 0, lse_grp[grp], dilation)

        for r0 in range(0, SEQ, 2 * BLK):
            rows = slice(r0, r0 + 2 * BLK)
            ls = [lse_grp[g][rows, :] for g in range(DIL_GROUPS)]
            m = jnp.maximum(jnp.maximum(ls[0], ls[1]), ls[2])
            w = [jnp.exp(l - m) for l in ls]
            den = w[0] + w[1] + w[2]
            o_ref[rows, :] = (w[0] * o_grp[0][rows, :] + w[1] * o_grp[1][rows, :] + w[2] * o_grp[2][rows, :]) / den
            lse_ref[rows, :] = m + jnp.log(den)

    def col(part, grp):
        return pl.BlockSpec((None, SEQ, LANES), lambda b, p: (b, 0, 6 * part + 2 * grp + p))

    out_spec = pl.BlockSpec((None, SEQ, LANES), lambda b, p: (b, 0, p))
    out = jax.ShapeDtypeStruct((B_LOC, SEQ, DIL_OUT), F32)
    return _call(
        body, send, name="dil_fwd", grid=(B_LOC, DIL_OUT // LANES),
        in_specs=[col(part, grp) for grp in range(DIL_GROUPS) for part in range(3)],
        out_specs=[out_spec, out_spec], out_shape=[out, out],
        scratch_shapes=[pltpu.VMEM((SEQ, LANES), BF16), pltpu.VMEM((SEQ + BLK, LANES), BF16),
                        pltpu.VMEM((SEQ + BLK, LANES), BF16), pltpu.VMEM((SEQ, LANES), F32),
                        pltpu.VMEM((SEQ, LANES), F32)] + [pltpu.VMEM((SEQ, LANES), F32)] * (2 * DIL_GROUPS),
        semantics=("parallel", "parallel"), operands=[qkv] * 9)


def _dil_bwd(qkv, d_o, lse, dsum, send=None):
    def body(*refs):
        ins, (do_ref, lse_ref, dsum_ref), outs = refs[:9], refs[9:12], refs[12:21]
        qs, ks, vs, dos, lse_res, dsum_res, dq_res, dk_acc, dv_acc = refs[21:]
        pair = pl.program_id(1)
        lane0 = lax.broadcasted_iota(jnp.int32, (BLK, LANES), 1) < HEAD_DIM
        lane1 = jnp.logical_not(lane0)
        ks[0:BLK, :] = jnp.zeros((BLK, LANES), BF16)
        vs[0:BLK, :] = jnp.zeros((BLK, LANES), BF16)
        for grp, (_, dilation) in enumerate(DIL_PAIRS):
            q_ref, k_ref, v_ref = ins[3 * grp:3 * grp + 3]
            dq_ref, dk_ref, dv_ref = outs[3 * grp:3 * grp + 3]
            per_residue = DIL_QBLOCKS // dilation
            _gather_residues(q_ref, qs, 0, dilation, QK_SCALE)
            _gather_residues(k_ref, ks, BLK, dilation)
            _gather_residues(v_ref, vs, BLK, dilation)
            _gather_residues(do_ref, dos, 0, dilation)
            _gather_residues(lse_ref, lse_res, 0, dilation)
            _gather_residues(dsum_ref, dsum_res, 0, dilation)
            dk_acc[...] = jnp.zeros_like(dk_acc)
            dv_acc[...] = jnp.zeros_like(dv_acc)
            geometry = _dil_geometry(grp, pair)

            def step(blk, _):
                off = pl.multiple_of(blk * BLK, BLK)
                q2 = _stack_heads(qs[pl.ds(off, BLK), :], lane0)
                do2 = _stack_heads(dos[pl.ds(off, BLK), :], lane0)
                kk = ks[pl.ds(off, 2 * BLK), :]
                vv = vs[pl.ds(off, 2 * BLK), :]
                lse_blk = lse_res[pl.ds(off, BLK), :]
                dsum_blk = dsum_res[pl.ds(off, BLK), :]
                lse2 = jnp.concatenate([_head_col(lse_blk, lane0), _head_col(lse_blk, lane1)], axis=0)
                dsum2 = jnp.concatenate([_head_col(dsum_blk, lane0), _head_col(dsum_blk, lane1)], axis=0)
                s = _dil_tile_scores(q2, kk, geometry, blk % per_residue != 0)
                p = jnp.exp(s - lse2)
                ds = (p * (_dot_nt(do2, vv) - dsum2)).astype(BF16)
                dq2 = _dot(ds, kk)
                dq_res[pl.ds(off, BLK), :] = jnp.where(lane0, dq2[:BLK], dq2[BLK:]) * QK_SCALE
                dk_acc[pl.ds(off, 2 * BLK), :] += _dot_tn(ds, q2)
                dv_acc[pl.ds(off, 2 * BLK), :] += _dot_tn(p.astype(BF16), do2)
                return 0

            lax.fori_loop(0, DIL_QBLOCKS, step, 0, unroll=8)
            _scatter_residues(dq_res, 0, dq_ref, dilation)
            _scatter_residues(dk_acc, BLK, dk_ref, dilation)
            _scatter_residues(dv_acc, BLK, dv_ref, dilation)

    def col(part, grp):
        return pl.BlockSpec((None, SEQ, LANES), lambda b, p: (b, 0, 6 * part + 2 * grp + p))

    slot = pl.BlockSpec((None, SEQ, LANES), lambda b, p: (b, 0, p))
    out = jax.ShapeDtypeStruct((B_LOC, SEQ, DIL_OUT), F32)
    return _call(
        body, send, name="dil_bwd", grid=(B_LOC, DIL_OUT // LANES),
        in_specs=[col(part, grp) for grp in range(DIL_GROUPS) for part in range(3)] + [slot] * 3,
        out_specs=[slot] * 9, out_shape=[out] * 9,
        scratch_shapes=[pltpu.VMEM((SEQ, LANES), BF16), pltpu.VMEM((SEQ + BLK, LANES), BF16),
                        pltpu.VMEM((SEQ + BLK, LANES), BF16), pltpu.VMEM((SEQ, LANES), BF16),
                        pltpu.VMEM((SEQ, LANES), F32), pltpu.VMEM((SEQ, LANES), F32),
                        pltpu.VMEM((SEQ, LANES), F32), pltpu.VMEM((SEQ + BLK, LANES), F32),
                        pltpu.VMEM((SEQ + BLK, LANES), F32)],
        semantics=("parallel", "parallel"), operands=[qkv] * 9 + [d_o, lse, dsum])


def _peers():
    x, y, c = lax.axis_index("x"), lax.axis_index("y"), lax.axis_index("c")
    me = 4 * x + 2 * y + c
    peers = []
    for mask in range(1, N_DEV):
        px = 1 - x if mask & 4 else x
        py = 1 - y if mask & 2 else y
        pc = 1 - c if mask & 1 else c
        peers.append(((px, py, pc), 4 * px + 2 * py + pc))
    return me, peers


def _all_gather(shard, name):
    rows, cols = shard.shape
    by_rows = rows % 32 == 0

    def body(src_ref, out_ref, send_sems, recv_sems, local_sem):
        x, y, c = lax.axis_index("x"), lax.axis_index("y"), lax.axis_index("c")
        me, sibling = (x, y, c), (x, y, 1 - c)
        x_chip, y_chip, across = (1 - x, y), (x, 1 - y), (1 - x, 1 - y)

        def slot(block, half=None):
            ref = out_ref.at[4 * block[0] + 2 * block[1] + block[2]]
            if half is None:
                return ref
            return ref.at[pl.ds(half * (rows // 2), rows // 2)] if by_rows else \
                ref.at[:, pl.ds(half * (cols // 2), cols // 2)]

        def copy(k, block, to, src=None, half=None):
            return pltpu.make_async_remote_copy(
                src_ref=slot(block, half) if src is None else src, dst_ref=slot(block, half),
                send_sem=send_sems.at[k], recv_sem=recv_sems.at[k], device_id=to,
                device_id_type=pl.DeviceIdType.MESH)

        mine = pltpu.make_async_copy(src_ref, slot(me), local_sem)
        mine.start()
        sent = [copy(0, me, sibling, src=src_ref), copy(1, me, (*x_chip, c), src=src_ref),
                copy(2, me, (*y_chip, c), src=src_ref)]
        for cp in sent:
            cp.start()

        def arrived(k, block, half=None):
            copy(k, block, me, half=half).wait_recv()
            onward = {1: [copy(3, block, (*y_chip, c), half=0), copy(5, block, sibling)],
                      2: [copy(4, block, (*x_chip, c), half=1), copy(6, block, sibling)],
                      3: [copy(7, block, sibling, half=0)],
                      4: [copy(8, block, sibling, half=1)]}.get(k, [])
            for cp in onward:
                cp.start()
            sent.extend(onward)

        arrived(1, (*x_chip, c))
        arrived(2, (*y_chip, c))
        arrived(3, (*across, c), half=0)
        arrived(4, (*across, c), half=1)
        arrived(0, sibling)
        arrived(5, (*x_chip, 1 - c))
        arrived(6, (*y_chip, 1 - c))
        arrived(7, (*across, 1 - c), half=0)
        arrived(8, (*across, 1 - c), half=1)
        for cp in sent:
            cp.wait_send()
        mine.wait()

    n_sems = 9
    return pl.pallas_call(
        body, name=name,
        in_specs=[pl.BlockSpec(memory_space=pl.ANY)],
        out_specs=pl.BlockSpec(memory_space=pl.ANY),
        out_shape=jax.ShapeDtypeStruct((N_DEV,) + shard.shape, shard.dtype),
        scratch_shapes=[pltpu.SemaphoreType.DMA((n_sems,)), pltpu.SemaphoreType.DMA((n_sems,)),
                        pltpu.SemaphoreType.DMA],
    )(shard)


def _call(body, send, *, name, grid, in_specs, out_specs, out_shape, scratch_shapes, semantics, operands):
    if send is None:
        return pl.pallas_call(
            body, name=name, grid=grid, in_specs=in_specs, out_specs=out_specs, out_shape=out_shape,
            scratch_shapes=scratch_shapes, compiler_params=_params(semantics))(*operands), []
    srcs, kinds = [s for s, _ in send], [k for _, k in send]
    n, n_in, n_out, n_scr = len(srcs), len(in_specs), len(out_specs), len(scratch_shapes)
    steps = math.prod(grid)
    relay_step = (13 * steps) // 16

    def plan(refs):
        src_refs, land_refs = refs[n_in:n_in + n], refs[n_in + n + n_out:n_in + 2 * n + n_out]
        send_sems, recv_sems, local_sems = refs[-3:]
        x, y, c = lax.axis_index("x"), lax.axis_index("y"), lax.axis_index("c")
        me, peers = _peers()
        first, relayed_in, relayed_out, arrivals, sends, own = [], [], [], [], [], []
        for a, kind in enumerate(kinds):
            def copy(k, src, dst_slot, to):
                return pltpu.make_async_remote_copy(
                    src_ref=src, dst_ref=land_refs[a].at[dst_slot], send_sem=send_sems.at[a * (N_DEV - 1) + k],
                    recv_sem=recv_sems.at[a * (N_DEV - 1) + k], device_id=to, device_id_type=pl.DeviceIdType.MESH)

            if kind == "gather_by_chip":
                idx = lambda px, py, pc: 4 * px + 2 * py + pc
                chips = [(1 - x, y), (x, 1 - y), (1 - x, 1 - y)]
                mine = [copy(0, src_refs[a], me, (x, y, 1 - c))]
                arrivals.append(copy(0, src_refs[a], idx(x, y, 1 - c), (x, y, 1 - c)))
                for j, (px, py) in enumerate(chips):
                    mine.append(copy(1 + j, src_refs[a], me, (px, py, c)))
                    relayed_in.append(copy(1 + j, src_refs[a], idx(px, py, c), (px, py, c)))
                    relayed_out.append(copy(4 + j, land_refs[a].at[idx(px, py, c)], idx(px, py, c), (x, y, 1 - c)))
                    arrivals.append(copy(4 + j, src_refs[a], idx(px, py, 1 - c), (x, y, 1 - c)))
                first += mine
                sends += mine + relayed_out[-3:]
                own.append(pltpu.make_async_copy(src_refs[a], land_refs[a].at[me], local_sems.at[a]))
            elif kind == "scatter_by_chip":
                for k, (px, py) in enumerate([(1 - x, y), (x, 1 - y), (1 - x, 1 - y)]):
                    first.append(copy(k, src_refs[a].at[2 * px + py], 2 * x + y, (px, py, c)))
                    arrivals.append(copy(k, src_refs[a].at[2 * px + py], 2 * px + py, (px, py, c)))
                sends += first[-3:]
                own.append(pltpu.make_async_copy(src_refs[a].at[2 * x + y], land_refs[a].at[2 * x + y],
                                                 local_sems.at[a]))
            else:
                part = (lambda i: src_refs[a].at[i]) if kind == "scatter" else (lambda i: src_refs[a])
                for k, (peer, peer_idx) in enumerate(peers):
                    first.append(copy(k, part(peer_idx), me, peer))
                    arrivals.append(copy(k, part(peer_idx), peer_idx, peer))
                sends += first[-(N_DEV - 1):]
                own.append(pltpu.make_async_copy(part(me), land_refs[a].at[me], local_sems.at[a]))
        return first, relayed_in, relayed_out, arrivals, sends, own

    def wrapped(*refs):
        step = 0
        for axis, size in enumerate(grid):
            step = step * size + pl.program_id(axis)

        @pl.when(step == 0)
        def _():
            first, _, _, _, _, own = plan(refs)
            for cp in first + own:
                cp.start()

        if "gather_by_chip" in kinds:
            @pl.when(step == relay_step)
            def _():
                _, relayed_in, relayed_out, _, _, _ = plan(refs)
                for cp_in, cp_out in zip(relayed_in, relayed_out):
                    cp_in.wait_recv()
                    cp_out.start()

        body(*refs[:n_in], *refs[n_in + n:n_in + n + n_out], *refs[n_in + 2 * n + n_out:n_in + 2 * n + n_out + n_scr])

        @pl.when(step == steps - 1)
        def _():
            _, _, _, arrivals, sends, own = plan(refs)
            for cp in arrivals:
                cp.wait_recv()
            for cp in sends:
                cp.wait_send()
            for cp in own:
                cp.wait()

    anywhere = pl.BlockSpec(memory_space=pl.ANY)
    lands = [jax.ShapeDtypeStruct((N_DEV // 2 if k == "scatter_by_chip" else N_DEV,) + s.shape[-2:], s.dtype)
             for s, k in send]
    out = pl.pallas_call(
        wrapped, name=name, grid=grid,
        in_specs=list(in_specs) + [anywhere] * n, out_specs=list(out_specs) + [anywhere] * n,
        out_shape=list(out_shape) + lands,
        scratch_shapes=list(scratch_shapes) + [pltpu.SemaphoreType.DMA((n * (N_DEV - 1),)),
                                               pltpu.SemaphoreType.DMA((n * (N_DEV - 1),)),
                                               pltpu.SemaphoreType.DMA((n,))],
        compiler_params=_params(("arbitrary",) * len(grid)),
    )(*operands, *srcs)
    return out[:n_out], list(out[n_out:])


def _pair_swap(blocks):
    def body(src_ref, out_ref, send_sems, recv_sems):
        x, y, c = lax.axis_index("x"), lax.axis_index("y"), lax.axis_index("c")
        copies = [pltpu.make_async_remote_copy(
            src_ref=src_ref.at[2 * chip + (1 - c)], dst_ref=out_ref.at[chip], send_sem=send_sems.at[chip],
            recv_sem=recv_sems.at[chip], device_id=(x, y, 1 - c), device_id_type=pl.DeviceIdType.MESH)
            for chip in range(N_DEV // 2)]
        for cp in copies:
            cp.start()
        for cp in copies:
            cp.wait()

    return pl.pallas_call(
        body, name="pair_swap_grad_w_in",
        in_specs=[pl.BlockSpec(memory_space=pl.ANY)], out_specs=pl.BlockSpec(memory_space=pl.ANY),
        out_shape=jax.ShapeDtypeStruct((N_DEV // 2,) + blocks.shape[1:], blocks.dtype),
        scratch_shapes=[pltpu.SemaphoreType.DMA((N_DEV // 2,)), pltpu.SemaphoreType.DMA((N_DEV // 2,))],
    )(blocks)


def _pair_sum(blocks, swapped, core):
    _, rows, cols = swapped.shape
    tile_rows = _row_tile(rows)

    def body(core_ref, mine_ref, theirs_ref, o_ref):
        o_ref[...] = (mine_ref[...].astype(F32) + theirs_ref[...].astype(F32)).astype(o_ref.dtype)

    return pl.pallas_call(
        body, name="pair_sum_grad_w_in",
        grid_spec=pltpu.PrefetchScalarGridSpec(
            num_scalar_prefetch=1, grid=(N_DEV // 2, rows // tile_rows),
            in_specs=[pl.BlockSpec((None, tile_rows, cols), lambda j, i, core_ref: (2 * j + core_ref[0], i, 0)),
                      pl.BlockSpec((None, tile_rows, cols), lambda j, i, core_ref: (j, i, 0))],
            out_specs=pl.BlockSpec((None, tile_rows, cols), lambda j, i, core_ref: (j, i, 0))),
        out_shape=jax.ShapeDtypeStruct(swapped.shape, swapped.dtype),
        compiler_params=_params(("parallel", "parallel")),
    )(core, blocks, swapped)


def _sum_in_device_order(land_ref):
    acc = land_ref[0].astype(F32)
    for j in range(1, land_ref.shape[0]):
        acc = acc + land_ref[j].astype(F32)
    return acc


def _adam_math(w, g, m, v):
    c1 = 1.0 - ADAM_B1 ** ADAM_STEP
    c2 = 1.0 - ADAM_B2 ** ADAM_STEP
    m_new = ADAM_B1 * m + (1.0 - ADAM_B1) * g
    v_new = ADAM_B2 * v + (1.0 - ADAM_B2) * (g * g)
    delta = -ADAM_LR * ((m_new / c1) / (jnp.sqrt(v_new / c2) + ADAM_EPS) + ADAM_WD * w)
    return delta, m_new, v_new


def _row_tile(rows):
    return max(t for t in range(16, 385, 16) if rows % t == 0) if rows % 16 == 0 else rows


def _sum_update(land, w, m, v, name):
    slots, rows, cols = land.shape
    tile_rows = _row_tile(rows)

    def body(land_ref, w_ref, m_ref, v_ref, g_ref, d_ref, nm_ref, nv_ref):
        g = _sum_in_device_order(land_ref)
        g_ref[...] = g
        d_ref[...], nm_ref[...], nv_ref[...] = _adam_math(w_ref[...], g, m_ref[...], v_ref[...])

    tile = pl.BlockSpec((None, tile_rows, cols), lambda i: (0, i, 0))
    out = jax.ShapeDtypeStruct((1, rows, cols), F32)
    return pl.pallas_call(
        body, name=name, grid=(rows // tile_rows,),
        in_specs=[pl.BlockSpec((slots, tile_rows, cols), lambda i: (0, i, 0)), tile, tile, tile],
        out_specs=[tile] * 4, out_shape=[out] * 4,
        compiler_params=_params(("parallel",)),
    )(land, w, m, v)


def _to_bf16(arrays):
    def body(*refs):
        for src_ref, dst_ref in zip(refs[:len(arrays)], refs[len(arrays):]):
            dst_ref[...] = src_ref[...].astype(BF16)

    whole = [pl.BlockSpec(a.shape, lambda i: (0, 0)) for a in arrays]
    return pl.pallas_call(
        body, name="shards_to_bf16", grid=(1,), in_specs=whole, out_specs=whole,
        out_shape=[jax.ShapeDtypeStruct(a.shape, BF16) for a in arrays],
        compiler_params=_params(("arbitrary",)),
    )(*arrays)


N_GAINS = 3


def _sum_update_gains(land, ws, ms, vs):
    def body(land_ref, *refs):
        w_refs, m_refs, v_refs = (refs[k * N_GAINS:(k + 1) * N_GAINS] for k in range(3))
        loss_ref, out_refs = refs[3 * N_GAINS], refs[3 * N_GAINS + 1:]
        rows = _sum_in_device_order(land_ref)
        loss_ref[...] = rows[N_GAINS:N_GAINS + 1]
        for k in range(N_GAINS):
            g = rows[k:k + 1]
            g_ref, d_ref, nm_ref, nv_ref = out_refs[4 * k:4 * k + 4]
            g_ref[...] = g
            d_ref[...], nm_ref[...], nv_ref[...] = _adam_math(w_refs[k][...], g, m_refs[k][...], v_refs[k][...])

    row = pl.BlockSpec((1, D_MODEL), lambda i: (0, 0))
    out = jax.ShapeDtypeStruct((1, D_MODEL), F32)
    n_out = 1 + 4 * N_GAINS
    return pl.pallas_call(
        body, name="update_gains", grid=(1,),
        in_specs=[pl.BlockSpec(land.shape, lambda i: (0, 0, 0))] + [row] * (3 * N_GAINS),
        out_specs=[row] * n_out, out_shape=[out] * n_out,
    )(land, *ws, *ms, *vs)


GROUP_FFN = ("w_ffn_in", "w_ffn_out")
GROUP_MIX = ("w_sb_up", "w_dil_up", "w_out")
COL_SHARDED = ("w_in", "w_sb_up", "w_dil_up", "w_ffn_in")
TRANSPOSED = ("w_in", "w_ffn_in")


def _full_from_shards(name, slots):
    _, r, c = slots.shape
    if name in TRANSPOSED:
        return slots.reshape(N_DEV * r, c).T
    if name in COL_SHARDED:
        return slots.transpose(1, 0, 2).reshape(r, N_DEV * c)
    return slots.reshape(N_DEV * r, c)


def _row_shards(full):
    rows, cols = full.shape
    return full.reshape(N_DEV, rows // N_DEV, cols)


def _local_step(x, target, g_mix, g_ffn, g_fin, w_in_t, shards=None, rest=None):
    gather = lambda names, kind: None if shards is None else [(shards[n], kind) for n in names]
    scatter = lambda blocks: None if shards is None else [(t, "scatter") for t in blocks]
    landed = lambda blocks, lands: lands if lands else blocks

    w = {"w_in": w_in_t}
    if shards is None:
        w.update(rest)
        w["w_ffn_in"] = rest["w_ffn_in"].T
    (qkv_sb, qkv_dl, gates, u), _ = _norm_proj(x, g_mix, w["w_in"])
    qkv_sb = qkv_sb.reshape(B_LOC, SEQ, 3 * SB_WIDTH)
    qkv_dl = qkv_dl.reshape(B_LOC, SEQ, 3 * DIL_WIDTH)
    (o_sb,), lands = _sb_fwd(qkv_sb, gather(GROUP_FFN, "gather_by_chip"))
    if lands:
        w["w_ffn_in"], w["w_ffn_out"] = lands[0].reshape(2 * D_FF, D_MODEL), _full_from_shards("w_ffn_out", lands[1])
    o_sb = o_sb.reshape(TOK, SB_WIDTH)
    (o_dl, lse), lands = _dil_fwd(qkv_dl, gather(GROUP_MIX, "gather"))
    w.update({n: _full_from_shards(n, t) for n, t in zip(GROUP_MIX, lands)})
    o_dl = o_dl.reshape(TOK, DIL_OUT)

    loss, dx1, merged, u2, act, dh, dx2, dg_fin, dg_ffn = _mix_ffn_fwd_bwd(
        x, o_sb, o_dl, gates, w["w_sb_up"], w["w_dil_up"], w["w_out"], target, g_ffn, g_fin,
        w["w_ffn_in"], w["w_ffn_out"])
    dgates, dy_sb, dy_dl, do_sb, do_dl, dsum = _mix_bwd(dx1, o_sb, o_dl, gates, w["w_sb_up"], w["w_dil_up"], w["w_out"])
    blocks = {
        "w_sb_up": _atb(o_sb, dy_sb, "grad_w_sb_up", SB_WIDTH, D_MODEL, col_blocks=N_DEV),
        "w_dil_up": _atb(o_dl, dy_dl, "grad_w_dil_up", DIL_OUT, D_MODEL, col_blocks=N_DEV),
        "w_out": _row_shards(_atb(merged, dx1, "grad_w_out", D_MODEL, D_MODEL)),
        "w_ffn_in": _row_shards(_atb_cols(dh, u2, "grad_w_ffn_in", 512)),
        "w_ffn_out": _row_shards(_atb_cols(act, dx2, "grad_w_ffn_out", 256)),
    }
    grads = {}

    early, late = GROUP_FFN, GROUP_MIX
    early_blocks = [blocks[n] for n in early]
    (dq_sb, dk_sb, dv_sb), lands = _sb_bwd(qkv_sb, do_sb.reshape(B_LOC, SEQ, SB_WIDTH), scatter(early_blocks))
    grads.update(zip(early, landed(early_blocks, lands)))
    as_batch = lambda t: t.reshape(B_LOC, SEQ, DIL_OUT)
    late_blocks = [blocks[n] for n in late]
    d_dl, lands = _dil_bwd(qkv_dl, as_batch(do_dl), lse, as_batch(dsum), scatter(late_blocks))
    grads.update(zip(late, landed(late_blocks, lands)))
    flat = lambda t: t.reshape(TOK, -1)
    dproj = ([flat(dq_sb), flat(dk_sb), flat(dv_sb)]
             + [flat(d_dl[3 * grp + part]) for part in range(3) for grp in range(DIL_GROUPS)] + [dgates])

    w_in_blocks = _row_shards(_atb_pieces(u, dproj, "grad_w_in", D_MODEL // 2))
    if shards is None:
        grads["w_in"] = w_in_blocks
        send = None
    else:
        core = lax.axis_index("c").astype(jnp.int32).reshape(1)
        send = [(_pair_sum(w_in_blocks, _pair_swap(w_in_blocks), core), "scatter_by_chip")]
    (grad_x, dg_mix), lands = _proj_bwd(dproj, dx1, x, g_mix, w["w_in"], send)
    if lands:
        grads["w_in"] = lands[0]
    gain_grads = jnp.concatenate([dg_mix, dg_ffn, dg_fin], axis=0)
    return loss, grad_x, gain_grads, grads


def kernel(x, norm_mix_g, w_in, w_sb_up, w_dil_up, w_out, norm_ffn_g, w_ffn_in, w_ffn_out, norm_final_g, loss_target, m_norm_mix_g, m_w_in, m_w_sb_up, m_w_dil_up, m_w_out, m_norm_ffn_g, m_w_ffn_in, m_w_ffn_out, m_norm_final_g, v_norm_mix_g, v_w_in, v_w_sb_up, v_w_dil_up, v_w_out, v_norm_ffn_g, v_w_ffn_in, v_w_ffn_out, v_norm_final_g):
    mats = {"w_in": w_in, "w_sb_up": w_sb_up, "w_dil_up": w_dil_up, "w_out": w_out,
            "w_ffn_in": w_ffn_in, "w_ffn_out": w_ffn_out}
    moments_m = {"w_in": m_w_in, "w_sb_up": m_w_sb_up, "w_dil_up": m_w_dil_up, "w_out": m_w_out,
                 "w_ffn_in": m_w_ffn_in, "w_ffn_out": m_w_ffn_out}
    moments_v = {"w_in": v_w_in, "w_sb_up": v_w_sb_up, "w_dil_up": v_w_dil_up, "w_out": v_w_out,
                 "w_ffn_in": v_w_ffn_in, "w_ffn_out": v_w_ffn_out}
    stored = lambda t, name: t.transpose(0, 2, 1) if name in TRANSPOSED else t
    mats = {name: stored(t, name) for name, t in mats.items()}
    shards = dict(zip(mats, _to_bf16([t[0] for t in mats.values()])))
    gathered_w_in = _all_gather(shards.pop("w_in"), "all_gather_w_in")
    g_fin = norm_final_g.reshape(1, D_MODEL)
    loss, grad_x, gain_grads, grad_slots = _local_step(
        x.reshape(TOK, D_MODEL), loss_target.reshape(TOK, D_MODEL), norm_mix_g, norm_ffn_g, g_fin,
        gathered_w_in.reshape(IN_WIDTH, D_MODEL), shards=shards)

    gain_rows = jnp.concatenate([gain_grads, jnp.tile(loss, (1, D_MODEL // LANES)),
                                 jnp.zeros((8 - 4, D_MODEL), F32)], axis=0)
    row = lambda t: t.reshape(1, D_MODEL)
    loss_row, *gain_outs = _sum_update_gains(
        _all_gather(gain_rows, "all_gather_gains"),
        [norm_mix_g, norm_ffn_g, g_fin], [m_norm_mix_g, m_norm_ffn_g, row(m_norm_final_g)],
        [v_norm_mix_g, v_norm_ffn_g, row(v_norm_final_g)])

    out_g, out_d, out_m, out_v = {}, {}, {}, {}
    for name, slots in grad_slots.items():
        out_g[name], out_d[name], out_m[name], out_v[name] = [stored(t, name) for t in _sum_update(
            slots, mats[name], stored(moments_m[name], name), stored(moments_v[name], name), "update_" + name)]
    for idx, name in enumerate(("norm_mix_g", "norm_ffn_g", "norm_final_g")):
        shape = (D_MODEL,) if name == "norm_final_g" else (1, D_MODEL)
        out_g[name], out_d[name], out_m[name], out_v[name] = [t.reshape(shape) for t in gain_outs[4 * idx:4 * idx + 4]]

    order = ("norm_mix_g", "w_in", "w_sb_up", "w_dil_up", "w_out", "norm_ffn_g", "w_ffn_in", "w_ffn_out",
             "norm_final_g")
    return (loss_row[0, 0], grad_x.reshape(B_LOC, SEQ, D_MODEL),
            *[out_g[n] for n in order], *[out_d[n] for n in order],
            *[out_m[n] for n in order], *[out_v[n] for n in order])
```

```python
import math

import jax
import jax.numpy as jnp
from jax import lax
from jax.experimental import pallas as pl
from jax.experimental.pallas import tpu as pltpu

F32 = jnp.float32
BF16 = jnp.bfloat16

N_DEV = 8
D_MODEL = 1024
SEQ = 2048
B_LOC = 2
TOK = B_LOC * SEQ
HEAD_DIM = 64
SB_WIDTH = 512
DIL_WIDTH = 768
DIL_OUT = 256
QKV_WIDTH = 3 * SB_WIDTH + 3 * DIL_WIDTH
IN_WIDTH = QKV_WIDTH + 2 * D_MODEL
D_FF = 2816
DIL_PAIRS = ((128, 1), (512, 4), (2048, 16))
DIL_HEADS = 12
RMS_EPS = 1e-6
ALIBI_MAX_BIAS = 8.0
QK_SCALE = 1.0 / math.sqrt(HEAD_DIM)
BLK = 128
LANES = 128
NEG_BIG = -1e30

ADAM_LR = 0.001
ADAM_B1 = 0.9
ADAM_B2 = 0.999
ADAM_EPS = 1e-08
ADAM_WD = 0.01
ADAM_STEP = 10

VMEM_LIMIT = 58 * 1024 * 1024


def _dot(a, b):
    return jnp.dot(a, b, preferred_element_type=F32)


def _dot_nt(a, b):
    return lax.dot_general(a, b, (((1,), (1,)), ((), ())), preferred_element_type=F32)


def _dot_tn(a, b):
    return lax.dot_general(a, b, (((0,), (0,)), ((), ())), preferred_element_type=F32)


def _sigmoid(z):
    return 1.0 / (1.0 + jnp.exp(-z))


def _split_bf16(v):
    hi = v.astype(BF16)
    lo = (v - hi.astype(F32)).astype(BF16)
    return hi, lo


def _chunks(width, step=512):
    out, c = [], 0
    while c < width:
        w = min(step, width - c)
        out.append((c, w))
        c += w
    return out


def _resident(shape):
    nd = len(shape)
    return pl.BlockSpec(shape, lambda *_: (0,) * nd, pipeline_mode=pl.Buffered(1))


def _params(sem):
    return pltpu.CompilerParams(dimension_semantics=sem, vmem_limit_bytes=VMEM_LIMIT)


def _rms_fwd(x, g):
    r = lax.rsqrt(jnp.mean(x * x, axis=-1, keepdims=True) + RMS_EPS)
    n = x * r
    return n, r, n * g


def _rms_bwd(dy, n, r, g):
    dg = jnp.sum(dy * n, axis=0, keepdims=True)
    dn = dy * g
    dx = r * (dn - n * jnp.mean(dn * n, axis=-1, keepdims=True))
    return dx, dg


TM = 256
TM_WIDE = 512


def _norm_proj(x, g, w_in_t, send=None):
    def body(x_ref, g_ref, w_ref, sb_ref, dl_ref, gate_ref, u_ref):
        _, _, u = _rms_fwd(x_ref[...], g_ref[...])
        u = u.astype(BF16)
        u_ref[...] = u
        for c0, w in _chunks(3 * SB_WIDTH):
            sb_ref[:, c0:c0 + w] = _dot_nt(u, w_ref[c0:c0 + w, :]).astype(BF16)
        for c0, w in _chunks(3 * DIL_WIDTH):
            dl_ref[:, c0:c0 + w] = _dot_nt(u, w_ref[3 * SB_WIDTH + c0:3 * SB_WIDTH + c0 + w, :])
        for c0, w in _chunks(2 * D_MODEL):
            gate_ref[:, c0:c0 + w] = _dot_nt(u, w_ref[QKV_WIDTH + c0:QKV_WIDTH + c0 + w, :])

    return _call(
        body, send, name="norm_proj", grid=(TOK // TM_WIDE,),
        in_specs=[pl.BlockSpec((TM_WIDE, D_MODEL), lambda i: (i, 0)), _resident((1, D_MODEL)),
                  _resident((IN_WIDTH, D_MODEL))],
        out_specs=[pl.BlockSpec((TM_WIDE, 3 * SB_WIDTH), lambda i: (i, 0)),
                   pl.BlockSpec((TM_WIDE, 3 * DIL_WIDTH), lambda i: (i, 0)),
                   pl.BlockSpec((TM_WIDE, 2 * D_MODEL), lambda i: (i, 0)),
                   pl.BlockSpec((TM_WIDE, D_MODEL), lambda i: (i, 0))],
        out_shape=[jax.ShapeDtypeStruct((TOK, 3 * SB_WIDTH), BF16),
                   jax.ShapeDtypeStruct((TOK, 3 * DIL_WIDTH), F32),
                   jax.ShapeDtypeStruct((TOK, 2 * D_MODEL), F32),
                   jax.ShapeDtypeStruct((TOK, D_MODEL), BF16)],
        scratch_shapes=[], semantics=("parallel",), operands=(x, g, w_in_t))


FF_CHUNK = 1024


def _mix_ffn_fwd_bwd(x, o_sb, o_dl, gates, w_sb_up, w_dil_up, w_out, target, g_ffn, g_fin, w_ffn_in_t, w_ffn_out):
    def body(x_ref, osb_ref, odl_ref, gate_ref, wsb_ref, wdl_ref, wo_ref, t_ref, gffn_ref, gfin_ref, win_ref, wout_ref,
             loss_ref, dx1_ref, mg_ref, u2_ref, act_ref, dh_ref, dx2_ref, dgfin_ref, dgffn_ref, h_scr):
        i = pl.program_id(0)

        @pl.when(i == 0)
        def _():
            loss_ref[...] = jnp.zeros_like(loss_ref)
            dgfin_ref[...] = jnp.zeros_like(dgfin_ref)
            dgffn_ref[...] = jnp.zeros_like(dgffn_ref)

        y_sb = _dot_nt(osb_ref[...], wsb_ref[...])
        y_dl = _dot_nt(odl_ref[...].astype(BF16), wdl_ref[...])
        merged =(_sigmoid(gate_ref[:, :D_MODEL]) * y_sb
                  + _sigmoid(gate_ref[:, D_MODEL:]) * y_dl).astype(BF16)
        mg_ref[...] = merged
        x1 = x_ref[...] + _dot(merged, wo_ref[...])
        g_ffn_v = gffn_ref[...]
        g_fin_v = gfin_ref[...]
        n2, r2, u2 = _rms_fwd(x1, g_ffn_v)
        u2 = u2.astype(BF16)
        u2_ref[...] = u2
        x2 = x1
        for c0, w in _chunks(D_FF, FF_CHUNK):
            gate = _dot_nt(u2, win_ref[c0:c0 + w, :])
            up = _dot_nt(u2, win_ref[D_FF + c0:D_FF + c0 + w, :])
            h_scr[:, c0:c0 + w] = gate
            h_scr[:, D_FF + c0:D_FF + c0 + w] = up
            act = (gate * _sigmoid(gate) * up).astype(BF16)
            act_ref[:, c0:c0 + w] = act
            x2 = x2 + _dot(act, wout_ref[c0:c0 + w, :])
        n3, r3, y = _rms_fwd(x2, g_fin_v)
        err = y - t_ref[...]
        sq = jnp.sum(jnp.sum(err * err, axis=1, keepdims=True), axis=0, keepdims=True)
        loss_ref[...] += sq * (0.5 / D_MODEL)
        dx2, dgfin = _rms_bwd(err * (1.0 / D_MODEL), n3, r3, g_fin_v)
        dgfin_ref[...] += dgfin
        dx2_b = dx2.astype(BF16)
        dx2_ref[...] = dx2_b
        du2 = jnp.zeros((TM, D_MODEL), F32)
        for c0, w in _chunks(D_FF, FF_CHUNK):
            gate = h_scr[:, c0:c0 + w]
            up = h_scr[:, D_FF + c0:D_FF + c0 + w]
            dact = _dot_nt(dx2_b, wout_ref[c0:c0 + w, :])
            sg = _sigmoid(gate)
            dgate = (dact * up * (sg * (1.0 + gate * (1.0 - sg)))).astype(BF16)
            dup = (dact * (gate * sg)).astype(BF16)
            dh_ref[:, c0:c0 + w] = dgate
            dh_ref[:, D_FF + c0:D_FF + c0 + w] = dup
            du2 = du2 + _dot(dgate, win_ref[c0:c0 + w, :])
            du2 = du2 + _dot(dup, win_ref[D_FF + c0:D_FF + c0 + w, :])
        dx1_n, dgffn = _rms_bwd(du2, n2, r2, g_ffn_v)
        dgffn_ref[...] += dgffn
        dx1_ref[...] = dx2 + dx1_n

    tile = lambda w: pl.BlockSpec((TM, w), lambda i: (i, 0))
    acc = lambda w: pl.BlockSpec((1, w), lambda i: (0, 0))
    return pl.pallas_call(
        body, name="mix_ffn_fwd_bwd", grid=(TOK // TM,),
        in_specs=[tile(D_MODEL), tile(SB_WIDTH), tile(DIL_OUT), tile(2 * D_MODEL),
                  _resident((D_MODEL, SB_WIDTH)), _resident((D_MODEL, DIL_OUT)), _resident((D_MODEL, D_MODEL)),
                  tile(D_MODEL), _resident((1, D_MODEL)), _resident((1, D_MODEL)),
                  _resident((2 * D_FF, D_MODEL)), _resident((D_FF, D_MODEL))],
        out_specs=[acc(LANES), tile(D_MODEL), tile(D_MODEL), tile(D_MODEL), tile(D_FF), tile(2 * D_FF),
                   tile(D_MODEL), acc(D_MODEL), acc(D_MODEL)],
        out_shape=[jax.ShapeDtypeStruct((1, LANES), F32),
                   jax.ShapeDtypeStruct((TOK, D_MODEL), F32),
                   jax.ShapeDtypeStruct((TOK, D_MODEL), BF16),
                   jax.ShapeDtypeStruct((TOK, D_MODEL), BF16),
                   jax.ShapeDtypeStruct((TOK, D_FF), BF16),
                   jax.ShapeDtypeStruct((TOK, 2 * D_FF), BF16),
                   jax.ShapeDtypeStruct((TOK, D_MODEL), BF16),
                   jax.ShapeDtypeStruct((1, D_MODEL), F32),
                   jax.ShapeDtypeStruct((1, D_MODEL), F32)],
        scratch_shapes=[pltpu.VMEM((TM, 2 * D_FF), F32)],
        compiler_params=_params(("arbitrary",)),
    )(x, o_sb, o_dl, gates, w_sb_up, w_dil_up, w_out, target, g_ffn, g_fin, w_ffn_in_t, w_ffn_out)


def _mix_bwd(dx1, o_sb, o_dl, gates, w_sb_up, w_dil_up, w_out):
    def body(dx1_ref, osb_ref, odl_ref, gate_ref, wsb_ref, wdl_ref, wout_ref,
             dgate_ref, dysb_ref, dydl_ref, dosb_ref, dodl_ref, dsum_ref):
        dmerged = _dot_nt(dx1_ref[...].astype(BF16), wout_ref[...])
        o_dl = odl_ref[...]
        y_sb = _dot_nt(osb_ref[...], wsb_ref[...])
        y_dl = _dot_nt(o_dl.astype(BF16), wdl_ref[...])
        s_sb = _sigmoid(gate_ref[:, :D_MODEL])
        s_dl = _sigmoid(gate_ref[:, D_MODEL:])
        dgate_ref[:, :D_MODEL] = (dmerged * y_sb * (s_sb * (1.0 - s_sb))).astype(BF16)
        dgate_ref[:, D_MODEL:] = (dmerged * y_dl * (s_dl * (1.0 - s_dl))).astype(BF16)
        dy_sb = (dmerged * s_sb).astype(BF16)
        dy_dl = (dmerged * s_dl).astype(BF16)
        dysb_ref[...] = dy_sb
        dydl_ref[...] = dy_dl
        dosb_ref[...] = _dot(dy_sb, wsb_ref[...]).astype(BF16)
        do_dl = _dot(dy_dl, wdl_ref[...])
        dodl_ref[...] = do_dl
        row = lax.broadcasted_iota(jnp.int32, (DIL_OUT, DIL_OUT), 0) // HEAD_DIM
        col = lax.broadcasted_iota(jnp.int32, (DIL_OUT, DIL_OUT), 1) // HEAD_DIM
        same_head = (row == col).astype(BF16)
        hi, lo = _split_bf16(do_dl * o_dl)
        dsum_ref[...] = _dot(hi, same_head) + _dot(lo, same_head)

    tile = lambda w: pl.BlockSpec((TM_WIDE, w), lambda i: (i, 0))
    return pl.pallas_call(
        body, name="mix_bwd", grid=(TOK // TM_WIDE,),
        in_specs=[tile(D_MODEL), tile(SB_WIDTH), tile(DIL_OUT), tile(2 * D_MODEL),
                  _resident((D_MODEL, SB_WIDTH)), _resident((D_MODEL, DIL_OUT)),
                  _resident((D_MODEL, D_MODEL))],
        out_specs=[tile(2 * D_MODEL), tile(D_MODEL), tile(D_MODEL), tile(SB_WIDTH), tile(DIL_OUT),
                   tile(DIL_OUT)],
        out_shape=[jax.ShapeDtypeStruct((TOK, 2 * D_MODEL), BF16),
                   jax.ShapeDtypeStruct((TOK, D_MODEL), BF16),
                   jax.ShapeDtypeStruct((TOK, D_MODEL), BF16),
                   jax.ShapeDtypeStruct((TOK, SB_WIDTH), BF16),
                   jax.ShapeDtypeStruct((TOK, DIL_OUT), F32),
                   jax.ShapeDtypeStruct((TOK, DIL_OUT), F32)],
        compiler_params=_params(("parallel",)),
    )(dx1, o_sb, o_dl, gates, w_sb_up, w_dil_up, w_out)


def _proj_bwd(dproj, dx1, x, g, w_in_t, send=None):
    widths = [p.shape[1] for p in dproj]

    def body(*refs):
        dx1_ref, x_ref, g_ref, w_ref, dx_ref, dg_ref = refs[len(widths):]

        @pl.when(pl.program_id(0) == 0)
        def _():
            dg_ref[...] = jnp.zeros_like(dg_ref)

        du = jnp.zeros((TM, D_MODEL), F32)
        c0 = 0
        for dp_ref, w in zip(refs, widths):
            du = du + _dot(dp_ref[...].astype(BF16), w_ref[c0:c0 + w, :])
            c0 += w
        g_v = g_ref[...]
        n, r, _ = _rms_fwd(x_ref[...], g_v)
        dx, dg = _rms_bwd(du, n, r, g_v)
        dg_ref[...] += dg
        dx_ref[...] = dx1_ref[...] + dx

    tile = lambda w: pl.BlockSpec((TM, w), lambda i: (i, 0))
    return _call(
        body, send, name="proj_bwd", grid=(TOK // TM,),
        in_specs=[tile(w) for w in widths] + [tile(D_MODEL), tile(D_MODEL), _resident((1, D_MODEL)),
                                              _resident((IN_WIDTH, D_MODEL))],
        out_specs=[tile(D_MODEL), pl.BlockSpec((1, D_MODEL), lambda i: (0, 0))],
        out_shape=[jax.ShapeDtypeStruct((TOK, D_MODEL), F32),
                   jax.ShapeDtypeStruct((1, D_MODEL), F32)],
        scratch_shapes=[], semantics=("arbitrary",), operands=(*dproj, dx1, x, g, w_in_t))


def _atb_pieces(a, pieces, name, tm, tk=512):
    m = a.shape[1]
    widths = [p.shape[1] for p in pieces]
    n = sum(widths)
    nk = TOK // tk

    def body(a_ref, *refs):
        o_ref, acc_ref = refs[len(widths):]
        k = pl.program_id(1)

        @pl.when(k == 0)
        def _():
            acc_ref[...] = jnp.zeros_like(acc_ref)

        a_v = a_ref[...]
        c0 = 0
        for p_ref, w in zip(refs, widths):
            acc_ref[:, c0:c0 + w] += _dot_tn(a_v, p_ref[...].astype(BF16))
            c0 += w

        @pl.when(k == nk - 1)
        def _():
            for c0, w in _chunks(n):
                o_ref[c0:c0 + w, :] = acc_ref[:, c0:c0 + w].T.astype(BF16)

    return pl.pallas_call(
        body, name=name, grid=(m // tm, nk),
        in_specs=[pl.BlockSpec((tk, tm), lambda i, k: (k, i))]
                 + [pl.BlockSpec((tk, w), lambda i, k: (k, 0)) for w in widths],
        out_specs=pl.BlockSpec((n, tm), lambda i, k: (0, i)),
        out_shape=jax.ShapeDtypeStruct((n, m), BF16),
        scratch_shapes=[pltpu.VMEM((tm, n), F32)],
        compiler_params=_params(("parallel", "arbitrary")),
    )(a, *pieces)


def _atb_cols(a, b, name, cols, tk=512):
    m, other = a.shape[1], b.shape[1]

    def body(a_ref, b_ref, o_ref):
        acc = jnp.zeros(o_ref.shape, F32)
        for k0 in range(0, TOK, tk):
            acc = acc + _dot_tn(a_ref[k0:k0 + tk, :], b_ref[k0:k0 + tk, :])
        o_ref[...] = acc.astype(BF16)

    assert m % cols == 0
    return pl.pallas_call(
        body, name=name, grid=(m // cols,),
        in_specs=[pl.BlockSpec((TOK, cols), lambda r: (0, r)), _resident((TOK, other))],
        out_specs=pl.BlockSpec((cols, other), lambda r: (r, 0)),
        out_shape=jax.ShapeDtypeStruct((m, other), BF16),
        compiler_params=_params(("parallel",)),
    )(a, b)


def _atb(a, b, name, tm, tn, col_blocks=0, tk=512):
    m, n = a.shape[1], b.shape[1]
    nk = TOK // tk

    def body(a_ref, b_ref, o_ref, acc_ref):
        k = pl.program_id(2)

        @pl.when(k == 0)
        def _():
            acc_ref[...] = jnp.zeros_like(acc_ref)

        acc_ref[...] += _dot_tn(a_ref[...].astype(BF16), b_ref[...].astype(BF16))

        @pl.when(k == nk - 1)
        def _():
            if col_blocks:
                width = n // col_blocks
                for blk in range(col_blocks):
                    o_ref[blk] = acc_ref[:, blk * width:(blk + 1) * width].astype(BF16)
            else:
                o_ref[...] = acc_ref[...].astype(BF16)

    if col_blocks:
        out_spec = pl.BlockSpec((col_blocks, tm, n // col_blocks), lambda i, j, k: (0, i, 0))
        out_shape = jax.ShapeDtypeStruct((col_blocks, m, n // col_blocks), BF16)
    else:
        out_spec = pl.BlockSpec((tm, tn), lambda i, j, k: (i, j))
        out_shape = jax.ShapeDtypeStruct((m, n), BF16)
    return pl.pallas_call(
        body, name=name, grid=(m // tm, n // tn, nk),
        in_specs=[pl.BlockSpec((tk, tm), lambda i, j, k: (k, i)),
                  pl.BlockSpec((tk, tn), lambda i, j, k: (k, j))],
        out_specs=out_spec, out_shape=out_shape,
        scratch_shapes=[pltpu.VMEM((tm, tn), F32)],
        compiler_params=_params(("parallel", "parallel", "arbitrary")),
    )(a, b)


SB_PAIRS = SB_WIDTH // LANES


def _two_heads(v, lane0):
    zero = jnp.zeros_like(v)
    return jnp.where(lane0, v, zero), jnp.where(lane0, zero, v)


SB_QBLK = 256
N_SB_STEPS = SEQ // SB_QBLK


SB_KCHUNK = 2 * BLK
SB_ROWS = 2 * SB_QBLK
SB_DEAD = -104.0


def _log_keep(z):
    neg_z = -z
    return jnp.minimum(neg_z, 0.0) - jnp.log(1.0 + jnp.exp(jnp.minimum(z, neg_z)))


def _stack_heads(v, lane0):
    return jnp.concatenate(_two_heads(v, lane0), axis=0)


def _block_sums(v, tri, split=True):
    halves = (v[:, :BLK], v[:, BLK:])
    stacked = jnp.concatenate(halves, axis=0)
    if split:
        hi, lo = _split_bf16(stacked)
        prod = _dot(jnp.concatenate([hi, lo], axis=0), tri)
        tri_sum = prod[:2 * SB_ROWS] + prod[2 * SB_ROWS:]
    else:
        tri_sum = _dot(stacked.astype(BF16), tri)
    sums = tuple(jnp.sum(h, axis=1, keepdims=True) for h in halves)
    return (tri_sum[:SB_ROWS], tri_sum[SB_ROWS:]), sums


def _sb_diag_mask():
    row = lax.broadcasted_iota(jnp.int32, (SB_ROWS, SB_KCHUNK), 0)
    col = lax.broadcasted_iota(jnp.int32, (SB_ROWS, SB_KCHUNK), 1)
    return col < jnp.where(row >= SB_QBLK, row - SB_QBLK, row)


def _sb_fwd(qkv, send=None):
    def body(q_ref, k_ref, v_ref, o_ref):
        i = pl.program_id(2)
        krow = lax.broadcasted_iota(jnp.int32, (BLK, BLK), 0)
        kcol = lax.broadcasted_iota(jnp.int32, (BLK, BLK), 1)
        later = (krow > kcol).astype(BF16)
        lane0 = lax.broadcasted_iota(jnp.int32, (SB_QBLK, LANES), 1) < HEAD_DIM
        q2 = _stack_heads(q_ref[0] * QK_SCALE, lane0)

        def chunk(c, carry, causal):
            acc, run = carry
            off = pl.multiple_of(c * SB_KCHUNK, SB_KCHUNK)
            z = _dot_nt(q2, k_ref[0, pl.ds(off, SB_KCHUNK), :])
            log_keep = _log_keep(z)
            if causal is not None:
                log_keep = jnp.where(causal, log_keep, 0.0)
            suffix, sums = _block_sums(log_keep, later)
            log_after = jnp.concatenate([suffix[0] + (run + sums[1]), suffix[1] + run], axis=1)
            a = jnp.exp(log_keep + z + log_after)
            if causal is not None:
                a = jnp.where(causal, a, 0.0)
            acc = acc + _dot(a.astype(BF16), v_ref[0, pl.ds(off, SB_KCHUNK), :])
            return acc, run + (sums[0] + sums[1])

        acc, run = chunk(i, (jnp.zeros((SB_ROWS, LANES), F32), jnp.zeros((SB_ROWS, 1), F32)), _sb_diag_mask())

        def some_alive(run):
            return (jnp.max(run) > SB_DEAD).astype(jnp.int32)

        def trip(state):
            t, _, acc, run = state
            acc, run = chunk(i - 1 - t, (acc, run), None)
            return t + 1, some_alive(run), acc, run

        _, _, acc, _ = lax.while_loop(lambda s: jnp.logical_and(s[0] < i, s[1] > 0), trip,
                                      (jnp.int32(0), some_alive(run), acc, run))
        o_ref[0] = jnp.where(lane0, acc[:SB_QBLK], acc[SB_QBLK:]).astype(BF16)

    blk = pl.BlockSpec((1, SB_QBLK, LANES), lambda b, h, i: (b, i, h))
    return _call(
        body, send, name="sb_fwd", grid=(B_LOC, SB_PAIRS, N_SB_STEPS),
        in_specs=[blk,
                  pl.BlockSpec((1, SEQ, LANES), lambda b, h, i: (b, 0, SB_PAIRS + h)),
                  pl.BlockSpec((1, SEQ, LANES), lambda b, h, i: (b, 0, 2 * SB_PAIRS + h))],
        out_specs=[blk], out_shape=[jax.ShapeDtypeStruct((B_LOC, SEQ, SB_WIDTH), BF16)],
        scratch_shapes=[], semantics=("parallel", "parallel", "arbitrary"), operands=(qkv, qkv, qkv))


def _sb_bwd(qkv, d_o, send=None):
    def body(q_ref, k_ref, v_ref, do_ref, dq_ref, dk_ref, dv_ref, dk_acc, dv_acc, z_scr, keep_scr):
        i = pl.program_id(2)
        krow = lax.broadcasted_iota(jnp.int32, (BLK, BLK), 0)
        kcol = lax.broadcasted_iota(jnp.int32, (BLK, BLK), 1)
        upto = (krow <= kcol).astype(BF16)
        earlier = (krow < kcol).astype(BF16)
        lane0 = lax.broadcasted_iota(jnp.int32, (SB_QBLK, LANES), 1) < HEAD_DIM
        q2 = _stack_heads(q_ref[0] * QK_SCALE, lane0)
        do2 = _stack_heads(do_ref[0], lane0)

        def keep_sum(c, causal):
            off = pl.multiple_of(c * SB_KCHUNK, SB_KCHUNK)
            z = _dot_nt(q2, k_ref[0, pl.ds(off, SB_KCHUNK), :])
            log_keep = _log_keep(z)
            if causal is not None:
                log_keep = jnp.where(causal, log_keep, 0.0)
            z_scr[c] = z
            keep_scr[c] = log_keep
            return jnp.sum(log_keep, axis=1, keepdims=True)

        def some_alive(run):
            return (jnp.max(run) > SB_DEAD).astype(jnp.int32)

        def scan(state):
            t, _, run = state
            run = run + keep_sum(i - 1 - t, None)
            return t + 1, some_alive(run), run

        diag_sum = keep_sum(i, _sb_diag_mask())
        walked, _, tot2 = lax.while_loop(lambda s: jnp.logical_and(s[0] < i, s[1] > 0), scan,
                                         (jnp.int32(0), some_alive(diag_sum), diag_sum))
        first = i - walked

        @pl.when(i == 0)
        def _():
            dk_acc[...] = jnp.zeros_like(dk_acc)
            dv_acc[...] = jnp.zeros_like(dv_acc)

        def chunk(c, carry, causal):
            dq, pre_keep, pre_e = carry
            off = pl.multiple_of(c * SB_KCHUNK, SB_KCHUNK)
            k_c = k_ref[0, pl.ds(off, SB_KCHUNK), :]
            v_c = v_ref[0, pl.ds(off, SB_KCHUNK), :]
            d_a = _dot_nt(do2, v_c)
            log_keep = keep_scr[c]
            log_beta = log_keep + z_scr[c]
            prefix, sums = _block_sums(log_keep, upto)
            inclusive = jnp.concatenate([prefix[0], prefix[1] + sums[0]], axis=1)
            a = jnp.exp(log_beta + ((tot2 - pre_keep) - inclusive))
            if causal is not None:
                a = jnp.where(causal, a, 0.0)
            e = d_a * a
            e_prefix, e_sums = _block_sums(e, earlier, split=False)
            before = jnp.concatenate([e_prefix[0] + pre_e, e_prefix[1] + (pre_e + e_sums[0])], axis=1)
            dz = e - (e + before) * jnp.exp(log_beta)
            if causal is not None:
                dz = jnp.where(causal, dz, 0.0)
            dz = dz.astype(BF16)
            dq = dq + _dot(dz, k_c)
            dk_acc[pl.ds(off, SB_KCHUNK), :] += _dot_tn(dz, q2)
            dv_acc[pl.ds(off, SB_KCHUNK), :] += _dot_tn(a.astype(BF16), do2)
            return dq, pre_keep + (sums[0] + sums[1]), pre_e + (e_sums[0] + e_sums[1])

        zero_col = jnp.zeros((SB_ROWS, 1), F32)
        carry = lax.fori_loop(first, i, lambda t, c: chunk(t, c, None),
                              (jnp.zeros((SB_ROWS, LANES), F32), zero_col, zero_col))
        dq, _, _ = chunk(i, carry, _sb_diag_mask())
        dq_ref[0] = (jnp.where(lane0, dq[:SB_QBLK], dq[SB_QBLK:]) * QK_SCALE).astype(BF16)

        @pl.when(i == N_SB_STEPS - 1)
        def _():
            dk_ref[0] = dk_acc[...].astype(BF16)
            dv_ref[0] = dv_acc[...].astype(BF16)

    blk = pl.BlockSpec((1, SB_QBLK, LANES), lambda b, h, i: (b, i, h))
    whole = lambda c: pl.BlockSpec((1, SEQ, LANES), lambda b, h, i: (b, 0, c * SB_PAIRS + h))
    out = jax.ShapeDtypeStruct((B_LOC, SEQ, SB_WIDTH), BF16)
    return _call(
        body, send, name="sb_bwd", grid=(B_LOC, SB_PAIRS, N_SB_STEPS),
        in_specs=[blk, whole(1), whole(2), blk],
        out_specs=[blk, whole(0), whole(0)],
        out_shape=[out, out, out],
        scratch_shapes=[pltpu.VMEM((SEQ, LANES), F32), pltpu.VMEM((SEQ, LANES), F32),
                        pltpu.VMEM((N_SB_STEPS, SB_ROWS, SB_KCHUNK), F32),
                        pltpu.VMEM((N_SB_STEPS, SB_ROWS, SB_KCHUNK), F32)],
        semantics=("parallel", "parallel", "arbitrary"), operands=(qkv, qkv, qkv, d_o))


DIL_GROUPS = len(DIL_PAIRS)
DIL_QBLOCKS = SEQ // BLK


def _residue_rows(j, dilation):
    length = SEQ // dilation
    return pl.ds(j, length, stride=dilation) if dilation > 1 else pl.ds(0, length)


def _gather_residues(src_ref, dst_ref, dst_off, dilation, scale=None):
    length = SEQ // dilation
    for j in range(dilation):
        v = src_ref[_residue_rows(j, dilation), :]
        if scale is not None:
            v = v * scale
        dst_ref[dst_off + j * length:dst_off + (j + 1) * length, :] = v.astype(dst_ref.dtype)


def _scatter_residues(src_ref, src_off, dst_ref, dilation):
    length = SEQ // dilation
    for j in range(dilation):
        dst_ref[_residue_rows(j, dilation), :] = (
            src_ref[src_off + j * length:src_off + (j + 1) * length, :].astype(dst_ref.dtype))


def _dil_geometry(group, pair):
    dilation = DIL_PAIRS[group][1]
    row = lax.broadcasted_iota(jnp.int32, (2 * BLK, 2 * BLK), 0)
    col = lax.broadcasted_iota(jnp.int32, (2 * BLK, 2 * BLK), 1)
    second = row >= BLK
    steps = BLK + jnp.where(second, row - BLK, row) - col
    coef = -ALIBI_MAX_BIAS / DIL_HEADS * math.log(2.0)
    first_head = float(4 * group + 1) + 2.0 * pair.astype(F32)
    slope = jnp.exp(coef * (first_head + jnp.where(second, 1.0, 0.0)))
    bias = slope * (steps * dilation).astype(F32)
    valid = jnp.logical_and(steps >= 0, steps <= BLK)
    return bias, valid, col >= BLK


def _dil_tile_scores(q2, kk, geometry, has_prev):
    bias, valid, own = geometry
    ok = jnp.logical_and(valid, jnp.logical_or(own, has_prev))
    return jnp.where(ok, _dot_nt(q2, kk) - bias, NEG_BIG)


def _head_col(v, lane_mask):
    return jnp.max(jnp.where(lane_mask, v, NEG_BIG), axis=1, keepdims=True)


def _dil_fwd(qkv, send=None):
    def body(*refs):
        ins, (o_ref, lse_ref), (qs, ks, vs, o_res, lse_res) = refs[:9], refs[9:11], refs[11:16]
        o_grp, lse_grp = refs[16:19], refs[19:22]
        pair = pl.program_id(1)
        lane0 = lax.broadcasted_iota(jnp.int32, (BLK, LANES), 1) < HEAD_DIM
        ks[0:BLK, :] = jnp.zeros((BLK, LANES), BF16)
        vs[0:BLK, :] = jnp.zeros((BLK, LANES), BF16)
        for grp, (_, dilation) in enumerate(DIL_PAIRS):
            q_ref, k_ref, v_ref = ins[3 * grp:3 * grp + 3]
            per_residue = DIL_QBLOCKS // dilation
            _gather_residues(q_ref, qs, 0, dilation, QK_SCALE)
            _gather_residues(k_ref, ks, BLK, dilation)
            _gather_residues(v_ref, vs, BLK, dilation)
            geometry = _dil_geometry(grp, pair)

            def step(blk, _):
                off = pl.multiple_of(blk * BLK, BLK)
                q2 = _stack_heads(qs[pl.ds(off, BLK), :], lane0)
                s = _dil_tile_scores(q2, ks[pl.ds(off, 2 * BLK), :], geometry, blk % per_residue != 0)
                m = jnp.max(s, axis=1, keepdims=True)
                p = jnp.exp(s - m)
                den = jnp.sum(p, axis=1, keepdims=True)
                out = _dot(p.astype(BF16), vs[pl.ds(off, 2 * BLK), :]) / den
                lse = m + jnp.log(den)
                o_res[pl.ds(off, BLK), :] = jnp.where(lane0, out[:BLK], out[BLK:])
                lse_res[pl.ds(off, BLK), :] = jnp.where(lane0, lse[:BLK], lse[BLK:])
                return 0

            lax.fori_loop(0, DIL_QBLOCKS, step, 0, unroll=8)
            _scatter_residues(o_res, 0, o_grp[grp], dilation)
            _scatter_residues(lse_res, 0, lse_grp[grp], dilation)

        for r0 in range(0, SEQ, 2 * BLK):
            rows = slice(r0, r0 + 2 * BLK)
            ls = [lse_grp[g][rows, :] for g in range(DIL_GROUPS)]
            m = jnp.maximum(jnp.maximum(ls[0], ls[1]), ls[2])
            w = [jnp.exp(l - m) for l in ls]
            den = w[0] + w[1] + w[2]
            o_ref[rows, :] = (w[0] * o_grp[0][rows, :] + w[1] * o_grp[1][rows, :] + w[2] * o_grp[2][rows, :]) / den
            lse_ref[rows, :] = m + jnp.log(den)

    def col(part, grp):
        return pl.BlockSpec((None, SEQ, LANES), lambda b, p: (b, 0, 6 * part + 2 * grp + p))

    out_spec = pl.BlockSpec((None, SEQ, LANES), lambda b, p: (b, 0, p))
    out = jax.ShapeDtypeStruct((B_LOC, SEQ, DIL_OUT), F32)
    return _call(
        body, send, name="dil_fwd", grid=(B_LOC, DIL_OUT // LANES),
        in_specs=[col(part, grp) for grp in range(DIL_GROUPS) for part in range(3)],
        out_specs=[out_spec, out_spec], out_shape=[out, out],
        scratch_shapes=[pltpu.VMEM((SEQ, LANES), BF16), pltpu.VMEM((SEQ + BLK, LANES), BF16),
                        pltpu.VMEM((SEQ + BLK, LANES), BF16), pltpu.VMEM((SEQ, LANES), F32),
                        pltpu.VMEM((SEQ, LANES), F32)] + [pltpu.VMEM((SEQ, LANES), F32)] * (2 * DIL_GROUPS),
        semantics=("parallel", "parallel"), operands=[qkv] * 9)


def _dil_bwd(qkv, d_o, lse, dsum, send=None):
    def body(*refs):
        ins, (do_ref, lse_ref, dsum_ref), outs = refs[:9], refs[9:12], refs[12:21]
        qs, ks, vs, dos, lse_res, dsum_res, dq_res, dk_acc, dv_acc = refs[21:]
        pair = pl.program_id(1)
        lane0 = lax.broadcasted_iota(jnp.int32, (BLK, LANES), 1) < HEAD_DIM
        lane1 = jnp.logical_not(lane0)
        ks[0:BLK, :] = jnp.zeros((BLK, LANES), BF16)
        vs[0:BLK, :] = jnp.zeros((BLK, LANES), BF16)
        for grp, (_, dilation) in enumerate(DIL_PAIRS):
            q_ref, k_ref, v_ref = ins[3 * grp:3 * grp + 3]
            dq_ref, dk_ref, dv_ref = outs[3 * grp:3 * grp + 3]
            per_residue = DIL_QBLOCKS // dilation
            _gather_residues(q_ref, qs, 0, dilation, QK_SCALE)
            _gather_residues(k_ref, ks, BLK, dilation)
            _gather_residues(v_ref, vs, BLK, dilation)
            _gather_residues(do_ref, dos, 0, dilation)
            _gather_residues(lse_ref, lse_res, 0, dilation)
            _gather_residues(dsum_ref, dsum_res, 0, dilation)
            dk_acc[...] = jnp.zeros_like(dk_acc)
            dv_acc[...] = jnp.zeros_like(dv_acc)
            geometry = _dil_geometry(grp, pair)

            def step(blk, _):
                off = pl.multiple_of(blk * BLK, BLK)
                q2 = _stack_heads(qs[pl.ds(off, BLK), :], lane0)
                do2 = _stack_heads(dos[pl.ds(off, BLK), :], lane0)
                kk = ks[pl.ds(off, 2 * BLK), :]
                vv = vs[pl.ds(off, 2 * BLK), :]
                lse_blk = lse_res[pl.ds(off, BLK), :]
                dsum_blk = dsum_res[pl.ds(off, BLK), :]
                lse2 = jnp.concatenate([_head_col(lse_blk, lane0), _head_col(lse_blk, lane1)], axis=0)
                dsum2 = jnp.concatenate([_head_col(dsum_blk, lane0), _head_col(dsum_blk, lane1)], axis=0)
                s = _dil_tile_scores(q2, kk, geometry, blk % per_residue != 0)
                p = jnp.exp(s - lse2)
                ds = (p * (_dot_nt(do2, vv) - dsum2)).astype(BF16)
                dq2 = _dot(ds, kk)
                dq_res[pl.ds(off, BLK), :] = jnp.where(lane0, dq2[:BLK], dq2[BLK:]) * QK_SCALE
                dk_acc[pl.ds(off, 2 * BLK), :] += _dot_tn(ds, q2)
                dv_acc[pl.ds(off, 2 * BLK), :] += _dot_tn(p.astype(BF16), do2)
                return 0

            lax.fori_loop(0, DIL_QBLOCKS, step, 0, unroll=8)
            _scatter_residues(dq_res, 0, dq_ref, dilation)
            _scatter_residues(dk_acc, BLK, dk_ref, dilation)
            _scatter_residues(dv_acc, BLK, dv_ref, dilation)

    def col(part, grp):
        return pl.BlockSpec((None, SEQ, LANES), lambda b, p: (b, 0, 6 * part + 2 * grp + p))

    slot = pl.BlockSpec((None, SEQ, LANES), lambda b, p: (b, 0, p))
    out = jax.ShapeDtypeStruct((B_LOC, SEQ, DIL_OUT), F32)
    return _call(
        body, send, name="dil_bwd", grid=(B_LOC, DIL_OUT // LANES),
        in_specs=[col(part, grp) for grp in range(DIL_GROUPS) for part in range(3)] + [slot] * 3,
        out_specs=[slot] * 9, out_shape=[out] * 9,
        scratch_shapes=[pltpu.VMEM((SEQ, LANES), BF16), pltpu.VMEM((SEQ + BLK, LANES), BF16),
                        pltpu.VMEM((SEQ + BLK, LANES), BF16), pltpu.VMEM((SEQ, LANES), BF16),
                        pltpu.VMEM((SEQ, LANES), F32), pltpu.VMEM((SEQ, LANES), F32),
                        pltpu.VMEM((SEQ, LANES), F32), pltpu.VMEM((SEQ + BLK, LANES), F32),
                        pltpu.VMEM((SEQ + BLK, LANES), F32)],
        semantics=("parallel", "parallel"), operands=[qkv] * 9 + [d_o, lse, dsum])


def _peers():
    x, y, c = lax.axis_index("x"), lax.axis_index("y"), lax.axis_index("c")
    me = 4 * x + 2 * y + c
    peers = []
    for mask in range(1, N_DEV):
        px = 1 - x if mask & 4 else x
        py = 1 - y if mask & 2 else y
        pc = 1 - c if mask & 1 else c
        peers.append(((px, py, pc), 4 * px + 2 * py + pc))
    return me, peers


def _all_gather(shard, name):
    rows, cols = shard.shape
    by_rows = rows % 32 == 0

    def body(src_ref, out_ref, send_sems, recv_sems, local_sem):
        x, y, c = lax.axis_index("x"), lax.axis_index("y"), lax.axis_index("c")
        me, sibling = (x, y, c), (x, y, 1 - c)
        x_chip, y_chip, across = (1 - x, y), (x, 1 - y), (1 - x, 1 - y)

        def slot(block, half=None):
            ref = out_ref.at[4 * block[0] + 2 * block[1] + block[2]]
            if half is None:
                return ref
            return ref.at[pl.ds(half * (rows // 2), rows // 2)] if by_rows else \
                ref.at[:, pl.ds(half * (cols // 2), cols // 2)]

        def copy(k, block, to, src=None, half=None):
            return pltpu.make_async_remote_copy(
                src_ref=slot(block, half) if src is None else src, dst_ref=slot(block, half),
                send_sem=send_sems.at[k], recv_sem=recv_sems.at[k], device_id=to,
                device_id_type=pl.DeviceIdType.MESH)

        mine = pltpu.make_async_copy(src_ref, slot(me), local_sem)
        mine.start()
        sent = [copy(0, me, sibling, src=src_ref), copy(1, me, (*x_chip, c), src=src_ref),
                copy(2, me, (*y_chip, c), src=src_ref)]
        for cp in sent:
            cp.start()

        def arrived(k, block, half=None):
            copy(k, block, me, half=half).wait_recv()
            onward = {1: [copy(3, block, (*y_chip, c), half=0), copy(5, block, sibling)],
                      2: [copy(4, block, (*x_chip, c), half=1), copy(6, block, sibling)],
                      3: [copy(7, block, sibling, half=0)],
                      4: [copy(8, block, sibling, half=1)]}.get(k, [])
            for cp in onward:
                cp.start()
            sent.extend(onward)

        arrived(1, (*x_chip, c))
        arrived(2, (*y_chip, c))
        arrived(3, (*across, c), half=0)
        arrived(4, (*across, c), half=1)
        arrived(0, sibling)
        arrived(5, (*x_chip, 1 - c))
        arrived(6, (*y_chip, 1 - c))
        arrived(7, (*across, 1 - c), half=0)
        arrived(8, (*across, 1 - c), half=1)
        for cp in sent:
            cp.wait_send()
        mine.wait()

    n_sems = 9
    return pl.pallas_call(
        body, name=name,
        in_specs=[pl.BlockSpec(memory_space=pl.ANY)],
        out_specs=pl.BlockSpec(memory_space=pl.ANY),
        out_shape=jax.ShapeDtypeStruct((N_DEV,) + shard.shape, shard.dtype),
        scratch_shapes=[pltpu.SemaphoreType.DMA((n_sems,)), pltpu.SemaphoreType.DMA((n_sems,)),
                        pltpu.SemaphoreType.DMA],
    )(shard)


def _call(body, send, *, name, grid, in_specs, out_specs, out_shape, scratch_shapes, semantics, operands):
    if send is None:
        return pl.pallas_call(
            body, name=name, grid=grid, in_specs=in_specs, out_specs=out_specs, out_shape=out_shape,
            scratch_shapes=scratch_shapes, compiler_params=_params(semantics))(*operands), []
    srcs, kinds = [s for s, _ in send], [k for _, k in send]
    n, n_in, n_out, n_scr = len(srcs), len(in_specs), len(out_specs), len(scratch_shapes)
    steps = math.prod(grid)
    relay_step = (13 * steps) // 16

    def plan(refs):
        src_refs, land_refs = refs[n_in:n_in + n], refs[n_in + n + n_out:n_in + 2 * n + n_out]
        send_sems, recv_sems, local_sems = refs[-3:]
        x, y, c = lax.axis_index("x"), lax.axis_index("y"), lax.axis_index("c")
        me, peers = _peers()
        first, relayed_in, relayed_out, arrivals, sends, own = [], [], [], [], [], []
        for a, kind in enumerate(kinds):
            def copy(k, src, dst_slot, to):
                return pltpu.make_async_remote_copy(
                    src_ref=src, dst_ref=land_refs[a].at[dst_slot], send_sem=send_sems.at[a * (N_DEV - 1) + k],
                    recv_sem=recv_sems.at[a * (N_DEV - 1) + k], device_id=to, device_id_type=pl.DeviceIdType.MESH)

            if kind == "gather_by_chip":
                idx = lambda px, py, pc: 4 * px + 2 * py + pc
                chips = [(1 - x, y), (x, 1 - y), (1 - x, 1 - y)]
                mine = [copy(0, src_refs[a], me, (x, y, 1 - c))]
                arrivals.append(copy(0, src_refs[a], idx(x, y, 1 - c), (x, y, 1 - c)))
                for j, (px, py) in enumerate(chips):
                    mine.append(copy(1 + j, src_refs[a], me, (px, py, c)))
                    relayed_in.append(copy(1 + j, src_refs[a], idx(px, py, c), (px, py, c)))
                    relayed_out.append(copy(4 + j, land_refs[a].at[idx(px, py, c)], idx(px, py, c), (x, y, 1 - c)))
                    arrivals.append(copy(4 + j, src_refs[a], idx(px, py, 1 - c), (x, y, 1 - c)))
                first += mine
                sends += mine + relayed_out[-3:]
                own.append(pltpu.make_async_copy(src_refs[a], land_refs[a].at[me], local_sems.at[a]))
            elif kind == "scatter_by_chip":
                for k, (px, py) in enumerate([(1 - x, y), (x, 1 - y), (1 - x, 1 - y)]):
                    first.append(copy(k, src_refs[a].at[2 * px + py], 2 * x + y, (px, py, c)))
                    arrivals.append(copy(k, src_refs[a].at[2 * px + py], 2 * px + py, (px, py, c)))
                sends += first[-3:]
                own.append(pltpu.make_async_copy(src_refs[a].at[2 * x + y], land_refs[a].at[2 * x + y],
                                                 local_sems.at[a]))
            else:
                part = (lambda i: src_refs[a].at[i]) if kind == "scatter" else (lambda i: src_refs[a])
                for k, (peer, peer_idx) in enumerate(peers):
                    first.append(copy(k, part(peer_idx), me, peer))
                    arrivals.append(copy(k, part(peer_idx), peer_idx, peer))
                sends += first[-(N_DEV - 1):]
                own.append(pltpu.make_async_copy(part(me), land_refs[a].at[me], local_sems.at[a]))
        return first, relayed_in, relayed_out, arrivals, sends, own

    def wrapped(*refs):
        step = 0
        for axis, size in enumerate(grid):
            step = step * size + pl.program_id(axis)

        @pl.when(step == 0)
        def _():
            first, _, _, _, _, own = plan(refs)
            for cp in first + own:
                cp.start()

        if "gather_by_chip" in kinds:
            @pl.when(step == relay_step)
            def _():
                _, relayed_in, relayed_out, _, _, _ = plan(refs)
                for cp_in, cp_out in zip(relayed_in, relayed_out):
                    cp_in.wait_recv()
                    cp_out.start()

        body(*refs[:n_in], *refs[n_in + n:n_in + n + n_out], *refs[n_in + 2 * n + n_out:n_in + 2 * n + n_out + n_scr])

        @pl.when(step == steps - 1)
        def _():
            _, _, _, arrivals, sends, own = plan(refs)
            for cp in arrivals:
                cp.wait_recv()
            for cp in sends:
                cp.wait_send()
            for cp in own:
                cp.wait()

    anywhere = pl.BlockSpec(memory_space=pl.ANY)
    lands = [jax.ShapeDtypeStruct((N_DEV // 2 if k == "scatter_by_chip" else N_DEV,) + s.shape[-2:], s.dtype)
             for s, k in send]
    out = pl.pallas_call(
        wrapped, name=name, grid=grid,
        in_specs=list(in_specs) + [anywhere] * n, out_specs=list(out_specs) + [anywhere] * n,
        out_shape=list(out_shape) + lands,
        scratch_shapes=list(scratch_shapes) + [pltpu.SemaphoreType.DMA((n * (N_DEV - 1),)),
                                               pltpu.SemaphoreType.DMA((n * (N_DEV - 1),)),
                                               pltpu.SemaphoreType.DMA((n,))],
        compiler_params=_params(("arbitrary",) * len(grid)),
    )(*operands, *srcs)
    return out[:n_out], list(out[n_out:])


def _pair_swap(blocks):
    def body(src_ref, out_ref, send_sems, recv_sems):
        x, y, c = lax.axis_index("x"), lax.axis_index("y"), lax.axis_index("c")
        copies = [pltpu.make_async_remote_copy(
            src_ref=src_ref.at[2 * chip + (1 - c)], dst_ref=out_ref.at[chip], send_sem=send_sems.at[chip],
            recv_sem=recv_sems.at[chip], device_id=(x, y, 1 - c), device_id_type=pl.DeviceIdType.MESH)
            for chip in range(N_DEV // 2)]
        for cp in copies:
            cp.start()
        for cp in copies:
            cp.wait()

    return pl.pallas_call(
        body, name="pair_swap_grad_w_in",
        in_specs=[pl.BlockSpec(memory_space=pl.ANY)], out_specs=pl.BlockSpec(memory_space=pl.ANY),
        out_shape=jax.ShapeDtypeStruct((N_DEV // 2,) + blocks.shape[1:], blocks.dtype),
        scratch_shapes=[pltpu.SemaphoreType.DMA((N_DEV // 2,)), pltpu.SemaphoreType.DMA((N_DEV // 2,))],
    )(blocks)


def _pair_sum(blocks, swapped, core):
    _, rows, cols = swapped.shape
    tile_rows = _row_tile(rows)

    def body(core_ref, mine_ref, theirs_ref, o_ref):
        o_ref[...] = (mine_ref[...].astype(F32) + theirs_ref[...].astype(F32)).astype(o_ref.dtype)

    return pl.pallas_call(
        body, name="pair_sum_grad_w_in",
        grid_spec=pltpu.PrefetchScalarGridSpec(
            num_scalar_prefetch=1, grid=(N_DEV // 2, rows // tile_rows),
            in_specs=[pl.BlockSpec((None, tile_rows, cols), lambda j, i, core_ref: (2 * j + core_ref[0], i, 0)),
                      pl.BlockSpec((None, tile_rows, cols), lambda j, i, core_ref: (j, i, 0))],
            out_specs=pl.BlockSpec((None, tile_rows, cols), lambda j, i, core_ref: (j, i, 0))),
        out_shape=jax.ShapeDtypeStruct(swapped.shape, swapped.dtype),
        compiler_params=_params(("parallel", "parallel")),
    )(core, blocks, swapped)


def _sum_in_device_order(land_ref):
    acc = land_ref[0].astype(F32)
    for j in range(1, land_ref.shape[0]):
        acc = acc + land_ref[j].astype(F32)
    return acc


def _adam_math(w, g, m, v):
    c1 = 1.0 - ADAM_B1 ** ADAM_STEP
    c2 = 1.0 - ADAM_B2 ** ADAM_STEP
    m_new = ADAM_B1 * m + (1.0 - ADAM_B1) * g
    v_new = ADAM_B2 * v + (1.0 - ADAM_B2) * (g * g)
    delta = -ADAM_LR * ((m_new / c1) / (jnp.sqrt(v_new / c2) + ADAM_EPS) + ADAM_WD * w)
    return delta, m_new, v_new


def _row_tile(rows):
    return max(t for t in range(16, 385, 16) if rows % t == 0) if rows % 16 == 0 else rows


def _sum_update(land, w, m, v, name):
    slots, rows, cols = land.shape
    tile_rows = _row_tile(rows)

    def body(land_ref, w_ref, m_ref, v_ref, g_ref, d_ref, nm_ref, nv_ref):
        g = _sum_in_device_order(land_ref)
        g_ref[...] = g
        d_ref[...], nm_ref[...], nv_ref[...] = _adam_math(w_ref[...], g, m_ref[...], v_ref[...])

    tile = pl.BlockSpec((None, tile_rows, cols), lambda i: (0, i, 0))
    out = jax.ShapeDtypeStruct((1, rows, cols), F32)
    return pl.pallas_call(
        body, name=name, grid=(rows // tile_rows,),
        in_specs=[pl.BlockSpec((slots, tile_rows, cols), lambda i: (0, i, 0)), tile, tile, tile],
        out_specs=[tile] * 4, out_shape=[out] * 4,
        compiler_params=_params(("parallel",)),
    )(land, w, m, v)


def _to_bf16(arrays, flip):
    def body(*refs):
        for src_ref, dst_ref, t in zip(refs[:len(arrays)], refs[len(arrays):], flip):
            v = src_ref[...]
            dst_ref[...] = (v.T if t else v).astype(BF16)

    shapes = [a.shape[::-1] if t else a.shape for a, t in zip(arrays, flip)]
    whole = lambda shape: pl.BlockSpec(shape, lambda i: (0, 0))
    return pl.pallas_call(
        body, name="shards_to_bf16", grid=(1,),
        in_specs=[whole(a.shape) for a in arrays], out_specs=[whole(s) for s in shapes],
        out_shape=[jax.ShapeDtypeStruct(s, BF16) for s in shapes],
        compiler_params=_params(("arbitrary",)),
    )(*arrays)


N_GAINS = 3


def _sum_update_gains(land, ws, ms, vs):
    def body(land_ref, *refs):
        w_refs, m_refs, v_refs = (refs[k * N_GAINS:(k + 1) * N_GAINS] for k in range(3))
        loss_ref, out_refs = refs[3 * N_GAINS], refs[3 * N_GAINS + 1:]
        rows = _sum_in_device_order(land_ref)
        loss_ref[...] = rows[N_GAINS:N_GAINS + 1]
        for k in range(N_GAINS):
            g = rows[k:k + 1]
            g_ref, d_ref, nm_ref, nv_ref = out_refs[4 * k:4 * k + 4]
            g_ref[...] = g
            d_ref[...], nm_ref[...], nv_ref[...] = _adam_math(w_refs[k][...], g, m_refs[k][...], v_refs[k][...])

    row = pl.BlockSpec((1, D_MODEL), lambda i: (0, 0))
    out = jax.ShapeDtypeStruct((1, D_MODEL), F32)
    n_out = 1 + 4 * N_GAINS
    return pl.pallas_call(
        body, name="update_gains", grid=(1,),
        in_specs=[pl.BlockSpec(land.shape, lambda i: (0, 0, 0))] + [row] * (3 * N_GAINS),
        out_specs=[row] * n_out, out_shape=[out] * n_out,
    )(land, *ws, *ms, *vs)


GROUP_FFN = ("w_ffn_in", "w_ffn_out")
GROUP_MIX = ("w_sb_up", "w_dil_up", "w_out")
COL_SHARDED = ("w_in", "w_sb_up", "w_dil_up", "w_ffn_in")
TRANSPOSED = ("w_in", "w_ffn_in")
MIX_UP = ("w_sb_up", "w_dil_up")


def _full_from_shards(name, slots):
    _, r, c = slots.shape
    if name in TRANSPOSED:
        return slots.reshape(N_DEV * r, c).T
    if name in COL_SHARDED:
        return slots.transpose(1, 0, 2).reshape(r, N_DEV * c)
    return slots.reshape(N_DEV * r, c)


def _row_shards(full):
    rows, cols = full.shape
    return full.reshape(N_DEV, rows // N_DEV, cols)


def _local_step(x, target, g_mix, g_ffn, g_fin, w_in_t, shards=None, rest=None):
    gather = lambda names, kind: None if shards is None else [(shards[n], kind) for n in names]
    scatter = lambda blocks: None if shards is None else [(t, "scatter") for t in blocks]
    landed = lambda blocks, lands: lands if lands else blocks

    w = {"w_in": w_in_t}
    if shards is None:
        w.update(rest)
        w["w_ffn_in"] = rest["w_ffn_in"].T
        w.update({n: rest[n].T for n in MIX_UP})
    (qkv_sb, qkv_dl, gates, u), _ = _norm_proj(x, g_mix, w["w_in"])
    qkv_sb = qkv_sb.reshape(B_LOC, SEQ, 3 * SB_WIDTH)
    qkv_dl = qkv_dl.reshape(B_LOC, SEQ, 3 * DIL_WIDTH)
    (o_sb,), lands = _sb_fwd(qkv_sb, gather(GROUP_FFN, "gather_by_chip"))
    if lands:
        w["w_ffn_in"], w["w_ffn_out"] = lands[0].reshape(2 * D_FF, D_MODEL), _full_from_shards("w_ffn_out", lands[1])
    o_sb = o_sb.reshape(TOK, SB_WIDTH)
    (o_dl, lse), lands = _dil_fwd(qkv_dl, gather(GROUP_MIX, "gather"))
    w.update({n: t.reshape(D_MODEL, -1) if n in MIX_UP else _full_from_shards(n, t)
              for n, t in zip(GROUP_MIX, lands)})
    o_dl = o_dl.reshape(TOK, DIL_OUT)

    loss, dx1, merged, u2, act, dh, dx2, dg_fin, dg_ffn = _mix_ffn_fwd_bwd(
        x, o_sb, o_dl, gates, w["w_sb_up"], w["w_dil_up"], w["w_out"], target, g_ffn, g_fin,
        w["w_ffn_in"], w["w_ffn_out"])
    dgates, dy_sb, dy_dl, do_sb, do_dl, dsum = _mix_bwd(dx1, o_sb, o_dl, gates, w["w_sb_up"], w["w_dil_up"], w["w_out"])
    blocks = {
        "w_sb_up": _atb(o_sb, dy_sb, "grad_w_sb_up", SB_WIDTH, D_MODEL, col_blocks=N_DEV),
        "w_dil_up": _atb(o_dl, dy_dl, "grad_w_dil_up", DIL_OUT, D_MODEL, col_blocks=N_DEV),
        "w_out": _row_shards(_atb(merged, dx1, "grad_w_out", D_MODEL, D_MODEL)),
        "w_ffn_in": _row_shards(_atb_cols(dh, u2, "grad_w_ffn_in", 512)),
        "w_ffn_out": _row_shards(_atb_cols(act, dx2, "grad_w_ffn_out", 256)),
    }
    grads = {}

    early, late = GROUP_FFN, GROUP_MIX
    early_blocks = [blocks[n] for n in early]
    (dq_sb, dk_sb, dv_sb), lands = _sb_bwd(qkv_sb, do_sb.reshape(B_LOC, SEQ, SB_WIDTH), scatter(early_blocks))
    grads.update(zip(early, landed(early_blocks, lands)))
    as_batch = lambda t: t.reshape(B_LOC, SEQ, DIL_OUT)
    late_blocks = [blocks[n] for n in late]
    d_dl, lands = _dil_bwd(qkv_dl, as_batch(do_dl), lse, as_batch(dsum), scatter(late_blocks))
    grads.update(zip(late, landed(late_blocks, lands)))
    flat = lambda t: t.reshape(TOK, -1)
    dproj = ([flat(dq_sb), flat(dk_sb), flat(dv_sb)]
             + [flat(d_dl[3 * grp + part]) for part in range(3) for grp in range(DIL_GROUPS)] + [dgates])

    w_in_blocks = _row_shards(_atb_pieces(u, dproj, "grad_w_in", D_MODEL // 2))
    if shards is None:
        grads["w_in"] = w_in_blocks
        send = None
    else:
        core = lax.axis_index("c").astype(jnp.int32).reshape(1)
        send = [(_pair_sum(w_in_blocks, _pair_swap(w_in_blocks), core), "scatter_by_chip")]
    (grad_x, dg_mix), lands = _proj_bwd(dproj, dx1, x, g_mix, w["w_in"], send)
    if lands:
        grads["w_in"] = lands[0]
    gain_grads = jnp.concatenate([dg_mix, dg_ffn, dg_fin], axis=0)
    return loss, grad_x, gain_grads, grads


def kernel(x, norm_mix_g, w_in, w_sb_up, w_dil_up, w_out, norm_ffn_g, w_ffn_in, w_ffn_out, norm_final_g, loss_target, m_norm_mix_g, m_w_in, m_w_sb_up, m_w_dil_up, m_w_out, m_norm_ffn_g, m_w_ffn_in, m_w_ffn_out, m_norm_final_g, v_norm_mix_g, v_w_in, v_w_sb_up, v_w_dil_up, v_w_out, v_norm_ffn_g, v_w_ffn_in, v_w_ffn_out, v_norm_final_g):
    mats = {"w_in": w_in, "w_sb_up": w_sb_up, "w_dil_up": w_dil_up, "w_out": w_out,
            "w_ffn_in": w_ffn_in, "w_ffn_out": w_ffn_out}
    moments_m = {"w_in": m_w_in, "w_sb_up": m_w_sb_up, "w_dil_up": m_w_dil_up, "w_out": m_w_out,
                 "w_ffn_in": m_w_ffn_in, "w_ffn_out": m_w_ffn_out}
    moments_v = {"w_in": v_w_in, "w_sb_up": v_w_sb_up, "w_dil_up": v_w_dil_up, "w_out": v_w_out,
                 "w_ffn_in": v_w_ffn_in, "w_ffn_out": v_w_ffn_out}
    stored = lambda t, name: t.transpose(0, 2, 1) if name in TRANSPOSED else t
    mats = {name: stored(t, name) for name, t in mats.items()}
    shards = dict(zip(mats, _to_bf16([t[0] for t in mats.values()], [name in MIX_UP for name in mats])))
    gathered_w_in = _all_gather(shards.pop("w_in"), "all_gather_w_in")
    g_fin = norm_final_g.reshape(1, D_MODEL)
    loss, grad_x, gain_grads, grad_slots = _local_step(
        x.reshape(TOK, D_MODEL), loss_target.reshape(TOK, D_MODEL), norm_mix_g, norm_ffn_g, g_fin,
        gathered_w_in.reshape(IN_WIDTH, D_MODEL), shards=shards)

    gain_rows = jnp.concatenate([gain_grads, jnp.tile(loss, (1, D_MODEL // LANES)),
                                 jnp.zeros((8 - 4, D_MODEL), F32)], axis=0)
    row = lambda t: t.reshape(1, D_MODEL)
    loss_row, *gain_outs = _sum_update_gains(
        _all_gather(gain_rows, "all_gather_gains"),
        [norm_mix_g, norm_ffn_g, g_fin], [m_norm_mix_g, m_norm_ffn_g, row(m_norm_final_g)],
        [v_norm_mix_g, v_norm_ffn_g, row(v_norm_final_g)])

    out_g, out_d, out_m, out_v = {}, {}, {}, {}
    for name, slots in grad_slots.items():
        out_g[name], out_d[name], out_m[name], out_v[name] = [stored(t, name) for t in _sum_update(
            slots, mats[name], stored(moments_m[name], name), stored(moments_v[name], name), "update_" + name)]
    for idx, name in enumerate(("norm_mix_g", "norm_ffn_g", "norm_final_g")):
        shape = (D_MODEL,) if name == "norm_final_g" else (1, D_MODEL)
        out_g[name], out_d[name], out_m[name], out_v[name] = [t.reshape(shape) for t in gain_outs[4 * idx:4 * idx + 4]]

    order = ("norm_mix_g", "w_in", "w_sb_up", "w_dil_up", "w_out", "norm_ffn_g", "w_ffn_in", "w_ffn_out",
             "norm_final_g")
    return (loss_row[0, 0], grad_x.reshape(B_LOC, SEQ, D_MODEL),
            *[out_g[n] for n in order], *[out_d[n] for n in order],
            *[out_m[n] for n in order], *[out_v[n] for n in order])
```

```python
import math

import jax
import jax.numpy as jnp
from jax import lax
from jax.experimental import pallas as pl
from jax.experimental.pallas import tpu as pltpu

F32 = jnp.float32
BF16 = jnp.bfloat16

N_DEV = 8
D_MODEL = 1024
SEQ = 2048
B_LOC = 2
TOK = B_LOC * SEQ
HEAD_DIM = 64
SB_WIDTH = 512
DIL_WIDTH = 768
DIL_OUT = 256
QKV_WIDTH = 3 * SB_WIDTH + 3 * DIL_WIDTH
IN_WIDTH = QKV_WIDTH + 2 * D_MODEL
D_FF = 2816
DIL_PAIRS = ((128, 1), (512, 4), (2048, 16))
DIL_HEADS = 12
RMS_EPS = 1e-6
ALIBI_MAX_BIAS = 8.0
QK_SCALE = 1.0 / math.sqrt(HEAD_DIM)
BLK = 128
LANES = 128
NEG_BIG = -1e30

ADAM_LR = 0.001
ADAM_B1 = 0.9
ADAM_B2 = 0.999
ADAM_EPS = 1e-08
ADAM_WD = 0.01
ADAM_STEP = 10

VMEM_LIMIT = 58 * 1024 * 1024


def _dot(a, b):
    return jnp.dot(a, b, preferred_element_type=F32)


def _dot_nt(a, b):
    return lax.dot_general(a, b, (((1,), (1,)), ((), ())), preferred_element_type=F32)


def _dot_tn(a, b):
    return lax.dot_general(a, b, (((0,), (0,)), ((), ())), preferred_element_type=F32)


def _sigmoid(z):
    return 1.0 / (1.0 + jnp.exp(-z))


def _split_bf16(v):
    hi = v.astype(BF16)
    lo = (v - hi.astype(F32)).astype(BF16)
    return hi, lo


def _chunks(width, step=512):
    out, c = [], 0
    while c < width:
        w = min(step, width - c)
        out.append((c, w))
        c += w
    return out


def _resident(shape):
    nd = len(shape)
    return pl.BlockSpec(shape, lambda *_: (0,) * nd, pipeline_mode=pl.Buffered(1))


def _params(sem):
    return pltpu.CompilerParams(dimension_semantics=sem, vmem_limit_bytes=VMEM_LIMIT)


def _rms_fwd(x, g):
    r = lax.rsqrt(jnp.mean(x * x, axis=-1, keepdims=True) + RMS_EPS)
    n = x * r
    return n, r, n * g


def _rms_bwd(dy, n, r, g):
    dg = jnp.sum(dy * n, axis=0, keepdims=True)
    dn = dy * g
    dx = r * (dn - n * jnp.mean(dn * n, axis=-1, keepdims=True))
    return dx, dg


TM = 256
TM_WIDE = 512


def _norm_proj(x, g, w_in_t, send=None):
    def body(x_ref, g_ref, w_ref, sb_ref, dl_ref, gate_ref, u_ref):
        _, _, u = _rms_fwd(x_ref[...], g_ref[...])
        u = u.astype(BF16)
        u_ref[...] = u
        for c0, w in _chunks(3 * SB_WIDTH):
            sb_ref[:, c0:c0 + w] = _dot_nt(u, w_ref[c0:c0 + w, :]).astype(BF16)
        for c0, w in _chunks(3 * DIL_WIDTH):
            dl_ref[:, c0:c0 + w] = _dot_nt(u, w_ref[3 * SB_WIDTH + c0:3 * SB_WIDTH + c0 + w, :])
        for c0, w in _chunks(2 * D_MODEL):
            gate_ref[:, c0:c0 + w] = _dot_nt(u, w_ref[QKV_WIDTH + c0:QKV_WIDTH + c0 + w, :])

    return _call(
        body, send, name="norm_proj", grid=(TOK // TM_WIDE,),
        in_specs=[pl.BlockSpec((TM_WIDE, D_MODEL), lambda i: (i, 0)), _resident((1, D_MODEL)),
                  _resident((IN_WIDTH, D_MODEL))],
        out_specs=[pl.BlockSpec((TM_WIDE, 3 * SB_WIDTH), lambda i: (i, 0)),
                   pl.BlockSpec((TM_WIDE, 3 * DIL_WIDTH), lambda i: (i, 0)),
                   pl.BlockSpec((TM_WIDE, 2 * D_MODEL), lambda i: (i, 0)),
                   pl.BlockSpec((TM_WIDE, D_MODEL), lambda i: (i, 0))],
        out_shape=[jax.ShapeDtypeStruct((TOK, 3 * SB_WIDTH), BF16),
                   jax.ShapeDtypeStruct((TOK, 3 * DIL_WIDTH), F32),
                   jax.ShapeDtypeStruct((TOK, 2 * D_MODEL), F32),
                   jax.ShapeDtypeStruct((TOK, D_MODEL), BF16)],
        scratch_shapes=[], semantics=("parallel",), operands=(x, g, w_in_t))


FF_CHUNK = 1024


def _mix_ffn_fwd_bwd(x, o_sb, o_dl, gates, w_sb_up, w_dil_up, w_out, target, g_ffn, g_fin, w_ffn_in_t, w_ffn_out):
    def body(x_ref, osb_ref, odl_ref, gate_ref, wsb_ref, wdl_ref, wo_ref, t_ref, gffn_ref, gfin_ref, win_ref, wout_ref,
             loss_ref, dx1_ref, mg_ref, u2_ref, act_ref, dh_ref, dx2_ref, dgfin_ref, dgffn_ref, h_scr):
        i = pl.program_id(0)

        @pl.when(i == 0)
        def _():
            loss_ref[...] = jnp.zeros_like(loss_ref)
            dgfin_ref[...] = jnp.zeros_like(dgfin_ref)
            dgffn_ref[...] = jnp.zeros_like(dgffn_ref)

        y_sb = _dot_nt(osb_ref[...], wsb_ref[...])
        y_dl = _dot_nt(odl_ref[...].astype(BF16), wdl_ref[...])
        merged =(_sigmoid(gate_ref[:, :D_MODEL]) * y_sb
                  + _sigmoid(gate_ref[:, D_MODEL:]) * y_dl).astype(BF16)
        mg_ref[...] = merged
        x1 = x_ref[...] + _dot(merged, wo_ref[...])
        g_ffn_v = gffn_ref[...]
        g_fin_v = gfin_ref[...]
        n2, r2, u2 = _rms_fwd(x1, g_ffn_v)
        u2 = u2.astype(BF16)
        u2_ref[...] = u2
        x2 = x1
        for c0, w in _chunks(D_FF, FF_CHUNK):
            gate = _dot_nt(u2, win_ref[c0:c0 + w, :])
            up = _dot_nt(u2, win_ref[D_FF + c0:D_FF + c0 + w, :])
            h_scr[:, c0:c0 + w] = gate
            h_scr[:, D_FF + c0:D_FF + c0 + w] = up
            act = (gate * _sigmoid(gate) * up).astype(BF16)
            act_ref[:, c0:c0 + w] = act
            x2 = x2 + _dot(act, wout_ref[c0:c0 + w, :])
        n3, r3, y = _rms_fwd(x2, g_fin_v)
        err = y - t_ref[...]
        sq = jnp.sum(jnp.sum(err * err, axis=1, keepdims=True), axis=0, keepdims=True)
        loss_ref[...] += sq * (0.5 / D_MODEL)
        dx2, dgfin = _rms_bwd(err * (1.0 / D_MODEL), n3, r3, g_fin_v)
        dgfin_ref[...] += dgfin
        dx2_b = dx2.astype(BF16)
        dx2_ref[...] = dx2_b
        du2 = jnp.zeros((TM, D_MODEL), F32)
        for c0, w in _chunks(D_FF, FF_CHUNK):
            gate = h_scr[:, c0:c0 + w]
            up = h_scr[:, D_FF + c0:D_FF + c0 + w]
            dact = _dot_nt(dx2_b, wout_ref[c0:c0 + w, :])
            sg = _sigmoid(gate)
            dgate = (dact * up * (sg * (1.0 + gate * (1.0 - sg)))).astype(BF16)
            dup = (dact * (gate * sg)).astype(BF16)
            dh_ref[:, c0:c0 + w] = dgate
            dh_ref[:, D_FF + c0:D_FF + c0 + w] = dup
            du2 = du2 + _dot(dgate, win_ref[c0:c0 + w, :])
            du2 = du2 + _dot(dup, win_ref[D_FF + c0:D_FF + c0 + w, :])
        dx1_n, dgffn = _rms_bwd(du2, n2, r2, g_ffn_v)
        dgffn_ref[...] += dgffn
        dx1_ref[...] = dx2 + dx1_n

    tile = lambda w: pl.BlockSpec((TM, w), lambda i: (i, 0))
    acc = lambda w: pl.BlockSpec((1, w), lambda i: (0, 0))
    return pl.pallas_call(
        body, name="mix_ffn_fwd_bwd", grid=(TOK // TM,),
        in_specs=[tile(D_MODEL), tile(SB_WIDTH), tile(DIL_OUT), tile(2 * D_MODEL),
                  _resident((D_MODEL, SB_WIDTH)), _resident((D_MODEL, DIL_OUT)), _resident((D_MODEL, D_MODEL)),
                  tile(D_MODEL), _resident((1, D_MODEL)), _resident((1, D_MODEL)),
                  _resident((2 * D_FF, D_MODEL)), _resident((D_FF, D_MODEL))],
        out_specs=[acc(LANES), tile(D_MODEL), tile(D_MODEL), tile(D_MODEL), tile(D_FF), tile(2 * D_FF),
                   tile(D_MODEL), acc(D_MODEL), acc(D_MODEL)],
        out_shape=[jax.ShapeDtypeStruct((1, LANES), F32),
                   jax.ShapeDtypeStruct((TOK, D_MODEL), F32),
                   jax.ShapeDtypeStruct((TOK, D_MODEL), BF16),
                   jax.ShapeDtypeStruct((TOK, D_MODEL), BF16),
                   jax.ShapeDtypeStruct((TOK, D_FF), BF16),
                   jax.ShapeDtypeStruct((TOK, 2 * D_FF), BF16),
                   jax.ShapeDtypeStruct((TOK, D_MODEL), BF16),
                   jax.ShapeDtypeStruct((1, D_MODEL), F32),
                   jax.ShapeDtypeStruct((1, D_MODEL), F32)],
        scratch_shapes=[pltpu.VMEM((TM, 2 * D_FF), F32)],
        compiler_params=_params(("arbitrary",)),
    )(x, o_sb, o_dl, gates, w_sb_up, w_dil_up, w_out, target, g_ffn, g_fin, w_ffn_in_t, w_ffn_out)


def _mix_bwd(dx1, o_sb, o_dl, gates, w_sb_up, w_dil_up, w_out):
    def body(dx1_ref, osb_ref, odl_ref, gate_ref, wsb_ref, wdl_ref, wout_ref,
             dgate_ref, dysb_ref, dydl_ref, dosb_ref, dodl_ref, dsum_ref):
        dmerged = _dot_nt(dx1_ref[...].astype(BF16), wout_ref[...])
        o_dl = odl_ref[...]
        y_sb = _dot_nt(osb_ref[...], wsb_ref[...])
        y_dl = _dot_nt(o_dl.astype(BF16), wdl_ref[...])
        s_sb = _sigmoid(gate_ref[:, :D_MODEL])
        s_dl = _sigmoid(gate_ref[:, D_MODEL:])
        dgate_ref[:, :D_MODEL] = (dmerged * y_sb * (s_sb * (1.0 - s_sb))).astype(BF16)
        dgate_ref[:, D_MODEL:] = (dmerged * y_dl * (s_dl * (1.0 - s_dl))).astype(BF16)
        dy_sb = (dmerged * s_sb).astype(BF16)
        dy_dl = (dmerged * s_dl).astype(BF16)
        dysb_ref[...] = dy_sb
        dydl_ref[...] = dy_dl
        dosb_ref[...] = _dot(dy_sb, wsb_ref[...]).astype(BF16)
        do_dl = _dot(dy_dl, wdl_ref[...])
        dodl_ref[...] = do_dl
        row = lax.broadcasted_iota(jnp.int32, (DIL_OUT, DIL_OUT), 0) // HEAD_DIM
        col = lax.broadcasted_iota(jnp.int32, (DIL_OUT, DIL_OUT), 1) // HEAD_DIM
        same_head = (row == col).astype(BF16)
        hi, lo = _split_bf16(do_dl * o_dl)
        dsum_ref[...] = _dot(hi, same_head) + _dot(lo, same_head)

    tile = lambda w: pl.BlockSpec((TM_WIDE, w), lambda i: (i, 0))
    return pl.pallas_call(
        body, name="mix_bwd", grid=(TOK // TM_WIDE,),
        in_specs=[tile(D_MODEL), tile(SB_WIDTH), tile(DIL_OUT), tile(2 * D_MODEL),
                  _resident((D_MODEL, SB_WIDTH)), _resident((D_MODEL, DIL_OUT)),
                  _resident((D_MODEL, D_MODEL))],
        out_specs=[tile(2 * D_MODEL), tile(D_MODEL), tile(D_MODEL), tile(SB_WIDTH), tile(DIL_OUT),
                   tile(DIL_OUT)],
        out_shape=[jax.ShapeDtypeStruct((TOK, 2 * D_MODEL), BF16),
                   jax.ShapeDtypeStruct((TOK, D_MODEL), BF16),
                   jax.ShapeDtypeStruct((TOK, D_MODEL), BF16),
                   jax.ShapeDtypeStruct((TOK, SB_WIDTH), BF16),
                   jax.ShapeDtypeStruct((TOK, DIL_OUT), F32),
                   jax.ShapeDtypeStruct((TOK, DIL_OUT), F32)],
        compiler_params=_params(("parallel",)),
    )(dx1, o_sb, o_dl, gates, w_sb_up, w_dil_up, w_out)


def _proj_bwd(dproj, dx1, x, g, w_in_t, send=None):
    widths = [p.shape[1] for p in dproj]

    def body(*refs):
        dx1_ref, x_ref, g_ref, w_ref, dx_ref, dg_ref = refs[len(widths):]

        @pl.when(pl.program_id(0) == 0)
        def _():
            dg_ref[...] = jnp.zeros_like(dg_ref)

        du = jnp.zeros((TM, D_MODEL), F32)
        c0 = 0
        for dp_ref, w in zip(refs, widths):
            du = du + _dot(dp_ref[...].astype(BF16), w_ref[c0:c0 + w, :])
            c0 += w
        g_v = g_ref[...]
        n, r, _ = _rms_fwd(x_ref[...], g_v)
        dx, dg = _rms_bwd(du, n, r, g_v)
        dg_ref[...] += dg
        dx_ref[...] = dx1_ref[...] + dx

    tile = lambda w: pl.BlockSpec((TM, w), lambda i: (i, 0))
    return _call(
        body, send, name="proj_bwd", grid=(TOK // TM,),
        in_specs=[tile(w) for w in widths] + [tile(D_MODEL), tile(D_MODEL), _resident((1, D_MODEL)),
                                              _resident((IN_WIDTH, D_MODEL))],
        out_specs=[tile(D_MODEL), pl.BlockSpec((1, D_MODEL), lambda i: (0, 0))],
        out_shape=[jax.ShapeDtypeStruct((TOK, D_MODEL), F32),
                   jax.ShapeDtypeStruct((1, D_MODEL), F32)],
        scratch_shapes=[], semantics=("arbitrary",), operands=(*dproj, dx1, x, g, w_in_t))


def _atb_pieces(a, pieces, name, tm, tk=512):
    m = a.shape[1]
    widths = [p.shape[1] for p in pieces]
    n = sum(widths)
    nk = TOK // tk

    def body(a_ref, *refs):
        o_ref, acc_ref = refs[len(widths):]
        k = pl.program_id(1)

        @pl.when(k == 0)
        def _():
            acc_ref[...] = jnp.zeros_like(acc_ref)

        a_v = a_ref[...]
        c0 = 0
        for p_ref, w in zip(refs, widths):
            acc_ref[:, c0:c0 + w] += _dot_tn(a_v, p_ref[...].astype(BF16))
            c0 += w

        @pl.when(k == nk - 1)
        def _():
            for c0, w in _chunks(n):
                o_ref[c0:c0 + w, :] = acc_ref[:, c0:c0 + w].T.astype(BF16)

    return pl.pallas_call(
        body, name=name, grid=(m // tm, nk),
        in_specs=[pl.BlockSpec((tk, tm), lambda i, k: (k, i))]
                 + [pl.BlockSpec((tk, w), lambda i, k: (k, 0)) for w in widths],
        out_specs=pl.BlockSpec((n, tm), lambda i, k: (0, i)),
        out_shape=jax.ShapeDtypeStruct((n, m), BF16),
        scratch_shapes=[pltpu.VMEM((tm, n), F32)],
        compiler_params=_params(("parallel", "arbitrary")),
    )(a, *pieces)


def _atb_cols(a, b, name, cols, tk=512):
    m, other = a.shape[1], b.shape[1]

    def body(a_ref, b_ref, o_ref):
        acc = jnp.zeros(o_ref.shape, F32)
        for k0 in range(0, TOK, tk):
            acc = acc + _dot_tn(a_ref[k0:k0 + tk, :], b_ref[k0:k0 + tk, :])
        o_ref[...] = acc.astype(BF16)

    assert m % cols == 0
    return pl.pallas_call(
        body, name=name, grid=(m // cols,),
        in_specs=[pl.BlockSpec((TOK, cols), lambda r: (0, r)), _resident((TOK, other))],
        out_specs=pl.BlockSpec((cols, other), lambda r: (r, 0)),
        out_shape=jax.ShapeDtypeStruct((m, other), BF16),
        compiler_params=_params(("parallel",)),
    )(a, b)


def _atb(a, b, name, tm, tn, col_blocks=0, tk=1024):
    m, n = a.shape[1], b.shape[1]
    nk = TOK // tk

    def body(a_ref, b_ref, o_ref, acc_ref):
        k = pl.program_id(2)

        @pl.when(k == 0)
        def _():
            acc_ref[...] = jnp.zeros_like(acc_ref)

        acc_ref[...] += _dot_tn(a_ref[...].astype(BF16), b_ref[...].astype(BF16))

        @pl.when(k == nk - 1)
        def _():
            if col_blocks:
                width = n // col_blocks
                for blk in range(col_blocks):
                    o_ref[blk] = acc_ref[:, blk * width:(blk + 1) * width].astype(BF16)
            else:
                o_ref[...] = acc_ref[...].astype(BF16)

    if col_blocks:
        out_spec = pl.BlockSpec((col_blocks, tm, n // col_blocks), lambda i, j, k: (0, i, 0))
        out_shape = jax.ShapeDtypeStruct((col_blocks, m, n // col_blocks), BF16)
    else:
        out_spec = pl.BlockSpec((tm, tn), lambda i, j, k: (i, j))
        out_shape = jax.ShapeDtypeStruct((m, n), BF16)
    return pl.pallas_call(
        body, name=name, grid=(m // tm, n // tn, nk),
        in_specs=[pl.BlockSpec((tk, tm), lambda i, j, k: (k, i)),
                  pl.BlockSpec((tk, tn), lambda i, j, k: (k, j))],
        out_specs=out_spec, out_shape=out_shape,
        scratch_shapes=[pltpu.VMEM((tm, tn), F32)],
        compiler_params=_params(("parallel", "parallel", "arbitrary")),
    )(a, b)


SB_PAIRS = SB_WIDTH // LANES


def _two_heads(v, lane0):
    zero = jnp.zeros_like(v)
    return jnp.where(lane0, v, zero), jnp.where(lane0, zero, v)


SB_QBLK = 256
N_SB_STEPS = SEQ // SB_QBLK


SB_KCHUNK = 2 * BLK
SB_ROWS = 2 * SB_QBLK
SB_DEAD = -104.0


def _log_keep(z):
    neg_z = -z
    return jnp.minimum(neg_z, 0.0) - jnp.log(1.0 + jnp.exp(jnp.minimum(z, neg_z)))


def _stack_heads(v, lane0):
    return jnp.concatenate(_two_heads(v, lane0), axis=0)


def _block_sums(v, tri, split=True):
    halves = (v[:, :BLK], v[:, BLK:])
    stacked = jnp.concatenate(halves, axis=0)
    if split:
        hi, lo = _split_bf16(stacked)
        prod = _dot(jnp.concatenate([hi, lo], axis=0), tri)
        tri_sum = prod[:2 * SB_ROWS] + prod[2 * SB_ROWS:]
    else:
        tri_sum = _dot(stacked.astype(BF16), tri)
    sums = tuple(jnp.sum(h, axis=1, keepdims=True) for h in halves)
    return (tri_sum[:SB_ROWS], tri_sum[SB_ROWS:]), sums


def _sb_diag_mask():
    row = lax.broadcasted_iota(jnp.int32, (SB_ROWS, SB_KCHUNK), 0)
    col = lax.broadcasted_iota(jnp.int32, (SB_ROWS, SB_KCHUNK), 1)
    return col < jnp.where(row >= SB_QBLK, row - SB_QBLK, row)


def _sb_fwd(qkv, send=None):
    def body(q_ref, k_ref, v_ref, o_ref):
        i = pl.program_id(2)
        krow = lax.broadcasted_iota(jnp.int32, (BLK, BLK), 0)
        kcol = lax.broadcasted_iota(jnp.int32, (BLK, BLK), 1)
        later = (krow > kcol).astype(BF16)
        lane0 = lax.broadcasted_iota(jnp.int32, (SB_QBLK, LANES), 1) < HEAD_DIM
        q2 = _stack_heads(q_ref[0] * QK_SCALE, lane0)

        def chunk(c, carry, causal):
            acc, run = carry
            off = pl.multiple_of(c * SB_KCHUNK, SB_KCHUNK)
            z = _dot_nt(q2, k_ref[0, pl.ds(off, SB_KCHUNK), :])
            log_keep = _log_keep(z)
            if causal is not None:
                log_keep = jnp.where(causal, log_keep, 0.0)
            suffix, sums = _block_sums(log_keep, later)
            log_after = jnp.concatenate([suffix[0] + (run + sums[1]), suffix[1] + run], axis=1)
            a = jnp.exp(log_keep + z + log_after)
            if causal is not None:
                a = jnp.where(causal, a, 0.0)
            acc = acc + _dot(a.astype(BF16), v_ref[0, pl.ds(off, SB_KCHUNK), :])
            return acc, run + (sums[0] + sums[1])

        acc, run = chunk(i, (jnp.zeros((SB_ROWS, LANES), F32), jnp.zeros((SB_ROWS, 1), F32)), _sb_diag_mask())

        def some_alive(run):
            return (jnp.max(run) > SB_DEAD).astype(jnp.int32)

        def trip(state):
            t, _, acc, run = state
            acc, run = chunk(i - 1 - t, (acc, run), None)
            return t + 1, some_alive(run), acc, run

        _, _, acc, _ = lax.while_loop(lambda s: jnp.logical_and(s[0] < i, s[1] > 0), trip,
                                      (jnp.int32(0), some_alive(run), acc, run))
        o_ref[0] = jnp.where(lane0, acc[:SB_QBLK], acc[SB_QBLK:]).astype(BF16)

    blk = pl.BlockSpec((1, SB_QBLK, LANES), lambda b, h, i: (b, i, h))
    return _call(
        body, send, name="sb_fwd", grid=(B_LOC, SB_PAIRS, N_SB_STEPS),
        in_specs=[blk,
                  pl.BlockSpec((1, SEQ, LANES), lambda b, h, i: (b, 0, SB_PAIRS + h)),
                  pl.BlockSpec((1, SEQ, LANES), lambda b, h, i: (b, 0, 2 * SB_PAIRS + h))],
        out_specs=[blk], out_shape=[jax.ShapeDtypeStruct((B_LOC, SEQ, SB_WIDTH), BF16)],
        scratch_shapes=[], semantics=("parallel", "parallel", "arbitrary"), operands=(qkv, qkv, qkv))


def _sb_bwd(qkv, d_o, send=None):
    def body(q_ref, k_ref, v_ref, do_ref, dq_ref, dk_ref, dv_ref, dk_acc, dv_acc, z_scr, keep_scr):
        i = pl.program_id(2)
        krow = lax.broadcasted_iota(jnp.int32, (BLK, BLK), 0)
        kcol = lax.broadcasted_iota(jnp.int32, (BLK, BLK), 1)
        upto = (krow <= kcol).astype(BF16)
        earlier = (krow < kcol).astype(BF16)
        lane0 = lax.broadcasted_iota(jnp.int32, (SB_QBLK, LANES), 1) < HEAD_DIM
        q2 = _stack_heads(q_ref[0] * QK_SCALE, lane0)
        do2 = _stack_heads(do_ref[0], lane0)

        def keep_sum(c, causal):
            off = pl.multiple_of(c * SB_KCHUNK, SB_KCHUNK)
            z = _dot_nt(q2, k_ref[0, pl.ds(off, SB_KCHUNK), :])
            log_keep = _log_keep(z)
            if causal is not None:
                log_keep = jnp.where(causal, log_keep, 0.0)
            z_scr[c] = z
            keep_scr[c] = log_keep
            return jnp.sum(log_keep, axis=1, keepdims=True)

        def some_alive(run):
            return (jnp.max(run) > SB_DEAD).astype(jnp.int32)

        def scan(state):
            t, _, run = state
            run = run + keep_sum(i - 1 - t, None)
            return t + 1, some_alive(run), run

        diag_sum = keep_sum(i, _sb_diag_mask())
        walked, _, tot2 = lax.while_loop(lambda s: jnp.logical_and(s[0] < i, s[1] > 0), scan,
                                         (jnp.int32(0), some_alive(diag_sum), diag_sum))
        first = i - walked

        @pl.when(i == 0)
        def _():
            dk_acc[...] = jnp.zeros_like(dk_acc)
            dv_acc[...] = jnp.zeros_like(dv_acc)

        def chunk(c, carry, causal):
            dq, pre_keep, pre_e = carry
            off = pl.multiple_of(c * SB_KCHUNK, SB_KCHUNK)
            k_c = k_ref[0, pl.ds(off, SB_KCHUNK), :]
            v_c = v_ref[0, pl.ds(off, SB_KCHUNK), :]
            d_a = _dot_nt(do2, v_c)
            log_keep = keep_scr[c]
            log_beta = log_keep + z_scr[c]
            prefix, sums = _block_sums(log_keep, upto)
            inclusive = jnp.concatenate([prefix[0], prefix[1] + sums[0]], axis=1)
            a = jnp.exp(log_beta + ((tot2 - pre_keep) - inclusive))
            if causal is not None:
                a = jnp.where(causal, a, 0.0)
            e = d_a * a
            e_prefix, e_sums = _block_sums(e, earlier, split=False)
            before = jnp.concatenate([e_prefix[0] + pre_e, e_prefix[1] + (pre_e + e_sums[0])], axis=1)
            dz = e - (e + before) * jnp.exp(log_beta)
            if causal is not None:
                dz = jnp.where(causal, dz, 0.0)
            dz = dz.astype(BF16)
            dq = dq + _dot(dz, k_c)
            dk_acc[pl.ds(off, SB_KCHUNK), :] += _dot_tn(dz, q2)
            dv_acc[pl.ds(off, SB_KCHUNK), :] += _dot_tn(a.astype(BF16), do2)
            return dq, pre_keep + (sums[0] + sums[1]), pre_e + (e_sums[0] + e_sums[1])

        zero_col = jnp.zeros((SB_ROWS, 1), F32)
        carry = lax.fori_loop(first, i, lambda t, c: chunk(t, c, None),
                              (jnp.zeros((SB_ROWS, LANES), F32), zero_col, zero_col))
        dq, _, _ = chunk(i, carry, _sb_diag_mask())
        dq_ref[0] = (jnp.where(lane0, dq[:SB_QBLK], dq[SB_QBLK:]) * QK_SCALE).astype(BF16)

        @pl.when(i == N_SB_STEPS - 1)
        def _():
            dk_ref[0] = dk_acc[...].astype(BF16)
            dv_ref[0] = dv_acc[...].astype(BF16)

    blk = pl.BlockSpec((1, SB_QBLK, LANES), lambda b, h, i: (b, i, h))
    whole = lambda c: pl.BlockSpec((1, SEQ, LANES), lambda b, h, i: (b, 0, c * SB_PAIRS + h))
    out = jax.ShapeDtypeStruct((B_LOC, SEQ, SB_WIDTH), BF16)
    return _call(
        body, send, name="sb_bwd", grid=(B_LOC, SB_PAIRS, N_SB_STEPS),
        in_specs=[blk, whole(1), whole(2), blk],
        out_specs=[blk, whole(0), whole(0)],
        out_shape=[out, out, out],
        scratch_shapes=[pltpu.VMEM((SEQ, LANES), F32), pltpu.VMEM((SEQ, LANES), F32),
                        pltpu.VMEM((N_SB_STEPS, SB_ROWS, SB_KCHUNK), F32),
                        pltpu.VMEM((N_SB_STEPS, SB_ROWS, SB_KCHUNK), F32)],
        semantics=("parallel", "parallel", "arbitrary"), operands=(qkv, qkv, qkv, d_o))


DIL_GROUPS = len(DIL_PAIRS)
DIL_QBLOCKS = SEQ // BLK


def _residue_rows(j, dilation):
    length = SEQ // dilation
    return pl.ds(j, length, stride=dilation) if dilation > 1 else pl.ds(0, length)


def _gather_residues(src_ref, dst_ref, dst_off, dilation, scale=None):
    length = SEQ // dilation
    for j in range(dilation):
        v = src_ref[_residue_rows(j, dilation), :]
        if scale is not None:
            v = v * scale
        dst_ref[dst_off + j * length:dst_off + (j + 1) * length, :] = v.astype(dst_ref.dtype)


def _scatter_residues(src_ref, src_off, dst_ref, dilation):
    length = SEQ // dilation
    for j in range(dilation):
        dst_ref[_residue_rows(j, dilation), :] = (
            src_ref[src_off + j * length:src_off + (j + 1) * length, :].astype(dst_ref.dtype))


def _dil_geometry(group, pair):
    dilation = DIL_PAIRS[group][1]
    row = lax.broadcasted_iota(jnp.int32, (2 * BLK, 2 * BLK), 0)
    col = lax.broadcasted_iota(jnp.int32, (2 * BLK, 2 * BLK), 1)
    second = row >= BLK
    steps = BLK + jnp.where(second, row - BLK, row) - col
    coef = -ALIBI_MAX_BIAS / DIL_HEADS * math.log(2.0)
    first_head = float(4 * group + 1) + 2.0 * pair.astype(F32)
    slope = jnp.exp(coef * (first_head + jnp.where(second, 1.0, 0.0)))
    bias = slope * (steps * dilation).astype(F32)
    valid = jnp.logical_and(steps >= 0, steps <= BLK)
    return bias, valid, col >= BLK


def _dil_tile_scores(q2, kk, geometry, has_prev):
    bias, valid, own = geometry
    ok = jnp.logical_and(valid, jnp.logical_or(own, has_prev))
    return jnp.where(ok, _dot_nt(q2, kk) - bias, NEG_BIG)


def _head_col(v, lane_mask):
    return jnp.max(jnp.where(lane_mask, v, NEG_BIG), axis=1, keepdims=True)


def _dil_fwd(qkv, send=None):
    def body(*refs):
        ins, (o_ref, lse_ref), (qs, ks, vs, o_res, lse_res) = refs[:9], refs[9:11], refs[11:16]
        o_grp, lse_grp = refs[16:19], refs[19:22]
        pair = pl.program_id(1)
        lane0 = lax.broadcasted_iota(jnp.int32, (BLK, LANES), 1) < HEAD_DIM
        ks[0:BLK, :] = jnp.zeros((BLK, LANES), BF16)
        vs[0:BLK, :] = jnp.zeros((BLK, LANES), BF16)
        for grp, (_, dilation) in enumerate(DIL_PAIRS):
            q_ref, k_ref, v_ref = ins[3 * grp:3 * grp + 3]
            per_residue = DIL_QBLOCKS // dilation
            _gather_residues(q_ref, qs, 0, dilation, QK_SCALE)
            _gather_residues(k_ref, ks, BLK, dilation)
            _gather_residues(v_ref, vs, BLK, dilation)
            geometry = _dil_geometry(grp, pair)

            def step(blk, _):
                off = pl.multiple_of(blk * BLK, BLK)
                q2 = _stack_heads(qs[pl.ds(off, BLK), :], lane0)
                s = _dil_tile_scores(q2, ks[pl.ds(off, 2 * BLK), :], geometry, blk % per_residue != 0)
                m = jnp.max(s, axis=1, keepdims=True)
                p = jnp.exp(s - m)
                den = jnp.sum(p, axis=1, keepdims=True)
                out = _dot(p.astype(BF16), vs[pl.ds(off, 2 * BLK), :]) / den
                lse = m + jnp.log(den)
                o_res[pl.ds(off, BLK), :] = jnp.where(lane0, out[:BLK], out[BLK:])
                lse_res[pl.ds(off, BLK), :] = jnp.where(lane0, lse[:BLK], lse[BLK:])
                return 0

            lax.fori_loop(0, DIL_QBLOCKS, step, 0, unroll=8)
            _scatter_residues(o_res, 0, o_grp[grp], dilation)
            _scatter_residues(lse_res, 0, lse_grp[grp], dilation)

        for r0 in range(0, SEQ, 2 * BLK):
            rows = slice(r0, r0 + 2 * BLK)
            ls = [lse_grp[g][rows, :] for g in range(DIL_GROUPS)]
            m = jnp.maximum(jnp.maximum(ls[0], ls[1]), ls[2])
            w = [jnp.exp(l - m) for l in ls]
            den = w[0] + w[1] + w[2]
            o_ref[rows, :] = (w[0] * o_grp[0][rows, :] + w[1] * o_grp[1][rows, :] + w[2] * o_grp[2][rows, :]) / den
            lse_ref[rows, :] = m + jnp.log(den)

    def col(part, grp):
        return pl.BlockSpec((None, SEQ, LANES), lambda b, p: (b, 0, 6 * part + 2 * grp + p))

    out_spec = pl.BlockSpec((None, SEQ, LANES), lambda b, p: (b, 0, p))
    out = jax.ShapeDtypeStruct((B_LOC, SEQ, DIL_OUT), F32)
    return _call(
        body, send, name="dil_fwd", grid=(B_LOC, DIL_OUT // LANES),
        in_specs=[col(part, grp) for grp in range(DIL_GROUPS) for part in range(3)],
        out_specs=[out_spec, out_spec], out_shape=[out, out],
        scratch_shapes=[pltpu.VMEM((SEQ, LANES), BF16), pltpu.VMEM((SEQ + BLK, LANES), BF16),
                        pltpu.VMEM((SEQ + BLK, LANES), BF16), pltpu.VMEM((SEQ, LANES), F32),
                        pltpu.VMEM((SEQ, LANES), F32)] + [pltpu.VMEM((SEQ, LANES), F32)] * (2 * DIL_GROUPS),
        semantics=("parallel", "parallel"), operands=[qkv] * 9)


def _dil_bwd(qkv, d_o, lse, dsum, send=None):
    def body(*refs):
        ins, (do_ref, lse_ref, dsum_ref), outs = refs[:9], refs[9:12], refs[12:21]
        qs, ks, vs, dos, lse_res, dsum_res, dq_res, dk_acc, dv_acc = refs[21:]
        pair = pl.program_id(1)
        lane0 = lax.broadcasted_iota(jnp.int32, (BLK, LANES), 1) < HEAD_DIM
        lane1 = jnp.logical_not(lane0)
        ks[0:BLK, :] = jnp.zeros((BLK, LANES), BF16)
        vs[0:BLK, :] = jnp.zeros((BLK, LANES), BF16)
        for grp, (_, dilation) in enumerate(DIL_PAIRS):
            q_ref, k_ref, v_ref = ins[3 * grp:3 * grp + 3]
            dq_ref, dk_ref, dv_ref = outs[3 * grp:3 * grp + 3]
            per_residue = DIL_QBLOCKS // dilation
            _gather_residues(q_ref, qs, 0, dilation, QK_SCALE)
            _gather_residues(k_ref, ks, BLK, dilation)
            _gather_residues(v_ref, vs, BLK, dilation)
            _gather_residues(do_ref, dos, 0, dilation)
            _gather_residues(lse_ref, lse_res, 0, dilation)
            _gather_residues(dsum_ref, dsum_res, 0, dilation)
            dk_acc[...] = jnp.zeros_like(dk_acc)
            dv_acc[...] = jnp.zeros_like(dv_acc)
            geometry = _dil_geometry(grp, pair)

            def step(blk, _):
                off = pl.multiple_of(blk * BLK, BLK)
                q2 = _stack_heads(qs[pl.ds(off, BLK), :], lane0)
                do2 = _stack_heads(dos[pl.ds(off, BLK), :], lane0)
                kk = ks[pl.ds(off, 2 * BLK), :]
                vv = vs[pl.ds(off, 2 * BLK), :]
                lse_blk = lse_res[pl.ds(off, BLK), :]
                dsum_blk = dsum_res[pl.ds(off, BLK), :]
                lse2 = jnp.concatenate([_head_col(lse_blk, lane0), _head_col(lse_blk, lane1)], axis=0)
                dsum2 = jnp.concatenate([_head_col(dsum_blk, lane0), _head_col(dsum_blk, lane1)], axis=0)
                s = _dil_tile_scores(q2, kk, geometry, blk % per_residue != 0)
                p = jnp.exp(s - lse2)
                ds = (p * (_dot_nt(do2, vv) - dsum2)).astype(BF16)
                dq2 = _dot(ds, kk)
                dq_res[pl.ds(off, BLK), :] = jnp.where(lane0, dq2[:BLK], dq2[BLK:]) * QK_SCALE
                dk_acc[pl.ds(off, 2 * BLK), :] += _dot_tn(ds, q2)
                dv_acc[pl.ds(off, 2 * BLK), :] += _dot_tn(p.astype(BF16), do2)
                return 0

            lax.fori_loop(0, DIL_QBLOCKS, step, 0, unroll=8)
            _scatter_residues(dq_res, 0, dq_ref, dilation)
            _scatter_residues(dk_acc, BLK, dk_ref, dilation)
            _scatter_residues(dv_acc, BLK, dv_ref, dilation)

    def col(part, grp):
        return pl.BlockSpec((None, SEQ, LANES), lambda b, p: (b, 0, 6 * part + 2 * grp + p))

    slot = pl.BlockSpec((None, SEQ, LANES), lambda b, p: (b, 0, p))
    out = jax.ShapeDtypeStruct((B_LOC, SEQ, DIL_OUT), F32)
    return _call(
        body, send, name="dil_bwd", grid=(B_LOC, DIL_OUT // LANES),
        in_specs=[col(part, grp) for grp in range(DIL_GROUPS) for part in range(3)] + [slot] * 3,
        out_specs=[slot] * 9, out_shape=[out] * 9,
        scratch_shapes=[pltpu.VMEM((SEQ, LANES), BF16), pltpu.VMEM((SEQ + BLK, LANES), BF16),
                        pltpu.VMEM((SEQ + BLK, LANES), BF16), pltpu.VMEM((SEQ, LANES), BF16),
                        pltpu.VMEM((SEQ, LANES), F32), pltpu.VMEM((SEQ, LANES), F32),
                        pltpu.VMEM((SEQ, LANES), F32), pltpu.VMEM((SEQ + BLK, LANES), F32),
                        pltpu.VMEM((SEQ + BLK, LANES), F32)],
        semantics=("parallel", "parallel"), operands=[qkv] * 9 + [d_o, lse, dsum])


def _peers():
    x, y, c = lax.axis_index("x"), lax.axis_index("y"), lax.axis_index("c")
    me = 4 * x + 2 * y + c
    peers = []
    for mask in range(1, N_DEV):
        px = 1 - x if mask & 4 else x
        py = 1 - y if mask & 2 else y
        pc = 1 - c if mask & 1 else c
        peers.append(((px, py, pc), 4 * px + 2 * py + pc))
    return me, peers


def _all_gather(shard, name):
    rows, cols = shard.shape
    by_rows = rows % 32 == 0

    def body(src_ref, out_ref, send_sems, recv_sems, local_sem):
        x, y, c = lax.axis_index("x"), lax.axis_index("y"), lax.axis_index("c")
        me, sibling = (x, y, c), (x, y, 1 - c)
        x_chip, y_chip, across = (1 - x, y), (x, 1 - y), (1 - x, 1 - y)

        def slot(block, half=None):
            ref = out_ref.at[4 * block[0] + 2 * block[1] + block[2]]
            if half is None:
                return ref
            return ref.at[pl.ds(half * (rows // 2), rows // 2)] if by_rows else \
                ref.at[:, pl.ds(half * (cols // 2), cols // 2)]

        def copy(k, block, to, src=None, half=None):
            return pltpu.make_async_remote_copy(
                src_ref=slot(block, half) if src is None else src, dst_ref=slot(block, half),
                send_sem=send_sems.at[k], recv_sem=recv_sems.at[k], device_id=to,
                device_id_type=pl.DeviceIdType.MESH)

        mine = pltpu.make_async_copy(src_ref, slot(me), local_sem)
        mine.start()
        sent = [copy(0, me, sibling, src=src_ref), copy(1, me, (*x_chip, c), src=src_ref),
                copy(2, me, (*y_chip, c), src=src_ref)]
        for cp in sent:
            cp.start()

        def arrived(k, block, half=None):
            copy(k, block, me, half=half).wait_recv()
            onward = {1: [copy(3, block, (*y_chip, c), half=0), copy(5, block, sibling)],
                      2: [copy(4, block, (*x_chip, c), half=1), copy(6, block, sibling)],
                      3: [copy(7, block, sibling, half=0)],
                      4: [copy(8, block, sibling, half=1)]}.get(k, [])
            for cp in onward:
                cp.start()
            sent.extend(onward)

        arrived(1, (*x_chip, c))
        arrived(2, (*y_chip, c))
        arrived(3, (*across, c), half=0)
        arrived(4, (*across, c), half=1)
        arrived(0, sibling)
        arrived(5, (*x_chip, 1 - c))
        arrived(6, (*y_chip, 1 - c))
        arrived(7, (*across, 1 - c), half=0)
        arrived(8, (*across, 1 - c), half=1)
        for cp in sent:
            cp.wait_send()
        mine.wait()

    n_sems = 9
    return pl.pallas_call(
        body, name=name,
        in_specs=[pl.BlockSpec(memory_space=pl.ANY)],
        out_specs=pl.BlockSpec(memory_space=pl.ANY),
        out_shape=jax.ShapeDtypeStruct((N_DEV,) + shard.shape, shard.dtype),
        scratch_shapes=[pltpu.SemaphoreType.DMA((n_sems,)), pltpu.SemaphoreType.DMA((n_sems,)),
                        pltpu.SemaphoreType.DMA],
    )(shard)


def _call(body, send, *, name, grid, in_specs, out_specs, out_shape, scratch_shapes, semantics, operands):
    if send is None:
        return pl.pallas_call(
            body, name=name, grid=grid, in_specs=in_specs, out_specs=out_specs, out_shape=out_shape,
            scratch_shapes=scratch_shapes, compiler_params=_params(semantics))(*operands), []
    srcs, kinds = [s for s, _ in send], [k for _, k in send]
    n, n_in, n_out, n_scr = len(srcs), len(in_specs), len(out_specs), len(scratch_shapes)
    steps = math.prod(grid)
    relay_step = (13 * steps) // 16

    def plan(refs):
        src_refs, land_refs = refs[n_in:n_in + n], refs[n_in + n + n_out:n_in + 2 * n + n_out]
        send_sems, recv_sems, local_sems = refs[-3:]
        x, y, c = lax.axis_index("x"), lax.axis_index("y"), lax.axis_index("c")
        me, peers = _peers()
        first, relayed_in, relayed_out, arrivals, sends, own = [], [], [], [], [], []
        for a, kind in enumerate(kinds):
            def copy(k, src, dst_slot, to):
                return pltpu.make_async_remote_copy(
                    src_ref=src, dst_ref=land_refs[a].at[dst_slot], send_sem=send_sems.at[a * (N_DEV - 1) + k],
                    recv_sem=recv_sems.at[a * (N_DEV - 1) + k], device_id=to, device_id_type=pl.DeviceIdType.MESH)

            if kind == "gather_by_chip":
                idx = lambda px, py, pc: 4 * px + 2 * py + pc
                chips = [(1 - x, y), (x, 1 - y), (1 - x, 1 - y)]
                mine = [copy(0, src_refs[a], me, (x, y, 1 - c))]
                arrivals.append(copy(0, src_refs[a], idx(x, y, 1 - c), (x, y, 1 - c)))
                for j, (px, py) in enumerate(chips):
                    mine.append(copy(1 + j, src_refs[a], me, (px, py, c)))
                    relayed_in.append(copy(1 + j, src_refs[a], idx(px, py, c), (px, py, c)))
                    relayed_out.append(copy(4 + j, land_refs[a].at[idx(px, py, c)], idx(px, py, c), (x, y, 1 - c)))
                    arrivals.append(copy(4 + j, src_refs[a], idx(px, py, 1 - c), (x, y, 1 - c)))
                first += mine
                sends += mine + relayed_out[-3:]
                own.append(pltpu.make_async_copy(src_refs[a], land_refs[a].at[me], local_sems.at[a]))
            elif kind == "scatter_by_chip":
                for k, (px, py) in enumerate([(1 - x, y), (x, 1 - y), (1 - x, 1 - y)]):
                    first.append(copy(k, src_refs[a].at[2 * px + py], 2 * x + y, (px, py, c)))
                    arrivals.append(copy(k, src_refs[a].at[2 * px + py], 2 * px + py, (px, py, c)))
                sends += first[-3:]
                own.append(pltpu.make_async_copy(src_refs[a].at[2 * x + y], land_refs[a].at[2 * x + y],
                                                 local_sems.at[a]))
            else:
                part = (lambda i: src_refs[a].at[i]) if kind == "scatter" else (lambda i: src_refs[a])
                for k, (peer, peer_idx) in enumerate(peers):
                    first.append(copy(k, part(peer_idx), me, peer))
                    arrivals.append(copy(k, part(peer_idx), peer_idx, peer))
                sends += first[-(N_DEV - 1):]
                own.append(pltpu.make_async_copy(part(me), land_refs[a].at[me], local_sems.at[a]))
        return first, relayed_in, relayed_out, arrivals, sends, own

    def wrapped(*refs):
        step = 0
        for axis, size in enumerate(grid):
            step = step * size + pl.program_id(axis)

        @pl.when(step == 0)
        def _():
            first, _, _, _, _, own = plan(refs)
            for cp in first + own:
                cp.start()

        if "gather_by_chip" in kinds:
            @pl.when(step == relay_step)
            def _():
                _, relayed_in, relayed_out, _, _, _ = plan(refs)
                for cp_in, cp_out in zip(relayed_in, relayed_out):
                    cp_in.wait_recv()
                    cp_out.start()

        body(*refs[:n_in], *refs[n_in + n:n_in + n + n_out], *refs[n_in + 2 * n + n_out:n_in + 2 * n + n_out + n_scr])

        @pl.when(step == steps - 1)
        def _():
            _, _, _, arrivals, sends, own = plan(refs)
            for cp in arrivals:
                cp.wait_recv()
            for cp in sends:
                cp.wait_send()
            for cp in own:
                cp.wait()

    anywhere = pl.BlockSpec(memory_space=pl.ANY)
    lands = [jax.ShapeDtypeStruct((N_DEV // 2 if k == "scatter_by_chip" else N_DEV,) + s.shape[-2:], s.dtype)
             for s, k in send]
    out = pl.pallas_call(
        wrapped, name=name, grid=grid,
        in_specs=list(in_specs) + [anywhere] * n, out_specs=list(out_specs) + [anywhere] * n,
        out_shape=list(out_shape) + lands,
        scratch_shapes=list(scratch_shapes) + [pltpu.SemaphoreType.DMA((n * (N_DEV - 1),)),
                                               pltpu.SemaphoreType.DMA((n * (N_DEV - 1),)),
                                               pltpu.SemaphoreType.DMA((n,))],
        compiler_params=_params(("arbitrary",) * len(grid)),
    )(*operands, *srcs)
    return out[:n_out], list(out[n_out:])


def _pair_swap(blocks):
    def body(src_ref, out_ref, send_sems, recv_sems):
        x, y, c = lax.axis_index("x"), lax.axis_index("y"), lax.axis_index("c")
        copies = [pltpu.make_async_remote_copy(
            src_ref=src_ref.at[2 * chip + (1 - c)], dst_ref=out_ref.at[chip], send_sem=send_sems.at[chip],
            recv_sem=recv_sems.at[chip], device_id=(x, y, 1 - c), device_id_type=pl.DeviceIdType.MESH)
            for chip in range(N_DEV // 2)]
        for cp in copies:
            cp.start()
        for cp in copies:
            cp.wait()

    return pl.pallas_call(
        body, name="pair_swap_grad_w_in",
        in_specs=[pl.BlockSpec(memory_space=pl.ANY)], out_specs=pl.BlockSpec(memory_space=pl.ANY),
        out_shape=jax.ShapeDtypeStruct((N_DEV // 2,) + blocks.shape[1:], blocks.dtype),
        scratch_shapes=[pltpu.SemaphoreType.DMA((N_DEV // 2,)), pltpu.SemaphoreType.DMA((N_DEV // 2,))],
    )(blocks)


def _pair_sum(blocks, swapped, core):
    _, rows, cols = swapped.shape
    tile_rows = _row_tile(rows)

    def body(core_ref, mine_ref, theirs_ref, o_ref):
        o_ref[...] = (mine_ref[...].astype(F32) + theirs_ref[...].astype(F32)).astype(o_ref.dtype)

    return pl.pallas_call(
        body, name="pair_sum_grad_w_in",
        grid_spec=pltpu.PrefetchScalarGridSpec(
            num_scalar_prefetch=1, grid=(N_DEV // 2, rows // tile_rows),
            in_specs=[pl.BlockSpec((None, tile_rows, cols), lambda j, i, core_ref: (2 * j + core_ref[0], i, 0)),
                      pl.BlockSpec((None, tile_rows, cols), lambda j, i, core_ref: (j, i, 0))],
            out_specs=pl.BlockSpec((None, tile_rows, cols), lambda j, i, core_ref: (j, i, 0))),
        out_shape=jax.ShapeDtypeStruct(swapped.shape, swapped.dtype),
        compiler_params=_params(("parallel", "parallel")),
    )(core, blocks, swapped)


def _sum_in_device_order(land_ref):
    acc = land_ref[0].astype(F32)
    for j in range(1, land_ref.shape[0]):
        acc = acc + land_ref[j].astype(F32)
    return acc


def _adam_math(w, g, m, v):
    c1 = 1.0 - ADAM_B1 ** ADAM_STEP
    c2 = 1.0 - ADAM_B2 ** ADAM_STEP
    m_new = ADAM_B1 * m + (1.0 - ADAM_B1) * g
    v_new = ADAM_B2 * v + (1.0 - ADAM_B2) * (g * g)
    delta = -ADAM_LR * ((m_new / c1) / (jnp.sqrt(v_new / c2) + ADAM_EPS) + ADAM_WD * w)
    return delta, m_new, v_new


def _row_tile(rows):
    return max(t for t in range(16, 385, 16) if rows % t == 0) if rows % 16 == 0 else rows


def _sum_update(land, w, m, v, name):
    slots, rows, cols = land.shape
    tile_rows = _row_tile(rows)

    def body(land_ref, w_ref, m_ref, v_ref, g_ref, d_ref, nm_ref, nv_ref):
        g = _sum_in_device_order(land_ref)
        g_ref[...] = g
        d_ref[...], nm_ref[...], nv_ref[...] = _adam_math(w_ref[...], g, m_ref[...], v_ref[...])

    tile = pl.BlockSpec((None, tile_rows, cols), lambda i: (0, i, 0))
    out = jax.ShapeDtypeStruct((1, rows, cols), F32)
    return pl.pallas_call(
        body, name=name, grid=(rows // tile_rows,),
        in_specs=[pl.BlockSpec((slots, tile_rows, cols), lambda i: (0, i, 0)), tile, tile, tile],
        out_specs=[tile] * 4, out_shape=[out] * 4,
        compiler_params=_params(("parallel",)),
    )(land, w, m, v)


def _to_bf16(arrays, flip):
    def body(*refs):
        for src_ref, dst_ref, t in zip(refs[:len(arrays)], refs[len(arrays):], flip):
            v = src_ref[...]
            dst_ref[...] = (v.T if t else v).astype(BF16)

    shapes = [a.shape[::-1] if t else a.shape for a, t in zip(arrays, flip)]
    whole = lambda shape: pl.BlockSpec(shape, lambda i: (0, 0))
    return pl.pallas_call(
        body, name="shards_to_bf16", grid=(1,),
        in_specs=[whole(a.shape) for a in arrays], out_specs=[whole(s) for s in shapes],
        out_shape=[jax.ShapeDtypeStruct(s, BF16) for s in shapes],
        compiler_params=_params(("arbitrary",)),
    )(*arrays)


N_GAINS = 3


def _sum_update_gains(land, ws, ms, vs):
    def body(land_ref, *refs):
        w_refs, m_refs, v_refs = (refs[k * N_GAINS:(k + 1) * N_GAINS] for k in range(3))
        loss_ref, out_refs = refs[3 * N_GAINS], refs[3 * N_GAINS + 1:]
        rows = _sum_in_device_order(land_ref)
        loss_ref[...] = rows[N_GAINS:N_GAINS + 1]
        for k in range(N_GAINS):
            g = rows[k:k + 1]
            g_ref, d_ref, nm_ref, nv_ref = out_refs[4 * k:4 * k + 4]
            g_ref[...] = g
            d_ref[...], nm_ref[...], nv_ref[...] = _adam_math(w_refs[k][...], g, m_refs[k][...], v_refs[k][...])

    row = pl.BlockSpec((1, D_MODEL), lambda i: (0, 0))
    out = jax.ShapeDtypeStruct((1, D_MODEL), F32)
    n_out = 1 + 4 * N_GAINS
    return pl.pallas_call(
        body, name="update_gains", grid=(1,),
        in_specs=[pl.BlockSpec(land.shape, lambda i: (0, 0, 0))] + [row] * (3 * N_GAINS),
        out_specs=[row] * n_out, out_shape=[out] * n_out,
    )(land, *ws, *ms, *vs)


GROUP_FFN = ("w_ffn_in", "w_ffn_out")
GROUP_MIX = ("w_sb_up", "w_dil_up", "w_out")
COL_SHARDED = ("w_in", "w_sb_up", "w_dil_up", "w_ffn_in")
TRANSPOSED = ("w_in", "w_ffn_in")
MIX_UP = ("w_sb_up", "w_dil_up")


def _full_from_shards(name, slots):
    _, r, c = slots.shape
    if name in TRANSPOSED:
        return slots.reshape(N_DEV * r, c).T
    if name in COL_SHARDED:
        return slots.transpose(1, 0, 2).reshape(r, N_DEV * c)
    return slots.reshape(N_DEV * r, c)


def _row_shards(full):
    rows, cols = full.shape
    return full.reshape(N_DEV, rows // N_DEV, cols)


def _local_step(x, target, g_mix, g_ffn, g_fin, w_in_t, shards=None, rest=None):
    gather = lambda names, kind: None if shards is None else [(shards[n], kind) for n in names]
    scatter = lambda blocks: None if shards is None else [(t, "scatter") for t in blocks]
    landed = lambda blocks, lands: lands if lands else blocks

    w = {"w_in": w_in_t}
    if shards is None:
        w.update(rest)
        w["w_ffn_in"] = rest["w_ffn_in"].T
        w.update({n: rest[n].T for n in MIX_UP})
    (qkv_sb, qkv_dl, gates, u), _ = _norm_proj(x, g_mix, w["w_in"])
    qkv_sb = qkv_sb.reshape(B_LOC, SEQ, 3 * SB_WIDTH)
    qkv_dl = qkv_dl.reshape(B_LOC, SEQ, 3 * DIL_WIDTH)
    (o_sb,), lands = _sb_fwd(qkv_sb, gather(GROUP_FFN, "gather_by_chip"))
    if lands:
        w["w_ffn_in"], w["w_ffn_out"] = lands[0].reshape(2 * D_FF, D_MODEL), _full_from_shards("w_ffn_out", lands[1])
    o_sb = o_sb.reshape(TOK, SB_WIDTH)
    (o_dl, lse), lands = _dil_fwd(qkv_dl, gather(GROUP_MIX, "gather"))
    w.update({n: t.reshape(D_MODEL, -1) if n in MIX_UP else _full_from_shards(n, t)
              for n, t in zip(GROUP_MIX, lands)})
    o_dl = o_dl.reshape(TOK, DIL_OUT)

    loss, dx1, merged, u2, act, dh, dx2, dg_fin, dg_ffn = _mix_ffn_fwd_bwd(
        x, o_sb, o_dl, gates, w["w_sb_up"], w["w_dil_up"], w["w_out"], target, g_ffn, g_fin,
        w["w_ffn_in"], w["w_ffn_out"])
    dgates, dy_sb, dy_dl, do_sb, do_dl, dsum = _mix_bwd(dx1, o_sb, o_dl, gates, w["w_sb_up"], w["w_dil_up"], w["w_out"])
    blocks = {
        "w_sb_up": _atb(o_sb, dy_sb, "grad_w_sb_up", SB_WIDTH, D_MODEL, col_blocks=N_DEV),
        "w_dil_up": _atb(o_dl, dy_dl, "grad_w_dil_up", DIL_OUT, D_MODEL, col_blocks=N_DEV),
        "w_out": _row_shards(_atb(merged, dx1, "grad_w_out", D_MODEL, D_MODEL)),
        "w_ffn_in": _row_shards(_atb_cols(dh, u2, "grad_w_ffn_in", 512)),
        "w_ffn_out": _row_shards(_atb_cols(act, dx2, "grad_w_ffn_out", 256)),
    }
    grads = {}

    early, late = GROUP_FFN, GROUP_MIX
    early_blocks = [blocks[n] for n in early]
    (dq_sb, dk_sb, dv_sb), lands = _sb_bwd(qkv_sb, do_sb.reshape(B_LOC, SEQ, SB_WIDTH), scatter(early_blocks))
    grads.update(zip(early, landed(early_blocks, lands)))
    as_batch = lambda t: t.reshape(B_LOC, SEQ, DIL_OUT)
    late_blocks = [blocks[n] for n in late]
    d_dl, lands = _dil_bwd(qkv_dl, as_batch(do_dl), lse, as_batch(dsum), scatter(late_blocks))
    grads.update(zip(late, landed(late_blocks, lands)))
    flat = lambda t: t.reshape(TOK, -1)
    dproj = ([flat(dq_sb), flat(dk_sb), flat(dv_sb)]
             + [flat(d_dl[3 * grp + part]) for part in range(3) for grp in range(DIL_GROUPS)] + [dgates])

    w_in_blocks = _row_shards(_atb_pieces(u, dproj, "grad_w_in", D_MODEL // 2))
    if shards is None:
        grads["w_in"] = w_in_blocks
        send = None
    else:
        core = lax.axis_index("c").astype(jnp.int32).reshape(1)
        send = [(_pair_sum(w_in_blocks, _pair_swap(w_in_blocks), core), "scatter_by_chip")]
    (grad_x, dg_mix), lands = _proj_bwd(dproj, dx1, x, g_mix, w["w_in"], send)
    if lands:
        grads["w_in"] = lands[0]
    gain_grads = jnp.concatenate([dg_mix, dg_ffn, dg_fin], axis=0)
    return loss, grad_x, gain_grads, grads


def kernel(x, norm_mix_g, w_in, w_sb_up, w_dil_up, w_out, norm_ffn_g, w_ffn_in, w_ffn_out, norm_final_g, loss_target, m_norm_mix_g, m_w_in, m_w_sb_up, m_w_dil_up, m_w_out, m_norm_ffn_g, m_w_ffn_in, m_w_ffn_out, m_norm_final_g, v_norm_mix_g, v_w_in, v_w_sb_up, v_w_dil_up, v_w_out, v_norm_ffn_g, v_w_ffn_in, v_w_ffn_out, v_norm_final_g):
    mats = {"w_in": w_in, "w_sb_up": w_sb_up, "w_dil_up": w_dil_up, "w_out": w_out,
            "w_ffn_in": w_ffn_in, "w_ffn_out": w_ffn_out}
    moments_m = {"w_in": m_w_in, "w_sb_up": m_w_sb_up, "w_dil_up": m_w_dil_up, "w_out": m_w_out,
                 "w_ffn_in": m_w_ffn_in, "w_ffn_out": m_w_ffn_out}
    moments_v = {"w_in": v_w_in, "w_sb_up": v_w_sb_up, "w_dil_up": v_w_dil_up, "w_out": v_w_out,
                 "w_ffn_in": v_w_ffn_in, "w_ffn_out": v_w_ffn_out}
    stored = lambda t, name: t.transpose(0, 2, 1) if name in TRANSPOSED else t
    mats = {name: stored(t, name) for name, t in mats.items()}
    shards = dict(zip(mats, _to_bf16([t[0] for t in mats.values()], [name in MIX_UP for name in mats])))
    gathered_w_in = _all_gather(shards.pop("w_in"), "all_gather_w_in")
    g_fin = norm_final_g.reshape(1, D_MODEL)
    loss, grad_x, gain_grads, grad_slots = _local_step(
        x.reshape(TOK, D_MODEL), loss_target.reshape(TOK, D_MODEL), norm_mix_g, norm_ffn_g, g_fin,
        gathered_w_in.reshape(IN_WIDTH, D_MODEL), shards=shards)

    gain_rows = jnp.concatenate([gain_grads, jnp.tile(loss, (1, D_MODEL // LANES)),
                                 jnp.zeros((8 - 4, D_MODEL), F32)], axis=0)
    row = lambda t: t.reshape(1, D_MODEL)
    loss_row, *gain_outs = _sum_update_gains(
        _all_gather(gain_rows, "all_gather_gains"),
        [norm_mix_g, norm_ffn_g, g_fin], [m_norm_mix_g, m_norm_ffn_g, row(m_norm_final_g)],
        [v_norm_mix_g, v_norm_ffn_g, row(v_norm_final_g)])

    out_g, out_d, out_m, out_v = {}, {}, {}, {}
    for name, slots in grad_slots.items():
        out_g[name], out_d[name], out_m[name], out_v[name] = [stored(t, name) for t in _sum_update(
            slots, mats[name], stored(moments_m[name], name), stored(moments_v[name], name), "update_" + name)]
    for idx, name in enumerate(("norm_mix_g", "norm_ffn_g", "norm_final_g")):
        shape = (D_MODEL,) if name == "norm_final_g" else (1, D_MODEL)
        out_g[name], out_d[name], out_m[name], out_v[name] = [t.reshape(shape) for t in gain_outs[4 * idx:4 * idx + 4]]

    order = ("norm_mix_g", "w_in", "w_sb_up", "w_dil_up", "w_out", "norm_ffn_g", "w_ffn_in", "w_ffn_out",
             "norm_final_g")
    return (loss_row[0, 0], grad_x.reshape(B_LOC, SEQ, D_MODEL),
            *[out_g[n] for n in order], *[out_d[n] for n in order],
            *[out_m[n] for n in order], *[out_v[n] for n in order])
```

```python
import math

import jax
import jax.numpy as jnp
from jax import lax
from jax.experimental import pallas as pl
from jax.experimental.pallas import tpu as pltpu

F32 = jnp.float32
BF16 = jnp.bfloat16

N_DEV = 8
D_MODEL = 1024
SEQ = 2048
B_LOC = 2
TOK = B_LOC * SEQ
HEAD_DIM = 64
SB_WIDTH = 512
DIL_WIDTH = 768
DIL_OUT = 256
QKV_WIDTH = 3 * SB_WIDTH + 3 * DIL_WIDTH
IN_WIDTH = QKV_WIDTH + 2 * D_MODEL
D_FF = 2816
DIL_PAIRS = ((128, 1), (512, 4), (2048, 16))
DIL_HEADS = 12
RMS_EPS = 1e-6
ALIBI_MAX_BIAS = 8.0
QK_SCALE = 1.0 / math.sqrt(HEAD_DIM)
BLK = 128
LANES = 128
NEG_BIG = -1e30

ADAM_LR = 0.001
ADAM_B1 = 0.9
ADAM_B2 = 0.999
ADAM_EPS = 1e-08
ADAM_WD = 0.01
ADAM_STEP = 10

VMEM_LIMIT = 58 * 1024 * 1024


def _dot(a, b):
    return jnp.dot(a, b, preferred_element_type=F32)


def _dot_nt(a, b):
    return lax.dot_general(a, b, (((1,), (1,)), ((), ())), preferred_element_type=F32)


def _dot_tn(a, b):
    return lax.dot_general(a, b, (((0,), (0,)), ((), ())), preferred_element_type=F32)


def _sigmoid(z):
    return 1.0 / (1.0 + jnp.exp(-z))


def _split_bf16(v):
    hi = v.astype(BF16)
    lo = (v - hi.astype(F32)).astype(BF16)
    return hi, lo


def _chunks(width, step=512):
    out, c = [], 0
    while c < width:
        w = min(step, width - c)
        out.append((c, w))
        c += w
    return out


def _resident(shape):
    nd = len(shape)
    return pl.BlockSpec(shape, lambda *_: (0,) * nd, pipeline_mode=pl.Buffered(1))


def _params(sem):
    return pltpu.CompilerParams(dimension_semantics=sem, vmem_limit_bytes=VMEM_LIMIT)


def _rms_fwd(x, g):
    r = lax.rsqrt(jnp.mean(x * x, axis=-1, keepdims=True) + RMS_EPS)
    n = x * r
    return n, r, n * g


def _rms_bwd(dy, n, r, g):
    dg = jnp.sum(dy * n, axis=0, keepdims=True)
    dn = dy * g
    dx = r * (dn - n * jnp.mean(dn * n, axis=-1, keepdims=True))
    return dx, dg


TM = 256
TM_WIDE = 512


def _norm_proj(x, g, w_in_t, send=None):
    def body(x_ref, g_ref, w_ref, sb_ref, dl_ref, gate_ref, u_ref):
        _, _, u = _rms_fwd(x_ref[...], g_ref[...])
        u = u.astype(BF16)
        u_ref[...] = u
        for c0, w in _chunks(3 * SB_WIDTH):
            sb_ref[:, c0:c0 + w] = _dot_nt(u, w_ref[c0:c0 + w, :]).astype(BF16)
        for c0, w in _chunks(3 * DIL_WIDTH):
            dl_ref[:, c0:c0 + w] = _dot_nt(u, w_ref[3 * SB_WIDTH + c0:3 * SB_WIDTH + c0 + w, :])
        for c0, w in _chunks(2 * D_MODEL):
            gate_ref[:, c0:c0 + w] = _dot_nt(u, w_ref[QKV_WIDTH + c0:QKV_WIDTH + c0 + w, :])

    return _call(
        body, send, name="norm_proj", grid=(TOK // TM_WIDE,),
        in_specs=[pl.BlockSpec((TM_WIDE, D_MODEL), lambda i: (i, 0)), _resident((1, D_MODEL)),
                  _resident((IN_WIDTH, D_MODEL))],
        out_specs=[pl.BlockSpec((TM_WIDE, 3 * SB_WIDTH), lambda i: (i, 0)),
                   pl.BlockSpec((TM_WIDE, 3 * DIL_WIDTH), lambda i: (i, 0)),
                   pl.BlockSpec((TM_WIDE, 2 * D_MODEL), lambda i: (i, 0)),
                   pl.BlockSpec((TM_WIDE, D_MODEL), lambda i: (i, 0))],
        out_shape=[jax.ShapeDtypeStruct((TOK, 3 * SB_WIDTH), BF16),
                   jax.ShapeDtypeStruct((TOK, 3 * DIL_WIDTH), F32),
                   jax.ShapeDtypeStruct((TOK, 2 * D_MODEL), F32),
                   jax.ShapeDtypeStruct((TOK, D_MODEL), BF16)],
        scratch_shapes=[], semantics=("parallel",), operands=(x, g, w_in_t))


FF_CHUNK = 1024


def _mix_ffn_fwd_bwd(x, o_sb, o_dl, gates, w_sb_up, w_dil_up, w_out, target, g_ffn, g_fin, w_ffn_in_t, w_ffn_out):
    def body(x_ref, osb_ref, odl_ref, gate_ref, wsb_ref, wdl_ref, wo_ref, t_ref, gffn_ref, gfin_ref, win_ref, wout_ref,
             loss_ref, dx1_ref, mg_ref, u2_ref, act_ref, dh_ref, dx2_ref, dgfin_ref, dgffn_ref, h_scr):
        i = pl.program_id(0)

        @pl.when(i == 0)
        def _():
            loss_ref[...] = jnp.zeros_like(loss_ref)
            dgfin_ref[...] = jnp.zeros_like(dgfin_ref)
            dgffn_ref[...] = jnp.zeros_like(dgffn_ref)

        y_sb = _dot_nt(osb_ref[...], wsb_ref[...])
        y_dl = _dot_nt(odl_ref[...].astype(BF16), wdl_ref[...])
        merged =(_sigmoid(gate_ref[:, :D_MODEL]) * y_sb
                  + _sigmoid(gate_ref[:, D_MODEL:]) * y_dl).astype(BF16)
        mg_ref[...] = merged
        x1 = x_ref[...] + _dot(merged, wo_ref[...])
        g_ffn_v = gffn_ref[...]
        g_fin_v = gfin_ref[...]
        n2, r2, u2 = _rms_fwd(x1, g_ffn_v)
        u2 = u2.astype(BF16)
        u2_ref[...] = u2
        x2 = x1
        for c0, w in _chunks(D_FF, FF_CHUNK):
            gate = _dot_nt(u2, win_ref[c0:c0 + w, :])
            up = _dot_nt(u2, win_ref[D_FF + c0:D_FF + c0 + w, :])
            h_scr[:, c0:c0 + w] = gate
            h_scr[:, D_FF + c0:D_FF + c0 + w] = up
            act = (gate * _sigmoid(gate) * up).astype(BF16)
            act_ref[:, c0:c0 + w] = act
            x2 = x2 + _dot(act, wout_ref[c0:c0 + w, :])
        n3, r3, y = _rms_fwd(x2, g_fin_v)
        err = y - t_ref[...]
        sq = jnp.sum(jnp.sum(err * err, axis=1, keepdims=True), axis=0, keepdims=True)
        loss_ref[...] += sq * (0.5 / D_MODEL)
        dx2, dgfin = _rms_bwd(err * (1.0 / D_MODEL), n3, r3, g_fin_v)
        dgfin_ref[...] += dgfin
        dx2_b = dx2.astype(BF16)
        dx2_ref[...] = dx2_b
        du2 = jnp.zeros((TM, D_MODEL), F32)
        for c0, w in _chunks(D_FF, FF_CHUNK):
            gate = h_scr[:, c0:c0 + w]
            up = h_scr[:, D_FF + c0:D_FF + c0 + w]
            dact = _dot_nt(dx2_b, wout_ref[c0:c0 + w, :])
            sg = _sigmoid(gate)
            dgate = (dact * up * (sg * (1.0 + gate * (1.0 - sg)))).astype(BF16)
            dup = (dact * (gate * sg)).astype(BF16)
            dh_ref[:, c0:c0 + w] = dgate
            dh_ref[:, D_FF + c0:D_FF + c0 + w] = dup
            du2 = du2 + _dot(dgate, win_ref[c0:c0 + w, :])
            du2 = du2 + _dot(dup, win_ref[D_FF + c0:D_FF + c0 + w, :])
        dx1_n, dgffn = _rms_bwd(du2, n2, r2, g_ffn_v)
        dgffn_ref[...] += dgffn
        dx1_ref[...] = dx2 + dx1_n

    tile = lambda w: pl.BlockSpec((TM, w), lambda i: (i, 0))
    acc = lambda w: pl.BlockSpec((1, w), lambda i: (0, 0))
    return pl.pallas_call(
        body, name="mix_ffn_fwd_bwd", grid=(TOK // TM,),
        in_specs=[tile(D_MODEL), tile(SB_WIDTH), tile(DIL_OUT), tile(2 * D_MODEL),
                  _resident((D_MODEL, SB_WIDTH)), _resident((D_MODEL, DIL_OUT)), _resident((D_MODEL, D_MODEL)),
                  tile(D_MODEL), _resident((1, D_MODEL)), _resident((1, D_MODEL)),
                  _resident((2 * D_FF, D_MODEL)), _resident((D_FF, D_MODEL))],
        out_specs=[acc(LANES), tile(D_MODEL), tile(D_MODEL), tile(D_MODEL), tile(D_FF), tile(2 * D_FF),
                   tile(D_MODEL), acc(D_MODEL), acc(D_MODEL)],
        out_shape=[jax.ShapeDtypeStruct((1, LANES), F32),
                   jax.ShapeDtypeStruct((TOK, D_MODEL), F32),
                   jax.ShapeDtypeStruct((TOK, D_MODEL), BF16),
                   jax.ShapeDtypeStruct((TOK, D_MODEL), BF16),
                   jax.ShapeDtypeStruct((TOK, D_FF), BF16),
                   jax.ShapeDtypeStruct((TOK, 2 * D_FF), BF16),
                   jax.ShapeDtypeStruct((TOK, D_MODEL), BF16),
                   jax.ShapeDtypeStruct((1, D_MODEL), F32),
                   jax.ShapeDtypeStruct((1, D_MODEL), F32)],
        scratch_shapes=[pltpu.VMEM((TM, 2 * D_FF), F32)],
        compiler_params=_params(("arbitrary",)),
    )(x, o_sb, o_dl, gates, w_sb_up, w_dil_up, w_out, target, g_ffn, g_fin, w_ffn_in_t, w_ffn_out)


def _mix_bwd(dx1, o_sb, o_dl, gates, w_sb_up, w_dil_up, w_out):
    def body(dx1_ref, osb_ref, odl_ref, gate_ref, wsb_ref, wdl_ref, wout_ref,
             dgate_ref, dysb_ref, dydl_ref, dosb_ref, dodl_ref, dsum_ref):
        dmerged = _dot_nt(dx1_ref[...].astype(BF16), wout_ref[...])
        o_dl = odl_ref[...]
        y_sb = _dot_nt(osb_ref[...], wsb_ref[...])
        y_dl = _dot_nt(o_dl.astype(BF16), wdl_ref[...])
        s_sb = _sigmoid(gate_ref[:, :D_MODEL])
        s_dl = _sigmoid(gate_ref[:, D_MODEL:])
        dgate_ref[:, :D_MODEL] = (dmerged * y_sb * (s_sb * (1.0 - s_sb))).astype(BF16)
        dgate_ref[:, D_MODEL:] = (dmerged * y_dl * (s_dl * (1.0 - s_dl))).astype(BF16)
        dy_sb = (dmerged * s_sb).astype(BF16)
        dy_dl = (dmerged * s_dl).astype(BF16)
        dysb_ref[...] = dy_sb
        dydl_ref[...] = dy_dl
        dosb_ref[...] = _dot(dy_sb, wsb_ref[...]).astype(BF16)
        do_dl = _dot(dy_dl, wdl_ref[...])
        dodl_ref[...] = do_dl
        row = lax.broadcasted_iota(jnp.int32, (DIL_OUT, DIL_OUT), 0) // HEAD_DIM
        col = lax.broadcasted_iota(jnp.int32, (DIL_OUT, DIL_OUT), 1) // HEAD_DIM
        same_head = (row == col).astype(BF16)
        hi, lo = _split_bf16(do_dl * o_dl)
        dsum_ref[...] = _dot(hi, same_head) + _dot(lo, same_head)

    tile = lambda w: pl.BlockSpec((TM_WIDE, w), lambda i: (i, 0))
    return pl.pallas_call(
        body, name="mix_bwd", grid=(TOK // TM_WIDE,),
        in_specs=[tile(D_MODEL), tile(SB_WIDTH), tile(DIL_OUT), tile(2 * D_MODEL),
                  _resident((D_MODEL, SB_WIDTH)), _resident((D_MODEL, DIL_OUT)),
                  _resident((D_MODEL, D_MODEL))],
        out_specs=[tile(2 * D_MODEL), tile(D_MODEL), tile(D_MODEL), tile(SB_WIDTH), tile(DIL_OUT),
                   tile(DIL_OUT)],
        out_shape=[jax.ShapeDtypeStruct((TOK, 2 * D_MODEL), BF16),
                   jax.ShapeDtypeStruct((TOK, D_MODEL), BF16),
                   jax.ShapeDtypeStruct((TOK, D_MODEL), BF16),
                   jax.ShapeDtypeStruct((TOK, SB_WIDTH), BF16),
                   jax.ShapeDtypeStruct((TOK, DIL_OUT), F32),
                   jax.ShapeDtypeStruct((TOK, DIL_OUT), F32)],
        compiler_params=_params(("parallel",)),
    )(dx1, o_sb, o_dl, gates, w_sb_up, w_dil_up, w_out)


def _proj_bwd(dproj, dx1, x, g, w_in_t, send=None):
    widths = [p.shape[1] for p in dproj]

    def body(*refs):
        dx1_ref, x_ref, g_ref, w_ref, dx_ref, dg_ref = refs[len(widths):]

        @pl.when(pl.program_id(0) == 0)
        def _():
            dg_ref[...] = jnp.zeros_like(dg_ref)

        du = jnp.zeros((TM, D_MODEL), F32)
        c0 = 0
        for dp_ref, w in zip(refs, widths):
            du = du + _dot(dp_ref[...].astype(BF16), w_ref[c0:c0 + w, :])
            c0 += w
        g_v = g_ref[...]
        n, r, _ = _rms_fwd(x_ref[...], g_v)
        dx, dg = _rms_bwd(du, n, r, g_v)
        dg_ref[...] += dg
        dx_ref[...] = dx1_ref[...] + dx

    tile = lambda w: pl.BlockSpec((TM, w), lambda i: (i, 0))
    return _call(
        body, send, name="proj_bwd", grid=(TOK // TM,),
        in_specs=[tile(w) for w in widths] + [tile(D_MODEL), tile(D_MODEL), _resident((1, D_MODEL)),
                                              _resident((IN_WIDTH, D_MODEL))],
        out_specs=[tile(D_MODEL), pl.BlockSpec((1, D_MODEL), lambda i: (0, 0))],
        out_shape=[jax.ShapeDtypeStruct((TOK, D_MODEL), F32),
                   jax.ShapeDtypeStruct((1, D_MODEL), F32)],
        scratch_shapes=[], semantics=("arbitrary",), operands=(*dproj, dx1, x, g, w_in_t))


def _atb_pieces(a, pieces, name, tm, tk=512):
    m = a.shape[1]
    widths = [p.shape[1] for p in pieces]
    n = sum(widths)
    nk = TOK // tk

    def body(a_ref, *refs):
        o_ref, acc_ref = refs[len(widths):]
        k = pl.program_id(1)

        @pl.when(k == 0)
        def _():
            acc_ref[...] = jnp.zeros_like(acc_ref)

        a_v = a_ref[...]
        c0 = 0
        for p_ref, w in zip(refs, widths):
            acc_ref[:, c0:c0 + w] += _dot_tn(a_v, p_ref[...].astype(BF16))
            c0 += w

        @pl.when(k == nk - 1)
        def _():
            for c0, w in _chunks(n):
                o_ref[c0:c0 + w, :] = acc_ref[:, c0:c0 + w].T.astype(BF16)

    return pl.pallas_call(
        body, name=name, grid=(m // tm, nk),
        in_specs=[pl.BlockSpec((tk, tm), lambda i, k: (k, i))]
                 + [pl.BlockSpec((tk, w), lambda i, k: (k, 0)) for w in widths],
        out_specs=pl.BlockSpec((n, tm), lambda i, k: (0, i)),
        out_shape=jax.ShapeDtypeStruct((n, m), BF16),
        scratch_shapes=[pltpu.VMEM((tm, n), F32)],
        compiler_params=_params(("parallel", "arbitrary")),
    )(a, *pieces)


def _atb_cols(a, b, name, cols, tk=512):
    m, other = a.shape[1], b.shape[1]

    def body(a_ref, b_ref, o_ref):
        acc = jnp.zeros(o_ref.shape, F32)
        for k0 in range(0, TOK, tk):
            acc = acc + _dot_tn(a_ref[k0:k0 + tk, :], b_ref[k0:k0 + tk, :])
        o_ref[...] = acc.astype(BF16)

    assert m % cols == 0
    return pl.pallas_call(
        body, name=name, grid=(m // cols,),
        in_specs=[pl.BlockSpec((TOK, cols), lambda r: (0, r)), _resident((TOK, other))],
        out_specs=pl.BlockSpec((cols, other), lambda r: (r, 0)),
        out_shape=jax.ShapeDtypeStruct((m, other), BF16),
        compiler_params=_params(("parallel",)),
    )(a, b)


def _atb(a, b, name, tm, tn, col_blocks=0, tk=2048):
    m, n = a.shape[1], b.shape[1]
    nk = TOK // tk

    def body(a_ref, b_ref, o_ref, acc_ref):
        k = pl.program_id(2)

        @pl.when(k == 0)
        def _():
            acc_ref[...] = jnp.zeros_like(acc_ref)

        acc_ref[...] += _dot_tn(a_ref[...].astype(BF16), b_ref[...].astype(BF16))

        @pl.when(k == nk - 1)
        def _():
            if col_blocks:
                width = n // col_blocks
                for blk in range(col_blocks):
                    o_ref[blk] = acc_ref[:, blk * width:(blk + 1) * width].astype(BF16)
            else:
                o_ref[...] = acc_ref[...].astype(BF16)

    if col_blocks:
        out_spec = pl.BlockSpec((col_blocks, tm, n // col_blocks), lambda i, j, k: (0, i, 0))
        out_shape = jax.ShapeDtypeStruct((col_blocks, m, n // col_blocks), BF16)
    else:
        out_spec = pl.BlockSpec((tm, tn), lambda i, j, k: (i, j))
        out_shape = jax.ShapeDtypeStruct((m, n), BF16)
    return pl.pallas_call(
        body, name=name, grid=(m // tm, n // tn, nk),
        in_specs=[pl.BlockSpec((tk, tm), lambda i, j, k: (k, i)),
                  pl.BlockSpec((tk, tn), lambda i, j, k: (k, j))],
        out_specs=out_spec, out_shape=out_shape,
        scratch_shapes=[pltpu.VMEM((tm, tn), F32)],
        compiler_params=_params(("parallel", "parallel", "arbitrary")),
    )(a, b)


SB_PAIRS = SB_WIDTH // LANES


def _two_heads(v, lane0):
    zero = jnp.zeros_like(v)
    return jnp.where(lane0, v, zero), jnp.where(lane0, zero, v)


SB_QBLK = 256
N_SB_STEPS = SEQ // SB_QBLK


SB_KCHUNK = 2 * BLK
SB_ROWS = 2 * SB_QBLK
SB_DEAD = -104.0


def _log_keep(z):
    neg_z = -z
    return jnp.minimum(neg_z, 0.0) - jnp.log(1.0 + jnp.exp(jnp.minimum(z, neg_z)))


def _stack_heads(v, lane0):
    return jnp.concatenate(_two_heads(v, lane0), axis=0)


def _block_sums(v, tri, split=True):
    halves = (v[:, :BLK], v[:, BLK:])
    stacked = jnp.concatenate(halves, axis=0)
    if split:
        hi, lo = _split_bf16(stacked)
        prod = _dot(jnp.concatenate([hi, lo], axis=0), tri)
        tri_sum = prod[:2 * SB_ROWS] + prod[2 * SB_ROWS:]
    else:
        tri_sum = _dot(stacked.astype(BF16), tri)
    sums = tuple(jnp.sum(h, axis=1, keepdims=True) for h in halves)
    return (tri_sum[:SB_ROWS], tri_sum[SB_ROWS:]), sums


def _sb_diag_mask():
    row = lax.broadcasted_iota(jnp.int32, (SB_ROWS, SB_KCHUNK), 0)
    col = lax.broadcasted_iota(jnp.int32, (SB_ROWS, SB_KCHUNK), 1)
    return col < jnp.where(row >= SB_QBLK, row - SB_QBLK, row)


def _sb_fwd(qkv, send=None):
    def body(q_ref, k_ref, v_ref, o_ref):
        i = pl.program_id(2)
        krow = lax.broadcasted_iota(jnp.int32, (BLK, BLK), 0)
        kcol = lax.broadcasted_iota(jnp.int32, (BLK, BLK), 1)
        later = (krow > kcol).astype(BF16)
        lane0 = lax.broadcasted_iota(jnp.int32, (SB_QBLK, LANES), 1) < HEAD_DIM
        q2 = _stack_heads(q_ref[0] * QK_SCALE, lane0)

        def chunk(c, carry, causal):
            acc, run = carry
            off = pl.multiple_of(c * SB_KCHUNK, SB_KCHUNK)
            z = _dot_nt(q2, k_ref[0, pl.ds(off, SB_KCHUNK), :])
            log_keep = _log_keep(z)
            if causal is not None:
                log_keep = jnp.where(causal, log_keep, 0.0)
            suffix, sums = _block_sums(log_keep, later)
            log_after = jnp.concatenate([suffix[0] + (run + sums[1]), suffix[1] + run], axis=1)
            a = jnp.exp(log_keep + z + log_after)
            if causal is not None:
                a = jnp.where(causal, a, 0.0)
            acc = acc + _dot(a.astype(BF16), v_ref[0, pl.ds(off, SB_KCHUNK), :])
            return acc, run + (sums[0] + sums[1])

        acc, run = chunk(i, (jnp.zeros((SB_ROWS, LANES), F32), jnp.zeros((SB_ROWS, 1), F32)), _sb_diag_mask())

        def some_alive(run):
            return (jnp.max(run) > SB_DEAD).astype(jnp.int32)

        def trip(state):
            t, _, acc, run = state
            acc, run = chunk(i - 1 - t, (acc, run), None)
            return t + 1, some_alive(run), acc, run

        _, _, acc, _ = lax.while_loop(lambda s: jnp.logical_and(s[0] < i, s[1] > 0), trip,
                                      (jnp.int32(0), some_alive(run), acc, run))
        o_ref[0] = jnp.where(lane0, acc[:SB_QBLK], acc[SB_QBLK:]).astype(BF16)

    blk = pl.BlockSpec((1, SB_QBLK, LANES), lambda b, h, i: (b, i, h))
    return _call(
        body, send, name="sb_fwd", grid=(B_LOC, SB_PAIRS, N_SB_STEPS),
        in_specs=[blk,
                  pl.BlockSpec((1, SEQ, LANES), lambda b, h, i: (b, 0, SB_PAIRS + h)),
                  pl.BlockSpec((1, SEQ, LANES), lambda b, h, i: (b, 0, 2 * SB_PAIRS + h))],
        out_specs=[blk], out_shape=[jax.ShapeDtypeStruct((B_LOC, SEQ, SB_WIDTH), BF16)],
        scratch_shapes=[], semantics=("parallel", "parallel", "arbitrary"), operands=(qkv, qkv, qkv))


def _sb_bwd(qkv, d_o, send=None):
    def body(q_ref, k_ref, v_ref, do_ref, dq_ref, dk_ref, dv_ref, dk_acc, dv_acc, z_scr, keep_scr):
        i = pl.program_id(2)
        krow = lax.broadcasted_iota(jnp.int32, (BLK, BLK), 0)
        kcol = lax.broadcasted_iota(jnp.int32, (BLK, BLK), 1)
        upto = (krow <= kcol).astype(BF16)
        earlier = (krow < kcol).astype(BF16)
        lane0 = lax.broadcasted_iota(jnp.int32, (SB_QBLK, LANES), 1) < HEAD_DIM
        q2 = _stack_heads(q_ref[0] * QK_SCALE, lane0)
        do2 = _stack_heads(do_ref[0], lane0)

        def keep_sum(c, causal):
            off = pl.multiple_of(c * SB_KCHUNK, SB_KCHUNK)
            z = _dot_nt(q2, k_ref[0, pl.ds(off, SB_KCHUNK), :])
            log_keep = _log_keep(z)
            if causal is not None:
                log_keep = jnp.where(causal, log_keep, 0.0)
            z_scr[c] = z
            keep_scr[c] = log_keep
            return jnp.sum(log_keep, axis=1, keepdims=True)

        def some_alive(run):
            return (jnp.max(run) > SB_DEAD).astype(jnp.int32)

        def scan(state):
            t, _, run = state
            run = run + keep_sum(i - 1 - t, None)
            return t + 1, some_alive(run), run

        diag_sum = keep_sum(i, _sb_diag_mask())
        walked, _, tot2 = lax.while_loop(lambda s: jnp.logical_and(s[0] < i, s[1] > 0), scan,
                                         (jnp.int32(0), some_alive(diag_sum), diag_sum))
        first = i - walked

        @pl.when(i == 0)
        def _():
            dk_acc[...] = jnp.zeros_like(dk_acc)
            dv_acc[...] = jnp.zeros_like(dv_acc)

        def chunk(c, carry, causal):
            dq, pre_keep, pre_e = carry
            off = pl.multiple_of(c * SB_KCHUNK, SB_KCHUNK)
            k_c = k_ref[0, pl.ds(off, SB_KCHUNK), :]
            v_c = v_ref[0, pl.ds(off, SB_KCHUNK), :]
            d_a = _dot_nt(do2, v_c)
            log_keep = keep_scr[c]
            log_beta = log_keep + z_scr[c]
            prefix, sums = _block_sums(log_keep, upto)
            inclusive = jnp.concatenate([prefix[0], prefix[1] + sums[0]], axis=1)
            a = jnp.exp(log_beta + ((tot2 - pre_keep) - inclusive))
            if causal is not None:
                a = jnp.where(causal, a, 0.0)
            e = d_a * a
            e_prefix, e_sums = _block_sums(e, earlier, split=False)
            before = jnp.concatenate([e_prefix[0] + pre_e, e_prefix[1] + (pre_e + e_sums[0])], axis=1)
            dz = e - (e + before) * jnp.exp(log_beta)
            if causal is not None:
                dz = jnp.where(causal, dz, 0.0)
            dz = dz.astype(BF16)
            dq = dq + _dot(dz, k_c)
            dk_acc[pl.ds(off, SB_KCHUNK), :] += _dot_tn(dz, q2)
            dv_acc[pl.ds(off, SB_KCHUNK), :] += _dot_tn(a.astype(BF16), do2)
            return dq, pre_keep + (sums[0] + sums[1]), pre_e + (e_sums[0] + e_sums[1])

        zero_col = jnp.zeros((SB_ROWS, 1), F32)
        carry = lax.fori_loop(first, i, lambda t, c: chunk(t, c, None),
                              (jnp.zeros((SB_ROWS, LANES), F32), zero_col, zero_col))
        dq, _, _ = chunk(i, carry, _sb_diag_mask())
        dq_ref[0] = (jnp.where(lane0, dq[:SB_QBLK], dq[SB_QBLK:]) * QK_SCALE).astype(BF16)

        @pl.when(i == N_SB_STEPS - 1)
        def _():
            dk_ref[0] = dk_acc[...].astype(BF16)
            dv_ref[0] = dv_acc[...].astype(BF16)

    blk = pl.BlockSpec((1, SB_QBLK, LANES), lambda b, h, i: (b, i, h))
    whole = lambda c: pl.BlockSpec((1, SEQ, LANES), lambda b, h, i: (b, 0, c * SB_PAIRS + h))
    out = jax.ShapeDtypeStruct((B_LOC, SEQ, SB_WIDTH), BF16)
    return _call(
        body, send, name="sb_bwd", grid=(B_LOC, SB_PAIRS, N_SB_STEPS),
        in_specs=[blk, whole(1), whole(2), blk],
        out_specs=[blk, whole(0), whole(0)],
        out_shape=[out, out, out],
        scratch_shapes=[pltpu.VMEM((SEQ, LANES), F32), pltpu.VMEM((SEQ, LANES), F32),
                        pltpu.VMEM((N_SB_STEPS, SB_ROWS, SB_KCHUNK), F32),
                        pltpu.VMEM((N_SB_STEPS, SB_ROWS, SB_KCHUNK), F32)],
        semantics=("parallel", "parallel", "arbitrary"), operands=(qkv, qkv, qkv, d_o))


DIL_GROUPS = len(DIL_PAIRS)
DIL_QBLOCKS = SEQ // BLK


def _residue_rows(j, dilation):
    length = SEQ // dilation
    return pl.ds(j, length, stride=dilation) if dilation > 1 else pl.ds(0, length)


def _gather_residues(src_ref, dst_ref, dst_off, dilation, scale=None):
    length = SEQ // dilation
    for j in range(dilation):
        v = src_ref[_residue_rows(j, dilation), :]
        if scale is not None:
            v = v * scale
        dst_ref[dst_off + j * length:dst_off + (j + 1) * length, :] = v.astype(dst_ref.dtype)


def _scatter_residues(src_ref, src_off, dst_ref, dilation):
    length = SEQ // dilation
    for j in range(dilation):
        dst_ref[_residue_rows(j, dilation), :] = (
            src_ref[src_off + j * length:src_off + (j + 1) * length, :].astype(dst_ref.dtype))


def _dil_geometry(group, pair):
    dilation = DIL_PAIRS[group][1]
    row = lax.broadcasted_iota(jnp.int32, (2 * BLK, 2 * BLK), 0)
    col = lax.broadcasted_iota(jnp.int32, (2 * BLK, 2 * BLK), 1)
    second = row >= BLK
    steps = BLK + jnp.where(second, row - BLK, row) - col
    coef = -ALIBI_MAX_BIAS / DIL_HEADS * math.log(2.0)
    first_head = float(4 * group + 1) + 2.0 * pair.astype(F32)
    slope = jnp.exp(coef * (first_head + jnp.where(second, 1.0, 0.0)))
    bias = slope * (steps * dilation).astype(F32)
    valid = jnp.logical_and(steps >= 0, steps <= BLK)
    return bias, valid, col >= BLK


def _dil_tile_scores(q2, kk, geometry, has_prev):
    bias, valid, own = geometry
    ok = jnp.logical_and(valid, jnp.logical_or(own, has_prev))
    return jnp.where(ok, _dot_nt(q2, kk) - bias, NEG_BIG)


def _head_col(v, lane_mask):
    return jnp.max(jnp.where(lane_mask, v, NEG_BIG), axis=1, keepdims=True)


def _dil_fwd(qkv, send=None):
    def body(*refs):
        ins, (o_ref, lse_ref), (qs, ks, vs, o_res, lse_res) = refs[:9], refs[9:11], refs[11:16]
        o_grp, lse_grp = refs[16:19], refs[19:22]
        pair = pl.program_id(1)
        lane0 = lax.broadcasted_iota(jnp.int32, (BLK, LANES), 1) < HEAD_DIM
        ks[0:BLK, :] = jnp.zeros((BLK, LANES), BF16)
        vs[0:BLK, :] = jnp.zeros((BLK, LANES), BF16)
        for grp, (_, dilation) in enumerate(DIL_PAIRS):
            q_ref, k_ref, v_ref = ins[3 * grp:3 * grp + 3]
            per_residue = DIL_QBLOCKS // dilation
            _gather_residues(q_ref, qs, 0, dilation, QK_SCALE)
            _gather_residues(k_ref, ks, BLK, dilation)
            _gather_residues(v_ref, vs, BLK, dilation)
            geometry = _dil_geometry(grp, pair)

            def step(blk, _):
                off = pl.multiple_of(blk * BLK, BLK)
                q2 = _stack_heads(qs[pl.ds(off, BLK), :], lane0)
                s = _dil_tile_scores(q2, ks[pl.ds(off, 2 * BLK), :], geometry, blk % per_residue != 0)
                m = jnp.max(s, axis=1, keepdims=True)
                p = jnp.exp(s - m)
                den = jnp.sum(p, axis=1, keepdims=True)
                out = _dot(p.astype(BF16), vs[pl.ds(off, 2 * BLK), :]) / den
                lse = m + jnp.log(den)
                o_res[pl.ds(off, BLK), :] = jnp.where(lane0, out[:BLK], out[BLK:])
                lse_res[pl.ds(off, BLK), :] = jnp.where(lane0, lse[:BLK], lse[BLK:])
                return 0

            lax.fori_loop(0, DIL_QBLOCKS, step, 0, unroll=8)
            _scatter_residues(o_res, 0, o_grp[grp], dilation)
            _scatter_residues(lse_res, 0, lse_grp[grp], dilation)

        for r0 in range(0, SEQ, 2 * BLK):
            rows = slice(r0, r0 + 2 * BLK)
            ls = [lse_grp[g][rows, :] for g in range(DIL_GROUPS)]
            m = jnp.maximum(jnp.maximum(ls[0], ls[1]), ls[2])
            w = [jnp.exp(l - m) for l in ls]
            den = w[0] + w[1] + w[2]
            o_ref[rows, :] = (w[0] * o_grp[0][rows, :] + w[1] * o_grp[1][rows, :] + w[2] * o_grp[2][rows, :]) / den
            lse_ref[rows, :] = m + jnp.log(den)

    def col(part, grp):
        return pl.BlockSpec((None, SEQ, LANES), lambda b, p: (b, 0, 6 * part + 2 * grp + p))

    out_spec = pl.BlockSpec((None, SEQ, LANES), lambda b, p: (b, 0, p))
    out = jax.ShapeDtypeStruct((B_LOC, SEQ, DIL_OUT), F32)
    return _call(
        body, send, name="dil_fwd", grid=(B_LOC, DIL_OUT // LANES),
        in_specs=[col(part, grp) for grp in range(DIL_GROUPS) for part in range(3)],
        out_specs=[out_spec, out_spec], out_shape=[out, out],
        scratch_shapes=[pltpu.VMEM((SEQ, LANES), BF16), pltpu.VMEM((SEQ + BLK, LANES), BF16),
                        pltpu.VMEM((SEQ + BLK, LANES), BF16), pltpu.VMEM((SEQ, LANES), F32),
                        pltpu.VMEM((SEQ, LANES), F32)] + [pltpu.VMEM((SEQ, LANES), F32)] * (2 * DIL_GROUPS),
        semantics=("parallel", "parallel"), operands=[qkv] * 9)


def _dil_bwd(qkv, d_o, lse, dsum, send=None):
    def body(*refs):
        ins, (do_ref, lse_ref, dsum_ref), outs = refs[:9], refs[9:12], refs[12:21]
        qs, ks, vs, dos, lse_res, dsum_res, dq_res, dk_acc, dv_acc = refs[21:]
        pair = pl.program_id(1)
        lane0 = lax.broadcasted_iota(jnp.int32, (BLK, LANES), 1) < HEAD_DIM
        lane1 = jnp.logical_not(lane0)
        ks[0:BLK, :] = jnp.zeros((BLK, LANES), BF16)
        vs[0:BLK, :] = jnp.zeros((BLK, LANES), BF16)
        for grp, (_, dilation) in enumerate(DIL_PAIRS):
            q_ref, k_ref, v_ref = ins[3 * grp:3 * grp + 3]
            dq_ref, dk_ref, dv_ref = outs[3 * grp:3 * grp + 3]
            per_residue = DIL_QBLOCKS // dilation
            _gather_residues(q_ref, qs, 0, dilation, QK_SCALE)
            _gather_residues(k_ref, ks, BLK, dilation)
            _gather_residues(v_ref, vs, BLK, dilation)
            _gather_residues(do_ref, dos, 0, dilation)
            _gather_residues(lse_ref, lse_res, 0, dilation)
            _gather_residues(dsum_ref, dsum_res, 0, dilation)
            dk_acc[...] = jnp.zeros_like(dk_acc)
            dv_acc[...] = jnp.zeros_like(dv_acc)
            geometry = _dil_geometry(grp, pair)

            def step(blk, _):
                off = pl.multiple_of(blk * BLK, BLK)
                q2 = _stack_heads(qs[pl.ds(off, BLK), :], lane0)
                do2 = _stack_heads(dos[pl.ds(off, BLK), :], lane0)
                kk = ks[pl.ds(off, 2 * BLK), :]
                vv = vs[pl.ds(off, 2 * BLK), :]
                lse_blk = lse_res[pl.ds(off, BLK), :]
                dsum_blk = dsum_res[pl.ds(off, BLK), :]
                lse2 = jnp.concatenate([_head_col(lse_blk, lane0), _head_col(lse_blk, lane1)], axis=0)
                dsum2 = jnp.concatenate([_head_col(dsum_blk, lane0), _head_col(dsum_blk, lane1)], axis=0)
                s = _dil_tile_scores(q2, kk, geometry, blk % per_residue != 0)
                p = jnp.exp(s - lse2)
                ds = (p * (_dot_nt(do2, vv) - dsum2)).astype(BF16)
                dq2 = _dot(ds, kk)
                dq_res[pl.ds(off, BLK), :] = jnp.where(lane0, dq2[:BLK], dq2[BLK:]) * QK_SCALE
                dk_acc[pl.ds(off, 2 * BLK), :] += _dot_tn(ds, q2)
                dv_acc[pl.ds(off, 2 * BLK), :] += _dot_tn(p.astype(BF16), do2)
                return 0

            lax.fori_loop(0, DIL_QBLOCKS, step, 0, unroll=8)
            _scatter_residues(dq_res, 0, dq_ref, dilation)
            _scatter_residues(dk_acc, BLK, dk_ref, dilation)
            _scatter_residues(dv_acc, BLK, dv_ref, dilation)

    def col(part, grp):
        return pl.BlockSpec((None, SEQ, LANES), lambda b, p: (b, 0, 6 * part + 2 * grp + p))

    slot = pl.BlockSpec((None, SEQ, LANES), lambda b, p: (b, 0, p))
    out = jax.ShapeDtypeStruct((B_LOC, SEQ, DIL_OUT), F32)
    return _call(
        body, send, name="dil_bwd", grid=(B_LOC, DIL_OUT // LANES),
        in_specs=[col(part, grp) for grp in range(DIL_GROUPS) for part in range(3)] + [slot] * 3,
        out_specs=[slot] * 9, out_shape=[out] * 9,
        scratch_shapes=[pltpu.VMEM((SEQ, LANES), BF16), pltpu.VMEM((SEQ + BLK, LANES), BF16),
                        pltpu.VMEM((SEQ + BLK, LANES), BF16), pltpu.VMEM((SEQ, LANES), BF16),
                        pltpu.VMEM((SEQ, LANES), F32), pltpu.VMEM((SEQ, LANES), F32),
                        pltpu.VMEM((SEQ, LANES), F32), pltpu.VMEM((SEQ + BLK, LANES), F32),
                        pltpu.VMEM((SEQ + BLK, LANES), F32)],
        semantics=("parallel", "parallel"), operands=[qkv] * 9 + [d_o, lse, dsum])


def _peers():
    x, y, c = lax.axis_index("x"), lax.axis_index("y"), lax.axis_index("c")
    me = 4 * x + 2 * y + c
    peers = []
    for mask in range(1, N_DEV):
        px = 1 - x if mask & 4 else x
        py = 1 - y if mask & 2 else y
        pc = 1 - c if mask & 1 else c
        peers.append(((px, py, pc), 4 * px + 2 * py + pc))
    return me, peers


def _all_gather(shard, name):
    rows, cols = shard.shape
    by_rows = rows % 32 == 0

    def body(src_ref, out_ref, send_sems, recv_sems, local_sem):
        x, y, c = lax.axis_index("x"), lax.axis_index("y"), lax.axis_index("c")
        me, sibling = (x, y, c), (x, y, 1 - c)
        x_chip, y_chip, across = (1 - x, y), (x, 1 - y), (1 - x, 1 - y)

        def slot(block, half=None):
            ref = out_ref.at[4 * block[0] + 2 * block[1] + block[2]]
            if half is None:
                return ref
            return ref.at[pl.ds(half * (rows // 2), rows // 2)] if by_rows else \
                ref.at[:, pl.ds(half * (cols // 2), cols // 2)]

        def copy(k, block, to, src=None, half=None):
            return pltpu.make_async_remote_copy(
                src_ref=slot(block, half) if src is None else src, dst_ref=slot(block, half),
                send_sem=send_sems.at[k], recv_sem=recv_sems.at[k], device_id=to,
                device_id_type=pl.DeviceIdType.MESH)

        mine = pltpu.make_async_copy(src_ref, slot(me), local_sem)
        mine.start()
        sent = [copy(0, me, sibling, src=src_ref), copy(1, me, (*x_chip, c), src=src_ref),
                copy(2, me, (*y_chip, c), src=src_ref)]
        for cp in sent:
            cp.start()

        def arrived(k, block, half=None):
            copy(k, block, me, half=half).wait_recv()
            onward = {1: [copy(3, block, (*y_chip, c), half=0), copy(5, block, sibling)],
                      2: [copy(4, block, (*x_chip, c), half=1), copy(6, block, sibling)],
                      3: [copy(7, block, sibling, half=0)],
                      4: [copy(8, block, sibling, half=1)]}.get(k, [])
            for cp in onward:
                cp.start()
            sent.extend(onward)

        arrived(1, (*x_chip, c))
        arrived(2, (*y_chip, c))
        arrived(3, (*across, c), half=0)
        arrived(4, (*across, c), half=1)
        arrived(0, sibling)
        arrived(5, (*x_chip, 1 - c))
        arrived(6, (*y_chip, 1 - c))
        arrived(7, (*across, 1 - c), half=0)
        arrived(8, (*across, 1 - c), half=1)
        for cp in sent:
            cp.wait_send()
        mine.wait()

    n_sems = 9
    return pl.pallas_call(
        body, name=name,
        in_specs=[pl.BlockSpec(memory_space=pl.ANY)],
        out_specs=pl.BlockSpec(memory_space=pl.ANY),
        out_shape=jax.ShapeDtypeStruct((N_DEV,) + shard.shape, shard.dtype),
        scratch_shapes=[pltpu.SemaphoreType.DMA((n_sems,)), pltpu.SemaphoreType.DMA((n_sems,)),
                        pltpu.SemaphoreType.DMA],
    )(shard)


def _call(body, send, *, name, grid, in_specs, out_specs, out_shape, scratch_shapes, semantics, operands):
    if send is None:
        return pl.pallas_call(
            body, name=name, grid=grid, in_specs=in_specs, out_specs=out_specs, out_shape=out_shape,
            scratch_shapes=scratch_shapes, compiler_params=_params(semantics))(*operands), []
    srcs, kinds = [s for s, _ in send], [k for _, k in send]
    n, n_in, n_out, n_scr = len(srcs), len(in_specs), len(out_specs), len(scratch_shapes)
    steps = math.prod(grid)
    relay_step = (13 * steps) // 16

    def plan(refs):
        src_refs, land_refs = refs[n_in:n_in + n], refs[n_in + n + n_out:n_in + 2 * n + n_out]
        send_sems, recv_sems, local_sems = refs[-3:]
        x, y, c = lax.axis_index("x"), lax.axis_index("y"), lax.axis_index("c")
        me, peers = _peers()
        first, relayed_in, relayed_out, arrivals, sends, own = [], [], [], [], [], []
        for a, kind in enumerate(kinds):
            def copy(k, src, dst_slot, to):
                return pltpu.make_async_remote_copy(
                    src_ref=src, dst_ref=land_refs[a].at[dst_slot], send_sem=send_sems.at[a * (N_DEV - 1) + k],
                    recv_sem=recv_sems.at[a * (N_DEV - 1) + k], device_id=to, device_id_type=pl.DeviceIdType.MESH)

            if kind == "gather_by_chip":
                idx = lambda px, py, pc: 4 * px + 2 * py + pc
                chips = [(1 - x, y), (x, 1 - y), (1 - x, 1 - y)]
                mine = [copy(0, src_refs[a], me, (x, y, 1 - c))]
                arrivals.append(copy(0, src_refs[a], idx(x, y, 1 - c), (x, y, 1 - c)))
                for j, (px, py) in enumerate(chips):
                    mine.append(copy(1 + j, src_refs[a], me, (px, py, c)))
                    relayed_in.append(copy(1 + j, src_refs[a], idx(px, py, c), (px, py, c)))
                    relayed_out.append(copy(4 + j, land_refs[a].at[idx(px, py, c)], idx(px, py, c), (x, y, 1 - c)))
                    arrivals.append(copy(4 + j, src_refs[a], idx(px, py, 1 - c), (x, y, 1 - c)))
                first += mine
                sends += mine + relayed_out[-3:]
                own.append(pltpu.make_async_copy(src_refs[a], land_refs[a].at[me], local_sems.at[a]))
            elif kind == "scatter_by_chip":
                for k, (px, py) in enumerate([(1 - x, y), (x, 1 - y), (1 - x, 1 - y)]):
                    first.append(copy(k, src_refs[a].at[2 * px + py], 2 * x + y, (px, py, c)))
                    arrivals.append(copy(k, src_refs[a].at[2 * px + py], 2 * px + py, (px, py, c)))
                sends += first[-3:]
                own.append(pltpu.make_async_copy(src_refs[a].at[2 * x + y], land_refs[a].at[2 * x + y],
                                                 local_sems.at[a]))
            else:
                part = (lambda i: src_refs[a].at[i]) if kind == "scatter" else (lambda i: src_refs[a])
                for k, (peer, peer_idx) in enumerate(peers):
                    first.append(copy(k, part(peer_idx), me, peer))
                    arrivals.append(copy(k, part(peer_idx), peer_idx, peer))
                sends += first[-(N_DEV - 1):]
                own.append(pltpu.make_async_copy(part(me), land_refs[a].at[me], local_sems.at[a]))
        return first, relayed_in, relayed_out, arrivals, sends, own

    def wrapped(*refs):
        step = 0
        for axis, size in enumerate(grid):
            step = step * size + pl.program_id(axis)

        @pl.when(step == 0)
        def _():
            first, _, _, _, _, own = plan(refs)
            for cp in first + own:
                cp.start()

        if "gather_by_chip" in kinds:
            @pl.when(step == relay_step)
            def _():
                _, relayed_in, relayed_out, _, _, _ = plan(refs)
                for cp_in, cp_out in zip(relayed_in, relayed_out):
                    cp_in.wait_recv()
                    cp_out.start()

        body(*refs[:n_in], *refs[n_in + n:n_in + n + n_out], *refs[n_in + 2 * n + n_out:n_in + 2 * n + n_out + n_scr])

        @pl.when(step == steps - 1)
        def _():
            _, _, _, arrivals, sends, own = plan(refs)
            for cp in arrivals:
                cp.wait_recv()
            for cp in sends:
                cp.wait_send()
            for cp in own:
                cp.wait()

    anywhere = pl.BlockSpec(memory_space=pl.ANY)
    lands = [jax.ShapeDtypeStruct((N_DEV // 2 if k == "scatter_by_chip" else N_DEV,) + s.shape[-2:], s.dtype)
             for s, k in send]
    out = pl.pallas_call(
        wrapped, name=name, grid=grid,
        in_specs=list(in_specs) + [anywhere] * n, out_specs=list(out_specs) + [anywhere] * n,
        out_shape=list(out_shape) + lands,
        scratch_shapes=list(scratch_shapes) + [pltpu.SemaphoreType.DMA((n * (N_DEV - 1),)),
                                               pltpu.SemaphoreType.DMA((n * (N_DEV - 1),)),
                                               pltpu.SemaphoreType.DMA((n,))],
        compiler_params=_params(("arbitrary",) * len(grid)),
    )(*operands, *srcs)
    return out[:n_out], list(out[n_out:])


def _pair_swap(blocks):
    def body(src_ref, out_ref, send_sems, recv_sems):
        x, y, c = lax.axis_index("x"), lax.axis_index("y"), lax.axis_index("c")
        copies = [pltpu.make_async_remote_copy(
            src_ref=src_ref.at[2 * chip + (1 - c)], dst_ref=out_ref.at[chip], send_sem=send_sems.at[chip],
            recv_sem=recv_sems.at[chip], device_id=(x, y, 1 - c), device_id_type=pl.DeviceIdType.MESH)
            for chip in range(N_DEV // 2)]
        for cp in copies:
            cp.start()
        for cp in copies:
            cp.wait()

    return pl.pallas_call(
        body, name="pair_swap_grad_w_in",
        in_specs=[pl.BlockSpec(memory_space=pl.ANY)], out_specs=pl.BlockSpec(memory_space=pl.ANY),
        out_shape=jax.ShapeDtypeStruct((N_DEV // 2,) + blocks.shape[1:], blocks.dtype),
        scratch_shapes=[pltpu.SemaphoreType.DMA((N_DEV // 2,)), pltpu.SemaphoreType.DMA((N_DEV // 2,))],
    )(blocks)


def _pair_sum(blocks, swapped, core):
    _, rows, cols = swapped.shape
    tile_rows = _row_tile(rows)

    def body(core_ref, mine_ref, theirs_ref, o_ref):
        o_ref[...] = (mine_ref[...].astype(F32) + theirs_ref[...].astype(F32)).astype(o_ref.dtype)

    return pl.pallas_call(
        body, name="pair_sum_grad_w_in",
        grid_spec=pltpu.PrefetchScalarGridSpec(
            num_scalar_prefetch=1, grid=(N_DEV // 2, rows // tile_rows),
            in_specs=[pl.BlockSpec((None, tile_rows, cols), lambda j, i, core_ref: (2 * j + core_ref[0], i, 0)),
                      pl.BlockSpec((None, tile_rows, cols), lambda j, i, core_ref: (j, i, 0))],
            out_specs=pl.BlockSpec((None, tile_rows, cols), lambda j, i, core_ref: (j, i, 0))),
        out_shape=jax.ShapeDtypeStruct(swapped.shape, swapped.dtype),
        compiler_params=_params(("parallel", "parallel")),
    )(core, blocks, swapped)


def _sum_in_device_order(land_ref):
    acc = land_ref[0].astype(F32)
    for j in range(1, land_ref.shape[0]):
        acc = acc + land_ref[j].astype(F32)
    return acc


def _adam_math(w, g, m, v):
    c1 = 1.0 - ADAM_B1 ** ADAM_STEP
    c2 = 1.0 - ADAM_B2 ** ADAM_STEP
    m_new = ADAM_B1 * m + (1.0 - ADAM_B1) * g
    v_new = ADAM_B2 * v + (1.0 - ADAM_B2) * (g * g)
    delta = -ADAM_LR * ((m_new / c1) / (jnp.sqrt(v_new / c2) + ADAM_EPS) + ADAM_WD * w)
    return delta, m_new, v_new


def _row_tile(rows):
    return max(t for t in range(16, 385, 16) if rows % t == 0) if rows % 16 == 0 else rows


def _sum_update(land, w, m, v, name):
    slots, rows, cols = land.shape
    tile_rows = _row_tile(rows)

    def body(land_ref, w_ref, m_ref, v_ref, g_ref, d_ref, nm_ref, nv_ref):
        g = _sum_in_device_order(land_ref)
        g_ref[...] = g
        d_ref[...], nm_ref[...], nv_ref[...] = _adam_math(w_ref[...], g, m_ref[...], v_ref[...])

    tile = pl.BlockSpec((None, tile_rows, cols), lambda i: (0, i, 0))
    out = jax.ShapeDtypeStruct((1, rows, cols), F32)
    return pl.pallas_call(
        body, name=name, grid=(rows // tile_rows,),
        in_specs=[pl.BlockSpec((slots, tile_rows, cols), lambda i: (0, i, 0)), tile, tile, tile],
        out_specs=[tile] * 4, out_shape=[out] * 4,
        compiler_params=_params(("parallel",)),
    )(land, w, m, v)


def _to_bf16(arrays, flip):
    def body(*refs):
        for src_ref, dst_ref, t in zip(refs[:len(arrays)], refs[len(arrays):], flip):
            v = src_ref[...]
            dst_ref[...] = (v.T if t else v).astype(BF16)

    shapes = [a.shape[::-1] if t else a.shape for a, t in zip(arrays, flip)]
    whole = lambda shape: pl.BlockSpec(shape, lambda i: (0, 0))
    return pl.pallas_call(
        body, name="shards_to_bf16", grid=(1,),
        in_specs=[whole(a.shape) for a in arrays], out_specs=[whole(s) for s in shapes],
        out_shape=[jax.ShapeDtypeStruct(s, BF16) for s in shapes],
        compiler_params=_params(("arbitrary",)),
    )(*arrays)


N_GAINS = 3


def _sum_update_gains(land, ws, ms, vs):
    def body(land_ref, *refs):
        w_refs, m_refs, v_refs = (refs[k * N_GAINS:(k + 1) * N_GAINS] for k in range(3))
        loss_ref, out_refs = refs[3 * N_GAINS], refs[3 * N_GAINS + 1:]
        rows = _sum_in_device_order(land_ref)
        loss_ref[...] = rows[N_GAINS:N_GAINS + 1]
        for k in range(N_GAINS):
            g = rows[k:k + 1]
            g_ref, d_ref, nm_ref, nv_ref = out_refs[4 * k:4 * k + 4]
            g_ref[...] = g
            d_ref[...], nm_ref[...], nv_ref[...] = _adam_math(w_refs[k][...], g, m_refs[k][...], v_refs[k][...])

    row = pl.BlockSpec((1, D_MODEL), lambda i: (0, 0))
    out = jax.ShapeDtypeStruct((1, D_MODEL), F32)
    n_out = 1 + 4 * N_GAINS
    return pl.pallas_call(
        body, name="update_gains", grid=(1,),
        in_specs=[pl.BlockSpec(land.shape, lambda i: (0, 0, 0))] + [row] * (3 * N_GAINS),
        out_specs=[row] * n_out, out_shape=[out] * n_out,
    )(land, *ws, *ms, *vs)


GROUP_FFN = ("w_ffn_in", "w_ffn_out")
GROUP_MIX = ("w_sb_up", "w_dil_up", "w_out")
COL_SHARDED = ("w_in", "w_sb_up", "w_dil_up", "w_ffn_in")
TRANSPOSED = ("w_in", "w_ffn_in")
MIX_UP = ("w_sb_up", "w_dil_up")


def _full_from_shards(name, slots):
    _, r, c = slots.shape
    if name in TRANSPOSED:
        return slots.reshape(N_DEV * r, c).T
    if name in COL_SHARDED:
        return slots.transpose(1, 0, 2).reshape(r, N_DEV * c)
    return slots.reshape(N_DEV * r, c)


def _row_shards(full):
    rows, cols = full.shape
    return full.reshape(N_DEV, rows // N_DEV, cols)


def _local_step(x, target, g_mix, g_ffn, g_fin, w_in_t, shards=None, rest=None):
    gather = lambda names, kind: None if shards is None else [(shards[n], kind) for n in names]
    scatter = lambda blocks: None if shards is None else [(t, "scatter") for t in blocks]
    landed = lambda blocks, lands: lands if lands else blocks

    w = {"w_in": w_in_t}
    if shards is None:
        w.update(rest)
        w["w_ffn_in"] = rest["w_ffn_in"].T
        w.update({n: rest[n].T for n in MIX_UP})
    (qkv_sb, qkv_dl, gates, u), _ = _norm_proj(x, g_mix, w["w_in"])
    qkv_sb = qkv_sb.reshape(B_LOC, SEQ, 3 * SB_WIDTH)
    qkv_dl = qkv_dl.reshape(B_LOC, SEQ, 3 * DIL_WIDTH)
    (o_sb,), lands = _sb_fwd(qkv_sb, gather(GROUP_FFN, "gather_by_chip"))
    if lands:
        w["w_ffn_in"], w["w_ffn_out"] = lands[0].reshape(2 * D_FF, D_MODEL), _full_from_shards("w_ffn_out", lands[1])
    o_sb = o_sb.reshape(TOK, SB_WIDTH)
    (o_dl, lse), lands = _dil_fwd(qkv_dl, gather(GROUP_MIX, "gather"))
    w.update({n: t.reshape(D_MODEL, -1) if n in MIX_UP else _full_from_shards(n, t)
              for n, t in zip(GROUP_MIX, lands)})
    o_dl = o_dl.reshape(TOK, DIL_OUT)

    loss, dx1, merged, u2, act, dh, dx2, dg_fin, dg_ffn = _mix_ffn_fwd_bwd(
        x, o_sb, o_dl, gates, w["w_sb_up"], w["w_dil_up"], w["w_out"], target, g_ffn, g_fin,
        w["w_ffn_in"], w["w_ffn_out"])
    dgates, dy_sb, dy_dl, do_sb, do_dl, dsum = _mix_bwd(dx1, o_sb, o_dl, gates, w["w_sb_up"], w["w_dil_up"], w["w_out"])
    blocks = {
        "w_sb_up": _atb(o_sb, dy_sb, "grad_w_sb_up", SB_WIDTH, D_MODEL, col_blocks=N_DEV),
        "w_dil_up": _atb(o_dl, dy_dl, "grad_w_dil_up", DIL_OUT, D_MODEL, col_blocks=N_DEV),
        "w_out": _row_shards(_atb(merged, dx1, "grad_w_out", D_MODEL, D_MODEL)),
        "w_ffn_in": _row_shards(_atb_cols(dh, u2, "grad_w_ffn_in", 512)),
        "w_ffn_out": _row_shards(_atb_cols(act, dx2, "grad_w_ffn_out", 256)),
    }
    grads = {}

    early, late = GROUP_FFN, GROUP_MIX
    early_blocks = [blocks[n] for n in early]
    (dq_sb, dk_sb, dv_sb), lands = _sb_bwd(qkv_sb, do_sb.reshape(B_LOC, SEQ, SB_WIDTH), scatter(early_blocks))
    grads.update(zip(early, landed(early_blocks, lands)))
    as_batch = lambda t: t.reshape(B_LOC, SEQ, DIL_OUT)
    late_blocks = [blocks[n] for n in late]
    d_dl, lands = _dil_bwd(qkv_dl, as_batch(do_dl), lse, as_batch(dsum), scatter(late_blocks))
    grads.update(zip(late, landed(late_blocks, lands)))
    flat = lambda t: t.reshape(TOK, -1)
    dproj = ([flat(dq_sb), flat(dk_sb), flat(dv_sb)]
             + [flat(d_dl[3 * grp + part]) for part in range(3) for grp in range(DIL_GROUPS)] + [dgates])

    w_in_blocks = _row_shards(_atb_pieces(u, dproj, "grad_w_in", D_MODEL // 2))
    if shards is None:
        grads["w_in"] = w_in_blocks
        send = None
    else:
        core = lax.axis_index("c").astype(jnp.int32).reshape(1)
        send = [(_pair_sum(w_in_blocks, _pair_swap(w_in_blocks), core), "scatter_by_chip")]
    (grad_x, dg_mix), lands = _proj_bwd(dproj, dx1, x, g_mix, w["w_in"], send)
    if lands:
        grads["w_in"] = lands[0]
    gain_grads = jnp.concatenate([dg_mix, dg_ffn, dg_fin], axis=0)
    return loss, grad_x, gain_grads, grads


def kernel(x, norm_mix_g, w_in, w_sb_up, w_dil_up, w_out, norm_ffn_g, w_ffn_in, w_ffn_out, norm_final_g, loss_target, m_norm_mix_g, m_w_in, m_w_sb_up, m_w_dil_up, m_w_out, m_norm_ffn_g, m_w_ffn_in, m_w_ffn_out, m_norm_final_g, v_norm_mix_g, v_w_in, v_w_sb_up, v_w_dil_up, v_w_out, v_norm_ffn_g, v_w_ffn_in, v_w_ffn_out, v_norm_final_g):
    mats = {"w_in": w_in, "w_sb_up": w_sb_up, "w_dil_up": w_dil_up, "w_out": w_out,
            "w_ffn_in": w_ffn_in, "w_ffn_out": w_ffn_out}
    moments_m = {"w_in": m_w_in, "w_sb_up": m_w_sb_up, "w_dil_up": m_w_dil_up, "w_out": m_w_out,
                 "w_ffn_in": m_w_ffn_in, "w_ffn_out": m_w_ffn_out}
    moments_v = {"w_in": v_w_in, "w_sb_up": v_w_sb_up, "w_dil_up": v_w_dil_up, "w_out": v_w_out,
                 "w_ffn_in": v_w_ffn_in, "w_ffn_out": v_w_ffn_out}
    stored = lambda t, name: t.transpose(0, 2, 1) if name in TRANSPOSED else t
    mats = {name: stored(t, name) for name, t in mats.items()}
    shards = dict(zip(mats, _to_bf16([t[0] for t in mats.values()], [name in MIX_UP for name in mats])))
    gathered_w_in = _all_gather(shards.pop("w_in"), "all_gather_w_in")
    g_fin = norm_final_g.reshape(1, D_MODEL)
    loss, grad_x, gain_grads, grad_slots = _local_step(
        x.reshape(TOK, D_MODEL), loss_target.reshape(TOK, D_MODEL), norm_mix_g, norm_ffn_g, g_fin,
        gathered_w_in.reshape(IN_WIDTH, D_MODEL), shards=shards)

    gain_rows = jnp.concatenate([gain_grads, jnp.tile(loss, (1, D_MODEL // LANES)),
                                 jnp.zeros((8 - 4, D_MODEL), F32)], axis=0)
    row = lambda t: t.reshape(1, D_MODEL)
    loss_row, *gain_outs = _sum_update_gains(
        _all_gather(gain_rows, "all_gather_gains"),
        [norm_mix_g, norm_ffn_g, g_fin], [m_norm_mix_g, m_norm_ffn_g, row(m_norm_final_g)],
        [v_norm_mix_g, v_norm_ffn_g, row(v_norm_final_g)])

    out_g, out_d, out_m, out_v = {}, {}, {}, {}
    for name, slots in grad_slots.items():
        out_g[name], out_d[name], out_m[name], out_v[name] = [stored(t, name) for t in _sum_update(
            slots, mats[name], stored(moments_m[name], name), stored(moments_v[name], name), "update_" + name)]
    for idx, name in enumerate(("norm_mix_g", "norm_ffn_g", "norm_final_g")):
        shape = (D_MODEL,) if name == "norm_final_g" else (1, D_MODEL)
        out_g[name], out_d[name], out_m[name], out_v[name] = [t.reshape(shape) for t in gain_outs[4 * idx:4 * idx + 4]]

    order = ("norm_mix_g", "w_in", "w_sb_up", "w_dil_up", "w_out", "norm_ffn_g", "w_ffn_in", "w_ffn_out",
             "norm_final_g")
    return (loss_row[0, 0], grad_x.reshape(B_LOC, SEQ, D_MODEL),
            *[out_g[n] for n in order], *[out_d[n] for n in order],
            *[out_m[n] for n in order], *[out_v[n] for n in order])
```
